```python
import jax, jax.numpy as jnp
from jax import lax
import numpy as np

D_MODEL = 1024
BATCH = 16
SEQ = 2048
DEPTH = 1

MEM_LEN = 256
HEAD_DIM = 64
FOX_HEADS = D_MODEL // 128
FOX_W = FOX_HEADS * HEAD_DIM
RWKV_HEADS = D_MODEL // 128
RWKV_W = RWKV_HEADS * HEAD_DIM
MEM_HEADS = 4
MEM_W = D_MODEL // 2
MEM_HEAD_DIM = MEM_W // MEM_HEADS
DECAY_LORA = 64
AAA_LORA = 64
GATE_LORA = 128
N_BRANCH = 3
D_FF = -(-8 * D_MODEL // (3 * 256)) * 256
Q_BLOCK = 128
NORM_EPS = 1e-6
GN_EPS = 64e-5

FOX_COLS = 3 * FOX_W + FOX_HEADS
RWKV_WIDTHS = (RWKV_W, RWKV_W, RWKV_W, DECAY_LORA, AAA_LORA, GATE_LORA)
RWKV_COLS = sum(RWKV_WIDTHS)
GATE_COLS = N_BRANCH * D_MODEL
IN_COLS = FOX_COLS + RWKV_COLS + MEM_W + GATE_COLS

kernel_name = "fox_rwkv7_memxattn_gated_hybrid"


def _split(x, widths):
    idx = [int(i) for i in np.cumsum(widths)[:-1]]
    return jnp.split(x, idx, axis=-1)


def rmsnorm(x, g):
    xf = x.astype(jnp.float32)
    y = xf * lax.rsqrt(jnp.mean(xf * xf, axis=-1, keepdims=True) + NORM_EPS)
    return (y * g.astype(jnp.float32)).astype(x.dtype)


def forgetting_attention(q, k, v, f_logit):
    B, S, H, Dh = q.shape
    c = jnp.cumsum(jax.nn.log_sigmoid(f_logit.astype(jnp.float32)), axis=1)
    cT = jnp.transpose(c, (0, 2, 1))
    scale = Dh ** -0.5
    tri = jnp.tril(jnp.ones((Q_BLOCK, Q_BLOCK), dtype=bool))
    outs = []
    for i in range(S // Q_BLOCK):
        lo, hi = i * Q_BLOCK, (i + 1) * Q_BLOCK
        logits = jnp.einsum('bqhd,bkhd->bhqk', q[:, lo:hi], k[:, :hi]).astype(jnp.float32) * scale
        bias = cT[:, :, lo:hi, None] - cT[:, :, None, :hi]
        mask = jnp.concatenate([jnp.ones((Q_BLOCK, lo), dtype=bool), tri], axis=1)
        logits = jnp.where(mask, logits + bias, -jnp.inf)
        p = jax.nn.softmax(logits, axis=-1).astype(v.dtype)
        outs.append(jnp.einsum('bhqk,bkhd->bqhd', p, v[:, :hi]))
    return jnp.concatenate(outs, axis=1).reshape(B, S, H * Dh)


def rwkv7_time_mix(p, mu, w0, w_up, a0, a_up, g_up, k_k, k_a, r_k, gn_g, gn_b):
    B, S, _ = p.shape
    H, N = RWKV_HEADS, HEAD_DIM
    p = p.astype(jnp.float32)
    p_prev = jnp.pad(p, ((0, 0), (1, 0), (0, 0)))[:, :-1]
    p = p + (p_prev - p) * mu
    r, k, v, wd, ad, gd = _split(p, RWKV_WIDTHS)
    w_log = -jnp.exp(jax.nn.log_sigmoid(w0 + jnp.tanh(wd) @ w_up) - 0.5)
    a = jax.nn.sigmoid(a0 + ad @ a_up)
    g = jax.nn.sigmoid(gd) @ g_up
    kk = (k * k_k).reshape(B, S, H, N)
    kk = kk * lax.rsqrt(jnp.maximum(jnp.sum(kk * kk, axis=-1, keepdims=True), 1e-24))
    k = k * (1.0 + (a - 1.0) * k_a)
    rh = r.reshape(B, S, H, N)
    kh = k.reshape(B, S, H, N)
    vh = v.reshape(B, S, H, N)
    ah = a.reshape(B, S, H, N)
    wh = jnp.exp(w_log).reshape(B, S, H, N)
    a_vec = -kk
    b_vec = kk * ah

    def step(state, inp):
        r_t, w_t, k_t, v_t, a_t, b_t = inp
        sa = jnp.einsum('bhij,bhj->bhi', state, a_t)
        state = state * w_t[:, :, None, :] + sa[..., None] * b_t[:, :, None, :] + v_t[..., None] * k_t[:, :, None, :]
        y = jnp.einsum('bhij,bhj->bhi', state, r_t)
        return state, y

    xs = tuple(jnp.moveaxis(t, 1, 0) for t in (rh, wh, kh, vh, a_vec, b_vec))
    s0 = jnp.zeros((B, H, N, N), jnp.float32)
    _, ys = lax.scan(step, s0, xs)
    y = jnp.moveaxis(ys, 0, 1)
    mean = jnp.mean(y, axis=-1, keepdims=True)
    var = jnp.mean(jnp.square(y - mean), axis=-1, keepdims=True)
    y = ((y - mean) * lax.rsqrt(var + GN_EPS)).reshape(B, S, H * N) * gn_g + gn_b
    bonus = jnp.sum(rh * kh * r_k, axis=-1, keepdims=True) * vh
    return (y + bonus.reshape(B, S, H * N)) * g


def memory_cross_attention(q, mem_kv):
    B, S, _ = q.shape
    km, vm = _split(mem_kv, (MEM_W, MEM_W))
    qh = q.reshape(B, S, MEM_HEADS, MEM_HEAD_DIM)
    kh = km.reshape(B, -1, MEM_HEADS, MEM_HEAD_DIM)
    vh = vm.reshape(B, -1, MEM_HEADS, MEM_HEAD_DIM)
    logits = jnp.einsum('bqhd,bkhd->bhqk', qh, kh).astype(jnp.float32) * MEM_HEAD_DIM ** -0.5
    p = jax.nn.softmax(logits, axis=-1).astype(vh.dtype)
    return jnp.einsum('bhqk,bkhd->bqhd', p, vh).reshape(B, S, MEM_W)


def _fwd_setup_inputs(seed: int = 0) -> dict:
    key = jax.random.key(seed)
    ks = jax.random.split(key, 32)
    f32 = jnp.float32
    L, D = DEPTH, D_MODEL

    def nrm(k, shape, fan_in):
        return jax.random.normal(k, shape, f32) * fan_in ** -0.5

    def gain(k, shape):
        return 1.0 + 0.1 * jax.random.normal(k, shape, f32)

    return {
        "x": jax.random.normal(ks[0], (BATCH, SEQ, D), f32),
        "mem": jax.random.normal(ks[1], (BATCH, MEM_LEN, D), f32),
        "pre1_g": gain(ks[2], (L, D)),
        "post1_g": gain(ks[3], (L, D)),
        "pre2_g": gain(ks[4], (L, D)),
        "post2_g": gain(ks[5], (L, D)),
        "mem_norm_g": gain(ks[6], (L, D)),
        "w_in": nrm(ks[7], (L, D, IN_COLS), D),
        "fox_f_bias": jax.random.uniform(ks[8], (L, FOX_HEADS), f32, 1.0, 4.0),
        "rwkv_mu": jax.random.uniform(ks[9], (L, RWKV_COLS), f32),
        "rwkv_w0": jax.random.normal(ks[10], (L, RWKV_W), f32),
        "rwkv_w_up": nrm(ks[11], (L, DECAY_LORA, RWKV_W), DECAY_LORA),
        "rwkv_a0": 0.5 * jax.random.normal(ks[12], (L, RWKV_W), f32),
        "rwkv_a_up": nrm(ks[13], (L, AAA_LORA, RWKV_W), AAA_LORA),
        "rwkv_g_up": nrm(ks[14], (L, GATE_LORA, RWKV_W), GATE_LORA),
        "rwkv_k_k": 0.85 + 0.05 * jax.random.normal(ks[15], (L, RWKV_W), f32),
        "rwkv_k_a": 1.0 + 0.05 * jax.random.normal(ks[16], (L, RWKV_W), f32),
        "rwkv_r_k": 0.1 * jax.random.normal(ks[17], (L, RWKV_HEADS, HEAD_DIM), f32),
        "rwkv_gn_g": gain(ks[18], (L, RWKV_W)),
        "rwkv_gn_b": 0.02 * jax.random.normal(ks[19], (L, RWKV_W), f32),
        "w_mem_kv": nrm(ks[20], (L, D, 2 * MEM_W), D),
        "w_fox_out": nrm(ks[21], (L, FOX_W, D), FOX_W),
        "w_rwkv_out": nrm(ks[22], (L, RWKV_W, D), RWKV_W),
        "w_mem_out": nrm(ks[23], (L, MEM_W, D), MEM_W),
        "w_o": nrm(ks[24], (L, D, D), D),
        "w_ffn_gate": nrm(ks[25], (L, D, D_FF), D),
        "w_ffn_up": nrm(ks[26], (L, D, D_FF), D),
        "w_ffn_down": nrm(ks[27], (L, D_FF, D), D_FF),
    }


def _fwd_reference(x, mem, pre1_g, post1_g, pre2_g, post2_g, mem_norm_g, w_in, fox_f_bias,
              rwkv_mu, rwkv_w0, rwkv_w_up, rwkv_a0, rwkv_a_up, rwkv_g_up, rwkv_k_k, rwkv_k_a,
              rwkv_r_k, rwkv_gn_g, rwkv_gn_b, w_mem_kv, w_fox_out, w_rwkv_out, w_mem_out, w_o,
              w_ffn_gate, w_ffn_up, w_ffn_down):
    B, S, D = x.shape
    h = x
    for l in range(DEPTH):
        u = rmsnorm(h, pre1_g[l])
        proj = u @ w_in[l]
        p_fox, p_rwkv, p_memq, p_gate = _split(proj, (FOX_COLS, RWKV_COLS, MEM_W, GATE_COLS))

        fq, fk, fv, ff = _split(p_fox, (FOX_W, FOX_W, FOX_W, FOX_HEADS))
        fox_out = forgetting_attention(
            fq.reshape(B, S, FOX_HEADS, HEAD_DIM), fk.reshape(B, S, FOX_HEADS, HEAD_DIM),
            fv.reshape(B, S, FOX_HEADS, HEAD_DIM), ff + fox_f_bias[l])

        rwkv_out = rwkv7_time_mix(p_rwkv, rwkv_mu[l], rwkv_w0[l], rwkv_w_up[l], rwkv_a0[l],
                                  rwkv_a_up[l], rwkv_g_up[l], rwkv_k_k[l], rwkv_k_a[l],
                                  rwkv_r_k[l], rwkv_gn_g[l], rwkv_gn_b[l])

        mem_kv = rmsnorm(mem, mem_norm_g[l]) @ w_mem_kv[l]
        mem_out = memory_cross_attention(p_memq, mem_kv)

        g_fox, g_rwkv, g_mem = _split(jax.nn.sigmoid(p_gate.astype(jnp.float32)), (D, D, D))
        merged = (g_fox * (fox_out @ w_fox_out[l])
                  + g_rwkv * (rwkv_out @ w_rwkv_out[l])
                  + g_mem * (mem_out @ w_mem_out[l]))
        y = merged @ w_o[l]
        h = h + rmsnorm(y, post1_g[l])

        u2 = rmsnorm(h, pre2_g[l])
        ffn = (jax.nn.silu(u2 @ w_ffn_gate[l]) * (u2 @ w_ffn_up[l])) @ w_ffn_down[l]
        h = h + rmsnorm(ffn, post2_g[l])
    return h.astype(x.dtype)


import jax as _jax
import jax.numpy as _jnp

TWIN_FORMAT = 'train_step'
FWD_PARAMS = ['x', 'mem', 'pre1_g', 'post1_g', 'pre2_g', 'post2_g', 'mem_norm_g', 'w_in', 'fox_f_bias', 'rwkv_mu', 'rwkv_w0', 'rwkv_w_up', 'rwkv_a0', 'rwkv_a_up', 'rwkv_g_up', 'rwkv_k_k', 'rwkv_k_a', 'rwkv_r_k', 'rwkv_gn_g', 'rwkv_gn_b', 'w_mem_kv', 'w_fox_out', 'w_rwkv_out', 'w_mem_out', 'w_o', 'w_ffn_gate', 'w_ffn_up', 'w_ffn_down']
TWIN_WEIGHTS = ['pre1_g', 'post1_g', 'pre2_g', 'post2_g', 'mem_norm_g', 'w_in', 'fox_f_bias', 'rwkv_mu', 'rwkv_w0', 'rwkv_w_up', 'rwkv_a0', 'rwkv_a_up', 'rwkv_g_up', 'rwkv_k_k', 'rwkv_k_a', 'rwkv_r_k', 'rwkv_gn_g', 'rwkv_gn_b', 'w_mem_kv', 'w_fox_out', 'w_rwkv_out', 'w_mem_out', 'w_o', 'w_ffn_gate', 'w_ffn_up', 'w_ffn_down']
TWIN_DIFF_INPUT = 'x'
TWIN_INPUTS = ['x', 'mem', 'pre1_g', 'post1_g', 'pre2_g', 'post2_g', 'mem_norm_g', 'w_in', 'fox_f_bias', 'rwkv_mu', 'rwkv_w0', 'rwkv_w_up', 'rwkv_a0', 'rwkv_a_up', 'rwkv_g_up', 'rwkv_k_k', 'rwkv_k_a', 'rwkv_r_k', 'rwkv_gn_g', 'rwkv_gn_b', 'w_mem_kv', 'w_fox_out', 'w_rwkv_out', 'w_mem_out', 'w_o', 'w_ffn_gate', 'w_ffn_up', 'w_ffn_down', 'loss_target', 'm_pre1_g', 'm_post1_g', 'm_pre2_g', 'm_post2_g', 'm_mem_norm_g', 'm_w_in', 'm_fox_f_bias', 'm_rwkv_mu', 'm_rwkv_w0', 'm_rwkv_w_up', 'm_rwkv_a0', 'm_rwkv_a_up', 'm_rwkv_g_up', 'm_rwkv_k_k', 'm_rwkv_k_a', 'm_rwkv_r_k', 'm_rwkv_gn_g', 'm_rwkv_gn_b', 'm_w_mem_kv', 'm_w_fox_out', 'm_w_rwkv_out', 'm_w_mem_out', 'm_w_o', 'm_w_ffn_gate', 'm_w_ffn_up', 'm_w_ffn_down', 'v_pre1_g', 'v_post1_g', 'v_pre2_g', 'v_post2_g', 'v_mem_norm_g', 'v_w_in', 'v_fox_f_bias', 'v_rwkv_mu', 'v_rwkv_w0', 'v_rwkv_w_up', 'v_rwkv_a0', 'v_rwkv_a_up', 'v_rwkv_g_up', 'v_rwkv_k_k', 'v_rwkv_k_a', 'v_rwkv_r_k', 'v_rwkv_gn_g', 'v_rwkv_gn_b', 'v_w_mem_kv', 'v_w_fox_out', 'v_w_rwkv_out', 'v_w_mem_out', 'v_w_o', 'v_w_ffn_gate', 'v_w_ffn_up', 'v_w_ffn_down']
TWIN_OUTPUTS = ['loss', 'grad_x', 'grad_pre1_g', 'grad_post1_g', 'grad_pre2_g', 'grad_post2_g', 'grad_mem_norm_g', 'grad_w_in', 'grad_fox_f_bias', 'grad_rwkv_mu', 'grad_rwkv_w0', 'grad_rwkv_w_up', 'grad_rwkv_a0', 'grad_rwkv_a_up', 'grad_rwkv_g_up', 'grad_rwkv_k_k', 'grad_rwkv_k_a', 'grad_rwkv_r_k', 'grad_rwkv_gn_g', 'grad_rwkv_gn_b', 'grad_w_mem_kv', 'grad_w_fox_out', 'grad_w_rwkv_out', 'grad_w_mem_out', 'grad_w_o', 'grad_w_ffn_gate', 'grad_w_ffn_up', 'grad_w_ffn_down', 'delta_pre1_g', 'delta_post1_g', 'delta_pre2_g', 'delta_post2_g', 'delta_mem_norm_g', 'delta_w_in', 'delta_fox_f_bias', 'delta_rwkv_mu', 'delta_rwkv_w0', 'delta_rwkv_w_up', 'delta_rwkv_a0', 'delta_rwkv_a_up', 'delta_rwkv_g_up', 'delta_rwkv_k_k', 'delta_rwkv_k_a', 'delta_rwkv_r_k', 'delta_rwkv_gn_g', 'delta_rwkv_gn_b', 'delta_w_mem_kv', 'delta_w_fox_out', 'delta_w_rwkv_out', 'delta_w_mem_out', 'delta_w_o', 'delta_w_ffn_gate', 'delta_w_ffn_up', 'delta_w_ffn_down', 'new_m_pre1_g', 'new_m_post1_g', 'new_m_pre2_g', 'new_m_post2_g', 'new_m_mem_norm_g', 'new_m_w_in', 'new_m_fox_f_bias', 'new_m_rwkv_mu', 'new_m_rwkv_w0', 'new_m_rwkv_w_up', 'new_m_rwkv_a0', 'new_m_rwkv_a_up', 'new_m_rwkv_g_up', 'new_m_rwkv_k_k', 'new_m_rwkv_k_a', 'new_m_rwkv_r_k', 'new_m_rwkv_gn_g', 'new_m_rwkv_gn_b', 'new_m_w_mem_kv', 'new_m_w_fox_out', 'new_m_w_rwkv_out', 'new_m_w_mem_out', 'new_m_w_o', 'new_m_w_ffn_gate', 'new_m_w_ffn_up', 'new_m_w_ffn_down', 'new_v_pre1_g', 'new_v_post1_g', 'new_v_pre2_g', 'new_v_post2_g', 'new_v_mem_norm_g', 'new_v_w_in', 'new_v_fox_f_bias', 'new_v_rwkv_mu', 'new_v_rwkv_w0', 'new_v_rwkv_w_up', 'new_v_rwkv_a0', 'new_v_rwkv_a_up', 'new_v_rwkv_g_up', 'new_v_rwkv_k_k', 'new_v_rwkv_k_a', 'new_v_rwkv_r_k', 'new_v_rwkv_gn_g', 'new_v_rwkv_gn_b', 'new_v_w_mem_kv', 'new_v_w_fox_out', 'new_v_w_rwkv_out', 'new_v_w_mem_out', 'new_v_w_o', 'new_v_w_ffn_gate', 'new_v_w_ffn_up', 'new_v_w_ffn_down']
TWIN_LEAF_KINDS = {'loss': 'loss', 'grad_x': 'grad_x', 'grad_pre1_g': 'grad_w', 'grad_post1_g': 'grad_w', 'grad_pre2_g': 'grad_w', 'grad_post2_g': 'grad_w', 'grad_mem_norm_g': 'grad_w', 'grad_w_in': 'grad_w', 'grad_fox_f_bias': 'grad_w', 'grad_rwkv_mu': 'grad_w', 'grad_rwkv_w0': 'grad_w', 'grad_rwkv_w_up': 'grad_w', 'grad_rwkv_a0': 'grad_w', 'grad_rwkv_a_up': 'grad_w', 'grad_rwkv_g_up': 'grad_w', 'grad_rwkv_k_k': 'grad_w', 'grad_rwkv_k_a': 'grad_w', 'grad_rwkv_r_k': 'grad_w', 'grad_rwkv_gn_g': 'grad_w', 'grad_rwkv_gn_b': 'grad_w', 'grad_w_mem_kv': 'grad_w', 'grad_w_fox_out': 'grad_w', 'grad_w_rwkv_out': 'grad_w', 'grad_w_mem_out': 'grad_w', 'grad_w_o': 'grad_w', 'grad_w_ffn_gate': 'grad_w', 'grad_w_ffn_up': 'grad_w', 'grad_w_ffn_down': 'grad_w', 'delta_pre1_g': 'delta_w', 'delta_post1_g': 'delta_w', 'delta_pre2_g': 'delta_w', 'delta_post2_g': 'delta_w', 'delta_mem_norm_g': 'delta_w', 'delta_w_in': 'delta_w', 'delta_fox_f_bias': 'delta_w', 'delta_rwkv_mu': 'delta_w', 'delta_rwkv_w0': 'delta_w', 'delta_rwkv_w_up': 'delta_w', 'delta_rwkv_a0': 'delta_w', 'delta_rwkv_a_up': 'delta_w', 'delta_rwkv_g_up': 'delta_w', 'delta_rwkv_k_k': 'delta_w', 'delta_rwkv_k_a': 'delta_w', 'delta_rwkv_r_k': 'delta_w', 'delta_rwkv_gn_g': 'delta_w', 'delta_rwkv_gn_b': 'delta_w', 'delta_w_mem_kv': 'delta_w', 'delta_w_fox_out': 'delta_w', 'delta_w_rwkv_out': 'delta_w', 'delta_w_mem_out': 'delta_w', 'delta_w_o': 'delta_w', 'delta_w_ffn_gate': 'delta_w', 'delta_w_ffn_up': 'delta_w', 'delta_w_ffn_down': 'delta_w', 'new_m_pre1_g': 'new_m', 'new_m_post1_g': 'new_m', 'new_m_pre2_g': 'new_m', 'new_m_post2_g': 'new_m', 'new_m_mem_norm_g': 'new_m', 'new_m_w_in': 'new_m', 'new_m_fox_f_bias': 'new_m', 'new_m_rwkv_mu': 'new_m', 'new_m_rwkv_w0': 'new_m', 'new_m_rwkv_w_up': 'new_m', 'new_m_rwkv_a0': 'new_m', 'new_m_rwkv_a_up': 'new_m', 'new_m_rwkv_g_up': 'new_m', 'new_m_rwkv_k_k': 'new_m', 'new_m_rwkv_k_a': 'new_m', 'new_m_rwkv_r_k': 'new_m', 'new_m_rwkv_gn_g': 'new_m', 'new_m_rwkv_gn_b': 'new_m', 'new_m_w_mem_kv': 'new_m', 'new_m_w_fox_out': 'new_m', 'new_m_w_rwkv_out': 'new_m', 'new_m_w_mem_out': 'new_m', 'new_m_w_o': 'new_m', 'new_m_w_ffn_gate': 'new_m', 'new_m_w_ffn_up': 'new_m', 'new_m_w_ffn_down': 'new_m', 'new_v_pre1_g': 'new_v', 'new_v_post1_g': 'new_v', 'new_v_pre2_g': 'new_v', 'new_v_post2_g': 'new_v', 'new_v_mem_norm_g': 'new_v', 'new_v_w_in': 'new_v', 'new_v_fox_f_bias': 'new_v', 'new_v_rwkv_mu': 'new_v', 'new_v_rwkv_w0': 'new_v', 'new_v_rwkv_w_up': 'new_v', 'new_v_rwkv_a0': 'new_v', 'new_v_rwkv_a_up': 'new_v', 'new_v_rwkv_g_up': 'new_v', 'new_v_rwkv_k_k': 'new_v', 'new_v_rwkv_k_a': 'new_v', 'new_v_rwkv_r_k': 'new_v', 'new_v_rwkv_gn_g': 'new_v', 'new_v_rwkv_gn_b': 'new_v', 'new_v_w_mem_kv': 'new_v', 'new_v_w_fox_out': 'new_v', 'new_v_w_rwkv_out': 'new_v', 'new_v_w_mem_out': 'new_v', 'new_v_w_o': 'new_v', 'new_v_w_ffn_gate': 'new_v', 'new_v_w_ffn_up': 'new_v', 'new_v_w_ffn_down': 'new_v'}


def _forward(args):
    return _fwd_reference(*[args[k] for k in FWD_PARAMS])


def _output_shape():
    out = _jax.eval_shape(lambda: _forward(_fwd_setup_inputs(0)))
    return out.shape, out.dtype

N_MICROBATCH = 1
ADAM_LR = 0.001
ADAM_B1 = 0.9
ADAM_B2 = 0.999
ADAM_EPS = 1e-08
ADAM_WD = 0.01
ADAM_STEP = 10
PER_EXAMPLE_BATCH_AXIS = {'x': 0, 'mem': 0, 'loss_target': 0}
SHARED_INPUTS = []
_WEIGHT_DTYPES = {'pre1_g': _jnp.float32, 'post1_g': _jnp.float32, 'pre2_g': _jnp.float32, 'post2_g': _jnp.float32, 'mem_norm_g': _jnp.float32, 'w_in': _jnp.float32, 'fox_f_bias': _jnp.float32, 'rwkv_mu': _jnp.float32, 'rwkv_w0': _jnp.float32, 'rwkv_w_up': _jnp.float32, 'rwkv_a0': _jnp.float32, 'rwkv_a_up': _jnp.float32, 'rwkv_g_up': _jnp.float32, 'rwkv_k_k': _jnp.float32, 'rwkv_k_a': _jnp.float32, 'rwkv_r_k': _jnp.float32, 'rwkv_gn_g': _jnp.float32, 'rwkv_gn_b': _jnp.float32, 'w_mem_kv': _jnp.float32, 'w_fox_out': _jnp.float32, 'w_rwkv_out': _jnp.float32, 'w_mem_out': _jnp.float32, 'w_o': _jnp.float32, 'w_ffn_gate': _jnp.float32, 'w_ffn_up': _jnp.float32, 'w_ffn_down': _jnp.float32}
MOMENT_SCALE = {'pre1_g': 8.389344e-01, 'post1_g': 3.194764e+01, 'pre2_g': 5.963511e-01, 'post2_g': 3.212174e+01, 'mem_norm_g': 1.213156e-01, 'w_in': 3.159857e-01, 'fox_f_bias': 1.463212e+00, 'rwkv_mu': 8.883568e-01, 'rwkv_w0': 2.096728e-01, 'rwkv_w_up': 5.642477e-02, 'rwkv_a0': 2.147623e-01, 'rwkv_a_up': 1.749434e-01, 'rwkv_g_up': 5.043524e-01, 'rwkv_k_k': 1.412002e-01, 'rwkv_k_a': 5.655575e-01, 'rwkv_r_k': 1.121493e+00, 'rwkv_gn_g': 6.208747e-01, 'rwkv_gn_b': 1.614081e+00, 'w_mem_kv': 1.206243e-01, 'w_fox_out': 3.213116e-01, 'w_rwkv_out': 4.539589e-01, 'w_mem_out': 9.502275e-02, 'w_o': 6.125806e-01, 'w_ffn_gate': 1.987129e-01, 'w_ffn_up': 3.006313e-01, 'w_ffn_down': 5.027800e-01}


def _to_microbatches(a, axis):
    t = _jnp.moveaxis(a, axis, 0)
    t = t.reshape((N_MICROBATCH, t.shape[0] // N_MICROBATCH) + t.shape[1:])
    return _jnp.moveaxis(t, 1, axis + 1)


def setup_inputs(seed: int = 0) -> dict:
    inp = _fwd_setup_inputs(seed)
    key = _jax.random.fold_in(_jax.random.key(seed), 7919)
    shape, _ = _output_shape()
    out = dict(inp)
    out["loss_target"] = _jax.random.normal(_jax.random.fold_in(key, 0), shape, _jnp.float32)
    for i, name in enumerate(TWIN_WEIGHTS):
        w = inp[name].astype(_jnp.float32)
        if MOMENT_SCALE is None:
            s = _jnp.sqrt(_jnp.mean(_jnp.square(w)) + 1e-30)
        else:
            s = MOMENT_SCALE[name]
        km, kv = _jax.random.split(_jax.random.fold_in(key, i + 1))
        out[name] = w
        out["m_" + name] = s * _jax.random.normal(km, w.shape, _jnp.float32)
        out["v_" + name] = (s * s) * _jax.random.uniform(kv, w.shape, _jnp.float32, 0.5, 1.5)
    if N_MICROBATCH > 1:
        for name, axis in PER_EXAMPLE_BATCH_AXIS.items():
            out[name] = _to_microbatches(out[name], axis)
    return {'x': out['x'], 'mem': out['mem'], 'pre1_g': out['pre1_g'], 'post1_g': out['post1_g'], 'pre2_g': out['pre2_g'], 'post2_g': out['post2_g'], 'mem_norm_g': out['mem_norm_g'], 'w_in': out['w_in'], 'fox_f_bias': out['fox_f_bias'], 'rwkv_mu': out['rwkv_mu'], 'rwkv_w0': out['rwkv_w0'], 'rwkv_w_up': out['rwkv_w_up'], 'rwkv_a0': out['rwkv_a0'], 'rwkv_a_up': out['rwkv_a_up'], 'rwkv_g_up': out['rwkv_g_up'], 'rwkv_k_k': out['rwkv_k_k'], 'rwkv_k_a': out['rwkv_k_a'], 'rwkv_r_k': out['rwkv_r_k'], 'rwkv_gn_g': out['rwkv_gn_g'], 'rwkv_gn_b': out['rwkv_gn_b'], 'w_mem_kv': out['w_mem_kv'], 'w_fox_out': out['w_fox_out'], 'w_rwkv_out': out['w_rwkv_out'], 'w_mem_out': out['w_mem_out'], 'w_o': out['w_o'], 'w_ffn_gate': out['w_ffn_gate'], 'w_ffn_up': out['w_ffn_up'], 'w_ffn_down': out['w_ffn_down'], 'loss_target': out['loss_target'], 'm_pre1_g': out['m_pre1_g'], 'm_post1_g': out['m_post1_g'], 'm_pre2_g': out['m_pre2_g'], 'm_post2_g': out['m_post2_g'], 'm_mem_norm_g': out['m_mem_norm_g'], 'm_w_in': out['m_w_in'], 'm_fox_f_bias': out['m_fox_f_bias'], 'm_rwkv_mu': out['m_rwkv_mu'], 'm_rwkv_w0': out['m_rwkv_w0'], 'm_rwkv_w_up': out['m_rwkv_w_up'], 'm_rwkv_a0': out['m_rwkv_a0'], 'm_rwkv_a_up': out['m_rwkv_a_up'], 'm_rwkv_g_up': out['m_rwkv_g_up'], 'm_rwkv_k_k': out['m_rwkv_k_k'], 'm_rwkv_k_a': out['m_rwkv_k_a'], 'm_rwkv_r_k': out['m_rwkv_r_k'], 'm_rwkv_gn_g': out['m_rwkv_gn_g'], 'm_rwkv_gn_b': out['m_rwkv_gn_b'], 'm_w_mem_kv': out['m_w_mem_kv'], 'm_w_fox_out': out['m_w_fox_out'], 'm_w_rwkv_out': out['m_w_rwkv_out'], 'm_w_mem_out': out['m_w_mem_out'], 'm_w_o': out['m_w_o'], 'm_w_ffn_gate': out['m_w_ffn_gate'], 'm_w_ffn_up': out['m_w_ffn_up'], 'm_w_ffn_down': out['m_w_ffn_down'], 'v_pre1_g': out['v_pre1_g'], 'v_post1_g': out['v_post1_g'], 'v_pre2_g': out['v_pre2_g'], 'v_post2_g': out['v_post2_g'], 'v_mem_norm_g': out['v_mem_norm_g'], 'v_w_in': out['v_w_in'], 'v_fox_f_bias': out['v_fox_f_bias'], 'v_rwkv_mu': out['v_rwkv_mu'], 'v_rwkv_w0': out['v_rwkv_w0'], 'v_rwkv_w_up': out['v_rwkv_w_up'], 'v_rwkv_a0': out['v_rwkv_a0'], 'v_rwkv_a_up': out['v_rwkv_a_up'], 'v_rwkv_g_up': out['v_rwkv_g_up'], 'v_rwkv_k_k': out['v_rwkv_k_k'], 'v_rwkv_k_a': out['v_rwkv_k_a'], 'v_rwkv_r_k': out['v_rwkv_r_k'], 'v_rwkv_gn_g': out['v_rwkv_gn_g'], 'v_rwkv_gn_b': out['v_rwkv_gn_b'], 'v_w_mem_kv': out['v_w_mem_kv'], 'v_w_fox_out': out['v_w_fox_out'], 'v_w_rwkv_out': out['v_w_rwkv_out'], 'v_w_mem_out': out['v_w_mem_out'], 'v_w_o': out['v_w_o'], 'v_w_ffn_gate': out['v_w_ffn_gate'], 'v_w_ffn_up': out['v_w_ffn_up'], 'v_w_ffn_down': out['v_w_ffn_down']}


def _loss(weights, diff, rest, loss_target):
    with _jax.named_scope("forward"):
        args = {**rest, TWIN_DIFF_INPUT: diff, **{k: w.astype(_WEIGHT_DTYPES[k]) for k, w in weights.items()}}
        y = _forward(args)
    with _jax.named_scope("loss_head"):
        err = _jnp.square(y.astype(_jnp.float32) - loss_target)
        return 0.5 * _jnp.sum(_jnp.mean(err, axis=-1)) if err.ndim else 0.5 * err


def _adamw(w, g, m, v):
    m = ADAM_B1 * m + (1.0 - ADAM_B1) * g
    v = ADAM_B2 * v + (1.0 - ADAM_B2) * _jnp.square(g)
    m_hat = m / (1.0 - ADAM_B1 ** ADAM_STEP)
    v_hat = v / (1.0 - ADAM_B2 ** ADAM_STEP)
    delta = -ADAM_LR * (m_hat / (_jnp.sqrt(v_hat) + ADAM_EPS) + ADAM_WD * w)
    return delta, m, v


def reference(x, mem, pre1_g, post1_g, pre2_g, post2_g, mem_norm_g, w_in, fox_f_bias, rwkv_mu, rwkv_w0, rwkv_w_up, rwkv_a0, rwkv_a_up, rwkv_g_up, rwkv_k_k, rwkv_k_a, rwkv_r_k, rwkv_gn_g, rwkv_gn_b, w_mem_kv, w_fox_out, w_rwkv_out, w_mem_out, w_o, w_ffn_gate, w_ffn_up, w_ffn_down, loss_target, m_pre1_g, m_post1_g, m_pre2_g, m_post2_g, m_mem_norm_g, m_w_in, m_fox_f_bias, m_rwkv_mu, m_rwkv_w0, m_rwkv_w_up, m_rwkv_a0, m_rwkv_a_up, m_rwkv_g_up, m_rwkv_k_k, m_rwkv_k_a, m_rwkv_r_k, m_rwkv_gn_g, m_rwkv_gn_b, m_w_mem_kv, m_w_fox_out, m_w_rwkv_out, m_w_mem_out, m_w_o, m_w_ffn_gate, m_w_ffn_up, m_w_ffn_down, v_pre1_g, v_post1_g, v_pre2_g, v_post2_g, v_mem_norm_g, v_w_in, v_fox_f_bias, v_rwkv_mu, v_rwkv_w0, v_rwkv_w_up, v_rwkv_a0, v_rwkv_a_up, v_rwkv_g_up, v_rwkv_k_k, v_rwkv_k_a, v_rwkv_r_k, v_rwkv_gn_g, v_rwkv_gn_b, v_w_mem_kv, v_w_fox_out, v_w_rwkv_out, v_w_mem_out, v_w_o, v_w_ffn_gate, v_w_ffn_up, v_w_ffn_down):
    given = dict(x=x, mem=mem, pre1_g=pre1_g, post1_g=post1_g, pre2_g=pre2_g, post2_g=post2_g, mem_norm_g=mem_norm_g, w_in=w_in, fox_f_bias=fox_f_bias, rwkv_mu=rwkv_mu, rwkv_w0=rwkv_w0, rwkv_w_up=rwkv_w_up, rwkv_a0=rwkv_a0, rwkv_a_up=rwkv_a_up, rwkv_g_up=rwkv_g_up, rwkv_k_k=rwkv_k_k, rwkv_k_a=rwkv_k_a, rwkv_r_k=rwkv_r_k, rwkv_gn_g=rwkv_gn_g, rwkv_gn_b=rwkv_gn_b, w_mem_kv=w_mem_kv, w_fox_out=w_fox_out, w_rwkv_out=w_rwkv_out, w_mem_out=w_mem_out, w_o=w_o, w_ffn_gate=w_ffn_gate, w_ffn_up=w_ffn_up, w_ffn_down=w_ffn_down, loss_target=loss_target, m_pre1_g=m_pre1_g, m_post1_g=m_post1_g, m_pre2_g=m_pre2_g, m_post2_g=m_post2_g, m_mem_norm_g=m_mem_norm_g, m_w_in=m_w_in, m_fox_f_bias=m_fox_f_bias, m_rwkv_mu=m_rwkv_mu, m_rwkv_w0=m_rwkv_w0, m_rwkv_w_up=m_rwkv_w_up, m_rwkv_a0=m_rwkv_a0, m_rwkv_a_up=m_rwkv_a_up, m_rwkv_g_up=m_rwkv_g_up, m_rwkv_k_k=m_rwkv_k_k, m_rwkv_k_a=m_rwkv_k_a, m_rwkv_r_k=m_rwkv_r_k, m_rwkv_gn_g=m_rwkv_gn_g, m_rwkv_gn_b=m_rwkv_gn_b, m_w_mem_kv=m_w_mem_kv, m_w_fox_out=m_w_fox_out, m_w_rwkv_out=m_w_rwkv_out, m_w_mem_out=m_w_mem_out, m_w_o=m_w_o, m_w_ffn_gate=m_w_ffn_gate, m_w_ffn_up=m_w_ffn_up, m_w_ffn_down=m_w_ffn_down, v_pre1_g=v_pre1_g, v_post1_g=v_post1_g, v_pre2_g=v_pre2_g, v_post2_g=v_post2_g, v_mem_norm_g=v_mem_norm_g, v_w_in=v_w_in, v_fox_f_bias=v_fox_f_bias, v_rwkv_mu=v_rwkv_mu, v_rwkv_w0=v_rwkv_w0, v_rwkv_w_up=v_rwkv_w_up, v_rwkv_a0=v_rwkv_a0, v_rwkv_a_up=v_rwkv_a_up, v_rwkv_g_up=v_rwkv_g_up, v_rwkv_k_k=v_rwkv_k_k, v_rwkv_k_a=v_rwkv_k_a, v_rwkv_r_k=v_rwkv_r_k, v_rwkv_gn_g=v_rwkv_gn_g, v_rwkv_gn_b=v_rwkv_gn_b, v_w_mem_kv=v_w_mem_kv, v_w_fox_out=v_w_fox_out, v_w_rwkv_out=v_w_rwkv_out, v_w_mem_out=v_w_mem_out, v_w_o=v_w_o, v_w_ffn_gate=v_w_ffn_gate, v_w_ffn_up=v_w_ffn_up, v_w_ffn_down=v_w_ffn_down)
    weights = {n: given[n] for n in TWIN_WEIGHTS}
    shared = {n: given[n] for n in SHARED_INPUTS}
    per_example = {n: given[n] for n in ['x', 'mem']}
    grad_fn = _jax.value_and_grad(_loss, argnums=(0, 1))

    def one_microbatch(ex, loss_target):
        ex = dict(ex)
        diff = ex.pop(TWIN_DIFF_INPUT)
        return grad_fn(weights, diff, {**shared, **ex}, loss_target)

    if N_MICROBATCH == 1:
        loss, (grad_w, grad_x) = one_microbatch(per_example, given["loss_target"])
    else:
        def body(carry, xs):
            loss_sum, grad_sum = carry
            l_k, (gw_k, gx_k) = one_microbatch(xs[0], xs[1])
            with _jax.named_scope("update"):
                return (loss_sum + l_k, _jax.tree.map(_jnp.add, grad_sum, gw_k)), gx_k

        init = (_jnp.zeros((), _jnp.float32), _jax.tree.map(_jnp.zeros_like, weights))
        (loss, grad_w), grad_x = _jax.lax.scan(body, init, (per_example, given["loss_target"]))
    with _jax.named_scope("update"):
        delta_w, new_m, new_v = {}, {}, {}
        for n in TWIN_WEIGHTS:
            delta_w[n], new_m[n], new_v[n] = _adamw(weights[n], grad_w[n], given["m_" + n], given["v_" + n])
    return (loss, grad_x, *[grad_w[n] for n in TWIN_WEIGHTS], *[delta_w[n] for n in TWIN_WEIGHTS],
            *[new_m[n] for n in TWIN_WEIGHTS], *[new_v[n] for n in TWIN_WEIGHTS])
```

```python
import functools

import jax
import jax.numpy as jnp
from jax import lax
from jax.experimental import pallas as pl
from jax.experimental.pallas import tpu as pltpu

f32 = jnp.float32
bf16 = jnp.bfloat16
_HI = lax.Precision.HIGHEST

D = 1024
HEADS = 8
HD = 64
HW = HEADS * HD
MEM_HEADS = 4
MEM_HD = 128
MEM_W = 512
MEM_LEN = 256
D_FF = 2816
LORA_PAD = 128
RW_COLS = 3 * HW + 3 * LORA_PAD
NORM_EPS = 1e-6
GN_EPS = 64e-5
Q_BLOCK = 128
SCAN_CHUNK = 64
N_DEV = 8
LANES = 1024
VMEM_LIMIT = 56 * 1024 * 1024

ADAM_LR = 0.001
ADAM_B1 = 0.9
ADAM_B2 = 0.999
ADAM_EPS = 1e-08
ADAM_WD = 0.01
ADAM_STEP = 10

SHARDED = (
    ("w_in", (1024, 6920), 1),
    ("w_ffn_gate", (1024, 2816), 1),
    ("w_ffn_up", (1024, 2816), 1),
    ("w_ffn_down", (2816, 1024), 0),
    ("w_mem_kv", (1024, 1024), 0),
    ("w_o", (1024, 1024), 0),
    ("w_fox_out", (512, 1024), 1),
    ("w_rwkv_out", (512, 1024), 1),
    ("w_mem_out", (512, 1024), 1),
    ("rwkv_w_up", (64, 512), 1),
    ("rwkv_a_up", (64, 512), 1),
    ("rwkv_g_up", (128, 512), 1),
)
REPLICATED = (
    ("pre1_g", (1, 1024)), ("post1_g", (1, 1024)), ("pre2_g", (1, 1024)), ("post2_g", (1, 1024)),
    ("mem_norm_g", (1, 1024)), ("fox_f_bias", (1, 8)), ("rwkv_mu", (1, 1792)), ("rwkv_w0", (1, 512)),
    ("rwkv_a0", (1, 512)), ("rwkv_k_k", (1, 512)), ("rwkv_k_a", (1, 512)), ("rwkv_r_k", (1, 8, 64)),
    ("rwkv_gn_g", (1, 512)), ("rwkv_gn_b", (1, 512)),
)
WEIGHT_ORDER = ('pre1_g', 'post1_g', 'pre2_g', 'post2_g', 'mem_norm_g', 'w_in', 'fox_f_bias', 'rwkv_mu',
                'rwkv_w0', 'rwkv_w_up', 'rwkv_a0', 'rwkv_a_up', 'rwkv_g_up', 'rwkv_k_k', 'rwkv_k_a',
                'rwkv_r_k', 'rwkv_gn_g', 'rwkv_gn_b', 'w_mem_kv', 'w_fox_out', 'w_rwkv_out', 'w_mem_out',
                'w_o', 'w_ffn_gate', 'w_ffn_up', 'w_ffn_down')


def _block_shape(shape, axis):
    return tuple(s // N_DEV if i == axis else s for i, s in enumerate(shape))


def _rows_of(shape):
    n = 1
    for s in shape:
        n *= s
    return n // LANES


SHARD_ROWS = sum(_rows_of(_block_shape(s, a)) for _, s, a in SHARDED)
REPL_ELEMS = sum(_rows_of((LANES,) + s) for _, s in REPLICATED)
REPL_ROWS = -(-REPL_ELEMS // LANES)
PACK_ROWS = -(-(SHARD_ROWS + REPL_ROWS) // 128) * 128
GATHER_ROWS = -(-SHARD_ROWS // 16) * 16


def _cp(sem=None):
    return pltpu.CompilerParams(dimension_semantics=sem, vmem_limit_bytes=VMEM_LIMIT)


def _tile(dim, cap):
    best = None
    for t in range(128, min(dim, cap) + 1, 128):
        if dim % t == 0:
            best = t
    return best if best is not None else dim


def _dg(a, b, dims, exact):
    if exact:
        return lax.dot_general(a, b, dims, precision=_HI, preferred_element_type=f32)
    return lax.dot_general(a.astype(bf16), b.astype(bf16), dims, preferred_element_type=f32)


def _make_mm(batched, exact):
    o = 1 if batched else 0
    bd = ((0,), (0,)) if batched else ((), ())
    d_nn = (((1 + o,), (o,)), bd)
    d_nt = (((1 + o,), (1 + o,)), bd)
    d_tn = (((o,), (o,)), bd)

    @jax.custom_vjp
    def nn(a, b):
        return _dg(a, b, d_nn, exact)

    @jax.custom_vjp
    def nt(a, b):
        return _dg(a, b, d_nt, exact)

    @jax.custom_vjp
    def tn(a, b):
        return _dg(a, b, d_tn, exact)

    nn.defvjp(lambda a, b: (_dg(a, b, d_nn, exact), (a, b)),
              lambda res, g: (_dg(g, res[1], d_nt, exact), _dg(res[0], g, d_tn, exact)))
    nt.defvjp(lambda a, b: (_dg(a, b, d_nt, exact), (a, b)),
              lambda res, g: (_dg(g, res[1], d_nn, exact), _dg(g, res[0], d_tn, exact)))
    tn.defvjp(lambda a, b: (_dg(a, b, d_tn, exact), (a, b)),
              lambda res, g: (_dg(res[1], g, d_nt, exact), _dg(res[0], g, d_nn, exact)))
    return nn, nt, tn


def _sigmoid(x):
    return 1.0 / (1.0 + jnp.exp(-x))


def _head_sum_matrix():
    i = lax.broadcasted_iota(jnp.int32, (HW, HW), 0) // HD
    j = lax.broadcasted_iota(jnp.int32, (HW, HW), 1) // HD
    return (i == j).astype(f32)


def _head_sum_raw(x):
    return _dg(x, _head_sum_matrix(), (((1,), (0,)), ((), ())), True)


@jax.custom_vjp
def _head_sum(x):
    return _head_sum_raw(x)


_head_sum.defvjp(lambda x: (_head_sum_raw(x), None), lambda _, g: (_head_sum_raw(g),))


def _matmul(name, a, b, mode, add=None):
    if mode == "nn":
        (m, k), (_, n) = a.shape, b.shape
    elif mode == "nt":
        (m, k), (n, _) = a.shape, b.shape
    else:
        (k, m), (_, n) = a.shape, b.shape
    tm, tn, tk = _tile(m, 512), _tile(n, 512), _tile(k, 1408)
    nk = k // tk
    dims = {"nn": (((1,), (0,)), ((), ())), "nt": (((1,), (1,)), ((), ())), "tn": (((0,), (0,)), ((), ()))}[mode]
    a_spec = pl.BlockSpec((tk, tm), lambda i, j, kk: (kk, i)) if mode == "tn" else pl.BlockSpec((tm, tk), lambda i, j, kk: (i, kk))
    b_spec = pl.BlockSpec((tn, tk), lambda i, j, kk: (j, kk)) if mode == "nt" else pl.BlockSpec((tk, tn), lambda i, j, kk: (kk, j))
    o_spec = pl.BlockSpec((tm, tn), lambda i, j, kk: (i, j))
    has_add = add is not None

    def body(*refs):
        if has_add:
            a_ref, b_ref, add_ref, o_ref, acc = refs
        else:
            a_ref, b_ref, o_ref, acc = refs
        kk = pl.program_id(2)

        @pl.when(kk == 0)
        def _():
            acc[...] = add_ref[...] if has_add else jnp.zeros_like(acc)

        acc[...] += lax.dot_general(a_ref[...].astype(bf16), b_ref[...].astype(bf16), dims,
                                    preferred_element_type=f32)

        @pl.when(kk == nk - 1)
        def _():
            o_ref[...] = acc[...]

    return pl.pallas_call(
        body, name=name, grid=(m // tm, n // tn, nk),
        in_specs=[a_spec, b_spec] + ([o_spec] if has_add else []),
        out_specs=o_spec, out_shape=jax.ShapeDtypeStruct((m, n), f32),
        scratch_shapes=[pltpu.VMEM((tm, tn), f32)],
        compiler_params=_cp(("parallel", "parallel", "arbitrary")),
    )(*((a, b, add) if has_add else (a, b)))


def _pieces(ref, widths):
    out, off = [], 0
    for w in widths:
        out.append(ref[:, off:off + w])
        off += w
    return out


def _store_pieces(ref, widths, vals, add_ref=None):
    off = 0
    for w, v in zip(widths, vals):
        ref[:, off:off + w] = v if add_ref is None else v + add_ref[:, off:off + w]
        off += w


def _rows_fwd(name, fn, consts, rows, params, outs, n_sums=0, tm=256):
    t = (consts + rows)[0][0].shape[0]
    tm = min(tm, t)
    ins = consts + rows
    n_in, n_p, n_o = len(ins), len(params), len(outs)

    def body(*refs):
        in_refs, p_refs = refs[:n_in], refs[n_in:n_in + n_p]
        o_refs, s_refs = refs[n_in + n_p:n_in + n_p + n_o], refs[n_in + n_p + n_o:]
        vals = []
        for r, (_, widths) in zip(in_refs, ins):
            vals += _pieces(r, widths)
        res = fn(*vals, *[p[...] for p in p_refs])
        pos = 0
        for r, widths in zip(o_refs, outs):
            _store_pieces(r, widths, res[pos:pos + len(widths)])
            pos += len(widths)

        @pl.when(pl.program_id(0) == 0)
        def _():
            for s in s_refs:
                s[...] = jnp.zeros_like(s)

        for s, v in zip(s_refs, res[pos:]):
            s[...] += v

    row_spec = lambda w: pl.BlockSpec((tm, w), lambda i: (i, 0))
    full = lambda p: pl.BlockSpec(p.shape, lambda i: (0,) * p.ndim)
    return pl.pallas_call(
        body, name=name, grid=(t // tm,),
        in_specs=[row_spec(a.shape[1]) for a, _ in ins] + [full(p) for p in params],
        out_specs=[row_spec(sum(w)) for w in outs] + [pl.BlockSpec((1, 1), lambda i: (0, 0))] * n_sums,
        out_shape=[jax.ShapeDtypeStruct((t, sum(w)), f32) for w in outs] + [jax.ShapeDtypeStruct((1, 1), f32)] * n_sums,
        compiler_params=_cp(("arbitrary",)),
    )(*[a for a, _ in ins], *params)


def _rows_bwd(name, fn, consts, rows, params, outs, cts, n_sums=0, add=None, tm=256):
    t = (consts + rows)[0][0].shape[0]
    tm = min(tm, t)
    n_c, n_r, n_p, n_o = len(consts), len(rows), len(params), len(outs)
    has_add = add is not None

    def body(*refs):
        pos = 0
        c_refs = refs[pos:pos + n_c]; pos += n_c
        r_refs = refs[pos:pos + n_r]; pos += n_r
        p_refs = refs[pos:pos + n_p]; pos += n_p
        ct_refs = refs[pos:pos + n_o]; pos += n_o
        add_ref = refs[pos] if has_add else None
        pos += 1 if has_add else 0
        dr_refs = refs[pos:pos + n_r]; pos += n_r
        dp_refs = refs[pos:pos + n_p]
        cvals, rvals = [], []
        for r, (_, widths) in zip(c_refs, consts):
            cvals += _pieces(r, widths)
        for r, (_, widths) in zip(r_refs, rows):
            rvals += _pieces(r, widths)
        pvals = [p[...] for p in p_refs]
        ctv = []
        for r, widths in zip(ct_refs, outs):
            ctv += _pieces(r, widths)
        ctv += [jnp.ones((1, 1), f32)] * n_sums
        _, vjp = jax.vjp(lambda *rp: tuple(fn(*cvals, *rp)), *rvals, *pvals)
        g = vjp(tuple(ctv))
        pos = 0
        for idx, (r, (_, widths)) in enumerate(zip(dr_refs, rows)):
            _store_pieces(r, widths, g[pos:pos + len(widths)], add_ref if idx == 0 else None)
            pos += len(widths)

        @pl.when(pl.program_id(0) == 0)
        def _():
            for dp in dp_refs:
                dp[...] = jnp.zeros_like(dp)

        for dp, v in zip(dp_refs, g[pos:]):
            dp[...] += v

    row_spec = lambda w: pl.BlockSpec((tm, w), lambda i: (i, 0))
    full = lambda p: pl.BlockSpec(p.shape, lambda i: (0,) * p.ndim)
    args = [a for a, _ in consts + rows] + list(params) + list(cts) + ([add] if has_add else [])
    res = pl.pallas_call(
        body, name=name, grid=(t // tm,),
        in_specs=[row_spec(a.shape[1]) for a, _ in consts + rows] + [full(p) for p in params]
        + [row_spec(sum(w)) for w in outs] + ([row_spec(add.shape[1])] if has_add else []),
        out_specs=[row_spec(a.shape[1]) for a, _ in rows] + [full(p) for p in params],
        out_shape=[jax.ShapeDtypeStruct(a.shape, f32) for a, _ in rows] + [jax.ShapeDtypeStruct(p.shape, f32) for p in params],
        compiler_params=_cp(("arbitrary",)),
    )(*args)
    return res[:n_r], res[n_r:]


def _rms(x, g):
    return x * lax.rsqrt(jnp.mean(x * x, axis=-1, keepdims=True) + NORM_EPS) * g


def _fn_rms(x, g):
    return (_rms(x, g),)


def _fn_rwkv_pre(r, k, v, wd, ad, gd, w0, w_up, a0, a_up, g_up, k_k, k_a):
    nn, _, _ = _make_mm(False, False)
    w_log = -_sigmoid(w0 + nn(jnp.tanh(wd), w_up)) * 0.6065306597126334
    a = _sigmoid(a0 + nn(ad, a_up))
    g = nn(_sigmoid(gd), g_up)
    kk = k * k_k
    kk = kk * lax.rsqrt(jnp.maximum(_head_sum(kk * kk), 1e-24))
    k2 = k * (1.0 + (a - 1.0) * k_a)
    return r, w_log, k2, v, -kk, kk * a, g


def _fn_rwkv_post(y, r, k2, v, g, gn_g, gn_b, r_k):
    mean = _head_sum(y) * (1.0 / HD)
    yc = y - mean
    var = _head_sum(yc * yc) * (1.0 / HD)
    yn = yc * lax.rsqrt(var + GN_EPS) * gn_g + gn_b
    bonus = _head_sum(r * k2 * r_k) * v
    return ((yn + bonus) * g,)


def _fn_merge(a_fox, a_rwkv, a_mem, g_fox, g_rwkv, g_mem):
    return (_sigmoid(g_fox) * a_fox + _sigmoid(g_rwkv) * a_rwkv + _sigmoid(g_mem) * a_mem,)


def _fn_post1(y, x, post1_g, pre2_g):
    h1 = x + _rms(y, post1_g)
    return h1, _rms(h1, pre2_g)


def _fn_swiglu(gp, up):
    return (gp * _sigmoid(gp) * up,)


def _fn_final(target, ffn, h1, post2_g):
    err = h1 + _rms(ffn, post2_g) - target
    per_row = jnp.mean(err * err, axis=-1, keepdims=True)
    return (0.5 * jnp.sum(per_row, axis=0, keepdims=True),)


def _shift_down(x):
    row = lax.broadcasted_iota(jnp.int32, x.shape, 0)
    return jnp.where(row == 0, 0.0, pltpu.roll(x, 1, 0))


def _shift_up(x):
    s = x.shape[0]
    row = lax.broadcasted_iota(jnp.int32, x.shape, 0)
    return jnp.where(row == s - 1, 0.0, pltpu.roll(x, s - 1, 0))


def _tokshift_fwd(p, mu, batch, seq):
    w = p.shape[1]
    tc = _tile(w, 384)

    def body(p_ref, mu_ref, o_ref):
        x = p_ref[...]
        o_ref[...] = x + (_shift_down(x) - x) * mu_ref[...]

    return pl.pallas_call(
        body, name="tokshift_fwd", grid=(w // tc, batch),
        in_specs=[pl.BlockSpec((seq, tc), lambda j, b: (b, j)), pl.BlockSpec((1, tc), lambda j, b: (0, j))],
        out_specs=pl.BlockSpec((seq, tc), lambda j, b: (b, j)),
        out_shape=jax.ShapeDtypeStruct(p.shape, f32),
        compiler_params=_cp(("parallel", "arbitrary")),
    )(p, mu)


def _tokshift_bwd(p, mu, dps, batch, seq):
    w = p.shape[1]
    tc = _tile(w, 384)

    def body(p_ref, mu_ref, d_ref, dp_ref, dmu_ref):
        x, mu_v, d = p_ref[...], mu_ref[...], d_ref[...]
        dp_ref[...] = d * (1.0 - mu_v) + _shift_up(d * mu_v)

        @pl.when(pl.program_id(1) == 0)
        def _():
            dmu_ref[...] = jnp.zeros_like(dmu_ref)

        dmu_ref[...] += jnp.sum(d * (_shift_down(x) - x), axis=0, keepdims=True)

    return pl.pallas_call(
        body, name="tokshift_bwd", grid=(w // tc, batch),
        in_specs=[pl.BlockSpec((seq, tc), lambda j, b: (b, j)), pl.BlockSpec((1, tc), lambda j, b: (0, j)),
                  pl.BlockSpec((seq, tc), lambda j, b: (b, j))],
        out_specs=[pl.BlockSpec((seq, tc), lambda j, b: (b, j)), pl.BlockSpec((1, tc), lambda j, b: (0, j))],
        out_shape=[jax.ShapeDtypeStruct(p.shape, f32), jax.ShapeDtypeStruct(mu.shape, f32)],
        compiler_params=_cp(("parallel", "arbitrary")),
    )(p, mu, dps)


def _cum_block(seq):
    return _tile(seq, 256)


def _fox_gate_fwd(f, bias, batch, seq):
    cb = _cum_block(seq)

    def body(f_ref, b_ref, c_ref):
        row = lax.broadcasted_iota(jnp.int32, (cb, cb), 0)
        col = lax.broadcasted_iota(jnp.int32, (cb, cb), 1)
        tri = (col <= row).astype(f32)
        carry = jnp.zeros((1, 128), f32)
        for i in range(seq // cb):
            z = f_ref[i * cb:(i + 1) * cb, :] + b_ref[...]
            ls = jnp.minimum(z, 0.0) - jnp.log(1.0 + jnp.exp(-jnp.abs(z)))
            c = _dg(tri, ls, (((1,), (0,)), ((), ())), True) + carry
            c_ref[i * cb:(i + 1) * cb, :] = c
            carry = c[cb - 1:cb, :]

    return pl.pallas_call(
        body, name="fox_gate_fwd", grid=(batch,),
        in_specs=[pl.BlockSpec((seq, 128), lambda b: (b, 0)), pl.BlockSpec((1, 128), lambda b: (0, 0))],
        out_specs=pl.BlockSpec((seq, 128), lambda b: (b, 0)),
        out_shape=jax.ShapeDtypeStruct(f.shape, f32),
        compiler_params=_cp(("arbitrary",)),
    )(f, bias)


def _fox_gate_bwd(f, bias, dc_a, dc_b, batch, seq):
    cb = _cum_block(seq)

    def body(f_ref, b_ref, da_ref, db_ref, df_ref, dbias_ref):
        row = lax.broadcasted_iota(jnp.int32, (cb, cb), 0)
        col = lax.broadcasted_iota(jnp.int32, (cb, cb), 1)
        triu = (col >= row).astype(f32)

        @pl.when(pl.program_id(0) == 0)
        def _():
            dbias_ref[...] = jnp.zeros_like(dbias_ref)

        carry = jnp.zeros((1, 128), f32)
        tot = jnp.zeros((1, 128), f32)
        for i in reversed(range(seq // cb)):
            sl = slice(i * cb, (i + 1) * cb)
            dc = da_ref[sl, :] + db_ref[sl, :]
            dls = _dg(triu, dc, (((1,), (0,)), ((), ())), True) + carry
            carry = dls[0:1, :]
            df = dls * _sigmoid(-(f_ref[sl, :] + b_ref[...]))
            df_ref[sl, :] = df
            tot = tot + jnp.sum(df, axis=0, keepdims=True)
        dbias_ref[...] += tot

    return pl.pallas_call(
        body, name="fox_gate_bwd", grid=(batch,),
        in_specs=[pl.BlockSpec((seq, 128), lambda b: (b, 0)), pl.BlockSpec((1, 128), lambda b: (0, 0)),
                  pl.BlockSpec((seq, 128), lambda b: (b, 0)), pl.BlockSpec((seq, 128), lambda b: (b, 0))],
        out_specs=[pl.BlockSpec((seq, 128), lambda b: (b, 0)), pl.BlockSpec((1, 128), lambda b: (0, 0))],
        out_shape=[jax.ShapeDtypeStruct(f.shape, f32), jax.ShapeDtypeStruct((1, 128), f32)],
        compiler_params=_cp(("arbitrary",)),
    )(f, bias, dc_a, dc_b)


def _fox_block(q, k, v, cq, ck, q0):
    nn, nt, _ = _make_mm(False, False)
    tq, s = q.shape[0], k.shape[0]
    logits = nt(q, k) * (HD ** -0.5) + (cq - ck)
    qi = q0 + lax.broadcasted_iota(jnp.int32, (tq, s), 0)
    ki = lax.broadcasted_iota(jnp.int32, (tq, s), 1)
    logits = jnp.where(ki <= qi, logits, -1e30)
    m = lax.stop_gradient(jnp.max(logits, axis=-1, keepdims=True))
    e = jnp.exp(logits - m)
    p = e / jnp.sum(e, axis=-1, keepdims=True)
    return nn(p, v)


def _fox_specs(seq):
    qb = pl.BlockSpec((1, Q_BLOCK, HD), lambda h, i: (h, i, 0))
    kb = pl.BlockSpec((1, seq, HD), lambda h, i: (h, 0, 0))
    cq = pl.BlockSpec((1, Q_BLOCK, 1), lambda h, i: (h, i, 0))
    ck = pl.BlockSpec((1, 1, seq), lambda h, i: (h, 0, 0))
    return qb, kb, cq, ck


def _fox_fwd(q, k, v, ccol, crow):
    bh, seq, _ = q.shape
    qb, kb, cqs, cks = _fox_specs(seq)

    def body(q_ref, k_ref, v_ref, cq_ref, ck_ref, o_ref):
        q0 = pl.program_id(1) * Q_BLOCK
        o_ref[0] = _fox_block(q_ref[0], k_ref[0], v_ref[0], cq_ref[0], ck_ref[0], q0)

    return pl.pallas_call(
        body, name="fox_attn_fwd", grid=(bh, seq // Q_BLOCK),
        in_specs=[qb, kb, kb, cqs, cks], out_specs=qb,
        out_shape=jax.ShapeDtypeStruct(q.shape, f32),
        compiler_params=_cp(("parallel", "arbitrary")),
    )(q, k, v, ccol, crow)


def _fox_bwd(q, k, v, ccol, crow, do):
    bh, seq, _ = q.shape
    qb, kb, cqs, cks = _fox_specs(seq)

    def body(q_ref, k_ref, v_ref, cq_ref, ck_ref, do_ref, dq_ref, dk_ref, dv_ref, dcq_ref, dck_ref):
        q0 = pl.program_id(1) * Q_BLOCK
        _, vjp = jax.vjp(functools.partial(_fox_block, q0=q0), q_ref[0], k_ref[0], v_ref[0], cq_ref[0], ck_ref[0])
        dq, dk, dv, dcq, dck = vjp(do_ref[0])
        dq_ref[0] = dq
        dcq_ref[0] = dcq

        @pl.when(pl.program_id(1) == 0)
        def _():
            dk_ref[...] = jnp.zeros_like(dk_ref)
            dv_ref[...] = jnp.zeros_like(dv_ref)
            dck_ref[...] = jnp.zeros_like(dck_ref)

        dk_ref[0] += dk
        dv_ref[0] += dv
        dck_ref[0] += dck

    return pl.pallas_call(
        body, name="fox_attn_bwd", grid=(bh, seq // Q_BLOCK),
        in_specs=[qb, kb, kb, cqs, cks, qb], out_specs=[qb, kb, kb, cqs, cks],
        out_shape=[jax.ShapeDtypeStruct(q.shape, f32), jax.ShapeDtypeStruct(k.shape, f32), jax.ShapeDtypeStruct(v.shape, f32),
                   jax.ShapeDtypeStruct(ccol.shape, f32), jax.ShapeDtypeStruct(crow.shape, f32)],
        compiler_params=_cp(("parallel", "arbitrary")),
    )(q, k, v, ccol, crow, do)


def _mem_block(q, km, vm):
    nn, nt, _ = _make_mm(False, False)
    logits = nt(q, km) * (MEM_HD ** -0.5)
    m = lax.stop_gradient(jnp.max(logits, axis=-1, keepdims=True))
    e = jnp.exp(logits - m)
    return nn(e / jnp.sum(e, axis=-1, keepdims=True), vm)


def _mem_specs(seq, tq):
    nq = seq // tq
    qs = pl.BlockSpec((tq, MEM_HD), lambda b, h, i: (b * nq + i, h))
    ks = pl.BlockSpec((MEM_LEN, MEM_HD), lambda b, h, i: (b, h))
    vs = pl.BlockSpec((MEM_LEN, MEM_HD), lambda b, h, i: (b, MEM_HEADS + h))
    return nq, qs, ks, vs


def _mem_fwd(q, mem_kv, batch, seq):
    tq = min(512, seq)
    nq, qs, ks, vs = _mem_specs(seq, tq)

    def body(q_ref, k_ref, v_ref, o_ref):
        o_ref[...] = _mem_block(q_ref[...], k_ref[...], v_ref[...])

    return pl.pallas_call(
        body, name="mem_attn_fwd", grid=(batch, MEM_HEADS, nq),
        in_specs=[qs, ks, vs], out_specs=qs, out_shape=jax.ShapeDtypeStruct(q.shape, f32),
        compiler_params=_cp(("parallel", "parallel", "arbitrary")),
    )(q, mem_kv, mem_kv)


def _mem_bwd(q, mem_kv, do, batch, seq):
    tq = min(512, seq)
    nq, qs, ks, vs = _mem_specs(seq, tq)

    def body(q_ref, k_ref, v_ref, do_ref, dq_ref, dk_ref, dv_ref):
        _, vjp = jax.vjp(_mem_block, q_ref[...], k_ref[...], v_ref[...])
        dq, dk, dv = vjp(do_ref[...])
        dq_ref[...] = dq

        @pl.when(pl.program_id(2) == 0)
        def _():
            dk_ref[...] = jnp.zeros_like(dk_ref)
            dv_ref[...] = jnp.zeros_like(dv_ref)

        dk_ref[...] += dk
        dv_ref[...] += dv

    return pl.pallas_call(
        body, name="mem_attn_bwd", grid=(batch, MEM_HEADS, nq),
        in_specs=[qs, ks, vs, qs], out_specs=[qs, ks, ks],
        out_shape=[jax.ShapeDtypeStruct(q.shape, f32), jax.ShapeDtypeStruct((batch * MEM_LEN, MEM_W), f32),
                   jax.ShapeDtypeStruct((batch * MEM_LEN, MEM_W), f32)],
        compiler_params=_cp(("parallel", "parallel", "arbitrary")),
    )(q, mem_kv, mem_kv, do)


def _scan_chunk(s0, r, wl, k, v, a, b):
    nn, nt, tn = _make_mm(True, True)
    hb, c, _ = r.shape
    row = lax.broadcasted_iota(jnp.int32, (c, c), 0)
    col = lax.broadcasted_iota(jnp.int32, (c, c), 1)
    tri = jnp.broadcast_to((col <= row).astype(f32)[None], (hb, c, c))
    lg = nn(tri, wl)
    lg_end = lg[:, c - 1:c, :]
    grow, shrink, to_end = jnp.exp(lg), jnp.exp(-lg), jnp.exp(lg_end - lg)
    rt, kt, bt, at = r * grow, k * shrink, b * shrink, a * jnp.exp(lg - wl)
    strict, incl = (col < row)[None], (col <= row)[None]
    l_ab = jnp.where(strict, nt(at, bt), 0.0)
    a_ak = jnp.where(strict, nt(at, kt), 0.0)
    a_rb = jnp.where(incl, nt(rt, bt), 0.0)
    a_rk = jnp.where(incl, nt(rt, kt), 0.0)
    inv = (col == row).astype(f32)[None] + l_ab
    power, n = l_ab, 1
    while 2 * n < c:
        power = nn(power, power)
        inv = inv + nn(inv, power)
        n *= 2
    sa = nn(inv, nt(at, s0) + nn(a_ak, v))
    y = nt(rt, s0) + nn(a_rk, v) + nn(a_rb, sa)
    s1 = s0 * jnp.exp(lg_end) + tn(v, k * to_end) + tn(sa, b * to_end)
    return y, s1


def _scan_fwd(z, hb):
    _, bh, seq, n = z.shape
    c = min(SCAN_CHUNK, seq)
    nc = seq // c

    def body(z_ref, y_ref, s_ref, st):
        @pl.when(pl.program_id(1) == 0)
        def _():
            st[...] = jnp.zeros_like(st)

        s0 = st[...]
        s_ref[:, 0] = s0
        y, s1 = _scan_chunk(s0, z_ref[0], z_ref[1], z_ref[2], z_ref[3], z_ref[4], z_ref[5])
        y_ref[...] = y
        st[...] = s1

    return pl.pallas_call(
        body, name="rwkv_scan_fwd", grid=(bh // hb, nc),
        in_specs=[pl.BlockSpec((6, hb, c, n), lambda h, i: (0, h, i, 0))],
        out_specs=[pl.BlockSpec((hb, c, n), lambda h, i: (h, i, 0)), pl.BlockSpec((hb, 1, n, n), lambda h, i: (h, i, 0, 0))],
        out_shape=[jax.ShapeDtypeStruct((bh, seq, n), f32), jax.ShapeDtypeStruct((bh, nc, n, n), f32)],
        scratch_shapes=[pltpu.VMEM((hb, n, n), f32)],
        compiler_params=_cp(("parallel", "arbitrary")),
    )(z)


def _scan_bwd(z, states, dy, hb):
    _, bh, seq, n = z.shape
    c = min(SCAN_CHUNK, seq)
    nc = seq // c

    def body(z_ref, s_ref, dy_ref, dz_ref, dst):
        @pl.when(pl.program_id(1) == 0)
        def _():
            dst[...] = jnp.zeros_like(dst)

        _, vjp = jax.vjp(_scan_chunk, s_ref[:, 0], z_ref[0], z_ref[1], z_ref[2], z_ref[3], z_ref[4], z_ref[5])
        g = vjp((dy_ref[...], dst[...]))
        dst[...] = g[0]
        for i in range(6):
            dz_ref[i] = g[1 + i]

    return pl.pallas_call(
        body, name="rwkv_scan_bwd", grid=(bh // hb, nc),
        in_specs=[pl.BlockSpec((6, hb, c, n), lambda h, i: (0, h, nc - 1 - i, 0)),
                  pl.BlockSpec((hb, 1, n, n), lambda h, i: (h, nc - 1 - i, 0, 0)),
                  pl.BlockSpec((hb, c, n), lambda h, i: (h, nc - 1 - i, 0))],
        out_specs=pl.BlockSpec((6, hb, c, n), lambda h, i: (0, h, nc - 1 - i, 0)),
        out_shape=jax.ShapeDtypeStruct(z.shape, f32),
        scratch_shapes=[pltpu.VMEM((hb, n, n), f32)],
        compiler_params=_cp(("parallel", "arbitrary")),
    )(z, states, dy)


def _to_heads(x, batch, seq, k):
    return x.reshape(batch, seq, k, HEADS, HD).transpose(2, 0, 3, 1, 4).reshape(k, batch * HEADS, seq, HD)


def _from_heads(x, batch, seq, k):
    return x.reshape(k, batch, HEADS, seq, HD).transpose(1, 3, 0, 2, 4).reshape(batch * seq, k * HW)


def _pad_cols(x, width):
    return jnp.pad(x, ((0, 0), (0, width - x.shape[1])))


def _split_w_in(w):
    z64 = jnp.zeros((w.shape[0], 64), w.dtype)
    w_r = jnp.concatenate([w[:, 1544:3080], w[:, 3080:3144], z64, w[:, 3144:3208], z64, w[:, 3208:3336]], axis=1)
    return w[:, :1536], _pad_cols(w[:, 1536:1544], 128), w_r, w[:, 3336:3848], w[:, 3848:]


def _merge_w_in(g_qkv, g_f, g_r, g_mq, g_g):
    return jnp.concatenate([g_qkv, g_f[:, :8], g_r[:, :1536], g_r[:, 1536:1600], g_r[:, 1664:1728], g_r[:, 1792:],
                            g_mq, g_g], axis=1)


def _pad_lora(v):
    z64 = jnp.zeros((1, 64), v.dtype)
    return jnp.concatenate([v[:, :1536], v[:, 1536:1600], z64, v[:, 1600:1664], z64, v[:, 1664:]], axis=1)


def _unpad_lora(v):
    return jnp.concatenate([v[:, :1536], v[:, 1536:1600], v[:, 1664:1728], v[:, 1792:]], axis=1)


def _local_step(x, mem, target, w, p):
    batch, seq, _ = x.shape
    t = batch * seq
    x2, tg2, mem2 = x.reshape(t, D), target.reshape(t, D), mem.reshape(batch * MEM_LEN, D)
    w_qkv, w_f, w_r, w_mq, w_g3 = _split_w_in(w["w_in"])
    mu = _pad_lora(p["rwkv_mu"])
    bias = _pad_cols(p["fox_f_bias"], 128)
    w_up = jnp.pad(p["rwkv_w_up"], ((0, LORA_PAD - 64), (0, 0)))
    a_up = jnp.pad(p["rwkv_a_up"], ((0, LORA_PAD - 64), (0, 0)))
    g_up = p["rwkv_g_up"]
    r_k = p["rwkv_r_k"].reshape(1, HW)
    pre_params = [p["rwkv_w0"], w_up, p["rwkv_a0"], a_up, g_up, p["rwkv_k_k"], p["rwkv_k_a"]]
    post_params = [p["rwkv_gn_g"], p["rwkv_gn_b"], r_k]
    rw_widths = [HW, HW, HW, LORA_PAD, LORA_PAD, LORA_PAD]
    six = [HW] * 6

    (u,) = _rows_fwd("rms_pre1", _fn_rms, [], [(x2, [D])], [p["pre1_g"]], [[D]])
    p_qkv = _matmul("proj_qkv", u, w_qkv, "nn")
    p_f = _matmul("proj_f", u, w_f, "nn")
    p_r = _matmul("proj_rwkv", u, w_r, "nn")
    p_mq = _matmul("proj_memq", u, w_mq, "nn")
    p_g = _matmul("proj_gate", u, w_g3, "nn")

    c = _fox_gate_fwd(p_f, bias, batch, seq)
    c8 = c[:, :HEADS].reshape(batch, seq, HEADS).transpose(0, 2, 1).reshape(batch * HEADS, seq)
    ccol, crow = c8[:, :, None], c8[:, None, :]
    qkv = _to_heads(p_qkv, batch, seq, 3)
    fox_hm = _fox_fwd(qkv[0], qkv[1], qkv[2], ccol, crow)
    fox_out = _from_heads(fox_hm[None], batch, seq, 1)

    ps = _tokshift_fwd(p_r, mu, batch, seq)
    main6, g_rw = _rows_fwd("rwkv_pre", _fn_rwkv_pre, [], [(ps, rw_widths)], pre_params, [six, [HW]])
    z = _to_heads(main6, batch, seq, 6)
    hb = HEADS
    y_hm, states = _scan_fwd(z, hb)
    y_rw = _from_heads(y_hm[None], batch, seq, 1)
    post_consts = []
    post_rows = [(y_rw, [HW]), (main6, six), (g_rw, [HW])]

    def fn_post(y, r, _wl, k2, v, _a, _b, g, gn_g, gn_b, rk):
        return _fn_rwkv_post(y, r, k2, v, g, gn_g, gn_b, rk)

    (rwkv_out,) = _rows_fwd("rwkv_post", fn_post, post_consts, post_rows, post_params, [[HW]])

    (memn,) = _rows_fwd("rms_mem", _fn_rms, [], [(mem2, [D])], [p["mem_norm_g"]], [[D]])
    mem_kv = _matmul("proj_memkv", memn, w["w_mem_kv"], "nn")
    mem_out = _mem_fwd(p_mq, mem_kv, batch, seq)

    a_fox = _matmul("out_fox", fox_out, w["w_fox_out"], "nn")
    a_rwkv = _matmul("out_rwkv", rwkv_out, w["w_rwkv_out"], "nn")
    a_mem = _matmul("out_mem", mem_out, w["w_mem_out"], "nn")
    merge_rows = [(a_fox, [D]), (a_rwkv, [D]), (a_mem, [D]), (p_g, [D, D, D])]
    (merged,) = _rows_fwd("merge", _fn_merge, [], merge_rows, [], [[D]])
    yy = _matmul("out_o", merged, w["w_o"], "nn")
    post1_rows = [(yy, [D]), (x2, [D])]
    post1_params = [p["post1_g"], p["pre2_g"]]
    h1, u2 = _rows_fwd("post1", _fn_post1, [], post1_rows, post1_params, [[D], [D]])
    gp = _matmul("ffn_gate", u2, w["w_ffn_gate"], "nn")
    up = _matmul("ffn_up", u2, w["w_ffn_up"], "nn")
    (hmid,) = _rows_fwd("swiglu", _fn_swiglu, [], [(gp, [D_FF]), (up, [D_FF])], [], [[D_FF]])
    ffn = _matmul("ffn_down", hmid, w["w_ffn_down"], "nn")
    final_rows = [(ffn, [D]), (h1, [D])]
    (loss,) = _rows_fwd("final", _fn_final, [(tg2, [D])], final_rows, [p["post2_g"]], [], n_sums=1)

    gw, gp_ = {}, {}
    (d_ffn, d_h1), (gp_["post2_g"],) = _rows_bwd("final_bwd", _fn_final, [(tg2, [D])], final_rows, [p["post2_g"]], [], [], n_sums=1)
    d_hmid = _matmul("ffn_down_dx", d_ffn, w["w_ffn_down"], "nt")
    gw["w_ffn_down"] = _matmul("ffn_down_dw", hmid, d_ffn, "tn")
    (d_gp, d_up), _ = _rows_bwd("swiglu_bwd", _fn_swiglu, [], [(gp, [D_FF]), (up, [D_FF])], [], [[D_FF]], [d_hmid])
    d_u2 = _matmul("ffn_gate_dx", d_gp, w["w_ffn_gate"], "nt")
    d_u2 = _matmul("ffn_up_dx", d_up, w["w_ffn_up"], "nt", add=d_u2)
    gw["w_ffn_gate"] = _matmul("ffn_gate_dw", u2, d_gp, "tn")
    gw["w_ffn_up"] = _matmul("ffn_up_dw", u2, d_up, "tn")
    (d_yy, d_x_res), (gp_["post1_g"], gp_["pre2_g"]) = _rows_bwd(
        "post1_bwd", _fn_post1, [], post1_rows, post1_params, [[D], [D]], [d_h1, d_u2])
    d_merged = _matmul("out_o_dx", d_yy, w["w_o"], "nt")
    gw["w_o"] = _matmul("out_o_dw", merged, d_yy, "tn")
    (d_a_fox, d_a_rwkv, d_a_mem, d_p_g), _ = _rows_bwd("merge_bwd", _fn_merge, [], merge_rows, [], [[D]], [d_merged])
    d_fox_out = _matmul("out_fox_dx", d_a_fox, w["w_fox_out"], "nt")
    gw["w_fox_out"] = _matmul("out_fox_dw", fox_out, d_a_fox, "tn")
    d_rwkv_out = _matmul("out_rwkv_dx", d_a_rwkv, w["w_rwkv_out"], "nt")
    gw["w_rwkv_out"] = _matmul("out_rwkv_dw", rwkv_out, d_a_rwkv, "tn")
    d_mem_out = _matmul("out_mem_dx", d_a_mem, w["w_mem_out"], "nt")
    gw["w_mem_out"] = _matmul("out_mem_dw", mem_out, d_a_mem, "tn")

    d_p_mq, d_km, d_vm = _mem_bwd(p_mq, mem_kv, d_mem_out, batch, seq)
    d_mem_kv = jnp.concatenate([d_km, d_vm], axis=1)
    gw["w_mem_kv"] = _matmul("proj_memkv_dw", memn, d_mem_kv, "tn")
    d_memn = _matmul("proj_memkv_dx", d_mem_kv, w["w_mem_kv"], "nt")
    _, (gp_["mem_norm_g"],) = _rows_bwd("rms_mem_bwd", _fn_rms, [], [(mem2, [D])], [p["mem_norm_g"]], [[D]], [d_memn])

    d_fox_hm = _to_heads(d_fox_out, batch, seq, 1)[0]
    d_q, d_k, d_v, d_ccol, d_crow = _fox_bwd(qkv[0], qkv[1], qkv[2], ccol, crow, d_fox_hm)
    d_p_qkv = _from_heads(jnp.stack([d_q, d_k, d_v]), batch, seq, 3)

    def c_layout(dc):
        return _pad_cols(dc.reshape(batch, HEADS, seq).transpose(0, 2, 1).reshape(t, HEADS), 128)

    d_p_f, d_bias = _fox_gate_bwd(p_f, bias, c_layout(d_ccol), c_layout(d_crow), batch, seq)
    gp_["fox_f_bias"] = d_bias[:, :HEADS]

    (d_y_rw, d_main6_post, d_g_rw), (gp_["rwkv_gn_g"], gp_["rwkv_gn_b"], d_rk) = _rows_bwd(
        "rwkv_post_bwd", fn_post, post_consts, post_rows, post_params, [[HW]], [d_rwkv_out])
    gp_["rwkv_r_k"] = d_rk.reshape(1, HEADS, HD)
    d_z = _scan_bwd(z, states, _to_heads(d_y_rw, batch, seq, 1)[0], hb)
    d_main6 = _from_heads(d_z, batch, seq, 6)

    def fn_pre_sum(*args):
        return _fn_rwkv_pre(*args)

    (d_ps,), d_pre = _rows_bwd("rwkv_pre_bwd", fn_pre_sum, [], [(ps, rw_widths)], pre_params, [six, [HW]],
                               [_rows_add("rwkv_dmain6", d_main6, d_main6_post), d_g_rw])
    gp_["rwkv_w0"], d_w_up, gp_["rwkv_a0"], d_a_up, gw["rwkv_g_up"], gp_["rwkv_k_k"], gp_["rwkv_k_a"] = d_pre
    gw["rwkv_w_up"], gw["rwkv_a_up"] = d_w_up[:64], d_a_up[:64]
    d_p_r, d_mu = _tokshift_bwd(p_r, mu, d_ps, batch, seq)
    gp_["rwkv_mu"] = _unpad_lora(d_mu)

    d_u = _matmul("proj_qkv_dx", d_p_qkv, w_qkv, "nt")
    d_u = _matmul("proj_f_dx", d_p_f, w_f, "nt", add=d_u)
    d_u = _matmul("proj_rwkv_dx", d_p_r, w_r, "nt", add=d_u)
    d_u = _matmul("proj_memq_dx", d_p_mq, w_mq, "nt", add=d_u)
    d_u = _matmul("proj_gate_dx", d_p_g, w_g3, "nt", add=d_u)
    gw["w_in"] = _merge_w_in(_matmul("proj_qkv_dw", u, d_p_qkv, "tn"), _matmul("proj_f_dw", u, d_p_f, "tn"),
                             _matmul("proj_rwkv_dw", u, d_p_r, "tn"), _matmul("proj_memq_dw", u, d_p_mq, "tn"),
                             _matmul("proj_gate_dw", u, d_p_g, "tn"))
    (d_x,), (gp_["pre1_g"],) = _rows_bwd("rms_pre1_bwd", _fn_rms, [], [(x2, [D])], [p["pre1_g"]], [[D]], [d_u], add=d_x_res)
    return loss, d_x.reshape(x.shape), gw, gp_


def _rows_add(name, a, b):
    (s,) = _rows_fwd(name, lambda u, v: (u + v,), [], [(a, [a.shape[1]]), (b, [b.shape[1]])], [], [[a.shape[1]]])
    return s


def _mesh_place():
    x, y, c = lax.axis_index("x"), lax.axis_index("y"), lax.axis_index("c")
    return x, y, c


def _exchange(name, src, per_peer):
    shape = src.shape[1:] if per_peer else src.shape

    def body(src_ref, out_ref, send_sems, recv_sems, local_sem):
        x, y, c = _mesh_place()
        me = 4 * x + 2 * y + c
        mine = src_ref.at[me] if per_peer else src_ref
        local = pltpu.make_async_copy(mine, out_ref.at[me], local_sem)
        local.start()
        copies = []
        for k in range(1, N_DEV):
            px = 1 - x if k & 4 else x
            py = 1 - y if k & 2 else y
            pc = 1 - c if k & 1 else c
            peer = 4 * px + 2 * py + pc
            copies.append(pltpu.make_async_remote_copy(
                src_ref=src_ref.at[peer] if per_peer else src_ref, dst_ref=out_ref.at[me],
                send_sem=send_sems.at[k - 1], recv_sem=recv_sems.at[k - 1],
                device_id=(px, py, pc), device_id_type=pl.DeviceIdType.MESH))
        for cp in copies:
            cp.start()
        for cp in copies:
            cp.wait()
        local.wait()

    return pl.pallas_call(
        body, name=name,
        in_specs=[pl.BlockSpec(memory_space=pltpu.HBM)], out_specs=pl.BlockSpec(memory_space=pltpu.HBM),
        out_shape=jax.ShapeDtypeStruct((N_DEV,) + tuple(shape), src.dtype),
        scratch_shapes=[pltpu.SemaphoreType.DMA((N_DEV - 1,)), pltpu.SemaphoreType.DMA((N_DEV - 1,)), pltpu.SemaphoreType.DMA],
    )(src)


def _adamw(recv, w, m, v):
    rows = w.shape[0]
    tr = 128

    def body(g_ref, w_ref, m_ref, v_ref, go_ref, d_ref, mo_ref, vo_ref):
        g = g_ref[0]
        for s in range(1, N_DEV):
            g = g + g_ref[s]
        m_new = ADAM_B1 * m_ref[...] + (1.0 - ADAM_B1) * g
        v_new = ADAM_B2 * v_ref[...] + (1.0 - ADAM_B2) * (g * g)
        m_hat = m_new / (1.0 - ADAM_B1 ** ADAM_STEP)
        v_hat = v_new / (1.0 - ADAM_B2 ** ADAM_STEP)
        go_ref[...] = g
        d_ref[...] = -ADAM_LR * (m_hat / (jnp.sqrt(v_hat) + ADAM_EPS) + ADAM_WD * w_ref[...])
        mo_ref[...] = m_new
        vo_ref[...] = v_new

    spec = pl.BlockSpec((tr, LANES), lambda i: (i, 0))
    return pl.pallas_call(
        body, name="adamw", grid=(rows // tr,),
        in_specs=[pl.BlockSpec((N_DEV, tr, LANES), lambda i: (0, i, 0)), spec, spec, spec],
        out_specs=[spec] * 4, out_shape=[jax.ShapeDtypeStruct(w.shape, f32)] * 4,
        compiler_params=_cp(("parallel",)),
    )(recv, w, m, v)


def _to_blocks(full, axis):
    if axis == 0:
        b = full.reshape(N_DEV, full.shape[0] // N_DEV, full.shape[1])
    else:
        b = full.reshape(full.shape[0], N_DEV, full.shape[1] // N_DEV).transpose(1, 0, 2)
    return b.reshape(N_DEV, -1, LANES)


def _from_blocks(blocks, shape, axis):
    b0, b1 = _block_shape(shape, axis)
    b = blocks.reshape(N_DEV, b0, b1)
    return b.reshape(shape) if axis == 0 else b.transpose(1, 0, 2).reshape(shape)


def _pack_local(shards, repl, pad_rows):
    parts = [shards[n].reshape(-1, LANES) for n, _, _ in SHARDED]
    rows = SHARD_ROWS
    if repl is not None:
        flat = jnp.concatenate([repl[n].reshape(-1) for n, _ in REPLICATED])
        parts.append(jnp.pad(flat, (0, REPL_ROWS * LANES - REPL_ELEMS)).reshape(REPL_ROWS, LANES))
        rows += REPL_ROWS
    parts.append(jnp.zeros((pad_rows - rows, LANES), parts[0].dtype))
    return jnp.concatenate(parts, axis=0)


def _unpack_local(packed):
    out, r = {}, 0
    for n, shape, axis in SHARDED:
        bs = _block_shape(shape, axis)
        k = _rows_of(bs)
        out[n] = packed[r:r + k].reshape((1,) + bs)
        r += k
    flat = packed[r:r + REPL_ROWS].reshape(-1)
    off = 0
    for n, shape in REPLICATED:
        k = 1
        for s in shape:
            k *= s
        out[n] = flat[off:off + k].reshape(shape)
        off += k
    return out


def kernel(x, mem, pre1_g, post1_g, pre2_g, post2_g, mem_norm_g, w_in, fox_f_bias, rwkv_mu, rwkv_w0, rwkv_w_up, rwkv_a0, rwkv_a_up, rwkv_g_up, rwkv_k_k, rwkv_k_a, rwkv_r_k, rwkv_gn_g, rwkv_gn_b, w_mem_kv, w_fox_out, w_rwkv_out, w_mem_out, w_o, w_ffn_gate, w_ffn_up, w_ffn_down, loss_target, m_pre1_g, m_post1_g, m_pre2_g, m_post2_g, m_mem_norm_g, m_w_in, m_fox_f_bias, m_rwkv_mu, m_rwkv_w0, m_rwkv_w_up, m_rwkv_a0, m_rwkv_a_up, m_rwkv_g_up, m_rwkv_k_k, m_rwkv_k_a, m_rwkv_r_k, m_rwkv_gn_g, m_rwkv_gn_b, m_w_mem_kv, m_w_fox_out, m_w_rwkv_out, m_w_mem_out, m_w_o, m_w_ffn_gate, m_w_ffn_up, m_w_ffn_down, v_pre1_g, v_post1_g, v_pre2_g, v_post2_g, v_mem_norm_g, v_w_in, v_fox_f_bias, v_rwkv_mu, v_rwkv_w0, v_rwkv_w_up, v_rwkv_a0, v_rwkv_a_up, v_rwkv_g_up, v_rwkv_k_k, v_rwkv_k_a, v_rwkv_r_k, v_rwkv_gn_g, v_rwkv_gn_b, v_w_mem_kv, v_w_fox_out, v_w_rwkv_out, v_w_mem_out, v_w_o, v_w_ffn_gate, v_w_ffn_up, v_w_ffn_down):
    args = dict(locals())
    wts = {n: args[n] for n in WEIGHT_ORDER}
    ms = {n: args["m_" + n] for n in WEIGHT_ORDER}
    vs = {n: args["v_" + n] for n in WEIGHT_ORDER}

    w_send = _pack_local({n: wts[n].astype(bf16) for n, _, _ in SHARDED}, None, GATHER_ROWS)
    w_all = _exchange("gather_weights", w_send, per_peer=False)
    full, r = {}, 0
    for n, shape, axis in SHARDED:
        k = _rows_of(_block_shape(shape, axis))
        full[n] = _from_blocks(w_all[:, r:r + k], shape, axis)
        r += k
    small = {n: wts[n] for n, _ in REPLICATED}
    small_in = dict(small)
    small_in.update({n: full[n].astype(f32) for n in ("rwkv_w_up", "rwkv_a_up", "rwkv_g_up")})
    small_in = {n: (v if n == "rwkv_r_k" else v.reshape(v.shape[-2:])) for n, v in small_in.items()}
    small_in["rwkv_r_k"] = small["rwkv_r_k"]

    loss_part, grad_x, gw, gp = _local_step(x, mem, loss_target, full, small_in)

    flat = jnp.concatenate([gp[n].reshape(-1) for n, _ in REPLICATED])
    tail = jnp.pad(flat, (0, (PACK_ROWS - SHARD_ROWS) * LANES - REPL_ELEMS)).reshape(1, PACK_ROWS - SHARD_ROWS, LANES)
    g_send = jnp.concatenate([_to_blocks(gw[n], axis) for n, _, axis in SHARDED]
                             + [jnp.broadcast_to(tail, (N_DEV,) + tail.shape[1:])], axis=1)
    g_recv = _exchange("exchange_grads", g_send, per_peer=True)

    def local(d):
        return _pack_local({n: d[n] for n, _, _ in SHARDED}, {n: d[n] for n, _ in REPLICATED}, PACK_ROWS)

    g_sum, delta, m_new, v_new = _adamw(g_recv, local(wts), local(ms), local(vs))
    outs = [_unpack_local(a) for a in (g_sum, delta, m_new, v_new)]
    loss = lax.psum(loss_part[0, 0], ("x", "y", "c"))
    return (loss, grad_x, *[o[n].reshape(wts[n].shape) for o in outs for n in WEIGHT_ORDER])
```

```python
import functools

import jax
import jax.numpy as jnp
from jax import lax
from jax.experimental import pallas as pl
from jax.experimental.pallas import tpu as pltpu

f32 = jnp.float32
bf16 = jnp.bfloat16
_HI = lax.Precision.HIGHEST

D = 1024
HEADS = 8
HD = 64
HW = HEADS * HD
MEM_HEADS = 4
MEM_HD = 128
MEM_W = 512
MEM_LEN = 256
D_FF = 2816
LORA_PAD = 128
RW_COLS = 3 * HW + 3 * LORA_PAD
NORM_EPS = 1e-6
GN_EPS = 64e-5
Q_BLOCK = 128
SCAN_CHUNK = 64
N_DEV = 8
LANES = 1024
VMEM_LIMIT = 56 * 1024 * 1024

ADAM_LR = 0.001
ADAM_B1 = 0.9
ADAM_B2 = 0.999
ADAM_EPS = 1e-08
ADAM_WD = 0.01
ADAM_STEP = 10

SHARDED = (
    ("w_in", (1024, 6920), 1),
    ("w_ffn_gate", (1024, 2816), 1),
    ("w_ffn_up", (1024, 2816), 1),
    ("w_ffn_down", (2816, 1024), 0),
    ("w_mem_kv", (1024, 1024), 0),
    ("w_o", (1024, 1024), 0),
    ("w_fox_out", (512, 1024), 1),
    ("w_rwkv_out", (512, 1024), 1),
    ("w_mem_out", (512, 1024), 1),
    ("rwkv_w_up", (64, 512), 1),
    ("rwkv_a_up", (64, 512), 1),
    ("rwkv_g_up", (128, 512), 1),
)
REPLICATED = (
    ("pre1_g", (1, 1024)), ("post1_g", (1, 1024)), ("pre2_g", (1, 1024)), ("post2_g", (1, 1024)),
    ("mem_norm_g", (1, 1024)), ("fox_f_bias", (1, 8)), ("rwkv_mu", (1, 1792)), ("rwkv_w0", (1, 512)),
    ("rwkv_a0", (1, 512)), ("rwkv_k_k", (1, 512)), ("rwkv_k_a", (1, 512)), ("rwkv_r_k", (1, 8, 64)),
    ("rwkv_gn_g", (1, 512)), ("rwkv_gn_b", (1, 512)),
)
WEIGHT_ORDER = ('pre1_g', 'post1_g', 'pre2_g', 'post2_g', 'mem_norm_g', 'w_in', 'fox_f_bias', 'rwkv_mu',
                'rwkv_w0', 'rwkv_w_up', 'rwkv_a0', 'rwkv_a_up', 'rwkv_g_up', 'rwkv_k_k', 'rwkv_k_a',
                'rwkv_r_k', 'rwkv_gn_g', 'rwkv_gn_b', 'w_mem_kv', 'w_fox_out', 'w_rwkv_out', 'w_mem_out',
                'w_o', 'w_ffn_gate', 'w_ffn_up', 'w_ffn_down')


def _block_shape(shape, axis):
    return tuple(s // N_DEV if i == axis else s for i, s in enumerate(shape))


def _rows_of(shape):
    n = 1
    for s in shape:
        n *= s
    return n // LANES


SHARD_ROWS = sum(_rows_of(_block_shape(s, a)) for _, s, a in SHARDED)
REPL_ELEMS = sum(_rows_of((LANES,) + s) for _, s in REPLICATED)
REPL_ROWS = -(-REPL_ELEMS // LANES)
PACK_ROWS = -(-(SHARD_ROWS + REPL_ROWS) // 128) * 128
GATHER_ROWS = -(-SHARD_ROWS // 16) * 16


def _cp(sem=None):
    return pltpu.CompilerParams(dimension_semantics=sem, vmem_limit_bytes=VMEM_LIMIT)


def _tile(dim, cap):
    best = None
    for t in range(128, min(dim, cap) + 1, 128):
        if dim % t == 0:
            best = t
    return best if best is not None else dim


def _dg(a, b, dims, exact):
    if exact:
        return lax.dot_general(a, b, dims, precision=_HI, preferred_element_type=f32)
    return lax.dot_general(a.astype(bf16), b.astype(bf16), dims, preferred_element_type=f32)


def _make_mm(batched, exact):
    o = 1 if batched else 0
    bd = ((0,), (0,)) if batched else ((), ())
    d_nn = (((1 + o,), (o,)), bd)
    d_nt = (((1 + o,), (1 + o,)), bd)
    d_tn = (((o,), (o,)), bd)

    @jax.custom_vjp
    def nn(a, b):
        return _dg(a, b, d_nn, exact)

    @jax.custom_vjp
    def nt(a, b):
        return _dg(a, b, d_nt, exact)

    @jax.custom_vjp
    def tn(a, b):
        return _dg(a, b, d_tn, exact)

    nn.defvjp(lambda a, b: (_dg(a, b, d_nn, exact), (a, b)),
              lambda res, g: (_dg(g, res[1], d_nt, exact), _dg(res[0], g, d_tn, exact)))
    nt.defvjp(lambda a, b: (_dg(a, b, d_nt, exact), (a, b)),
              lambda res, g: (_dg(g, res[1], d_nn, exact), _dg(g, res[0], d_tn, exact)))
    tn.defvjp(lambda a, b: (_dg(a, b, d_tn, exact), (a, b)),
              lambda res, g: (_dg(res[1], g, d_nt, exact), _dg(res[0], g, d_nn, exact)))
    return nn, nt, tn


def _sigmoid(x):
    return 1.0 / (1.0 + jnp.exp(-x))


def _head_sum_matrix():
    i = lax.broadcasted_iota(jnp.int32, (HW, HW), 0) // HD
    j = lax.broadcasted_iota(jnp.int32, (HW, HW), 1) // HD
    return (i == j).astype(f32)


def _head_sum_raw(x):
    return _dg(x, _head_sum_matrix(), (((1,), (0,)), ((), ())), True)


@jax.custom_vjp
def _head_sum(x):
    return _head_sum_raw(x)


_head_sum.defvjp(lambda x: (_head_sum_raw(x), None), lambda _, g: (_head_sum_raw(g),))


WEIGHT_TILE_BYTES = 13 * 512 * 1024
ACC_TILE_BYTES = 8 * 1024 * 1024


def _matmul(name, a, b, mode, add=None, out_dtype=f32):
    has_add = add is not None
    if mode == "tn":
        (k, m), (_, n) = a.shape, b.shape
        tn = _tile(n, max(128, ACC_TILE_BYTES // (4 * m)))
        tk = _tile(k, 1024)

        def body(a_ref, b_ref, o_ref):
            @pl.when(pl.program_id(1) == 0)
            def _():
                o_ref[...] = jnp.zeros_like(o_ref)

            o_ref[...] += lax.dot_general(a_ref[...].astype(bf16), b_ref[...].astype(bf16),
                                          (((0,), (0,)), ((), ())), preferred_element_type=f32)

        return pl.pallas_call(
            body, name=name, grid=(n // tn, k // tk),
            in_specs=[pl.BlockSpec((tk, m), lambda j, kk: (kk, 0)), pl.BlockSpec((tk, tn), lambda j, kk: (kk, j))],
            out_specs=pl.BlockSpec((m, tn), lambda j, kk: (0, j)), out_shape=jax.ShapeDtypeStruct((m, n), f32),
            compiler_params=_cp(("parallel", "arbitrary")),
        )(a, b)

    (m, k) = a.shape
    n = b.shape[1] if mode == "nn" else b.shape[0]
    tm = _tile(m, 512)
    tn = _tile(n, max(128, WEIGHT_TILE_BYTES // (2 * k)))
    dims = (((1,), (0,)), ((), ())) if mode == "nn" else (((1,), (1,)), ((), ()))
    b_spec = pl.BlockSpec((k, tn), lambda j, i: (0, j)) if mode == "nn" else pl.BlockSpec((tn, k), lambda j, i: (j, 0))
    o_spec = pl.BlockSpec((tm, tn), lambda j, i: (i, j))

    def body(*refs):
        a_ref, b_ref = refs[0], refs[1]
        o_ref = refs[-1]
        r = lax.dot_general(a_ref[...].astype(bf16), b_ref[...].astype(bf16), dims, preferred_element_type=f32)
        if has_add:
            r = r + refs[2][...]
        o_ref[...] = r.astype(o_ref.dtype)

    return pl.pallas_call(
        body, name=name, grid=(n // tn, m // tm),
        in_specs=[pl.BlockSpec((tm, k), lambda j, i: (i, 0)), b_spec] + ([o_spec] if has_add else []),
        out_specs=o_spec, out_shape=jax.ShapeDtypeStruct((m, n), out_dtype),
        compiler_params=_cp(("parallel", "arbitrary")),
    )(*((a, b, add) if has_add else (a, b)))


def _pieces(ref, widths):
    out, off = [], 0
    for w in widths:
        out.append(ref[:, off:off + w].astype(f32))
        off += w
    return out


def _store_pieces(ref, widths, vals, add_ref=None):
    off = 0
    for w, v in zip(widths, vals):
        ref[:, off:off + w] = (v if add_ref is None else v + add_ref[:, off:off + w]).astype(ref.dtype)
        off += w


def _rows_fwd(name, fn, consts, rows, params, outs, n_sums=0, tm=256, dtypes=None):
    t = (consts + rows)[0][0].shape[0]
    tm = min(tm, t)
    ins = consts + rows
    n_in, n_p, n_o = len(ins), len(params), len(outs)
    dtypes = dtypes or [f32] * n_o

    def body(*refs):
        in_refs, p_refs = refs[:n_in], refs[n_in:n_in + n_p]
        o_refs, s_refs = refs[n_in + n_p:n_in + n_p + n_o], refs[n_in + n_p + n_o:]
        vals = []
        for r, (_, widths) in zip(in_refs, ins):
            vals += _pieces(r, widths)
        res = fn(*vals, *[p[...] for p in p_refs])
        pos = 0
        for r, widths in zip(o_refs, outs):
            _store_pieces(r, widths, res[pos:pos + len(widths)])
            pos += len(widths)

        @pl.when(pl.program_id(0) == 0)
        def _():
            for s in s_refs:
                s[...] = jnp.zeros_like(s)

        for s, v in zip(s_refs, res[pos:]):
            s[...] += v

    row_spec = lambda w: pl.BlockSpec((tm, w), lambda i: (i, 0))
    full = lambda p: pl.BlockSpec(p.shape, lambda i: (0,) * p.ndim)
    return pl.pallas_call(
        body, name=name, grid=(t // tm,),
        in_specs=[row_spec(a.shape[1]) for a, _ in ins] + [full(p) for p in params],
        out_specs=[row_spec(sum(w)) for w in outs] + [pl.BlockSpec((1, 1), lambda i: (0, 0))] * n_sums,
        out_shape=[jax.ShapeDtypeStruct((t, sum(w)), dt) for w, dt in zip(outs, dtypes)] + [jax.ShapeDtypeStruct((1, 1), f32)] * n_sums,
        compiler_params=_cp(("arbitrary",)),
    )(*[a for a, _ in ins], *params)


def _rows_bwd(name, fn, consts, rows, params, outs, cts, n_sums=0, add=None, tm=256, dtypes=None):
    t = (consts + rows)[0][0].shape[0]
    tm = min(tm, t)
    n_c, n_r, n_p, n_o = len(consts), len(rows), len(params), len(outs)
    has_add = add is not None
    dtypes = dtypes or [f32] * n_r

    def body(*refs):
        pos = 0
        c_refs = refs[pos:pos + n_c]; pos += n_c
        r_refs = refs[pos:pos + n_r]; pos += n_r
        p_refs = refs[pos:pos + n_p]; pos += n_p
        ct_refs = refs[pos:pos + n_o]; pos += n_o
        add_ref = refs[pos] if has_add else None
        pos += 1 if has_add else 0
        dr_refs = refs[pos:pos + n_r]; pos += n_r
        dp_refs = refs[pos:pos + n_p]
        cvals, rvals = [], []
        for r, (_, widths) in zip(c_refs, consts):
            cvals += _pieces(r, widths)
        for r, (_, widths) in zip(r_refs, rows):
            rvals += _pieces(r, widths)
        pvals = [p[...] for p in p_refs]
        ctv = []
        for r, widths in zip(ct_refs, outs):
            ctv += _pieces(r, widths)
        ctv += [jnp.ones((1, 1), f32)] * n_sums
        _, vjp = jax.vjp(lambda *rp: tuple(fn(*cvals, *rp)), *rvals, *pvals)
        g = vjp(tuple(ctv))
        pos = 0
        for idx, (r, (_, widths)) in enumerate(zip(dr_refs, rows)):
            _store_pieces(r, widths, g[pos:pos + len(widths)], add_ref if idx == 0 else None)
            pos += len(widths)

        @pl.when(pl.program_id(0) == 0)
        def _():
            for dp in dp_refs:
                dp[...] = jnp.zeros_like(dp)

        for dp, v in zip(dp_refs, g[pos:]):
            dp[...] += v

    row_spec = lambda w: pl.BlockSpec((tm, w), lambda i: (i, 0))
    full = lambda p: pl.BlockSpec(p.shape, lambda i: (0,) * p.ndim)
    args = [a for a, _ in consts + rows] + list(params) + list(cts) + ([add] if has_add else [])
    res = pl.pallas_call(
        body, name=name, grid=(t // tm,),
        in_specs=[row_spec(a.shape[1]) for a, _ in consts + rows] + [full(p) for p in params]
        + [row_spec(sum(w)) for w in outs] + ([row_spec(add.shape[1])] if has_add else []),
        out_specs=[row_spec(a.shape[1]) for a, _ in rows] + [full(p) for p in params],
        out_shape=[jax.ShapeDtypeStruct(a.shape, dt) for (a, _), dt in zip(rows, dtypes)]
        + [jax.ShapeDtypeStruct(p.shape, f32) for p in params],
        compiler_params=_cp(("arbitrary",)),
    )(*args)
    return res[:n_r], res[n_r:]


def _rms(x, g):
    return x * lax.rsqrt(jnp.mean(x * x, axis=-1, keepdims=True) + NORM_EPS) * g


def _fn_rms(x, g):
    return (_rms(x, g),)


def _fn_rwkv_pre(r, k, v, wd, ad, gd, w0, w_up, a0, a_up, g_up, k_k, k_a):
    nn, _, _ = _make_mm(False, False)
    w_log = -_sigmoid(w0 + nn(jnp.tanh(wd), w_up)) * 0.6065306597126334
    a = _sigmoid(a0 + nn(ad, a_up))
    g = nn(_sigmoid(gd), g_up)
    kk = k * k_k
    kk = kk * lax.rsqrt(jnp.maximum(_head_sum(kk * kk), 1e-24))
    k2 = k * (1.0 + (a - 1.0) * k_a)
    return r, w_log, k2, v, -kk, kk * a, g


def _fn_rwkv_post(y, r, k2, v, g, gn_g, gn_b, r_k):
    mean = _head_sum(y) * (1.0 / HD)
    yc = y - mean
    var = _head_sum(yc * yc) * (1.0 / HD)
    yn = yc * lax.rsqrt(var + GN_EPS) * gn_g + gn_b
    bonus = _head_sum(r * k2 * r_k) * v
    return ((yn + bonus) * g,)


def _fn_merge(a_fox, a_rwkv, a_mem, g_fox, g_rwkv, g_mem):
    return (_sigmoid(g_fox) * a_fox + _sigmoid(g_rwkv) * a_rwkv + _sigmoid(g_mem) * a_mem,)


def _fn_post1(y, x, post1_g, pre2_g):
    h1 = x + _rms(y, post1_g)
    return h1, _rms(h1, pre2_g)


def _fn_swiglu(gp, up):
    return (gp * _sigmoid(gp) * up,)


def _fn_final(target, ffn, h1, post2_g):
    err = h1 + _rms(ffn, post2_g) - target
    per_row = jnp.mean(err * err, axis=-1, keepdims=True)
    return (0.5 * jnp.sum(per_row, axis=0, keepdims=True),)


def _shift_down(x):
    row = lax.broadcasted_iota(jnp.int32, x.shape, 0)
    return jnp.where(row == 0, 0.0, pltpu.roll(x, 1, 0))


def _shift_up(x):
    s = x.shape[0]
    row = lax.broadcasted_iota(jnp.int32, x.shape, 0)
    return jnp.where(row == s - 1, 0.0, pltpu.roll(x, s - 1, 0))


def _tokshift_fwd(p, mu, batch, seq):
    w = p.shape[1]
    tc = _tile(w, 384)

    def body(p_ref, mu_ref, o_ref):
        x = p_ref[...]
        o_ref[...] = x + (_shift_down(x) - x) * mu_ref[...]

    return pl.pallas_call(
        body, name="tokshift_fwd", grid=(w // tc, batch),
        in_specs=[pl.BlockSpec((seq, tc), lambda j, b: (b, j)), pl.BlockSpec((1, tc), lambda j, b: (0, j))],
        out_specs=pl.BlockSpec((seq, tc), lambda j, b: (b, j)),
        out_shape=jax.ShapeDtypeStruct(p.shape, f32),
        compiler_params=_cp(("parallel", "arbitrary")),
    )(p, mu)


def _tokshift_bwd(p, mu, dps, batch, seq):
    w = p.shape[1]
    tc = _tile(w, 384)

    def body(p_ref, mu_ref, d_ref, dp_ref, dmu_ref):
        x, mu_v, d = p_ref[...], mu_ref[...], d_ref[...]
        dp_ref[...] = (d * (1.0 - mu_v) + _shift_up(d * mu_v)).astype(dp_ref.dtype)

        @pl.when(pl.program_id(1) == 0)
        def _():
            dmu_ref[...] = jnp.zeros_like(dmu_ref)

        dmu_ref[...] += jnp.sum(d * (_shift_down(x) - x), axis=0, keepdims=True)

    return pl.pallas_call(
        body, name="tokshift_bwd", grid=(w // tc, batch),
        in_specs=[pl.BlockSpec((seq, tc), lambda j, b: (b, j)), pl.BlockSpec((1, tc), lambda j, b: (0, j)),
                  pl.BlockSpec((seq, tc), lambda j, b: (b, j))],
        out_specs=[pl.BlockSpec((seq, tc), lambda j, b: (b, j)), pl.BlockSpec((1, tc), lambda j, b: (0, j))],
        out_shape=[jax.ShapeDtypeStruct(p.shape, bf16), jax.ShapeDtypeStruct(mu.shape, f32)],
        compiler_params=_cp(("parallel", "arbitrary")),
    )(p, mu, dps)


def _cum_block(seq):
    return _tile(seq, 256)


def _fox_gate_fwd(f, bias, batch, seq):
    cb = _cum_block(seq)

    def body(f_ref, b_ref, c_ref):
        row = lax.broadcasted_iota(jnp.int32, (cb, cb), 0)
        col = lax.broadcasted_iota(jnp.int32, (cb, cb), 1)
        tri = (col <= row).astype(f32)
        carry = jnp.zeros((1, 128), f32)
        for i in range(seq // cb):
            z = f_ref[i * cb:(i + 1) * cb, :] + b_ref[...]
            ls = jnp.minimum(z, 0.0) - jnp.log(1.0 + jnp.exp(-jnp.abs(z)))
            c = _dg(tri, ls, (((1,), (0,)), ((), ())), True) + carry
            c_ref[i * cb:(i + 1) * cb, :] = c
            carry = c[cb - 1:cb, :]

    return pl.pallas_call(
        body, name="fox_gate_fwd", grid=(batch,),
        in_specs=[pl.BlockSpec((seq, 128), lambda b: (b, 0)), pl.BlockSpec((1, 128), lambda b: (0, 0))],
        out_specs=pl.BlockSpec((seq, 128), lambda b: (b, 0)),
        out_shape=jax.ShapeDtypeStruct(f.shape, f32),
        compiler_params=_cp(("arbitrary",)),
    )(f, bias)


def _fox_gate_bwd(f, bias, dc_a, dc_b, batch, seq):
    cb = _cum_block(seq)

    def body(f_ref, b_ref, da_ref, db_ref, df_ref, dbias_ref):
        row = lax.broadcasted_iota(jnp.int32, (cb, cb), 0)
        col = lax.broadcasted_iota(jnp.int32, (cb, cb), 1)
        triu = (col >= row).astype(f32)

        @pl.when(pl.program_id(0) == 0)
        def _():
            dbias_ref[...] = jnp.zeros_like(dbias_ref)

        carry = jnp.zeros((1, 128), f32)
        tot = jnp.zeros((1, 128), f32)
        for i in reversed(range(seq // cb)):
            sl = slice(i * cb, (i + 1) * cb)
            dc = da_ref[sl, :] + db_ref[sl, :]
            dls = _dg(triu, dc, (((1,), (0,)), ((), ())), True) + carry
            carry = dls[0:1, :]
            df = dls * _sigmoid(-(f_ref[sl, :] + b_ref[...]))
            df_ref[sl, :] = df.astype(df_ref.dtype)
            tot = tot + jnp.sum(df, axis=0, keepdims=True)
        dbias_ref[...] += tot

    return pl.pallas_call(
        body, name="fox_gate_bwd", grid=(batch,),
        in_specs=[pl.BlockSpec((seq, 128), lambda b: (b, 0)), pl.BlockSpec((1, 128), lambda b: (0, 0)),
                  pl.BlockSpec((seq, 128), lambda b: (b, 0)), pl.BlockSpec((seq, 128), lambda b: (b, 0))],
        out_specs=[pl.BlockSpec((seq, 128), lambda b: (b, 0)), pl.BlockSpec((1, 128), lambda b: (0, 0))],
        out_shape=[jax.ShapeDtypeStruct(f.shape, bf16), jax.ShapeDtypeStruct((1, 128), f32)],
        compiler_params=_cp(("arbitrary",)),
    )(f, bias, dc_a, dc_b)


def _fox_block(q, k, v, cq, ck, q0):
    nn, nt, _ = _make_mm(False, False)
    tq, s = q.shape[0], k.shape[0]
    logits = nt(q, k) * (HD ** -0.5) + (cq - ck)
    qi = q0 + lax.broadcasted_iota(jnp.int32, (tq, s), 0)
    ki = lax.broadcasted_iota(jnp.int32, (tq, s), 1)
    logits = jnp.where(ki <= qi, logits, -1e30)
    m = lax.stop_gradient(jnp.max(logits, axis=-1, keepdims=True))
    e = jnp.exp(logits - m)
    p = e / jnp.sum(e, axis=-1, keepdims=True)
    return nn(p, v)


def _fox_specs(seq):
    qb = pl.BlockSpec((1, Q_BLOCK, HD), lambda h, i: (h, i, 0))
    kb = pl.BlockSpec((1, seq, HD), lambda h, i: (h, 0, 0))
    cq = pl.BlockSpec((1, Q_BLOCK, 1), lambda h, i: (h, i, 0))
    ck = pl.BlockSpec((1, 1, seq), lambda h, i: (h, 0, 0))
    return qb, kb, cq, ck


def _fox_fwd(q, k, v, ccol, crow):
    bh, seq, _ = q.shape
    qb, kb, cqs, cks = _fox_specs(seq)

    def body(q_ref, k_ref, v_ref, cq_ref, ck_ref, o_ref):
        q0 = pl.program_id(1) * Q_BLOCK
        o_ref[0] = _fox_block(q_ref[0], k_ref[0], v_ref[0], cq_ref[0], ck_ref[0], q0)

    return pl.pallas_call(
        body, name="fox_attn_fwd", grid=(bh, seq // Q_BLOCK),
        in_specs=[qb, kb, kb, cqs, cks], out_specs=qb,
        out_shape=jax.ShapeDtypeStruct(q.shape, f32),
        compiler_params=_cp(("parallel", "arbitrary")),
    )(q, k, v, ccol, crow)


def _fox_bwd(q, k, v, ccol, crow, do):
    bh, seq, _ = q.shape
    qb, kb, cqs, cks = _fox_specs(seq)

    def body(q_ref, k_ref, v_ref, cq_ref, ck_ref, do_ref, dq_ref, dk_ref, dv_ref, dcq_ref, dck_ref):
        q0 = pl.program_id(1) * Q_BLOCK
        _, vjp = jax.vjp(functools.partial(_fox_block, q0=q0), q_ref[0], k_ref[0], v_ref[0], cq_ref[0], ck_ref[0])
        dq, dk, dv, dcq, dck = vjp(do_ref[0])
        dq_ref[0] = dq
        dcq_ref[0] = dcq

        @pl.when(pl.program_id(1) == 0)
        def _():
            dk_ref[...] = jnp.zeros_like(dk_ref)
            dv_ref[...] = jnp.zeros_like(dv_ref)
            dck_ref[...] = jnp.zeros_like(dck_ref)

        dk_ref[0] += dk
        dv_ref[0] += dv
        dck_ref[0] += dck

    return pl.pallas_call(
        body, name="fox_attn_bwd", grid=(bh, seq // Q_BLOCK),
        in_specs=[qb, kb, kb, cqs, cks, qb], out_specs=[qb, kb, kb, cqs, cks],
        out_shape=[jax.ShapeDtypeStruct(q.shape, f32), jax.ShapeDtypeStruct(k.shape, f32), jax.ShapeDtypeStruct(v.shape, f32),
                   jax.ShapeDtypeStruct(ccol.shape, f32), jax.ShapeDtypeStruct(crow.shape, f32)],
        compiler_params=_cp(("parallel", "arbitrary")),
    )(q, k, v, ccol, crow, do)


def _mem_block(q, km, vm):
    nn, nt, _ = _make_mm(False, False)
    logits = nt(q, km) * (MEM_HD ** -0.5)
    m = lax.stop_gradient(jnp.max(logits, axis=-1, keepdims=True))
    e = jnp.exp(logits - m)
    return nn(e / jnp.sum(e, axis=-1, keepdims=True), vm)


def _mem_specs(seq, tq):
    nq = seq // tq
    qs = pl.BlockSpec((tq, MEM_HD), lambda b, h, i: (b * nq + i, h))
    ks = pl.BlockSpec((MEM_LEN, MEM_HD), lambda b, h, i: (b, h))
    vs = pl.BlockSpec((MEM_LEN, MEM_HD), lambda b, h, i: (b, MEM_HEADS + h))
    return nq, qs, ks, vs


def _mem_fwd(q, mem_kv, batch, seq):
    tq = min(512, seq)
    nq, qs, ks, vs = _mem_specs(seq, tq)

    def body(q_ref, k_ref, v_ref, o_ref):
        o_ref[...] = _mem_block(q_ref[...], k_ref[...], v_ref[...]).astype(o_ref.dtype)

    return pl.pallas_call(
        body, name="mem_attn_fwd", grid=(batch, MEM_HEADS, nq),
        in_specs=[qs, ks, vs], out_specs=qs, out_shape=jax.ShapeDtypeStruct(q.shape, bf16),
        compiler_params=_cp(("parallel", "parallel", "arbitrary")),
    )(q, mem_kv, mem_kv)


def _mem_bwd(q, mem_kv, do, batch, seq):
    tq = min(512, seq)
    nq, qs, ks, vs = _mem_specs(seq, tq)

    def body(q_ref, k_ref, v_ref, do_ref, dq_ref, dk_ref, dv_ref):
        _, vjp = jax.vjp(_mem_block, q_ref[...], k_ref[...], v_ref[...])
        dq, dk, dv = vjp(do_ref[...])
        dq_ref[...] = dq.astype(dq_ref.dtype)

        @pl.when(pl.program_id(2) == 0)
        def _():
            dk_ref[...] = jnp.zeros_like(dk_ref)
            dv_ref[...] = jnp.zeros_like(dv_ref)

        dk_ref[...] += dk
        dv_ref[...] += dv

    return pl.pallas_call(
        body, name="mem_attn_bwd", grid=(batch, MEM_HEADS, nq),
        in_specs=[qs, ks, vs, qs], out_specs=[qs, ks, ks],
        out_shape=[jax.ShapeDtypeStruct(q.shape, bf16), jax.ShapeDtypeStruct((batch * MEM_LEN, MEM_W), f32),
                   jax.ShapeDtypeStruct((batch * MEM_LEN, MEM_W), f32)],
        compiler_params=_cp(("parallel", "parallel", "arbitrary")),
    )(q, mem_kv, mem_kv, do)


@jax.custom_vjp
def _halves(x):
    c = x.shape[1] // 2
    return x[:, :c], x[:, c:]


_halves.defvjp(lambda x: ((x[:, :x.shape[1] // 2], x[:, x.shape[1] // 2:]), None),
               lambda _, g: (jnp.concatenate(g, axis=1),))


def _scan_chunk(s0, r, wl, k, v, a, b):
    nn, nt, tn = _make_mm(True, False)
    nn_exact, nt_exact, _ = _make_mm(True, True)
    hb, c, _ = r.shape
    row = lax.broadcasted_iota(jnp.int32, (c, c), 0)
    col = lax.broadcasted_iota(jnp.int32, (c, c), 1)
    tri = jnp.broadcast_to((col <= row).astype(f32)[None], (hb, c, c))
    lg = nn_exact(tri, wl)
    lg_end = lg[:, c - 1:c, :]
    grow, shrink, to_end = jnp.exp(lg), jnp.exp(-lg), jnp.exp(lg_end - lg)
    rt, kt, bt, at = r * grow, k * shrink, b * shrink, a * jnp.exp(lg - wl)
    strict, incl = (col < row)[None], (col <= row)[None]
    queries = jnp.concatenate([at, rt], axis=1)
    (ab, rb), (ak, rk) = _halves(nt_exact(queries, bt)), _halves(nt_exact(queries, kt))
    l_ab = jnp.where(strict, ab, 0.0)
    a_ak = jnp.where(strict, ak, 0.0)
    a_rb = jnp.where(incl, rb, 0.0)
    a_rk = jnp.where(incl, rk, 0.0)
    inv = (col == row).astype(f32)[None] + l_ab
    power, n = l_ab, 1
    while 2 * n < c:
        power = nn(power, power)
        inv = inv + nn(inv, power)
        n *= 2
    sa = nn(inv, nt(at, s0) + nn(a_ak, v))
    y = nt(rt, s0) + nn(a_rk, v) + nn(a_rb, sa)
    s1 = s0 * jnp.exp(lg_end) + tn(v, k * to_end) + tn(sa, b * to_end)
    return y, s1


def _scan_fwd(z, hb):
    _, bh, seq, n = z.shape
    c = min(SCAN_CHUNK, seq)
    nc = seq // c

    def body(z_ref, y_ref, s_ref, st):
        @pl.when(pl.program_id(1) == 0)
        def _():
            st[...] = jnp.zeros_like(st)

        s0 = st[...]
        s_ref[:, 0] = s0
        y, s1 = _scan_chunk(s0, z_ref[0], z_ref[1], z_ref[2], z_ref[3], z_ref[4], z_ref[5])
        y_ref[...] = y
        st[...] = s1

    return pl.pallas_call(
        body, name="rwkv_scan_fwd", grid=(bh // hb, nc),
        in_specs=[pl.BlockSpec((6, hb, c, n), lambda h, i: (0, h, i, 0))],
        out_specs=[pl.BlockSpec((hb, c, n), lambda h, i: (h, i, 0)), pl.BlockSpec((hb, 1, n, n), lambda h, i: (h, i, 0, 0))],
        out_shape=[jax.ShapeDtypeStruct((bh, seq, n), f32), jax.ShapeDtypeStruct((bh, nc, n, n), f32)],
        scratch_shapes=[pltpu.VMEM((hb, n, n), f32)],
        compiler_params=_cp(("parallel", "arbitrary")),
    )(z)


def _scan_bwd(z, states, dy, hb):
    _, bh, seq, n = z.shape
    c = min(SCAN_CHUNK, seq)
    nc = seq // c

    def body(z_ref, s_ref, dy_ref, dz_ref, dst):
        @pl.when(pl.program_id(1) == 0)
        def _():
            dst[...] = jnp.zeros_like(dst)

        _, vjp = jax.vjp(_scan_chunk, s_ref[:, 0], z_ref[0], z_ref[1], z_ref[2], z_ref[3], z_ref[4], z_ref[5])
        g = vjp((dy_ref[...], dst[...]))
        dst[...] = g[0]
        for i in range(6):
            dz_ref[i] = g[1 + i]

    return pl.pallas_call(
        body, name="rwkv_scan_bwd", grid=(bh // hb, nc),
        in_specs=[pl.BlockSpec((6, hb, c, n), lambda h, i: (0, h, nc - 1 - i, 0)),
                  pl.BlockSpec((hb, 1, n, n), lambda h, i: (h, nc - 1 - i, 0, 0)),
                  pl.BlockSpec((hb, c, n), lambda h, i: (h, nc - 1 - i, 0))],
        out_specs=pl.BlockSpec((6, hb, c, n), lambda h, i: (0, h, nc - 1 - i, 0)),
        out_shape=jax.ShapeDtypeStruct(z.shape, f32),
        scratch_shapes=[pltpu.VMEM((hb, n, n), f32)],
        compiler_params=_cp(("parallel", "arbitrary")),
    )(z, states, dy)


def _to_heads(x, batch, seq, k):
    return x.reshape(batch, seq, k, HEADS, HD).transpose(2, 0, 3, 1, 4).reshape(k, batch * HEADS, seq, HD)


def _from_heads(x, batch, seq, k):
    return x.reshape(k, batch, HEADS, seq, HD).transpose(1, 3, 0, 2, 4).reshape(batch * seq, k * HW)


def _pad_cols(x, width):
    return jnp.pad(x, ((0, 0), (0, width - x.shape[1])))


def _split_w_in(w):
    z64 = jnp.zeros((w.shape[0], 64), w.dtype)
    w_r = jnp.concatenate([w[:, 1544:3080], w[:, 3080:3144], z64, w[:, 3144:3208], z64, w[:, 3208:3336]], axis=1)
    return w[:, :1536], _pad_cols(w[:, 1536:1544], 128), w_r, w[:, 3336:3848], w[:, 3848:]


def _merge_w_in(g_qkv, g_f, g_r, g_mq, g_g):
    return jnp.concatenate([g_qkv, g_f[:, :8], g_r[:, :1536], g_r[:, 1536:1600], g_r[:, 1664:1728], g_r[:, 1792:],
                            g_mq, g_g], axis=1)


def _pad_lora(v):
    z64 = jnp.zeros((1, 64), v.dtype)
    return jnp.concatenate([v[:, :1536], v[:, 1536:1600], z64, v[:, 1600:1664], z64, v[:, 1664:]], axis=1)


def _unpad_lora(v):
    return jnp.concatenate([v[:, :1536], v[:, 1536:1600], v[:, 1664:1728], v[:, 1792:]], axis=1)


def _local_step(x, mem, target, w, p):
    batch, seq, _ = x.shape
    t = batch * seq
    x2, tg2, mem2 = x.reshape(t, D), target.reshape(t, D), mem.reshape(batch * MEM_LEN, D)
    w_qkv, w_f, w_r, w_mq, w_g3 = _split_w_in(w["w_in"])
    mu = _pad_lora(p["rwkv_mu"])
    bias = _pad_cols(p["fox_f_bias"], 128)
    w_up = jnp.pad(p["rwkv_w_up"], ((0, LORA_PAD - 64), (0, 0)))
    a_up = jnp.pad(p["rwkv_a_up"], ((0, LORA_PAD - 64), (0, 0)))
    g_up = p["rwkv_g_up"]
    r_k = p["rwkv_r_k"].reshape(1, HW)
    pre_params = [p["rwkv_w0"], w_up, p["rwkv_a0"], a_up, g_up, p["rwkv_k_k"], p["rwkv_k_a"]]
    post_params = [p["rwkv_gn_g"], p["rwkv_gn_b"], r_k]
    rw_widths = [HW, HW, HW, LORA_PAD, LORA_PAD, LORA_PAD]
    six = [HW] * 6

    (u,) = _rows_fwd("rms_pre1", _fn_rms, [], [(x2, [D])], [p["pre1_g"]], [[D]], dtypes=[bf16])
    p_qkv = _matmul("proj_qkv", u, w_qkv, "nn")
    p_f = _matmul("proj_f", u, w_f, "nn")
    p_r = _matmul("proj_rwkv", u, w_r, "nn")
    p_mq = _matmul("proj_memq", u, w_mq, "nn")
    p_g = _matmul("proj_gate", u, w_g3, "nn")

    c = _fox_gate_fwd(p_f, bias, batch, seq)
    c8 = c[:, :HEADS].reshape(batch, seq, HEADS).transpose(0, 2, 1).reshape(batch * HEADS, seq)
    ccol, crow = c8[:, :, None], c8[:, None, :]
    qkv = _to_heads(p_qkv, batch, seq, 3)
    fox_hm = _fox_fwd(qkv[0], qkv[1], qkv[2], ccol, crow)
    fox_out = _from_heads(fox_hm[None], batch, seq, 1).astype(bf16)

    ps = _tokshift_fwd(p_r, mu, batch, seq)
    main6, g_rw = _rows_fwd("rwkv_pre", _fn_rwkv_pre, [], [(ps, rw_widths)], pre_params, [six, [HW]])
    z = _to_heads(main6, batch, seq, 6)
    hb = HEADS
    y_hm, states = _scan_fwd(z, hb)
    y_rw = _from_heads(y_hm[None], batch, seq, 1)
    post_consts = []
    post_rows = [(y_rw, [HW]), (main6, six), (g_rw, [HW])]

    def fn_post(y, r, _wl, k2, v, _a, _b, g, gn_g, gn_b, rk):
        return _fn_rwkv_post(y, r, k2, v, g, gn_g, gn_b, rk)

    (rwkv_out,) = _rows_fwd("rwkv_post", fn_post, post_consts, post_rows, post_params, [[HW]], dtypes=[bf16])

    (memn,) = _rows_fwd("rms_mem", _fn_rms, [], [(mem2, [D])], [p["mem_norm_g"]], [[D]], dtypes=[bf16])
    mem_kv = _matmul("proj_memkv", memn, w["w_mem_kv"], "nn")
    mem_out = _mem_fwd(p_mq, mem_kv, batch, seq)

    a_fox = _matmul("out_fox", fox_out, w["w_fox_out"], "nn")
    a_rwkv = _matmul("out_rwkv", rwkv_out, w["w_rwkv_out"], "nn")
    a_mem = _matmul("out_mem", mem_out, w["w_mem_out"], "nn")
    merge_rows = [(a_fox, [D]), (a_rwkv, [D]), (a_mem, [D]), (p_g, [D, D, D])]
    (merged,) = _rows_fwd("merge", _fn_merge, [], merge_rows, [], [[D]], dtypes=[bf16])
    yy = _matmul("out_o", merged, w["w_o"], "nn")
    post1_rows = [(yy, [D]), (x2, [D])]
    post1_params = [p["post1_g"], p["pre2_g"]]
    h1, u2 = _rows_fwd("post1", _fn_post1, [], post1_rows, post1_params, [[D], [D]], dtypes=[f32, bf16])
    gp = _matmul("ffn_gate", u2, w["w_ffn_gate"], "nn")
    up = _matmul("ffn_up", u2, w["w_ffn_up"], "nn")
    (hmid,) = _rows_fwd("swiglu", _fn_swiglu, [], [(gp, [D_FF]), (up, [D_FF])], [], [[D_FF]], dtypes=[bf16])
    ffn = _matmul("ffn_down", hmid, w["w_ffn_down"], "nn")
    final_rows = [(ffn, [D]), (h1, [D])]
    (loss,) = _rows_fwd("final", _fn_final, [(tg2, [D])], final_rows, [p["post2_g"]], [], n_sums=1)

    gw, gp_ = {}, {}
    (d_ffn, d_h1), (gp_["post2_g"],) = _rows_bwd("final_bwd", _fn_final, [(tg2, [D])], final_rows, [p["post2_g"]], [], [],
                                                  n_sums=1, dtypes=[bf16, f32])
    d_hmid = _matmul("ffn_down_dx", d_ffn, w["w_ffn_down"], "nt")
    gw["w_ffn_down"] = _matmul("ffn_down_dw", hmid, d_ffn, "tn")
    (d_gp, d_up), _ = _rows_bwd("swiglu_bwd", _fn_swiglu, [], [(gp, [D_FF]), (up, [D_FF])], [], [[D_FF]], [d_hmid],
                                dtypes=[bf16, bf16])
    d_u2 = _matmul("ffn_gate_dx", d_gp, w["w_ffn_gate"], "nt")
    d_u2 = _matmul("ffn_up_dx", d_up, w["w_ffn_up"], "nt", add=d_u2)
    gw["w_ffn_gate"] = _matmul("ffn_gate_dw", u2, d_gp, "tn")
    gw["w_ffn_up"] = _matmul("ffn_up_dw", u2, d_up, "tn")
    (d_yy, d_x_res), (gp_["post1_g"], gp_["pre2_g"]) = _rows_bwd(
        "post1_bwd", _fn_post1, [], post1_rows, post1_params, [[D], [D]], [d_h1, d_u2], dtypes=[bf16, f32])
    d_merged = _matmul("out_o_dx", d_yy, w["w_o"], "nt")
    gw["w_o"] = _matmul("out_o_dw", merged, d_yy, "tn")
    (d_a_fox, d_a_rwkv, d_a_mem, d_p_g), _ = _rows_bwd("merge_bwd", _fn_merge, [], merge_rows, [], [[D]], [d_merged],
                                                       dtypes=[bf16] * 4)
    d_fox_out = _matmul("out_fox_dx", d_a_fox, w["w_fox_out"], "nt")
    gw["w_fox_out"] = _matmul("out_fox_dw", fox_out, d_a_fox, "tn")
    d_rwkv_out = _matmul("out_rwkv_dx", d_a_rwkv, w["w_rwkv_out"], "nt")
    gw["w_rwkv_out"] = _matmul("out_rwkv_dw", rwkv_out, d_a_rwkv, "tn")
    d_mem_out = _matmul("out_mem_dx", d_a_mem, w["w_mem_out"], "nt")
    gw["w_mem_out"] = _matmul("out_mem_dw", mem_out, d_a_mem, "tn")

    d_p_mq, d_km, d_vm = _mem_bwd(p_mq, mem_kv, d_mem_out, batch, seq)
    d_mem_kv = jnp.concatenate([d_km, d_vm], axis=1).astype(bf16)
    gw["w_mem_kv"] = _matmul("proj_memkv_dw", memn, d_mem_kv, "tn")
    d_memn = _matmul("proj_memkv_dx", d_mem_kv, w["w_mem_kv"], "nt")
    _, (gp_["mem_norm_g"],) = _rows_bwd("rms_mem_bwd", _fn_rms, [], [(mem2, [D])], [p["mem_norm_g"]], [[D]], [d_memn])

    d_fox_hm = _to_heads(d_fox_out, batch, seq, 1)[0]
    d_q, d_k, d_v, d_ccol, d_crow = _fox_bwd(qkv[0], qkv[1], qkv[2], ccol, crow, d_fox_hm)
    d_p_qkv = _from_heads(jnp.stack([d_q, d_k, d_v]), batch, seq, 3).astype(bf16)

    def c_layout(dc):
        return _pad_cols(dc.reshape(batch, HEADS, seq).transpose(0, 2, 1).reshape(t, HEADS), 128)

    d_p_f, d_bias = _fox_gate_bwd(p_f, bias, c_layout(d_ccol), c_layout(d_crow), batch, seq)
    gp_["fox_f_bias"] = d_bias[:, :HEADS]

    (d_y_rw, d_main6_post, d_g_rw), (gp_["rwkv_gn_g"], gp_["rwkv_gn_b"], d_rk) = _rows_bwd(
        "rwkv_post_bwd", fn_post, post_consts, post_rows, post_params, [[HW]], [d_rwkv_out])
    gp_["rwkv_r_k"] = d_rk.reshape(1, HEADS, HD)
    d_z = _scan_bwd(z, states, _to_heads(d_y_rw, batch, seq, 1)[0], hb)
    d_main6 = _from_heads(d_z, batch, seq, 6)

    def fn_pre_sum(*args):
        return _fn_rwkv_pre(*args)

    (d_ps,), d_pre = _rows_bwd("rwkv_pre_bwd", fn_pre_sum, [], [(ps, rw_widths)], pre_params, [six, [HW]],
                               [_rows_add("rwkv_dmain6", d_main6, d_main6_post), d_g_rw])
    gp_["rwkv_w0"], d_w_up, gp_["rwkv_a0"], d_a_up, gw["rwkv_g_up"], gp_["rwkv_k_k"], gp_["rwkv_k_a"] = d_pre
    gw["rwkv_w_up"], gw["rwkv_a_up"] = d_w_up[:64], d_a_up[:64]
    d_p_r, d_mu = _tokshift_bwd(p_r, mu, d_ps, batch, seq)
    gp_["rwkv_mu"] = _unpad_lora(d_mu)

    d_u = _matmul("proj_qkv_dx", d_p_qkv, w_qkv, "nt")
    d_u = _matmul("proj_f_dx", d_p_f, w_f, "nt", add=d_u)
    d_u = _matmul("proj_rwkv_dx", d_p_r, w_r, "nt", add=d_u)
    d_u = _matmul("proj_memq_dx", d_p_mq, w_mq, "nt", add=d_u)
    d_u = _matmul("proj_gate_dx", d_p_g, w_g3, "nt", add=d_u)
    gw["w_in"] = _merge_w_in(_matmul("proj_qkv_dw", u, d_p_qkv, "tn"), _matmul("proj_f_dw", u, d_p_f, "tn"),
                             _matmul("proj_rwkv_dw", u, d_p_r, "tn"), _matmul("proj_memq_dw", u, d_p_mq, "tn"),
                             _matmul("proj_gate_dw", u, d_p_g, "tn"))
    (d_x,), (gp_["pre1_g"],) = _rows_bwd("rms_pre1_bwd", _fn_rms, [], [(x2, [D])], [p["pre1_g"]], [[D]], [d_u], add=d_x_res)
    return loss, d_x.reshape(x.shape), gw, gp_


def _rows_add(name, a, b):
    (s,) = _rows_fwd(name, lambda u, v: (u + v,), [], [(a, [a.shape[1]]), (b, [b.shape[1]])], [], [[a.shape[1]]])
    return s


def _mesh_place():
    x, y, c = lax.axis_index("x"), lax.axis_index("y"), lax.axis_index("c")
    return x, y, c


def _exchange(name, src, per_peer):
    shape = src.shape[1:] if per_peer else src.shape

    def body(src_ref, out_ref, send_sems, recv_sems, local_sem):
        x, y, c = _mesh_place()
        me = 4 * x + 2 * y + c
        mine = src_ref.at[me] if per_peer else src_ref
        local = pltpu.make_async_copy(mine, out_ref.at[me], local_sem)
        local.start()
        copies = []
        for k in range(1, N_DEV):
            px = 1 - x if k & 4 else x
            py = 1 - y if k & 2 else y
            pc = 1 - c if k & 1 else c
            peer = 4 * px + 2 * py + pc
            copies.append(pltpu.make_async_remote_copy(
                src_ref=src_ref.at[peer] if per_peer else src_ref, dst_ref=out_ref.at[me],
                send_sem=send_sems.at[k - 1], recv_sem=recv_sems.at[k - 1],
                device_id=(px, py, pc), device_id_type=pl.DeviceIdType.MESH))
        for cp in copies:
            cp.start()
        for cp in copies:
            cp.wait()
        local.wait()

    return pl.pallas_call(
        body, name=name,
        in_specs=[pl.BlockSpec(memory_space=pltpu.HBM)], out_specs=pl.BlockSpec(memory_space=pltpu.HBM),
        out_shape=jax.ShapeDtypeStruct((N_DEV,) + tuple(shape), src.dtype),
        scratch_shapes=[pltpu.SemaphoreType.DMA((N_DEV - 1,)), pltpu.SemaphoreType.DMA((N_DEV - 1,)), pltpu.SemaphoreType.DMA],
    )(src)


def _adamw(recv, w, m, v):
    rows = w.shape[0]
    tr = 128

    def body(g_ref, w_ref, m_ref, v_ref, go_ref, d_ref, mo_ref, vo_ref):
        g = g_ref[0].astype(f32)
        for s in range(1, N_DEV):
            g = g + g_ref[s].astype(f32)
        m_new = ADAM_B1 * m_ref[...] + (1.0 - ADAM_B1) * g
        v_new = ADAM_B2 * v_ref[...] + (1.0 - ADAM_B2) * (g * g)
        m_hat = m_new / (1.0 - ADAM_B1 ** ADAM_STEP)
        v_hat = v_new / (1.0 - ADAM_B2 ** ADAM_STEP)
        go_ref[...] = g
        d_ref[...] = -ADAM_LR * (m_hat / (jnp.sqrt(v_hat) + ADAM_EPS) + ADAM_WD * w_ref[...])
        mo_ref[...] = m_new
        vo_ref[...] = v_new

    spec = pl.BlockSpec((tr, LANES), lambda i: (i, 0))
    return pl.pallas_call(
        body, name="adamw", grid=(rows // tr,),
        in_specs=[pl.BlockSpec((N_DEV, tr, LANES), lambda i: (0, i, 0)), spec, spec, spec],
        out_specs=[spec] * 4, out_shape=[jax.ShapeDtypeStruct(w.shape, f32)] * 4,
        compiler_params=_cp(("parallel",)),
    )(recv, w, m, v)


def _to_blocks(full, axis):
    if axis == 0:
        b = full.reshape(N_DEV, full.shape[0] // N_DEV, full.shape[1])
    else:
        b = full.reshape(full.shape[0], N_DEV, full.shape[1] // N_DEV).transpose(1, 0, 2)
    return b.reshape(N_DEV, -1, LANES)


def _from_blocks(blocks, shape, axis):
    b0, b1 = _block_shape(shape, axis)
    b = blocks.reshape(N_DEV, b0, b1)
    return b.reshape(shape) if axis == 0 else b.transpose(1, 0, 2).reshape(shape)


def _pack_local(shards, repl, pad_rows):
    parts = [shards[n].reshape(-1, LANES) for n, _, _ in SHARDED]
    rows = SHARD_ROWS
    if repl is not None:
        flat = jnp.concatenate([repl[n].reshape(-1) for n, _ in REPLICATED])
        parts.append(jnp.pad(flat, (0, REPL_ROWS * LANES - REPL_ELEMS)).reshape(REPL_ROWS, LANES))
        rows += REPL_ROWS
    parts.append(jnp.zeros((pad_rows - rows, LANES), parts[0].dtype))
    return jnp.concatenate(parts, axis=0)


def _unpack_local(packed):
    out, r = {}, 0
    for n, shape, axis in SHARDED:
        bs = _block_shape(shape, axis)
        k = _rows_of(bs)
        out[n] = packed[r:r + k].reshape((1,) + bs)
        r += k
    flat = packed[r:r + REPL_ROWS].reshape(-1)
    off = 0
    for n, shape in REPLICATED:
        k = 1
        for s in shape:
            k *= s
        out[n] = flat[off:off + k].reshape(shape)
        off += k
    return out


def kernel(x, mem, pre1_g, post1_g, pre2_g, post2_g, mem_norm_g, w_in, fox_f_bias, rwkv_mu, rwkv_w0, rwkv_w_up, rwkv_a0, rwkv_a_up, rwkv_g_up, rwkv_k_k, rwkv_k_a, rwkv_r_k, rwkv_gn_g, rwkv_gn_b, w_mem_kv, w_fox_out, w_rwkv_out, w_mem_out, w_o, w_ffn_gate, w_ffn_up, w_ffn_down, loss_target, m_pre1_g, m_post1_g, m_pre2_g, m_post2_g, m_mem_norm_g, m_w_in, m_fox_f_bias, m_rwkv_mu, m_rwkv_w0, m_rwkv_w_up, m_rwkv_a0, m_rwkv_a_up, m_rwkv_g_up, m_rwkv_k_k, m_rwkv_k_a, m_rwkv_r_k, m_rwkv_gn_g, m_rwkv_gn_b, m_w_mem_kv, m_w_fox_out, m_w_rwkv_out, m_w_mem_out, m_w_o, m_w_ffn_gate, m_w_ffn_up, m_w_ffn_down, v_pre1_g, v_post1_g, v_pre2_g, v_post2_g, v_mem_norm_g, v_w_in, v_fox_f_bias, v_rwkv_mu, v_rwkv_w0, v_rwkv_w_up, v_rwkv_a0, v_rwkv_a_up, v_rwkv_g_up, v_rwkv_k_k, v_rwkv_k_a, v_rwkv_r_k, v_rwkv_gn_g, v_rwkv_gn_b, v_w_mem_kv, v_w_fox_out, v_w_rwkv_out, v_w_mem_out, v_w_o, v_w_ffn_gate, v_w_ffn_up, v_w_ffn_down):
    args = dict(locals())
    wts = {n: args[n] for n in WEIGHT_ORDER}
    ms = {n: args["m_" + n] for n in WEIGHT_ORDER}
    vs = {n: args["v_" + n] for n in WEIGHT_ORDER}

    w_send = _pack_local({n: wts[n].astype(bf16) for n, _, _ in SHARDED}, None, GATHER_ROWS)
    w_all = _exchange("gather_weights", w_send, per_peer=False)
    full, r = {}, 0
    for n, shape, axis in SHARDED:
        k = _rows_of(_block_shape(shape, axis))
        full[n] = _from_blocks(w_all[:, r:r + k], shape, axis)
        r += k
    small = {n: wts[n] for n, _ in REPLICATED}
    small_in = dict(small)
    small_in.update({n: full[n].astype(f32) for n in ("rwkv_w_up", "rwkv_a_up", "rwkv_g_up")})
    small_in = {n: (v if n == "rwkv_r_k" else v.reshape(v.shape[-2:])) for n, v in small_in.items()}
    small_in["rwkv_r_k"] = small["rwkv_r_k"]

    loss_part, grad_x, gw, gp = _local_step(x, mem, loss_target, full, small_in)

    flat = jnp.concatenate([gp[n].reshape(-1) for n, _ in REPLICATED])
    tail = jnp.pad(flat, (0, (PACK_ROWS - SHARD_ROWS) * LANES - REPL_ELEMS)).reshape(1, PACK_ROWS - SHARD_ROWS, LANES)
    g_send = jnp.concatenate([_to_blocks(gw[n].astype(bf16), axis) for n, _, axis in SHARDED]
                             + [jnp.broadcast_to(tail.astype(bf16), (N_DEV,) + tail.shape[1:])], axis=1)
    g_recv = _exchange("exchange_grads", g_send, per_peer=True)

    def local(d):
        return _pack_local({n: d[n] for n, _, _ in SHARDED}, {n: d[n] for n, _ in REPLICATED}, PACK_ROWS)

    g_sum, delta, m_new, v_new = _adamw(g_recv, local(wts), local(ms), local(vs))
    outs = [_unpack_local(a) for a in (g_sum, delta, m_new, v_new)]
    loss = lax.psum(loss_part[0, 0], ("x", "y", "c"))
    return (loss, grad_x, *[o[n].reshape(wts[n].shape) for o in outs for n in WEIGHT_ORDER])
```

```python
import functools

import jax
import jax.numpy as jnp
from jax import lax
from jax.experimental import pallas as pl
from jax.experimental.pallas import tpu as pltpu

f32 = jnp.float32
bf16 = jnp.bfloat16
_HI = lax.Precision.HIGHEST

D = 1024
HEADS = 8
HD = 64
HW = HEADS * HD
MEM_HEADS = 4
MEM_HD = 128
MEM_W = 512
MEM_LEN = 256
D_FF = 2816
LORA_PAD = 128
RW_COLS = 3 * HW + 3 * LORA_PAD
NORM_EPS = 1e-6
GN_EPS = 64e-5
Q_BLOCK = 128
SCAN_CHUNK = 64
N_DEV = 8
LANES = 1024
VMEM_LIMIT = 56 * 1024 * 1024

ADAM_LR = 0.001
ADAM_B1 = 0.9
ADAM_B2 = 0.999
ADAM_EPS = 1e-08
ADAM_WD = 0.01
ADAM_STEP = 10

SHARDED = (
    ("w_in", (1024, 6920), 1),
    ("w_ffn_gate", (1024, 2816), 1),
    ("w_ffn_up", (1024, 2816), 1),
    ("w_ffn_down", (2816, 1024), 0),
    ("w_mem_kv", (1024, 1024), 0),
    ("w_o", (1024, 1024), 0),
    ("w_fox_out", (512, 1024), 1),
    ("w_rwkv_out", (512, 1024), 1),
    ("w_mem_out", (512, 1024), 1),
    ("rwkv_w_up", (64, 512), 1),
    ("rwkv_a_up", (64, 512), 1),
    ("rwkv_g_up", (128, 512), 1),
)
REPLICATED = (
    ("pre1_g", (1, 1024)), ("post1_g", (1, 1024)), ("pre2_g", (1, 1024)), ("post2_g", (1, 1024)),
    ("mem_norm_g", (1, 1024)), ("fox_f_bias", (1, 8)), ("rwkv_mu", (1, 1792)), ("rwkv_w0", (1, 512)),
    ("rwkv_a0", (1, 512)), ("rwkv_k_k", (1, 512)), ("rwkv_k_a", (1, 512)), ("rwkv_r_k", (1, 8, 64)),
    ("rwkv_gn_g", (1, 512)), ("rwkv_gn_b", (1, 512)),
)
WEIGHT_ORDER = ('pre1_g', 'post1_g', 'pre2_g', 'post2_g', 'mem_norm_g', 'w_in', 'fox_f_bias', 'rwkv_mu',
                'rwkv_w0', 'rwkv_w_up', 'rwkv_a0', 'rwkv_a_up', 'rwkv_g_up', 'rwkv_k_k', 'rwkv_k_a',
                'rwkv_r_k', 'rwkv_gn_g', 'rwkv_gn_b', 'w_mem_kv', 'w_fox_out', 'w_rwkv_out', 'w_mem_out',
                'w_o', 'w_ffn_gate', 'w_ffn_up', 'w_ffn_down')


def _block_shape(shape, axis):
    return tuple(s // N_DEV if i == axis else s for i, s in enumerate(shape))


def _rows_of(shape):
    n = 1
    for s in shape:
        n *= s
    return n // LANES


SHARD_ROWS = sum(_rows_of(_block_shape(s, a)) for _, s, a in SHARDED)
REPL_ELEMS = sum(_rows_of((LANES,) + s) for _, s in REPLICATED)
REPL_ROWS = -(-REPL_ELEMS // LANES)
PACK_ROWS = -(-(SHARD_ROWS + REPL_ROWS) // 128) * 128
GATHER_ROWS = -(-SHARD_ROWS // 16) * 16


def _cp(sem=None):
    return pltpu.CompilerParams(dimension_semantics=sem, vmem_limit_bytes=VMEM_LIMIT)


def _tile(dim, cap):
    best = None
    for t in range(128, min(dim, cap) + 1, 128):
        if dim % t == 0:
            best = t
    return best if best is not None else dim


def _dg(a, b, dims, exact):
    if exact:
        return lax.dot_general(a, b, dims, precision=_HI, preferred_element_type=f32)
    return lax.dot_general(a.astype(bf16), b.astype(bf16), dims, preferred_element_type=f32)


def _make_mm(batched, exact):
    o = 1 if batched else 0
    bd = ((0,), (0,)) if batched else ((), ())
    d_nn = (((1 + o,), (o,)), bd)
    d_nt = (((1 + o,), (1 + o,)), bd)
    d_tn = (((o,), (o,)), bd)

    @jax.custom_vjp
    def nn(a, b):
        return _dg(a, b, d_nn, exact)

    @jax.custom_vjp
    def nt(a, b):
        return _dg(a, b, d_nt, exact)

    @jax.custom_vjp
    def tn(a, b):
        return _dg(a, b, d_tn, exact)

    nn.defvjp(lambda a, b: (_dg(a, b, d_nn, exact), (a, b)),
              lambda res, g: (_dg(g, res[1], d_nt, exact), _dg(res[0], g, d_tn, exact)))
    nt.defvjp(lambda a, b: (_dg(a, b, d_nt, exact), (a, b)),
              lambda res, g: (_dg(g, res[1], d_nn, exact), _dg(g, res[0], d_tn, exact)))
    tn.defvjp(lambda a, b: (_dg(a, b, d_tn, exact), (a, b)),
              lambda res, g: (_dg(res[1], g, d_nt, exact), _dg(res[0], g, d_nn, exact)))
    return nn, nt, tn


def _sigmoid(x):
    return 1.0 / (1.0 + jnp.exp(-x))


def _head_sum_matrix():
    i = lax.broadcasted_iota(jnp.int32, (HW, HW), 0) // HD
    j = lax.broadcasted_iota(jnp.int32, (HW, HW), 1) // HD
    return (i == j).astype(f32)


def _head_sum_raw(x):
    return _dg(x, _head_sum_matrix(), (((1,), (0,)), ((), ())), True)


@jax.custom_vjp
def _head_sum(x):
    return _head_sum_raw(x)


_head_sum.defvjp(lambda x: (_head_sum_raw(x), None), lambda _, g: (_head_sum_raw(g),))


WEIGHT_TILE_BYTES = 13 * 512 * 1024
ACC_TILE_BYTES = 8 * 1024 * 1024


def _matmul(name, a, b, mode, add=None, out_dtype=f32):
    has_add = add is not None
    if mode == "tn":
        (k, m), (_, n) = a.shape, b.shape
        tn = _tile(n, max(128, ACC_TILE_BYTES // (4 * m)))
        tk = _tile(k, 1024)

        def body(a_ref, b_ref, o_ref):
            @pl.when(pl.program_id(1) == 0)
            def _():
                o_ref[...] = jnp.zeros_like(o_ref)

            o_ref[...] += lax.dot_general(a_ref[...].astype(bf16), b_ref[...].astype(bf16),
                                          (((0,), (0,)), ((), ())), preferred_element_type=f32)

        return pl.pallas_call(
            body, name=name, grid=(n // tn, k // tk),
            in_specs=[pl.BlockSpec((tk, m), lambda j, kk: (kk, 0)), pl.BlockSpec((tk, tn), lambda j, kk: (kk, j))],
            out_specs=pl.BlockSpec((m, tn), lambda j, kk: (0, j)), out_shape=jax.ShapeDtypeStruct((m, n), f32),
            compiler_params=_cp(("parallel", "arbitrary")),
        )(a, b)

    (m, k) = a.shape
    n = b.shape[1] if mode == "nn" else b.shape[0]
    tm = _tile(m, 512)
    tn = _tile(n, max(128, WEIGHT_TILE_BYTES // (2 * k)))
    dims = (((1,), (0,)), ((), ())) if mode == "nn" else (((1,), (1,)), ((), ()))
    b_spec = pl.BlockSpec((k, tn), lambda j, i: (0, j)) if mode == "nn" else pl.BlockSpec((tn, k), lambda j, i: (j, 0))
    o_spec = pl.BlockSpec((tm, tn), lambda j, i: (i, j))

    def body(*refs):
        a_ref, b_ref = refs[0], refs[1]
        o_ref = refs[-1]
        r = lax.dot_general(a_ref[...].astype(bf16), b_ref[...].astype(bf16), dims, preferred_element_type=f32)
        if has_add:
            r = r + refs[2][...]
        o_ref[...] = r.astype(o_ref.dtype)

    return pl.pallas_call(
        body, name=name, grid=(n // tn, m // tm),
        in_specs=[pl.BlockSpec((tm, k), lambda j, i: (i, 0)), b_spec] + ([o_spec] if has_add else []),
        out_specs=o_spec, out_shape=jax.ShapeDtypeStruct((m, n), out_dtype),
        compiler_params=_cp(("parallel", "arbitrary")),
    )(*((a, b, add) if has_add else (a, b)))


def _pieces(ref, widths):
    out, off = [], 0
    for w in widths:
        out.append(ref[:, off:off + w].astype(f32))
        off += w
    return out


def _store_pieces(ref, widths, vals, add_ref=None):
    off = 0
    for w, v in zip(widths, vals):
        ref[:, off:off + w] = (v if add_ref is None else v + add_ref[:, off:off + w]).astype(ref.dtype)
        off += w


def _rows_fwd(name, fn, consts, rows, params, outs, n_sums=0, tm=256, dtypes=None):
    t = (consts + rows)[0][0].shape[0]
    tm = min(tm, t)
    ins = consts + rows
    n_in, n_p, n_o = len(ins), len(params), len(outs)
    dtypes = dtypes or [f32] * n_o

    def body(*refs):
        in_refs, p_refs = refs[:n_in], refs[n_in:n_in + n_p]
        o_refs, s_refs = refs[n_in + n_p:n_in + n_p + n_o], refs[n_in + n_p + n_o:]
        vals = []
        for r, (_, widths) in zip(in_refs, ins):
            vals += _pieces(r, widths)
        res = fn(*vals, *[p[...] for p in p_refs])
        pos = 0
        for r, widths in zip(o_refs, outs):
            _store_pieces(r, widths, res[pos:pos + len(widths)])
            pos += len(widths)

        @pl.when(pl.program_id(0) == 0)
        def _():
            for s in s_refs:
                s[...] = jnp.zeros_like(s)

        for s, v in zip(s_refs, res[pos:]):
            s[...] += v

    row_spec = lambda w: pl.BlockSpec((tm, w), lambda i: (i, 0))
    full = lambda p: pl.BlockSpec(p.shape, lambda i: (0,) * p.ndim)
    return pl.pallas_call(
        body, name=name, grid=(t // tm,),
        in_specs=[row_spec(a.shape[1]) for a, _ in ins] + [full(p) for p in params],
        out_specs=[row_spec(sum(w)) for w in outs] + [pl.BlockSpec((1, 1), lambda i: (0, 0))] * n_sums,
        out_shape=[jax.ShapeDtypeStruct((t, sum(w)), dt) for w, dt in zip(outs, dtypes)] + [jax.ShapeDtypeStruct((1, 1), f32)] * n_sums,
        compiler_params=_cp(("arbitrary",)),
    )(*[a for a, _ in ins], *params)


def _rows_bwd(name, fn, consts, rows, params, outs, cts, n_sums=0, add=None, tm=256, dtypes=None):
    t = (consts + rows)[0][0].shape[0]
    tm = min(tm, t)
    n_c, n_r, n_p, n_o = len(consts), len(rows), len(params), len(outs)
    has_add = add is not None
    dtypes = dtypes or [f32] * n_r

    def body(*refs):
        pos = 0
        c_refs = refs[pos:pos + n_c]; pos += n_c
        r_refs = refs[pos:pos + n_r]; pos += n_r
        p_refs = refs[pos:pos + n_p]; pos += n_p
        ct_refs = refs[pos:pos + n_o]; pos += n_o
        add_ref = refs[pos] if has_add else None
        pos += 1 if has_add else 0
        dr_refs = refs[pos:pos + n_r]; pos += n_r
        dp_refs = refs[pos:pos + n_p]
        cvals, rvals = [], []
        for r, (_, widths) in zip(c_refs, consts):
            cvals += _pieces(r, widths)
        for r, (_, widths) in zip(r_refs, rows):
            rvals += _pieces(r, widths)
        pvals = [p[...] for p in p_refs]
        ctv = []
        for r, widths in zip(ct_refs, outs):
            ctv += _pieces(r, widths)
        ctv += [jnp.ones((1, 1), f32)] * n_sums
        _, vjp = jax.vjp(lambda *rp: tuple(fn(*cvals, *rp)), *rvals, *pvals)
        g = vjp(tuple(ctv))
        pos = 0
        for idx, (r, (_, widths)) in enumerate(zip(dr_refs, rows)):
            _store_pieces(r, widths, g[pos:pos + len(widths)], add_ref if idx == 0 else None)
            pos += len(widths)

        @pl.when(pl.program_id(0) == 0)
        def _():
            for dp in dp_refs:
                dp[...] = jnp.zeros_like(dp)

        for dp, v in zip(dp_refs, g[pos:]):
            dp[...] += v

    row_spec = lambda w: pl.BlockSpec((tm, w), lambda i: (i, 0))
    full = lambda p: pl.BlockSpec(p.shape, lambda i: (0,) * p.ndim)
    args = [a for a, _ in consts + rows] + list(params) + list(cts) + ([add] if has_add else [])
    res = pl.pallas_call(
        body, name=name, grid=(t // tm,),
        in_specs=[row_spec(a.shape[1]) for a, _ in consts + rows] + [full(p) for p in params]
        + [row_spec(sum(w)) for w in outs] + ([row_spec(add.shape[1])] if has_add else []),
        out_specs=[row_spec(a.shape[1]) for a, _ in rows] + [full(p) for p in params],
        out_shape=[jax.ShapeDtypeStruct(a.shape, dt) for (a, _), dt in zip(rows, dtypes)]
        + [jax.ShapeDtypeStruct(p.shape, f32) for p in params],
        compiler_params=_cp(("arbitrary",)),
    )(*args)
    return res[:n_r], res[n_r:]


def _rms(x, g):
    return x * lax.rsqrt(jnp.mean(x * x, axis=-1, keepdims=True) + NORM_EPS) * g


def _fn_rms(x, g):
    return (_rms(x, g),)


def _fn_rwkv_pre(r, k, v, wd, ad, gd, w0, w_up, a0, a_up, g_up, k_k, k_a):
    nn, _, _ = _make_mm(False, False)
    w_log = -_sigmoid(w0 + nn(jnp.tanh(wd), w_up)) * 0.6065306597126334
    a = _sigmoid(a0 + nn(ad, a_up))
    g = nn(_sigmoid(gd), g_up)
    kk = k * k_k
    kk = kk * lax.rsqrt(jnp.maximum(_head_sum(kk * kk), 1e-24))
    k2 = k * (1.0 + (a - 1.0) * k_a)
    return r, w_log, k2, v, -kk, kk * a, g


def _fn_rwkv_post(y, r, k2, v, g, gn_g, gn_b, r_k):
    mean = _head_sum(y) * (1.0 / HD)
    yc = y - mean
    var = _head_sum(yc * yc) * (1.0 / HD)
    yn = yc * lax.rsqrt(var + GN_EPS) * gn_g + gn_b
    bonus = _head_sum(r * k2 * r_k) * v
    return ((yn + bonus) * g,)


def _fn_merge(a_fox, a_rwkv, a_mem, g_fox, g_rwkv, g_mem):
    return (_sigmoid(g_fox) * a_fox + _sigmoid(g_rwkv) * a_rwkv + _sigmoid(g_mem) * a_mem,)


def _fn_post1(y, x, post1_g, pre2_g):
    h1 = x + _rms(y, post1_g)
    return h1, _rms(h1, pre2_g)


def _fn_swiglu(gp, up):
    return (gp * _sigmoid(gp) * up,)


def _fn_final(target, ffn, h1, post2_g):
    err = h1 + _rms(ffn, post2_g) - target
    per_row = jnp.mean(err * err, axis=-1, keepdims=True)
    return (0.5 * jnp.sum(per_row, axis=0, keepdims=True),)


def _shift_down(x):
    row = lax.broadcasted_iota(jnp.int32, x.shape, 0)
    return jnp.where(row == 0, 0.0, pltpu.roll(x, 1, 0))


def _shift_up(x):
    s = x.shape[0]
    row = lax.broadcasted_iota(jnp.int32, x.shape, 0)
    return jnp.where(row == s - 1, 0.0, pltpu.roll(x, s - 1, 0))


def _tokshift_fwd(p, mu, batch, seq):
    w = p.shape[1]
    tc = _tile(w, 384)

    def body(p_ref, mu_ref, o_ref):
        x = p_ref[...]
        o_ref[...] = x + (_shift_down(x) - x) * mu_ref[...]

    return pl.pallas_call(
        body, name="tokshift_fwd", grid=(w // tc, batch),
        in_specs=[pl.BlockSpec((seq, tc), lambda j, b: (b, j)), pl.BlockSpec((1, tc), lambda j, b: (0, j))],
        out_specs=pl.BlockSpec((seq, tc), lambda j, b: (b, j)),
        out_shape=jax.ShapeDtypeStruct(p.shape, f32),
        compiler_params=_cp(("parallel", "arbitrary")),
    )(p, mu)


def _tokshift_bwd(p, mu, dps, batch, seq):
    w = p.shape[1]
    tc = _tile(w, 384)

    def body(p_ref, mu_ref, d_ref, dp_ref, dmu_ref):
        x, mu_v, d = p_ref[...], mu_ref[...], d_ref[...]
        dp_ref[...] = (d * (1.0 - mu_v) + _shift_up(d * mu_v)).astype(dp_ref.dtype)

        @pl.when(pl.program_id(1) == 0)
        def _():
            dmu_ref[...] = jnp.zeros_like(dmu_ref)

        dmu_ref[...] += jnp.sum(d * (_shift_down(x) - x), axis=0, keepdims=True)

    return pl.pallas_call(
        body, name="tokshift_bwd", grid=(w // tc, batch),
        in_specs=[pl.BlockSpec((seq, tc), lambda j, b: (b, j)), pl.BlockSpec((1, tc), lambda j, b: (0, j)),
                  pl.BlockSpec((seq, tc), lambda j, b: (b, j))],
        out_specs=[pl.BlockSpec((seq, tc), lambda j, b: (b, j)), pl.BlockSpec((1, tc), lambda j, b: (0, j))],
        out_shape=[jax.ShapeDtypeStruct(p.shape, bf16), jax.ShapeDtypeStruct(mu.shape, f32)],
        compiler_params=_cp(("parallel", "arbitrary")),
    )(p, mu, dps)


def _cum_block(seq):
    return _tile(seq, 256)


def _fox_gate_fwd(f, bias, batch, seq):
    cb = _cum_block(seq)

    def body(f_ref, b_ref, c_ref):
        row = lax.broadcasted_iota(jnp.int32, (cb, cb), 0)
        col = lax.broadcasted_iota(jnp.int32, (cb, cb), 1)
        tri = (col <= row).astype(f32)
        carry = jnp.zeros((1, 128), f32)
        for i in range(seq // cb):
            z = f_ref[i * cb:(i + 1) * cb, :] + b_ref[...]
            ls = jnp.minimum(z, 0.0) - jnp.log(1.0 + jnp.exp(-jnp.abs(z)))
            c = _dg(tri, ls, (((1,), (0,)), ((), ())), True) + carry
            c_ref[i * cb:(i + 1) * cb, :] = c
            carry = c[cb - 1:cb, :]

    return pl.pallas_call(
        body, name="fox_gate_fwd", grid=(batch,),
        in_specs=[pl.BlockSpec((seq, 128), lambda b: (b, 0)), pl.BlockSpec((1, 128), lambda b: (0, 0))],
        out_specs=pl.BlockSpec((seq, 128), lambda b: (b, 0)),
        out_shape=jax.ShapeDtypeStruct(f.shape, f32),
        compiler_params=_cp(("arbitrary",)),
    )(f, bias)


def _fox_gate_bwd(f, bias, dc_a, dc_b, batch, seq):
    cb = _cum_block(seq)

    def body(f_ref, b_ref, da_ref, db_ref, df_ref, dbias_ref):
        row = lax.broadcasted_iota(jnp.int32, (cb, cb), 0)
        col = lax.broadcasted_iota(jnp.int32, (cb, cb), 1)
        triu = (col >= row).astype(f32)

        @pl.when(pl.program_id(0) == 0)
        def _():
            dbias_ref[...] = jnp.zeros_like(dbias_ref)

        carry = jnp.zeros((1, 128), f32)
        tot = jnp.zeros((1, 128), f32)
        for i in reversed(range(seq // cb)):
            sl = slice(i * cb, (i + 1) * cb)
            dc = da_ref[sl, :] + db_ref[sl, :]
            dls = _dg(triu, dc, (((1,), (0,)), ((), ())), True) + carry
            carry = dls[0:1, :]
            df = dls * _sigmoid(-(f_ref[sl, :] + b_ref[...]))
            df_ref[sl, :] = df.astype(df_ref.dtype)
            tot = tot + jnp.sum(df, axis=0, keepdims=True)
        dbias_ref[...] += tot

    return pl.pallas_call(
        body, name="fox_gate_bwd", grid=(batch,),
        in_specs=[pl.BlockSpec((seq, 128), lambda b: (b, 0)), pl.BlockSpec((1, 128), lambda b: (0, 0)),
                  pl.BlockSpec((seq, 128), lambda b: (b, 0)), pl.BlockSpec((seq, 128), lambda b: (b, 0))],
        out_specs=[pl.BlockSpec((seq, 128), lambda b: (b, 0)), pl.BlockSpec((1, 128), lambda b: (0, 0))],
        out_shape=[jax.ShapeDtypeStruct(f.shape, bf16), jax.ShapeDtypeStruct((1, 128), f32)],
        compiler_params=_cp(("arbitrary",)),
    )(f, bias, dc_a, dc_b)


_HBM_SPEC = pl.BlockSpec(memory_space=pltpu.HBM)


def _side_out_shapes(srcs, per_peer):
    return [jax.ShapeDtypeStruct(((N_DEV,) + tuple(s.shape[1:] if per_peer else s.shape)), s.dtype) for s in srcs]


def _side_sems(n):
    if n == 0:
        return []
    return [pltpu.SemaphoreType.DMA((n, N_DEV - 1)), pltpu.SemaphoreType.DMA((n, N_DEV - 1)), pltpu.SemaphoreType.DMA((n,))]


def _peer_copies(src_refs, dst_refs, per_peer, sems):
    send_sems, recv_sems, local_sems = sems
    x, y, c = lax.axis_index("x"), lax.axis_index("y"), lax.axis_index("c")
    me = 4 * x + 2 * y + c
    copies = []
    for t, (s, d) in enumerate(zip(src_refs, dst_refs)):
        copies.append(pltpu.make_async_copy(s.at[me] if per_peer else s, d.at[me], local_sems.at[t]))
        for k in range(1, N_DEV):
            px = 1 - x if k & 4 else x
            py = 1 - y if k & 2 else y
            pc = 1 - c if k & 1 else c
            copies.append(pltpu.make_async_remote_copy(
                src_ref=s.at[4 * px + 2 * py + pc] if per_peer else s, dst_ref=d.at[me],
                send_sem=send_sems.at[t, k - 1], recv_sem=recv_sems.at[t, k - 1],
                device_id=(px, py, pc), device_id_type=pl.DeviceIdType.MESH))
    return copies


def _side_exchange(src_refs, dst_refs, per_peer, sems, n0, n1):
    if not src_refs:
        return
    i, j = pl.program_id(0), pl.program_id(1)

    @pl.when(jnp.logical_and(i == 0, j == 0))
    def _():
        for cp in _peer_copies(src_refs, dst_refs, per_peer, sems):
            cp.start()

    @pl.when(jnp.logical_and(i == n0 - 1, j == n1 - 1))
    def _():
        for cp in _peer_copies(src_refs, dst_refs, per_peer, sems):
            cp.wait()


def _exchange(name, srcs, per_peer):
    n = len(srcs)

    def body(*refs):
        copies = _peer_copies(refs[:n], refs[n:2 * n], per_peer, refs[2 * n:])
        for cp in copies:
            cp.start()
        for cp in copies:
            cp.wait()

    return pl.pallas_call(
        body, name=name, in_specs=[_HBM_SPEC] * n, out_specs=[_HBM_SPEC] * n,
        out_shape=_side_out_shapes(srcs, per_peer), scratch_shapes=_side_sems(n),
    )(*srcs)


def _fox_block(q, k, v, cq, ck, q0):
    nn, nt, _ = _make_mm(False, False)
    tq, s = q.shape[0], k.shape[0]
    logits = nt(q, k) * (HD ** -0.5) + (cq - ck)
    qi = q0 + lax.broadcasted_iota(jnp.int32, (tq, s), 0)
    ki = lax.broadcasted_iota(jnp.int32, (tq, s), 1)
    logits = jnp.where(ki <= qi, logits, -1e30)
    m = lax.stop_gradient(jnp.max(logits, axis=-1, keepdims=True))
    e = jnp.exp(logits - m)
    p = e / jnp.sum(e, axis=-1, keepdims=True)
    return nn(p, v)


def _fox_specs(seq):
    qb = pl.BlockSpec((1, Q_BLOCK, HD), lambda h, i: (h, i, 0))
    kb = pl.BlockSpec((1, seq, HD), lambda h, i: (h, 0, 0))
    cq = pl.BlockSpec((1, Q_BLOCK, 1), lambda h, i: (h, i, 0))
    ck = pl.BlockSpec((1, 1, seq), lambda h, i: (h, 0, 0))
    return qb, kb, cq, ck


def _fox_fwd(q, k, v, ccol, crow, side=None):
    bh, seq, _ = q.shape
    nq = seq // Q_BLOCK
    qb, kb, cqs, cks = _fox_specs(seq)
    srcs, per_peer = side if side is not None else ([], False)
    n_s = len(srcs)

    def body(*refs):
        q_ref, k_ref, v_ref, cq_ref, ck_ref = refs[:5]
        o_ref = refs[5 + n_s]
        _side_exchange(refs[5:5 + n_s], refs[6 + n_s:6 + 2 * n_s], per_peer, refs[6 + 2 * n_s:], bh, nq)
        q0 = pl.program_id(1) * Q_BLOCK
        o_ref[0] = _fox_block(q_ref[0], k_ref[0], v_ref[0], cq_ref[0], ck_ref[0], q0)

    res = pl.pallas_call(
        body, name="fox_attn_fwd", grid=(bh, nq),
        in_specs=[qb, kb, kb, cqs, cks] + [_HBM_SPEC] * n_s, out_specs=[qb] + [_HBM_SPEC] * n_s,
        out_shape=[jax.ShapeDtypeStruct(q.shape, f32)] + _side_out_shapes(srcs, per_peer),
        scratch_shapes=_side_sems(n_s),
        compiler_params=_cp(("arbitrary", "arbitrary")),
    )(q, k, v, ccol, crow, *srcs)
    return res[0], list(res[1:])


def _fox_bwd(q, k, v, ccol, crow, do, side=None):
    bh, seq, _ = q.shape
    nq = seq // Q_BLOCK
    qb, kb, cqs, cks = _fox_specs(seq)
    srcs, per_peer = side if side is not None else ([], False)
    n_s = len(srcs)

    def body(*refs):
        q_ref, k_ref, v_ref, cq_ref, ck_ref, do_ref = refs[:6]
        dq_ref, dk_ref, dv_ref, dcq_ref, dck_ref = refs[6 + n_s:11 + n_s]
        _side_exchange(refs[6:6 + n_s], refs[11 + n_s:11 + 2 * n_s], per_peer, refs[11 + 2 * n_s:], bh, nq)
        q0 = pl.program_id(1) * Q_BLOCK
        _, vjp = jax.vjp(functools.partial(_fox_block, q0=q0), q_ref[0], k_ref[0], v_ref[0], cq_ref[0], ck_ref[0])
        dq, dk, dv, dcq, dck = vjp(do_ref[0])
        dq_ref[0] = dq
        dcq_ref[0] = dcq

        @pl.when(pl.program_id(1) == 0)
        def _():
            dk_ref[...] = jnp.zeros_like(dk_ref)
            dv_ref[...] = jnp.zeros_like(dv_ref)
            dck_ref[...] = jnp.zeros_like(dck_ref)

        dk_ref[0] += dk
        dv_ref[0] += dv
        dck_ref[0] += dck

    res = pl.pallas_call(
        body, name="fox_attn_bwd", grid=(bh, nq),
        in_specs=[qb, kb, kb, cqs, cks, qb] + [_HBM_SPEC] * n_s, out_specs=[qb, kb, kb, cqs, cks] + [_HBM_SPEC] * n_s,
        out_shape=[jax.ShapeDtypeStruct(q.shape, f32), jax.ShapeDtypeStruct(k.shape, f32), jax.ShapeDtypeStruct(v.shape, f32),
                   jax.ShapeDtypeStruct(ccol.shape, f32), jax.ShapeDtypeStruct(crow.shape, f32)] + _side_out_shapes(srcs, per_peer),
        scratch_shapes=_side_sems(n_s),
        compiler_params=_cp(("arbitrary", "arbitrary")),
    )(q, k, v, ccol, crow, do, *srcs)
    return res[:5], list(res[5:])


def _mem_block(q, km, vm):
    nn, nt, _ = _make_mm(False, False)
    logits = nt(q, km) * (MEM_HD ** -0.5)
    m = lax.stop_gradient(jnp.max(logits, axis=-1, keepdims=True))
    e = jnp.exp(logits - m)
    return nn(e / jnp.sum(e, axis=-1, keepdims=True), vm)


def _mem_specs(seq, tq):
    nq = seq // tq
    qs = pl.BlockSpec((tq, MEM_HD), lambda b, h, i: (b * nq + i, h))
    ks = pl.BlockSpec((MEM_LEN, MEM_HD), lambda b, h, i: (b, h))
    vs = pl.BlockSpec((MEM_LEN, MEM_HD), lambda b, h, i: (b, MEM_HEADS + h))
    return nq, qs, ks, vs


def _mem_fwd(q, mem_kv, batch, seq):
    tq = min(512, seq)
    nq, qs, ks, vs = _mem_specs(seq, tq)

    def body(q_ref, k_ref, v_ref, o_ref):
        o_ref[...] = _mem_block(q_ref[...], k_ref[...], v_ref[...]).astype(o_ref.dtype)

    return pl.pallas_call(
        body, name="mem_attn_fwd", grid=(batch, MEM_HEADS, nq),
        in_specs=[qs, ks, vs], out_specs=qs, out_shape=jax.ShapeDtypeStruct(q.shape, bf16),
        compiler_params=_cp(("parallel", "parallel", "arbitrary")),
    )(q, mem_kv, mem_kv)


def _mem_bwd(q, mem_kv, do, batch, seq):
    tq = min(512, seq)
    nq, qs, ks, vs = _mem_specs(seq, tq)

    def body(q_ref, k_ref, v_ref, do_ref, dq_ref, dk_ref, dv_ref):
        _, vjp = jax.vjp(_mem_block, q_ref[...], k_ref[...], v_ref[...])
        dq, dk, dv = vjp(do_ref[...])
        dq_ref[...] = dq.astype(dq_ref.dtype)

        @pl.when(pl.program_id(2) == 0)
        def _():
            dk_ref[...] = jnp.zeros_like(dk_ref)
            dv_ref[...] = jnp.zeros_like(dv_ref)

        dk_ref[...] += dk
        dv_ref[...] += dv

    return pl.pallas_call(
        body, name="mem_attn_bwd", grid=(batch, MEM_HEADS, nq),
        in_specs=[qs, ks, vs, qs], out_specs=[qs, ks, ks],
        out_shape=[jax.ShapeDtypeStruct(q.shape, bf16), jax.ShapeDtypeStruct((batch * MEM_LEN, MEM_W), f32),
                   jax.ShapeDtypeStruct((batch * MEM_LEN, MEM_W), f32)],
        compiler_params=_cp(("parallel", "parallel", "arbitrary")),
    )(q, mem_kv, mem_kv, do)


@jax.custom_vjp
def _halves(x):
    c = x.shape[1] // 2
    return x[:, :c], x[:, c:]


_halves.defvjp(lambda x: ((x[:, :x.shape[1] // 2], x[:, x.shape[1] // 2:]), None),
               lambda _, g: (jnp.concatenate(g, axis=1),))


def _scan_chunk(s0, r, wl, k, v, a, b):
    nn, nt, tn = _make_mm(True, False)
    nn_exact, nt_exact, _ = _make_mm(True, True)
    hb, c, _ = r.shape
    row = lax.broadcasted_iota(jnp.int32, (c, c), 0)
    col = lax.broadcasted_iota(jnp.int32, (c, c), 1)
    tri = jnp.broadcast_to((col <= row).astype(f32)[None], (hb, c, c))
    lg = nn_exact(tri, wl)
    lg_end = lg[:, c - 1:c, :]
    grow, shrink, to_end = jnp.exp(lg), jnp.exp(-lg), jnp.exp(lg_end - lg)
    rt, kt, bt, at = r * grow, k * shrink, b * shrink, a * jnp.exp(lg - wl)
    strict, incl = (col < row)[None], (col <= row)[None]
    queries = jnp.concatenate([at, rt], axis=1)
    (ab, rb), (ak, rk) = _halves(nt_exact(queries, bt)), _halves(nt_exact(queries, kt))
    l_ab = jnp.where(strict, ab, 0.0)
    a_ak = jnp.where(strict, ak, 0.0)
    a_rb = jnp.where(incl, rb, 0.0)
    a_rk = jnp.where(incl, rk, 0.0)
    inv = (col == row).astype(f32)[None] + l_ab
    power, n = l_ab, 1
    while 2 * n < c:
        power = nn(power, power)
        inv = inv + nn(inv, power)
        n *= 2
    sa = nn(inv, nt(at, s0) + nn(a_ak, v))
    y = nt(rt, s0) + nn(a_rk, v) + nn(a_rb, sa)
    s1 = s0 * jnp.exp(lg_end) + tn(v, k * to_end) + tn(sa, b * to_end)
    return y, s1


def _scan_fwd(z, hb):
    _, bh, seq, n = z.shape
    c = min(SCAN_CHUNK, seq)
    nc = seq // c

    def body(z_ref, y_ref, s_ref, st):
        @pl.when(pl.program_id(1) == 0)
        def _():
            st[...] = jnp.zeros_like(st)

        s0 = st[...]
        s_ref[:, 0] = s0
        y, s1 = _scan_chunk(s0, z_ref[0], z_ref[1], z_ref[2], z_ref[3], z_ref[4], z_ref[5])
        y_ref[...] = y
        st[...] = s1

    return pl.pallas_call(
        body, name="rwkv_scan_fwd", grid=(bh // hb, nc),
        in_specs=[pl.BlockSpec((6, hb, c, n), lambda h, i: (0, h, i, 0))],
        out_specs=[pl.BlockSpec((hb, c, n), lambda h, i: (h, i, 0)), pl.BlockSpec((hb, 1, n, n), lambda h, i: (h, i, 0, 0))],
        out_shape=[jax.ShapeDtypeStruct((bh, seq, n), f32), jax.ShapeDtypeStruct((bh, nc, n, n), f32)],
        scratch_shapes=[pltpu.VMEM((hb, n, n), f32)],
        compiler_params=_cp(("parallel", "arbitrary")),
    )(z)


def _scan_bwd(z, states, dy, hb):
    _, bh, seq, n = z.shape
    c = min(SCAN_CHUNK, seq)
    nc = seq // c

    def body(z_ref, s_ref, dy_ref, dz_ref, dst):
        @pl.when(pl.program_id(1) == 0)
        def _():
            dst[...] = jnp.zeros_like(dst)

        _, vjp = jax.vjp(_scan_chunk, s_ref[:, 0], z_ref[0], z_ref[1], z_ref[2], z_ref[3], z_ref[4], z_ref[5])
        g = vjp((dy_ref[...], dst[...]))
        dst[...] = g[0]
        for i in range(6):
            dz_ref[i] = g[1 + i]

    return pl.pallas_call(
        body, name="rwkv_scan_bwd", grid=(bh // hb, nc),
        in_specs=[pl.BlockSpec((6, hb, c, n), lambda h, i: (0, h, nc - 1 - i, 0)),
                  pl.BlockSpec((hb, 1, n, n), lambda h, i: (h, nc - 1 - i, 0, 0)),
                  pl.BlockSpec((hb, c, n), lambda h, i: (h, nc - 1 - i, 0))],
        out_specs=pl.BlockSpec((6, hb, c, n), lambda h, i: (0, h, nc - 1 - i, 0)),
        out_shape=jax.ShapeDtypeStruct(z.shape, f32),
        scratch_shapes=[pltpu.VMEM((hb, n, n), f32)],
        compiler_params=_cp(("parallel", "arbitrary")),
    )(z, states, dy)


def _to_heads(x, batch, seq, k):
    return x.reshape(batch, seq, k, HEADS, HD).transpose(2, 0, 3, 1, 4).reshape(k, batch * HEADS, seq, HD)


def _from_heads(x, batch, seq, k):
    return x.reshape(k, batch, HEADS, seq, HD).transpose(1, 3, 0, 2, 4).reshape(batch * seq, k * HW)


def _pad_cols(x, width):
    return jnp.pad(x, ((0, 0), (0, width - x.shape[1])))


def _split_w_in(w):
    z64 = jnp.zeros((w.shape[0], 64), w.dtype)
    w_r = jnp.concatenate([w[:, 1544:3080], w[:, 3080:3144], z64, w[:, 3144:3208], z64, w[:, 3208:3336]], axis=1)
    return w[:, :1536], _pad_cols(w[:, 1536:1544], 128), w_r, w[:, 3336:3848], w[:, 3848:]


def _merge_w_in(g_qkv, g_f, g_r, g_mq, g_g):
    return jnp.concatenate([g_qkv, g_f[:, :8], g_r[:, :1536], g_r[:, 1536:1600], g_r[:, 1664:1728], g_r[:, 1792:],
                            g_mq, g_g], axis=1)


def _pad_lora(v):
    z64 = jnp.zeros((1, 64), v.dtype)
    return jnp.concatenate([v[:, :1536], v[:, 1536:1600], z64, v[:, 1600:1664], z64, v[:, 1664:]], axis=1)


def _unpad_lora(v):
    return jnp.concatenate([v[:, :1536], v[:, 1536:1600], v[:, 1664:1728], v[:, 1792:]], axis=1)


def _local_step(x, mem, target, w, p, late=None, early=None):
    batch, seq, _ = x.shape
    t = batch * seq
    x2, tg2, mem2 = x.reshape(t, D), target.reshape(t, D), mem.reshape(batch * MEM_LEN, D)
    w_qkv, w_f, w_r, w_mq, w_g3 = _split_w_in(w["w_in"])
    mu = _pad_lora(p["rwkv_mu"])
    bias = _pad_cols(p["fox_f_bias"], 128)
    r_k = p["rwkv_r_k"].reshape(1, HW)
    post_params = [p["rwkv_gn_g"], p["rwkv_gn_b"], r_k]
    rw_widths = [HW, HW, HW, LORA_PAD, LORA_PAD, LORA_PAD]
    six = [HW] * 6

    (u,) = _rows_fwd("rms_pre1", _fn_rms, [], [(x2, [D])], [p["pre1_g"]], [[D]], dtypes=[bf16])
    p_qkv = _matmul("proj_qkv", u, w_qkv, "nn")
    p_f = _matmul("proj_f", u, w_f, "nn")
    p_r = _matmul("proj_rwkv", u, w_r, "nn")
    p_mq = _matmul("proj_memq", u, w_mq, "nn")
    p_g = _matmul("proj_gate", u, w_g3, "nn")

    c = _fox_gate_fwd(p_f, bias, batch, seq)
    c8 = c[:, :HEADS].reshape(batch, seq, HEADS).transpose(0, 2, 1).reshape(batch * HEADS, seq)
    ccol, crow = c8[:, :, None], c8[:, None, :]
    qkv = _to_heads(p_qkv, batch, seq, 3)
    fox_hm, gathered = _fox_fwd(qkv[0], qkv[1], qkv[2], ccol, crow, side=(late[0], False) if late else None)
    if late:
        w = {**w, **late[1](gathered)}
    fox_out = _from_heads(fox_hm[None], batch, seq, 1).astype(bf16)

    w_up = jnp.pad(w["rwkv_w_up"].astype(f32), ((0, LORA_PAD - 64), (0, 0)))
    a_up = jnp.pad(w["rwkv_a_up"].astype(f32), ((0, LORA_PAD - 64), (0, 0)))
    pre_params = [p["rwkv_w0"], w_up, p["rwkv_a0"], a_up, w["rwkv_g_up"].astype(f32), p["rwkv_k_k"], p["rwkv_k_a"]]
    ps = _tokshift_fwd(p_r, mu, batch, seq)
    main6, g_rw = _rows_fwd("rwkv_pre", _fn_rwkv_pre, [], [(ps, rw_widths)], pre_params, [six, [HW]])
    z = _to_heads(main6, batch, seq, 6)
    hb = HEADS
    y_hm, states = _scan_fwd(z, hb)
    y_rw = _from_heads(y_hm[None], batch, seq, 1)
    post_consts = []
    post_rows = [(y_rw, [HW]), (main6, six), (g_rw, [HW])]

    def fn_post(y, r, _wl, k2, v, _a, _b, g, gn_g, gn_b, rk):
        return _fn_rwkv_post(y, r, k2, v, g, gn_g, gn_b, rk)

    (rwkv_out,) = _rows_fwd("rwkv_post", fn_post, post_consts, post_rows, post_params, [[HW]], dtypes=[bf16])

    (memn,) = _rows_fwd("rms_mem", _fn_rms, [], [(mem2, [D])], [p["mem_norm_g"]], [[D]], dtypes=[bf16])
    mem_kv = _matmul("proj_memkv", memn, w["w_mem_kv"], "nn")
    mem_out = _mem_fwd(p_mq, mem_kv, batch, seq)

    a_fox = _matmul("out_fox", fox_out, w["w_fox_out"], "nn")
    a_rwkv = _matmul("out_rwkv", rwkv_out, w["w_rwkv_out"], "nn")
    a_mem = _matmul("out_mem", mem_out, w["w_mem_out"], "nn")
    merge_rows = [(a_fox, [D]), (a_rwkv, [D]), (a_mem, [D]), (p_g, [D, D, D])]
    (merged,) = _rows_fwd("merge", _fn_merge, [], merge_rows, [], [[D]], dtypes=[bf16])
    yy = _matmul("out_o", merged, w["w_o"], "nn")
    post1_rows = [(yy, [D]), (x2, [D])]
    post1_params = [p["post1_g"], p["pre2_g"]]
    h1, u2 = _rows_fwd("post1", _fn_post1, [], post1_rows, post1_params, [[D], [D]], dtypes=[f32, bf16])
    gp = _matmul("ffn_gate", u2, w["w_ffn_gate"], "nn")
    up = _matmul("ffn_up", u2, w["w_ffn_up"], "nn")
    (hmid,) = _rows_fwd("swiglu", _fn_swiglu, [], [(gp, [D_FF]), (up, [D_FF])], [], [[D_FF]], dtypes=[bf16])
    ffn = _matmul("ffn_down", hmid, w["w_ffn_down"], "nn")
    final_rows = [(ffn, [D]), (h1, [D])]
    (loss,) = _rows_fwd("final", _fn_final, [(tg2, [D])], final_rows, [p["post2_g"]], [], n_sums=1)

    gw, gp_ = {}, {}
    (d_ffn, d_h1), (gp_["post2_g"],) = _rows_bwd("final_bwd", _fn_final, [(tg2, [D])], final_rows, [p["post2_g"]], [], [],
                                                  n_sums=1, dtypes=[bf16, f32])
    d_hmid = _matmul("ffn_down_dx", d_ffn, w["w_ffn_down"], "nt")
    gw["w_ffn_down"] = _matmul("ffn_down_dw", hmid, d_ffn, "tn")
    (d_gp, d_up), _ = _rows_bwd("swiglu_bwd", _fn_swiglu, [], [(gp, [D_FF]), (up, [D_FF])], [], [[D_FF]], [d_hmid],
                                dtypes=[bf16, bf16])
    d_u2 = _matmul("ffn_gate_dx", d_gp, w["w_ffn_gate"], "nt")
    d_u2 = _matmul("ffn_up_dx", d_up, w["w_ffn_up"], "nt", add=d_u2)
    gw["w_ffn_gate"] = _matmul("ffn_gate_dw", u2, d_gp, "tn")
    gw["w_ffn_up"] = _matmul("ffn_up_dw", u2, d_up, "tn")
    (d_yy, d_x_res), (gp_["post1_g"], gp_["pre2_g"]) = _rows_bwd(
        "post1_bwd", _fn_post1, [], post1_rows, post1_params, [[D], [D]], [d_h1, d_u2], dtypes=[bf16, f32])
    d_merged = _matmul("out_o_dx", d_yy, w["w_o"], "nt")
    gw["w_o"] = _matmul("out_o_dw", merged, d_yy, "tn")
    (d_a_fox, d_a_rwkv, d_a_mem, d_p_g), _ = _rows_bwd("merge_bwd", _fn_merge, [], merge_rows, [], [[D]], [d_merged],
                                                       dtypes=[bf16] * 4)
    d_fox_out = _matmul("out_fox_dx", d_a_fox, w["w_fox_out"], "nt")
    gw["w_fox_out"] = _matmul("out_fox_dw", fox_out, d_a_fox, "tn")
    d_rwkv_out = _matmul("out_rwkv_dx", d_a_rwkv, w["w_rwkv_out"], "nt")
    gw["w_rwkv_out"] = _matmul("out_rwkv_dw", rwkv_out, d_a_rwkv, "tn")
    d_mem_out = _matmul("out_mem_dx", d_a_mem, w["w_mem_out"], "nt")
    gw["w_mem_out"] = _matmul("out_mem_dw", mem_out, d_a_mem, "tn")

    d_p_mq, d_km, d_vm = _mem_bwd(p_mq, mem_kv, d_mem_out, batch, seq)
    d_mem_kv = jnp.concatenate([d_km, d_vm], axis=1).astype(bf16)
    gw["w_mem_kv"] = _matmul("proj_memkv_dw", memn, d_mem_kv, "tn")
    d_memn = _matmul("proj_memkv_dx", d_mem_kv, w["w_mem_kv"], "nt")
    _, (gp_["mem_norm_g"],) = _rows_bwd("rms_mem_bwd", _fn_rms, [], [(mem2, [D])], [p["mem_norm_g"]], [[D]], [d_memn])

    d_fox_hm = _to_heads(d_fox_out, batch, seq, 1)[0]
    (d_q, d_k, d_v, d_ccol, d_crow), early_got = _fox_bwd(qkv[0], qkv[1], qkv[2], ccol, crow, d_fox_hm,
                                                          side=(early(gw), True) if early else None)
    d_p_qkv = _from_heads(jnp.stack([d_q, d_k, d_v]), batch, seq, 3).astype(bf16)

    def c_layout(dc):
        return _pad_cols(dc.reshape(batch, HEADS, seq).transpose(0, 2, 1).reshape(t, HEADS), 128)

    d_p_f, d_bias = _fox_gate_bwd(p_f, bias, c_layout(d_ccol), c_layout(d_crow), batch, seq)
    gp_["fox_f_bias"] = d_bias[:, :HEADS]

    (d_y_rw, d_main6_post, d_g_rw), (gp_["rwkv_gn_g"], gp_["rwkv_gn_b"], d_rk) = _rows_bwd(
        "rwkv_post_bwd", fn_post, post_consts, post_rows, post_params, [[HW]], [d_rwkv_out])
    gp_["rwkv_r_k"] = d_rk.reshape(1, HEADS, HD)
    d_z = _scan_bwd(z, states, _to_heads(d_y_rw, batch, seq, 1)[0], hb)
    d_main6 = _from_heads(d_z, batch, seq, 6)

    def fn_pre_sum(*args):
        return _fn_rwkv_pre(*args)

    (d_ps,), d_pre = _rows_bwd("rwkv_pre_bwd", fn_pre_sum, [], [(ps, rw_widths)], pre_params, [six, [HW]],
                               [_rows_add("rwkv_dmain6", d_main6, d_main6_post), d_g_rw])
    gp_["rwkv_w0"], d_w_up, gp_["rwkv_a0"], d_a_up, gw["rwkv_g_up"], gp_["rwkv_k_k"], gp_["rwkv_k_a"] = d_pre
    gw["rwkv_w_up"], gw["rwkv_a_up"] = d_w_up[:64], d_a_up[:64]
    d_p_r, d_mu = _tokshift_bwd(p_r, mu, d_ps, batch, seq)
    gp_["rwkv_mu"] = _unpad_lora(d_mu)

    d_u = _matmul("proj_qkv_dx", d_p_qkv, w_qkv, "nt")
    d_u = _matmul("proj_f_dx", d_p_f, w_f, "nt", add=d_u)
    d_u = _matmul("proj_rwkv_dx", d_p_r, w_r, "nt", add=d_u)
    d_u = _matmul("proj_memq_dx", d_p_mq, w_mq, "nt", add=d_u)
    d_u = _matmul("proj_gate_dx", d_p_g, w_g3, "nt", add=d_u)
    gw["w_in"] = _merge_w_in(_matmul("proj_qkv_dw", u, d_p_qkv, "tn"), _matmul("proj_f_dw", u, d_p_f, "tn"),
                             _matmul("proj_rwkv_dw", u, d_p_r, "tn"), _matmul("proj_memq_dw", u, d_p_mq, "tn"),
                             _matmul("proj_gate_dw", u, d_p_g, "tn"))
    (d_x,), (gp_["pre1_g"],) = _rows_bwd("rms_pre1_bwd", _fn_rms, [], [(x2, [D])], [p["pre1_g"]], [[D]], [d_u], add=d_x_res)
    return loss, d_x.reshape(x.shape), gw, gp_, early_got


def _rows_add(name, a, b):
    (s,) = _rows_fwd(name, lambda u, v: (u + v,), [], [(a, [a.shape[1]]), (b, [b.shape[1]])], [], [[a.shape[1]]])
    return s


def _adamw(name, recv, w, m, v):
    rows, cols = w.shape
    tr = max(t for t in range(16, min(rows, 128) + 1, 16) if rows % t == 0)

    def body(g_ref, w_ref, m_ref, v_ref, go_ref, d_ref, mo_ref, vo_ref):
        g = g_ref[0].astype(f32)
        for s in range(1, N_DEV):
            g = g + g_ref[s].astype(f32)
        m_new = ADAM_B1 * m_ref[...] + (1.0 - ADAM_B1) * g
        v_new = ADAM_B2 * v_ref[...] + (1.0 - ADAM_B2) * (g * g)
        m_hat = m_new / (1.0 - ADAM_B1 ** ADAM_STEP)
        v_hat = v_new / (1.0 - ADAM_B2 ** ADAM_STEP)
        go_ref[...] = g
        d_ref[...] = -ADAM_LR * (m_hat / (jnp.sqrt(v_hat) + ADAM_EPS) + ADAM_WD * w_ref[...])
        mo_ref[...] = m_new
        vo_ref[...] = v_new

    spec = pl.BlockSpec((tr, cols), lambda i: (i, 0))
    return pl.pallas_call(
        body, name=name, grid=(rows // tr,),
        in_specs=[pl.BlockSpec((N_DEV, tr, cols), lambda i: (0, i, 0)), spec, spec, spec],
        out_specs=[spec] * 4, out_shape=[jax.ShapeDtypeStruct(w.shape, f32)] * 4,
        compiler_params=_cp(("parallel",)),
    )(recv, w, m, v)


GROUPS = (
    ("in", ("w_in",), 1),
    ("memkv", ("w_mem_kv",), 0),
    ("ffn_gu", ("w_ffn_gate", "w_ffn_up"), 1),
    ("down_o", ("w_ffn_down", "w_o"), 0),
    ("outs", ("w_fox_out", "w_rwkv_out", "w_mem_out"), 1),
    ("lora", ("rwkv_w_up", "rwkv_a_up", "rwkv_g_up"), 0),
)
FIRST_GROUPS = ("in", "memkv")
LATE_GROUPS = ("ffn_gu", "down_o", "outs", "lora")
EARLY_GRAD_GROUPS = ("memkv", "ffn_gu", "down_o", "outs")
LAST_GRAD_GROUPS = ("in", "lora")
SHARD_AXIS = {n: a for n, _, a in SHARDED}
SMALL_ROWS = 16


def _group_local(shards, members, join):
    parts = [shards[n].reshape(shards[n].shape[-2:]) for n in members]
    return parts[0] if len(parts) == 1 else jnp.concatenate(parts, axis=join)


def _group_split(arr, members, join, lead=False):
    out, off = {}, 0
    for n in members:
        shape = dict((k, s) for k, s, _ in SHARDED)[n]
        size = _block_shape(shape, SHARD_AXIS[n])[join]
        idx = [slice(None)] * arr.ndim
        idx[arr.ndim - 2 + join] = slice(off, off + size)
        out[n] = arr[tuple(idx)]
        off += size
    return out


def _full_from_blocks(blocks, axis):
    if axis == 0:
        return blocks.reshape(-1, blocks.shape[2])
    return blocks.transpose(1, 0, 2).reshape(blocks.shape[1], -1)


def _blocks_from_full(full, axis):
    if axis == 0:
        return full.reshape(N_DEV, -1, full.shape[1])
    return full.reshape(full.shape[0], N_DEV, -1).transpose(1, 0, 2)


def _assemble(gathered, names):
    out = {}
    for arr, g in zip(gathered, names):
        _, members, join = [grp for grp in GROUPS if grp[0] == g][0]
        for n, blk in _group_split(arr, members, join, lead=True).items():
            out[n] = _full_from_blocks(blk, SHARD_AXIS[n])
    return out


def _grad_blocks(gw, names):
    out = []
    for g in names:
        _, members, join = [grp for grp in GROUPS if grp[0] == g][0]
        parts = [_blocks_from_full(gw[n].astype(bf16), SHARD_AXIS[n]) for n in members]
        out.append(parts[0] if len(parts) == 1 else jnp.concatenate(parts, axis=1 + join))
    return out


def _small_pack(d):
    flat = jnp.concatenate([d[n].reshape(-1) for n, _ in REPLICATED])
    return jnp.pad(flat, (0, SMALL_ROWS * LANES - REPL_ELEMS)).reshape(SMALL_ROWS, LANES)


def _small_unpack(packed):
    out, flat, off = {}, packed.reshape(-1), 0
    for n, shape in REPLICATED:
        k = _rows_of((LANES,) + shape)
        out[n] = flat[off:off + k].reshape(shape)
        off += k
    return out


def kernel(x, mem, pre1_g, post1_g, pre2_g, post2_g, mem_norm_g, w_in, fox_f_bias, rwkv_mu, rwkv_w0, rwkv_w_up, rwkv_a0, rwkv_a_up, rwkv_g_up, rwkv_k_k, rwkv_k_a, rwkv_r_k, rwkv_gn_g, rwkv_gn_b, w_mem_kv, w_fox_out, w_rwkv_out, w_mem_out, w_o, w_ffn_gate, w_ffn_up, w_ffn_down, loss_target, m_pre1_g, m_post1_g, m_pre2_g, m_post2_g, m_mem_norm_g, m_w_in, m_fox_f_bias, m_rwkv_mu, m_rwkv_w0, m_rwkv_w_up, m_rwkv_a0, m_rwkv_a_up, m_rwkv_g_up, m_rwkv_k_k, m_rwkv_k_a, m_rwkv_r_k, m_rwkv_gn_g, m_rwkv_gn_b, m_w_mem_kv, m_w_fox_out, m_w_rwkv_out, m_w_mem_out, m_w_o, m_w_ffn_gate, m_w_ffn_up, m_w_ffn_down, v_pre1_g, v_post1_g, v_pre2_g, v_post2_g, v_mem_norm_g, v_w_in, v_fox_f_bias, v_rwkv_mu, v_rwkv_w0, v_rwkv_w_up, v_rwkv_a0, v_rwkv_a_up, v_rwkv_g_up, v_rwkv_k_k, v_rwkv_k_a, v_rwkv_r_k, v_rwkv_gn_g, v_rwkv_gn_b, v_w_mem_kv, v_w_fox_out, v_w_rwkv_out, v_w_mem_out, v_w_o, v_w_ffn_gate, v_w_ffn_up, v_w_ffn_down):
    args = dict(locals())
    wts = {n: args[n] for n in WEIGHT_ORDER}
    ms = {n: args["m_" + n] for n in WEIGHT_ORDER}
    vs = {n: args["v_" + n] for n in WEIGHT_ORDER}

    groups = {g: (members, join) for g, members, join in GROUPS}
    w_bf16 = {n: wts[n].astype(bf16) for n, _, _ in SHARDED}

    def send(g):
        return _group_local(w_bf16, *groups[g])

    first = _exchange("gather_first", [send(g) for g in FIRST_GROUPS], per_peer=False)
    full = _assemble(first, FIRST_GROUPS)
    small_in = {n: (wts[n] if n == "rwkv_r_k" else wts[n].reshape(wts[n].shape[-2:])) for n, _ in REPLICATED}
    late = ([send(g) for g in LATE_GROUPS], lambda got: _assemble(got, LATE_GROUPS))
    loss_part, grad_x, gw, gp, early_got = _local_step(
        x, mem, loss_target, full, small_in, late=late, early=lambda g: _grad_blocks(g, EARLY_GRAD_GROUPS))

    small_send = jnp.broadcast_to(_small_pack(gp).astype(bf16)[None], (N_DEV, SMALL_ROWS, LANES))
    *last_got, small_got = _exchange("exchange_last", _grad_blocks(gw, LAST_GRAD_GROUPS) + [small_send], per_peer=True)
    received = dict(zip(EARLY_GRAD_GROUPS + LAST_GRAD_GROUPS, list(early_got) + list(last_got)))

    outs = [{}, {}, {}, {}]
    for g, members, join in GROUPS:
        res = _adamw("adamw_" + g, received[g], *[_group_local(d, members, join) for d in (wts, ms, vs)])
        for o, arr in zip(outs, res):
            o.update(_group_split(arr, members, join))
    res = _adamw("adamw_small", small_got, *[_small_pack(d) for d in (wts, ms, vs)])
    for o, arr in zip(outs, res):
        o.update(_small_unpack(arr))
    loss = lax.psum(loss_part[0, 0], ("x", "y", "c"))
    return (loss, grad_x, *[o[n].reshape(wts[n].shape) for o in outs for n in WEIGHT_ORDER])
```

```python
import functools

import jax
import jax.numpy as jnp
from jax import lax
from jax.experimental import pallas as pl
from jax.experimental.pallas import tpu as pltpu

f32 = jnp.float32
bf16 = jnp.bfloat16
_HI = lax.Precision.HIGHEST

D = 1024
HEADS = 8
HD = 64
HW = HEADS * HD
MEM_HEADS = 4
MEM_HD = 128
MEM_W = 512
MEM_LEN = 256
D_FF = 2816
LORA_PAD = 128
RW_COLS = 3 * HW + 3 * LORA_PAD
NORM_EPS = 1e-6
GN_EPS = 64e-5
Q_BLOCK = 128
SCAN_CHUNK = 64
N_DEV = 8
LANES = 1024
VMEM_LIMIT = 56 * 1024 * 1024

ADAM_LR = 0.001
ADAM_B1 = 0.9
ADAM_B2 = 0.999
ADAM_EPS = 1e-08
ADAM_WD = 0.01
ADAM_STEP = 10

SHARDED = (
    ("w_in", (1024, 6920), 1),
    ("w_ffn_gate", (1024, 2816), 1),
    ("w_ffn_up", (1024, 2816), 1),
    ("w_ffn_down", (2816, 1024), 0),
    ("w_mem_kv", (1024, 1024), 0),
    ("w_o", (1024, 1024), 0),
    ("w_fox_out", (512, 1024), 1),
    ("w_rwkv_out", (512, 1024), 1),
    ("w_mem_out", (512, 1024), 1),
    ("rwkv_w_up", (64, 512), 1),
    ("rwkv_a_up", (64, 512), 1),
    ("rwkv_g_up", (128, 512), 1),
)
REPLICATED = (
    ("pre1_g", (1, 1024)), ("post1_g", (1, 1024)), ("pre2_g", (1, 1024)), ("post2_g", (1, 1024)),
    ("mem_norm_g", (1, 1024)), ("fox_f_bias", (1, 8)), ("rwkv_mu", (1, 1792)), ("rwkv_w0", (1, 512)),
    ("rwkv_a0", (1, 512)), ("rwkv_k_k", (1, 512)), ("rwkv_k_a", (1, 512)), ("rwkv_r_k", (1, 8, 64)),
    ("rwkv_gn_g", (1, 512)), ("rwkv_gn_b", (1, 512)),
)
WEIGHT_ORDER = ('pre1_g', 'post1_g', 'pre2_g', 'post2_g', 'mem_norm_g', 'w_in', 'fox_f_bias', 'rwkv_mu',
                'rwkv_w0', 'rwkv_w_up', 'rwkv_a0', 'rwkv_a_up', 'rwkv_g_up', 'rwkv_k_k', 'rwkv_k_a',
                'rwkv_r_k', 'rwkv_gn_g', 'rwkv_gn_b', 'w_mem_kv', 'w_fox_out', 'w_rwkv_out', 'w_mem_out',
                'w_o', 'w_ffn_gate', 'w_ffn_up', 'w_ffn_down')


def _block_shape(shape, axis):
    return tuple(s // N_DEV if i == axis else s for i, s in enumerate(shape))


def _rows_of(shape):
    n = 1
    for s in shape:
        n *= s
    return n // LANES


SHARD_ROWS = sum(_rows_of(_block_shape(s, a)) for _, s, a in SHARDED)
REPL_ELEMS = sum(_rows_of((LANES,) + s) for _, s in REPLICATED)
REPL_ROWS = -(-REPL_ELEMS // LANES)
PACK_ROWS = -(-(SHARD_ROWS + REPL_ROWS) // 128) * 128
GATHER_ROWS = -(-SHARD_ROWS // 16) * 16


def _cp(sem=None):
    return pltpu.CompilerParams(dimension_semantics=sem, vmem_limit_bytes=VMEM_LIMIT)


def _tile(dim, cap):
    best = None
    for t in range(128, min(dim, cap) + 1, 128):
        if dim % t == 0:
            best = t
    return best if best is not None else dim


def _dg(a, b, dims, exact):
    if exact:
        return lax.dot_general(a, b, dims, precision=_HI, preferred_element_type=f32)
    return lax.dot_general(a.astype(bf16), b.astype(bf16), dims, preferred_element_type=f32)


def _make_mm(batched, exact):
    o = 1 if batched else 0
    bd = ((0,), (0,)) if batched else ((), ())
    d_nn = (((1 + o,), (o,)), bd)
    d_nt = (((1 + o,), (1 + o,)), bd)
    d_tn = (((o,), (o,)), bd)

    @jax.custom_vjp
    def nn(a, b):
        return _dg(a, b, d_nn, exact)

    @jax.custom_vjp
    def nt(a, b):
        return _dg(a, b, d_nt, exact)

    @jax.custom_vjp
    def tn(a, b):
        return _dg(a, b, d_tn, exact)

    nn.defvjp(lambda a, b: (_dg(a, b, d_nn, exact), (a, b)),
              lambda res, g: (_dg(g, res[1], d_nt, exact), _dg(res[0], g, d_tn, exact)))
    nt.defvjp(lambda a, b: (_dg(a, b, d_nt, exact), (a, b)),
              lambda res, g: (_dg(g, res[1], d_nn, exact), _dg(g, res[0], d_tn, exact)))
    tn.defvjp(lambda a, b: (_dg(a, b, d_tn, exact), (a, b)),
              lambda res, g: (_dg(res[1], g, d_nt, exact), _dg(res[0], g, d_nn, exact)))
    return nn, nt, tn


def _sigmoid(x):
    return 1.0 / (1.0 + jnp.exp(-x))


def _head_sum_matrix():
    i = lax.broadcasted_iota(jnp.int32, (HW, HW), 0) // HD
    j = lax.broadcasted_iota(jnp.int32, (HW, HW), 1) // HD
    return (i == j).astype(f32)


def _head_sum_raw(x):
    return _dg(x, _head_sum_matrix(), (((1,), (0,)), ((), ())), True)


@jax.custom_vjp
def _head_sum(x):
    return _head_sum_raw(x)


_head_sum.defvjp(lambda x: (_head_sum_raw(x), None), lambda _, g: (_head_sum_raw(g),))


WEIGHT_TILE_BYTES = 13 * 512 * 1024
ACC_TILE_BYTES = 8 * 1024 * 1024


def _matmul(name, a, b, mode, add=None, out_dtype=f32):
    has_add = add is not None
    if mode == "tn":
        (k, m), (_, n) = a.shape, b.shape
        tn = _tile(n, max(128, ACC_TILE_BYTES // (4 * m)))
        tk = _tile(k, 1024)

        def body(a_ref, b_ref, o_ref):
            @pl.when(pl.program_id(1) == 0)
            def _():
                o_ref[...] = jnp.zeros_like(o_ref)

            o_ref[...] += lax.dot_general(a_ref[...].astype(bf16), b_ref[...].astype(bf16),
                                          (((0,), (0,)), ((), ())), preferred_element_type=f32)

        return pl.pallas_call(
            body, name=name, grid=(n // tn, k // tk),
            in_specs=[pl.BlockSpec((tk, m), lambda j, kk: (kk, 0)), pl.BlockSpec((tk, tn), lambda j, kk: (kk, j))],
            out_specs=pl.BlockSpec((m, tn), lambda j, kk: (0, j)), out_shape=jax.ShapeDtypeStruct((m, n), f32),
            compiler_params=_cp(("parallel", "arbitrary")),
        )(a, b)

    (m, k) = a.shape
    n = b.shape[1] if mode == "nn" else b.shape[0]
    tm = _tile(m, 512)
    tn = _tile(n, max(128, WEIGHT_TILE_BYTES // (2 * k)))
    dims = (((1,), (0,)), ((), ())) if mode == "nn" else (((1,), (1,)), ((), ()))
    b_spec = pl.BlockSpec((k, tn), lambda j, i: (0, j)) if mode == "nn" else pl.BlockSpec((tn, k), lambda j, i: (j, 0))
    o_spec = pl.BlockSpec((tm, tn), lambda j, i: (i, j))

    def body(*refs):
        a_ref, b_ref = refs[0], refs[1]
        o_ref = refs[-1]
        r = lax.dot_general(a_ref[...].astype(bf16), b_ref[...].astype(bf16), dims, preferred_element_type=f32)
        if has_add:
            r = r + refs[2][...]
        o_ref[...] = r.astype(o_ref.dtype)

    return pl.pallas_call(
        body, name=name, grid=(n // tn, m // tm),
        in_specs=[pl.BlockSpec((tm, k), lambda j, i: (i, 0)), b_spec] + ([o_spec] if has_add else []),
        out_specs=o_spec, out_shape=jax.ShapeDtypeStruct((m, n), out_dtype),
        compiler_params=_cp(("parallel", "arbitrary")),
    )(*((a, b, add) if has_add else (a, b)))


def _pieces(ref, widths):
    out, off = [], 0
    for w in widths:
        out.append(ref[:, off:off + w].astype(f32))
        off += w
    return out


def _store_pieces(ref, widths, vals, add_ref=None):
    off = 0
    for w, v in zip(widths, vals):
        ref[:, off:off + w] = (v if add_ref is None else v + add_ref[:, off:off + w]).astype(ref.dtype)
        off += w


def _rows_fwd(name, fn, consts, rows, params, outs, n_sums=0, tm=256, dtypes=None):
    t = (consts + rows)[0][0].shape[0]
    tm = min(tm, t)
    ins = consts + rows
    n_in, n_p, n_o = len(ins), len(params), len(outs)
    dtypes = dtypes or [f32] * n_o

    def body(*refs):
        in_refs, p_refs = refs[:n_in], refs[n_in:n_in + n_p]
        o_refs, s_refs = refs[n_in + n_p:n_in + n_p + n_o], refs[n_in + n_p + n_o:]
        vals = []
        for r, (_, widths) in zip(in_refs, ins):
            vals += _pieces(r, widths)
        res = fn(*vals, *[p[...] for p in p_refs])
        pos = 0
        for r, widths in zip(o_refs, outs):
            _store_pieces(r, widths, res[pos:pos + len(widths)])
            pos += len(widths)

        @pl.when(pl.program_id(0) == 0)
        def _():
            for s in s_refs:
                s[...] = jnp.zeros_like(s)

        for s, v in zip(s_refs, res[pos:]):
            s[...] += v

    row_spec = lambda w: pl.BlockSpec((tm, w), lambda i: (i, 0))
    full = lambda p: pl.BlockSpec(p.shape, lambda i: (0,) * p.ndim)
    return pl.pallas_call(
        body, name=name, grid=(t // tm,),
        in_specs=[row_spec(a.shape[1]) for a, _ in ins] + [full(p) for p in params],
        out_specs=[row_spec(sum(w)) for w in outs] + [pl.BlockSpec((1, 1), lambda i: (0, 0))] * n_sums,
        out_shape=[jax.ShapeDtypeStruct((t, sum(w)), dt) for w, dt in zip(outs, dtypes)] + [jax.ShapeDtypeStruct((1, 1), f32)] * n_sums,
        compiler_params=_cp(("arbitrary",)),
    )(*[a for a, _ in ins], *params)


def _rows_bwd(name, fn, consts, rows, params, outs, cts, n_sums=0, add=None, tm=256, dtypes=None):
    t = (consts + rows)[0][0].shape[0]
    tm = min(tm, t)
    n_c, n_r, n_p, n_o = len(consts), len(rows), len(params), len(outs)
    has_add = add is not None
    dtypes = dtypes or [f32] * n_r

    def body(*refs):
        pos = 0
        c_refs = refs[pos:pos + n_c]; pos += n_c
        r_refs = refs[pos:pos + n_r]; pos += n_r
        p_refs = refs[pos:pos + n_p]; pos += n_p
        ct_refs = refs[pos:pos + n_o]; pos += n_o
        add_ref = refs[pos] if has_add else None
        pos += 1 if has_add else 0
        dr_refs = refs[pos:pos + n_r]; pos += n_r
        dp_refs = refs[pos:pos + n_p]
        cvals, rvals = [], []
        for r, (_, widths) in zip(c_refs, consts):
            cvals += _pieces(r, widths)
        for r, (_, widths) in zip(r_refs, rows):
            rvals += _pieces(r, widths)
        pvals = [p[...] for p in p_refs]
        ctv = []
        for r, widths in zip(ct_refs, outs):
            ctv += _pieces(r, widths)
        ctv += [jnp.ones((1, 1), f32)] * n_sums
        _, vjp = jax.vjp(lambda *rp: tuple(fn(*cvals, *rp)), *rvals, *pvals)
        g = vjp(tuple(ctv))
        pos = 0
        for idx, (r, (_, widths)) in enumerate(zip(dr_refs, rows)):
            _store_pieces(r, widths, g[pos:pos + len(widths)], add_ref if idx == 0 else None)
            pos += len(widths)

        @pl.when(pl.program_id(0) == 0)
        def _():
            for dp in dp_refs:
                dp[...] = jnp.zeros_like(dp)

        for dp, v in zip(dp_refs, g[pos:]):
            dp[...] += v

    row_spec = lambda w: pl.BlockSpec((tm, w), lambda i: (i, 0))
    full = lambda p: pl.BlockSpec(p.shape, lambda i: (0,) * p.ndim)
    args = [a for a, _ in consts + rows] + list(params) + list(cts) + ([add] if has_add else [])
    res = pl.pallas_call(
        body, name=name, grid=(t // tm,),
        in_specs=[row_spec(a.shape[1]) for a, _ in consts + rows] + [full(p) for p in params]
        + [row_spec(sum(w)) for w in outs] + ([row_spec(add.shape[1])] if has_add else []),
        out_specs=[row_spec(a.shape[1]) for a, _ in rows] + [full(p) for p in params],
        out_shape=[jax.ShapeDtypeStruct(a.shape, dt) for (a, _), dt in zip(rows, dtypes)]
        + [jax.ShapeDtypeStruct(p.shape, f32) for p in params],
        compiler_params=_cp(("arbitrary",)),
    )(*args)
    return res[:n_r], res[n_r:]


def _rms(x, g):
    return x * lax.rsqrt(jnp.mean(x * x, axis=-1, keepdims=True) + NORM_EPS) * g


def _fn_rms(x, g):
    return (_rms(x, g),)


def _fn_rwkv_pre(r, k, v, wd, ad, gd, w0, w_up, a0, a_up, g_up, k_k, k_a):
    nn, _, _ = _make_mm(False, False)
    w_log = -_sigmoid(w0 + nn(jnp.tanh(wd), w_up)) * 0.6065306597126334
    a = _sigmoid(a0 + nn(ad, a_up))
    g = nn(_sigmoid(gd), g_up)
    kk = k * k_k
    kk = kk * lax.rsqrt(jnp.maximum(_head_sum(kk * kk), 1e-24))
    k2 = k * (1.0 + (a - 1.0) * k_a)
    return r, w_log, k2, v, -kk, kk * a, g


def _fn_rwkv_post(y, r, k2, v, g, gn_g, gn_b, r_k):
    mean = _head_sum(y) * (1.0 / HD)
    yc = y - mean
    var = _head_sum(yc * yc) * (1.0 / HD)
    yn = yc * lax.rsqrt(var + GN_EPS) * gn_g + gn_b
    bonus = _head_sum(r * k2 * r_k) * v
    return ((yn + bonus) * g,)


def _fn_merge(a_fox, a_rwkv, a_mem, g_fox, g_rwkv, g_mem):
    return (_sigmoid(g_fox) * a_fox + _sigmoid(g_rwkv) * a_rwkv + _sigmoid(g_mem) * a_mem,)


def _fn_post1(y, x, post1_g, pre2_g):
    h1 = x + _rms(y, post1_g)
    return h1, _rms(h1, pre2_g)


def _fn_swiglu(gp, up):
    return (gp * _sigmoid(gp) * up,)


def _fn_final(target, ffn, h1, post2_g):
    err = h1 + _rms(ffn, post2_g) - target
    per_row = jnp.mean(err * err, axis=-1, keepdims=True)
    return (0.5 * jnp.sum(per_row, axis=0, keepdims=True),)


def _shift_down(x):
    row = lax.broadcasted_iota(jnp.int32, x.shape, 0)
    return jnp.where(row == 0, 0.0, pltpu.roll(x, 1, 0))


def _shift_up(x):
    s = x.shape[0]
    row = lax.broadcasted_iota(jnp.int32, x.shape, 0)
    return jnp.where(row == s - 1, 0.0, pltpu.roll(x, s - 1, 0))


def _tokshift_fwd(p, mu, batch, seq):
    w = p.shape[1]
    tc = _tile(w, 384)

    def body(p_ref, mu_ref, o_ref):
        x = p_ref[...]
        o_ref[...] = x + (_shift_down(x) - x) * mu_ref[...]

    return pl.pallas_call(
        body, name="tokshift_fwd", grid=(w // tc, batch),
        in_specs=[pl.BlockSpec((seq, tc), lambda j, b: (b, j)), pl.BlockSpec((1, tc), lambda j, b: (0, j))],
        out_specs=pl.BlockSpec((seq, tc), lambda j, b: (b, j)),
        out_shape=jax.ShapeDtypeStruct(p.shape, f32),
        compiler_params=_cp(("parallel", "arbitrary")),
    )(p, mu)


def _tokshift_bwd(p, mu, dps, batch, seq):
    w = p.shape[1]
    tc = _tile(w, 384)

    def body(p_ref, mu_ref, d_ref, dp_ref, dmu_ref):
        x, mu_v, d = p_ref[...], mu_ref[...], d_ref[...]
        dp_ref[...] = (d * (1.0 - mu_v) + _shift_up(d * mu_v)).astype(dp_ref.dtype)

        @pl.when(pl.program_id(1) == 0)
        def _():
            dmu_ref[...] = jnp.zeros_like(dmu_ref)

        dmu_ref[...] += jnp.sum(d * (_shift_down(x) - x), axis=0, keepdims=True)

    return pl.pallas_call(
        body, name="tokshift_bwd", grid=(w // tc, batch),
        in_specs=[pl.BlockSpec((seq, tc), lambda j, b: (b, j)), pl.BlockSpec((1, tc), lambda j, b: (0, j)),
                  pl.BlockSpec((seq, tc), lambda j, b: (b, j))],
        out_specs=[pl.BlockSpec((seq, tc), lambda j, b: (b, j)), pl.BlockSpec((1, tc), lambda j, b: (0, j))],
        out_shape=[jax.ShapeDtypeStruct(p.shape, bf16), jax.ShapeDtypeStruct(mu.shape, f32)],
        compiler_params=_cp(("parallel", "arbitrary")),
    )(p, mu, dps)


def _cum_block(seq):
    return _tile(seq, 256)


def _fox_gate_fwd(f, bias, batch, seq):
    cb = _cum_block(seq)

    def body(f_ref, b_ref, c_ref):
        row = lax.broadcasted_iota(jnp.int32, (cb, cb), 0)
        col = lax.broadcasted_iota(jnp.int32, (cb, cb), 1)
        tri = (col <= row).astype(f32)
        carry = jnp.zeros((1, 128), f32)
        for i in range(seq // cb):
            z = f_ref[i * cb:(i + 1) * cb, :] + b_ref[...]
            ls = jnp.minimum(z, 0.0) - jnp.log(1.0 + jnp.exp(-jnp.abs(z)))
            c = _dg(tri, ls, (((1,), (0,)), ((), ())), True) + carry
            c_ref[i * cb:(i + 1) * cb, :] = c
            carry = c[cb - 1:cb, :]

    return pl.pallas_call(
        body, name="fox_gate_fwd", grid=(batch,),
        in_specs=[pl.BlockSpec((seq, 128), lambda b: (b, 0)), pl.BlockSpec((1, 128), lambda b: (0, 0))],
        out_specs=pl.BlockSpec((seq, 128), lambda b: (b, 0)),
        out_shape=jax.ShapeDtypeStruct(f.shape, f32),
        compiler_params=_cp(("arbitrary",)),
    )(f, bias)


def _fox_gate_bwd(f, bias, dc_a, dc_b, batch, seq):
    cb = _cum_block(seq)

    def body(f_ref, b_ref, da_ref, db_ref, df_ref, dbias_ref):
        row = lax.broadcasted_iota(jnp.int32, (cb, cb), 0)
        col = lax.broadcasted_iota(jnp.int32, (cb, cb), 1)
        triu = (col >= row).astype(f32)

        @pl.when(pl.program_id(0) == 0)
        def _():
            dbias_ref[...] = jnp.zeros_like(dbias_ref)

        carry = jnp.zeros((1, 128), f32)
        tot = jnp.zeros((1, 128), f32)
        for i in reversed(range(seq // cb)):
            sl = slice(i * cb, (i + 1) * cb)
            dc = da_ref[sl, :] + db_ref[sl, :]
            dls = _dg(triu, dc, (((1,), (0,)), ((), ())), True) + carry
            carry = dls[0:1, :]
            df = dls * _sigmoid(-(f_ref[sl, :] + b_ref[...]))
            df_ref[sl, :] = df.astype(df_ref.dtype)
            tot = tot + jnp.sum(df, axis=0, keepdims=True)
        dbias_ref[...] += tot

    return pl.pallas_call(
        body, name="fox_gate_bwd", grid=(batch,),
        in_specs=[pl.BlockSpec((seq, 128), lambda b: (b, 0)), pl.BlockSpec((1, 128), lambda b: (0, 0)),
                  pl.BlockSpec((seq, 128), lambda b: (b, 0)), pl.BlockSpec((seq, 128), lambda b: (b, 0))],
        out_specs=[pl.BlockSpec((seq, 128), lambda b: (b, 0)), pl.BlockSpec((1, 128), lambda b: (0, 0))],
        out_shape=[jax.ShapeDtypeStruct(f.shape, bf16), jax.ShapeDtypeStruct((1, 128), f32)],
        compiler_params=_cp(("arbitrary",)),
    )(f, bias, dc_a, dc_b)


_HBM_SPEC = pl.BlockSpec(memory_space=pltpu.HBM)


def _side_out_shapes(srcs, per_peer):
    return [jax.ShapeDtypeStruct(((N_DEV,) + tuple(s.shape[1:] if per_peer else s.shape)), s.dtype) for s in srcs]


def _side_sems(n):
    if n == 0:
        return []
    return [pltpu.SemaphoreType.DMA((n, N_DEV - 1)), pltpu.SemaphoreType.DMA((n, N_DEV - 1)), pltpu.SemaphoreType.DMA((n,))]


def _peer_copies(src_refs, dst_refs, per_peer, sems):
    send_sems, recv_sems, local_sems = sems
    x, y, c = lax.axis_index("x"), lax.axis_index("y"), lax.axis_index("c")
    me = 4 * x + 2 * y + c
    copies = []
    for t, (s, d) in enumerate(zip(src_refs, dst_refs)):
        copies.append(pltpu.make_async_copy(s.at[me] if per_peer else s, d.at[me], local_sems.at[t]))
        for k in range(1, N_DEV):
            px = 1 - x if k & 4 else x
            py = 1 - y if k & 2 else y
            pc = 1 - c if k & 1 else c
            copies.append(pltpu.make_async_remote_copy(
                src_ref=s.at[4 * px + 2 * py + pc] if per_peer else s, dst_ref=d.at[me],
                send_sem=send_sems.at[t, k - 1], recv_sem=recv_sems.at[t, k - 1],
                device_id=(px, py, pc), device_id_type=pl.DeviceIdType.MESH))
    return copies


def _side_exchange(src_refs, dst_refs, per_peer, sems, *grid):
    if not src_refs:
        return
    first = functools.reduce(jnp.logical_and, [pl.program_id(a) == 0 for a in range(len(grid))])
    last = functools.reduce(jnp.logical_and, [pl.program_id(a) == n - 1 for a, n in enumerate(grid)])

    @pl.when(first)
    def _():
        for cp in _peer_copies(src_refs, dst_refs, per_peer, sems):
            cp.start()

    @pl.when(last)
    def _():
        for cp in _peer_copies(src_refs, dst_refs, per_peer, sems):
            cp.wait()


def _exchange(name, srcs, per_peer):
    n = len(srcs)

    def body(*refs):
        copies = _peer_copies(refs[:n], refs[n:2 * n], per_peer, refs[2 * n:])
        for cp in copies:
            cp.start()
        for cp in copies:
            cp.wait()

    return pl.pallas_call(
        body, name=name, in_specs=[_HBM_SPEC] * n, out_specs=[_HBM_SPEC] * n,
        out_shape=_side_out_shapes(srcs, per_peer), scratch_shapes=_side_sems(n),
    )(*srcs)


FOX_T = 256
_NEG = -1e30
_D2 = (((1,), (1,)), ((), ()))
_D1 = (((1,), (0,)), ((), ()))
_D0 = (((0,), (0,)), ((), ()))


def _bdot(a, b, dims):
    return lax.dot_general(a.astype(bf16), b.astype(bf16), dims, preferred_element_type=f32)


def _pick_lane(x, lane):
    idx = lax.broadcasted_iota(jnp.int32, x.shape, 1)
    return jnp.sum(jnp.where(idx == lane, x, 0.0), axis=1, keepdims=True)


def _pick_row(x, row):
    idx = lax.broadcasted_iota(jnp.int32, x.shape, 0)
    return jnp.sum(jnp.where(idx == row, x, 0.0), axis=0, keepdims=True)


def _fox_fwd(qkv, c, c_rows, batch, seq, side=None):
    t = min(FOX_T, seq)
    nq = seq // t
    scale = HD ** -0.5
    srcs, per_peer = side if side is not None else ([], False)
    n_s = len(srcs)

    def body(*refs):
        q_ref, k_ref, v_ref, cq_ref, ck_ref = refs[:5]
        o_ref, lse_ref = refs[5 + n_s:7 + n_s]
        _side_exchange(refs[5:5 + n_s], refs[7 + n_s:7 + 2 * n_s], per_peer, refs[7 + 2 * n_s:], batch, PAIRS, nq)
        pair, i = pl.program_id(1), pl.program_id(2)
        lane = lax.broadcasted_iota(jnp.int32, (1, PAIR_W), 1)
        first = (lane // HD) == 0
        q = q_ref[...] * scale
        qs = [jnp.where(first, q, 0.0), jnp.where(first, 0.0, q)]
        cqs = [_pick_lane(cq_ref[...], 2 * pair + e) for e in range(2)]
        qidx = i * t + lax.broadcasted_iota(jnp.int32, (t, t), 0)

        def step(j, carry):
            rows = pl.ds(pl.multiple_of(j * t, t), t)
            kj, vj = k_ref[rows, :], v_ref[rows, :]
            ck_blk = ck_ref[0, :, rows]
            vis = (j * t + lax.broadcasted_iota(jnp.int32, (t, t), 1)) <= qidx
            out = []
            for e in range(2):
                m, l, acc = carry[3 * e:3 * e + 3]
                s = _bdot(qs[e], kj, _D2) + (cqs[e] - _pick_row(ck_blk, 2 * pair + e))
                s = jnp.where(vis, s, _NEG)
                m_new = jnp.maximum(m, jnp.max(s, axis=1, keepdims=True))
                alpha = jnp.exp(m - m_new)
                p = jnp.exp(s - m_new)
                out += [m_new, alpha * l + jnp.sum(p, axis=1, keepdims=True), alpha * acc + _bdot(p, vj, _D1)]
            return tuple(out)

        init = (jnp.full((t, 1), _NEG, f32), jnp.zeros((t, 1), f32), jnp.zeros((t, PAIR_W), f32)) * 2
        m0, l0, a0, m1, l1, a1 = lax.fori_loop(0, i + 1, step, init)
        o_ref[...] = jnp.where(first, a0 / l0, a1 / l1)
        lse_ref[...] = jnp.where(lane == 0, m0 + jnp.log(l0), jnp.where(lane == 1, m1 + jnp.log(l1), 0.0))

    q_spec = pl.BlockSpec((t, PAIR_W), lambda b, p, i: (b * nq + i, p))
    res = pl.pallas_call(
        body, name="fox_attn_fwd", grid=(batch, PAIRS, nq),
        in_specs=[q_spec,
                  pl.BlockSpec((seq, PAIR_W), lambda b, p, i: (b, PAIRS + p)),
                  pl.BlockSpec((seq, PAIR_W), lambda b, p, i: (b, 2 * PAIRS + p)),
                  pl.BlockSpec((t, 128), lambda b, p, i: (b * nq + i, 0)),
                  pl.BlockSpec((1, 8, seq), lambda b, p, i: (b, 0, 0))] + [_HBM_SPEC] * n_s,
        out_specs=[q_spec, q_spec] + [_HBM_SPEC] * n_s,
        out_shape=[jax.ShapeDtypeStruct((batch * seq, HW), f32)] * 2 + _side_out_shapes(srcs, per_peer),
        scratch_shapes=_side_sems(n_s),
        compiler_params=_cp(("arbitrary", "arbitrary", "arbitrary")),
    )(qkv, qkv, qkv, c, c_rows, *srcs)
    return res[0], res[1], list(res[2:])


def _fox_bwd(qkv, c, c_rows, o, lse, do, batch, seq, side=None):
    t = min(FOX_T, seq)
    nq = seq // t
    scale = HD ** -0.5
    srcs, per_peer = side if side is not None else ([], False)
    n_s = len(srcs)

    def body(*refs):
        q_ref, k_ref, v_ref, cq_ref, ck_ref, o_ref, lse_ref, do_ref = refs[:8]
        dq_ref, dk_ref, dv_ref, dcq_ref, dck_ref = refs[8 + n_s:13 + n_s]
        _side_exchange(refs[8:8 + n_s], refs[13 + n_s:13 + 2 * n_s], per_peer, refs[13 + 2 * n_s:], batch, PAIRS, nq)
        pair, j = pl.program_id(1), pl.program_id(2)

        @pl.when(j == 0)
        def _():
            dq_ref[...] = jnp.zeros_like(dq_ref)
            dcq_ref[...] = jnp.zeros_like(dcq_ref)

        lane = lax.broadcasted_iota(jnp.int32, (1, PAIR_W), 1)
        first = (lane // HD) == 0
        sub = lax.broadcasted_iota(jnp.int32, (8, t), 0)
        kj, vj = k_ref[...], v_ref[...]
        ks = [jnp.where(first, kj, 0.0), jnp.where(first, 0.0, kj)]
        cks = [_pick_row(ck_ref[0], 2 * pair + e) for e in range(2)]
        kidx = j * t + lax.broadcasted_iota(jnp.int32, (t, t), 1)
        ones8 = jnp.ones((8, t), f32)

        def step(i, carry):
            dk, dv, dck0, dck1 = carry
            rows = pl.ds(pl.multiple_of(i * t, t), t)
            q = q_ref[rows, :] * scale
            d_o, o_i, lse_i, cq_i = do_ref[rows, :], o_ref[rows, :], lse_ref[rows, :], cq_ref[rows, :]
            vis = kidx <= (i * t + lax.broadcasted_iota(jnp.int32, (t, t), 0))
            dq_acc = jnp.zeros((t, PAIR_W), f32)
            dcq_acc = jnp.zeros((8, t), f32)
            dcks = [dck0, dck1]
            for e in range(2):
                mine = first if e == 0 else jnp.logical_not(first)
                qe, doe = jnp.where(mine, q, 0.0), jnp.where(mine, d_o, 0.0)
                s = _bdot(qe, kj, _D2) + (_pick_lane(cq_i, 2 * pair + e) - cks[e])
                p = jnp.exp(jnp.where(vis, s, _NEG) - _pick_lane(lse_i, e))
                dv = dv + _bdot(p, doe, _D0)
                delta = jnp.sum(doe * o_i, axis=1, keepdims=True)
                ds = p * (_bdot(doe, vj, _D2) - delta)
                dk = dk + _bdot(ds, qe, _D0)
                dq_acc = dq_acc + _bdot(ds, ks[e], _D1)
                row_sums = lax.dot_general(ones8, ds, _D2, precision=_HI, preferred_element_type=f32)
                dcq_acc = dcq_acc + jnp.where(sub == e, row_sums, 0.0)
                dcks[e] = dcks[e] - jnp.sum(ds, axis=0, keepdims=True)
            dq_ref[rows, :] += dq_acc * scale
            dcq_ref[0, :, rows] += dcq_acc
            return dk, dv, dcks[0], dcks[1]

        zero = jnp.zeros((t, PAIR_W), f32)
        dk, dv, dck0, dck1 = lax.fori_loop(j, nq, step, (zero, zero, jnp.zeros((1, t), f32), jnp.zeros((1, t), f32)))
        dk_ref[...] = dk
        dv_ref[...] = dv
        dck_ref[0] = jnp.where(sub == 0, dck0, jnp.where(sub == 1, dck1, 0.0))

    whole = lambda col: pl.BlockSpec((seq, PAIR_W), lambda b, p, j: (b, col * PAIRS + p))
    blk = lambda col: pl.BlockSpec((t, PAIR_W), lambda b, p, j: (b * nq + j, col * PAIRS + p))
    rows_whole = pl.BlockSpec((1, 8, seq), lambda b, p, j: (b * PAIRS + p, 0, 0))
    rows_blk = pl.BlockSpec((1, 8, t), lambda b, p, j: (b * PAIRS + p, 0, j))
    t_all = batch * seq
    res = pl.pallas_call(
        body, name="fox_attn_bwd", grid=(batch, PAIRS, nq),
        in_specs=[whole(0), blk(1), blk(2),
                  pl.BlockSpec((seq, 128), lambda b, p, j: (b, 0)),
                  pl.BlockSpec((1, 8, t), lambda b, p, j: (b, 0, j)),
                  whole(0), whole(0), whole(0)] + [_HBM_SPEC] * n_s,
        out_specs=[whole(0), blk(0), blk(0), rows_whole, rows_blk] + [_HBM_SPEC] * n_s,
        out_shape=[jax.ShapeDtypeStruct((t_all, HW), f32)] * 3
        + [jax.ShapeDtypeStruct((batch * PAIRS, 8, seq), f32)] * 2 + _side_out_shapes(srcs, per_peer),
        scratch_shapes=_side_sems(n_s),
        compiler_params=_cp(("arbitrary", "arbitrary", "arbitrary")),
    )(qkv, qkv, qkv, c, c_rows, o, lse, do, *srcs)
    return res[:5], list(res[5:])


def _mem_block(q, km, vm):
    nn, nt, _ = _make_mm(False, False)
    logits = nt(q, km) * (MEM_HD ** -0.5)
    m = lax.stop_gradient(jnp.max(logits, axis=-1, keepdims=True))
    e = jnp.exp(logits - m)
    return nn(e / jnp.sum(e, axis=-1, keepdims=True), vm)


def _mem_specs(seq, tq):
    nq = seq // tq
    qs = pl.BlockSpec((tq, MEM_HD), lambda b, h, i: (b * nq + i, h))
    ks = pl.BlockSpec((MEM_LEN, MEM_HD), lambda b, h, i: (b, h))
    vs = pl.BlockSpec((MEM_LEN, MEM_HD), lambda b, h, i: (b, MEM_HEADS + h))
    return nq, qs, ks, vs


def _mem_fwd(q, mem_kv, batch, seq):
    tq = min(512, seq)
    nq, qs, ks, vs = _mem_specs(seq, tq)

    def body(q_ref, k_ref, v_ref, o_ref):
        o_ref[...] = _mem_block(q_ref[...], k_ref[...], v_ref[...]).astype(o_ref.dtype)

    return pl.pallas_call(
        body, name="mem_attn_fwd", grid=(batch, MEM_HEADS, nq),
        in_specs=[qs, ks, vs], out_specs=qs, out_shape=jax.ShapeDtypeStruct(q.shape, bf16),
        compiler_params=_cp(("parallel", "parallel", "arbitrary")),
    )(q, mem_kv, mem_kv)


def _mem_bwd(q, mem_kv, do, batch, seq):
    tq = min(512, seq)
    nq, qs, ks, vs = _mem_specs(seq, tq)

    def body(q_ref, k_ref, v_ref, do_ref, dq_ref, dk_ref, dv_ref):
        _, vjp = jax.vjp(_mem_block, q_ref[...], k_ref[...], v_ref[...])
        dq, dk, dv = vjp(do_ref[...])
        dq_ref[...] = dq.astype(dq_ref.dtype)

        @pl.when(pl.program_id(2) == 0)
        def _():
            dk_ref[...] = jnp.zeros_like(dk_ref)
            dv_ref[...] = jnp.zeros_like(dv_ref)

        dk_ref[...] += dk
        dv_ref[...] += dv

    return pl.pallas_call(
        body, name="mem_attn_bwd", grid=(batch, MEM_HEADS, nq),
        in_specs=[qs, ks, vs, qs], out_specs=[qs, ks, ks],
        out_shape=[jax.ShapeDtypeStruct(q.shape, bf16), jax.ShapeDtypeStruct((batch * MEM_LEN, MEM_W), f32),
                   jax.ShapeDtypeStruct((batch * MEM_LEN, MEM_W), f32)],
        compiler_params=_cp(("parallel", "parallel", "arbitrary")),
    )(q, mem_kv, mem_kv, do)


@jax.custom_vjp
def _halves(x):
    c = x.shape[1] // 2
    return x[:, :c], x[:, c:]


_halves.defvjp(lambda x: ((x[:, :x.shape[1] // 2], x[:, x.shape[1] // 2:]), None),
               lambda _, g: (jnp.concatenate(g, axis=1),))


@jax.custom_vjp
def _lead_halves(x):
    n = x.shape[0] // 2
    return x[:n], x[n:]


_lead_halves.defvjp(lambda x: ((x[:x.shape[0] // 2], x[x.shape[0] // 2:]), None),
                    lambda _, g: (jnp.concatenate(g, axis=0),))


def _scan_chunk(s0, r, wl, k, v, a, b):
    nn, nt, tn = _make_mm(True, False)
    nn_exact, nt_exact, _ = _make_mm(True, True)
    hp, c, lanes = r.shape
    row = lax.broadcasted_iota(jnp.int32, (c, c), 0)
    col = lax.broadcasted_iota(jnp.int32, (c, c), 1)
    first = (lax.broadcasted_iota(jnp.int32, (1, 1, lanes), 2) // HD) == 0
    tri = jnp.broadcast_to((col <= row).astype(f32)[None], (hp, c, c))
    lg = nn_exact(tri, wl)
    lg_end = lg[:, c - 1:c, :]
    grow, shrink, to_end = jnp.exp(lg), jnp.exp(-lg), jnp.exp(lg_end - lg)
    rt, kt, bt, at = r * grow, k * shrink, b * shrink, a * jnp.exp(lg - wl)
    strict, incl = (col < row)[None], (col <= row)[None]
    twice = lambda t: jnp.concatenate([t, t], axis=0)
    queries = jnp.concatenate([at, rt], axis=1)
    per_head = jnp.concatenate([jnp.where(first, queries, 0.0), jnp.where(first, 0.0, queries)], axis=0)
    (ab, rb), (ak, rk) = _halves(nt_exact(per_head, twice(bt))), _halves(nt_exact(per_head, twice(kt)))
    l_ab = jnp.where(strict, ab, 0.0)
    a_ak = jnp.where(strict, ak, 0.0)
    a_rb = jnp.where(incl, rb, 0.0)
    a_rk = jnp.where(incl, rk, 0.0)
    inv = (col == row).astype(f32)[None] + l_ab
    power, n = l_ab, 1
    while 2 * n < c:
        power = nn(power, power)
        inv = inv + nn(inv, power)
        n *= 2

    def apply(m, t):
        lo, hi = _lead_halves(nn(m, twice(t)))
        return jnp.where(first, lo, hi)

    sa = apply(inv, nt(at, s0) + apply(a_ak, v))
    y = nt(rt, s0) + apply(a_rk, v) + apply(a_rb, sa)
    same_head = ((lax.broadcasted_iota(jnp.int32, (lanes, lanes), 0) // HD)
                 == (lax.broadcasted_iota(jnp.int32, (lanes, lanes), 1) // HD))[None]
    s1 = s0 * jnp.exp(lg_end) + jnp.where(same_head, tn(v, k * to_end) + tn(sa, b * to_end), 0.0)
    return y, s1


PAIRS = HEADS // 2
PAIR_W = 2 * HD


def _pair_stack(ref, comp):
    return jnp.stack([ref[:, comp * HW + p * PAIR_W:comp * HW + (p + 1) * PAIR_W] for p in range(PAIRS)])


def _scan_fwd(main6, batch, seq):
    c = min(SCAN_CHUNK, seq)
    nc = seq // c

    def body(z_ref, y_ref, s_ref, st):
        @pl.when(pl.program_id(1) == 0)
        def _():
            st[...] = jnp.zeros_like(st)

        s0 = st[...]
        s_ref[0, 0] = s0
        y, s1 = _scan_chunk(s0, *[_pair_stack(z_ref, comp) for comp in range(6)])
        for p in range(PAIRS):
            y_ref[:, p * PAIR_W:(p + 1) * PAIR_W] = y[p]
        st[...] = s1

    return pl.pallas_call(
        body, name="rwkv_scan_fwd", grid=(batch, nc),
        in_specs=[pl.BlockSpec((c, 6 * HW), lambda b, i: (b * nc + i, 0))],
        out_specs=[pl.BlockSpec((c, HW), lambda b, i: (b * nc + i, 0)),
                   pl.BlockSpec((1, 1, PAIRS, PAIR_W, PAIR_W), lambda b, i: (b, i, 0, 0, 0))],
        out_shape=[jax.ShapeDtypeStruct((batch * seq, HW), f32), jax.ShapeDtypeStruct((batch, nc, PAIRS, PAIR_W, PAIR_W), f32)],
        scratch_shapes=[pltpu.VMEM((PAIRS, PAIR_W, PAIR_W), f32)],
        compiler_params=_cp(("parallel", "arbitrary")),
    )(main6)


def _scan_bwd(main6, states, dy, extra, batch, seq):
    c = min(SCAN_CHUNK, seq)
    nc = seq // c

    def body(z_ref, s_ref, dy_ref, ex_ref, dz_ref, dst):
        @pl.when(pl.program_id(1) == 0)
        def _():
            dst[...] = jnp.zeros_like(dst)

        dy_pairs = jnp.stack([dy_ref[:, p * PAIR_W:(p + 1) * PAIR_W] for p in range(PAIRS)])
        _, vjp = jax.vjp(_scan_chunk, s_ref[0, 0], *[_pair_stack(z_ref, comp) for comp in range(6)])
        g = vjp((dy_pairs, dst[...]))
        dst[...] = g[0]
        for comp in range(6):
            for p in range(PAIRS):
                sl = slice(comp * HW + p * PAIR_W, comp * HW + (p + 1) * PAIR_W)
                dz_ref[:, sl] = g[1 + comp][p] + ex_ref[:, sl]

    back = lambda b, i: (b * nc + nc - 1 - i, 0)
    return pl.pallas_call(
        body, name="rwkv_scan_bwd", grid=(batch, nc),
        in_specs=[pl.BlockSpec((c, 6 * HW), back),
                  pl.BlockSpec((1, 1, PAIRS, PAIR_W, PAIR_W), lambda b, i: (b, nc - 1 - i, 0, 0, 0)),
                  pl.BlockSpec((c, HW), back), pl.BlockSpec((c, 6 * HW), back)],
        out_specs=pl.BlockSpec((c, 6 * HW), back),
        out_shape=jax.ShapeDtypeStruct(main6.shape, f32),
        scratch_shapes=[pltpu.VMEM((PAIRS, PAIR_W, PAIR_W), f32)],
        compiler_params=_cp(("parallel", "arbitrary")),
    )(main6, states, dy, extra)


def _to_heads(x, batch, seq, k):
    return x.reshape(batch, seq, k, HEADS, HD).transpose(2, 0, 3, 1, 4).reshape(k, batch * HEADS, seq, HD)


def _from_heads(x, batch, seq, k):
    return x.reshape(k, batch, HEADS, seq, HD).transpose(1, 3, 0, 2, 4).reshape(batch * seq, k * HW)


def _pad_cols(x, width):
    return jnp.pad(x, ((0, 0), (0, width - x.shape[1])))


def _split_w_in(w):
    z64 = jnp.zeros((w.shape[0], 64), w.dtype)
    w_r = jnp.concatenate([w[:, 1544:3080], w[:, 3080:3144], z64, w[:, 3144:3208], z64, w[:, 3208:3336]], axis=1)
    return w[:, :1536], _pad_cols(w[:, 1536:1544], 128), w_r, w[:, 3336:3848], w[:, 3848:]


def _merge_w_in(g_qkv, g_f, g_r, g_mq, g_g):
    return jnp.concatenate([g_qkv, g_f[:, :8], g_r[:, :1536], g_r[:, 1536:1600], g_r[:, 1664:1728], g_r[:, 1792:],
                            g_mq, g_g], axis=1)


def _pad_lora(v):
    z64 = jnp.zeros((1, 64), v.dtype)
    return jnp.concatenate([v[:, :1536], v[:, 1536:1600], z64, v[:, 1600:1664], z64, v[:, 1664:]], axis=1)


def _unpad_lora(v):
    return jnp.concatenate([v[:, :1536], v[:, 1536:1600], v[:, 1664:1728], v[:, 1792:]], axis=1)


def _local_step(x, mem, target, w, p, late=None, early=None):
    batch, seq, _ = x.shape
    t = batch * seq
    x2, tg2, mem2 = x.reshape(t, D), target.reshape(t, D), mem.reshape(batch * MEM_LEN, D)
    w_qkv, w_f, w_r, w_mq, w_g3 = _split_w_in(w["w_in"])
    mu = _pad_lora(p["rwkv_mu"])
    bias = _pad_cols(p["fox_f_bias"], 128)
    r_k = p["rwkv_r_k"].reshape(1, HW)
    post_params = [p["rwkv_gn_g"], p["rwkv_gn_b"], r_k]
    rw_widths = [HW, HW, HW, LORA_PAD, LORA_PAD, LORA_PAD]
    six = [HW] * 6

    (u,) = _rows_fwd("rms_pre1", _fn_rms, [], [(x2, [D])], [p["pre1_g"]], [[D]], dtypes=[bf16])
    p_qkv = _matmul("proj_qkv", u, w_qkv, "nn")
    p_f = _matmul("proj_f", u, w_f, "nn")
    p_r = _matmul("proj_rwkv", u, w_r, "nn")
    p_mq = _matmul("proj_memq", u, w_mq, "nn")
    p_g = _matmul("proj_gate", u, w_g3, "nn")

    c = _fox_gate_fwd(p_f, bias, batch, seq)
    c_rows = c[:, :HEADS].reshape(batch, seq, HEADS).transpose(0, 2, 1)
    fox_o, lse, gathered = _fox_fwd(p_qkv, c, c_rows, batch, seq, side=(late[0], False) if late else None)
    if late:
        w = {**w, **late[1](gathered)}
    fox_out = fox_o.astype(bf16)

    w_up = jnp.pad(w["rwkv_w_up"].astype(f32), ((0, LORA_PAD - 64), (0, 0)))
    a_up = jnp.pad(w["rwkv_a_up"].astype(f32), ((0, LORA_PAD - 64), (0, 0)))
    pre_params = [p["rwkv_w0"], w_up, p["rwkv_a0"], a_up, w["rwkv_g_up"].astype(f32), p["rwkv_k_k"], p["rwkv_k_a"]]
    ps = _tokshift_fwd(p_r, mu, batch, seq)
    main6, g_rw = _rows_fwd("rwkv_pre", _fn_rwkv_pre, [], [(ps, rw_widths)], pre_params, [six, [HW]])
    y_rw, states = _scan_fwd(main6, batch, seq)
    post_consts = []
    post_rows = [(y_rw, [HW]), (main6, six), (g_rw, [HW])]

    def fn_post(y, r, _wl, k2, v, _a, _b, g, gn_g, gn_b, rk):
        return _fn_rwkv_post(y, r, k2, v, g, gn_g, gn_b, rk)

    (rwkv_out,) = _rows_fwd("rwkv_post", fn_post, post_consts, post_rows, post_params, [[HW]], dtypes=[bf16])

    (memn,) = _rows_fwd("rms_mem", _fn_rms, [], [(mem2, [D])], [p["mem_norm_g"]], [[D]], dtypes=[bf16])
    mem_kv = _matmul("proj_memkv", memn, w["w_mem_kv"], "nn")
    mem_out = _mem_fwd(p_mq, mem_kv, batch, seq)

    a_fox = _matmul("out_fox", fox_out, w["w_fox_out"], "nn")
    a_rwkv = _matmul("out_rwkv", rwkv_out, w["w_rwkv_out"], "nn")
    a_mem = _matmul("out_mem", mem_out, w["w_mem_out"], "nn")
    merge_rows = [(a_fox, [D]), (a_rwkv, [D]), (a_mem, [D]), (p_g, [D, D, D])]
    (merged,) = _rows_fwd("merge", _fn_merge, [], merge_rows, [], [[D]], dtypes=[bf16])
    yy = _matmul("out_o", merged, w["w_o"], "nn")
    post1_rows = [(yy, [D]), (x2, [D])]
    post1_params = [p["post1_g"], p["pre2_g"]]
    h1, u2 = _rows_fwd("post1", _fn_post1, [], post1_rows, post1_params, [[D], [D]], dtypes=[f32, bf16])
    gp = _matmul("ffn_gate", u2, w["w_ffn_gate"], "nn")
    up = _matmul("ffn_up", u2, w["w_ffn_up"], "nn")
    (hmid,) = _rows_fwd("swiglu", _fn_swiglu, [], [(gp, [D_FF]), (up, [D_FF])], [], [[D_FF]], dtypes=[bf16])
    ffn = _matmul("ffn_down", hmid, w["w_ffn_down"], "nn")
    final_rows = [(ffn, [D]), (h1, [D])]
    (loss,) = _rows_fwd("final", _fn_final, [(tg2, [D])], final_rows, [p["post2_g"]], [], n_sums=1)

    gw, gp_ = {}, {}
    (d_ffn, d_h1), (gp_["post2_g"],) = _rows_bwd("final_bwd", _fn_final, [(tg2, [D])], final_rows, [p["post2_g"]], [], [],
                                                  n_sums=1, dtypes=[bf16, f32])
    d_hmid = _matmul("ffn_down_dx", d_ffn, w["w_ffn_down"], "nt")
    gw["w_ffn_down"] = _matmul("ffn_down_dw", hmid, d_ffn, "tn")
    (d_gp, d_up), _ = _rows_bwd("swiglu_bwd", _fn_swiglu, [], [(gp, [D_FF]), (up, [D_FF])], [], [[D_FF]], [d_hmid],
                                dtypes=[bf16, bf16])
    d_u2 = _matmul("ffn_gate_dx", d_gp, w["w_ffn_gate"], "nt")
    d_u2 = _matmul("ffn_up_dx", d_up, w["w_ffn_up"], "nt", add=d_u2)
    gw["w_ffn_gate"] = _matmul("ffn_gate_dw", u2, d_gp, "tn")
    gw["w_ffn_up"] = _matmul("ffn_up_dw", u2, d_up, "tn")
    (d_yy, d_x_res), (gp_["post1_g"], gp_["pre2_g"]) = _rows_bwd(
        "post1_bwd", _fn_post1, [], post1_rows, post1_params, [[D], [D]], [d_h1, d_u2], dtypes=[bf16, f32])
    d_merged = _matmul("out_o_dx", d_yy, w["w_o"], "nt")
    gw["w_o"] = _matmul("out_o_dw", merged, d_yy, "tn")
    (d_a_fox, d_a_rwkv, d_a_mem, d_p_g), _ = _rows_bwd("merge_bwd", _fn_merge, [], merge_rows, [], [[D]], [d_merged],
                                                       dtypes=[bf16] * 4)
    d_fox_out = _matmul("out_fox_dx", d_a_fox, w["w_fox_out"], "nt")
    gw["w_fox_out"] = _matmul("out_fox_dw", fox_out, d_a_fox, "tn")
    d_rwkv_out = _matmul("out_rwkv_dx", d_a_rwkv, w["w_rwkv_out"], "nt")
    gw["w_rwkv_out"] = _matmul("out_rwkv_dw", rwkv_out, d_a_rwkv, "tn")
    d_mem_out = _matmul("out_mem_dx", d_a_mem, w["w_mem_out"], "nt")
    gw["w_mem_out"] = _matmul("out_mem_dw", mem_out, d_a_mem, "tn")

    d_p_mq, d_km, d_vm = _mem_bwd(p_mq, mem_kv, d_mem_out, batch, seq)
    d_mem_kv = jnp.concatenate([d_km, d_vm], axis=1).astype(bf16)
    gw["w_mem_kv"] = _matmul("proj_memkv_dw", memn, d_mem_kv, "tn")
    d_memn = _matmul("proj_memkv_dx", d_mem_kv, w["w_mem_kv"], "nt")
    _, (gp_["mem_norm_g"],) = _rows_bwd("rms_mem_bwd", _fn_rms, [], [(mem2, [D])], [p["mem_norm_g"]], [[D]], [d_memn])

    (d_q, d_k, d_v, d_cq, d_ck), early_got = _fox_bwd(p_qkv, c, c_rows, fox_o, lse, d_fox_out, batch, seq,
                                                      side=(early(gw), True) if early else None)
    d_p_qkv = jnp.concatenate([d_q, d_k, d_v], axis=1).astype(bf16)

    def c_layout(dc):
        return _pad_cols(dc[:, :2].reshape(batch, HEADS, seq).transpose(0, 2, 1).reshape(t, HEADS), 128)

    d_p_f, d_bias = _fox_gate_bwd(p_f, bias, c_layout(d_cq), c_layout(d_ck), batch, seq)
    gp_["fox_f_bias"] = d_bias[:, :HEADS]

    (d_y_rw, d_main6_post, d_g_rw), (gp_["rwkv_gn_g"], gp_["rwkv_gn_b"], d_rk) = _rows_bwd(
        "rwkv_post_bwd", fn_post, post_consts, post_rows, post_params, [[HW]], [d_rwkv_out])
    gp_["rwkv_r_k"] = d_rk.reshape(1, HEADS, HD)
    d_main6 = _scan_bwd(main6, states, d_y_rw, d_main6_post, batch, seq)

    def fn_pre_sum(*args):
        return _fn_rwkv_pre(*args)

    (d_ps,), d_pre = _rows_bwd("rwkv_pre_bwd", fn_pre_sum, [], [(ps, rw_widths)], pre_params, [six, [HW]],
                               [d_main6, d_g_rw])
    gp_["rwkv_w0"], d_w_up, gp_["rwkv_a0"], d_a_up, gw["rwkv_g_up"], gp_["rwkv_k_k"], gp_["rwkv_k_a"] = d_pre
    gw["rwkv_w_up"], gw["rwkv_a_up"] = d_w_up[:64], d_a_up[:64]
    d_p_r, d_mu = _tokshift_bwd(p_r, mu, d_ps, batch, seq)
    gp_["rwkv_mu"] = _unpad_lora(d_mu)

    d_u = _matmul("proj_qkv_dx", d_p_qkv, w_qkv, "nt")
    d_u = _matmul("proj_f_dx", d_p_f, w_f, "nt", add=d_u)
    d_u = _matmul("proj_rwkv_dx", d_p_r, w_r, "nt", add=d_u)
    d_u = _matmul("proj_memq_dx", d_p_mq, w_mq, "nt", add=d_u)
    d_u = _matmul("proj_gate_dx", d_p_g, w_g3, "nt", add=d_u)
    gw["w_in"] = _merge_w_in(_matmul("proj_qkv_dw", u, d_p_qkv, "tn"), _matmul("proj_f_dw", u, d_p_f, "tn"),
                             _matmul("proj_rwkv_dw", u, d_p_r, "tn"), _matmul("proj_memq_dw", u, d_p_mq, "tn"),
                             _matmul("proj_gate_dw", u, d_p_g, "tn"))
    (d_x,), (gp_["pre1_g"],) = _rows_bwd("rms_pre1_bwd", _fn_rms, [], [(x2, [D])], [p["pre1_g"]], [[D]], [d_u], add=d_x_res)
    return loss, d_x.reshape(x.shape), gw, gp_, early_got


def _rows_add(name, a, b):
    (s,) = _rows_fwd(name, lambda u, v: (u + v,), [], [(a, [a.shape[1]]), (b, [b.shape[1]])], [], [[a.shape[1]]])
    return s


def _adamw(name, recv, w, m, v):
    rows, cols = w.shape
    tr = max(t for t in range(16, min(rows, 128) + 1, 16) if rows % t == 0)

    def body(g_ref, w_ref, m_ref, v_ref, go_ref, d_ref, mo_ref, vo_ref):
        g = g_ref[0].astype(f32)
        for s in range(1, N_DEV):
            g = g + g_ref[s].astype(f32)
        m_new = ADAM_B1 * m_ref[...] + (1.0 - ADAM_B1) * g
        v_new = ADAM_B2 * v_ref[...] + (1.0 - ADAM_B2) * (g * g)
        m_hat = m_new / (1.0 - ADAM_B1 ** ADAM_STEP)
        v_hat = v_new / (1.0 - ADAM_B2 ** ADAM_STEP)
        go_ref[...] = g
        d_ref[...] = -ADAM_LR * (m_hat / (jnp.sqrt(v_hat) + ADAM_EPS) + ADAM_WD * w_ref[...])
        mo_ref[...] = m_new
        vo_ref[...] = v_new

    spec = pl.BlockSpec((tr, cols), lambda i: (i, 0))
    return pl.pallas_call(
        body, name=name, grid=(rows // tr,),
        in_specs=[pl.BlockSpec((N_DEV, tr, cols), lambda i: (0, i, 0)), spec, spec, spec],
        out_specs=[spec] * 4, out_shape=[jax.ShapeDtypeStruct(w.shape, f32)] * 4,
        compiler_params=_cp(("parallel",)),
    )(recv, w, m, v)


GROUPS = (
    ("in", ("w_in",), 1),
    ("memkv", ("w_mem_kv",), 0),
    ("ffn_gu", ("w_ffn_gate", "w_ffn_up"), 1),
    ("down_o", ("w_ffn_down", "w_o"), 0),
    ("outs", ("w_fox_out", "w_rwkv_out", "w_mem_out"), 1),
    ("lora", ("rwkv_w_up", "rwkv_a_up", "rwkv_g_up"), 0),
)
FIRST_GROUPS = ("in", "memkv")
LATE_GROUPS = ("ffn_gu", "down_o", "outs", "lora")
EARLY_GRAD_GROUPS = ("memkv", "ffn_gu", "down_o", "outs")
LAST_GRAD_GROUPS = ("in", "lora")
SHARD_AXIS = {n: a for n, _, a in SHARDED}
SMALL_ROWS = 16


def _group_local(shards, members, join):
    parts = [shards[n].reshape(shards[n].shape[-2:]) for n in members]
    return parts[0] if len(parts) == 1 else jnp.concatenate(parts, axis=join)


def _group_split(arr, members, join, lead=False):
    out, off = {}, 0
    for n in members:
        shape = dict((k, s) for k, s, _ in SHARDED)[n]
        size = _block_shape(shape, SHARD_AXIS[n])[join]
        idx = [slice(None)] * arr.ndim
        idx[arr.ndim - 2 + join] = slice(off, off + size)
        out[n] = arr[tuple(idx)]
        off += size
    return out


def _full_from_blocks(blocks, axis):
    if axis == 0:
        return blocks.reshape(-1, blocks.shape[2])
    return blocks.transpose(1, 0, 2).reshape(blocks.shape[1], -1)


def _blocks_from_full(full, axis):
    if axis == 0:
        return full.reshape(N_DEV, -1, full.shape[1])
    return full.reshape(full.shape[0], N_DEV, -1).transpose(1, 0, 2)


def _assemble(gathered, names):
    out = {}
    for arr, g in zip(gathered, names):
        _, members, join = [grp for grp in GROUPS if grp[0] == g][0]
        for n, blk in _group_split(arr, members, join, lead=True).items():
            out[n] = _full_from_blocks(blk, SHARD_AXIS[n])
    return out


def _grad_blocks(gw, names):
    out = []
    for g in names:
        _, members, join = [grp for grp in GROUPS if grp[0] == g][0]
        parts = [_blocks_from_full(gw[n].astype(bf16), SHARD_AXIS[n]) for n in members]
        out.append(parts[0] if len(parts) == 1 else jnp.concatenate(parts, axis=1 + join))
    return out


def _small_pack(d):
    flat = jnp.concatenate([d[n].reshape(-1) for n, _ in REPLICATED])
    return jnp.pad(flat, (0, SMALL_ROWS * LANES - REPL_ELEMS)).reshape(SMALL_ROWS, LANES)


def _small_unpack(packed):
    out, flat, off = {}, packed.reshape(-1), 0
    for n, shape in REPLICATED:
        k = _rows_of((LANES,) + shape)
        out[n] = flat[off:off + k].reshape(shape)
        off += k
    return out


def kernel(x, mem, pre1_g, post1_g, pre2_g, post2_g, mem_norm_g, w_in, fox_f_bias, rwkv_mu, rwkv_w0, rwkv_w_up, rwkv_a0, rwkv_a_up, rwkv_g_up, rwkv_k_k, rwkv_k_a, rwkv_r_k, rwkv_gn_g, rwkv_gn_b, w_mem_kv, w_fox_out, w_rwkv_out, w_mem_out, w_o, w_ffn_gate, w_ffn_up, w_ffn_down, loss_target, m_pre1_g, m_post1_g, m_pre2_g, m_post2_g, m_mem_norm_g, m_w_in, m_fox_f_bias, m_rwkv_mu, m_rwkv_w0, m_rwkv_w_up, m_rwkv_a0, m_rwkv_a_up, m_rwkv_g_up, m_rwkv_k_k, m_rwkv_k_a, m_rwkv_r_k, m_rwkv_gn_g, m_rwkv_gn_b, m_w_mem_kv, m_w_fox_out, m_w_rwkv_out, m_w_mem_out, m_w_o, m_w_ffn_gate, m_w_ffn_up, m_w_ffn_down, v_pre1_g, v_post1_g, v_pre2_g, v_post2_g, v_mem_norm_g, v_w_in, v_fox_f_bias, v_rwkv_mu, v_rwkv_w0, v_rwkv_w_up, v_rwkv_a0, v_rwkv_a_up, v_rwkv_g_up, v_rwkv_k_k, v_rwkv_k_a, v_rwkv_r_k, v_rwkv_gn_g, v_rwkv_gn_b, v_w_mem_kv, v_w_fox_out, v_w_rwkv_out, v_w_mem_out, v_w_o, v_w_ffn_gate, v_w_ffn_up, v_w_ffn_down):
    args = dict(locals())
    wts = {n: args[n] for n in WEIGHT_ORDER}
    ms = {n: args["m_" + n] for n in WEIGHT_ORDER}
    vs = {n: args["v_" + n] for n in WEIGHT_ORDER}

    groups = {g: (members, join) for g, members, join in GROUPS}
    w_bf16 = {n: wts[n].astype(bf16) for n, _, _ in SHARDED}

    def send(g):
        return _group_local(w_bf16, *groups[g])

    first = _exchange("gather_first", [send(g) for g in FIRST_GROUPS], per_peer=False)
    full = _assemble(first, FIRST_GROUPS)
    small_in = {n: (wts[n] if n == "rwkv_r_k" else wts[n].reshape(wts[n].shape[-2:])) for n, _ in REPLICATED}
    late = ([send(g) for g in LATE_GROUPS], lambda got: _assemble(got, LATE_GROUPS))
    loss_part, grad_x, gw, gp, early_got = _local_step(
        x, mem, loss_target, full, small_in, late=late, early=lambda g: _grad_blocks(g, EARLY_GRAD_GROUPS))

    small_send = jnp.broadcast_to(_small_pack(gp).astype(bf16)[None], (N_DEV, SMALL_ROWS, LANES))
    *last_got, small_got = _exchange("exchange_last", _grad_blocks(gw, LAST_GRAD_GROUPS) + [small_send], per_peer=True)
    received = dict(zip(EARLY_GRAD_GROUPS + LAST_GRAD_GROUPS, list(early_got) + list(last_got)))

    outs = [{}, {}, {}, {}]
    for g, members, join in GROUPS:
        res = _adamw("adamw_" + g, received[g], *[_group_local(d, members, join) for d in (wts, ms, vs)])
        for o, arr in zip(outs, res):
            o.update(_group_split(arr, members, join))
    res = _adamw("adamw_small", small_got, *[_small_pack(d) for d in (wts, ms, vs)])
    for o, arr in zip(outs, res):
        o.update(_small_unpack(arr))
    loss = lax.psum(loss_part[0, 0], ("x", "y", "c"))
    return (loss, grad_x, *[o[n].reshape(wts[n].shape) for o in outs for n in WEIGHT_ORDER])
```

```python
import functools

import jax
import jax.numpy as jnp
from jax import lax
from jax.experimental import pallas as pl
from jax.experimental.pallas import tpu as pltpu

f32 = jnp.float32
bf16 = jnp.bfloat16
_HI = lax.Precision.HIGHEST

D = 1024
HEADS = 8
HD = 64
HW = HEADS * HD
MEM_HEADS = 4
MEM_HD = 128
MEM_W = 512
MEM_LEN = 256
D_FF = 2816
LORA_PAD = 128
RW_COLS = 3 * HW + 3 * LORA_PAD
NORM_EPS = 1e-6
GN_EPS = 64e-5
Q_BLOCK = 128
SCAN_CHUNK = 64
N_DEV = 8
LANES = 1024
VMEM_LIMIT = 56 * 1024 * 1024

ADAM_LR = 0.001
ADAM_B1 = 0.9
ADAM_B2 = 0.999
ADAM_EPS = 1e-08
ADAM_WD = 0.01
ADAM_STEP = 10

SHARDED = (
    ("w_in", (1024, 6920), 1),
    ("w_ffn_gate", (1024, 2816), 1),
    ("w_ffn_up", (1024, 2816), 1),
    ("w_ffn_down", (2816, 1024), 0),
    ("w_mem_kv", (1024, 1024), 0),
    ("w_o", (1024, 1024), 0),
    ("w_fox_out", (512, 1024), 1),
    ("w_rwkv_out", (512, 1024), 1),
    ("w_mem_out", (512, 1024), 1),
    ("rwkv_w_up", (64, 512), 1),
    ("rwkv_a_up", (64, 512), 1),
    ("rwkv_g_up", (128, 512), 1),
)
REPLICATED = (
    ("pre1_g", (1, 1024)), ("post1_g", (1, 1024)), ("pre2_g", (1, 1024)), ("post2_g", (1, 1024)),
    ("mem_norm_g", (1, 1024)), ("fox_f_bias", (1, 8)), ("rwkv_mu", (1, 1792)), ("rwkv_w0", (1, 512)),
    ("rwkv_a0", (1, 512)), ("rwkv_k_k", (1, 512)), ("rwkv_k_a", (1, 512)), ("rwkv_r_k", (1, 8, 64)),
    ("rwkv_gn_g", (1, 512)), ("rwkv_gn_b", (1, 512)),
)
WEIGHT_ORDER = ('pre1_g', 'post1_g', 'pre2_g', 'post2_g', 'mem_norm_g', 'w_in', 'fox_f_bias', 'rwkv_mu',
                'rwkv_w0', 'rwkv_w_up', 'rwkv_a0', 'rwkv_a_up', 'rwkv_g_up', 'rwkv_k_k', 'rwkv_k_a',
                'rwkv_r_k', 'rwkv_gn_g', 'rwkv_gn_b', 'w_mem_kv', 'w_fox_out', 'w_rwkv_out', 'w_mem_out',
                'w_o', 'w_ffn_gate', 'w_ffn_up', 'w_ffn_down')


def _block_shape(shape, axis):
    return tuple(s // N_DEV if i == axis else s for i, s in enumerate(shape))


def _rows_of(shape):
    n = 1
    for s in shape:
        n *= s
    return n // LANES


SHARD_ROWS = sum(_rows_of(_block_shape(s, a)) for _, s, a in SHARDED)
REPL_ELEMS = sum(_rows_of((LANES,) + s) for _, s in REPLICATED)
REPL_ROWS = -(-REPL_ELEMS // LANES)
PACK_ROWS = -(-(SHARD_ROWS + REPL_ROWS) // 128) * 128
GATHER_ROWS = -(-SHARD_ROWS // 16) * 16


def _cp(sem=None):
    return pltpu.CompilerParams(dimension_semantics=sem, vmem_limit_bytes=VMEM_LIMIT)


def _tile(dim, cap):
    best = None
    for t in range(128, min(dim, cap) + 1, 128):
        if dim % t == 0:
            best = t
    return best if best is not None else dim


def _dg(a, b, dims, exact):
    if exact:
        return lax.dot_general(a, b, dims, precision=_HI, preferred_element_type=f32)
    return lax.dot_general(a.astype(bf16), b.astype(bf16), dims, preferred_element_type=f32)


def _make_mm(batched, exact):
    o = 1 if batched else 0
    bd = ((0,), (0,)) if batched else ((), ())
    d_nn = (((1 + o,), (o,)), bd)
    d_nt = (((1 + o,), (1 + o,)), bd)
    d_tn = (((o,), (o,)), bd)

    @jax.custom_vjp
    def nn(a, b):
        return _dg(a, b, d_nn, exact)

    @jax.custom_vjp
    def nt(a, b):
        return _dg(a, b, d_nt, exact)

    @jax.custom_vjp
    def tn(a, b):
        return _dg(a, b, d_tn, exact)

    nn.defvjp(lambda a, b: (_dg(a, b, d_nn, exact), (a, b)),
              lambda res, g: (_dg(g, res[1], d_nt, exact), _dg(res[0], g, d_tn, exact)))
    nt.defvjp(lambda a, b: (_dg(a, b, d_nt, exact), (a, b)),
              lambda res, g: (_dg(g, res[1], d_nn, exact), _dg(g, res[0], d_tn, exact)))
    tn.defvjp(lambda a, b: (_dg(a, b, d_tn, exact), (a, b)),
              lambda res, g: (_dg(res[1], g, d_nt, exact), _dg(res[0], g, d_nn, exact)))
    return nn, nt, tn


def _sigmoid(x):
    return 1.0 / (1.0 + jnp.exp(-x))


def _head_sum_raw(x):
    width = 2 * HD
    i = lax.broadcasted_iota(jnp.int32, (width, width), 0) // HD
    j = lax.broadcasted_iota(jnp.int32, (width, width), 1) // HD
    m = (i == j).astype(bf16)
    dims = (((1,), (0,)), ((), ()))
    out = []
    for p in range(x.shape[1] // width):
        xp = x[:, p * width:(p + 1) * width]
        hi = xp.astype(bf16)
        lo = (xp - hi.astype(f32)).astype(bf16)
        out.append(lax.dot_general(hi, m, dims, preferred_element_type=f32)
                   + lax.dot_general(lo, m, dims, preferred_element_type=f32))
    return jnp.concatenate(out, axis=1)


@jax.custom_vjp
def _head_sum(x):
    return _head_sum_raw(x)


_head_sum.defvjp(lambda x: (_head_sum_raw(x), None), lambda _, g: (_head_sum_raw(g),))


WEIGHT_TILE_BYTES = 13 * 512 * 1024
ACC_TILE_BYTES = 8 * 1024 * 1024


def _matmul(name, a, b, mode, add=None, out_dtype=f32):
    has_add = add is not None
    if mode == "tn":
        (k, m), (_, n) = a.shape, b.shape
        tn = _tile(n, max(128, ACC_TILE_BYTES // (4 * m)))
        tk = _tile(k, 1024)

        def body(a_ref, b_ref, o_ref):
            @pl.when(pl.program_id(1) == 0)
            def _():
                o_ref[...] = jnp.zeros_like(o_ref)

            o_ref[...] += lax.dot_general(a_ref[...].astype(bf16), b_ref[...].astype(bf16),
                                          (((0,), (0,)), ((), ())), preferred_element_type=f32)

        return pl.pallas_call(
            body, name=name, grid=(n // tn, k // tk),
            in_specs=[pl.BlockSpec((tk, m), lambda j, kk: (kk, 0)), pl.BlockSpec((tk, tn), lambda j, kk: (kk, j))],
            out_specs=pl.BlockSpec((m, tn), lambda j, kk: (0, j)), out_shape=jax.ShapeDtypeStruct((m, n), f32),
            compiler_params=_cp(("parallel", "arbitrary")),
        )(a, b)

    (m, k) = a.shape
    n = b.shape[1] if mode == "nn" else b.shape[0]
    tm = _tile(m, 512)
    tn = _tile(n, max(128, WEIGHT_TILE_BYTES // (2 * k)))
    dims = (((1,), (0,)), ((), ())) if mode == "nn" else (((1,), (1,)), ((), ()))
    b_spec = pl.BlockSpec((k, tn), lambda j, i: (0, j)) if mode == "nn" else pl.BlockSpec((tn, k), lambda j, i: (j, 0))
    o_spec = pl.BlockSpec((tm, tn), lambda j, i: (i, j))

    def body(*refs):
        a_ref, b_ref = refs[0], refs[1]
        o_ref = refs[-1]
        r = lax.dot_general(a_ref[...].astype(bf16), b_ref[...].astype(bf16), dims, preferred_element_type=f32)
        if has_add:
            r = r + refs[2][...]
        o_ref[...] = r.astype(o_ref.dtype)

    return pl.pallas_call(
        body, name=name, grid=(n // tn, m // tm),
        in_specs=[pl.BlockSpec((tm, k), lambda j, i: (i, 0)), b_spec] + ([o_spec] if has_add else []),
        out_specs=o_spec, out_shape=jax.ShapeDtypeStruct((m, n), out_dtype),
        compiler_params=_cp(("parallel", "arbitrary")),
    )(*((a, b, add) if has_add else (a, b)))


def _pieces(ref, widths):
    out, off = [], 0
    for w in widths:
        out.append(ref[:, off:off + w].astype(f32))
        off += w
    return out


def _store_pieces(ref, widths, vals, add_ref=None):
    off = 0
    for w, v in zip(widths, vals):
        ref[:, off:off + w] = (v if add_ref is None else v + add_ref[:, off:off + w]).astype(ref.dtype)
        off += w


def _rows_fwd(name, fn, consts, rows, params, outs, n_sums=0, tm=256, dtypes=None):
    t = (consts + rows)[0][0].shape[0]
    tm = min(tm, t)
    ins = consts + rows
    n_in, n_p, n_o = len(ins), len(params), len(outs)
    dtypes = dtypes or [f32] * n_o

    def body(*refs):
        in_refs, p_refs = refs[:n_in], refs[n_in:n_in + n_p]
        o_refs, s_refs = refs[n_in + n_p:n_in + n_p + n_o], refs[n_in + n_p + n_o:]
        vals = []
        for r, (_, widths) in zip(in_refs, ins):
            vals += _pieces(r, widths)
        res = fn(*vals, *[p[...] for p in p_refs])
        pos = 0
        for r, widths in zip(o_refs, outs):
            _store_pieces(r, widths, res[pos:pos + len(widths)])
            pos += len(widths)

        @pl.when(pl.program_id(0) == 0)
        def _():
            for s in s_refs:
                s[...] = jnp.zeros_like(s)

        for s, v in zip(s_refs, res[pos:]):
            s[...] += v

    row_spec = lambda w: pl.BlockSpec((tm, w), lambda i: (i, 0))
    full = lambda p: pl.BlockSpec(p.shape, lambda i: (0,) * p.ndim)
    return pl.pallas_call(
        body, name=name, grid=(t // tm,),
        in_specs=[row_spec(a.shape[1]) for a, _ in ins] + [full(p) for p in params],
        out_specs=[row_spec(sum(w)) for w in outs] + [pl.BlockSpec((1, 1), lambda i: (0, 0))] * n_sums,
        out_shape=[jax.ShapeDtypeStruct((t, sum(w)), dt) for w, dt in zip(outs, dtypes)] + [jax.ShapeDtypeStruct((1, 1), f32)] * n_sums,
        compiler_params=_cp(("arbitrary",)),
    )(*[a for a, _ in ins], *params)


def _rows_bwd(name, fn, consts, rows, params, outs, cts, n_sums=0, add=None, tm=256, dtypes=None):
    t = (consts + rows)[0][0].shape[0]
    tm = min(tm, t)
    n_c, n_r, n_p, n_o = len(consts), len(rows), len(params), len(outs)
    has_add = add is not None
    dtypes = dtypes or [f32] * n_r

    def body(*refs):
        pos = 0
        c_refs = refs[pos:pos + n_c]; pos += n_c
        r_refs = refs[pos:pos + n_r]; pos += n_r
        p_refs = refs[pos:pos + n_p]; pos += n_p
        ct_refs = refs[pos:pos + n_o]; pos += n_o
        add_ref = refs[pos] if has_add else None
        pos += 1 if has_add else 0
        dr_refs = refs[pos:pos + n_r]; pos += n_r
        dp_refs = refs[pos:pos + n_p]
        cvals, rvals = [], []
        for r, (_, widths) in zip(c_refs, consts):
            cvals += _pieces(r, widths)
        for r, (_, widths) in zip(r_refs, rows):
            rvals += _pieces(r, widths)
        pvals = [p[...] for p in p_refs]
        ctv = []
        for r, widths in zip(ct_refs, outs):
            ctv += _pieces(r, widths)
        ctv += [jnp.ones((1, 1), f32)] * n_sums
        _, vjp = jax.vjp(lambda *rp: tuple(fn(*cvals, *rp)), *rvals, *pvals)
        g = vjp(tuple(ctv))
        pos = 0
        for idx, (r, (_, widths)) in enumerate(zip(dr_refs, rows)):
            _store_pieces(r, widths, g[pos:pos + len(widths)], add_ref if idx == 0 else None)
            pos += len(widths)

        @pl.when(pl.program_id(0) == 0)
        def _():
            for dp in dp_refs:
                dp[...] = jnp.zeros_like(dp)

        for dp, v in zip(dp_refs, g[pos:]):
            dp[...] += v

    row_spec = lambda w: pl.BlockSpec((tm, w), lambda i: (i, 0))
    full = lambda p: pl.BlockSpec(p.shape, lambda i: (0,) * p.ndim)
    args = [a for a, _ in consts + rows] + list(params) + list(cts) + ([add] if has_add else [])
    res = pl.pallas_call(
        body, name=name, grid=(t // tm,),
        in_specs=[row_spec(a.shape[1]) for a, _ in consts + rows] + [full(p) for p in params]
        + [row_spec(sum(w)) for w in outs] + ([row_spec(add.shape[1])] if has_add else []),
        out_specs=[row_spec(a.shape[1]) for a, _ in rows] + [full(p) for p in params],
        out_shape=[jax.ShapeDtypeStruct(a.shape, dt) for (a, _), dt in zip(rows, dtypes)]
        + [jax.ShapeDtypeStruct(p.shape, f32) for p in params],
        compiler_params=_cp(("arbitrary",)),
    )(*args)
    return res[:n_r], res[n_r:]


def _rms(x, g):
    return x * lax.rsqrt(jnp.mean(x * x, axis=-1, keepdims=True) + NORM_EPS) * g


def _fn_rms(x, g):
    return (_rms(x, g),)


def _fn_rwkv_pre(r, k, v, wd, ad, gd, w0, w_up, a0, a_up, g_up, k_k, k_a):
    nn, _, _ = _make_mm(False, False)
    w_log = -_sigmoid(w0 + nn(jnp.tanh(wd), w_up)) * 0.6065306597126334
    a = _sigmoid(a0 + nn(ad, a_up))
    g = nn(_sigmoid(gd), g_up)
    kk = k * k_k
    kk = kk * lax.rsqrt(jnp.maximum(_head_sum(kk * kk), 1e-24))
    k2 = k * (1.0 + (a - 1.0) * k_a)
    return r, w_log, k2, v, -kk, kk * a, g


def _fn_rwkv_post(y, r, k2, v, g, gn_g, gn_b, r_k):
    mean = _head_sum(y) * (1.0 / HD)
    yc = y - mean
    var = _head_sum(yc * yc) * (1.0 / HD)
    yn = yc * lax.rsqrt(var + GN_EPS) * gn_g + gn_b
    bonus = _head_sum(r * k2 * r_k) * v
    return ((yn + bonus) * g,)


def _fn_merge(a_fox, a_rwkv, a_mem, g_fox, g_rwkv, g_mem):
    return (_sigmoid(g_fox) * a_fox + _sigmoid(g_rwkv) * a_rwkv + _sigmoid(g_mem) * a_mem,)


def _fn_post1(y, x, post1_g, pre2_g):
    h1 = x + _rms(y, post1_g)
    return h1, _rms(h1, pre2_g)


def _fn_swiglu(gp, up):
    return (gp * _sigmoid(gp) * up,)


def _fn_final(target, ffn, h1, post2_g):
    err = h1 + _rms(ffn, post2_g) - target
    per_row = jnp.mean(err * err, axis=-1, keepdims=True)
    return (0.5 * jnp.sum(per_row, axis=0, keepdims=True),)


def _shift_down(x):
    row = lax.broadcasted_iota(jnp.int32, x.shape, 0)
    return jnp.where(row == 0, 0.0, pltpu.roll(x, 1, 0))


def _shift_up(x):
    s = x.shape[0]
    row = lax.broadcasted_iota(jnp.int32, x.shape, 0)
    return jnp.where(row == s - 1, 0.0, pltpu.roll(x, s - 1, 0))


def _tokshift_fwd(p, mu, batch, seq):
    w = p.shape[1]
    tc = _tile(w, 384)

    def body(p_ref, mu_ref, o_ref):
        x = p_ref[...]
        o_ref[...] = x + (_shift_down(x) - x) * mu_ref[...]

    return pl.pallas_call(
        body, name="tokshift_fwd", grid=(w // tc, batch),
        in_specs=[pl.BlockSpec((seq, tc), lambda j, b: (b, j)), pl.BlockSpec((1, tc), lambda j, b: (0, j))],
        out_specs=pl.BlockSpec((seq, tc), lambda j, b: (b, j)),
        out_shape=jax.ShapeDtypeStruct(p.shape, f32),
        compiler_params=_cp(("parallel", "arbitrary")),
    )(p, mu)


def _tokshift_bwd(p, mu, dps, batch, seq):
    w = p.shape[1]
    tc = _tile(w, 384)

    def body(p_ref, mu_ref, d_ref, dp_ref, dmu_ref):
        x, mu_v, d = p_ref[...], mu_ref[...], d_ref[...]
        dp_ref[...] = (d * (1.0 - mu_v) + _shift_up(d * mu_v)).astype(dp_ref.dtype)

        @pl.when(pl.program_id(1) == 0)
        def _():
            dmu_ref[...] = jnp.zeros_like(dmu_ref)

        dmu_ref[...] += jnp.sum(d * (_shift_down(x) - x), axis=0, keepdims=True)

    return pl.pallas_call(
        body, name="tokshift_bwd", grid=(w // tc, batch),
        in_specs=[pl.BlockSpec((seq, tc), lambda j, b: (b, j)), pl.BlockSpec((1, tc), lambda j, b: (0, j)),
                  pl.BlockSpec((seq, tc), lambda j, b: (b, j))],
        out_specs=[pl.BlockSpec((seq, tc), lambda j, b: (b, j)), pl.BlockSpec((1, tc), lambda j, b: (0, j))],
        out_shape=[jax.ShapeDtypeStruct(p.shape, bf16), jax.ShapeDtypeStruct(mu.shape, f32)],
        compiler_params=_cp(("parallel", "arbitrary")),
    )(p, mu, dps)


def _cum_block(seq):
    return _tile(seq, 256)


def _fox_gate_fwd(f, bias, batch, seq):
    cb = _cum_block(seq)

    def body(f_ref, b_ref, c_ref):
        row = lax.broadcasted_iota(jnp.int32, (cb, cb), 0)
        col = lax.broadcasted_iota(jnp.int32, (cb, cb), 1)
        tri = (col <= row).astype(f32)
        carry = jnp.zeros((1, 128), f32)
        for i in range(seq // cb):
            z = f_ref[i * cb:(i + 1) * cb, :] + b_ref[...]
            ls = jnp.minimum(z, 0.0) - jnp.log(1.0 + jnp.exp(-jnp.abs(z)))
            c = _dg(tri, ls, (((1,), (0,)), ((), ())), True) + carry
            c_ref[i * cb:(i + 1) * cb, :] = c
            carry = c[cb - 1:cb, :]

    return pl.pallas_call(
        body, name="fox_gate_fwd", grid=(batch,),
        in_specs=[pl.BlockSpec((seq, 128), lambda b: (b, 0)), pl.BlockSpec((1, 128), lambda b: (0, 0))],
        out_specs=pl.BlockSpec((seq, 128), lambda b: (b, 0)),
        out_shape=jax.ShapeDtypeStruct(f.shape, f32),
        compiler_params=_cp(("arbitrary",)),
    )(f, bias)


def _fox_gate_bwd(f, bias, dc_a, dc_b, batch, seq):
    cb = _cum_block(seq)

    def body(f_ref, b_ref, da_ref, db_ref, df_ref, dbias_ref):
        row = lax.broadcasted_iota(jnp.int32, (cb, cb), 0)
        col = lax.broadcasted_iota(jnp.int32, (cb, cb), 1)
        triu = (col >= row).astype(f32)

        @pl.when(pl.program_id(0) == 0)
        def _():
            dbias_ref[...] = jnp.zeros_like(dbias_ref)

        carry = jnp.zeros((1, 128), f32)
        tot = jnp.zeros((1, 128), f32)
        for i in reversed(range(seq // cb)):
            sl = slice(i * cb, (i + 1) * cb)
            dc = da_ref[sl, :] + db_ref[sl, :]
            dls = _dg(triu, dc, (((1,), (0,)), ((), ())), True) + carry
            carry = dls[0:1, :]
            df = dls * _sigmoid(-(f_ref[sl, :] + b_ref[...]))
            df_ref[sl, :] = df.astype(df_ref.dtype)
            tot = tot + jnp.sum(df, axis=0, keepdims=True)
        dbias_ref[...] += tot

    return pl.pallas_call(
        body, name="fox_gate_bwd", grid=(batch,),
        in_specs=[pl.BlockSpec((seq, 128), lambda b: (b, 0)), pl.BlockSpec((1, 128), lambda b: (0, 0)),
                  pl.BlockSpec((seq, 128), lambda b: (b, 0)), pl.BlockSpec((seq, 128), lambda b: (b, 0))],
        out_specs=[pl.BlockSpec((seq, 128), lambda b: (b, 0)), pl.BlockSpec((1, 128), lambda b: (0, 0))],
        out_shape=[jax.ShapeDtypeStruct(f.shape, bf16), jax.ShapeDtypeStruct((1, 128), f32)],
        compiler_params=_cp(("arbitrary",)),
    )(f, bias, dc_a, dc_b)


_HBM_SPEC = pl.BlockSpec(memory_space=pltpu.HBM)


def _side_out_shapes(srcs, per_peer):
    return [jax.ShapeDtypeStruct(((N_DEV,) + tuple(s.shape[1:] if per_peer else s.shape)), s.dtype) for s in srcs]


def _side_sems(n):
    if n == 0:
        return []
    return [pltpu.SemaphoreType.DMA((n, N_DEV - 1)), pltpu.SemaphoreType.DMA((n, N_DEV - 1)), pltpu.SemaphoreType.DMA((n,))]


def _peer_copies(src_refs, dst_refs, per_peer, sems):
    send_sems, recv_sems, local_sems = sems
    x, y, c = lax.axis_index("x"), lax.axis_index("y"), lax.axis_index("c")
    me = 4 * x + 2 * y + c
    copies = []
    for t, (s, d) in enumerate(zip(src_refs, dst_refs)):
        copies.append(pltpu.make_async_copy(s.at[me] if per_peer else s, d.at[me], local_sems.at[t]))
        for k in range(1, N_DEV):
            px = 1 - x if k & 4 else x
            py = 1 - y if k & 2 else y
            pc = 1 - c if k & 1 else c
            copies.append(pltpu.make_async_remote_copy(
                src_ref=s.at[4 * px + 2 * py + pc] if per_peer else s, dst_ref=d.at[me],
                send_sem=send_sems.at[t, k - 1], recv_sem=recv_sems.at[t, k - 1],
                device_id=(px, py, pc), device_id_type=pl.DeviceIdType.MESH))
    return copies


def _side_exchange(src_refs, dst_refs, per_peer, sems, *grid):
    if not src_refs:
        return
    first = functools.reduce(jnp.logical_and, [pl.program_id(a) == 0 for a in range(len(grid))])
    last = functools.reduce(jnp.logical_and, [pl.program_id(a) == n - 1 for a, n in enumerate(grid)])

    @pl.when(first)
    def _():
        for cp in _peer_copies(src_refs, dst_refs, per_peer, sems):
            cp.start()

    @pl.when(last)
    def _():
        for cp in _peer_copies(src_refs, dst_refs, per_peer, sems):
            cp.wait()


def _exchange(name, srcs, per_peer):
    n = len(srcs)

    def body(*refs):
        copies = _peer_copies(refs[:n], refs[n:2 * n], per_peer, refs[2 * n:])
        for cp in copies:
            cp.start()
        for cp in copies:
            cp.wait()

    return pl.pallas_call(
        body, name=name, in_specs=[_HBM_SPEC] * n, out_specs=[_HBM_SPEC] * n,
        out_shape=_side_out_shapes(srcs, per_peer), scratch_shapes=_side_sems(n),
    )(*srcs)


FOX_T = 256
_NEG = -1e30
_D2 = (((1,), (1,)), ((), ()))
_D1 = (((1,), (0,)), ((), ()))
_D0 = (((0,), (0,)), ((), ()))


def _bdot(a, b, dims):
    return lax.dot_general(a.astype(bf16), b.astype(bf16), dims, preferred_element_type=f32)


def _pick_lane(x, lane):
    idx = lax.broadcasted_iota(jnp.int32, x.shape, 1)
    return jnp.sum(jnp.where(idx == lane, x, 0.0), axis=1, keepdims=True)


def _pick_row(x, row):
    idx = lax.broadcasted_iota(jnp.int32, x.shape, 0)
    return jnp.sum(jnp.where(idx == row, x, 0.0), axis=0, keepdims=True)


def _fox_fwd(qkv, c, c_rows, batch, seq, side=None):
    t = min(FOX_T, seq)
    nq = seq // t
    scale = HD ** -0.5
    srcs, per_peer = side if side is not None else ([], False)
    n_s = len(srcs)

    def body(*refs):
        q_ref, k_ref, v_ref, cq_ref, ck_ref = refs[:5]
        o_ref, lse_ref = refs[5 + n_s:7 + n_s]
        _side_exchange(refs[5:5 + n_s], refs[7 + n_s:7 + 2 * n_s], per_peer, refs[7 + 2 * n_s:], batch, PAIRS, nq)
        pair, i = pl.program_id(1), pl.program_id(2)
        lane = lax.broadcasted_iota(jnp.int32, (1, PAIR_W), 1)
        first = (lane // HD) == 0
        q = q_ref[...] * scale
        qs = [jnp.where(first, q, 0.0), jnp.where(first, 0.0, q)]
        cqs = [_pick_lane(cq_ref[...], 2 * pair + e) for e in range(2)]
        qidx = i * t + lax.broadcasted_iota(jnp.int32, (t, t), 0)

        def step(j, carry):
            rows = pl.ds(pl.multiple_of(j * t, t), t)
            kj, vj = k_ref[rows, :], v_ref[rows, :]
            ck_blk = ck_ref[0, :, rows]
            vis = (j * t + lax.broadcasted_iota(jnp.int32, (t, t), 1)) <= qidx
            out = []
            for e in range(2):
                m, l, acc = carry[3 * e:3 * e + 3]
                s = _bdot(qs[e], kj, _D2) + (cqs[e] - _pick_row(ck_blk, 2 * pair + e))
                s = jnp.where(vis, s, _NEG)
                m_new = jnp.maximum(m, jnp.max(s, axis=1, keepdims=True))
                alpha = jnp.exp(m - m_new)
                p = jnp.exp(s - m_new)
                out += [m_new, alpha * l + jnp.sum(p, axis=1, keepdims=True), alpha * acc + _bdot(p, vj, _D1)]
            return tuple(out)

        init = (jnp.full((t, 1), _NEG, f32), jnp.zeros((t, 1), f32), jnp.zeros((t, PAIR_W), f32)) * 2
        m0, l0, a0, m1, l1, a1 = lax.fori_loop(0, i + 1, step, init)
        o_ref[...] = jnp.where(first, a0 / l0, a1 / l1)
        lse_ref[...] = jnp.where(lane == 0, m0 + jnp.log(l0), jnp.where(lane == 1, m1 + jnp.log(l1), 0.0))

    q_spec = pl.BlockSpec((t, PAIR_W), lambda b, p, i: (b * nq + i, p))
    res = pl.pallas_call(
        body, name="fox_attn_fwd", grid=(batch, PAIRS, nq),
        in_specs=[q_spec,
                  pl.BlockSpec((seq, PAIR_W), lambda b, p, i: (b, PAIRS + p)),
                  pl.BlockSpec((seq, PAIR_W), lambda b, p, i: (b, 2 * PAIRS + p)),
                  pl.BlockSpec((t, 128), lambda b, p, i: (b * nq + i, 0)),
                  pl.BlockSpec((1, 8, seq), lambda b, p, i: (b, 0, 0))] + [_HBM_SPEC] * n_s,
        out_specs=[q_spec, q_spec] + [_HBM_SPEC] * n_s,
        out_shape=[jax.ShapeDtypeStruct((batch * seq, HW), f32)] * 2 + _side_out_shapes(srcs, per_peer),
        scratch_shapes=_side_sems(n_s),
        compiler_params=_cp(("arbitrary", "arbitrary", "arbitrary")),
    )(qkv, qkv, qkv, c, c_rows, *srcs)
    return res[0], res[1], list(res[2:])


def _fox_bwd(qkv, c, c_rows, o, lse, do, batch, seq, side=None):
    t = min(FOX_T, seq)
    nq = seq // t
    scale = HD ** -0.5
    srcs, per_peer = side if side is not None else ([], False)
    n_s = len(srcs)

    def body(*refs):
        q_ref, k_ref, v_ref, cq_ref, ck_ref, o_ref, lse_ref, do_ref = refs[:8]
        dq_ref, dk_ref, dv_ref, dcq_ref, dck_ref = refs[8 + n_s:13 + n_s]
        _side_exchange(refs[8:8 + n_s], refs[13 + n_s:13 + 2 * n_s], per_peer, refs[13 + 2 * n_s:], batch, PAIRS, nq)
        pair, j = pl.program_id(1), pl.program_id(2)

        @pl.when(j == 0)
        def _():
            dq_ref[...] = jnp.zeros_like(dq_ref)
            dcq_ref[...] = jnp.zeros_like(dcq_ref)

        lane = lax.broadcasted_iota(jnp.int32, (1, PAIR_W), 1)
        first = (lane // HD) == 0
        sub = lax.broadcasted_iota(jnp.int32, (8, t), 0)
        kj, vj = k_ref[...], v_ref[...]
        ks = [jnp.where(first, kj, 0.0), jnp.where(first, 0.0, kj)]
        cks = [_pick_row(ck_ref[0], 2 * pair + e) for e in range(2)]
        kidx = j * t + lax.broadcasted_iota(jnp.int32, (t, t), 1)
        ones8 = jnp.ones((8, t), f32)

        def step(i, carry):
            dk, dv, dck0, dck1 = carry
            rows = pl.ds(pl.multiple_of(i * t, t), t)
            q = q_ref[rows, :] * scale
            d_o, o_i, lse_i, cq_i = do_ref[rows, :], o_ref[rows, :], lse_ref[rows, :], cq_ref[rows, :]
            vis = kidx <= (i * t + lax.broadcasted_iota(jnp.int32, (t, t), 0))
            dq_acc = jnp.zeros((t, PAIR_W), f32)
            dcq_acc = jnp.zeros((8, t), f32)
            dcks = [dck0, dck1]
            for e in range(2):
                mine = first if e == 0 else jnp.logical_not(first)
                qe, doe = jnp.where(mine, q, 0.0), jnp.where(mine, d_o, 0.0)
                s = _bdot(qe, kj, _D2) + (_pick_lane(cq_i, 2 * pair + e) - cks[e])
                p = jnp.exp(jnp.where(vis, s, _NEG) - _pick_lane(lse_i, e))
                dv = dv + _bdot(p, doe, _D0)
                delta = jnp.sum(doe * o_i, axis=1, keepdims=True)
                ds = p * (_bdot(doe, vj, _D2) - delta)
                dk = dk + _bdot(ds, qe, _D0)
                dq_acc = dq_acc + _bdot(ds, ks[e], _D1)
                row_sums = lax.dot_general(ones8, ds, _D2, precision=_HI, preferred_element_type=f32)
                dcq_acc = dcq_acc + jnp.where(sub == e, row_sums, 0.0)
                dcks[e] = dcks[e] - jnp.sum(ds, axis=0, keepdims=True)
            dq_ref[rows, :] += dq_acc * scale
            dcq_ref[0, :, rows] += dcq_acc
            return dk, dv, dcks[0], dcks[1]

        zero = jnp.zeros((t, PAIR_W), f32)
        dk, dv, dck0, dck1 = lax.fori_loop(j, nq, step, (zero, zero, jnp.zeros((1, t), f32), jnp.zeros((1, t), f32)))
        dk_ref[...] = dk
        dv_ref[...] = dv
        dck_ref[0] = jnp.where(sub == 0, dck0, jnp.where(sub == 1, dck1, 0.0))

    whole = lambda col: pl.BlockSpec((seq, PAIR_W), lambda b, p, j: (b, col * PAIRS + p))
    blk = lambda col: pl.BlockSpec((t, PAIR_W), lambda b, p, j: (b * nq + j, col * PAIRS + p))
    rows_whole = pl.BlockSpec((1, 8, seq), lambda b, p, j: (b * PAIRS + p, 0, 0))
    rows_blk = pl.BlockSpec((1, 8, t), lambda b, p, j: (b * PAIRS + p, 0, j))
    t_all = batch * seq
    res = pl.pallas_call(
        body, name="fox_attn_bwd", grid=(batch, PAIRS, nq),
        in_specs=[whole(0), blk(1), blk(2),
                  pl.BlockSpec((seq, 128), lambda b, p, j: (b, 0)),
                  pl.BlockSpec((1, 8, t), lambda b, p, j: (b, 0, j)),
                  whole(0), whole(0), whole(0)] + [_HBM_SPEC] * n_s,
        out_specs=[whole(0), blk(0), blk(0), rows_whole, rows_blk] + [_HBM_SPEC] * n_s,
        out_shape=[jax.ShapeDtypeStruct((t_all, HW), f32)] * 3
        + [jax.ShapeDtypeStruct((batch * PAIRS, 8, seq), f32)] * 2 + _side_out_shapes(srcs, per_peer),
        scratch_shapes=_side_sems(n_s),
        compiler_params=_cp(("arbitrary", "arbitrary", "arbitrary")),
    )(qkv, qkv, qkv, c, c_rows, o, lse, do, *srcs)
    return res[:5], list(res[5:])


def _mem_block(q, km, vm):
    nn, nt, _ = _make_mm(False, False)
    logits = nt(q, km) * (MEM_HD ** -0.5)
    m = lax.stop_gradient(jnp.max(logits, axis=-1, keepdims=True))
    e = jnp.exp(logits - m)
    return nn(e / jnp.sum(e, axis=-1, keepdims=True), vm)


def _mem_specs(seq, tq):
    nq = seq // tq
    qs = pl.BlockSpec((tq, MEM_HD), lambda b, h, i: (b * nq + i, h))
    ks = pl.BlockSpec((MEM_LEN, MEM_HD), lambda b, h, i: (b, h))
    vs = pl.BlockSpec((MEM_LEN, MEM_HD), lambda b, h, i: (b, MEM_HEADS + h))
    return nq, qs, ks, vs


def _mem_fwd(q, mem_kv, batch, seq):
    tq = min(512, seq)
    nq, qs, ks, vs = _mem_specs(seq, tq)

    def body(q_ref, k_ref, v_ref, o_ref):
        o_ref[...] = _mem_block(q_ref[...], k_ref[...], v_ref[...]).astype(o_ref.dtype)

    return pl.pallas_call(
        body, name="mem_attn_fwd", grid=(batch, MEM_HEADS, nq),
        in_specs=[qs, ks, vs], out_specs=qs, out_shape=jax.ShapeDtypeStruct(q.shape, bf16),
        compiler_params=_cp(("parallel", "parallel", "arbitrary")),
    )(q, mem_kv, mem_kv)


def _mem_bwd(q, mem_kv, do, batch, seq):
    tq = min(512, seq)
    nq, qs, ks, vs = _mem_specs(seq, tq)

    def body(q_ref, k_ref, v_ref, do_ref, dq_ref, dk_ref, dv_ref):
        _, vjp = jax.vjp(_mem_block, q_ref[...], k_ref[...], v_ref[...])
        dq, dk, dv = vjp(do_ref[...])
        dq_ref[...] = dq.astype(dq_ref.dtype)

        @pl.when(pl.program_id(2) == 0)
        def _():
            dk_ref[...] = jnp.zeros_like(dk_ref)
            dv_ref[...] = jnp.zeros_like(dv_ref)

        dk_ref[...] += dk
        dv_ref[...] += dv

    return pl.pallas_call(
        body, name="mem_attn_bwd", grid=(batch, MEM_HEADS, nq),
        in_specs=[qs, ks, vs, qs], out_specs=[qs, ks, ks],
        out_shape=[jax.ShapeDtypeStruct(q.shape, bf16), jax.ShapeDtypeStruct((batch * MEM_LEN, MEM_W), f32),
                   jax.ShapeDtypeStruct((batch * MEM_LEN, MEM_W), f32)],
        compiler_params=_cp(("parallel", "parallel", "arbitrary")),
    )(q, mem_kv, mem_kv, do)


@jax.custom_vjp
def _halves(x):
    c = x.shape[1] // 2
    return x[:, :c], x[:, c:]


_halves.defvjp(lambda x: ((x[:, :x.shape[1] // 2], x[:, x.shape[1] // 2:]), None),
               lambda _, g: (jnp.concatenate(g, axis=1),))


@jax.custom_vjp
def _lead_halves(x):
    n = x.shape[0] // 2
    return x[:n], x[n:]


_lead_halves.defvjp(lambda x: ((x[:x.shape[0] // 2], x[x.shape[0] // 2:]), None),
                    lambda _, g: (jnp.concatenate(g, axis=0),))


def _scan_chunk(s0, r, wl, k, v, a, b):
    nn, nt, tn = _make_mm(True, False)
    nn_exact, nt_exact, _ = _make_mm(True, True)
    hp, c, lanes = r.shape
    row = lax.broadcasted_iota(jnp.int32, (c, c), 0)
    col = lax.broadcasted_iota(jnp.int32, (c, c), 1)
    first = (lax.broadcasted_iota(jnp.int32, (1, 1, lanes), 2) // HD) == 0
    tri = jnp.broadcast_to((col <= row).astype(f32)[None], (hp, c, c))
    lg = nn_exact(tri, wl)
    lg_end = lg[:, c - 1:c, :]
    grow, shrink, to_end = jnp.exp(lg), jnp.exp(-lg), jnp.exp(lg_end - lg)
    rt, kt, bt, at = r * grow, k * shrink, b * shrink, a * jnp.exp(lg - wl)
    strict, incl = (col < row)[None], (col <= row)[None]
    twice = lambda t: jnp.concatenate([t, t], axis=0)
    queries = jnp.concatenate([at, rt], axis=1)
    per_head = jnp.concatenate([jnp.where(first, queries, 0.0), jnp.where(first, 0.0, queries)], axis=0)
    (ab, rb), (ak, rk) = _halves(nt_exact(per_head, twice(bt))), _halves(nt_exact(per_head, twice(kt)))
    l_ab = jnp.where(strict, ab, 0.0)
    a_ak = jnp.where(strict, ak, 0.0)
    a_rb = jnp.where(incl, rb, 0.0)
    a_rk = jnp.where(incl, rk, 0.0)
    inv = (col == row).astype(f32)[None] + l_ab
    power, n = l_ab, 1
    while 2 * n < c:
        power = nn(power, power)
        inv = inv + nn(inv, power)
        n *= 2

    def apply(m, t):
        lo, hi = _lead_halves(nn(m, twice(t)))
        return jnp.where(first, lo, hi)

    sa = apply(inv, nt(at, s0) + apply(a_ak, v))
    y = nt(rt, s0) + apply(a_rk, v) + apply(a_rb, sa)
    same_head = ((lax.broadcasted_iota(jnp.int32, (lanes, lanes), 0) // HD)
                 == (lax.broadcasted_iota(jnp.int32, (lanes, lanes), 1) // HD))[None]
    s1 = s0 * jnp.exp(lg_end) + jnp.where(same_head, tn(v, k * to_end) + tn(sa, b * to_end), 0.0)
    return y, s1


PAIRS = HEADS // 2
PAIR_W = 2 * HD


def _pair_stack(ref, comp):
    return jnp.stack([ref[:, comp * HW + p * PAIR_W:comp * HW + (p + 1) * PAIR_W] for p in range(PAIRS)])


def _scan_fwd(main6, batch, seq, side=None):
    c = min(SCAN_CHUNK, seq)
    nc = seq // c
    srcs, per_peer = side if side is not None else ([], False)
    n_s = len(srcs)

    def body(*refs):
        z_ref, y_ref, s_ref, st = refs[0], refs[1 + n_s], refs[2 + n_s], refs[3 + 2 * n_s]
        _side_exchange(refs[1:1 + n_s], refs[3 + n_s:3 + 2 * n_s], per_peer, refs[4 + 2 * n_s:], batch, nc)

        @pl.when(pl.program_id(1) == 0)
        def _():
            st[...] = jnp.zeros_like(st)

        s0 = st[...]
        s_ref[0, 0] = s0
        y, s1 = _scan_chunk(s0, *[_pair_stack(z_ref, comp) for comp in range(6)])
        for p in range(PAIRS):
            y_ref[:, p * PAIR_W:(p + 1) * PAIR_W] = y[p]
        st[...] = s1

    res = pl.pallas_call(
        body, name="rwkv_scan_fwd", grid=(batch, nc),
        in_specs=[pl.BlockSpec((c, 6 * HW), lambda b, i: (b * nc + i, 0))] + [_HBM_SPEC] * n_s,
        out_specs=[pl.BlockSpec((c, HW), lambda b, i: (b * nc + i, 0)),
                   pl.BlockSpec((1, 1, PAIRS, PAIR_W, PAIR_W), lambda b, i: (b, i, 0, 0, 0))] + [_HBM_SPEC] * n_s,
        out_shape=[jax.ShapeDtypeStruct((batch * seq, HW), f32), jax.ShapeDtypeStruct((batch, nc, PAIRS, PAIR_W, PAIR_W), f32)]
        + _side_out_shapes(srcs, per_peer),
        scratch_shapes=[pltpu.VMEM((PAIRS, PAIR_W, PAIR_W), f32)] + _side_sems(n_s),
        compiler_params=_cp(("arbitrary", "arbitrary")),
    )(main6, *srcs)
    return res[0], res[1], list(res[2:])


def _scan_bwd(main6, states, dy, extra, batch, seq, side=None):
    c = min(SCAN_CHUNK, seq)
    nc = seq // c
    srcs, per_peer = side if side is not None else ([], False)
    n_s = len(srcs)

    def body(*refs):
        z_ref, s_ref, dy_ref, ex_ref = refs[:4]
        dz_ref, dst = refs[4 + n_s], refs[5 + 2 * n_s]
        _side_exchange(refs[4:4 + n_s], refs[5 + n_s:5 + 2 * n_s], per_peer, refs[6 + 2 * n_s:], batch, nc)

        @pl.when(pl.program_id(1) == 0)
        def _():
            dst[...] = jnp.zeros_like(dst)

        dy_pairs = jnp.stack([dy_ref[:, p * PAIR_W:(p + 1) * PAIR_W] for p in range(PAIRS)])
        _, vjp = jax.vjp(_scan_chunk, s_ref[0, 0], *[_pair_stack(z_ref, comp) for comp in range(6)])
        g = vjp((dy_pairs, dst[...]))
        dst[...] = g[0]
        for comp in range(6):
            for p in range(PAIRS):
                sl = slice(comp * HW + p * PAIR_W, comp * HW + (p + 1) * PAIR_W)
                dz_ref[:, sl] = g[1 + comp][p] + ex_ref[:, sl]

    back = lambda b, i: (b * nc + nc - 1 - i, 0)
    res = pl.pallas_call(
        body, name="rwkv_scan_bwd", grid=(batch, nc),
        in_specs=[pl.BlockSpec((c, 6 * HW), back),
                  pl.BlockSpec((1, 1, PAIRS, PAIR_W, PAIR_W), lambda b, i: (b, nc - 1 - i, 0, 0, 0)),
                  pl.BlockSpec((c, HW), back), pl.BlockSpec((c, 6 * HW), back)] + [_HBM_SPEC] * n_s,
        out_specs=[pl.BlockSpec((c, 6 * HW), back)] + [_HBM_SPEC] * n_s,
        out_shape=[jax.ShapeDtypeStruct(main6.shape, f32)] + _side_out_shapes(srcs, per_peer),
        scratch_shapes=[pltpu.VMEM((PAIRS, PAIR_W, PAIR_W), f32)] + _side_sems(n_s),
        compiler_params=_cp(("arbitrary", "arbitrary")),
    )(main6, states, dy, extra, *srcs)
    return res[0], list(res[1:])


def _to_heads(x, batch, seq, k):
    return x.reshape(batch, seq, k, HEADS, HD).transpose(2, 0, 3, 1, 4).reshape(k, batch * HEADS, seq, HD)


def _from_heads(x, batch, seq, k):
    return x.reshape(k, batch, HEADS, seq, HD).transpose(1, 3, 0, 2, 4).reshape(batch * seq, k * HW)


def _pad_cols(x, width):
    return jnp.pad(x, ((0, 0), (0, width - x.shape[1])))


def _split_w_in(w):
    z64 = jnp.zeros((w.shape[0], 64), w.dtype)
    w_r = jnp.concatenate([w[:, 1544:3080], w[:, 3080:3144], z64, w[:, 3144:3208], z64, w[:, 3208:3336]], axis=1)
    return w[:, :1536], _pad_cols(w[:, 1536:1544], 128), w_r, w[:, 3336:3848], w[:, 3848:]


def _merge_w_in(g_qkv, g_f, g_r, g_mq, g_g):
    return jnp.concatenate([g_qkv, g_f[:, :8], g_r[:, :1536], g_r[:, 1536:1600], g_r[:, 1664:1728], g_r[:, 1792:],
                            g_mq, g_g], axis=1)


def _pad_lora(v):
    z64 = jnp.zeros((1, 64), v.dtype)
    return jnp.concatenate([v[:, :1536], v[:, 1536:1600], z64, v[:, 1600:1664], z64, v[:, 1664:]], axis=1)


def _unpad_lora(v):
    return jnp.concatenate([v[:, :1536], v[:, 1536:1600], v[:, 1664:1728], v[:, 1792:]], axis=1)


def _local_step(x, mem, target, w, p, late=None, early=None):
    batch, seq, _ = x.shape
    t = batch * seq
    x2, tg2, mem2 = x.reshape(t, D), target.reshape(t, D), mem.reshape(batch * MEM_LEN, D)
    w_qkv, w_f, w_r, w_mq, w_g3 = _split_w_in(w["w_in"])
    mu = _pad_lora(p["rwkv_mu"])
    bias = _pad_cols(p["fox_f_bias"], 128)
    r_k = p["rwkv_r_k"].reshape(1, HW)
    post_params = [p["rwkv_gn_g"], p["rwkv_gn_b"], r_k]
    rw_widths = [HW, HW, HW, LORA_PAD, LORA_PAD, LORA_PAD]
    six = [HW] * 6

    (u,) = _rows_fwd("rms_pre1", _fn_rms, [], [(x2, [D])], [p["pre1_g"]], [[D]], dtypes=[bf16])
    p_qkv = _matmul("proj_qkv", u, w_qkv, "nn")
    p_f = _matmul("proj_f", u, w_f, "nn")
    p_r = _matmul("proj_rwkv", u, w_r, "nn")
    p_mq = _matmul("proj_memq", u, w_mq, "nn")
    p_g = _matmul("proj_gate", u, w_g3, "nn")

    c = _fox_gate_fwd(p_f, bias, batch, seq)
    c_rows = c[:, :HEADS].reshape(batch, seq, HEADS).transpose(0, 2, 1)
    fox_o, lse, gathered = _fox_fwd(p_qkv, c, c_rows, batch, seq, side=(late[0], False) if late else None)
    if late:
        w = {**w, **late[2](gathered, 0)}
    fox_out = fox_o.astype(bf16)

    w_up = jnp.pad(w["rwkv_w_up"].astype(f32), ((0, LORA_PAD - 64), (0, 0)))
    a_up = jnp.pad(w["rwkv_a_up"].astype(f32), ((0, LORA_PAD - 64), (0, 0)))
    pre_params = [p["rwkv_w0"], w_up, p["rwkv_a0"], a_up, w["rwkv_g_up"].astype(f32), p["rwkv_k_k"], p["rwkv_k_a"]]
    ps = _tokshift_fwd(p_r, mu, batch, seq)
    main6, g_rw = _rows_fwd("rwkv_pre", _fn_rwkv_pre, [], [(ps, rw_widths)], pre_params, [six, [HW]])
    y_rw, states, gathered = _scan_fwd(main6, batch, seq, side=(late[1], False) if late else None)
    if late:
        w = {**w, **late[2](gathered, 1)}
    post_consts = []
    post_rows = [(y_rw, [HW]), (main6, six), (g_rw, [HW])]

    def fn_post(y, r, _wl, k2, v, _a, _b, g, gn_g, gn_b, rk):
        return _fn_rwkv_post(y, r, k2, v, g, gn_g, gn_b, rk)

    (rwkv_out,) = _rows_fwd("rwkv_post", fn_post, post_consts, post_rows, post_params, [[HW]], dtypes=[bf16])

    (memn,) = _rows_fwd("rms_mem", _fn_rms, [], [(mem2, [D])], [p["mem_norm_g"]], [[D]], dtypes=[bf16])
    mem_kv = _matmul("proj_memkv", memn, w["w_mem_kv"], "nn")
    mem_out = _mem_fwd(p_mq, mem_kv, batch, seq)

    a_fox = _matmul("out_fox", fox_out, w["w_fox_out"], "nn")
    a_rwkv = _matmul("out_rwkv", rwkv_out, w["w_rwkv_out"], "nn")
    a_mem = _matmul("out_mem", mem_out, w["w_mem_out"], "nn")
    merge_rows = [(a_fox, [D]), (a_rwkv, [D]), (a_mem, [D]), (p_g, [D, D, D])]
    (merged,) = _rows_fwd("merge", _fn_merge, [], merge_rows, [], [[D]], dtypes=[bf16])
    yy = _matmul("out_o", merged, w["w_o"], "nn")
    post1_rows = [(yy, [D]), (x2, [D])]
    post1_params = [p["post1_g"], p["pre2_g"]]
    h1, u2 = _rows_fwd("post1", _fn_post1, [], post1_rows, post1_params, [[D], [D]], dtypes=[f32, bf16])
    gp = _matmul("ffn_gate", u2, w["w_ffn_gate"], "nn")
    up = _matmul("ffn_up", u2, w["w_ffn_up"], "nn")
    (hmid,) = _rows_fwd("swiglu", _fn_swiglu, [], [(gp, [D_FF]), (up, [D_FF])], [], [[D_FF]], dtypes=[bf16])
    ffn = _matmul("ffn_down", hmid, w["w_ffn_down"], "nn")
    final_rows = [(ffn, [D]), (h1, [D])]
    (loss,) = _rows_fwd("final", _fn_final, [(tg2, [D])], final_rows, [p["post2_g"]], [], n_sums=1)

    gw, gp_ = {}, {}
    (d_ffn, d_h1), (gp_["post2_g"],) = _rows_bwd("final_bwd", _fn_final, [(tg2, [D])], final_rows, [p["post2_g"]], [], [],
                                                  n_sums=1, dtypes=[bf16, f32])
    d_hmid = _matmul("ffn_down_dx", d_ffn, w["w_ffn_down"], "nt")
    gw["w_ffn_down"] = _matmul("ffn_down_dw", hmid, d_ffn, "tn")
    (d_gp, d_up), _ = _rows_bwd("swiglu_bwd", _fn_swiglu, [], [(gp, [D_FF]), (up, [D_FF])], [], [[D_FF]], [d_hmid],
                                dtypes=[bf16, bf16])
    d_u2 = _matmul("ffn_gate_dx", d_gp, w["w_ffn_gate"], "nt")
    d_u2 = _matmul("ffn_up_dx", d_up, w["w_ffn_up"], "nt", add=d_u2)
    gw["w_ffn_gate"] = _matmul("ffn_gate_dw", u2, d_gp, "tn")
    gw["w_ffn_up"] = _matmul("ffn_up_dw", u2, d_up, "tn")
    (d_yy, d_x_res), (gp_["post1_g"], gp_["pre2_g"]) = _rows_bwd(
        "post1_bwd", _fn_post1, [], post1_rows, post1_params, [[D], [D]], [d_h1, d_u2], dtypes=[bf16, f32])
    d_merged = _matmul("out_o_dx", d_yy, w["w_o"], "nt")
    gw["w_o"] = _matmul("out_o_dw", merged, d_yy, "tn")
    (d_a_fox, d_a_rwkv, d_a_mem, d_p_g), _ = _rows_bwd("merge_bwd", _fn_merge, [], merge_rows, [], [[D]], [d_merged],
                                                       dtypes=[bf16] * 4)
    d_fox_out = _matmul("out_fox_dx", d_a_fox, w["w_fox_out"], "nt")
    gw["w_fox_out"] = _matmul("out_fox_dw", fox_out, d_a_fox, "tn")
    d_rwkv_out = _matmul("out_rwkv_dx", d_a_rwkv, w["w_rwkv_out"], "nt")
    gw["w_rwkv_out"] = _matmul("out_rwkv_dw", rwkv_out, d_a_rwkv, "tn")
    d_mem_out = _matmul("out_mem_dx", d_a_mem, w["w_mem_out"], "nt")
    gw["w_mem_out"] = _matmul("out_mem_dw", mem_out, d_a_mem, "tn")

    d_p_mq, d_km, d_vm = _mem_bwd(p_mq, mem_kv, d_mem_out, batch, seq)
    d_mem_kv = jnp.concatenate([d_km, d_vm], axis=1).astype(bf16)
    gw["w_mem_kv"] = _matmul("proj_memkv_dw", memn, d_mem_kv, "tn")
    d_memn = _matmul("proj_memkv_dx", d_mem_kv, w["w_mem_kv"], "nt")
    _, (gp_["mem_norm_g"],) = _rows_bwd("rms_mem_bwd", _fn_rms, [], [(mem2, [D])], [p["mem_norm_g"]], [[D]], [d_memn])

    (d_q, d_k, d_v, d_cq, d_ck), _ = _fox_bwd(p_qkv, c, c_rows, fox_o, lse, d_fox_out, batch, seq)
    d_p_qkv = jnp.concatenate([d_q, d_k, d_v], axis=1).astype(bf16)

    def c_layout(dc):
        return _pad_cols(dc[:, :2].reshape(batch, HEADS, seq).transpose(0, 2, 1).reshape(t, HEADS), 128)

    d_p_f, d_bias = _fox_gate_bwd(p_f, bias, c_layout(d_cq), c_layout(d_ck), batch, seq)
    gp_["fox_f_bias"] = d_bias[:, :HEADS]

    (d_y_rw, d_main6_post, d_g_rw), (gp_["rwkv_gn_g"], gp_["rwkv_gn_b"], d_rk) = _rows_bwd(
        "rwkv_post_bwd", fn_post, post_consts, post_rows, post_params, [[HW]], [d_rwkv_out])
    gp_["rwkv_r_k"] = d_rk.reshape(1, HEADS, HD)
    d_main6, early_got = _scan_bwd(main6, states, d_y_rw, d_main6_post, batch, seq,
                                   side=(early(gw), True) if early else None)

    def fn_pre_sum(*args):
        return _fn_rwkv_pre(*args)

    (d_ps,), d_pre = _rows_bwd("rwkv_pre_bwd", fn_pre_sum, [], [(ps, rw_widths)], pre_params, [six, [HW]],
                               [d_main6, d_g_rw])
    gp_["rwkv_w0"], d_w_up, gp_["rwkv_a0"], d_a_up, gw["rwkv_g_up"], gp_["rwkv_k_k"], gp_["rwkv_k_a"] = d_pre
    gw["rwkv_w_up"], gw["rwkv_a_up"] = d_w_up[:64], d_a_up[:64]
    d_p_r, d_mu = _tokshift_bwd(p_r, mu, d_ps, batch, seq)
    gp_["rwkv_mu"] = _unpad_lora(d_mu)

    d_u = _matmul("proj_qkv_dx", d_p_qkv, w_qkv, "nt")
    d_u = _matmul("proj_f_dx", d_p_f, w_f, "nt", add=d_u)
    d_u = _matmul("proj_rwkv_dx", d_p_r, w_r, "nt", add=d_u)
    d_u = _matmul("proj_memq_dx", d_p_mq, w_mq, "nt", add=d_u)
    d_u = _matmul("proj_gate_dx", d_p_g, w_g3, "nt", add=d_u)
    gw["w_in"] = _merge_w_in(_matmul("proj_qkv_dw", u, d_p_qkv, "tn"), _matmul("proj_f_dw", u, d_p_f, "tn"),
                             _matmul("proj_rwkv_dw", u, d_p_r, "tn"), _matmul("proj_memq_dw", u, d_p_mq, "tn"),
                             _matmul("proj_gate_dw", u, d_p_g, "tn"))
    (d_x,), (gp_["pre1_g"],) = _rows_bwd("rms_pre1_bwd", _fn_rms, [], [(x2, [D])], [p["pre1_g"]], [[D]], [d_u], add=d_x_res)
    return loss, d_x.reshape(x.shape), gw, gp_, early_got


def _rows_add(name, a, b):
    (s,) = _rows_fwd(name, lambda u, v: (u + v,), [], [(a, [a.shape[1]]), (b, [b.shape[1]])], [], [[a.shape[1]]])
    return s


def _adamw(name, recv, w, m, v):
    rows, cols = w.shape
    tr = max(t for t in range(16, min(rows, 128) + 1, 16) if rows % t == 0)

    def body(g_ref, w_ref, m_ref, v_ref, go_ref, d_ref, mo_ref, vo_ref):
        g = g_ref[0].astype(f32)
        for s in range(1, N_DEV):
            g = g + g_ref[s].astype(f32)
        m_new = ADAM_B1 * m_ref[...] + (1.0 - ADAM_B1) * g
        v_new = ADAM_B2 * v_ref[...] + (1.0 - ADAM_B2) * (g * g)
        m_hat = m_new / (1.0 - ADAM_B1 ** ADAM_STEP)
        v_hat = v_new / (1.0 - ADAM_B2 ** ADAM_STEP)
        go_ref[...] = g
        d_ref[...] = -ADAM_LR * (m_hat / (jnp.sqrt(v_hat) + ADAM_EPS) + ADAM_WD * w_ref[...])
        mo_ref[...] = m_new
        vo_ref[...] = v_new

    spec = pl.BlockSpec((tr, cols), lambda i: (i, 0))
    return pl.pallas_call(
        body, name=name, grid=(rows // tr,),
        in_specs=[pl.BlockSpec((N_DEV, tr, cols), lambda i: (0, i, 0)), spec, spec, spec],
        out_specs=[spec] * 4, out_shape=[jax.ShapeDtypeStruct(w.shape, f32)] * 4,
        compiler_params=_cp(("parallel",)),
    )(recv, w, m, v)


GROUPS = (
    ("in", ("w_in",), 1),
    ("memkv", ("w_mem_kv",), 0),
    ("ffn_gu", ("w_ffn_gate", "w_ffn_up"), 1),
    ("down_o", ("w_ffn_down", "w_o"), 0),
    ("outs", ("w_fox_out", "w_rwkv_out", "w_mem_out"), 1),
    ("lora", ("rwkv_w_up", "rwkv_a_up", "rwkv_g_up"), 0),
)
FIRST_GROUPS = ("in", "memkv")
LATE_GROUPS = (("down_o", "outs", "lora"), ("ffn_gu",))
EARLY_GRAD_GROUPS = ("memkv", "ffn_gu", "down_o", "outs")
LAST_GRAD_GROUPS = ("in", "lora")
SHARD_AXIS = {n: a for n, _, a in SHARDED}
SMALL_ROWS = 16


def _group_local(shards, members, join):
    parts = [shards[n].reshape(shards[n].shape[-2:]) for n in members]
    return parts[0] if len(parts) == 1 else jnp.concatenate(parts, axis=join)


def _group_split(arr, members, join, lead=False):
    out, off = {}, 0
    for n in members:
        shape = dict((k, s) for k, s, _ in SHARDED)[n]
        size = _block_shape(shape, SHARD_AXIS[n])[join]
        idx = [slice(None)] * arr.ndim
        idx[arr.ndim - 2 + join] = slice(off, off + size)
        out[n] = arr[tuple(idx)]
        off += size
    return out


def _full_from_blocks(blocks, axis):
    if axis == 0:
        return blocks.reshape(-1, blocks.shape[2])
    return blocks.transpose(1, 0, 2).reshape(blocks.shape[1], -1)


def _blocks_from_full(full, axis):
    if axis == 0:
        return full.reshape(N_DEV, -1, full.shape[1])
    return full.reshape(full.shape[0], N_DEV, -1).transpose(1, 0, 2)


def _assemble(gathered, names):
    out = {}
    for arr, g in zip(gathered, names):
        _, members, join = [grp for grp in GROUPS if grp[0] == g][0]
        for n, blk in _group_split(arr, members, join, lead=True).items():
            out[n] = _full_from_blocks(blk, SHARD_AXIS[n])
    return out


def _grad_blocks(gw, names):
    out = []
    for g in names:
        _, members, join = [grp for grp in GROUPS if grp[0] == g][0]
        parts = [_blocks_from_full(gw[n].astype(bf16), SHARD_AXIS[n]) for n in members]
        out.append(parts[0] if len(parts) == 1 else jnp.concatenate(parts, axis=1 + join))
    return out


def _small_pack(d):
    flat = jnp.concatenate([d[n].reshape(-1) for n, _ in REPLICATED])
    return jnp.pad(flat, (0, SMALL_ROWS * LANES - REPL_ELEMS)).reshape(SMALL_ROWS, LANES)


def _small_unpack(packed):
    out, flat, off = {}, packed.reshape(-1), 0
    for n, shape in REPLICATED:
        k = _rows_of((LANES,) + shape)
        out[n] = flat[off:off + k].reshape(shape)
        off += k
    return out


def kernel(x, mem, pre1_g, post1_g, pre2_g, post2_g, mem_norm_g, w_in, fox_f_bias, rwkv_mu, rwkv_w0, rwkv_w_up, rwkv_a0, rwkv_a_up, rwkv_g_up, rwkv_k_k, rwkv_k_a, rwkv_r_k, rwkv_gn_g, rwkv_gn_b, w_mem_kv, w_fox_out, w_rwkv_out, w_mem_out, w_o, w_ffn_gate, w_ffn_up, w_ffn_down, loss_target, m_pre1_g, m_post1_g, m_pre2_g, m_post2_g, m_mem_norm_g, m_w_in, m_fox_f_bias, m_rwkv_mu, m_rwkv_w0, m_rwkv_w_up, m_rwkv_a0, m_rwkv_a_up, m_rwkv_g_up, m_rwkv_k_k, m_rwkv_k_a, m_rwkv_r_k, m_rwkv_gn_g, m_rwkv_gn_b, m_w_mem_kv, m_w_fox_out, m_w_rwkv_out, m_w_mem_out, m_w_o, m_w_ffn_gate, m_w_ffn_up, m_w_ffn_down, v_pre1_g, v_post1_g, v_pre2_g, v_post2_g, v_mem_norm_g, v_w_in, v_fox_f_bias, v_rwkv_mu, v_rwkv_w0, v_rwkv_w_up, v_rwkv_a0, v_rwkv_a_up, v_rwkv_g_up, v_rwkv_k_k, v_rwkv_k_a, v_rwkv_r_k, v_rwkv_gn_g, v_rwkv_gn_b, v_w_mem_kv, v_w_fox_out, v_w_rwkv_out, v_w_mem_out, v_w_o, v_w_ffn_gate, v_w_ffn_up, v_w_ffn_down):
    args = dict(locals())
    wts = {n: args[n] for n in WEIGHT_ORDER}
    ms = {n: args["m_" + n] for n in WEIGHT_ORDER}
    vs = {n: args["v_" + n] for n in WEIGHT_ORDER}

    groups = {g: (members, join) for g, members, join in GROUPS}
    w_bf16 = {n: wts[n].astype(bf16) for n, _, _ in SHARDED}

    def send(g):
        return _group_local(w_bf16, *groups[g])

    first = _exchange("gather_first", [send(g) for g in FIRST_GROUPS], per_peer=False)
    full = _assemble(first, FIRST_GROUPS)
    small_in = {n: (wts[n] if n == "rwkv_r_k" else wts[n].reshape(wts[n].shape[-2:])) for n, _ in REPLICATED}
    late = ([send(g) for g in LATE_GROUPS[0]], [send(g) for g in LATE_GROUPS[1]],
            lambda got, which: _assemble(got, LATE_GROUPS[which]))
    loss_part, grad_x, gw, gp, early_got = _local_step(
        x, mem, loss_target, full, small_in, late=late, early=lambda g: _grad_blocks(g, EARLY_GRAD_GROUPS))

    small_send = jnp.broadcast_to(_small_pack(gp).astype(bf16)[None], (N_DEV, SMALL_ROWS, LANES))
    *last_got, small_got = _exchange("exchange_last", _grad_blocks(gw, LAST_GRAD_GROUPS) + [small_send], per_peer=True)
    received = dict(zip(EARLY_GRAD_GROUPS + LAST_GRAD_GROUPS, list(early_got) + list(last_got)))

    outs = [{}, {}, {}, {}]
    for g, members, join in GROUPS:
        res = _adamw("adamw_" + g, received[g], *[_group_local(d, members, join) for d in (wts, ms, vs)])
        for o, arr in zip(outs, res):
            o.update(_group_split(arr, members, join))
    res = _adamw("adamw_small", small_got, *[_small_pack(d) for d in (wts, ms, vs)])
    for o, arr in zip(outs, res):
        o.update(_small_unpack(arr))
    loss = lax.psum(loss_part[0, 0], ("x", "y", "c"))
    return (loss, grad_x, *[o[n].reshape(wts[n].shape) for o in outs for n in WEIGHT_ORDER])
```

```python
import functools

import jax
import jax.numpy as jnp
from jax import lax
from jax.experimental import pallas as pl
from jax.experimental.pallas import tpu as pltpu

f32 = jnp.float32
bf16 = jnp.bfloat16
_HI = lax.Precision.HIGHEST

D = 1024
HEADS = 8
HD = 64
HW = HEADS * HD
MEM_HEADS = 4
MEM_HD = 128
MEM_W = 512
MEM_LEN = 256
D_FF = 2816
LORA_PAD = 128
RW_COLS = 3 * HW + 3 * LORA_PAD
NORM_EPS = 1e-6
GN_EPS = 64e-5
Q_BLOCK = 128
SCAN_CHUNK = 64
N_DEV = 8
LANES = 1024
VMEM_LIMIT = 56 * 1024 * 1024

ADAM_LR = 0.001
ADAM_B1 = 0.9
ADAM_B2 = 0.999
ADAM_EPS = 1e-08
ADAM_WD = 0.01
ADAM_STEP = 10

SHARDED = (
    ("w_in", (1024, 6920), 1),
    ("w_ffn_gate", (1024, 2816), 1),
    ("w_ffn_up", (1024, 2816), 1),
    ("w_ffn_down", (2816, 1024), 0),
    ("w_mem_kv", (1024, 1024), 0),
    ("w_o", (1024, 1024), 0),
    ("w_fox_out", (512, 1024), 1),
    ("w_rwkv_out", (512, 1024), 1),
    ("w_mem_out", (512, 1024), 1),
    ("rwkv_w_up", (64, 512), 1),
    ("rwkv_a_up", (64, 512), 1),
    ("rwkv_g_up", (128, 512), 1),
)
REPLICATED = (
    ("pre1_g", (1, 1024)), ("post1_g", (1, 1024)), ("pre2_g", (1, 1024)), ("post2_g", (1, 1024)),
    ("mem_norm_g", (1, 1024)), ("fox_f_bias", (1, 8)), ("rwkv_mu", (1, 1792)), ("rwkv_w0", (1, 512)),
    ("rwkv_a0", (1, 512)), ("rwkv_k_k", (1, 512)), ("rwkv_k_a", (1, 512)), ("rwkv_r_k", (1, 8, 64)),
    ("rwkv_gn_g", (1, 512)), ("rwkv_gn_b", (1, 512)),
)
WEIGHT_ORDER = ('pre1_g', 'post1_g', 'pre2_g', 'post2_g', 'mem_norm_g', 'w_in', 'fox_f_bias', 'rwkv_mu',
                'rwkv_w0', 'rwkv_w_up', 'rwkv_a0', 'rwkv_a_up', 'rwkv_g_up', 'rwkv_k_k', 'rwkv_k_a',
                'rwkv_r_k', 'rwkv_gn_g', 'rwkv_gn_b', 'w_mem_kv', 'w_fox_out', 'w_rwkv_out', 'w_mem_out',
                'w_o', 'w_ffn_gate', 'w_ffn_up', 'w_ffn_down')


def _block_shape(shape, axis):
    return tuple(s // N_DEV if i == axis else s for i, s in enumerate(shape))


def _rows_of(shape):
    n = 1
    for s in shape:
        n *= s
    return n // LANES


SHARD_ROWS = sum(_rows_of(_block_shape(s, a)) for _, s, a in SHARDED)
REPL_ELEMS = sum(_rows_of((LANES,) + s) for _, s in REPLICATED)
REPL_ROWS = -(-REPL_ELEMS // LANES)
PACK_ROWS = -(-(SHARD_ROWS + REPL_ROWS) // 128) * 128
GATHER_ROWS = -(-SHARD_ROWS // 16) * 16


def _cp(sem=None):
    return pltpu.CompilerParams(dimension_semantics=sem, vmem_limit_bytes=VMEM_LIMIT)


def _tile(dim, cap):
    best = None
    for t in range(128, min(dim, cap) + 1, 128):
        if dim % t == 0:
            best = t
    return best if best is not None else dim


def _two_terms(x):
    hi = x.astype(bf16)
    return hi, (x - hi.astype(f32)).astype(bf16)


def _dg(a, b, dims, exact):
    if exact == "split":
        (a_hi, a_lo), (b_hi, b_lo) = _two_terms(a), _two_terms(b)
        dot = functools.partial(lax.dot_general, dimension_numbers=dims, preferred_element_type=f32)
        return dot(a_hi, b_hi) + (dot(a_hi, b_lo) + dot(a_lo, b_hi))
    if exact:
        return lax.dot_general(a, b, dims, precision=_HI, preferred_element_type=f32)
    return lax.dot_general(a.astype(bf16), b.astype(bf16), dims, preferred_element_type=f32)


def _make_mm(batched, exact):
    o = 1 if batched else 0
    bd = ((0,), (0,)) if batched else ((), ())
    d_nn = (((1 + o,), (o,)), bd)
    d_nt = (((1 + o,), (1 + o,)), bd)
    d_tn = (((o,), (o,)), bd)

    @jax.custom_vjp
    def nn(a, b):
        return _dg(a, b, d_nn, exact)

    @jax.custom_vjp
    def nt(a, b):
        return _dg(a, b, d_nt, exact)

    @jax.custom_vjp
    def tn(a, b):
        return _dg(a, b, d_tn, exact)

    nn.defvjp(lambda a, b: (_dg(a, b, d_nn, exact), (a, b)),
              lambda res, g: (_dg(g, res[1], d_nt, exact), _dg(res[0], g, d_tn, exact)))
    nt.defvjp(lambda a, b: (_dg(a, b, d_nt, exact), (a, b)),
              lambda res, g: (_dg(g, res[1], d_nn, exact), _dg(g, res[0], d_tn, exact)))
    tn.defvjp(lambda a, b: (_dg(a, b, d_tn, exact), (a, b)),
              lambda res, g: (_dg(res[1], g, d_nt, exact), _dg(res[0], g, d_nn, exact)))
    return nn, nt, tn


def _sigmoid(x):
    return 1.0 / (1.0 + jnp.exp(-x))


def _head_sum_raw(x):
    width = 2 * HD
    i = lax.broadcasted_iota(jnp.int32, (width, width), 0) // HD
    j = lax.broadcasted_iota(jnp.int32, (width, width), 1) // HD
    m = (i == j).astype(bf16)
    dims = (((1,), (0,)), ((), ()))
    out = []
    for p in range(x.shape[1] // width):
        xp = x[:, p * width:(p + 1) * width]
        hi = xp.astype(bf16)
        lo = (xp - hi.astype(f32)).astype(bf16)
        out.append(lax.dot_general(hi, m, dims, preferred_element_type=f32)
                   + lax.dot_general(lo, m, dims, preferred_element_type=f32))
    return jnp.concatenate(out, axis=1)


@jax.custom_vjp
def _head_sum(x):
    return _head_sum_raw(x)


_head_sum.defvjp(lambda x: (_head_sum_raw(x), None), lambda _, g: (_head_sum_raw(g),))


WEIGHT_TILE_BYTES = 13 * 512 * 1024
ACC_TILE_BYTES = 8 * 1024 * 1024


def _matmul(name, a, b, mode, add=None, out_dtype=f32):
    has_add = add is not None
    if mode == "tn":
        (k, m), (_, n) = a.shape, b.shape
        tn = _tile(n, max(128, ACC_TILE_BYTES // (4 * m)))
        tk = _tile(k, 1024)

        def body(a_ref, b_ref, o_ref):
            @pl.when(pl.program_id(1) == 0)
            def _():
                o_ref[...] = jnp.zeros_like(o_ref)

            o_ref[...] += lax.dot_general(a_ref[...].astype(bf16), b_ref[...].astype(bf16),
                                          (((0,), (0,)), ((), ())), preferred_element_type=f32)

        return pl.pallas_call(
            body, name=name, grid=(n // tn, k // tk),
            in_specs=[pl.BlockSpec((tk, m), lambda j, kk: (kk, 0)), pl.BlockSpec((tk, tn), lambda j, kk: (kk, j))],
            out_specs=pl.BlockSpec((m, tn), lambda j, kk: (0, j)), out_shape=jax.ShapeDtypeStruct((m, n), f32),
            compiler_params=_cp(("parallel", "arbitrary")),
        )(a, b)

    (m, k) = a.shape
    n = b.shape[1] if mode == "nn" else b.shape[0]
    tm = _tile(m, 512)
    tn = _tile(n, max(128, WEIGHT_TILE_BYTES // (2 * k)))
    dims = (((1,), (0,)), ((), ())) if mode == "nn" else (((1,), (1,)), ((), ()))
    b_spec = pl.BlockSpec((k, tn), lambda j, i: (0, j)) if mode == "nn" else pl.BlockSpec((tn, k), lambda j, i: (j, 0))
    o_spec = pl.BlockSpec((tm, tn), lambda j, i: (i, j))

    def body(*refs):
        a_ref, b_ref = refs[0], refs[1]
        o_ref = refs[-1]
        r = lax.dot_general(a_ref[...].astype(bf16), b_ref[...].astype(bf16), dims, preferred_element_type=f32)
        if has_add:
            r = r + refs[2][...]
        o_ref[...] = r.astype(o_ref.dtype)

    return pl.pallas_call(
        body, name=name, grid=(n // tn, m // tm),
        in_specs=[pl.BlockSpec((tm, k), lambda j, i: (i, 0)), b_spec] + ([o_spec] if has_add else []),
        out_specs=o_spec, out_shape=jax.ShapeDtypeStruct((m, n), out_dtype),
        compiler_params=_cp(("parallel", "arbitrary")),
    )(*((a, b, add) if has_add else (a, b)))


def _pieces(ref, widths):
    out, off = [], 0
    for w in widths:
        out.append(ref[:, off:off + w].astype(f32))
        off += w
    return out


def _store_pieces(ref, widths, vals, add_ref=None):
    off = 0
    for w, v in zip(widths, vals):
        ref[:, off:off + w] = (v if add_ref is None else v + add_ref[:, off:off + w]).astype(ref.dtype)
        off += w


def _rows_fwd(name, fn, consts, rows, params, outs, n_sums=0, tm=256, dtypes=None):
    t = (consts + rows)[0][0].shape[0]
    tm = min(tm, t)
    ins = consts + rows
    n_in, n_p, n_o = len(ins), len(params), len(outs)
    dtypes = dtypes or [f32] * n_o

    def body(*refs):
        in_refs, p_refs = refs[:n_in], refs[n_in:n_in + n_p]
        o_refs, s_refs = refs[n_in + n_p:n_in + n_p + n_o], refs[n_in + n_p + n_o:]
        vals = []
        for r, (_, widths) in zip(in_refs, ins):
            vals += _pieces(r, widths)
        res = fn(*vals, *[p[...] for p in p_refs])
        pos = 0
        for r, widths in zip(o_refs, outs):
            _store_pieces(r, widths, res[pos:pos + len(widths)])
            pos += len(widths)

        @pl.when(pl.program_id(0) == 0)
        def _():
            for s in s_refs:
                s[...] = jnp.zeros_like(s)

        for s, v in zip(s_refs, res[pos:]):
            s[...] += v

    row_spec = lambda w: pl.BlockSpec((tm, w), lambda i: (i, 0))
    full = lambda p: pl.BlockSpec(p.shape, lambda i: (0,) * p.ndim)
    return pl.pallas_call(
        body, name=name, grid=(t // tm,),
        in_specs=[row_spec(a.shape[1]) for a, _ in ins] + [full(p) for p in params],
        out_specs=[row_spec(sum(w)) for w in outs] + [pl.BlockSpec((1, 1), lambda i: (0, 0))] * n_sums,
        out_shape=[jax.ShapeDtypeStruct((t, sum(w)), dt) for w, dt in zip(outs, dtypes)] + [jax.ShapeDtypeStruct((1, 1), f32)] * n_sums,
        compiler_params=_cp(("arbitrary",)),
    )(*[a for a, _ in ins], *params)


def _rows_bwd(name, fn, consts, rows, params, outs, cts, n_sums=0, add=None, tm=256, dtypes=None):
    t = (consts + rows)[0][0].shape[0]
    tm = min(tm, t)
    n_c, n_r, n_p, n_o = len(consts), len(rows), len(params), len(outs)
    has_add = add is not None
    dtypes = dtypes or [f32] * n_r

    def body(*refs):
        pos = 0
        c_refs = refs[pos:pos + n_c]; pos += n_c
        r_refs = refs[pos:pos + n_r]; pos += n_r
        p_refs = refs[pos:pos + n_p]; pos += n_p
        ct_refs = refs[pos:pos + n_o]; pos += n_o
        add_ref = refs[pos] if has_add else None
        pos += 1 if has_add else 0
        dr_refs = refs[pos:pos + n_r]; pos += n_r
        dp_refs = refs[pos:pos + n_p]
        cvals, rvals = [], []
        for r, (_, widths) in zip(c_refs, consts):
            cvals += _pieces(r, widths)
        for r, (_, widths) in zip(r_refs, rows):
            rvals += _pieces(r, widths)
        pvals = [p[...] for p in p_refs]
        ctv = []
        for r, widths in zip(ct_refs, outs):
            ctv += _pieces(r, widths)
        ctv += [jnp.ones((1, 1), f32)] * n_sums
        _, vjp = jax.vjp(lambda *rp: tuple(fn(*cvals, *rp)), *rvals, *pvals)
        g = vjp(tuple(ctv))
        pos = 0
        for idx, (r, (_, widths)) in enumerate(zip(dr_refs, rows)):
            _store_pieces(r, widths, g[pos:pos + len(widths)], add_ref if idx == 0 else None)
            pos += len(widths)

        @pl.when(pl.program_id(0) == 0)
        def _():
            for dp in dp_refs:
                dp[...] = jnp.zeros_like(dp)

        for dp, v in zip(dp_refs, g[pos:]):
            dp[...] += v

    row_spec = lambda w: pl.BlockSpec((tm, w), lambda i: (i, 0))
    full = lambda p: pl.BlockSpec(p.shape, lambda i: (0,) * p.ndim)
    args = [a for a, _ in consts + rows] + list(params) + list(cts) + ([add] if has_add else [])
    res = pl.pallas_call(
        body, name=name, grid=(t // tm,),
        in_specs=[row_spec(a.shape[1]) for a, _ in consts + rows] + [full(p) for p in params]
        + [row_spec(sum(w)) for w in outs] + ([row_spec(add.shape[1])] if has_add else []),
        out_specs=[row_spec(a.shape[1]) for a, _ in rows] + [full(p) for p in params],
        out_shape=[jax.ShapeDtypeStruct(a.shape, dt) for (a, _), dt in zip(rows, dtypes)]
        + [jax.ShapeDtypeStruct(p.shape, f32) for p in params],
        compiler_params=_cp(("arbitrary",)),
    )(*args)
    return res[:n_r], res[n_r:]


def _rms(x, g):
    return x * lax.rsqrt(jnp.mean(x * x, axis=-1, keepdims=True) + NORM_EPS) * g


def _fn_rms(x, g):
    return (_rms(x, g),)


def _fn_rwkv_pre(r, k, v, wd, ad, gd, w0, w_up, a0, a_up, g_up, k_k, k_a):
    nn, _, _ = _make_mm(False, False)
    w_log = -_sigmoid(w0 + nn(jnp.tanh(wd), w_up)) * 0.6065306597126334
    a = _sigmoid(a0 + nn(ad, a_up))
    g = nn(_sigmoid(gd), g_up)
    kk = k * k_k
    kk = kk * lax.rsqrt(jnp.maximum(_head_sum(kk * kk), 1e-24))
    k2 = k * (1.0 + (a - 1.0) * k_a)
    return r, w_log, k2, v, -kk, kk * a, g


def _fn_rwkv_post(y, r, k2, v, g, gn_g, gn_b, r_k):
    mean = _head_sum(y) * (1.0 / HD)
    yc = y - mean
    var = _head_sum(yc * yc) * (1.0 / HD)
    yn = yc * lax.rsqrt(var + GN_EPS) * gn_g + gn_b
    bonus = _head_sum(r * k2 * r_k) * v
    return ((yn + bonus) * g,)


def _fn_merge(a_fox, a_rwkv, a_mem, g_fox, g_rwkv, g_mem):
    return (_sigmoid(g_fox) * a_fox + _sigmoid(g_rwkv) * a_rwkv + _sigmoid(g_mem) * a_mem,)


def _fn_post1(y, x, post1_g, pre2_g):
    h1 = x + _rms(y, post1_g)
    return h1, _rms(h1, pre2_g)


def _fn_swiglu(gp, up):
    return (gp * _sigmoid(gp) * up,)


def _fn_final(target, ffn, h1, post2_g):
    err = h1 + _rms(ffn, post2_g) - target
    per_row = jnp.mean(err * err, axis=-1, keepdims=True)
    return (0.5 * jnp.sum(per_row, axis=0, keepdims=True),)


def _shift_down(x):
    row = lax.broadcasted_iota(jnp.int32, x.shape, 0)
    return jnp.where(row == 0, 0.0, pltpu.roll(x, 1, 0))


def _shift_up(x):
    s = x.shape[0]
    row = lax.broadcasted_iota(jnp.int32, x.shape, 0)
    return jnp.where(row == s - 1, 0.0, pltpu.roll(x, s - 1, 0))


def _tokshift_fwd(p, mu, batch, seq):
    w = p.shape[1]
    tc = _tile(w, 384)

    def body(p_ref, mu_ref, o_ref):
        x = p_ref[...]
        o_ref[...] = x + (_shift_down(x) - x) * mu_ref[...]

    return pl.pallas_call(
        body, name="tokshift_fwd", grid=(w // tc, batch),
        in_specs=[pl.BlockSpec((seq, tc), lambda j, b: (b, j)), pl.BlockSpec((1, tc), lambda j, b: (0, j))],
        out_specs=pl.BlockSpec((seq, tc), lambda j, b: (b, j)),
        out_shape=jax.ShapeDtypeStruct(p.shape, f32),
        compiler_params=_cp(("parallel", "arbitrary")),
    )(p, mu)


def _tokshift_bwd(p, mu, dps, batch, seq):
    w = p.shape[1]
    tc = _tile(w, 384)

    def body(p_ref, mu_ref, d_ref, dp_ref, dmu_ref):
        x, mu_v, d = p_ref[...], mu_ref[...], d_ref[...]
        dp_ref[...] = (d * (1.0 - mu_v) + _shift_up(d * mu_v)).astype(dp_ref.dtype)

        @pl.when(pl.program_id(1) == 0)
        def _():
            dmu_ref[...] = jnp.zeros_like(dmu_ref)

        dmu_ref[...] += jnp.sum(d * (_shift_down(x) - x), axis=0, keepdims=True)

    return pl.pallas_call(
        body, name="tokshift_bwd", grid=(w // tc, batch),
        in_specs=[pl.BlockSpec((seq, tc), lambda j, b: (b, j)), pl.BlockSpec((1, tc), lambda j, b: (0, j)),
                  pl.BlockSpec((seq, tc), lambda j, b: (b, j))],
        out_specs=[pl.BlockSpec((seq, tc), lambda j, b: (b, j)), pl.BlockSpec((1, tc), lambda j, b: (0, j))],
        out_shape=[jax.ShapeDtypeStruct(p.shape, bf16), jax.ShapeDtypeStruct(mu.shape, f32)],
        compiler_params=_cp(("parallel", "arbitrary")),
    )(p, mu, dps)


def _cum_block(seq):
    return _tile(seq, 256)


def _fox_gate_fwd(f, bias, batch, seq):
    cb = _cum_block(seq)

    def body(f_ref, b_ref, c_ref):
        row = lax.broadcasted_iota(jnp.int32, (cb, cb), 0)
        col = lax.broadcasted_iota(jnp.int32, (cb, cb), 1)
        tri = (col <= row).astype(f32)
        carry = jnp.zeros((1, 128), f32)
        for i in range(seq // cb):
            z = f_ref[i * cb:(i + 1) * cb, :] + b_ref[...]
            ls = jnp.minimum(z, 0.0) - jnp.log(1.0 + jnp.exp(-jnp.abs(z)))
            c = _dg(tri, ls, (((1,), (0,)), ((), ())), True) + carry
            c_ref[i * cb:(i + 1) * cb, :] = c
            carry = c[cb - 1:cb, :]

    return pl.pallas_call(
        body, name="fox_gate_fwd", grid=(batch,),
        in_specs=[pl.BlockSpec((seq, 128), lambda b: (b, 0)), pl.BlockSpec((1, 128), lambda b: (0, 0))],
        out_specs=pl.BlockSpec((seq, 128), lambda b: (b, 0)),
        out_shape=jax.ShapeDtypeStruct(f.shape, f32),
        compiler_params=_cp(("arbitrary",)),
    )(f, bias)


def _fox_gate_bwd(f, bias, dc_a, dc_b, batch, seq):
    cb = _cum_block(seq)

    def body(f_ref, b_ref, da_ref, db_ref, df_ref, dbias_ref):
        row = lax.broadcasted_iota(jnp.int32, (cb, cb), 0)
        col = lax.broadcasted_iota(jnp.int32, (cb, cb), 1)
        triu = (col >= row).astype(f32)

        @pl.when(pl.program_id(0) == 0)
        def _():
            dbias_ref[...] = jnp.zeros_like(dbias_ref)

        carry = jnp.zeros((1, 128), f32)
        tot = jnp.zeros((1, 128), f32)
        for i in reversed(range(seq // cb)):
            sl = slice(i * cb, (i + 1) * cb)
            dc = da_ref[sl, :] + db_ref[sl, :]
            dls = _dg(triu, dc, (((1,), (0,)), ((), ())), True) + carry
            carry = dls[0:1, :]
            df = dls * _sigmoid(-(f_ref[sl, :] + b_ref[...]))
            df_ref[sl, :] = df.astype(df_ref.dtype)
            tot = tot + jnp.sum(df, axis=0, keepdims=True)
        dbias_ref[...] += tot

    return pl.pallas_call(
        body, name="fox_gate_bwd", grid=(batch,),
        in_specs=[pl.BlockSpec((seq, 128), lambda b: (b, 0)), pl.BlockSpec((1, 128), lambda b: (0, 0)),
                  pl.BlockSpec((seq, 128), lambda b: (b, 0)), pl.BlockSpec((seq, 128), lambda b: (b, 0))],
        out_specs=[pl.BlockSpec((seq, 128), lambda b: (b, 0)), pl.BlockSpec((1, 128), lambda b: (0, 0))],
        out_shape=[jax.ShapeDtypeStruct(f.shape, bf16), jax.ShapeDtypeStruct((1, 128), f32)],
        compiler_params=_cp(("arbitrary",)),
    )(f, bias, dc_a, dc_b)


_HBM_SPEC = pl.BlockSpec(memory_space=pltpu.HBM)


def _side_out_shapes(srcs, per_peer):
    return [jax.ShapeDtypeStruct(((N_DEV,) + tuple(s.shape[1:] if per_peer else s.shape)), s.dtype) for s in srcs]


def _side_sems(n):
    if n == 0:
        return []
    return [pltpu.SemaphoreType.DMA((n, N_DEV - 1)), pltpu.SemaphoreType.DMA((n, N_DEV - 1)), pltpu.SemaphoreType.DMA((n,))]


def _peer_copies(src_refs, dst_refs, per_peer, sems):
    send_sems, recv_sems, local_sems = sems
    x, y, c = lax.axis_index("x"), lax.axis_index("y"), lax.axis_index("c")
    me = 4 * x + 2 * y + c
    copies = []
    for t, (s, d) in enumerate(zip(src_refs, dst_refs)):
        copies.append(pltpu.make_async_copy(s.at[me] if per_peer else s, d.at[me], local_sems.at[t]))
        for k in range(1, N_DEV):
            px = 1 - x if k & 4 else x
            py = 1 - y if k & 2 else y
            pc = 1 - c if k & 1 else c
            copies.append(pltpu.make_async_remote_copy(
                src_ref=s.at[4 * px + 2 * py + pc] if per_peer else s, dst_ref=d.at[me],
                send_sem=send_sems.at[t, k - 1], recv_sem=recv_sems.at[t, k - 1],
                device_id=(px, py, pc), device_id_type=pl.DeviceIdType.MESH))
    return copies


def _side_exchange(src_refs, dst_refs, per_peer, sems, *grid):
    if not src_refs:
        return
    first = functools.reduce(jnp.logical_and, [pl.program_id(a) == 0 for a in range(len(grid))])
    last = functools.reduce(jnp.logical_and, [pl.program_id(a) == n - 1 for a, n in enumerate(grid)])

    @pl.when(first)
    def _():
        for cp in _peer_copies(src_refs, dst_refs, per_peer, sems):
            cp.start()

    @pl.when(last)
    def _():
        for cp in _peer_copies(src_refs, dst_refs, per_peer, sems):
            cp.wait()


def _exchange(name, srcs, per_peer):
    n = len(srcs)

    def body(*refs):
        copies = _peer_copies(refs[:n], refs[n:2 * n], per_peer, refs[2 * n:])
        for cp in copies:
            cp.start()
        for cp in copies:
            cp.wait()

    return pl.pallas_call(
        body, name=name, in_specs=[_HBM_SPEC] * n, out_specs=[_HBM_SPEC] * n,
        out_shape=_side_out_shapes(srcs, per_peer), scratch_shapes=_side_sems(n),
    )(*srcs)


FOX_T = 256
_NEG = -1e30
_D2 = (((1,), (1,)), ((), ()))
_D1 = (((1,), (0,)), ((), ()))
_D0 = (((0,), (0,)), ((), ()))


def _bdot(a, b, dims):
    return lax.dot_general(a.astype(bf16), b.astype(bf16), dims, preferred_element_type=f32)


def _pick_lane(x, lane):
    idx = lax.broadcasted_iota(jnp.int32, x.shape, 1)
    return jnp.sum(jnp.where(idx == lane, x, 0.0), axis=1, keepdims=True)


def _pick_row(x, row):
    idx = lax.broadcasted_iota(jnp.int32, x.shape, 0)
    return jnp.sum(jnp.where(idx == row, x, 0.0), axis=0, keepdims=True)


def _fox_fwd(qkv, c, c_rows, batch, seq, side=None):
    t = min(FOX_T, seq)
    nq = seq // t
    scale = HD ** -0.5
    srcs, per_peer = side if side is not None else ([], False)
    n_s = len(srcs)

    def body(*refs):
        q_ref, k_ref, v_ref, cq_ref, ck_ref = refs[:5]
        o_ref, lse_ref = refs[5 + n_s:7 + n_s]
        _side_exchange(refs[5:5 + n_s], refs[7 + n_s:7 + 2 * n_s], per_peer, refs[7 + 2 * n_s:], batch, PAIRS, nq)
        pair, i = pl.program_id(1), pl.program_id(2)
        lane = lax.broadcasted_iota(jnp.int32, (1, PAIR_W), 1)
        first = (lane // HD) == 0
        q = q_ref[...] * scale
        qs = [jnp.where(first, q, 0.0), jnp.where(first, 0.0, q)]
        cqs = [_pick_lane(cq_ref[...], 2 * pair + e) for e in range(2)]
        qidx = i * t + lax.broadcasted_iota(jnp.int32, (t, t), 0)

        def step(j, carry):
            rows = pl.ds(pl.multiple_of(j * t, t), t)
            kj, vj = k_ref[rows, :], v_ref[rows, :]
            ck_blk = ck_ref[0, :, rows]
            vis = (j * t + lax.broadcasted_iota(jnp.int32, (t, t), 1)) <= qidx
            out = []
            for e in range(2):
                m, l, acc = carry[3 * e:3 * e + 3]
                s = _bdot(qs[e], kj, _D2) + (cqs[e] - _pick_row(ck_blk, 2 * pair + e))
                s = jnp.where(vis, s, _NEG)
                m_new = jnp.maximum(m, jnp.max(s, axis=1, keepdims=True))
                alpha = jnp.exp(m - m_new)
                p = jnp.exp(s - m_new)
                out += [m_new, alpha * l + jnp.sum(p, axis=1, keepdims=True), alpha * acc + _bdot(p, vj, _D1)]
            return tuple(out)

        init = (jnp.full((t, 1), _NEG, f32), jnp.zeros((t, 1), f32), jnp.zeros((t, PAIR_W), f32)) * 2
        m0, l0, a0, m1, l1, a1 = lax.fori_loop(0, i + 1, step, init)
        o_ref[...] = jnp.where(first, a0 / l0, a1 / l1)
        lse_ref[...] = jnp.where(lane == 0, m0 + jnp.log(l0), jnp.where(lane == 1, m1 + jnp.log(l1), 0.0))

    q_spec = pl.BlockSpec((t, PAIR_W), lambda b, p, i: (b * nq + i, p))
    res = pl.pallas_call(
        body, name="fox_attn_fwd", grid=(batch, PAIRS, nq),
        in_specs=[q_spec,
                  pl.BlockSpec((seq, PAIR_W), lambda b, p, i: (b, PAIRS + p)),
                  pl.BlockSpec((seq, PAIR_W), lambda b, p, i: (b, 2 * PAIRS + p)),
                  pl.BlockSpec((t, 128), lambda b, p, i: (b * nq + i, 0)),
                  pl.BlockSpec((1, 8, seq), lambda b, p, i: (b, 0, 0))] + [_HBM_SPEC] * n_s,
        out_specs=[q_spec, q_spec] + [_HBM_SPEC] * n_s,
        out_shape=[jax.ShapeDtypeStruct((batch * seq, HW), f32)] * 2 + _side_out_shapes(srcs, per_peer),
        scratch_shapes=_side_sems(n_s),
        compiler_params=_cp(("arbitrary", "arbitrary", "arbitrary")),
    )(qkv, qkv, qkv, c, c_rows, *srcs)
    return res[0], res[1], list(res[2:])


def _fox_bwd(qkv, c, c_rows, o, lse, do, batch, seq, side=None):
    t = min(FOX_T, seq)
    nq = seq // t
    scale = HD ** -0.5
    srcs, per_peer = side if side is not None else ([], False)
    n_s = len(srcs)

    def body(*refs):
        q_ref, k_ref, v_ref, cq_ref, ck_ref, o_ref, lse_ref, do_ref = refs[:8]
        dq_ref, dk_ref, dv_ref, dcq_ref, dck_ref = refs[8 + n_s:13 + n_s]
        _side_exchange(refs[8:8 + n_s], refs[13 + n_s:13 + 2 * n_s], per_peer, refs[13 + 2 * n_s:], batch, PAIRS, nq)
        pair, j = pl.program_id(1), pl.program_id(2)

        @pl.when(j == 0)
        def _():
            dq_ref[...] = jnp.zeros_like(dq_ref)
            dcq_ref[...] = jnp.zeros_like(dcq_ref)

        lane = lax.broadcasted_iota(jnp.int32, (1, PAIR_W), 1)
        first = (lane // HD) == 0
        sub = lax.broadcasted_iota(jnp.int32, (8, t), 0)
        kj, vj = k_ref[...], v_ref[...]
        ks = [jnp.where(first, kj, 0.0), jnp.where(first, 0.0, kj)]
        cks = [_pick_row(ck_ref[0], 2 * pair + e) for e in range(2)]
        kidx = j * t + lax.broadcasted_iota(jnp.int32, (t, t), 1)
        ones8 = jnp.ones((8, t), f32)

        def step(i, carry):
            dk, dv, dck0, dck1 = carry
            rows = pl.ds(pl.multiple_of(i * t, t), t)
            q = q_ref[rows, :] * scale
            d_o, o_i, lse_i, cq_i = do_ref[rows, :], o_ref[rows, :], lse_ref[rows, :], cq_ref[rows, :]
            vis = kidx <= (i * t + lax.broadcasted_iota(jnp.int32, (t, t), 0))
            dq_acc = jnp.zeros((t, PAIR_W), f32)
            dcq_acc = jnp.zeros((8, t), f32)
            dcks = [dck0, dck1]
            for e in range(2):
                mine = first if e == 0 else jnp.logical_not(first)
                qe, doe = jnp.where(mine, q, 0.0), jnp.where(mine, d_o, 0.0)
                s = _bdot(qe, kj, _D2) + (_pick_lane(cq_i, 2 * pair + e) - cks[e])
                p = jnp.exp(jnp.where(vis, s, _NEG) - _pick_lane(lse_i, e))
                dv = dv + _bdot(p, doe, _D0)
                delta = jnp.sum(doe * o_i, axis=1, keepdims=True)
                ds = p * (_bdot(doe, vj, _D2) - delta)
                dk = dk + _bdot(ds, qe, _D0)
                dq_acc = dq_acc + _bdot(ds, ks[e], _D1)
                row_sums = lax.dot_general(ones8, ds, _D2, precision=_HI, preferred_element_type=f32)
                dcq_acc = dcq_acc + jnp.where(sub == e, row_sums, 0.0)
                dcks[e] = dcks[e] - jnp.sum(ds, axis=0, keepdims=True)
            dq_ref[rows, :] += dq_acc * scale
            dcq_ref[0, :, rows] += dcq_acc
            return dk, dv, dcks[0], dcks[1]

        zero = jnp.zeros((t, PAIR_W), f32)
        dk, dv, dck0, dck1 = lax.fori_loop(j, nq, step, (zero, zero, jnp.zeros((1, t), f32), jnp.zeros((1, t), f32)))
        dk_ref[...] = dk
        dv_ref[...] = dv
        dck_ref[0] = jnp.where(sub == 0, dck0, jnp.where(sub == 1, dck1, 0.0))

    whole = lambda col: pl.BlockSpec((seq, PAIR_W), lambda b, p, j: (b, col * PAIRS + p))
    blk = lambda col: pl.BlockSpec((t, PAIR_W), lambda b, p, j: (b * nq + j, col * PAIRS + p))
    rows_whole = pl.BlockSpec((1, 8, seq), lambda b, p, j: (b * PAIRS + p, 0, 0))
    rows_blk = pl.BlockSpec((1, 8, t), lambda b, p, j: (b * PAIRS + p, 0, j))
    t_all = batch * seq
    res = pl.pallas_call(
        body, name="fox_attn_bwd", grid=(batch, PAIRS, nq),
        in_specs=[whole(0), blk(1), blk(2),
                  pl.BlockSpec((seq, 128), lambda b, p, j: (b, 0)),
                  pl.BlockSpec((1, 8, t), lambda b, p, j: (b, 0, j)),
                  whole(0), whole(0), whole(0)] + [_HBM_SPEC] * n_s,
        out_specs=[whole(0), blk(0), blk(0), rows_whole, rows_blk] + [_HBM_SPEC] * n_s,
        out_shape=[jax.ShapeDtypeStruct((t_all, HW), f32)] * 3
        + [jax.ShapeDtypeStruct((batch * PAIRS, 8, seq), f32)] * 2 + _side_out_shapes(srcs, per_peer),
        scratch_shapes=_side_sems(n_s),
        compiler_params=_cp(("arbitrary", "arbitrary", "arbitrary")),
    )(qkv, qkv, qkv, c, c_rows, o, lse, do, *srcs)
    return res[:5], list(res[5:])


def _mem_block(q, km, vm):
    nn, nt, _ = _make_mm(False, False)
    logits = nt(q, km) * (MEM_HD ** -0.5)
    m = lax.stop_gradient(jnp.max(logits, axis=-1, keepdims=True))
    e = jnp.exp(logits - m)
    return nn(e / jnp.sum(e, axis=-1, keepdims=True), vm)


def _mem_specs(seq, tq):
    nq = seq // tq
    qs = pl.BlockSpec((tq, MEM_HD), lambda b, h, i: (b * nq + i, h))
    ks = pl.BlockSpec((MEM_LEN, MEM_HD), lambda b, h, i: (b, h))
    vs = pl.BlockSpec((MEM_LEN, MEM_HD), lambda b, h, i: (b, MEM_HEADS + h))
    return nq, qs, ks, vs


def _mem_fwd(q, mem_kv, batch, seq):
    tq = min(512, seq)
    nq, qs, ks, vs = _mem_specs(seq, tq)

    def body(q_ref, k_ref, v_ref, o_ref):
        o_ref[...] = _mem_block(q_ref[...], k_ref[...], v_ref[...]).astype(o_ref.dtype)

    return pl.pallas_call(
        body, name="mem_attn_fwd", grid=(batch, MEM_HEADS, nq),
        in_specs=[qs, ks, vs], out_specs=qs, out_shape=jax.ShapeDtypeStruct(q.shape, bf16),
        compiler_params=_cp(("parallel", "parallel", "arbitrary")),
    )(q, mem_kv, mem_kv)


def _mem_bwd(q, mem_kv, do, batch, seq):
    tq = min(512, seq)
    nq, qs, ks, vs = _mem_specs(seq, tq)

    def body(q_ref, k_ref, v_ref, do_ref, dq_ref, dk_ref, dv_ref):
        _, vjp = jax.vjp(_mem_block, q_ref[...], k_ref[...], v_ref[...])
        dq, dk, dv = vjp(do_ref[...])
        dq_ref[...] = dq.astype(dq_ref.dtype)

        @pl.when(pl.program_id(2) == 0)
        def _():
            dk_ref[...] = jnp.zeros_like(dk_ref)
            dv_ref[...] = jnp.zeros_like(dv_ref)

        dk_ref[...] += dk
        dv_ref[...] += dv

    return pl.pallas_call(
        body, name="mem_attn_bwd", grid=(batch, MEM_HEADS, nq),
        in_specs=[qs, ks, vs, qs], out_specs=[qs, ks, ks],
        out_shape=[jax.ShapeDtypeStruct(q.shape, bf16), jax.ShapeDtypeStruct((batch * MEM_LEN, MEM_W), f32),
                   jax.ShapeDtypeStruct((batch * MEM_LEN, MEM_W), f32)],
        compiler_params=_cp(("parallel", "parallel", "arbitrary")),
    )(q, mem_kv, mem_kv, do)


@jax.custom_vjp
def _halves(x):
    c = x.shape[1] // 2
    return x[:, :c], x[:, c:]


_halves.defvjp(lambda x: ((x[:, :x.shape[1] // 2], x[:, x.shape[1] // 2:]), None),
               lambda _, g: (jnp.concatenate(g, axis=1),))


@jax.custom_vjp
def _lead_halves(x):
    n = x.shape[0] // 2
    return x[:n], x[n:]


_lead_halves.defvjp(lambda x: ((x[:x.shape[0] // 2], x[x.shape[0] // 2:]), None),
                    lambda _, g: (jnp.concatenate(g, axis=0),))


def _scan_chunk(s0, r, wl, k, v, a, b):
    nn, nt, tn = _make_mm(True, False)
    nn_exact, _, _ = _make_mm(True, True)
    _, nt_exact, _ = _make_mm(True, "split")
    hp, c, lanes = r.shape
    row = lax.broadcasted_iota(jnp.int32, (c, c), 0)
    col = lax.broadcasted_iota(jnp.int32, (c, c), 1)
    first = (lax.broadcasted_iota(jnp.int32, (1, 1, lanes), 2) // HD) == 0
    tri = jnp.broadcast_to((col <= row).astype(f32)[None], (hp, c, c))
    lg = nn_exact(tri, wl)
    lg_end = lg[:, c - 1:c, :]
    grow, shrink, to_end = jnp.exp(lg), jnp.exp(-lg), jnp.exp(lg_end - lg)
    rt, kt, bt, at = r * grow, k * shrink, b * shrink, a * jnp.exp(lg - wl)
    strict, incl = (col < row)[None], (col <= row)[None]
    twice = lambda t: jnp.concatenate([t, t], axis=0)
    queries = jnp.concatenate([at, rt], axis=1)
    per_head = jnp.concatenate([jnp.where(first, queries, 0.0), jnp.where(first, 0.0, queries)], axis=0)
    (ab, rb), (ak, rk) = _halves(nt_exact(per_head, twice(bt))), _halves(nt_exact(per_head, twice(kt)))
    l_ab = jnp.where(strict, ab, 0.0)
    a_ak = jnp.where(strict, ak, 0.0)
    a_rb = jnp.where(incl, rb, 0.0)
    a_rk = jnp.where(incl, rk, 0.0)
    inv = (col == row).astype(f32)[None] + l_ab
    power, n = l_ab, 1
    while 2 * n < c:
        power = nn(power, power)
        inv = inv + nn(inv, power)
        n *= 2

    def apply(m, t):
        lo, hi = _lead_halves(nn(m, twice(t)))
        return jnp.where(first, lo, hi)

    sa = apply(inv, nt(at, s0) + apply(a_ak, v))
    y = nt(rt, s0) + apply(a_rk, v) + apply(a_rb, sa)
    same_head = ((lax.broadcasted_iota(jnp.int32, (lanes, lanes), 0) // HD)
                 == (lax.broadcasted_iota(jnp.int32, (lanes, lanes), 1) // HD))[None]
    s1 = s0 * jnp.exp(lg_end) + jnp.where(same_head, tn(v, k * to_end) + tn(sa, b * to_end), 0.0)
    return y, s1


PAIRS = HEADS // 2
PAIR_W = 2 * HD


def _pair_stack(ref, off):
    return jnp.stack([ref[b, :, off + p * PAIR_W:off + (p + 1) * PAIR_W]
                      for b in range(ref.shape[0]) for p in range(PAIRS)])


def _pair_store(ref, off, val, add_ref=None):
    for b in range(ref.shape[0]):
        for p in range(PAIRS):
            sl = slice(off + p * PAIR_W, off + (p + 1) * PAIR_W)
            v = val[b * PAIRS + p]
            ref[b, :, sl] = v if add_ref is None else v + add_ref[b, :, sl]


def _scan_fwd(main6, batch, seq, side=None):
    c = min(SCAN_CHUNK, seq)
    nc = seq // c
    hp = batch * PAIRS
    srcs, per_peer = side if side is not None else ([], False)
    n_s = len(srcs)

    def body(*refs):
        z_ref, y_ref, s_ref, st = refs[0], refs[1 + n_s], refs[2 + n_s], refs[3 + 2 * n_s]
        _side_exchange(refs[1:1 + n_s], refs[3 + n_s:3 + 2 * n_s], per_peer, refs[4 + 2 * n_s:], nc)

        @pl.when(pl.program_id(0) == 0)
        def _():
            st[...] = jnp.zeros_like(st)

        s0 = st[...]
        s_ref[0] = s0
        y, s1 = _scan_chunk(s0, *[_pair_stack(z_ref, comp * HW) for comp in range(6)])
        _pair_store(y_ref, 0, y)
        st[...] = s1

    res = pl.pallas_call(
        body, name="rwkv_scan_fwd", grid=(nc,),
        in_specs=[pl.BlockSpec((batch, c, 6 * HW), lambda i: (0, i, 0))] + [_HBM_SPEC] * n_s,
        out_specs=[pl.BlockSpec((batch, c, HW), lambda i: (0, i, 0)),
                   pl.BlockSpec((1, hp, PAIR_W, PAIR_W), lambda i: (i, 0, 0, 0))] + [_HBM_SPEC] * n_s,
        out_shape=[jax.ShapeDtypeStruct((batch, seq, HW), f32), jax.ShapeDtypeStruct((nc, hp, PAIR_W, PAIR_W), f32)]
        + _side_out_shapes(srcs, per_peer),
        scratch_shapes=[pltpu.VMEM((hp, PAIR_W, PAIR_W), f32)] + _side_sems(n_s),
        compiler_params=_cp(("arbitrary",)),
    )(main6.reshape(batch, seq, 6 * HW), *srcs)
    return res[0].reshape(batch * seq, HW), res[1], list(res[2:])


def _scan_bwd(main6, states, dy, extra, batch, seq, side=None):
    c = min(SCAN_CHUNK, seq)
    nc = seq // c
    hp = batch * PAIRS
    srcs, per_peer = side if side is not None else ([], False)
    n_s = len(srcs)

    def body(*refs):
        z_ref, s_ref, dy_ref, ex_ref = refs[:4]
        dz_ref, dst = refs[4 + n_s], refs[5 + 2 * n_s]
        _side_exchange(refs[4:4 + n_s], refs[5 + n_s:5 + 2 * n_s], per_peer, refs[6 + 2 * n_s:], nc)

        @pl.when(pl.program_id(0) == 0)
        def _():
            dst[...] = jnp.zeros_like(dst)

        _, vjp = jax.vjp(_scan_chunk, s_ref[0], *[_pair_stack(z_ref, comp * HW) for comp in range(6)])
        g = vjp((_pair_stack(dy_ref, 0), dst[...]))
        dst[...] = g[0]
        for comp in range(6):
            _pair_store(dz_ref, comp * HW, g[1 + comp], ex_ref)

    back = lambda i: (0, nc - 1 - i, 0)
    wide = pl.BlockSpec((batch, c, 6 * HW), back)
    res = pl.pallas_call(
        body, name="rwkv_scan_bwd", grid=(nc,),
        in_specs=[wide, pl.BlockSpec((1, hp, PAIR_W, PAIR_W), lambda i: (nc - 1 - i, 0, 0, 0)),
                  pl.BlockSpec((batch, c, HW), back), wide] + [_HBM_SPEC] * n_s,
        out_specs=[wide] + [_HBM_SPEC] * n_s,
        out_shape=[jax.ShapeDtypeStruct((batch, seq, 6 * HW), f32)] + _side_out_shapes(srcs, per_peer),
        scratch_shapes=[pltpu.VMEM((hp, PAIR_W, PAIR_W), f32)] + _side_sems(n_s),
        compiler_params=_cp(("arbitrary",)),
    )(main6.reshape(batch, seq, 6 * HW), states, dy.reshape(batch, seq, HW), extra.reshape(batch, seq, 6 * HW), *srcs)
    return res[0].reshape(batch * seq, 6 * HW), list(res[1:])


def _to_heads(x, batch, seq, k):
    return x.reshape(batch, seq, k, HEADS, HD).transpose(2, 0, 3, 1, 4).reshape(k, batch * HEADS, seq, HD)


def _from_heads(x, batch, seq, k):
    return x.reshape(k, batch, HEADS, seq, HD).transpose(1, 3, 0, 2, 4).reshape(batch * seq, k * HW)


def _pad_cols(x, width):
    return jnp.pad(x, ((0, 0), (0, width - x.shape[1])))


def _split_w_in(w):
    z64 = jnp.zeros((w.shape[0], 64), w.dtype)
    w_r = jnp.concatenate([w[:, 1544:3080], w[:, 3080:3144], z64, w[:, 3144:3208], z64, w[:, 3208:3336]], axis=1)
    return w[:, :1536], _pad_cols(w[:, 1536:1544], 128), w_r, w[:, 3336:3848], w[:, 3848:]


def _merge_w_in(g_qkv, g_f, g_r, g_mq, g_g):
    return jnp.concatenate([g_qkv, g_f[:, :8], g_r[:, :1536], g_r[:, 1536:1600], g_r[:, 1664:1728], g_r[:, 1792:],
                            g_mq, g_g], axis=1)


def _pad_lora(v):
    z64 = jnp.zeros((1, 64), v.dtype)
    return jnp.concatenate([v[:, :1536], v[:, 1536:1600], z64, v[:, 1600:1664], z64, v[:, 1664:]], axis=1)


def _unpad_lora(v):
    return jnp.concatenate([v[:, :1536], v[:, 1536:1600], v[:, 1664:1728], v[:, 1792:]], axis=1)


def _local_step(x, mem, target, w, p, late=None, early=None):
    batch, seq, _ = x.shape
    t = batch * seq
    x2, tg2, mem2 = x.reshape(t, D), target.reshape(t, D), mem.reshape(batch * MEM_LEN, D)
    w_qkv, w_f, w_r, w_mq, w_g3 = _split_w_in(w["w_in"])
    mu = _pad_lora(p["rwkv_mu"])
    bias = _pad_cols(p["fox_f_bias"], 128)
    r_k = p["rwkv_r_k"].reshape(1, HW)
    post_params = [p["rwkv_gn_g"], p["rwkv_gn_b"], r_k]
    rw_widths = [HW, HW, HW, LORA_PAD, LORA_PAD, LORA_PAD]
    six = [HW] * 6

    (u,) = _rows_fwd("rms_pre1", _fn_rms, [], [(x2, [D])], [p["pre1_g"]], [[D]], dtypes=[bf16])
    p_qkv = _matmul("proj_qkv", u, w_qkv, "nn")
    p_f = _matmul("proj_f", u, w_f, "nn")
    p_r = _matmul("proj_rwkv", u, w_r, "nn")
    p_mq = _matmul("proj_memq", u, w_mq, "nn")
    p_g = _matmul("proj_gate", u, w_g3, "nn")

    c = _fox_gate_fwd(p_f, bias, batch, seq)
    c_rows = c[:, :HEADS].reshape(batch, seq, HEADS).transpose(0, 2, 1)
    fox_o, lse, gathered = _fox_fwd(p_qkv, c, c_rows, batch, seq, side=(late[0], False) if late else None)
    if late:
        w = {**w, **late[2](gathered, 0)}
    fox_out = fox_o.astype(bf16)

    w_up = jnp.pad(w["rwkv_w_up"].astype(f32), ((0, LORA_PAD - 64), (0, 0)))
    a_up = jnp.pad(w["rwkv_a_up"].astype(f32), ((0, LORA_PAD - 64), (0, 0)))
    pre_params = [p["rwkv_w0"], w_up, p["rwkv_a0"], a_up, w["rwkv_g_up"].astype(f32), p["rwkv_k_k"], p["rwkv_k_a"]]
    ps = _tokshift_fwd(p_r, mu, batch, seq)
    main6, g_rw = _rows_fwd("rwkv_pre", _fn_rwkv_pre, [], [(ps, rw_widths)], pre_params, [six, [HW]])
    y_rw, states, gathered = _scan_fwd(main6, batch, seq, side=(late[1], False) if late else None)
    if late:
        w = {**w, **late[2](gathered, 1)}
    post_consts = []
    post_rows = [(y_rw, [HW]), (main6, six), (g_rw, [HW])]

    def fn_post(y, r, _wl, k2, v, _a, _b, g, gn_g, gn_b, rk):
        return _fn_rwkv_post(y, r, k2, v, g, gn_g, gn_b, rk)

    (rwkv_out,) = _rows_fwd("rwkv_post", fn_post, post_consts, post_rows, post_params, [[HW]], dtypes=[bf16])

    (memn,) = _rows_fwd("rms_mem", _fn_rms, [], [(mem2, [D])], [p["mem_norm_g"]], [[D]], dtypes=[bf16])
    mem_kv = _matmul("proj_memkv", memn, w["w_mem_kv"], "nn")
    mem_out = _mem_fwd(p_mq, mem_kv, batch, seq)

    a_fox = _matmul("out_fox", fox_out, w["w_fox_out"], "nn")
    a_rwkv = _matmul("out_rwkv", rwkv_out, w["w_rwkv_out"], "nn")
    a_mem = _matmul("out_mem", mem_out, w["w_mem_out"], "nn")
    merge_rows = [(a_fox, [D]), (a_rwkv, [D]), (a_mem, [D]), (p_g, [D, D, D])]
    (merged,) = _rows_fwd("merge", _fn_merge, [], merge_rows, [], [[D]], dtypes=[bf16])
    yy = _matmul("out_o", merged, w["w_o"], "nn")
    post1_rows = [(yy, [D]), (x2, [D])]
    post1_params = [p["post1_g"], p["pre2_g"]]
    h1, u2 = _rows_fwd("post1", _fn_post1, [], post1_rows, post1_params, [[D], [D]], dtypes=[f32, bf16])
    gp = _matmul("ffn_gate", u2, w["w_ffn_gate"], "nn")
    up = _matmul("ffn_up", u2, w["w_ffn_up"], "nn")
    (hmid,) = _rows_fwd("swiglu", _fn_swiglu, [], [(gp, [D_FF]), (up, [D_FF])], [], [[D_FF]], dtypes=[bf16])
    ffn = _matmul("ffn_down", hmid, w["w_ffn_down"], "nn")
    final_rows = [(ffn, [D]), (h1, [D])]
    (loss,) = _rows_fwd("final", _fn_final, [(tg2, [D])], final_rows, [p["post2_g"]], [], n_sums=1)

    gw, gp_ = {}, {}
    (d_ffn, d_h1), (gp_["post2_g"],) = _rows_bwd("final_bwd", _fn_final, [(tg2, [D])], final_rows, [p["post2_g"]], [], [],
                                                  n_sums=1, dtypes=[bf16, f32])
    d_hmid = _matmul("ffn_down_dx", d_ffn, w["w_ffn_down"], "nt")
    gw["w_ffn_down"] = _matmul("ffn_down_dw", hmid, d_ffn, "tn")
    (d_gp, d_up), _ = _rows_bwd("swiglu_bwd", _fn_swiglu, [], [(gp, [D_FF]), (up, [D_FF])], [], [[D_FF]], [d_hmid],
                                dtypes=[bf16, bf16])
    d_u2 = _matmul("ffn_gate_dx", d_gp, w["w_ffn_gate"], "nt")
    d_u2 = _matmul("ffn_up_dx", d_up, w["w_ffn_up"], "nt", add=d_u2)
    gw["w_ffn_gate"] = _matmul("ffn_gate_dw", u2, d_gp, "tn")
    gw["w_ffn_up"] = _matmul("ffn_up_dw", u2, d_up, "tn")
    (d_yy, d_x_res), (gp_["post1_g"], gp_["pre2_g"]) = _rows_bwd(
        "post1_bwd", _fn_post1, [], post1_rows, post1_params, [[D], [D]], [d_h1, d_u2], dtypes=[bf16, f32])
    d_merged = _matmul("out_o_dx", d_yy, w["w_o"], "nt")
    gw["w_o"] = _matmul("out_o_dw", merged, d_yy, "tn")
    (d_a_fox, d_a_rwkv, d_a_mem, d_p_g), _ = _rows_bwd("merge_bwd", _fn_merge, [], merge_rows, [], [[D]], [d_merged],
                                                       dtypes=[bf16] * 4)
    d_fox_out = _matmul("out_fox_dx", d_a_fox, w["w_fox_out"], "nt")
    gw["w_fox_out"] = _matmul("out_fox_dw", fox_out, d_a_fox, "tn")
    d_rwkv_out = _matmul("out_rwkv_dx", d_a_rwkv, w["w_rwkv_out"], "nt")
    gw["w_rwkv_out"] = _matmul("out_rwkv_dw", rwkv_out, d_a_rwkv, "tn")
    d_mem_out = _matmul("out_mem_dx", d_a_mem, w["w_mem_out"], "nt")
    gw["w_mem_out"] = _matmul("out_mem_dw", mem_out, d_a_mem, "tn")

    d_p_mq, d_km, d_vm = _mem_bwd(p_mq, mem_kv, d_mem_out, batch, seq)
    d_mem_kv = jnp.concatenate([d_km, d_vm], axis=1).astype(bf16)
    gw["w_mem_kv"] = _matmul("proj_memkv_dw", memn, d_mem_kv, "tn")
    d_memn = _matmul("proj_memkv_dx", d_mem_kv, w["w_mem_kv"], "nt")
    _, (gp_["mem_norm_g"],) = _rows_bwd("rms_mem_bwd", _fn_rms, [], [(mem2, [D])], [p["mem_norm_g"]], [[D]], [d_memn])

    (d_q, d_k, d_v, d_cq, d_ck), _ = _fox_bwd(p_qkv, c, c_rows, fox_o, lse, d_fox_out, batch, seq)
    d_p_qkv = jnp.concatenate([d_q, d_k, d_v], axis=1).astype(bf16)

    def c_layout(dc):
        return _pad_cols(dc[:, :2].reshape(batch, HEADS, seq).transpose(0, 2, 1).reshape(t, HEADS), 128)

    d_p_f, d_bias = _fox_gate_bwd(p_f, bias, c_layout(d_cq), c_layout(d_ck), batch, seq)
    gp_["fox_f_bias"] = d_bias[:, :HEADS]

    (d_y_rw, d_main6_post, d_g_rw), (gp_["rwkv_gn_g"], gp_["rwkv_gn_b"], d_rk) = _rows_bwd(
        "rwkv_post_bwd", fn_post, post_consts, post_rows, post_params, [[HW]], [d_rwkv_out])
    gp_["rwkv_r_k"] = d_rk.reshape(1, HEADS, HD)
    d_main6, early_got = _scan_bwd(main6, states, d_y_rw, d_main6_post, batch, seq,
                                   side=(early(gw), True) if early else None)

    def fn_pre_sum(*args):
        return _fn_rwkv_pre(*args)

    (d_ps,), d_pre = _rows_bwd("rwkv_pre_bwd", fn_pre_sum, [], [(ps, rw_widths)], pre_params, [six, [HW]],
                               [d_main6, d_g_rw])
    gp_["rwkv_w0"], d_w_up, gp_["rwkv_a0"], d_a_up, gw["rwkv_g_up"], gp_["rwkv_k_k"], gp_["rwkv_k_a"] = d_pre
    gw["rwkv_w_up"], gw["rwkv_a_up"] = d_w_up[:64], d_a_up[:64]
    d_p_r, d_mu = _tokshift_bwd(p_r, mu, d_ps, batch, seq)
    gp_["rwkv_mu"] = _unpad_lora(d_mu)

    d_u = _matmul("proj_qkv_dx", d_p_qkv, w_qkv, "nt")
    d_u = _matmul("proj_f_dx", d_p_f, w_f, "nt", add=d_u)
    d_u = _matmul("proj_rwkv_dx", d_p_r, w_r, "nt", add=d_u)
    d_u = _matmul("proj_memq_dx", d_p_mq, w_mq, "nt", add=d_u)
    d_u = _matmul("proj_gate_dx", d_p_g, w_g3, "nt", add=d_u)
    gw["w_in"] = _merge_w_in(_matmul("proj_qkv_dw", u, d_p_qkv, "tn"), _matmul("proj_f_dw", u, d_p_f, "tn"),
                             _matmul("proj_rwkv_dw", u, d_p_r, "tn"), _matmul("proj_memq_dw", u, d_p_mq, "tn"),
                             _matmul("proj_gate_dw", u, d_p_g, "tn"))
    (d_x,), (gp_["pre1_g"],) = _rows_bwd("rms_pre1_bwd", _fn_rms, [], [(x2, [D])], [p["pre1_g"]], [[D]], [d_u], add=d_x_res)
    return loss, d_x.reshape(x.shape), gw, gp_, early_got


def _rows_add(name, a, b):
    (s,) = _rows_fwd(name, lambda u, v: (u + v,), [], [(a, [a.shape[1]]), (b, [b.shape[1]])], [], [[a.shape[1]]])
    return s


def _adamw(name, recv, w, m, v):
    rows, cols = w.shape
    tr = max(t for t in range(16, min(rows, 128) + 1, 16) if rows % t == 0)

    def body(g_ref, w_ref, m_ref, v_ref, go_ref, d_ref, mo_ref, vo_ref):
        g = g_ref[0].astype(f32)
        for s in range(1, N_DEV):
            g = g + g_ref[s].astype(f32)
        m_new = ADAM_B1 * m_ref[...] + (1.0 - ADAM_B1) * g
        v_new = ADAM_B2 * v_ref[...] + (1.0 - ADAM_B2) * (g * g)
        m_hat = m_new / (1.0 - ADAM_B1 ** ADAM_STEP)
        v_hat = v_new / (1.0 - ADAM_B2 ** ADAM_STEP)
        go_ref[...] = g
        d_ref[...] = -ADAM_LR * (m_hat / (jnp.sqrt(v_hat) + ADAM_EPS) + ADAM_WD * w_ref[...])
        mo_ref[...] = m_new
        vo_ref[...] = v_new

    spec = pl.BlockSpec((tr, cols), lambda i: (i, 0))
    return pl.pallas_call(
        body, name=name, grid=(rows // tr,),
        in_specs=[pl.BlockSpec((N_DEV, tr, cols), lambda i: (0, i, 0)), spec, spec, spec],
        out_specs=[spec] * 4, out_shape=[jax.ShapeDtypeStruct(w.shape, f32)] * 4,
        compiler_params=_cp(("parallel",)),
    )(recv, w, m, v)


GROUPS = (
    ("in", ("w_in",), 1),
    ("memkv", ("w_mem_kv",), 0),
    ("ffn_gu", ("w_ffn_gate", "w_ffn_up"), 1),
    ("down_o", ("w_ffn_down", "w_o"), 0),
    ("outs", ("w_fox_out", "w_rwkv_out", "w_mem_out"), 1),
    ("lora", ("rwkv_w_up", "rwkv_a_up", "rwkv_g_up"), 0),
)
FIRST_GROUPS = ("in", "memkv")
LATE_GROUPS = (("down_o", "outs", "lora"), ("ffn_gu",))
EARLY_GRAD_GROUPS = ("memkv", "ffn_gu", "down_o", "outs")
LAST_GRAD_GROUPS = ("in", "lora")
SHARD_AXIS = {n: a for n, _, a in SHARDED}
SMALL_ROWS = 16


def _group_local(shards, members, join):
    parts = [shards[n].reshape(shards[n].shape[-2:]) for n in members]
    return parts[0] if len(parts) == 1 else jnp.concatenate(parts, axis=join)


def _group_split(arr, members, join, lead=False):
    out, off = {}, 0
    for n in members:
        shape = dict((k, s) for k, s, _ in SHARDED)[n]
        size = _block_shape(shape, SHARD_AXIS[n])[join]
        idx = [slice(None)] * arr.ndim
        idx[arr.ndim - 2 + join] = slice(off, off + size)
        out[n] = arr[tuple(idx)]
        off += size
    return out


def _full_from_blocks(blocks, axis):
    if axis == 0:
        return blocks.reshape(-1, blocks.shape[2])
    return blocks.transpose(1, 0, 2).reshape(blocks.shape[1], -1)


def _blocks_from_full(full, axis):
    if axis == 0:
        return full.reshape(N_DEV, -1, full.shape[1])
    return full.reshape(full.shape[0], N_DEV, -1).transpose(1, 0, 2)


def _assemble(gathered, names):
    out = {}
    for arr, g in zip(gathered, names):
        _, members, join = [grp for grp in GROUPS if grp[0] == g][0]
        for n, blk in _group_split(arr, members, join, lead=True).items():
            out[n] = _full_from_blocks(blk, SHARD_AXIS[n])
    return out


def _grad_blocks(gw, names):
    out = []
    for g in names:
        _, members, join = [grp for grp in GROUPS if grp[0] == g][0]
        parts = [_blocks_from_full(gw[n].astype(bf16), SHARD_AXIS[n]) for n in members]
        out.append(parts[0] if len(parts) == 1 else jnp.concatenate(parts, axis=1 + join))
    return out


def _small_pack(d):
    flat = jnp.concatenate([d[n].reshape(-1) for n, _ in REPLICATED])
    return jnp.pad(flat, (0, SMALL_ROWS * LANES - REPL_ELEMS)).reshape(SMALL_ROWS, LANES)


def _small_unpack(packed):
    out, flat, off = {}, packed.reshape(-1), 0
    for n, shape in REPLICATED:
        k = _rows_of((LANES,) + shape)
        out[n] = flat[off:off + k].reshape(shape)
        off += k
    return out


def kernel(x, mem, pre1_g, post1_g, pre2_g, post2_g, mem_norm_g, w_in, fox_f_bias, rwkv_mu, rwkv_w0, rwkv_w_up, rwkv_a0, rwkv_a_up, rwkv_g_up, rwkv_k_k, rwkv_k_a, rwkv_r_k, rwkv_gn_g, rwkv_gn_b, w_mem_kv, w_fox_out, w_rwkv_out, w_mem_out, w_o, w_ffn_gate, w_ffn_up, w_ffn_down, loss_target, m_pre1_g, m_post1_g, m_pre2_g, m_post2_g, m_mem_norm_g, m_w_in, m_fox_f_bias, m_rwkv_mu, m_rwkv_w0, m_rwkv_w_up, m_rwkv_a0, m_rwkv_a_up, m_rwkv_g_up, m_rwkv_k_k, m_rwkv_k_a, m_rwkv_r_k, m_rwkv_gn_g, m_rwkv_gn_b, m_w_mem_kv, m_w_fox_out, m_w_rwkv_out, m_w_mem_out, m_w_o, m_w_ffn_gate, m_w_ffn_up, m_w_ffn_down, v_pre1_g, v_post1_g, v_pre2_g, v_post2_g, v_mem_norm_g, v_w_in, v_fox_f_bias, v_rwkv_mu, v_rwkv_w0, v_rwkv_w_up, v_rwkv_a0, v_rwkv_a_up, v_rwkv_g_up, v_rwkv_k_k, v_rwkv_k_a, v_rwkv_r_k, v_rwkv_gn_g, v_rwkv_gn_b, v_w_mem_kv, v_w_fox_out, v_w_rwkv_out, v_w_mem_out, v_w_o, v_w_ffn_gate, v_w_ffn_up, v_w_ffn_down):
    args = dict(locals())
    wts = {n: args[n] for n in WEIGHT_ORDER}
    ms = {n: args["m_" + n] for n in WEIGHT_ORDER}
    vs = {n: args["v_" + n] for n in WEIGHT_ORDER}

    groups = {g: (members, join) for g, members, join in GROUPS}
    w_bf16 = {n: wts[n].astype(bf16) for n, _, _ in SHARDED}

    def send(g):
        return _group_local(w_bf16, *groups[g])

    first = _exchange("gather_first", [send(g) for g in FIRST_GROUPS], per_peer=False)
    full = _assemble(first, FIRST_GROUPS)
    small_in = {n: (wts[n] if n == "rwkv_r_k" else wts[n].reshape(wts[n].shape[-2:])) for n, _ in REPLICATED}
    late = ([send(g) for g in LATE_GROUPS[0]], [send(g) for g in LATE_GROUPS[1]],
            lambda got, which: _assemble(got, LATE_GROUPS[which]))
    loss_part, grad_x, gw, gp, early_got = _local_step(
        x, mem, loss_target, full, small_in, late=late, early=lambda g: _grad_blocks(g, EARLY_GRAD_GROUPS))

    small_send = jnp.broadcast_to(_small_pack(gp).astype(bf16)[None], (N_DEV, SMALL_ROWS, LANES))
    *last_got, small_got = _exchange("exchange_last", _grad_blocks(gw, LAST_GRAD_GROUPS) + [small_send], per_peer=True)
    received = dict(zip(EARLY_GRAD_GROUPS + LAST_GRAD_GROUPS, list(early_got) + list(last_got)))

    outs = [{}, {}, {}, {}]
    for g, members, join in GROUPS:
        res = _adamw("adamw_" + g, received[g], *[_group_local(d, members, join) for d in (wts, ms, vs)])
        for o, arr in zip(outs, res):
            o.update(_group_split(arr, members, join))
    res = _adamw("adamw_small", small_got, *[_small_pack(d) for d in (wts, ms, vs)])
    for o, arr in zip(outs, res):
        o.update(_small_unpack(arr))
    loss = lax.psum(loss_part[0, 0], ("x", "y", "c"))
    return (loss, grad_x, *[o[n].reshape(wts[n].shape) for o in outs for n in WEIGHT_ORDER])
```

```python
import functools

import jax
import jax.numpy as jnp
from jax import lax
from jax.experimental import pallas as pl
from jax.experimental.pallas import tpu as pltpu

f32 = jnp.float32
bf16 = jnp.bfloat16
_HI = lax.Precision.HIGHEST

D = 1024
HEADS = 8
HD = 64
HW = HEADS * HD
MEM_HEADS = 4
MEM_HD = 128
MEM_W = 512
MEM_LEN = 256
D_FF = 2816
LORA_PAD = 128
RW_COLS = 3 * HW + 3 * LORA_PAD
NORM_EPS = 1e-6
GN_EPS = 64e-5
Q_BLOCK = 128
SCAN_CHUNK = 64
N_DEV = 8
LANES = 1024
VMEM_LIMIT = 56 * 1024 * 1024

ADAM_LR = 0.001
ADAM_B1 = 0.9
ADAM_B2 = 0.999
ADAM_EPS = 1e-08
ADAM_WD = 0.01
ADAM_STEP = 10

SHARDED = (
    ("w_in", (1024, 6920), 1),
    ("w_ffn_gate", (1024, 2816), 1),
    ("w_ffn_up", (1024, 2816), 1),
    ("w_ffn_down", (2816, 1024), 0),
    ("w_mem_kv", (1024, 1024), 0),
    ("w_o", (1024, 1024), 0),
    ("w_fox_out", (512, 1024), 1),
    ("w_rwkv_out", (512, 1024), 1),
    ("w_mem_out", (512, 1024), 1),
    ("rwkv_w_up", (64, 512), 1),
    ("rwkv_a_up", (64, 512), 1),
    ("rwkv_g_up", (128, 512), 1),
)
REPLICATED = (
    ("pre1_g", (1, 1024)), ("post1_g", (1, 1024)), ("pre2_g", (1, 1024)), ("post2_g", (1, 1024)),
    ("mem_norm_g", (1, 1024)), ("fox_f_bias", (1, 8)), ("rwkv_mu", (1, 1792)), ("rwkv_w0", (1, 512)),
    ("rwkv_a0", (1, 512)), ("rwkv_k_k", (1, 512)), ("rwkv_k_a", (1, 512)), ("rwkv_r_k", (1, 8, 64)),
    ("rwkv_gn_g", (1, 512)), ("rwkv_gn_b", (1, 512)),
)
WEIGHT_ORDER = ('pre1_g', 'post1_g', 'pre2_g', 'post2_g', 'mem_norm_g', 'w_in', 'fox_f_bias', 'rwkv_mu',
                'rwkv_w0', 'rwkv_w_up', 'rwkv_a0', 'rwkv_a_up', 'rwkv_g_up', 'rwkv_k_k', 'rwkv_k_a',
                'rwkv_r_k', 'rwkv_gn_g', 'rwkv_gn_b', 'w_mem_kv', 'w_fox_out', 'w_rwkv_out', 'w_mem_out',
                'w_o', 'w_ffn_gate', 'w_ffn_up', 'w_ffn_down')


def _block_shape(shape, axis):
    return tuple(s // N_DEV if i == axis else s for i, s in enumerate(shape))


def _rows_of(shape):
    n = 1
    for s in shape:
        n *= s
    return n // LANES


SHARD_ROWS = sum(_rows_of(_block_shape(s, a)) for _, s, a in SHARDED)
REPL_ELEMS = sum(_rows_of((LANES,) + s) for _, s in REPLICATED)
REPL_ROWS = -(-REPL_ELEMS // LANES)
PACK_ROWS = -(-(SHARD_ROWS + REPL_ROWS) // 128) * 128
GATHER_ROWS = -(-SHARD_ROWS // 16) * 16


def _cp(sem=None):
    return pltpu.CompilerParams(dimension_semantics=sem, vmem_limit_bytes=VMEM_LIMIT)


def _tile(dim, cap):
    best = None
    for t in range(128, min(dim, cap) + 1, 128):
        if dim % t == 0:
            best = t
    return best if best is not None else dim


def _two_terms(x):
    hi = x.astype(bf16)
    return hi, (x - hi.astype(f32)).astype(bf16)


def _dg(a, b, dims, exact):
    if exact == "split":
        (a_hi, a_lo), (b_hi, b_lo) = _two_terms(a), _two_terms(b)
        dot = functools.partial(lax.dot_general, dimension_numbers=dims, preferred_element_type=f32)
        return dot(a_hi, b_hi) + (dot(a_hi, b_lo) + dot(a_lo, b_hi))
    if exact:
        return lax.dot_general(a, b, dims, precision=_HI, preferred_element_type=f32)
    return lax.dot_general(a.astype(bf16), b.astype(bf16), dims, preferred_element_type=f32)


def _make_mm(batched, exact):
    o = 1 if batched else 0
    bd = ((0,), (0,)) if batched else ((), ())
    d_nn = (((1 + o,), (o,)), bd)
    d_nt = (((1 + o,), (1 + o,)), bd)
    d_tn = (((o,), (o,)), bd)

    @jax.custom_vjp
    def nn(a, b):
        return _dg(a, b, d_nn, exact)

    @jax.custom_vjp
    def nt(a, b):
        return _dg(a, b, d_nt, exact)

    @jax.custom_vjp
    def tn(a, b):
        return _dg(a, b, d_tn, exact)

    nn.defvjp(lambda a, b: (_dg(a, b, d_nn, exact), (a, b)),
              lambda res, g: (_dg(g, res[1], d_nt, exact), _dg(res[0], g, d_tn, exact)))
    nt.defvjp(lambda a, b: (_dg(a, b, d_nt, exact), (a, b)),
              lambda res, g: (_dg(g, res[1], d_nn, exact), _dg(g, res[0], d_tn, exact)))
    tn.defvjp(lambda a, b: (_dg(a, b, d_tn, exact), (a, b)),
              lambda res, g: (_dg(res[1], g, d_nt, exact), _dg(res[0], g, d_nn, exact)))
    return nn, nt, tn


def _sigmoid(x):
    return 1.0 / (1.0 + jnp.exp(-x))


def _head_sum_raw(x):
    width = 2 * HD
    i = lax.broadcasted_iota(jnp.int32, (width, width), 0) // HD
    j = lax.broadcasted_iota(jnp.int32, (width, width), 1) // HD
    m = (i == j).astype(bf16)
    dims = (((1,), (0,)), ((), ()))
    out = []
    for p in range(x.shape[1] // width):
        xp = x[:, p * width:(p + 1) * width]
        hi = xp.astype(bf16)
        lo = (xp - hi.astype(f32)).astype(bf16)
        out.append(lax.dot_general(hi, m, dims, preferred_element_type=f32)
                   + lax.dot_general(lo, m, dims, preferred_element_type=f32))
    return jnp.concatenate(out, axis=1)


@jax.custom_vjp
def _head_sum(x):
    return _head_sum_raw(x)


_head_sum.defvjp(lambda x: (_head_sum_raw(x), None), lambda _, g: (_head_sum_raw(g),))


WEIGHT_TILE_BYTES = 13 * 512 * 1024
ACC_TILE_BYTES = 8 * 1024 * 1024


def _matmul(name, a, b, mode, add=None, out_dtype=f32):
    has_add = add is not None
    if mode == "tn":
        (k, m), (_, n) = a.shape, b.shape
        tn = _tile(n, max(128, ACC_TILE_BYTES // (4 * m)))
        tk = _tile(k, 1024)

        def body(a_ref, b_ref, o_ref):
            @pl.when(pl.program_id(1) == 0)
            def _():
                o_ref[...] = jnp.zeros_like(o_ref)

            o_ref[...] += lax.dot_general(a_ref[...].astype(bf16), b_ref[...].astype(bf16),
                                          (((0,), (0,)), ((), ())), preferred_element_type=f32)

        return pl.pallas_call(
            body, name=name, grid=(n // tn, k // tk),
            in_specs=[pl.BlockSpec((tk, m), lambda j, kk: (kk, 0)), pl.BlockSpec((tk, tn), lambda j, kk: (kk, j))],
            out_specs=pl.BlockSpec((m, tn), lambda j, kk: (0, j)), out_shape=jax.ShapeDtypeStruct((m, n), f32),
            compiler_params=_cp(("parallel", "arbitrary")),
        )(a, b)

    (m, k) = a.shape
    n = b.shape[1] if mode == "nn" else b.shape[0]
    tm = _tile(m, 512)
    tn = _tile(n, max(128, WEIGHT_TILE_BYTES // (2 * k)))
    dims = (((1,), (0,)), ((), ())) if mode == "nn" else (((1,), (1,)), ((), ()))
    b_spec = pl.BlockSpec((k, tn), lambda j, i: (0, j)) if mode == "nn" else pl.BlockSpec((tn, k), lambda j, i: (j, 0))
    o_spec = pl.BlockSpec((tm, tn), lambda j, i: (i, j))

    def body(*refs):
        a_ref, b_ref = refs[0], refs[1]
        o_ref = refs[-1]
        r = lax.dot_general(a_ref[...].astype(bf16), b_ref[...].astype(bf16), dims, preferred_element_type=f32)
        if has_add:
            r = r + refs[2][...]
        o_ref[...] = r.astype(o_ref.dtype)

    return pl.pallas_call(
        body, name=name, grid=(n // tn, m // tm),
        in_specs=[pl.BlockSpec((tm, k), lambda j, i: (i, 0)), b_spec] + ([o_spec] if has_add else []),
        out_specs=o_spec, out_shape=jax.ShapeDtypeStruct((m, n), out_dtype),
        compiler_params=_cp(("parallel", "arbitrary")),
    )(*((a, b, add) if has_add else (a, b)))


def _pieces(ref, widths):
    out, off = [], 0
    for w in widths:
        out.append(ref[:, off:off + w].astype(f32))
        off += w
    return out


def _store_pieces(ref, widths, vals, add_ref=None):
    off = 0
    for w, v in zip(widths, vals):
        ref[:, off:off + w] = (v if add_ref is None else v + add_ref[:, off:off + w]).astype(ref.dtype)
        off += w


def _rows_fwd(name, fn, consts, rows, params, outs, n_sums=0, tm=256, dtypes=None):
    t = (consts + rows)[0][0].shape[0]
    tm = min(tm, t)
    ins = consts + rows
    n_in, n_p, n_o = len(ins), len(params), len(outs)
    dtypes = dtypes or [f32] * n_o

    def body(*refs):
        in_refs, p_refs = refs[:n_in], refs[n_in:n_in + n_p]
        o_refs, s_refs = refs[n_in + n_p:n_in + n_p + n_o], refs[n_in + n_p + n_o:]
        vals = []
        for r, (_, widths) in zip(in_refs, ins):
            vals += _pieces(r, widths)
        res = fn(*vals, *[p[...] for p in p_refs])
        pos = 0
        for r, widths in zip(o_refs, outs):
            _store_pieces(r, widths, res[pos:pos + len(widths)])
            pos += len(widths)

        @pl.when(pl.program_id(0) == 0)
        def _():
            for s in s_refs:
                s[...] = jnp.zeros_like(s)

        for s, v in zip(s_refs, res[pos:]):
            s[...] += v

    row_spec = lambda w: pl.BlockSpec((tm, w), lambda i: (i, 0))
    full = lambda p: pl.BlockSpec(p.shape, lambda i: (0,) * p.ndim)
    return pl.pallas_call(
        body, name=name, grid=(t // tm,),
        in_specs=[row_spec(a.shape[1]) for a, _ in ins] + [full(p) for p in params],
        out_specs=[row_spec(sum(w)) for w in outs] + [pl.BlockSpec((1, 1), lambda i: (0, 0))] * n_sums,
        out_shape=[jax.ShapeDtypeStruct((t, sum(w)), dt) for w, dt in zip(outs, dtypes)] + [jax.ShapeDtypeStruct((1, 1), f32)] * n_sums,
        compiler_params=_cp(("arbitrary",)),
    )(*[a for a, _ in ins], *params)


def _rows_bwd(name, fn, consts, rows, params, outs, cts, n_sums=0, add=None, tm=256, dtypes=None):
    t = (consts + rows)[0][0].shape[0]
    tm = min(tm, t)
    n_c, n_r, n_p, n_o = len(consts), len(rows), len(params), len(outs)
    has_add = add is not None
    dtypes = dtypes or [f32] * n_r

    def body(*refs):
        pos = 0
        c_refs = refs[pos:pos + n_c]; pos += n_c
        r_refs = refs[pos:pos + n_r]; pos += n_r
        p_refs = refs[pos:pos + n_p]; pos += n_p
        ct_refs = refs[pos:pos + n_o]; pos += n_o
        add_ref = refs[pos] if has_add else None
        pos += 1 if has_add else 0
        dr_refs = refs[pos:pos + n_r]; pos += n_r
        dp_refs = refs[pos:pos + n_p]
        cvals, rvals = [], []
        for r, (_, widths) in zip(c_refs, consts):
            cvals += _pieces(r, widths)
        for r, (_, widths) in zip(r_refs, rows):
            rvals += _pieces(r, widths)
        pvals = [p[...] for p in p_refs]
        ctv = []
        for r, widths in zip(ct_refs, outs):
            ctv += _pieces(r, widths)
        ctv += [jnp.ones((1, 1), f32)] * n_sums
        _, vjp = jax.vjp(lambda *rp: tuple(fn(*cvals, *rp)), *rvals, *pvals)
        g = vjp(tuple(ctv))
        pos = 0
        for idx, (r, (_, widths)) in enumerate(zip(dr_refs, rows)):
            _store_pieces(r, widths, g[pos:pos + len(widths)], add_ref if idx == 0 else None)
            pos += len(widths)

        @pl.when(pl.program_id(0) == 0)
        def _():
            for dp in dp_refs:
                dp[...] = jnp.zeros_like(dp)

        for dp, v in zip(dp_refs, g[pos:]):
            dp[...] += v

    row_spec = lambda w: pl.BlockSpec((tm, w), lambda i: (i, 0))
    full = lambda p: pl.BlockSpec(p.shape, lambda i: (0,) * p.ndim)
    args = [a for a, _ in consts + rows] + list(params) + list(cts) + ([add] if has_add else [])
    res = pl.pallas_call(
        body, name=name, grid=(t // tm,),
        in_specs=[row_spec(a.shape[1]) for a, _ in consts + rows] + [full(p) for p in params]
        + [row_spec(sum(w)) for w in outs] + ([row_spec(add.shape[1])] if has_add else []),
        out_specs=[row_spec(a.shape[1]) for a, _ in rows] + [full(p) for p in params],
        out_shape=[jax.ShapeDtypeStruct(a.shape, dt) for (a, _), dt in zip(rows, dtypes)]
        + [jax.ShapeDtypeStruct(p.shape, f32) for p in params],
        compiler_params=_cp(("arbitrary",)),
    )(*args)
    return res[:n_r], res[n_r:]


def _rms(x, g):
    return x * lax.rsqrt(jnp.mean(x * x, axis=-1, keepdims=True) + NORM_EPS) * g


def _fn_rms(x, g):
    return (_rms(x, g),)


def _fn_rwkv_pre(r, k, v, wd, ad, gd, w0, w_up, a0, a_up, g_up, k_k, k_a):
    nn, _, _ = _make_mm(False, False)
    w_log = -_sigmoid(w0 + nn(jnp.tanh(wd), w_up)) * 0.6065306597126334
    a = _sigmoid(a0 + nn(ad, a_up))
    g = nn(_sigmoid(gd), g_up)
    kk = k * k_k
    kk = kk * lax.rsqrt(jnp.maximum(_head_sum(kk * kk), 1e-24))
    k2 = k * (1.0 + (a - 1.0) * k_a)
    return r, w_log, k2, v, -kk, kk * a, g


def _fn_rwkv_post(y, r, k2, v, g, gn_g, gn_b, r_k):
    mean = _head_sum(y) * (1.0 / HD)
    yc = y - mean
    var = _head_sum(yc * yc) * (1.0 / HD)
    yn = yc * lax.rsqrt(var + GN_EPS) * gn_g + gn_b
    bonus = _head_sum(r * k2 * r_k) * v
    return ((yn + bonus) * g,)


def _fn_merge(a_fox, a_rwkv, a_mem, g_fox, g_rwkv, g_mem):
    return (_sigmoid(g_fox) * a_fox + _sigmoid(g_rwkv) * a_rwkv + _sigmoid(g_mem) * a_mem,)


def _fn_post1(y, x, post1_g, pre2_g):
    h1 = x + _rms(y, post1_g)
    return h1, _rms(h1, pre2_g)


def _fn_swiglu(gp, up):
    return (gp * _sigmoid(gp) * up,)


def _fn_final(target, ffn, h1, post2_g):
    err = h1 + _rms(ffn, post2_g) - target
    per_row = jnp.mean(err * err, axis=-1, keepdims=True)
    return (0.5 * jnp.sum(per_row, axis=0, keepdims=True),)


def _shift_down(x):
    row = lax.broadcasted_iota(jnp.int32, x.shape, 0)
    return jnp.where(row == 0, 0.0, pltpu.roll(x, 1, 0))


def _shift_up(x):
    s = x.shape[0]
    row = lax.broadcasted_iota(jnp.int32, x.shape, 0)
    return jnp.where(row == s - 1, 0.0, pltpu.roll(x, s - 1, 0))


def _tokshift_fwd(p, mu, batch, seq):
    w = p.shape[1]
    tc = _tile(w, 384)

    def body(p_ref, mu_ref, o_ref):
        x = p_ref[...]
        o_ref[...] = x + (_shift_down(x) - x) * mu_ref[...]

    return pl.pallas_call(
        body, name="tokshift_fwd", grid=(w // tc, batch),
        in_specs=[pl.BlockSpec((seq, tc), lambda j, b: (b, j)), pl.BlockSpec((1, tc), lambda j, b: (0, j))],
        out_specs=pl.BlockSpec((seq, tc), lambda j, b: (b, j)),
        out_shape=jax.ShapeDtypeStruct(p.shape, f32),
        compiler_params=_cp(("parallel", "arbitrary")),
    )(p, mu)


def _tokshift_bwd(p, mu, dps, batch, seq):
    w = p.shape[1]
    tc = _tile(w, 384)

    def body(p_ref, mu_ref, d_ref, dp_ref, dmu_ref):
        x, mu_v, d = p_ref[...], mu_ref[...], d_ref[...]
        dp_ref[...] = (d * (1.0 - mu_v) + _shift_up(d * mu_v)).astype(dp_ref.dtype)

        @pl.when(pl.program_id(1) == 0)
        def _():
            dmu_ref[...] = jnp.zeros_like(dmu_ref)

        dmu_ref[...] += jnp.sum(d * (_shift_down(x) - x), axis=0, keepdims=True)

    return pl.pallas_call(
        body, name="tokshift_bwd", grid=(w // tc, batch),
        in_specs=[pl.BlockSpec((seq, tc), lambda j, b: (b, j)), pl.BlockSpec((1, tc), lambda j, b: (0, j)),
                  pl.BlockSpec((seq, tc), lambda j, b: (b, j))],
        out_specs=[pl.BlockSpec((seq, tc), lambda j, b: (b, j)), pl.BlockSpec((1, tc), lambda j, b: (0, j))],
        out_shape=[jax.ShapeDtypeStruct(p.shape, bf16), jax.ShapeDtypeStruct(mu.shape, f32)],
        compiler_params=_cp(("parallel", "arbitrary")),
    )(p, mu, dps)


def _cum_block(seq):
    return _tile(seq, 256)


def _fox_gate_fwd(f, bias, batch, seq):
    cb = _cum_block(seq)

    def body(f_ref, b_ref, c_ref):
        row = lax.broadcasted_iota(jnp.int32, (cb, cb), 0)
        col = lax.broadcasted_iota(jnp.int32, (cb, cb), 1)
        tri = (col <= row).astype(f32)
        carry = jnp.zeros((1, 128), f32)
        for i in range(seq // cb):
            z = f_ref[i * cb:(i + 1) * cb, :] + b_ref[...]
            ls = jnp.minimum(z, 0.0) - jnp.log(1.0 + jnp.exp(-jnp.abs(z)))
            c = _dg(tri, ls, (((1,), (0,)), ((), ())), True) + carry
            c_ref[i * cb:(i + 1) * cb, :] = c
            carry = c[cb - 1:cb, :]

    return pl.pallas_call(
        body, name="fox_gate_fwd", grid=(batch,),
        in_specs=[pl.BlockSpec((seq, 128), lambda b: (b, 0)), pl.BlockSpec((1, 128), lambda b: (0, 0))],
        out_specs=pl.BlockSpec((seq, 128), lambda b: (b, 0)),
        out_shape=jax.ShapeDtypeStruct(f.shape, f32),
        compiler_params=_cp(("arbitrary",)),
    )(f, bias)


def _fox_gate_bwd(f, bias, dc_a, dc_b, batch, seq):
    cb = _cum_block(seq)

    def body(f_ref, b_ref, da_ref, db_ref, df_ref, dbias_ref):
        row = lax.broadcasted_iota(jnp.int32, (cb, cb), 0)
        col = lax.broadcasted_iota(jnp.int32, (cb, cb), 1)
        triu = (col >= row).astype(f32)

        @pl.when(pl.program_id(0) == 0)
        def _():
            dbias_ref[...] = jnp.zeros_like(dbias_ref)

        lane = lax.broadcasted_iota(jnp.int32, (1, 128), 1)

        def by_head(blk):
            out = jnp.zeros((cb, 128), f32)
            for p in range(HEADS // 2):
                for e in range(2):
                    out = jnp.where(lane == 2 * p + e, _pick_lane(blk[:, p * 128:(p + 1) * 128], e), out)
            return out

        carry = jnp.zeros((1, 128), f32)
        tot = jnp.zeros((1, 128), f32)
        for i in reversed(range(seq // cb)):
            sl = slice(i * cb, (i + 1) * cb)
            dc = by_head(da_ref[sl, :] + db_ref[sl, :])
            dls = _dg(triu, dc, (((1,), (0,)), ((), ())), True) + carry
            carry = dls[0:1, :]
            df = dls * _sigmoid(-(f_ref[sl, :] + b_ref[...]))
            df_ref[sl, :] = df.astype(df_ref.dtype)
            tot = tot + jnp.sum(df, axis=0, keepdims=True)
        dbias_ref[...] += tot

    return pl.pallas_call(
        body, name="fox_gate_bwd", grid=(batch,),
        in_specs=[pl.BlockSpec((seq, 128), lambda b: (b, 0)), pl.BlockSpec((1, 128), lambda b: (0, 0)),
                  pl.BlockSpec((seq, HW), lambda b: (b, 0)), pl.BlockSpec((seq, HW), lambda b: (b, 0))],
        out_specs=[pl.BlockSpec((seq, 128), lambda b: (b, 0)), pl.BlockSpec((1, 128), lambda b: (0, 0))],
        out_shape=[jax.ShapeDtypeStruct(f.shape, bf16), jax.ShapeDtypeStruct((1, 128), f32)],
        compiler_params=_cp(("arbitrary",)),
    )(f, bias, dc_a, dc_b)


_HBM_SPEC = pl.BlockSpec(memory_space=pltpu.HBM)


def _side_out_shapes(srcs, per_peer):
    return [jax.ShapeDtypeStruct(((N_DEV,) + tuple(s.shape[1:] if per_peer else s.shape)), s.dtype) for s in srcs]


def _side_sems(n):
    if n == 0:
        return []
    return [pltpu.SemaphoreType.DMA((n, N_DEV - 1)), pltpu.SemaphoreType.DMA((n, N_DEV - 1)), pltpu.SemaphoreType.DMA((n,))]


def _peer_copies(src_refs, dst_refs, per_peer, sems):
    send_sems, recv_sems, local_sems = sems
    x, y, c = lax.axis_index("x"), lax.axis_index("y"), lax.axis_index("c")
    me = 4 * x + 2 * y + c
    copies = []
    for t, (s, d) in enumerate(zip(src_refs, dst_refs)):
        copies.append(pltpu.make_async_copy(s.at[me] if per_peer else s, d.at[me], local_sems.at[t]))
        for k in range(1, N_DEV):
            px = 1 - x if k & 4 else x
            py = 1 - y if k & 2 else y
            pc = 1 - c if k & 1 else c
            copies.append(pltpu.make_async_remote_copy(
                src_ref=s.at[4 * px + 2 * py + pc] if per_peer else s, dst_ref=d.at[me],
                send_sem=send_sems.at[t, k - 1], recv_sem=recv_sems.at[t, k - 1],
                device_id=(px, py, pc), device_id_type=pl.DeviceIdType.MESH))
    return copies


def _side_exchange(src_refs, dst_refs, per_peer, sems, *grid):
    if not src_refs:
        return
    first = functools.reduce(jnp.logical_and, [pl.program_id(a) == 0 for a in range(len(grid))])
    last = functools.reduce(jnp.logical_and, [pl.program_id(a) == n - 1 for a, n in enumerate(grid)])

    @pl.when(first)
    def _():
        for cp in _peer_copies(src_refs, dst_refs, per_peer, sems):
            cp.start()

    @pl.when(last)
    def _():
        for cp in _peer_copies(src_refs, dst_refs, per_peer, sems):
            cp.wait()


def _exchange(name, srcs, per_peer):
    n = len(srcs)

    def body(*refs):
        copies = _peer_copies(refs[:n], refs[n:2 * n], per_peer, refs[2 * n:])
        for cp in copies:
            cp.start()
        for cp in copies:
            cp.wait()

    return pl.pallas_call(
        body, name=name, in_specs=[_HBM_SPEC] * n, out_specs=[_HBM_SPEC] * n,
        out_shape=_side_out_shapes(srcs, per_peer), scratch_shapes=_side_sems(n),
    )(*srcs)


FOX_T = 512
_NEG = -1e30
_D2 = (((1,), (1,)), ((), ()))
_D1 = (((1,), (0,)), ((), ()))
_D0 = (((0,), (0,)), ((), ()))


def _bdot(a, b, dims):
    return lax.dot_general(a.astype(bf16), b.astype(bf16), dims, preferred_element_type=f32)


def _pick_lane(x, lane):
    idx = lax.broadcasted_iota(jnp.int32, x.shape, 1)
    return jnp.sum(jnp.where(idx == lane, x, 0.0), axis=1, keepdims=True)


def _pick_row(x, row):
    idx = lax.broadcasted_iota(jnp.int32, x.shape, 0)
    return jnp.sum(jnp.where(idx == row, x, 0.0), axis=0, keepdims=True)


def _fox_fwd(qkv, c, c_rows, batch, seq, side=None):
    t = min(FOX_T, seq)
    nq = seq // t
    scale = HD ** -0.5
    srcs, per_peer = side if side is not None else ([], False)
    n_s = len(srcs)

    def body(*refs):
        q_ref, k_ref, v_ref, cq_ref, ck_ref = refs[:5]
        o_ref, lse_ref = refs[5 + n_s:7 + n_s]
        _side_exchange(refs[5:5 + n_s], refs[7 + n_s:7 + 2 * n_s], per_peer, refs[7 + 2 * n_s:], batch, PAIRS, nq)
        pair, i = pl.program_id(1), pl.program_id(2)
        lane = lax.broadcasted_iota(jnp.int32, (1, PAIR_W), 1)
        first = (lane // HD) == 0
        q = q_ref[...] * scale
        qs = [jnp.where(first, q, 0.0), jnp.where(first, 0.0, q)]
        cqs = [_pick_lane(cq_ref[...], 2 * pair + e) for e in range(2)]
        qidx = i * t + lax.broadcasted_iota(jnp.int32, (t, t), 0)

        def step(j, carry):
            rows = pl.ds(pl.multiple_of(j * t, t), t)
            kj, vj = k_ref[rows, :], v_ref[rows, :]
            ck_blk = ck_ref[0, :, rows]
            vis = (j * t + lax.broadcasted_iota(jnp.int32, (t, t), 1)) <= qidx
            out = []
            for e in range(2):
                m, l, acc = carry[3 * e:3 * e + 3]
                s = _bdot(qs[e], kj, _D2) + (cqs[e] - _pick_row(ck_blk, 2 * pair + e))
                s = jnp.where(vis, s, _NEG)
                m_new = jnp.maximum(m, jnp.max(s, axis=1, keepdims=True))
                alpha = jnp.exp(m - m_new)
                p = jnp.exp(s - m_new)
                out += [m_new, alpha * l + jnp.sum(p, axis=1, keepdims=True), alpha * acc + _bdot(p, vj, _D1)]
            return tuple(out)

        init = (jnp.full((t, 1), _NEG, f32), jnp.zeros((t, 1), f32), jnp.zeros((t, PAIR_W), f32)) * 2
        m0, l0, a0, m1, l1, a1 = lax.fori_loop(0, i + 1, step, init)
        o_ref[...] = jnp.where(first, a0 / l0, a1 / l1)
        lse_ref[...] = jnp.where(lane == 0, m0 + jnp.log(l0), jnp.where(lane == 1, m1 + jnp.log(l1), 0.0))

    q_spec = pl.BlockSpec((t, PAIR_W), lambda b, p, i: (b * nq + i, p))
    res = pl.pallas_call(
        body, name="fox_attn_fwd", grid=(batch, PAIRS, nq),
        in_specs=[q_spec,
                  pl.BlockSpec((seq, PAIR_W), lambda b, p, i: (b, PAIRS + p)),
                  pl.BlockSpec((seq, PAIR_W), lambda b, p, i: (b, 2 * PAIRS + p)),
                  pl.BlockSpec((t, 128), lambda b, p, i: (b * nq + i, 0)),
                  pl.BlockSpec((1, 8, seq), lambda b, p, i: (b, 0, 0))] + [_HBM_SPEC] * n_s,
        out_specs=[q_spec, q_spec] + [_HBM_SPEC] * n_s,
        out_shape=[jax.ShapeDtypeStruct((batch * seq, HW), f32)] * 2 + _side_out_shapes(srcs, per_peer),
        scratch_shapes=_side_sems(n_s),
        compiler_params=_cp(("arbitrary", "arbitrary", "arbitrary")),
    )(qkv, qkv, qkv, c, c_rows, *srcs)
    return res[0], res[1], list(res[2:])


def _fox_bwd(qkv, c, c_rows, o, lse, do, batch, seq, side=None):
    t = min(FOX_T, seq)
    nq = seq // t
    scale = HD ** -0.5
    srcs, per_peer = side if side is not None else ([], False)
    n_s = len(srcs)

    def body(*refs):
        q_ref, k_ref, v_ref, cq_ref, ck_ref, o_ref, lse_ref, do_ref = refs[:8]
        dq_ref, dk_ref, dv_ref, dcq_ref, dck_ref = refs[8 + n_s:13 + n_s]
        _side_exchange(refs[8:8 + n_s], refs[13 + n_s:13 + 2 * n_s], per_peer, refs[13 + 2 * n_s:], batch, PAIRS, nq)
        pair, j = pl.program_id(1), pl.program_id(2)

        @pl.when(j == 0)
        def _():
            dq_ref[...] = jnp.zeros_like(dq_ref)
            dcq_ref[...] = jnp.zeros_like(dcq_ref)

        lane = lax.broadcasted_iota(jnp.int32, (1, PAIR_W), 1)
        first = (lane // HD) == 0
        sub = lax.broadcasted_iota(jnp.int32, (8, t), 0)
        kj, vj = k_ref[...], v_ref[...]
        ks = [jnp.where(first, kj, 0.0), jnp.where(first, 0.0, kj)]
        cks = [_pick_row(ck_ref[0], 2 * pair + e) for e in range(2)]
        kidx = j * t + lax.broadcasted_iota(jnp.int32, (t, t), 1)
        ones8 = jnp.ones((8, t), f32)

        def step(i, carry):
            dk, dv, dck0, dck1 = carry
            rows = pl.ds(pl.multiple_of(i * t, t), t)
            q = q_ref[rows, :] * scale
            d_o, o_i, lse_i, cq_i = do_ref[rows, :], o_ref[rows, :], lse_ref[rows, :], cq_ref[rows, :]
            vis = kidx <= (i * t + lax.broadcasted_iota(jnp.int32, (t, t), 0))
            dq_acc = jnp.zeros((t, PAIR_W), f32)
            dcq_acc = jnp.zeros((8, t), f32)
            dcks = [dck0, dck1]
            for e in range(2):
                mine = first if e == 0 else jnp.logical_not(first)
                qe, doe = jnp.where(mine, q, 0.0), jnp.where(mine, d_o, 0.0)
                s = _bdot(qe, kj, _D2) + (_pick_lane(cq_i, 2 * pair + e) - cks[e])
                p = jnp.exp(jnp.where(vis, s, _NEG) - _pick_lane(lse_i, e))
                dv = dv + _bdot(p, doe, _D0)
                delta = jnp.sum(doe * o_i, axis=1, keepdims=True)
                ds = p * (_bdot(doe, vj, _D2) - delta)
                dk = dk + _bdot(ds, qe, _D0)
                dq_acc = dq_acc + _bdot(ds, ks[e], _D1)
                row_sums = lax.dot_general(ones8, ds, _D2, precision=_HI, preferred_element_type=f32)
                dcq_acc = dcq_acc + jnp.where(sub == e, row_sums, 0.0)
                dcks[e] = dcks[e] - jnp.sum(ds, axis=0, keepdims=True)
            dq_ref[rows, :] += dq_acc * scale
            dcq_ref[0, :, rows] += dcq_acc
            return dk, dv, dcks[0], dcks[1]

        zero = jnp.zeros((t, PAIR_W), f32)
        dk, dv, dck0, dck1 = lax.fori_loop(j, nq, step, (zero, zero, jnp.zeros((1, t), f32), jnp.zeros((1, t), f32)))
        dk_ref[...] = dk
        dv_ref[...] = dv
        dck_ref[0] = jnp.where(sub == 0, dck0, jnp.where(sub == 1, dck1, 0.0))

    whole = lambda col: pl.BlockSpec((seq, PAIR_W), lambda b, p, j: (b, col * PAIRS + p))
    blk = lambda col: pl.BlockSpec((t, PAIR_W), lambda b, p, j: (b * nq + j, col * PAIRS + p))
    rows_whole = pl.BlockSpec((1, 8, seq), lambda b, p, j: (b * PAIRS + p, 0, 0))
    rows_blk = pl.BlockSpec((1, 8, t), lambda b, p, j: (b * PAIRS + p, 0, j))
    t_all = batch * seq
    res = pl.pallas_call(
        body, name="fox_attn_bwd", grid=(batch, PAIRS, nq),
        in_specs=[whole(0), blk(1), blk(2),
                  pl.BlockSpec((seq, 128), lambda b, p, j: (b, 0)),
                  pl.BlockSpec((1, 8, t), lambda b, p, j: (b, 0, j)),
                  whole(0), whole(0), whole(0)] + [_HBM_SPEC] * n_s,
        out_specs=[whole(0), blk(0), blk(0), rows_whole, rows_blk] + [_HBM_SPEC] * n_s,
        out_shape=[jax.ShapeDtypeStruct((t_all, HW), f32)] * 3
        + [jax.ShapeDtypeStruct((batch * PAIRS, 8, seq), f32)] * 2 + _side_out_shapes(srcs, per_peer),
        scratch_shapes=_side_sems(n_s),
        compiler_params=_cp(("arbitrary", "arbitrary", "arbitrary")),
    )(qkv, qkv, qkv, c, c_rows, o, lse, do, *srcs)
    return res[:5], list(res[5:])


def _fox_fwd(qkv, c, c_rows, batch, seq, side=None):
    t = min(FOX_T, seq)
    nq = seq // t
    scale = HD ** -0.5
    srcs, per_peer = side if side is not None else ([], False)
    n_s = len(srcs)

    def body(*refs):
        q_ref, k_ref, v_ref, cq_ref, ck_ref = refs[:5]
        o_ref, lse_ref = refs[5 + n_s:7 + n_s]
        _side_exchange(refs[5:5 + n_s], refs[7 + n_s:7 + 2 * n_s], per_peer, refs[7 + 2 * n_s:], batch, PAIRS, nq)
        pair, i = pl.program_id(1), pl.program_id(2)
        lane = lax.broadcasted_iota(jnp.int32, (1, PAIR_W), 1)
        first = (lane // HD) == 0
        mine = [first, jnp.logical_not(first)]
        q = q_ref[...] * scale
        qs = [jnp.where(mine[e], q, 0.0) for e in range(2)]
        cqs = [_pick_lane(cq_ref[...], 2 * pair + e) for e in range(2)]
        causal = lax.broadcasted_iota(jnp.int32, (t, t), 1) <= lax.broadcasted_iota(jnp.int32, (t, t), 0)

        def block(j, carry, diagonal):
            rows = pl.ds(pl.multiple_of(j * t, t), t)
            kj, vj = k_ref[rows, :], v_ref[rows, :]
            ck_blk = ck_ref[0, :, rows]
            out = []
            for e in range(2):
                m, acc = carry[2 * e:2 * e + 2]
                s = _bdot(qs[e], kj, _D2) + cqs[e] - _pick_row(ck_blk, 2 * pair + e)
                if diagonal:
                    s = jnp.where(causal, s, _NEG)
                m_new = jnp.maximum(m, jnp.max(s, axis=1, keepdims=True))
                p = jnp.exp(s - m_new)
                out += [m_new, jnp.exp(m - m_new) * acc + _bdot(p, jnp.where(mine[e], vj, 1.0), _D1)]
            return tuple(out)

        init = (jnp.full((t, 1), _NEG, f32), jnp.zeros((t, PAIR_W), f32)) * 2
        carry = lax.fori_loop(0, i, lambda j, cr: block(j, cr, False), init)
        m0, a0, m1, a1 = block(i, carry, True)
        l0, l1 = _pick_lane(a0, HD), _pick_lane(a1, 0)
        o_ref[...] = jnp.where(first, a0 / l0, a1 / l1)
        lse_ref[...] = jnp.where(lane == 0, m0 + jnp.log(l0), jnp.where(lane == 1, m1 + jnp.log(l1), 0.0))

    q_spec = pl.BlockSpec((t, PAIR_W), lambda b, p, i: (b * nq + i, p))
    res = pl.pallas_call(
        body, name="fox_attn_fwd", grid=(batch, PAIRS, nq),
        in_specs=[q_spec,
                  pl.BlockSpec((seq, PAIR_W), lambda b, p, i: (b, PAIRS + p)),
                  pl.BlockSpec((seq, PAIR_W), lambda b, p, i: (b, 2 * PAIRS + p)),
                  pl.BlockSpec((t, 128), lambda b, p, i: (b * nq + i, 0)),
                  pl.BlockSpec((1, 8, seq), lambda b, p, i: (b, 0, 0))] + [_HBM_SPEC] * n_s,
        out_specs=[q_spec, q_spec] + [_HBM_SPEC] * n_s,
        out_shape=[jax.ShapeDtypeStruct((batch * seq, HW), f32)] * 2 + _side_out_shapes(srcs, per_peer),
        scratch_shapes=_side_sems(n_s),
        compiler_params=_cp(("arbitrary", "arbitrary", "arbitrary")),
    )(qkv, qkv, qkv, c, c_rows, *srcs)
    return res[0], res[1], list(res[2:])


def _fox_bwd(qkv, c, c_rows, o, lse, do, batch, seq):
    t = min(FOX_T, seq)
    nq = seq // t
    scale = HD ** -0.5

    def body(q_ref, k_ref, v_ref, cq_ref, ck_ref, o_ref, lse_ref, do_ref,
             dq_ref, dk_ref, dv_ref, dcq_ref, dck_ref, acc0, acc1):
        pair, i = pl.program_id(1), pl.program_id(2)
        accs = [acc0, acc1]

        @pl.when(i == 0)
        def _():
            dv_ref[...] = jnp.zeros_like(dv_ref)
            acc0[...] = jnp.zeros_like(acc0)
            acc1[...] = jnp.zeros_like(acc1)

        lane = lax.broadcasted_iota(jnp.int32, (1, PAIR_W), 1)
        first = (lane // HD) == 0
        mine = [first, jnp.logical_not(first)]
        q, d_o, o_i = q_ref[...] * scale, do_ref[...], o_ref[...]
        q0s = [jnp.where(mine[e], q, 0.0) for e in range(2)]
        q1s = [jnp.where(mine[e], q, 1.0) for e in range(2)]
        dos = [jnp.where(mine[e], d_o, 0.0) for e in range(2)]
        deltas = [jnp.sum(dos[e] * o_i, axis=1, keepdims=True) for e in range(2)]
        lses = [_pick_lane(lse_ref[...], e) for e in range(2)]
        cqs = [_pick_lane(cq_ref[...], 2 * pair + e) for e in range(2)]
        causal = lax.broadcasted_iota(jnp.int32, (t, t), 1) <= lax.broadcasted_iota(jnp.int32, (t, t), 0)

        def block(j, dqs, diagonal):
            rows = pl.ds(pl.multiple_of(j * t, t), t)
            kj, vj = k_ref[rows, :], v_ref[rows, :]
            ck_blk = ck_ref[0, :, rows]
            out = []
            for e in range(2):
                s = _bdot(q0s[e], kj, _D2) + cqs[e] - _pick_row(ck_blk, 2 * pair + e)
                if diagonal:
                    s = jnp.where(causal, s, _NEG)
                p = jnp.exp(s - lses[e])
                ds = p * (_bdot(dos[e], vj, _D2) - deltas[e])
                dv_ref[rows, :] += _bdot(p, dos[e], _D0)
                accs[e][rows, :] += _bdot(ds, q1s[e], _D0)
                out.append(dqs[e] + _bdot(ds, jnp.where(mine[e], kj, 1.0), _D1))
            return tuple(out)

        zero = jnp.zeros((t, PAIR_W), f32)
        dqs = lax.fori_loop(0, i, lambda j, cr: block(j, cr, False), (zero, zero))
        dq0, dq1 = block(i, dqs, True)
        dq_ref[...] = jnp.where(first, dq0, dq1) * scale
        dcq_ref[...] = jnp.where(lane == 0, _pick_lane(dq0, HD), jnp.where(lane == 1, _pick_lane(dq1, 0), 0.0))

        @pl.when(i == nq - 1)
        def _():
            a0, a1 = acc0[...], acc1[...]
            dk_ref[...] = jnp.where(first, a0, a1)
            dck_ref[...] = jnp.where(lane == 0, -_pick_lane(a0, HD), jnp.where(lane == 1, -_pick_lane(a1, 0), 0.0))

    blk = lambda col: pl.BlockSpec((t, PAIR_W), lambda b, p, i: (b * nq + i, col * PAIRS + p))
    whole = lambda col: pl.BlockSpec((seq, PAIR_W), lambda b, p, i: (b, col * PAIRS + p))
    t_all = batch * seq
    return pl.pallas_call(
        body, name="fox_attn_bwd", grid=(batch, PAIRS, nq),
        in_specs=[blk(0), whole(1), whole(2),
                  pl.BlockSpec((t, 128), lambda b, p, i: (b * nq + i, 0)),
                  pl.BlockSpec((1, 8, seq), lambda b, p, i: (b, 0, 0)),
                  blk(0), blk(0), blk(0)],
        out_specs=[blk(0), whole(0), whole(0), blk(0), whole(0)],
        out_shape=[jax.ShapeDtypeStruct((t_all, HW), f32)] * 5,
        scratch_shapes=[pltpu.VMEM((seq, PAIR_W), f32), pltpu.VMEM((seq, PAIR_W), f32)],
        compiler_params=_cp(("parallel", "parallel", "arbitrary")),
    )(qkv, qkv, qkv, c, c_rows, o, lse, do)


def _mem_block(q, km, vm):
    nn, nt, _ = _make_mm(False, False)
    logits = nt(q, km) * (MEM_HD ** -0.5)
    m = lax.stop_gradient(jnp.max(logits, axis=-1, keepdims=True))
    e = jnp.exp(logits - m)
    return nn(e / jnp.sum(e, axis=-1, keepdims=True), vm)


def _mem_specs(seq, tq):
    nq = seq // tq
    qs = pl.BlockSpec((tq, MEM_HD), lambda b, h, i: (b * nq + i, h))
    ks = pl.BlockSpec((MEM_LEN, MEM_HD), lambda b, h, i: (b, h))
    vs = pl.BlockSpec((MEM_LEN, MEM_HD), lambda b, h, i: (b, MEM_HEADS + h))
    return nq, qs, ks, vs


def _mem_fwd(q, mem_kv, batch, seq):
    tq = min(512, seq)
    nq, qs, ks, vs = _mem_specs(seq, tq)

    def body(q_ref, k_ref, v_ref, o_ref):
        o_ref[...] = _mem_block(q_ref[...], k_ref[...], v_ref[...]).astype(o_ref.dtype)

    return pl.pallas_call(
        body, name="mem_attn_fwd", grid=(batch, MEM_HEADS, nq),
        in_specs=[qs, ks, vs], out_specs=qs, out_shape=jax.ShapeDtypeStruct(q.shape, bf16),
        compiler_params=_cp(("parallel", "parallel", "arbitrary")),
    )(q, mem_kv, mem_kv)


def _mem_bwd(q, mem_kv, do, batch, seq):
    tq = min(512, seq)
    nq, qs, ks, vs = _mem_specs(seq, tq)

    def body(q_ref, k_ref, v_ref, do_ref, dq_ref, dk_ref, dv_ref):
        _, vjp = jax.vjp(_mem_block, q_ref[...], k_ref[...], v_ref[...])
        dq, dk, dv = vjp(do_ref[...])
        dq_ref[...] = dq.astype(dq_ref.dtype)

        @pl.when(pl.program_id(2) == 0)
        def _():
            dk_ref[...] = jnp.zeros_like(dk_ref)
            dv_ref[...] = jnp.zeros_like(dv_ref)

        dk_ref[...] += dk
        dv_ref[...] += dv

    return pl.pallas_call(
        body, name="mem_attn_bwd", grid=(batch, MEM_HEADS, nq),
        in_specs=[qs, ks, vs, qs], out_specs=[qs, ks, ks],
        out_shape=[jax.ShapeDtypeStruct(q.shape, bf16), jax.ShapeDtypeStruct((batch * MEM_LEN, MEM_W), f32),
                   jax.ShapeDtypeStruct((batch * MEM_LEN, MEM_W), f32)],
        compiler_params=_cp(("parallel", "parallel", "arbitrary")),
    )(q, mem_kv, mem_kv, do)


@jax.custom_vjp
def _halves(x):
    c = x.shape[1] // 2
    return x[:, :c], x[:, c:]


_halves.defvjp(lambda x: ((x[:, :x.shape[1] // 2], x[:, x.shape[1] // 2:]), None),
               lambda _, g: (jnp.concatenate(g, axis=1),))


@jax.custom_vjp
def _lead_halves(x):
    n = x.shape[0] // 2
    return x[:n], x[n:]


_lead_halves.defvjp(lambda x: ((x[:x.shape[0] // 2], x[x.shape[0] // 2:]), None),
                    lambda _, g: (jnp.concatenate(g, axis=0),))


def _scan_chunk(s0, r, wl, k, v, a, b):
    nn, nt, tn = _make_mm(True, False)
    nn_exact, _, _ = _make_mm(True, True)
    _, nt_exact, _ = _make_mm(True, "split")
    hp, c, lanes = r.shape
    row = lax.broadcasted_iota(jnp.int32, (c, c), 0)
    col = lax.broadcasted_iota(jnp.int32, (c, c), 1)
    first = (lax.broadcasted_iota(jnp.int32, (1, 1, lanes), 2) // HD) == 0
    tri = jnp.broadcast_to((col <= row).astype(f32)[None], (hp, c, c))
    lg = nn_exact(tri, wl)
    lg_end = lg[:, c - 1:c, :]
    grow, shrink, to_end = jnp.exp(lg), jnp.exp(-lg), jnp.exp(lg_end - lg)
    rt, kt, bt, at = r * grow, k * shrink, b * shrink, a * jnp.exp(lg - wl)
    strict, incl = (col < row)[None], (col <= row)[None]
    twice = lambda t: jnp.concatenate([t, t], axis=0)
    queries = jnp.concatenate([at, rt], axis=1)
    per_head = jnp.concatenate([jnp.where(first, queries, 0.0), jnp.where(first, 0.0, queries)], axis=0)
    (ab, rb), (ak, rk) = _halves(nt_exact(per_head, twice(bt))), _halves(nt_exact(per_head, twice(kt)))
    l_ab = jnp.where(strict, ab, 0.0)
    a_ak = jnp.where(strict, ak, 0.0)
    a_rb = jnp.where(incl, rb, 0.0)
    a_rk = jnp.where(incl, rk, 0.0)
    inv = (col == row).astype(f32)[None] + l_ab
    power, n = l_ab, 1
    while 2 * n < c:
        power = nn(power, power)
        inv = inv + nn(inv, power)
        n *= 2

    def apply(m, t):
        lo, hi = _lead_halves(nn(m, twice(t)))
        return jnp.where(first, lo, hi)

    sa = apply(inv, nt(at, s0) + apply(a_ak, v))
    y = nt(rt, s0) + apply(a_rk, v) + apply(a_rb, sa)
    same_head = ((lax.broadcasted_iota(jnp.int32, (lanes, lanes), 0) // HD)
                 == (lax.broadcasted_iota(jnp.int32, (lanes, lanes), 1) // HD))[None]
    s1 = s0 * jnp.exp(lg_end) + jnp.where(same_head, tn(v, k * to_end) + tn(sa, b * to_end), 0.0)
    return y, s1


PAIRS = HEADS // 2
PAIR_W = 2 * HD


def _pair_stack(ref, off):
    return jnp.stack([ref[b, :, off + p * PAIR_W:off + (p + 1) * PAIR_W]
                      for b in range(ref.shape[0]) for p in range(PAIRS)])


def _pair_store(ref, off, val, add_ref=None):
    for b in range(ref.shape[0]):
        for p in range(PAIRS):
            sl = slice(off + p * PAIR_W, off + (p + 1) * PAIR_W)
            v = val[b * PAIRS + p]
            ref[b, :, sl] = v if add_ref is None else v + add_ref[b, :, sl]


def _scan_fwd(main6, batch, seq, side=None):
    c = min(SCAN_CHUNK, seq)
    nc = seq // c
    hp = batch * PAIRS
    srcs, per_peer = side if side is not None else ([], False)
    n_s = len(srcs)

    def body(*refs):
        z_ref, y_ref, s_ref, st = refs[0], refs[1 + n_s], refs[2 + n_s], refs[3 + 2 * n_s]
        _side_exchange(refs[1:1 + n_s], refs[3 + n_s:3 + 2 * n_s], per_peer, refs[4 + 2 * n_s:], nc)

        @pl.when(pl.program_id(0) == 0)
        def _():
            st[...] = jnp.zeros_like(st)

        s0 = st[...]
        s_ref[0] = s0
        y, s1 = _scan_chunk(s0, *[_pair_stack(z_ref, comp * HW) for comp in range(6)])
        _pair_store(y_ref, 0, y)
        st[...] = s1

    res = pl.pallas_call(
        body, name="rwkv_scan_fwd", grid=(nc,),
        in_specs=[pl.BlockSpec((batch, c, 6 * HW), lambda i: (0, i, 0))] + [_HBM_SPEC] * n_s,
        out_specs=[pl.BlockSpec((batch, c, HW), lambda i: (0, i, 0)),
                   pl.BlockSpec((1, hp, PAIR_W, PAIR_W), lambda i: (i, 0, 0, 0))] + [_HBM_SPEC] * n_s,
        out_shape=[jax.ShapeDtypeStruct((batch, seq, HW), f32), jax.ShapeDtypeStruct((nc, hp, PAIR_W, PAIR_W), f32)]
        + _side_out_shapes(srcs, per_peer),
        scratch_shapes=[pltpu.VMEM((hp, PAIR_W, PAIR_W), f32)] + _side_sems(n_s),
        compiler_params=_cp(("arbitrary",)),
    )(main6.reshape(batch, seq, 6 * HW), *srcs)
    return res[0].reshape(batch * seq, HW), res[1], list(res[2:])


def _scan_bwd(main6, states, dy, extra, batch, seq, side=None):
    c = min(SCAN_CHUNK, seq)
    nc = seq // c
    hp = batch * PAIRS
    srcs, per_peer = side if side is not None else ([], False)
    n_s = len(srcs)

    def body(*refs):
        z_ref, s_ref, dy_ref, ex_ref = refs[:4]
        dz_ref, dst = refs[4 + n_s], refs[5 + 2 * n_s]
        _side_exchange(refs[4:4 + n_s], refs[5 + n_s:5 + 2 * n_s], per_peer, refs[6 + 2 * n_s:], nc)

        @pl.when(pl.program_id(0) == 0)
        def _():
            dst[...] = jnp.zeros_like(dst)

        _, vjp = jax.vjp(_scan_chunk, s_ref[0], *[_pair_stack(z_ref, comp * HW) for comp in range(6)])
        g = vjp((_pair_stack(dy_ref, 0), dst[...]))
        dst[...] = g[0]
        for comp in range(6):
            _pair_store(dz_ref, comp * HW, g[1 + comp], ex_ref)

    back = lambda i: (0, nc - 1 - i, 0)
    wide = pl.BlockSpec((batch, c, 6 * HW), back)
    res = pl.pallas_call(
        body, name="rwkv_scan_bwd", grid=(nc,),
        in_specs=[wide, pl.BlockSpec((1, hp, PAIR_W, PAIR_W), lambda i: (nc - 1 - i, 0, 0, 0)),
                  pl.BlockSpec((batch, c, HW), back), wide] + [_HBM_SPEC] * n_s,
        out_specs=[wide] + [_HBM_SPEC] * n_s,
        out_shape=[jax.ShapeDtypeStruct((batch, seq, 6 * HW), f32)] + _side_out_shapes(srcs, per_peer),
        scratch_shapes=[pltpu.VMEM((hp, PAIR_W, PAIR_W), f32)] + _side_sems(n_s),
        compiler_params=_cp(("arbitrary",)),
    )(main6.reshape(batch, seq, 6 * HW), states, dy.reshape(batch, seq, HW), extra.reshape(batch, seq, 6 * HW), *srcs)
    return res[0].reshape(batch * seq, 6 * HW), list(res[1:])


def _to_heads(x, batch, seq, k):
    return x.reshape(batch, seq, k, HEADS, HD).transpose(2, 0, 3, 1, 4).reshape(k, batch * HEADS, seq, HD)


def _from_heads(x, batch, seq, k):
    return x.reshape(k, batch, HEADS, seq, HD).transpose(1, 3, 0, 2, 4).reshape(batch * seq, k * HW)


def _pad_cols(x, width):
    return jnp.pad(x, ((0, 0), (0, width - x.shape[1])))


def _split_w_in(w):
    z64 = jnp.zeros((w.shape[0], 64), w.dtype)
    w_r = jnp.concatenate([w[:, 1544:3080], w[:, 3080:3144], z64, w[:, 3144:3208], z64, w[:, 3208:3336]], axis=1)
    return w[:, :1536], _pad_cols(w[:, 1536:1544], 128), w_r, w[:, 3336:3848], w[:, 3848:]


def _merge_w_in(g_qkv, g_f, g_r, g_mq, g_g):
    return jnp.concatenate([g_qkv, g_f[:, :8], g_r[:, :1536], g_r[:, 1536:1600], g_r[:, 1664:1728], g_r[:, 1792:],
                            g_mq, g_g], axis=1)


def _pad_lora(v):
    z64 = jnp.zeros((1, 64), v.dtype)
    return jnp.concatenate([v[:, :1536], v[:, 1536:1600], z64, v[:, 1600:1664], z64, v[:, 1664:]], axis=1)


def _unpad_lora(v):
    return jnp.concatenate([v[:, :1536], v[:, 1536:1600], v[:, 1664:1728], v[:, 1792:]], axis=1)


def _local_step(x, mem, target, w, p, late=None, early=None):
    batch, seq, _ = x.shape
    t = batch * seq
    x2, tg2, mem2 = x.reshape(t, D), target.reshape(t, D), mem.reshape(batch * MEM_LEN, D)
    w_qkv, w_f, w_r, w_mq, w_g3 = _split_w_in(w["w_in"])
    mu = _pad_lora(p["rwkv_mu"])
    bias = _pad_cols(p["fox_f_bias"], 128)
    r_k = p["rwkv_r_k"].reshape(1, HW)
    post_params = [p["rwkv_gn_g"], p["rwkv_gn_b"], r_k]
    rw_widths = [HW, HW, HW, LORA_PAD, LORA_PAD, LORA_PAD]
    six = [HW] * 6

    (u,) = _rows_fwd("rms_pre1", _fn_rms, [], [(x2, [D])], [p["pre1_g"]], [[D]], dtypes=[bf16])
    p_qkv = _matmul("proj_qkv", u, w_qkv, "nn")
    p_f = _matmul("proj_f", u, w_f, "nn")
    p_r = _matmul("proj_rwkv", u, w_r, "nn")
    p_mq = _matmul("proj_memq", u, w_mq, "nn")
    p_g = _matmul("proj_gate", u, w_g3, "nn")

    c = _fox_gate_fwd(p_f, bias, batch, seq)
    c_rows = c[:, :HEADS].reshape(batch, seq, HEADS).transpose(0, 2, 1)
    fox_o, lse, gathered = _fox_fwd(p_qkv, c, c_rows, batch, seq, side=(late[0], False) if late else None)
    if late:
        w = {**w, **late[2](gathered, 0)}
    fox_out = fox_o.astype(bf16)

    w_up = jnp.pad(w["rwkv_w_up"].astype(f32), ((0, LORA_PAD - 64), (0, 0)))
    a_up = jnp.pad(w["rwkv_a_up"].astype(f32), ((0, LORA_PAD - 64), (0, 0)))
    pre_params = [p["rwkv_w0"], w_up, p["rwkv_a0"], a_up, w["rwkv_g_up"].astype(f32), p["rwkv_k_k"], p["rwkv_k_a"]]
    ps = _tokshift_fwd(p_r, mu, batch, seq)
    main6, g_rw = _rows_fwd("rwkv_pre", _fn_rwkv_pre, [], [(ps, rw_widths)], pre_params, [six, [HW]])
    y_rw, states, gathered = _scan_fwd(main6, batch, seq, side=(late[1], False) if late else None)
    if late:
        w = {**w, **late[2](gathered, 1)}
    post_consts = []
    post_rows = [(y_rw, [HW]), (main6, six), (g_rw, [HW])]

    def fn_post(y, r, _wl, k2, v, _a, _b, g, gn_g, gn_b, rk):
        return _fn_rwkv_post(y, r, k2, v, g, gn_g, gn_b, rk)

    (rwkv_out,) = _rows_fwd("rwkv_post", fn_post, post_consts, post_rows, post_params, [[HW]], dtypes=[bf16])

    (memn,) = _rows_fwd("rms_mem", _fn_rms, [], [(mem2, [D])], [p["mem_norm_g"]], [[D]], dtypes=[bf16])
    mem_kv = _matmul("proj_memkv", memn, w["w_mem_kv"], "nn")
    mem_out = _mem_fwd(p_mq, mem_kv, batch, seq)

    a_fox = _matmul("out_fox", fox_out, w["w_fox_out"], "nn")
    a_rwkv = _matmul("out_rwkv", rwkv_out, w["w_rwkv_out"], "nn")
    a_mem = _matmul("out_mem", mem_out, w["w_mem_out"], "nn")
    merge_rows = [(a_fox, [D]), (a_rwkv, [D]), (a_mem, [D]), (p_g, [D, D, D])]
    (merged,) = _rows_fwd("merge", _fn_merge, [], merge_rows, [], [[D]], dtypes=[bf16])
    yy = _matmul("out_o", merged, w["w_o"], "nn")
    post1_rows = [(yy, [D]), (x2, [D])]
    post1_params = [p["post1_g"], p["pre2_g"]]
    h1, u2 = _rows_fwd("post1", _fn_post1, [], post1_rows, post1_params, [[D], [D]], dtypes=[f32, bf16])
    gp = _matmul("ffn_gate", u2, w["w_ffn_gate"], "nn")
    up = _matmul("ffn_up", u2, w["w_ffn_up"], "nn")
    (hmid,) = _rows_fwd("swiglu", _fn_swiglu, [], [(gp, [D_FF]), (up, [D_FF])], [], [[D_FF]], dtypes=[bf16])
    ffn = _matmul("ffn_down", hmid, w["w_ffn_down"], "nn")
    final_rows = [(ffn, [D]), (h1, [D])]
    (loss,) = _rows_fwd("final", _fn_final, [(tg2, [D])], final_rows, [p["post2_g"]], [], n_sums=1)

    gw, gp_ = {}, {}
    (d_ffn, d_h1), (gp_["post2_g"],) = _rows_bwd("final_bwd", _fn_final, [(tg2, [D])], final_rows, [p["post2_g"]], [], [],
                                                  n_sums=1, dtypes=[bf16, f32])
    d_hmid = _matmul("ffn_down_dx", d_ffn, w["w_ffn_down"], "nt")
    gw["w_ffn_down"] = _matmul("ffn_down_dw", hmid, d_ffn, "tn")
    (d_gp, d_up), _ = _rows_bwd("swiglu_bwd", _fn_swiglu, [], [(gp, [D_FF]), (up, [D_FF])], [], [[D_FF]], [d_hmid],
                                dtypes=[bf16, bf16])
    d_u2 = _matmul("ffn_gate_dx", d_gp, w["w_ffn_gate"], "nt")
    d_u2 = _matmul("ffn_up_dx", d_up, w["w_ffn_up"], "nt", add=d_u2)
    gw["w_ffn_gate"] = _matmul("ffn_gate_dw", u2, d_gp, "tn")
    gw["w_ffn_up"] = _matmul("ffn_up_dw", u2, d_up, "tn")
    (d_yy, d_x_res), (gp_["post1_g"], gp_["pre2_g"]) = _rows_bwd(
        "post1_bwd", _fn_post1, [], post1_rows, post1_params, [[D], [D]], [d_h1, d_u2], dtypes=[bf16, f32])
    d_merged = _matmul("out_o_dx", d_yy, w["w_o"], "nt")
    gw["w_o"] = _matmul("out_o_dw", merged, d_yy, "tn")
    (d_a_fox, d_a_rwkv, d_a_mem, d_p_g), _ = _rows_bwd("merge_bwd", _fn_merge, [], merge_rows, [], [[D]], [d_merged],
                                                       dtypes=[bf16] * 4)
    d_fox_out = _matmul("out_fox_dx", d_a_fox, w["w_fox_out"], "nt")
    gw["w_fox_out"] = _matmul("out_fox_dw", fox_out, d_a_fox, "tn")
    d_rwkv_out = _matmul("out_rwkv_dx", d_a_rwkv, w["w_rwkv_out"], "nt")
    gw["w_rwkv_out"] = _matmul("out_rwkv_dw", rwkv_out, d_a_rwkv, "tn")
    d_mem_out = _matmul("out_mem_dx", d_a_mem, w["w_mem_out"], "nt")
    gw["w_mem_out"] = _matmul("out_mem_dw", mem_out, d_a_mem, "tn")

    d_p_mq, d_km, d_vm = _mem_bwd(p_mq, mem_kv, d_mem_out, batch, seq)
    d_mem_kv = jnp.concatenate([d_km, d_vm], axis=1).astype(bf16)
    gw["w_mem_kv"] = _matmul("proj_memkv_dw", memn, d_mem_kv, "tn")
    d_memn = _matmul("proj_memkv_dx", d_mem_kv, w["w_mem_kv"], "nt")
    _, (gp_["mem_norm_g"],) = _rows_bwd("rms_mem_bwd", _fn_rms, [], [(mem2, [D])], [p["mem_norm_g"]], [[D]], [d_memn])

    d_q, d_k, d_v, d_cq, d_ck = _fox_bwd(p_qkv, c, c_rows, fox_o, lse, d_fox_out, batch, seq)
    d_p_qkv = jnp.concatenate([d_q, d_k, d_v], axis=1).astype(bf16)
    d_p_f, d_bias = _fox_gate_bwd(p_f, bias, d_cq, d_ck, batch, seq)
    gp_["fox_f_bias"] = d_bias[:, :HEADS]

    (d_y_rw, d_main6_post, d_g_rw), (gp_["rwkv_gn_g"], gp_["rwkv_gn_b"], d_rk) = _rows_bwd(
        "rwkv_post_bwd", fn_post, post_consts, post_rows, post_params, [[HW]], [d_rwkv_out])
    gp_["rwkv_r_k"] = d_rk.reshape(1, HEADS, HD)
    d_main6, early_got = _scan_bwd(main6, states, d_y_rw, d_main6_post, batch, seq,
                                   side=(early(gw), True) if early else None)

    def fn_pre_sum(*args):
        return _fn_rwkv_pre(*args)

    (d_ps,), d_pre = _rows_bwd("rwkv_pre_bwd", fn_pre_sum, [], [(ps, rw_widths)], pre_params, [six, [HW]],
                               [d_main6, d_g_rw])
    gp_["rwkv_w0"], d_w_up, gp_["rwkv_a0"], d_a_up, gw["rwkv_g_up"], gp_["rwkv_k_k"], gp_["rwkv_k_a"] = d_pre
    gw["rwkv_w_up"], gw["rwkv_a_up"] = d_w_up[:64], d_a_up[:64]
    d_p_r, d_mu = _tokshift_bwd(p_r, mu, d_ps, batch, seq)
    gp_["rwkv_mu"] = _unpad_lora(d_mu)

    d_u = _matmul("proj_qkv_dx", d_p_qkv, w_qkv, "nt")
    d_u = _matmul("proj_f_dx", d_p_f, w_f, "nt", add=d_u)
    d_u = _matmul("proj_rwkv_dx", d_p_r, w_r, "nt", add=d_u)
    d_u = _matmul("proj_memq_dx", d_p_mq, w_mq, "nt", add=d_u)
    d_u = _matmul("proj_gate_dx", d_p_g, w_g3, "nt", add=d_u)
    gw["w_in"] = _merge_w_in(_matmul("proj_qkv_dw", u, d_p_qkv, "tn"), _matmul("proj_f_dw", u, d_p_f, "tn"),
                             _matmul("proj_rwkv_dw", u, d_p_r, "tn"), _matmul("proj_memq_dw", u, d_p_mq, "tn"),
                             _matmul("proj_gate_dw", u, d_p_g, "tn"))
    (d_x,), (gp_["pre1_g"],) = _rows_bwd("rms_pre1_bwd", _fn_rms, [], [(x2, [D])], [p["pre1_g"]], [[D]], [d_u], add=d_x_res)
    return loss, d_x.reshape(x.shape), gw, gp_, early_got


def _rows_add(name, a, b):
    (s,) = _rows_fwd(name, lambda u, v: (u + v,), [], [(a, [a.shape[1]]), (b, [b.shape[1]])], [], [[a.shape[1]]])
    return s


def _adamw(name, recv, w, m, v):
    rows, cols = w.shape
    tr = max(t for t in range(16, min(rows, 128) + 1, 16) if rows % t == 0)

    def body(g_ref, w_ref, m_ref, v_ref, go_ref, d_ref, mo_ref, vo_ref):
        g = g_ref[0].astype(f32)
        for s in range(1, N_DEV):
            g = g + g_ref[s].astype(f32)
        m_new = ADAM_B1 * m_ref[...] + (1.0 - ADAM_B1) * g
        v_new = ADAM_B2 * v_ref[...] + (1.0 - ADAM_B2) * (g * g)
        m_hat = m_new / (1.0 - ADAM_B1 ** ADAM_STEP)
        v_hat = v_new / (1.0 - ADAM_B2 ** ADAM_STEP)
        go_ref[...] = g
        d_ref[...] = -ADAM_LR * (m_hat / (jnp.sqrt(v_hat) + ADAM_EPS) + ADAM_WD * w_ref[...])
        mo_ref[...] = m_new
        vo_ref[...] = v_new

    spec = pl.BlockSpec((tr, cols), lambda i: (i, 0))
    return pl.pallas_call(
        body, name=name, grid=(rows // tr,),
        in_specs=[pl.BlockSpec((N_DEV, tr, cols), lambda i: (0, i, 0)), spec, spec, spec],
        out_specs=[spec] * 4, out_shape=[jax.ShapeDtypeStruct(w.shape, f32)] * 4,
        compiler_params=_cp(("parallel",)),
    )(recv, w, m, v)


GROUPS = (
    ("in", ("w_in",), 1),
    ("memkv", ("w_mem_kv",), 0),
    ("ffn_gu", ("w_ffn_gate", "w_ffn_up"), 1),
    ("down_o", ("w_ffn_down", "w_o"), 0),
    ("outs", ("w_fox_out", "w_rwkv_out", "w_mem_out"), 1),
    ("lora", ("rwkv_w_up", "rwkv_a_up", "rwkv_g_up"), 0),
)
FIRST_GROUPS = ("in", "memkv")
LATE_GROUPS = (("down_o", "outs", "lora"), ("ffn_gu",))
EARLY_GRAD_GROUPS = ("memkv", "ffn_gu", "down_o", "outs")
LAST_GRAD_GROUPS = ("in", "lora")
SHARD_AXIS = {n: a for n, _, a in SHARDED}
SMALL_ROWS = 16


def _group_local(shards, members, join):
    parts = [shards[n].reshape(shards[n].shape[-2:]) for n in members]
    return parts[0] if len(parts) == 1 else jnp.concatenate(parts, axis=join)


def _group_split(arr, members, join, lead=False):
    out, off = {}, 0
    for n in members:
        shape = dict((k, s) for k, s, _ in SHARDED)[n]
        size = _block_shape(shape, SHARD_AXIS[n])[join]
        idx = [slice(None)] * arr.ndim
        idx[arr.ndim - 2 + join] = slice(off, off + size)
        out[n] = arr[tuple(idx)]
        off += size
    return out


def _full_from_blocks(blocks, axis):
    if axis == 0:
        return blocks.reshape(-1, blocks.shape[2])
    return blocks.transpose(1, 0, 2).reshape(blocks.shape[1], -1)


def _blocks_from_full(full, axis):
    if axis == 0:
        return full.reshape(N_DEV, -1, full.shape[1])
    return full.reshape(full.shape[0], N_DEV, -1).transpose(1, 0, 2)


def _assemble(gathered, names):
    out = {}
    for arr, g in zip(gathered, names):
        _, members, join = [grp for grp in GROUPS if grp[0] == g][0]
        for n, blk in _group_split(arr, members, join, lead=True).items():
            out[n] = _full_from_blocks(blk, SHARD_AXIS[n])
    return out


def _grad_blocks(gw, names):
    out = []
    for g in names:
        _, members, join = [grp for grp in GROUPS if grp[0] == g][0]
        parts = [_blocks_from_full(gw[n].astype(bf16), SHARD_AXIS[n]) for n in members]
        out.append(parts[0] if len(parts) == 1 else jnp.concatenate(parts, axis=1 + join))
    return out


def _small_pack(d):
    flat = jnp.concatenate([d[n].reshape(-1) for n, _ in REPLICATED])
    return jnp.pad(flat, (0, SMALL_ROWS * LANES - REPL_ELEMS)).reshape(SMALL_ROWS, LANES)


def _small_unpack(packed):
    out, flat, off = {}, packed.reshape(-1), 0
    for n, shape in REPLICATED:
        k = _rows_of((LANES,) + shape)
        out[n] = flat[off:off + k].reshape(shape)
        off += k
    return out


def kernel(x, mem, pre1_g, post1_g, pre2_g, post2_g, mem_norm_g, w_in, fox_f_bias, rwkv_mu, rwkv_w0, rwkv_w_up, rwkv_a0, rwkv_a_up, rwkv_g_up, rwkv_k_k, rwkv_k_a, rwkv_r_k, rwkv_gn_g, rwkv_gn_b, w_mem_kv, w_fox_out, w_rwkv_out, w_mem_out, w_o, w_ffn_gate, w_ffn_up, w_ffn_down, loss_target, m_pre1_g, m_post1_g, m_pre2_g, m_post2_g, m_mem_norm_g, m_w_in, m_fox_f_bias, m_rwkv_mu, m_rwkv_w0, m_rwkv_w_up, m_rwkv_a0, m_rwkv_a_up, m_rwkv_g_up, m_rwkv_k_k, m_rwkv_k_a, m_rwkv_r_k, m_rwkv_gn_g, m_rwkv_gn_b, m_w_mem_kv, m_w_fox_out, m_w_rwkv_out, m_w_mem_out, m_w_o, m_w_ffn_gate, m_w_ffn_up, m_w_ffn_down, v_pre1_g, v_post1_g, v_pre2_g, v_post2_g, v_mem_norm_g, v_w_in, v_fox_f_bias, v_rwkv_mu, v_rwkv_w0, v_rwkv_w_up, v_rwkv_a0, v_rwkv_a_up, v_rwkv_g_up, v_rwkv_k_k, v_rwkv_k_a, v_rwkv_r_k, v_rwkv_gn_g, v_rwkv_gn_b, v_w_mem_kv, v_w_fox_out, v_w_rwkv_out, v_w_mem_out, v_w_o, v_w_ffn_gate, v_w_ffn_up, v_w_ffn_down):
    args = dict(locals())
    wts = {n: args[n] for n in WEIGHT_ORDER}
    ms = {n: args["m_" + n] for n in WEIGHT_ORDER}
    vs = {n: args["v_" + n] for n in WEIGHT_ORDER}

    groups = {g: (members, join) for g, members, join in GROUPS}
    w_bf16 = {n: wts[n].astype(bf16) for n, _, _ in SHARDED}

    def send(g):
        return _group_local(w_bf16, *groups[g])

    first = _exchange("gather_first", [send(g) for g in FIRST_GROUPS], per_peer=False)
    full = _assemble(first, FIRST_GROUPS)
    small_in = {n: (wts[n] if n == "rwkv_r_k" else wts[n].reshape(wts[n].shape[-2:])) for n, _ in REPLICATED}
    late = ([send(g) for g in LATE_GROUPS[0]], [send(g) for g in LATE_GROUPS[1]],
            lambda got, which: _assemble(got, LATE_GROUPS[which]))
    loss_part, grad_x, gw, gp, early_got = _local_step(
        x, mem, loss_target, full, small_in, late=late, early=lambda g: _grad_blocks(g, EARLY_GRAD_GROUPS))

    small_send = jnp.broadcast_to(_small_pack(gp).astype(bf16)[None], (N_DEV, SMALL_ROWS, LANES))
    *last_got, small_got = _exchange("exchange_last", _grad_blocks(gw, LAST_GRAD_GROUPS) + [small_send], per_peer=True)
    received = dict(zip(EARLY_GRAD_GROUPS + LAST_GRAD_GROUPS, list(early_got) + list(last_got)))

    outs = [{}, {}, {}, {}]
    for g, members, join in GROUPS:
        res = _adamw("adamw_" + g, received[g], *[_group_local(d, members, join) for d in (wts, ms, vs)])
        for o, arr in zip(outs, res):
            o.update(_group_split(arr, members, join))
    res = _adamw("adamw_small", small_got, *[_small_pack(d) for d in (wts, ms, vs)])
    for o, arr in zip(outs, res):
        o.update(_small_unpack(arr))
    loss = lax.psum(loss_part[0, 0], ("x", "y", "c"))
    return (loss, grad_x, *[o[n].reshape(wts[n].shape) for o in outs for n in WEIGHT_ORDER])
```

```python
import functools

import jax
import jax.numpy as jnp
from jax import lax
from jax.experimental import pallas as pl
from jax.experimental.pallas import tpu as pltpu

f32 = jnp.float32
bf16 = jnp.bfloat16
_HI = lax.Precision.HIGHEST

D = 1024
HEADS = 8
HD = 64
HW = HEADS * HD
MEM_HEADS = 4
MEM_HD = 128
MEM_W = 512
MEM_LEN = 256
D_FF = 2816
LORA_PAD = 128
RW_COLS = 3 * HW + 3 * LORA_PAD
NORM_EPS = 1e-6
GN_EPS = 64e-5
Q_BLOCK = 128
SCAN_CHUNK = 64
N_DEV = 8
LANES = 1024
VMEM_LIMIT = 56 * 1024 * 1024

ADAM_LR = 0.001
ADAM_B1 = 0.9
ADAM_B2 = 0.999
ADAM_EPS = 1e-08
ADAM_WD = 0.01
ADAM_STEP = 10

SHARDED = (
    ("w_in", (1024, 6920), 1),
    ("w_ffn_gate", (1024, 2816), 1),
    ("w_ffn_up", (1024, 2816), 1),
    ("w_ffn_down", (2816, 1024), 0),
    ("w_mem_kv", (1024, 1024), 0),
    ("w_o", (1024, 1024), 0),
    ("w_fox_out", (512, 1024), 1),
    ("w_rwkv_out", (512, 1024), 1),
    ("w_mem_out", (512, 1024), 1),
    ("rwkv_w_up", (64, 512), 1),
    ("rwkv_a_up", (64, 512), 1),
    ("rwkv_g_up", (128, 512), 1),
)
REPLICATED = (
    ("pre1_g", (1, 1024)), ("post1_g", (1, 1024)), ("pre2_g", (1, 1024)), ("post2_g", (1, 1024)),
    ("mem_norm_g", (1, 1024)), ("fox_f_bias", (1, 8)), ("rwkv_mu", (1, 1792)), ("rwkv_w0", (1, 512)),
    ("rwkv_a0", (1, 512)), ("rwkv_k_k", (1, 512)), ("rwkv_k_a", (1, 512)), ("rwkv_r_k", (1, 8, 64)),
    ("rwkv_gn_g", (1, 512)), ("rwkv_gn_b", (1, 512)),
)
WEIGHT_ORDER = ('pre1_g', 'post1_g', 'pre2_g', 'post2_g', 'mem_norm_g', 'w_in', 'fox_f_bias', 'rwkv_mu',
                'rwkv_w0', 'rwkv_w_up', 'rwkv_a0', 'rwkv_a_up', 'rwkv_g_up', 'rwkv_k_k', 'rwkv_k_a',
                'rwkv_r_k', 'rwkv_gn_g', 'rwkv_gn_b', 'w_mem_kv', 'w_fox_out', 'w_rwkv_out', 'w_mem_out',
                'w_o', 'w_ffn_gate', 'w_ffn_up', 'w_ffn_down')


def _block_shape(shape, axis):
    return tuple(s // N_DEV if i == axis else s for i, s in enumerate(shape))


def _rows_of(shape):
    n = 1
    for s in shape:
        n *= s
    return n // LANES


SHARD_ROWS = sum(_rows_of(_block_shape(s, a)) for _, s, a in SHARDED)
REPL_ELEMS = sum(_rows_of((LANES,) + s) for _, s in REPLICATED)
REPL_ROWS = -(-REPL_ELEMS // LANES)
PACK_ROWS = -(-(SHARD_ROWS + REPL_ROWS) // 128) * 128
GATHER_ROWS = -(-SHARD_ROWS // 16) * 16


def _cp(sem=None):
    return pltpu.CompilerParams(dimension_semantics=sem, vmem_limit_bytes=VMEM_LIMIT)


def _tile(dim, cap):
    best = None
    for t in range(128, min(dim, cap) + 1, 128):
        if dim % t == 0:
            best = t
    return best if best is not None else dim


def _two_terms(x):
    hi = x.astype(bf16)
    return hi, (x - hi.astype(f32)).astype(bf16)


def _dg(a, b, dims, exact):
    if exact == "split":
        (a_hi, a_lo), (b_hi, b_lo) = _two_terms(a), _two_terms(b)
        dot = functools.partial(lax.dot_general, dimension_numbers=dims, preferred_element_type=f32)
        return dot(a_hi, b_hi) + (dot(a_hi, b_lo) + dot(a_lo, b_hi))
    if exact:
        return lax.dot_general(a, b, dims, precision=_HI, preferred_element_type=f32)
    return lax.dot_general(a.astype(bf16), b.astype(bf16), dims, preferred_element_type=f32)


def _make_mm(batched, exact):
    o = 1 if batched else 0
    bd = ((0,), (0,)) if batched else ((), ())
    d_nn = (((1 + o,), (o,)), bd)
    d_nt = (((1 + o,), (1 + o,)), bd)
    d_tn = (((o,), (o,)), bd)

    @jax.custom_vjp
    def nn(a, b):
        return _dg(a, b, d_nn, exact)

    @jax.custom_vjp
    def nt(a, b):
        return _dg(a, b, d_nt, exact)

    @jax.custom_vjp
    def tn(a, b):
        return _dg(a, b, d_tn, exact)

    nn.defvjp(lambda a, b: (_dg(a, b, d_nn, exact), (a, b)),
              lambda res, g: (_dg(g, res[1], d_nt, exact), _dg(res[0], g, d_tn, exact)))
    nt.defvjp(lambda a, b: (_dg(a, b, d_nt, exact), (a, b)),
              lambda res, g: (_dg(g, res[1], d_nn, exact), _dg(g, res[0], d_tn, exact)))
    tn.defvjp(lambda a, b: (_dg(a, b, d_tn, exact), (a, b)),
              lambda res, g: (_dg(res[1], g, d_nt, exact), _dg(res[0], g, d_nn, exact)))
    return nn, nt, tn


def _sigmoid(x):
    return 1.0 / (1.0 + jnp.exp(-x))


def _head_sum_raw(x):
    width = 2 * HD
    i = lax.broadcasted_iota(jnp.int32, (width, width), 0) // HD
    j = lax.broadcasted_iota(jnp.int32, (width, width), 1) // HD
    m = (i == j).astype(bf16)
    dims = (((1,), (0,)), ((), ()))
    out = []
    for p in range(x.shape[1] // width):
        xp = x[:, p * width:(p + 1) * width]
        hi = xp.astype(bf16)
        lo = (xp - hi.astype(f32)).astype(bf16)
        out.append(lax.dot_general(hi, m, dims, preferred_element_type=f32)
                   + lax.dot_general(lo, m, dims, preferred_element_type=f32))
    return jnp.concatenate(out, axis=1)


@jax.custom_vjp
def _head_sum(x):
    return _head_sum_raw(x)


_head_sum.defvjp(lambda x: (_head_sum_raw(x), None), lambda _, g: (_head_sum_raw(g),))


WEIGHT_TILE_BYTES = 13 * 512 * 1024
ACC_TILE_BYTES = 8 * 1024 * 1024


def _matmul(name, a, b, mode, add=None, out_dtype=f32):
    has_add = add is not None
    if mode == "tn":
        (k, m), (_, n) = a.shape, b.shape
        tn = _tile(n, max(128, ACC_TILE_BYTES // (4 * m)))
        tk = _tile(k, 1024)

        def body(a_ref, b_ref, o_ref):
            @pl.when(pl.program_id(1) == 0)
            def _():
                o_ref[...] = jnp.zeros_like(o_ref)

            o_ref[...] += lax.dot_general(a_ref[...].astype(bf16), b_ref[...].astype(bf16),
                                          (((0,), (0,)), ((), ())), preferred_element_type=f32)

        return pl.pallas_call(
            body, name=name, grid=(n // tn, k // tk),
            in_specs=[pl.BlockSpec((tk, m), lambda j, kk: (kk, 0)), pl.BlockSpec((tk, tn), lambda j, kk: (kk, j))],
            out_specs=pl.BlockSpec((m, tn), lambda j, kk: (0, j)), out_shape=jax.ShapeDtypeStruct((m, n), f32),
            compiler_params=_cp(("parallel", "arbitrary")),
        )(a, b)

    (m, k) = a.shape
    n = b.shape[1] if mode == "nn" else b.shape[0]
    tm = _tile(m, 512)
    tn = _tile(n, max(128, WEIGHT_TILE_BYTES // (2 * k)))
    dims = (((1,), (0,)), ((), ())) if mode == "nn" else (((1,), (1,)), ((), ()))
    b_spec = pl.BlockSpec((k, tn), lambda j, i: (0, j)) if mode == "nn" else pl.BlockSpec((tn, k), lambda j, i: (j, 0))
    o_spec = pl.BlockSpec((tm, tn), lambda j, i: (i, j))

    def body(*refs):
        a_ref, b_ref = refs[0], refs[1]
        o_ref = refs[-1]
        r = lax.dot_general(a_ref[...].astype(bf16), b_ref[...].astype(bf16), dims, preferred_element_type=f32)
        if has_add:
            r = r + refs[2][...]
        o_ref[...] = r.astype(o_ref.dtype)

    return pl.pallas_call(
        body, name=name, grid=(n // tn, m // tm),
        in_specs=[pl.BlockSpec((tm, k), lambda j, i: (i, 0)), b_spec] + ([o_spec] if has_add else []),
        out_specs=o_spec, out_shape=jax.ShapeDtypeStruct((m, n), out_dtype),
        compiler_params=_cp(("parallel", "arbitrary")),
    )(*((a, b, add) if has_add else (a, b)))


def _pieces(ref, widths):
    out, off = [], 0
    for w in widths:
        out.append(ref[:, off:off + w].astype(f32))
        off += w
    return out


def _store_pieces(ref, widths, vals, add_ref=None):
    off = 0
    for w, v in zip(widths, vals):
        ref[:, off:off + w] = (v if add_ref is None else v + add_ref[:, off:off + w]).astype(ref.dtype)
        off += w


def _rows_fwd(name, fn, consts, rows, params, outs, n_sums=0, tm=256, dtypes=None):
    t = (consts + rows)[0][0].shape[0]
    tm = min(tm, t)
    ins = consts + rows
    n_in, n_p, n_o = len(ins), len(params), len(outs)
    dtypes = dtypes or [f32] * n_o

    def body(*refs):
        in_refs, p_refs = refs[:n_in], refs[n_in:n_in + n_p]
        o_refs, s_refs = refs[n_in + n_p:n_in + n_p + n_o], refs[n_in + n_p + n_o:]
        vals = []
        for r, (_, widths) in zip(in_refs, ins):
            vals += _pieces(r, widths)
        res = fn(*vals, *[p[...] for p in p_refs])
        pos = 0
        for r, widths in zip(o_refs, outs):
            _store_pieces(r, widths, res[pos:pos + len(widths)])
            pos += len(widths)

        @pl.when(pl.program_id(0) == 0)
        def _():
            for s in s_refs:
                s[...] = jnp.zeros_like(s)

        for s, v in zip(s_refs, res[pos:]):
            s[...] += v

    row_spec = lambda w: pl.BlockSpec((tm, w), lambda i: (i, 0))
    full = lambda p: pl.BlockSpec(p.shape, lambda i: (0,) * p.ndim)
    return pl.pallas_call(
        body, name=name, grid=(t // tm,),
        in_specs=[row_spec(a.shape[1]) for a, _ in ins] + [full(p) for p in params],
        out_specs=[row_spec(sum(w)) for w in outs] + [pl.BlockSpec((1, 1), lambda i: (0, 0))] * n_sums,
        out_shape=[jax.ShapeDtypeStruct((t, sum(w)), dt) for w, dt in zip(outs, dtypes)] + [jax.ShapeDtypeStruct((1, 1), f32)] * n_sums,
        compiler_params=_cp(("arbitrary",)),
    )(*[a for a, _ in ins], *params)


def _rows_bwd(name, fn, consts, rows, params, outs, cts, n_sums=0, add=None, tm=256, dtypes=None):
    t = (consts + rows)[0][0].shape[0]
    tm = min(tm, t)
    n_c, n_r, n_p, n_o = len(consts), len(rows), len(params), len(outs)
    has_add = add is not None
    dtypes = dtypes or [f32] * n_r

    def body(*refs):
        pos = 0
        c_refs = refs[pos:pos + n_c]; pos += n_c
        r_refs = refs[pos:pos + n_r]; pos += n_r
        p_refs = refs[pos:pos + n_p]; pos += n_p
        ct_refs = refs[pos:pos + n_o]; pos += n_o
        add_ref = refs[pos] if has_add else None
        pos += 1 if has_add else 0
        dr_refs = refs[pos:pos + n_r]; pos += n_r
        dp_refs = refs[pos:pos + n_p]
        cvals, rvals = [], []
        for r, (_, widths) in zip(c_refs, consts):
            cvals += _pieces(r, widths)
        for r, (_, widths) in zip(r_refs, rows):
            rvals += _pieces(r, widths)
        pvals = [p[...] for p in p_refs]
        ctv = []
        for r, widths in zip(ct_refs, outs):
            ctv += _pieces(r, widths)
        ctv += [jnp.ones((1, 1), f32)] * n_sums
        _, vjp = jax.vjp(lambda *rp: tuple(fn(*cvals, *rp)), *rvals, *pvals)
        g = vjp(tuple(ctv))
        pos = 0
        for idx, (r, (_, widths)) in enumerate(zip(dr_refs, rows)):
            _store_pieces(r, widths, g[pos:pos + len(widths)], add_ref if idx == 0 else None)
            pos += len(widths)

        @pl.when(pl.program_id(0) == 0)
        def _():
            for dp in dp_refs:
                dp[...] = jnp.zeros_like(dp)

        for dp, v in zip(dp_refs, g[pos:]):
            dp[...] += v

    row_spec = lambda w: pl.BlockSpec((tm, w), lambda i: (i, 0))
    full = lambda p: pl.BlockSpec(p.shape, lambda i: (0,) * p.ndim)
    args = [a for a, _ in consts + rows] + list(params) + list(cts) + ([add] if has_add else [])
    res = pl.pallas_call(
        body, name=name, grid=(t // tm,),
        in_specs=[row_spec(a.shape[1]) for a, _ in consts + rows] + [full(p) for p in params]
        + [row_spec(sum(w)) for w in outs] + ([row_spec(add.shape[1])] if has_add else []),
        out_specs=[row_spec(a.shape[1]) for a, _ in rows] + [full(p) for p in params],
        out_shape=[jax.ShapeDtypeStruct(a.shape, dt) for (a, _), dt in zip(rows, dtypes)]
        + [jax.ShapeDtypeStruct(p.shape, f32) for p in params],
        compiler_params=_cp(("arbitrary",)),
    )(*args)
    return res[:n_r], res[n_r:]


def _rms(x, g):
    return x * lax.rsqrt(jnp.mean(x * x, axis=-1, keepdims=True) + NORM_EPS) * g


def _fn_rms(x, g):
    return (_rms(x, g),)


def _fn_rwkv_pre(r, k, v, wd, ad, gd, w0, w_up, a0, a_up, g_up, k_k, k_a):
    nn, _, _ = _make_mm(False, False)
    w_log = -_sigmoid(w0 + nn(jnp.tanh(wd), w_up)) * 0.6065306597126334
    a = _sigmoid(a0 + nn(ad, a_up))
    g = nn(_sigmoid(gd), g_up)
    kk = k * k_k
    kk = kk * lax.rsqrt(jnp.maximum(_head_sum(kk * kk), 1e-24))
    k2 = k * (1.0 + (a - 1.0) * k_a)
    return r, w_log, k2, v, -kk, kk * a, g


def _fn_rwkv_post(y, r, k2, v, g, gn_g, gn_b, r_k):
    mean = _head_sum(y) * (1.0 / HD)
    yc = y - mean
    var = _head_sum(yc * yc) * (1.0 / HD)
    yn = yc * lax.rsqrt(var + GN_EPS) * gn_g + gn_b
    bonus = _head_sum(r * k2 * r_k) * v
    return ((yn + bonus) * g,)


def _fn_merge(a_fox, a_rwkv, a_mem, g_fox, g_rwkv, g_mem):
    return (_sigmoid(g_fox) * a_fox + _sigmoid(g_rwkv) * a_rwkv + _sigmoid(g_mem) * a_mem,)


def _fn_post1(y, x, post1_g, pre2_g):
    h1 = x + _rms(y, post1_g)
    return h1, _rms(h1, pre2_g)


def _fn_swiglu(gp, up):
    return (gp * _sigmoid(gp) * up,)


def _fn_final(target, ffn, h1, post2_g):
    err = h1 + _rms(ffn, post2_g) - target
    per_row = jnp.mean(err * err, axis=-1, keepdims=True)
    return (0.5 * jnp.sum(per_row, axis=0, keepdims=True),)


def _shift_down(x):
    row = lax.broadcasted_iota(jnp.int32, x.shape, 0)
    return jnp.where(row == 0, 0.0, pltpu.roll(x, 1, 0))


def _shift_up(x):
    s = x.shape[0]
    row = lax.broadcasted_iota(jnp.int32, x.shape, 0)
    return jnp.where(row == s - 1, 0.0, pltpu.roll(x, s - 1, 0))


def _tokshift_fwd(p, mu, batch, seq):
    w = p.shape[1]
    tc = _tile(w, 384)

    def body(p_ref, mu_ref, o_ref):
        x = p_ref[...]
        o_ref[...] = x + (_shift_down(x) - x) * mu_ref[...]

    return pl.pallas_call(
        body, name="tokshift_fwd", grid=(w // tc, batch),
        in_specs=[pl.BlockSpec((seq, tc), lambda j, b: (b, j)), pl.BlockSpec((1, tc), lambda j, b: (0, j))],
        out_specs=pl.BlockSpec((seq, tc), lambda j, b: (b, j)),
        out_shape=jax.ShapeDtypeStruct(p.shape, f32),
        compiler_params=_cp(("parallel", "arbitrary")),
    )(p, mu)


def _tokshift_bwd(p, mu, dps, batch, seq):
    w = p.shape[1]
    tc = _tile(w, 384)

    def body(p_ref, mu_ref, d_ref, dp_ref, dmu_ref):
        x, mu_v, d = p_ref[...], mu_ref[...], d_ref[...]
        dp_ref[...] = (d * (1.0 - mu_v) + _shift_up(d * mu_v)).astype(dp_ref.dtype)

        @pl.when(pl.program_id(1) == 0)
        def _():
            dmu_ref[...] = jnp.zeros_like(dmu_ref)

        dmu_ref[...] += jnp.sum(d * (_shift_down(x) - x), axis=0, keepdims=True)

    return pl.pallas_call(
        body, name="tokshift_bwd", grid=(w // tc, batch),
        in_specs=[pl.BlockSpec((seq, tc), lambda j, b: (b, j)), pl.BlockSpec((1, tc), lambda j, b: (0, j)),
                  pl.BlockSpec((seq, tc), lambda j, b: (b, j))],
        out_specs=[pl.BlockSpec((seq, tc), lambda j, b: (b, j)), pl.BlockSpec((1, tc), lambda j, b: (0, j))],
        out_shape=[jax.ShapeDtypeStruct(p.shape, bf16), jax.ShapeDtypeStruct(mu.shape, f32)],
        compiler_params=_cp(("parallel", "arbitrary")),
    )(p, mu, dps)


def _cum_block(seq):
    return _tile(seq, 256)


def _fox_gate_fwd(f, bias, batch, seq):
    cb = _cum_block(seq)

    def body(f_ref, b_ref, c_ref):
        row = lax.broadcasted_iota(jnp.int32, (cb, cb), 0)
        col = lax.broadcasted_iota(jnp.int32, (cb, cb), 1)
        tri = (col <= row).astype(f32)
        carry = jnp.zeros((1, 128), f32)
        for i in range(seq // cb):
            z = f_ref[i * cb:(i + 1) * cb, :] + b_ref[...]
            ls = jnp.minimum(z, 0.0) - jnp.log(1.0 + jnp.exp(-jnp.abs(z)))
            c = _dg(tri, ls, (((1,), (0,)), ((), ())), True) + carry
            c_ref[i * cb:(i + 1) * cb, :] = c
            carry = c[cb - 1:cb, :]

    return pl.pallas_call(
        body, name="fox_gate_fwd", grid=(batch,),
        in_specs=[pl.BlockSpec((seq, 128), lambda b: (b, 0)), pl.BlockSpec((1, 128), lambda b: (0, 0))],
        out_specs=pl.BlockSpec((seq, 128), lambda b: (b, 0)),
        out_shape=jax.ShapeDtypeStruct(f.shape, f32),
        compiler_params=_cp(("arbitrary",)),
    )(f, bias)


def _fox_gate_bwd(f, bias, dc_a, dc_b, batch, seq):
    cb = _cum_block(seq)

    def body(f_ref, b_ref, da_ref, db_ref, df_ref, dbias_ref):
        row = lax.broadcasted_iota(jnp.int32, (cb, cb), 0)
        col = lax.broadcasted_iota(jnp.int32, (cb, cb), 1)
        triu = (col >= row).astype(f32)

        @pl.when(pl.program_id(0) == 0)
        def _():
            dbias_ref[...] = jnp.zeros_like(dbias_ref)

        lane = lax.broadcasted_iota(jnp.int32, (1, 128), 1)

        def by_head(blk):
            out = jnp.zeros((cb, 128), f32)
            for p in range(HEADS // 2):
                for e in range(2):
                    out = jnp.where(lane == 2 * p + e, _pick_lane(blk[:, p * 128:(p + 1) * 128], e), out)
            return out

        carry = jnp.zeros((1, 128), f32)
        tot = jnp.zeros((1, 128), f32)
        for i in reversed(range(seq // cb)):
            sl = slice(i * cb, (i + 1) * cb)
            dc = by_head(da_ref[sl, :] + db_ref[sl, :])
            dls = _dg(triu, dc, (((1,), (0,)), ((), ())), True) + carry
            carry = dls[0:1, :]
            df = dls * _sigmoid(-(f_ref[sl, :] + b_ref[...]))
            df_ref[sl, :] = df.astype(df_ref.dtype)
            tot = tot + jnp.sum(df, axis=0, keepdims=True)
        dbias_ref[...] += tot

    return pl.pallas_call(
        body, name="fox_gate_bwd", grid=(batch,),
        in_specs=[pl.BlockSpec((seq, 128), lambda b: (b, 0)), pl.BlockSpec((1, 128), lambda b: (0, 0)),
                  pl.BlockSpec((seq, HW), lambda b: (b, 0)), pl.BlockSpec((seq, HW), lambda b: (b, 0))],
        out_specs=[pl.BlockSpec((seq, 128), lambda b: (b, 0)), pl.BlockSpec((1, 128), lambda b: (0, 0))],
        out_shape=[jax.ShapeDtypeStruct(f.shape, bf16), jax.ShapeDtypeStruct((1, 128), f32)],
        compiler_params=_cp(("arbitrary",)),
    )(f, bias, dc_a, dc_b)


_HBM_SPEC = pl.BlockSpec(memory_space=pltpu.HBM)


def _side_out_shapes(srcs, per_peer):
    return [jax.ShapeDtypeStruct(((N_DEV,) + tuple(s.shape[1:] if per_peer else s.shape)), s.dtype) for s in srcs]


def _side_sems(n):
    if n == 0:
        return []
    return [pltpu.SemaphoreType.DMA((n, N_DEV - 1)), pltpu.SemaphoreType.DMA((n, N_DEV - 1)), pltpu.SemaphoreType.DMA((n,))]


def _peer_copies(src_refs, dst_refs, per_peer, sems):
    send_sems, recv_sems, local_sems = sems
    x, y, c = lax.axis_index("x"), lax.axis_index("y"), lax.axis_index("c")
    me = 4 * x + 2 * y + c

    def remote(src, dst, t, k, to):
        return pltpu.make_async_remote_copy(src_ref=src, dst_ref=dst, send_sem=send_sems.at[t, k - 1],
                                            recv_sem=recv_sems.at[t, k - 1], device_id=to,
                                            device_id_type=pl.DeviceIdType.MESH)

    direct, relays = [], []
    for t, (s, d) in enumerate(zip(src_refs, dst_refs)):
        direct.append((t, 0, pltpu.make_async_copy(s.at[me] if per_peer else s, d.at[me], local_sems.at[t])))
        for k in range(1, N_DEV):
            px = 1 - x if k & 4 else x
            py = 1 - y if k & 2 else y
            pc = 1 - c if k & 1 else c
            if per_peer:
                direct.append((t, k, remote(s.at[4 * px + 2 * py + pc], d.at[me], t, k, (px, py, pc))))
            elif k == 1 or not k & 1:
                direct.append((t, k, remote(s, d.at[me], t, k, (px, py, pc))))
            else:
                origin = d.at[4 * px + 2 * py + c]
                relays.append((t, k - 1, remote(origin, origin, t, k, (x, y, 1 - c))))
    return direct, relays


def _exchange_start(direct):
    for _, _, cp in direct:
        cp.start()


def _exchange_finish(direct, relays):
    landed = {(t, k): cp for t, k, cp in direct}
    for t, j, cp in relays:
        landed[(t, j)].wait_recv()
        cp.start()
    relayed = {(t, j) for t, j, _ in relays}
    for t, k, cp in direct:
        if k == 0:
            cp.wait()
        else:
            cp.wait_send()
            if (t, k) not in relayed:
                cp.wait_recv()
    for _, _, cp in relays:
        cp.wait()


def _side_exchange(src_refs, dst_refs, per_peer, sems, *grid):
    if not src_refs:
        return
    first = functools.reduce(jnp.logical_and, [pl.program_id(a) == 0 for a in range(len(grid))])
    last = functools.reduce(jnp.logical_and, [pl.program_id(a) == n - 1 for a, n in enumerate(grid)])

    @pl.when(first)
    def _():
        _exchange_start(_peer_copies(src_refs, dst_refs, per_peer, sems)[0])

    @pl.when(last)
    def _():
        _exchange_finish(*_peer_copies(src_refs, dst_refs, per_peer, sems))


def _exchange(name, srcs, per_peer):
    n = len(srcs)

    def body(*refs):
        direct, relays = _peer_copies(refs[:n], refs[n:2 * n], per_peer, refs[2 * n:])
        _exchange_start(direct)
        _exchange_finish(direct, relays)

    return pl.pallas_call(
        body, name=name, in_specs=[_HBM_SPEC] * n, out_specs=[_HBM_SPEC] * n,
        out_shape=_side_out_shapes(srcs, per_peer), scratch_shapes=_side_sems(n),
    )(*srcs)


FOX_T = 512
_NEG = -1e30
_D2 = (((1,), (1,)), ((), ()))
_D1 = (((1,), (0,)), ((), ()))
_D0 = (((0,), (0,)), ((), ()))


def _bdot(a, b, dims):
    return lax.dot_general(a.astype(bf16), b.astype(bf16), dims, preferred_element_type=f32)


def _pick_lane(x, lane):
    idx = lax.broadcasted_iota(jnp.int32, x.shape, 1)
    return jnp.sum(jnp.where(idx == lane, x, 0.0), axis=1, keepdims=True)


def _pick_row(x, row):
    idx = lax.broadcasted_iota(jnp.int32, x.shape, 0)
    return jnp.sum(jnp.where(idx == row, x, 0.0), axis=0, keepdims=True)


def _fox_fwd(qkv, c, c_rows, batch, seq, side=None):
    t = min(FOX_T, seq)
    nq = seq // t
    scale = HD ** -0.5
    srcs, per_peer = side if side is not None else ([], False)
    n_s = len(srcs)

    def body(*refs):
        q_ref, k_ref, v_ref, cq_ref, ck_ref = refs[:5]
        o_ref, lse_ref = refs[5 + n_s:7 + n_s]
        _side_exchange(refs[5:5 + n_s], refs[7 + n_s:7 + 2 * n_s], per_peer, refs[7 + 2 * n_s:], batch, PAIRS, nq)
        pair, i = pl.program_id(1), pl.program_id(2)
        lane = lax.broadcasted_iota(jnp.int32, (1, PAIR_W), 1)
        first = (lane // HD) == 0
        mine = [first, jnp.logical_not(first)]
        q = q_ref[...] * scale
        qs = [jnp.where(mine[e], q, 0.0) for e in range(2)]
        cqs = [_pick_lane(cq_ref[...], 2 * pair + e) for e in range(2)]
        causal = lax.broadcasted_iota(jnp.int32, (t, t), 1) <= lax.broadcasted_iota(jnp.int32, (t, t), 0)

        def block(j, carry, diagonal):
            rows = pl.ds(pl.multiple_of(j * t, t), t)
            kj, vj = k_ref[rows, :], v_ref[rows, :]
            ck_blk = ck_ref[0, :, rows]
            out = []
            for e in range(2):
                m, acc = carry[2 * e:2 * e + 2]
                s = _bdot(qs[e], kj, _D2) + cqs[e] - _pick_row(ck_blk, 2 * pair + e)
                if diagonal:
                    s = jnp.where(causal, s, _NEG)
                m_new = jnp.maximum(m, jnp.max(s, axis=1, keepdims=True))
                p = jnp.exp(s - m_new)
                out += [m_new, jnp.exp(m - m_new) * acc + _bdot(p, jnp.where(mine[e], vj, 1.0), _D1)]
            return tuple(out)

        init = (jnp.full((t, 1), _NEG, f32), jnp.zeros((t, PAIR_W), f32)) * 2
        carry = lax.fori_loop(0, i, lambda j, cr: block(j, cr, False), init)
        m0, a0, m1, a1 = block(i, carry, True)
        l0, l1 = _pick_lane(a0, HD), _pick_lane(a1, 0)
        o_ref[...] = jnp.where(first, a0 / l0, a1 / l1)
        lse_ref[...] = jnp.where(lane == 0, m0 + jnp.log(l0), jnp.where(lane == 1, m1 + jnp.log(l1), 0.0))

    q_spec = pl.BlockSpec((t, PAIR_W), lambda b, p, i: (b * nq + i, p))
    res = pl.pallas_call(
        body, name="fox_attn_fwd", grid=(batch, PAIRS, nq),
        in_specs=[q_spec,
                  pl.BlockSpec((seq, PAIR_W), lambda b, p, i: (b, PAIRS + p)),
                  pl.BlockSpec((seq, PAIR_W), lambda b, p, i: (b, 2 * PAIRS + p)),
                  pl.BlockSpec((t, 128), lambda b, p, i: (b * nq + i, 0)),
                  pl.BlockSpec((1, 8, seq), lambda b, p, i: (b, 0, 0))] + [_HBM_SPEC] * n_s,
        out_specs=[q_spec, q_spec] + [_HBM_SPEC] * n_s,
        out_shape=[jax.ShapeDtypeStruct((batch * seq, HW), f32)] * 2 + _side_out_shapes(srcs, per_peer),
        scratch_shapes=_side_sems(n_s),
        compiler_params=_cp(("arbitrary", "arbitrary", "arbitrary")),
    )(qkv, qkv, qkv, c, c_rows, *srcs)
    return res[0], res[1], list(res[2:])


def _fox_bwd(qkv, c, c_rows, o, lse, do, batch, seq):
    t = min(FOX_T, seq)
    nq = seq // t
    scale = HD ** -0.5

    def body(q_ref, k_ref, v_ref, cq_ref, ck_ref, o_ref, lse_ref, do_ref,
             dq_ref, dk_ref, dv_ref, dcq_ref, dck_ref, acc0, acc1):
        pair, i = pl.program_id(1), pl.program_id(2)
        accs = [acc0, acc1]

        @pl.when(i == 0)
        def _():
            dv_ref[...] = jnp.zeros_like(dv_ref)
            acc0[...] = jnp.zeros_like(acc0)
            acc1[...] = jnp.zeros_like(acc1)

        lane = lax.broadcasted_iota(jnp.int32, (1, PAIR_W), 1)
        first = (lane // HD) == 0
        mine = [first, jnp.logical_not(first)]
        q, d_o, o_i = q_ref[...] * scale, do_ref[...], o_ref[...]
        q0s = [jnp.where(mine[e], q, 0.0) for e in range(2)]
        q1s = [jnp.where(mine[e], q, 1.0) for e in range(2)]
        dos = [jnp.where(mine[e], d_o, 0.0) for e in range(2)]
        deltas = [jnp.sum(dos[e] * o_i, axis=1, keepdims=True) for e in range(2)]
        lses = [_pick_lane(lse_ref[...], e) for e in range(2)]
        cqs = [_pick_lane(cq_ref[...], 2 * pair + e) for e in range(2)]
        causal = lax.broadcasted_iota(jnp.int32, (t, t), 1) <= lax.broadcasted_iota(jnp.int32, (t, t), 0)

        def block(j, dqs, diagonal):
            rows = pl.ds(pl.multiple_of(j * t, t), t)
            kj, vj = k_ref[rows, :], v_ref[rows, :]
            ck_blk = ck_ref[0, :, rows]
            out = []
            for e in range(2):
                s = _bdot(q0s[e], kj, _D2) + cqs[e] - _pick_row(ck_blk, 2 * pair + e)
                if diagonal:
                    s = jnp.where(causal, s, _NEG)
                p = jnp.exp(s - lses[e])
                ds = p * (_bdot(dos[e], vj, _D2) - deltas[e])
                dv_ref[rows, :] += _bdot(p, dos[e], _D0)
                accs[e][rows, :] += _bdot(ds, q1s[e], _D0)
                out.append(dqs[e] + _bdot(ds, jnp.where(mine[e], kj, 1.0), _D1))
            return tuple(out)

        zero = jnp.zeros((t, PAIR_W), f32)
        dqs = lax.fori_loop(0, i, lambda j, cr: block(j, cr, False), (zero, zero))
        dq0, dq1 = block(i, dqs, True)
        dq_ref[...] = jnp.where(first, dq0, dq1) * scale
        dcq_ref[...] = jnp.where(lane == 0, _pick_lane(dq0, HD), jnp.where(lane == 1, _pick_lane(dq1, 0), 0.0))

        @pl.when(i == nq - 1)
        def _():
            a0, a1 = acc0[...], acc1[...]
            dk_ref[...] = jnp.where(first, a0, a1)
            dck_ref[...] = jnp.where(lane == 0, -_pick_lane(a0, HD), jnp.where(lane == 1, -_pick_lane(a1, 0), 0.0))

    blk = lambda col: pl.BlockSpec((t, PAIR_W), lambda b, p, i: (b * nq + i, col * PAIRS + p))
    whole = lambda col: pl.BlockSpec((seq, PAIR_W), lambda b, p, i: (b, col * PAIRS + p))
    t_all = batch * seq
    return pl.pallas_call(
        body, name="fox_attn_bwd", grid=(batch, PAIRS, nq),
        in_specs=[blk(0), whole(1), whole(2),
                  pl.BlockSpec((t, 128), lambda b, p, i: (b * nq + i, 0)),
                  pl.BlockSpec((1, 8, seq), lambda b, p, i: (b, 0, 0)),
                  blk(0), blk(0), blk(0)],
        out_specs=[blk(0), whole(0), whole(0), blk(0), whole(0)],
        out_shape=[jax.ShapeDtypeStruct((t_all, HW), f32)] * 5,
        scratch_shapes=[pltpu.VMEM((seq, PAIR_W), f32), pltpu.VMEM((seq, PAIR_W), f32)],
        compiler_params=_cp(("parallel", "parallel", "arbitrary")),
    )(qkv, qkv, qkv, c, c_rows, o, lse, do)


def _mem_block(q, km, vm):
    nn, nt, _ = _make_mm(False, False)
    logits = nt(q, km) * (MEM_HD ** -0.5)
    m = lax.stop_gradient(jnp.max(logits, axis=-1, keepdims=True))
    e = jnp.exp(logits - m)
    return nn(e / jnp.sum(e, axis=-1, keepdims=True), vm)


def _mem_specs(seq, tq):
    nq = seq // tq
    qs = pl.BlockSpec((tq, MEM_HD), lambda b, h, i: (b * nq + i, h))
    ks = pl.BlockSpec((MEM_LEN, MEM_HD), lambda b, h, i: (b, h))
    vs = pl.BlockSpec((MEM_LEN, MEM_HD), lambda b, h, i: (b, MEM_HEADS + h))
    return nq, qs, ks, vs


def _mem_fwd(q, mem_kv, batch, seq):
    tq = min(512, seq)
    nq, qs, ks, vs = _mem_specs(seq, tq)

    def body(q_ref, k_ref, v_ref, o_ref):
        o_ref[...] = _mem_block(q_ref[...].astype(f32), k_ref[...], v_ref[...]).astype(o_ref.dtype)

    return pl.pallas_call(
        body, name="mem_attn_fwd", grid=(batch, MEM_HEADS, nq),
        in_specs=[qs, ks, vs], out_specs=qs, out_shape=jax.ShapeDtypeStruct(q.shape, bf16),
        compiler_params=_cp(("parallel", "parallel", "arbitrary")),
    )(q, mem_kv, mem_kv)


def _mem_bwd(q, mem_kv, do, batch, seq):
    tq = min(512, seq)
    nq, qs, ks, vs = _mem_specs(seq, tq)

    def body(q_ref, k_ref, v_ref, do_ref, dq_ref, dk_ref, dv_ref):
        _, vjp = jax.vjp(_mem_block, q_ref[...].astype(f32), k_ref[...], v_ref[...])
        dq, dk, dv = vjp(do_ref[...])
        dq_ref[...] = dq.astype(dq_ref.dtype)

        @pl.when(pl.program_id(2) == 0)
        def _():
            dk_ref[...] = jnp.zeros_like(dk_ref)
            dv_ref[...] = jnp.zeros_like(dv_ref)

        dk_ref[...] += dk
        dv_ref[...] += dv

    return pl.pallas_call(
        body, name="mem_attn_bwd", grid=(batch, MEM_HEADS, nq),
        in_specs=[qs, ks, vs, qs], out_specs=[qs, ks, ks],
        out_shape=[jax.ShapeDtypeStruct(q.shape, bf16), jax.ShapeDtypeStruct((batch * MEM_LEN, MEM_W), f32),
                   jax.ShapeDtypeStruct((batch * MEM_LEN, MEM_W), f32)],
        compiler_params=_cp(("parallel", "parallel", "arbitrary")),
    )(q, mem_kv, mem_kv, do)


@jax.custom_vjp
def _halves(x):
    c = x.shape[1] // 2
    return x[:, :c], x[:, c:]


_halves.defvjp(lambda x: ((x[:, :x.shape[1] // 2], x[:, x.shape[1] // 2:]), None),
               lambda _, g: (jnp.concatenate(g, axis=1),))


@jax.custom_vjp
def _lead_halves(x):
    n = x.shape[0] // 2
    return x[:n], x[n:]


_lead_halves.defvjp(lambda x: ((x[:x.shape[0] // 2], x[x.shape[0] // 2:]), None),
                    lambda _, g: (jnp.concatenate(g, axis=0),))


def _scan_chunk(s0, r, wl, k, v, a, b):
    nn, nt, tn = _make_mm(True, False)
    nn_exact, _, _ = _make_mm(True, True)
    _, nt_exact, _ = _make_mm(True, "split")
    hp, c, lanes = r.shape
    row = lax.broadcasted_iota(jnp.int32, (c, c), 0)
    col = lax.broadcasted_iota(jnp.int32, (c, c), 1)
    first = (lax.broadcasted_iota(jnp.int32, (1, 1, lanes), 2) // HD) == 0
    tri = jnp.broadcast_to((col <= row).astype(f32)[None], (hp, c, c))
    lg = nn_exact(tri, wl)
    lg_end = lg[:, c - 1:c, :]
    grow, shrink, to_end = jnp.exp(lg), jnp.exp(-lg), jnp.exp(lg_end - lg)
    rt, kt, bt, at = r * grow, k * shrink, b * shrink, a * jnp.exp(lg - wl)
    strict, incl = (col < row)[None], (col <= row)[None]
    twice = lambda t: jnp.concatenate([t, t], axis=0)
    queries = jnp.concatenate([at, rt], axis=1)
    per_head = jnp.concatenate([jnp.where(first, queries, 0.0), jnp.where(first, 0.0, queries)], axis=0)
    (ab, rb), (ak, rk) = _halves(nt_exact(per_head, twice(bt))), _halves(nt_exact(per_head, twice(kt)))
    l_ab = jnp.where(strict, ab, 0.0)
    a_ak = jnp.where(strict, ak, 0.0)
    a_rb = jnp.where(incl, rb, 0.0)
    a_rk = jnp.where(incl, rk, 0.0)
    inv = (col == row).astype(f32)[None] + l_ab
    power, n = l_ab, 1
    while 2 * n < c:
        power = nn(power, power)
        inv = inv + nn(inv, power)
        n *= 2

    def apply(m, t):
        lo, hi = _lead_halves(nn(m, twice(t)))
        return jnp.where(first, lo, hi)

    sa = apply(inv, nt(at, s0) + apply(a_ak, v))
    y = nt(rt, s0) + apply(a_rk, v) + apply(a_rb, sa)
    same_head = ((lax.broadcasted_iota(jnp.int32, (lanes, lanes), 0) // HD)
                 == (lax.broadcasted_iota(jnp.int32, (lanes, lanes), 1) // HD))[None]
    s1 = s0 * jnp.exp(lg_end) + jnp.where(same_head, tn(v, k * to_end) + tn(sa, b * to_end), 0.0)
    return y, s1


PAIRS = HEADS // 2
PAIR_W = 2 * HD


def _pair_stack(ref, off):
    return jnp.stack([ref[b, :, off + p * PAIR_W:off + (p + 1) * PAIR_W]
                      for b in range(ref.shape[0]) for p in range(PAIRS)])


def _pair_store(ref, off, val, add_ref=None):
    for b in range(ref.shape[0]):
        for p in range(PAIRS):
            sl = slice(off + p * PAIR_W, off + (p + 1) * PAIR_W)
            v = val[b * PAIRS + p]
            ref[b, :, sl] = v if add_ref is None else v + add_ref[b, :, sl]


def _scan_fwd(main6, batch, seq, side=None):
    c = min(SCAN_CHUNK, seq)
    nc = seq // c
    hp = batch * PAIRS
    srcs, per_peer = side if side is not None else ([], False)
    n_s = len(srcs)

    def body(*refs):
        z_ref, y_ref, s_ref, st = refs[0], refs[1 + n_s], refs[2 + n_s], refs[3 + 2 * n_s]
        _side_exchange(refs[1:1 + n_s], refs[3 + n_s:3 + 2 * n_s], per_peer, refs[4 + 2 * n_s:], nc)

        @pl.when(pl.program_id(0) == 0)
        def _():
            st[...] = jnp.zeros_like(st)

        s0 = st[...]
        s_ref[0] = s0
        y, s1 = _scan_chunk(s0, *[_pair_stack(z_ref, comp * HW) for comp in range(6)])
        _pair_store(y_ref, 0, y)
        st[...] = s1

    res = pl.pallas_call(
        body, name="rwkv_scan_fwd", grid=(nc,),
        in_specs=[pl.BlockSpec((batch, c, 6 * HW), lambda i: (0, i, 0))] + [_HBM_SPEC] * n_s,
        out_specs=[pl.BlockSpec((batch, c, HW), lambda i: (0, i, 0)),
                   pl.BlockSpec((1, hp, PAIR_W, PAIR_W), lambda i: (i, 0, 0, 0))] + [_HBM_SPEC] * n_s,
        out_shape=[jax.ShapeDtypeStruct((batch, seq, HW), f32), jax.ShapeDtypeStruct((nc, hp, PAIR_W, PAIR_W), f32)]
        + _side_out_shapes(srcs, per_peer),
        scratch_shapes=[pltpu.VMEM((hp, PAIR_W, PAIR_W), f32)] + _side_sems(n_s),
        compiler_params=_cp(("arbitrary",)),
    )(main6.reshape(batch, seq, 6 * HW), *srcs)
    return res[0].reshape(batch * seq, HW), res[1], list(res[2:])


def _scan_bwd(main6, states, dy, extra, batch, seq, side=None):
    c = min(SCAN_CHUNK, seq)
    nc = seq // c
    hp = batch * PAIRS
    srcs, per_peer = side if side is not None else ([], False)
    n_s = len(srcs)

    def body(*refs):
        z_ref, s_ref, dy_ref, ex_ref = refs[:4]
        dz_ref, dst = refs[4 + n_s], refs[5 + 2 * n_s]
        _side_exchange(refs[4:4 + n_s], refs[5 + n_s:5 + 2 * n_s], per_peer, refs[6 + 2 * n_s:], nc)

        @pl.when(pl.program_id(0) == 0)
        def _():
            dst[...] = jnp.zeros_like(dst)

        _, vjp = jax.vjp(_scan_chunk, s_ref[0], *[_pair_stack(z_ref, comp * HW) for comp in range(6)])
        g = vjp((_pair_stack(dy_ref, 0), dst[...]))
        dst[...] = g[0]
        for comp in range(6):
            _pair_store(dz_ref, comp * HW, g[1 + comp], ex_ref)

    back = lambda i: (0, nc - 1 - i, 0)
    wide = pl.BlockSpec((batch, c, 6 * HW), back)
    res = pl.pallas_call(
        body, name="rwkv_scan_bwd", grid=(nc,),
        in_specs=[wide, pl.BlockSpec((1, hp, PAIR_W, PAIR_W), lambda i: (nc - 1 - i, 0, 0, 0)),
                  pl.BlockSpec((batch, c, HW), back), wide] + [_HBM_SPEC] * n_s,
        out_specs=[wide] + [_HBM_SPEC] * n_s,
        out_shape=[jax.ShapeDtypeStruct((batch, seq, 6 * HW), f32)] + _side_out_shapes(srcs, per_peer),
        scratch_shapes=[pltpu.VMEM((hp, PAIR_W, PAIR_W), f32)] + _side_sems(n_s),
        compiler_params=_cp(("arbitrary",)),
    )(main6.reshape(batch, seq, 6 * HW), states, dy.reshape(batch, seq, HW), extra.reshape(batch, seq, 6 * HW), *srcs)
    return res[0].reshape(batch * seq, 6 * HW), list(res[1:])


def _to_heads(x, batch, seq, k):
    return x.reshape(batch, seq, k, HEADS, HD).transpose(2, 0, 3, 1, 4).reshape(k, batch * HEADS, seq, HD)


def _from_heads(x, batch, seq, k):
    return x.reshape(k, batch, HEADS, seq, HD).transpose(1, 3, 0, 2, 4).reshape(batch * seq, k * HW)


def _pad_cols(x, width):
    return jnp.pad(x, ((0, 0), (0, width - x.shape[1])))


def _split_w_in(w):
    z64 = jnp.zeros((w.shape[0], 64), w.dtype)
    w_r = jnp.concatenate([w[:, 1544:3080], w[:, 3080:3144], z64, w[:, 3144:3208], z64, w[:, 3208:3336]], axis=1)
    return w[:, :1536], _pad_cols(w[:, 1536:1544], 128), w_r, w[:, 3336:3848], w[:, 3848:]


def _merge_w_in(g_qkv, g_f, g_r, g_mq, g_g):
    return jnp.concatenate([g_qkv, g_f[:, :8], g_r[:, :1536], g_r[:, 1536:1600], g_r[:, 1664:1728], g_r[:, 1792:],
                            g_mq, g_g], axis=1)


def _pad_lora(v):
    z64 = jnp.zeros((1, 64), v.dtype)
    return jnp.concatenate([v[:, :1536], v[:, 1536:1600], z64, v[:, 1600:1664], z64, v[:, 1664:]], axis=1)


def _unpad_lora(v):
    return jnp.concatenate([v[:, :1536], v[:, 1536:1600], v[:, 1664:1728], v[:, 1792:]], axis=1)


def _local_step(x, mem, target, w, p, late=None, early=None):
    batch, seq, _ = x.shape
    t = batch * seq
    x2, tg2, mem2 = x.reshape(t, D), target.reshape(t, D), mem.reshape(batch * MEM_LEN, D)
    w_qkv, w_f, w_r, w_mq, w_g3 = _split_w_in(w["w_in"])
    mu = _pad_lora(p["rwkv_mu"])
    bias = _pad_cols(p["fox_f_bias"], 128)
    r_k = p["rwkv_r_k"].reshape(1, HW)
    post_params = [p["rwkv_gn_g"], p["rwkv_gn_b"], r_k]
    rw_widths = [HW, HW, HW, LORA_PAD, LORA_PAD, LORA_PAD]
    six = [HW] * 6

    (u,) = _rows_fwd("rms_pre1", _fn_rms, [], [(x2, [D])], [p["pre1_g"]], [[D]], dtypes=[bf16])
    p_qkv = _matmul("proj_qkv", u, w_qkv, "nn", out_dtype=bf16)
    p_f = _matmul("proj_f", u, w_f, "nn")
    p_r = _matmul("proj_rwkv", u, w_r, "nn")
    p_mq = _matmul("proj_memq", u, w_mq, "nn", out_dtype=bf16)
    p_g = _matmul("proj_gate", u, w_g3, "nn", out_dtype=bf16)

    c = _fox_gate_fwd(p_f, bias, batch, seq)
    c_rows = c[:, :HEADS].reshape(batch, seq, HEADS).transpose(0, 2, 1)
    fox_o, lse, gathered = _fox_fwd(p_qkv, c, c_rows, batch, seq, side=(late[0], False) if late else None)
    if late:
        w = {**w, **late[2](gathered, 0)}
    fox_out = fox_o.astype(bf16)

    w_up = jnp.pad(w["rwkv_w_up"].astype(f32), ((0, LORA_PAD - 64), (0, 0)))
    a_up = jnp.pad(w["rwkv_a_up"].astype(f32), ((0, LORA_PAD - 64), (0, 0)))
    pre_params = [p["rwkv_w0"], w_up, p["rwkv_a0"], a_up, w["rwkv_g_up"].astype(f32), p["rwkv_k_k"], p["rwkv_k_a"]]
    ps = _tokshift_fwd(p_r, mu, batch, seq)
    main6, g_rw = _rows_fwd("rwkv_pre", _fn_rwkv_pre, [], [(ps, rw_widths)], pre_params, [six, [HW]])
    y_rw, states, gathered = _scan_fwd(main6, batch, seq, side=(late[1], False) if late else None)
    if late:
        w = {**w, **late[2](gathered, 1)}
    post_consts = []
    post_rows = [(y_rw, [HW]), (main6, six), (g_rw, [HW])]

    def fn_post(y, r, _wl, k2, v, _a, _b, g, gn_g, gn_b, rk):
        return _fn_rwkv_post(y, r, k2, v, g, gn_g, gn_b, rk)

    (rwkv_out,) = _rows_fwd("rwkv_post", fn_post, post_consts, post_rows, post_params, [[HW]], dtypes=[bf16])

    (memn,) = _rows_fwd("rms_mem", _fn_rms, [], [(mem2, [D])], [p["mem_norm_g"]], [[D]], dtypes=[bf16])
    mem_kv = _matmul("proj_memkv", memn, w["w_mem_kv"], "nn")
    mem_out = _mem_fwd(p_mq, mem_kv, batch, seq)

    a_fox = _matmul("out_fox", fox_out, w["w_fox_out"], "nn")
    a_rwkv = _matmul("out_rwkv", rwkv_out, w["w_rwkv_out"], "nn")
    a_mem = _matmul("out_mem", mem_out, w["w_mem_out"], "nn")
    merge_rows = [(a_fox, [D]), (a_rwkv, [D]), (a_mem, [D]), (p_g, [D, D, D])]
    (merged,) = _rows_fwd("merge", _fn_merge, [], merge_rows, [], [[D]], dtypes=[bf16])
    yy = _matmul("out_o", merged, w["w_o"], "nn")
    post1_rows = [(yy, [D]), (x2, [D])]
    post1_params = [p["post1_g"], p["pre2_g"]]
    h1, u2 = _rows_fwd("post1", _fn_post1, [], post1_rows, post1_params, [[D], [D]], dtypes=[f32, bf16])
    gp = _matmul("ffn_gate", u2, w["w_ffn_gate"], "nn", out_dtype=bf16)
    up = _matmul("ffn_up", u2, w["w_ffn_up"], "nn", out_dtype=bf16)
    (hmid,) = _rows_fwd("swiglu", _fn_swiglu, [], [(gp, [D_FF]), (up, [D_FF])], [], [[D_FF]], dtypes=[bf16])
    ffn = _matmul("ffn_down", hmid, w["w_ffn_down"], "nn")
    final_rows = [(ffn, [D]), (h1, [D])]
    (loss,) = _rows_fwd("final", _fn_final, [(tg2, [D])], final_rows, [p["post2_g"]], [], n_sums=1)

    gw, gp_ = {}, {}
    (d_ffn, d_h1), (gp_["post2_g"],) = _rows_bwd("final_bwd", _fn_final, [(tg2, [D])], final_rows, [p["post2_g"]], [], [],
                                                  n_sums=1, dtypes=[bf16, f32])
    d_hmid = _matmul("ffn_down_dx", d_ffn, w["w_ffn_down"], "nt", out_dtype=bf16)
    gw["w_ffn_down"] = _matmul("ffn_down_dw", hmid, d_ffn, "tn")
    (d_gp, d_up), _ = _rows_bwd("swiglu_bwd", _fn_swiglu, [], [(gp, [D_FF]), (up, [D_FF])], [], [[D_FF]], [d_hmid],
                                dtypes=[bf16, bf16])
    d_u2 = _matmul("ffn_gate_dx", d_gp, w["w_ffn_gate"], "nt")
    d_u2 = _matmul("ffn_up_dx", d_up, w["w_ffn_up"], "nt", add=d_u2)
    gw["w_ffn_gate"] = _matmul("ffn_gate_dw", u2, d_gp, "tn")
    gw["w_ffn_up"] = _matmul("ffn_up_dw", u2, d_up, "tn")
    (d_yy, d_x_res), (gp_["post1_g"], gp_["pre2_g"]) = _rows_bwd(
        "post1_bwd", _fn_post1, [], post1_rows, post1_params, [[D], [D]], [d_h1, d_u2], dtypes=[bf16, f32])
    d_merged = _matmul("out_o_dx", d_yy, w["w_o"], "nt", out_dtype=bf16)
    gw["w_o"] = _matmul("out_o_dw", merged, d_yy, "tn")
    (d_a_fox, d_a_rwkv, d_a_mem, d_p_g), _ = _rows_bwd("merge_bwd", _fn_merge, [], merge_rows, [], [[D]], [d_merged],
                                                       dtypes=[bf16] * 4)
    d_fox_out = _matmul("out_fox_dx", d_a_fox, w["w_fox_out"], "nt")
    gw["w_fox_out"] = _matmul("out_fox_dw", fox_out, d_a_fox, "tn")
    d_rwkv_out = _matmul("out_rwkv_dx", d_a_rwkv, w["w_rwkv_out"], "nt")
    gw["w_rwkv_out"] = _matmul("out_rwkv_dw", rwkv_out, d_a_rwkv, "tn")
    d_mem_out = _matmul("out_mem_dx", d_a_mem, w["w_mem_out"], "nt")
    gw["w_mem_out"] = _matmul("out_mem_dw", mem_out, d_a_mem, "tn")

    d_p_mq, d_km, d_vm = _mem_bwd(p_mq, mem_kv, d_mem_out, batch, seq)
    d_mem_kv = jnp.concatenate([d_km, d_vm], axis=1).astype(bf16)
    gw["w_mem_kv"] = _matmul("proj_memkv_dw", memn, d_mem_kv, "tn")
    d_memn = _matmul("proj_memkv_dx", d_mem_kv, w["w_mem_kv"], "nt")
    _, (gp_["mem_norm_g"],) = _rows_bwd("rms_mem_bwd", _fn_rms, [], [(mem2, [D])], [p["mem_norm_g"]], [[D]], [d_memn])

    d_q, d_k, d_v, d_cq, d_ck = _fox_bwd(p_qkv, c, c_rows, fox_o, lse, d_fox_out, batch, seq)
    d_p_qkv = jnp.concatenate([d_q, d_k, d_v], axis=1).astype(bf16)
    d_p_f, d_bias = _fox_gate_bwd(p_f, bias, d_cq, d_ck, batch, seq)
    gp_["fox_f_bias"] = d_bias[:, :HEADS]

    (d_y_rw, d_main6_post, d_g_rw), (gp_["rwkv_gn_g"], gp_["rwkv_gn_b"], d_rk) = _rows_bwd(
        "rwkv_post_bwd", fn_post, post_consts, post_rows, post_params, [[HW]], [d_rwkv_out])
    gp_["rwkv_r_k"] = d_rk.reshape(1, HEADS, HD)
    d_main6, early_got = _scan_bwd(main6, states, d_y_rw, d_main6_post, batch, seq,
                                   side=(early(gw), True) if early else None)

    def fn_pre_sum(*args):
        return _fn_rwkv_pre(*args)

    (d_ps,), d_pre = _rows_bwd("rwkv_pre_bwd", fn_pre_sum, [], [(ps, rw_widths)], pre_params, [six, [HW]],
                               [d_main6, d_g_rw])
    gp_["rwkv_w0"], d_w_up, gp_["rwkv_a0"], d_a_up, gw["rwkv_g_up"], gp_["rwkv_k_k"], gp_["rwkv_k_a"] = d_pre
    gw["rwkv_w_up"], gw["rwkv_a_up"] = d_w_up[:64], d_a_up[:64]
    d_p_r, d_mu = _tokshift_bwd(p_r, mu, d_ps, batch, seq)
    gp_["rwkv_mu"] = _unpad_lora(d_mu)

    d_u = _matmul("proj_qkv_dx", d_p_qkv, w_qkv, "nt")
    d_u = _matmul("proj_f_dx", d_p_f, w_f, "nt", add=d_u)
    d_u = _matmul("proj_rwkv_dx", d_p_r, w_r, "nt", add=d_u)
    d_u = _matmul("proj_memq_dx", d_p_mq, w_mq, "nt", add=d_u)
    d_u = _matmul("proj_gate_dx", d_p_g, w_g3, "nt", add=d_u)
    gw["w_in"] = _merge_w_in(_matmul("proj_qkv_dw", u, d_p_qkv, "tn"), _matmul("proj_f_dw", u, d_p_f, "tn"),
                             _matmul("proj_rwkv_dw", u, d_p_r, "tn"), _matmul("proj_memq_dw", u, d_p_mq, "tn"),
                             _matmul("proj_gate_dw", u, d_p_g, "tn"))
    (d_x,), (gp_["pre1_g"],) = _rows_bwd("rms_pre1_bwd", _fn_rms, [], [(x2, [D])], [p["pre1_g"]], [[D]], [d_u], add=d_x_res)
    return loss, d_x.reshape(x.shape), gw, gp_, early_got


def _rows_add(name, a, b):
    (s,) = _rows_fwd(name, lambda u, v: (u + v,), [], [(a, [a.shape[1]]), (b, [b.shape[1]])], [], [[a.shape[1]]])
    return s


def _adamw(name, recv, w, m, v):
    rows, cols = w.shape
    tr = max(t for t in range(16, min(rows, 128) + 1, 16) if rows % t == 0)

    def body(g_ref, w_ref, m_ref, v_ref, go_ref, d_ref, mo_ref, vo_ref):
        g = g_ref[0].astype(f32)
        for s in range(1, N_DEV):
            g = g + g_ref[s].astype(f32)
        m_new = ADAM_B1 * m_ref[...] + (1.0 - ADAM_B1) * g
        v_new = ADAM_B2 * v_ref[...] + (1.0 - ADAM_B2) * (g * g)
        m_hat = m_new / (1.0 - ADAM_B1 ** ADAM_STEP)
        v_hat = v_new / (1.0 - ADAM_B2 ** ADAM_STEP)
        go_ref[...] = g
        d_ref[...] = -ADAM_LR * (m_hat / (jnp.sqrt(v_hat) + ADAM_EPS) + ADAM_WD * w_ref[...])
        mo_ref[...] = m_new
        vo_ref[...] = v_new

    spec = pl.BlockSpec((tr, cols), lambda i: (i, 0))
    return pl.pallas_call(
        body, name=name, grid=(rows // tr,),
        in_specs=[pl.BlockSpec((N_DEV, tr, cols), lambda i: (0, i, 0)), spec, spec, spec],
        out_specs=[spec] * 4, out_shape=[jax.ShapeDtypeStruct(w.shape, f32)] * 4,
        compiler_params=_cp(("parallel",)),
    )(recv, w, m, v)


GROUPS = (
    ("in", ("w_in",), 1),
    ("memkv", ("w_mem_kv",), 0),
    ("ffn_gu", ("w_ffn_gate", "w_ffn_up"), 1),
    ("down_o", ("w_ffn_down", "w_o"), 0),
    ("outs", ("w_fox_out", "w_rwkv_out", "w_mem_out"), 1),
    ("lora", ("rwkv_w_up", "rwkv_a_up", "rwkv_g_up"), 0),
)
FIRST_GROUPS = ("in", "memkv")
LATE_GROUPS = (("down_o", "outs", "lora"), ("ffn_gu",))
EARLY_GRAD_GROUPS = ("memkv", "ffn_gu", "down_o", "outs")
LAST_GRAD_GROUPS = ("in", "lora")
SHARD_AXIS = {n: a for n, _, a in SHARDED}
SMALL_ROWS = 16


def _group_local(shards, members, join):
    parts = [shards[n].reshape(shards[n].shape[-2:]) for n in members]
    return parts[0] if len(parts) == 1 else jnp.concatenate(parts, axis=join)


def _group_split(arr, members, join, lead=False):
    out, off = {}, 0
    for n in members:
        shape = dict((k, s) for k, s, _ in SHARDED)[n]
        size = _block_shape(shape, SHARD_AXIS[n])[join]
        idx = [slice(None)] * arr.ndim
        idx[arr.ndim - 2 + join] = slice(off, off + size)
        out[n] = arr[tuple(idx)]
        off += size
    return out


def _full_from_blocks(blocks, axis):
    if axis == 0:
        return blocks.reshape(-1, blocks.shape[2])
    return blocks.transpose(1, 0, 2).reshape(blocks.shape[1], -1)


def _blocks_from_full(full, axis):
    if axis == 0:
        return full.reshape(N_DEV, -1, full.shape[1])
    return full.reshape(full.shape[0], N_DEV, -1).transpose(1, 0, 2)


def _assemble(gathered, names):
    out = {}
    for arr, g in zip(gathered, names):
        _, members, join = [grp for grp in GROUPS if grp[0] == g][0]
        for n, blk in _group_split(arr, members, join, lead=True).items():
            out[n] = _full_from_blocks(blk, SHARD_AXIS[n])
    return out


def _grad_blocks(gw, names):
    out = []
    for g in names:
        _, members, join = [grp for grp in GROUPS if grp[0] == g][0]
        parts = [_blocks_from_full(gw[n].astype(bf16), SHARD_AXIS[n]) for n in members]
        out.append(parts[0] if len(parts) == 1 else jnp.concatenate(parts, axis=1 + join))
    return out


def _small_pack(d):
    flat = jnp.concatenate([d[n].reshape(-1) for n, _ in REPLICATED])
    return jnp.pad(flat, (0, SMALL_ROWS * LANES - REPL_ELEMS)).reshape(SMALL_ROWS, LANES)


def _small_unpack(packed):
    out, flat, off = {}, packed.reshape(-1), 0
    for n, shape in REPLICATED:
        k = _rows_of((LANES,) + shape)
        out[n] = flat[off:off + k].reshape(shape)
        off += k
    return out


def kernel(x, mem, pre1_g, post1_g, pre2_g, post2_g, mem_norm_g, w_in, fox_f_bias, rwkv_mu, rwkv_w0, rwkv_w_up, rwkv_a0, rwkv_a_up, rwkv_g_up, rwkv_k_k, rwkv_k_a, rwkv_r_k, rwkv_gn_g, rwkv_gn_b, w_mem_kv, w_fox_out, w_rwkv_out, w_mem_out, w_o, w_ffn_gate, w_ffn_up, w_ffn_down, loss_target, m_pre1_g, m_post1_g, m_pre2_g, m_post2_g, m_mem_norm_g, m_w_in, m_fox_f_bias, m_rwkv_mu, m_rwkv_w0, m_rwkv_w_up, m_rwkv_a0, m_rwkv_a_up, m_rwkv_g_up, m_rwkv_k_k, m_rwkv_k_a, m_rwkv_r_k, m_rwkv_gn_g, m_rwkv_gn_b, m_w_mem_kv, m_w_fox_out, m_w_rwkv_out, m_w_mem_out, m_w_o, m_w_ffn_gate, m_w_ffn_up, m_w_ffn_down, v_pre1_g, v_post1_g, v_pre2_g, v_post2_g, v_mem_norm_g, v_w_in, v_fox_f_bias, v_rwkv_mu, v_rwkv_w0, v_rwkv_w_up, v_rwkv_a0, v_rwkv_a_up, v_rwkv_g_up, v_rwkv_k_k, v_rwkv_k_a, v_rwkv_r_k, v_rwkv_gn_g, v_rwkv_gn_b, v_w_mem_kv, v_w_fox_out, v_w_rwkv_out, v_w_mem_out, v_w_o, v_w_ffn_gate, v_w_ffn_up, v_w_ffn_down):
    args = dict(locals())
    wts = {n: args[n] for n in WEIGHT_ORDER}
    ms = {n: args["m_" + n] for n in WEIGHT_ORDER}
    vs = {n: args["v_" + n] for n in WEIGHT_ORDER}

    groups = {g: (members, join) for g, members, join in GROUPS}
    w_bf16 = {n: wts[n].astype(bf16) for n, _, _ in SHARDED}

    def send(g):
        return _group_local(w_bf16, *groups[g])

    first = _exchange("gather_first", [send(g) for g in FIRST_GROUPS], per_peer=False)
    full = _assemble(first, FIRST_GROUPS)
    small_in = {n: (wts[n] if n == "rwkv_r_k" else wts[n].reshape(wts[n].shape[-2:])) for n, _ in REPLICATED}
    late = ([send(g) for g in LATE_GROUPS[0]], [send(g) for g in LATE_GROUPS[1]],
            lambda got, which: _assemble(got, LATE_GROUPS[which]))
    loss_part, grad_x, gw, gp, early_got = _local_step(
        x, mem, loss_target, full, small_in, late=late, early=lambda g: _grad_blocks(g, EARLY_GRAD_GROUPS))

    small_send = jnp.broadcast_to(_small_pack(gp).astype(bf16)[None], (N_DEV, SMALL_ROWS, LANES))
    *last_got, small_got = _exchange("exchange_last", _grad_blocks(gw, LAST_GRAD_GROUPS) + [small_send], per_peer=True)
    received = dict(zip(EARLY_GRAD_GROUPS + LAST_GRAD_GROUPS, list(early_got) + list(last_got)))

    outs = [{}, {}, {}, {}]
    for g, members, join in GROUPS:
        res = _adamw("adamw_" + g, received[g], *[_group_local(d, members, join) for d in (wts, ms, vs)])
        for o, arr in zip(outs, res):
            o.update(_group_split(arr, members, join))
    res = _adamw("adamw_small", small_got, *[_small_pack(d) for d in (wts, ms, vs)])
    for o, arr in zip(outs, res):
        o.update(_small_unpack(arr))
    loss = lax.psum(loss_part[0, 0], ("x", "y", "c"))
    return (loss, grad_x, *[o[n].reshape(wts[n].shape) for o in outs for n in WEIGHT_ORDER])
```

```python
import functools

import jax
import jax.numpy as jnp
from jax import lax
from jax.experimental import pallas as pl
from jax.experimental.pallas import tpu as pltpu

f32 = jnp.float32
bf16 = jnp.bfloat16
_HI = lax.Precision.HIGHEST

D = 1024
HEADS = 8
HD = 64
HW = HEADS * HD
MEM_HEADS = 4
MEM_HD = 128
MEM_W = 512
MEM_LEN = 256
D_FF = 2816
LORA_PAD = 128
RW_COLS = 3 * HW + 3 * LORA_PAD
NORM_EPS = 1e-6
GN_EPS = 64e-5
Q_BLOCK = 128
SCAN_CHUNK = 64
N_DEV = 8
LANES = 1024
VMEM_LIMIT = 56 * 1024 * 1024

ADAM_LR = 0.001
ADAM_B1 = 0.9
ADAM_B2 = 0.999
ADAM_EPS = 1e-08
ADAM_WD = 0.01
ADAM_STEP = 10

SHARDED = (
    ("w_in", (1024, 6920), 1),
    ("w_ffn_gate", (1024, 2816), 1),
    ("w_ffn_up", (1024, 2816), 1),
    ("w_ffn_down", (2816, 1024), 0),
    ("w_mem_kv", (1024, 1024), 0),
    ("w_o", (1024, 1024), 0),
    ("w_fox_out", (512, 1024), 1),
    ("w_rwkv_out", (512, 1024), 1),
    ("w_mem_out", (512, 1024), 1),
    ("rwkv_w_up", (64, 512), 1),
    ("rwkv_a_up", (64, 512), 1),
    ("rwkv_g_up", (128, 512), 1),
)
REPLICATED = (
    ("pre1_g", (1, 1024)), ("post1_g", (1, 1024)), ("pre2_g", (1, 1024)), ("post2_g", (1, 1024)),
    ("mem_norm_g", (1, 1024)), ("fox_f_bias", (1, 8)), ("rwkv_mu", (1, 1792)), ("rwkv_w0", (1, 512)),
    ("rwkv_a0", (1, 512)), ("rwkv_k_k", (1, 512)), ("rwkv_k_a", (1, 512)), ("rwkv_r_k", (1, 8, 64)),
    ("rwkv_gn_g", (1, 512)), ("rwkv_gn_b", (1, 512)),
)
WEIGHT_ORDER = ('pre1_g', 'post1_g', 'pre2_g', 'post2_g', 'mem_norm_g', 'w_in', 'fox_f_bias', 'rwkv_mu',
                'rwkv_w0', 'rwkv_w_up', 'rwkv_a0', 'rwkv_a_up', 'rwkv_g_up', 'rwkv_k_k', 'rwkv_k_a',
                'rwkv_r_k', 'rwkv_gn_g', 'rwkv_gn_b', 'w_mem_kv', 'w_fox_out', 'w_rwkv_out', 'w_mem_out',
                'w_o', 'w_ffn_gate', 'w_ffn_up', 'w_ffn_down')


def _block_shape(shape, axis):
    return tuple(s // N_DEV if i == axis else s for i, s in enumerate(shape))


def _rows_of(shape):
    n = 1
    for s in shape:
        n *= s
    return n // LANES


SHARD_ROWS = sum(_rows_of(_block_shape(s, a)) for _, s, a in SHARDED)
REPL_ELEMS = sum(_rows_of((LANES,) + s) for _, s in REPLICATED)
REPL_ROWS = -(-REPL_ELEMS // LANES)
PACK_ROWS = -(-(SHARD_ROWS + REPL_ROWS) // 128) * 128
GATHER_ROWS = -(-SHARD_ROWS // 16) * 16


def _cp(sem=None):
    return pltpu.CompilerParams(dimension_semantics=sem, vmem_limit_bytes=VMEM_LIMIT)


def _tile(dim, cap):
    best = None
    for t in range(128, min(dim, cap) + 1, 128):
        if dim % t == 0:
            best = t
    return best if best is not None else dim


def _two_terms(x):
    hi = x.astype(bf16)
    return hi, (x - hi.astype(f32)).astype(bf16)


def _dg(a, b, dims, exact):
    if exact == "split":
        (a_hi, a_lo), (b_hi, b_lo) = _two_terms(a), _two_terms(b)
        dot = functools.partial(lax.dot_general, dimension_numbers=dims, preferred_element_type=f32)
        return dot(a_hi, b_hi) + (dot(a_hi, b_lo) + dot(a_lo, b_hi))
    if exact:
        return lax.dot_general(a, b, dims, precision=_HI, preferred_element_type=f32)
    return lax.dot_general(a.astype(bf16), b.astype(bf16), dims, preferred_element_type=f32)


def _make_mm(batched, exact):
    o = 1 if batched else 0
    bd = ((0,), (0,)) if batched else ((), ())
    d_nn = (((1 + o,), (o,)), bd)
    d_nt = (((1 + o,), (1 + o,)), bd)
    d_tn = (((o,), (o,)), bd)

    @jax.custom_vjp
    def nn(a, b):
        return _dg(a, b, d_nn, exact)

    @jax.custom_vjp
    def nt(a, b):
        return _dg(a, b, d_nt, exact)

    @jax.custom_vjp
    def tn(a, b):
        return _dg(a, b, d_tn, exact)

    nn.defvjp(lambda a, b: (_dg(a, b, d_nn, exact), (a, b)),
              lambda res, g: (_dg(g, res[1], d_nt, exact), _dg(res[0], g, d_tn, exact)))
    nt.defvjp(lambda a, b: (_dg(a, b, d_nt, exact), (a, b)),
              lambda res, g: (_dg(g, res[1], d_nn, exact), _dg(g, res[0], d_tn, exact)))
    tn.defvjp(lambda a, b: (_dg(a, b, d_tn, exact), (a, b)),
              lambda res, g: (_dg(res[1], g, d_nt, exact), _dg(res[0], g, d_nn, exact)))
    return nn, nt, tn


def _sigmoid(x):
    return 1.0 / (1.0 + jnp.exp(-x))


def _head_sum_raw(x):
    width = 2 * HD
    i = lax.broadcasted_iota(jnp.int32, (width, width), 0) // HD
    j = lax.broadcasted_iota(jnp.int32, (width, width), 1) // HD
    m = (i == j).astype(bf16)
    dims = (((1,), (0,)), ((), ()))
    out = []
    for p in range(x.shape[1] // width):
        xp = x[:, p * width:(p + 1) * width]
        hi = xp.astype(bf16)
        lo = (xp - hi.astype(f32)).astype(bf16)
        out.append(lax.dot_general(hi, m, dims, preferred_element_type=f32)
                   + lax.dot_general(lo, m, dims, preferred_element_type=f32))
    return jnp.concatenate(out, axis=1)


@jax.custom_vjp
def _head_sum(x):
    return _head_sum_raw(x)


_head_sum.defvjp(lambda x: (_head_sum_raw(x), None), lambda _, g: (_head_sum_raw(g),))


WEIGHT_TILE_BYTES = 13 * 512 * 1024
ACC_TILE_BYTES = 8 * 1024 * 1024


def _matmul(name, a, b, mode, add=None, out_dtype=f32):
    has_add = add is not None
    if mode == "tn":
        (k, m), (_, n) = a.shape, b.shape
        tn = _tile(n, max(128, ACC_TILE_BYTES // (4 * m)))
        tk = _tile(k, 1024)

        def body(a_ref, b_ref, o_ref):
            @pl.when(pl.program_id(1) == 0)
            def _():
                o_ref[...] = jnp.zeros_like(o_ref)

            o_ref[...] += lax.dot_general(a_ref[...].astype(bf16), b_ref[...].astype(bf16),
                                          (((0,), (0,)), ((), ())), preferred_element_type=f32)

        return pl.pallas_call(
            body, name=name, grid=(n // tn, k // tk),
            in_specs=[pl.BlockSpec((tk, m), lambda j, kk: (kk, 0)), pl.BlockSpec((tk, tn), lambda j, kk: (kk, j))],
            out_specs=pl.BlockSpec((m, tn), lambda j, kk: (0, j)), out_shape=jax.ShapeDtypeStruct((m, n), f32),
            compiler_params=_cp(("parallel", "arbitrary")),
        )(a, b)

    (m, k) = a.shape
    n = b.shape[1] if mode == "nn" else b.shape[0]
    tm = _tile(m, 512)
    tn = _tile(n, max(128, WEIGHT_TILE_BYTES // (2 * k)))
    dims = (((1,), (0,)), ((), ())) if mode == "nn" else (((1,), (1,)), ((), ()))
    b_spec = pl.BlockSpec((k, tn), lambda j, i: (0, j)) if mode == "nn" else pl.BlockSpec((tn, k), lambda j, i: (j, 0))
    o_spec = pl.BlockSpec((tm, tn), lambda j, i: (i, j))

    def body(*refs):
        a_ref, b_ref = refs[0], refs[1]
        o_ref = refs[-1]
        r = lax.dot_general(a_ref[...].astype(bf16), b_ref[...].astype(bf16), dims, preferred_element_type=f32)
        if has_add:
            r = r + refs[2][...]
        o_ref[...] = r.astype(o_ref.dtype)

    return pl.pallas_call(
        body, name=name, grid=(n // tn, m // tm),
        in_specs=[pl.BlockSpec((tm, k), lambda j, i: (i, 0)), b_spec] + ([o_spec] if has_add else []),
        out_specs=o_spec, out_shape=jax.ShapeDtypeStruct((m, n), out_dtype),
        compiler_params=_cp(("parallel", "arbitrary")),
    )(*((a, b, add) if has_add else (a, b)))


def _sum_nt(name, a_list, b_list, side=None):
    m, n = a_list[0].shape[0], b_list[0].shape[0]
    tm = _tile(m, 256)
    n_g = len(a_list)
    srcs, per_peer = side if side is not None else ([], False)
    n_s = len(srcs)

    def body(*refs):
        o_ref = refs[2 * n_g + n_s]
        _side_exchange(refs[2 * n_g:2 * n_g + n_s], refs[2 * n_g + n_s + 1:2 * n_g + 2 * n_s + 1], per_peer,
                       refs[2 * n_g + 2 * n_s + 1:], m // tm)
        acc = None
        for g in range(n_g):
            r = lax.dot_general(refs[g][...].astype(bf16), refs[n_g + g][...].astype(bf16), (((1,), (1,)), ((), ())),
                                preferred_element_type=f32)
            acc = r if acc is None else acc + r
        o_ref[...] = acc

    res = pl.pallas_call(
        body, name=name, grid=(m // tm,),
        in_specs=[pl.BlockSpec((tm, a.shape[1]), lambda i: (i, 0)) for a in a_list]
        + [pl.BlockSpec(b.shape, lambda i: (0, 0)) for b in b_list] + [_HBM_SPEC] * n_s,
        out_specs=[pl.BlockSpec((tm, n), lambda i: (i, 0))] + [_HBM_SPEC] * n_s,
        out_shape=[jax.ShapeDtypeStruct((m, n), f32)] + _side_out_shapes(srcs, per_peer),
        scratch_shapes=_side_sems(n_s),
        compiler_params=_cp(("arbitrary",)),
    )(*a_list, *b_list, *srcs)
    return res[0], list(res[1:])


def _pieces(ref, widths):
    out, off = [], 0
    for w in widths:
        out.append(ref[:, off:off + w].astype(f32))
        off += w
    return out


def _store_pieces(ref, widths, vals, add_ref=None):
    off = 0
    for w, v in zip(widths, vals):
        ref[:, off:off + w] = (v if add_ref is None else v + add_ref[:, off:off + w]).astype(ref.dtype)
        off += w


def _rows_fwd(name, fn, consts, rows, params, outs, n_sums=0, tm=256, dtypes=None):
    t = (consts + rows)[0][0].shape[0]
    tm = min(tm, t)
    ins = consts + rows
    n_in, n_p, n_o = len(ins), len(params), len(outs)
    dtypes = dtypes or [f32] * n_o

    def body(*refs):
        in_refs, p_refs = refs[:n_in], refs[n_in:n_in + n_p]
        o_refs, s_refs = refs[n_in + n_p:n_in + n_p + n_o], refs[n_in + n_p + n_o:]
        vals = []
        for r, (_, widths) in zip(in_refs, ins):
            vals += _pieces(r, widths)
        res = fn(*vals, *[p[...] for p in p_refs])
        pos = 0
        for r, widths in zip(o_refs, outs):
            _store_pieces(r, widths, res[pos:pos + len(widths)])
            pos += len(widths)

        @pl.when(pl.program_id(0) == 0)
        def _():
            for s in s_refs:
                s[...] = jnp.zeros_like(s)

        for s, v in zip(s_refs, res[pos:]):
            s[...] += v

    row_spec = lambda w: pl.BlockSpec((tm, w), lambda i: (i, 0))
    full = lambda p: pl.BlockSpec(p.shape, lambda i: (0,) * p.ndim)
    return pl.pallas_call(
        body, name=name, grid=(t // tm,),
        in_specs=[row_spec(a.shape[1]) for a, _ in ins] + [full(p) for p in params],
        out_specs=[row_spec(sum(w)) for w in outs] + [pl.BlockSpec((1, 1), lambda i: (0, 0))] * n_sums,
        out_shape=[jax.ShapeDtypeStruct((t, sum(w)), dt) for w, dt in zip(outs, dtypes)] + [jax.ShapeDtypeStruct((1, 1), f32)] * n_sums,
        compiler_params=_cp(("arbitrary",)),
    )(*[a for a, _ in ins], *params)


def _rows_bwd(name, fn, consts, rows, params, outs, cts, n_sums=0, add=None, tm=256, dtypes=None):
    t = (consts + rows)[0][0].shape[0]
    tm = min(tm, t)
    n_c, n_r, n_p, n_o = len(consts), len(rows), len(params), len(outs)
    has_add = add is not None
    dtypes = dtypes or [f32] * n_r

    def body(*refs):
        pos = 0
        c_refs = refs[pos:pos + n_c]; pos += n_c
        r_refs = refs[pos:pos + n_r]; pos += n_r
        p_refs = refs[pos:pos + n_p]; pos += n_p
        ct_refs = refs[pos:pos + n_o]; pos += n_o
        add_ref = refs[pos] if has_add else None
        pos += 1 if has_add else 0
        dr_refs = refs[pos:pos + n_r]; pos += n_r
        dp_refs = refs[pos:pos + n_p]
        cvals, rvals = [], []
        for r, (_, widths) in zip(c_refs, consts):
            cvals += _pieces(r, widths)
        for r, (_, widths) in zip(r_refs, rows):
            rvals += _pieces(r, widths)
        pvals = [p[...] for p in p_refs]
        ctv = []
        for r, widths in zip(ct_refs, outs):
            ctv += _pieces(r, widths)
        ctv += [jnp.ones((1, 1), f32)] * n_sums
        _, vjp = jax.vjp(lambda *rp: tuple(fn(*cvals, *rp)), *rvals, *pvals)
        g = vjp(tuple(ctv))
        pos = 0
        for idx, (r, (_, widths)) in enumerate(zip(dr_refs, rows)):
            _store_pieces(r, widths, g[pos:pos + len(widths)], add_ref if idx == 0 else None)
            pos += len(widths)

        @pl.when(pl.program_id(0) == 0)
        def _():
            for dp in dp_refs:
                dp[...] = jnp.zeros_like(dp)

        for dp, v in zip(dp_refs, g[pos:]):
            dp[...] += v

    row_spec = lambda w: pl.BlockSpec((tm, w), lambda i: (i, 0))
    full = lambda p: pl.BlockSpec(p.shape, lambda i: (0,) * p.ndim)
    args = [a for a, _ in consts + rows] + list(params) + list(cts) + ([add] if has_add else [])
    res = pl.pallas_call(
        body, name=name, grid=(t // tm,),
        in_specs=[row_spec(a.shape[1]) for a, _ in consts + rows] + [full(p) for p in params]
        + [row_spec(sum(w)) for w in outs] + ([row_spec(add.shape[1])] if has_add else []),
        out_specs=[row_spec(a.shape[1]) for a, _ in rows] + [full(p) for p in params],
        out_shape=[jax.ShapeDtypeStruct(a.shape, dt) for (a, _), dt in zip(rows, dtypes)]
        + [jax.ShapeDtypeStruct(p.shape, f32) for p in params],
        compiler_params=_cp(("arbitrary",)),
    )(*args)
    return res[:n_r], res[n_r:]


def _rms(x, g):
    return x * lax.rsqrt(jnp.mean(x * x, axis=-1, keepdims=True) + NORM_EPS) * g


def _fn_rms(x, g):
    return (_rms(x, g),)


def _fn_rwkv_pre(r, k, v, wd, ad, gd, w0, w_up, a0, a_up, g_up, k_k, k_a):
    nn, _, _ = _make_mm(False, False)
    w_log = -_sigmoid(w0 + nn(jnp.tanh(wd), w_up)) * 0.6065306597126334
    a = _sigmoid(a0 + nn(ad, a_up))
    g = nn(_sigmoid(gd), g_up)
    kk = k * k_k
    kk = kk * lax.rsqrt(jnp.maximum(_head_sum(kk * kk), 1e-24))
    k2 = k * (1.0 + (a - 1.0) * k_a)
    return r, w_log, k2, v, -kk, kk * a, g


def _fn_rwkv_post(y, r, k2, v, g, gn_g, gn_b, r_k):
    mean = _head_sum(y) * (1.0 / HD)
    yc = y - mean
    var = _head_sum(yc * yc) * (1.0 / HD)
    yn = yc * lax.rsqrt(var + GN_EPS) * gn_g + gn_b
    bonus = _head_sum(r * k2 * r_k) * v
    return ((yn + bonus) * g,)


def _fn_merge(a_fox, a_rwkv, a_mem, g_fox, g_rwkv, g_mem):
    return (_sigmoid(g_fox) * a_fox + _sigmoid(g_rwkv) * a_rwkv + _sigmoid(g_mem) * a_mem,)


def _fn_post1(y, x, post1_g, pre2_g):
    h1 = x + _rms(y, post1_g)
    return h1, _rms(h1, pre2_g)


def _fn_swiglu(gp, up):
    return (gp * _sigmoid(gp) * up,)


def _fn_final(target, ffn, h1, post2_g):
    err = h1 + _rms(ffn, post2_g) - target
    per_row = jnp.mean(err * err, axis=-1, keepdims=True)
    return (0.5 * jnp.sum(per_row, axis=0, keepdims=True),)


def _shift_down(x):
    row = lax.broadcasted_iota(jnp.int32, x.shape, 0)
    return jnp.where(row == 0, 0.0, pltpu.roll(x, 1, 0))


def _shift_up(x):
    s = x.shape[0]
    row = lax.broadcasted_iota(jnp.int32, x.shape, 0)
    return jnp.where(row == s - 1, 0.0, pltpu.roll(x, s - 1, 0))


def _tokshift_fwd(p, mu, batch, seq):
    w = p.shape[1]
    tc = _tile(w, 384)

    def body(p_ref, mu_ref, o_ref):
        x = p_ref[...]
        o_ref[...] = x + (_shift_down(x) - x) * mu_ref[...]

    return pl.pallas_call(
        body, name="tokshift_fwd", grid=(w // tc, batch),
        in_specs=[pl.BlockSpec((seq, tc), lambda j, b: (b, j)), pl.BlockSpec((1, tc), lambda j, b: (0, j))],
        out_specs=pl.BlockSpec((seq, tc), lambda j, b: (b, j)),
        out_shape=jax.ShapeDtypeStruct(p.shape, f32),
        compiler_params=_cp(("parallel", "arbitrary")),
    )(p, mu)


def _tokshift_bwd(p, mu, dps, batch, seq):
    w = p.shape[1]
    tc = _tile(w, 384)

    def body(p_ref, mu_ref, d_ref, dp_ref, dmu_ref):
        x, mu_v, d = p_ref[...], mu_ref[...], d_ref[...]
        dp_ref[...] = (d * (1.0 - mu_v) + _shift_up(d * mu_v)).astype(dp_ref.dtype)

        @pl.when(pl.program_id(1) == 0)
        def _():
            dmu_ref[...] = jnp.zeros_like(dmu_ref)

        dmu_ref[...] += jnp.sum(d * (_shift_down(x) - x), axis=0, keepdims=True)

    return pl.pallas_call(
        body, name="tokshift_bwd", grid=(w // tc, batch),
        in_specs=[pl.BlockSpec((seq, tc), lambda j, b: (b, j)), pl.BlockSpec((1, tc), lambda j, b: (0, j)),
                  pl.BlockSpec((seq, tc), lambda j, b: (b, j))],
        out_specs=[pl.BlockSpec((seq, tc), lambda j, b: (b, j)), pl.BlockSpec((1, tc), lambda j, b: (0, j))],
        out_shape=[jax.ShapeDtypeStruct(p.shape, bf16), jax.ShapeDtypeStruct(mu.shape, f32)],
        compiler_params=_cp(("parallel", "arbitrary")),
    )(p, mu, dps)


def _cum_block(seq):
    return _tile(seq, 256)


def _fox_gate_fwd(f, bias, batch, seq):
    cb = _cum_block(seq)

    def body(f_ref, b_ref, c_ref):
        row = lax.broadcasted_iota(jnp.int32, (cb, cb), 0)
        col = lax.broadcasted_iota(jnp.int32, (cb, cb), 1)
        tri = (col <= row).astype(f32)
        carry = jnp.zeros((1, 128), f32)
        for i in range(seq // cb):
            z = f_ref[i * cb:(i + 1) * cb, :] + b_ref[...]
            ls = jnp.minimum(z, 0.0) - jnp.log(1.0 + jnp.exp(-jnp.abs(z)))
            c = _dg(tri, ls, (((1,), (0,)), ((), ())), True) + carry
            c_ref[i * cb:(i + 1) * cb, :] = c
            carry = c[cb - 1:cb, :]

    return pl.pallas_call(
        body, name="fox_gate_fwd", grid=(batch,),
        in_specs=[pl.BlockSpec((seq, 128), lambda b: (b, 0)), pl.BlockSpec((1, 128), lambda b: (0, 0))],
        out_specs=pl.BlockSpec((seq, 128), lambda b: (b, 0)),
        out_shape=jax.ShapeDtypeStruct(f.shape, f32),
        compiler_params=_cp(("arbitrary",)),
    )(f, bias)


def _fox_gate_bwd(f, bias, dc_a, dc_b, batch, seq):
    cb = _cum_block(seq)

    def body(f_ref, b_ref, da_ref, db_ref, df_ref, dbias_ref):
        row = lax.broadcasted_iota(jnp.int32, (cb, cb), 0)
        col = lax.broadcasted_iota(jnp.int32, (cb, cb), 1)
        triu = (col >= row).astype(f32)

        @pl.when(pl.program_id(0) == 0)
        def _():
            dbias_ref[...] = jnp.zeros_like(dbias_ref)

        lane = lax.broadcasted_iota(jnp.int32, (1, 128), 1)

        def by_head(blk):
            out = jnp.zeros((cb, 128), f32)
            for p in range(HEADS // 2):
                for e in range(2):
                    out = jnp.where(lane == 2 * p + e, _pick_lane(blk[:, p * 128:(p + 1) * 128], e), out)
            return out

        carry = jnp.zeros((1, 128), f32)
        tot = jnp.zeros((1, 128), f32)
        for i in reversed(range(seq // cb)):
            sl = slice(i * cb, (i + 1) * cb)
            dc = by_head(da_ref[sl, :] + db_ref[sl, :])
            dls = _dg(triu, dc, (((1,), (0,)), ((), ())), True) + carry
            carry = dls[0:1, :]
            df = dls * _sigmoid(-(f_ref[sl, :] + b_ref[...]))
            df_ref[sl, :] = df.astype(df_ref.dtype)
            tot = tot + jnp.sum(df, axis=0, keepdims=True)
        dbias_ref[...] += tot

    return pl.pallas_call(
        body, name="fox_gate_bwd", grid=(batch,),
        in_specs=[pl.BlockSpec((seq, 128), lambda b: (b, 0)), pl.BlockSpec((1, 128), lambda b: (0, 0)),
                  pl.BlockSpec((seq, HW), lambda b: (b, 0)), pl.BlockSpec((seq, HW), lambda b: (b, 0))],
        out_specs=[pl.BlockSpec((seq, 128), lambda b: (b, 0)), pl.BlockSpec((1, 128), lambda b: (0, 0))],
        out_shape=[jax.ShapeDtypeStruct(f.shape, bf16), jax.ShapeDtypeStruct((1, 128), f32)],
        compiler_params=_cp(("arbitrary",)),
    )(f, bias, dc_a, dc_b)


_HBM_SPEC = pl.BlockSpec(memory_space=pltpu.HBM)


def _side_out_shapes(srcs, per_peer):
    return [jax.ShapeDtypeStruct(((N_DEV,) + tuple(s.shape[1:] if per_peer else s.shape)), s.dtype) for s in srcs]


def _side_sems(n):
    if n == 0:
        return []
    return [pltpu.SemaphoreType.DMA((n, N_DEV - 1)), pltpu.SemaphoreType.DMA((n, N_DEV - 1)), pltpu.SemaphoreType.DMA((n,))]


def _peer_copies(src_refs, dst_refs, per_peer, sems):
    send_sems, recv_sems, local_sems = sems
    x, y, c = lax.axis_index("x"), lax.axis_index("y"), lax.axis_index("c")
    me = 4 * x + 2 * y + c

    def remote(src, dst, t, k, to):
        return pltpu.make_async_remote_copy(src_ref=src, dst_ref=dst, send_sem=send_sems.at[t, k - 1],
                                            recv_sem=recv_sems.at[t, k - 1], device_id=to,
                                            device_id_type=pl.DeviceIdType.MESH)

    direct, relays = [], []
    for t, (s, d) in enumerate(zip(src_refs, dst_refs)):
        direct.append((t, 0, pltpu.make_async_copy(s.at[me] if per_peer else s, d.at[me], local_sems.at[t])))
        for k in range(1, N_DEV):
            px = 1 - x if k & 4 else x
            py = 1 - y if k & 2 else y
            pc = 1 - c if k & 1 else c
            if per_peer:
                direct.append((t, k, remote(s.at[4 * px + 2 * py + pc], d.at[me], t, k, (px, py, pc))))
            elif k == 1 or not k & 1:
                direct.append((t, k, remote(s, d.at[me], t, k, (px, py, pc))))
            else:
                origin = d.at[4 * px + 2 * py + c]
                relays.append((t, k - 1, remote(origin, origin, t, k, (x, y, 1 - c))))
    return direct, relays


def _exchange_start(direct):
    for _, _, cp in direct:
        cp.start()


def _exchange_finish(direct, relays):
    landed = {(t, k): cp for t, k, cp in direct}
    for t, j, cp in relays:
        landed[(t, j)].wait_recv()
        cp.start()
    relayed = {(t, j) for t, j, _ in relays}
    for t, k, cp in direct:
        if k == 0:
            cp.wait()
        else:
            cp.wait_send()
            if (t, k) not in relayed:
                cp.wait_recv()
    for _, _, cp in relays:
        cp.wait()


def _side_exchange(src_refs, dst_refs, per_peer, sems, *grid):
    if not src_refs:
        return
    first = functools.reduce(jnp.logical_and, [pl.program_id(a) == 0 for a in range(len(grid))])
    last = functools.reduce(jnp.logical_and, [pl.program_id(a) == n - 1 for a, n in enumerate(grid)])

    @pl.when(first)
    def _():
        _exchange_start(_peer_copies(src_refs, dst_refs, per_peer, sems)[0])

    @pl.when(last)
    def _():
        _exchange_finish(*_peer_copies(src_refs, dst_refs, per_peer, sems))


def _exchange(name, srcs, per_peer):
    n = len(srcs)

    def body(*refs):
        direct, relays = _peer_copies(refs[:n], refs[n:2 * n], per_peer, refs[2 * n:])
        _exchange_start(direct)
        _exchange_finish(direct, relays)

    return pl.pallas_call(
        body, name=name, in_specs=[_HBM_SPEC] * n, out_specs=[_HBM_SPEC] * n,
        out_shape=_side_out_shapes(srcs, per_peer), scratch_shapes=_side_sems(n),
    )(*srcs)


FOX_T = 512
_NEG = -1e30
_D2 = (((1,), (1,)), ((), ()))
_D1 = (((1,), (0,)), ((), ()))
_D0 = (((0,), (0,)), ((), ()))


def _bdot(a, b, dims):
    return lax.dot_general(a.astype(bf16), b.astype(bf16), dims, preferred_element_type=f32)


def _pick_lane(x, lane):
    idx = lax.broadcasted_iota(jnp.int32, x.shape, 1)
    return jnp.sum(jnp.where(idx == lane, x, 0.0), axis=1, keepdims=True)


def _pick_row(x, row):
    idx = lax.broadcasted_iota(jnp.int32, x.shape, 0)
    return jnp.sum(jnp.where(idx == row, x, 0.0), axis=0, keepdims=True)


def _fox_fwd(qkv, c, c_rows, batch, seq, side=None):
    t = min(FOX_T, seq)
    nq = seq // t
    scale = HD ** -0.5
    srcs, per_peer = side if side is not None else ([], False)
    n_s = len(srcs)

    def body(*refs):
        q_ref, k_ref, v_ref, cq_ref, ck_ref = refs[:5]
        o_ref, lse_ref = refs[5 + n_s:7 + n_s]
        _side_exchange(refs[5:5 + n_s], refs[7 + n_s:7 + 2 * n_s], per_peer, refs[7 + 2 * n_s:], batch, PAIRS, nq)
        pair, i = pl.program_id(1), pl.program_id(2)
        lane = lax.broadcasted_iota(jnp.int32, (1, PAIR_W), 1)
        first = (lane // HD) == 0
        mine = [first, jnp.logical_not(first)]
        q = q_ref[...] * scale
        qs = [jnp.where(mine[e], q, 0.0) for e in range(2)]
        cqs = [_pick_lane(cq_ref[...], 2 * pair + e) for e in range(2)]
        causal = lax.broadcasted_iota(jnp.int32, (t, t), 1) <= lax.broadcasted_iota(jnp.int32, (t, t), 0)

        def block(j, carry, diagonal):
            rows = pl.ds(pl.multiple_of(j * t, t), t)
            kj, vj = k_ref[rows, :], v_ref[rows, :]
            ck_blk = ck_ref[0, :, rows]
            out = []
            for e in range(2):
                m, acc = carry[2 * e:2 * e + 2]
                s = _bdot(qs[e], kj, _D2) + cqs[e] - _pick_row(ck_blk, 2 * pair + e)
                if diagonal:
                    s = jnp.where(causal, s, _NEG)
                m_new = jnp.maximum(m, jnp.max(s, axis=1, keepdims=True))
                p = jnp.exp(s - m_new)
                out += [m_new, jnp.exp(m - m_new) * acc + _bdot(p, jnp.where(mine[e], vj, 1.0), _D1)]
            return tuple(out)

        init = (jnp.full((t, 1), _NEG, f32), jnp.zeros((t, PAIR_W), f32)) * 2
        carry = lax.fori_loop(0, i, lambda j, cr: block(j, cr, False), init)
        m0, a0, m1, a1 = block(i, carry, True)
        l0, l1 = _pick_lane(a0, HD), _pick_lane(a1, 0)
        o_ref[...] = jnp.where(first, a0 / l0, a1 / l1)
        lse_ref[...] = jnp.where(lane == 0, m0 + jnp.log(l0), jnp.where(lane == 1, m1 + jnp.log(l1), 0.0))

    q_spec = pl.BlockSpec((t, PAIR_W), lambda b, p, i: (b * nq + i, p))
    res = pl.pallas_call(
        body, name="fox_attn_fwd", grid=(batch, PAIRS, nq),
        in_specs=[q_spec,
                  pl.BlockSpec((seq, PAIR_W), lambda b, p, i: (b, PAIRS + p)),
                  pl.BlockSpec((seq, PAIR_W), lambda b, p, i: (b, 2 * PAIRS + p)),
                  pl.BlockSpec((t, 128), lambda b, p, i: (b * nq + i, 0)),
                  pl.BlockSpec((1, 8, seq), lambda b, p, i: (b, 0, 0))] + [_HBM_SPEC] * n_s,
        out_specs=[q_spec, q_spec] + [_HBM_SPEC] * n_s,
        out_shape=[jax.ShapeDtypeStruct((batch * seq, HW), f32)] * 2 + _side_out_shapes(srcs, per_peer),
        scratch_shapes=_side_sems(n_s),
        compiler_params=_cp(("arbitrary", "arbitrary", "arbitrary")),
    )(qkv, qkv, qkv, c, c_rows, *srcs)
    return res[0], res[1], list(res[2:])


def _fox_bwd(qkv, c, c_rows, o, lse, do, batch, seq):
    t = min(FOX_T, seq)
    nq = seq // t
    scale = HD ** -0.5

    def body(q_ref, k_ref, v_ref, cq_ref, ck_ref, o_ref, lse_ref, do_ref,
             dq_ref, dk_ref, dv_ref, dcq_ref, dck_ref, acc0, acc1):
        pair, i = pl.program_id(1), pl.program_id(2)
        accs = [acc0, acc1]

        @pl.when(i == 0)
        def _():
            dv_ref[...] = jnp.zeros_like(dv_ref)
            acc0[...] = jnp.zeros_like(acc0)
            acc1[...] = jnp.zeros_like(acc1)

        lane = lax.broadcasted_iota(jnp.int32, (1, PAIR_W), 1)
        first = (lane // HD) == 0
        mine = [first, jnp.logical_not(first)]
        q, d_o, o_i = q_ref[...] * scale, do_ref[...], o_ref[...]
        q0s = [jnp.where(mine[e], q, 0.0) for e in range(2)]
        q1s = [jnp.where(mine[e], q, 1.0) for e in range(2)]
        dos = [jnp.where(mine[e], d_o, 0.0) for e in range(2)]
        deltas = [jnp.sum(dos[e] * o_i, axis=1, keepdims=True) for e in range(2)]
        lses = [_pick_lane(lse_ref[...], e) for e in range(2)]
        cqs = [_pick_lane(cq_ref[...], 2 * pair + e) for e in range(2)]
        causal = lax.broadcasted_iota(jnp.int32, (t, t), 1) <= lax.broadcasted_iota(jnp.int32, (t, t), 0)

        def block(j, dqs, diagonal):
            rows = pl.ds(pl.multiple_of(j * t, t), t)
            kj, vj = k_ref[rows, :], v_ref[rows, :]
            ck_blk = ck_ref[0, :, rows]
            out = []
            for e in range(2):
                s = _bdot(q0s[e], kj, _D2) + cqs[e] - _pick_row(ck_blk, 2 * pair + e)
                if diagonal:
                    s = jnp.where(causal, s, _NEG)
                p = jnp.exp(s - lses[e])
                ds = p * (_bdot(dos[e], vj, _D2) - deltas[e])
                dv_ref[rows, :] += _bdot(p, dos[e], _D0)
                accs[e][rows, :] += _bdot(ds, q1s[e], _D0)
                out.append(dqs[e] + _bdot(ds, jnp.where(mine[e], kj, 1.0), _D1))
            return tuple(out)

        zero = jnp.zeros((t, PAIR_W), f32)
        dqs = lax.fori_loop(0, i, lambda j, cr: block(j, cr, False), (zero, zero))
        dq0, dq1 = block(i, dqs, True)
        dq_ref[...] = jnp.where(first, dq0, dq1) * scale
        dcq_ref[...] = jnp.where(lane == 0, _pick_lane(dq0, HD), jnp.where(lane == 1, _pick_lane(dq1, 0), 0.0))

        @pl.when(i == nq - 1)
        def _():
            a0, a1 = acc0[...], acc1[...]
            dk_ref[...] = jnp.where(first, a0, a1)
            dck_ref[...] = jnp.where(lane == 0, -_pick_lane(a0, HD), jnp.where(lane == 1, -_pick_lane(a1, 0), 0.0))

    blk = lambda col: pl.BlockSpec((t, PAIR_W), lambda b, p, i: (b * nq + i, col * PAIRS + p))
    whole = lambda col: pl.BlockSpec((seq, PAIR_W), lambda b, p, i: (b, col * PAIRS + p))
    t_all = batch * seq
    return pl.pallas_call(
        body, name="fox_attn_bwd", grid=(batch, PAIRS, nq),
        in_specs=[blk(0), whole(1), whole(2),
                  pl.BlockSpec((t, 128), lambda b, p, i: (b * nq + i, 0)),
                  pl.BlockSpec((1, 8, seq), lambda b, p, i: (b, 0, 0)),
                  blk(0), blk(0), blk(0)],
        out_specs=[blk(0), whole(0), whole(0), blk(0), whole(0)],
        out_shape=[jax.ShapeDtypeStruct((t_all, HW), f32)] * 5,
        scratch_shapes=[pltpu.VMEM((seq, PAIR_W), f32), pltpu.VMEM((seq, PAIR_W), f32)],
        compiler_params=_cp(("parallel", "parallel", "arbitrary")),
    )(qkv, qkv, qkv, c, c_rows, o, lse, do)


def _mem_block(q, km, vm):
    nn, nt, _ = _make_mm(False, False)
    logits = nt(q, km) * (MEM_HD ** -0.5)
    m = lax.stop_gradient(jnp.max(logits, axis=-1, keepdims=True))
    e = jnp.exp(logits - m)
    return nn(e / jnp.sum(e, axis=-1, keepdims=True), vm)


def _mem_specs(seq, tq):
    nq = seq // tq
    qs = pl.BlockSpec((tq, MEM_HD), lambda b, h, i: (b * nq + i, h))
    ks = pl.BlockSpec((MEM_LEN, MEM_HD), lambda b, h, i: (b, h))
    vs = pl.BlockSpec((MEM_LEN, MEM_HD), lambda b, h, i: (b, MEM_HEADS + h))
    return nq, qs, ks, vs


def _mem_fwd(q, mem_kv, batch, seq):
    tq = min(512, seq)
    nq, qs, ks, vs = _mem_specs(seq, tq)

    def body(q_ref, k_ref, v_ref, o_ref):
        o_ref[...] = _mem_block(q_ref[...].astype(f32), k_ref[...], v_ref[...]).astype(o_ref.dtype)

    return pl.pallas_call(
        body, name="mem_attn_fwd", grid=(batch, MEM_HEADS, nq),
        in_specs=[qs, ks, vs], out_specs=qs, out_shape=jax.ShapeDtypeStruct(q.shape, bf16),
        compiler_params=_cp(("parallel", "parallel", "arbitrary")),
    )(q, mem_kv, mem_kv)


def _mem_bwd(q, mem_kv, do, batch, seq):
    tq = min(512, seq)
    nq, qs, ks, vs = _mem_specs(seq, tq)

    def body(q_ref, k_ref, v_ref, do_ref, dq_ref, dk_ref, dv_ref):
        _, vjp = jax.vjp(_mem_block, q_ref[...].astype(f32), k_ref[...], v_ref[...])
        dq, dk, dv = vjp(do_ref[...])
        dq_ref[...] = dq.astype(dq_ref.dtype)

        @pl.when(pl.program_id(2) == 0)
        def _():
            dk_ref[...] = jnp.zeros_like(dk_ref)
            dv_ref[...] = jnp.zeros_like(dv_ref)

        dk_ref[...] += dk
        dv_ref[...] += dv

    return pl.pallas_call(
        body, name="mem_attn_bwd", grid=(batch, MEM_HEADS, nq),
        in_specs=[qs, ks, vs, qs], out_specs=[qs, ks, ks],
        out_shape=[jax.ShapeDtypeStruct(q.shape, bf16), jax.ShapeDtypeStruct((batch * MEM_LEN, MEM_W), f32),
                   jax.ShapeDtypeStruct((batch * MEM_LEN, MEM_W), f32)],
        compiler_params=_cp(("parallel", "parallel", "arbitrary")),
    )(q, mem_kv, mem_kv, do)


@jax.custom_vjp
def _halves(x):
    c = x.shape[1] // 2
    return x[:, :c], x[:, c:]


_halves.defvjp(lambda x: ((x[:, :x.shape[1] // 2], x[:, x.shape[1] // 2:]), None),
               lambda _, g: (jnp.concatenate(g, axis=1),))


@jax.custom_vjp
def _lead_halves(x):
    n = x.shape[0] // 2
    return x[:n], x[n:]


_lead_halves.defvjp(lambda x: ((x[:x.shape[0] // 2], x[x.shape[0] // 2:]), None),
                    lambda _, g: (jnp.concatenate(g, axis=0),))


def _scan_chunk(s0, r, wl, k, v, a, b):
    nn, nt, tn = _make_mm(True, False)
    nn_exact, _, _ = _make_mm(True, True)
    _, nt_exact, _ = _make_mm(True, "split")
    hp, c, lanes = r.shape
    row = lax.broadcasted_iota(jnp.int32, (c, c), 0)
    col = lax.broadcasted_iota(jnp.int32, (c, c), 1)
    first = (lax.broadcasted_iota(jnp.int32, (1, 1, lanes), 2) // HD) == 0
    tri = jnp.broadcast_to((col <= row).astype(f32)[None], (hp, c, c))
    lg = nn_exact(tri, wl)
    lg_end = lg[:, c - 1:c, :]
    grow, shrink, to_end = jnp.exp(lg), jnp.exp(-lg), jnp.exp(lg_end - lg)
    rt, kt, bt, at = r * grow, k * shrink, b * shrink, a * jnp.exp(lg - wl)
    strict, incl = (col < row)[None], (col <= row)[None]
    twice = lambda t: jnp.concatenate([t, t], axis=0)
    queries = jnp.concatenate([at, rt], axis=1)
    per_head = jnp.concatenate([jnp.where(first, queries, 0.0), jnp.where(first, 0.0, queries)], axis=0)
    (ab, rb), (ak, rk) = _halves(nt_exact(per_head, twice(bt))), _halves(nt_exact(per_head, twice(kt)))
    l_ab = jnp.where(strict, ab, 0.0)
    a_ak = jnp.where(strict, ak, 0.0)
    a_rb = jnp.where(incl, rb, 0.0)
    a_rk = jnp.where(incl, rk, 0.0)
    inv = (col == row).astype(f32)[None] + l_ab
    power, n = l_ab, 1
    while 2 * n < c:
        power = nn(power, power)
        inv = inv + nn(inv, power)
        n *= 2

    def apply(m, t):
        lo, hi = _lead_halves(nn(m, twice(t)))
        return jnp.where(first, lo, hi)

    sa = apply(inv, nt(at, s0) + apply(a_ak, v))
    y = nt(rt, s0) + apply(a_rk, v) + apply(a_rb, sa)
    same_head = ((lax.broadcasted_iota(jnp.int32, (lanes, lanes), 0) // HD)
                 == (lax.broadcasted_iota(jnp.int32, (lanes, lanes), 1) // HD))[None]
    s1 = s0 * jnp.exp(lg_end) + jnp.where(same_head, tn(v, k * to_end) + tn(sa, b * to_end), 0.0)
    return y, s1


PAIRS = HEADS // 2
PAIR_W = 2 * HD


def _pair_stack(ref, off):
    return jnp.stack([ref[b, :, off + p * PAIR_W:off + (p + 1) * PAIR_W]
                      for b in range(ref.shape[0]) for p in range(PAIRS)])


def _pair_store(ref, off, val, add_ref=None):
    for b in range(ref.shape[0]):
        for p in range(PAIRS):
            sl = slice(off + p * PAIR_W, off + (p + 1) * PAIR_W)
            v = val[b * PAIRS + p]
            ref[b, :, sl] = v if add_ref is None else v + add_ref[b, :, sl]


def _scan_fwd(main6, batch, seq, side=None):
    c = min(SCAN_CHUNK, seq)
    nc = seq // c
    hp = batch * PAIRS
    srcs, per_peer = side if side is not None else ([], False)
    n_s = len(srcs)

    def body(*refs):
        z_ref, y_ref, s_ref, st = refs[0], refs[1 + n_s], refs[2 + n_s], refs[3 + 2 * n_s]
        _side_exchange(refs[1:1 + n_s], refs[3 + n_s:3 + 2 * n_s], per_peer, refs[4 + 2 * n_s:], nc)

        @pl.when(pl.program_id(0) == 0)
        def _():
            st[...] = jnp.zeros_like(st)

        s0 = st[...]
        s_ref[0] = s0
        y, s1 = _scan_chunk(s0, *[_pair_stack(z_ref, comp * HW) for comp in range(6)])
        _pair_store(y_ref, 0, y)
        st[...] = s1

    res = pl.pallas_call(
        body, name="rwkv_scan_fwd", grid=(nc,),
        in_specs=[pl.BlockSpec((batch, c, 6 * HW), lambda i: (0, i, 0))] + [_HBM_SPEC] * n_s,
        out_specs=[pl.BlockSpec((batch, c, HW), lambda i: (0, i, 0)),
                   pl.BlockSpec((1, hp, PAIR_W, PAIR_W), lambda i: (i, 0, 0, 0))] + [_HBM_SPEC] * n_s,
        out_shape=[jax.ShapeDtypeStruct((batch, seq, HW), f32), jax.ShapeDtypeStruct((nc, hp, PAIR_W, PAIR_W), f32)]
        + _side_out_shapes(srcs, per_peer),
        scratch_shapes=[pltpu.VMEM((hp, PAIR_W, PAIR_W), f32)] + _side_sems(n_s),
        compiler_params=_cp(("arbitrary",)),
    )(main6.reshape(batch, seq, 6 * HW), *srcs)
    return res[0].reshape(batch * seq, HW), res[1], list(res[2:])


def _scan_bwd(main6, states, dy, extra, batch, seq, side=None):
    c = min(SCAN_CHUNK, seq)
    nc = seq // c
    hp = batch * PAIRS
    srcs, per_peer = side if side is not None else ([], False)
    n_s = len(srcs)

    def body(*refs):
        z_ref, s_ref, dy_ref, ex_ref = refs[:4]
        dz_ref, dst = refs[4 + n_s], refs[5 + 2 * n_s]
        _side_exchange(refs[4:4 + n_s], refs[5 + n_s:5 + 2 * n_s], per_peer, refs[6 + 2 * n_s:], nc)

        @pl.when(pl.program_id(0) == 0)
        def _():
            dst[...] = jnp.zeros_like(dst)

        _, vjp = jax.vjp(_scan_chunk, s_ref[0], *[_pair_stack(z_ref, comp * HW) for comp in range(6)])
        g = vjp((_pair_stack(dy_ref, 0), dst[...]))
        dst[...] = g[0]
        for comp in range(6):
            _pair_store(dz_ref, comp * HW, g[1 + comp], ex_ref)

    back = lambda i: (0, nc - 1 - i, 0)
    wide = pl.BlockSpec((batch, c, 6 * HW), back)
    res = pl.pallas_call(
        body, name="rwkv_scan_bwd", grid=(nc,),
        in_specs=[wide, pl.BlockSpec((1, hp, PAIR_W, PAIR_W), lambda i: (nc - 1 - i, 0, 0, 0)),
                  pl.BlockSpec((batch, c, HW), back), wide] + [_HBM_SPEC] * n_s,
        out_specs=[wide] + [_HBM_SPEC] * n_s,
        out_shape=[jax.ShapeDtypeStruct((batch, seq, 6 * HW), f32)] + _side_out_shapes(srcs, per_peer),
        scratch_shapes=[pltpu.VMEM((hp, PAIR_W, PAIR_W), f32)] + _side_sems(n_s),
        compiler_params=_cp(("arbitrary",)),
    )(main6.reshape(batch, seq, 6 * HW), states, dy.reshape(batch, seq, HW), extra.reshape(batch, seq, 6 * HW), *srcs)
    return res[0].reshape(batch * seq, 6 * HW), list(res[1:])


def _to_heads(x, batch, seq, k):
    return x.reshape(batch, seq, k, HEADS, HD).transpose(2, 0, 3, 1, 4).reshape(k, batch * HEADS, seq, HD)


def _from_heads(x, batch, seq, k):
    return x.reshape(k, batch, HEADS, seq, HD).transpose(1, 3, 0, 2, 4).reshape(batch * seq, k * HW)


def _pad_cols(x, width):
    return jnp.pad(x, ((0, 0), (0, width - x.shape[1])))


def _split_w_in(w):
    z64 = jnp.zeros((w.shape[0], 64), w.dtype)
    w_r = jnp.concatenate([w[:, 1544:3080], w[:, 3080:3144], z64, w[:, 3144:3208], z64, w[:, 3208:3336]], axis=1)
    return w[:, :1536], _pad_cols(w[:, 1536:1544], 128), w_r, w[:, 3336:3848], w[:, 3848:]


def _merge_w_in(g_qkv, g_f, g_r, g_mq, g_g):
    return jnp.concatenate([g_qkv, g_f[:, :8], g_r[:, :1536], g_r[:, 1536:1600], g_r[:, 1664:1728], g_r[:, 1792:],
                            g_mq, g_g], axis=1)


def _pad_lora(v):
    z64 = jnp.zeros((1, 64), v.dtype)
    return jnp.concatenate([v[:, :1536], v[:, 1536:1600], z64, v[:, 1600:1664], z64, v[:, 1664:]], axis=1)


def _unpad_lora(v):
    return jnp.concatenate([v[:, :1536], v[:, 1536:1600], v[:, 1664:1728], v[:, 1792:]], axis=1)


def _local_step(x, mem, target, w, p, late=None, early=None, last=None):
    batch, seq, _ = x.shape
    t = batch * seq
    x2, tg2, mem2 = x.reshape(t, D), target.reshape(t, D), mem.reshape(batch * MEM_LEN, D)
    w_qkv, w_f, w_r, w_mq, w_g3 = _split_w_in(w["w_in"])
    mu = _pad_lora(p["rwkv_mu"])
    bias = _pad_cols(p["fox_f_bias"], 128)
    r_k = p["rwkv_r_k"].reshape(1, HW)
    post_params = [p["rwkv_gn_g"], p["rwkv_gn_b"], r_k]
    rw_widths = [HW, HW, HW, LORA_PAD, LORA_PAD, LORA_PAD]
    six = [HW] * 6

    (u,) = _rows_fwd("rms_pre1", _fn_rms, [], [(x2, [D])], [p["pre1_g"]], [[D]], dtypes=[bf16])
    p_qkv = _matmul("proj_qkv", u, w_qkv, "nn", out_dtype=bf16)
    p_f = _matmul("proj_f", u, w_f, "nn")
    p_r = _matmul("proj_rwkv", u, w_r, "nn")
    p_mq = _matmul("proj_memq", u, w_mq, "nn", out_dtype=bf16)
    p_g = _matmul("proj_gate", u, w_g3, "nn", out_dtype=bf16)

    c = _fox_gate_fwd(p_f, bias, batch, seq)
    c_rows = c[:, :HEADS].reshape(batch, seq, HEADS).transpose(0, 2, 1)
    fox_o, lse, gathered = _fox_fwd(p_qkv, c, c_rows, batch, seq, side=(late[0], False) if late else None)
    if late:
        w = {**w, **late[2](gathered, 0)}
    fox_out = fox_o.astype(bf16)

    w_up = jnp.pad(w["rwkv_w_up"].astype(f32), ((0, LORA_PAD - 64), (0, 0)))
    a_up = jnp.pad(w["rwkv_a_up"].astype(f32), ((0, LORA_PAD - 64), (0, 0)))
    pre_params = [p["rwkv_w0"], w_up, p["rwkv_a0"], a_up, w["rwkv_g_up"].astype(f32), p["rwkv_k_k"], p["rwkv_k_a"]]
    ps = _tokshift_fwd(p_r, mu, batch, seq)
    main6, g_rw = _rows_fwd("rwkv_pre", _fn_rwkv_pre, [], [(ps, rw_widths)], pre_params, [six, [HW]])
    y_rw, states, gathered = _scan_fwd(main6, batch, seq, side=(late[1], False) if late else None)
    if late:
        w = {**w, **late[2](gathered, 1)}
    post_consts = []
    post_rows = [(y_rw, [HW]), (main6, six), (g_rw, [HW])]

    def fn_post(y, r, _wl, k2, v, _a, _b, g, gn_g, gn_b, rk):
        return _fn_rwkv_post(y, r, k2, v, g, gn_g, gn_b, rk)

    (rwkv_out,) = _rows_fwd("rwkv_post", fn_post, post_consts, post_rows, post_params, [[HW]], dtypes=[bf16])

    (memn,) = _rows_fwd("rms_mem", _fn_rms, [], [(mem2, [D])], [p["mem_norm_g"]], [[D]], dtypes=[bf16])
    mem_kv = _matmul("proj_memkv", memn, w["w_mem_kv"], "nn")
    mem_out = _mem_fwd(p_mq, mem_kv, batch, seq)

    a_fox = _matmul("out_fox", fox_out, w["w_fox_out"], "nn")
    a_rwkv = _matmul("out_rwkv", rwkv_out, w["w_rwkv_out"], "nn")
    a_mem = _matmul("out_mem", mem_out, w["w_mem_out"], "nn")
    merge_rows = [(a_fox, [D]), (a_rwkv, [D]), (a_mem, [D]), (p_g, [D, D, D])]
    (merged,) = _rows_fwd("merge", _fn_merge, [], merge_rows, [], [[D]], dtypes=[bf16])
    yy = _matmul("out_o", merged, w["w_o"], "nn")
    post1_rows = [(yy, [D]), (x2, [D])]
    post1_params = [p["post1_g"], p["pre2_g"]]
    h1, u2 = _rows_fwd("post1", _fn_post1, [], post1_rows, post1_params, [[D], [D]], dtypes=[f32, bf16])
    gp = _matmul("ffn_gate", u2, w["w_ffn_gate"], "nn", out_dtype=bf16)
    up = _matmul("ffn_up", u2, w["w_ffn_up"], "nn", out_dtype=bf16)
    (hmid,) = _rows_fwd("swiglu", _fn_swiglu, [], [(gp, [D_FF]), (up, [D_FF])], [], [[D_FF]], dtypes=[bf16])
    ffn = _matmul("ffn_down", hmid, w["w_ffn_down"], "nn")
    final_rows = [(ffn, [D]), (h1, [D])]
    (loss,) = _rows_fwd("final", _fn_final, [(tg2, [D])], final_rows, [p["post2_g"]], [], n_sums=1)

    gw, gp_ = {}, {}
    (d_ffn, d_h1), (gp_["post2_g"],) = _rows_bwd("final_bwd", _fn_final, [(tg2, [D])], final_rows, [p["post2_g"]], [], [],
                                                  n_sums=1, dtypes=[bf16, f32])
    d_hmid = _matmul("ffn_down_dx", d_ffn, w["w_ffn_down"], "nt", out_dtype=bf16)
    gw["w_ffn_down"] = _matmul("ffn_down_dw", hmid, d_ffn, "tn")
    (d_gp, d_up), _ = _rows_bwd("swiglu_bwd", _fn_swiglu, [], [(gp, [D_FF]), (up, [D_FF])], [], [[D_FF]], [d_hmid],
                                dtypes=[bf16, bf16])
    d_u2 = _matmul("ffn_gate_dx", d_gp, w["w_ffn_gate"], "nt")
    d_u2 = _matmul("ffn_up_dx", d_up, w["w_ffn_up"], "nt", add=d_u2)
    gw["w_ffn_gate"] = _matmul("ffn_gate_dw", u2, d_gp, "tn")
    gw["w_ffn_up"] = _matmul("ffn_up_dw", u2, d_up, "tn")
    (d_yy, d_x_res), (gp_["post1_g"], gp_["pre2_g"]) = _rows_bwd(
        "post1_bwd", _fn_post1, [], post1_rows, post1_params, [[D], [D]], [d_h1, d_u2], dtypes=[bf16, f32])
    d_merged = _matmul("out_o_dx", d_yy, w["w_o"], "nt", out_dtype=bf16)
    gw["w_o"] = _matmul("out_o_dw", merged, d_yy, "tn")
    (d_a_fox, d_a_rwkv, d_a_mem, d_p_g), _ = _rows_bwd("merge_bwd", _fn_merge, [], merge_rows, [], [[D]], [d_merged],
                                                       dtypes=[bf16] * 4)
    d_fox_out = _matmul("out_fox_dx", d_a_fox, w["w_fox_out"], "nt")
    gw["w_fox_out"] = _matmul("out_fox_dw", fox_out, d_a_fox, "tn")
    d_rwkv_out = _matmul("out_rwkv_dx", d_a_rwkv, w["w_rwkv_out"], "nt")
    gw["w_rwkv_out"] = _matmul("out_rwkv_dw", rwkv_out, d_a_rwkv, "tn")
    d_mem_out = _matmul("out_mem_dx", d_a_mem, w["w_mem_out"], "nt")
    gw["w_mem_out"] = _matmul("out_mem_dw", mem_out, d_a_mem, "tn")

    d_p_mq, d_km, d_vm = _mem_bwd(p_mq, mem_kv, d_mem_out, batch, seq)
    d_mem_kv = jnp.concatenate([d_km, d_vm], axis=1).astype(bf16)
    gw["w_mem_kv"] = _matmul("proj_memkv_dw", memn, d_mem_kv, "tn")
    d_memn = _matmul("proj_memkv_dx", d_mem_kv, w["w_mem_kv"], "nt")
    _, (gp_["mem_norm_g"],) = _rows_bwd("rms_mem_bwd", _fn_rms, [], [(mem2, [D])], [p["mem_norm_g"]], [[D]], [d_memn])

    d_q, d_k, d_v, d_cq, d_ck = _fox_bwd(p_qkv, c, c_rows, fox_o, lse, d_fox_out, batch, seq)
    d_p_qkv = jnp.concatenate([d_q, d_k, d_v], axis=1).astype(bf16)
    d_p_f, d_bias = _fox_gate_bwd(p_f, bias, d_cq, d_ck, batch, seq)
    gp_["fox_f_bias"] = d_bias[:, :HEADS]

    (d_y_rw, d_main6_post, d_g_rw), (gp_["rwkv_gn_g"], gp_["rwkv_gn_b"], d_rk) = _rows_bwd(
        "rwkv_post_bwd", fn_post, post_consts, post_rows, post_params, [[HW]], [d_rwkv_out])
    gp_["rwkv_r_k"] = d_rk.reshape(1, HEADS, HD)
    d_main6, early_got = _scan_bwd(main6, states, d_y_rw, d_main6_post, batch, seq,
                                   side=(early(gw), True) if early else None)

    def fn_pre_sum(*args):
        return _fn_rwkv_pre(*args)

    (d_ps,), d_pre = _rows_bwd("rwkv_pre_bwd", fn_pre_sum, [], [(ps, rw_widths)], pre_params, [six, [HW]],
                               [d_main6, d_g_rw])
    gp_["rwkv_w0"], d_w_up, gp_["rwkv_a0"], d_a_up, gw["rwkv_g_up"], gp_["rwkv_k_k"], gp_["rwkv_k_a"] = d_pre
    gw["rwkv_w_up"], gw["rwkv_a_up"] = d_w_up[:64], d_a_up[:64]
    d_p_r, d_mu = _tokshift_bwd(p_r, mu, d_ps, batch, seq)
    gp_["rwkv_mu"] = _unpad_lora(d_mu)

    gw["w_in"] = _merge_w_in(_matmul("proj_qkv_dw", u, d_p_qkv, "tn"), _matmul("proj_f_dw", u, d_p_f, "tn"),
                             _matmul("proj_rwkv_dw", u, d_p_r, "tn"), _matmul("proj_memq_dw", u, d_p_mq, "tn"),
                             _matmul("proj_gate_dw", u, d_p_g, "tn"))
    d_u, last_got = _sum_nt("proj_dx", [d_p_qkv, d_p_f, d_p_r, d_p_mq, d_p_g], [w_qkv, w_f, w_r, w_mq, w_g3],
                            side=(last(gw), True) if last else None)
    (d_x,), (gp_["pre1_g"],) = _rows_bwd("rms_pre1_bwd", _fn_rms, [], [(x2, [D])], [p["pre1_g"]], [[D]], [d_u], add=d_x_res)
    return loss, d_x.reshape(x.shape), gw, gp_, early_got, last_got


def _rows_add(name, a, b):
    (s,) = _rows_fwd(name, lambda u, v: (u + v,), [], [(a, [a.shape[1]]), (b, [b.shape[1]])], [], [[a.shape[1]]])
    return s


def _adamw(name, recv, w, m, v):
    rows, cols = w.shape
    tr = max(t for t in range(16, min(rows, 128) + 1, 16) if rows % t == 0)

    def body(g_ref, w_ref, m_ref, v_ref, go_ref, d_ref, mo_ref, vo_ref):
        g = g_ref[0].astype(f32)
        for s in range(1, N_DEV):
            g = g + g_ref[s].astype(f32)
        m_new = ADAM_B1 * m_ref[...] + (1.0 - ADAM_B1) * g
        v_new = ADAM_B2 * v_ref[...] + (1.0 - ADAM_B2) * (g * g)
        m_hat = m_new / (1.0 - ADAM_B1 ** ADAM_STEP)
        v_hat = v_new / (1.0 - ADAM_B2 ** ADAM_STEP)
        go_ref[...] = g
        d_ref[...] = -ADAM_LR * (m_hat / (jnp.sqrt(v_hat) + ADAM_EPS) + ADAM_WD * w_ref[...])
        mo_ref[...] = m_new
        vo_ref[...] = v_new

    spec = pl.BlockSpec((tr, cols), lambda i: (i, 0))
    return pl.pallas_call(
        body, name=name, grid=(rows // tr,),
        in_specs=[pl.BlockSpec((N_DEV, tr, cols), lambda i: (0, i, 0)), spec, spec, spec],
        out_specs=[spec] * 4, out_shape=[jax.ShapeDtypeStruct(w.shape, f32)] * 4,
        compiler_params=_cp(("parallel",)),
    )(recv, w, m, v)


GROUPS = (
    ("in", ("w_in",), 1),
    ("memkv", ("w_mem_kv",), 0),
    ("ffn_gu", ("w_ffn_gate", "w_ffn_up"), 1),
    ("down_o", ("w_ffn_down", "w_o"), 0),
    ("outs", ("w_fox_out", "w_rwkv_out", "w_mem_out"), 1),
    ("lora", ("rwkv_w_up", "rwkv_a_up", "rwkv_g_up"), 0),
)
FIRST_GROUPS = ("in", "memkv")
LATE_GROUPS = (("down_o", "outs", "lora"), ("ffn_gu",))
EARLY_GRAD_GROUPS = ("memkv", "ffn_gu", "down_o", "outs")
LAST_GRAD_GROUPS = ("in", "lora")
SHARD_AXIS = {n: a for n, _, a in SHARDED}
SMALL_ROWS = 16


def _group_local(shards, members, join):
    parts = [shards[n].reshape(shards[n].shape[-2:]) for n in members]
    return parts[0] if len(parts) == 1 else jnp.concatenate(parts, axis=join)


def _group_split(arr, members, join, lead=False):
    out, off = {}, 0
    for n in members:
        shape = dict((k, s) for k, s, _ in SHARDED)[n]
        size = _block_shape(shape, SHARD_AXIS[n])[join]
        idx = [slice(None)] * arr.ndim
        idx[arr.ndim - 2 + join] = slice(off, off + size)
        out[n] = arr[tuple(idx)]
        off += size
    return out


def _full_from_blocks(blocks, axis):
    if axis == 0:
        return blocks.reshape(-1, blocks.shape[2])
    return blocks.transpose(1, 0, 2).reshape(blocks.shape[1], -1)


def _blocks_from_full(full, axis):
    if axis == 0:
        return full.reshape(N_DEV, -1, full.shape[1])
    return full.reshape(full.shape[0], N_DEV, -1).transpose(1, 0, 2)


def _assemble(gathered, names):
    out = {}
    for arr, g in zip(gathered, names):
        _, members, join = [grp for grp in GROUPS if grp[0] == g][0]
        for n, blk in _group_split(arr, members, join, lead=True).items():
            out[n] = _full_from_blocks(blk, SHARD_AXIS[n])
    return out


def _grad_blocks(gw, names):
    out = []
    for g in names:
        _, members, join = [grp for grp in GROUPS if grp[0] == g][0]
        parts = [_blocks_from_full(gw[n].astype(bf16), SHARD_AXIS[n]) for n in members]
        out.append(parts[0] if len(parts) == 1 else jnp.concatenate(parts, axis=1 + join))
    return out


def _small_pack(d):
    flat = jnp.concatenate([d[n].reshape(-1) for n, _ in REPLICATED])
    return jnp.pad(flat, (0, SMALL_ROWS * LANES - REPL_ELEMS)).reshape(SMALL_ROWS, LANES)


def _small_unpack(packed):
    out, flat, off = {}, packed.reshape(-1), 0
    for n, shape in REPLICATED:
        k = _rows_of((LANES,) + shape)
        out[n] = flat[off:off + k].reshape(shape)
        off += k
    return out


def kernel(x, mem, pre1_g, post1_g, pre2_g, post2_g, mem_norm_g, w_in, fox_f_bias, rwkv_mu, rwkv_w0, rwkv_w_up, rwkv_a0, rwkv_a_up, rwkv_g_up, rwkv_k_k, rwkv_k_a, rwkv_r_k, rwkv_gn_g, rwkv_gn_b, w_mem_kv, w_fox_out, w_rwkv_out, w_mem_out, w_o, w_ffn_gate, w_ffn_up, w_ffn_down, loss_target, m_pre1_g, m_post1_g, m_pre2_g, m_post2_g, m_mem_norm_g, m_w_in, m_fox_f_bias, m_rwkv_mu, m_rwkv_w0, m_rwkv_w_up, m_rwkv_a0, m_rwkv_a_up, m_rwkv_g_up, m_rwkv_k_k, m_rwkv_k_a, m_rwkv_r_k, m_rwkv_gn_g, m_rwkv_gn_b, m_w_mem_kv, m_w_fox_out, m_w_rwkv_out, m_w_mem_out, m_w_o, m_w_ffn_gate, m_w_ffn_up, m_w_ffn_down, v_pre1_g, v_post1_g, v_pre2_g, v_post2_g, v_mem_norm_g, v_w_in, v_fox_f_bias, v_rwkv_mu, v_rwkv_w0, v_rwkv_w_up, v_rwkv_a0, v_rwkv_a_up, v_rwkv_g_up, v_rwkv_k_k, v_rwkv_k_a, v_rwkv_r_k, v_rwkv_gn_g, v_rwkv_gn_b, v_w_mem_kv, v_w_fox_out, v_w_rwkv_out, v_w_mem_out, v_w_o, v_w_ffn_gate, v_w_ffn_up, v_w_ffn_down):
    args = dict(locals())
    wts = {n: args[n] for n in WEIGHT_ORDER}
    ms = {n: args["m_" + n] for n in WEIGHT_ORDER}
    vs = {n: args["v_" + n] for n in WEIGHT_ORDER}

    groups = {g: (members, join) for g, members, join in GROUPS}
    w_bf16 = {n: wts[n].astype(bf16) for n, _, _ in SHARDED}

    def send(g):
        return _group_local(w_bf16, *groups[g])

    first = _exchange("gather_first", [send(g) for g in FIRST_GROUPS], per_peer=False)
    full = _assemble(first, FIRST_GROUPS)
    small_in = {n: (wts[n] if n == "rwkv_r_k" else wts[n].reshape(wts[n].shape[-2:])) for n, _ in REPLICATED}
    late = ([send(g) for g in LATE_GROUPS[0]], [send(g) for g in LATE_GROUPS[1]],
            lambda got, which: _assemble(got, LATE_GROUPS[which]))
    loss_part, grad_x, gw, gp, early_got, last_got = _local_step(
        x, mem, loss_target, full, small_in, late=late, early=lambda g: _grad_blocks(g, EARLY_GRAD_GROUPS),
        last=lambda g: _grad_blocks(g, LAST_GRAD_GROUPS))
    (small_got,) = _exchange("exchange_small", [_small_pack(gp).astype(bf16)], per_peer=False)
    received = dict(zip(EARLY_GRAD_GROUPS + LAST_GRAD_GROUPS, list(early_got) + list(last_got)))

    outs = [{}, {}, {}, {}]
    for g, members, join in GROUPS:
        res = _adamw("adamw_" + g, received[g], *[_group_local(d, members, join) for d in (wts, ms, vs)])
        for o, arr in zip(outs, res):
            o.update(_group_split(arr, members, join))
    res = _adamw("adamw_small", small_got, *[_small_pack(d) for d in (wts, ms, vs)])
    for o, arr in zip(outs, res):
        o.update(_small_unpack(arr))
    loss = lax.psum(loss_part[0, 0], ("x", "y", "c"))
    return (loss, grad_x, *[o[n].reshape(wts[n].shape) for o in outs for n in WEIGHT_ORDER])
```

```python
import functools

import jax
import jax.numpy as jnp
from jax import lax
from jax.experimental import pallas as pl
from jax.experimental.pallas import tpu as pltpu

f32 = jnp.float32
bf16 = jnp.bfloat16
_HI = lax.Precision.HIGHEST

D = 1024
HEADS = 8
HD = 64
HW = HEADS * HD
MEM_HEADS = 4
MEM_HD = 128
MEM_W = 512
MEM_LEN = 256
D_FF = 2816
LORA_PAD = 128
RW_COLS = 3 * HW + 3 * LORA_PAD
NORM_EPS = 1e-6
GN_EPS = 64e-5
Q_BLOCK = 128
SCAN_CHUNK = 64
N_DEV = 8
LANES = 1024
VMEM_LIMIT = 56 * 1024 * 1024

ADAM_LR = 0.001
ADAM_B1 = 0.9
ADAM_B2 = 0.999
ADAM_EPS = 1e-08
ADAM_WD = 0.01
ADAM_STEP = 10

SHARDED = (
    ("w_in", (1024, 6920), 1),
    ("w_ffn_gate", (1024, 2816), 1),
    ("w_ffn_up", (1024, 2816), 1),
    ("w_ffn_down", (2816, 1024), 0),
    ("w_mem_kv", (1024, 1024), 0),
    ("w_o", (1024, 1024), 0),
    ("w_fox_out", (512, 1024), 1),
    ("w_rwkv_out", (512, 1024), 1),
    ("w_mem_out", (512, 1024), 1),
    ("rwkv_w_up", (64, 512), 1),
    ("rwkv_a_up", (64, 512), 1),
    ("rwkv_g_up", (128, 512), 1),
)
REPLICATED = (
    ("pre1_g", (1, 1024)), ("post1_g", (1, 1024)), ("pre2_g", (1, 1024)), ("post2_g", (1, 1024)),
    ("mem_norm_g", (1, 1024)), ("fox_f_bias", (1, 8)), ("rwkv_mu", (1, 1792)), ("rwkv_w0", (1, 512)),
    ("rwkv_a0", (1, 512)), ("rwkv_k_k", (1, 512)), ("rwkv_k_a", (1, 512)), ("rwkv_r_k", (1, 8, 64)),
    ("rwkv_gn_g", (1, 512)), ("rwkv_gn_b", (1, 512)),
)
WEIGHT_ORDER = ('pre1_g', 'post1_g', 'pre2_g', 'post2_g', 'mem_norm_g', 'w_in', 'fox_f_bias', 'rwkv_mu',
                'rwkv_w0', 'rwkv_w_up', 'rwkv_a0', 'rwkv_a_up', 'rwkv_g_up', 'rwkv_k_k', 'rwkv_k_a',
                'rwkv_r_k', 'rwkv_gn_g', 'rwkv_gn_b', 'w_mem_kv', 'w_fox_out', 'w_rwkv_out', 'w_mem_out',
                'w_o', 'w_ffn_gate', 'w_ffn_up', 'w_ffn_down')


def _block_shape(shape, axis):
    return tuple(s // N_DEV if i == axis else s for i, s in enumerate(shape))


def _rows_of(shape):
    n = 1
    for s in shape:
        n *= s
    return n // LANES


SHARD_ROWS = sum(_rows_of(_block_shape(s, a)) for _, s, a in SHARDED)
REPL_ELEMS = sum(_rows_of((LANES,) + s) for _, s in REPLICATED)
REPL_ROWS = -(-REPL_ELEMS // LANES)
PACK_ROWS = -(-(SHARD_ROWS + REPL_ROWS) // 128) * 128
GATHER_ROWS = -(-SHARD_ROWS // 16) * 16


def _cp(sem=None):
    return pltpu.CompilerParams(dimension_semantics=sem, vmem_limit_bytes=VMEM_LIMIT)


def _tile(dim, cap):
    best = None
    for t in range(128, min(dim, cap) + 1, 128):
        if dim % t == 0:
            best = t
    return best if best is not None else dim


def _two_terms(x):
    hi = x.astype(bf16)
    return hi, (x - hi.astype(f32)).astype(bf16)


def _dg(a, b, dims, exact):
    if exact == "split":
        (a_hi, a_lo), (b_hi, b_lo) = _two_terms(a), _two_terms(b)
        dot = functools.partial(lax.dot_general, dimension_numbers=dims, preferred_element_type=f32)
        return dot(a_hi, b_hi) + (dot(a_hi, b_lo) + dot(a_lo, b_hi))
    if exact:
        return lax.dot_general(a, b, dims, precision=_HI, preferred_element_type=f32)
    return lax.dot_general(a.astype(bf16), b.astype(bf16), dims, preferred_element_type=f32)


def _make_mm(batched, exact):
    o = 1 if batched else 0
    bd = ((0,), (0,)) if batched else ((), ())
    d_nn = (((1 + o,), (o,)), bd)
    d_nt = (((1 + o,), (1 + o,)), bd)
    d_tn = (((o,), (o,)), bd)

    @jax.custom_vjp
    def nn(a, b):
        return _dg(a, b, d_nn, exact)

    @jax.custom_vjp
    def nt(a, b):
        return _dg(a, b, d_nt, exact)

    @jax.custom_vjp
    def tn(a, b):
        return _dg(a, b, d_tn, exact)

    nn.defvjp(lambda a, b: (_dg(a, b, d_nn, exact), (a, b)),
              lambda res, g: (_dg(g, res[1], d_nt, exact), _dg(res[0], g, d_tn, exact)))
    nt.defvjp(lambda a, b: (_dg(a, b, d_nt, exact), (a, b)),
              lambda res, g: (_dg(g, res[1], d_nn, exact), _dg(g, res[0], d_tn, exact)))
    tn.defvjp(lambda a, b: (_dg(a, b, d_tn, exact), (a, b)),
              lambda res, g: (_dg(res[1], g, d_nt, exact), _dg(res[0], g, d_nn, exact)))
    return nn, nt, tn


def _sigmoid(x):
    return 1.0 / (1.0 + jnp.exp(-x))


def _head_sum_raw(x):
    width = 2 * HD
    i = lax.broadcasted_iota(jnp.int32, (width, width), 0) // HD
    j = lax.broadcasted_iota(jnp.int32, (width, width), 1) // HD
    m = (i == j).astype(bf16)
    dims = (((1,), (0,)), ((), ()))
    out = []
    for p in range(x.shape[1] // width):
        xp = x[:, p * width:(p + 1) * width]
        hi = xp.astype(bf16)
        lo = (xp - hi.astype(f32)).astype(bf16)
        out.append(lax.dot_general(hi, m, dims, preferred_element_type=f32)
                   + lax.dot_general(lo, m, dims, preferred_element_type=f32))
    return jnp.concatenate(out, axis=1)


@jax.custom_vjp
def _head_sum(x):
    return _head_sum_raw(x)


_head_sum.defvjp(lambda x: (_head_sum_raw(x), None), lambda _, g: (_head_sum_raw(g),))


WEIGHT_TILE_BYTES = 13 * 512 * 1024
ACC_TILE_BYTES = 8 * 1024 * 1024


def _matmul(name, a, b, mode, add=None, out_dtype=f32):
    has_add = add is not None
    if mode == "tn":
        (k, m), (_, n) = a.shape, b.shape
        tn = _tile(n, max(128, ACC_TILE_BYTES // (4 * m)))
        tk = _tile(k, 1024)

        nk = k // tk

        def body(a_ref, b_ref, o_ref, acc):
            @pl.when(pl.program_id(1) == 0)
            def _():
                acc[...] = jnp.zeros_like(acc)

            acc[...] += lax.dot_general(a_ref[...].astype(bf16), b_ref[...].astype(bf16),
                                        (((0,), (0,)), ((), ())), preferred_element_type=f32)

            @pl.when(pl.program_id(1) == nk - 1)
            def _():
                o_ref[...] = acc[...].astype(o_ref.dtype)

        return pl.pallas_call(
            body, name=name, grid=(n // tn, nk),
            in_specs=[pl.BlockSpec((tk, m), lambda j, kk: (kk, 0)), pl.BlockSpec((tk, tn), lambda j, kk: (kk, j))],
            out_specs=pl.BlockSpec((m, tn), lambda j, kk: (0, j)), out_shape=jax.ShapeDtypeStruct((m, n), out_dtype),
            scratch_shapes=[pltpu.VMEM((m, tn), f32)],
            compiler_params=_cp(("parallel", "arbitrary")),
        )(a, b)

    (m, k) = a.shape
    n = b.shape[1] if mode == "nn" else b.shape[0]
    tm = _tile(m, 512)
    tn = _tile(n, max(128, WEIGHT_TILE_BYTES // (2 * k)))
    dims = (((1,), (0,)), ((), ())) if mode == "nn" else (((1,), (1,)), ((), ()))
    b_spec = pl.BlockSpec((k, tn), lambda j, i: (0, j)) if mode == "nn" else pl.BlockSpec((tn, k), lambda j, i: (j, 0))
    o_spec = pl.BlockSpec((tm, tn), lambda j, i: (i, j))

    def body(*refs):
        a_ref, b_ref = refs[0], refs[1]
        o_ref = refs[-1]
        r = lax.dot_general(a_ref[...].astype(bf16), b_ref[...].astype(bf16), dims, preferred_element_type=f32)
        if has_add:
            r = r + refs[2][...]
        o_ref[...] = r.astype(o_ref.dtype)

    return pl.pallas_call(
        body, name=name, grid=(n // tn, m // tm),
        in_specs=[pl.BlockSpec((tm, k), lambda j, i: (i, 0)), b_spec] + ([o_spec] if has_add else []),
        out_specs=o_spec, out_shape=jax.ShapeDtypeStruct((m, n), out_dtype),
        compiler_params=_cp(("parallel", "arbitrary")),
    )(*((a, b, add) if has_add else (a, b)))


def _sum_nt(name, a_list, b_list, side=None):
    m, n = a_list[0].shape[0], b_list[0].shape[0]
    tm = _tile(m, 256)
    n_g = len(a_list)
    srcs, per_peer = side if side is not None else ([], False)
    n_s = len(srcs)

    def body(*refs):
        o_ref = refs[2 * n_g + n_s]
        _side_exchange(refs[2 * n_g:2 * n_g + n_s], refs[2 * n_g + n_s + 1:2 * n_g + 2 * n_s + 1], per_peer,
                       refs[2 * n_g + 2 * n_s + 1:], m // tm)
        acc = None
        for g in range(n_g):
            r = lax.dot_general(refs[g][...].astype(bf16), refs[n_g + g][...].astype(bf16), (((1,), (1,)), ((), ())),
                                preferred_element_type=f32)
            acc = r if acc is None else acc + r
        o_ref[...] = acc

    res = pl.pallas_call(
        body, name=name, grid=(m // tm,),
        in_specs=[pl.BlockSpec((tm, a.shape[1]), lambda i: (i, 0)) for a in a_list]
        + [pl.BlockSpec(b.shape, lambda i: (0, 0)) for b in b_list] + [_HBM_SPEC] * n_s,
        out_specs=[pl.BlockSpec((tm, n), lambda i: (i, 0))] + [_HBM_SPEC] * n_s,
        out_shape=[jax.ShapeDtypeStruct((m, n), f32)] + _side_out_shapes(srcs, per_peer),
        scratch_shapes=_side_sems(n_s),
        compiler_params=_cp(("arbitrary",)),
    )(*a_list, *b_list, *srcs)
    return res[0], list(res[1:])


def _pieces(ref, widths):
    out, off = [], 0
    for w in widths:
        out.append(ref[:, off:off + w].astype(f32))
        off += w
    return out


def _store_pieces(ref, widths, vals, add_ref=None):
    off = 0
    for w, v in zip(widths, vals):
        ref[:, off:off + w] = (v if add_ref is None else v + add_ref[:, off:off + w]).astype(ref.dtype)
        off += w


def _rows_fwd(name, fn, consts, rows, params, outs, n_sums=0, tm=512, dtypes=None):
    t = (consts + rows)[0][0].shape[0]
    tm = min(tm, t)
    ins = consts + rows
    n_in, n_p, n_o = len(ins), len(params), len(outs)
    dtypes = dtypes or [f32] * n_o

    def body(*refs):
        in_refs, p_refs = refs[:n_in], refs[n_in:n_in + n_p]
        o_refs, s_refs = refs[n_in + n_p:n_in + n_p + n_o], refs[n_in + n_p + n_o:]
        vals = []
        for r, (_, widths) in zip(in_refs, ins):
            vals += _pieces(r, widths)
        res = fn(*vals, *[p[...] for p in p_refs])
        pos = 0
        for r, widths in zip(o_refs, outs):
            _store_pieces(r, widths, res[pos:pos + len(widths)])
            pos += len(widths)

        @pl.when(pl.program_id(0) == 0)
        def _():
            for s in s_refs:
                s[...] = jnp.zeros_like(s)

        for s, v in zip(s_refs, res[pos:]):
            s[...] += v

    row_spec = lambda w: pl.BlockSpec((tm, w), lambda i: (i, 0))
    full = lambda p: pl.BlockSpec(p.shape, lambda i: (0,) * p.ndim)
    return pl.pallas_call(
        body, name=name, grid=(t // tm,),
        in_specs=[row_spec(a.shape[1]) for a, _ in ins] + [full(p) for p in params],
        out_specs=[row_spec(sum(w)) for w in outs] + [pl.BlockSpec((1, 1), lambda i: (0, 0))] * n_sums,
        out_shape=[jax.ShapeDtypeStruct((t, sum(w)), dt) for w, dt in zip(outs, dtypes)] + [jax.ShapeDtypeStruct((1, 1), f32)] * n_sums,
        compiler_params=_cp(("arbitrary",)),
    )(*[a for a, _ in ins], *params)


def _rows_bwd(name, fn, consts, rows, params, outs, cts, n_sums=0, add=None, tm=512, dtypes=None):
    t = (consts + rows)[0][0].shape[0]
    tm = min(tm, t)
    n_c, n_r, n_p, n_o = len(consts), len(rows), len(params), len(outs)
    has_add = add is not None
    dtypes = dtypes or [f32] * n_r

    def body(*refs):
        pos = 0
        c_refs = refs[pos:pos + n_c]; pos += n_c
        r_refs = refs[pos:pos + n_r]; pos += n_r
        p_refs = refs[pos:pos + n_p]; pos += n_p
        ct_refs = refs[pos:pos + n_o]; pos += n_o
        add_ref = refs[pos] if has_add else None
        pos += 1 if has_add else 0
        dr_refs = refs[pos:pos + n_r]; pos += n_r
        dp_refs = refs[pos:pos + n_p]
        cvals, rvals = [], []
        for r, (_, widths) in zip(c_refs, consts):
            cvals += _pieces(r, widths)
        for r, (_, widths) in zip(r_refs, rows):
            rvals += _pieces(r, widths)
        pvals = [p[...] for p in p_refs]
        ctv = []
        for r, widths in zip(ct_refs, outs):
            ctv += _pieces(r, widths)
        ctv += [jnp.ones((1, 1), f32)] * n_sums
        _, vjp = jax.vjp(lambda *rp: tuple(fn(*cvals, *rp)), *rvals, *pvals)
        g = vjp(tuple(ctv))
        pos = 0
        for idx, (r, (_, widths)) in enumerate(zip(dr_refs, rows)):
            _store_pieces(r, widths, g[pos:pos + len(widths)], add_ref if idx == 0 else None)
            pos += len(widths)

        @pl.when(pl.program_id(0) == 0)
        def _():
            for dp in dp_refs:
                dp[...] = jnp.zeros_like(dp)

        for dp, v in zip(dp_refs, g[pos:]):
            dp[...] += v

    row_spec = lambda w: pl.BlockSpec((tm, w), lambda i: (i, 0))
    full = lambda p: pl.BlockSpec(p.shape, lambda i: (0,) * p.ndim)
    args = [a for a, _ in consts + rows] + list(params) + list(cts) + ([add] if has_add else [])
    res = pl.pallas_call(
        body, name=name, grid=(t // tm,),
        in_specs=[row_spec(a.shape[1]) for a, _ in consts + rows] + [full(p) for p in params]
        + [row_spec(sum(w)) for w in outs] + ([row_spec(add.shape[1])] if has_add else []),
        out_specs=[row_spec(a.shape[1]) for a, _ in rows] + [full(p) for p in params],
        out_shape=[jax.ShapeDtypeStruct(a.shape, dt) for (a, _), dt in zip(rows, dtypes)]
        + [jax.ShapeDtypeStruct(p.shape, f32) for p in params],
        compiler_params=_cp(("arbitrary",)),
    )(*args)
    return res[:n_r], res[n_r:]


def _rms(x, g):
    return x * lax.rsqrt(jnp.mean(x * x, axis=-1, keepdims=True) + NORM_EPS) * g


def _fn_rms(x, g):
    return (_rms(x, g),)


def _fn_rwkv_pre(r, k, v, wd, ad, gd, w0, w_up, a0, a_up, g_up, k_k, k_a):
    nn, _, _ = _make_mm(False, False)
    w_log = -_sigmoid(w0 + nn(jnp.tanh(wd), w_up)) * 0.6065306597126334
    a = _sigmoid(a0 + nn(ad, a_up))
    g = nn(_sigmoid(gd), g_up)
    kk = k * k_k
    kk = kk * lax.rsqrt(jnp.maximum(_head_sum(kk * kk), 1e-24))
    k2 = k * (1.0 + (a - 1.0) * k_a)
    return r, w_log, k2, v, -kk, kk * a, g


def _fn_rwkv_post(y, r, k2, v, g, gn_g, gn_b, r_k):
    mean = _head_sum(y) * (1.0 / HD)
    yc = y - mean
    var = _head_sum(yc * yc) * (1.0 / HD)
    yn = yc * lax.rsqrt(var + GN_EPS) * gn_g + gn_b
    bonus = _head_sum(r * k2 * r_k) * v
    return ((yn + bonus) * g,)


def _fn_merge(a_fox, a_rwkv, a_mem, g_fox, g_rwkv, g_mem):
    return (_sigmoid(g_fox) * a_fox + _sigmoid(g_rwkv) * a_rwkv + _sigmoid(g_mem) * a_mem,)


def _fn_post1(y, x, post1_g, pre2_g):
    h1 = x + _rms(y, post1_g)
    return h1, _rms(h1, pre2_g)


def _fn_swiglu(gp, up):
    return (gp * _sigmoid(gp) * up,)


def _fn_final(target, ffn, h1, post2_g):
    err = h1 + _rms(ffn, post2_g) - target
    per_row = jnp.mean(err * err, axis=-1, keepdims=True)
    return (0.5 * jnp.sum(per_row, axis=0, keepdims=True),)


def _shift_down(x):
    row = lax.broadcasted_iota(jnp.int32, x.shape, 0)
    return jnp.where(row == 0, 0.0, pltpu.roll(x, 1, 0))


def _shift_up(x):
    s = x.shape[0]
    row = lax.broadcasted_iota(jnp.int32, x.shape, 0)
    return jnp.where(row == s - 1, 0.0, pltpu.roll(x, s - 1, 0))


def _tokshift_fwd(p, mu, batch, seq):
    w = p.shape[1]
    tc = _tile(w, 384)

    def body(p_ref, mu_ref, o_ref):
        x = p_ref[...]
        o_ref[...] = x + (_shift_down(x) - x) * mu_ref[...]

    return pl.pallas_call(
        body, name="tokshift_fwd", grid=(w // tc, batch),
        in_specs=[pl.BlockSpec((seq, tc), lambda j, b: (b, j)), pl.BlockSpec((1, tc), lambda j, b: (0, j))],
        out_specs=pl.BlockSpec((seq, tc), lambda j, b: (b, j)),
        out_shape=jax.ShapeDtypeStruct(p.shape, f32),
        compiler_params=_cp(("parallel", "arbitrary")),
    )(p, mu)


def _tokshift_bwd(p, mu, dps, batch, seq):
    w = p.shape[1]
    tc = _tile(w, 384)

    def body(p_ref, mu_ref, d_ref, dp_ref, dmu_ref):
        x, mu_v, d = p_ref[...], mu_ref[...], d_ref[...]
        dp_ref[...] = (d * (1.0 - mu_v) + _shift_up(d * mu_v)).astype(dp_ref.dtype)

        @pl.when(pl.program_id(1) == 0)
        def _():
            dmu_ref[...] = jnp.zeros_like(dmu_ref)

        dmu_ref[...] += jnp.sum(d * (_shift_down(x) - x), axis=0, keepdims=True)

    return pl.pallas_call(
        body, name="tokshift_bwd", grid=(w // tc, batch),
        in_specs=[pl.BlockSpec((seq, tc), lambda j, b: (b, j)), pl.BlockSpec((1, tc), lambda j, b: (0, j)),
                  pl.BlockSpec((seq, tc), lambda j, b: (b, j))],
        out_specs=[pl.BlockSpec((seq, tc), lambda j, b: (b, j)), pl.BlockSpec((1, tc), lambda j, b: (0, j))],
        out_shape=[jax.ShapeDtypeStruct(p.shape, bf16), jax.ShapeDtypeStruct(mu.shape, f32)],
        compiler_params=_cp(("parallel", "arbitrary")),
    )(p, mu, dps)


def _cum_block(seq):
    return _tile(seq, 256)


def _fox_gate_fwd(f, bias, batch, seq):
    cb = _cum_block(seq)

    def body(f_ref, b_ref, c_ref):
        row = lax.broadcasted_iota(jnp.int32, (cb, cb), 0)
        col = lax.broadcasted_iota(jnp.int32, (cb, cb), 1)
        tri = (col <= row).astype(f32)
        carry = jnp.zeros((1, 128), f32)
        for i in range(seq // cb):
            z = f_ref[i * cb:(i + 1) * cb, :] + b_ref[...]
            ls = jnp.minimum(z, 0.0) - jnp.log(1.0 + jnp.exp(-jnp.abs(z)))
            c = _dg(tri, ls, (((1,), (0,)), ((), ())), True) + carry
            c_ref[i * cb:(i + 1) * cb, :] = c
            carry = c[cb - 1:cb, :]

    return pl.pallas_call(
        body, name="fox_gate_fwd", grid=(batch,),
        in_specs=[pl.BlockSpec((seq, 128), lambda b: (b, 0)), pl.BlockSpec((1, 128), lambda b: (0, 0))],
        out_specs=pl.BlockSpec((seq, 128), lambda b: (b, 0)),
        out_shape=jax.ShapeDtypeStruct(f.shape, f32),
        compiler_params=_cp(("arbitrary",)),
    )(f, bias)


def _fox_gate_bwd(f, bias, dc_a, dc_b, batch, seq):
    cb = _cum_block(seq)

    def body(f_ref, b_ref, da_ref, db_ref, df_ref, dbias_ref):
        row = lax.broadcasted_iota(jnp.int32, (cb, cb), 0)
        col = lax.broadcasted_iota(jnp.int32, (cb, cb), 1)
        triu = (col >= row).astype(f32)

        @pl.when(pl.program_id(0) == 0)
        def _():
            dbias_ref[...] = jnp.zeros_like(dbias_ref)

        lane = lax.broadcasted_iota(jnp.int32, (1, 128), 1)

        def by_head(blk):
            out = jnp.zeros((cb, 128), f32)
            for p in range(HEADS // 2):
                for e in range(2):
                    out = jnp.where(lane == 2 * p + e, _pick_lane(blk[:, p * 128:(p + 1) * 128], e), out)
            return out

        carry = jnp.zeros((1, 128), f32)
        tot = jnp.zeros((1, 128), f32)
        for i in reversed(range(seq // cb)):
            sl = slice(i * cb, (i + 1) * cb)
            dc = by_head(da_ref[sl, :] + db_ref[sl, :])
            dls = _dg(triu, dc, (((1,), (0,)), ((), ())), True) + carry
            carry = dls[0:1, :]
            df = dls * _sigmoid(-(f_ref[sl, :] + b_ref[...]))
            df_ref[sl, :] = df.astype(df_ref.dtype)
            tot = tot + jnp.sum(df, axis=0, keepdims=True)
        dbias_ref[...] += tot

    return pl.pallas_call(
        body, name="fox_gate_bwd", grid=(batch,),
        in_specs=[pl.BlockSpec((seq, 128), lambda b: (b, 0)), pl.BlockSpec((1, 128), lambda b: (0, 0)),
                  pl.BlockSpec((seq, HW), lambda b: (b, 0)), pl.BlockSpec((seq, HW), lambda b: (b, 0))],
        out_specs=[pl.BlockSpec((seq, 128), lambda b: (b, 0)), pl.BlockSpec((1, 128), lambda b: (0, 0))],
        out_shape=[jax.ShapeDtypeStruct(f.shape, bf16), jax.ShapeDtypeStruct((1, 128), f32)],
        compiler_params=_cp(("arbitrary",)),
    )(f, bias, dc_a, dc_b)


_HBM_SPEC = pl.BlockSpec(memory_space=pltpu.HBM)


def _side_out_shapes(srcs, per_peer):
    return [jax.ShapeDtypeStruct(((N_DEV,) + tuple(s.shape[1:] if per_peer else s.shape)), s.dtype) for s in srcs]


def _side_sems(n):
    if n == 0:
        return []
    return [pltpu.SemaphoreType.DMA((n, N_DEV - 1)), pltpu.SemaphoreType.DMA((n, N_DEV - 1)), pltpu.SemaphoreType.DMA((n,))]


def _peer_copies(src_refs, dst_refs, per_peer, sems):
    send_sems, recv_sems, local_sems = sems
    x, y, c = lax.axis_index("x"), lax.axis_index("y"), lax.axis_index("c")
    me = 4 * x + 2 * y + c

    def remote(src, dst, t, k, to):
        return pltpu.make_async_remote_copy(src_ref=src, dst_ref=dst, send_sem=send_sems.at[t, k - 1],
                                            recv_sem=recv_sems.at[t, k - 1], device_id=to,
                                            device_id_type=pl.DeviceIdType.MESH)

    direct, relays = [], []
    for t, (s, d) in enumerate(zip(src_refs, dst_refs)):
        direct.append((t, 0, pltpu.make_async_copy(s.at[me] if per_peer else s, d.at[me], local_sems.at[t])))
        for k in range(1, N_DEV):
            px = 1 - x if k & 4 else x
            py = 1 - y if k & 2 else y
            pc = 1 - c if k & 1 else c
            if per_peer:
                direct.append((t, k, remote(s.at[4 * px + 2 * py + pc], d.at[me], t, k, (px, py, pc))))
            elif k == 1 or not k & 1:
                direct.append((t, k, remote(s, d.at[me], t, k, (px, py, pc))))
            else:
                origin = d.at[4 * px + 2 * py + c]
                relays.append((t, k - 1, remote(origin, origin, t, k, (x, y, 1 - c))))
    return direct, relays


def _exchange_start(direct):
    for _, _, cp in direct:
        cp.start()


def _exchange_finish(direct, relays):
    landed = {(t, k): cp for t, k, cp in direct}
    for t, j, cp in relays:
        landed[(t, j)].wait_recv()
        cp.start()
    relayed = {(t, j) for t, j, _ in relays}
    for t, k, cp in direct:
        if k == 0:
            cp.wait()
        else:
            cp.wait_send()
            if (t, k) not in relayed:
                cp.wait_recv()
    for _, _, cp in relays:
        cp.wait()


def _side_exchange(src_refs, dst_refs, per_peer, sems, *grid):
    if not src_refs:
        return
    first = functools.reduce(jnp.logical_and, [pl.program_id(a) == 0 for a in range(len(grid))])
    last = functools.reduce(jnp.logical_and, [pl.program_id(a) == n - 1 for a, n in enumerate(grid)])

    @pl.when(first)
    def _():
        _exchange_start(_peer_copies(src_refs, dst_refs, per_peer, sems)[0])

    @pl.when(last)
    def _():
        _exchange_finish(*_peer_copies(src_refs, dst_refs, per_peer, sems))


def _exchange(name, srcs, per_peer):
    n = len(srcs)

    def body(*refs):
        direct, relays = _peer_copies(refs[:n], refs[n:2 * n], per_peer, refs[2 * n:])
        _exchange_start(direct)
        _exchange_finish(direct, relays)

    return pl.pallas_call(
        body, name=name, in_specs=[_HBM_SPEC] * n, out_specs=[_HBM_SPEC] * n,
        out_shape=_side_out_shapes(srcs, per_peer), scratch_shapes=_side_sems(n),
    )(*srcs)


FOX_T = 512
_NEG = -1e30
_D2 = (((1,), (1,)), ((), ()))
_D1 = (((1,), (0,)), ((), ()))
_D0 = (((0,), (0,)), ((), ()))


def _bdot(a, b, dims):
    return lax.dot_general(a.astype(bf16), b.astype(bf16), dims, preferred_element_type=f32)


def _pick_lane(x, lane):
    idx = lax.broadcasted_iota(jnp.int32, x.shape, 1)
    return jnp.sum(jnp.where(idx == lane, x, 0.0), axis=1, keepdims=True)


def _pick_row(x, row):
    idx = lax.broadcasted_iota(jnp.int32, x.shape, 0)
    return jnp.sum(jnp.where(idx == row, x, 0.0), axis=0, keepdims=True)


def _fox_fwd(qkv, c, c_rows, batch, seq, side=None):
    t = min(FOX_T, seq)
    nq = seq // t
    scale = HD ** -0.5
    srcs, per_peer = side if side is not None else ([], False)
    n_s = len(srcs)

    def body(*refs):
        q_ref, k_ref, v_ref, cq_ref, ck_ref = refs[:5]
        o_ref, lse_ref = refs[5 + n_s:7 + n_s]
        _side_exchange(refs[5:5 + n_s], refs[7 + n_s:7 + 2 * n_s], per_peer, refs[7 + 2 * n_s:], batch, PAIRS, nq)
        pair, i = pl.program_id(1), pl.program_id(2)
        lane = lax.broadcasted_iota(jnp.int32, (1, PAIR_W), 1)
        first = (lane // HD) == 0
        mine = [first, jnp.logical_not(first)]
        q = q_ref[...] * scale
        qs = [jnp.where(mine[e], q, 0.0) for e in range(2)]
        cqs = [_pick_lane(cq_ref[...], 2 * pair + e) for e in range(2)]
        causal = lax.broadcasted_iota(jnp.int32, (t, t), 1) <= lax.broadcasted_iota(jnp.int32, (t, t), 0)

        def block(j, carry, diagonal):
            rows = pl.ds(pl.multiple_of(j * t, t), t)
            kj, vj = k_ref[rows, :], v_ref[rows, :]
            ck_blk = ck_ref[0, :, rows]
            out = []
            for e in range(2):
                m, acc = carry[2 * e:2 * e + 2]
                s = _bdot(qs[e], kj, _D2) + cqs[e] - _pick_row(ck_blk, 2 * pair + e)
                if diagonal:
                    s = jnp.where(causal, s, _NEG)
                m_new = jnp.maximum(m, jnp.max(s, axis=1, keepdims=True))
                p = jnp.exp(s - m_new)
                out += [m_new, jnp.exp(m - m_new) * acc + _bdot(p, jnp.where(mine[e], vj, 1.0), _D1)]
            return tuple(out)

        init = (jnp.full((t, 1), _NEG, f32), jnp.zeros((t, PAIR_W), f32)) * 2
        carry = lax.fori_loop(0, i, lambda j, cr: block(j, cr, False), init)
        m0, a0, m1, a1 = block(i, carry, True)
        l0, l1 = _pick_lane(a0, HD), _pick_lane(a1, 0)
        o_ref[...] = jnp.where(first, a0 / l0, a1 / l1)
        lse_ref[...] = jnp.where(lane == 0, m0 + jnp.log(l0), jnp.where(lane == 1, m1 + jnp.log(l1), 0.0))

    q_spec = pl.BlockSpec((t, PAIR_W), lambda b, p, i: (b * nq + i, p))
    res = pl.pallas_call(
        body, name="fox_attn_fwd", grid=(batch, PAIRS, nq),
        in_specs=[q_spec,
                  pl.BlockSpec((seq, PAIR_W), lambda b, p, i: (b, PAIRS + p)),
                  pl.BlockSpec((seq, PAIR_W), lambda b, p, i: (b, 2 * PAIRS + p)),
                  pl.BlockSpec((t, 128), lambda b, p, i: (b * nq + i, 0)),
                  pl.BlockSpec((1, 8, seq), lambda b, p, i: (b, 0, 0))] + [_HBM_SPEC] * n_s,
        out_specs=[q_spec, q_spec] + [_HBM_SPEC] * n_s,
        out_shape=[jax.ShapeDtypeStruct((batch * seq, HW), f32)] * 2 + _side_out_shapes(srcs, per_peer),
        scratch_shapes=_side_sems(n_s),
        compiler_params=_cp(("arbitrary", "arbitrary", "arbitrary")),
    )(qkv, qkv, qkv, c, c_rows, *srcs)
    return res[0], res[1], list(res[2:])


def _fox_bwd(qkv, c, c_rows, o, lse, do, batch, seq):
    t = min(FOX_T, seq)
    nq = seq // t
    scale = HD ** -0.5

    def body(q_ref, k_ref, v_ref, cq_ref, ck_ref, o_ref, lse_ref, do_ref,
             dq_ref, dk_ref, dv_ref, dcq_ref, dck_ref, acc0, acc1):
        pair, i = pl.program_id(1), pl.program_id(2)
        accs = [acc0, acc1]

        @pl.when(i == 0)
        def _():
            dv_ref[...] = jnp.zeros_like(dv_ref)
            acc0[...] = jnp.zeros_like(acc0)
            acc1[...] = jnp.zeros_like(acc1)

        lane = lax.broadcasted_iota(jnp.int32, (1, PAIR_W), 1)
        first = (lane // HD) == 0
        mine = [first, jnp.logical_not(first)]
        q, d_o, o_i = q_ref[...] * scale, do_ref[...], o_ref[...]
        q0s = [jnp.where(mine[e], q, 0.0) for e in range(2)]
        q1s = [jnp.where(mine[e], q, 1.0) for e in range(2)]
        dos = [jnp.where(mine[e], d_o, 0.0) for e in range(2)]
        deltas = [jnp.sum(dos[e] * o_i, axis=1, keepdims=True) for e in range(2)]
        lses = [_pick_lane(lse_ref[...], e) for e in range(2)]
        cqs = [_pick_lane(cq_ref[...], 2 * pair + e) for e in range(2)]
        causal = lax.broadcasted_iota(jnp.int32, (t, t), 1) <= lax.broadcasted_iota(jnp.int32, (t, t), 0)

        def block(j, dqs, diagonal):
            rows = pl.ds(pl.multiple_of(j * t, t), t)
            kj, vj = k_ref[rows, :], v_ref[rows, :]
            ck_blk = ck_ref[0, :, rows]
            out = []
            for e in range(2):
                s = _bdot(q0s[e], kj, _D2) + cqs[e] - _pick_row(ck_blk, 2 * pair + e)
                if diagonal:
                    s = jnp.where(causal, s, _NEG)
                p = jnp.exp(s - lses[e])
                ds = p * (_bdot(dos[e], vj, _D2) - deltas[e])
                dv_ref[rows, :] += _bdot(p, dos[e], _D0)
                accs[e][rows, :] += _bdot(ds, q1s[e], _D0)
                out.append(dqs[e] + _bdot(ds, jnp.where(mine[e], kj, 1.0), _D1))
            return tuple(out)

        zero = jnp.zeros((t, PAIR_W), f32)
        dqs = lax.fori_loop(0, i, lambda j, cr: block(j, cr, False), (zero, zero))
        dq0, dq1 = block(i, dqs, True)
        dq_ref[...] = jnp.where(first, dq0, dq1) * scale
        dcq_ref[...] = jnp.where(lane == 0, _pick_lane(dq0, HD), jnp.where(lane == 1, _pick_lane(dq1, 0), 0.0))

        @pl.when(i == nq - 1)
        def _():
            a0, a1 = acc0[...], acc1[...]
            dk_ref[...] = jnp.where(first, a0, a1)
            dck_ref[...] = jnp.where(lane == 0, -_pick_lane(a0, HD), jnp.where(lane == 1, -_pick_lane(a1, 0), 0.0))

    blk = lambda col: pl.BlockSpec((t, PAIR_W), lambda b, p, i: (b * nq + i, col * PAIRS + p))
    whole = lambda col: pl.BlockSpec((seq, PAIR_W), lambda b, p, i: (b, col * PAIRS + p))
    t_all = batch * seq
    return pl.pallas_call(
        body, name="fox_attn_bwd", grid=(batch, PAIRS, nq),
        in_specs=[blk(0), whole(1), whole(2),
                  pl.BlockSpec((t, 128), lambda b, p, i: (b * nq + i, 0)),
                  pl.BlockSpec((1, 8, seq), lambda b, p, i: (b, 0, 0)),
                  blk(0), blk(0), blk(0)],
        out_specs=[blk(0), whole(0), whole(0), blk(0), whole(0)],
        out_shape=[jax.ShapeDtypeStruct((t_all, HW), f32)] * 5,
        scratch_shapes=[pltpu.VMEM((seq, PAIR_W), f32), pltpu.VMEM((seq, PAIR_W), f32)],
        compiler_params=_cp(("parallel", "parallel", "arbitrary")),
    )(qkv, qkv, qkv, c, c_rows, o, lse, do)


def _mem_block(q, km, vm):
    nn, nt, _ = _make_mm(False, False)
    logits = nt(q, km) * (MEM_HD ** -0.5)
    m = lax.stop_gradient(jnp.max(logits, axis=-1, keepdims=True))
    e = jnp.exp(logits - m)
    return nn(e / jnp.sum(e, axis=-1, keepdims=True), vm)


def _mem_specs(seq, tq):
    nq = seq // tq
    qs = pl.BlockSpec((tq, MEM_HD), lambda b, h, i: (b * nq + i, h))
    ks = pl.BlockSpec((MEM_LEN, MEM_HD), lambda b, h, i: (b, h))
    vs = pl.BlockSpec((MEM_LEN, MEM_HD), lambda b, h, i: (b, MEM_HEADS + h))
    return nq, qs, ks, vs


def _mem_fwd(q, mem_kv, batch, seq):
    tq = min(512, seq)
    nq, qs, ks, vs = _mem_specs(seq, tq)

    def body(q_ref, k_ref, v_ref, o_ref):
        o_ref[...] = _mem_block(q_ref[...].astype(f32), k_ref[...], v_ref[...]).astype(o_ref.dtype)

    return pl.pallas_call(
        body, name="mem_attn_fwd", grid=(batch, MEM_HEADS, nq),
        in_specs=[qs, ks, vs], out_specs=qs, out_shape=jax.ShapeDtypeStruct(q.shape, bf16),
        compiler_params=_cp(("parallel", "parallel", "arbitrary")),
    )(q, mem_kv, mem_kv)


def _mem_bwd(q, mem_kv, do, batch, seq):
    tq = min(512, seq)
    nq, qs, ks, vs = _mem_specs(seq, tq)

    def body(q_ref, k_ref, v_ref, do_ref, dq_ref, dk_ref, dv_ref):
        _, vjp = jax.vjp(_mem_block, q_ref[...].astype(f32), k_ref[...], v_ref[...])
        dq, dk, dv = vjp(do_ref[...])
        dq_ref[...] = dq.astype(dq_ref.dtype)

        @pl.when(pl.program_id(2) == 0)
        def _():
            dk_ref[...] = jnp.zeros_like(dk_ref)
            dv_ref[...] = jnp.zeros_like(dv_ref)

        dk_ref[...] += dk
        dv_ref[...] += dv

    return pl.pallas_call(
        body, name="mem_attn_bwd", grid=(batch, MEM_HEADS, nq),
        in_specs=[qs, ks, vs, qs], out_specs=[qs, ks, ks],
        out_shape=[jax.ShapeDtypeStruct(q.shape, bf16), jax.ShapeDtypeStruct((batch * MEM_LEN, MEM_W), f32),
                   jax.ShapeDtypeStruct((batch * MEM_LEN, MEM_W), f32)],
        compiler_params=_cp(("parallel", "parallel", "arbitrary")),
    )(q, mem_kv, mem_kv, do)


@jax.custom_vjp
def _halves(x):
    c = x.shape[1] // 2
    return x[:, :c], x[:, c:]


_halves.defvjp(lambda x: ((x[:, :x.shape[1] // 2], x[:, x.shape[1] // 2:]), None),
               lambda _, g: (jnp.concatenate(g, axis=1),))


@jax.custom_vjp
def _lead_halves(x):
    n = x.shape[0] // 2
    return x[:n], x[n:]


_lead_halves.defvjp(lambda x: ((x[:x.shape[0] // 2], x[x.shape[0] // 2:]), None),
                    lambda _, g: (jnp.concatenate(g, axis=0),))


def _scan_chunk(s0, r, wl, k, v, a, b):
    nn, nt, tn = _make_mm(True, False)
    nn_exact, _, _ = _make_mm(True, True)
    _, nt_exact, _ = _make_mm(True, "split")
    hp, c, lanes = r.shape
    row = lax.broadcasted_iota(jnp.int32, (c, c), 0)
    col = lax.broadcasted_iota(jnp.int32, (c, c), 1)
    first = (lax.broadcasted_iota(jnp.int32, (1, 1, lanes), 2) // HD) == 0
    tri = jnp.broadcast_to((col <= row).astype(f32)[None], (hp, c, c))
    lg = nn_exact(tri, wl)
    lg_end = lg[:, c - 1:c, :]
    grow, shrink, to_end = jnp.exp(lg), jnp.exp(-lg), jnp.exp(lg_end - lg)
    rt, kt, bt, at = r * grow, k * shrink, b * shrink, a * jnp.exp(lg - wl)
    strict, incl = (col < row)[None], (col <= row)[None]
    twice = lambda t: jnp.concatenate([t, t], axis=0)
    queries = jnp.concatenate([at, rt], axis=1)
    per_head = jnp.concatenate([jnp.where(first, queries, 0.0), jnp.where(first, 0.0, queries)], axis=0)
    (ab, rb), (ak, rk) = _halves(nt_exact(per_head, twice(bt))), _halves(nt_exact(per_head, twice(kt)))
    l_ab = jnp.where(strict, ab, 0.0)
    a_ak = jnp.where(strict, ak, 0.0)
    a_rb = jnp.where(incl, rb, 0.0)
    a_rk = jnp.where(incl, rk, 0.0)
    inv = (col == row).astype(f32)[None] + l_ab
    power, n = l_ab, 1
    while 2 * n < c:
        power = nn(power, power)
        inv = inv + nn(inv, power)
        n *= 2

    def apply(m, t):
        lo, hi = _lead_halves(nn(m, twice(t)))
        return jnp.where(first, lo, hi)

    sa = apply(inv, nt(at, s0) + apply(a_ak, v))
    y = nt(rt, s0) + apply(a_rk, v) + apply(a_rb, sa)
    same_head = ((lax.broadcasted_iota(jnp.int32, (lanes, lanes), 0) // HD)
                 == (lax.broadcasted_iota(jnp.int32, (lanes, lanes), 1) // HD))[None]
    s1 = s0 * jnp.exp(lg_end) + jnp.where(same_head, tn(v, k * to_end) + tn(sa, b * to_end), 0.0)
    return y, s1


PAIRS = HEADS // 2
PAIR_W = 2 * HD


def _pair_stack(ref, off):
    return jnp.stack([ref[b, :, off + p * PAIR_W:off + (p + 1) * PAIR_W]
                      for b in range(ref.shape[0]) for p in range(PAIRS)])


def _pair_store(ref, off, val, add_ref=None):
    for b in range(ref.shape[0]):
        for p in range(PAIRS):
            sl = slice(off + p * PAIR_W, off + (p + 1) * PAIR_W)
            v = val[b * PAIRS + p]
            ref[b, :, sl] = v if add_ref is None else v + add_ref[b, :, sl]


def _scan_fwd(main6, batch, seq, side=None):
    c = min(SCAN_CHUNK, seq)
    nc = seq // c
    hp = batch * PAIRS
    srcs, per_peer = side if side is not None else ([], False)
    n_s = len(srcs)

    def body(*refs):
        z_ref, y_ref, s_ref, st = refs[0], refs[1 + n_s], refs[2 + n_s], refs[3 + 2 * n_s]
        _side_exchange(refs[1:1 + n_s], refs[3 + n_s:3 + 2 * n_s], per_peer, refs[4 + 2 * n_s:], nc)

        @pl.when(pl.program_id(0) == 0)
        def _():
            st[...] = jnp.zeros_like(st)

        s0 = st[...]
        s_ref[0] = s0
        y, s1 = _scan_chunk(s0, *[_pair_stack(z_ref, comp * HW) for comp in range(6)])
        _pair_store(y_ref, 0, y)
        st[...] = s1

    res = pl.pallas_call(
        body, name="rwkv_scan_fwd", grid=(nc,),
        in_specs=[pl.BlockSpec((batch, c, 6 * HW), lambda i: (0, i, 0))] + [_HBM_SPEC] * n_s,
        out_specs=[pl.BlockSpec((batch, c, HW), lambda i: (0, i, 0)),
                   pl.BlockSpec((1, hp, PAIR_W, PAIR_W), lambda i: (i, 0, 0, 0))] + [_HBM_SPEC] * n_s,
        out_shape=[jax.ShapeDtypeStruct((batch, seq, HW), f32), jax.ShapeDtypeStruct((nc, hp, PAIR_W, PAIR_W), f32)]
        + _side_out_shapes(srcs, per_peer),
        scratch_shapes=[pltpu.VMEM((hp, PAIR_W, PAIR_W), f32)] + _side_sems(n_s),
        compiler_params=_cp(("arbitrary",)),
    )(main6.reshape(batch, seq, 6 * HW), *srcs)
    return res[0].reshape(batch * seq, HW), res[1], list(res[2:])


def _scan_bwd(main6, states, dy, extra, batch, seq, side=None):
    c = min(SCAN_CHUNK, seq)
    nc = seq // c
    hp = batch * PAIRS
    srcs, per_peer = side if side is not None else ([], False)
    n_s = len(srcs)

    def body(*refs):
        z_ref, s_ref, dy_ref, ex_ref = refs[:4]
        dz_ref, dst = refs[4 + n_s], refs[5 + 2 * n_s]
        _side_exchange(refs[4:4 + n_s], refs[5 + n_s:5 + 2 * n_s], per_peer, refs[6 + 2 * n_s:], nc)

        @pl.when(pl.program_id(0) == 0)
        def _():
            dst[...] = jnp.zeros_like(dst)

        _, vjp = jax.vjp(_scan_chunk, s_ref[0], *[_pair_stack(z_ref, comp * HW) for comp in range(6)])
        g = vjp((_pair_stack(dy_ref, 0), dst[...]))
        dst[...] = g[0]
        for comp in range(6):
            _pair_store(dz_ref, comp * HW, g[1 + comp], ex_ref)

    back = lambda i: (0, nc - 1 - i, 0)
    wide = pl.BlockSpec((batch, c, 6 * HW), back)
    res = pl.pallas_call(
        body, name="rwkv_scan_bwd", grid=(nc,),
        in_specs=[wide, pl.BlockSpec((1, hp, PAIR_W, PAIR_W), lambda i: (nc - 1 - i, 0, 0, 0)),
                  pl.BlockSpec((batch, c, HW), back), wide] + [_HBM_SPEC] * n_s,
        out_specs=[wide] + [_HBM_SPEC] * n_s,
        out_shape=[jax.ShapeDtypeStruct((batch, seq, 6 * HW), f32)] + _side_out_shapes(srcs, per_peer),
        scratch_shapes=[pltpu.VMEM((hp, PAIR_W, PAIR_W), f32)] + _side_sems(n_s),
        compiler_params=_cp(("arbitrary",)),
    )(main6.reshape(batch, seq, 6 * HW), states, dy.reshape(batch, seq, HW), extra.reshape(batch, seq, 6 * HW), *srcs)
    return res[0].reshape(batch * seq, 6 * HW), list(res[1:])


def _to_heads(x, batch, seq, k):
    return x.reshape(batch, seq, k, HEADS, HD).transpose(2, 0, 3, 1, 4).reshape(k, batch * HEADS, seq, HD)


def _from_heads(x, batch, seq, k):
    return x.reshape(k, batch, HEADS, seq, HD).transpose(1, 3, 0, 2, 4).reshape(batch * seq, k * HW)


def _pad_cols(x, width):
    return jnp.pad(x, ((0, 0), (0, width - x.shape[1])))


def _split_w_in(w):
    z64 = jnp.zeros((w.shape[0], 64), w.dtype)
    w_r = jnp.concatenate([w[:, 1544:3080], w[:, 3080:3144], z64, w[:, 3144:3208], z64, w[:, 3208:3336]], axis=1)
    return w[:, :1536], _pad_cols(w[:, 1536:1544], 128), w_r, w[:, 3336:3848], w[:, 3848:]


def _merge_w_in(g_qkv, g_f, g_r, g_mq, g_g):
    return jnp.concatenate([g_qkv, g_f[:, :8], g_r[:, :1536], g_r[:, 1536:1600], g_r[:, 1664:1728], g_r[:, 1792:],
                            g_mq, g_g], axis=1)


def _pad_lora(v):
    z64 = jnp.zeros((1, 64), v.dtype)
    return jnp.concatenate([v[:, :1536], v[:, 1536:1600], z64, v[:, 1600:1664], z64, v[:, 1664:]], axis=1)


def _unpad_lora(v):
    return jnp.concatenate([v[:, :1536], v[:, 1536:1600], v[:, 1664:1728], v[:, 1792:]], axis=1)


def _local_step(x, mem, target, w, p, late=None, early=None, last=None):
    batch, seq, _ = x.shape
    t = batch * seq
    x2, tg2, mem2 = x.reshape(t, D), target.reshape(t, D), mem.reshape(batch * MEM_LEN, D)
    w_qkv, w_f, w_r, w_mq, w_g3 = _split_w_in(w["w_in"])
    mu = _pad_lora(p["rwkv_mu"])
    bias = _pad_cols(p["fox_f_bias"], 128)
    r_k = p["rwkv_r_k"].reshape(1, HW)
    post_params = [p["rwkv_gn_g"], p["rwkv_gn_b"], r_k]
    rw_widths = [HW, HW, HW, LORA_PAD, LORA_PAD, LORA_PAD]
    six = [HW] * 6

    (u,) = _rows_fwd("rms_pre1", _fn_rms, [], [(x2, [D])], [p["pre1_g"]], [[D]], dtypes=[bf16])
    p_qkv = _matmul("proj_qkv", u, w_qkv, "nn", out_dtype=bf16)
    p_f = _matmul("proj_f", u, w_f, "nn")
    p_r = _matmul("proj_rwkv", u, w_r, "nn")
    p_mq = _matmul("proj_memq", u, w_mq, "nn", out_dtype=bf16)
    p_g = _matmul("proj_gate", u, w_g3, "nn", out_dtype=bf16)

    c = _fox_gate_fwd(p_f, bias, batch, seq)
    c_rows = c[:, :HEADS].reshape(batch, seq, HEADS).transpose(0, 2, 1)
    fox_o, lse, gathered = _fox_fwd(p_qkv, c, c_rows, batch, seq, side=(late[0], False) if late else None)
    if late:
        w = {**w, **late[2](gathered, 0)}
    fox_out = fox_o.astype(bf16)

    w_up = jnp.pad(w["rwkv_w_up"].astype(f32), ((0, LORA_PAD - 64), (0, 0)))
    a_up = jnp.pad(w["rwkv_a_up"].astype(f32), ((0, LORA_PAD - 64), (0, 0)))
    pre_params = [p["rwkv_w0"], w_up, p["rwkv_a0"], a_up, w["rwkv_g_up"].astype(f32), p["rwkv_k_k"], p["rwkv_k_a"]]
    ps = _tokshift_fwd(p_r, mu, batch, seq)
    main6, g_rw = _rows_fwd("rwkv_pre", _fn_rwkv_pre, [], [(ps, rw_widths)], pre_params, [six, [HW]], tm=256)
    y_rw, states, gathered = _scan_fwd(main6, batch, seq, side=(late[1], False) if late else None)
    if late:
        w = {**w, **late[2](gathered, 1)}
    post_consts = []
    post_rows = [(y_rw, [HW]), (main6, six), (g_rw, [HW])]

    def fn_post(y, r, _wl, k2, v, _a, _b, g, gn_g, gn_b, rk):
        return _fn_rwkv_post(y, r, k2, v, g, gn_g, gn_b, rk)

    (rwkv_out,) = _rows_fwd("rwkv_post", fn_post, post_consts, post_rows, post_params, [[HW]], dtypes=[bf16], tm=256)

    (memn,) = _rows_fwd("rms_mem", _fn_rms, [], [(mem2, [D])], [p["mem_norm_g"]], [[D]], dtypes=[bf16])
    mem_kv = _matmul("proj_memkv", memn, w["w_mem_kv"], "nn")
    mem_out = _mem_fwd(p_mq, mem_kv, batch, seq)

    a_fox = _matmul("out_fox", fox_out, w["w_fox_out"], "nn", out_dtype=bf16)
    a_rwkv = _matmul("out_rwkv", rwkv_out, w["w_rwkv_out"], "nn", out_dtype=bf16)
    a_mem = _matmul("out_mem", mem_out, w["w_mem_out"], "nn", out_dtype=bf16)
    merge_rows = [(a_fox, [D]), (a_rwkv, [D]), (a_mem, [D]), (p_g, [D, D, D])]
    (merged,) = _rows_fwd("merge", _fn_merge, [], merge_rows, [], [[D]], dtypes=[bf16])
    yy = _matmul("out_o", merged, w["w_o"], "nn")
    post1_rows = [(yy, [D]), (x2, [D])]
    post1_params = [p["post1_g"], p["pre2_g"]]
    h1, u2 = _rows_fwd("post1", _fn_post1, [], post1_rows, post1_params, [[D], [D]], dtypes=[f32, bf16])
    gp = _matmul("ffn_gate", u2, w["w_ffn_gate"], "nn", out_dtype=bf16)
    up = _matmul("ffn_up", u2, w["w_ffn_up"], "nn", out_dtype=bf16)
    (hmid,) = _rows_fwd("swiglu", _fn_swiglu, [], [(gp, [D_FF]), (up, [D_FF])], [], [[D_FF]], dtypes=[bf16])
    ffn = _matmul("ffn_down", hmid, w["w_ffn_down"], "nn")
    final_rows = [(ffn, [D]), (h1, [D])]
    (loss,) = _rows_fwd("final", _fn_final, [(tg2, [D])], final_rows, [p["post2_g"]], [], n_sums=1)

    gw, gp_ = {}, {}
    (d_ffn, d_h1), (gp_["post2_g"],) = _rows_bwd("final_bwd", _fn_final, [(tg2, [D])], final_rows, [p["post2_g"]], [], [],
                                                  n_sums=1, dtypes=[bf16, f32])
    d_hmid = _matmul("ffn_down_dx", d_ffn, w["w_ffn_down"], "nt", out_dtype=bf16)
    gw["w_ffn_down"] = _matmul("ffn_down_dw", hmid, d_ffn, "tn", out_dtype=bf16)
    (d_gp, d_up), _ = _rows_bwd("swiglu_bwd", _fn_swiglu, [], [(gp, [D_FF]), (up, [D_FF])], [], [[D_FF]], [d_hmid],
                                dtypes=[bf16, bf16])
    d_u2 = _matmul("ffn_gate_dx", d_gp, w["w_ffn_gate"], "nt")
    d_u2 = _matmul("ffn_up_dx", d_up, w["w_ffn_up"], "nt", add=d_u2)
    gw["w_ffn_gate"] = _matmul("ffn_gate_dw", u2, d_gp, "tn", out_dtype=bf16)
    gw["w_ffn_up"] = _matmul("ffn_up_dw", u2, d_up, "tn", out_dtype=bf16)
    (d_yy, d_x_res), (gp_["post1_g"], gp_["pre2_g"]) = _rows_bwd(
        "post1_bwd", _fn_post1, [], post1_rows, post1_params, [[D], [D]], [d_h1, d_u2], dtypes=[bf16, f32])
    d_merged = _matmul("out_o_dx", d_yy, w["w_o"], "nt", out_dtype=bf16)
    gw["w_o"] = _matmul("out_o_dw", merged, d_yy, "tn", out_dtype=bf16)
    (d_a_fox, d_a_rwkv, d_a_mem, d_p_g), _ = _rows_bwd("merge_bwd", _fn_merge, [], merge_rows, [], [[D]], [d_merged],
                                                       dtypes=[bf16] * 4)
    d_fox_out = _matmul("out_fox_dx", d_a_fox, w["w_fox_out"], "nt")
    gw["w_fox_out"] = _matmul("out_fox_dw", fox_out, d_a_fox, "tn", out_dtype=bf16)
    d_rwkv_out = _matmul("out_rwkv_dx", d_a_rwkv, w["w_rwkv_out"], "nt")
    gw["w_rwkv_out"] = _matmul("out_rwkv_dw", rwkv_out, d_a_rwkv, "tn", out_dtype=bf16)
    d_mem_out = _matmul("out_mem_dx", d_a_mem, w["w_mem_out"], "nt")
    gw["w_mem_out"] = _matmul("out_mem_dw", mem_out, d_a_mem, "tn", out_dtype=bf16)

    d_p_mq, d_km, d_vm = _mem_bwd(p_mq, mem_kv, d_mem_out, batch, seq)
    d_mem_kv = jnp.concatenate([d_km, d_vm], axis=1).astype(bf16)
    gw["w_mem_kv"] = _matmul("proj_memkv_dw", memn, d_mem_kv, "tn", out_dtype=bf16)
    d_memn = _matmul("proj_memkv_dx", d_mem_kv, w["w_mem_kv"], "nt")
    _, (gp_["mem_norm_g"],) = _rows_bwd("rms_mem_bwd", _fn_rms, [], [(mem2, [D])], [p["mem_norm_g"]], [[D]], [d_memn])

    d_q, d_k, d_v, d_cq, d_ck = _fox_bwd(p_qkv, c, c_rows, fox_o, lse, d_fox_out, batch, seq)
    d_p_qkv = jnp.concatenate([d_q, d_k, d_v], axis=1).astype(bf16)
    d_p_f, d_bias = _fox_gate_bwd(p_f, bias, d_cq, d_ck, batch, seq)
    gp_["fox_f_bias"] = d_bias[:, :HEADS]

    (d_y_rw, d_main6_post, d_g_rw), (gp_["rwkv_gn_g"], gp_["rwkv_gn_b"], d_rk) = _rows_bwd(
        "rwkv_post_bwd", fn_post, post_consts, post_rows, post_params, [[HW]], [d_rwkv_out], tm=256)
    gp_["rwkv_r_k"] = d_rk.reshape(1, HEADS, HD)
    d_main6, early_got = _scan_bwd(main6, states, d_y_rw, d_main6_post, batch, seq,
                                   side=(early(gw), True) if early else None)

    def fn_pre_sum(*args):
        return _fn_rwkv_pre(*args)

    (d_ps,), d_pre = _rows_bwd("rwkv_pre_bwd", fn_pre_sum, [], [(ps, rw_widths)], pre_params, [six, [HW]],
                               [d_main6, d_g_rw], tm=256)
    gp_["rwkv_w0"], d_w_up, gp_["rwkv_a0"], d_a_up, gw["rwkv_g_up"], gp_["rwkv_k_k"], gp_["rwkv_k_a"] = d_pre
    gw["rwkv_w_up"], gw["rwkv_a_up"] = d_w_up[:64], d_a_up[:64]
    d_p_r, d_mu = _tokshift_bwd(p_r, mu, d_ps, batch, seq)
    gp_["rwkv_mu"] = _unpad_lora(d_mu)

    gw["w_in"] = _merge_w_in(_matmul("proj_qkv_dw", u, d_p_qkv, "tn", out_dtype=bf16), _matmul("proj_f_dw", u, d_p_f, "tn", out_dtype=bf16),
                             _matmul("proj_rwkv_dw", u, d_p_r, "tn", out_dtype=bf16), _matmul("proj_memq_dw", u, d_p_mq, "tn", out_dtype=bf16),
                             _matmul("proj_gate_dw", u, d_p_g, "tn", out_dtype=bf16))
    d_u, last_got = _sum_nt("proj_dx", [d_p_qkv, d_p_f, d_p_r, d_p_mq, d_p_g], [w_qkv, w_f, w_r, w_mq, w_g3],
                            side=(last(gw), True) if last else None)
    (d_x,), (gp_["pre1_g"],) = _rows_bwd("rms_pre1_bwd", _fn_rms, [], [(x2, [D])], [p["pre1_g"]], [[D]], [d_u], add=d_x_res)
    return loss, d_x.reshape(x.shape), gw, gp_, early_got, last_got


def _rows_add(name, a, b):
    (s,) = _rows_fwd(name, lambda u, v: (u + v,), [], [(a, [a.shape[1]]), (b, [b.shape[1]])], [], [[a.shape[1]]])
    return s


def _adamw(name, recv, w, m, v):
    rows, cols = w.shape
    tr = max(t for t in range(16, min(rows, 128) + 1, 16) if rows % t == 0)

    def body(g_ref, w_ref, m_ref, v_ref, go_ref, d_ref, mo_ref, vo_ref):
        g = g_ref[0].astype(f32)
        for s in range(1, N_DEV):
            g = g + g_ref[s].astype(f32)
        m_new = ADAM_B1 * m_ref[...] + (1.0 - ADAM_B1) * g
        v_new = ADAM_B2 * v_ref[...] + (1.0 - ADAM_B2) * (g * g)
        m_hat = m_new / (1.0 - ADAM_B1 ** ADAM_STEP)
        v_hat = v_new / (1.0 - ADAM_B2 ** ADAM_STEP)
        go_ref[...] = g
        d_ref[...] = -ADAM_LR * (m_hat / (jnp.sqrt(v_hat) + ADAM_EPS) + ADAM_WD * w_ref[...])
        mo_ref[...] = m_new
        vo_ref[...] = v_new

    spec = pl.BlockSpec((tr, cols), lambda i: (i, 0))
    return pl.pallas_call(
        body, name=name, grid=(rows // tr,),
        in_specs=[pl.BlockSpec((N_DEV, tr, cols), lambda i: (0, i, 0)), spec, spec, spec],
        out_specs=[spec] * 4, out_shape=[jax.ShapeDtypeStruct(w.shape, f32)] * 4,
        compiler_params=_cp(("parallel",)),
    )(recv, w, m, v)


GROUPS = (
    ("in", ("w_in",), 1),
    ("memkv", ("w_mem_kv",), 0),
    ("ffn_gu", ("w_ffn_gate", "w_ffn_up"), 1),
    ("down_o", ("w_ffn_down", "w_o"), 0),
    ("outs", ("w_fox_out", "w_rwkv_out", "w_mem_out"), 1),
    ("lora", ("rwkv_w_up", "rwkv_a_up", "rwkv_g_up"), 0),
)
FIRST_GROUPS = ("in", "memkv")
LATE_GROUPS = (("down_o", "outs", "lora"), ("ffn_gu",))
EARLY_GRAD_GROUPS = ("memkv", "ffn_gu", "down_o", "outs")
LAST_GRAD_GROUPS = ("in", "lora")
SHARD_AXIS = {n: a for n, _, a in SHARDED}
SMALL_ROWS = 16


def _group_local(shards, members, join):
    parts = [shards[n].reshape(shards[n].shape[-2:]) for n in members]
    return parts[0] if len(parts) == 1 else jnp.concatenate(parts, axis=join)


def _group_split(arr, members, join, lead=False):
    out, off = {}, 0
    for n in members:
        shape = dict((k, s) for k, s, _ in SHARDED)[n]
        size = _block_shape(shape, SHARD_AXIS[n])[join]
        idx = [slice(None)] * arr.ndim
        idx[arr.ndim - 2 + join] = slice(off, off + size)
        out[n] = arr[tuple(idx)]
        off += size
    return out


def _full_from_blocks(blocks, axis):
    if axis == 0:
        return blocks.reshape(-1, blocks.shape[2])
    return blocks.transpose(1, 0, 2).reshape(blocks.shape[1], -1)


def _blocks_from_full(full, axis):
    if axis == 0:
        return full.reshape(N_DEV, -1, full.shape[1])
    return full.reshape(full.shape[0], N_DEV, -1).transpose(1, 0, 2)


def _assemble(gathered, names):
    out = {}
    for arr, g in zip(gathered, names):
        _, members, join = [grp for grp in GROUPS if grp[0] == g][0]
        for n, blk in _group_split(arr, members, join, lead=True).items():
            out[n] = _full_from_blocks(blk, SHARD_AXIS[n])
    return out


def _grad_blocks(gw, names):
    out = []
    for g in names:
        _, members, join = [grp for grp in GROUPS if grp[0] == g][0]
        parts = [_blocks_from_full(gw[n].astype(bf16), SHARD_AXIS[n]) for n in members]
        out.append(parts[0] if len(parts) == 1 else jnp.concatenate(parts, axis=1 + join))
    return out


def _small_pack(d):
    flat = jnp.concatenate([d[n].reshape(-1) for n, _ in REPLICATED])
    return jnp.pad(flat, (0, SMALL_ROWS * LANES - REPL_ELEMS)).reshape(SMALL_ROWS, LANES)


def _small_unpack(packed):
    out, flat, off = {}, packed.reshape(-1), 0
    for n, shape in REPLICATED:
        k = _rows_of((LANES,) + shape)
        out[n] = flat[off:off + k].reshape(shape)
        off += k
    return out


def kernel(x, mem, pre1_g, post1_g, pre2_g, post2_g, mem_norm_g, w_in, fox_f_bias, rwkv_mu, rwkv_w0, rwkv_w_up, rwkv_a0, rwkv_a_up, rwkv_g_up, rwkv_k_k, rwkv_k_a, rwkv_r_k, rwkv_gn_g, rwkv_gn_b, w_mem_kv, w_fox_out, w_rwkv_out, w_mem_out, w_o, w_ffn_gate, w_ffn_up, w_ffn_down, loss_target, m_pre1_g, m_post1_g, m_pre2_g, m_post2_g, m_mem_norm_g, m_w_in, m_fox_f_bias, m_rwkv_mu, m_rwkv_w0, m_rwkv_w_up, m_rwkv_a0, m_rwkv_a_up, m_rwkv_g_up, m_rwkv_k_k, m_rwkv_k_a, m_rwkv_r_k, m_rwkv_gn_g, m_rwkv_gn_b, m_w_mem_kv, m_w_fox_out, m_w_rwkv_out, m_w_mem_out, m_w_o, m_w_ffn_gate, m_w_ffn_up, m_w_ffn_down, v_pre1_g, v_post1_g, v_pre2_g, v_post2_g, v_mem_norm_g, v_w_in, v_fox_f_bias, v_rwkv_mu, v_rwkv_w0, v_rwkv_w_up, v_rwkv_a0, v_rwkv_a_up, v_rwkv_g_up, v_rwkv_k_k, v_rwkv_k_a, v_rwkv_r_k, v_rwkv_gn_g, v_rwkv_gn_b, v_w_mem_kv, v_w_fox_out, v_w_rwkv_out, v_w_mem_out, v_w_o, v_w_ffn_gate, v_w_ffn_up, v_w_ffn_down):
    args = dict(locals())
    wts = {n: args[n] for n in WEIGHT_ORDER}
    ms = {n: args["m_" + n] for n in WEIGHT_ORDER}
    vs = {n: args["v_" + n] for n in WEIGHT_ORDER}

    groups = {g: (members, join) for g, members, join in GROUPS}
    w_bf16 = {n: wts[n].astype(bf16) for n, _, _ in SHARDED}

    def send(g):
        return _group_local(w_bf16, *groups[g])

    first = _exchange("gather_first", [send(g) for g in FIRST_GROUPS], per_peer=False)
    full = _assemble(first, FIRST_GROUPS)
    small_in = {n: (wts[n] if n == "rwkv_r_k" else wts[n].reshape(wts[n].shape[-2:])) for n, _ in REPLICATED}
    late = ([send(g) for g in LATE_GROUPS[0]], [send(g) for g in LATE_GROUPS[1]],
            lambda got, which: _assemble(got, LATE_GROUPS[which]))
    loss_part, grad_x, gw, gp, early_got, last_got = _local_step(
        x, mem, loss_target, full, small_in, late=late, early=lambda g: _grad_blocks(g, EARLY_GRAD_GROUPS),
        last=lambda g: _grad_blocks(g, LAST_GRAD_GROUPS))
    (small_got,) = _exchange("exchange_small", [_small_pack(gp).astype(bf16)], per_peer=False)
    received = dict(zip(EARLY_GRAD_GROUPS + LAST_GRAD_GROUPS, list(early_got) + list(last_got)))

    outs = [{}, {}, {}, {}]
    for g, members, join in GROUPS:
        res = _adamw("adamw_" + g, received[g], *[_group_local(d, members, join) for d in (wts, ms, vs)])
        for o, arr in zip(outs, res):
            o.update(_group_split(arr, members, join))
    res = _adamw("adamw_small", small_got, *[_small_pack(d) for d in (wts, ms, vs)])
    for o, arr in zip(outs, res):
        o.update(_small_unpack(arr))
    loss = lax.psum(loss_part[0, 0], ("x", "y", "c"))
    return (loss, grad_x, *[o[n].reshape(wts[n].shape) for o in outs for n in WEIGHT_ORDER])
```

```python
import functools

import jax
import jax.numpy as jnp
from jax import lax
from jax.experimental import pallas as pl
from jax.experimental.pallas import tpu as pltpu

f32 = jnp.float32
bf16 = jnp.bfloat16
_HI = lax.Precision.HIGHEST

D = 1024
HEADS = 8
HD = 64
HW = HEADS * HD
MEM_HEADS = 4
MEM_HD = 128
MEM_W = 512
MEM_LEN = 256
D_FF = 2816
LORA_PAD = 128
RW_COLS = 3 * HW + 3 * LORA_PAD
NORM_EPS = 1e-6
GN_EPS = 64e-5
Q_BLOCK = 128
SCAN_CHUNK = 64
N_DEV = 8
LANES = 1024
VMEM_LIMIT = 56 * 1024 * 1024

ADAM_LR = 0.001
ADAM_B1 = 0.9
ADAM_B2 = 0.999
ADAM_EPS = 1e-08
ADAM_WD = 0.01
ADAM_STEP = 10

SHARDED = (
    ("w_in", (1024, 6920), 1),
    ("w_ffn_gate", (1024, 2816), 1),
    ("w_ffn_up", (1024, 2816), 1),
    ("w_ffn_down", (2816, 1024), 0),
    ("w_mem_kv", (1024, 1024), 0),
    ("w_o", (1024, 1024), 0),
    ("w_fox_out", (512, 1024), 1),
    ("w_rwkv_out", (512, 1024), 1),
    ("w_mem_out", (512, 1024), 1),
    ("rwkv_w_up", (64, 512), 1),
    ("rwkv_a_up", (64, 512), 1),
    ("rwkv_g_up", (128, 512), 1),
)
REPLICATED = (
    ("pre1_g", (1, 1024)), ("post1_g", (1, 1024)), ("pre2_g", (1, 1024)), ("post2_g", (1, 1024)),
    ("mem_norm_g", (1, 1024)), ("fox_f_bias", (1, 8)), ("rwkv_mu", (1, 1792)), ("rwkv_w0", (1, 512)),
    ("rwkv_a0", (1, 512)), ("rwkv_k_k", (1, 512)), ("rwkv_k_a", (1, 512)), ("rwkv_r_k", (1, 8, 64)),
    ("rwkv_gn_g", (1, 512)), ("rwkv_gn_b", (1, 512)),
)
WEIGHT_ORDER = ('pre1_g', 'post1_g', 'pre2_g', 'post2_g', 'mem_norm_g', 'w_in', 'fox_f_bias', 'rwkv_mu',
                'rwkv_w0', 'rwkv_w_up', 'rwkv_a0', 'rwkv_a_up', 'rwkv_g_up', 'rwkv_k_k', 'rwkv_k_a',
                'rwkv_r_k', 'rwkv_gn_g', 'rwkv_gn_b', 'w_mem_kv', 'w_fox_out', 'w_rwkv_out', 'w_mem_out',
                'w_o', 'w_ffn_gate', 'w_ffn_up', 'w_ffn_down')


def _block_shape(shape, axis):
    return tuple(s // N_DEV if i == axis else s for i, s in enumerate(shape))


def _rows_of(shape):
    n = 1
    for s in shape:
        n *= s
    return n // LANES


SHARD_ROWS = sum(_rows_of(_block_shape(s, a)) for _, s, a in SHARDED)
REPL_ELEMS = sum(_rows_of((LANES,) + s) for _, s in REPLICATED)
REPL_ROWS = -(-REPL_ELEMS // LANES)
PACK_ROWS = -(-(SHARD_ROWS + REPL_ROWS) // 128) * 128
GATHER_ROWS = -(-SHARD_ROWS // 16) * 16


def _cp(sem=None):
    return pltpu.CompilerParams(dimension_semantics=sem, vmem_limit_bytes=VMEM_LIMIT)


def _tile(dim, cap):
    best = None
    for t in range(128, min(dim, cap) + 1, 128):
        if dim % t == 0:
            best = t
    return best if best is not None else dim


def _two_terms(x):
    hi = x.astype(bf16)
    return hi, (x - hi.astype(f32)).astype(bf16)


def _dg(a, b, dims, exact):
    if exact == "split":
        (a_hi, a_lo), (b_hi, b_lo) = _two_terms(a), _two_terms(b)
        dot = functools.partial(lax.dot_general, dimension_numbers=dims, preferred_element_type=f32)
        return dot(a_hi, b_hi) + (dot(a_hi, b_lo) + dot(a_lo, b_hi))
    if exact:
        return lax.dot_general(a, b, dims, precision=_HI, preferred_element_type=f32)
    return lax.dot_general(a.astype(bf16), b.astype(bf16), dims, preferred_element_type=f32)


def _make_mm(batched, exact):
    o = 1 if batched else 0
    bd = ((0,), (0,)) if batched else ((), ())
    d_nn = (((1 + o,), (o,)), bd)
    d_nt = (((1 + o,), (1 + o,)), bd)
    d_tn = (((o,), (o,)), bd)

    @jax.custom_vjp
    def nn(a, b):
        return _dg(a, b, d_nn, exact)

    @jax.custom_vjp
    def nt(a, b):
        return _dg(a, b, d_nt, exact)

    @jax.custom_vjp
    def tn(a, b):
        return _dg(a, b, d_tn, exact)

    nn.defvjp(lambda a, b: (_dg(a, b, d_nn, exact), (a, b)),
              lambda res, g: (_dg(g, res[1], d_nt, exact), _dg(res[0], g, d_tn, exact)))
    nt.defvjp(lambda a, b: (_dg(a, b, d_nt, exact), (a, b)),
              lambda res, g: (_dg(g, res[1], d_nn, exact), _dg(g, res[0], d_tn, exact)))
    tn.defvjp(lambda a, b: (_dg(a, b, d_tn, exact), (a, b)),
              lambda res, g: (_dg(res[1], g, d_nt, exact), _dg(res[0], g, d_nn, exact)))
    return nn, nt, tn


def _sigmoid(x):
    return 1.0 / (1.0 + jnp.exp(-x))


def _head_sum_raw(x):
    width = 2 * HD
    i = lax.broadcasted_iota(jnp.int32, (width, width), 0) // HD
    j = lax.broadcasted_iota(jnp.int32, (width, width), 1) // HD
    m = (i == j).astype(bf16)
    dims = (((1,), (0,)), ((), ()))
    out = []
    for p in range(x.shape[1] // width):
        xp = x[:, p * width:(p + 1) * width]
        hi = xp.astype(bf16)
        lo = (xp - hi.astype(f32)).astype(bf16)
        out.append(lax.dot_general(hi, m, dims, preferred_element_type=f32)
                   + lax.dot_general(lo, m, dims, preferred_element_type=f32))
    return jnp.concatenate(out, axis=1)


@jax.custom_vjp
def _head_sum(x):
    return _head_sum_raw(x)


_head_sum.defvjp(lambda x: (_head_sum_raw(x), None), lambda _, g: (_head_sum_raw(g),))


WEIGHT_TILE_BYTES = 13 * 512 * 1024
ACC_TILE_BYTES = 8 * 1024 * 1024


def _matmul(name, a, b, mode, add=None, out_dtype=f32):
    has_add = add is not None
    if mode == "tn":
        (k, m), (_, n) = a.shape, b.shape
        tn = _tile(n, max(128, ACC_TILE_BYTES // (4 * m)))
        tk = _tile(k, 1024)

        nk = k // tk

        def body(a_ref, b_ref, o_ref, acc):
            @pl.when(pl.program_id(1) == 0)
            def _():
                acc[...] = jnp.zeros_like(acc)

            acc[...] += lax.dot_general(a_ref[...].astype(bf16), b_ref[...].astype(bf16),
                                        (((0,), (0,)), ((), ())), preferred_element_type=f32)

            @pl.when(pl.program_id(1) == nk - 1)
            def _():
                o_ref[...] = acc[...].astype(o_ref.dtype)

        return pl.pallas_call(
            body, name=name, grid=(n // tn, nk),
            in_specs=[pl.BlockSpec((tk, m), lambda j, kk: (kk, 0)), pl.BlockSpec((tk, tn), lambda j, kk: (kk, j))],
            out_specs=pl.BlockSpec((m, tn), lambda j, kk: (0, j)), out_shape=jax.ShapeDtypeStruct((m, n), out_dtype),
            scratch_shapes=[pltpu.VMEM((m, tn), f32)],
            compiler_params=_cp(("parallel", "arbitrary")),
        )(a, b)

    (m, k) = a.shape
    n = b.shape[1] if mode == "nn" else b.shape[0]
    tm = _tile(m, 512)
    tn = _tile(n, max(128, WEIGHT_TILE_BYTES // (2 * k)))
    dims = (((1,), (0,)), ((), ())) if mode == "nn" else (((1,), (1,)), ((), ()))
    b_spec = pl.BlockSpec((k, tn), lambda j, i: (0, j)) if mode == "nn" else pl.BlockSpec((tn, k), lambda j, i: (j, 0))
    o_spec = pl.BlockSpec((tm, tn), lambda j, i: (i, j))

    def body(*refs):
        a_ref, b_ref = refs[0], refs[1]
        o_ref = refs[-1]
        r = lax.dot_general(a_ref[...].astype(bf16), b_ref[...].astype(bf16), dims, preferred_element_type=f32)
        if has_add:
            r = r + refs[2][...]
        o_ref[...] = r.astype(o_ref.dtype)

    return pl.pallas_call(
        body, name=name, grid=(n // tn, m // tm),
        in_specs=[pl.BlockSpec((tm, k), lambda j, i: (i, 0)), b_spec] + ([o_spec] if has_add else []),
        out_specs=o_spec, out_shape=jax.ShapeDtypeStruct((m, n), out_dtype),
        compiler_params=_cp(("parallel", "arbitrary")),
    )(*((a, b, add) if has_add else (a, b)))


def _sum_nt(name, a_list, b_list, side=None):
    m, n = a_list[0].shape[0], b_list[0].shape[0]
    tm = _tile(m, 256)
    n_g = len(a_list)
    srcs, per_peer = side if side is not None else ([], False)
    n_s = len(srcs)

    def body(*refs):
        o_ref = refs[2 * n_g + n_s]
        _side_exchange(refs[2 * n_g:2 * n_g + n_s], refs[2 * n_g + n_s + 1:2 * n_g + 2 * n_s + 1], per_peer,
                       refs[2 * n_g + 2 * n_s + 1:], m // tm)
        acc = None
        for g in range(n_g):
            r = lax.dot_general(refs[g][...].astype(bf16), refs[n_g + g][...].astype(bf16), (((1,), (1,)), ((), ())),
                                preferred_element_type=f32)
            acc = r if acc is None else acc + r
        o_ref[...] = acc

    res = pl.pallas_call(
        body, name=name, grid=(m // tm,),
        in_specs=[pl.BlockSpec((tm, a.shape[1]), lambda i: (i, 0)) for a in a_list]
        + [pl.BlockSpec(b.shape, lambda i: (0, 0)) for b in b_list] + [_HBM_SPEC] * n_s,
        out_specs=[pl.BlockSpec((tm, n), lambda i: (i, 0))] + [_HBM_SPEC] * n_s,
        out_shape=[jax.ShapeDtypeStruct((m, n), f32)] + _side_out_shapes(srcs, per_peer),
        scratch_shapes=_side_sems(n_s),
        compiler_params=_cp(("arbitrary",)),
    )(*a_list, *b_list, *srcs)
    return res[0], list(res[1:])


def _pieces(ref, widths):
    out, off = [], 0
    for w in widths:
        out.append(ref[:, off:off + w].astype(f32))
        off += w
    return out


def _store_pieces(ref, widths, vals, add_ref=None):
    off = 0
    for w, v in zip(widths, vals):
        ref[:, off:off + w] = (v if add_ref is None else v + add_ref[:, off:off + w]).astype(ref.dtype)
        off += w


def _rows_fwd(name, fn, consts, rows, params, outs, n_sums=0, tm=512, dtypes=None):
    t = (consts + rows)[0][0].shape[0]
    tm = min(tm, t)
    ins = consts + rows
    n_in, n_p, n_o = len(ins), len(params), len(outs)
    dtypes = dtypes or [f32] * n_o

    def body(*refs):
        in_refs, p_refs = refs[:n_in], refs[n_in:n_in + n_p]
        o_refs, s_refs = refs[n_in + n_p:n_in + n_p + n_o], refs[n_in + n_p + n_o:]
        vals = []
        for r, (_, widths) in zip(in_refs, ins):
            vals += _pieces(r, widths)
        res = fn(*vals, *[p[...] for p in p_refs])
        pos = 0
        for r, widths in zip(o_refs, outs):
            _store_pieces(r, widths, res[pos:pos + len(widths)])
            pos += len(widths)

        @pl.when(pl.program_id(0) == 0)
        def _():
            for s in s_refs:
                s[...] = jnp.zeros_like(s)

        for s, v in zip(s_refs, res[pos:]):
            s[...] += v

    row_spec = lambda w: pl.BlockSpec((tm, w), lambda i: (i, 0))
    full = lambda p: pl.BlockSpec(p.shape, lambda i: (0,) * p.ndim)
    return pl.pallas_call(
        body, name=name, grid=(t // tm,),
        in_specs=[row_spec(a.shape[1]) for a, _ in ins] + [full(p) for p in params],
        out_specs=[row_spec(sum(w)) for w in outs] + [pl.BlockSpec((1, 1), lambda i: (0, 0))] * n_sums,
        out_shape=[jax.ShapeDtypeStruct((t, sum(w)), dt) for w, dt in zip(outs, dtypes)] + [jax.ShapeDtypeStruct((1, 1), f32)] * n_sums,
        compiler_params=_cp(("arbitrary",)),
    )(*[a for a, _ in ins], *params)


def _rows_bwd(name, fn, consts, rows, params, outs, cts, n_sums=0, add=None, tm=512, dtypes=None):
    t = (consts + rows)[0][0].shape[0]
    tm = min(tm, t)
    n_c, n_r, n_p, n_o = len(consts), len(rows), len(params), len(outs)
    has_add = add is not None
    dtypes = dtypes or [f32] * n_r

    def body(*refs):
        pos = 0
        c_refs = refs[pos:pos + n_c]; pos += n_c
        r_refs = refs[pos:pos + n_r]; pos += n_r
        p_refs = refs[pos:pos + n_p]; pos += n_p
        ct_refs = refs[pos:pos + n_o]; pos += n_o
        add_ref = refs[pos] if has_add else None
        pos += 1 if has_add else 0
        dr_refs = refs[pos:pos + n_r]; pos += n_r
        dp_refs = refs[pos:pos + n_p]
        cvals, rvals = [], []
        for r, (_, widths) in zip(c_refs, consts):
            cvals += _pieces(r, widths)
        for r, (_, widths) in zip(r_refs, rows):
            rvals += _pieces(r, widths)
        pvals = [p[...] for p in p_refs]
        ctv = []
        for r, widths in zip(ct_refs, outs):
            ctv += _pieces(r, widths)
        ctv += [jnp.ones((1, 1), f32)] * n_sums
        _, vjp = jax.vjp(lambda *rp: tuple(fn(*cvals, *rp)), *rvals, *pvals)
        g = vjp(tuple(ctv))
        pos = 0
        for idx, (r, (_, widths)) in enumerate(zip(dr_refs, rows)):
            _store_pieces(r, widths, g[pos:pos + len(widths)], add_ref if idx == 0 else None)
            pos += len(widths)

        @pl.when(pl.program_id(0) == 0)
        def _():
            for dp in dp_refs:
                dp[...] = jnp.zeros_like(dp)

        for dp, v in zip(dp_refs, g[pos:]):
            dp[...] += v

    row_spec = lambda w: pl.BlockSpec((tm, w), lambda i: (i, 0))
    full = lambda p: pl.BlockSpec(p.shape, lambda i: (0,) * p.ndim)
    args = [a for a, _ in consts + rows] + list(params) + list(cts) + ([add] if has_add else [])
    res = pl.pallas_call(
        body, name=name, grid=(t // tm,),
        in_specs=[row_spec(a.shape[1]) for a, _ in consts + rows] + [full(p) for p in params]
        + [row_spec(sum(w)) for w in outs] + ([row_spec(add.shape[1])] if has_add else []),
        out_specs=[row_spec(a.shape[1]) for a, _ in rows] + [full(p) for p in params],
        out_shape=[jax.ShapeDtypeStruct(a.shape, dt) for (a, _), dt in zip(rows, dtypes)]
        + [jax.ShapeDtypeStruct(p.shape, f32) for p in params],
        compiler_params=_cp(("arbitrary",)),
    )(*args)
    return res[:n_r], res[n_r:]


def _rms(x, g):
    return x * lax.rsqrt(jnp.mean(x * x, axis=-1, keepdims=True) + NORM_EPS) * g


def _fn_rms(x, g):
    return (_rms(x, g),)


def _fn_rwkv_pre(r, k, v, wd, ad, gd, w0, w_up, a0, a_up, g_up, k_k, k_a):
    nn, _, _ = _make_mm(False, False)
    w_log = -_sigmoid(w0 + nn(jnp.tanh(wd), w_up)) * 0.6065306597126334
    a = _sigmoid(a0 + nn(ad, a_up))
    g = nn(_sigmoid(gd), g_up)
    kk = k * k_k
    kk = kk * lax.rsqrt(jnp.maximum(_head_sum(kk * kk), 1e-24))
    k2 = k * (1.0 + (a - 1.0) * k_a)
    return r, w_log, k2, v, -kk, kk * a, g


def _fn_rwkv_post(y, r, k2, v, g, gn_g, gn_b, r_k):
    mean = _head_sum(y) * (1.0 / HD)
    yc = y - mean
    var = _head_sum(yc * yc) * (1.0 / HD)
    yn = yc * lax.rsqrt(var + GN_EPS) * gn_g + gn_b
    bonus = _head_sum(r * k2 * r_k) * v
    return ((yn + bonus) * g,)


def _fn_merge(a_fox, a_rwkv, a_mem, g_fox, g_rwkv, g_mem):
    return (_sigmoid(g_fox) * a_fox + _sigmoid(g_rwkv) * a_rwkv + _sigmoid(g_mem) * a_mem,)


def _fn_post1(y, x, post1_g, pre2_g):
    h1 = x + _rms(y, post1_g)
    return h1, _rms(h1, pre2_g)


def _fn_swiglu(gp, up):
    return (gp * _sigmoid(gp) * up,)


def _fn_final(target, ffn, h1, post2_g):
    err = h1 + _rms(ffn, post2_g) - target
    per_row = jnp.mean(err * err, axis=-1, keepdims=True)
    return (0.5 * jnp.sum(per_row, axis=0, keepdims=True),)


def _shift_down(x):
    row = lax.broadcasted_iota(jnp.int32, x.shape, 0)
    return jnp.where(row == 0, 0.0, pltpu.roll(x, 1, 0))


def _shift_up(x):
    s = x.shape[0]
    row = lax.broadcasted_iota(jnp.int32, x.shape, 0)
    return jnp.where(row == s - 1, 0.0, pltpu.roll(x, s - 1, 0))


def _tokshift_fwd(p, mu, batch, seq):
    w = p.shape[1]
    tc = _tile(w, 384)

    def body(p_ref, mu_ref, o_ref):
        x = p_ref[...]
        o_ref[...] = x + (_shift_down(x) - x) * mu_ref[...]

    return pl.pallas_call(
        body, name="tokshift_fwd", grid=(w // tc, batch),
        in_specs=[pl.BlockSpec((seq, tc), lambda j, b: (b, j)), pl.BlockSpec((1, tc), lambda j, b: (0, j))],
        out_specs=pl.BlockSpec((seq, tc), lambda j, b: (b, j)),
        out_shape=jax.ShapeDtypeStruct(p.shape, f32),
        compiler_params=_cp(("parallel", "arbitrary")),
    )(p, mu)


def _tokshift_bwd(p, mu, dps, batch, seq):
    w = p.shape[1]
    tc = _tile(w, 384)

    def body(p_ref, mu_ref, d_ref, dp_ref, dmu_ref):
        x, mu_v, d = p_ref[...], mu_ref[...], d_ref[...]
        dp_ref[...] = (d * (1.0 - mu_v) + _shift_up(d * mu_v)).astype(dp_ref.dtype)

        @pl.when(pl.program_id(1) == 0)
        def _():
            dmu_ref[...] = jnp.zeros_like(dmu_ref)

        dmu_ref[...] += jnp.sum(d * (_shift_down(x) - x), axis=0, keepdims=True)

    return pl.pallas_call(
        body, name="tokshift_bwd", grid=(w // tc, batch),
        in_specs=[pl.BlockSpec((seq, tc), lambda j, b: (b, j)), pl.BlockSpec((1, tc), lambda j, b: (0, j)),
                  pl.BlockSpec((seq, tc), lambda j, b: (b, j))],
        out_specs=[pl.BlockSpec((seq, tc), lambda j, b: (b, j)), pl.BlockSpec((1, tc), lambda j, b: (0, j))],
        out_shape=[jax.ShapeDtypeStruct(p.shape, bf16), jax.ShapeDtypeStruct(mu.shape, f32)],
        compiler_params=_cp(("parallel", "arbitrary")),
    )(p, mu, dps)


def _cum_block(seq):
    return _tile(seq, 256)


def _fox_gate_fwd(f, bias, batch, seq):
    cb = _cum_block(seq)

    def body(f_ref, b_ref, c_ref):
        row = lax.broadcasted_iota(jnp.int32, (cb, cb), 0)
        col = lax.broadcasted_iota(jnp.int32, (cb, cb), 1)
        tri = (col <= row).astype(f32)
        carry = jnp.zeros((1, 128), f32)
        for i in range(seq // cb):
            z = f_ref[i * cb:(i + 1) * cb, :] + b_ref[...]
            ls = jnp.minimum(z, 0.0) - jnp.log(1.0 + jnp.exp(-jnp.abs(z)))
            c = _dg(tri, ls, (((1,), (0,)), ((), ())), True) + carry
            c_ref[i * cb:(i + 1) * cb, :] = c
            carry = c[cb - 1:cb, :]

    return pl.pallas_call(
        body, name="fox_gate_fwd", grid=(batch,),
        in_specs=[pl.BlockSpec((seq, 128), lambda b: (b, 0)), pl.BlockSpec((1, 128), lambda b: (0, 0))],
        out_specs=pl.BlockSpec((seq, 128), lambda b: (b, 0)),
        out_shape=jax.ShapeDtypeStruct(f.shape, f32),
        compiler_params=_cp(("arbitrary",)),
    )(f, bias)


def _fox_gate_bwd(f, bias, dc_a, dc_b, batch, seq):
    cb = _cum_block(seq)

    def body(f_ref, b_ref, da_ref, db_ref, df_ref, dbias_ref):
        row = lax.broadcasted_iota(jnp.int32, (cb, cb), 0)
        col = lax.broadcasted_iota(jnp.int32, (cb, cb), 1)
        triu = (col >= row).astype(f32)

        @pl.when(pl.program_id(0) == 0)
        def _():
            dbias_ref[...] = jnp.zeros_like(dbias_ref)

        lane = lax.broadcasted_iota(jnp.int32, (1, 128), 1)

        def by_head(blk):
            out = jnp.zeros((cb, 128), f32)
            for p in range(HEADS // 2):
                for e in range(2):
                    out = jnp.where(lane == 2 * p + e, _pick_lane(blk[:, p * 128:(p + 1) * 128], e), out)
            return out

        carry = jnp.zeros((1, 128), f32)
        tot = jnp.zeros((1, 128), f32)
        for i in reversed(range(seq // cb)):
            sl = slice(i * cb, (i + 1) * cb)
            dc = by_head(da_ref[sl, :] + db_ref[sl, :])
            dls = _dg(triu, dc, (((1,), (0,)), ((), ())), True) + carry
            carry = dls[0:1, :]
            df = dls * _sigmoid(-(f_ref[sl, :] + b_ref[...]))
            df_ref[sl, :] = df.astype(df_ref.dtype)
            tot = tot + jnp.sum(df, axis=0, keepdims=True)
        dbias_ref[...] += tot

    return pl.pallas_call(
        body, name="fox_gate_bwd", grid=(batch,),
        in_specs=[pl.BlockSpec((seq, 128), lambda b: (b, 0)), pl.BlockSpec((1, 128), lambda b: (0, 0)),
                  pl.BlockSpec((seq, HW), lambda b: (b, 0)), pl.BlockSpec((seq, HW), lambda b: (b, 0))],
        out_specs=[pl.BlockSpec((seq, 128), lambda b: (b, 0)), pl.BlockSpec((1, 128), lambda b: (0, 0))],
        out_shape=[jax.ShapeDtypeStruct(f.shape, bf16), jax.ShapeDtypeStruct((1, 128), f32)],
        compiler_params=_cp(("arbitrary",)),
    )(f, bias, dc_a, dc_b)


_HBM_SPEC = pl.BlockSpec(memory_space=pltpu.HBM)


def _side_out_shapes(srcs, per_peer):
    return [jax.ShapeDtypeStruct(((N_DEV,) + tuple(s.shape[1:] if per_peer else s.shape)), s.dtype) for s in srcs]


def _side_sems(n):
    if n == 0:
        return []
    return [pltpu.SemaphoreType.DMA((n, N_DEV - 1)), pltpu.SemaphoreType.DMA((n, N_DEV - 1)), pltpu.SemaphoreType.DMA((n,))]


def _peer_copies(src_refs, dst_refs, per_peer, sems):
    send_sems, recv_sems, local_sems = sems
    x, y, c = lax.axis_index("x"), lax.axis_index("y"), lax.axis_index("c")
    me = 4 * x + 2 * y + c

    def remote(src, dst, t, k, to):
        return pltpu.make_async_remote_copy(src_ref=src, dst_ref=dst, send_sem=send_sems.at[t, k - 1],
                                            recv_sem=recv_sems.at[t, k - 1], device_id=to,
                                            device_id_type=pl.DeviceIdType.MESH)

    direct, relays = [], []
    for t, (s, d) in enumerate(zip(src_refs, dst_refs)):
        direct.append((t, 0, pltpu.make_async_copy(s.at[me] if per_peer else s, d.at[me], local_sems.at[t])))
        for k in range(1, N_DEV):
            px = 1 - x if k & 4 else x
            py = 1 - y if k & 2 else y
            pc = 1 - c if k & 1 else c
            if per_peer:
                direct.append((t, k, remote(s.at[4 * px + 2 * py + pc], d.at[me], t, k, (px, py, pc))))
            elif k == 1 or not k & 1:
                direct.append((t, k, remote(s, d.at[me], t, k, (px, py, pc))))
            else:
                origin = d.at[4 * px + 2 * py + c]
                relays.append((t, k - 1, remote(origin, origin, t, k, (x, y, 1 - c))))
    return direct, relays


def _exchange_start(direct):
    for _, _, cp in direct:
        cp.start()


def _exchange_finish(direct, relays):
    landed = {(t, k): cp for t, k, cp in direct}
    for t, j, cp in relays:
        landed[(t, j)].wait_recv()
        cp.start()
    relayed = {(t, j) for t, j, _ in relays}
    for t, k, cp in direct:
        if k == 0:
            cp.wait()
        else:
            cp.wait_send()
            if (t, k) not in relayed:
                cp.wait_recv()
    for _, _, cp in relays:
        cp.wait()


def _side_exchange(src_refs, dst_refs, per_peer, sems, *grid):
    if not src_refs:
        return
    first = functools.reduce(jnp.logical_and, [pl.program_id(a) == 0 for a in range(len(grid))])
    last = functools.reduce(jnp.logical_and, [pl.program_id(a) == n - 1 for a, n in enumerate(grid)])

    @pl.when(first)
    def _():
        _exchange_start(_peer_copies(src_refs, dst_refs, per_peer, sems)[0])

    @pl.when(last)
    def _():
        _exchange_finish(*_peer_copies(src_refs, dst_refs, per_peer, sems))


def _exchange(name, srcs, per_peer):
    n = len(srcs)

    def body(*refs):
        direct, relays = _peer_copies(refs[:n], refs[n:2 * n], per_peer, refs[2 * n:])
        _exchange_start(direct)
        _exchange_finish(direct, relays)

    return pl.pallas_call(
        body, name=name, in_specs=[_HBM_SPEC] * n, out_specs=[_HBM_SPEC] * n,
        out_shape=_side_out_shapes(srcs, per_peer), scratch_shapes=_side_sems(n),
    )(*srcs)


FOX_T = 512
_NEG = -1e30
_D2 = (((1,), (1,)), ((), ()))
_D1 = (((1,), (0,)), ((), ()))
_D0 = (((0,), (0,)), ((), ()))


def _bdot(a, b, dims):
    return lax.dot_general(a.astype(bf16), b.astype(bf16), dims, preferred_element_type=f32)


def _pick_lane(x, lane):
    idx = lax.broadcasted_iota(jnp.int32, x.shape, 1)
    return jnp.sum(jnp.where(idx == lane, x, 0.0), axis=1, keepdims=True)


def _pick_row(x, row):
    idx = lax.broadcasted_iota(jnp.int32, x.shape, 0)
    return jnp.sum(jnp.where(idx == row, x, 0.0), axis=0, keepdims=True)


def _fox_fwd(qkv, c, c_rows, batch, seq, side=None):
    t = min(FOX_T, seq)
    nq = seq // t
    scale = HD ** -0.5
    srcs, per_peer = side if side is not None else ([], False)
    n_s = len(srcs)

    def body(*refs):
        q_ref, k_ref, v_ref, cq_ref, ck_ref = refs[:5]
        o_ref, lse_ref = refs[5 + n_s:7 + n_s]
        _side_exchange(refs[5:5 + n_s], refs[7 + n_s:7 + 2 * n_s], per_peer, refs[7 + 2 * n_s:], batch, PAIRS, nq)
        pair, i = pl.program_id(1), pl.program_id(2)
        lane = lax.broadcasted_iota(jnp.int32, (1, PAIR_W), 1)
        first = (lane // HD) == 0
        mine = [first, jnp.logical_not(first)]
        q = q_ref[...] * scale
        qs = [jnp.where(mine[e], q, 0.0) for e in range(2)]
        cqs = [_pick_lane(cq_ref[...], 2 * pair + e) for e in range(2)]
        causal = lax.broadcasted_iota(jnp.int32, (t, t), 1) <= lax.broadcasted_iota(jnp.int32, (t, t), 0)

        def block(j, carry, diagonal):
            rows = pl.ds(pl.multiple_of(j * t, t), t)
            kj, vj = k_ref[rows, :], v_ref[rows, :]
            ck_blk = ck_ref[0, :, rows]
            out = []
            for e in range(2):
                m, acc = carry[2 * e:2 * e + 2]
                s = _bdot(qs[e], kj, _D2) + cqs[e] - _pick_row(ck_blk, 2 * pair + e)
                if diagonal:
                    s = jnp.where(causal, s, _NEG)
                m_new = jnp.maximum(m, jnp.max(s, axis=1, keepdims=True))
                p = jnp.exp(s - m_new)
                out += [m_new, jnp.exp(m - m_new) * acc + _bdot(p, jnp.where(mine[e], vj, 1.0), _D1)]
            return tuple(out)

        init = (jnp.full((t, 1), _NEG, f32), jnp.zeros((t, PAIR_W), f32)) * 2
        carry = lax.fori_loop(0, i, lambda j, cr: block(j, cr, False), init)
        m0, a0, m1, a1 = block(i, carry, True)
        l0, l1 = _pick_lane(a0, HD), _pick_lane(a1, 0)
        o_ref[...] = jnp.where(first, a0 / l0, a1 / l1)
        lse_ref[...] = jnp.where(lane == 0, m0 + jnp.log(l0), jnp.where(lane == 1, m1 + jnp.log(l1), 0.0))

    q_spec = pl.BlockSpec((t, PAIR_W), lambda b, p, i: (b * nq + i, p))
    res = pl.pallas_call(
        body, name="fox_attn_fwd", grid=(batch, PAIRS, nq),
        in_specs=[q_spec,
                  pl.BlockSpec((seq, PAIR_W), lambda b, p, i: (b, PAIRS + p)),
                  pl.BlockSpec((seq, PAIR_W), lambda b, p, i: (b, 2 * PAIRS + p)),
                  pl.BlockSpec((t, 128), lambda b, p, i: (b * nq + i, 0)),
                  pl.BlockSpec((1, 8, seq), lambda b, p, i: (b, 0, 0))] + [_HBM_SPEC] * n_s,
        out_specs=[q_spec, q_spec] + [_HBM_SPEC] * n_s,
        out_shape=[jax.ShapeDtypeStruct((batch * seq, HW), f32)] * 2 + _side_out_shapes(srcs, per_peer),
        scratch_shapes=_side_sems(n_s),
        compiler_params=_cp(("arbitrary", "arbitrary", "arbitrary")),
    )(qkv, qkv, qkv, c, c_rows, *srcs)
    return res[0], res[1], list(res[2:])


def _fox_bwd(qkv, c, c_rows, o, lse, do, batch, seq):
    t = min(FOX_T, seq)
    nq = seq // t
    scale = HD ** -0.5

    def body(q_ref, k_ref, v_ref, cq_ref, ck_ref, o_ref, lse_ref, do_ref,
             dq_ref, dk_ref, dv_ref, dcq_ref, dck_ref, acc0, acc1):
        pair, i = pl.program_id(1), pl.program_id(2)
        accs = [acc0, acc1]

        @pl.when(i == 0)
        def _():
            dv_ref[...] = jnp.zeros_like(dv_ref)
            acc0[...] = jnp.zeros_like(acc0)
            acc1[...] = jnp.zeros_like(acc1)

        lane = lax.broadcasted_iota(jnp.int32, (1, PAIR_W), 1)
        first = (lane // HD) == 0
        mine = [first, jnp.logical_not(first)]
        q, d_o, o_i = q_ref[...] * scale, do_ref[...], o_ref[...]
        q0s = [jnp.where(mine[e], q, 0.0) for e in range(2)]
        q1s = [jnp.where(mine[e], q, 1.0) for e in range(2)]
        dos = [jnp.where(mine[e], d_o, 0.0) for e in range(2)]
        deltas = [jnp.sum(dos[e] * o_i, axis=1, keepdims=True) for e in range(2)]
        lses = [_pick_lane(lse_ref[...], e) for e in range(2)]
        cqs = [_pick_lane(cq_ref[...], 2 * pair + e) for e in range(2)]
        causal = lax.broadcasted_iota(jnp.int32, (t, t), 1) <= lax.broadcasted_iota(jnp.int32, (t, t), 0)

        def block(j, dqs, diagonal):
            rows = pl.ds(pl.multiple_of(j * t, t), t)
            kj, vj = k_ref[rows, :], v_ref[rows, :]
            ck_blk = ck_ref[0, :, rows]
            out = []
            for e in range(2):
                s = _bdot(q0s[e], kj, _D2) + cqs[e] - _pick_row(ck_blk, 2 * pair + e)
                if diagonal:
                    s = jnp.where(causal, s, _NEG)
                p = jnp.exp(s - lses[e])
                ds = p * (_bdot(dos[e], vj, _D2) - deltas[e])
                dv_ref[rows, :] += _bdot(p, dos[e], _D0)
                accs[e][rows, :] += _bdot(ds, q1s[e], _D0)
                out.append(dqs[e] + _bdot(ds, jnp.where(mine[e], kj, 1.0), _D1))
            return tuple(out)

        zero = jnp.zeros((t, PAIR_W), f32)
        dqs = lax.fori_loop(0, i, lambda j, cr: block(j, cr, False), (zero, zero))
        dq0, dq1 = block(i, dqs, True)
        dq_ref[...] = jnp.where(first, dq0, dq1) * scale
        dcq_ref[...] = jnp.where(lane == 0, _pick_lane(dq0, HD), jnp.where(lane == 1, _pick_lane(dq1, 0), 0.0))

        @pl.when(i == nq - 1)
        def _():
            a0, a1 = acc0[...], acc1[...]
            dk_ref[...] = jnp.where(first, a0, a1)
            dck_ref[...] = jnp.where(lane == 0, -_pick_lane(a0, HD), jnp.where(lane == 1, -_pick_lane(a1, 0), 0.0))

    blk = lambda col: pl.BlockSpec((t, PAIR_W), lambda b, p, i: (b * nq + i, col * PAIRS + p))
    whole = lambda col: pl.BlockSpec((seq, PAIR_W), lambda b, p, i: (b, col * PAIRS + p))
    t_all = batch * seq
    return pl.pallas_call(
        body, name="fox_attn_bwd", grid=(batch, PAIRS, nq),
        in_specs=[blk(0), whole(1), whole(2),
                  pl.BlockSpec((t, 128), lambda b, p, i: (b * nq + i, 0)),
                  pl.BlockSpec((1, 8, seq), lambda b, p, i: (b, 0, 0)),
                  blk(0), blk(0), blk(0)],
        out_specs=[blk(0), whole(0), whole(0), blk(0), whole(0)],
        out_shape=[jax.ShapeDtypeStruct((t_all, HW), f32)] * 5,
        scratch_shapes=[pltpu.VMEM((seq, PAIR_W), f32), pltpu.VMEM((seq, PAIR_W), f32)],
        compiler_params=_cp(("parallel", "parallel", "arbitrary")),
    )(qkv, qkv, qkv, c, c_rows, o, lse, do)


def _mem_block(q, km, vm):
    nn, nt, _ = _make_mm(False, False)
    logits = nt(q, km) * (MEM_HD ** -0.5)
    m = lax.stop_gradient(jnp.max(logits, axis=-1, keepdims=True))
    e = jnp.exp(logits - m)
    return nn(e / jnp.sum(e, axis=-1, keepdims=True), vm)


def _mem_specs(seq, tq):
    nq = seq // tq
    qs = pl.BlockSpec((tq, MEM_HD), lambda b, h, i: (b * nq + i, h))
    ks = pl.BlockSpec((MEM_LEN, MEM_HD), lambda b, h, i: (b, h))
    vs = pl.BlockSpec((MEM_LEN, MEM_HD), lambda b, h, i: (b, MEM_HEADS + h))
    return nq, qs, ks, vs


def _mem_fwd(q, mem_kv, batch, seq):
    tq = min(512, seq)
    nq, qs, ks, vs = _mem_specs(seq, tq)

    def body(q_ref, k_ref, v_ref, o_ref):
        o_ref[...] = _mem_block(q_ref[...].astype(f32), k_ref[...], v_ref[...]).astype(o_ref.dtype)

    return pl.pallas_call(
        body, name="mem_attn_fwd", grid=(batch, MEM_HEADS, nq),
        in_specs=[qs, ks, vs], out_specs=qs, out_shape=jax.ShapeDtypeStruct(q.shape, bf16),
        compiler_params=_cp(("parallel", "parallel", "arbitrary")),
    )(q, mem_kv, mem_kv)


def _mem_bwd(q, mem_kv, do, batch, seq):
    tq = min(512, seq)
    nq, qs, ks, vs = _mem_specs(seq, tq)

    def body(q_ref, k_ref, v_ref, do_ref, dq_ref, dk_ref, dv_ref):
        _, vjp = jax.vjp(_mem_block, q_ref[...].astype(f32), k_ref[...], v_ref[...])
        dq, dk, dv = vjp(do_ref[...])
        dq_ref[...] = dq.astype(dq_ref.dtype)

        @pl.when(pl.program_id(2) == 0)
        def _():
            dk_ref[...] = jnp.zeros_like(dk_ref)
            dv_ref[...] = jnp.zeros_like(dv_ref)

        dk_ref[...] += dk
        dv_ref[...] += dv

    return pl.pallas_call(
        body, name="mem_attn_bwd", grid=(batch, MEM_HEADS, nq),
        in_specs=[qs, ks, vs, qs], out_specs=[qs, ks, ks],
        out_shape=[jax.ShapeDtypeStruct(q.shape, bf16), jax.ShapeDtypeStruct((batch * MEM_LEN, MEM_W), f32),
                   jax.ShapeDtypeStruct((batch * MEM_LEN, MEM_W), f32)],
        compiler_params=_cp(("parallel", "parallel", "arbitrary")),
    )(q, mem_kv, mem_kv, do)


@jax.custom_vjp
def _halves(x):
    c = x.shape[1] // 2
    return x[:, :c], x[:, c:]


_halves.defvjp(lambda x: ((x[:, :x.shape[1] // 2], x[:, x.shape[1] // 2:]), None),
               lambda _, g: (jnp.concatenate(g, axis=1),))


@jax.custom_vjp
def _lead_halves(x):
    n = x.shape[0] // 2
    return x[:n], x[n:]


_lead_halves.defvjp(lambda x: ((x[:x.shape[0] // 2], x[x.shape[0] // 2:]), None),
                    lambda _, g: (jnp.concatenate(g, axis=0),))


def _scan_chunk(s0, r, wl, k, v, a, b):
    nn, nt, tn = _make_mm(True, False)
    nn_exact, _, _ = _make_mm(True, True)
    _, nt_exact, _ = _make_mm(True, "split")
    hp, c, lanes = r.shape
    row = lax.broadcasted_iota(jnp.int32, (c, c), 0)
    col = lax.broadcasted_iota(jnp.int32, (c, c), 1)
    first = (lax.broadcasted_iota(jnp.int32, (1, 1, lanes), 2) // HD) == 0
    tri = jnp.broadcast_to((col <= row).astype(f32)[None], (hp, c, c))
    lg = nn_exact(tri, wl)
    lg_end = lg[:, c - 1:c, :]
    grow, shrink, to_end = jnp.exp(lg), jnp.exp(-lg), jnp.exp(lg_end - lg)
    rt, kt, bt, at = r * grow, k * shrink, b * shrink, a * jnp.exp(lg - wl)
    strict, incl = (col < row)[None], (col <= row)[None]
    twice = lambda t: jnp.concatenate([t, t], axis=0)
    queries = jnp.concatenate([at, rt], axis=1)
    per_head = jnp.concatenate([jnp.where(first, queries, 0.0), jnp.where(first, 0.0, queries)], axis=0)
    (ab, rb), (ak, rk) = _halves(nt_exact(per_head, twice(bt))), _halves(nt_exact(per_head, twice(kt)))
    l_ab = jnp.where(strict, ab, 0.0)
    a_ak = jnp.where(strict, ak, 0.0)
    a_rb = jnp.where(incl, rb, 0.0)
    a_rk = jnp.where(incl, rk, 0.0)
    inv = (col == row).astype(f32)[None] + l_ab
    power, n = l_ab, 1
    while 2 * n < c:
        power = nn(power, power)
        inv = inv + nn(inv, power)
        n *= 2

    def apply(m, t):
        lo, hi = _lead_halves(nn(m, twice(t)))
        return jnp.where(first, lo, hi)

    sa = apply(inv, nt(at, s0) + apply(a_ak, v))
    y = nt(rt, s0) + apply(a_rk, v) + apply(a_rb, sa)
    same_head = ((lax.broadcasted_iota(jnp.int32, (lanes, lanes), 0) // HD)
                 == (lax.broadcasted_iota(jnp.int32, (lanes, lanes), 1) // HD))[None]
    s1 = s0 * jnp.exp(lg_end) + jnp.where(same_head, tn(v, k * to_end) + tn(sa, b * to_end), 0.0)
    return y, s1


PAIRS = HEADS // 2
PAIR_W = 2 * HD


def _pair_stack(ref, off):
    return jnp.stack([ref[b, :, off + p * PAIR_W:off + (p + 1) * PAIR_W]
                      for b in range(ref.shape[0]) for p in range(PAIRS)])


def _pair_store(ref, off, val, add_ref=None):
    for b in range(ref.shape[0]):
        for p in range(PAIRS):
            sl = slice(off + p * PAIR_W, off + (p + 1) * PAIR_W)
            v = val[b * PAIRS + p]
            ref[b, :, sl] = v if add_ref is None else v + add_ref[b, :, sl]


def _scan_fwd(main6, batch, seq, side=None):
    c = min(SCAN_CHUNK, seq)
    nc = seq // c
    hp = batch * PAIRS
    srcs, per_peer = side if side is not None else ([], False)
    n_s = len(srcs)

    def body(*refs):
        z_ref, y_ref, s_ref, st = refs[0], refs[1 + n_s], refs[2 + n_s], refs[3 + 2 * n_s]
        _side_exchange(refs[1:1 + n_s], refs[3 + n_s:3 + 2 * n_s], per_peer, refs[4 + 2 * n_s:], nc)

        @pl.when(pl.program_id(0) == 0)
        def _():
            st[...] = jnp.zeros_like(st)

        s0 = st[...]
        s_ref[0] = s0
        y, s1 = _scan_chunk(s0, *[_pair_stack(z_ref, comp * HW) for comp in range(6)])
        _pair_store(y_ref, 0, y)
        st[...] = s1

    res = pl.pallas_call(
        body, name="rwkv_scan_fwd", grid=(nc,),
        in_specs=[pl.BlockSpec((batch, c, 6 * HW), lambda i: (0, i, 0))] + [_HBM_SPEC] * n_s,
        out_specs=[pl.BlockSpec((batch, c, HW), lambda i: (0, i, 0)),
                   pl.BlockSpec((1, hp, PAIR_W, PAIR_W), lambda i: (i, 0, 0, 0))] + [_HBM_SPEC] * n_s,
        out_shape=[jax.ShapeDtypeStruct((batch, seq, HW), f32), jax.ShapeDtypeStruct((nc, hp, PAIR_W, PAIR_W), f32)]
        + _side_out_shapes(srcs, per_peer),
        scratch_shapes=[pltpu.VMEM((hp, PAIR_W, PAIR_W), f32)] + _side_sems(n_s),
        compiler_params=_cp(("arbitrary",)),
    )(main6.reshape(batch, seq, 6 * HW), *srcs)
    return res[0].reshape(batch * seq, HW), res[1], list(res[2:])


def _scan_bwd(main6, states, dy, extra, batch, seq, side=None):
    c = min(SCAN_CHUNK, seq)
    nc = seq // c
    hp = batch * PAIRS
    srcs, per_peer = side if side is not None else ([], False)
    n_s = len(srcs)

    def body(*refs):
        z_ref, s_ref, dy_ref, ex_ref = refs[:4]
        dz_ref, dst = refs[4 + n_s], refs[5 + 2 * n_s]
        _side_exchange(refs[4:4 + n_s], refs[5 + n_s:5 + 2 * n_s], per_peer, refs[6 + 2 * n_s:], nc)

        @pl.when(pl.program_id(0) == 0)
        def _():
            dst[...] = jnp.zeros_like(dst)

        _, vjp = jax.vjp(_scan_chunk, s_ref[0], *[_pair_stack(z_ref, comp * HW) for comp in range(6)])
        g = vjp((_pair_stack(dy_ref, 0), dst[...]))
        dst[...] = g[0]
        for comp in range(6):
            _pair_store(dz_ref, comp * HW, g[1 + comp], ex_ref)

    back = lambda i: (0, nc - 1 - i, 0)
    wide = pl.BlockSpec((batch, c, 6 * HW), back)
    res = pl.pallas_call(
        body, name="rwkv_scan_bwd", grid=(nc,),
        in_specs=[wide, pl.BlockSpec((1, hp, PAIR_W, PAIR_W), lambda i: (nc - 1 - i, 0, 0, 0)),
                  pl.BlockSpec((batch, c, HW), back), wide] + [_HBM_SPEC] * n_s,
        out_specs=[wide] + [_HBM_SPEC] * n_s,
        out_shape=[jax.ShapeDtypeStruct((batch, seq, 6 * HW), f32)] + _side_out_shapes(srcs, per_peer),
        scratch_shapes=[pltpu.VMEM((hp, PAIR_W, PAIR_W), f32)] + _side_sems(n_s),
        compiler_params=_cp(("arbitrary",)),
    )(main6.reshape(batch, seq, 6 * HW), states, dy.reshape(batch, seq, HW), extra.reshape(batch, seq, 6 * HW), *srcs)
    return res[0].reshape(batch * seq, 6 * HW), list(res[1:])


def _to_heads(x, batch, seq, k):
    return x.reshape(batch, seq, k, HEADS, HD).transpose(2, 0, 3, 1, 4).reshape(k, batch * HEADS, seq, HD)


def _from_heads(x, batch, seq, k):
    return x.reshape(k, batch, HEADS, seq, HD).transpose(1, 3, 0, 2, 4).reshape(batch * seq, k * HW)


def _pad_cols(x, width):
    return jnp.pad(x, ((0, 0), (0, width - x.shape[1])))


def _split_w_in(w):
    z64 = jnp.zeros((w.shape[0], 64), w.dtype)
    w_r = jnp.concatenate([w[:, 1544:3080], w[:, 3080:3144], z64, w[:, 3144:3208], z64, w[:, 3208:3336]], axis=1)
    return w[:, :1536], _pad_cols(w[:, 1536:1544], 128), w_r, w[:, 3336:3848], w[:, 3848:]


def _merge_w_in(g_qkv, g_f, g_r, g_mq, g_g):
    return jnp.concatenate([g_qkv, g_f[:, :8], g_r[:, :1536], g_r[:, 1536:1600], g_r[:, 1664:1728], g_r[:, 1792:],
                            g_mq, g_g], axis=1)


def _pad_lora(v):
    z64 = jnp.zeros((1, 64), v.dtype)
    return jnp.concatenate([v[:, :1536], v[:, 1536:1600], z64, v[:, 1600:1664], z64, v[:, 1664:]], axis=1)


def _unpad_lora(v):
    return jnp.concatenate([v[:, :1536], v[:, 1536:1600], v[:, 1664:1728], v[:, 1792:]], axis=1)


def _local_step(x, mem, target, w, p, late=None, early=None, last=None):
    batch, seq, _ = x.shape
    t = batch * seq
    x2, tg2, mem2 = x.reshape(t, D), target.reshape(t, D), mem.reshape(batch * MEM_LEN, D)
    w_qkv, w_f, w_r, w_mq, w_g3 = _split_w_in(w["w_in"])
    mu = _pad_lora(p["rwkv_mu"])
    bias = _pad_cols(p["fox_f_bias"], 128)
    r_k = p["rwkv_r_k"].reshape(1, HW)
    post_params = [p["rwkv_gn_g"], p["rwkv_gn_b"], r_k]
    rw_widths = [HW, HW, HW, LORA_PAD, LORA_PAD, LORA_PAD]
    six = [HW] * 6

    (u,) = _rows_fwd("rms_pre1", _fn_rms, [], [(x2, [D])], [p["pre1_g"]], [[D]], dtypes=[bf16])
    p_qkv = _matmul("proj_qkv", u, w_qkv, "nn", out_dtype=bf16)
    p_f = _matmul("proj_f", u, w_f, "nn")
    p_r = _matmul("proj_rwkv", u, w_r, "nn")
    p_mq = _matmul("proj_memq", u, w_mq, "nn", out_dtype=bf16)
    p_g = _matmul("proj_gate", u, w_g3, "nn", out_dtype=bf16)

    c = _fox_gate_fwd(p_f, bias, batch, seq)
    c_rows = c[:, :HEADS].reshape(batch, seq, HEADS).transpose(0, 2, 1)
    fox_o, lse, gathered = _fox_fwd(p_qkv, c, c_rows, batch, seq, side=(late[0], False) if late else None)
    if late:
        w = {**w, **late[2](gathered, 0)}
    fox_out = fox_o.astype(bf16)

    w_up = jnp.pad(w["rwkv_w_up"].astype(f32), ((0, LORA_PAD - 64), (0, 0)))
    a_up = jnp.pad(w["rwkv_a_up"].astype(f32), ((0, LORA_PAD - 64), (0, 0)))
    pre_params = [p["rwkv_w0"], w_up, p["rwkv_a0"], a_up, w["rwkv_g_up"].astype(f32), p["rwkv_k_k"], p["rwkv_k_a"]]
    ps = _tokshift_fwd(p_r, mu, batch, seq)
    main6, g_rw = _rows_fwd("rwkv_pre", _fn_rwkv_pre, [], [(ps, rw_widths)], pre_params, [six, [HW]], tm=256)
    y_rw, states, gathered = _scan_fwd(main6, batch, seq, side=(late[1], False) if late else None)
    if late:
        w = {**w, **late[2](gathered, 1)}
    post_consts = []
    post_rows = [(y_rw, [HW]), (main6, six), (g_rw, [HW])]

    def fn_post(y, r, _wl, k2, v, _a, _b, g, gn_g, gn_b, rk):
        return _fn_rwkv_post(y, r, k2, v, g, gn_g, gn_b, rk)

    (rwkv_out,) = _rows_fwd("rwkv_post", fn_post, post_consts, post_rows, post_params, [[HW]], dtypes=[bf16], tm=256)

    (memn,) = _rows_fwd("rms_mem", _fn_rms, [], [(mem2, [D])], [p["mem_norm_g"]], [[D]], dtypes=[bf16])
    mem_kv = _matmul("proj_memkv", memn, w["w_mem_kv"], "nn")
    mem_out = _mem_fwd(p_mq, mem_kv, batch, seq)

    a_fox = _matmul("out_fox", fox_out, w["w_fox_out"], "nn", out_dtype=bf16)
    a_rwkv = _matmul("out_rwkv", rwkv_out, w["w_rwkv_out"], "nn", out_dtype=bf16)
    a_mem = _matmul("out_mem", mem_out, w["w_mem_out"], "nn", out_dtype=bf16)
    merge_rows = [(a_fox, [D]), (a_rwkv, [D]), (a_mem, [D]), (p_g, [D, D, D])]
    (merged,) = _rows_fwd("merge", _fn_merge, [], merge_rows, [], [[D]], dtypes=[bf16])
    yy = _matmul("out_o", merged, w["w_o"], "nn")
    post1_rows = [(yy, [D]), (x2, [D])]
    post1_params = [p["post1_g"], p["pre2_g"]]
    h1, u2 = _rows_fwd("post1", _fn_post1, [], post1_rows, post1_params, [[D], [D]], dtypes=[f32, bf16])
    gp = _matmul("ffn_gate", u2, w["w_ffn_gate"], "nn", out_dtype=bf16)
    up = _matmul("ffn_up", u2, w["w_ffn_up"], "nn", out_dtype=bf16)
    (hmid,) = _rows_fwd("swiglu", _fn_swiglu, [], [(gp, [D_FF]), (up, [D_FF])], [], [[D_FF]], dtypes=[bf16])
    ffn = _matmul("ffn_down", hmid, w["w_ffn_down"], "nn")
    final_rows = [(ffn, [D]), (h1, [D])]
    (loss,) = _rows_fwd("final", _fn_final, [(tg2, [D])], final_rows, [p["post2_g"]], [], n_sums=1)

    gw, gp_ = {}, {}
    (d_ffn, d_h1), (gp_["post2_g"],) = _rows_bwd("final_bwd", _fn_final, [(tg2, [D])], final_rows, [p["post2_g"]], [], [],
                                                  n_sums=1, dtypes=[bf16, f32])
    d_hmid = _matmul("ffn_down_dx", d_ffn, w["w_ffn_down"], "nt", out_dtype=bf16)
    gw["w_ffn_down"] = _matmul("ffn_down_dw", hmid, d_ffn, "tn", out_dtype=bf16)
    (d_gp, d_up), _ = _rows_bwd("swiglu_bwd", _fn_swiglu, [], [(gp, [D_FF]), (up, [D_FF])], [], [[D_FF]], [d_hmid],
                                dtypes=[bf16, bf16])
    d_u2 = _matmul("ffn_gate_dx", d_gp, w["w_ffn_gate"], "nt")
    d_u2 = _matmul("ffn_up_dx", d_up, w["w_ffn_up"], "nt", add=d_u2)
    gw["w_ffn_gate"] = _matmul("ffn_gate_dw", u2, d_gp, "tn", out_dtype=bf16)
    gw["w_ffn_up"] = _matmul("ffn_up_dw", u2, d_up, "tn", out_dtype=bf16)
    (d_yy, d_x_res), (gp_["post1_g"], gp_["pre2_g"]) = _rows_bwd(
        "post1_bwd", _fn_post1, [], post1_rows, post1_params, [[D], [D]], [d_h1, d_u2], dtypes=[bf16, f32])
    d_merged = _matmul("out_o_dx", d_yy, w["w_o"], "nt", out_dtype=bf16)
    gw["w_o"] = _matmul("out_o_dw", merged, d_yy, "tn", out_dtype=bf16)
    (d_a_fox, d_a_rwkv, d_a_mem, d_p_g), _ = _rows_bwd("merge_bwd", _fn_merge, [], merge_rows, [], [[D]], [d_merged],
                                                       dtypes=[bf16] * 4)
    d_fox_out = _matmul("out_fox_dx", d_a_fox, w["w_fox_out"], "nt")
    gw["w_fox_out"] = _matmul("out_fox_dw", fox_out, d_a_fox, "tn", out_dtype=bf16)
    d_rwkv_out = _matmul("out_rwkv_dx", d_a_rwkv, w["w_rwkv_out"], "nt")
    gw["w_rwkv_out"] = _matmul("out_rwkv_dw", rwkv_out, d_a_rwkv, "tn", out_dtype=bf16)
    d_mem_out = _matmul("out_mem_dx", d_a_mem, w["w_mem_out"], "nt")
    gw["w_mem_out"] = _matmul("out_mem_dw", mem_out, d_a_mem, "tn", out_dtype=bf16)

    d_p_mq, d_km, d_vm = _mem_bwd(p_mq, mem_kv, d_mem_out, batch, seq)
    d_mem_kv = jnp.concatenate([d_km, d_vm], axis=1).astype(bf16)
    gw["w_mem_kv"] = _matmul("proj_memkv_dw", memn, d_mem_kv, "tn", out_dtype=bf16)
    d_memn = _matmul("proj_memkv_dx", d_mem_kv, w["w_mem_kv"], "nt")
    _, (gp_["mem_norm_g"],) = _rows_bwd("rms_mem_bwd", _fn_rms, [], [(mem2, [D])], [p["mem_norm_g"]], [[D]], [d_memn])

    d_q, d_k, d_v, d_cq, d_ck = _fox_bwd(p_qkv, c, c_rows, fox_o, lse, d_fox_out, batch, seq)
    d_p_qkv = jnp.concatenate([d_q, d_k, d_v], axis=1).astype(bf16)
    d_p_f, d_bias = _fox_gate_bwd(p_f, bias, d_cq, d_ck, batch, seq)
    gp_["fox_f_bias"] = d_bias[:, :HEADS]

    (d_y_rw, d_main6_post, d_g_rw), (gp_["rwkv_gn_g"], gp_["rwkv_gn_b"], d_rk) = _rows_bwd(
        "rwkv_post_bwd", fn_post, post_consts, post_rows, post_params, [[HW]], [d_rwkv_out], tm=256)
    gp_["rwkv_r_k"] = d_rk.reshape(1, HEADS, HD)
    d_main6, early_got = _scan_bwd(main6, states, d_y_rw, d_main6_post, batch, seq,
                                   side=(early(gw), True) if early else None)

    def fn_pre_sum(*args):
        return _fn_rwkv_pre(*args)

    (d_ps,), d_pre = _rows_bwd("rwkv_pre_bwd", fn_pre_sum, [], [(ps, rw_widths)], pre_params, [six, [HW]],
                               [d_main6, d_g_rw], tm=256)
    gp_["rwkv_w0"], d_w_up, gp_["rwkv_a0"], d_a_up, gw["rwkv_g_up"], gp_["rwkv_k_k"], gp_["rwkv_k_a"] = d_pre
    gw["rwkv_w_up"], gw["rwkv_a_up"] = d_w_up[:64], d_a_up[:64]
    d_p_r, d_mu = _tokshift_bwd(p_r, mu, d_ps, batch, seq)
    gp_["rwkv_mu"] = _unpad_lora(d_mu)

    gw["w_in"] = _merge_w_in(_matmul("proj_qkv_dw", u, d_p_qkv, "tn", out_dtype=bf16), _matmul("proj_f_dw", u, d_p_f, "tn", out_dtype=bf16),
                             _matmul("proj_rwkv_dw", u, d_p_r, "tn", out_dtype=bf16), _matmul("proj_memq_dw", u, d_p_mq, "tn", out_dtype=bf16),
                             _matmul("proj_gate_dw", u, d_p_g, "tn", out_dtype=bf16))
    d_u, last_got = _sum_nt("proj_dx", [d_p_qkv, d_p_f, d_p_r, d_p_mq, d_p_g], [w_qkv, w_f, w_r, w_mq, w_g3],
                            side=(last(gw), True) if last else None)
    (d_x,), (gp_["pre1_g"],) = _rows_bwd("rms_pre1_bwd", _fn_rms, [], [(x2, [D])], [p["pre1_g"]], [[D]], [d_u], add=d_x_res)
    return loss, d_x.reshape(x.shape), gw, gp_, early_got, last_got


def _rows_add(name, a, b):
    (s,) = _rows_fwd(name, lambda u, v: (u + v,), [], [(a, [a.shape[1]]), (b, [b.shape[1]])], [], [[a.shape[1]]])
    return s


def _adamw(name, recv, row_off, w, m, v):
    _, rows, cols = w.shape
    tr = max(t for t in range(16, min(rows, 128) + 1, 16) if rows % t == 0 and row_off % t == 0)
    first = row_off // tr

    def body(g_ref, w_ref, m_ref, v_ref, go_ref, d_ref, mo_ref, vo_ref):
        g = g_ref[0].astype(f32)
        for s in range(1, N_DEV):
            g = g + g_ref[s].astype(f32)
        m_new = ADAM_B1 * m_ref[0] + (1.0 - ADAM_B1) * g
        v_new = ADAM_B2 * v_ref[0] + (1.0 - ADAM_B2) * (g * g)
        m_hat = m_new / (1.0 - ADAM_B1 ** ADAM_STEP)
        v_hat = v_new / (1.0 - ADAM_B2 ** ADAM_STEP)
        go_ref[0] = g
        d_ref[0] = -ADAM_LR * (m_hat / (jnp.sqrt(v_hat) + ADAM_EPS) + ADAM_WD * w_ref[0])
        mo_ref[0] = m_new
        vo_ref[0] = v_new

    spec = pl.BlockSpec((1, tr, cols), lambda i: (0, i, 0))
    return pl.pallas_call(
        body, name=name, grid=(rows // tr,),
        in_specs=[pl.BlockSpec((N_DEV, tr, cols), lambda i: (0, first + i, 0)), spec, spec, spec],
        out_specs=[spec] * 4, out_shape=[jax.ShapeDtypeStruct(w.shape, f32)] * 4,
        compiler_params=_cp(("parallel",)),
    )(recv, w, m, v)


GROUPS = (
    ("in", ("w_in",), 0),
    ("memkv", ("w_mem_kv",), 0),
    ("ffn_gu", ("w_ffn_gate", "w_ffn_up"), 0),
    ("down_o", ("w_ffn_down", "w_o"), 0),
    ("outs", ("w_fox_out", "w_rwkv_out", "w_mem_out"), 0),
    ("lora", ("rwkv_w_up", "rwkv_a_up", "rwkv_g_up"), 0),
)
FIRST_GROUPS = ("in", "memkv")
LATE_GROUPS = (("down_o", "outs", "lora"), ("ffn_gu",))
EARLY_GRAD_GROUPS = ("memkv", "ffn_gu", "down_o", "outs")
LAST_GRAD_GROUPS = ("in", "lora")
SHARD_AXIS = {n: a for n, _, a in SHARDED}
SMALL_ROWS = 16


def _group_local(shards, members, join):
    parts = [shards[n].reshape(shards[n].shape[-2:]) for n in members]
    return parts[0] if len(parts) == 1 else jnp.concatenate(parts, axis=join)


def _group_split(arr, members, join, lead=False):
    out, off = {}, 0
    for n in members:
        shape = dict((k, s) for k, s, _ in SHARDED)[n]
        size = _block_shape(shape, SHARD_AXIS[n])[join]
        idx = [slice(None)] * arr.ndim
        idx[arr.ndim - 2 + join] = slice(off, off + size)
        out[n] = arr[tuple(idx)]
        off += size
    return out


def _full_from_blocks(blocks, axis):
    if axis == 0:
        return blocks.reshape(-1, blocks.shape[2])
    return blocks.transpose(1, 0, 2).reshape(blocks.shape[1], -1)


def _blocks_from_full(full, axis):
    if axis == 0:
        return full.reshape(N_DEV, -1, full.shape[1])
    return full.reshape(full.shape[0], N_DEV, -1).transpose(1, 0, 2)


def _assemble(gathered, names):
    out = {}
    for arr, g in zip(gathered, names):
        _, members, join = [grp for grp in GROUPS if grp[0] == g][0]
        for n, blk in _group_split(arr, members, join, lead=True).items():
            out[n] = _full_from_blocks(blk, SHARD_AXIS[n])
    return out


def _grad_blocks(gw, names):
    out = []
    for g in names:
        _, members, join = [grp for grp in GROUPS if grp[0] == g][0]
        parts = [_blocks_from_full(gw[n].astype(bf16), SHARD_AXIS[n]) for n in members]
        out.append(parts[0] if len(parts) == 1 else jnp.concatenate(parts, axis=1 + join))
    return out


def _small_pack(d):
    flat = jnp.concatenate([d[n].reshape(-1) for n, _ in REPLICATED])
    return jnp.pad(flat, (0, SMALL_ROWS * LANES - REPL_ELEMS)).reshape(SMALL_ROWS, LANES)


def _small_unpack(packed):
    out, flat, off = {}, packed.reshape(-1), 0
    for n, shape in REPLICATED:
        k = _rows_of((LANES,) + shape)
        out[n] = flat[off:off + k].reshape(shape)
        off += k
    return out


def kernel(x, mem, pre1_g, post1_g, pre2_g, post2_g, mem_norm_g, w_in, fox_f_bias, rwkv_mu, rwkv_w0, rwkv_w_up, rwkv_a0, rwkv_a_up, rwkv_g_up, rwkv_k_k, rwkv_k_a, rwkv_r_k, rwkv_gn_g, rwkv_gn_b, w_mem_kv, w_fox_out, w_rwkv_out, w_mem_out, w_o, w_ffn_gate, w_ffn_up, w_ffn_down, loss_target, m_pre1_g, m_post1_g, m_pre2_g, m_post2_g, m_mem_norm_g, m_w_in, m_fox_f_bias, m_rwkv_mu, m_rwkv_w0, m_rwkv_w_up, m_rwkv_a0, m_rwkv_a_up, m_rwkv_g_up, m_rwkv_k_k, m_rwkv_k_a, m_rwkv_r_k, m_rwkv_gn_g, m_rwkv_gn_b, m_w_mem_kv, m_w_fox_out, m_w_rwkv_out, m_w_mem_out, m_w_o, m_w_ffn_gate, m_w_ffn_up, m_w_ffn_down, v_pre1_g, v_post1_g, v_pre2_g, v_post2_g, v_mem_norm_g, v_w_in, v_fox_f_bias, v_rwkv_mu, v_rwkv_w0, v_rwkv_w_up, v_rwkv_a0, v_rwkv_a_up, v_rwkv_g_up, v_rwkv_k_k, v_rwkv_k_a, v_rwkv_r_k, v_rwkv_gn_g, v_rwkv_gn_b, v_w_mem_kv, v_w_fox_out, v_w_rwkv_out, v_w_mem_out, v_w_o, v_w_ffn_gate, v_w_ffn_up, v_w_ffn_down):
    args = dict(locals())
    wts = {n: args[n] for n in WEIGHT_ORDER}
    ms = {n: args["m_" + n] for n in WEIGHT_ORDER}
    vs = {n: args["v_" + n] for n in WEIGHT_ORDER}

    groups = {g: (members, join) for g, members, join in GROUPS}
    w_bf16 = {n: wts[n].astype(bf16) for n, _, _ in SHARDED}

    def send(g):
        return _group_local(w_bf16, *groups[g])

    first = _exchange("gather_first", [send(g) for g in FIRST_GROUPS], per_peer=False)
    full = _assemble(first, FIRST_GROUPS)
    small_in = {n: (wts[n] if n == "rwkv_r_k" else wts[n].reshape(wts[n].shape[-2:])) for n, _ in REPLICATED}
    late = ([send(g) for g in LATE_GROUPS[0]], [send(g) for g in LATE_GROUPS[1]],
            lambda got, which: _assemble(got, LATE_GROUPS[which]))
    loss_part, grad_x, gw, gp, early_got, last_got = _local_step(
        x, mem, loss_target, full, small_in, late=late, early=lambda g: _grad_blocks(g, EARLY_GRAD_GROUPS),
        last=lambda g: _grad_blocks(g, LAST_GRAD_GROUPS))
    (small_got,) = _exchange("exchange_small", [_small_pack(gp).astype(bf16)], per_peer=False)
    received = dict(zip(EARLY_GRAD_GROUPS + LAST_GRAD_GROUPS, list(early_got) + list(last_got)))

    outs = [{}, {}, {}, {}]
    for g, members, _ in GROUPS:
        off = 0
        for n in members:
            for o, arr in zip(outs, _adamw("adamw_" + n, received[g], off, wts[n], ms[n], vs[n])):
                o[n] = arr
            off += wts[n].shape[1]
    res = _adamw("adamw_small", small_got, 0, *[_small_pack(d)[None] for d in (wts, ms, vs)])
    for o, arr in zip(outs, res):
        o.update(_small_unpack(arr))
    loss = lax.psum(loss_part[0, 0], ("x", "y", "c"))
    return (loss, grad_x, *[o[n].reshape(wts[n].shape) for o in outs for n in WEIGHT_ORDER])
```

```python
import functools

import jax
import jax.numpy as jnp
from jax import lax
from jax.experimental import pallas as pl
from jax.experimental.pallas import tpu as pltpu

f32 = jnp.float32
bf16 = jnp.bfloat16
_HI = lax.Precision.HIGHEST

D = 1024
HEADS = 8
HD = 64
HW = HEADS * HD
MEM_HEADS = 4
MEM_HD = 128
MEM_W = 512
MEM_LEN = 256
D_FF = 2816
LORA_PAD = 128
RW_COLS = 3 * HW + 3 * LORA_PAD
NORM_EPS = 1e-6
GN_EPS = 64e-5
Q_BLOCK = 128
SCAN_CHUNK = 64
N_DEV = 8
LANES = 1024
VMEM_LIMIT = 56 * 1024 * 1024

ADAM_LR = 0.001
ADAM_B1 = 0.9
ADAM_B2 = 0.999
ADAM_EPS = 1e-08
ADAM_WD = 0.01
ADAM_STEP = 10

TRANSPOSED = ("w_in", "w_ffn_gate", "w_ffn_up")
SHARDED = (
    ("w_in", (6920, 1024), 0),
    ("w_ffn_gate", (2816, 1024), 0),
    ("w_ffn_up", (2816, 1024), 0),
    ("w_ffn_down", (2816, 1024), 0),
    ("w_mem_kv", (1024, 1024), 0),
    ("w_o", (1024, 1024), 0),
    ("w_fox_out", (512, 1024), 1),
    ("w_rwkv_out", (512, 1024), 1),
    ("w_mem_out", (512, 1024), 1),
    ("rwkv_w_up", (64, 512), 1),
    ("rwkv_a_up", (64, 512), 1),
    ("rwkv_g_up", (128, 512), 1),
)
REPLICATED = (
    ("pre1_g", (1, 1024)), ("post1_g", (1, 1024)), ("pre2_g", (1, 1024)), ("post2_g", (1, 1024)),
    ("mem_norm_g", (1, 1024)), ("fox_f_bias", (1, 8)), ("rwkv_mu", (1, 1792)), ("rwkv_w0", (1, 512)),
    ("rwkv_a0", (1, 512)), ("rwkv_k_k", (1, 512)), ("rwkv_k_a", (1, 512)), ("rwkv_r_k", (1, 8, 64)),
    ("rwkv_gn_g", (1, 512)), ("rwkv_gn_b", (1, 512)),
)
WEIGHT_ORDER = ('pre1_g', 'post1_g', 'pre2_g', 'post2_g', 'mem_norm_g', 'w_in', 'fox_f_bias', 'rwkv_mu',
                'rwkv_w0', 'rwkv_w_up', 'rwkv_a0', 'rwkv_a_up', 'rwkv_g_up', 'rwkv_k_k', 'rwkv_k_a',
                'rwkv_r_k', 'rwkv_gn_g', 'rwkv_gn_b', 'w_mem_kv', 'w_fox_out', 'w_rwkv_out', 'w_mem_out',
                'w_o', 'w_ffn_gate', 'w_ffn_up', 'w_ffn_down')


def _block_shape(shape, axis):
    return tuple(s // N_DEV if i == axis else s for i, s in enumerate(shape))


def _rows_of(shape):
    n = 1
    for s in shape:
        n *= s
    return n // LANES


SHARD_ROWS = sum(_rows_of(_block_shape(s, a)) for _, s, a in SHARDED)
REPL_ELEMS = sum(_rows_of((LANES,) + s) for _, s in REPLICATED)
REPL_ROWS = -(-REPL_ELEMS // LANES)
PACK_ROWS = -(-(SHARD_ROWS + REPL_ROWS) // 128) * 128
GATHER_ROWS = -(-SHARD_ROWS // 16) * 16


def _cp(sem=None):
    return pltpu.CompilerParams(dimension_semantics=sem, vmem_limit_bytes=VMEM_LIMIT)


def _tile(dim, cap):
    best = None
    for t in range(128, min(dim, cap) + 1, 128):
        if dim % t == 0:
            best = t
    return best if best is not None else dim


def _two_terms(x):
    hi = x.astype(bf16)
    return hi, (x - hi.astype(f32)).astype(bf16)


def _dg(a, b, dims, exact):
    if exact == "split":
        (a_hi, a_lo), (b_hi, b_lo) = _two_terms(a), _two_terms(b)
        dot = functools.partial(lax.dot_general, dimension_numbers=dims, preferred_element_type=f32)
        return dot(a_hi, b_hi) + (dot(a_hi, b_lo) + dot(a_lo, b_hi))
    if exact:
        return lax.dot_general(a, b, dims, precision=_HI, preferred_element_type=f32)
    return lax.dot_general(a.astype(bf16), b.astype(bf16), dims, preferred_element_type=f32)


def _make_mm(batched, exact):
    o = 1 if batched else 0
    bd = ((0,), (0,)) if batched else ((), ())
    d_nn = (((1 + o,), (o,)), bd)
    d_nt = (((1 + o,), (1 + o,)), bd)
    d_tn = (((o,), (o,)), bd)

    @jax.custom_vjp
    def nn(a, b):
        return _dg(a, b, d_nn, exact)

    @jax.custom_vjp
    def nt(a, b):
        return _dg(a, b, d_nt, exact)

    @jax.custom_vjp
    def tn(a, b):
        return _dg(a, b, d_tn, exact)

    nn.defvjp(lambda a, b: (_dg(a, b, d_nn, exact), (a, b)),
              lambda res, g: (_dg(g, res[1], d_nt, exact), _dg(res[0], g, d_tn, exact)))
    nt.defvjp(lambda a, b: (_dg(a, b, d_nt, exact), (a, b)),
              lambda res, g: (_dg(g, res[1], d_nn, exact), _dg(g, res[0], d_tn, exact)))
    tn.defvjp(lambda a, b: (_dg(a, b, d_tn, exact), (a, b)),
              lambda res, g: (_dg(res[1], g, d_nt, exact), _dg(res[0], g, d_nn, exact)))
    return nn, nt, tn


def _sigmoid(x):
    return 1.0 / (1.0 + jnp.exp(-x))


def _head_sum_raw(x):
    width = 2 * HD
    i = lax.broadcasted_iota(jnp.int32, (width, width), 0) // HD
    j = lax.broadcasted_iota(jnp.int32, (width, width), 1) // HD
    m = (i == j).astype(bf16)
    dims = (((1,), (0,)), ((), ()))
    out = []
    for p in range(x.shape[1] // width):
        xp = x[:, p * width:(p + 1) * width]
        hi = xp.astype(bf16)
        lo = (xp - hi.astype(f32)).astype(bf16)
        out.append(lax.dot_general(hi, m, dims, preferred_element_type=f32)
                   + lax.dot_general(lo, m, dims, preferred_element_type=f32))
    return jnp.concatenate(out, axis=1)


@jax.custom_vjp
def _head_sum(x):
    return _head_sum_raw(x)


_head_sum.defvjp(lambda x: (_head_sum_raw(x), None), lambda _, g: (_head_sum_raw(g),))


WEIGHT_TILE_BYTES = 13 * 512 * 1024
ACC_TILE_BYTES = 8 * 1024 * 1024


def _matmul(name, a, b, mode, add=None, out_dtype=f32):
    has_add = add is not None
    if mode == "tn":
        (k, m), (_, n) = a.shape, b.shape
        tn = _tile(n, max(128, ACC_TILE_BYTES // (4 * m)))
        tk = _tile(k, 1024)

        nk = k // tk

        def body(a_ref, b_ref, o_ref, acc):
            @pl.when(pl.program_id(1) == 0)
            def _():
                acc[...] = jnp.zeros_like(acc)

            acc[...] += lax.dot_general(a_ref[...].astype(bf16), b_ref[...].astype(bf16),
                                        (((0,), (0,)), ((), ())), preferred_element_type=f32)

            @pl.when(pl.program_id(1) == nk - 1)
            def _():
                o_ref[...] = acc[...].astype(o_ref.dtype)

        return pl.pallas_call(
            body, name=name, grid=(n // tn, nk),
            in_specs=[pl.BlockSpec((tk, m), lambda j, kk: (kk, 0)), pl.BlockSpec((tk, tn), lambda j, kk: (kk, j))],
            out_specs=pl.BlockSpec((m, tn), lambda j, kk: (0, j)), out_shape=jax.ShapeDtypeStruct((m, n), out_dtype),
            scratch_shapes=[pltpu.VMEM((m, tn), f32)],
            compiler_params=_cp(("parallel", "arbitrary")),
        )(a, b)

    (m, k) = a.shape
    n = b.shape[1] if mode == "nn" else b.shape[0]
    tm = _tile(m, 512)
    tn = _tile(n, max(128, WEIGHT_TILE_BYTES // (2 * k)))
    dims = (((1,), (0,)), ((), ())) if mode == "nn" else (((1,), (1,)), ((), ()))
    b_spec = pl.BlockSpec((k, tn), lambda j, i: (0, j)) if mode == "nn" else pl.BlockSpec((tn, k), lambda j, i: (j, 0))
    o_spec = pl.BlockSpec((tm, tn), lambda j, i: (i, j))

    def body(*refs):
        a_ref, b_ref = refs[0], refs[1]
        o_ref = refs[-1]
        r = lax.dot_general(a_ref[...].astype(bf16), b_ref[...].astype(bf16), dims, preferred_element_type=f32)
        if has_add:
            r = r + refs[2][...]
        o_ref[...] = r.astype(o_ref.dtype)

    return pl.pallas_call(
        body, name=name, grid=(n // tn, m // tm),
        in_specs=[pl.BlockSpec((tm, k), lambda j, i: (i, 0)), b_spec] + ([o_spec] if has_add else []),
        out_specs=o_spec, out_shape=jax.ShapeDtypeStruct((m, n), out_dtype),
        compiler_params=_cp(("parallel", "arbitrary")),
    )(*((a, b, add) if has_add else (a, b)))


def _sum_nn(name, a_list, b_list, side=None):
    m, n = a_list[0].shape[0], b_list[0].shape[1]
    tm = _tile(m, 256)
    n_g = len(a_list)
    srcs, per_peer = side if side is not None else ([], False)
    n_s = len(srcs)

    def body(*refs):
        o_ref = refs[2 * n_g + n_s]
        _side_exchange(refs[2 * n_g:2 * n_g + n_s], refs[2 * n_g + n_s + 1:2 * n_g + 2 * n_s + 1], per_peer,
                       refs[2 * n_g + 2 * n_s + 1:], m // tm)
        acc = None
        for g in range(n_g):
            r = lax.dot_general(refs[g][...].astype(bf16), refs[n_g + g][...].astype(bf16), (((1,), (0,)), ((), ())),
                                preferred_element_type=f32)
            acc = r if acc is None else acc + r
        o_ref[...] = acc

    res = pl.pallas_call(
        body, name=name, grid=(m // tm,),
        in_specs=[pl.BlockSpec((tm, a.shape[1]), lambda i: (i, 0)) for a in a_list]
        + [pl.BlockSpec(b.shape, lambda i: (0, 0)) for b in b_list] + [_HBM_SPEC] * n_s,
        out_specs=[pl.BlockSpec((tm, n), lambda i: (i, 0))] + [_HBM_SPEC] * n_s,
        out_shape=[jax.ShapeDtypeStruct((m, n), f32)] + _side_out_shapes(srcs, per_peer),
        scratch_shapes=_side_sems(n_s),
        compiler_params=_cp(("arbitrary",)),
    )(*a_list, *b_list, *srcs)
    return res[0], list(res[1:])


def _pieces(ref, widths):
    out, off = [], 0
    for w in widths:
        out.append(ref[:, off:off + w].astype(f32))
        off += w
    return out


def _store_pieces(ref, widths, vals, add_ref=None):
    off = 0
    for w, v in zip(widths, vals):
        ref[:, off:off + w] = (v if add_ref is None else v + add_ref[:, off:off + w]).astype(ref.dtype)
        off += w


def _rows_fwd(name, fn, consts, rows, params, outs, n_sums=0, tm=512, dtypes=None):
    t = (consts + rows)[0][0].shape[0]
    tm = min(tm, t)
    ins = consts + rows
    n_in, n_p, n_o = len(ins), len(params), len(outs)
    dtypes = dtypes or [f32] * n_o

    def body(*refs):
        in_refs, p_refs = refs[:n_in], refs[n_in:n_in + n_p]
        o_refs, s_refs = refs[n_in + n_p:n_in + n_p + n_o], refs[n_in + n_p + n_o:]
        vals = []
        for r, (_, widths) in zip(in_refs, ins):
            vals += _pieces(r, widths)
        res = fn(*vals, *[p[...] for p in p_refs])
        pos = 0
        for r, widths in zip(o_refs, outs):
            _store_pieces(r, widths, res[pos:pos + len(widths)])
            pos += len(widths)

        @pl.when(pl.program_id(0) == 0)
        def _():
            for s in s_refs:
                s[...] = jnp.zeros_like(s)

        for s, v in zip(s_refs, res[pos:]):
            s[...] += v

    row_spec = lambda w: pl.BlockSpec((tm, w), lambda i: (i, 0))
    full = lambda p: pl.BlockSpec(p.shape, lambda i: (0,) * p.ndim)
    return pl.pallas_call(
        body, name=name, grid=(t // tm,),
        in_specs=[row_spec(a.shape[1]) for a, _ in ins] + [full(p) for p in params],
        out_specs=[row_spec(sum(w)) for w in outs] + [pl.BlockSpec((1, 1), lambda i: (0, 0))] * n_sums,
        out_shape=[jax.ShapeDtypeStruct((t, sum(w)), dt) for w, dt in zip(outs, dtypes)] + [jax.ShapeDtypeStruct((1, 1), f32)] * n_sums,
        compiler_params=_cp(("arbitrary",)),
    )(*[a for a, _ in ins], *params)


def _rows_bwd(name, fn, consts, rows, params, outs, cts, n_sums=0, add=None, tm=512, dtypes=None):
    t = (consts + rows)[0][0].shape[0]
    tm = min(tm, t)
    n_c, n_r, n_p, n_o = len(consts), len(rows), len(params), len(outs)
    has_add = add is not None
    dtypes = dtypes or [f32] * n_r

    def body(*refs):
        pos = 0
        c_refs = refs[pos:pos + n_c]; pos += n_c
        r_refs = refs[pos:pos + n_r]; pos += n_r
        p_refs = refs[pos:pos + n_p]; pos += n_p
        ct_refs = refs[pos:pos + n_o]; pos += n_o
        add_ref = refs[pos] if has_add else None
        pos += 1 if has_add else 0
        dr_refs = refs[pos:pos + n_r]; pos += n_r
        dp_refs = refs[pos:pos + n_p]
        cvals, rvals = [], []
        for r, (_, widths) in zip(c_refs, consts):
            cvals += _pieces(r, widths)
        for r, (_, widths) in zip(r_refs, rows):
            rvals += _pieces(r, widths)
        pvals = [p[...] for p in p_refs]
        ctv = []
        for r, widths in zip(ct_refs, outs):
            ctv += _pieces(r, widths)
        ctv += [jnp.ones((1, 1), f32)] * n_sums
        _, vjp = jax.vjp(lambda *rp: tuple(fn(*cvals, *rp)), *rvals, *pvals)
        g = vjp(tuple(ctv))
        pos = 0
        for idx, (r, (_, widths)) in enumerate(zip(dr_refs, rows)):
            _store_pieces(r, widths, g[pos:pos + len(widths)], add_ref if idx == 0 else None)
            pos += len(widths)

        @pl.when(pl.program_id(0) == 0)
        def _():
            for dp in dp_refs:
                dp[...] = jnp.zeros_like(dp)

        for dp, v in zip(dp_refs, g[pos:]):
            dp[...] += v

    row_spec = lambda w: pl.BlockSpec((tm, w), lambda i: (i, 0))
    full = lambda p: pl.BlockSpec(p.shape, lambda i: (0,) * p.ndim)
    args = [a for a, _ in consts + rows] + list(params) + list(cts) + ([add] if has_add else [])
    res = pl.pallas_call(
        body, name=name, grid=(t // tm,),
        in_specs=[row_spec(a.shape[1]) for a, _ in consts + rows] + [full(p) for p in params]
        + [row_spec(sum(w)) for w in outs] + ([row_spec(add.shape[1])] if has_add else []),
        out_specs=[row_spec(a.shape[1]) for a, _ in rows] + [full(p) for p in params],
        out_shape=[jax.ShapeDtypeStruct(a.shape, dt) for (a, _), dt in zip(rows, dtypes)]
        + [jax.ShapeDtypeStruct(p.shape, f32) for p in params],
        compiler_params=_cp(("arbitrary",)),
    )(*args)
    return res[:n_r], res[n_r:]


def _rms(x, g):
    return x * lax.rsqrt(jnp.mean(x * x, axis=-1, keepdims=True) + NORM_EPS) * g


def _fn_rms(x, g):
    return (_rms(x, g),)


def _fn_rwkv_pre(r, k, v, wd, ad, gd, w0, w_up, a0, a_up, g_up, k_k, k_a):
    nn, _, _ = _make_mm(False, False)
    w_log = -_sigmoid(w0 + nn(jnp.tanh(wd), w_up)) * 0.6065306597126334
    a = _sigmoid(a0 + nn(ad, a_up))
    g = nn(_sigmoid(gd), g_up)
    kk = k * k_k
    kk = kk * lax.rsqrt(jnp.maximum(_head_sum(kk * kk), 1e-24))
    k2 = k * (1.0 + (a - 1.0) * k_a)
    return r, w_log, k2, v, -kk, kk * a, g


def _fn_rwkv_post(y, r, k2, v, g, gn_g, gn_b, r_k):
    mean = _head_sum(y) * (1.0 / HD)
    yc = y - mean
    var = _head_sum(yc * yc) * (1.0 / HD)
    yn = yc * lax.rsqrt(var + GN_EPS) * gn_g + gn_b
    bonus = _head_sum(r * k2 * r_k) * v
    return ((yn + bonus) * g,)


def _fn_merge(a_fox, a_rwkv, a_mem, g_fox, g_rwkv, g_mem):
    return (_sigmoid(g_fox) * a_fox + _sigmoid(g_rwkv) * a_rwkv + _sigmoid(g_mem) * a_mem,)


def _fn_post1(y, x, post1_g, pre2_g):
    h1 = x + _rms(y, post1_g)
    return h1, _rms(h1, pre2_g)


def _fn_swiglu(gp, up):
    return (gp * _sigmoid(gp) * up,)


def _fn_final(target, ffn, h1, post2_g):
    err = h1 + _rms(ffn, post2_g) - target
    per_row = jnp.mean(err * err, axis=-1, keepdims=True)
    return (0.5 * jnp.sum(per_row, axis=0, keepdims=True),)


def _shift_down(x):
    row = lax.broadcasted_iota(jnp.int32, x.shape, 0)
    return jnp.where(row == 0, 0.0, pltpu.roll(x, 1, 0))


def _shift_up(x):
    s = x.shape[0]
    row = lax.broadcasted_iota(jnp.int32, x.shape, 0)
    return jnp.where(row == s - 1, 0.0, pltpu.roll(x, s - 1, 0))


def _tokshift_fwd(p, mu, batch, seq):
    w = p.shape[1]
    tc = _tile(w, 384)

    def body(p_ref, mu_ref, o_ref):
        x = p_ref[...]
        o_ref[...] = x + (_shift_down(x) - x) * mu_ref[...]

    return pl.pallas_call(
        body, name="tokshift_fwd", grid=(w // tc, batch),
        in_specs=[pl.BlockSpec((seq, tc), lambda j, b: (b, j)), pl.BlockSpec((1, tc), lambda j, b: (0, j))],
        out_specs=pl.BlockSpec((seq, tc), lambda j, b: (b, j)),
        out_shape=jax.ShapeDtypeStruct(p.shape, f32),
        compiler_params=_cp(("parallel", "arbitrary")),
    )(p, mu)


def _tokshift_bwd(p, mu, dps, batch, seq):
    w = p.shape[1]
    tc = _tile(w, 384)

    def body(p_ref, mu_ref, d_ref, dp_ref, dmu_ref):
        x, mu_v, d = p_ref[...], mu_ref[...], d_ref[...]
        dp_ref[...] = (d * (1.0 - mu_v) + _shift_up(d * mu_v)).astype(dp_ref.dtype)

        @pl.when(pl.program_id(1) == 0)
        def _():
            dmu_ref[...] = jnp.zeros_like(dmu_ref)

        dmu_ref[...] += jnp.sum(d * (_shift_down(x) - x), axis=0, keepdims=True)

    return pl.pallas_call(
        body, name="tokshift_bwd", grid=(w // tc, batch),
        in_specs=[pl.BlockSpec((seq, tc), lambda j, b: (b, j)), pl.BlockSpec((1, tc), lambda j, b: (0, j)),
                  pl.BlockSpec((seq, tc), lambda j, b: (b, j))],
        out_specs=[pl.BlockSpec((seq, tc), lambda j, b: (b, j)), pl.BlockSpec((1, tc), lambda j, b: (0, j))],
        out_shape=[jax.ShapeDtypeStruct(p.shape, bf16), jax.ShapeDtypeStruct(mu.shape, f32)],
        compiler_params=_cp(("parallel", "arbitrary")),
    )(p, mu, dps)


def _cum_block(seq):
    return _tile(seq, 256)


def _fox_gate_fwd(f, bias, batch, seq):
    cb = _cum_block(seq)

    def body(f_ref, b_ref, c_ref):
        row = lax.broadcasted_iota(jnp.int32, (cb, cb), 0)
        col = lax.broadcasted_iota(jnp.int32, (cb, cb), 1)
        tri = (col <= row).astype(f32)
        carry = jnp.zeros((1, 128), f32)
        for i in range(seq // cb):
            z = f_ref[i * cb:(i + 1) * cb, :] + b_ref[...]
            ls = jnp.minimum(z, 0.0) - jnp.log(1.0 + jnp.exp(-jnp.abs(z)))
            c = _dg(tri, ls, (((1,), (0,)), ((), ())), True) + carry
            c_ref[i * cb:(i + 1) * cb, :] = c
            carry = c[cb - 1:cb, :]

    return pl.pallas_call(
        body, name="fox_gate_fwd", grid=(batch,),
        in_specs=[pl.BlockSpec((seq, 128), lambda b: (b, 0)), pl.BlockSpec((1, 128), lambda b: (0, 0))],
        out_specs=pl.BlockSpec((seq, 128), lambda b: (b, 0)),
        out_shape=jax.ShapeDtypeStruct(f.shape, f32),
        compiler_params=_cp(("arbitrary",)),
    )(f, bias)


def _fox_gate_bwd(f, bias, dc_a, dc_b, batch, seq):
    cb = _cum_block(seq)

    def body(f_ref, b_ref, da_ref, db_ref, df_ref, dbias_ref):
        row = lax.broadcasted_iota(jnp.int32, (cb, cb), 0)
        col = lax.broadcasted_iota(jnp.int32, (cb, cb), 1)
        triu = (col >= row).astype(f32)

        @pl.when(pl.program_id(0) == 0)
        def _():
            dbias_ref[...] = jnp.zeros_like(dbias_ref)

        lane = lax.broadcasted_iota(jnp.int32, (1, 128), 1)

        def by_head(blk):
            out = jnp.zeros((cb, 128), f32)
            for p in range(HEADS // 2):
                for e in range(2):
                    out = jnp.where(lane == 2 * p + e, _pick_lane(blk[:, p * 128:(p + 1) * 128], e), out)
            return out

        carry = jnp.zeros((1, 128), f32)
        tot = jnp.zeros((1, 128), f32)
        for i in reversed(range(seq // cb)):
            sl = slice(i * cb, (i + 1) * cb)
            dc = by_head(da_ref[sl, :] + db_ref[sl, :])
            dls = _dg(triu, dc, (((1,), (0,)), ((), ())), True) + carry
            carry = dls[0:1, :]
            df = dls * _sigmoid(-(f_ref[sl, :] + b_ref[...]))
            df_ref[sl, :] = df.astype(df_ref.dtype)
            tot = tot + jnp.sum(df, axis=0, keepdims=True)
        dbias_ref[...] += tot

    return pl.pallas_call(
        body, name="fox_gate_bwd", grid=(batch,),
        in_specs=[pl.BlockSpec((seq, 128), lambda b: (b, 0)), pl.BlockSpec((1, 128), lambda b: (0, 0)),
                  pl.BlockSpec((seq, HW), lambda b: (b, 0)), pl.BlockSpec((seq, HW), lambda b: (b, 0))],
        out_specs=[pl.BlockSpec((seq, 128), lambda b: (b, 0)), pl.BlockSpec((1, 128), lambda b: (0, 0))],
        out_shape=[jax.ShapeDtypeStruct(f.shape, bf16), jax.ShapeDtypeStruct((1, 128), f32)],
        compiler_params=_cp(("arbitrary",)),
    )(f, bias, dc_a, dc_b)


_HBM_SPEC = pl.BlockSpec(memory_space=pltpu.HBM)


def _side_out_shapes(srcs, per_peer):
    return [jax.ShapeDtypeStruct(((N_DEV,) + tuple(s.shape[1:] if per_peer else s.shape)), s.dtype) for s in srcs]


def _side_sems(n):
    if n == 0:
        return []
    return [pltpu.SemaphoreType.DMA((n, N_DEV - 1)), pltpu.SemaphoreType.DMA((n, N_DEV - 1)), pltpu.SemaphoreType.DMA((n,))]


def _peer_copies(src_refs, dst_refs, per_peer, sems):
    send_sems, recv_sems, local_sems = sems
    x, y, c = lax.axis_index("x"), lax.axis_index("y"), lax.axis_index("c")
    me = 4 * x + 2 * y + c

    def remote(src, dst, t, k, to):
        return pltpu.make_async_remote_copy(src_ref=src, dst_ref=dst, send_sem=send_sems.at[t, k - 1],
                                            recv_sem=recv_sems.at[t, k - 1], device_id=to,
                                            device_id_type=pl.DeviceIdType.MESH)

    direct, relays = [], []
    for t, (s, d) in enumerate(zip(src_refs, dst_refs)):
        direct.append((t, 0, pltpu.make_async_copy(s.at[me] if per_peer else s, d.at[me], local_sems.at[t])))
        for k in range(1, N_DEV):
            px = 1 - x if k & 4 else x
            py = 1 - y if k & 2 else y
            pc = 1 - c if k & 1 else c
            if per_peer:
                direct.append((t, k, remote(s.at[4 * px + 2 * py + pc], d.at[me], t, k, (px, py, pc))))
            elif k == 1 or not k & 1:
                direct.append((t, k, remote(s, d.at[me], t, k, (px, py, pc))))
            else:
                origin = d.at[4 * px + 2 * py + c]
                relays.append((t, k - 1, remote(origin, origin, t, k, (x, y, 1 - c))))
    return direct, relays


def _exchange_start(direct):
    for _, _, cp in direct:
        cp.start()


def _exchange_finish(direct, relays):
    landed = {(t, k): cp for t, k, cp in direct}
    for t, j, cp in relays:
        landed[(t, j)].wait_recv()
        cp.start()
    relayed = {(t, j) for t, j, _ in relays}
    for t, k, cp in direct:
        if k == 0:
            cp.wait()
        else:
            cp.wait_send()
            if (t, k) not in relayed:
                cp.wait_recv()
    for _, _, cp in relays:
        cp.wait()


def _side_exchange(src_refs, dst_refs, per_peer, sems, *grid):
    if not src_refs:
        return
    first = functools.reduce(jnp.logical_and, [pl.program_id(a) == 0 for a in range(len(grid))])
    last = functools.reduce(jnp.logical_and, [pl.program_id(a) == n - 1 for a, n in enumerate(grid)])

    @pl.when(first)
    def _():
        _exchange_start(_peer_copies(src_refs, dst_refs, per_peer, sems)[0])

    @pl.when(last)
    def _():
        _exchange_finish(*_peer_copies(src_refs, dst_refs, per_peer, sems))


def _exchange(name, srcs, per_peer):
    n = len(srcs)

    def body(*refs):
        direct, relays = _peer_copies(refs[:n], refs[n:2 * n], per_peer, refs[2 * n:])
        _exchange_start(direct)
        _exchange_finish(direct, relays)

    return pl.pallas_call(
        body, name=name, in_specs=[_HBM_SPEC] * n, out_specs=[_HBM_SPEC] * n,
        out_shape=_side_out_shapes(srcs, per_peer), scratch_shapes=_side_sems(n),
    )(*srcs)


FOX_T = 512
_NEG = -1e30
_D2 = (((1,), (1,)), ((), ()))
_D1 = (((1,), (0,)), ((), ()))
_D0 = (((0,), (0,)), ((), ()))


def _bdot(a, b, dims):
    return lax.dot_general(a.astype(bf16), b.astype(bf16), dims, preferred_element_type=f32)


def _pick_lane(x, lane):
    idx = lax.broadcasted_iota(jnp.int32, x.shape, 1)
    return jnp.sum(jnp.where(idx == lane, x, 0.0), axis=1, keepdims=True)


def _pick_row(x, row):
    idx = lax.broadcasted_iota(jnp.int32, x.shape, 0)
    return jnp.sum(jnp.where(idx == row, x, 0.0), axis=0, keepdims=True)


def _fox_fwd(qkv, c, c_rows, batch, seq, side=None):
    t = min(FOX_T, seq)
    nq = seq // t
    scale = HD ** -0.5
    srcs, per_peer = side if side is not None else ([], False)
    n_s = len(srcs)

    def body(*refs):
        q_ref, k_ref, v_ref, cq_ref, ck_ref = refs[:5]
        o_ref, lse_ref = refs[5 + n_s:7 + n_s]
        _side_exchange(refs[5:5 + n_s], refs[7 + n_s:7 + 2 * n_s], per_peer, refs[7 + 2 * n_s:], batch, PAIRS, nq)
        pair, i = pl.program_id(1), pl.program_id(2)
        lane = lax.broadcasted_iota(jnp.int32, (1, PAIR_W), 1)
        first = (lane // HD) == 0
        mine = [first, jnp.logical_not(first)]
        q = q_ref[...] * scale
        qs = [jnp.where(mine[e], q, 0.0) for e in range(2)]
        cqs = [_pick_lane(cq_ref[...], 2 * pair + e) for e in range(2)]
        causal = lax.broadcasted_iota(jnp.int32, (t, t), 1) <= lax.broadcasted_iota(jnp.int32, (t, t), 0)

        def block(j, carry, diagonal):
            rows = pl.ds(pl.multiple_of(j * t, t), t)
            kj, vj = k_ref[rows, :], v_ref[rows, :]
            ck_blk = ck_ref[0, :, rows]
            out = []
            for e in range(2):
                m, acc = carry[2 * e:2 * e + 2]
                s = _bdot(qs[e], kj, _D2) + cqs[e] - _pick_row(ck_blk, 2 * pair + e)
                if diagonal:
                    s = jnp.where(causal, s, _NEG)
                m_new = jnp.maximum(m, jnp.max(s, axis=1, keepdims=True))
                p = jnp.exp(s - m_new)
                out += [m_new, jnp.exp(m - m_new) * acc + _bdot(p, jnp.where(mine[e], vj, 1.0), _D1)]
            return tuple(out)

        init = (jnp.full((t, 1), _NEG, f32), jnp.zeros((t, PAIR_W), f32)) * 2
        carry = lax.fori_loop(0, i, lambda j, cr: block(j, cr, False), init)
        m0, a0, m1, a1 = block(i, carry, True)
        l0, l1 = _pick_lane(a0, HD), _pick_lane(a1, 0)
        o_ref[...] = jnp.where(first, a0 / l0, a1 / l1)
        lse_ref[...] = jnp.where(lane == 0, m0 + jnp.log(l0), jnp.where(lane == 1, m1 + jnp.log(l1), 0.0))

    q_spec = pl.BlockSpec((t, PAIR_W), lambda b, p, i: (b * nq + i, p))
    res = pl.pallas_call(
        body, name="fox_attn_fwd", grid=(batch, PAIRS, nq),
        in_specs=[q_spec,
                  pl.BlockSpec((seq, PAIR_W), lambda b, p, i: (b, PAIRS + p)),
                  pl.BlockSpec((seq, PAIR_W), lambda b, p, i: (b, 2 * PAIRS + p)),
                  pl.BlockSpec((t, 128), lambda b, p, i: (b * nq + i, 0)),
                  pl.BlockSpec((1, 8, seq), lambda b, p, i: (b, 0, 0))] + [_HBM_SPEC] * n_s,
        out_specs=[q_spec, q_spec] + [_HBM_SPEC] * n_s,
        out_shape=[jax.ShapeDtypeStruct((batch * seq, HW), f32)] * 2 + _side_out_shapes(srcs, per_peer),
        scratch_shapes=_side_sems(n_s),
        compiler_params=_cp(("arbitrary", "arbitrary", "arbitrary")),
    )(qkv, qkv, qkv, c, c_rows, *srcs)
    return res[0], res[1], list(res[2:])


def _fox_bwd(qkv, c, c_rows, o, lse, do, batch, seq):
    t = min(FOX_T, seq)
    nq = seq // t
    scale = HD ** -0.5

    def body(q_ref, k_ref, v_ref, cq_ref, ck_ref, o_ref, lse_ref, do_ref,
             dq_ref, dk_ref, dv_ref, dcq_ref, dck_ref, acc0, acc1):
        pair, i = pl.program_id(1), pl.program_id(2)
        accs = [acc0, acc1]

        @pl.when(i == 0)
        def _():
            dv_ref[...] = jnp.zeros_like(dv_ref)
            acc0[...] = jnp.zeros_like(acc0)
            acc1[...] = jnp.zeros_like(acc1)

        lane = lax.broadcasted_iota(jnp.int32, (1, PAIR_W), 1)
        first = (lane // HD) == 0
        mine = [first, jnp.logical_not(first)]
        q, d_o, o_i = q_ref[...] * scale, do_ref[...], o_ref[...]
        q0s = [jnp.where(mine[e], q, 0.0) for e in range(2)]
        q1s = [jnp.where(mine[e], q, 1.0) for e in range(2)]
        dos = [jnp.where(mine[e], d_o, 0.0) for e in range(2)]
        deltas = [jnp.sum(dos[e] * o_i, axis=1, keepdims=True) for e in range(2)]
        lses = [_pick_lane(lse_ref[...], e) for e in range(2)]
        cqs = [_pick_lane(cq_ref[...], 2 * pair + e) for e in range(2)]
        causal = lax.broadcasted_iota(jnp.int32, (t, t), 1) <= lax.broadcasted_iota(jnp.int32, (t, t), 0)

        def block(j, dqs, diagonal):
            rows = pl.ds(pl.multiple_of(j * t, t), t)
            kj, vj = k_ref[rows, :], v_ref[rows, :]
            ck_blk = ck_ref[0, :, rows]
            out = []
            for e in range(2):
                s = _bdot(q0s[e], kj, _D2) + cqs[e] - _pick_row(ck_blk, 2 * pair + e)
                if diagonal:
                    s = jnp.where(causal, s, _NEG)
                p = jnp.exp(s - lses[e])
                ds = p * (_bdot(dos[e], vj, _D2) - deltas[e])
                dv_ref[rows, :] += _bdot(p, dos[e], _D0)
                accs[e][rows, :] += _bdot(ds, q1s[e], _D0)
                out.append(dqs[e] + _bdot(ds, jnp.where(mine[e], kj, 1.0), _D1))
            return tuple(out)

        zero = jnp.zeros((t, PAIR_W), f32)
        dqs = lax.fori_loop(0, i, lambda j, cr: block(j, cr, False), (zero, zero))
        dq0, dq1 = block(i, dqs, True)
        dq_ref[...] = jnp.where(first, dq0, dq1) * scale
        dcq_ref[...] = jnp.where(lane == 0, _pick_lane(dq0, HD), jnp.where(lane == 1, _pick_lane(dq1, 0), 0.0))

        @pl.when(i == nq - 1)
        def _():
            a0, a1 = acc0[...], acc1[...]
            dk_ref[...] = jnp.where(first, a0, a1)
            dck_ref[...] = jnp.where(lane == 0, -_pick_lane(a0, HD), jnp.where(lane == 1, -_pick_lane(a1, 0), 0.0))

    blk = lambda col: pl.BlockSpec((t, PAIR_W), lambda b, p, i: (b * nq + i, col * PAIRS + p))
    whole = lambda col: pl.BlockSpec((seq, PAIR_W), lambda b, p, i: (b, col * PAIRS + p))
    t_all = batch * seq
    return pl.pallas_call(
        body, name="fox_attn_bwd", grid=(batch, PAIRS, nq),
        in_specs=[blk(0), whole(1), whole(2),
                  pl.BlockSpec((t, 128), lambda b, p, i: (b * nq + i, 0)),
                  pl.BlockSpec((1, 8, seq), lambda b, p, i: (b, 0, 0)),
                  blk(0), blk(0), blk(0)],
        out_specs=[blk(0), whole(0), whole(0), blk(0), whole(0)],
        out_shape=[jax.ShapeDtypeStruct((t_all, HW), f32)] * 5,
        scratch_shapes=[pltpu.VMEM((seq, PAIR_W), f32), pltpu.VMEM((seq, PAIR_W), f32)],
        compiler_params=_cp(("parallel", "parallel", "arbitrary")),
    )(qkv, qkv, qkv, c, c_rows, o, lse, do)


def _mem_block(q, km, vm):
    nn, nt, _ = _make_mm(False, False)
    logits = nt(q, km) * (MEM_HD ** -0.5)
    m = lax.stop_gradient(jnp.max(logits, axis=-1, keepdims=True))
    e = jnp.exp(logits - m)
    return nn(e / jnp.sum(e, axis=-1, keepdims=True), vm)


def _mem_specs(seq, tq):
    nq = seq // tq
    qs = pl.BlockSpec((tq, MEM_HD), lambda b, h, i: (b * nq + i, h))
    ks = pl.BlockSpec((MEM_LEN, MEM_HD), lambda b, h, i: (b, h))
    vs = pl.BlockSpec((MEM_LEN, MEM_HD), lambda b, h, i: (b, MEM_HEADS + h))
    return nq, qs, ks, vs


def _mem_fwd(q, mem_kv, batch, seq):
    tq = min(512, seq)
    nq, qs, ks, vs = _mem_specs(seq, tq)

    def body(q_ref, k_ref, v_ref, o_ref):
        o_ref[...] = _mem_block(q_ref[...].astype(f32), k_ref[...], v_ref[...]).astype(o_ref.dtype)

    return pl.pallas_call(
        body, name="mem_attn_fwd", grid=(batch, MEM_HEADS, nq),
        in_specs=[qs, ks, vs], out_specs=qs, out_shape=jax.ShapeDtypeStruct(q.shape, bf16),
        compiler_params=_cp(("parallel", "parallel", "arbitrary")),
    )(q, mem_kv, mem_kv)


def _mem_bwd(q, mem_kv, do, batch, seq):
    tq = min(512, seq)
    nq, qs, ks, vs = _mem_specs(seq, tq)

    def body(q_ref, k_ref, v_ref, do_ref, dq_ref, dk_ref, dv_ref):
        _, vjp = jax.vjp(_mem_block, q_ref[...].astype(f32), k_ref[...], v_ref[...])
        dq, dk, dv = vjp(do_ref[...])
        dq_ref[...] = dq.astype(dq_ref.dtype)

        @pl.when(pl.program_id(2) == 0)
        def _():
            dk_ref[...] = jnp.zeros_like(dk_ref)
            dv_ref[...] = jnp.zeros_like(dv_ref)

        dk_ref[...] += dk
        dv_ref[...] += dv

    return pl.pallas_call(
        body, name="mem_attn_bwd", grid=(batch, MEM_HEADS, nq),
        in_specs=[qs, ks, vs, qs], out_specs=[qs, ks, ks],
        out_shape=[jax.ShapeDtypeStruct(q.shape, bf16), jax.ShapeDtypeStruct((batch * MEM_LEN, MEM_W), f32),
                   jax.ShapeDtypeStruct((batch * MEM_LEN, MEM_W), f32)],
        compiler_params=_cp(("parallel", "parallel", "arbitrary")),
    )(q, mem_kv, mem_kv, do)


@jax.custom_vjp
def _halves(x):
    c = x.shape[1] // 2
    return x[:, :c], x[:, c:]


_halves.defvjp(lambda x: ((x[:, :x.shape[1] // 2], x[:, x.shape[1] // 2:]), None),
               lambda _, g: (jnp.concatenate(g, axis=1),))


@jax.custom_vjp
def _lead_halves(x):
    n = x.shape[0] // 2
    return x[:n], x[n:]


_lead_halves.defvjp(lambda x: ((x[:x.shape[0] // 2], x[x.shape[0] // 2:]), None),
                    lambda _, g: (jnp.concatenate(g, axis=0),))


def _scan_chunk(s0, r, wl, k, v, a, b):
    nn, nt, tn = _make_mm(True, False)
    nn_exact, _, _ = _make_mm(True, True)
    _, nt_exact, _ = _make_mm(True, "split")
    hp, c, lanes = r.shape
    row = lax.broadcasted_iota(jnp.int32, (c, c), 0)
    col = lax.broadcasted_iota(jnp.int32, (c, c), 1)
    first = (lax.broadcasted_iota(jnp.int32, (1, 1, lanes), 2) // HD) == 0
    tri = jnp.broadcast_to((col <= row).astype(f32)[None], (hp, c, c))
    lg = nn_exact(tri, wl)
    lg_end = lg[:, c - 1:c, :]
    grow, shrink, to_end = jnp.exp(lg), jnp.exp(-lg), jnp.exp(lg_end - lg)
    rt, kt, bt, at = r * grow, k * shrink, b * shrink, a * jnp.exp(lg - wl)
    strict, incl = (col < row)[None], (col <= row)[None]
    twice = lambda t: jnp.concatenate([t, t], axis=0)
    queries = jnp.concatenate([at, rt], axis=1)
    per_head = jnp.concatenate([jnp.where(first, queries, 0.0), jnp.where(first, 0.0, queries)], axis=0)
    (ab, rb), (ak, rk) = _halves(nt_exact(per_head, twice(bt))), _halves(nt_exact(per_head, twice(kt)))
    l_ab = jnp.where(strict, ab, 0.0)
    a_ak = jnp.where(strict, ak, 0.0)
    a_rb = jnp.where(incl, rb, 0.0)
    a_rk = jnp.where(incl, rk, 0.0)
    inv = (col == row).astype(f32)[None] + l_ab
    power, n = l_ab, 1
    while 2 * n < c:
        power = nn(power, power)
        inv = inv + nn(inv, power)
        n *= 2

    def apply(m, t):
        lo, hi = _lead_halves(nn(m, twice(t)))
        return jnp.where(first, lo, hi)

    sa = apply(inv, nt(at, s0) + apply(a_ak, v))
    y = nt(rt, s0) + apply(a_rk, v) + apply(a_rb, sa)
    same_head = ((lax.broadcasted_iota(jnp.int32, (lanes, lanes), 0) // HD)
                 == (lax.broadcasted_iota(jnp.int32, (lanes, lanes), 1) // HD))[None]
    s1 = s0 * jnp.exp(lg_end) + jnp.where(same_head, tn(v, k * to_end) + tn(sa, b * to_end), 0.0)
    return y, s1


PAIRS = HEADS // 2
PAIR_W = 2 * HD


def _pair_stack(ref, off):
    return jnp.stack([ref[b, :, off + p * PAIR_W:off + (p + 1) * PAIR_W]
                      for b in range(ref.shape[0]) for p in range(PAIRS)])


def _pair_store(ref, off, val, add_ref=None):
    for b in range(ref.shape[0]):
        for p in range(PAIRS):
            sl = slice(off + p * PAIR_W, off + (p + 1) * PAIR_W)
            v = val[b * PAIRS + p]
            ref[b, :, sl] = v if add_ref is None else v + add_ref[b, :, sl]


def _scan_fwd(main6, batch, seq, side=None):
    c = min(SCAN_CHUNK, seq)
    nc = seq // c
    hp = batch * PAIRS
    srcs, per_peer = side if side is not None else ([], False)
    n_s = len(srcs)

    def body(*refs):
        z_ref, y_ref, s_ref, st = refs[0], refs[1 + n_s], refs[2 + n_s], refs[3 + 2 * n_s]
        _side_exchange(refs[1:1 + n_s], refs[3 + n_s:3 + 2 * n_s], per_peer, refs[4 + 2 * n_s:], nc)

        @pl.when(pl.program_id(0) == 0)
        def _():
            st[...] = jnp.zeros_like(st)

        s0 = st[...]
        s_ref[0] = s0
        y, s1 = _scan_chunk(s0, *[_pair_stack(z_ref, comp * HW) for comp in range(6)])
        _pair_store(y_ref, 0, y)
        st[...] = s1

    res = pl.pallas_call(
        body, name="rwkv_scan_fwd", grid=(nc,),
        in_specs=[pl.BlockSpec((batch, c, 6 * HW), lambda i: (0, i, 0))] + [_HBM_SPEC] * n_s,
        out_specs=[pl.BlockSpec((batch, c, HW), lambda i: (0, i, 0)),
                   pl.BlockSpec((1, hp, PAIR_W, PAIR_W), lambda i: (i, 0, 0, 0))] + [_HBM_SPEC] * n_s,
        out_shape=[jax.ShapeDtypeStruct((batch, seq, HW), f32), jax.ShapeDtypeStruct((nc, hp, PAIR_W, PAIR_W), f32)]
        + _side_out_shapes(srcs, per_peer),
        scratch_shapes=[pltpu.VMEM((hp, PAIR_W, PAIR_W), f32)] + _side_sems(n_s),
        compiler_params=_cp(("arbitrary",)),
    )(main6.reshape(batch, seq, 6 * HW), *srcs)
    return res[0].reshape(batch * seq, HW), res[1], list(res[2:])


def _scan_bwd(main6, states, dy, extra, batch, seq, side=None):
    c = min(SCAN_CHUNK, seq)
    nc = seq // c
    hp = batch * PAIRS
    srcs, per_peer = side if side is not None else ([], False)
    n_s = len(srcs)

    def body(*refs):
        z_ref, s_ref, dy_ref, ex_ref = refs[:4]
        dz_ref, dst = refs[4 + n_s], refs[5 + 2 * n_s]
        _side_exchange(refs[4:4 + n_s], refs[5 + n_s:5 + 2 * n_s], per_peer, refs[6 + 2 * n_s:], nc)

        @pl.when(pl.program_id(0) == 0)
        def _():
            dst[...] = jnp.zeros_like(dst)

        _, vjp = jax.vjp(_scan_chunk, s_ref[0], *[_pair_stack(z_ref, comp * HW) for comp in range(6)])
        g = vjp((_pair_stack(dy_ref, 0), dst[...]))
        dst[...] = g[0]
        for comp in range(6):
            _pair_store(dz_ref, comp * HW, g[1 + comp], ex_ref)

    back = lambda i: (0, nc - 1 - i, 0)
    wide = pl.BlockSpec((batch, c, 6 * HW), back)
    res = pl.pallas_call(
        body, name="rwkv_scan_bwd", grid=(nc,),
        in_specs=[wide, pl.BlockSpec((1, hp, PAIR_W, PAIR_W), lambda i: (nc - 1 - i, 0, 0, 0)),
                  pl.BlockSpec((batch, c, HW), back), wide] + [_HBM_SPEC] * n_s,
        out_specs=[wide] + [_HBM_SPEC] * n_s,
        out_shape=[jax.ShapeDtypeStruct((batch, seq, 6 * HW), f32)] + _side_out_shapes(srcs, per_peer),
        scratch_shapes=[pltpu.VMEM((hp, PAIR_W, PAIR_W), f32)] + _side_sems(n_s),
        compiler_params=_cp(("arbitrary",)),
    )(main6.reshape(batch, seq, 6 * HW), states, dy.reshape(batch, seq, HW), extra.reshape(batch, seq, 6 * HW), *srcs)
    return res[0].reshape(batch * seq, 6 * HW), list(res[1:])


def _to_heads(x, batch, seq, k):
    return x.reshape(batch, seq, k, HEADS, HD).transpose(2, 0, 3, 1, 4).reshape(k, batch * HEADS, seq, HD)


def _from_heads(x, batch, seq, k):
    return x.reshape(k, batch, HEADS, seq, HD).transpose(1, 3, 0, 2, 4).reshape(batch * seq, k * HW)


def _pad_cols(x, width):
    return jnp.pad(x, ((0, 0), (0, width - x.shape[1])))


def _split_w_in(wt):
    z = lambda rows: jnp.zeros((rows, wt.shape[1]), wt.dtype)
    w_r = jnp.concatenate([wt[1544:3080], wt[3080:3144], z(64), wt[3144:3208], z(64), wt[3208:3336]], axis=0)
    return wt[:1536], jnp.concatenate([wt[1536:1544], z(120)], axis=0), w_r, wt[3336:3848], wt[3848:]


def _merge_w_in(g_qkv, g_f, g_r, g_mq, g_g):
    return jnp.concatenate([g_qkv, g_f[:8], g_r[:1536], g_r[1536:1600], g_r[1664:1728], g_r[1792:], g_mq, g_g], axis=0)


def _pad_lora(v):
    z64 = jnp.zeros((1, 64), v.dtype)
    return jnp.concatenate([v[:, :1536], v[:, 1536:1600], z64, v[:, 1600:1664], z64, v[:, 1664:]], axis=1)


def _unpad_lora(v):
    return jnp.concatenate([v[:, :1536], v[:, 1536:1600], v[:, 1664:1728], v[:, 1792:]], axis=1)


def _local_step(x, mem, target, w, p, late=None, early=None, last=None):
    batch, seq, _ = x.shape
    t = batch * seq
    x2, tg2, mem2 = x.reshape(t, D), target.reshape(t, D), mem.reshape(batch * MEM_LEN, D)
    w_qkv, w_f, w_r, w_mq, w_g3 = _split_w_in(w["w_in"])
    mu = _pad_lora(p["rwkv_mu"])
    bias = _pad_cols(p["fox_f_bias"], 128)
    r_k = p["rwkv_r_k"].reshape(1, HW)
    post_params = [p["rwkv_gn_g"], p["rwkv_gn_b"], r_k]
    rw_widths = [HW, HW, HW, LORA_PAD, LORA_PAD, LORA_PAD]
    six = [HW] * 6

    (u,) = _rows_fwd("rms_pre1", _fn_rms, [], [(x2, [D])], [p["pre1_g"]], [[D]], dtypes=[bf16])
    p_qkv = _matmul("proj_qkv", u, w_qkv, "nt", out_dtype=bf16)
    p_f = _matmul("proj_f", u, w_f, "nt")
    p_r = _matmul("proj_rwkv", u, w_r, "nt")
    p_mq = _matmul("proj_memq", u, w_mq, "nt", out_dtype=bf16)
    p_g = _matmul("proj_gate", u, w_g3, "nt", out_dtype=bf16)

    c = _fox_gate_fwd(p_f, bias, batch, seq)
    c_rows = c[:, :HEADS].reshape(batch, seq, HEADS).transpose(0, 2, 1)
    fox_o, lse, gathered = _fox_fwd(p_qkv, c, c_rows, batch, seq, side=(late[0], False) if late else None)
    if late:
        w = {**w, **late[2](gathered, 0)}
    fox_out = fox_o.astype(bf16)

    w_up = jnp.pad(w["rwkv_w_up"].astype(f32), ((0, LORA_PAD - 64), (0, 0)))
    a_up = jnp.pad(w["rwkv_a_up"].astype(f32), ((0, LORA_PAD - 64), (0, 0)))
    pre_params = [p["rwkv_w0"], w_up, p["rwkv_a0"], a_up, w["rwkv_g_up"].astype(f32), p["rwkv_k_k"], p["rwkv_k_a"]]
    ps = _tokshift_fwd(p_r, mu, batch, seq)
    main6, g_rw = _rows_fwd("rwkv_pre", _fn_rwkv_pre, [], [(ps, rw_widths)], pre_params, [six, [HW]], tm=256)
    y_rw, states, gathered = _scan_fwd(main6, batch, seq, side=(late[1], False) if late else None)
    if late:
        w = {**w, **late[2](gathered, 1)}
    post_consts = []
    post_rows = [(y_rw, [HW]), (main6, six), (g_rw, [HW])]

    def fn_post(y, r, _wl, k2, v, _a, _b, g, gn_g, gn_b, rk):
        return _fn_rwkv_post(y, r, k2, v, g, gn_g, gn_b, rk)

    (rwkv_out,) = _rows_fwd("rwkv_post", fn_post, post_consts, post_rows, post_params, [[HW]], dtypes=[bf16], tm=256)

    (memn,) = _rows_fwd("rms_mem", _fn_rms, [], [(mem2, [D])], [p["mem_norm_g"]], [[D]], dtypes=[bf16])
    mem_kv = _matmul("proj_memkv", memn, w["w_mem_kv"], "nn")
    mem_out = _mem_fwd(p_mq, mem_kv, batch, seq)

    a_fox = _matmul("out_fox", fox_out, w["w_fox_out"], "nn", out_dtype=bf16)
    a_rwkv = _matmul("out_rwkv", rwkv_out, w["w_rwkv_out"], "nn", out_dtype=bf16)
    a_mem = _matmul("out_mem", mem_out, w["w_mem_out"], "nn", out_dtype=bf16)
    merge_rows = [(a_fox, [D]), (a_rwkv, [D]), (a_mem, [D]), (p_g, [D, D, D])]
    (merged,) = _rows_fwd("merge", _fn_merge, [], merge_rows, [], [[D]], dtypes=[bf16])
    yy = _matmul("out_o", merged, w["w_o"], "nn")
    post1_rows = [(yy, [D]), (x2, [D])]
    post1_params = [p["post1_g"], p["pre2_g"]]
    h1, u2 = _rows_fwd("post1", _fn_post1, [], post1_rows, post1_params, [[D], [D]], dtypes=[f32, bf16])
    gp = _matmul("ffn_gate", u2, w["w_ffn_gate"], "nt", out_dtype=bf16)
    up = _matmul("ffn_up", u2, w["w_ffn_up"], "nt", out_dtype=bf16)
    (hmid,) = _rows_fwd("swiglu", _fn_swiglu, [], [(gp, [D_FF]), (up, [D_FF])], [], [[D_FF]], dtypes=[bf16])
    ffn = _matmul("ffn_down", hmid, w["w_ffn_down"], "nn")
    final_rows = [(ffn, [D]), (h1, [D])]
    (loss,) = _rows_fwd("final", _fn_final, [(tg2, [D])], final_rows, [p["post2_g"]], [], n_sums=1)

    gw, gp_ = {}, {}
    (d_ffn, d_h1), (gp_["post2_g"],) = _rows_bwd("final_bwd", _fn_final, [(tg2, [D])], final_rows, [p["post2_g"]], [], [],
                                                  n_sums=1, dtypes=[bf16, f32])
    d_hmid = _matmul("ffn_down_dx", d_ffn, w["w_ffn_down"], "nt", out_dtype=bf16)
    gw["w_ffn_down"] = _matmul("ffn_down_dw", hmid, d_ffn, "tn", out_dtype=bf16)
    (d_gp, d_up), _ = _rows_bwd("swiglu_bwd", _fn_swiglu, [], [(gp, [D_FF]), (up, [D_FF])], [], [[D_FF]], [d_hmid],
                                dtypes=[bf16, bf16])
    d_u2 = _matmul("ffn_gate_dx", d_gp, w["w_ffn_gate"], "nn")
    d_u2 = _matmul("ffn_up_dx", d_up, w["w_ffn_up"], "nn", add=d_u2)
    gw["w_ffn_gate"] = _matmul("ffn_gate_dw", d_gp, u2, "tn", out_dtype=bf16)
    gw["w_ffn_up"] = _matmul("ffn_up_dw", d_up, u2, "tn", out_dtype=bf16)
    (d_yy, d_x_res), (gp_["post1_g"], gp_["pre2_g"]) = _rows_bwd(
        "post1_bwd", _fn_post1, [], post1_rows, post1_params, [[D], [D]], [d_h1, d_u2], dtypes=[bf16, f32])
    d_merged = _matmul("out_o_dx", d_yy, w["w_o"], "nt", out_dtype=bf16)
    gw["w_o"] = _matmul("out_o_dw", merged, d_yy, "tn", out_dtype=bf16)
    (d_a_fox, d_a_rwkv, d_a_mem, d_p_g), _ = _rows_bwd("merge_bwd", _fn_merge, [], merge_rows, [], [[D]], [d_merged],
                                                       dtypes=[bf16] * 4)
    d_fox_out = _matmul("out_fox_dx", d_a_fox, w["w_fox_out"], "nt")
    gw["w_fox_out"] = _matmul("out_fox_dw", fox_out, d_a_fox, "tn", out_dtype=bf16)
    d_rwkv_out = _matmul("out_rwkv_dx", d_a_rwkv, w["w_rwkv_out"], "nt")
    gw["w_rwkv_out"] = _matmul("out_rwkv_dw", rwkv_out, d_a_rwkv, "tn", out_dtype=bf16)
    d_mem_out = _matmul("out_mem_dx", d_a_mem, w["w_mem_out"], "nt")
    gw["w_mem_out"] = _matmul("out_mem_dw", mem_out, d_a_mem, "tn", out_dtype=bf16)

    d_p_mq, d_km, d_vm = _mem_bwd(p_mq, mem_kv, d_mem_out, batch, seq)
    d_mem_kv = jnp.concatenate([d_km, d_vm], axis=1).astype(bf16)
    gw["w_mem_kv"] = _matmul("proj_memkv_dw", memn, d_mem_kv, "tn", out_dtype=bf16)
    d_memn = _matmul("proj_memkv_dx", d_mem_kv, w["w_mem_kv"], "nt")
    _, (gp_["mem_norm_g"],) = _rows_bwd("rms_mem_bwd", _fn_rms, [], [(mem2, [D])], [p["mem_norm_g"]], [[D]], [d_memn])

    d_q, d_k, d_v, d_cq, d_ck = _fox_bwd(p_qkv, c, c_rows, fox_o, lse, d_fox_out, batch, seq)
    d_p_qkv = jnp.concatenate([d_q, d_k, d_v], axis=1).astype(bf16)
    d_p_f, d_bias = _fox_gate_bwd(p_f, bias, d_cq, d_ck, batch, seq)
    gp_["fox_f_bias"] = d_bias[:, :HEADS]

    (d_y_rw, d_main6_post, d_g_rw), (gp_["rwkv_gn_g"], gp_["rwkv_gn_b"], d_rk) = _rows_bwd(
        "rwkv_post_bwd", fn_post, post_consts, post_rows, post_params, [[HW]], [d_rwkv_out], tm=256)
    gp_["rwkv_r_k"] = d_rk.reshape(1, HEADS, HD)
    d_main6, early_got = _scan_bwd(main6, states, d_y_rw, d_main6_post, batch, seq,
                                   side=(early(gw), True) if early else None)

    def fn_pre_sum(*args):
        return _fn_rwkv_pre(*args)

    (d_ps,), d_pre = _rows_bwd("rwkv_pre_bwd", fn_pre_sum, [], [(ps, rw_widths)], pre_params, [six, [HW]],
                               [d_main6, d_g_rw], tm=256)
    gp_["rwkv_w0"], d_w_up, gp_["rwkv_a0"], d_a_up, gw["rwkv_g_up"], gp_["rwkv_k_k"], gp_["rwkv_k_a"] = d_pre
    gw["rwkv_w_up"], gw["rwkv_a_up"] = d_w_up[:64], d_a_up[:64]
    d_p_r, d_mu = _tokshift_bwd(p_r, mu, d_ps, batch, seq)
    gp_["rwkv_mu"] = _unpad_lora(d_mu)

    gw["w_in"] = _merge_w_in(_matmul("proj_qkv_dw", d_p_qkv, u, "tn", out_dtype=bf16), _matmul("proj_f_dw", d_p_f, u, "tn", out_dtype=bf16),
                             _matmul("proj_rwkv_dw", d_p_r, u, "tn", out_dtype=bf16), _matmul("proj_memq_dw", d_p_mq, u, "tn", out_dtype=bf16),
                             _matmul("proj_gate_dw", d_p_g, u, "tn", out_dtype=bf16))
    d_u, last_got = _sum_nn("proj_dx", [d_p_qkv, d_p_f, d_p_r, d_p_mq, d_p_g], [w_qkv, w_f, w_r, w_mq, w_g3],
                            side=(last(gw), True) if last else None)
    (d_x,), (gp_["pre1_g"],) = _rows_bwd("rms_pre1_bwd", _fn_rms, [], [(x2, [D])], [p["pre1_g"]], [[D]], [d_u], add=d_x_res)
    return loss, d_x.reshape(x.shape), gw, gp_, early_got, last_got


def _rows_add(name, a, b):
    (s,) = _rows_fwd(name, lambda u, v: (u + v,), [], [(a, [a.shape[1]]), (b, [b.shape[1]])], [], [[a.shape[1]]])
    return s


def _adamw(name, recv, row_off, w, m, v):
    _, rows, cols = w.shape
    row_tiles = [t for t in range(16, min(rows, 128) + 1, 16) if rows % t == 0 and row_off % t == 0]
    if row_tiles:
        tr, tc = max(row_tiles), cols
        first, grid = row_off // tr, (rows // tr,)
        at = lambda i: (0, first + i, 0)
        mine = lambda i: (0, i, 0)
    else:
        assert row_off == 0 and recv.shape[1] == rows
        tr, tc = rows, 128
        grid = (cols // tc,)
        at = mine = lambda i: (0, 0, i)

    def body(g_ref, w_ref, m_ref, v_ref, go_ref, d_ref, mo_ref, vo_ref):
        g = g_ref[0].astype(f32)
        for s in range(1, N_DEV):
            g = g + g_ref[s].astype(f32)
        m_new = ADAM_B1 * m_ref[0] + (1.0 - ADAM_B1) * g
        v_new = ADAM_B2 * v_ref[0] + (1.0 - ADAM_B2) * (g * g)
        m_hat = m_new / (1.0 - ADAM_B1 ** ADAM_STEP)
        v_hat = v_new / (1.0 - ADAM_B2 ** ADAM_STEP)
        go_ref[0] = g
        d_ref[0] = -ADAM_LR * (m_hat / (jnp.sqrt(v_hat) + ADAM_EPS) + ADAM_WD * w_ref[0])
        mo_ref[0] = m_new
        vo_ref[0] = v_new

    spec = pl.BlockSpec((1, tr, tc), mine)
    return pl.pallas_call(
        body, name=name, grid=grid,
        in_specs=[pl.BlockSpec((N_DEV, tr, tc), at), spec, spec, spec],
        out_specs=[spec] * 4, out_shape=[jax.ShapeDtypeStruct(w.shape, f32)] * 4,
        compiler_params=_cp(("parallel",)),
    )(recv, w, m, v)


GROUPS = (
    ("in", ("w_in",), 0),
    ("memkv", ("w_mem_kv",), 0),
    ("ffn_gu", ("w_ffn_gate", "w_ffn_up"), 0),
    ("down_o", ("w_ffn_down", "w_o"), 0),
    ("outs", ("w_fox_out", "w_rwkv_out", "w_mem_out"), 0),
    ("lora", ("rwkv_w_up", "rwkv_a_up", "rwkv_g_up"), 0),
)
FIRST_GROUPS = ("in", "memkv")
LATE_GROUPS = (("down_o", "outs", "lora"), ("ffn_gu",))
EARLY_GRAD_GROUPS = ("memkv", "ffn_gu", "down_o", "outs")
LAST_GRAD_GROUPS = ("in", "lora")
SHARD_AXIS = {n: a for n, _, a in SHARDED}
SMALL_ROWS = 16


def _group_local(shards, members, join):
    parts = [shards[n].reshape(shards[n].shape[-2:]) for n in members]
    return parts[0] if len(parts) == 1 else jnp.concatenate(parts, axis=join)


def _group_split(arr, members, join, lead=False):
    out, off = {}, 0
    for n in members:
        shape = dict((k, s) for k, s, _ in SHARDED)[n]
        size = _block_shape(shape, SHARD_AXIS[n])[join]
        idx = [slice(None)] * arr.ndim
        idx[arr.ndim - 2 + join] = slice(off, off + size)
        out[n] = arr[tuple(idx)]
        off += size
    return out


def _full_from_blocks(blocks, axis):
    if axis == 0:
        return blocks.reshape(-1, blocks.shape[2])
    return blocks.transpose(1, 0, 2).reshape(blocks.shape[1], -1)


def _blocks_from_full(full, axis):
    if axis == 0:
        return full.reshape(N_DEV, -1, full.shape[1])
    return full.reshape(full.shape[0], N_DEV, -1).transpose(1, 0, 2)


def _assemble(gathered, names):
    out = {}
    for arr, g in zip(gathered, names):
        _, members, join = [grp for grp in GROUPS if grp[0] == g][0]
        for n, blk in _group_split(arr, members, join, lead=True).items():
            out[n] = _full_from_blocks(blk, SHARD_AXIS[n])
    return out


def _grad_blocks(gw, names):
    out = []
    for g in names:
        _, members, join = [grp for grp in GROUPS if grp[0] == g][0]
        parts = [_blocks_from_full(gw[n].astype(bf16), SHARD_AXIS[n]) for n in members]
        out.append(parts[0] if len(parts) == 1 else jnp.concatenate(parts, axis=1 + join))
    return out


def _small_pack(d):
    flat = jnp.concatenate([d[n].reshape(-1) for n, _ in REPLICATED])
    return jnp.pad(flat, (0, SMALL_ROWS * LANES - REPL_ELEMS)).reshape(SMALL_ROWS, LANES)


def _small_unpack(packed):
    out, flat, off = {}, packed.reshape(-1), 0
    for n, shape in REPLICATED:
        k = _rows_of((LANES,) + shape)
        out[n] = flat[off:off + k].reshape(shape)
        off += k
    return out


def kernel(x, mem, pre1_g, post1_g, pre2_g, post2_g, mem_norm_g, w_in, fox_f_bias, rwkv_mu, rwkv_w0, rwkv_w_up, rwkv_a0, rwkv_a_up, rwkv_g_up, rwkv_k_k, rwkv_k_a, rwkv_r_k, rwkv_gn_g, rwkv_gn_b, w_mem_kv, w_fox_out, w_rwkv_out, w_mem_out, w_o, w_ffn_gate, w_ffn_up, w_ffn_down, loss_target, m_pre1_g, m_post1_g, m_pre2_g, m_post2_g, m_mem_norm_g, m_w_in, m_fox_f_bias, m_rwkv_mu, m_rwkv_w0, m_rwkv_w_up, m_rwkv_a0, m_rwkv_a_up, m_rwkv_g_up, m_rwkv_k_k, m_rwkv_k_a, m_rwkv_r_k, m_rwkv_gn_g, m_rwkv_gn_b, m_w_mem_kv, m_w_fox_out, m_w_rwkv_out, m_w_mem_out, m_w_o, m_w_ffn_gate, m_w_ffn_up, m_w_ffn_down, v_pre1_g, v_post1_g, v_pre2_g, v_post2_g, v_mem_norm_g, v_w_in, v_fox_f_bias, v_rwkv_mu, v_rwkv_w0, v_rwkv_w_up, v_rwkv_a0, v_rwkv_a_up, v_rwkv_g_up, v_rwkv_k_k, v_rwkv_k_a, v_rwkv_r_k, v_rwkv_gn_g, v_rwkv_gn_b, v_w_mem_kv, v_w_fox_out, v_w_rwkv_out, v_w_mem_out, v_w_o, v_w_ffn_gate, v_w_ffn_up, v_w_ffn_down):
    args = dict(locals())
    turn = lambda n, a: jnp.swapaxes(a, 1, 2) if n in TRANSPOSED else a
    wts = {n: turn(n, args[n]) for n in WEIGHT_ORDER}
    ms = {n: turn(n, args["m_" + n]) for n in WEIGHT_ORDER}
    vs = {n: turn(n, args["v_" + n]) for n in WEIGHT_ORDER}

    groups = {g: (members, join) for g, members, join in GROUPS}
    w_bf16 = {n: wts[n].astype(bf16) for n, _, _ in SHARDED}

    def send(g):
        return _group_local(w_bf16, *groups[g])

    first = _exchange("gather_first", [send(g) for g in FIRST_GROUPS], per_peer=False)
    full = _assemble(first, FIRST_GROUPS)
    small_in = {n: (wts[n] if n == "rwkv_r_k" else wts[n].reshape(wts[n].shape[-2:])) for n, _ in REPLICATED}
    late = ([send(g) for g in LATE_GROUPS[0]], [send(g) for g in LATE_GROUPS[1]],
            lambda got, which: _assemble(got, LATE_GROUPS[which]))
    loss_part, grad_x, gw, gp, early_got, last_got = _local_step(
        x, mem, loss_target, full, small_in, late=late, early=lambda g: _grad_blocks(g, EARLY_GRAD_GROUPS),
        last=lambda g: _grad_blocks(g, LAST_GRAD_GROUPS))
    (small_got,) = _exchange("exchange_small", [_small_pack(gp).astype(bf16)], per_peer=False)
    received = dict(zip(EARLY_GRAD_GROUPS + LAST_GRAD_GROUPS, list(early_got) + list(last_got)))

    outs = [{}, {}, {}, {}]
    for g, members, _ in GROUPS:
        off = 0
        for n in members:
            for o, arr in zip(outs, _adamw("adamw_" + n, received[g], off, wts[n], ms[n], vs[n])):
                o[n] = arr
            off += wts[n].shape[1]
    res = _adamw("adamw_small", small_got, 0, *[_small_pack(d)[None] for d in (wts, ms, vs)])
    for o, arr in zip(outs, res):
        o.update(_small_unpack(arr))
    loss = lax.psum(loss_part[0, 0], ("x", "y", "c"))
    return (loss, grad_x, *[turn(n, o[n].reshape(wts[n].shape)) for o in outs for n in WEIGHT_ORDER])
```

```python
import functools

import jax
import jax.numpy as jnp
from jax import lax
from jax.experimental import pallas as pl
from jax.experimental.pallas import tpu as pltpu

f32 = jnp.float32
bf16 = jnp.bfloat16
_HI = lax.Precision.HIGHEST

D = 1024
HEADS = 8
HD = 64
HW = HEADS * HD
MEM_HEADS = 4
MEM_HD = 128
MEM_W = 512
MEM_LEN = 256
D_FF = 2816
LORA_PAD = 128
RW_COLS = 3 * HW + 3 * LORA_PAD
NORM_EPS = 1e-6
GN_EPS = 64e-5
Q_BLOCK = 128
SCAN_CHUNK = 64
N_DEV = 8
LANES = 1024
VMEM_LIMIT = 56 * 1024 * 1024

ADAM_LR = 0.001
ADAM_B1 = 0.9
ADAM_B2 = 0.999
ADAM_EPS = 1e-08
ADAM_WD = 0.01
ADAM_STEP = 10

TRANSPOSED = ("w_in", "w_ffn_gate", "w_ffn_up")
SHARDED = (
    ("w_in", (6920, 1024), 0),
    ("w_ffn_gate", (2816, 1024), 0),
    ("w_ffn_up", (2816, 1024), 0),
    ("w_ffn_down", (2816, 1024), 0),
    ("w_mem_kv", (1024, 1024), 0),
    ("w_o", (1024, 1024), 0),
    ("w_fox_out", (512, 1024), 1),
    ("w_rwkv_out", (512, 1024), 1),
    ("w_mem_out", (512, 1024), 1),
    ("rwkv_w_up", (64, 512), 1),
    ("rwkv_a_up", (64, 512), 1),
    ("rwkv_g_up", (128, 512), 1),
)
REPLICATED = (
    ("pre1_g", (1, 1024)), ("post1_g", (1, 1024)), ("pre2_g", (1, 1024)), ("post2_g", (1, 1024)),
    ("mem_norm_g", (1, 1024)), ("fox_f_bias", (1, 8)), ("rwkv_mu", (1, 1792)), ("rwkv_w0", (1, 512)),
    ("rwkv_a0", (1, 512)), ("rwkv_k_k", (1, 512)), ("rwkv_k_a", (1, 512)), ("rwkv_r_k", (1, 8, 64)),
    ("rwkv_gn_g", (1, 512)), ("rwkv_gn_b", (1, 512)),
)
WEIGHT_ORDER = ('pre1_g', 'post1_g', 'pre2_g', 'post2_g', 'mem_norm_g', 'w_in', 'fox_f_bias', 'rwkv_mu',
                'rwkv_w0', 'rwkv_w_up', 'rwkv_a0', 'rwkv_a_up', 'rwkv_g_up', 'rwkv_k_k', 'rwkv_k_a',
                'rwkv_r_k', 'rwkv_gn_g', 'rwkv_gn_b', 'w_mem_kv', 'w_fox_out', 'w_rwkv_out', 'w_mem_out',
                'w_o', 'w_ffn_gate', 'w_ffn_up', 'w_ffn_down')


def _block_shape(shape, axis):
    return tuple(s // N_DEV if i == axis else s for i, s in enumerate(shape))


def _rows_of(shape):
    n = 1
    for s in shape:
        n *= s
    return n // LANES


SHARD_ROWS = sum(_rows_of(_block_shape(s, a)) for _, s, a in SHARDED)
REPL_ELEMS = sum(_rows_of((LANES,) + s) for _, s in REPLICATED)
REPL_ROWS = -(-REPL_ELEMS // LANES)
PACK_ROWS = -(-(SHARD_ROWS + REPL_ROWS) // 128) * 128
GATHER_ROWS = -(-SHARD_ROWS // 16) * 16


def _cp(sem=None):
    return pltpu.CompilerParams(dimension_semantics=sem, vmem_limit_bytes=VMEM_LIMIT)


def _tile(dim, cap):
    best = None
    for t in range(128, min(dim, cap) + 1, 128):
        if dim % t == 0:
            best = t
    return best if best is not None else dim


def _two_terms(x):
    hi = x.astype(bf16)
    return hi, (x - hi.astype(f32)).astype(bf16)


def _dg(a, b, dims, exact):
    if exact == "split":
        (a_hi, a_lo), (b_hi, b_lo) = _two_terms(a), _two_terms(b)
        dot = functools.partial(lax.dot_general, dimension_numbers=dims, preferred_element_type=f32)
        return dot(a_hi, b_hi) + (dot(a_hi, b_lo) + dot(a_lo, b_hi))
    if exact:
        return lax.dot_general(a, b, dims, precision=_HI, preferred_element_type=f32)
    return lax.dot_general(a.astype(bf16), b.astype(bf16), dims, preferred_element_type=f32)


def _make_mm(batched, exact):
    o = 1 if batched else 0
    bd = ((0,), (0,)) if batched else ((), ())
    d_nn = (((1 + o,), (o,)), bd)
    d_nt = (((1 + o,), (1 + o,)), bd)
    d_tn = (((o,), (o,)), bd)

    @jax.custom_vjp
    def nn(a, b):
        return _dg(a, b, d_nn, exact)

    @jax.custom_vjp
    def nt(a, b):
        return _dg(a, b, d_nt, exact)

    @jax.custom_vjp
    def tn(a, b):
        return _dg(a, b, d_tn, exact)

    nn.defvjp(lambda a, b: (_dg(a, b, d_nn, exact), (a, b)),
              lambda res, g: (_dg(g, res[1], d_nt, exact), _dg(res[0], g, d_tn, exact)))
    nt.defvjp(lambda a, b: (_dg(a, b, d_nt, exact), (a, b)),
              lambda res, g: (_dg(g, res[1], d_nn, exact), _dg(g, res[0], d_tn, exact)))
    tn.defvjp(lambda a, b: (_dg(a, b, d_tn, exact), (a, b)),
              lambda res, g: (_dg(res[1], g, d_nt, exact), _dg(res[0], g, d_nn, exact)))
    return nn, nt, tn


def _sigmoid(x):
    return 1.0 / (1.0 + jnp.exp(-x))


def _head_sum_raw(x):
    width = 2 * HD
    i = lax.broadcasted_iota(jnp.int32, (width, width), 0) // HD
    j = lax.broadcasted_iota(jnp.int32, (width, width), 1) // HD
    m = (i == j).astype(bf16)
    dims = (((1,), (0,)), ((), ()))
    out = []
    for p in range(x.shape[1] // width):
        xp = x[:, p * width:(p + 1) * width]
        hi = xp.astype(bf16)
        lo = (xp - hi.astype(f32)).astype(bf16)
        out.append(lax.dot_general(hi, m, dims, preferred_element_type=f32)
                   + lax.dot_general(lo, m, dims, preferred_element_type=f32))
    return jnp.concatenate(out, axis=1)


@jax.custom_vjp
def _head_sum(x):
    return _head_sum_raw(x)


_head_sum.defvjp(lambda x: (_head_sum_raw(x), None), lambda _, g: (_head_sum_raw(g),))


WEIGHT_TILE_BYTES = 13 * 512 * 1024
ACC_TILE_BYTES = 8 * 1024 * 1024


def _matmul(name, a, b, mode, add=None, out_dtype=f32):
    has_add = add is not None
    if mode == "tn":
        (k, m), (_, n) = a.shape, b.shape
        tn = _tile(n, max(128, ACC_TILE_BYTES // (4 * m)))
        tk = _tile(k, 1024)

        nk = k // tk

        def body(a_ref, b_ref, o_ref, acc):
            @pl.when(pl.program_id(1) == 0)
            def _():
                acc[...] = jnp.zeros_like(acc)

            acc[...] += lax.dot_general(a_ref[...].astype(bf16), b_ref[...].astype(bf16),
                                        (((0,), (0,)), ((), ())), preferred_element_type=f32)

            @pl.when(pl.program_id(1) == nk - 1)
            def _():
                o_ref[...] = acc[...].astype(o_ref.dtype)

        return pl.pallas_call(
            body, name=name, grid=(n // tn, nk),
            in_specs=[pl.BlockSpec((tk, m), lambda j, kk: (kk, 0)), pl.BlockSpec((tk, tn), lambda j, kk: (kk, j))],
            out_specs=pl.BlockSpec((m, tn), lambda j, kk: (0, j)), out_shape=jax.ShapeDtypeStruct((m, n), out_dtype),
            scratch_shapes=[pltpu.VMEM((m, tn), f32)],
            compiler_params=_cp(("parallel", "arbitrary")),
        )(a, b)

    (m, k) = a.shape
    n = b.shape[1] if mode == "nn" else b.shape[0]
    tm = _tile(m, 1024)
    tn = _tile(n, max(128, WEIGHT_TILE_BYTES // (2 * k)))
    dims = (((1,), (0,)), ((), ())) if mode == "nn" else (((1,), (1,)), ((), ()))
    b_spec = pl.BlockSpec((k, tn), lambda j, i: (0, j)) if mode == "nn" else pl.BlockSpec((tn, k), lambda j, i: (j, 0))
    o_spec = pl.BlockSpec((tm, tn), lambda j, i: (i, j))

    def body(*refs):
        a_ref, b_ref = refs[0], refs[1]
        o_ref = refs[-1]
        r = lax.dot_general(a_ref[...].astype(bf16), b_ref[...].astype(bf16), dims, preferred_element_type=f32)
        if has_add:
            r = r + refs[2][...]
        o_ref[...] = r.astype(o_ref.dtype)

    return pl.pallas_call(
        body, name=name, grid=(n // tn, m // tm),
        in_specs=[pl.BlockSpec((tm, k), lambda j, i: (i, 0)), b_spec] + ([o_spec] if has_add else []),
        out_specs=o_spec, out_shape=jax.ShapeDtypeStruct((m, n), out_dtype),
        compiler_params=_cp(("parallel", "arbitrary")),
    )(*((a, b, add) if has_add else (a, b)))


def _sum_nn(name, a_list, b_list, side=None):
    m, n = a_list[0].shape[0], b_list[0].shape[1]
    tm = _tile(m, 256)
    n_g = len(a_list)
    srcs, per_peer = side if side is not None else ([], False)
    n_s = len(srcs)

    def body(*refs):
        o_ref = refs[2 * n_g + n_s]
        _side_exchange(refs[2 * n_g:2 * n_g + n_s], refs[2 * n_g + n_s + 1:2 * n_g + 2 * n_s + 1], per_peer,
                       refs[2 * n_g + 2 * n_s + 1:], m // tm)
        acc = None
        for g in range(n_g):
            r = lax.dot_general(refs[g][...].astype(bf16), refs[n_g + g][...].astype(bf16), (((1,), (0,)), ((), ())),
                                preferred_element_type=f32)
            acc = r if acc is None else acc + r
        o_ref[...] = acc

    res = pl.pallas_call(
        body, name=name, grid=(m // tm,),
        in_specs=[pl.BlockSpec((tm, a.shape[1]), lambda i: (i, 0)) for a in a_list]
        + [pl.BlockSpec(b.shape, lambda i: (0, 0)) for b in b_list] + [_HBM_SPEC] * n_s,
        out_specs=[pl.BlockSpec((tm, n), lambda i: (i, 0))] + [_HBM_SPEC] * n_s,
        out_shape=[jax.ShapeDtypeStruct((m, n), f32)] + _side_out_shapes(srcs, per_peer),
        scratch_shapes=_side_sems(n_s),
        compiler_params=_cp(("arbitrary",)),
    )(*a_list, *b_list, *srcs)
    return res[0], list(res[1:])


def _pieces(ref, widths):
    out, off = [], 0
    for w in widths:
        out.append(ref[:, off:off + w].astype(f32))
        off += w
    return out


def _store_pieces(ref, widths, vals, add_ref=None):
    off = 0
    for w, v in zip(widths, vals):
        ref[:, off:off + w] = (v if add_ref is None else v + add_ref[:, off:off + w]).astype(ref.dtype)
        off += w


def _rows_fwd(name, fn, consts, rows, params, outs, n_sums=0, tm=512, dtypes=None):
    t = (consts + rows)[0][0].shape[0]
    tm = min(tm, t)
    ins = consts + rows
    n_in, n_p, n_o = len(ins), len(params), len(outs)
    dtypes = dtypes or [f32] * n_o

    def body(*refs):
        in_refs, p_refs = refs[:n_in], refs[n_in:n_in + n_p]
        o_refs, s_refs = refs[n_in + n_p:n_in + n_p + n_o], refs[n_in + n_p + n_o:]
        vals = []
        for r, (_, widths) in zip(in_refs, ins):
            vals += _pieces(r, widths)
        res = fn(*vals, *[p[...] for p in p_refs])
        pos = 0
        for r, widths in zip(o_refs, outs):
            _store_pieces(r, widths, res[pos:pos + len(widths)])
            pos += len(widths)

        @pl.when(pl.program_id(0) == 0)
        def _():
            for s in s_refs:
                s[...] = jnp.zeros_like(s)

        for s, v in zip(s_refs, res[pos:]):
            s[...] += v

    row_spec = lambda w: pl.BlockSpec((tm, w), lambda i: (i, 0))
    full = lambda p: pl.BlockSpec(p.shape, lambda i: (0,) * p.ndim)
    return pl.pallas_call(
        body, name=name, grid=(t // tm,),
        in_specs=[row_spec(a.shape[1]) for a, _ in ins] + [full(p) for p in params],
        out_specs=[row_spec(sum(w)) for w in outs] + [pl.BlockSpec((1, 1), lambda i: (0, 0))] * n_sums,
        out_shape=[jax.ShapeDtypeStruct((t, sum(w)), dt) for w, dt in zip(outs, dtypes)] + [jax.ShapeDtypeStruct((1, 1), f32)] * n_sums,
        compiler_params=_cp(("arbitrary",)),
    )(*[a for a, _ in ins], *params)


def _rows_bwd(name, fn, consts, rows, params, outs, cts, n_sums=0, add=None, tm=512, dtypes=None):
    t = (consts + rows)[0][0].shape[0]
    tm = min(tm, t)
    n_c, n_r, n_p, n_o = len(consts), len(rows), len(params), len(outs)
    has_add = add is not None
    dtypes = dtypes or [f32] * n_r

    def body(*refs):
        pos = 0
        c_refs = refs[pos:pos + n_c]; pos += n_c
        r_refs = refs[pos:pos + n_r]; pos += n_r
        p_refs = refs[pos:pos + n_p]; pos += n_p
        ct_refs = refs[pos:pos + n_o]; pos += n_o
        add_ref = refs[pos] if has_add else None
        pos += 1 if has_add else 0
        dr_refs = refs[pos:pos + n_r]; pos += n_r
        dp_refs = refs[pos:pos + n_p]; pos += n_p
        s_refs = refs[pos:pos + n_sums]
        cvals, rvals = [], []
        for r, (_, widths) in zip(c_refs, consts):
            cvals += _pieces(r, widths)
        for r, (_, widths) in zip(r_refs, rows):
            rvals += _pieces(r, widths)
        pvals = [p[...] for p in p_refs]
        ctv = []
        for r, widths in zip(ct_refs, outs):
            ctv += _pieces(r, widths)
        ctv += [jnp.ones((1, 1), f32)] * n_sums
        primal, vjp = jax.vjp(lambda *rp: tuple(fn(*cvals, *rp)), *rvals, *pvals)
        g = vjp(tuple(ctv))
        pos = 0
        for idx, (r, (_, widths)) in enumerate(zip(dr_refs, rows)):
            _store_pieces(r, widths, g[pos:pos + len(widths)], add_ref if idx == 0 else None)
            pos += len(widths)

        @pl.when(pl.program_id(0) == 0)
        def _():
            for acc in list(dp_refs) + list(s_refs):
                acc[...] = jnp.zeros_like(acc)

        for dp, v in zip(dp_refs, g[pos:]):
            dp[...] += v
        for s, v in zip(s_refs, primal[len(primal) - n_sums:]):
            s[...] += v

    row_spec = lambda w: pl.BlockSpec((tm, w), lambda i: (i, 0))
    full = lambda p: pl.BlockSpec(p.shape, lambda i: (0,) * p.ndim)
    args = [a for a, _ in consts + rows] + list(params) + list(cts) + ([add] if has_add else [])
    res = pl.pallas_call(
        body, name=name, grid=(t // tm,),
        in_specs=[row_spec(a.shape[1]) for a, _ in consts + rows] + [full(p) for p in params]
        + [row_spec(sum(w)) for w in outs] + ([row_spec(add.shape[1])] if has_add else []),
        out_specs=[row_spec(a.shape[1]) for a, _ in rows] + [full(p) for p in params]
        + [pl.BlockSpec((1, 1), lambda i: (0, 0))] * n_sums,
        out_shape=[jax.ShapeDtypeStruct(a.shape, dt) for (a, _), dt in zip(rows, dtypes)]
        + [jax.ShapeDtypeStruct(p.shape, f32) for p in params] + [jax.ShapeDtypeStruct((1, 1), f32)] * n_sums,
        compiler_params=_cp(("arbitrary",)),
    )(*args)
    return res[:n_r], res[n_r:n_r + n_p] + res[n_r + n_p:]


def _rms(x, g):
    return x * lax.rsqrt(jnp.mean(x * x, axis=-1, keepdims=True) + NORM_EPS) * g


def _fn_rms(x, g):
    return (_rms(x, g),)


def _fn_rwkv_pre(r, k, v, wd, ad, gd, w0, w_up, a0, a_up, g_up, k_k, k_a):
    nn, _, _ = _make_mm(False, False)
    w_log = -_sigmoid(w0 + nn(jnp.tanh(wd), w_up)) * 0.6065306597126334
    a = _sigmoid(a0 + nn(ad, a_up))
    g = nn(_sigmoid(gd), g_up)
    kk = k * k_k
    kk = kk * lax.rsqrt(jnp.maximum(_head_sum(kk * kk), 1e-24))
    k2 = k * (1.0 + (a - 1.0) * k_a)
    return r, w_log, k2, v, -kk, kk * a, g


def _fn_rwkv_post(y, r, k2, v, g, gn_g, gn_b, r_k):
    mean = _head_sum(y) * (1.0 / HD)
    yc = y - mean
    var = _head_sum(yc * yc) * (1.0 / HD)
    yn = yc * lax.rsqrt(var + GN_EPS) * gn_g + gn_b
    bonus = _head_sum(r * k2 * r_k) * v
    return ((yn + bonus) * g,)


def _fn_merge(a_fox, a_rwkv, a_mem, g_fox, g_rwkv, g_mem):
    return (_sigmoid(g_fox) * a_fox + _sigmoid(g_rwkv) * a_rwkv + _sigmoid(g_mem) * a_mem,)


def _fn_post1(y, x, post1_g, pre2_g):
    h1 = x + _rms(y, post1_g)
    return h1, _rms(h1, pre2_g)


def _fn_swiglu(gp, up):
    return (gp * _sigmoid(gp) * up,)


def _fn_final(target, ffn, h1, post2_g):
    err = h1 + _rms(ffn, post2_g) - target
    per_row = jnp.mean(err * err, axis=-1, keepdims=True)
    return (0.5 * jnp.sum(per_row, axis=0, keepdims=True),)


def _shift_down(x):
    row = lax.broadcasted_iota(jnp.int32, x.shape, 0)
    return jnp.where(row == 0, 0.0, pltpu.roll(x, 1, 0))


def _shift_up(x):
    s = x.shape[0]
    row = lax.broadcasted_iota(jnp.int32, x.shape, 0)
    return jnp.where(row == s - 1, 0.0, pltpu.roll(x, s - 1, 0))


def _tokshift_fwd(p, mu, batch, seq):
    w = p.shape[1]
    tc = _tile(w, 384)

    def body(p_ref, mu_ref, o_ref):
        x = p_ref[...]
        o_ref[...] = x + (_shift_down(x) - x) * mu_ref[...]

    return pl.pallas_call(
        body, name="tokshift_fwd", grid=(w // tc, batch),
        in_specs=[pl.BlockSpec((seq, tc), lambda j, b: (b, j)), pl.BlockSpec((1, tc), lambda j, b: (0, j))],
        out_specs=pl.BlockSpec((seq, tc), lambda j, b: (b, j)),
        out_shape=jax.ShapeDtypeStruct(p.shape, f32),
        compiler_params=_cp(("parallel", "arbitrary")),
    )(p, mu)


def _tokshift_bwd(p, mu, dps, batch, seq):
    w = p.shape[1]
    tc = _tile(w, 384)

    def body(p_ref, mu_ref, d_ref, dp_ref, dmu_ref):
        x, mu_v, d = p_ref[...], mu_ref[...], d_ref[...]
        dp_ref[...] = (d * (1.0 - mu_v) + _shift_up(d * mu_v)).astype(dp_ref.dtype)

        @pl.when(pl.program_id(1) == 0)
        def _():
            dmu_ref[...] = jnp.zeros_like(dmu_ref)

        dmu_ref[...] += jnp.sum(d * (_shift_down(x) - x), axis=0, keepdims=True)

    return pl.pallas_call(
        body, name="tokshift_bwd", grid=(w // tc, batch),
        in_specs=[pl.BlockSpec((seq, tc), lambda j, b: (b, j)), pl.BlockSpec((1, tc), lambda j, b: (0, j)),
                  pl.BlockSpec((seq, tc), lambda j, b: (b, j))],
        out_specs=[pl.BlockSpec((seq, tc), lambda j, b: (b, j)), pl.BlockSpec((1, tc), lambda j, b: (0, j))],
        out_shape=[jax.ShapeDtypeStruct(p.shape, bf16), jax.ShapeDtypeStruct(mu.shape, f32)],
        compiler_params=_cp(("parallel", "arbitrary")),
    )(p, mu, dps)


def _cum_block(seq):
    return _tile(seq, 256)


def _fox_gate_fwd(f, bias, batch, seq):
    cb = _cum_block(seq)

    def body(f_ref, b_ref, c_ref):
        row = lax.broadcasted_iota(jnp.int32, (cb, cb), 0)
        col = lax.broadcasted_iota(jnp.int32, (cb, cb), 1)
        tri = (col <= row).astype(f32)
        carry = jnp.zeros((1, 128), f32)
        for i in range(seq // cb):
            z = f_ref[i * cb:(i + 1) * cb, :] + b_ref[...]
            ls = jnp.minimum(z, 0.0) - jnp.log(1.0 + jnp.exp(-jnp.abs(z)))
            c = _dg(tri, ls, (((1,), (0,)), ((), ())), True) + carry
            c_ref[i * cb:(i + 1) * cb, :] = c
            carry = c[cb - 1:cb, :]

    return pl.pallas_call(
        body, name="fox_gate_fwd", grid=(batch,),
        in_specs=[pl.BlockSpec((seq, 128), lambda b: (b, 0)), pl.BlockSpec((1, 128), lambda b: (0, 0))],
        out_specs=pl.BlockSpec((seq, 128), lambda b: (b, 0)),
        out_shape=jax.ShapeDtypeStruct(f.shape, f32),
        compiler_params=_cp(("arbitrary",)),
    )(f, bias)


def _fox_gate_bwd(f, bias, dc_a, dc_b, batch, seq):
    cb = _cum_block(seq)

    def body(f_ref, b_ref, da_ref, db_ref, df_ref, dbias_ref):
        row = lax.broadcasted_iota(jnp.int32, (cb, cb), 0)
        col = lax.broadcasted_iota(jnp.int32, (cb, cb), 1)
        triu = (col >= row).astype(f32)

        @pl.when(pl.program_id(0) == 0)
        def _():
            dbias_ref[...] = jnp.zeros_like(dbias_ref)

        lane = lax.broadcasted_iota(jnp.int32, (1, 128), 1)

        def by_head(blk):
            out = jnp.zeros((cb, 128), f32)
            for p in range(HEADS // 2):
                for e in range(2):
                    out = jnp.where(lane == 2 * p + e, _pick_lane(blk[:, p * 128:(p + 1) * 128], e), out)
            return out

        carry = jnp.zeros((1, 128), f32)
        tot = jnp.zeros((1, 128), f32)
        for i in reversed(range(seq // cb)):
            sl = slice(i * cb, (i + 1) * cb)
            dc = by_head(da_ref[sl, :] + db_ref[sl, :])
            dls = _dg(triu, dc, (((1,), (0,)), ((), ())), True) + carry
            carry = dls[0:1, :]
            df = dls * _sigmoid(-(f_ref[sl, :] + b_ref[...]))
            df_ref[sl, :] = df.astype(df_ref.dtype)
            tot = tot + jnp.sum(df, axis=0, keepdims=True)
        dbias_ref[...] += tot

    return pl.pallas_call(
        body, name="fox_gate_bwd", grid=(batch,),
        in_specs=[pl.BlockSpec((seq, 128), lambda b: (b, 0)), pl.BlockSpec((1, 128), lambda b: (0, 0)),
                  pl.BlockSpec((seq, HW), lambda b: (b, 0)), pl.BlockSpec((seq, HW), lambda b: (b, 0))],
        out_specs=[pl.BlockSpec((seq, 128), lambda b: (b, 0)), pl.BlockSpec((1, 128), lambda b: (0, 0))],
        out_shape=[jax.ShapeDtypeStruct(f.shape, bf16), jax.ShapeDtypeStruct((1, 128), f32)],
        compiler_params=_cp(("arbitrary",)),
    )(f, bias, dc_a, dc_b)


_HBM_SPEC = pl.BlockSpec(memory_space=pltpu.HBM)


def _side_out_shapes(srcs, per_peer):
    return [jax.ShapeDtypeStruct(((N_DEV,) + tuple(s.shape[1:] if per_peer else s.shape)), s.dtype) for s in srcs]


def _side_sems(n):
    if n == 0:
        return []
    return [pltpu.SemaphoreType.DMA((n, N_DEV - 1)), pltpu.SemaphoreType.DMA((n, N_DEV - 1)), pltpu.SemaphoreType.DMA((n,))]


def _peer_copies(src_refs, dst_refs, per_peer, sems):
    send_sems, recv_sems, local_sems = sems
    x, y, c = lax.axis_index("x"), lax.axis_index("y"), lax.axis_index("c")
    me = 4 * x + 2 * y + c

    def remote(src, dst, t, k, to):
        return pltpu.make_async_remote_copy(src_ref=src, dst_ref=dst, send_sem=send_sems.at[t, k - 1],
                                            recv_sem=recv_sems.at[t, k - 1], device_id=to,
                                            device_id_type=pl.DeviceIdType.MESH)

    direct, relays = [], []
    for t, (s, d) in enumerate(zip(src_refs, dst_refs)):
        direct.append((t, 0, pltpu.make_async_copy(s.at[me] if per_peer else s, d.at[me], local_sems.at[t])))
        for k in range(1, N_DEV):
            px = 1 - x if k & 4 else x
            py = 1 - y if k & 2 else y
            pc = 1 - c if k & 1 else c
            if per_peer:
                direct.append((t, k, remote(s.at[4 * px + 2 * py + pc], d.at[me], t, k, (px, py, pc))))
            elif k == 1 or not k & 1:
                direct.append((t, k, remote(s, d.at[me], t, k, (px, py, pc))))
            else:
                origin = d.at[4 * px + 2 * py + c]
                relays.append((t, k - 1, remote(origin, origin, t, k, (x, y, 1 - c))))
    return direct, relays


def _exchange_start(direct):
    for _, _, cp in direct:
        cp.start()


def _exchange_finish(direct, relays):
    landed = {(t, k): cp for t, k, cp in direct}
    for t, j, cp in relays:
        landed[(t, j)].wait_recv()
        cp.start()
    relayed = {(t, j) for t, j, _ in relays}
    for t, k, cp in direct:
        if k == 0:
            cp.wait()
        else:
            cp.wait_send()
            if (t, k) not in relayed:
                cp.wait_recv()
    for _, _, cp in relays:
        cp.wait()


def _side_exchange(src_refs, dst_refs, per_peer, sems, *grid):
    if not src_refs:
        return
    first = functools.reduce(jnp.logical_and, [pl.program_id(a) == 0 for a in range(len(grid))])
    last = functools.reduce(jnp.logical_and, [pl.program_id(a) == n - 1 for a, n in enumerate(grid)])

    @pl.when(first)
    def _():
        _exchange_start(_peer_copies(src_refs, dst_refs, per_peer, sems)[0])

    @pl.when(last)
    def _():
        _exchange_finish(*_peer_copies(src_refs, dst_refs, per_peer, sems))


def _exchange(name, srcs, per_peer):
    n = len(srcs)

    def body(*refs):
        direct, relays = _peer_copies(refs[:n], refs[n:2 * n], per_peer, refs[2 * n:])
        _exchange_start(direct)
        _exchange_finish(direct, relays)

    return pl.pallas_call(
        body, name=name, in_specs=[_HBM_SPEC] * n, out_specs=[_HBM_SPEC] * n,
        out_shape=_side_out_shapes(srcs, per_peer), scratch_shapes=_side_sems(n),
    )(*srcs)


FOX_T = 512
_NEG = -1e30
_D2 = (((1,), (1,)), ((), ()))
_D1 = (((1,), (0,)), ((), ()))
_D0 = (((0,), (0,)), ((), ()))


def _bdot(a, b, dims):
    return lax.dot_general(a.astype(bf16), b.astype(bf16), dims, preferred_element_type=f32)


def _pick_lane(x, lane):
    idx = lax.broadcasted_iota(jnp.int32, x.shape, 1)
    return jnp.sum(jnp.where(idx == lane, x, 0.0), axis=1, keepdims=True)


def _pick_row(x, row):
    idx = lax.broadcasted_iota(jnp.int32, x.shape, 0)
    return jnp.sum(jnp.where(idx == row, x, 0.0), axis=0, keepdims=True)


def _fox_fwd(qkv, c, c_rows, batch, seq, side=None):
    t = min(FOX_T, seq)
    nq = seq // t
    scale = HD ** -0.5
    srcs, per_peer = side if side is not None else ([], False)
    n_s = len(srcs)

    def body(*refs):
        q_ref, k_ref, v_ref, cq_ref, ck_ref = refs[:5]
        o_ref, lse_ref = refs[5 + n_s:7 + n_s]
        _side_exchange(refs[5:5 + n_s], refs[7 + n_s:7 + 2 * n_s], per_peer, refs[7 + 2 * n_s:], batch, PAIRS, nq)
        pair, i = pl.program_id(1), pl.program_id(2)
        lane = lax.broadcasted_iota(jnp.int32, (1, PAIR_W), 1)
        first = (lane // HD) == 0
        mine = [first, jnp.logical_not(first)]
        q = q_ref[...] * scale
        qs = [jnp.where(mine[e], q, 0.0) for e in range(2)]
        cqs = [_pick_lane(cq_ref[...], 2 * pair + e) for e in range(2)]
        causal = lax.broadcasted_iota(jnp.int32, (t, t), 1) <= lax.broadcasted_iota(jnp.int32, (t, t), 0)

        def block(j, carry, diagonal):
            rows = pl.ds(pl.multiple_of(j * t, t), t)
            kj, vj = k_ref[rows, :], v_ref[rows, :]
            ck_blk = ck_ref[0, :, rows]
            out = []
            for e in range(2):
                m, acc = carry[2 * e:2 * e + 2]
                s = _bdot(qs[e], kj, _D2) + cqs[e] - _pick_row(ck_blk, 2 * pair + e)
                if diagonal:
                    s = jnp.where(causal, s, _NEG)
                m_new = jnp.maximum(m, jnp.max(s, axis=1, keepdims=True))
                p = jnp.exp(s - m_new)
                out += [m_new, jnp.exp(m - m_new) * acc + _bdot(p, jnp.where(mine[e], vj, 1.0), _D1)]
            return tuple(out)

        init = (jnp.full((t, 1), _NEG, f32), jnp.zeros((t, PAIR_W), f32)) * 2
        carry = lax.fori_loop(0, i, lambda j, cr: block(j, cr, False), init)
        m0, a0, m1, a1 = block(i, carry, True)
        l0, l1 = _pick_lane(a0, HD), _pick_lane(a1, 0)
        o_ref[...] = jnp.where(first, a0 / l0, a1 / l1)
        lse_ref[...] = jnp.where(lane == 0, m0 + jnp.log(l0), jnp.where(lane == 1, m1 + jnp.log(l1), 0.0))

    q_spec = pl.BlockSpec((t, PAIR_W), lambda b, p, i: (b * nq + i, p))
    res = pl.pallas_call(
        body, name="fox_attn_fwd", grid=(batch, PAIRS, nq),
        in_specs=[q_spec,
                  pl.BlockSpec((seq, PAIR_W), lambda b, p, i: (b, PAIRS + p)),
                  pl.BlockSpec((seq, PAIR_W), lambda b, p, i: (b, 2 * PAIRS + p)),
                  pl.BlockSpec((t, 128), lambda b, p, i: (b * nq + i, 0)),
                  pl.BlockSpec((1, 8, seq), lambda b, p, i: (b, 0, 0))] + [_HBM_SPEC] * n_s,
        out_specs=[q_spec, q_spec] + [_HBM_SPEC] * n_s,
        out_shape=[jax.ShapeDtypeStruct((batch * seq, HW), f32)] * 2 + _side_out_shapes(srcs, per_peer),
        scratch_shapes=_side_sems(n_s),
        compiler_params=_cp(("arbitrary", "arbitrary", "arbitrary")),
    )(qkv, qkv, qkv, c, c_rows, *srcs)
    return res[0], res[1], list(res[2:])


def _fox_bwd(qkv, c, c_rows, o, lse, do, batch, seq):
    t = min(FOX_T, seq)
    nq = seq // t
    scale = HD ** -0.5

    def body(q_ref, k_ref, v_ref, cq_ref, ck_ref, o_ref, lse_ref, do_ref,
             dq_ref, dk_ref, dv_ref, dcq_ref, dck_ref, acc0, acc1):
        pair, i = pl.program_id(1), pl.program_id(2)
        accs = [acc0, acc1]

        @pl.when(i == 0)
        def _():
            dv_ref[...] = jnp.zeros_like(dv_ref)
            acc0[...] = jnp.zeros_like(acc0)
            acc1[...] = jnp.zeros_like(acc1)

        lane = lax.broadcasted_iota(jnp.int32, (1, PAIR_W), 1)
        first = (lane // HD) == 0
        mine = [first, jnp.logical_not(first)]
        q, d_o, o_i = q_ref[...] * scale, do_ref[...], o_ref[...]
        q0s = [jnp.where(mine[e], q, 0.0) for e in range(2)]
        q1s = [jnp.where(mine[e], q, 1.0) for e in range(2)]
        dos = [jnp.where(mine[e], d_o, 0.0) for e in range(2)]
        deltas = [jnp.sum(dos[e] * o_i, axis=1, keepdims=True) for e in range(2)]
        lses = [_pick_lane(lse_ref[...], e) for e in range(2)]
        cqs = [_pick_lane(cq_ref[...], 2 * pair + e) for e in range(2)]
        causal = lax.broadcasted_iota(jnp.int32, (t, t), 1) <= lax.broadcasted_iota(jnp.int32, (t, t), 0)

        def block(j, dqs, diagonal):
            rows = pl.ds(pl.multiple_of(j * t, t), t)
            kj, vj = k_ref[rows, :], v_ref[rows, :]
            ck_blk = ck_ref[0, :, rows]
            out = []
            for e in range(2):
                s = _bdot(q0s[e], kj, _D2) + cqs[e] - _pick_row(ck_blk, 2 * pair + e)
                if diagonal:
                    s = jnp.where(causal, s, _NEG)
                p = jnp.exp(s - lses[e])
                ds = p * (_bdot(dos[e], vj, _D2) - deltas[e])
                dv_ref[rows, :] += _bdot(p, dos[e], _D0)
                accs[e][rows, :] += _bdot(ds, q1s[e], _D0)
                out.append(dqs[e] + _bdot(ds, jnp.where(mine[e], kj, 1.0), _D1))
            return tuple(out)

        zero = jnp.zeros((t, PAIR_W), f32)
        dqs = lax.fori_loop(0, i, lambda j, cr: block(j, cr, False), (zero, zero))
        dq0, dq1 = block(i, dqs, True)
        dq_ref[...] = jnp.where(first, dq0, dq1) * scale
        dcq_ref[...] = jnp.where(lane == 0, _pick_lane(dq0, HD), jnp.where(lane == 1, _pick_lane(dq1, 0), 0.0))

        @pl.when(i == nq - 1)
        def _():
            a0, a1 = acc0[...], acc1[...]
            dk_ref[...] = jnp.where(first, a0, a1)
            dck_ref[...] = jnp.where(lane == 0, -_pick_lane(a0, HD), jnp.where(lane == 1, -_pick_lane(a1, 0), 0.0))

    blk = lambda col: pl.BlockSpec((t, PAIR_W), lambda b, p, i: (b * nq + i, col * PAIRS + p))
    whole = lambda col: pl.BlockSpec((seq, PAIR_W), lambda b, p, i: (b, col * PAIRS + p))
    t_all = batch * seq
    return pl.pallas_call(
        body, name="fox_attn_bwd", grid=(batch, PAIRS, nq),
        in_specs=[blk(0), whole(1), whole(2),
                  pl.BlockSpec((t, 128), lambda b, p, i: (b * nq + i, 0)),
                  pl.BlockSpec((1, 8, seq), lambda b, p, i: (b, 0, 0)),
                  blk(0), blk(0), blk(0)],
        out_specs=[blk(0), whole(0), whole(0), blk(0), whole(0)],
        out_shape=[jax.ShapeDtypeStruct((t_all, HW), f32)] * 5,
        scratch_shapes=[pltpu.VMEM((seq, PAIR_W), f32), pltpu.VMEM((seq, PAIR_W), f32)],
        compiler_params=_cp(("parallel", "parallel", "arbitrary")),
    )(qkv, qkv, qkv, c, c_rows, o, lse, do)


def _mem_block(q, km, vm):
    nn, nt, _ = _make_mm(False, False)
    logits = nt(q, km) * (MEM_HD ** -0.5)
    m = lax.stop_gradient(jnp.max(logits, axis=-1, keepdims=True))
    e = jnp.exp(logits - m)
    return nn(e / jnp.sum(e, axis=-1, keepdims=True), vm)


def _mem_specs(seq, tq):
    nq = seq // tq
    qs = pl.BlockSpec((tq, MEM_HD), lambda b, h, i: (b * nq + i, h))
    ks = pl.BlockSpec((MEM_LEN, MEM_HD), lambda b, h, i: (b, h))
    vs = pl.BlockSpec((MEM_LEN, MEM_HD), lambda b, h, i: (b, MEM_HEADS + h))
    return nq, qs, ks, vs


def _mem_fwd(q, mem_kv, batch, seq):
    tq = min(512, seq)
    nq, qs, ks, vs = _mem_specs(seq, tq)

    def body(q_ref, k_ref, v_ref, o_ref):
        o_ref[...] = _mem_block(q_ref[...].astype(f32), k_ref[...], v_ref[...]).astype(o_ref.dtype)

    return pl.pallas_call(
        body, name="mem_attn_fwd", grid=(batch, MEM_HEADS, nq),
        in_specs=[qs, ks, vs], out_specs=qs, out_shape=jax.ShapeDtypeStruct(q.shape, bf16),
        compiler_params=_cp(("parallel", "parallel", "arbitrary")),
    )(q, mem_kv, mem_kv)


def _mem_bwd(q, mem_kv, do, batch, seq):
    tq = min(512, seq)
    nq, qs, ks, vs = _mem_specs(seq, tq)

    def body(q_ref, k_ref, v_ref, do_ref, dq_ref, dk_ref, dv_ref):
        _, vjp = jax.vjp(_mem_block, q_ref[...].astype(f32), k_ref[...], v_ref[...])
        dq, dk, dv = vjp(do_ref[...])
        dq_ref[...] = dq.astype(dq_ref.dtype)

        @pl.when(pl.program_id(2) == 0)
        def _():
            dk_ref[...] = jnp.zeros_like(dk_ref)
            dv_ref[...] = jnp.zeros_like(dv_ref)

        dk_ref[...] += dk
        dv_ref[...] += dv

    return pl.pallas_call(
        body, name="mem_attn_bwd", grid=(batch, MEM_HEADS, nq),
        in_specs=[qs, ks, vs, qs], out_specs=[qs, ks, ks],
        out_shape=[jax.ShapeDtypeStruct(q.shape, bf16), jax.ShapeDtypeStruct((batch * MEM_LEN, MEM_W), f32),
                   jax.ShapeDtypeStruct((batch * MEM_LEN, MEM_W), f32)],
        compiler_params=_cp(("parallel", "parallel", "arbitrary")),
    )(q, mem_kv, mem_kv, do)


@jax.custom_vjp
def _halves(x):
    c = x.shape[1] // 2
    return x[:, :c], x[:, c:]


_halves.defvjp(lambda x: ((x[:, :x.shape[1] // 2], x[:, x.shape[1] // 2:]), None),
               lambda _, g: (jnp.concatenate(g, axis=1),))


@jax.custom_vjp
def _lead_halves(x):
    n = x.shape[0] // 2
    return x[:n], x[n:]


_lead_halves.defvjp(lambda x: ((x[:x.shape[0] // 2], x[x.shape[0] // 2:]), None),
                    lambda _, g: (jnp.concatenate(g, axis=0),))


def _scan_chunk(s0, r, wl, k, v, a, b):
    nn, nt, tn = _make_mm(True, False)
    nn_exact, _, _ = _make_mm(True, True)
    _, nt_exact, _ = _make_mm(True, "split")
    hp, c, lanes = r.shape
    row = lax.broadcasted_iota(jnp.int32, (c, c), 0)
    col = lax.broadcasted_iota(jnp.int32, (c, c), 1)
    first = (lax.broadcasted_iota(jnp.int32, (1, 1, lanes), 2) // HD) == 0
    tri = jnp.broadcast_to((col <= row).astype(f32)[None], (hp, c, c))
    lg = nn_exact(tri, wl)
    lg_end = lg[:, c - 1:c, :]
    grow, shrink, to_end = jnp.exp(lg), jnp.exp(-lg), jnp.exp(lg_end - lg)
    rt, kt, bt, at = r * grow, k * shrink, b * shrink, a * jnp.exp(lg - wl)
    strict, incl = (col < row)[None], (col <= row)[None]
    twice = lambda t: jnp.concatenate([t, t], axis=0)
    queries = jnp.concatenate([at, rt], axis=1)
    per_head = jnp.concatenate([jnp.where(first, queries, 0.0), jnp.where(first, 0.0, queries)], axis=0)
    (ab, rb), (ak, rk) = _halves(nt_exact(per_head, twice(bt))), _halves(nt_exact(per_head, twice(kt)))
    l_ab = jnp.where(strict, ab, 0.0)
    a_ak = jnp.where(strict, ak, 0.0)
    a_rb = jnp.where(incl, rb, 0.0)
    a_rk = jnp.where(incl, rk, 0.0)
    inv = (col == row).astype(f32)[None] + l_ab
    power, n = l_ab, 1
    while 2 * n < c:
        power = nn(power, power)
        inv = inv + nn(inv, power)
        n *= 2

    def apply(m, t):
        lo, hi = _lead_halves(nn(m, twice(t)))
        return jnp.where(first, lo, hi)

    sa = apply(inv, nt(at, s0) + apply(a_ak, v))
    y = nt(rt, s0) + apply(a_rk, v) + apply(a_rb, sa)
    same_head = ((lax.broadcasted_iota(jnp.int32, (lanes, lanes), 0) // HD)
                 == (lax.broadcasted_iota(jnp.int32, (lanes, lanes), 1) // HD))[None]
    s1 = s0 * jnp.exp(lg_end) + jnp.where(same_head, tn(v, k * to_end) + tn(sa, b * to_end), 0.0)
    return y, s1


PAIRS = HEADS // 2
PAIR_W = 2 * HD


def _pair_stack(ref, off):
    return jnp.stack([ref[b, :, off + p * PAIR_W:off + (p + 1) * PAIR_W]
                      for b in range(ref.shape[0]) for p in range(PAIRS)])


def _pair_store(ref, off, val, add_ref=None):
    for b in range(ref.shape[0]):
        for p in range(PAIRS):
            sl = slice(off + p * PAIR_W, off + (p + 1) * PAIR_W)
            v = val[b * PAIRS + p]
            ref[b, :, sl] = v if add_ref is None else v + add_ref[b, :, sl]


def _scan_fwd(main6, batch, seq, side=None):
    c = min(SCAN_CHUNK, seq)
    nc = seq // c
    hp = batch * PAIRS
    srcs, per_peer = side if side is not None else ([], False)
    n_s = len(srcs)

    def body(*refs):
        z_ref, y_ref, s_ref, st = refs[0], refs[1 + n_s], refs[2 + n_s], refs[3 + 2 * n_s]
        _side_exchange(refs[1:1 + n_s], refs[3 + n_s:3 + 2 * n_s], per_peer, refs[4 + 2 * n_s:], nc)

        @pl.when(pl.program_id(0) == 0)
        def _():
            st[...] = jnp.zeros_like(st)

        s0 = st[...]
        s_ref[0] = s0
        y, s1 = _scan_chunk(s0, *[_pair_stack(z_ref, comp * HW) for comp in range(6)])
        _pair_store(y_ref, 0, y)
        st[...] = s1

    res = pl.pallas_call(
        body, name="rwkv_scan_fwd", grid=(nc,),
        in_specs=[pl.BlockSpec((batch, c, 6 * HW), lambda i: (0, i, 0))] + [_HBM_SPEC] * n_s,
        out_specs=[pl.BlockSpec((batch, c, HW), lambda i: (0, i, 0)),
                   pl.BlockSpec((1, hp, PAIR_W, PAIR_W), lambda i: (i, 0, 0, 0))] + [_HBM_SPEC] * n_s,
        out_shape=[jax.ShapeDtypeStruct((batch, seq, HW), f32), jax.ShapeDtypeStruct((nc, hp, PAIR_W, PAIR_W), f32)]
        + _side_out_shapes(srcs, per_peer),
        scratch_shapes=[pltpu.VMEM((hp, PAIR_W, PAIR_W), f32)] + _side_sems(n_s),
        compiler_params=_cp(("arbitrary",)),
    )(main6.reshape(batch, seq, 6 * HW), *srcs)
    return res[0].reshape(batch * seq, HW), res[1], list(res[2:])


def _scan_bwd(main6, states, dy, extra, batch, seq, side=None):
    c = min(SCAN_CHUNK, seq)
    nc = seq // c
    hp = batch * PAIRS
    srcs, per_peer = side if side is not None else ([], False)
    n_s = len(srcs)

    def body(*refs):
        z_ref, s_ref, dy_ref, ex_ref = refs[:4]
        dz_ref, dst = refs[4 + n_s], refs[5 + 2 * n_s]
        _side_exchange(refs[4:4 + n_s], refs[5 + n_s:5 + 2 * n_s], per_peer, refs[6 + 2 * n_s:], nc)

        @pl.when(pl.program_id(0) == 0)
        def _():
            dst[...] = jnp.zeros_like(dst)

        _, vjp = jax.vjp(_scan_chunk, s_ref[0], *[_pair_stack(z_ref, comp * HW) for comp in range(6)])
        g = vjp((_pair_stack(dy_ref, 0), dst[...]))
        dst[...] = g[0]
        for comp in range(6):
            _pair_store(dz_ref, comp * HW, g[1 + comp], ex_ref)

    back = lambda i: (0, nc - 1 - i, 0)
    wide = pl.BlockSpec((batch, c, 6 * HW), back)
    res = pl.pallas_call(
        body, name="rwkv_scan_bwd", grid=(nc,),
        in_specs=[wide, pl.BlockSpec((1, hp, PAIR_W, PAIR_W), lambda i: (nc - 1 - i, 0, 0, 0)),
                  pl.BlockSpec((batch, c, HW), back), wide] + [_HBM_SPEC] * n_s,
        out_specs=[wide] + [_HBM_SPEC] * n_s,
        out_shape=[jax.ShapeDtypeStruct((batch, seq, 6 * HW), f32)] + _side_out_shapes(srcs, per_peer),
        scratch_shapes=[pltpu.VMEM((hp, PAIR_W, PAIR_W), f32)] + _side_sems(n_s),
        compiler_params=_cp(("arbitrary",)),
    )(main6.reshape(batch, seq, 6 * HW), states, dy.reshape(batch, seq, HW), extra.reshape(batch, seq, 6 * HW), *srcs)
    return res[0].reshape(batch * seq, 6 * HW), list(res[1:])


def _to_heads(x, batch, seq, k):
    return x.reshape(batch, seq, k, HEADS, HD).transpose(2, 0, 3, 1, 4).reshape(k, batch * HEADS, seq, HD)


def _from_heads(x, batch, seq, k):
    return x.reshape(k, batch, HEADS, seq, HD).transpose(1, 3, 0, 2, 4).reshape(batch * seq, k * HW)


def _pad_cols(x, width):
    return jnp.pad(x, ((0, 0), (0, width - x.shape[1])))


def _split_w_in(wt):
    z = lambda rows: jnp.zeros((rows, wt.shape[1]), wt.dtype)
    w_r = jnp.concatenate([wt[1544:3080], wt[3080:3144], z(64), wt[3144:3208], z(64), wt[3208:3336]], axis=0)
    return wt[:1536], jnp.concatenate([wt[1536:1544], z(120)], axis=0), w_r, wt[3336:3848], wt[3848:]


def _merge_w_in(g_qkv, g_f, g_r, g_mq, g_g):
    return jnp.concatenate([g_qkv, g_f[:8], g_r[:1536], g_r[1536:1600], g_r[1664:1728], g_r[1792:], g_mq, g_g], axis=0)


def _pad_lora(v):
    z64 = jnp.zeros((1, 64), v.dtype)
    return jnp.concatenate([v[:, :1536], v[:, 1536:1600], z64, v[:, 1600:1664], z64, v[:, 1664:]], axis=1)


def _unpad_lora(v):
    return jnp.concatenate([v[:, :1536], v[:, 1536:1600], v[:, 1664:1728], v[:, 1792:]], axis=1)


def _local_step(x, mem, target, w, p, late=None, early=None, last=None):
    batch, seq, _ = x.shape
    t = batch * seq
    x2, tg2, mem2 = x.reshape(t, D), target.reshape(t, D), mem.reshape(batch * MEM_LEN, D)
    w_qkv, w_f, w_r, w_mq, w_g3 = _split_w_in(w["w_in"])
    mu = _pad_lora(p["rwkv_mu"])
    bias = _pad_cols(p["fox_f_bias"], 128)
    r_k = p["rwkv_r_k"].reshape(1, HW)
    post_params = [p["rwkv_gn_g"], p["rwkv_gn_b"], r_k]
    rw_widths = [HW, HW, HW, LORA_PAD, LORA_PAD, LORA_PAD]
    six = [HW] * 6

    (u,) = _rows_fwd("rms_pre1", _fn_rms, [], [(x2, [D])], [p["pre1_g"]], [[D]], dtypes=[bf16])
    p_qkv = _matmul("proj_qkv", u, w_qkv, "nt", out_dtype=bf16)
    p_f = _matmul("proj_f", u, w_f, "nt")
    p_r = _matmul("proj_rwkv", u, w_r, "nt")
    p_mq = _matmul("proj_memq", u, w_mq, "nt", out_dtype=bf16)
    p_g = _matmul("proj_gate", u, w_g3, "nt", out_dtype=bf16)

    c = _fox_gate_fwd(p_f, bias, batch, seq)
    c_rows = c[:, :HEADS].reshape(batch, seq, HEADS).transpose(0, 2, 1)
    fox_o, lse, gathered = _fox_fwd(p_qkv, c, c_rows, batch, seq, side=(late[0], False) if late else None)
    if late:
        w = {**w, **late[2](gathered, 0)}
    fox_out = fox_o.astype(bf16)

    w_up = jnp.pad(w["rwkv_w_up"].astype(f32), ((0, LORA_PAD - 64), (0, 0)))
    a_up = jnp.pad(w["rwkv_a_up"].astype(f32), ((0, LORA_PAD - 64), (0, 0)))
    pre_params = [p["rwkv_w0"], w_up, p["rwkv_a0"], a_up, w["rwkv_g_up"].astype(f32), p["rwkv_k_k"], p["rwkv_k_a"]]
    ps = _tokshift_fwd(p_r, mu, batch, seq)
    main6, g_rw = _rows_fwd("rwkv_pre", _fn_rwkv_pre, [], [(ps, rw_widths)], pre_params, [six, [HW]], tm=256)
    y_rw, states, gathered = _scan_fwd(main6, batch, seq, side=(late[1], False) if late else None)
    if late:
        w = {**w, **late[2](gathered, 1)}
    post_consts = []
    post_rows = [(y_rw, [HW]), (main6, six), (g_rw, [HW])]

    def fn_post(y, r, _wl, k2, v, _a, _b, g, gn_g, gn_b, rk):
        return _fn_rwkv_post(y, r, k2, v, g, gn_g, gn_b, rk)

    (rwkv_out,) = _rows_fwd("rwkv_post", fn_post, post_consts, post_rows, post_params, [[HW]], dtypes=[bf16], tm=256)

    (memn,) = _rows_fwd("rms_mem", _fn_rms, [], [(mem2, [D])], [p["mem_norm_g"]], [[D]], dtypes=[bf16])
    mem_kv = _matmul("proj_memkv", memn, w["w_mem_kv"], "nn")
    mem_out = _mem_fwd(p_mq, mem_kv, batch, seq)

    a_fox = _matmul("out_fox", fox_out, w["w_fox_out"], "nn", out_dtype=bf16)
    a_rwkv = _matmul("out_rwkv", rwkv_out, w["w_rwkv_out"], "nn", out_dtype=bf16)
    a_mem = _matmul("out_mem", mem_out, w["w_mem_out"], "nn", out_dtype=bf16)
    merge_rows = [(a_fox, [D]), (a_rwkv, [D]), (a_mem, [D]), (p_g, [D, D, D])]
    (merged,) = _rows_fwd("merge", _fn_merge, [], merge_rows, [], [[D]], dtypes=[bf16])
    yy = _matmul("out_o", merged, w["w_o"], "nn")
    post1_rows = [(yy, [D]), (x2, [D])]
    post1_params = [p["post1_g"], p["pre2_g"]]
    h1, u2 = _rows_fwd("post1", _fn_post1, [], post1_rows, post1_params, [[D], [D]], dtypes=[f32, bf16])
    gp = _matmul("ffn_gate", u2, w["w_ffn_gate"], "nt", out_dtype=bf16)
    up = _matmul("ffn_up", u2, w["w_ffn_up"], "nt", out_dtype=bf16)
    (hmid,) = _rows_fwd("swiglu", _fn_swiglu, [], [(gp, [D_FF]), (up, [D_FF])], [], [[D_FF]], dtypes=[bf16])
    ffn = _matmul("ffn_down", hmid, w["w_ffn_down"], "nn")
    final_rows = [(ffn, [D]), (h1, [D])]

    gw, gp_ = {}, {}
    (d_ffn, d_h1), (gp_["post2_g"], loss) = _rows_bwd("final", _fn_final, [(tg2, [D])], final_rows, [p["post2_g"]], [], [],
                                                      n_sums=1, dtypes=[bf16, f32])
    d_hmid = _matmul("ffn_down_dx", d_ffn, w["w_ffn_down"], "nt", out_dtype=bf16)
    gw["w_ffn_down"] = _matmul("ffn_down_dw", hmid, d_ffn, "tn", out_dtype=bf16)
    (d_gp, d_up), _ = _rows_bwd("swiglu_bwd", _fn_swiglu, [], [(gp, [D_FF]), (up, [D_FF])], [], [[D_FF]], [d_hmid],
                                dtypes=[bf16, bf16])
    d_u2 = _matmul("ffn_gate_dx", d_gp, w["w_ffn_gate"], "nn")
    d_u2 = _matmul("ffn_up_dx", d_up, w["w_ffn_up"], "nn", add=d_u2)
    gw["w_ffn_gate"] = _matmul("ffn_gate_dw", d_gp, u2, "tn", out_dtype=bf16)
    gw["w_ffn_up"] = _matmul("ffn_up_dw", d_up, u2, "tn", out_dtype=bf16)
    (d_yy, d_x_res), (gp_["post1_g"], gp_["pre2_g"]) = _rows_bwd(
        "post1_bwd", _fn_post1, [], post1_rows, post1_params, [[D], [D]], [d_h1, d_u2], dtypes=[bf16, f32])
    d_merged = _matmul("out_o_dx", d_yy, w["w_o"], "nt", out_dtype=bf16)
    gw["w_o"] = _matmul("out_o_dw", merged, d_yy, "tn", out_dtype=bf16)
    (d_a_fox, d_a_rwkv, d_a_mem, d_p_g), _ = _rows_bwd("merge_bwd", _fn_merge, [], merge_rows, [], [[D]], [d_merged],
                                                       dtypes=[bf16] * 4)
    d_fox_out = _matmul("out_fox_dx", d_a_fox, w["w_fox_out"], "nt")
    gw["w_fox_out"] = _matmul("out_fox_dw", fox_out, d_a_fox, "tn", out_dtype=bf16)
    d_rwkv_out = _matmul("out_rwkv_dx", d_a_rwkv, w["w_rwkv_out"], "nt")
    gw["w_rwkv_out"] = _matmul("out_rwkv_dw", rwkv_out, d_a_rwkv, "tn", out_dtype=bf16)
    d_mem_out = _matmul("out_mem_dx", d_a_mem, w["w_mem_out"], "nt")
    gw["w_mem_out"] = _matmul("out_mem_dw", mem_out, d_a_mem, "tn", out_dtype=bf16)

    d_p_mq, d_km, d_vm = _mem_bwd(p_mq, mem_kv, d_mem_out, batch, seq)
    d_mem_kv = jnp.concatenate([d_km, d_vm], axis=1).astype(bf16)
    gw["w_mem_kv"] = _matmul("proj_memkv_dw", memn, d_mem_kv, "tn", out_dtype=bf16)
    d_memn = _matmul("proj_memkv_dx", d_mem_kv, w["w_mem_kv"], "nt")
    _, (gp_["mem_norm_g"],) = _rows_bwd("rms_mem_bwd", _fn_rms, [], [(mem2, [D])], [p["mem_norm_g"]], [[D]], [d_memn])

    d_q, d_k, d_v, d_cq, d_ck = _fox_bwd(p_qkv, c, c_rows, fox_o, lse, d_fox_out, batch, seq)
    d_p_qkv = jnp.concatenate([d_q, d_k, d_v], axis=1).astype(bf16)
    d_p_f, d_bias = _fox_gate_bwd(p_f, bias, d_cq, d_ck, batch, seq)
    gp_["fox_f_bias"] = d_bias[:, :HEADS]

    (d_y_rw, d_main6_post, d_g_rw), (gp_["rwkv_gn_g"], gp_["rwkv_gn_b"], d_rk) = _rows_bwd(
        "rwkv_post_bwd", fn_post, post_consts, post_rows, post_params, [[HW]], [d_rwkv_out], tm=256)
    gp_["rwkv_r_k"] = d_rk.reshape(1, HEADS, HD)
    d_main6, early_got = _scan_bwd(main6, states, d_y_rw, d_main6_post, batch, seq,
                                   side=(early(gw), True) if early else None)

    def fn_pre_sum(*args):
        return _fn_rwkv_pre(*args)

    (d_ps,), d_pre = _rows_bwd("rwkv_pre_bwd", fn_pre_sum, [], [(ps, rw_widths)], pre_params, [six, [HW]],
                               [d_main6, d_g_rw], tm=256)
    gp_["rwkv_w0"], d_w_up, gp_["rwkv_a0"], d_a_up, gw["rwkv_g_up"], gp_["rwkv_k_k"], gp_["rwkv_k_a"] = d_pre
    gw["rwkv_w_up"], gw["rwkv_a_up"] = d_w_up[:64], d_a_up[:64]
    d_p_r, d_mu = _tokshift_bwd(p_r, mu, d_ps, batch, seq)
    gp_["rwkv_mu"] = _unpad_lora(d_mu)

    gw["w_in"] = _merge_w_in(_matmul("proj_qkv_dw", d_p_qkv, u, "tn", out_dtype=bf16), _matmul("proj_f_dw", d_p_f, u, "tn", out_dtype=bf16),
                             _matmul("proj_rwkv_dw", d_p_r, u, "tn", out_dtype=bf16), _matmul("proj_memq_dw", d_p_mq, u, "tn", out_dtype=bf16),
                             _matmul("proj_gate_dw", d_p_g, u, "tn", out_dtype=bf16))
    d_u, last_got = _sum_nn("proj_dx", [d_p_qkv, d_p_f, d_p_r, d_p_mq, d_p_g], [w_qkv, w_f, w_r, w_mq, w_g3],
                            side=(last(gw), True) if last else None)
    (d_x,), (gp_["pre1_g"],) = _rows_bwd("rms_pre1_bwd", _fn_rms, [], [(x2, [D])], [p["pre1_g"]], [[D]], [d_u], add=d_x_res)
    return loss, d_x.reshape(x.shape), gw, gp_, early_got, last_got


def _rows_add(name, a, b):
    (s,) = _rows_fwd(name, lambda u, v: (u + v,), [], [(a, [a.shape[1]]), (b, [b.shape[1]])], [], [[a.shape[1]]])
    return s


def _adamw(name, recv, row_off, w, m, v):
    _, rows, cols = w.shape
    row_tiles = [t for t in range(16, min(rows, 128) + 1, 16) if rows % t == 0 and row_off % t == 0]
    if row_tiles:
        tr, tc = max(row_tiles), cols
        first, grid = row_off // tr, (rows // tr,)
        at = lambda i: (0, first + i, 0)
        mine = lambda i: (0, i, 0)
    else:
        assert row_off == 0 and recv.shape[1] == rows
        tr, tc = rows, 128
        grid = (cols // tc,)
        at = mine = lambda i: (0, 0, i)

    def body(g_ref, w_ref, m_ref, v_ref, go_ref, d_ref, mo_ref, vo_ref):
        g = g_ref[0].astype(f32)
        for s in range(1, N_DEV):
            g = g + g_ref[s].astype(f32)
        m_new = ADAM_B1 * m_ref[0] + (1.0 - ADAM_B1) * g
        v_new = ADAM_B2 * v_ref[0] + (1.0 - ADAM_B2) * (g * g)
        m_hat = m_new / (1.0 - ADAM_B1 ** ADAM_STEP)
        v_hat = v_new / (1.0 - ADAM_B2 ** ADAM_STEP)
        go_ref[0] = g
        d_ref[0] = -ADAM_LR * (m_hat / (jnp.sqrt(v_hat) + ADAM_EPS) + ADAM_WD * w_ref[0])
        mo_ref[0] = m_new
        vo_ref[0] = v_new

    spec = pl.BlockSpec((1, tr, tc), mine)
    return pl.pallas_call(
        body, name=name, grid=grid,
        in_specs=[pl.BlockSpec((N_DEV, tr, tc), at), spec, spec, spec],
        out_specs=[spec] * 4, out_shape=[jax.ShapeDtypeStruct(w.shape, f32)] * 4,
        compiler_params=_cp(("parallel",)),
    )(recv, w, m, v)


GROUPS = (
    ("in", ("w_in",), 0),
    ("memkv", ("w_mem_kv",), 0),
    ("ffn_gu", ("w_ffn_gate", "w_ffn_up"), 0),
    ("down_o", ("w_ffn_down", "w_o"), 0),
    ("outs", ("w_fox_out", "w_rwkv_out", "w_mem_out"), 0),
    ("lora", ("rwkv_w_up", "rwkv_a_up", "rwkv_g_up"), 0),
)
FIRST_GROUPS = ("in", "memkv")
LATE_GROUPS = (("down_o", "outs", "lora"), ("ffn_gu",))
EARLY_GRAD_GROUPS = ("memkv", "ffn_gu", "down_o", "outs")
LAST_GRAD_GROUPS = ("in", "lora")
SHARD_AXIS = {n: a for n, _, a in SHARDED}
SMALL_ROWS = 16


def _group_local(shards, members, join):
    parts = [shards[n].reshape(shards[n].shape[-2:]) for n in members]
    return parts[0] if len(parts) == 1 else jnp.concatenate(parts, axis=join)


def _group_split(arr, members, join, lead=False):
    out, off = {}, 0
    for n in members:
        shape = dict((k, s) for k, s, _ in SHARDED)[n]
        size = _block_shape(shape, SHARD_AXIS[n])[join]
        idx = [slice(None)] * arr.ndim
        idx[arr.ndim - 2 + join] = slice(off, off + size)
        out[n] = arr[tuple(idx)]
        off += size
    return out


def _full_from_blocks(blocks, axis):
    if axis == 0:
        return blocks.reshape(-1, blocks.shape[2])
    return blocks.transpose(1, 0, 2).reshape(blocks.shape[1], -1)


def _blocks_from_full(full, axis):
    if axis == 0:
        return full.reshape(N_DEV, -1, full.shape[1])
    return full.reshape(full.shape[0], N_DEV, -1).transpose(1, 0, 2)


def _assemble(gathered, names):
    out = {}
    for arr, g in zip(gathered, names):
        _, members, join = [grp for grp in GROUPS if grp[0] == g][0]
        for n, blk in _group_split(arr, members, join, lead=True).items():
            out[n] = _full_from_blocks(blk, SHARD_AXIS[n])
    return out


def _grad_blocks(gw, names):
    out = []
    for g in names:
        _, members, join = [grp for grp in GROUPS if grp[0] == g][0]
        parts = [_blocks_from_full(gw[n].astype(bf16), SHARD_AXIS[n]) for n in members]
        out.append(parts[0] if len(parts) == 1 else jnp.concatenate(parts, axis=1 + join))
    return out


def _small_pack(d):
    flat = jnp.concatenate([d[n].reshape(-1) for n, _ in REPLICATED])
    return jnp.pad(flat, (0, SMALL_ROWS * LANES - REPL_ELEMS)).reshape(SMALL_ROWS, LANES)


def _small_unpack(packed):
    out, flat, off = {}, packed.reshape(-1), 0
    for n, shape in REPLICATED:
        k = _rows_of((LANES,) + shape)
        out[n] = flat[off:off + k].reshape(shape)
        off += k
    return out


def kernel(x, mem, pre1_g, post1_g, pre2_g, post2_g, mem_norm_g, w_in, fox_f_bias, rwkv_mu, rwkv_w0, rwkv_w_up, rwkv_a0, rwkv_a_up, rwkv_g_up, rwkv_k_k, rwkv_k_a, rwkv_r_k, rwkv_gn_g, rwkv_gn_b, w_mem_kv, w_fox_out, w_rwkv_out, w_mem_out, w_o, w_ffn_gate, w_ffn_up, w_ffn_down, loss_target, m_pre1_g, m_post1_g, m_pre2_g, m_post2_g, m_mem_norm_g, m_w_in, m_fox_f_bias, m_rwkv_mu, m_rwkv_w0, m_rwkv_w_up, m_rwkv_a0, m_rwkv_a_up, m_rwkv_g_up, m_rwkv_k_k, m_rwkv_k_a, m_rwkv_r_k, m_rwkv_gn_g, m_rwkv_gn_b, m_w_mem_kv, m_w_fox_out, m_w_rwkv_out, m_w_mem_out, m_w_o, m_w_ffn_gate, m_w_ffn_up, m_w_ffn_down, v_pre1_g, v_post1_g, v_pre2_g, v_post2_g, v_mem_norm_g, v_w_in, v_fox_f_bias, v_rwkv_mu, v_rwkv_w0, v_rwkv_w_up, v_rwkv_a0, v_rwkv_a_up, v_rwkv_g_up, v_rwkv_k_k, v_rwkv_k_a, v_rwkv_r_k, v_rwkv_gn_g, v_rwkv_gn_b, v_w_mem_kv, v_w_fox_out, v_w_rwkv_out, v_w_mem_out, v_w_o, v_w_ffn_gate, v_w_ffn_up, v_w_ffn_down):
    args = dict(locals())
    turn = lambda n, a: jnp.swapaxes(a, 1, 2) if n in TRANSPOSED else a
    wts = {n: turn(n, args[n]) for n in WEIGHT_ORDER}
    ms = {n: turn(n, args["m_" + n]) for n in WEIGHT_ORDER}
    vs = {n: turn(n, args["v_" + n]) for n in WEIGHT_ORDER}

    groups = {g: (members, join) for g, members, join in GROUPS}
    w_bf16 = {n: wts[n].astype(bf16) for n, _, _ in SHARDED}

    def send(g):
        return _group_local(w_bf16, *groups[g])

    first = _exchange("gather_first", [send(g) for g in FIRST_GROUPS], per_peer=False)
    full = _assemble(first, FIRST_GROUPS)
    small_in = {n: (wts[n] if n == "rwkv_r_k" else wts[n].reshape(wts[n].shape[-2:])) for n, _ in REPLICATED}
    late = ([send(g) for g in LATE_GROUPS[0]], [send(g) for g in LATE_GROUPS[1]],
            lambda got, which: _assemble(got, LATE_GROUPS[which]))
    loss_part, grad_x, gw, gp, early_got, last_got = _local_step(
        x, mem, loss_target, full, small_in, late=late, early=lambda g: _grad_blocks(g, EARLY_GRAD_GROUPS),
        last=lambda g: _grad_blocks(g, LAST_GRAD_GROUPS))
    (small_got,) = _exchange("exchange_small", [_small_pack(gp).astype(bf16)], per_peer=False)
    received = dict(zip(EARLY_GRAD_GROUPS + LAST_GRAD_GROUPS, list(early_got) + list(last_got)))

    outs = [{}, {}, {}, {}]
    for g, members, _ in GROUPS:
        off = 0
        for n in members:
            for o, arr in zip(outs, _adamw("adamw_" + n, received[g], off, wts[n], ms[n], vs[n])):
                o[n] = arr
            off += wts[n].shape[1]
    res = _adamw("adamw_small", small_got, 0, *[_small_pack(d)[None] for d in (wts, ms, vs)])
    for o, arr in zip(outs, res):
        o.update(_small_unpack(arr))
    loss = lax.psum(loss_part[0, 0], ("x", "y", "c"))
    return (loss, grad_x, *[turn(n, o[n].reshape(wts[n].shape)) for o in outs for n in WEIGHT_ORDER])
```

```python
import functools

import jax
import jax.numpy as jnp
from jax import lax
from jax.experimental import pallas as pl
from jax.experimental.pallas import tpu as pltpu

f32 = jnp.float32
bf16 = jnp.bfloat16
_HI = lax.Precision.HIGHEST

D = 1024
HEADS = 8
HD = 64
HW = HEADS * HD
MEM_HEADS = 4
MEM_HD = 128
MEM_W = 512
MEM_LEN = 256
D_FF = 2816
LORA_PAD = 128
NORM_EPS = 1e-6
GN_EPS = 64e-5
SCAN_CHUNK = 64
N_DEV = 8
LANES = 1024
VMEM_LIMIT = 56 * 1024 * 1024

ADAM_LR = 0.001
ADAM_B1 = 0.9
ADAM_B2 = 0.999
ADAM_EPS = 1e-08
ADAM_WD = 0.01
ADAM_STEP = 10

TRANSPOSED = ("w_in", "w_ffn_gate", "w_ffn_up")
SHARDED = (
    ("w_in", (6920, 1024), 0),
    ("w_ffn_gate", (2816, 1024), 0),
    ("w_ffn_up", (2816, 1024), 0),
    ("w_ffn_down", (2816, 1024), 0),
    ("w_mem_kv", (1024, 1024), 0),
    ("w_o", (1024, 1024), 0),
    ("w_fox_out", (512, 1024), 1),
    ("w_rwkv_out", (512, 1024), 1),
    ("w_mem_out", (512, 1024), 1),
    ("rwkv_w_up", (64, 512), 1),
    ("rwkv_a_up", (64, 512), 1),
    ("rwkv_g_up", (128, 512), 1),
)
REPLICATED = (
    ("pre1_g", (1, 1024)), ("post1_g", (1, 1024)), ("pre2_g", (1, 1024)), ("post2_g", (1, 1024)),
    ("mem_norm_g", (1, 1024)), ("fox_f_bias", (1, 8)), ("rwkv_mu", (1, 1792)), ("rwkv_w0", (1, 512)),
    ("rwkv_a0", (1, 512)), ("rwkv_k_k", (1, 512)), ("rwkv_k_a", (1, 512)), ("rwkv_r_k", (1, 8, 64)),
    ("rwkv_gn_g", (1, 512)), ("rwkv_gn_b", (1, 512)),
)
WEIGHT_ORDER = ('pre1_g', 'post1_g', 'pre2_g', 'post2_g', 'mem_norm_g', 'w_in', 'fox_f_bias', 'rwkv_mu',
                'rwkv_w0', 'rwkv_w_up', 'rwkv_a0', 'rwkv_a_up', 'rwkv_g_up', 'rwkv_k_k', 'rwkv_k_a',
                'rwkv_r_k', 'rwkv_gn_g', 'rwkv_gn_b', 'w_mem_kv', 'w_fox_out', 'w_rwkv_out', 'w_mem_out',
                'w_o', 'w_ffn_gate', 'w_ffn_up', 'w_ffn_down')


def _block_shape(shape, axis):
    return tuple(s // N_DEV if i == axis else s for i, s in enumerate(shape))


def _rows_of(shape):
    n = 1
    for s in shape:
        n *= s
    return n // LANES


REPL_ELEMS = sum(_rows_of((LANES,) + s) for _, s in REPLICATED)


def _cp(sem=None):
    return pltpu.CompilerParams(dimension_semantics=sem, vmem_limit_bytes=VMEM_LIMIT)


def _tile(dim, cap):
    best = None
    for t in range(128, min(dim, cap) + 1, 128):
        if dim % t == 0:
            best = t
    return best if best is not None else dim


def _two_terms(x):
    hi = x.astype(bf16)
    return hi, (x - hi.astype(f32)).astype(bf16)


def _dg(a, b, dims, exact):
    if exact == "split":
        (a_hi, a_lo), (b_hi, b_lo) = _two_terms(a), _two_terms(b)
        dot = functools.partial(lax.dot_general, dimension_numbers=dims, preferred_element_type=f32)
        return dot(a_hi, b_hi) + (dot(a_hi, b_lo) + dot(a_lo, b_hi))
    if exact:
        return lax.dot_general(a, b, dims, precision=_HI, preferred_element_type=f32)
    return lax.dot_general(a.astype(bf16), b.astype(bf16), dims, preferred_element_type=f32)


def _make_mm(batched, exact):
    o = 1 if batched else 0
    bd = ((0,), (0,)) if batched else ((), ())
    d_nn = (((1 + o,), (o,)), bd)
    d_nt = (((1 + o,), (1 + o,)), bd)
    d_tn = (((o,), (o,)), bd)

    @jax.custom_vjp
    def nn(a, b):
        return _dg(a, b, d_nn, exact)

    @jax.custom_vjp
    def nt(a, b):
        return _dg(a, b, d_nt, exact)

    @jax.custom_vjp
    def tn(a, b):
        return _dg(a, b, d_tn, exact)

    nn.defvjp(lambda a, b: (_dg(a, b, d_nn, exact), (a, b)),
              lambda res, g: (_dg(g, res[1], d_nt, exact), _dg(res[0], g, d_tn, exact)))
    nt.defvjp(lambda a, b: (_dg(a, b, d_nt, exact), (a, b)),
              lambda res, g: (_dg(g, res[1], d_nn, exact), _dg(g, res[0], d_tn, exact)))
    tn.defvjp(lambda a, b: (_dg(a, b, d_tn, exact), (a, b)),
              lambda res, g: (_dg(res[1], g, d_nt, exact), _dg(res[0], g, d_nn, exact)))
    return nn, nt, tn


def _sigmoid(x):
    return 1.0 / (1.0 + jnp.exp(-x))


def _head_sum_raw(x):
    width = 2 * HD
    i = lax.broadcasted_iota(jnp.int32, (width, width), 0) // HD
    j = lax.broadcasted_iota(jnp.int32, (width, width), 1) // HD
    m = (i == j).astype(bf16)
    dims = (((1,), (0,)), ((), ()))
    out = []
    for p in range(x.shape[1] // width):
        xp = x[:, p * width:(p + 1) * width]
        hi = xp.astype(bf16)
        lo = (xp - hi.astype(f32)).astype(bf16)
        out.append(lax.dot_general(hi, m, dims, preferred_element_type=f32)
                   + lax.dot_general(lo, m, dims, preferred_element_type=f32))
    return jnp.concatenate(out, axis=1)


@jax.custom_vjp
def _head_sum(x):
    return _head_sum_raw(x)


_head_sum.defvjp(lambda x: (_head_sum_raw(x), None), lambda _, g: (_head_sum_raw(g),))


WEIGHT_TILE_BYTES = 13 * 512 * 1024
ACC_TILE_BYTES = 8 * 1024 * 1024


def _matmul(name, a, b, mode, add=None, out_dtype=f32):
    has_add = add is not None
    if mode == "tn":
        (k, m), (_, n) = a.shape, b.shape
        tn = _tile(n, max(128, ACC_TILE_BYTES // (4 * m)))
        tk = _tile(k, 1024)

        nk = k // tk

        def body(a_ref, b_ref, o_ref, acc):
            @pl.when(pl.program_id(1) == 0)
            def _():
                acc[...] = jnp.zeros_like(acc)

            acc[...] += lax.dot_general(a_ref[...].astype(bf16), b_ref[...].astype(bf16),
                                        (((0,), (0,)), ((), ())), preferred_element_type=f32)

            @pl.when(pl.program_id(1) == nk - 1)
            def _():
                o_ref[...] = acc[...].astype(o_ref.dtype)

        return pl.pallas_call(
            body, name=name, grid=(n // tn, nk),
            in_specs=[pl.BlockSpec((tk, m), lambda j, kk: (kk, 0)), pl.BlockSpec((tk, tn), lambda j, kk: (kk, j))],
            out_specs=pl.BlockSpec((m, tn), lambda j, kk: (0, j)), out_shape=jax.ShapeDtypeStruct((m, n), out_dtype),
            scratch_shapes=[pltpu.VMEM((m, tn), f32)],
            compiler_params=_cp(("parallel", "arbitrary")),
        )(a, b)

    (m, k) = a.shape
    n = b.shape[1] if mode == "nn" else b.shape[0]
    tm = _tile(m, 1024)
    tn = _tile(n, max(128, WEIGHT_TILE_BYTES // (2 * k)))
    dims = (((1,), (0,)), ((), ())) if mode == "nn" else (((1,), (1,)), ((), ()))
    b_spec = pl.BlockSpec((k, tn), lambda j, i: (0, j)) if mode == "nn" else pl.BlockSpec((tn, k), lambda j, i: (j, 0))
    o_spec = pl.BlockSpec((tm, tn), lambda j, i: (i, j))

    def body(*refs):
        a_ref, b_ref = refs[0], refs[1]
        o_ref = refs[-1]
        r = lax.dot_general(a_ref[...].astype(bf16), b_ref[...].astype(bf16), dims, preferred_element_type=f32)
        if has_add:
            r = r + refs[2][...]
        o_ref[...] = r.astype(o_ref.dtype)

    return pl.pallas_call(
        body, name=name, grid=(n // tn, m // tm),
        in_specs=[pl.BlockSpec((tm, k), lambda j, i: (i, 0)), b_spec] + ([o_spec] if has_add else []),
        out_specs=o_spec, out_shape=jax.ShapeDtypeStruct((m, n), out_dtype),
        compiler_params=_cp(("parallel", "arbitrary")),
    )(*((a, b, add) if has_add else (a, b)))


def _input_cotangent(name, a_list, b_list, x, gain, add, side=None):
    m = a_list[0].shape[0]
    tm = _tile(m, 256)
    n_g = len(a_list)
    srcs, per_peer = side if side is not None else ([], False)
    n_s = len(srcs)

    def body(*refs):
        x_ref, g_ref, add_ref = refs[2 * n_g:2 * n_g + 3]
        src_refs = refs[2 * n_g + 3:2 * n_g + 3 + n_s]
        dx_ref, dg_ref = refs[2 * n_g + 3 + n_s:2 * n_g + 5 + n_s]
        _side_exchange(src_refs, refs[2 * n_g + 5 + n_s:2 * n_g + 5 + 2 * n_s], per_peer, refs[2 * n_g + 5 + 2 * n_s:], m // tm)
        d_u = None
        for g in range(n_g):
            r = lax.dot_general(refs[g][...].astype(bf16), refs[n_g + g][...].astype(bf16), (((1,), (0,)), ((), ())),
                                preferred_element_type=f32)
            d_u = r if d_u is None else d_u + r
        _, vjp = jax.vjp(_rms, x_ref[...], g_ref[...])
        d_x, d_gain = vjp(d_u)
        dx_ref[...] = d_x + add_ref[...]

        @pl.when(pl.program_id(0) == 0)
        def _():
            dg_ref[...] = jnp.zeros_like(dg_ref)

        dg_ref[...] += d_gain

    rows = pl.BlockSpec((tm, x.shape[1]), lambda i: (i, 0))
    whole = lambda b: pl.BlockSpec(b.shape, lambda i: (0, 0))
    res = pl.pallas_call(
        body, name=name, grid=(m // tm,),
        in_specs=[pl.BlockSpec((tm, a.shape[1]), lambda i: (i, 0)) for a in a_list] + [whole(b) for b in b_list]
        + [rows, whole(gain), rows] + [_HBM_SPEC] * n_s,
        out_specs=[rows, whole(gain)] + [_HBM_SPEC] * n_s,
        out_shape=[jax.ShapeDtypeStruct(x.shape, f32), jax.ShapeDtypeStruct(gain.shape, f32)] + _side_out_shapes(srcs, per_peer),
        scratch_shapes=_side_sems(n_s),
        compiler_params=_cp(("arbitrary",)),
    )(*a_list, *b_list, x, gain, add, *srcs)
    return res[0], res[1], list(res[2:])


def _pieces(ref, widths):
    out, off = [], 0
    for w in widths:
        out.append(ref[:, off:off + w].astype(f32))
        off += w
    return out


def _store_pieces(ref, widths, vals, add_ref=None):
    off = 0
    for w, v in zip(widths, vals):
        ref[:, off:off + w] = (v if add_ref is None else v + add_ref[:, off:off + w]).astype(ref.dtype)
        off += w


def _rows_fwd(name, fn, consts, rows, params, outs, n_sums=0, tm=512, dtypes=None):
    t = (consts + rows)[0][0].shape[0]
    tm = min(tm, t)
    ins = consts + rows
    n_in, n_p, n_o = len(ins), len(params), len(outs)
    dtypes = dtypes or [f32] * n_o

    def body(*refs):
        in_refs, p_refs = refs[:n_in], refs[n_in:n_in + n_p]
        o_refs, s_refs = refs[n_in + n_p:n_in + n_p + n_o], refs[n_in + n_p + n_o:]
        vals = []
        for r, (_, widths) in zip(in_refs, ins):
            vals += _pieces(r, widths)
        res = fn(*vals, *[p[...] for p in p_refs])
        pos = 0
        for r, widths in zip(o_refs, outs):
            _store_pieces(r, widths, res[pos:pos + len(widths)])
            pos += len(widths)

        @pl.when(pl.program_id(0) == 0)
        def _():
            for s in s_refs:
                s[...] = jnp.zeros_like(s)

        for s, v in zip(s_refs, res[pos:]):
            s[...] += v

    row_spec = lambda w: pl.BlockSpec((tm, w), lambda i: (i, 0))
    full = lambda p: pl.BlockSpec(p.shape, lambda i: (0,) * p.ndim)
    return pl.pallas_call(
        body, name=name, grid=(t // tm,),
        in_specs=[row_spec(a.shape[1]) for a, _ in ins] + [full(p) for p in params],
        out_specs=[row_spec(sum(w)) for w in outs] + [pl.BlockSpec((1, 1), lambda i: (0, 0))] * n_sums,
        out_shape=[jax.ShapeDtypeStruct((t, sum(w)), dt) for w, dt in zip(outs, dtypes)] + [jax.ShapeDtypeStruct((1, 1), f32)] * n_sums,
        compiler_params=_cp(("arbitrary",)),
    )(*[a for a, _ in ins], *params)


def _rows_bwd(name, fn, consts, rows, params, outs, cts, n_sums=0, add=None, tm=512, dtypes=None):
    t = (consts + rows)[0][0].shape[0]
    tm = min(tm, t)
    n_c, n_r, n_p, n_o = len(consts), len(rows), len(params), len(outs)
    has_add = add is not None
    dtypes = dtypes or [f32] * n_r

    def body(*refs):
        pos = 0
        c_refs = refs[pos:pos + n_c]; pos += n_c
        r_refs = refs[pos:pos + n_r]; pos += n_r
        p_refs = refs[pos:pos + n_p]; pos += n_p
        ct_refs = refs[pos:pos + n_o]; pos += n_o
        add_ref = refs[pos] if has_add else None
        pos += 1 if has_add else 0
        dr_refs = refs[pos:pos + n_r]; pos += n_r
        dp_refs = refs[pos:pos + n_p]; pos += n_p
        s_refs = refs[pos:pos + n_sums]
        cvals, rvals = [], []
        for r, (_, widths) in zip(c_refs, consts):
            cvals += _pieces(r, widths)
        for r, (_, widths) in zip(r_refs, rows):
            rvals += _pieces(r, widths)
        pvals = [p[...] for p in p_refs]
        ctv = []
        for r, widths in zip(ct_refs, outs):
            ctv += _pieces(r, widths)
        ctv += [jnp.ones((1, 1), f32)] * n_sums
        primal, vjp = jax.vjp(lambda *rp: tuple(fn(*cvals, *rp)), *rvals, *pvals)
        g = vjp(tuple(ctv))
        pos = 0
        for idx, (r, (_, widths)) in enumerate(zip(dr_refs, rows)):
            _store_pieces(r, widths, g[pos:pos + len(widths)], add_ref if idx == 0 else None)
            pos += len(widths)

        @pl.when(pl.program_id(0) == 0)
        def _():
            for acc in list(dp_refs) + list(s_refs):
                acc[...] = jnp.zeros_like(acc)

        for dp, v in zip(dp_refs, g[pos:]):
            dp[...] += v
        for s, v in zip(s_refs, primal[len(primal) - n_sums:]):
            s[...] += v

    row_spec = lambda w: pl.BlockSpec((tm, w), lambda i: (i, 0))
    full = lambda p: pl.BlockSpec(p.shape, lambda i: (0,) * p.ndim)
    args = [a for a, _ in consts + rows] + list(params) + list(cts) + ([add] if has_add else [])
    res = pl.pallas_call(
        body, name=name, grid=(t // tm,),
        in_specs=[row_spec(a.shape[1]) for a, _ in consts + rows] + [full(p) for p in params]
        + [row_spec(sum(w)) for w in outs] + ([row_spec(add.shape[1])] if has_add else []),
        out_specs=[row_spec(a.shape[1]) for a, _ in rows] + [full(p) for p in params]
        + [pl.BlockSpec((1, 1), lambda i: (0, 0))] * n_sums,
        out_shape=[jax.ShapeDtypeStruct(a.shape, dt) for (a, _), dt in zip(rows, dtypes)]
        + [jax.ShapeDtypeStruct(p.shape, f32) for p in params] + [jax.ShapeDtypeStruct((1, 1), f32)] * n_sums,
        compiler_params=_cp(("arbitrary",)),
    )(*args)
    return res[:n_r], res[n_r:n_r + n_p] + res[n_r + n_p:]


def _rms(x, g):
    return x * lax.rsqrt(jnp.mean(x * x, axis=-1, keepdims=True) + NORM_EPS) * g


def _fn_rms(x, g):
    return (_rms(x, g),)


def _fn_rwkv_pre(r, k, v, wd, ad, gd, w0, w_up, a0, a_up, g_up, k_k, k_a):
    nn, _, _ = _make_mm(False, False)
    w_log = -_sigmoid(w0 + nn(jnp.tanh(wd), w_up)) * 0.6065306597126334
    a = _sigmoid(a0 + nn(ad, a_up))
    g = nn(_sigmoid(gd), g_up)
    kk = k * k_k
    kk = kk * lax.rsqrt(jnp.maximum(_head_sum(kk * kk), 1e-24))
    k2 = k * (1.0 + (a - 1.0) * k_a)
    return r, w_log, k2, v, -kk, kk * a, g


def _fn_rwkv_post(y, r, k2, v, g, gn_g, gn_b, r_k):
    mean = _head_sum(y) * (1.0 / HD)
    yc = y - mean
    var = _head_sum(yc * yc) * (1.0 / HD)
    yn = yc * lax.rsqrt(var + GN_EPS) * gn_g + gn_b
    bonus = _head_sum(r * k2 * r_k) * v
    return ((yn + bonus) * g,)


def _fn_merge(a_fox, a_rwkv, a_mem, g_fox, g_rwkv, g_mem):
    return (_sigmoid(g_fox) * a_fox + _sigmoid(g_rwkv) * a_rwkv + _sigmoid(g_mem) * a_mem,)


def _fn_post1(y, x, post1_g, pre2_g):
    h1 = x + _rms(y, post1_g)
    return h1, _rms(h1, pre2_g)


def _fn_swiglu(gp, up):
    return (gp * _sigmoid(gp) * up,)


def _fn_final(target, ffn, h1, post2_g):
    err = h1 + _rms(ffn, post2_g) - target
    per_row = jnp.mean(err * err, axis=-1, keepdims=True)
    return (0.5 * jnp.sum(per_row, axis=0, keepdims=True),)


def _shift_down(x):
    row = lax.broadcasted_iota(jnp.int32, x.shape, 0)
    return jnp.where(row == 0, 0.0, pltpu.roll(x, 1, 0))


def _shift_up(x):
    s = x.shape[0]
    row = lax.broadcasted_iota(jnp.int32, x.shape, 0)
    return jnp.where(row == s - 1, 0.0, pltpu.roll(x, s - 1, 0))


def _tokshift_fwd(p, mu, batch, seq):
    w = p.shape[1]
    tc = _tile(w, 384)

    def body(p_ref, mu_ref, o_ref):
        x = p_ref[...]
        o_ref[...] = x + (_shift_down(x) - x) * mu_ref[...]

    return pl.pallas_call(
        body, name="tokshift_fwd", grid=(w // tc, batch),
        in_specs=[pl.BlockSpec((seq, tc), lambda j, b: (b, j)), pl.BlockSpec((1, tc), lambda j, b: (0, j))],
        out_specs=pl.BlockSpec((seq, tc), lambda j, b: (b, j)),
        out_shape=jax.ShapeDtypeStruct(p.shape, f32),
        compiler_params=_cp(("parallel", "arbitrary")),
    )(p, mu)


def _tokshift_bwd(p, mu, dps, batch, seq):
    w = p.shape[1]
    tc = _tile(w, 384)

    def body(p_ref, mu_ref, d_ref, dp_ref, dmu_ref):
        x, mu_v, d = p_ref[...], mu_ref[...], d_ref[...]
        dp_ref[...] = (d * (1.0 - mu_v) + _shift_up(d * mu_v)).astype(dp_ref.dtype)

        @pl.when(pl.program_id(1) == 0)
        def _():
            dmu_ref[...] = jnp.zeros_like(dmu_ref)

        dmu_ref[...] += jnp.sum(d * (_shift_down(x) - x), axis=0, keepdims=True)

    return pl.pallas_call(
        body, name="tokshift_bwd", grid=(w // tc, batch),
        in_specs=[pl.BlockSpec((seq, tc), lambda j, b: (b, j)), pl.BlockSpec((1, tc), lambda j, b: (0, j)),
                  pl.BlockSpec((seq, tc), lambda j, b: (b, j))],
        out_specs=[pl.BlockSpec((seq, tc), lambda j, b: (b, j)), pl.BlockSpec((1, tc), lambda j, b: (0, j))],
        out_shape=[jax.ShapeDtypeStruct(p.shape, bf16), jax.ShapeDtypeStruct(mu.shape, f32)],
        compiler_params=_cp(("parallel", "arbitrary")),
    )(p, mu, dps)


def _cum_block(seq):
    return _tile(seq, 256)


def _fox_gate_fwd(f, bias, batch, seq):
    cb = _cum_block(seq)

    def body(f_ref, b_ref, c_ref):
        row = lax.broadcasted_iota(jnp.int32, (cb, cb), 0)
        col = lax.broadcasted_iota(jnp.int32, (cb, cb), 1)
        tri = (col <= row).astype(f32)
        carry = jnp.zeros((1, 128), f32)
        for i in range(seq // cb):
            z = f_ref[i * cb:(i + 1) * cb, :] + b_ref[...]
            ls = jnp.minimum(z, 0.0) - jnp.log(1.0 + jnp.exp(-jnp.abs(z)))
            c = _dg(tri, ls, (((1,), (0,)), ((), ())), True) + carry
            c_ref[i * cb:(i + 1) * cb, :] = c
            carry = c[cb - 1:cb, :]

    return pl.pallas_call(
        body, name="fox_gate_fwd", grid=(batch,),
        in_specs=[pl.BlockSpec((seq, 128), lambda b: (b, 0)), pl.BlockSpec((1, 128), lambda b: (0, 0))],
        out_specs=pl.BlockSpec((seq, 128), lambda b: (b, 0)),
        out_shape=jax.ShapeDtypeStruct(f.shape, f32),
        compiler_params=_cp(("arbitrary",)),
    )(f, bias)


def _fox_gate_bwd(f, bias, dc_a, dc_b, batch, seq):
    cb = _cum_block(seq)

    def body(f_ref, b_ref, da_ref, db_ref, df_ref, dbias_ref):
        row = lax.broadcasted_iota(jnp.int32, (cb, cb), 0)
        col = lax.broadcasted_iota(jnp.int32, (cb, cb), 1)
        triu = (col >= row).astype(f32)

        @pl.when(pl.program_id(0) == 0)
        def _():
            dbias_ref[...] = jnp.zeros_like(dbias_ref)

        lane = lax.broadcasted_iota(jnp.int32, (1, 128), 1)

        def by_head(blk):
            out = jnp.zeros((cb, 128), f32)
            for p in range(HEADS // 2):
                for e in range(2):
                    out = jnp.where(lane == 2 * p + e, _pick_lane(blk[:, p * 128:(p + 1) * 128], e), out)
            return out

        carry = jnp.zeros((1, 128), f32)
        tot = jnp.zeros((1, 128), f32)
        for i in reversed(range(seq // cb)):
            sl = slice(i * cb, (i + 1) * cb)
            dc = by_head(da_ref[sl, :] + db_ref[sl, :])
            dls = _dg(triu, dc, (((1,), (0,)), ((), ())), True) + carry
            carry = dls[0:1, :]
            df = dls * _sigmoid(-(f_ref[sl, :] + b_ref[...]))
            df_ref[sl, :] = df.astype(df_ref.dtype)
            tot = tot + jnp.sum(df, axis=0, keepdims=True)
        dbias_ref[...] += tot

    return pl.pallas_call(
        body, name="fox_gate_bwd", grid=(batch,),
        in_specs=[pl.BlockSpec((seq, 128), lambda b: (b, 0)), pl.BlockSpec((1, 128), lambda b: (0, 0)),
                  pl.BlockSpec((seq, HW), lambda b: (b, 0)), pl.BlockSpec((seq, HW), lambda b: (b, 0))],
        out_specs=[pl.BlockSpec((seq, 128), lambda b: (b, 0)), pl.BlockSpec((1, 128), lambda b: (0, 0))],
        out_shape=[jax.ShapeDtypeStruct(f.shape, bf16), jax.ShapeDtypeStruct((1, 128), f32)],
        compiler_params=_cp(("arbitrary",)),
    )(f, bias, dc_a, dc_b)


_HBM_SPEC = pl.BlockSpec(memory_space=pltpu.HBM)


def _side_out_shapes(srcs, per_peer):
    return [jax.ShapeDtypeStruct(((N_DEV,) + tuple(s.shape[1:] if per_peer else s.shape)), s.dtype) for s in srcs]


def _side_sems(n):
    if n == 0:
        return []
    return [pltpu.SemaphoreType.DMA((n, N_DEV - 1)), pltpu.SemaphoreType.DMA((n, N_DEV - 1)), pltpu.SemaphoreType.DMA((n,))]


def _peer_copies(src_refs, dst_refs, per_peer, sems):
    send_sems, recv_sems, local_sems = sems
    x, y, c = lax.axis_index("x"), lax.axis_index("y"), lax.axis_index("c")
    me = 4 * x + 2 * y + c

    def remote(src, dst, t, k, to):
        return pltpu.make_async_remote_copy(src_ref=src, dst_ref=dst, send_sem=send_sems.at[t, k - 1],
                                            recv_sem=recv_sems.at[t, k - 1], device_id=to,
                                            device_id_type=pl.DeviceIdType.MESH)

    direct, relays = [], []
    for t, (s, d) in enumerate(zip(src_refs, dst_refs)):
        direct.append((t, 0, pltpu.make_async_copy(s.at[me] if per_peer else s, d.at[me], local_sems.at[t])))
        for k in range(1, N_DEV):
            px = 1 - x if k & 4 else x
            py = 1 - y if k & 2 else y
            pc = 1 - c if k & 1 else c
            if per_peer:
                direct.append((t, k, remote(s.at[4 * px + 2 * py + pc], d.at[me], t, k, (px, py, pc))))
            elif k == 1 or not k & 1:
                direct.append((t, k, remote(s, d.at[me], t, k, (px, py, pc))))
            else:
                origin = d.at[4 * px + 2 * py + c]
                relays.append((t, k - 1, remote(origin, origin, t, k, (x, y, 1 - c))))
    return direct, relays


def _exchange_start(direct):
    for _, _, cp in direct:
        cp.start()


def _exchange_finish(direct, relays):
    landed = {(t, k): cp for t, k, cp in direct}
    for t, j, cp in relays:
        landed[(t, j)].wait_recv()
        cp.start()
    relayed = {(t, j) for t, j, _ in relays}
    for t, k, cp in direct:
        if k == 0:
            cp.wait()
        else:
            cp.wait_send()
            if (t, k) not in relayed:
                cp.wait_recv()
    for _, _, cp in relays:
        cp.wait()


def _side_exchange(src_refs, dst_refs, per_peer, sems, *grid):
    if not src_refs:
        return
    first = functools.reduce(jnp.logical_and, [pl.program_id(a) == 0 for a in range(len(grid))])
    last = functools.reduce(jnp.logical_and, [pl.program_id(a) == n - 1 for a, n in enumerate(grid)])

    @pl.when(first)
    def _():
        _exchange_start(_peer_copies(src_refs, dst_refs, per_peer, sems)[0])

    @pl.when(last)
    def _():
        _exchange_finish(*_peer_copies(src_refs, dst_refs, per_peer, sems))


def _exchange(name, srcs, per_peer):
    n = len(srcs)

    def body(*refs):
        direct, relays = _peer_copies(refs[:n], refs[n:2 * n], per_peer, refs[2 * n:])
        _exchange_start(direct)
        _exchange_finish(direct, relays)

    return pl.pallas_call(
        body, name=name, in_specs=[_HBM_SPEC] * n, out_specs=[_HBM_SPEC] * n,
        out_shape=_side_out_shapes(srcs, per_peer), scratch_shapes=_side_sems(n),
    )(*srcs)


FOX_T = 512
_NEG = -1e30
_D2 = (((1,), (1,)), ((), ()))
_D1 = (((1,), (0,)), ((), ()))
_D0 = (((0,), (0,)), ((), ()))


def _bdot(a, b, dims):
    return lax.dot_general(a.astype(bf16), b.astype(bf16), dims, preferred_element_type=f32)


def _pick_lane(x, lane):
    idx = lax.broadcasted_iota(jnp.int32, x.shape, 1)
    return jnp.sum(jnp.where(idx == lane, x, 0.0), axis=1, keepdims=True)


def _pick_row(x, row):
    idx = lax.broadcasted_iota(jnp.int32, x.shape, 0)
    return jnp.sum(jnp.where(idx == row, x, 0.0), axis=0, keepdims=True)


def _fox_fwd(qkv, c, c_rows, batch, seq, side=None):
    t = min(FOX_T, seq)
    nq = seq // t
    scale = HD ** -0.5
    srcs, per_peer = side if side is not None else ([], False)
    n_s = len(srcs)

    def body(*refs):
        q_ref, k_ref, v_ref, cq_ref, ck_ref = refs[:5]
        o_ref, lse_ref = refs[5 + n_s:7 + n_s]
        _side_exchange(refs[5:5 + n_s], refs[7 + n_s:7 + 2 * n_s], per_peer, refs[7 + 2 * n_s:], batch, PAIRS, nq)
        pair, i = pl.program_id(1), pl.program_id(2)
        lane = lax.broadcasted_iota(jnp.int32, (1, PAIR_W), 1)
        first = (lane // HD) == 0
        mine = [first, jnp.logical_not(first)]
        q = q_ref[...] * scale
        qs = [jnp.where(mine[e], q, 0.0) for e in range(2)]
        cqs = [_pick_lane(cq_ref[...], 2 * pair + e) for e in range(2)]
        causal = lax.broadcasted_iota(jnp.int32, (t, t), 1) <= lax.broadcasted_iota(jnp.int32, (t, t), 0)

        def block(j, carry, diagonal):
            rows = pl.ds(pl.multiple_of(j * t, t), t)
            kj, vj = k_ref[rows, :], v_ref[rows, :]
            ck_blk = ck_ref[0, :, rows]
            out = []
            for e in range(2):
                m, acc = carry[2 * e:2 * e + 2]
                s = _bdot(qs[e], kj, _D2) + cqs[e] - _pick_row(ck_blk, 2 * pair + e)
                if diagonal:
                    s = jnp.where(causal, s, _NEG)
                m_new = jnp.maximum(m, jnp.max(s, axis=1, keepdims=True))
                p = jnp.exp(s - m_new)
                out += [m_new, jnp.exp(m - m_new) * acc + _bdot(p, jnp.where(mine[e], vj, 1.0), _D1)]
            return tuple(out)

        init = (jnp.full((t, 1), _NEG, f32), jnp.zeros((t, PAIR_W), f32)) * 2
        carry = lax.fori_loop(0, i, lambda j, cr: block(j, cr, False), init)
        m0, a0, m1, a1 = block(i, carry, True)
        l0, l1 = _pick_lane(a0, HD), _pick_lane(a1, 0)
        o_ref[...] = jnp.where(first, a0 / l0, a1 / l1)
        lse_ref[...] = jnp.where(lane == 0, m0 + jnp.log(l0), jnp.where(lane == 1, m1 + jnp.log(l1), 0.0))

    q_spec = pl.BlockSpec((t, PAIR_W), lambda b, p, i: (b * nq + i, p))
    res = pl.pallas_call(
        body, name="fox_attn_fwd", grid=(batch, PAIRS, nq),
        in_specs=[q_spec,
                  pl.BlockSpec((seq, PAIR_W), lambda b, p, i: (b, PAIRS + p)),
                  pl.BlockSpec((seq, PAIR_W), lambda b, p, i: (b, 2 * PAIRS + p)),
                  pl.BlockSpec((t, 128), lambda b, p, i: (b * nq + i, 0)),
                  pl.BlockSpec((1, 8, seq), lambda b, p, i: (b, 0, 0))] + [_HBM_SPEC] * n_s,
        out_specs=[q_spec, q_spec] + [_HBM_SPEC] * n_s,
        out_shape=[jax.ShapeDtypeStruct((batch * seq, HW), f32)] * 2 + _side_out_shapes(srcs, per_peer),
        scratch_shapes=_side_sems(n_s),
        compiler_params=_cp(("arbitrary", "arbitrary", "arbitrary")),
    )(qkv, qkv, qkv, c, c_rows, *srcs)
    return res[0], res[1], list(res[2:])


def _fox_bwd(qkv, c, c_rows, o, lse, do, batch, seq):
    t = min(FOX_T, seq)
    nq = seq // t
    scale = HD ** -0.5

    def body(q_ref, k_ref, v_ref, cq_ref, ck_ref, o_ref, lse_ref, do_ref,
             dq_ref, dk_ref, dv_ref, dcq_ref, dck_ref, acc0, acc1):
        pair, i = pl.program_id(1), pl.program_id(2)
        accs = [acc0, acc1]

        @pl.when(i == 0)
        def _():
            dv_ref[...] = jnp.zeros_like(dv_ref)
            acc0[...] = jnp.zeros_like(acc0)
            acc1[...] = jnp.zeros_like(acc1)

        lane = lax.broadcasted_iota(jnp.int32, (1, PAIR_W), 1)
        first = (lane // HD) == 0
        mine = [first, jnp.logical_not(first)]
        q, d_o, o_i = q_ref[...] * scale, do_ref[...], o_ref[...]
        q0s = [jnp.where(mine[e], q, 0.0) for e in range(2)]
        q1s = [jnp.where(mine[e], q, 1.0) for e in range(2)]
        dos = [jnp.where(mine[e], d_o, 0.0) for e in range(2)]
        deltas = [jnp.sum(dos[e] * o_i, axis=1, keepdims=True) for e in range(2)]
        lses = [_pick_lane(lse_ref[...], e) for e in range(2)]
        cqs = [_pick_lane(cq_ref[...], 2 * pair + e) for e in range(2)]
        causal = lax.broadcasted_iota(jnp.int32, (t, t), 1) <= lax.broadcasted_iota(jnp.int32, (t, t), 0)

        def block(j, dqs, diagonal):
            rows = pl.ds(pl.multiple_of(j * t, t), t)
            kj, vj = k_ref[rows, :], v_ref[rows, :]
            ck_blk = ck_ref[0, :, rows]
            out = []
            for e in range(2):
                s = _bdot(q0s[e], kj, _D2) + cqs[e] - _pick_row(ck_blk, 2 * pair + e)
                if diagonal:
                    s = jnp.where(causal, s, _NEG)
                p = jnp.exp(s - lses[e])
                ds = p * (_bdot(dos[e], vj, _D2) - deltas[e])
                dv_ref[rows, :] += _bdot(p, dos[e], _D0)
                accs[e][rows, :] += _bdot(ds, q1s[e], _D0)
                out.append(dqs[e] + _bdot(ds, jnp.where(mine[e], kj, 1.0), _D1))
            return tuple(out)

        zero = jnp.zeros((t, PAIR_W), f32)
        dqs = lax.fori_loop(0, i, lambda j, cr: block(j, cr, False), (zero, zero))
        dq0, dq1 = block(i, dqs, True)
        dq_ref[...] = jnp.where(first, dq0, dq1) * scale
        dcq_ref[...] = jnp.where(lane == 0, _pick_lane(dq0, HD), jnp.where(lane == 1, _pick_lane(dq1, 0), 0.0))

        @pl.when(i == nq - 1)
        def _():
            a0, a1 = acc0[...], acc1[...]
            dk_ref[...] = jnp.where(first, a0, a1)
            dck_ref[...] = jnp.where(lane == 0, -_pick_lane(a0, HD), jnp.where(lane == 1, -_pick_lane(a1, 0), 0.0))

    blk = lambda col: pl.BlockSpec((t, PAIR_W), lambda b, p, i: (b * nq + i, col * PAIRS + p))
    whole = lambda col: pl.BlockSpec((seq, PAIR_W), lambda b, p, i: (b, col * PAIRS + p))
    t_all = batch * seq
    return pl.pallas_call(
        body, name="fox_attn_bwd", grid=(batch, PAIRS, nq),
        in_specs=[blk(0), whole(1), whole(2),
                  pl.BlockSpec((t, 128), lambda b, p, i: (b * nq + i, 0)),
                  pl.BlockSpec((1, 8, seq), lambda b, p, i: (b, 0, 0)),
                  blk(0), blk(0), blk(0)],
        out_specs=[blk(0), whole(0), whole(0), blk(0), whole(0)],
        out_shape=[jax.ShapeDtypeStruct((t_all, HW), f32)] * 5,
        scratch_shapes=[pltpu.VMEM((seq, PAIR_W), f32), pltpu.VMEM((seq, PAIR_W), f32)],
        compiler_params=_cp(("parallel", "parallel", "arbitrary")),
    )(qkv, qkv, qkv, c, c_rows, o, lse, do)


MEM_TQ = 1024


def _mem_block(q, km, vm):
    nn, nt, _ = _make_mm(False, False)
    logits = nt(q, km) * (MEM_HD ** -0.5)
    m = lax.stop_gradient(jnp.max(logits, axis=-1, keepdims=True))
    e = jnp.exp(logits - m)
    return nn(e / jnp.sum(e, axis=-1, keepdims=True), vm)


def _mem_specs(seq, tq):
    nq = seq // tq
    qs = pl.BlockSpec((tq, MEM_HD), lambda b, h, i: (b * nq + i, h))
    ks = pl.BlockSpec((MEM_LEN, MEM_HD), lambda b, h, i: (b, h))
    vs = pl.BlockSpec((MEM_LEN, MEM_HD), lambda b, h, i: (b, MEM_HEADS + h))
    return nq, qs, ks, vs


def _mem_fwd(q, mem_kv, batch, seq):
    tq = min(MEM_TQ, seq)
    nq, qs, ks, vs = _mem_specs(seq, tq)

    def body(q_ref, k_ref, v_ref, o_ref):
        o_ref[...] = _mem_block(q_ref[...].astype(f32), k_ref[...], v_ref[...]).astype(o_ref.dtype)

    return pl.pallas_call(
        body, name="mem_attn_fwd", grid=(batch, MEM_HEADS, nq),
        in_specs=[qs, ks, vs], out_specs=qs, out_shape=jax.ShapeDtypeStruct(q.shape, bf16),
        compiler_params=_cp(("parallel", "parallel", "arbitrary")),
    )(q, mem_kv, mem_kv)


def _mem_bwd(q, mem_kv, do, batch, seq):
    tq = min(MEM_TQ, seq)
    nq, qs, ks, vs = _mem_specs(seq, tq)

    def body(q_ref, k_ref, v_ref, do_ref, dq_ref, dk_ref, dv_ref):
        _, vjp = jax.vjp(_mem_block, q_ref[...].astype(f32), k_ref[...], v_ref[...])
        dq, dk, dv = vjp(do_ref[...])
        dq_ref[...] = dq.astype(dq_ref.dtype)

        @pl.when(pl.program_id(2) == 0)
        def _():
            dk_ref[...] = jnp.zeros_like(dk_ref)
            dv_ref[...] = jnp.zeros_like(dv_ref)

        dk_ref[...] += dk
        dv_ref[...] += dv

    return pl.pallas_call(
        body, name="mem_attn_bwd", grid=(batch, MEM_HEADS, nq),
        in_specs=[qs, ks, vs, qs], out_specs=[qs, ks, ks],
        out_shape=[jax.ShapeDtypeStruct(q.shape, bf16), jax.ShapeDtypeStruct((batch * MEM_LEN, MEM_W), f32),
                   jax.ShapeDtypeStruct((batch * MEM_LEN, MEM_W), f32)],
        compiler_params=_cp(("parallel", "parallel", "arbitrary")),
    )(q, mem_kv, mem_kv, do)


@jax.custom_vjp
def _halves(x):
    c = x.shape[1] // 2
    return x[:, :c], x[:, c:]


_halves.defvjp(lambda x: ((x[:, :x.shape[1] // 2], x[:, x.shape[1] // 2:]), None),
               lambda _, g: (jnp.concatenate(g, axis=1),))


@jax.custom_vjp
def _lead_halves(x):
    n = x.shape[0] // 2
    return x[:n], x[n:]


_lead_halves.defvjp(lambda x: ((x[:x.shape[0] // 2], x[x.shape[0] // 2:]), None),
                    lambda _, g: (jnp.concatenate(g, axis=0),))


def _scan_chunk(s0, r, wl, k, v, a, b):
    nn, nt, tn = _make_mm(True, False)
    nn_exact, _, _ = _make_mm(True, True)
    _, nt_exact, _ = _make_mm(True, "split")
    hp, c, lanes = r.shape
    row = lax.broadcasted_iota(jnp.int32, (c, c), 0)
    col = lax.broadcasted_iota(jnp.int32, (c, c), 1)
    first = (lax.broadcasted_iota(jnp.int32, (1, 1, lanes), 2) // HD) == 0
    tri = jnp.broadcast_to((col <= row).astype(f32)[None], (hp, c, c))
    lg = nn_exact(tri, wl)
    lg_end = lg[:, c - 1:c, :]
    grow, shrink, to_end = jnp.exp(lg), jnp.exp(-lg), jnp.exp(lg_end - lg)
    rt, kt, bt, at = r * grow, k * shrink, b * shrink, a * jnp.exp(lg - wl)
    strict, incl = (col < row)[None], (col <= row)[None]
    twice = lambda t: jnp.concatenate([t, t], axis=0)
    queries = jnp.concatenate([at, rt], axis=1)
    per_head = jnp.concatenate([jnp.where(first, queries, 0.0), jnp.where(first, 0.0, queries)], axis=0)
    (ab, rb), (ak, rk) = _halves(nt_exact(per_head, twice(bt))), _halves(nt_exact(per_head, twice(kt)))
    l_ab = jnp.where(strict, ab, 0.0)
    a_ak = jnp.where(strict, ak, 0.0)
    a_rb = jnp.where(incl, rb, 0.0)
    a_rk = jnp.where(incl, rk, 0.0)
    inv = (col == row).astype(f32)[None] + l_ab
    power, n = l_ab, 1
    while 2 * n < c:
        power = nn(power, power)
        inv = inv + nn(inv, power)
        n *= 2

    def apply(m, t):
        lo, hi = _lead_halves(nn(m, twice(t)))
        return jnp.where(first, lo, hi)

    sa = apply(inv, nt(at, s0) + apply(a_ak, v))
    y = nt(rt, s0) + apply(a_rk, v) + apply(a_rb, sa)
    same_head = ((lax.broadcasted_iota(jnp.int32, (lanes, lanes), 0) // HD)
                 == (lax.broadcasted_iota(jnp.int32, (lanes, lanes), 1) // HD))[None]
    s1 = s0 * jnp.exp(lg_end) + jnp.where(same_head, tn(v, k * to_end) + tn(sa, b * to_end), 0.0)
    return y, s1


PAIRS = HEADS // 2
PAIR_W = 2 * HD


def _pair_stack(ref, off):
    return jnp.stack([ref[b, :, off + p * PAIR_W:off + (p + 1) * PAIR_W]
                      for b in range(ref.shape[0]) for p in range(PAIRS)])


def _pair_store(ref, off, val, add_ref=None):
    for b in range(ref.shape[0]):
        for p in range(PAIRS):
            sl = slice(off + p * PAIR_W, off + (p + 1) * PAIR_W)
            v = val[b * PAIRS + p]
            ref[b, :, sl] = v if add_ref is None else v + add_ref[b, :, sl]


def _scan_fwd(main6, batch, seq, side=None):
    c = min(SCAN_CHUNK, seq)
    nc = seq // c
    hp = batch * PAIRS
    srcs, per_peer = side if side is not None else ([], False)
    n_s = len(srcs)

    def body(*refs):
        z_ref, y_ref, s_ref, st = refs[0], refs[1 + n_s], refs[2 + n_s], refs[3 + 2 * n_s]
        _side_exchange(refs[1:1 + n_s], refs[3 + n_s:3 + 2 * n_s], per_peer, refs[4 + 2 * n_s:], nc)

        @pl.when(pl.program_id(0) == 0)
        def _():
            st[...] = jnp.zeros_like(st)

        s0 = st[...]
        s_ref[0] = s0
        y, s1 = _scan_chunk(s0, *[_pair_stack(z_ref, comp * HW) for comp in range(6)])
        _pair_store(y_ref, 0, y)
        st[...] = s1

    res = pl.pallas_call(
        body, name="rwkv_scan_fwd", grid=(nc,),
        in_specs=[pl.BlockSpec((batch, c, 6 * HW), lambda i: (0, i, 0))] + [_HBM_SPEC] * n_s,
        out_specs=[pl.BlockSpec((batch, c, HW), lambda i: (0, i, 0)),
                   pl.BlockSpec((1, hp, PAIR_W, PAIR_W), lambda i: (i, 0, 0, 0))] + [_HBM_SPEC] * n_s,
        out_shape=[jax.ShapeDtypeStruct((batch, seq, HW), f32), jax.ShapeDtypeStruct((nc, hp, PAIR_W, PAIR_W), f32)]
        + _side_out_shapes(srcs, per_peer),
        scratch_shapes=[pltpu.VMEM((hp, PAIR_W, PAIR_W), f32)] + _side_sems(n_s),
        compiler_params=_cp(("arbitrary",)),
    )(main6.reshape(batch, seq, 6 * HW), *srcs)
    return res[0].reshape(batch * seq, HW), res[1], list(res[2:])


def _scan_bwd(main6, states, dy, extra, batch, seq, side=None):
    c = min(SCAN_CHUNK, seq)
    nc = seq // c
    hp = batch * PAIRS
    srcs, per_peer = side if side is not None else ([], False)
    n_s = len(srcs)

    def body(*refs):
        z_ref, s_ref, dy_ref, ex_ref = refs[:4]
        dz_ref, dst = refs[4 + n_s], refs[5 + 2 * n_s]
        _side_exchange(refs[4:4 + n_s], refs[5 + n_s:5 + 2 * n_s], per_peer, refs[6 + 2 * n_s:], nc)

        @pl.when(pl.program_id(0) == 0)
        def _():
            dst[...] = jnp.zeros_like(dst)

        _, vjp = jax.vjp(_scan_chunk, s_ref[0], *[_pair_stack(z_ref, comp * HW) for comp in range(6)])
        g = vjp((_pair_stack(dy_ref, 0), dst[...]))
        dst[...] = g[0]
        for comp in range(6):
            _pair_store(dz_ref, comp * HW, g[1 + comp], ex_ref)

    back = lambda i: (0, nc - 1 - i, 0)
    wide = pl.BlockSpec((batch, c, 6 * HW), back)
    res = pl.pallas_call(
        body, name="rwkv_scan_bwd", grid=(nc,),
        in_specs=[wide, pl.BlockSpec((1, hp, PAIR_W, PAIR_W), lambda i: (nc - 1 - i, 0, 0, 0)),
                  pl.BlockSpec((batch, c, HW), back), wide] + [_HBM_SPEC] * n_s,
        out_specs=[wide] + [_HBM_SPEC] * n_s,
        out_shape=[jax.ShapeDtypeStruct((batch, seq, 6 * HW), f32)] + _side_out_shapes(srcs, per_peer),
        scratch_shapes=[pltpu.VMEM((hp, PAIR_W, PAIR_W), f32)] + _side_sems(n_s),
        compiler_params=_cp(("arbitrary",)),
    )(main6.reshape(batch, seq, 6 * HW), states, dy.reshape(batch, seq, HW), extra.reshape(batch, seq, 6 * HW), *srcs)
    return res[0].reshape(batch * seq, 6 * HW), list(res[1:])


def _pad_cols(x, width):
    return jnp.pad(x, ((0, 0), (0, width - x.shape[1])))


def _split_w_in(wt):
    z = lambda rows: jnp.zeros((rows, wt.shape[1]), wt.dtype)
    w_r = jnp.concatenate([wt[1544:3080], wt[3080:3144], z(64), wt[3144:3208], z(64), wt[3208:3336]], axis=0)
    return wt[:1536], jnp.concatenate([wt[1536:1544], z(120)], axis=0), w_r, wt[3336:3848], wt[3848:]


def _merge_w_in(g_qkv, g_f, g_r, g_mq, g_g):
    return jnp.concatenate([g_qkv, g_f[:8], g_r[:1536], g_r[1536:1600], g_r[1664:1728], g_r[1792:], g_mq, g_g], axis=0)


def _pad_lora(v):
    z64 = jnp.zeros((1, 64), v.dtype)
    return jnp.concatenate([v[:, :1536], v[:, 1536:1600], z64, v[:, 1600:1664], z64, v[:, 1664:]], axis=1)


def _unpad_lora(v):
    return jnp.concatenate([v[:, :1536], v[:, 1536:1600], v[:, 1664:1728], v[:, 1792:]], axis=1)


def _local_step(x, mem, target, w, p, late=None, early=None, last=None):
    batch, seq, _ = x.shape
    t = batch * seq
    x2, tg2, mem2 = x.reshape(t, D), target.reshape(t, D), mem.reshape(batch * MEM_LEN, D)
    w_qkv, w_f, w_r, w_mq, w_g3 = _split_w_in(w["w_in"])
    mu = _pad_lora(p["rwkv_mu"])
    bias = _pad_cols(p["fox_f_bias"], 128)
    r_k = p["rwkv_r_k"].reshape(1, HW)
    post_params = [p["rwkv_gn_g"], p["rwkv_gn_b"], r_k]
    rw_widths = [HW, HW, HW, LORA_PAD, LORA_PAD, LORA_PAD]
    six = [HW] * 6

    (u,) = _rows_fwd("rms_pre1", _fn_rms, [], [(x2, [D])], [p["pre1_g"]], [[D]], dtypes=[bf16])
    p_qkv = _matmul("proj_qkv", u, w_qkv, "nt", out_dtype=bf16)
    p_f = _matmul("proj_f", u, w_f, "nt")
    p_r = _matmul("proj_rwkv", u, w_r, "nt")
    p_mq = _matmul("proj_memq", u, w_mq, "nt", out_dtype=bf16)
    p_g = _matmul("proj_gate", u, w_g3, "nt", out_dtype=bf16)

    c = _fox_gate_fwd(p_f, bias, batch, seq)
    c_rows = c[:, :HEADS].reshape(batch, seq, HEADS).transpose(0, 2, 1)
    fox_o, lse, gathered = _fox_fwd(p_qkv, c, c_rows, batch, seq, side=(late[0], False) if late else None)
    if late:
        w = {**w, **late[2](gathered, 0)}
    fox_out = fox_o.astype(bf16)

    w_up = jnp.pad(w["rwkv_w_up"].astype(f32), ((0, LORA_PAD - 64), (0, 0)))
    a_up = jnp.pad(w["rwkv_a_up"].astype(f32), ((0, LORA_PAD - 64), (0, 0)))
    pre_params = [p["rwkv_w0"], w_up, p["rwkv_a0"], a_up, w["rwkv_g_up"].astype(f32), p["rwkv_k_k"], p["rwkv_k_a"]]
    ps = _tokshift_fwd(p_r, mu, batch, seq)
    main6, g_rw = _rows_fwd("rwkv_pre", _fn_rwkv_pre, [], [(ps, rw_widths)], pre_params, [six, [HW]], tm=256)
    y_rw, states, gathered = _scan_fwd(main6, batch, seq, side=(late[1], False) if late else None)
    if late:
        w = {**w, **late[2](gathered, 1)}
    post_consts = []
    post_rows = [(y_rw, [HW]), (main6, six), (g_rw, [HW])]

    def fn_post(y, r, _wl, k2, v, _a, _b, g, gn_g, gn_b, rk):
        return _fn_rwkv_post(y, r, k2, v, g, gn_g, gn_b, rk)

    (rwkv_out,) = _rows_fwd("rwkv_post", fn_post, post_consts, post_rows, post_params, [[HW]], dtypes=[bf16], tm=256)

    (memn,) = _rows_fwd("rms_mem", _fn_rms, [], [(mem2, [D])], [p["mem_norm_g"]], [[D]], dtypes=[bf16])
    mem_kv = _matmul("proj_memkv", memn, w["w_mem_kv"], "nn")
    mem_out = _mem_fwd(p_mq, mem_kv, batch, seq)

    a_fox = _matmul("out_fox", fox_out, w["w_fox_out"], "nn", out_dtype=bf16)
    a_rwkv = _matmul("out_rwkv", rwkv_out, w["w_rwkv_out"], "nn", out_dtype=bf16)
    a_mem = _matmul("out_mem", mem_out, w["w_mem_out"], "nn", out_dtype=bf16)
    merge_rows = [(a_fox, [D]), (a_rwkv, [D]), (a_mem, [D]), (p_g, [D, D, D])]
    (merged,) = _rows_fwd("merge", _fn_merge, [], merge_rows, [], [[D]], dtypes=[bf16])
    yy = _matmul("out_o", merged, w["w_o"], "nn")
    post1_rows = [(yy, [D]), (x2, [D])]
    post1_params = [p["post1_g"], p["pre2_g"]]
    h1, u2 = _rows_fwd("post1", _fn_post1, [], post1_rows, post1_params, [[D], [D]], dtypes=[f32, bf16])
    gp = _matmul("ffn_gate", u2, w["w_ffn_gate"], "nt", out_dtype=bf16)
    up = _matmul("ffn_up", u2, w["w_ffn_up"], "nt", out_dtype=bf16)
    (hmid,) = _rows_fwd("swiglu", _fn_swiglu, [], [(gp, [D_FF]), (up, [D_FF])], [], [[D_FF]], dtypes=[bf16])
    ffn = _matmul("ffn_down", hmid, w["w_ffn_down"], "nn")
    final_rows = [(ffn, [D]), (h1, [D])]

    gw, gp_ = {}, {}
    (d_ffn, d_h1), (gp_["post2_g"], loss) = _rows_bwd("final", _fn_final, [(tg2, [D])], final_rows, [p["post2_g"]], [], [],
                                                      n_sums=1, dtypes=[bf16, f32])
    d_hmid = _matmul("ffn_down_dx", d_ffn, w["w_ffn_down"], "nt", out_dtype=bf16)
    gw["w_ffn_down"] = _matmul("ffn_down_dw", hmid, d_ffn, "tn", out_dtype=bf16)
    (d_gp, d_up), _ = _rows_bwd("swiglu_bwd", _fn_swiglu, [], [(gp, [D_FF]), (up, [D_FF])], [], [[D_FF]], [d_hmid],
                                dtypes=[bf16, bf16])
    d_u2 = _matmul("ffn_gate_dx", d_gp, w["w_ffn_gate"], "nn")
    d_u2 = _matmul("ffn_up_dx", d_up, w["w_ffn_up"], "nn", add=d_u2)
    gw["w_ffn_gate"] = _matmul("ffn_gate_dw", d_gp, u2, "tn", out_dtype=bf16)
    gw["w_ffn_up"] = _matmul("ffn_up_dw", d_up, u2, "tn", out_dtype=bf16)
    (d_yy, d_x_res), (gp_["post1_g"], gp_["pre2_g"]) = _rows_bwd(
        "post1_bwd", _fn_post1, [], post1_rows, post1_params, [[D], [D]], [d_h1, d_u2], dtypes=[bf16, f32])
    d_merged = _matmul("out_o_dx", d_yy, w["w_o"], "nt", out_dtype=bf16)
    gw["w_o"] = _matmul("out_o_dw", merged, d_yy, "tn", out_dtype=bf16)
    (d_a_fox, d_a_rwkv, d_a_mem, d_p_g), _ = _rows_bwd("merge_bwd", _fn_merge, [], merge_rows, [], [[D]], [d_merged],
                                                       dtypes=[bf16] * 4)
    d_fox_out = _matmul("out_fox_dx", d_a_fox, w["w_fox_out"], "nt")
    gw["w_fox_out"] = _matmul("out_fox_dw", fox_out, d_a_fox, "tn", out_dtype=bf16)
    d_rwkv_out = _matmul("out_rwkv_dx", d_a_rwkv, w["w_rwkv_out"], "nt")
    gw["w_rwkv_out"] = _matmul("out_rwkv_dw", rwkv_out, d_a_rwkv, "tn", out_dtype=bf16)
    d_mem_out = _matmul("out_mem_dx", d_a_mem, w["w_mem_out"], "nt")
    gw["w_mem_out"] = _matmul("out_mem_dw", mem_out, d_a_mem, "tn", out_dtype=bf16)

    d_p_mq, d_km, d_vm = _mem_bwd(p_mq, mem_kv, d_mem_out, batch, seq)
    d_mem_kv = jnp.concatenate([d_km, d_vm], axis=1).astype(bf16)
    gw["w_mem_kv"] = _matmul("proj_memkv_dw", memn, d_mem_kv, "tn", out_dtype=bf16)
    d_memn = _matmul("proj_memkv_dx", d_mem_kv, w["w_mem_kv"], "nt")
    _, (gp_["mem_norm_g"],) = _rows_bwd("rms_mem_bwd", _fn_rms, [], [(mem2, [D])], [p["mem_norm_g"]], [[D]], [d_memn])

    d_q, d_k, d_v, d_cq, d_ck = _fox_bwd(p_qkv, c, c_rows, fox_o, lse, d_fox_out, batch, seq)
    d_p_qkv = jnp.concatenate([d_q, d_k, d_v], axis=1).astype(bf16)
    d_p_f, d_bias = _fox_gate_bwd(p_f, bias, d_cq, d_ck, batch, seq)
    gp_["fox_f_bias"] = d_bias[:, :HEADS]

    (d_y_rw, d_main6_post, d_g_rw), (gp_["rwkv_gn_g"], gp_["rwkv_gn_b"], d_rk) = _rows_bwd(
        "rwkv_post_bwd", fn_post, post_consts, post_rows, post_params, [[HW]], [d_rwkv_out], tm=256)
    gp_["rwkv_r_k"] = d_rk.reshape(1, HEADS, HD)
    d_main6, early_got = _scan_bwd(main6, states, d_y_rw, d_main6_post, batch, seq,
                                   side=(early(gw), True) if early else None)

    def fn_pre_sum(*args):
        return _fn_rwkv_pre(*args)

    (d_ps,), d_pre = _rows_bwd("rwkv_pre_bwd", fn_pre_sum, [], [(ps, rw_widths)], pre_params, [six, [HW]],
                               [d_main6, d_g_rw], tm=256)
    gp_["rwkv_w0"], d_w_up, gp_["rwkv_a0"], d_a_up, gw["rwkv_g_up"], gp_["rwkv_k_k"], gp_["rwkv_k_a"] = d_pre
    gw["rwkv_w_up"], gw["rwkv_a_up"] = d_w_up[:64], d_a_up[:64]
    d_p_r, d_mu = _tokshift_bwd(p_r, mu, d_ps, batch, seq)
    gp_["rwkv_mu"] = _unpad_lora(d_mu)

    gw["w_in"] = _merge_w_in(_matmul("proj_qkv_dw", d_p_qkv, u, "tn", out_dtype=bf16), _matmul("proj_f_dw", d_p_f, u, "tn", out_dtype=bf16),
                             _matmul("proj_rwkv_dw", d_p_r, u, "tn", out_dtype=bf16), _matmul("proj_memq_dw", d_p_mq, u, "tn", out_dtype=bf16),
                             _matmul("proj_gate_dw", d_p_g, u, "tn", out_dtype=bf16))
    d_x, gp_["pre1_g"], last_got = _input_cotangent(
        "proj_dx", [d_p_qkv, d_p_f, d_p_r, d_p_mq, d_p_g], [w_qkv, w_f, w_r, w_mq, w_g3], x2, p["pre1_g"], d_x_res,
        side=(last(gw), True) if last else None)
    return loss, d_x.reshape(x.shape), gw, gp_, early_got, last_got


def _adamw(name, recv, row_off, w, m, v):
    _, rows, cols = w.shape
    row_tiles = [t for t in range(16, min(rows, 128) + 1, 16) if rows % t == 0 and row_off % t == 0]
    if row_tiles:
        tr, tc = max(row_tiles), cols
        first, grid = row_off // tr, (rows // tr,)
        at = lambda i: (0, first + i, 0)
        mine = lambda i: (0, i, 0)
    else:
        assert row_off == 0 and recv.shape[1] == rows
        tr, tc = rows, 128
        grid = (cols // tc,)
        at = mine = lambda i: (0, 0, i)

    def body(g_ref, w_ref, m_ref, v_ref, go_ref, d_ref, mo_ref, vo_ref):
        g = g_ref[0].astype(f32)
        for s in range(1, N_DEV):
            g = g + g_ref[s].astype(f32)
        m_new = ADAM_B1 * m_ref[0] + (1.0 - ADAM_B1) * g
        v_new = ADAM_B2 * v_ref[0] + (1.0 - ADAM_B2) * (g * g)
        m_hat = m_new / (1.0 - ADAM_B1 ** ADAM_STEP)
        v_hat = v_new / (1.0 - ADAM_B2 ** ADAM_STEP)
        go_ref[0] = g
        d_ref[0] = -ADAM_LR * (m_hat / (jnp.sqrt(v_hat) + ADAM_EPS) + ADAM_WD * w_ref[0])
        mo_ref[0] = m_new
        vo_ref[0] = v_new

    spec = pl.BlockSpec((1, tr, tc), mine)
    return pl.pallas_call(
        body, name=name, grid=grid,
        in_specs=[pl.BlockSpec((N_DEV, tr, tc), at), spec, spec, spec],
        out_specs=[spec] * 4, out_shape=[jax.ShapeDtypeStruct(w.shape, f32)] * 4,
        compiler_params=_cp(("parallel",)),
    )(recv, w, m, v)


GROUPS = (
    ("in", ("w_in",), 0),
    ("memkv", ("w_mem_kv",), 0),
    ("ffn_gu", ("w_ffn_gate", "w_ffn_up"), 0),
    ("down_o", ("w_ffn_down", "w_o"), 0),
    ("outs", ("w_fox_out", "w_rwkv_out", "w_mem_out"), 0),
    ("lora", ("rwkv_w_up", "rwkv_a_up", "rwkv_g_up"), 0),
)
FIRST_GROUPS = ("in", "memkv")
LATE_GROUPS = (("down_o", "outs", "lora"), ("ffn_gu",))
EARLY_GRAD_GROUPS = ("memkv", "ffn_gu", "down_o", "outs")
LAST_GRAD_GROUPS = ("in", "lora")
SHARD_AXIS = {n: a for n, _, a in SHARDED}
SMALL_ROWS = 16


def _group_local(shards, members, join):
    parts = [shards[n].reshape(shards[n].shape[-2:]) for n in members]
    return parts[0] if len(parts) == 1 else jnp.concatenate(parts, axis=join)


def _group_split(arr, members, join, lead=False):
    out, off = {}, 0
    for n in members:
        shape = dict((k, s) for k, s, _ in SHARDED)[n]
        size = _block_shape(shape, SHARD_AXIS[n])[join]
        idx = [slice(None)] * arr.ndim
        idx[arr.ndim - 2 + join] = slice(off, off + size)
        out[n] = arr[tuple(idx)]
        off += size
    return out


def _full_from_blocks(blocks, axis):
    if axis == 0:
        return blocks.reshape(-1, blocks.shape[2])
    return blocks.transpose(1, 0, 2).reshape(blocks.shape[1], -1)


def _blocks_from_full(full, axis):
    if axis == 0:
        return full.reshape(N_DEV, -1, full.shape[1])
    return full.reshape(full.shape[0], N_DEV, -1).transpose(1, 0, 2)


def _assemble(gathered, names):
    out = {}
    for arr, g in zip(gathered, names):
        _, members, join = [grp for grp in GROUPS if grp[0] == g][0]
        for n, blk in _group_split(arr, members, join, lead=True).items():
            out[n] = _full_from_blocks(blk, SHARD_AXIS[n])
    return out


def _grad_blocks(gw, names):
    out = []
    for g in names:
        _, members, join = [grp for grp in GROUPS if grp[0] == g][0]
        parts = [_blocks_from_full(gw[n].astype(bf16), SHARD_AXIS[n]) for n in members]
        out.append(parts[0] if len(parts) == 1 else jnp.concatenate(parts, axis=1 + join))
    return out


def _small_pack(d):
    flat = jnp.concatenate([d[n].reshape(-1) for n, _ in REPLICATED])
    return jnp.pad(flat, (0, SMALL_ROWS * LANES - REPL_ELEMS)).reshape(SMALL_ROWS, LANES)


def _small_unpack(packed):
    out, flat, off = {}, packed.reshape(-1), 0
    for n, shape in REPLICATED:
        k = _rows_of((LANES,) + shape)
        out[n] = flat[off:off + k].reshape(shape)
        off += k
    return out


def kernel(x, mem, pre1_g, post1_g, pre2_g, post2_g, mem_norm_g, w_in, fox_f_bias, rwkv_mu, rwkv_w0, rwkv_w_up, rwkv_a0, rwkv_a_up, rwkv_g_up, rwkv_k_k, rwkv_k_a, rwkv_r_k, rwkv_gn_g, rwkv_gn_b, w_mem_kv, w_fox_out, w_rwkv_out, w_mem_out, w_o, w_ffn_gate, w_ffn_up, w_ffn_down, loss_target, m_pre1_g, m_post1_g, m_pre2_g, m_post2_g, m_mem_norm_g, m_w_in, m_fox_f_bias, m_rwkv_mu, m_rwkv_w0, m_rwkv_w_up, m_rwkv_a0, m_rwkv_a_up, m_rwkv_g_up, m_rwkv_k_k, m_rwkv_k_a, m_rwkv_r_k, m_rwkv_gn_g, m_rwkv_gn_b, m_w_mem_kv, m_w_fox_out, m_w_rwkv_out, m_w_mem_out, m_w_o, m_w_ffn_gate, m_w_ffn_up, m_w_ffn_down, v_pre1_g, v_post1_g, v_pre2_g, v_post2_g, v_mem_norm_g, v_w_in, v_fox_f_bias, v_rwkv_mu, v_rwkv_w0, v_rwkv_w_up, v_rwkv_a0, v_rwkv_a_up, v_rwkv_g_up, v_rwkv_k_k, v_rwkv_k_a, v_rwkv_r_k, v_rwkv_gn_g, v_rwkv_gn_b, v_w_mem_kv, v_w_fox_out, v_w_rwkv_out, v_w_mem_out, v_w_o, v_w_ffn_gate, v_w_ffn_up, v_w_ffn_down):
    args = dict(locals())
    turn = lambda n, a: jnp.swapaxes(a, 1, 2) if n in TRANSPOSED else a
    wts = {n: turn(n, args[n]) for n in WEIGHT_ORDER}
    ms = {n: turn(n, args["m_" + n]) for n in WEIGHT_ORDER}
    vs = {n: turn(n, args["v_" + n]) for n in WEIGHT_ORDER}

    groups = {g: (members, join) for g, members, join in GROUPS}
    w_bf16 = {n: wts[n].astype(bf16) for n, _, _ in SHARDED}

    def send(g):
        return _group_local(w_bf16, *groups[g])

    first = _exchange("gather_first", [send(g) for g in FIRST_GROUPS], per_peer=False)
    full = _assemble(first, FIRST_GROUPS)
    small_in = {n: (wts[n] if n == "rwkv_r_k" else wts[n].reshape(wts[n].shape[-2:])) for n, _ in REPLICATED}
    late = ([send(g) for g in LATE_GROUPS[0]], [send(g) for g in LATE_GROUPS[1]],
            lambda got, which: _assemble(got, LATE_GROUPS[which]))
    loss_part, grad_x, gw, gp, early_got, last_got = _local_step(
        x, mem, loss_target, full, small_in, late=late, early=lambda g: _grad_blocks(g, EARLY_GRAD_GROUPS),
        last=lambda g: _grad_blocks(g, LAST_GRAD_GROUPS))
    (small_got,) = _exchange("exchange_small", [_small_pack(gp).astype(bf16)], per_peer=False)
    received = dict(zip(EARLY_GRAD_GROUPS + LAST_GRAD_GROUPS, list(early_got) + list(last_got)))

    outs = [{}, {}, {}, {}]
    for g, members, _ in GROUPS:
        off = 0
        for n in members:
            for o, arr in zip(outs, _adamw("adamw_" + n, received[g], off, wts[n], ms[n], vs[n])):
                o[n] = arr
            off += wts[n].shape[1]
    res = _adamw("adamw_small", small_got, 0, *[_small_pack(d)[None] for d in (wts, ms, vs)])
    for o, arr in zip(outs, res):
        o.update(_small_unpack(arr))
    loss = lax.psum(loss_part[0, 0], ("x", "y", "c"))
    return (loss, grad_x, *[turn(n, o[n].reshape(wts[n].shape)) for o in outs for n in WEIGHT_ORDER])
```

```python
import functools

import jax
import jax.numpy as jnp
from jax import lax
from jax.experimental import pallas as pl
from jax.experimental.pallas import tpu as pltpu

f32 = jnp.float32
bf16 = jnp.bfloat16
_HI = lax.Precision.HIGHEST

D = 1024
HEADS = 8
HD = 64
HW = HEADS * HD
MEM_HEADS = 4
MEM_HD = 128
MEM_W = 512
MEM_LEN = 256
D_FF = 2816
LORA_PAD = 128
NORM_EPS = 1e-6
GN_EPS = 64e-5
SCAN_CHUNK = 64
N_DEV = 8
LANES = 1024
VMEM_LIMIT = 56 * 1024 * 1024

ADAM_LR = 0.001
ADAM_B1 = 0.9
ADAM_B2 = 0.999
ADAM_EPS = 1e-08
ADAM_WD = 0.01
ADAM_STEP = 10

TRANSPOSED = ("w_in", "w_ffn_gate", "w_ffn_up")
SHARDED = (
    ("w_in", (6920, 1024), 0),
    ("w_ffn_gate", (2816, 1024), 0),
    ("w_ffn_up", (2816, 1024), 0),
    ("w_ffn_down", (2816, 1024), 0),
    ("w_mem_kv", (1024, 1024), 0),
    ("w_o", (1024, 1024), 0),
    ("w_fox_out", (512, 1024), 1),
    ("w_rwkv_out", (512, 1024), 1),
    ("w_mem_out", (512, 1024), 1),
    ("rwkv_w_up", (64, 512), 1),
    ("rwkv_a_up", (64, 512), 1),
    ("rwkv_g_up", (128, 512), 1),
)
REPLICATED = (
    ("pre1_g", (1, 1024)), ("post1_g", (1, 1024)), ("pre2_g", (1, 1024)), ("post2_g", (1, 1024)),
    ("mem_norm_g", (1, 1024)), ("fox_f_bias", (1, 8)), ("rwkv_mu", (1, 1792)), ("rwkv_w0", (1, 512)),
    ("rwkv_a0", (1, 512)), ("rwkv_k_k", (1, 512)), ("rwkv_k_a", (1, 512)), ("rwkv_r_k", (1, 8, 64)),
    ("rwkv_gn_g", (1, 512)), ("rwkv_gn_b", (1, 512)),
)
WEIGHT_ORDER = ('pre1_g', 'post1_g', 'pre2_g', 'post2_g', 'mem_norm_g', 'w_in', 'fox_f_bias', 'rwkv_mu',
                'rwkv_w0', 'rwkv_w_up', 'rwkv_a0', 'rwkv_a_up', 'rwkv_g_up', 'rwkv_k_k', 'rwkv_k_a',
                'rwkv_r_k', 'rwkv_gn_g', 'rwkv_gn_b', 'w_mem_kv', 'w_fox_out', 'w_rwkv_out', 'w_mem_out',
                'w_o', 'w_ffn_gate', 'w_ffn_up', 'w_ffn_down')


def _block_shape(shape, axis):
    return tuple(s // N_DEV if i == axis else s for i, s in enumerate(shape))


def _rows_of(shape):
    n = 1
    for s in shape:
        n *= s
    return n // LANES


REPL_ELEMS = sum(_rows_of((LANES,) + s) for _, s in REPLICATED)


def _cp(sem=None):
    return pltpu.CompilerParams(dimension_semantics=sem, vmem_limit_bytes=VMEM_LIMIT)


def _tile(dim, cap):
    best = None
    for t in range(128, min(dim, cap) + 1, 128):
        if dim % t == 0:
            best = t
    return best if best is not None else dim


def _two_terms(x):
    hi = x.astype(bf16)
    return hi, (x - hi.astype(f32)).astype(bf16)


def _dg(a, b, dims, exact):
    if exact == "split":
        (a_hi, a_lo), (b_hi, b_lo) = _two_terms(a), _two_terms(b)
        dot = functools.partial(lax.dot_general, dimension_numbers=dims, preferred_element_type=f32)
        return dot(a_hi, b_hi) + (dot(a_hi, b_lo) + dot(a_lo, b_hi))
    if exact:
        return lax.dot_general(a, b, dims, precision=_HI, preferred_element_type=f32)
    return lax.dot_general(a.astype(bf16), b.astype(bf16), dims, preferred_element_type=f32)


def _make_mm(batched, exact):
    o = 1 if batched else 0
    bd = ((0,), (0,)) if batched else ((), ())
    d_nn = (((1 + o,), (o,)), bd)
    d_nt = (((1 + o,), (1 + o,)), bd)
    d_tn = (((o,), (o,)), bd)

    @jax.custom_vjp
    def nn(a, b):
        return _dg(a, b, d_nn, exact)

    @jax.custom_vjp
    def nt(a, b):
        return _dg(a, b, d_nt, exact)

    @jax.custom_vjp
    def tn(a, b):
        return _dg(a, b, d_tn, exact)

    nn.defvjp(lambda a, b: (_dg(a, b, d_nn, exact), (a, b)),
              lambda res, g: (_dg(g, res[1], d_nt, exact), _dg(res[0], g, d_tn, exact)))
    nt.defvjp(lambda a, b: (_dg(a, b, d_nt, exact), (a, b)),
              lambda res, g: (_dg(g, res[1], d_nn, exact), _dg(g, res[0], d_tn, exact)))
    tn.defvjp(lambda a, b: (_dg(a, b, d_tn, exact), (a, b)),
              lambda res, g: (_dg(res[1], g, d_nt, exact), _dg(res[0], g, d_nn, exact)))
    return nn, nt, tn


def _sigmoid(x):
    return 1.0 / (1.0 + jnp.exp(-x))


def _head_sum_raw(x):
    width = 2 * HD
    i = lax.broadcasted_iota(jnp.int32, (width, width), 0) // HD
    j = lax.broadcasted_iota(jnp.int32, (width, width), 1) // HD
    m = (i == j).astype(bf16)
    dims = (((1,), (0,)), ((), ()))
    out = []
    for p in range(x.shape[1] // width):
        xp = x[:, p * width:(p + 1) * width]
        hi = xp.astype(bf16)
        lo = (xp - hi.astype(f32)).astype(bf16)
        out.append(lax.dot_general(hi, m, dims, preferred_element_type=f32)
                   + lax.dot_general(lo, m, dims, preferred_element_type=f32))
    return jnp.concatenate(out, axis=1)


@jax.custom_vjp
def _head_sum(x):
    return _head_sum_raw(x)


_head_sum.defvjp(lambda x: (_head_sum_raw(x), None), lambda _, g: (_head_sum_raw(g),))


WEIGHT_TILE_BYTES = 13 * 512 * 1024
ACC_TILE_BYTES = 8 * 1024 * 1024


def _matmul(name, a, b, mode, add=None, out_dtype=f32):
    has_add = add is not None
    if mode == "tn":
        (k, m), (_, n) = a.shape, b.shape
        tn = _tile(n, max(128, ACC_TILE_BYTES // (4 * m)))
        tk = _tile(k, 1024)

        nk = k // tk

        def body(a_ref, b_ref, o_ref, acc):
            @pl.when(pl.program_id(1) == 0)
            def _():
                acc[...] = jnp.zeros_like(acc)

            acc[...] += lax.dot_general(a_ref[...].astype(bf16), b_ref[...].astype(bf16),
                                        (((0,), (0,)), ((), ())), preferred_element_type=f32)

            @pl.when(pl.program_id(1) == nk - 1)
            def _():
                o_ref[...] = acc[...].astype(o_ref.dtype)

        return pl.pallas_call(
            body, name=name, grid=(n // tn, nk),
            in_specs=[pl.BlockSpec((tk, m), lambda j, kk: (kk, 0)), pl.BlockSpec((tk, tn), lambda j, kk: (kk, j))],
            out_specs=pl.BlockSpec((m, tn), lambda j, kk: (0, j)), out_shape=jax.ShapeDtypeStruct((m, n), out_dtype),
            scratch_shapes=[pltpu.VMEM((m, tn), f32)],
            compiler_params=_cp(("parallel", "arbitrary")),
        )(a, b)

    (m, k) = a.shape
    n = b.shape[1] if mode == "nn" else b.shape[0]
    tm = _tile(m, 1024)
    tn = _tile(n, max(128, WEIGHT_TILE_BYTES // (2 * k)))
    dims = (((1,), (0,)), ((), ())) if mode == "nn" else (((1,), (1,)), ((), ()))
    b_spec = pl.BlockSpec((k, tn), lambda j, i: (0, j)) if mode == "nn" else pl.BlockSpec((tn, k), lambda j, i: (j, 0))
    o_spec = pl.BlockSpec((tm, tn), lambda j, i: (i, j))

    def body(*refs):
        a_ref, b_ref = refs[0], refs[1]
        o_ref = refs[-1]
        r = lax.dot_general(a_ref[...].astype(bf16), b_ref[...].astype(bf16), dims, preferred_element_type=f32)
        if has_add:
            r = r + refs[2][...]
        o_ref[...] = r.astype(o_ref.dtype)

    return pl.pallas_call(
        body, name=name, grid=(n // tn, m // tm),
        in_specs=[pl.BlockSpec((tm, k), lambda j, i: (i, 0)), b_spec] + ([o_spec] if has_add else []),
        out_specs=o_spec, out_shape=jax.ShapeDtypeStruct((m, n), out_dtype),
        compiler_params=_cp(("parallel", "arbitrary")),
    )(*((a, b, add) if has_add else (a, b)))


def _input_cotangent(name, a_list, b_list, x, gain, add, side=None):
    m = a_list[0].shape[0]
    tm = _tile(m, 256)
    n_g = len(a_list)
    srcs, per_peer = side if side is not None else ([], False)
    n_s = len(srcs)

    def body(*refs):
        x_ref, g_ref, add_ref = refs[2 * n_g:2 * n_g + 3]
        src_refs = refs[2 * n_g + 3:2 * n_g + 3 + n_s]
        dx_ref, dg_ref = refs[2 * n_g + 3 + n_s:2 * n_g + 5 + n_s]
        _side_exchange(src_refs, refs[2 * n_g + 5 + n_s:2 * n_g + 5 + 2 * n_s], per_peer, refs[2 * n_g + 5 + 2 * n_s:], m // tm)
        d_u = None
        for g in range(n_g):
            r = lax.dot_general(refs[g][...].astype(bf16), refs[n_g + g][...].astype(bf16), (((1,), (0,)), ((), ())),
                                preferred_element_type=f32)
            d_u = r if d_u is None else d_u + r
        _, vjp = jax.vjp(_rms, x_ref[...], g_ref[...])
        d_x, d_gain = vjp(d_u)
        dx_ref[...] = d_x + add_ref[...]

        @pl.when(pl.program_id(0) == 0)
        def _():
            dg_ref[...] = jnp.zeros_like(dg_ref)

        dg_ref[...] += d_gain

    rows = pl.BlockSpec((tm, x.shape[1]), lambda i: (i, 0))
    whole = lambda b: pl.BlockSpec(b.shape, lambda i: (0, 0))
    res = pl.pallas_call(
        body, name=name, grid=(m // tm,),
        in_specs=[pl.BlockSpec((tm, a.shape[1]), lambda i: (i, 0)) for a in a_list] + [whole(b) for b in b_list]
        + [rows, whole(gain), rows] + [_HBM_SPEC] * n_s,
        out_specs=[rows, whole(gain)] + [_HBM_SPEC] * n_s,
        out_shape=[jax.ShapeDtypeStruct(x.shape, f32), jax.ShapeDtypeStruct(gain.shape, f32)] + _side_out_shapes(srcs, per_peer),
        scratch_shapes=_side_sems(n_s),
        compiler_params=_cp(("arbitrary",)),
    )(*a_list, *b_list, x, gain, add, *srcs)
    return res[0], res[1], list(res[2:])


def _pieces(ref, widths):
    out, off = [], 0
    for w in widths:
        out.append(ref[:, off:off + w].astype(f32))
        off += w
    return out


def _store_pieces(ref, widths, vals, add_ref=None):
    off = 0
    for w, v in zip(widths, vals):
        ref[:, off:off + w] = (v if add_ref is None else v + add_ref[:, off:off + w]).astype(ref.dtype)
        off += w


def _rows_fwd(name, fn, consts, rows, params, outs, n_sums=0, tm=512, dtypes=None):
    t = (consts + rows)[0][0].shape[0]
    tm = min(tm, t)
    ins = consts + rows
    n_in, n_p, n_o = len(ins), len(params), len(outs)
    dtypes = dtypes or [f32] * n_o

    def body(*refs):
        in_refs, p_refs = refs[:n_in], refs[n_in:n_in + n_p]
        o_refs, s_refs = refs[n_in + n_p:n_in + n_p + n_o], refs[n_in + n_p + n_o:]
        vals = []
        for r, (_, widths) in zip(in_refs, ins):
            vals += _pieces(r, widths)
        res = fn(*vals, *[p[...] for p in p_refs])
        pos = 0
        for r, widths in zip(o_refs, outs):
            _store_pieces(r, widths, res[pos:pos + len(widths)])
            pos += len(widths)

        @pl.when(pl.program_id(0) == 0)
        def _():
            for s in s_refs:
                s[...] = jnp.zeros_like(s)

        for s, v in zip(s_refs, res[pos:]):
            s[...] += v

    row_spec = lambda w: pl.BlockSpec((tm, w), lambda i: (i, 0))
    full = lambda p: pl.BlockSpec(p.shape, lambda i: (0,) * p.ndim)
    return pl.pallas_call(
        body, name=name, grid=(t // tm,),
        in_specs=[row_spec(a.shape[1]) for a, _ in ins] + [full(p) for p in params],
        out_specs=[row_spec(sum(w)) for w in outs] + [pl.BlockSpec((1, 1), lambda i: (0, 0))] * n_sums,
        out_shape=[jax.ShapeDtypeStruct((t, sum(w)), dt) for w, dt in zip(outs, dtypes)] + [jax.ShapeDtypeStruct((1, 1), f32)] * n_sums,
        compiler_params=_cp(("arbitrary",)),
    )(*[a for a, _ in ins], *params)


def _rows_bwd(name, fn, consts, rows, params, outs, cts, n_sums=0, add=None, tm=512, dtypes=None):
    t = (consts + rows)[0][0].shape[0]
    tm = min(tm, t)
    n_c, n_r, n_p, n_o = len(consts), len(rows), len(params), len(outs)
    has_add = add is not None
    dtypes = dtypes or [f32] * n_r

    def body(*refs):
        pos = 0
        c_refs = refs[pos:pos + n_c]; pos += n_c
        r_refs = refs[pos:pos + n_r]; pos += n_r
        p_refs = refs[pos:pos + n_p]; pos += n_p
        ct_refs = refs[pos:pos + n_o]; pos += n_o
        add_ref = refs[pos] if has_add else None
        pos += 1 if has_add else 0
        dr_refs = refs[pos:pos + n_r]; pos += n_r
        dp_refs = refs[pos:pos + n_p]; pos += n_p
        s_refs = refs[pos:pos + n_sums]
        cvals, rvals = [], []
        for r, (_, widths) in zip(c_refs, consts):
            cvals += _pieces(r, widths)
        for r, (_, widths) in zip(r_refs, rows):
            rvals += _pieces(r, widths)
        pvals = [p[...] for p in p_refs]
        ctv = []
        for r, widths in zip(ct_refs, outs):
            ctv += _pieces(r, widths)
        ctv += [jnp.ones((1, 1), f32)] * n_sums
        primal, vjp = jax.vjp(lambda *rp: tuple(fn(*cvals, *rp)), *rvals, *pvals)
        g = vjp(tuple(ctv))
        pos = 0
        for idx, (r, (_, widths)) in enumerate(zip(dr_refs, rows)):
            _store_pieces(r, widths, g[pos:pos + len(widths)], add_ref if idx == 0 else None)
            pos += len(widths)

        @pl.when(pl.program_id(0) == 0)
        def _():
            for acc in list(dp_refs) + list(s_refs):
                acc[...] = jnp.zeros_like(acc)

        for dp, v in zip(dp_refs, g[pos:]):
            dp[...] += v
        for s, v in zip(s_refs, primal[len(primal) - n_sums:]):
            s[...] += v

    row_spec = lambda w: pl.BlockSpec((tm, w), lambda i: (i, 0))
    full = lambda p: pl.BlockSpec(p.shape, lambda i: (0,) * p.ndim)
    args = [a for a, _ in consts + rows] + list(params) + list(cts) + ([add] if has_add else [])
    res = pl.pallas_call(
        body, name=name, grid=(t // tm,),
        in_specs=[row_spec(a.shape[1]) for a, _ in consts + rows] + [full(p) for p in params]
        + [row_spec(sum(w)) for w in outs] + ([row_spec(add.shape[1])] if has_add else []),
        out_specs=[row_spec(a.shape[1]) for a, _ in rows] + [full(p) for p in params]
        + [pl.BlockSpec((1, 1), lambda i: (0, 0))] * n_sums,
        out_shape=[jax.ShapeDtypeStruct(a.shape, dt) for (a, _), dt in zip(rows, dtypes)]
        + [jax.ShapeDtypeStruct(p.shape, f32) for p in params] + [jax.ShapeDtypeStruct((1, 1), f32)] * n_sums,
        compiler_params=_cp(("arbitrary",)),
    )(*args)
    return res[:n_r], res[n_r:n_r + n_p] + res[n_r + n_p:]


def _matmul_then_vjp(name, a, b, mode, fn, rows, dtypes, tm=256):
    m, k = a.shape
    tm = min(tm, m)
    dims = (((1,), (0,)), ((), ())) if mode == "nn" else (((1,), (1,)), ((), ()))
    n_r = len(rows)

    def body(*refs):
        a_ref, b_ref = refs[:2]
        ct = lax.dot_general(a_ref[...].astype(bf16), b_ref[...].astype(bf16), dims, preferred_element_type=f32)
        rvals = []
        for r, (_, widths) in zip(refs[2:2 + n_r], rows):
            rvals += _pieces(r, widths)
        _, vjp = jax.vjp(lambda *rp: fn(*rp)[0], *rvals)
        g = vjp(ct)
        pos = 0
        for r, (_, widths) in zip(refs[2 + n_r:], rows):
            _store_pieces(r, widths, g[pos:pos + len(widths)])
            pos += len(widths)

    row_spec = lambda w: pl.BlockSpec((tm, w), lambda i: (i, 0))
    return pl.pallas_call(
        body, name=name, grid=(m // tm,),
        in_specs=[row_spec(k), pl.BlockSpec(b.shape, lambda i: (0, 0))] + [row_spec(r.shape[1]) for r, _ in rows],
        out_specs=[row_spec(r.shape[1]) for r, _ in rows],
        out_shape=[jax.ShapeDtypeStruct(r.shape, dt) for (r, _), dt in zip(rows, dtypes)],
        compiler_params=_cp(("parallel",)),
    )(a, b, *[r for r, _ in rows])


def _rms(x, g):
    return x * lax.rsqrt(jnp.mean(x * x, axis=-1, keepdims=True) + NORM_EPS) * g


def _fn_rms(x, g):
    return (_rms(x, g),)


def _fn_rwkv_pre(r, k, v, wd, ad, gd, w0, w_up, a0, a_up, g_up, k_k, k_a):
    nn, _, _ = _make_mm(False, False)
    w_log = -_sigmoid(w0 + nn(jnp.tanh(wd), w_up)) * 0.6065306597126334
    a = _sigmoid(a0 + nn(ad, a_up))
    g = nn(_sigmoid(gd), g_up)
    kk = k * k_k
    kk = kk * lax.rsqrt(jnp.maximum(_head_sum(kk * kk), 1e-24))
    k2 = k * (1.0 + (a - 1.0) * k_a)
    return r, w_log, k2, v, -kk, kk * a, g


def _fn_rwkv_post(y, r, k2, v, g, gn_g, gn_b, r_k):
    mean = _head_sum(y) * (1.0 / HD)
    yc = y - mean
    var = _head_sum(yc * yc) * (1.0 / HD)
    yn = yc * lax.rsqrt(var + GN_EPS) * gn_g + gn_b
    bonus = _head_sum(r * k2 * r_k) * v
    return ((yn + bonus) * g,)


def _fn_merge(a_fox, a_rwkv, a_mem, g_fox, g_rwkv, g_mem):
    return (_sigmoid(g_fox) * a_fox + _sigmoid(g_rwkv) * a_rwkv + _sigmoid(g_mem) * a_mem,)


def _fn_post1(y, x, post1_g, pre2_g):
    h1 = x + _rms(y, post1_g)
    return h1, _rms(h1, pre2_g)


def _fn_swiglu(gp, up):
    return (gp * _sigmoid(gp) * up,)


def _fn_final(target, ffn, h1, post2_g):
    err = h1 + _rms(ffn, post2_g) - target
    per_row = jnp.mean(err * err, axis=-1, keepdims=True)
    return (0.5 * jnp.sum(per_row, axis=0, keepdims=True),)


def _shift_down(x):
    row = lax.broadcasted_iota(jnp.int32, x.shape, 0)
    return jnp.where(row == 0, 0.0, pltpu.roll(x, 1, 0))


def _shift_up(x):
    s = x.shape[0]
    row = lax.broadcasted_iota(jnp.int32, x.shape, 0)
    return jnp.where(row == s - 1, 0.0, pltpu.roll(x, s - 1, 0))


def _tokshift_fwd(p, mu, batch, seq):
    w = p.shape[1]
    tc = _tile(w, 384)

    def body(p_ref, mu_ref, o_ref):
        x = p_ref[...]
        o_ref[...] = x + (_shift_down(x) - x) * mu_ref[...]

    return pl.pallas_call(
        body, name="tokshift_fwd", grid=(w // tc, batch),
        in_specs=[pl.BlockSpec((seq, tc), lambda j, b: (b, j)), pl.BlockSpec((1, tc), lambda j, b: (0, j))],
        out_specs=pl.BlockSpec((seq, tc), lambda j, b: (b, j)),
        out_shape=jax.ShapeDtypeStruct(p.shape, f32),
        compiler_params=_cp(("parallel", "arbitrary")),
    )(p, mu)


def _tokshift_bwd(p, mu, dps, batch, seq):
    w = p.shape[1]
    tc = _tile(w, 384)

    def body(p_ref, mu_ref, d_ref, dp_ref, dmu_ref):
        x, mu_v, d = p_ref[...], mu_ref[...], d_ref[...]
        dp_ref[...] = (d * (1.0 - mu_v) + _shift_up(d * mu_v)).astype(dp_ref.dtype)

        @pl.when(pl.program_id(1) == 0)
        def _():
            dmu_ref[...] = jnp.zeros_like(dmu_ref)

        dmu_ref[...] += jnp.sum(d * (_shift_down(x) - x), axis=0, keepdims=True)

    return pl.pallas_call(
        body, name="tokshift_bwd", grid=(w // tc, batch),
        in_specs=[pl.BlockSpec((seq, tc), lambda j, b: (b, j)), pl.BlockSpec((1, tc), lambda j, b: (0, j)),
                  pl.BlockSpec((seq, tc), lambda j, b: (b, j))],
        out_specs=[pl.BlockSpec((seq, tc), lambda j, b: (b, j)), pl.BlockSpec((1, tc), lambda j, b: (0, j))],
        out_shape=[jax.ShapeDtypeStruct(p.shape, bf16), jax.ShapeDtypeStruct(mu.shape, f32)],
        compiler_params=_cp(("parallel", "arbitrary")),
    )(p, mu, dps)


def _cum_block(seq):
    return _tile(seq, 256)


def _fox_gate_fwd(f, bias, batch, seq):
    cb = _cum_block(seq)

    def body(f_ref, b_ref, c_ref):
        row = lax.broadcasted_iota(jnp.int32, (cb, cb), 0)
        col = lax.broadcasted_iota(jnp.int32, (cb, cb), 1)
        tri = (col <= row).astype(f32)
        carry = jnp.zeros((1, 128), f32)
        for i in range(seq // cb):
            z = f_ref[i * cb:(i + 1) * cb, :] + b_ref[...]
            ls = jnp.minimum(z, 0.0) - jnp.log(1.0 + jnp.exp(-jnp.abs(z)))
            c = _dg(tri, ls, (((1,), (0,)), ((), ())), True) + carry
            c_ref[i * cb:(i + 1) * cb, :] = c
            carry = c[cb - 1:cb, :]

    return pl.pallas_call(
        body, name="fox_gate_fwd", grid=(batch,),
        in_specs=[pl.BlockSpec((seq, 128), lambda b: (b, 0)), pl.BlockSpec((1, 128), lambda b: (0, 0))],
        out_specs=pl.BlockSpec((seq, 128), lambda b: (b, 0)),
        out_shape=jax.ShapeDtypeStruct(f.shape, f32),
        compiler_params=_cp(("arbitrary",)),
    )(f, bias)


def _fox_gate_bwd(f, bias, dc_a, dc_b, batch, seq):
    cb = _cum_block(seq)

    def body(f_ref, b_ref, da_ref, db_ref, df_ref, dbias_ref):
        row = lax.broadcasted_iota(jnp.int32, (cb, cb), 0)
        col = lax.broadcasted_iota(jnp.int32, (cb, cb), 1)
        triu = (col >= row).astype(f32)

        @pl.when(pl.program_id(0) == 0)
        def _():
            dbias_ref[...] = jnp.zeros_like(dbias_ref)

        lane = lax.broadcasted_iota(jnp.int32, (1, 128), 1)

        def by_head(blk):
            out = jnp.zeros((cb, 128), f32)
            for p in range(HEADS // 2):
                for e in range(2):
                    out = jnp.where(lane == 2 * p + e, _pick_lane(blk[:, p * 128:(p + 1) * 128], e), out)
            return out

        carry = jnp.zeros((1, 128), f32)
        tot = jnp.zeros((1, 128), f32)
        for i in reversed(range(seq // cb)):
            sl = slice(i * cb, (i + 1) * cb)
            dc = by_head(da_ref[sl, :] + db_ref[sl, :])
            dls = _dg(triu, dc, (((1,), (0,)), ((), ())), True) + carry
            carry = dls[0:1, :]
            df = dls * _sigmoid(-(f_ref[sl, :] + b_ref[...]))
            df_ref[sl, :] = df.astype(df_ref.dtype)
            tot = tot + jnp.sum(df, axis=0, keepdims=True)
        dbias_ref[...] += tot

    return pl.pallas_call(
        body, name="fox_gate_bwd", grid=(batch,),
        in_specs=[pl.BlockSpec((seq, 128), lambda b: (b, 0)), pl.BlockSpec((1, 128), lambda b: (0, 0)),
                  pl.BlockSpec((seq, HW), lambda b: (b, 0)), pl.BlockSpec((seq, HW), lambda b: (b, 0))],
        out_specs=[pl.BlockSpec((seq, 128), lambda b: (b, 0)), pl.BlockSpec((1, 128), lambda b: (0, 0))],
        out_shape=[jax.ShapeDtypeStruct(f.shape, bf16), jax.ShapeDtypeStruct((1, 128), f32)],
        compiler_params=_cp(("arbitrary",)),
    )(f, bias, dc_a, dc_b)


_HBM_SPEC = pl.BlockSpec(memory_space=pltpu.HBM)


def _side_out_shapes(srcs, per_peer):
    return [jax.ShapeDtypeStruct(((N_DEV,) + tuple(s.shape[1:] if per_peer else s.shape)), s.dtype) for s in srcs]


def _side_sems(n):
    if n == 0:
        return []
    return [pltpu.SemaphoreType.DMA((n, N_DEV - 1)), pltpu.SemaphoreType.DMA((n, N_DEV - 1)), pltpu.SemaphoreType.DMA((n,))]


def _peer_copies(src_refs, dst_refs, per_peer, sems):
    send_sems, recv_sems, local_sems = sems
    x, y, c = lax.axis_index("x"), lax.axis_index("y"), lax.axis_index("c")
    me = 4 * x + 2 * y + c

    def remote(src, dst, t, k, to):
        return pltpu.make_async_remote_copy(src_ref=src, dst_ref=dst, send_sem=send_sems.at[t, k - 1],
                                            recv_sem=recv_sems.at[t, k - 1], device_id=to,
                                            device_id_type=pl.DeviceIdType.MESH)

    direct, relays = [], []
    for t, (s, d) in enumerate(zip(src_refs, dst_refs)):
        direct.append((t, 0, pltpu.make_async_copy(s.at[me] if per_peer else s, d.at[me], local_sems.at[t])))
        for k in range(1, N_DEV):
            px = 1 - x if k & 4 else x
            py = 1 - y if k & 2 else y
            pc = 1 - c if k & 1 else c
            if per_peer:
                direct.append((t, k, remote(s.at[4 * px + 2 * py + pc], d.at[me], t, k, (px, py, pc))))
            elif k == 1 or not k & 1:
                direct.append((t, k, remote(s, d.at[me], t, k, (px, py, pc))))
            else:
                origin = d.at[4 * px + 2 * py + c]
                relays.append((t, k - 1, remote(origin, origin, t, k, (x, y, 1 - c))))
    return direct, relays


def _exchange_start(direct):
    for _, _, cp in direct:
        cp.start()


def _exchange_relay(direct, relays):
    landed = {(t, k): cp for t, k, cp in direct}
    for t, j, cp in relays:
        landed[(t, j)].wait_recv()
        cp.start()


def _exchange_finish(direct, relays):
    relayed = {(t, j) for t, j, _ in relays}
    for t, k, cp in direct:
        if k == 0:
            cp.wait()
        else:
            cp.wait_send()
            if (t, k) not in relayed:
                cp.wait_recv()
    for _, _, cp in relays:
        cp.wait()


def _side_exchange(src_refs, dst_refs, per_peer, sems, *grid):
    if not src_refs:
        return
    step, total = 0, 1
    for a, n in enumerate(grid):
        step, total = step * n + pl.program_id(a), total * n

    @pl.when(step == 0)
    def _():
        _exchange_start(_peer_copies(src_refs, dst_refs, per_peer, sems)[0])

    @pl.when(step == (3 * total) // 4)
    def _():
        _exchange_relay(*_peer_copies(src_refs, dst_refs, per_peer, sems))

    @pl.when(step == total - 1)
    def _():
        _exchange_finish(*_peer_copies(src_refs, dst_refs, per_peer, sems))


def _exchange(name, srcs, per_peer):
    n = len(srcs)

    def body(*refs):
        direct, relays = _peer_copies(refs[:n], refs[n:2 * n], per_peer, refs[2 * n:])
        _exchange_start(direct)
        _exchange_relay(direct, relays)
        _exchange_finish(direct, relays)

    return pl.pallas_call(
        body, name=name, in_specs=[_HBM_SPEC] * n, out_specs=[_HBM_SPEC] * n,
        out_shape=_side_out_shapes(srcs, per_peer), scratch_shapes=_side_sems(n),
    )(*srcs)


FOX_T = 512
_NEG = -1e30
_D2 = (((1,), (1,)), ((), ()))
_D1 = (((1,), (0,)), ((), ()))
_D0 = (((0,), (0,)), ((), ()))


def _bdot(a, b, dims):
    return lax.dot_general(a.astype(bf16), b.astype(bf16), dims, preferred_element_type=f32)


def _pick_lane(x, lane):
    idx = lax.broadcasted_iota(jnp.int32, x.shape, 1)
    return jnp.sum(jnp.where(idx == lane, x, 0.0), axis=1, keepdims=True)


def _pick_row(x, row):
    idx = lax.broadcasted_iota(jnp.int32, x.shape, 0)
    return jnp.sum(jnp.where(idx == row, x, 0.0), axis=0, keepdims=True)


def _fox_fwd(qkv, c, c_rows, batch, seq, side=None):
    t = min(FOX_T, seq)
    nq = seq // t
    scale = HD ** -0.5
    srcs, per_peer = side if side is not None else ([], False)
    n_s = len(srcs)

    def body(*refs):
        q_ref, k_ref, v_ref, cq_ref, ck_ref = refs[:5]
        o_ref, lse_ref = refs[5 + n_s:7 + n_s]
        _side_exchange(refs[5:5 + n_s], refs[7 + n_s:7 + 2 * n_s], per_peer, refs[7 + 2 * n_s:], batch, PAIRS, nq)
        pair, i = pl.program_id(1), pl.program_id(2)
        lane = lax.broadcasted_iota(jnp.int32, (1, PAIR_W), 1)
        first = (lane // HD) == 0
        mine = [first, jnp.logical_not(first)]
        q = q_ref[...] * scale
        qs = [jnp.where(mine[e], q, 0.0) for e in range(2)]
        cqs = [_pick_lane(cq_ref[...], 2 * pair + e) for e in range(2)]
        causal = lax.broadcasted_iota(jnp.int32, (t, t), 1) <= lax.broadcasted_iota(jnp.int32, (t, t), 0)

        def block(j, carry, diagonal):
            rows = pl.ds(pl.multiple_of(j * t, t), t)
            kj, vj = k_ref[rows, :], v_ref[rows, :]
            ck_blk = ck_ref[0, :, rows]
            out = []
            for e in range(2):
                m, acc = carry[2 * e:2 * e + 2]
                s = _bdot(qs[e], kj, _D2) + cqs[e] - _pick_row(ck_blk, 2 * pair + e)
                if diagonal:
                    s = jnp.where(causal, s, _NEG)
                m_new = jnp.maximum(m, jnp.max(s, axis=1, keepdims=True))
                p = jnp.exp(s - m_new)
                out += [m_new, jnp.exp(m - m_new) * acc + _bdot(p, jnp.where(mine[e], vj, 1.0), _D1)]
            return tuple(out)

        init = (jnp.full((t, 1), _NEG, f32), jnp.zeros((t, PAIR_W), f32)) * 2
        carry = lax.fori_loop(0, i, lambda j, cr: block(j, cr, False), init)
        m0, a0, m1, a1 = block(i, carry, True)
        l0, l1 = _pick_lane(a0, HD), _pick_lane(a1, 0)
        o_ref[...] = jnp.where(first, a0 / l0, a1 / l1)
        lse_ref[...] = jnp.where(lane == 0, m0 + jnp.log(l0), jnp.where(lane == 1, m1 + jnp.log(l1), 0.0))

    q_spec = pl.BlockSpec((t, PAIR_W), lambda b, p, i: (b * nq + i, p))
    res = pl.pallas_call(
        body, name="fox_attn_fwd", grid=(batch, PAIRS, nq),
        in_specs=[q_spec,
                  pl.BlockSpec((seq, PAIR_W), lambda b, p, i: (b, PAIRS + p)),
                  pl.BlockSpec((seq, PAIR_W), lambda b, p, i: (b, 2 * PAIRS + p)),
                  pl.BlockSpec((t, 128), lambda b, p, i: (b * nq + i, 0)),
                  pl.BlockSpec((1, 8, seq), lambda b, p, i: (b, 0, 0))] + [_HBM_SPEC] * n_s,
        out_specs=[q_spec, q_spec] + [_HBM_SPEC] * n_s,
        out_shape=[jax.ShapeDtypeStruct((batch * seq, HW), f32)] * 2 + _side_out_shapes(srcs, per_peer),
        scratch_shapes=_side_sems(n_s),
        compiler_params=_cp(("arbitrary", "arbitrary", "arbitrary")),
    )(qkv, qkv, qkv, c, c_rows, *srcs)
    return res[0], res[1], list(res[2:])


def _fox_bwd(qkv, c, c_rows, o, lse, do, batch, seq):
    t = min(FOX_T, seq)
    nq = seq // t
    scale = HD ** -0.5

    def body(q_ref, k_ref, v_ref, cq_ref, ck_ref, o_ref, lse_ref, do_ref,
             dq_ref, dk_ref, dv_ref, dcq_ref, dck_ref, acc0, acc1):
        pair, i = pl.program_id(1), pl.program_id(2)
        accs = [acc0, acc1]

        @pl.when(i == 0)
        def _():
            dv_ref[...] = jnp.zeros_like(dv_ref)
            acc0[...] = jnp.zeros_like(acc0)
            acc1[...] = jnp.zeros_like(acc1)

        lane = lax.broadcasted_iota(jnp.int32, (1, PAIR_W), 1)
        first = (lane // HD) == 0
        mine = [first, jnp.logical_not(first)]
        q, d_o, o_i = q_ref[...] * scale, do_ref[...], o_ref[...]
        q0s = [jnp.where(mine[e], q, 0.0) for e in range(2)]
        q1s = [jnp.where(mine[e], q, 1.0) for e in range(2)]
        dos = [jnp.where(mine[e], d_o, 0.0) for e in range(2)]
        deltas = [jnp.sum(dos[e] * o_i, axis=1, keepdims=True) for e in range(2)]
        lses = [_pick_lane(lse_ref[...], e) for e in range(2)]
        cqs = [_pick_lane(cq_ref[...], 2 * pair + e) for e in range(2)]
        causal = lax.broadcasted_iota(jnp.int32, (t, t), 1) <= lax.broadcasted_iota(jnp.int32, (t, t), 0)

        def block(j, dqs, diagonal):
            rows = pl.ds(pl.multiple_of(j * t, t), t)
            kj, vj = k_ref[rows, :], v_ref[rows, :]
            ck_blk = ck_ref[0, :, rows]
            out = []
            for e in range(2):
                s = _bdot(q0s[e], kj, _D2) + cqs[e] - _pick_row(ck_blk, 2 * pair + e)
                if diagonal:
                    s = jnp.where(causal, s, _NEG)
                p = jnp.exp(s - lses[e])
                ds = p * (_bdot(dos[e], vj, _D2) - deltas[e])
                dv_ref[rows, :] += _bdot(p, dos[e], _D0)
                accs[e][rows, :] += _bdot(ds, q1s[e], _D0)
                out.append(dqs[e] + _bdot(ds, jnp.where(mine[e], kj, 1.0), _D1))
            return tuple(out)

        zero = jnp.zeros((t, PAIR_W), f32)
        dqs = lax.fori_loop(0, i, lambda j, cr: block(j, cr, False), (zero, zero))
        dq0, dq1 = block(i, dqs, True)
        dq_ref[...] = jnp.where(first, dq0, dq1) * scale
        dcq_ref[...] = jnp.where(lane == 0, _pick_lane(dq0, HD), jnp.where(lane == 1, _pick_lane(dq1, 0), 0.0))

        @pl.when(i == nq - 1)
        def _():
            a0, a1 = acc0[...], acc1[...]
            dk_ref[...] = jnp.where(first, a0, a1)
            dck_ref[...] = jnp.where(lane == 0, -_pick_lane(a0, HD), jnp.where(lane == 1, -_pick_lane(a1, 0), 0.0))

    blk = lambda col: pl.BlockSpec((t, PAIR_W), lambda b, p, i: (b * nq + i, col * PAIRS + p))
    whole = lambda col: pl.BlockSpec((seq, PAIR_W), lambda b, p, i: (b, col * PAIRS + p))
    t_all = batch * seq
    return pl.pallas_call(
        body, name="fox_attn_bwd", grid=(batch, PAIRS, nq),
        in_specs=[blk(0), whole(1), whole(2),
                  pl.BlockSpec((t, 128), lambda b, p, i: (b * nq + i, 0)),
                  pl.BlockSpec((1, 8, seq), lambda b, p, i: (b, 0, 0)),
                  blk(0), blk(0), blk(0)],
        out_specs=[blk(0), whole(0), whole(0), blk(0), whole(0)],
        out_shape=[jax.ShapeDtypeStruct((t_all, HW), f32)] * 5,
        scratch_shapes=[pltpu.VMEM((seq, PAIR_W), f32), pltpu.VMEM((seq, PAIR_W), f32)],
        compiler_params=_cp(("parallel", "parallel", "arbitrary")),
    )(qkv, qkv, qkv, c, c_rows, o, lse, do)


MEM_TQ = 1024


def _mem_block(q, km, vm):
    nn, nt, _ = _make_mm(False, False)
    logits = nt(q, km) * (MEM_HD ** -0.5)
    m = lax.stop_gradient(jnp.max(logits, axis=-1, keepdims=True))
    e = jnp.exp(logits - m)
    return nn(e / jnp.sum(e, axis=-1, keepdims=True), vm)


def _mem_specs(seq, tq):
    nq = seq // tq
    qs = pl.BlockSpec((tq, MEM_HD), lambda b, h, i: (b * nq + i, h))
    ks = pl.BlockSpec((MEM_LEN, MEM_HD), lambda b, h, i: (b, h))
    vs = pl.BlockSpec((MEM_LEN, MEM_HD), lambda b, h, i: (b, MEM_HEADS + h))
    return nq, qs, ks, vs


def _mem_fwd(q, mem_kv, batch, seq):
    tq = min(MEM_TQ, seq)
    nq, qs, ks, vs = _mem_specs(seq, tq)

    def body(q_ref, k_ref, v_ref, o_ref):
        o_ref[...] = _mem_block(q_ref[...].astype(f32), k_ref[...], v_ref[...]).astype(o_ref.dtype)

    return pl.pallas_call(
        body, name="mem_attn_fwd", grid=(batch, MEM_HEADS, nq),
        in_specs=[qs, ks, vs], out_specs=qs, out_shape=jax.ShapeDtypeStruct(q.shape, bf16),
        compiler_params=_cp(("parallel", "parallel", "arbitrary")),
    )(q, mem_kv, mem_kv)


def _mem_bwd(q, mem_kv, do, batch, seq):
    tq = min(MEM_TQ, seq)
    nq, qs, ks, vs = _mem_specs(seq, tq)

    def body(q_ref, k_ref, v_ref, do_ref, dq_ref, dk_ref, dv_ref):
        _, vjp = jax.vjp(_mem_block, q_ref[...].astype(f32), k_ref[...], v_ref[...])
        dq, dk, dv = vjp(do_ref[...])
        dq_ref[...] = dq.astype(dq_ref.dtype)

        @pl.when(pl.program_id(2) == 0)
        def _():
            dk_ref[...] = jnp.zeros_like(dk_ref)
            dv_ref[...] = jnp.zeros_like(dv_ref)

        dk_ref[...] += dk
        dv_ref[...] += dv

    return pl.pallas_call(
        body, name="mem_attn_bwd", grid=(batch, MEM_HEADS, nq),
        in_specs=[qs, ks, vs, qs], out_specs=[qs, ks, ks],
        out_shape=[jax.ShapeDtypeStruct(q.shape, bf16), jax.ShapeDtypeStruct((batch * MEM_LEN, MEM_W), f32),
                   jax.ShapeDtypeStruct((batch * MEM_LEN, MEM_W), f32)],
        compiler_params=_cp(("parallel", "parallel", "arbitrary")),
    )(q, mem_kv, mem_kv, do)


@jax.custom_vjp
def _halves(x):
    c = x.shape[1] // 2
    return x[:, :c], x[:, c:]


_halves.defvjp(lambda x: ((x[:, :x.shape[1] // 2], x[:, x.shape[1] // 2:]), None),
               lambda _, g: (jnp.concatenate(g, axis=1),))


@jax.custom_vjp
def _lead_halves(x):
    n = x.shape[0] // 2
    return x[:n], x[n:]


_lead_halves.defvjp(lambda x: ((x[:x.shape[0] // 2], x[x.shape[0] // 2:]), None),
                    lambda _, g: (jnp.concatenate(g, axis=0),))


def _scan_chunk(s0, r, wl, k, v, a, b):
    nn, nt, tn = _make_mm(True, False)
    nn_exact, _, _ = _make_mm(True, True)
    _, nt_exact, _ = _make_mm(True, "split")
    hp, c, lanes = r.shape
    row = lax.broadcasted_iota(jnp.int32, (c, c), 0)
    col = lax.broadcasted_iota(jnp.int32, (c, c), 1)
    first = (lax.broadcasted_iota(jnp.int32, (1, 1, lanes), 2) // HD) == 0
    tri = jnp.broadcast_to((col <= row).astype(f32)[None], (hp, c, c))
    lg = nn_exact(tri, wl)
    lg_end = lg[:, c - 1:c, :]
    grow, shrink, to_end = jnp.exp(lg), jnp.exp(-lg), jnp.exp(lg_end - lg)
    rt, kt, bt, at = r * grow, k * shrink, b * shrink, a * jnp.exp(lg - wl)
    strict, incl = (col < row)[None], (col <= row)[None]
    twice = lambda t: jnp.concatenate([t, t], axis=0)
    queries = jnp.concatenate([at, rt], axis=1)
    per_head = jnp.concatenate([jnp.where(first, queries, 0.0), jnp.where(first, 0.0, queries)], axis=0)
    (ab, rb), (ak, rk) = _halves(nt_exact(per_head, twice(bt))), _halves(nt_exact(per_head, twice(kt)))
    l_ab = jnp.where(strict, ab, 0.0)
    a_ak = jnp.where(strict, ak, 0.0)
    a_rb = jnp.where(incl, rb, 0.0)
    a_rk = jnp.where(incl, rk, 0.0)
    inv = (col == row).astype(f32)[None] + l_ab
    power, n = l_ab, 1
    while 2 * n < c:
        power = nn(power, power)
        inv = inv + nn(inv, power)
        n *= 2

    def apply(m, t):
        lo, hi = _lead_halves(nn(m, twice(t)))
        return jnp.where(first, lo, hi)

    sa = apply(inv, nt(at, s0) + apply(a_ak, v))
    y = nt(rt, s0) + apply(a_rk, v) + apply(a_rb, sa)
    same_head = ((lax.broadcasted_iota(jnp.int32, (lanes, lanes), 0) // HD)
                 == (lax.broadcasted_iota(jnp.int32, (lanes, lanes), 1) // HD))[None]
    s1 = s0 * jnp.exp(lg_end) + jnp.where(same_head, tn(v, k * to_end) + tn(sa, b * to_end), 0.0)
    return y, s1


PAIRS = HEADS // 2
PAIR_W = 2 * HD


def _pair_stack(ref, off):
    return jnp.stack([ref[b, :, off + p * PAIR_W:off + (p + 1) * PAIR_W]
                      for b in range(ref.shape[0]) for p in range(PAIRS)])


def _pair_store(ref, off, val, add_ref=None):
    for b in range(ref.shape[0]):
        for p in range(PAIRS):
            sl = slice(off + p * PAIR_W, off + (p + 1) * PAIR_W)
            v = val[b * PAIRS + p]
            ref[b, :, sl] = v if add_ref is None else v + add_ref[b, :, sl]


def _scan_fwd(main6, batch, seq, side=None):
    c = min(SCAN_CHUNK, seq)
    nc = seq // c
    hp = batch * PAIRS
    srcs, per_peer = side if side is not None else ([], False)
    n_s = len(srcs)

    def body(*refs):
        z_ref, y_ref, s_ref, st = refs[0], refs[1 + n_s], refs[2 + n_s], refs[3 + 2 * n_s]
        _side_exchange(refs[1:1 + n_s], refs[3 + n_s:3 + 2 * n_s], per_peer, refs[4 + 2 * n_s:], nc)

        @pl.when(pl.program_id(0) == 0)
        def _():
            st[...] = jnp.zeros_like(st)

        s0 = st[...]
        s_ref[0] = s0
        y, s1 = _scan_chunk(s0, *[_pair_stack(z_ref, comp * HW) for comp in range(6)])
        _pair_store(y_ref, 0, y)
        st[...] = s1

    res = pl.pallas_call(
        body, name="rwkv_scan_fwd", grid=(nc,),
        in_specs=[pl.BlockSpec((batch, c, 6 * HW), lambda i: (0, i, 0))] + [_HBM_SPEC] * n_s,
        out_specs=[pl.BlockSpec((batch, c, HW), lambda i: (0, i, 0)),
                   pl.BlockSpec((1, hp, PAIR_W, PAIR_W), lambda i: (i, 0, 0, 0))] + [_HBM_SPEC] * n_s,
        out_shape=[jax.ShapeDtypeStruct((batch, seq, HW), f32), jax.ShapeDtypeStruct((nc, hp, PAIR_W, PAIR_W), f32)]
        + _side_out_shapes(srcs, per_peer),
        scratch_shapes=[pltpu.VMEM((hp, PAIR_W, PAIR_W), f32)] + _side_sems(n_s),
        compiler_params=_cp(("arbitrary",)),
    )(main6.reshape(batch, seq, 6 * HW), *srcs)
    return res[0].reshape(batch * seq, HW), res[1], list(res[2:])


def _scan_bwd(main6, states, dy, extra, batch, seq, side=None):
    c = min(SCAN_CHUNK, seq)
    nc = seq // c
    hp = batch * PAIRS
    srcs, per_peer = side if side is not None else ([], False)
    n_s = len(srcs)

    def body(*refs):
        z_ref, s_ref, dy_ref, ex_ref = refs[:4]
        dz_ref, dst = refs[4 + n_s], refs[5 + 2 * n_s]
        _side_exchange(refs[4:4 + n_s], refs[5 + n_s:5 + 2 * n_s], per_peer, refs[6 + 2 * n_s:], nc)

        @pl.when(pl.program_id(0) == 0)
        def _():
            dst[...] = jnp.zeros_like(dst)

        _, vjp = jax.vjp(_scan_chunk, s_ref[0], *[_pair_stack(z_ref, comp * HW) for comp in range(6)])
        g = vjp((_pair_stack(dy_ref, 0), dst[...]))
        dst[...] = g[0]
        for comp in range(6):
            _pair_store(dz_ref, comp * HW, g[1 + comp], ex_ref)

    back = lambda i: (0, nc - 1 - i, 0)
    wide = pl.BlockSpec((batch, c, 6 * HW), back)
    res = pl.pallas_call(
        body, name="rwkv_scan_bwd", grid=(nc,),
        in_specs=[wide, pl.BlockSpec((1, hp, PAIR_W, PAIR_W), lambda i: (nc - 1 - i, 0, 0, 0)),
                  pl.BlockSpec((batch, c, HW), back), wide] + [_HBM_SPEC] * n_s,
        out_specs=[wide] + [_HBM_SPEC] * n_s,
        out_shape=[jax.ShapeDtypeStruct((batch, seq, 6 * HW), f32)] + _side_out_shapes(srcs, per_peer),
        scratch_shapes=[pltpu.VMEM((hp, PAIR_W, PAIR_W), f32)] + _side_sems(n_s),
        compiler_params=_cp(("arbitrary",)),
    )(main6.reshape(batch, seq, 6 * HW), states, dy.reshape(batch, seq, HW), extra.reshape(batch, seq, 6 * HW), *srcs)
    return res[0].reshape(batch * seq, 6 * HW), list(res[1:])


def _pad_cols(x, width):
    return jnp.pad(x, ((0, 0), (0, width - x.shape[1])))


def _split_w_in(wt):
    z = lambda rows: jnp.zeros((rows, wt.shape[1]), wt.dtype)
    w_r = jnp.concatenate([wt[1544:3080], wt[3080:3144], z(64), wt[3144:3208], z(64), wt[3208:3336]], axis=0)
    return wt[:1536], jnp.concatenate([wt[1536:1544], z(120)], axis=0), w_r, wt[3336:3848], wt[3848:]


def _merge_w_in(g_qkv, g_f, g_r, g_mq, g_g):
    return jnp.concatenate([g_qkv, g_f[:8], g_r[:1536], g_r[1536:1600], g_r[1664:1728], g_r[1792:], g_mq, g_g], axis=0)


def _pad_lora(v):
    z64 = jnp.zeros((1, 64), v.dtype)
    return jnp.concatenate([v[:, :1536], v[:, 1536:1600], z64, v[:, 1600:1664], z64, v[:, 1664:]], axis=1)


def _unpad_lora(v):
    return jnp.concatenate([v[:, :1536], v[:, 1536:1600], v[:, 1664:1728], v[:, 1792:]], axis=1)


def _local_step(x, mem, target, w, p, late=None, early=None, last=None):
    batch, seq, _ = x.shape
    t = batch * seq
    x2, tg2, mem2 = x.reshape(t, D), target.reshape(t, D), mem.reshape(batch * MEM_LEN, D)
    w_qkv, w_f, w_r, w_mq, w_g3 = _split_w_in(w["w_in"])
    mu = _pad_lora(p["rwkv_mu"])
    bias = _pad_cols(p["fox_f_bias"], 128)
    r_k = p["rwkv_r_k"].reshape(1, HW)
    post_params = [p["rwkv_gn_g"], p["rwkv_gn_b"], r_k]
    rw_widths = [HW, HW, HW, LORA_PAD, LORA_PAD, LORA_PAD]
    six = [HW] * 6

    (u,) = _rows_fwd("rms_pre1", _fn_rms, [], [(x2, [D])], [p["pre1_g"]], [[D]], dtypes=[bf16])
    p_qkv = _matmul("proj_qkv", u, w_qkv, "nt", out_dtype=bf16)
    p_f = _matmul("proj_f", u, w_f, "nt")
    p_r = _matmul("proj_rwkv", u, w_r, "nt")
    p_mq = _matmul("proj_memq", u, w_mq, "nt", out_dtype=bf16)
    p_g = _matmul("proj_gate", u, w_g3, "nt", out_dtype=bf16)

    c = _fox_gate_fwd(p_f, bias, batch, seq)
    c_rows = c[:, :HEADS].reshape(batch, seq, HEADS).transpose(0, 2, 1)
    fox_o, lse, gathered = _fox_fwd(p_qkv, c, c_rows, batch, seq, side=(late[0], False) if late else None)
    if late:
        w = {**w, **late[2](gathered, 0)}
    fox_out = fox_o.astype(bf16)

    w_up = jnp.pad(w["rwkv_w_up"].astype(f32), ((0, LORA_PAD - 64), (0, 0)))
    a_up = jnp.pad(w["rwkv_a_up"].astype(f32), ((0, LORA_PAD - 64), (0, 0)))
    pre_params = [p["rwkv_w0"], w_up, p["rwkv_a0"], a_up, w["rwkv_g_up"].astype(f32), p["rwkv_k_k"], p["rwkv_k_a"]]
    ps = _tokshift_fwd(p_r, mu, batch, seq)
    main6, g_rw = _rows_fwd("rwkv_pre", _fn_rwkv_pre, [], [(ps, rw_widths)], pre_params, [six, [HW]], tm=256)
    y_rw, states, gathered = _scan_fwd(main6, batch, seq, side=(late[1], False) if late else None)
    if late:
        w = {**w, **late[2](gathered, 1)}
    post_consts = []
    post_rows = [(y_rw, [HW]), (main6, six), (g_rw, [HW])]

    def fn_post(y, r, _wl, k2, v, _a, _b, g, gn_g, gn_b, rk):
        return _fn_rwkv_post(y, r, k2, v, g, gn_g, gn_b, rk)

    (rwkv_out,) = _rows_fwd("rwkv_post", fn_post, post_consts, post_rows, post_params, [[HW]], dtypes=[bf16], tm=256)

    (memn,) = _rows_fwd("rms_mem", _fn_rms, [], [(mem2, [D])], [p["mem_norm_g"]], [[D]], dtypes=[bf16])
    mem_kv = _matmul("proj_memkv", memn, w["w_mem_kv"], "nn")
    mem_out = _mem_fwd(p_mq, mem_kv, batch, seq)

    a_fox = _matmul("out_fox", fox_out, w["w_fox_out"], "nn", out_dtype=bf16)
    a_rwkv = _matmul("out_rwkv", rwkv_out, w["w_rwkv_out"], "nn", out_dtype=bf16)
    a_mem = _matmul("out_mem", mem_out, w["w_mem_out"], "nn", out_dtype=bf16)
    merge_rows = [(a_fox, [D]), (a_rwkv, [D]), (a_mem, [D]), (p_g, [D, D, D])]
    (merged,) = _rows_fwd("merge", _fn_merge, [], merge_rows, [], [[D]], dtypes=[bf16])
    yy = _matmul("out_o", merged, w["w_o"], "nn")
    post1_rows = [(yy, [D]), (x2, [D])]
    post1_params = [p["post1_g"], p["pre2_g"]]
    h1, u2 = _rows_fwd("post1", _fn_post1, [], post1_rows, post1_params, [[D], [D]], dtypes=[f32, bf16])
    gp = _matmul("ffn_gate", u2, w["w_ffn_gate"], "nt", out_dtype=bf16)
    up = _matmul("ffn_up", u2, w["w_ffn_up"], "nt", out_dtype=bf16)
    (hmid,) = _rows_fwd("swiglu", _fn_swiglu, [], [(gp, [D_FF]), (up, [D_FF])], [], [[D_FF]], dtypes=[bf16])
    ffn = _matmul("ffn_down", hmid, w["w_ffn_down"], "nn")
    final_rows = [(ffn, [D]), (h1, [D])]

    gw, gp_ = {}, {}
    (d_ffn, d_h1), (gp_["post2_g"], loss) = _rows_bwd("final", _fn_final, [(tg2, [D])], final_rows, [p["post2_g"]], [], [],
                                                      n_sums=1, dtypes=[bf16, f32])
    gw["w_ffn_down"] = _matmul("ffn_down_dw", hmid, d_ffn, "tn", out_dtype=bf16)
    d_gp, d_up = _matmul_then_vjp("ffn_down_dx", d_ffn, w["w_ffn_down"], "nt", _fn_swiglu,
                                  [(gp, [D_FF]), (up, [D_FF])], [bf16, bf16])
    d_u2 = _matmul("ffn_gate_dx", d_gp, w["w_ffn_gate"], "nn")
    d_u2 = _matmul("ffn_up_dx", d_up, w["w_ffn_up"], "nn", add=d_u2)
    gw["w_ffn_gate"] = _matmul("ffn_gate_dw", d_gp, u2, "tn", out_dtype=bf16)
    gw["w_ffn_up"] = _matmul("ffn_up_dw", d_up, u2, "tn", out_dtype=bf16)
    (d_yy, d_x_res), (gp_["post1_g"], gp_["pre2_g"]) = _rows_bwd(
        "post1_bwd", _fn_post1, [], post1_rows, post1_params, [[D], [D]], [d_h1, d_u2], dtypes=[bf16, f32])
    gw["w_o"] = _matmul("out_o_dw", merged, d_yy, "tn", out_dtype=bf16)
    d_a_fox, d_a_rwkv, d_a_mem, d_p_g = _matmul_then_vjp("out_o_dx", d_yy, w["w_o"], "nt", _fn_merge, merge_rows, [bf16] * 4)
    d_fox_out = _matmul("out_fox_dx", d_a_fox, w["w_fox_out"], "nt")
    gw["w_fox_out"] = _matmul("out_fox_dw", fox_out, d_a_fox, "tn", out_dtype=bf16)
    d_rwkv_out = _matmul("out_rwkv_dx", d_a_rwkv, w["w_rwkv_out"], "nt")
    gw["w_rwkv_out"] = _matmul("out_rwkv_dw", rwkv_out, d_a_rwkv, "tn", out_dtype=bf16)
    d_mem_out = _matmul("out_mem_dx", d_a_mem, w["w_mem_out"], "nt")
    gw["w_mem_out"] = _matmul("out_mem_dw", mem_out, d_a_mem, "tn", out_dtype=bf16)

    d_p_mq, d_km, d_vm = _mem_bwd(p_mq, mem_kv, d_mem_out, batch, seq)
    d_mem_kv = jnp.concatenate([d_km, d_vm], axis=1).astype(bf16)
    gw["w_mem_kv"] = _matmul("proj_memkv_dw", memn, d_mem_kv, "tn", out_dtype=bf16)
    d_memn = _matmul("proj_memkv_dx", d_mem_kv, w["w_mem_kv"], "nt")
    _, (gp_["mem_norm_g"],) = _rows_bwd("rms_mem_bwd", _fn_rms, [], [(mem2, [D])], [p["mem_norm_g"]], [[D]], [d_memn])

    d_q, d_k, d_v, d_cq, d_ck = _fox_bwd(p_qkv, c, c_rows, fox_o, lse, d_fox_out, batch, seq)
    d_p_qkv = jnp.concatenate([d_q, d_k, d_v], axis=1).astype(bf16)
    d_p_f, d_bias = _fox_gate_bwd(p_f, bias, d_cq, d_ck, batch, seq)
    gp_["fox_f_bias"] = d_bias[:, :HEADS]

    (d_y_rw, d_main6_post, d_g_rw), (gp_["rwkv_gn_g"], gp_["rwkv_gn_b"], d_rk) = _rows_bwd(
        "rwkv_post_bwd", fn_post, post_consts, post_rows, post_params, [[HW]], [d_rwkv_out], tm=256)
    gp_["rwkv_r_k"] = d_rk.reshape(1, HEADS, HD)
    d_main6, early_got = _scan_bwd(main6, states, d_y_rw, d_main6_post, batch, seq,
                                   side=(early(gw), True) if early else None)

    def fn_pre_sum(*args):
        return _fn_rwkv_pre(*args)

    (d_ps,), d_pre = _rows_bwd("rwkv_pre_bwd", fn_pre_sum, [], [(ps, rw_widths)], pre_params, [six, [HW]],
                               [d_main6, d_g_rw], tm=256)
    gp_["rwkv_w0"], d_w_up, gp_["rwkv_a0"], d_a_up, gw["rwkv_g_up"], gp_["rwkv_k_k"], gp_["rwkv_k_a"] = d_pre
    gw["rwkv_w_up"], gw["rwkv_a_up"] = d_w_up[:64], d_a_up[:64]
    d_p_r, d_mu = _tokshift_bwd(p_r, mu, d_ps, batch, seq)
    gp_["rwkv_mu"] = _unpad_lora(d_mu)

    gw["w_in"] = _merge_w_in(_matmul("proj_qkv_dw", d_p_qkv, u, "tn", out_dtype=bf16), _matmul("proj_f_dw", d_p_f, u, "tn", out_dtype=bf16),
                             _matmul("proj_rwkv_dw", d_p_r, u, "tn", out_dtype=bf16), _matmul("proj_memq_dw", d_p_mq, u, "tn", out_dtype=bf16),
                             _matmul("proj_gate_dw", d_p_g, u, "tn", out_dtype=bf16))
    d_x, gp_["pre1_g"], last_got = _input_cotangent(
        "proj_dx", [d_p_qkv, d_p_f, d_p_r, d_p_mq, d_p_g], [w_qkv, w_f, w_r, w_mq, w_g3], x2, p["pre1_g"], d_x_res,
        side=(last(gw), True) if last else None)
    return loss, d_x.reshape(x.shape), gw, gp_, early_got, last_got


def _adamw(name, recv, row_off, w, m, v):
    _, rows, cols = w.shape
    row_tiles = [t for t in range(16, min(rows, 128) + 1, 16) if rows % t == 0 and row_off % t == 0]
    if row_tiles:
        tr, tc = max(row_tiles), cols
        first, grid = row_off // tr, (rows // tr,)
        at = lambda i: (0, first + i, 0)
        mine = lambda i: (0, i, 0)
    else:
        assert row_off == 0 and recv.shape[1] == rows
        tr, tc = rows, 128
        grid = (cols // tc,)
        at = mine = lambda i: (0, 0, i)

    def body(g_ref, w_ref, m_ref, v_ref, go_ref, d_ref, mo_ref, vo_ref):
        g = g_ref[0].astype(f32)
        for s in range(1, N_DEV):
            g = g + g_ref[s].astype(f32)
        m_new = ADAM_B1 * m_ref[0] + (1.0 - ADAM_B1) * g
        v_new = ADAM_B2 * v_ref[0] + (1.0 - ADAM_B2) * (g * g)
        m_hat = m_new / (1.0 - ADAM_B1 ** ADAM_STEP)
        v_hat = v_new / (1.0 - ADAM_B2 ** ADAM_STEP)
        go_ref[0] = g
        d_ref[0] = -ADAM_LR * (m_hat / (jnp.sqrt(v_hat) + ADAM_EPS) + ADAM_WD * w_ref[0])
        mo_ref[0] = m_new
        vo_ref[0] = v_new

    spec = pl.BlockSpec((1, tr, tc), mine)
    return pl.pallas_call(
        body, name=name, grid=grid,
        in_specs=[pl.BlockSpec((N_DEV, tr, tc), at), spec, spec, spec],
        out_specs=[spec] * 4, out_shape=[jax.ShapeDtypeStruct(w.shape, f32)] * 4,
        compiler_params=_cp(("parallel",)),
    )(recv, w, m, v)


GROUPS = (
    ("in", ("w_in",), 0),
    ("memkv", ("w_mem_kv",), 0),
    ("ffn_gu", ("w_ffn_gate", "w_ffn_up"), 0),
    ("down_o", ("w_ffn_down", "w_o"), 0),
    ("outs", ("w_fox_out", "w_rwkv_out", "w_mem_out"), 0),
    ("lora", ("rwkv_w_up", "rwkv_a_up", "rwkv_g_up"), 0),
)
FIRST_GROUPS = ("in", "memkv")
LATE_GROUPS = (("down_o", "outs", "lora"), ("ffn_gu",))
EARLY_GRAD_GROUPS = ("memkv", "ffn_gu", "down_o", "outs")
LAST_GRAD_GROUPS = ("in", "lora")
SHARD_AXIS = {n: a for n, _, a in SHARDED}
SMALL_ROWS = 16


def _group_local(shards, members, join):
    parts = [shards[n].reshape(shards[n].shape[-2:]) for n in members]
    return parts[0] if len(parts) == 1 else jnp.concatenate(parts, axis=join)


def _group_split(arr, members, join, lead=False):
    out, off = {}, 0
    for n in members:
        shape = dict((k, s) for k, s, _ in SHARDED)[n]
        size = _block_shape(shape, SHARD_AXIS[n])[join]
        idx = [slice(None)] * arr.ndim
        idx[arr.ndim - 2 + join] = slice(off, off + size)
        out[n] = arr[tuple(idx)]
        off += size
    return out


def _full_from_blocks(blocks, axis):
    if axis == 0:
        return blocks.reshape(-1, blocks.shape[2])
    return blocks.transpose(1, 0, 2).reshape(blocks.shape[1], -1)


def _blocks_from_full(full, axis):
    if axis == 0:
        return full.reshape(N_DEV, -1, full.shape[1])
    return full.reshape(full.shape[0], N_DEV, -1).transpose(1, 0, 2)


def _assemble(gathered, names):
    out = {}
    for arr, g in zip(gathered, names):
        _, members, join = [grp for grp in GROUPS if grp[0] == g][0]
        for n, blk in _group_split(arr, members, join, lead=True).items():
            out[n] = _full_from_blocks(blk, SHARD_AXIS[n])
    return out


def _grad_blocks(gw, names):
    out = []
    for g in names:
        _, members, join = [grp for grp in GROUPS if grp[0] == g][0]
        parts = [_blocks_from_full(gw[n].astype(bf16), SHARD_AXIS[n]) for n in members]
        out.append(parts[0] if len(parts) == 1 else jnp.concatenate(parts, axis=1 + join))
    return out


def _small_pack(d):
    flat = jnp.concatenate([d[n].reshape(-1) for n, _ in REPLICATED])
    return jnp.pad(flat, (0, SMALL_ROWS * LANES - REPL_ELEMS)).reshape(SMALL_ROWS, LANES)


def _small_unpack(packed):
    out, flat, off = {}, packed.reshape(-1), 0
    for n, shape in REPLICATED:
        k = _rows_of((LANES,) + shape)
        out[n] = flat[off:off + k].reshape(shape)
        off += k
    return out


def kernel(x, mem, pre1_g, post1_g, pre2_g, post2_g, mem_norm_g, w_in, fox_f_bias, rwkv_mu, rwkv_w0, rwkv_w_up, rwkv_a0, rwkv_a_up, rwkv_g_up, rwkv_k_k, rwkv_k_a, rwkv_r_k, rwkv_gn_g, rwkv_gn_b, w_mem_kv, w_fox_out, w_rwkv_out, w_mem_out, w_o, w_ffn_gate, w_ffn_up, w_ffn_down, loss_target, m_pre1_g, m_post1_g, m_pre2_g, m_post2_g, m_mem_norm_g, m_w_in, m_fox_f_bias, m_rwkv_mu, m_rwkv_w0, m_rwkv_w_up, m_rwkv_a0, m_rwkv_a_up, m_rwkv_g_up, m_rwkv_k_k, m_rwkv_k_a, m_rwkv_r_k, m_rwkv_gn_g, m_rwkv_gn_b, m_w_mem_kv, m_w_fox_out, m_w_rwkv_out, m_w_mem_out, m_w_o, m_w_ffn_gate, m_w_ffn_up, m_w_ffn_down, v_pre1_g, v_post1_g, v_pre2_g, v_post2_g, v_mem_norm_g, v_w_in, v_fox_f_bias, v_rwkv_mu, v_rwkv_w0, v_rwkv_w_up, v_rwkv_a0, v_rwkv_a_up, v_rwkv_g_up, v_rwkv_k_k, v_rwkv_k_a, v_rwkv_r_k, v_rwkv_gn_g, v_rwkv_gn_b, v_w_mem_kv, v_w_fox_out, v_w_rwkv_out, v_w_mem_out, v_w_o, v_w_ffn_gate, v_w_ffn_up, v_w_ffn_down):
    args = dict(locals())
    turn = lambda n, a: jnp.swapaxes(a, 1, 2) if n in TRANSPOSED else a
    wts = {n: turn(n, args[n]) for n in WEIGHT_ORDER}
    ms = {n: turn(n, args["m_" + n]) for n in WEIGHT_ORDER}
    vs = {n: turn(n, args["v_" + n]) for n in WEIGHT_ORDER}

    groups = {g: (members, join) for g, members, join in GROUPS}
    w_bf16 = {n: wts[n].astype(bf16) for n, _, _ in SHARDED}

    def send(g):
        return _group_local(w_bf16, *groups[g])

    first = _exchange("gather_first", [send(g) for g in FIRST_GROUPS], per_peer=False)
    full = _assemble(first, FIRST_GROUPS)
    small_in = {n: (wts[n] if n == "rwkv_r_k" else wts[n].reshape(wts[n].shape[-2:])) for n, _ in REPLICATED}
    late = ([send(g) for g in LATE_GROUPS[0]], [send(g) for g in LATE_GROUPS[1]],
            lambda got, which: _assemble(got, LATE_GROUPS[which]))
    loss_part, grad_x, gw, gp, early_got, last_got = _local_step(
        x, mem, loss_target, full, small_in, late=late, early=lambda g: _grad_blocks(g, EARLY_GRAD_GROUPS),
        last=lambda g: _grad_blocks(g, LAST_GRAD_GROUPS))
    (small_got,) = _exchange("exchange_small", [_small_pack(gp).astype(bf16)], per_peer=False)
    received = dict(zip(EARLY_GRAD_GROUPS + LAST_GRAD_GROUPS, list(early_got) + list(last_got)))

    outs = [{}, {}, {}, {}]
    for g, members, _ in GROUPS:
        off = 0
        for n in members:
            for o, arr in zip(outs, _adamw("adamw_" + n, received[g], off, wts[n], ms[n], vs[n])):
                o[n] = arr
            off += wts[n].shape[1]
    res = _adamw("adamw_small", small_got, 0, *[_small_pack(d)[None] for d in (wts, ms, vs)])
    for o, arr in zip(outs, res):
        o.update(_small_unpack(arr))
    loss = lax.psum(loss_part[0, 0], ("x", "y", "c"))
    return (loss, grad_x, *[turn(n, o[n].reshape(wts[n].shape)) for o in outs for n in WEIGHT_ORDER])
```

```python
import functools

import jax
import jax.numpy as jnp
from jax import lax
from jax.experimental import pallas as pl
from jax.experimental.pallas import tpu as pltpu

f32 = jnp.float32
bf16 = jnp.bfloat16
_HI = lax.Precision.HIGHEST

D = 1024
HEADS = 8
HD = 64
HW = HEADS * HD
MEM_HEADS = 4
MEM_HD = 128
MEM_W = 512
MEM_LEN = 256
D_FF = 2816
LORA_PAD = 128
NORM_EPS = 1e-6
GN_EPS = 64e-5
SCAN_CHUNK = 64
N_DEV = 8
LANES = 1024
VMEM_LIMIT = 56 * 1024 * 1024

ADAM_LR = 0.001
ADAM_B1 = 0.9
ADAM_B2 = 0.999
ADAM_EPS = 1e-08
ADAM_WD = 0.01
ADAM_STEP = 10

TRANSPOSED = ("w_in", "w_ffn_gate", "w_ffn_up")
SHARDED = (
    ("w_in", (6920, 1024), 0),
    ("w_ffn_gate", (2816, 1024), 0),
    ("w_ffn_up", (2816, 1024), 0),
    ("w_ffn_down", (2816, 1024), 0),
    ("w_mem_kv", (1024, 1024), 0),
    ("w_o", (1024, 1024), 0),
    ("w_fox_out", (512, 1024), 1),
    ("w_rwkv_out", (512, 1024), 1),
    ("w_mem_out", (512, 1024), 1),
    ("rwkv_w_up", (64, 512), 1),
    ("rwkv_a_up", (64, 512), 1),
    ("rwkv_g_up", (128, 512), 1),
)
REPLICATED = (
    ("pre1_g", (1, 1024)), ("post1_g", (1, 1024)), ("pre2_g", (1, 1024)), ("post2_g", (1, 1024)),
    ("mem_norm_g", (1, 1024)), ("fox_f_bias", (1, 8)), ("rwkv_mu", (1, 1792)), ("rwkv_w0", (1, 512)),
    ("rwkv_a0", (1, 512)), ("rwkv_k_k", (1, 512)), ("rwkv_k_a", (1, 512)), ("rwkv_r_k", (1, 8, 64)),
    ("rwkv_gn_g", (1, 512)), ("rwkv_gn_b", (1, 512)),
)
WEIGHT_ORDER = ('pre1_g', 'post1_g', 'pre2_g', 'post2_g', 'mem_norm_g', 'w_in', 'fox_f_bias', 'rwkv_mu',
                'rwkv_w0', 'rwkv_w_up', 'rwkv_a0', 'rwkv_a_up', 'rwkv_g_up', 'rwkv_k_k', 'rwkv_k_a',
                'rwkv_r_k', 'rwkv_gn_g', 'rwkv_gn_b', 'w_mem_kv', 'w_fox_out', 'w_rwkv_out', 'w_mem_out',
                'w_o', 'w_ffn_gate', 'w_ffn_up', 'w_ffn_down')


def _block_shape(shape, axis):
    return tuple(s // N_DEV if i == axis else s for i, s in enumerate(shape))


def _rows_of(shape):
    n = 1
    for s in shape:
        n *= s
    return n // LANES


REPL_ELEMS = sum(_rows_of((LANES,) + s) for _, s in REPLICATED)


def _cp(sem=None):
    return pltpu.CompilerParams(dimension_semantics=sem, vmem_limit_bytes=VMEM_LIMIT)


def _tile(dim, cap):
    best = None
    for t in range(128, min(dim, cap) + 1, 128):
        if dim % t == 0:
            best = t
    return best if best is not None else dim


def _two_terms(x):
    hi = x.astype(bf16)
    return hi, (x - hi.astype(f32)).astype(bf16)


def _dg(a, b, dims, exact):
    if exact == "split":
        (a_hi, a_lo), (b_hi, b_lo) = _two_terms(a), _two_terms(b)
        dot = functools.partial(lax.dot_general, dimension_numbers=dims, preferred_element_type=f32)
        return dot(a_hi, b_hi) + (dot(a_hi, b_lo) + dot(a_lo, b_hi))
    if exact:
        return lax.dot_general(a, b, dims, precision=_HI, preferred_element_type=f32)
    return lax.dot_general(a.astype(bf16), b.astype(bf16), dims, preferred_element_type=f32)


def _make_mm(batched, exact):
    o = 1 if batched else 0
    bd = ((0,), (0,)) if batched else ((), ())
    d_nn = (((1 + o,), (o,)), bd)
    d_nt = (((1 + o,), (1 + o,)), bd)
    d_tn = (((o,), (o,)), bd)

    @jax.custom_vjp
    def nn(a, b):
        return _dg(a, b, d_nn, exact)

    @jax.custom_vjp
    def nt(a, b):
        return _dg(a, b, d_nt, exact)

    @jax.custom_vjp
    def tn(a, b):
        return _dg(a, b, d_tn, exact)

    nn.defvjp(lambda a, b: (_dg(a, b, d_nn, exact), (a, b)),
              lambda res, g: (_dg(g, res[1], d_nt, exact), _dg(res[0], g, d_tn, exact)))
    nt.defvjp(lambda a, b: (_dg(a, b, d_nt, exact), (a, b)),
              lambda res, g: (_dg(g, res[1], d_nn, exact), _dg(g, res[0], d_tn, exact)))
    tn.defvjp(lambda a, b: (_dg(a, b, d_tn, exact), (a, b)),
              lambda res, g: (_dg(res[1], g, d_nt, exact), _dg(res[0], g, d_nn, exact)))
    return nn, nt, tn


def _sigmoid(x):
    return 1.0 / (1.0 + jnp.exp(-x))


def _head_sum_raw(x):
    width = 2 * HD
    i = lax.broadcasted_iota(jnp.int32, (width, width), 0) // HD
    j = lax.broadcasted_iota(jnp.int32, (width, width), 1) // HD
    m = (i == j).astype(bf16)
    dims = (((1,), (0,)), ((), ()))
    out = []
    for p in range(x.shape[1] // width):
        xp = x[:, p * width:(p + 1) * width]
        hi = xp.astype(bf16)
        lo = (xp - hi.astype(f32)).astype(bf16)
        out.append(lax.dot_general(hi, m, dims, preferred_element_type=f32)
                   + lax.dot_general(lo, m, dims, preferred_element_type=f32))
    return jnp.concatenate(out, axis=1)


@jax.custom_vjp
def _head_sum(x):
    return _head_sum_raw(x)


_head_sum.defvjp(lambda x: (_head_sum_raw(x), None), lambda _, g: (_head_sum_raw(g),))


WEIGHT_TILE_BYTES = 13 * 512 * 1024
ACC_TILE_BYTES = 8 * 1024 * 1024


def _lazy(fn, rows, width):
    return (fn, rows, width)


def _matmul(name, a, b, mode, add=None, out_dtype=f32):
    has_add = add is not None
    if isinstance(a, tuple):
        a_fn, a_rows, a_width = a
        a_arrays = [r for r, _ in a_rows]
        a_shape = (a_arrays[0].shape[0], a_width)
    else:
        a_fn, a_rows, a_arrays, a_shape = None, None, [a], a.shape
    n_a = len(a_arrays)

    def load_a(refs):
        if a_fn is None:
            return refs[0][...].astype(bf16)
        pieces = []
        for r, (_, widths) in zip(refs, a_rows):
            pieces += _pieces(r, widths)
        return a_fn(*pieces)[0].astype(bf16)

    if mode == "tn":
        (k, m), (_, n) = a_shape, b.shape
        tn = _tile(n, max(128, ACC_TILE_BYTES // (4 * m)))
        tk = _tile(k, 1024 if a_fn is None else 256)
        nk = k // tk

        def body(*refs):
            b_ref, o_ref, acc = refs[n_a:]

            @pl.when(pl.program_id(1) == 0)
            def _():
                acc[...] = jnp.zeros_like(acc)

            acc[...] += lax.dot_general(load_a(refs[:n_a]), b_ref[...].astype(bf16),
                                        (((0,), (0,)), ((), ())), preferred_element_type=f32)

            @pl.when(pl.program_id(1) == nk - 1)
            def _():
                o_ref[...] = acc[...].astype(o_ref.dtype)

        return pl.pallas_call(
            body, name=name, grid=(n // tn, nk),
            in_specs=[pl.BlockSpec((tk, r.shape[1]), lambda j, kk: (kk, 0)) for r in a_arrays]
            + [pl.BlockSpec((tk, tn), lambda j, kk: (kk, j))],
            out_specs=pl.BlockSpec((m, tn), lambda j, kk: (0, j)), out_shape=jax.ShapeDtypeStruct((m, n), out_dtype),
            scratch_shapes=[pltpu.VMEM((m, tn), f32)],
            compiler_params=_cp(("parallel", "arbitrary")),
        )(*a_arrays, b)

    (m, k) = a_shape
    n = b.shape[1] if mode == "nn" else b.shape[0]
    tm = _tile(m, 1024 if a_fn is None else 512)
    tn = _tile(n, max(128, WEIGHT_TILE_BYTES // (2 * k)))
    dims = (((1,), (0,)), ((), ())) if mode == "nn" else (((1,), (1,)), ((), ()))
    b_spec = pl.BlockSpec((k, tn), lambda j, i: (0, j)) if mode == "nn" else pl.BlockSpec((tn, k), lambda j, i: (j, 0))
    o_spec = pl.BlockSpec((tm, tn), lambda j, i: (i, j))

    def body(*refs):
        b_ref, o_ref = refs[n_a], refs[-1]
        r = lax.dot_general(load_a(refs[:n_a]), b_ref[...].astype(bf16), dims, preferred_element_type=f32)
        if has_add:
            r = r + refs[n_a + 1][...]
        o_ref[...] = r.astype(o_ref.dtype)

    return pl.pallas_call(
        body, name=name, grid=(n // tn, m // tm),
        in_specs=[pl.BlockSpec((tm, r.shape[1]), lambda j, i: (i, 0)) for r in a_arrays] + [b_spec]
        + ([o_spec] if has_add else []),
        out_specs=o_spec, out_shape=jax.ShapeDtypeStruct((m, n), out_dtype),
        compiler_params=_cp(("parallel", "arbitrary")),
    )(*a_arrays, b, *([add] if has_add else []))


def _input_cotangent(name, a_list, b_list, x, gain, add, side=None):
    m = a_list[0].shape[0]
    tm = _tile(m, 256)
    n_g = len(a_list)
    srcs, per_peer = side if side is not None else ([], False)
    n_s = len(srcs)

    def body(*refs):
        x_ref, g_ref, add_ref = refs[2 * n_g:2 * n_g + 3]
        src_refs = refs[2 * n_g + 3:2 * n_g + 3 + n_s]
        dx_ref, dg_ref = refs[2 * n_g + 3 + n_s:2 * n_g + 5 + n_s]
        _side_exchange(src_refs, refs[2 * n_g + 5 + n_s:2 * n_g + 5 + 2 * n_s], per_peer, refs[2 * n_g + 5 + 2 * n_s:], m // tm)
        d_u = None
        for g in range(n_g):
            r = lax.dot_general(refs[g][...].astype(bf16), refs[n_g + g][...].astype(bf16), (((1,), (0,)), ((), ())),
                                preferred_element_type=f32)
            d_u = r if d_u is None else d_u + r
        _, vjp = jax.vjp(_rms, x_ref[...], g_ref[...])
        d_x, d_gain = vjp(d_u)
        dx_ref[...] = d_x + add_ref[...]

        @pl.when(pl.program_id(0) == 0)
        def _():
            dg_ref[...] = jnp.zeros_like(dg_ref)

        dg_ref[...] += d_gain

    rows = pl.BlockSpec((tm, x.shape[1]), lambda i: (i, 0))
    whole = lambda b: pl.BlockSpec(b.shape, lambda i: (0, 0))
    res = pl.pallas_call(
        body, name=name, grid=(m // tm,),
        in_specs=[pl.BlockSpec((tm, a.shape[1]), lambda i: (i, 0)) for a in a_list] + [whole(b) for b in b_list]
        + [rows, whole(gain), rows] + [_HBM_SPEC] * n_s,
        out_specs=[rows, whole(gain)] + [_HBM_SPEC] * n_s,
        out_shape=[jax.ShapeDtypeStruct(x.shape, f32), jax.ShapeDtypeStruct(gain.shape, f32)] + _side_out_shapes(srcs, per_peer),
        scratch_shapes=_side_sems(n_s),
        compiler_params=_cp(("arbitrary",)),
    )(*a_list, *b_list, x, gain, add, *srcs)
    return res[0], res[1], list(res[2:])


def _pieces(ref, widths):
    out, off = [], 0
    for w in widths:
        out.append(ref[:, off:off + w].astype(f32))
        off += w
    return out


def _store_pieces(ref, widths, vals, add_ref=None):
    off = 0
    for w, v in zip(widths, vals):
        ref[:, off:off + w] = (v if add_ref is None else v + add_ref[:, off:off + w]).astype(ref.dtype)
        off += w


def _rows_fwd(name, fn, consts, rows, params, outs, n_sums=0, tm=512, dtypes=None):
    t = (consts + rows)[0][0].shape[0]
    tm = min(tm, t)
    ins = consts + rows
    n_in, n_p, n_o = len(ins), len(params), len(outs)
    dtypes = dtypes or [f32] * n_o

    def body(*refs):
        in_refs, p_refs = refs[:n_in], refs[n_in:n_in + n_p]
        o_refs, s_refs = refs[n_in + n_p:n_in + n_p + n_o], refs[n_in + n_p + n_o:]
        vals = []
        for r, (_, widths) in zip(in_refs, ins):
            vals += _pieces(r, widths)
        res = fn(*vals, *[p[...] for p in p_refs])
        pos = 0
        for r, widths in zip(o_refs, outs):
            _store_pieces(r, widths, res[pos:pos + len(widths)])
            pos += len(widths)

        @pl.when(pl.program_id(0) == 0)
        def _():
            for s in s_refs:
                s[...] = jnp.zeros_like(s)

        for s, v in zip(s_refs, res[pos:]):
            s[...] += v

    row_spec = lambda w: pl.BlockSpec((tm, w), lambda i: (i, 0))
    full = lambda p: pl.BlockSpec(p.shape, lambda i: (0,) * p.ndim)
    return pl.pallas_call(
        body, name=name, grid=(t // tm,),
        in_specs=[row_spec(a.shape[1]) for a, _ in ins] + [full(p) for p in params],
        out_specs=[row_spec(sum(w)) for w in outs] + [pl.BlockSpec((1, 1), lambda i: (0, 0))] * n_sums,
        out_shape=[jax.ShapeDtypeStruct((t, sum(w)), dt) for w, dt in zip(outs, dtypes)] + [jax.ShapeDtypeStruct((1, 1), f32)] * n_sums,
        compiler_params=_cp(("arbitrary",)),
    )(*[a for a, _ in ins], *params)


def _rows_bwd(name, fn, consts, rows, params, outs, cts, n_sums=0, add=None, tm=512, dtypes=None):
    t = (consts + rows)[0][0].shape[0]
    tm = min(tm, t)
    n_c, n_r, n_p, n_o = len(consts), len(rows), len(params), len(outs)
    has_add = add is not None
    dtypes = dtypes or [f32] * n_r

    def body(*refs):
        pos = 0
        c_refs = refs[pos:pos + n_c]; pos += n_c
        r_refs = refs[pos:pos + n_r]; pos += n_r
        p_refs = refs[pos:pos + n_p]; pos += n_p
        ct_refs = refs[pos:pos + n_o]; pos += n_o
        add_ref = refs[pos] if has_add else None
        pos += 1 if has_add else 0
        dr_refs = refs[pos:pos + n_r]; pos += n_r
        dp_refs = refs[pos:pos + n_p]; pos += n_p
        s_refs = refs[pos:pos + n_sums]
        cvals, rvals = [], []
        for r, (_, widths) in zip(c_refs, consts):
            cvals += _pieces(r, widths)
        for r, (_, widths) in zip(r_refs, rows):
            rvals += _pieces(r, widths)
        pvals = [p[...] for p in p_refs]
        ctv = []
        for r, widths in zip(ct_refs, outs):
            ctv += _pieces(r, widths)
        ctv += [jnp.ones((1, 1), f32)] * n_sums
        primal, vjp = jax.vjp(lambda *rp: tuple(fn(*cvals, *rp)), *rvals, *pvals)
        g = vjp(tuple(ctv))
        pos = 0
        for idx, (r, (_, widths)) in enumerate(zip(dr_refs, rows)):
            _store_pieces(r, widths, g[pos:pos + len(widths)], add_ref if idx == 0 else None)
            pos += len(widths)

        @pl.when(pl.program_id(0) == 0)
        def _():
            for acc in list(dp_refs) + list(s_refs):
                acc[...] = jnp.zeros_like(acc)

        for dp, v in zip(dp_refs, g[pos:]):
            dp[...] += v
        for s, v in zip(s_refs, primal[len(primal) - n_sums:]):
            s[...] += v

    row_spec = lambda w: pl.BlockSpec((tm, w), lambda i: (i, 0))
    full = lambda p: pl.BlockSpec(p.shape, lambda i: (0,) * p.ndim)
    args = [a for a, _ in consts + rows] + list(params) + list(cts) + ([add] if has_add else [])
    res = pl.pallas_call(
        body, name=name, grid=(t // tm,),
        in_specs=[row_spec(a.shape[1]) for a, _ in consts + rows] + [full(p) for p in params]
        + [row_spec(sum(w)) for w in outs] + ([row_spec(add.shape[1])] if has_add else []),
        out_specs=[row_spec(a.shape[1]) for a, _ in rows] + [full(p) for p in params]
        + [pl.BlockSpec((1, 1), lambda i: (0, 0))] * n_sums,
        out_shape=[jax.ShapeDtypeStruct(a.shape, dt) for (a, _), dt in zip(rows, dtypes)]
        + [jax.ShapeDtypeStruct(p.shape, f32) for p in params] + [jax.ShapeDtypeStruct((1, 1), f32)] * n_sums,
        compiler_params=_cp(("arbitrary",)),
    )(*args)
    return res[:n_r], res[n_r:n_r + n_p] + res[n_r + n_p:]


def _matmul_then_vjp(name, a, b, mode, fn, rows, dtypes, tm=256):
    m, k = a.shape
    tm = min(tm, m)
    dims = (((1,), (0,)), ((), ())) if mode == "nn" else (((1,), (1,)), ((), ()))
    n_r = len(rows)

    def body(*refs):
        a_ref, b_ref = refs[:2]
        ct = lax.dot_general(a_ref[...].astype(bf16), b_ref[...].astype(bf16), dims, preferred_element_type=f32)
        rvals = []
        for r, (_, widths) in zip(refs[2:2 + n_r], rows):
            rvals += _pieces(r, widths)
        _, vjp = jax.vjp(lambda *rp: fn(*rp)[0], *rvals)
        g = vjp(ct)
        pos = 0
        for r, (_, widths) in zip(refs[2 + n_r:], rows):
            _store_pieces(r, widths, g[pos:pos + len(widths)])
            pos += len(widths)

    row_spec = lambda w: pl.BlockSpec((tm, w), lambda i: (i, 0))
    return pl.pallas_call(
        body, name=name, grid=(m // tm,),
        in_specs=[row_spec(k), pl.BlockSpec(b.shape, lambda i: (0, 0))] + [row_spec(r.shape[1]) for r, _ in rows],
        out_specs=[row_spec(r.shape[1]) for r, _ in rows],
        out_shape=[jax.ShapeDtypeStruct(r.shape, dt) for (r, _), dt in zip(rows, dtypes)],
        compiler_params=_cp(("parallel",)),
    )(a, b, *[r for r, _ in rows])


def _rms(x, g):
    return x * lax.rsqrt(jnp.mean(x * x, axis=-1, keepdims=True) + NORM_EPS) * g


def _fn_rms(x, g):
    return (_rms(x, g),)


def _fn_rwkv_pre(r, k, v, wd, ad, gd, w0, w_up, a0, a_up, g_up, k_k, k_a):
    nn, _, _ = _make_mm(False, False)
    w_log = -_sigmoid(w0 + nn(jnp.tanh(wd), w_up)) * 0.6065306597126334
    a = _sigmoid(a0 + nn(ad, a_up))
    g = nn(_sigmoid(gd), g_up)
    kk = k * k_k
    kk = kk * lax.rsqrt(jnp.maximum(_head_sum(kk * kk), 1e-24))
    k2 = k * (1.0 + (a - 1.0) * k_a)
    return r, w_log, k2, v, -kk, kk * a, g


def _fn_rwkv_post(y, r, k2, v, g, gn_g, gn_b, r_k):
    mean = _head_sum(y) * (1.0 / HD)
    yc = y - mean
    var = _head_sum(yc * yc) * (1.0 / HD)
    yn = yc * lax.rsqrt(var + GN_EPS) * gn_g + gn_b
    bonus = _head_sum(r * k2 * r_k) * v
    return ((yn + bonus) * g,)


def _fn_merge(a_fox, a_rwkv, a_mem, g_fox, g_rwkv, g_mem):
    return (_sigmoid(g_fox) * a_fox + _sigmoid(g_rwkv) * a_rwkv + _sigmoid(g_mem) * a_mem,)


def _fn_post1(y, x, post1_g, pre2_g):
    h1 = x + _rms(y, post1_g)
    return h1, _rms(h1, pre2_g)


def _fn_swiglu(gp, up):
    return (gp * _sigmoid(gp) * up,)


def _fn_final(target, ffn, h1, post2_g):
    err = h1 + _rms(ffn, post2_g) - target
    per_row = jnp.mean(err * err, axis=-1, keepdims=True)
    return (0.5 * jnp.sum(per_row, axis=0, keepdims=True),)


def _shift_down(x):
    row = lax.broadcasted_iota(jnp.int32, x.shape, 0)
    return jnp.where(row == 0, 0.0, pltpu.roll(x, 1, 0))


def _shift_up(x):
    s = x.shape[0]
    row = lax.broadcasted_iota(jnp.int32, x.shape, 0)
    return jnp.where(row == s - 1, 0.0, pltpu.roll(x, s - 1, 0))


def _tokshift_fwd(p, mu, batch, seq):
    w = p.shape[1]
    tc = _tile(w, 384)

    def body(p_ref, mu_ref, o_ref):
        x = p_ref[...]
        o_ref[...] = x + (_shift_down(x) - x) * mu_ref[...]

    return pl.pallas_call(
        body, name="tokshift_fwd", grid=(w // tc, batch),
        in_specs=[pl.BlockSpec((seq, tc), lambda j, b: (b, j)), pl.BlockSpec((1, tc), lambda j, b: (0, j))],
        out_specs=pl.BlockSpec((seq, tc), lambda j, b: (b, j)),
        out_shape=jax.ShapeDtypeStruct(p.shape, f32),
        compiler_params=_cp(("parallel", "arbitrary")),
    )(p, mu)


def _tokshift_bwd(p, mu, dps, batch, seq):
    w = p.shape[1]
    tc = _tile(w, 384)

    def body(p_ref, mu_ref, d_ref, dp_ref, dmu_ref):
        x, mu_v, d = p_ref[...], mu_ref[...], d_ref[...]
        dp_ref[...] = (d * (1.0 - mu_v) + _shift_up(d * mu_v)).astype(dp_ref.dtype)

        @pl.when(pl.program_id(1) == 0)
        def _():
            dmu_ref[...] = jnp.zeros_like(dmu_ref)

        dmu_ref[...] += jnp.sum(d * (_shift_down(x) - x), axis=0, keepdims=True)

    return pl.pallas_call(
        body, name="tokshift_bwd", grid=(w // tc, batch),
        in_specs=[pl.BlockSpec((seq, tc), lambda j, b: (b, j)), pl.BlockSpec((1, tc), lambda j, b: (0, j)),
                  pl.BlockSpec((seq, tc), lambda j, b: (b, j))],
        out_specs=[pl.BlockSpec((seq, tc), lambda j, b: (b, j)), pl.BlockSpec((1, tc), lambda j, b: (0, j))],
        out_shape=[jax.ShapeDtypeStruct(p.shape, bf16), jax.ShapeDtypeStruct(mu.shape, f32)],
        compiler_params=_cp(("parallel", "arbitrary")),
    )(p, mu, dps)


def _cum_block(seq):
    return _tile(seq, 256)


def _fox_gate_fwd(f, bias, batch, seq):
    cb = _cum_block(seq)

    def body(f_ref, b_ref, c_ref):
        row = lax.broadcasted_iota(jnp.int32, (cb, cb), 0)
        col = lax.broadcasted_iota(jnp.int32, (cb, cb), 1)
        tri = (col <= row).astype(f32)
        carry = jnp.zeros((1, 128), f32)
        for i in range(seq // cb):
            z = f_ref[i * cb:(i + 1) * cb, :] + b_ref[...]
            ls = jnp.minimum(z, 0.0) - jnp.log(1.0 + jnp.exp(-jnp.abs(z)))
            c = _dg(tri, ls, (((1,), (0,)), ((), ())), True) + carry
            c_ref[i * cb:(i + 1) * cb, :] = c
            carry = c[cb - 1:cb, :]

    return pl.pallas_call(
        body, name="fox_gate_fwd", grid=(batch,),
        in_specs=[pl.BlockSpec((seq, 128), lambda b: (b, 0)), pl.BlockSpec((1, 128), lambda b: (0, 0))],
        out_specs=pl.BlockSpec((seq, 128), lambda b: (b, 0)),
        out_shape=jax.ShapeDtypeStruct(f.shape, f32),
        compiler_params=_cp(("arbitrary",)),
    )(f, bias)


def _fox_gate_bwd(f, bias, dc_a, dc_b, batch, seq):
    cb = _cum_block(seq)

    def body(f_ref, b_ref, da_ref, db_ref, df_ref, dbias_ref):
        row = lax.broadcasted_iota(jnp.int32, (cb, cb), 0)
        col = lax.broadcasted_iota(jnp.int32, (cb, cb), 1)
        triu = (col >= row).astype(f32)

        @pl.when(pl.program_id(0) == 0)
        def _():
            dbias_ref[...] = jnp.zeros_like(dbias_ref)

        lane = lax.broadcasted_iota(jnp.int32, (1, 128), 1)

        def by_head(blk):
            out = jnp.zeros((cb, 128), f32)
            for p in range(HEADS // 2):
                for e in range(2):
                    out = jnp.where(lane == 2 * p + e, _pick_lane(blk[:, p * 128:(p + 1) * 128], e), out)
            return out

        carry = jnp.zeros((1, 128), f32)
        tot = jnp.zeros((1, 128), f32)
        for i in reversed(range(seq // cb)):
            sl = slice(i * cb, (i + 1) * cb)
            dc = by_head(da_ref[sl, :] + db_ref[sl, :])
            dls = _dg(triu, dc, (((1,), (0,)), ((), ())), True) + carry
            carry = dls[0:1, :]
            df = dls * _sigmoid(-(f_ref[sl, :] + b_ref[...]))
            df_ref[sl, :] = df.astype(df_ref.dtype)
            tot = tot + jnp.sum(df, axis=0, keepdims=True)
        dbias_ref[...] += tot

    return pl.pallas_call(
        body, name="fox_gate_bwd", grid=(batch,),
        in_specs=[pl.BlockSpec((seq, 128), lambda b: (b, 0)), pl.BlockSpec((1, 128), lambda b: (0, 0)),
                  pl.BlockSpec((seq, HW), lambda b: (b, 0)), pl.BlockSpec((seq, HW), lambda b: (b, 0))],
        out_specs=[pl.BlockSpec((seq, 128), lambda b: (b, 0)), pl.BlockSpec((1, 128), lambda b: (0, 0))],
        out_shape=[jax.ShapeDtypeStruct(f.shape, bf16), jax.ShapeDtypeStruct((1, 128), f32)],
        compiler_params=_cp(("arbitrary",)),
    )(f, bias, dc_a, dc_b)


_HBM_SPEC = pl.BlockSpec(memory_space=pltpu.HBM)


def _side_out_shapes(srcs, per_peer):
    return [jax.ShapeDtypeStruct(((N_DEV,) + tuple(s.shape[1:] if per_peer else s.shape)), s.dtype) for s in srcs]


def _side_sems(n):
    if n == 0:
        return []
    return [pltpu.SemaphoreType.DMA((n, N_DEV - 1)), pltpu.SemaphoreType.DMA((n, N_DEV - 1)), pltpu.SemaphoreType.DMA((n,))]


def _peer_copies(src_refs, dst_refs, per_peer, sems):
    send_sems, recv_sems, local_sems = sems
    x, y, c = lax.axis_index("x"), lax.axis_index("y"), lax.axis_index("c")
    me = 4 * x + 2 * y + c

    def remote(src, dst, t, k, to):
        return pltpu.make_async_remote_copy(src_ref=src, dst_ref=dst, send_sem=send_sems.at[t, k - 1],
                                            recv_sem=recv_sems.at[t, k - 1], device_id=to,
                                            device_id_type=pl.DeviceIdType.MESH)

    direct, relays = [], []
    for t, (s, d) in enumerate(zip(src_refs, dst_refs)):
        direct.append((t, 0, pltpu.make_async_copy(s.at[me] if per_peer else s, d.at[me], local_sems.at[t])))
        for k in range(1, N_DEV):
            px = 1 - x if k & 4 else x
            py = 1 - y if k & 2 else y
            pc = 1 - c if k & 1 else c
            if per_peer:
                direct.append((t, k, remote(s.at[4 * px + 2 * py + pc], d.at[me], t, k, (px, py, pc))))
            elif k == 1 or not k & 1:
                direct.append((t, k, remote(s, d.at[me], t, k, (px, py, pc))))
            else:
                origin = d.at[4 * px + 2 * py + c]
                relays.append((t, k - 1, remote(origin, origin, t, k, (x, y, 1 - c))))
    return direct, relays


def _exchange_start(direct):
    for _, _, cp in direct:
        cp.start()


def _exchange_relay(direct, relays):
    landed = {(t, k): cp for t, k, cp in direct}
    for t, j, cp in relays:
        landed[(t, j)].wait_recv()
        cp.start()


def _exchange_finish(direct, relays):
    relayed = {(t, j) for t, j, _ in relays}
    for t, k, cp in direct:
        if k == 0:
            cp.wait()
        else:
            cp.wait_send()
            if (t, k) not in relayed:
                cp.wait_recv()
    for _, _, cp in relays:
        cp.wait()


def _side_exchange(src_refs, dst_refs, per_peer, sems, *grid):
    if not src_refs:
        return
    step, total = 0, 1
    for a, n in enumerate(grid):
        step, total = step * n + pl.program_id(a), total * n

    @pl.when(step == 0)
    def _():
        _exchange_start(_peer_copies(src_refs, dst_refs, per_peer, sems)[0])

    @pl.when(step == (3 * total) // 4)
    def _():
        _exchange_relay(*_peer_copies(src_refs, dst_refs, per_peer, sems))

    @pl.when(step == total - 1)
    def _():
        _exchange_finish(*_peer_copies(src_refs, dst_refs, per_peer, sems))


def _exchange(name, srcs, per_peer):
    n = len(srcs)

    def body(*refs):
        direct, relays = _peer_copies(refs[:n], refs[n:2 * n], per_peer, refs[2 * n:])
        _exchange_start(direct)
        _exchange_relay(direct, relays)
        _exchange_finish(direct, relays)

    return pl.pallas_call(
        body, name=name, in_specs=[_HBM_SPEC] * n, out_specs=[_HBM_SPEC] * n,
        out_shape=_side_out_shapes(srcs, per_peer), scratch_shapes=_side_sems(n),
    )(*srcs)


FOX_T = 512
_NEG = -1e30
_D2 = (((1,), (1,)), ((), ()))
_D1 = (((1,), (0,)), ((), ()))
_D0 = (((0,), (0,)), ((), ()))


def _bdot(a, b, dims):
    return lax.dot_general(a.astype(bf16), b.astype(bf16), dims, preferred_element_type=f32)


def _pick_lane(x, lane):
    idx = lax.broadcasted_iota(jnp.int32, x.shape, 1)
    return jnp.sum(jnp.where(idx == lane, x, 0.0), axis=1, keepdims=True)


def _pick_row(x, row):
    idx = lax.broadcasted_iota(jnp.int32, x.shape, 0)
    return jnp.sum(jnp.where(idx == row, x, 0.0), axis=0, keepdims=True)


def _fox_fwd(qkv, c, c_rows, batch, seq, side=None):
    t = min(FOX_T, seq)
    nq = seq // t
    scale = HD ** -0.5
    srcs, per_peer = side if side is not None else ([], False)
    n_s = len(srcs)

    def body(*refs):
        q_ref, k_ref, v_ref, cq_ref, ck_ref = refs[:5]
        o_ref, lse_ref = refs[5 + n_s:7 + n_s]
        _side_exchange(refs[5:5 + n_s], refs[7 + n_s:7 + 2 * n_s], per_peer, refs[7 + 2 * n_s:], batch, PAIRS, nq)
        pair, i = pl.program_id(1), pl.program_id(2)
        lane = lax.broadcasted_iota(jnp.int32, (1, PAIR_W), 1)
        first = (lane // HD) == 0
        mine = [first, jnp.logical_not(first)]
        q = q_ref[...] * scale
        qs = [jnp.where(mine[e], q, 0.0) for e in range(2)]
        cqs = [_pick_lane(cq_ref[...], 2 * pair + e) for e in range(2)]
        causal = lax.broadcasted_iota(jnp.int32, (t, t), 1) <= lax.broadcasted_iota(jnp.int32, (t, t), 0)

        def block(j, carry, diagonal):
            rows = pl.ds(pl.multiple_of(j * t, t), t)
            kj, vj = k_ref[rows, :], v_ref[rows, :]
            ck_blk = ck_ref[0, :, rows]
            out = []
            for e in range(2):
                m, acc = carry[2 * e:2 * e + 2]
                s = _bdot(qs[e], kj, _D2) + cqs[e] - _pick_row(ck_blk, 2 * pair + e)
                if diagonal:
                    s = jnp.where(causal, s, _NEG)
                m_new = jnp.maximum(m, jnp.max(s, axis=1, keepdims=True))
                p = jnp.exp(s - m_new)
                out += [m_new, jnp.exp(m - m_new) * acc + _bdot(p, jnp.where(mine[e], vj, 1.0), _D1)]
            return tuple(out)

        init = (jnp.full((t, 1), _NEG, f32), jnp.zeros((t, PAIR_W), f32)) * 2
        carry = lax.fori_loop(0, i, lambda j, cr: block(j, cr, False), init)
        m0, a0, m1, a1 = block(i, carry, True)
        l0, l1 = _pick_lane(a0, HD), _pick_lane(a1, 0)
        o_ref[...] = jnp.where(first, a0 / l0, a1 / l1)
        lse_ref[...] = jnp.where(lane == 0, m0 + jnp.log(l0), jnp.where(lane == 1, m1 + jnp.log(l1), 0.0))

    q_spec = pl.BlockSpec((t, PAIR_W), lambda b, p, i: (b * nq + i, p))
    res = pl.pallas_call(
        body, name="fox_attn_fwd", grid=(batch, PAIRS, nq),
        in_specs=[q_spec,
                  pl.BlockSpec((seq, PAIR_W), lambda b, p, i: (b, PAIRS + p)),
                  pl.BlockSpec((seq, PAIR_W), lambda b, p, i: (b, 2 * PAIRS + p)),
                  pl.BlockSpec((t, 128), lambda b, p, i: (b * nq + i, 0)),
                  pl.BlockSpec((1, 8, seq), lambda b, p, i: (b, 0, 0))] + [_HBM_SPEC] * n_s,
        out_specs=[q_spec, q_spec] + [_HBM_SPEC] * n_s,
        out_shape=[jax.ShapeDtypeStruct((batch * seq, HW), f32)] * 2 + _side_out_shapes(srcs, per_peer),
        scratch_shapes=_side_sems(n_s),
        compiler_params=_cp(("arbitrary", "arbitrary", "arbitrary")),
    )(qkv, qkv, qkv, c, c_rows, *srcs)
    return res[0], res[1], list(res[2:])


def _fox_bwd(qkv, c, c_rows, o, lse, do, batch, seq):
    t = min(FOX_T, seq)
    nq = seq // t
    scale = HD ** -0.5

    def body(q_ref, k_ref, v_ref, cq_ref, ck_ref, o_ref, lse_ref, do_ref,
             dq_ref, dk_ref, dv_ref, dcq_ref, dck_ref, acc0, acc1):
        pair, i = pl.program_id(1), pl.program_id(2)
        accs = [acc0, acc1]

        @pl.when(i == 0)
        def _():
            dv_ref[...] = jnp.zeros_like(dv_ref)
            acc0[...] = jnp.zeros_like(acc0)
            acc1[...] = jnp.zeros_like(acc1)

        lane = lax.broadcasted_iota(jnp.int32, (1, PAIR_W), 1)
        first = (lane // HD) == 0
        mine = [first, jnp.logical_not(first)]
        q, d_o, o_i = q_ref[...] * scale, do_ref[...], o_ref[...]
        q0s = [jnp.where(mine[e], q, 0.0) for e in range(2)]
        q1s = [jnp.where(mine[e], q, 1.0) for e in range(2)]
        dos = [jnp.where(mine[e], d_o, 0.0) for e in range(2)]
        deltas = [jnp.sum(dos[e] * o_i, axis=1, keepdims=True) for e in range(2)]
        lses = [_pick_lane(lse_ref[...], e) for e in range(2)]
        cqs = [_pick_lane(cq_ref[...], 2 * pair + e) for e in range(2)]
        causal = lax.broadcasted_iota(jnp.int32, (t, t), 1) <= lax.broadcasted_iota(jnp.int32, (t, t), 0)

        def block(j, dqs, diagonal):
            rows = pl.ds(pl.multiple_of(j * t, t), t)
            kj, vj = k_ref[rows, :], v_ref[rows, :]
            ck_blk = ck_ref[0, :, rows]
            out = []
            for e in range(2):
                s = _bdot(q0s[e], kj, _D2) + cqs[e] - _pick_row(ck_blk, 2 * pair + e)
                if diagonal:
                    s = jnp.where(causal, s, _NEG)
                p = jnp.exp(s - lses[e])
                ds = p * (_bdot(dos[e], vj, _D2) - deltas[e])
                dv_ref[rows, :] += _bdot(p, dos[e], _D0)
                accs[e][rows, :] += _bdot(ds, q1s[e], _D0)
                out.append(dqs[e] + _bdot(ds, jnp.where(mine[e], kj, 1.0), _D1))
            return tuple(out)

        zero = jnp.zeros((t, PAIR_W), f32)
        dqs = lax.fori_loop(0, i, lambda j, cr: block(j, cr, False), (zero, zero))
        dq0, dq1 = block(i, dqs, True)
        dq_ref[...] = jnp.where(first, dq0, dq1) * scale
        dcq_ref[...] = jnp.where(lane == 0, _pick_lane(dq0, HD), jnp.where(lane == 1, _pick_lane(dq1, 0), 0.0))

        @pl.when(i == nq - 1)
        def _():
            a0, a1 = acc0[...], acc1[...]
            dk_ref[...] = jnp.where(first, a0, a1)
            dck_ref[...] = jnp.where(lane == 0, -_pick_lane(a0, HD), jnp.where(lane == 1, -_pick_lane(a1, 0), 0.0))

    blk = lambda col: pl.BlockSpec((t, PAIR_W), lambda b, p, i: (b * nq + i, col * PAIRS + p))
    whole = lambda col: pl.BlockSpec((seq, PAIR_W), lambda b, p, i: (b, col * PAIRS + p))
    t_all = batch * seq
    return pl.pallas_call(
        body, name="fox_attn_bwd", grid=(batch, PAIRS, nq),
        in_specs=[blk(0), whole(1), whole(2),
                  pl.BlockSpec((t, 128), lambda b, p, i: (b * nq + i, 0)),
                  pl.BlockSpec((1, 8, seq), lambda b, p, i: (b, 0, 0)),
                  blk(0), blk(0), blk(0)],
        out_specs=[blk(0), whole(0), whole(0), blk(0), whole(0)],
        out_shape=[jax.ShapeDtypeStruct((t_all, HW), f32)] * 5,
        scratch_shapes=[pltpu.VMEM((seq, PAIR_W), f32), pltpu.VMEM((seq, PAIR_W), f32)],
        compiler_params=_cp(("parallel", "parallel", "arbitrary")),
    )(qkv, qkv, qkv, c, c_rows, o, lse, do)


MEM_TQ = 1024


def _mem_block(q, km, vm):
    nn, nt, _ = _make_mm(False, False)
    logits = nt(q, km) * (MEM_HD ** -0.5)
    m = lax.stop_gradient(jnp.max(logits, axis=-1, keepdims=True))
    e = jnp.exp(logits - m)
    return nn(e / jnp.sum(e, axis=-1, keepdims=True), vm)


def _mem_specs(seq, tq):
    nq = seq // tq
    qs = pl.BlockSpec((tq, MEM_HD), lambda b, h, i: (b * nq + i, h))
    ks = pl.BlockSpec((MEM_LEN, MEM_HD), lambda b, h, i: (b, h))
    vs = pl.BlockSpec((MEM_LEN, MEM_HD), lambda b, h, i: (b, MEM_HEADS + h))
    return nq, qs, ks, vs


def _mem_fwd(q, mem_kv, batch, seq):
    tq = min(MEM_TQ, seq)
    nq, qs, ks, vs = _mem_specs(seq, tq)

    def body(q_ref, k_ref, v_ref, o_ref):
        o_ref[...] = _mem_block(q_ref[...].astype(f32), k_ref[...], v_ref[...]).astype(o_ref.dtype)

    return pl.pallas_call(
        body, name="mem_attn_fwd", grid=(batch, MEM_HEADS, nq),
        in_specs=[qs, ks, vs], out_specs=qs, out_shape=jax.ShapeDtypeStruct(q.shape, bf16),
        compiler_params=_cp(("parallel", "parallel", "arbitrary")),
    )(q, mem_kv, mem_kv)


def _mem_bwd(q, mem_kv, do, batch, seq):
    tq = min(MEM_TQ, seq)
    nq, qs, ks, vs = _mem_specs(seq, tq)

    def body(q_ref, k_ref, v_ref, do_ref, dq_ref, dk_ref, dv_ref):
        _, vjp = jax.vjp(_mem_block, q_ref[...].astype(f32), k_ref[...], v_ref[...])
        dq, dk, dv = vjp(do_ref[...])
        dq_ref[...] = dq.astype(dq_ref.dtype)

        @pl.when(pl.program_id(2) == 0)
        def _():
            dk_ref[...] = jnp.zeros_like(dk_ref)
            dv_ref[...] = jnp.zeros_like(dv_ref)

        dk_ref[...] += dk
        dv_ref[...] += dv

    return pl.pallas_call(
        body, name="mem_attn_bwd", grid=(batch, MEM_HEADS, nq),
        in_specs=[qs, ks, vs, qs], out_specs=[qs, ks, ks],
        out_shape=[jax.ShapeDtypeStruct(q.shape, bf16), jax.ShapeDtypeStruct((batch * MEM_LEN, MEM_W), f32),
                   jax.ShapeDtypeStruct((batch * MEM_LEN, MEM_W), f32)],
        compiler_params=_cp(("parallel", "parallel", "arbitrary")),
    )(q, mem_kv, mem_kv, do)


@jax.custom_vjp
def _halves(x):
    c = x.shape[1] // 2
    return x[:, :c], x[:, c:]


_halves.defvjp(lambda x: ((x[:, :x.shape[1] // 2], x[:, x.shape[1] // 2:]), None),
               lambda _, g: (jnp.concatenate(g, axis=1),))


@jax.custom_vjp
def _lead_halves(x):
    n = x.shape[0] // 2
    return x[:n], x[n:]


_lead_halves.defvjp(lambda x: ((x[:x.shape[0] // 2], x[x.shape[0] // 2:]), None),
                    lambda _, g: (jnp.concatenate(g, axis=0),))


def _scan_chunk(s0, r, wl, k, v, a, b):
    nn, nt, tn = _make_mm(True, False)
    nn_exact, _, _ = _make_mm(True, True)
    _, nt_exact, _ = _make_mm(True, "split")
    hp, c, lanes = r.shape
    row = lax.broadcasted_iota(jnp.int32, (c, c), 0)
    col = lax.broadcasted_iota(jnp.int32, (c, c), 1)
    first = (lax.broadcasted_iota(jnp.int32, (1, 1, lanes), 2) // HD) == 0
    tri = jnp.broadcast_to((col <= row).astype(f32)[None], (hp, c, c))
    lg = nn_exact(tri, wl)
    lg_end = lg[:, c - 1:c, :]
    grow, shrink, to_end = jnp.exp(lg), jnp.exp(-lg), jnp.exp(lg_end - lg)
    rt, kt, bt, at = r * grow, k * shrink, b * shrink, a * jnp.exp(lg - wl)
    strict, incl = (col < row)[None], (col <= row)[None]
    twice = lambda t: jnp.concatenate([t, t], axis=0)
    queries = jnp.concatenate([at, rt], axis=1)
    per_head = jnp.concatenate([jnp.where(first, queries, 0.0), jnp.where(first, 0.0, queries)], axis=0)
    (ab, rb), (ak, rk) = _halves(nt_exact(per_head, twice(bt))), _halves(nt_exact(per_head, twice(kt)))
    l_ab = jnp.where(strict, ab, 0.0)
    a_ak = jnp.where(strict, ak, 0.0)
    a_rb = jnp.where(incl, rb, 0.0)
    a_rk = jnp.where(incl, rk, 0.0)
    inv = (col == row).astype(f32)[None] + l_ab
    power, n = l_ab, 1
    while 2 * n < c:
        power = nn(power, power)
        inv = inv + nn(inv, power)
        n *= 2

    def apply(m, t):
        lo, hi = _lead_halves(nn(m, twice(t)))
        return jnp.where(first, lo, hi)

    sa = apply(inv, nt(at, s0) + apply(a_ak, v))
    y = nt(rt, s0) + apply(a_rk, v) + apply(a_rb, sa)
    same_head = ((lax.broadcasted_iota(jnp.int32, (lanes, lanes), 0) // HD)
                 == (lax.broadcasted_iota(jnp.int32, (lanes, lanes), 1) // HD))[None]
    s1 = s0 * jnp.exp(lg_end) + jnp.where(same_head, tn(v, k * to_end) + tn(sa, b * to_end), 0.0)
    return y, s1


PAIRS = HEADS // 2
PAIR_W = 2 * HD


def _pair_stack(ref, off):
    return jnp.stack([ref[b, :, off + p * PAIR_W:off + (p + 1) * PAIR_W]
                      for b in range(ref.shape[0]) for p in range(PAIRS)])


def _pair_store(ref, off, val, add_ref=None):
    for b in range(ref.shape[0]):
        for p in range(PAIRS):
            sl = slice(off + p * PAIR_W, off + (p + 1) * PAIR_W)
            v = val[b * PAIRS + p]
            ref[b, :, sl] = v if add_ref is None else v + add_ref[b, :, sl]


def _scan_fwd(main6, batch, seq, side=None):
    c = min(SCAN_CHUNK, seq)
    nc = seq // c
    hp = batch * PAIRS
    srcs, per_peer = side if side is not None else ([], False)
    n_s = len(srcs)

    def body(*refs):
        z_ref, y_ref, s_ref, st = refs[0], refs[1 + n_s], refs[2 + n_s], refs[3 + 2 * n_s]
        _side_exchange(refs[1:1 + n_s], refs[3 + n_s:3 + 2 * n_s], per_peer, refs[4 + 2 * n_s:], nc)

        @pl.when(pl.program_id(0) == 0)
        def _():
            st[...] = jnp.zeros_like(st)

        s0 = st[...]
        s_ref[0] = s0
        y, s1 = _scan_chunk(s0, *[_pair_stack(z_ref, comp * HW) for comp in range(6)])
        _pair_store(y_ref, 0, y)
        st[...] = s1

    res = pl.pallas_call(
        body, name="rwkv_scan_fwd", grid=(nc,),
        in_specs=[pl.BlockSpec((batch, c, 6 * HW), lambda i: (0, i, 0))] + [_HBM_SPEC] * n_s,
        out_specs=[pl.BlockSpec((batch, c, HW), lambda i: (0, i, 0)),
                   pl.BlockSpec((1, hp, PAIR_W, PAIR_W), lambda i: (i, 0, 0, 0))] + [_HBM_SPEC] * n_s,
        out_shape=[jax.ShapeDtypeStruct((batch, seq, HW), f32), jax.ShapeDtypeStruct((nc, hp, PAIR_W, PAIR_W), f32)]
        + _side_out_shapes(srcs, per_peer),
        scratch_shapes=[pltpu.VMEM((hp, PAIR_W, PAIR_W), f32)] + _side_sems(n_s),
        compiler_params=_cp(("arbitrary",)),
    )(main6.reshape(batch, seq, 6 * HW), *srcs)
    return res[0].reshape(batch * seq, HW), res[1], list(res[2:])


def _scan_bwd(main6, states, dy, extra, batch, seq, side=None):
    c = min(SCAN_CHUNK, seq)
    nc = seq // c
    hp = batch * PAIRS
    srcs, per_peer = side if side is not None else ([], False)
    n_s = len(srcs)

    def body(*refs):
        z_ref, s_ref, dy_ref, ex_ref = refs[:4]
        dz_ref, dst = refs[4 + n_s], refs[5 + 2 * n_s]
        _side_exchange(refs[4:4 + n_s], refs[5 + n_s:5 + 2 * n_s], per_peer, refs[6 + 2 * n_s:], nc)

        @pl.when(pl.program_id(0) == 0)
        def _():
            dst[...] = jnp.zeros_like(dst)

        _, vjp = jax.vjp(_scan_chunk, s_ref[0], *[_pair_stack(z_ref, comp * HW) for comp in range(6)])
        g = vjp((_pair_stack(dy_ref, 0), dst[...]))
        dst[...] = g[0]
        for comp in range(6):
            _pair_store(dz_ref, comp * HW, g[1 + comp], ex_ref)

    back = lambda i: (0, nc - 1 - i, 0)
    wide = pl.BlockSpec((batch, c, 6 * HW), back)
    res = pl.pallas_call(
        body, name="rwkv_scan_bwd", grid=(nc,),
        in_specs=[wide, pl.BlockSpec((1, hp, PAIR_W, PAIR_W), lambda i: (nc - 1 - i, 0, 0, 0)),
                  pl.BlockSpec((batch, c, HW), back), wide] + [_HBM_SPEC] * n_s,
        out_specs=[wide] + [_HBM_SPEC] * n_s,
        out_shape=[jax.ShapeDtypeStruct((batch, seq, 6 * HW), f32)] + _side_out_shapes(srcs, per_peer),
        scratch_shapes=[pltpu.VMEM((hp, PAIR_W, PAIR_W), f32)] + _side_sems(n_s),
        compiler_params=_cp(("arbitrary",)),
    )(main6.reshape(batch, seq, 6 * HW), states, dy.reshape(batch, seq, HW), extra.reshape(batch, seq, 6 * HW), *srcs)
    return res[0].reshape(batch * seq, 6 * HW), list(res[1:])


def _pad_cols(x, width):
    return jnp.pad(x, ((0, 0), (0, width - x.shape[1])))


def _split_w_in(wt):
    z = lambda rows: jnp.zeros((rows, wt.shape[1]), wt.dtype)
    w_r = jnp.concatenate([wt[1544:3080], wt[3080:3144], z(64), wt[3144:3208], z(64), wt[3208:3336]], axis=0)
    return wt[:1536], jnp.concatenate([wt[1536:1544], z(120)], axis=0), w_r, wt[3336:3848], wt[3848:]


def _merge_w_in(g_qkv, g_f, g_r, g_mq, g_g):
    return jnp.concatenate([g_qkv, g_f[:8], g_r[:1536], g_r[1536:1600], g_r[1664:1728], g_r[1792:], g_mq, g_g], axis=0)


def _pad_lora(v):
    z64 = jnp.zeros((1, 64), v.dtype)
    return jnp.concatenate([v[:, :1536], v[:, 1536:1600], z64, v[:, 1600:1664], z64, v[:, 1664:]], axis=1)


def _unpad_lora(v):
    return jnp.concatenate([v[:, :1536], v[:, 1536:1600], v[:, 1664:1728], v[:, 1792:]], axis=1)


def _local_step(x, mem, target, w, p, late=None, early=None, last=None):
    batch, seq, _ = x.shape
    t = batch * seq
    x2, tg2, mem2 = x.reshape(t, D), target.reshape(t, D), mem.reshape(batch * MEM_LEN, D)
    w_qkv, w_f, w_r, w_mq, w_g3 = _split_w_in(w["w_in"])
    mu = _pad_lora(p["rwkv_mu"])
    bias = _pad_cols(p["fox_f_bias"], 128)
    r_k = p["rwkv_r_k"].reshape(1, HW)
    post_params = [p["rwkv_gn_g"], p["rwkv_gn_b"], r_k]
    rw_widths = [HW, HW, HW, LORA_PAD, LORA_PAD, LORA_PAD]
    six = [HW] * 6

    (u,) = _rows_fwd("rms_pre1", _fn_rms, [], [(x2, [D])], [p["pre1_g"]], [[D]], dtypes=[bf16])
    p_qkv = _matmul("proj_qkv", u, w_qkv, "nt", out_dtype=bf16)
    p_f = _matmul("proj_f", u, w_f, "nt")
    p_r = _matmul("proj_rwkv", u, w_r, "nt")
    p_mq = _matmul("proj_memq", u, w_mq, "nt", out_dtype=bf16)
    p_g = _matmul("proj_gate", u, w_g3, "nt", out_dtype=bf16)

    c = _fox_gate_fwd(p_f, bias, batch, seq)
    c_rows = c[:, :HEADS].reshape(batch, seq, HEADS).transpose(0, 2, 1)
    fox_o, lse, gathered = _fox_fwd(p_qkv, c, c_rows, batch, seq, side=(late[0], False) if late else None)
    if late:
        w = {**w, **late[2](gathered, 0)}
    fox_out = fox_o.astype(bf16)

    w_up = jnp.pad(w["rwkv_w_up"].astype(f32), ((0, LORA_PAD - 64), (0, 0)))
    a_up = jnp.pad(w["rwkv_a_up"].astype(f32), ((0, LORA_PAD - 64), (0, 0)))
    pre_params = [p["rwkv_w0"], w_up, p["rwkv_a0"], a_up, w["rwkv_g_up"].astype(f32), p["rwkv_k_k"], p["rwkv_k_a"]]
    ps = _tokshift_fwd(p_r, mu, batch, seq)
    main6, g_rw = _rows_fwd("rwkv_pre", _fn_rwkv_pre, [], [(ps, rw_widths)], pre_params, [six, [HW]], tm=256)
    y_rw, states, gathered = _scan_fwd(main6, batch, seq, side=(late[1], False) if late else None)
    if late:
        w = {**w, **late[2](gathered, 1)}
    post_consts = []
    post_rows = [(y_rw, [HW]), (main6, six), (g_rw, [HW])]

    def fn_post(y, r, _wl, k2, v, _a, _b, g, gn_g, gn_b, rk):
        return _fn_rwkv_post(y, r, k2, v, g, gn_g, gn_b, rk)

    (rwkv_out,) = _rows_fwd("rwkv_post", fn_post, post_consts, post_rows, post_params, [[HW]], dtypes=[bf16], tm=256)

    (memn,) = _rows_fwd("rms_mem", _fn_rms, [], [(mem2, [D])], [p["mem_norm_g"]], [[D]], dtypes=[bf16])
    mem_kv = _matmul("proj_memkv", memn, w["w_mem_kv"], "nn")
    mem_out = _mem_fwd(p_mq, mem_kv, batch, seq)

    a_fox = _matmul("out_fox", fox_out, w["w_fox_out"], "nn", out_dtype=bf16)
    a_rwkv = _matmul("out_rwkv", rwkv_out, w["w_rwkv_out"], "nn", out_dtype=bf16)
    a_mem = _matmul("out_mem", mem_out, w["w_mem_out"], "nn", out_dtype=bf16)
    merge_rows = [(a_fox, [D]), (a_rwkv, [D]), (a_mem, [D]), (p_g, [D, D, D])]
    merged = _lazy(_fn_merge, merge_rows, D)
    yy = _matmul("out_o", merged, w["w_o"], "nn")
    post1_rows = [(yy, [D]), (x2, [D])]
    post1_params = [p["post1_g"], p["pre2_g"]]
    h1, u2 = _rows_fwd("post1", _fn_post1, [], post1_rows, post1_params, [[D], [D]], dtypes=[f32, bf16])
    gp = _matmul("ffn_gate", u2, w["w_ffn_gate"], "nt", out_dtype=bf16)
    up = _matmul("ffn_up", u2, w["w_ffn_up"], "nt", out_dtype=bf16)
    hmid = _lazy(_fn_swiglu, [(gp, [D_FF]), (up, [D_FF])], D_FF)
    ffn = _matmul("ffn_down", hmid, w["w_ffn_down"], "nn")
    final_rows = [(ffn, [D]), (h1, [D])]

    gw, gp_ = {}, {}
    (d_ffn, d_h1), (gp_["post2_g"], loss) = _rows_bwd("final", _fn_final, [(tg2, [D])], final_rows, [p["post2_g"]], [], [],
                                                      n_sums=1, dtypes=[bf16, f32])
    gw["w_ffn_down"] = _matmul("ffn_down_dw", hmid, d_ffn, "tn", out_dtype=bf16)
    d_gp, d_up = _matmul_then_vjp("ffn_down_dx", d_ffn, w["w_ffn_down"], "nt", _fn_swiglu,
                                  [(gp, [D_FF]), (up, [D_FF])], [bf16, bf16])
    d_u2 = _matmul("ffn_gate_dx", d_gp, w["w_ffn_gate"], "nn")
    d_u2 = _matmul("ffn_up_dx", d_up, w["w_ffn_up"], "nn", add=d_u2)
    gw["w_ffn_gate"] = _matmul("ffn_gate_dw", d_gp, u2, "tn", out_dtype=bf16)
    gw["w_ffn_up"] = _matmul("ffn_up_dw", d_up, u2, "tn", out_dtype=bf16)
    (d_yy, d_x_res), (gp_["post1_g"], gp_["pre2_g"]) = _rows_bwd(
        "post1_bwd", _fn_post1, [], post1_rows, post1_params, [[D], [D]], [d_h1, d_u2], dtypes=[bf16, f32])
    gw["w_o"] = _matmul("out_o_dw", merged, d_yy, "tn", out_dtype=bf16)
    d_a_fox, d_a_rwkv, d_a_mem, d_p_g = _matmul_then_vjp("out_o_dx", d_yy, w["w_o"], "nt", _fn_merge, merge_rows, [bf16] * 4)
    d_fox_out = _matmul("out_fox_dx", d_a_fox, w["w_fox_out"], "nt")
    gw["w_fox_out"] = _matmul("out_fox_dw", fox_out, d_a_fox, "tn", out_dtype=bf16)
    d_rwkv_out = _matmul("out_rwkv_dx", d_a_rwkv, w["w_rwkv_out"], "nt")
    gw["w_rwkv_out"] = _matmul("out_rwkv_dw", rwkv_out, d_a_rwkv, "tn", out_dtype=bf16)
    d_mem_out = _matmul("out_mem_dx", d_a_mem, w["w_mem_out"], "nt")
    gw["w_mem_out"] = _matmul("out_mem_dw", mem_out, d_a_mem, "tn", out_dtype=bf16)

    d_p_mq, d_km, d_vm = _mem_bwd(p_mq, mem_kv, d_mem_out, batch, seq)
    d_mem_kv = jnp.concatenate([d_km, d_vm], axis=1).astype(bf16)
    gw["w_mem_kv"] = _matmul("proj_memkv_dw", memn, d_mem_kv, "tn", out_dtype=bf16)
    d_memn = _matmul("proj_memkv_dx", d_mem_kv, w["w_mem_kv"], "nt")
    _, (gp_["mem_norm_g"],) = _rows_bwd("rms_mem_bwd", _fn_rms, [], [(mem2, [D])], [p["mem_norm_g"]], [[D]], [d_memn])

    d_q, d_k, d_v, d_cq, d_ck = _fox_bwd(p_qkv, c, c_rows, fox_o, lse, d_fox_out, batch, seq)
    d_p_qkv = jnp.concatenate([d_q, d_k, d_v], axis=1).astype(bf16)
    d_p_f, d_bias = _fox_gate_bwd(p_f, bias, d_cq, d_ck, batch, seq)
    gp_["fox_f_bias"] = d_bias[:, :HEADS]

    (d_y_rw, d_main6_post, d_g_rw), (gp_["rwkv_gn_g"], gp_["rwkv_gn_b"], d_rk) = _rows_bwd(
        "rwkv_post_bwd", fn_post, post_consts, post_rows, post_params, [[HW]], [d_rwkv_out], tm=256)
    gp_["rwkv_r_k"] = d_rk.reshape(1, HEADS, HD)
    d_main6, early_got = _scan_bwd(main6, states, d_y_rw, d_main6_post, batch, seq,
                                   side=(early(gw), True) if early else None)

    def fn_pre_sum(*args):
        return _fn_rwkv_pre(*args)

    (d_ps,), d_pre = _rows_bwd("rwkv_pre_bwd", fn_pre_sum, [], [(ps, rw_widths)], pre_params, [six, [HW]],
                               [d_main6, d_g_rw], tm=256)
    gp_["rwkv_w0"], d_w_up, gp_["rwkv_a0"], d_a_up, gw["rwkv_g_up"], gp_["rwkv_k_k"], gp_["rwkv_k_a"] = d_pre
    gw["rwkv_w_up"], gw["rwkv_a_up"] = d_w_up[:64], d_a_up[:64]
    d_p_r, d_mu = _tokshift_bwd(p_r, mu, d_ps, batch, seq)
    gp_["rwkv_mu"] = _unpad_lora(d_mu)

    gw["w_in"] = _merge_w_in(_matmul("proj_qkv_dw", d_p_qkv, u, "tn", out_dtype=bf16), _matmul("proj_f_dw", d_p_f, u, "tn", out_dtype=bf16),
                             _matmul("proj_rwkv_dw", d_p_r, u, "tn", out_dtype=bf16), _matmul("proj_memq_dw", d_p_mq, u, "tn", out_dtype=bf16),
                             _matmul("proj_gate_dw", d_p_g, u, "tn", out_dtype=bf16))
    d_x, gp_["pre1_g"], last_got = _input_cotangent(
        "proj_dx", [d_p_qkv, d_p_f, d_p_r, d_p_mq, d_p_g], [w_qkv, w_f, w_r, w_mq, w_g3], x2, p["pre1_g"], d_x_res,
        side=(last(gw), True) if last else None)
    return loss, d_x.reshape(x.shape), gw, gp_, early_got, last_got


def _adamw(name, recv, row_off, w, m, v):
    _, rows, cols = w.shape
    row_tiles = [t for t in range(16, min(rows, 128) + 1, 16) if rows % t == 0 and row_off % t == 0]
    if row_tiles:
        tr, tc = max(row_tiles), cols
        first, grid = row_off // tr, (rows // tr,)
        at = lambda i: (0, first + i, 0)
        mine = lambda i: (0, i, 0)
    else:
        assert row_off == 0 and recv.shape[1] == rows
        tr, tc = rows, 128
        grid = (cols // tc,)
        at = mine = lambda i: (0, 0, i)

    def body(g_ref, w_ref, m_ref, v_ref, go_ref, d_ref, mo_ref, vo_ref):
        g = g_ref[0].astype(f32)
        for s in range(1, N_DEV):
            g = g + g_ref[s].astype(f32)
        m_new = ADAM_B1 * m_ref[0] + (1.0 - ADAM_B1) * g
        v_new = ADAM_B2 * v_ref[0] + (1.0 - ADAM_B2) * (g * g)
        m_hat = m_new / (1.0 - ADAM_B1 ** ADAM_STEP)
        v_hat = v_new / (1.0 - ADAM_B2 ** ADAM_STEP)
        go_ref[0] = g
        d_ref[0] = -ADAM_LR * (m_hat / (jnp.sqrt(v_hat) + ADAM_EPS) + ADAM_WD * w_ref[0])
        mo_ref[0] = m_new
        vo_ref[0] = v_new

    spec = pl.BlockSpec((1, tr, tc), mine)
    return pl.pallas_call(
        body, name=name, grid=grid,
        in_specs=[pl.BlockSpec((N_DEV, tr, tc), at), spec, spec, spec],
        out_specs=[spec] * 4, out_shape=[jax.ShapeDtypeStruct(w.shape, f32)] * 4,
        compiler_params=_cp(("parallel",)),
    )(recv, w, m, v)


GROUPS = (
    ("in", ("w_in",), 0),
    ("memkv", ("w_mem_kv",), 0),
    ("ffn_gu", ("w_ffn_gate", "w_ffn_up"), 0),
    ("down_o", ("w_ffn_down", "w_o"), 0),
    ("outs", ("w_fox_out", "w_rwkv_out", "w_mem_out"), 0),
    ("lora", ("rwkv_w_up", "rwkv_a_up", "rwkv_g_up"), 0),
)
FIRST_GROUPS = ("in", "memkv")
LATE_GROUPS = (("down_o", "outs", "lora"), ("ffn_gu",))
EARLY_GRAD_GROUPS = ("memkv", "ffn_gu", "down_o", "outs")
LAST_GRAD_GROUPS = ("in", "lora")
SHARD_AXIS = {n: a for n, _, a in SHARDED}
SMALL_ROWS = 16


def _group_local(shards, members, join):
    parts = [shards[n].reshape(shards[n].shape[-2:]) for n in members]
    return parts[0] if len(parts) == 1 else jnp.concatenate(parts, axis=join)


def _group_split(arr, members, join, lead=False):
    out, off = {}, 0
    for n in members:
        shape = dict((k, s) for k, s, _ in SHARDED)[n]
        size = _block_shape(shape, SHARD_AXIS[n])[join]
        idx = [slice(None)] * arr.ndim
        idx[arr.ndim - 2 + join] = slice(off, off + size)
        out[n] = arr[tuple(idx)]
        off += size
    return out


def _full_from_blocks(blocks, axis):
    if axis == 0:
        return blocks.reshape(-1, blocks.shape[2])
    return blocks.transpose(1, 0, 2).reshape(blocks.shape[1], -1)


def _blocks_from_full(full, axis):
    if axis == 0:
        return full.reshape(N_DEV, -1, full.shape[1])
    return full.reshape(full.shape[0], N_DEV, -1).transpose(1, 0, 2)


def _assemble(gathered, names):
    out = {}
    for arr, g in zip(gathered, names):
        _, members, join = [grp for grp in GROUPS if grp[0] == g][0]
        for n, blk in _group_split(arr, members, join, lead=True).items():
            out[n] = _full_from_blocks(blk, SHARD_AXIS[n])
    return out


def _grad_blocks(gw, names):
    out = []
    for g in names:
        _, members, join = [grp for grp in GROUPS if grp[0] == g][0]
        parts = [_blocks_from_full(gw[n].astype(bf16), SHARD_AXIS[n]) for n in members]
        out.append(parts[0] if len(parts) == 1 else jnp.concatenate(parts, axis=1 + join))
    return out


def _small_pack(d):
    flat = jnp.concatenate([d[n].reshape(-1) for n, _ in REPLICATED])
    return jnp.pad(flat, (0, SMALL_ROWS * LANES - REPL_ELEMS)).reshape(SMALL_ROWS, LANES)


def _small_unpack(packed):
    out, flat, off = {}, packed.reshape(-1), 0
    for n, shape in REPLICATED:
        k = _rows_of((LANES,) + shape)
        out[n] = flat[off:off + k].reshape(shape)
        off += k
    return out


def kernel(x, mem, pre1_g, post1_g, pre2_g, post2_g, mem_norm_g, w_in, fox_f_bias, rwkv_mu, rwkv_w0, rwkv_w_up, rwkv_a0, rwkv_a_up, rwkv_g_up, rwkv_k_k, rwkv_k_a, rwkv_r_k, rwkv_gn_g, rwkv_gn_b, w_mem_kv, w_fox_out, w_rwkv_out, w_mem_out, w_o, w_ffn_gate, w_ffn_up, w_ffn_down, loss_target, m_pre1_g, m_post1_g, m_pre2_g, m_post2_g, m_mem_norm_g, m_w_in, m_fox_f_bias, m_rwkv_mu, m_rwkv_w0, m_rwkv_w_up, m_rwkv_a0, m_rwkv_a_up, m_rwkv_g_up, m_rwkv_k_k, m_rwkv_k_a, m_rwkv_r_k, m_rwkv_gn_g, m_rwkv_gn_b, m_w_mem_kv, m_w_fox_out, m_w_rwkv_out, m_w_mem_out, m_w_o, m_w_ffn_gate, m_w_ffn_up, m_w_ffn_down, v_pre1_g, v_post1_g, v_pre2_g, v_post2_g, v_mem_norm_g, v_w_in, v_fox_f_bias, v_rwkv_mu, v_rwkv_w0, v_rwkv_w_up, v_rwkv_a0, v_rwkv_a_up, v_rwkv_g_up, v_rwkv_k_k, v_rwkv_k_a, v_rwkv_r_k, v_rwkv_gn_g, v_rwkv_gn_b, v_w_mem_kv, v_w_fox_out, v_w_rwkv_out, v_w_mem_out, v_w_o, v_w_ffn_gate, v_w_ffn_up, v_w_ffn_down):
    args = dict(locals())
    turn = lambda n, a: jnp.swapaxes(a, 1, 2) if n in TRANSPOSED else a
    wts = {n: turn(n, args[n]) for n in WEIGHT_ORDER}
    ms = {n: turn(n, args["m_" + n]) for n in WEIGHT_ORDER}
    vs = {n: turn(n, args["v_" + n]) for n in WEIGHT_ORDER}

    groups = {g: (members, join) for g, members, join in GROUPS}
    w_bf16 = {n: wts[n].astype(bf16) for n, _, _ in SHARDED}

    def send(g):
        return _group_local(w_bf16, *groups[g])

    first = _exchange("gather_first", [send(g) for g in FIRST_GROUPS], per_peer=False)
    full = _assemble(first, FIRST_GROUPS)
    small_in = {n: (wts[n] if n == "rwkv_r_k" else wts[n].reshape(wts[n].shape[-2:])) for n, _ in REPLICATED}
    late = ([send(g) for g in LATE_GROUPS[0]], [send(g) for g in LATE_GROUPS[1]],
            lambda got, which: _assemble(got, LATE_GROUPS[which]))
    loss_part, grad_x, gw, gp, early_got, last_got = _local_step(
        x, mem, loss_target, full, small_in, late=late, early=lambda g: _grad_blocks(g, EARLY_GRAD_GROUPS),
        last=lambda g: _grad_blocks(g, LAST_GRAD_GROUPS))
    (small_got,) = _exchange("exchange_small", [_small_pack(gp).astype(bf16)], per_peer=False)
    received = dict(zip(EARLY_GRAD_GROUPS + LAST_GRAD_GROUPS, list(early_got) + list(last_got)))

    outs = [{}, {}, {}, {}]
    for g, members, _ in GROUPS:
        off = 0
        for n in members:
            for o, arr in zip(outs, _adamw("adamw_" + n, received[g], off, wts[n], ms[n], vs[n])):
                o[n] = arr
            off += wts[n].shape[1]
    res = _adamw("adamw_small", small_got, 0, *[_small_pack(d)[None] for d in (wts, ms, vs)])
    for o, arr in zip(outs, res):
        o.update(_small_unpack(arr))
    loss = lax.psum(loss_part[0, 0], ("x", "y", "c"))
    return (loss, grad_x, *[turn(n, o[n].reshape(wts[n].shape)) for o in outs for n in WEIGHT_ORDER])
```

```python
import functools

import jax
import jax.numpy as jnp
from jax import lax
from jax.experimental import pallas as pl
from jax.experimental.pallas import tpu as pltpu

f32 = jnp.float32
bf16 = jnp.bfloat16
_HI = lax.Precision.HIGHEST

D = 1024
HEADS = 8
HD = 64
HW = HEADS * HD
MEM_HEADS = 4
MEM_HD = 128
MEM_W = 512
MEM_LEN = 256
D_FF = 2816
LORA_PAD = 128
NORM_EPS = 1e-6
GN_EPS = 64e-5
SCAN_CHUNK = 64
N_DEV = 8
LANES = 1024
VMEM_LIMIT = 56 * 1024 * 1024

ADAM_LR = 0.001
ADAM_B1 = 0.9
ADAM_B2 = 0.999
ADAM_EPS = 1e-08
ADAM_WD = 0.01
ADAM_STEP = 10

TRANSPOSED = ("w_in", "w_ffn_gate", "w_ffn_up")
SHARDED = (
    ("w_in", (6920, 1024), 0),
    ("w_ffn_gate", (2816, 1024), 0),
    ("w_ffn_up", (2816, 1024), 0),
    ("w_ffn_down", (2816, 1024), 0),
    ("w_mem_kv", (1024, 1024), 0),
    ("w_o", (1024, 1024), 0),
    ("w_fox_out", (512, 1024), 1),
    ("w_rwkv_out", (512, 1024), 1),
    ("w_mem_out", (512, 1024), 1),
    ("rwkv_w_up", (64, 512), 1),
    ("rwkv_a_up", (64, 512), 1),
    ("rwkv_g_up", (128, 512), 1),
)
REPLICATED = (
    ("pre1_g", (1, 1024)), ("post1_g", (1, 1024)), ("pre2_g", (1, 1024)), ("post2_g", (1, 1024)),
    ("mem_norm_g", (1, 1024)), ("fox_f_bias", (1, 8)), ("rwkv_mu", (1, 1792)), ("rwkv_w0", (1, 512)),
    ("rwkv_a0", (1, 512)), ("rwkv_k_k", (1, 512)), ("rwkv_k_a", (1, 512)), ("rwkv_r_k", (1, 8, 64)),
    ("rwkv_gn_g", (1, 512)), ("rwkv_gn_b", (1, 512)),
)
WEIGHT_ORDER = ('pre1_g', 'post1_g', 'pre2_g', 'post2_g', 'mem_norm_g', 'w_in', 'fox_f_bias', 'rwkv_mu',
                'rwkv_w0', 'rwkv_w_up', 'rwkv_a0', 'rwkv_a_up', 'rwkv_g_up', 'rwkv_k_k', 'rwkv_k_a',
                'rwkv_r_k', 'rwkv_gn_g', 'rwkv_gn_b', 'w_mem_kv', 'w_fox_out', 'w_rwkv_out', 'w_mem_out',
                'w_o', 'w_ffn_gate', 'w_ffn_up', 'w_ffn_down')


def _block_shape(shape, axis):
    return tuple(s // N_DEV if i == axis else s for i, s in enumerate(shape))


def _rows_of(shape):
    n = 1
    for s in shape:
        n *= s
    return n // LANES


REPL_ELEMS = sum(_rows_of((LANES,) + s) for _, s in REPLICATED)


def _cp(sem=None):
    return pltpu.CompilerParams(dimension_semantics=sem, vmem_limit_bytes=VMEM_LIMIT)


def _tile(dim, cap):
    best = None
    for t in range(128, min(dim, cap) + 1, 128):
        if dim % t == 0:
            best = t
    return best if best is not None else dim


def _two_terms(x):
    hi = x.astype(bf16)
    return hi, (x - hi.astype(f32)).astype(bf16)


def _dg(a, b, dims, exact):
    if exact == "split":
        (a_hi, a_lo), (b_hi, b_lo) = _two_terms(a), _two_terms(b)
        dot = functools.partial(lax.dot_general, dimension_numbers=dims, preferred_element_type=f32)
        return dot(a_hi, b_hi) + (dot(a_hi, b_lo) + dot(a_lo, b_hi))
    if exact:
        return lax.dot_general(a, b, dims, precision=_HI, preferred_element_type=f32)
    return lax.dot_general(a.astype(bf16), b.astype(bf16), dims, preferred_element_type=f32)


def _make_mm(batched, exact):
    o = 1 if batched else 0
    bd = ((0,), (0,)) if batched else ((), ())
    d_nn = (((1 + o,), (o,)), bd)
    d_nt = (((1 + o,), (1 + o,)), bd)
    d_tn = (((o,), (o,)), bd)

    @jax.custom_vjp
    def nn(a, b):
        return _dg(a, b, d_nn, exact)

    @jax.custom_vjp
    def nt(a, b):
        return _dg(a, b, d_nt, exact)

    @jax.custom_vjp
    def tn(a, b):
        return _dg(a, b, d_tn, exact)

    nn.defvjp(lambda a, b: (_dg(a, b, d_nn, exact), (a, b)),
              lambda res, g: (_dg(g, res[1], d_nt, exact), _dg(res[0], g, d_tn, exact)))
    nt.defvjp(lambda a, b: (_dg(a, b, d_nt, exact), (a, b)),
              lambda res, g: (_dg(g, res[1], d_nn, exact), _dg(g, res[0], d_tn, exact)))
    tn.defvjp(lambda a, b: (_dg(a, b, d_tn, exact), (a, b)),
              lambda res, g: (_dg(res[1], g, d_nt, exact), _dg(res[0], g, d_nn, exact)))
    return nn, nt, tn


def _sigmoid(x):
    return 1.0 / (1.0 + jnp.exp(-x))


def _head_sum_raw(x):
    width = 2 * HD
    i = lax.broadcasted_iota(jnp.int32, (width, width), 0) // HD
    j = lax.broadcasted_iota(jnp.int32, (width, width), 1) // HD
    m = (i == j).astype(bf16)
    dims = (((1,), (0,)), ((), ()))
    out = []
    for p in range(x.shape[1] // width):
        xp = x[:, p * width:(p + 1) * width]
        hi = xp.astype(bf16)
        lo = (xp - hi.astype(f32)).astype(bf16)
        out.append(lax.dot_general(hi, m, dims, preferred_element_type=f32)
                   + lax.dot_general(lo, m, dims, preferred_element_type=f32))
    return jnp.concatenate(out, axis=1)


@jax.custom_vjp
def _head_sum(x):
    return _head_sum_raw(x)


_head_sum.defvjp(lambda x: (_head_sum_raw(x), None), lambda _, g: (_head_sum_raw(g),))


WEIGHT_TILE_BYTES = 13 * 512 * 1024
ACC_TILE_BYTES = 8 * 1024 * 1024


def _lazy(fn, rows, width):
    return (fn, rows, width)


def _matmul(name, a, b, mode, add=None, out_dtype=f32):
    has_add = add is not None
    if isinstance(a, tuple):
        a_fn, a_rows, a_width = a
        a_arrays = [r for r, _ in a_rows]
        a_shape = (a_arrays[0].shape[0], a_width)
    else:
        a_fn, a_rows, a_arrays, a_shape = None, None, [a], a.shape
    n_a = len(a_arrays)

    def load_a(refs):
        if a_fn is None:
            return refs[0][...].astype(bf16)
        pieces = []
        for r, (_, widths) in zip(refs, a_rows):
            pieces += _pieces(r, widths)
        return a_fn(*pieces)[0].astype(bf16)

    if mode == "tn":
        (k, m), (_, n) = a_shape, b.shape
        tn = _tile(n, max(128, ACC_TILE_BYTES // (4 * m)))
        tk = _tile(k, 1024 if a_fn is None else 256)
        nk = k // tk

        def body(*refs):
            b_ref, o_ref, acc = refs[n_a:]

            @pl.when(pl.program_id(1) == 0)
            def _():
                acc[...] = jnp.zeros_like(acc)

            acc[...] += lax.dot_general(load_a(refs[:n_a]), b_ref[...].astype(bf16),
                                        (((0,), (0,)), ((), ())), preferred_element_type=f32)

            @pl.when(pl.program_id(1) == nk - 1)
            def _():
                o_ref[...] = acc[...].astype(o_ref.dtype)

        return pl.pallas_call(
            body, name=name, grid=(n // tn, nk),
            in_specs=[pl.BlockSpec((tk, r.shape[1]), lambda j, kk: (kk, 0)) for r in a_arrays]
            + [pl.BlockSpec((tk, tn), lambda j, kk: (kk, j))],
            out_specs=pl.BlockSpec((m, tn), lambda j, kk: (0, j)), out_shape=jax.ShapeDtypeStruct((m, n), out_dtype),
            scratch_shapes=[pltpu.VMEM((m, tn), f32)],
            compiler_params=_cp(("parallel", "arbitrary")),
        )(*a_arrays, b)

    (m, k) = a_shape
    n = b.shape[1] if mode == "nn" else b.shape[0]
    tm = _tile(m, 1024 if a_fn is None else 512)
    tn = _tile(n, max(128, WEIGHT_TILE_BYTES // (2 * k)))
    dims = (((1,), (0,)), ((), ())) if mode == "nn" else (((1,), (1,)), ((), ()))
    b_spec = pl.BlockSpec((k, tn), lambda j, i: (0, j)) if mode == "nn" else pl.BlockSpec((tn, k), lambda j, i: (j, 0))
    o_spec = pl.BlockSpec((tm, tn), lambda j, i: (i, j))

    keep = a_fn is not None
    assert not keep or tn == n

    def body(*refs):
        b_ref = refs[n_a]
        a_val = load_a(refs[:n_a])
        r = lax.dot_general(a_val, b_ref[...].astype(bf16), dims, preferred_element_type=f32)
        if has_add:
            r = r + refs[n_a + 1][...]
        if keep:
            refs[-2][...] = r.astype(refs[-2].dtype)
            refs[-1][...] = a_val
        else:
            refs[-1][...] = r.astype(refs[-1].dtype)

    res = pl.pallas_call(
        body, name=name, grid=(n // tn, m // tm),
        in_specs=[pl.BlockSpec((tm, r.shape[1]), lambda j, i: (i, 0)) for r in a_arrays] + [b_spec]
        + ([o_spec] if has_add else []),
        out_specs=[o_spec] + ([pl.BlockSpec((tm, k), lambda j, i: (i, 0))] if keep else []),
        out_shape=[jax.ShapeDtypeStruct((m, n), out_dtype)] + ([jax.ShapeDtypeStruct((m, k), bf16)] if keep else []),
        compiler_params=_cp(("parallel", "arbitrary")),
    )(*a_arrays, b, *([add] if has_add else []))
    return tuple(res) if keep else res[0]


def _input_cotangent(name, a_list, b_list, x, gain, add, side=None):
    m = a_list[0].shape[0]
    tm = _tile(m, 256)
    n_g = len(a_list)
    srcs, per_peer = side if side is not None else ([], False)
    n_s = len(srcs)

    def body(*refs):
        x_ref, g_ref, add_ref = refs[2 * n_g:2 * n_g + 3]
        src_refs = refs[2 * n_g + 3:2 * n_g + 3 + n_s]
        dx_ref, dg_ref = refs[2 * n_g + 3 + n_s:2 * n_g + 5 + n_s]
        _side_exchange(src_refs, refs[2 * n_g + 5 + n_s:2 * n_g + 5 + 2 * n_s], per_peer, refs[2 * n_g + 5 + 2 * n_s:], m // tm)
        d_u = None
        for g in range(n_g):
            r = lax.dot_general(refs[g][...].astype(bf16), refs[n_g + g][...].astype(bf16), (((1,), (0,)), ((), ())),
                                preferred_element_type=f32)
            d_u = r if d_u is None else d_u + r
        _, vjp = jax.vjp(_rms, x_ref[...], g_ref[...])
        d_x, d_gain = vjp(d_u)
        dx_ref[...] = d_x + add_ref[...]

        @pl.when(pl.program_id(0) == 0)
        def _():
            dg_ref[...] = jnp.zeros_like(dg_ref)

        dg_ref[...] += d_gain

    rows = pl.BlockSpec((tm, x.shape[1]), lambda i: (i, 0))
    whole = lambda b: pl.BlockSpec(b.shape, lambda i: (0, 0))
    res = pl.pallas_call(
        body, name=name, grid=(m // tm,),
        in_specs=[pl.BlockSpec((tm, a.shape[1]), lambda i: (i, 0)) for a in a_list] + [whole(b) for b in b_list]
        + [rows, whole(gain), rows] + [_HBM_SPEC] * n_s,
        out_specs=[rows, whole(gain)] + [_HBM_SPEC] * n_s,
        out_shape=[jax.ShapeDtypeStruct(x.shape, f32), jax.ShapeDtypeStruct(gain.shape, f32)] + _side_out_shapes(srcs, per_peer),
        scratch_shapes=_side_sems(n_s),
        compiler_params=_cp(("arbitrary",)),
    )(*a_list, *b_list, x, gain, add, *srcs)
    return res[0], res[1], list(res[2:])


def _pieces(ref, widths):
    out, off = [], 0
    for w in widths:
        out.append(ref[:, off:off + w].astype(f32))
        off += w
    return out


def _store_pieces(ref, widths, vals, add_ref=None):
    off = 0
    for w, v in zip(widths, vals):
        ref[:, off:off + w] = (v if add_ref is None else v + add_ref[:, off:off + w]).astype(ref.dtype)
        off += w


def _rows_fwd(name, fn, consts, rows, params, outs, n_sums=0, tm=512, dtypes=None):
    t = (consts + rows)[0][0].shape[0]
    tm = min(tm, t)
    ins = consts + rows
    n_in, n_p, n_o = len(ins), len(params), len(outs)
    dtypes = dtypes or [f32] * n_o

    def body(*refs):
        in_refs, p_refs = refs[:n_in], refs[n_in:n_in + n_p]
        o_refs, s_refs = refs[n_in + n_p:n_in + n_p + n_o], refs[n_in + n_p + n_o:]
        vals = []
        for r, (_, widths) in zip(in_refs, ins):
            vals += _pieces(r, widths)
        res = fn(*vals, *[p[...] for p in p_refs])
        pos = 0
        for r, widths in zip(o_refs, outs):
            _store_pieces(r, widths, res[pos:pos + len(widths)])
            pos += len(widths)

        @pl.when(pl.program_id(0) == 0)
        def _():
            for s in s_refs:
                s[...] = jnp.zeros_like(s)

        for s, v in zip(s_refs, res[pos:]):
            s[...] += v

    row_spec = lambda w: pl.BlockSpec((tm, w), lambda i: (i, 0))
    full = lambda p: pl.BlockSpec(p.shape, lambda i: (0,) * p.ndim)
    return pl.pallas_call(
        body, name=name, grid=(t // tm,),
        in_specs=[row_spec(a.shape[1]) for a, _ in ins] + [full(p) for p in params],
        out_specs=[row_spec(sum(w)) for w in outs] + [pl.BlockSpec((1, 1), lambda i: (0, 0))] * n_sums,
        out_shape=[jax.ShapeDtypeStruct((t, sum(w)), dt) for w, dt in zip(outs, dtypes)] + [jax.ShapeDtypeStruct((1, 1), f32)] * n_sums,
        compiler_params=_cp(("arbitrary",)),
    )(*[a for a, _ in ins], *params)


def _rows_bwd(name, fn, consts, rows, params, outs, cts, n_sums=0, add=None, tm=512, dtypes=None):
    t = (consts + rows)[0][0].shape[0]
    tm = min(tm, t)
    n_c, n_r, n_p, n_o = len(consts), len(rows), len(params), len(outs)
    has_add = add is not None
    dtypes = dtypes or [f32] * n_r

    def body(*refs):
        pos = 0
        c_refs = refs[pos:pos + n_c]; pos += n_c
        r_refs = refs[pos:pos + n_r]; pos += n_r
        p_refs = refs[pos:pos + n_p]; pos += n_p
        ct_refs = refs[pos:pos + n_o]; pos += n_o
        add_ref = refs[pos] if has_add else None
        pos += 1 if has_add else 0
        dr_refs = refs[pos:pos + n_r]; pos += n_r
        dp_refs = refs[pos:pos + n_p]; pos += n_p
        s_refs = refs[pos:pos + n_sums]
        cvals, rvals = [], []
        for r, (_, widths) in zip(c_refs, consts):
            cvals += _pieces(r, widths)
        for r, (_, widths) in zip(r_refs, rows):
            rvals += _pieces(r, widths)
        pvals = [p[...] for p in p_refs]
        ctv = []
        for r, widths in zip(ct_refs, outs):
            ctv += _pieces(r, widths)
        ctv += [jnp.ones((1, 1), f32)] * n_sums
        primal, vjp = jax.vjp(lambda *rp: tuple(fn(*cvals, *rp)), *rvals, *pvals)
        g = vjp(tuple(ctv))
        pos = 0
        for idx, (r, (_, widths)) in enumerate(zip(dr_refs, rows)):
            _store_pieces(r, widths, g[pos:pos + len(widths)], add_ref if idx == 0 else None)
            pos += len(widths)

        @pl.when(pl.program_id(0) == 0)
        def _():
            for acc in list(dp_refs) + list(s_refs):
                acc[...] = jnp.zeros_like(acc)

        for dp, v in zip(dp_refs, g[pos:]):
            dp[...] += v
        for s, v in zip(s_refs, primal[len(primal) - n_sums:]):
            s[...] += v

    row_spec = lambda w: pl.BlockSpec((tm, w), lambda i: (i, 0))
    full = lambda p: pl.BlockSpec(p.shape, lambda i: (0,) * p.ndim)
    args = [a for a, _ in consts + rows] + list(params) + list(cts) + ([add] if has_add else [])
    res = pl.pallas_call(
        body, name=name, grid=(t // tm,),
        in_specs=[row_spec(a.shape[1]) for a, _ in consts + rows] + [full(p) for p in params]
        + [row_spec(sum(w)) for w in outs] + ([row_spec(add.shape[1])] if has_add else []),
        out_specs=[row_spec(a.shape[1]) for a, _ in rows] + [full(p) for p in params]
        + [pl.BlockSpec((1, 1), lambda i: (0, 0))] * n_sums,
        out_shape=[jax.ShapeDtypeStruct(a.shape, dt) for (a, _), dt in zip(rows, dtypes)]
        + [jax.ShapeDtypeStruct(p.shape, f32) for p in params] + [jax.ShapeDtypeStruct((1, 1), f32)] * n_sums,
        compiler_params=_cp(("arbitrary",)),
    )(*args)
    return res[:n_r], res[n_r:n_r + n_p] + res[n_r + n_p:]


def _matmul_then_vjp(name, a, b, mode, fn, rows, dtypes, tm=256):
    m, k = a.shape
    tm = min(tm, m)
    dims = (((1,), (0,)), ((), ())) if mode == "nn" else (((1,), (1,)), ((), ()))
    n_r = len(rows)

    def body(*refs):
        a_ref, b_ref = refs[:2]
        ct = lax.dot_general(a_ref[...].astype(bf16), b_ref[...].astype(bf16), dims, preferred_element_type=f32)
        rvals = []
        for r, (_, widths) in zip(refs[2:2 + n_r], rows):
            rvals += _pieces(r, widths)
        _, vjp = jax.vjp(lambda *rp: fn(*rp)[0], *rvals)
        g = vjp(ct)
        pos = 0
        for r, (_, widths) in zip(refs[2 + n_r:], rows):
            _store_pieces(r, widths, g[pos:pos + len(widths)])
            pos += len(widths)

    row_spec = lambda w: pl.BlockSpec((tm, w), lambda i: (i, 0))
    return pl.pallas_call(
        body, name=name, grid=(m // tm,),
        in_specs=[row_spec(k), pl.BlockSpec(b.shape, lambda i: (0, 0))] + [row_spec(r.shape[1]) for r, _ in rows],
        out_specs=[row_spec(r.shape[1]) for r, _ in rows],
        out_shape=[jax.ShapeDtypeStruct(r.shape, dt) for (r, _), dt in zip(rows, dtypes)],
        compiler_params=_cp(("parallel",)),
    )(a, b, *[r for r, _ in rows])


def _rms(x, g):
    return x * lax.rsqrt(jnp.mean(x * x, axis=-1, keepdims=True) + NORM_EPS) * g


def _fn_rms(x, g):
    return (_rms(x, g),)


def _fn_rwkv_pre(r, k, v, wd, ad, gd, w0, w_up, a0, a_up, g_up, k_k, k_a):
    nn, _, _ = _make_mm(False, False)
    w_log = -_sigmoid(w0 + nn(jnp.tanh(wd), w_up)) * 0.6065306597126334
    a = _sigmoid(a0 + nn(ad, a_up))
    g = nn(_sigmoid(gd), g_up)
    kk = k * k_k
    kk = kk * lax.rsqrt(jnp.maximum(_head_sum(kk * kk), 1e-24))
    k2 = k * (1.0 + (a - 1.0) * k_a)
    return r, w_log, k2, v, -kk, kk * a, g


def _fn_rwkv_post(y, r, k2, v, g, gn_g, gn_b, r_k):
    mean = _head_sum(y) * (1.0 / HD)
    yc = y - mean
    var = _head_sum(yc * yc) * (1.0 / HD)
    yn = yc * lax.rsqrt(var + GN_EPS) * gn_g + gn_b
    bonus = _head_sum(r * k2 * r_k) * v
    return ((yn + bonus) * g,)


def _fn_merge(a_fox, a_rwkv, a_mem, g_fox, g_rwkv, g_mem):
    return (_sigmoid(g_fox) * a_fox + _sigmoid(g_rwkv) * a_rwkv + _sigmoid(g_mem) * a_mem,)


def _fn_post1(y, x, post1_g, pre2_g):
    h1 = x + _rms(y, post1_g)
    return h1, _rms(h1, pre2_g)


def _fn_swiglu(gp, up):
    return (gp * _sigmoid(gp) * up,)


def _fn_final(target, ffn, h1, post2_g):
    err = h1 + _rms(ffn, post2_g) - target
    per_row = jnp.mean(err * err, axis=-1, keepdims=True)
    return (0.5 * jnp.sum(per_row, axis=0, keepdims=True),)


def _shift_down(x):
    row = lax.broadcasted_iota(jnp.int32, x.shape, 0)
    return jnp.where(row == 0, 0.0, pltpu.roll(x, 1, 0))


def _shift_up(x):
    s = x.shape[0]
    row = lax.broadcasted_iota(jnp.int32, x.shape, 0)
    return jnp.where(row == s - 1, 0.0, pltpu.roll(x, s - 1, 0))


def _tokshift_fwd(p, mu, batch, seq):
    w = p.shape[1]
    tc = _tile(w, 384)

    def body(p_ref, mu_ref, o_ref):
        x = p_ref[...]
        o_ref[...] = x + (_shift_down(x) - x) * mu_ref[...]

    return pl.pallas_call(
        body, name="tokshift_fwd", grid=(w // tc, batch),
        in_specs=[pl.BlockSpec((seq, tc), lambda j, b: (b, j)), pl.BlockSpec((1, tc), lambda j, b: (0, j))],
        out_specs=pl.BlockSpec((seq, tc), lambda j, b: (b, j)),
        out_shape=jax.ShapeDtypeStruct(p.shape, f32),
        compiler_params=_cp(("parallel", "arbitrary")),
    )(p, mu)


def _tokshift_bwd(p, mu, dps, batch, seq):
    w = p.shape[1]
    tc = _tile(w, 384)

    def body(p_ref, mu_ref, d_ref, dp_ref, dmu_ref):
        x, mu_v, d = p_ref[...], mu_ref[...], d_ref[...]
        dp_ref[...] = (d * (1.0 - mu_v) + _shift_up(d * mu_v)).astype(dp_ref.dtype)

        @pl.when(pl.program_id(1) == 0)
        def _():
            dmu_ref[...] = jnp.zeros_like(dmu_ref)

        dmu_ref[...] += jnp.sum(d * (_shift_down(x) - x), axis=0, keepdims=True)

    return pl.pallas_call(
        body, name="tokshift_bwd", grid=(w // tc, batch),
        in_specs=[pl.BlockSpec((seq, tc), lambda j, b: (b, j)), pl.BlockSpec((1, tc), lambda j, b: (0, j)),
                  pl.BlockSpec((seq, tc), lambda j, b: (b, j))],
        out_specs=[pl.BlockSpec((seq, tc), lambda j, b: (b, j)), pl.BlockSpec((1, tc), lambda j, b: (0, j))],
        out_shape=[jax.ShapeDtypeStruct(p.shape, bf16), jax.ShapeDtypeStruct(mu.shape, f32)],
        compiler_params=_cp(("parallel", "arbitrary")),
    )(p, mu, dps)


def _cum_block(seq):
    return _tile(seq, 256)


def _fox_gate_fwd(f, bias, batch, seq):
    cb = _cum_block(seq)

    def body(f_ref, b_ref, c_ref):
        row = lax.broadcasted_iota(jnp.int32, (cb, cb), 0)
        col = lax.broadcasted_iota(jnp.int32, (cb, cb), 1)
        tri = (col <= row).astype(f32)
        carry = jnp.zeros((1, 128), f32)
        for i in range(seq // cb):
            z = f_ref[i * cb:(i + 1) * cb, :] + b_ref[...]
            ls = jnp.minimum(z, 0.0) - jnp.log(1.0 + jnp.exp(-jnp.abs(z)))
            c = _dg(tri, ls, (((1,), (0,)), ((), ())), True) + carry
            c_ref[i * cb:(i + 1) * cb, :] = c
            carry = c[cb - 1:cb, :]

    return pl.pallas_call(
        body, name="fox_gate_fwd", grid=(batch,),
        in_specs=[pl.BlockSpec((seq, 128), lambda b: (b, 0)), pl.BlockSpec((1, 128), lambda b: (0, 0))],
        out_specs=pl.BlockSpec((seq, 128), lambda b: (b, 0)),
        out_shape=jax.ShapeDtypeStruct(f.shape, f32),
        compiler_params=_cp(("arbitrary",)),
    )(f, bias)


def _fox_gate_bwd(f, bias, dc_a, dc_b, batch, seq):
    cb = _cum_block(seq)

    def body(f_ref, b_ref, da_ref, db_ref, df_ref, dbias_ref):
        row = lax.broadcasted_iota(jnp.int32, (cb, cb), 0)
        col = lax.broadcasted_iota(jnp.int32, (cb, cb), 1)
        triu = (col >= row).astype(f32)

        @pl.when(pl.program_id(0) == 0)
        def _():
            dbias_ref[...] = jnp.zeros_like(dbias_ref)

        lane = lax.broadcasted_iota(jnp.int32, (1, 128), 1)

        def by_head(blk):
            out = jnp.zeros((cb, 128), f32)
            for p in range(HEADS // 2):
                for e in range(2):
                    out = jnp.where(lane == 2 * p + e, _pick_lane(blk[:, p * 128:(p + 1) * 128], e), out)
            return out

        carry = jnp.zeros((1, 128), f32)
        tot = jnp.zeros((1, 128), f32)
        for i in reversed(range(seq // cb)):
            sl = slice(i * cb, (i + 1) * cb)
            dc = by_head(da_ref[sl, :] + db_ref[sl, :])
            dls = _dg(triu, dc, (((1,), (0,)), ((), ())), True) + carry
            carry = dls[0:1, :]
            df = dls * _sigmoid(-(f_ref[sl, :] + b_ref[...]))
            df_ref[sl, :] = df.astype(df_ref.dtype)
            tot = tot + jnp.sum(df, axis=0, keepdims=True)
        dbias_ref[...] += tot

    return pl.pallas_call(
        body, name="fox_gate_bwd", grid=(batch,),
        in_specs=[pl.BlockSpec((seq, 128), lambda b: (b, 0)), pl.BlockSpec((1, 128), lambda b: (0, 0)),
                  pl.BlockSpec((seq, HW), lambda b: (b, 0)), pl.BlockSpec((seq, HW), lambda b: (b, 0))],
        out_specs=[pl.BlockSpec((seq, 128), lambda b: (b, 0)), pl.BlockSpec((1, 128), lambda b: (0, 0))],
        out_shape=[jax.ShapeDtypeStruct(f.shape, bf16), jax.ShapeDtypeStruct((1, 128), f32)],
        compiler_params=_cp(("arbitrary",)),
    )(f, bias, dc_a, dc_b)


_HBM_SPEC = pl.BlockSpec(memory_space=pltpu.HBM)


def _side_out_shapes(srcs, per_peer):
    return [jax.ShapeDtypeStruct(((N_DEV,) + tuple(s.shape[1:] if per_peer else s.shape)), s.dtype) for s in srcs]


def _side_sems(n):
    if n == 0:
        return []
    return [pltpu.SemaphoreType.DMA((n, N_DEV - 1)), pltpu.SemaphoreType.DMA((n, N_DEV - 1)), pltpu.SemaphoreType.DMA((n,))]


def _peer_copies(src_refs, dst_refs, per_peer, sems):
    send_sems, recv_sems, local_sems = sems
    x, y, c = lax.axis_index("x"), lax.axis_index("y"), lax.axis_index("c")
    me = 4 * x + 2 * y + c

    def remote(src, dst, t, k, to):
        return pltpu.make_async_remote_copy(src_ref=src, dst_ref=dst, send_sem=send_sems.at[t, k - 1],
                                            recv_sem=recv_sems.at[t, k - 1], device_id=to,
                                            device_id_type=pl.DeviceIdType.MESH)

    direct, relays = [], []
    for t, (s, d) in enumerate(zip(src_refs, dst_refs)):
        direct.append((t, 0, pltpu.make_async_copy(s.at[me] if per_peer else s, d.at[me], local_sems.at[t])))
        for k in range(1, N_DEV):
            px = 1 - x if k & 4 else x
            py = 1 - y if k & 2 else y
            pc = 1 - c if k & 1 else c
            if per_peer:
                direct.append((t, k, remote(s.at[4 * px + 2 * py + pc], d.at[me], t, k, (px, py, pc))))
            elif k == 1 or not k & 1:
                direct.append((t, k, remote(s, d.at[me], t, k, (px, py, pc))))
            else:
                origin = d.at[4 * px + 2 * py + c]
                relays.append((t, k - 1, remote(origin, origin, t, k, (x, y, 1 - c))))
    return direct, relays


def _exchange_start(direct):
    for _, _, cp in direct:
        cp.start()


def _exchange_relay(direct, relays):
    landed = {(t, k): cp for t, k, cp in direct}
    for t, j, cp in relays:
        landed[(t, j)].wait_recv()
        cp.start()


def _exchange_finish(direct, relays):
    relayed = {(t, j) for t, j, _ in relays}
    for t, k, cp in direct:
        if k == 0:
            cp.wait()
        else:
            cp.wait_send()
            if (t, k) not in relayed:
                cp.wait_recv()
    for _, _, cp in relays:
        cp.wait()


def _side_exchange(src_refs, dst_refs, per_peer, sems, *grid):
    if not src_refs:
        return
    step, total = 0, 1
    for a, n in enumerate(grid):
        step, total = step * n + pl.program_id(a), total * n

    @pl.when(step == 0)
    def _():
        _exchange_start(_peer_copies(src_refs, dst_refs, per_peer, sems)[0])

    @pl.when(step == (3 * total) // 4)
    def _():
        _exchange_relay(*_peer_copies(src_refs, dst_refs, per_peer, sems))

    @pl.when(step == total - 1)
    def _():
        _exchange_finish(*_peer_copies(src_refs, dst_refs, per_peer, sems))


def _exchange(name, srcs, per_peer):
    n = len(srcs)

    def body(*refs):
        direct, relays = _peer_copies(refs[:n], refs[n:2 * n], per_peer, refs[2 * n:])
        _exchange_start(direct)
        _exchange_relay(direct, relays)
        _exchange_finish(direct, relays)

    return pl.pallas_call(
        body, name=name, in_specs=[_HBM_SPEC] * n, out_specs=[_HBM_SPEC] * n,
        out_shape=_side_out_shapes(srcs, per_peer), scratch_shapes=_side_sems(n),
    )(*srcs)


FOX_T = 512
_NEG = -1e30
_D2 = (((1,), (1,)), ((), ()))
_D1 = (((1,), (0,)), ((), ()))
_D0 = (((0,), (0,)), ((), ()))


def _bdot(a, b, dims):
    return lax.dot_general(a.astype(bf16), b.astype(bf16), dims, preferred_element_type=f32)


def _pick_lane(x, lane):
    idx = lax.broadcasted_iota(jnp.int32, x.shape, 1)
    return jnp.sum(jnp.where(idx == lane, x, 0.0), axis=1, keepdims=True)


def _pick_row(x, row):
    idx = lax.broadcasted_iota(jnp.int32, x.shape, 0)
    return jnp.sum(jnp.where(idx == row, x, 0.0), axis=0, keepdims=True)


def _fox_fwd(qkv, c, c_rows, batch, seq, side=None):
    t = min(FOX_T, seq)
    nq = seq // t
    scale = HD ** -0.5
    srcs, per_peer = side if side is not None else ([], False)
    n_s = len(srcs)

    def body(*refs):
        q_ref, k_ref, v_ref, cq_ref, ck_ref = refs[:5]
        o_ref, lse_ref = refs[5 + n_s:7 + n_s]
        _side_exchange(refs[5:5 + n_s], refs[7 + n_s:7 + 2 * n_s], per_peer, refs[7 + 2 * n_s:], batch, PAIRS, nq)
        pair, i = pl.program_id(1), pl.program_id(2)
        lane = lax.broadcasted_iota(jnp.int32, (1, PAIR_W), 1)
        first = (lane // HD) == 0
        mine = [first, jnp.logical_not(first)]
        q = q_ref[...] * scale
        qs = [jnp.where(mine[e], q, 0.0) for e in range(2)]
        cqs = [_pick_lane(cq_ref[...], 2 * pair + e) for e in range(2)]
        causal = lax.broadcasted_iota(jnp.int32, (t, t), 1) <= lax.broadcasted_iota(jnp.int32, (t, t), 0)

        def block(j, carry, diagonal):
            rows = pl.ds(pl.multiple_of(j * t, t), t)
            kj, vj = k_ref[rows, :], v_ref[rows, :]
            ck_blk = ck_ref[0, :, rows]
            out = []
            for e in range(2):
                m, acc = carry[2 * e:2 * e + 2]
                s = _bdot(qs[e], kj, _D2) + cqs[e] - _pick_row(ck_blk, 2 * pair + e)
                if diagonal:
                    s = jnp.where(causal, s, _NEG)
                m_new = jnp.maximum(m, jnp.max(s, axis=1, keepdims=True))
                p = jnp.exp(s - m_new)
                out += [m_new, jnp.exp(m - m_new) * acc + _bdot(p, jnp.where(mine[e], vj, 1.0), _D1)]
            return tuple(out)

        init = (jnp.full((t, 1), _NEG, f32), jnp.zeros((t, PAIR_W), f32)) * 2
        carry = lax.fori_loop(0, i, lambda j, cr: block(j, cr, False), init)
        m0, a0, m1, a1 = block(i, carry, True)
        l0, l1 = _pick_lane(a0, HD), _pick_lane(a1, 0)
        o_ref[...] = jnp.where(first, a0 / l0, a1 / l1)
        lse_ref[...] = jnp.where(lane == 0, m0 + jnp.log(l0), jnp.where(lane == 1, m1 + jnp.log(l1), 0.0))

    q_spec = pl.BlockSpec((t, PAIR_W), lambda b, p, i: (b * nq + i, p))
    res = pl.pallas_call(
        body, name="fox_attn_fwd", grid=(batch, PAIRS, nq),
        in_specs=[q_spec,
                  pl.BlockSpec((seq, PAIR_W), lambda b, p, i: (b, PAIRS + p)),
                  pl.BlockSpec((seq, PAIR_W), lambda b, p, i: (b, 2 * PAIRS + p)),
                  pl.BlockSpec((t, 128), lambda b, p, i: (b * nq + i, 0)),
                  pl.BlockSpec((1, 8, seq), lambda b, p, i: (b, 0, 0))] + [_HBM_SPEC] * n_s,
        out_specs=[q_spec, q_spec] + [_HBM_SPEC] * n_s,
        out_shape=[jax.ShapeDtypeStruct((batch * seq, HW), f32)] * 2 + _side_out_shapes(srcs, per_peer),
        scratch_shapes=_side_sems(n_s),
        compiler_params=_cp(("arbitrary", "arbitrary", "arbitrary")),
    )(qkv, qkv, qkv, c, c_rows, *srcs)
    return res[0], res[1], list(res[2:])


def _fox_bwd(qkv, c, c_rows, o, lse, do, batch, seq):
    t = min(FOX_T, seq)
    nq = seq // t
    scale = HD ** -0.5

    def body(q_ref, k_ref, v_ref, cq_ref, ck_ref, o_ref, lse_ref, do_ref,
             dq_ref, dk_ref, dv_ref, dcq_ref, dck_ref, acc0, acc1):
        pair, i = pl.program_id(1), pl.program_id(2)
        accs = [acc0, acc1]

        @pl.when(i == 0)
        def _():
            dv_ref[...] = jnp.zeros_like(dv_ref)
            acc0[...] = jnp.zeros_like(acc0)
            acc1[...] = jnp.zeros_like(acc1)

        lane = lax.broadcasted_iota(jnp.int32, (1, PAIR_W), 1)
        first = (lane // HD) == 0
        mine = [first, jnp.logical_not(first)]
        q, d_o, o_i = q_ref[...] * scale, do_ref[...], o_ref[...]
        q0s = [jnp.where(mine[e], q, 0.0) for e in range(2)]
        q1s = [jnp.where(mine[e], q, 1.0) for e in range(2)]
        dos = [jnp.where(mine[e], d_o, 0.0) for e in range(2)]
        deltas = [jnp.sum(dos[e] * o_i, axis=1, keepdims=True) for e in range(2)]
        lses = [_pick_lane(lse_ref[...], e) for e in range(2)]
        cqs = [_pick_lane(cq_ref[...], 2 * pair + e) for e in range(2)]
        causal = lax.broadcasted_iota(jnp.int32, (t, t), 1) <= lax.broadcasted_iota(jnp.int32, (t, t), 0)

        def block(j, dqs, diagonal):
            rows = pl.ds(pl.multiple_of(j * t, t), t)
            kj, vj = k_ref[rows, :], v_ref[rows, :]
            ck_blk = ck_ref[0, :, rows]
            out = []
            for e in range(2):
                s = _bdot(q0s[e], kj, _D2) + cqs[e] - _pick_row(ck_blk, 2 * pair + e)
                if diagonal:
                    s = jnp.where(causal, s, _NEG)
                p = jnp.exp(s - lses[e])
                ds = p * (_bdot(dos[e], vj, _D2) - deltas[e])
                dv_ref[rows, :] += _bdot(p, dos[e], _D0)
                accs[e][rows, :] += _bdot(ds, q1s[e], _D0)
                out.append(dqs[e] + _bdot(ds, jnp.where(mine[e], kj, 1.0), _D1))
            return tuple(out)

        zero = jnp.zeros((t, PAIR_W), f32)
        dqs = lax.fori_loop(0, i, lambda j, cr: block(j, cr, False), (zero, zero))
        dq0, dq1 = block(i, dqs, True)
        dq_ref[...] = jnp.where(first, dq0, dq1) * scale
        dcq_ref[...] = jnp.where(lane == 0, _pick_lane(dq0, HD), jnp.where(lane == 1, _pick_lane(dq1, 0), 0.0))

        @pl.when(i == nq - 1)
        def _():
            a0, a1 = acc0[...], acc1[...]
            dk_ref[...] = jnp.where(first, a0, a1)
            dck_ref[...] = jnp.where(lane == 0, -_pick_lane(a0, HD), jnp.where(lane == 1, -_pick_lane(a1, 0), 0.0))

    blk = lambda col: pl.BlockSpec((t, PAIR_W), lambda b, p, i: (b * nq + i, col * PAIRS + p))
    whole = lambda col: pl.BlockSpec((seq, PAIR_W), lambda b, p, i: (b, col * PAIRS + p))
    t_all = batch * seq
    return pl.pallas_call(
        body, name="fox_attn_bwd", grid=(batch, PAIRS, nq),
        in_specs=[blk(0), whole(1), whole(2),
                  pl.BlockSpec((t, 128), lambda b, p, i: (b * nq + i, 0)),
                  pl.BlockSpec((1, 8, seq), lambda b, p, i: (b, 0, 0)),
                  blk(0), blk(0), blk(0)],
        out_specs=[blk(0), whole(0), whole(0), blk(0), whole(0)],
        out_shape=[jax.ShapeDtypeStruct((t_all, HW), f32)] * 5,
        scratch_shapes=[pltpu.VMEM((seq, PAIR_W), f32), pltpu.VMEM((seq, PAIR_W), f32)],
        compiler_params=_cp(("parallel", "parallel", "arbitrary")),
    )(qkv, qkv, qkv, c, c_rows, o, lse, do)


MEM_TQ = 1024


def _mem_block(q, km, vm):
    nn, nt, _ = _make_mm(False, False)
    logits = nt(q, km) * (MEM_HD ** -0.5)
    m = lax.stop_gradient(jnp.max(logits, axis=-1, keepdims=True))
    e = jnp.exp(logits - m)
    return nn(e / jnp.sum(e, axis=-1, keepdims=True), vm)


def _mem_specs(seq, tq):
    nq = seq // tq
    qs = pl.BlockSpec((tq, MEM_HD), lambda b, h, i: (b * nq + i, h))
    ks = pl.BlockSpec((MEM_LEN, MEM_HD), lambda b, h, i: (b, h))
    vs = pl.BlockSpec((MEM_LEN, MEM_HD), lambda b, h, i: (b, MEM_HEADS + h))
    return nq, qs, ks, vs


def _mem_fwd(q, mem_kv, batch, seq):
    tq = min(MEM_TQ, seq)
    nq, qs, ks, vs = _mem_specs(seq, tq)

    def body(q_ref, k_ref, v_ref, o_ref):
        o_ref[...] = _mem_block(q_ref[...].astype(f32), k_ref[...], v_ref[...]).astype(o_ref.dtype)

    return pl.pallas_call(
        body, name="mem_attn_fwd", grid=(batch, MEM_HEADS, nq),
        in_specs=[qs, ks, vs], out_specs=qs, out_shape=jax.ShapeDtypeStruct(q.shape, bf16),
        compiler_params=_cp(("parallel", "parallel", "arbitrary")),
    )(q, mem_kv, mem_kv)


def _mem_bwd(q, mem_kv, do, batch, seq):
    tq = min(MEM_TQ, seq)
    nq, qs, ks, vs = _mem_specs(seq, tq)

    def body(q_ref, k_ref, v_ref, do_ref, dq_ref, dk_ref, dv_ref):
        _, vjp = jax.vjp(_mem_block, q_ref[...].astype(f32), k_ref[...], v_ref[...])
        dq, dk, dv = vjp(do_ref[...])
        dq_ref[...] = dq.astype(dq_ref.dtype)

        @pl.when(pl.program_id(2) == 0)
        def _():
            dk_ref[...] = jnp.zeros_like(dk_ref)
            dv_ref[...] = jnp.zeros_like(dv_ref)

        dk_ref[...] += dk
        dv_ref[...] += dv

    return pl.pallas_call(
        body, name="mem_attn_bwd", grid=(batch, MEM_HEADS, nq),
        in_specs=[qs, ks, vs, qs], out_specs=[qs, ks, ks],
        out_shape=[jax.ShapeDtypeStruct(q.shape, bf16), jax.ShapeDtypeStruct((batch * MEM_LEN, MEM_W), f32),
                   jax.ShapeDtypeStruct((batch * MEM_LEN, MEM_W), f32)],
        compiler_params=_cp(("parallel", "parallel", "arbitrary")),
    )(q, mem_kv, mem_kv, do)


@jax.custom_vjp
def _halves(x):
    c = x.shape[1] // 2
    return x[:, :c], x[:, c:]


_halves.defvjp(lambda x: ((x[:, :x.shape[1] // 2], x[:, x.shape[1] // 2:]), None),
               lambda _, g: (jnp.concatenate(g, axis=1),))


@jax.custom_vjp
def _lead_halves(x):
    n = x.shape[0] // 2
    return x[:n], x[n:]


_lead_halves.defvjp(lambda x: ((x[:x.shape[0] // 2], x[x.shape[0] // 2:]), None),
                    lambda _, g: (jnp.concatenate(g, axis=0),))


def _scan_chunk(s0, r, wl, k, v, a, b):
    nn, nt, tn = _make_mm(True, False)
    nn_exact, _, _ = _make_mm(True, True)
    _, nt_exact, _ = _make_mm(True, "split")
    hp, c, lanes = r.shape
    row = lax.broadcasted_iota(jnp.int32, (c, c), 0)
    col = lax.broadcasted_iota(jnp.int32, (c, c), 1)
    first = (lax.broadcasted_iota(jnp.int32, (1, 1, lanes), 2) // HD) == 0
    tri = jnp.broadcast_to((col <= row).astype(f32)[None], (hp, c, c))
    lg = nn_exact(tri, wl)
    lg_end = lg[:, c - 1:c, :]
    grow, shrink, to_end = jnp.exp(lg), jnp.exp(-lg), jnp.exp(lg_end - lg)
    rt, kt, bt, at = r * grow, k * shrink, b * shrink, a * jnp.exp(lg - wl)
    strict, incl = (col < row)[None], (col <= row)[None]
    twice = lambda t: jnp.concatenate([t, t], axis=0)
    queries = jnp.concatenate([at, rt], axis=1)
    per_head = jnp.concatenate([jnp.where(first, queries, 0.0), jnp.where(first, 0.0, queries)], axis=0)
    (ab, rb), (ak, rk) = _halves(nt_exact(per_head, twice(bt))), _halves(nt_exact(per_head, twice(kt)))
    l_ab = jnp.where(strict, ab, 0.0)
    a_ak = jnp.where(strict, ak, 0.0)
    a_rb = jnp.where(incl, rb, 0.0)
    a_rk = jnp.where(incl, rk, 0.0)
    inv = (col == row).astype(f32)[None] + l_ab
    power, n = l_ab, 1
    while 2 * n < c:
        power = nn(power, power)
        inv = inv + nn(inv, power)
        n *= 2

    def apply(m, t):
        lo, hi = _lead_halves(nn(m, twice(t)))
        return jnp.where(first, lo, hi)

    sa = apply(inv, nt(at, s0) + apply(a_ak, v))
    y = nt(rt, s0) + apply(a_rk, v) + apply(a_rb, sa)
    same_head = ((lax.broadcasted_iota(jnp.int32, (lanes, lanes), 0) // HD)
                 == (lax.broadcasted_iota(jnp.int32, (lanes, lanes), 1) // HD))[None]
    s1 = s0 * jnp.exp(lg_end) + jnp.where(same_head, tn(v, k * to_end) + tn(sa, b * to_end), 0.0)
    return y, s1


PAIRS = HEADS // 2
PAIR_W = 2 * HD


def _pair_stack(ref, off):
    return jnp.stack([ref[b, :, off + p * PAIR_W:off + (p + 1) * PAIR_W]
                      for b in range(ref.shape[0]) for p in range(PAIRS)])


def _pair_store(ref, off, val, add_ref=None):
    for b in range(ref.shape[0]):
        for p in range(PAIRS):
            sl = slice(off + p * PAIR_W, off + (p + 1) * PAIR_W)
            v = val[b * PAIRS + p]
            ref[b, :, sl] = v if add_ref is None else v + add_ref[b, :, sl]


def _scan_fwd(main6, batch, seq, side=None):
    c = min(SCAN_CHUNK, seq)
    nc = seq // c
    hp = batch * PAIRS
    srcs, per_peer = side if side is not None else ([], False)
    n_s = len(srcs)

    def body(*refs):
        z_ref, y_ref, s_ref, st = refs[0], refs[1 + n_s], refs[2 + n_s], refs[3 + 2 * n_s]
        _side_exchange(refs[1:1 + n_s], refs[3 + n_s:3 + 2 * n_s], per_peer, refs[4 + 2 * n_s:], nc)

        @pl.when(pl.program_id(0) == 0)
        def _():
            st[...] = jnp.zeros_like(st)

        s0 = st[...]
        s_ref[0] = s0
        y, s1 = _scan_chunk(s0, *[_pair_stack(z_ref, comp * HW) for comp in range(6)])
        _pair_store(y_ref, 0, y)
        st[...] = s1

    res = pl.pallas_call(
        body, name="rwkv_scan_fwd", grid=(nc,),
        in_specs=[pl.BlockSpec((batch, c, 6 * HW), lambda i: (0, i, 0))] + [_HBM_SPEC] * n_s,
        out_specs=[pl.BlockSpec((batch, c, HW), lambda i: (0, i, 0)),
                   pl.BlockSpec((1, hp, PAIR_W, PAIR_W), lambda i: (i, 0, 0, 0))] + [_HBM_SPEC] * n_s,
        out_shape=[jax.ShapeDtypeStruct((batch, seq, HW), f32), jax.ShapeDtypeStruct((nc, hp, PAIR_W, PAIR_W), f32)]
        + _side_out_shapes(srcs, per_peer),
        scratch_shapes=[pltpu.VMEM((hp, PAIR_W, PAIR_W), f32)] + _side_sems(n_s),
        compiler_params=_cp(("arbitrary",)),
    )(main6.reshape(batch, seq, 6 * HW), *srcs)
    return res[0].reshape(batch * seq, HW), res[1], list(res[2:])


def _scan_bwd(main6, states, dy, extra, batch, seq, side=None):
    c = min(SCAN_CHUNK, seq)
    nc = seq // c
    hp = batch * PAIRS
    srcs, per_peer = side if side is not None else ([], False)
    n_s = len(srcs)

    def body(*refs):
        z_ref, s_ref, dy_ref, ex_ref = refs[:4]
        dz_ref, dst = refs[4 + n_s], refs[5 + 2 * n_s]
        _side_exchange(refs[4:4 + n_s], refs[5 + n_s:5 + 2 * n_s], per_peer, refs[6 + 2 * n_s:], nc)

        @pl.when(pl.program_id(0) == 0)
        def _():
            dst[...] = jnp.zeros_like(dst)

        _, vjp = jax.vjp(_scan_chunk, s_ref[0], *[_pair_stack(z_ref, comp * HW) for comp in range(6)])
        g = vjp((_pair_stack(dy_ref, 0), dst[...]))
        dst[...] = g[0]
        for comp in range(6):
            _pair_store(dz_ref, comp * HW, g[1 + comp], ex_ref)

    back = lambda i: (0, nc - 1 - i, 0)
    wide = pl.BlockSpec((batch, c, 6 * HW), back)
    res = pl.pallas_call(
        body, name="rwkv_scan_bwd", grid=(nc,),
        in_specs=[wide, pl.BlockSpec((1, hp, PAIR_W, PAIR_W), lambda i: (nc - 1 - i, 0, 0, 0)),
                  pl.BlockSpec((batch, c, HW), back), wide] + [_HBM_SPEC] * n_s,
        out_specs=[wide] + [_HBM_SPEC] * n_s,
        out_shape=[jax.ShapeDtypeStruct((batch, seq, 6 * HW), f32)] + _side_out_shapes(srcs, per_peer),
        scratch_shapes=[pltpu.VMEM((hp, PAIR_W, PAIR_W), f32)] + _side_sems(n_s),
        compiler_params=_cp(("arbitrary",)),
    )(main6.reshape(batch, seq, 6 * HW), states, dy.reshape(batch, seq, HW), extra.reshape(batch, seq, 6 * HW), *srcs)
    return res[0].reshape(batch * seq, 6 * HW), list(res[1:])


def _pad_cols(x, width):
    return jnp.pad(x, ((0, 0), (0, width - x.shape[1])))


def _split_w_in(wt):
    z = lambda rows: jnp.zeros((rows, wt.shape[1]), wt.dtype)
    w_r = jnp.concatenate([wt[1544:3080], wt[3080:3144], z(64), wt[3144:3208], z(64), wt[3208:3336]], axis=0)
    return wt[:1536], jnp.concatenate([wt[1536:1544], z(120)], axis=0), w_r, wt[3336:3848], wt[3848:]


def _merge_w_in(g_qkv, g_f, g_r, g_mq, g_g):
    return jnp.concatenate([g_qkv, g_f[:8], g_r[:1536], g_r[1536:1600], g_r[1664:1728], g_r[1792:], g_mq, g_g], axis=0)


def _pad_lora(v):
    z64 = jnp.zeros((1, 64), v.dtype)
    return jnp.concatenate([v[:, :1536], v[:, 1536:1600], z64, v[:, 1600:1664], z64, v[:, 1664:]], axis=1)


def _unpad_lora(v):
    return jnp.concatenate([v[:, :1536], v[:, 1536:1600], v[:, 1664:1728], v[:, 1792:]], axis=1)


def _local_step(x, mem, target, w, p, late=None, early=None, last=None):
    batch, seq, _ = x.shape
    t = batch * seq
    x2, tg2, mem2 = x.reshape(t, D), target.reshape(t, D), mem.reshape(batch * MEM_LEN, D)
    w_qkv, w_f, w_r, w_mq, w_g3 = _split_w_in(w["w_in"])
    mu = _pad_lora(p["rwkv_mu"])
    bias = _pad_cols(p["fox_f_bias"], 128)
    r_k = p["rwkv_r_k"].reshape(1, HW)
    post_params = [p["rwkv_gn_g"], p["rwkv_gn_b"], r_k]
    rw_widths = [HW, HW, HW, LORA_PAD, LORA_PAD, LORA_PAD]
    six = [HW] * 6

    (u,) = _rows_fwd("rms_pre1", _fn_rms, [], [(x2, [D])], [p["pre1_g"]], [[D]], dtypes=[bf16])
    p_qkv = _matmul("proj_qkv", u, w_qkv, "nt", out_dtype=bf16)
    p_f = _matmul("proj_f", u, w_f, "nt")
    p_r = _matmul("proj_rwkv", u, w_r, "nt")
    p_mq = _matmul("proj_memq", u, w_mq, "nt", out_dtype=bf16)
    p_g = _matmul("proj_gate", u, w_g3, "nt", out_dtype=bf16)

    c = _fox_gate_fwd(p_f, bias, batch, seq)
    c_rows = c[:, :HEADS].reshape(batch, seq, HEADS).transpose(0, 2, 1)
    fox_o, lse, gathered = _fox_fwd(p_qkv, c, c_rows, batch, seq, side=(late[0], False) if late else None)
    if late:
        w = {**w, **late[2](gathered, 0)}
    fox_out = fox_o.astype(bf16)

    w_up = jnp.pad(w["rwkv_w_up"].astype(f32), ((0, LORA_PAD - 64), (0, 0)))
    a_up = jnp.pad(w["rwkv_a_up"].astype(f32), ((0, LORA_PAD - 64), (0, 0)))
    pre_params = [p["rwkv_w0"], w_up, p["rwkv_a0"], a_up, w["rwkv_g_up"].astype(f32), p["rwkv_k_k"], p["rwkv_k_a"]]
    ps = _tokshift_fwd(p_r, mu, batch, seq)
    main6, g_rw = _rows_fwd("rwkv_pre", _fn_rwkv_pre, [], [(ps, rw_widths)], pre_params, [six, [HW]], tm=256)
    y_rw, states, gathered = _scan_fwd(main6, batch, seq, side=(late[1], False) if late else None)
    if late:
        w = {**w, **late[2](gathered, 1)}
    post_consts = []
    post_rows = [(y_rw, [HW]), (main6, six), (g_rw, [HW])]

    def fn_post(y, r, _wl, k2, v, _a, _b, g, gn_g, gn_b, rk):
        return _fn_rwkv_post(y, r, k2, v, g, gn_g, gn_b, rk)

    (rwkv_out,) = _rows_fwd("rwkv_post", fn_post, post_consts, post_rows, post_params, [[HW]], dtypes=[bf16], tm=256)

    (memn,) = _rows_fwd("rms_mem", _fn_rms, [], [(mem2, [D])], [p["mem_norm_g"]], [[D]], dtypes=[bf16])
    mem_kv = _matmul("proj_memkv", memn, w["w_mem_kv"], "nn")
    mem_out = _mem_fwd(p_mq, mem_kv, batch, seq)

    a_fox = _matmul("out_fox", fox_out, w["w_fox_out"], "nn", out_dtype=bf16)
    a_rwkv = _matmul("out_rwkv", rwkv_out, w["w_rwkv_out"], "nn", out_dtype=bf16)
    a_mem = _matmul("out_mem", mem_out, w["w_mem_out"], "nn", out_dtype=bf16)
    merge_rows = [(a_fox, [D]), (a_rwkv, [D]), (a_mem, [D]), (p_g, [D, D, D])]
    yy, merged = _matmul("out_o", _lazy(_fn_merge, merge_rows, D), w["w_o"], "nn")
    post1_rows = [(yy, [D]), (x2, [D])]
    post1_params = [p["post1_g"], p["pre2_g"]]
    h1, u2 = _rows_fwd("post1", _fn_post1, [], post1_rows, post1_params, [[D], [D]], dtypes=[f32, bf16])
    gp = _matmul("ffn_gate", u2, w["w_ffn_gate"], "nt", out_dtype=bf16)
    up = _matmul("ffn_up", u2, w["w_ffn_up"], "nt", out_dtype=bf16)
    ffn, hmid = _matmul("ffn_down", _lazy(_fn_swiglu, [(gp, [D_FF]), (up, [D_FF])], D_FF), w["w_ffn_down"], "nn")
    final_rows = [(ffn, [D]), (h1, [D])]

    gw, gp_ = {}, {}
    (d_ffn, d_h1), (gp_["post2_g"], loss) = _rows_bwd("final", _fn_final, [(tg2, [D])], final_rows, [p["post2_g"]], [], [],
                                                      n_sums=1, dtypes=[bf16, f32])
    gw["w_ffn_down"] = _matmul("ffn_down_dw", hmid, d_ffn, "tn", out_dtype=bf16)
    d_gp, d_up = _matmul_then_vjp("ffn_down_dx", d_ffn, w["w_ffn_down"], "nt", _fn_swiglu,
                                  [(gp, [D_FF]), (up, [D_FF])], [bf16, bf16])
    d_u2 = _matmul("ffn_gate_dx", d_gp, w["w_ffn_gate"], "nn")
    d_u2 = _matmul("ffn_up_dx", d_up, w["w_ffn_up"], "nn", add=d_u2)
    gw["w_ffn_gate"] = _matmul("ffn_gate_dw", d_gp, u2, "tn", out_dtype=bf16)
    gw["w_ffn_up"] = _matmul("ffn_up_dw", d_up, u2, "tn", out_dtype=bf16)
    (d_yy, d_x_res), (gp_["post1_g"], gp_["pre2_g"]) = _rows_bwd(
        "post1_bwd", _fn_post1, [], post1_rows, post1_params, [[D], [D]], [d_h1, d_u2], dtypes=[bf16, f32])
    gw["w_o"] = _matmul("out_o_dw", merged, d_yy, "tn", out_dtype=bf16)
    d_a_fox, d_a_rwkv, d_a_mem, d_p_g = _matmul_then_vjp("out_o_dx", d_yy, w["w_o"], "nt", _fn_merge, merge_rows, [bf16] * 4)
    d_fox_out = _matmul("out_fox_dx", d_a_fox, w["w_fox_out"], "nt")
    gw["w_fox_out"] = _matmul("out_fox_dw", fox_out, d_a_fox, "tn", out_dtype=bf16)
    d_rwkv_out = _matmul("out_rwkv_dx", d_a_rwkv, w["w_rwkv_out"], "nt")
    gw["w_rwkv_out"] = _matmul("out_rwkv_dw", rwkv_out, d_a_rwkv, "tn", out_dtype=bf16)
    d_mem_out = _matmul("out_mem_dx", d_a_mem, w["w_mem_out"], "nt")
    gw["w_mem_out"] = _matmul("out_mem_dw", mem_out, d_a_mem, "tn", out_dtype=bf16)

    d_p_mq, d_km, d_vm = _mem_bwd(p_mq, mem_kv, d_mem_out, batch, seq)
    d_mem_kv = jnp.concatenate([d_km, d_vm], axis=1).astype(bf16)
    gw["w_mem_kv"] = _matmul("proj_memkv_dw", memn, d_mem_kv, "tn", out_dtype=bf16)
    d_memn = _matmul("proj_memkv_dx", d_mem_kv, w["w_mem_kv"], "nt")
    _, (gp_["mem_norm_g"],) = _rows_bwd("rms_mem_bwd", _fn_rms, [], [(mem2, [D])], [p["mem_norm_g"]], [[D]], [d_memn])

    d_q, d_k, d_v, d_cq, d_ck = _fox_bwd(p_qkv, c, c_rows, fox_o, lse, d_fox_out, batch, seq)
    d_p_qkv = jnp.concatenate([d_q, d_k, d_v], axis=1).astype(bf16)
    d_p_f, d_bias = _fox_gate_bwd(p_f, bias, d_cq, d_ck, batch, seq)
    gp_["fox_f_bias"] = d_bias[:, :HEADS]

    (d_y_rw, d_main6_post, d_g_rw), (gp_["rwkv_gn_g"], gp_["rwkv_gn_b"], d_rk) = _rows_bwd(
        "rwkv_post_bwd", fn_post, post_consts, post_rows, post_params, [[HW]], [d_rwkv_out], tm=256)
    gp_["rwkv_r_k"] = d_rk.reshape(1, HEADS, HD)
    d_main6, early_got = _scan_bwd(main6, states, d_y_rw, d_main6_post, batch, seq,
                                   side=(early(gw), True) if early else None)

    def fn_pre_sum(*args):
        return _fn_rwkv_pre(*args)

    (d_ps,), d_pre = _rows_bwd("rwkv_pre_bwd", fn_pre_sum, [], [(ps, rw_widths)], pre_params, [six, [HW]],
                               [d_main6, d_g_rw], tm=256)
    gp_["rwkv_w0"], d_w_up, gp_["rwkv_a0"], d_a_up, gw["rwkv_g_up"], gp_["rwkv_k_k"], gp_["rwkv_k_a"] = d_pre
    gw["rwkv_w_up"], gw["rwkv_a_up"] = d_w_up[:64], d_a_up[:64]
    d_p_r, d_mu = _tokshift_bwd(p_r, mu, d_ps, batch, seq)
    gp_["rwkv_mu"] = _unpad_lora(d_mu)

    gw["w_in"] = _merge_w_in(_matmul("proj_qkv_dw", d_p_qkv, u, "tn", out_dtype=bf16), _matmul("proj_f_dw", d_p_f, u, "tn", out_dtype=bf16),
                             _matmul("proj_rwkv_dw", d_p_r, u, "tn", out_dtype=bf16), _matmul("proj_memq_dw", d_p_mq, u, "tn", out_dtype=bf16),
                             _matmul("proj_gate_dw", d_p_g, u, "tn", out_dtype=bf16))
    d_x, gp_["pre1_g"], last_got = _input_cotangent(
        "proj_dx", [d_p_qkv, d_p_f, d_p_r, d_p_mq, d_p_g], [w_qkv, w_f, w_r, w_mq, w_g3], x2, p["pre1_g"], d_x_res,
        side=(last(gw), True) if last else None)
    return loss, d_x.reshape(x.shape), gw, gp_, early_got, last_got


def _adamw(name, recv, row_off, w, m, v):
    _, rows, cols = w.shape
    row_tiles = [t for t in range(16, min(rows, 128) + 1, 16) if rows % t == 0 and row_off % t == 0]
    if row_tiles:
        tr, tc = max(row_tiles), cols
        first, grid = row_off // tr, (rows // tr,)
        at = lambda i: (0, first + i, 0)
        mine = lambda i: (0, i, 0)
    else:
        assert row_off == 0 and recv.shape[1] == rows
        tr, tc = rows, 128
        grid = (cols // tc,)
        at = mine = lambda i: (0, 0, i)

    def body(g_ref, w_ref, m_ref, v_ref, go_ref, d_ref, mo_ref, vo_ref):
        g = g_ref[0].astype(f32)
        for s in range(1, N_DEV):
            g = g + g_ref[s].astype(f32)
        m_new = ADAM_B1 * m_ref[0] + (1.0 - ADAM_B1) * g
        v_new = ADAM_B2 * v_ref[0] + (1.0 - ADAM_B2) * (g * g)
        m_hat = m_new / (1.0 - ADAM_B1 ** ADAM_STEP)
        v_hat = v_new / (1.0 - ADAM_B2 ** ADAM_STEP)
        go_ref[0] = g
        d_ref[0] = -ADAM_LR * (m_hat / (jnp.sqrt(v_hat) + ADAM_EPS) + ADAM_WD * w_ref[0])
        mo_ref[0] = m_new
        vo_ref[0] = v_new

    spec = pl.BlockSpec((1, tr, tc), mine)
    return pl.pallas_call(
        body, name=name, grid=grid,
        in_specs=[pl.BlockSpec((N_DEV, tr, tc), at), spec, spec, spec],
        out_specs=[spec] * 4, out_shape=[jax.ShapeDtypeStruct(w.shape, f32)] * 4,
        compiler_params=_cp(("parallel",)),
    )(recv, w, m, v)


GROUPS = (
    ("in", ("w_in",), 0),
    ("memkv", ("w_mem_kv",), 0),
    ("ffn_gu", ("w_ffn_gate", "w_ffn_up"), 0),
    ("down_o", ("w_ffn_down", "w_o"), 0),
    ("outs", ("w_fox_out", "w_rwkv_out", "w_mem_out"), 0),
    ("lora", ("rwkv_w_up", "rwkv_a_up", "rwkv_g_up"), 0),
)
FIRST_GROUPS = ("in", "memkv")
LATE_GROUPS = (("down_o", "outs", "lora"), ("ffn_gu",))
EARLY_GRAD_GROUPS = ("memkv", "ffn_gu", "down_o", "outs")
LAST_GRAD_GROUPS = ("in", "lora")
SHARD_AXIS = {n: a for n, _, a in SHARDED}
SMALL_ROWS = 16


def _group_local(shards, members, join):
    parts = [shards[n].reshape(shards[n].shape[-2:]) for n in members]
    return parts[0] if len(parts) == 1 else jnp.concatenate(parts, axis=join)


def _group_split(arr, members, join, lead=False):
    out, off = {}, 0
    for n in members:
        shape = dict((k, s) for k, s, _ in SHARDED)[n]
        size = _block_shape(shape, SHARD_AXIS[n])[join]
        idx = [slice(None)] * arr.ndim
        idx[arr.ndim - 2 + join] = slice(off, off + size)
        out[n] = arr[tuple(idx)]
        off += size
    return out


def _full_from_blocks(blocks, axis):
    if axis == 0:
        return blocks.reshape(-1, blocks.shape[2])
    return blocks.transpose(1, 0, 2).reshape(blocks.shape[1], -1)


def _blocks_from_full(full, axis):
    if axis == 0:
        return full.reshape(N_DEV, -1, full.shape[1])
    return full.reshape(full.shape[0], N_DEV, -1).transpose(1, 0, 2)


def _assemble(gathered, names):
    out = {}
    for arr, g in zip(gathered, names):
        _, members, join = [grp for grp in GROUPS if grp[0] == g][0]
        for n, blk in _group_split(arr, members, join, lead=True).items():
            out[n] = _full_from_blocks(blk, SHARD_AXIS[n])
    return out


def _grad_blocks(gw, names):
    out = []
    for g in names:
        _, members, join = [grp for grp in GROUPS if grp[0] == g][0]
        parts = [_blocks_from_full(gw[n].astype(bf16), SHARD_AXIS[n]) for n in members]
        out.append(parts[0] if len(parts) == 1 else jnp.concatenate(parts, axis=1 + join))
    return out


def _small_pack(d):
    flat = jnp.concatenate([d[n].reshape(-1) for n, _ in REPLICATED])
    return jnp.pad(flat, (0, SMALL_ROWS * LANES - REPL_ELEMS)).reshape(SMALL_ROWS, LANES)


def _small_unpack(packed):
    out, flat, off = {}, packed.reshape(-1), 0
    for n, shape in REPLICATED:
        k = _rows_of((LANES,) + shape)
        out[n] = flat[off:off + k].reshape(shape)
        off += k
    return out


def kernel(x, mem, pre1_g, post1_g, pre2_g, post2_g, mem_norm_g, w_in, fox_f_bias, rwkv_mu, rwkv_w0, rwkv_w_up, rwkv_a0, rwkv_a_up, rwkv_g_up, rwkv_k_k, rwkv_k_a, rwkv_r_k, rwkv_gn_g, rwkv_gn_b, w_mem_kv, w_fox_out, w_rwkv_out, w_mem_out, w_o, w_ffn_gate, w_ffn_up, w_ffn_down, loss_target, m_pre1_g, m_post1_g, m_pre2_g, m_post2_g, m_mem_norm_g, m_w_in, m_fox_f_bias, m_rwkv_mu, m_rwkv_w0, m_rwkv_w_up, m_rwkv_a0, m_rwkv_a_up, m_rwkv_g_up, m_rwkv_k_k, m_rwkv_k_a, m_rwkv_r_k, m_rwkv_gn_g, m_rwkv_gn_b, m_w_mem_kv, m_w_fox_out, m_w_rwkv_out, m_w_mem_out, m_w_o, m_w_ffn_gate, m_w_ffn_up, m_w_ffn_down, v_pre1_g, v_post1_g, v_pre2_g, v_post2_g, v_mem_norm_g, v_w_in, v_fox_f_bias, v_rwkv_mu, v_rwkv_w0, v_rwkv_w_up, v_rwkv_a0, v_rwkv_a_up, v_rwkv_g_up, v_rwkv_k_k, v_rwkv_k_a, v_rwkv_r_k, v_rwkv_gn_g, v_rwkv_gn_b, v_w_mem_kv, v_w_fox_out, v_w_rwkv_out, v_w_mem_out, v_w_o, v_w_ffn_gate, v_w_ffn_up, v_w_ffn_down):
    args = dict(locals())
    turn = lambda n, a: jnp.swapaxes(a, 1, 2) if n in TRANSPOSED else a
    wts = {n: turn(n, args[n]) for n in WEIGHT_ORDER}
    ms = {n: turn(n, args["m_" + n]) for n in WEIGHT_ORDER}
    vs = {n: turn(n, args["v_" + n]) for n in WEIGHT_ORDER}

    groups = {g: (members, join) for g, members, join in GROUPS}
    w_bf16 = {n: wts[n].astype(bf16) for n, _, _ in SHARDED}

    def send(g):
        return _group_local(w_bf16, *groups[g])

    first = _exchange("gather_first", [send(g) for g in FIRST_GROUPS], per_peer=False)
    full = _assemble(first, FIRST_GROUPS)
    small_in = {n: (wts[n] if n == "rwkv_r_k" else wts[n].reshape(wts[n].shape[-2:])) for n, _ in REPLICATED}
    late = ([send(g) for g in LATE_GROUPS[0]], [send(g) for g in LATE_GROUPS[1]],
            lambda got, which: _assemble(got, LATE_GROUPS[which]))
    loss_part, grad_x, gw, gp, early_got, last_got = _local_step(
        x, mem, loss_target, full, small_in, late=late, early=lambda g: _grad_blocks(g, EARLY_GRAD_GROUPS),
        last=lambda g: _grad_blocks(g, LAST_GRAD_GROUPS))
    (small_got,) = _exchange("exchange_small", [_small_pack(gp).astype(bf16)], per_peer=False)
    received = dict(zip(EARLY_GRAD_GROUPS + LAST_GRAD_GROUPS, list(early_got) + list(last_got)))

    outs = [{}, {}, {}, {}]
    for g, members, _ in GROUPS:
        off = 0
        for n in members:
            for o, arr in zip(outs, _adamw("adamw_" + n, received[g], off, wts[n], ms[n], vs[n])):
                o[n] = arr
            off += wts[n].shape[1]
    res = _adamw("adamw_small", small_got, 0, *[_small_pack(d)[None] for d in (wts, ms, vs)])
    for o, arr in zip(outs, res):
        o.update(_small_unpack(arr))
    loss = lax.psum(loss_part[0, 0], ("x", "y", "c"))
    return (loss, grad_x, *[turn(n, o[n].reshape(wts[n].shape)) for o in outs for n in WEIGHT_ORDER])
```

```python
import functools

import jax
import jax.numpy as jnp
from jax import lax
from jax.experimental import pallas as pl
from jax.experimental.pallas import tpu as pltpu

f32 = jnp.float32
bf16 = jnp.bfloat16
_HI = lax.Precision.HIGHEST

D = 1024
HEADS = 8
HD = 64
HW = HEADS * HD
MEM_HEADS = 4
MEM_HD = 128
MEM_W = 512
MEM_LEN = 256
D_FF = 2816
LORA_PAD = 128
NORM_EPS = 1e-6
GN_EPS = 64e-5
SCAN_CHUNK = 64
N_DEV = 8
LANES = 1024
VMEM_LIMIT = 56 * 1024 * 1024

ADAM_LR = 0.001
ADAM_B1 = 0.9
ADAM_B2 = 0.999
ADAM_EPS = 1e-08
ADAM_WD = 0.01
ADAM_STEP = 10

TRANSPOSED = ("w_in", "w_ffn_gate", "w_ffn_up")
SHARDED = (
    ("w_in", (6920, 1024), 0),
    ("w_ffn_gate", (2816, 1024), 0),
    ("w_ffn_up", (2816, 1024), 0),
    ("w_ffn_down", (2816, 1024), 0),
    ("w_mem_kv", (1024, 1024), 0),
    ("w_o", (1024, 1024), 0),
    ("w_fox_out", (512, 1024), 1),
    ("w_rwkv_out", (512, 1024), 1),
    ("w_mem_out", (512, 1024), 1),
    ("rwkv_w_up", (64, 512), 1),
    ("rwkv_a_up", (64, 512), 1),
    ("rwkv_g_up", (128, 512), 1),
)
REPLICATED = (
    ("pre1_g", (1, 1024)), ("post1_g", (1, 1024)), ("pre2_g", (1, 1024)), ("post2_g", (1, 1024)),
    ("mem_norm_g", (1, 1024)), ("fox_f_bias", (1, 8)), ("rwkv_mu", (1, 1792)), ("rwkv_w0", (1, 512)),
    ("rwkv_a0", (1, 512)), ("rwkv_k_k", (1, 512)), ("rwkv_k_a", (1, 512)), ("rwkv_r_k", (1, 8, 64)),
    ("rwkv_gn_g", (1, 512)), ("rwkv_gn_b", (1, 512)),
)
WEIGHT_ORDER = ('pre1_g', 'post1_g', 'pre2_g', 'post2_g', 'mem_norm_g', 'w_in', 'fox_f_bias', 'rwkv_mu',
                'rwkv_w0', 'rwkv_w_up', 'rwkv_a0', 'rwkv_a_up', 'rwkv_g_up', 'rwkv_k_k', 'rwkv_k_a',
                'rwkv_r_k', 'rwkv_gn_g', 'rwkv_gn_b', 'w_mem_kv', 'w_fox_out', 'w_rwkv_out', 'w_mem_out',
                'w_o', 'w_ffn_gate', 'w_ffn_up', 'w_ffn_down')


def _block_shape(shape, axis):
    return tuple(s // N_DEV if i == axis else s for i, s in enumerate(shape))


def _rows_of(shape):
    n = 1
    for s in shape:
        n *= s
    return n // LANES


REPL_ELEMS = sum(_rows_of((LANES,) + s) for _, s in REPLICATED)


def _cp(sem=None):
    return pltpu.CompilerParams(dimension_semantics=sem, vmem_limit_bytes=VMEM_LIMIT)


def _tile(dim, cap):
    best = None
    for t in range(128, min(dim, cap) + 1, 128):
        if dim % t == 0:
            best = t
    return best if best is not None else dim


def _two_terms(x):
    hi = x.astype(bf16)
    return hi, (x - hi.astype(f32)).astype(bf16)


def _dg(a, b, dims, exact):
    if exact == "split":
        (a_hi, a_lo), (b_hi, b_lo) = _two_terms(a), _two_terms(b)
        dot = functools.partial(lax.dot_general, dimension_numbers=dims, preferred_element_type=f32)
        return dot(a_hi, b_hi) + (dot(a_hi, b_lo) + dot(a_lo, b_hi))
    if exact:
        return lax.dot_general(a, b, dims, precision=_HI, preferred_element_type=f32)
    return lax.dot_general(a.astype(bf16), b.astype(bf16), dims, preferred_element_type=f32)


def _make_mm(batched, exact):
    o = 1 if batched else 0
    bd = ((0,), (0,)) if batched else ((), ())
    d_nn = (((1 + o,), (o,)), bd)
    d_nt = (((1 + o,), (1 + o,)), bd)
    d_tn = (((o,), (o,)), bd)

    @jax.custom_vjp
    def nn(a, b):
        return _dg(a, b, d_nn, exact)

    @jax.custom_vjp
    def nt(a, b):
        return _dg(a, b, d_nt, exact)

    @jax.custom_vjp
    def tn(a, b):
        return _dg(a, b, d_tn, exact)

    nn.defvjp(lambda a, b: (_dg(a, b, d_nn, exact), (a, b)),
              lambda res, g: (_dg(g, res[1], d_nt, exact), _dg(res[0], g, d_tn, exact)))
    nt.defvjp(lambda a, b: (_dg(a, b, d_nt, exact), (a, b)),
              lambda res, g: (_dg(g, res[1], d_nn, exact), _dg(g, res[0], d_tn, exact)))
    tn.defvjp(lambda a, b: (_dg(a, b, d_tn, exact), (a, b)),
              lambda res, g: (_dg(res[1], g, d_nt, exact), _dg(res[0], g, d_nn, exact)))
    return nn, nt, tn


def _sigmoid(x):
    return 1.0 / (1.0 + jnp.exp(-x))


def _head_sum_raw(x):
    width = 2 * HD
    i = lax.broadcasted_iota(jnp.int32, (width, width), 0) // HD
    j = lax.broadcasted_iota(jnp.int32, (width, width), 1) // HD
    m = (i == j).astype(bf16)
    dims = (((1,), (0,)), ((), ()))
    out = []
    for p in range(x.shape[1] // width):
        xp = x[:, p * width:(p + 1) * width]
        hi = xp.astype(bf16)
        lo = (xp - hi.astype(f32)).astype(bf16)
        out.append(lax.dot_general(hi, m, dims, preferred_element_type=f32)
                   + lax.dot_general(lo, m, dims, preferred_element_type=f32))
    return jnp.concatenate(out, axis=1)


@jax.custom_vjp
def _head_sum(x):
    return _head_sum_raw(x)


_head_sum.defvjp(lambda x: (_head_sum_raw(x), None), lambda _, g: (_head_sum_raw(g),))


WEIGHT_TILE_BYTES = 13 * 512 * 1024
ACC_TILE_BYTES = 8 * 1024 * 1024


def _lazy(fn, rows, width, params=()):
    return (fn, rows, width, list(params))


def _matmul(name, a, b, mode, add=None, out_dtype=f32):
    has_add = add is not None
    if isinstance(a, tuple):
        a_fn, a_rows, a_width, a_params = a
        a_arrays = [r for r, _ in a_rows]
        a_shape = (a_arrays[0].shape[0], a_width)
    else:
        a_fn, a_rows, a_params, a_arrays, a_shape = None, None, [], [a], a.shape
    n_r = len(a_arrays)
    n_a = n_r + len(a_params)

    def load_a(refs):
        if a_fn is None:
            return refs[0][...].astype(bf16)
        pieces = []
        for r, (_, widths) in zip(refs[:n_r], a_rows):
            pieces += _pieces(r, widths)
        return a_fn(*pieces, *[p[...] for p in refs[n_r:]])[0].astype(bf16)

    if mode == "tn":
        assert a_fn is None
        (k, m), (_, n) = a_shape, b.shape
        tn = _tile(n, max(128, ACC_TILE_BYTES // (4 * m)))
        tk = _tile(k, 1024)
        nk = k // tk

        def body(*refs):
            b_ref, o_ref, acc = refs[n_a:]

            @pl.when(pl.program_id(1) == 0)
            def _():
                acc[...] = jnp.zeros_like(acc)

            acc[...] += lax.dot_general(load_a(refs[:n_a]), b_ref[...].astype(bf16),
                                        (((0,), (0,)), ((), ())), preferred_element_type=f32)

            @pl.when(pl.program_id(1) == nk - 1)
            def _():
                o_ref[...] = acc[...].astype(o_ref.dtype)

        return pl.pallas_call(
            body, name=name, grid=(n // tn, nk),
            in_specs=[pl.BlockSpec((tk, r.shape[1]), lambda j, kk: (kk, 0)) for r in a_arrays]
            + [pl.BlockSpec((tk, tn), lambda j, kk: (kk, j))],
            out_specs=pl.BlockSpec((m, tn), lambda j, kk: (0, j)), out_shape=jax.ShapeDtypeStruct((m, n), out_dtype),
            scratch_shapes=[pltpu.VMEM((m, tn), f32)],
            compiler_params=_cp(("parallel", "arbitrary")),
        )(*a_arrays, b)

    (m, k) = a_shape
    n = b.shape[1] if mode == "nn" else b.shape[0]
    tm = _tile(m, 1024 if a_fn is None else 512)
    tn = _tile(n, max(128, WEIGHT_TILE_BYTES // (2 * k)))
    dims = (((1,), (0,)), ((), ())) if mode == "nn" else (((1,), (1,)), ((), ()))
    b_spec = pl.BlockSpec((k, tn), lambda j, i: (0, j)) if mode == "nn" else pl.BlockSpec((tn, k), lambda j, i: (j, 0))
    o_spec = pl.BlockSpec((tm, tn), lambda j, i: (i, j))

    keep = a_fn is not None
    assert not keep or tn == n

    def body(*refs):
        b_ref = refs[n_a]
        a_val = load_a(refs[:n_a])
        r = lax.dot_general(a_val, b_ref[...].astype(bf16), dims, preferred_element_type=f32)
        if has_add:
            r = r + refs[n_a + 1][...]
        if keep:
            refs[-2][...] = r.astype(refs[-2].dtype)
            refs[-1][...] = a_val
        else:
            refs[-1][...] = r.astype(refs[-1].dtype)

    res = pl.pallas_call(
        body, name=name, grid=(n // tn, m // tm),
        in_specs=[pl.BlockSpec((tm, r.shape[1]), lambda j, i: (i, 0)) for r in a_arrays]
        + [pl.BlockSpec(p.shape, lambda j, i: (0, 0)) for p in a_params] + [b_spec] + ([o_spec] if has_add else []),
        out_specs=[o_spec] + ([pl.BlockSpec((tm, k), lambda j, i: (i, 0))] if keep else []),
        out_shape=[jax.ShapeDtypeStruct((m, n), out_dtype)] + ([jax.ShapeDtypeStruct((m, k), bf16)] if keep else []),
        compiler_params=_cp(("parallel", "arbitrary")),
    )(*a_arrays, *a_params, b, *([add] if has_add else []))
    return tuple(res) if keep else res[0]


def _input_cotangent(name, a_list, b_list, x, gain, add, side=None):
    m = a_list[0].shape[0]
    tm = _tile(m, 256)
    n_g = len(a_list)
    srcs, per_peer = side if side is not None else ([], False)
    n_s = len(srcs)

    def body(*refs):
        x_ref, g_ref, add_ref = refs[2 * n_g:2 * n_g + 3]
        src_refs = refs[2 * n_g + 3:2 * n_g + 3 + n_s]
        dx_ref, dg_ref = refs[2 * n_g + 3 + n_s:2 * n_g + 5 + n_s]
        _side_exchange(src_refs, refs[2 * n_g + 5 + n_s:2 * n_g + 5 + 2 * n_s], per_peer, refs[2 * n_g + 5 + 2 * n_s:], m // tm)
        d_u = None
        for g in range(n_g):
            r = lax.dot_general(refs[g][...].astype(bf16), refs[n_g + g][...].astype(bf16), (((1,), (0,)), ((), ())),
                                preferred_element_type=f32)
            d_u = r if d_u is None else d_u + r
        _, vjp = jax.vjp(_rms, x_ref[...], g_ref[...])
        d_x, d_gain = vjp(d_u)
        dx_ref[...] = d_x + add_ref[...]

        @pl.when(pl.program_id(0) == 0)
        def _():
            dg_ref[...] = jnp.zeros_like(dg_ref)

        dg_ref[...] += d_gain

    rows = pl.BlockSpec((tm, x.shape[1]), lambda i: (i, 0))
    whole = lambda b: pl.BlockSpec(b.shape, lambda i: (0, 0))
    res = pl.pallas_call(
        body, name=name, grid=(m // tm,),
        in_specs=[pl.BlockSpec((tm, a.shape[1]), lambda i: (i, 0)) for a in a_list] + [whole(b) for b in b_list]
        + [rows, whole(gain), rows] + [_HBM_SPEC] * n_s,
        out_specs=[rows, whole(gain)] + [_HBM_SPEC] * n_s,
        out_shape=[jax.ShapeDtypeStruct(x.shape, f32), jax.ShapeDtypeStruct(gain.shape, f32)] + _side_out_shapes(srcs, per_peer),
        scratch_shapes=_side_sems(n_s),
        compiler_params=_cp(("arbitrary",)),
    )(*a_list, *b_list, x, gain, add, *srcs)
    return res[0], res[1], list(res[2:])


def _pieces(ref, widths):
    out, off = [], 0
    for w in widths:
        out.append(ref[:, off:off + w].astype(f32))
        off += w
    return out


def _store_pieces(ref, widths, vals, add_ref=None):
    off = 0
    for w, v in zip(widths, vals):
        ref[:, off:off + w] = (v if add_ref is None else v + add_ref[:, off:off + w]).astype(ref.dtype)
        off += w


def _rows_fwd(name, fn, consts, rows, params, outs, n_sums=0, tm=512, dtypes=None):
    t = (consts + rows)[0][0].shape[0]
    tm = min(tm, t)
    ins = consts + rows
    n_in, n_p, n_o = len(ins), len(params), len(outs)
    dtypes = dtypes or [f32] * n_o

    def body(*refs):
        in_refs, p_refs = refs[:n_in], refs[n_in:n_in + n_p]
        o_refs, s_refs = refs[n_in + n_p:n_in + n_p + n_o], refs[n_in + n_p + n_o:]
        vals = []
        for r, (_, widths) in zip(in_refs, ins):
            vals += _pieces(r, widths)
        res = fn(*vals, *[p[...] for p in p_refs])
        pos = 0
        for r, widths in zip(o_refs, outs):
            _store_pieces(r, widths, res[pos:pos + len(widths)])
            pos += len(widths)

        @pl.when(pl.program_id(0) == 0)
        def _():
            for s in s_refs:
                s[...] = jnp.zeros_like(s)

        for s, v in zip(s_refs, res[pos:]):
            s[...] += v

    row_spec = lambda w: pl.BlockSpec((tm, w), lambda i: (i, 0))
    full = lambda p: pl.BlockSpec(p.shape, lambda i: (0,) * p.ndim)
    return pl.pallas_call(
        body, name=name, grid=(t // tm,),
        in_specs=[row_spec(a.shape[1]) for a, _ in ins] + [full(p) for p in params],
        out_specs=[row_spec(sum(w)) for w in outs] + [pl.BlockSpec((1, 1), lambda i: (0, 0))] * n_sums,
        out_shape=[jax.ShapeDtypeStruct((t, sum(w)), dt) for w, dt in zip(outs, dtypes)] + [jax.ShapeDtypeStruct((1, 1), f32)] * n_sums,
        compiler_params=_cp(("arbitrary",)),
    )(*[a for a, _ in ins], *params)


def _rows_bwd(name, fn, consts, rows, params, outs, cts, n_sums=0, add=None, tm=512, dtypes=None):
    t = (consts + rows)[0][0].shape[0]
    tm = min(tm, t)
    n_c, n_r, n_p, n_o = len(consts), len(rows), len(params), len(outs)
    has_add = add is not None
    dtypes = dtypes or [f32] * n_r

    def body(*refs):
        pos = 0
        c_refs = refs[pos:pos + n_c]; pos += n_c
        r_refs = refs[pos:pos + n_r]; pos += n_r
        p_refs = refs[pos:pos + n_p]; pos += n_p
        ct_refs = refs[pos:pos + n_o]; pos += n_o
        add_ref = refs[pos] if has_add else None
        pos += 1 if has_add else 0
        dr_refs = refs[pos:pos + n_r]; pos += n_r
        dp_refs = refs[pos:pos + n_p]; pos += n_p
        s_refs = refs[pos:pos + n_sums]
        cvals, rvals = [], []
        for r, (_, widths) in zip(c_refs, consts):
            cvals += _pieces(r, widths)
        for r, (_, widths) in zip(r_refs, rows):
            rvals += _pieces(r, widths)
        pvals = [p[...] for p in p_refs]
        ctv = []
        for r, widths in zip(ct_refs, outs):
            ctv += _pieces(r, widths)
        ctv += [jnp.ones((1, 1), f32)] * n_sums
        primal, vjp = jax.vjp(lambda *rp: tuple(fn(*cvals, *rp)), *rvals, *pvals)
        g = vjp(tuple(ctv))
        pos = 0
        for idx, (r, (_, widths)) in enumerate(zip(dr_refs, rows)):
            _store_pieces(r, widths, g[pos:pos + len(widths)], add_ref if idx == 0 else None)
            pos += len(widths)

        @pl.when(pl.program_id(0) == 0)
        def _():
            for acc in list(dp_refs) + list(s_refs):
                acc[...] = jnp.zeros_like(acc)

        for dp, v in zip(dp_refs, g[pos:]):
            dp[...] += v
        for s, v in zip(s_refs, primal[len(primal) - n_sums:]):
            s[...] += v

    row_spec = lambda w: pl.BlockSpec((tm, w), lambda i: (i, 0))
    full = lambda p: pl.BlockSpec(p.shape, lambda i: (0,) * p.ndim)
    args = [a for a, _ in consts + rows] + list(params) + list(cts) + ([add] if has_add else [])
    res = pl.pallas_call(
        body, name=name, grid=(t // tm,),
        in_specs=[row_spec(a.shape[1]) for a, _ in consts + rows] + [full(p) for p in params]
        + [row_spec(sum(w)) for w in outs] + ([row_spec(add.shape[1])] if has_add else []),
        out_specs=[row_spec(a.shape[1]) for a, _ in rows] + [full(p) for p in params]
        + [pl.BlockSpec((1, 1), lambda i: (0, 0))] * n_sums,
        out_shape=[jax.ShapeDtypeStruct(a.shape, dt) for (a, _), dt in zip(rows, dtypes)]
        + [jax.ShapeDtypeStruct(p.shape, f32) for p in params] + [jax.ShapeDtypeStruct((1, 1), f32)] * n_sums,
        compiler_params=_cp(("arbitrary",)),
    )(*args)
    return res[:n_r], res[n_r:n_r + n_p] + res[n_r + n_p:]


def _matmul_then_vjp(name, a, b, mode, fn, rows, dtypes, tm=256):
    m, k = a.shape
    tm = min(tm, m)
    dims = (((1,), (0,)), ((), ())) if mode == "nn" else (((1,), (1,)), ((), ()))
    n_r = len(rows)

    def body(*refs):
        a_ref, b_ref = refs[:2]
        ct = lax.dot_general(a_ref[...].astype(bf16), b_ref[...].astype(bf16), dims, preferred_element_type=f32)
        rvals = []
        for r, (_, widths) in zip(refs[2:2 + n_r], rows):
            rvals += _pieces(r, widths)
        _, vjp = jax.vjp(lambda *rp: fn(*rp)[0], *rvals)
        g = vjp(ct)
        pos = 0
        for r, (_, widths) in zip(refs[2 + n_r:], rows):
            _store_pieces(r, widths, g[pos:pos + len(widths)])
            pos += len(widths)

    row_spec = lambda w: pl.BlockSpec((tm, w), lambda i: (i, 0))
    return pl.pallas_call(
        body, name=name, grid=(m // tm,),
        in_specs=[row_spec(k), pl.BlockSpec(b.shape, lambda i: (0, 0))] + [row_spec(r.shape[1]) for r, _ in rows],
        out_specs=[row_spec(r.shape[1]) for r, _ in rows],
        out_shape=[jax.ShapeDtypeStruct(r.shape, dt) for (r, _), dt in zip(rows, dtypes)],
        compiler_params=_cp(("parallel",)),
    )(a, b, *[r for r, _ in rows])


def _rms(x, g):
    return x * lax.rsqrt(jnp.mean(x * x, axis=-1, keepdims=True) + NORM_EPS) * g


def _fn_rms(x, g):
    return (_rms(x, g),)


def _fn_rwkv_pre(r, k, v, wd, ad, gd, w0, w_up, a0, a_up, g_up, k_k, k_a):
    nn, _, _ = _make_mm(False, False)
    w_log = -_sigmoid(w0 + nn(jnp.tanh(wd), w_up)) * 0.6065306597126334
    a = _sigmoid(a0 + nn(ad, a_up))
    g = nn(_sigmoid(gd), g_up)
    kk = k * k_k
    kk = kk * lax.rsqrt(jnp.maximum(_head_sum(kk * kk), 1e-24))
    k2 = k * (1.0 + (a - 1.0) * k_a)
    return r, w_log, k2, v, -kk, kk * a, g


def _fn_rwkv_post(y, r, k2, v, g, gn_g, gn_b, r_k):
    mean = _head_sum(y) * (1.0 / HD)
    yc = y - mean
    var = _head_sum(yc * yc) * (1.0 / HD)
    yn = yc * lax.rsqrt(var + GN_EPS) * gn_g + gn_b
    bonus = _head_sum(r * k2 * r_k) * v
    return ((yn + bonus) * g,)


def _fn_merge(a_fox, a_rwkv, a_mem, g_fox, g_rwkv, g_mem):
    return (_sigmoid(g_fox) * a_fox + _sigmoid(g_rwkv) * a_rwkv + _sigmoid(g_mem) * a_mem,)


def _fn_post1(y, x, post1_g, pre2_g):
    h1 = x + _rms(y, post1_g)
    return h1, _rms(h1, pre2_g)


def _fn_swiglu(gp, up):
    return (gp * _sigmoid(gp) * up,)


def _fn_final(target, ffn, h1, post2_g):
    err = h1 + _rms(ffn, post2_g) - target
    per_row = jnp.mean(err * err, axis=-1, keepdims=True)
    return (0.5 * jnp.sum(per_row, axis=0, keepdims=True),)


def _shift_down(x):
    row = lax.broadcasted_iota(jnp.int32, x.shape, 0)
    return jnp.where(row == 0, 0.0, pltpu.roll(x, 1, 0))


def _shift_up(x):
    s = x.shape[0]
    row = lax.broadcasted_iota(jnp.int32, x.shape, 0)
    return jnp.where(row == s - 1, 0.0, pltpu.roll(x, s - 1, 0))


def _tokshift_fwd(p, mu, batch, seq):
    w = p.shape[1]
    tc = _tile(w, 384)

    def body(p_ref, mu_ref, o_ref):
        x = p_ref[...]
        o_ref[...] = x + (_shift_down(x) - x) * mu_ref[...]

    return pl.pallas_call(
        body, name="tokshift_fwd", grid=(w // tc, batch),
        in_specs=[pl.BlockSpec((seq, tc), lambda j, b: (b, j)), pl.BlockSpec((1, tc), lambda j, b: (0, j))],
        out_specs=pl.BlockSpec((seq, tc), lambda j, b: (b, j)),
        out_shape=jax.ShapeDtypeStruct(p.shape, f32),
        compiler_params=_cp(("parallel", "arbitrary")),
    )(p, mu)


def _tokshift_bwd(p, mu, dps, batch, seq):
    w = p.shape[1]
    tc = _tile(w, 384)

    def body(p_ref, mu_ref, d_ref, dp_ref, dmu_ref):
        x, mu_v, d = p_ref[...], mu_ref[...], d_ref[...]
        dp_ref[...] = (d * (1.0 - mu_v) + _shift_up(d * mu_v)).astype(dp_ref.dtype)

        @pl.when(pl.program_id(1) == 0)
        def _():
            dmu_ref[...] = jnp.zeros_like(dmu_ref)

        dmu_ref[...] += jnp.sum(d * (_shift_down(x) - x), axis=0, keepdims=True)

    return pl.pallas_call(
        body, name="tokshift_bwd", grid=(w // tc, batch),
        in_specs=[pl.BlockSpec((seq, tc), lambda j, b: (b, j)), pl.BlockSpec((1, tc), lambda j, b: (0, j)),
                  pl.BlockSpec((seq, tc), lambda j, b: (b, j))],
        out_specs=[pl.BlockSpec((seq, tc), lambda j, b: (b, j)), pl.BlockSpec((1, tc), lambda j, b: (0, j))],
        out_shape=[jax.ShapeDtypeStruct(p.shape, bf16), jax.ShapeDtypeStruct(mu.shape, f32)],
        compiler_params=_cp(("parallel", "arbitrary")),
    )(p, mu, dps)


def _cum_block(seq):
    return _tile(seq, 256)


def _fox_gate_fwd(f, bias, batch, seq):
    cb = _cum_block(seq)

    def body(f_ref, b_ref, c_ref):
        row = lax.broadcasted_iota(jnp.int32, (cb, cb), 0)
        col = lax.broadcasted_iota(jnp.int32, (cb, cb), 1)
        tri = (col <= row).astype(f32)
        carry = jnp.zeros((1, 128), f32)
        for i in range(seq // cb):
            z = f_ref[i * cb:(i + 1) * cb, :] + b_ref[...]
            ls = jnp.minimum(z, 0.0) - jnp.log(1.0 + jnp.exp(-jnp.abs(z)))
            c = _dg(tri, ls, (((1,), (0,)), ((), ())), True) + carry
            c_ref[i * cb:(i + 1) * cb, :] = c
            carry = c[cb - 1:cb, :]

    return pl.pallas_call(
        body, name="fox_gate_fwd", grid=(batch,),
        in_specs=[pl.BlockSpec((seq, 128), lambda b: (b, 0)), pl.BlockSpec((1, 128), lambda b: (0, 0))],
        out_specs=pl.BlockSpec((seq, 128), lambda b: (b, 0)),
        out_shape=jax.ShapeDtypeStruct(f.shape, f32),
        compiler_params=_cp(("arbitrary",)),
    )(f, bias)


def _fox_gate_bwd(f, bias, dc_a, dc_b, batch, seq):
    cb = _cum_block(seq)

    def body(f_ref, b_ref, da_ref, db_ref, df_ref, dbias_ref):
        row = lax.broadcasted_iota(jnp.int32, (cb, cb), 0)
        col = lax.broadcasted_iota(jnp.int32, (cb, cb), 1)
        triu = (col >= row).astype(f32)

        @pl.when(pl.program_id(0) == 0)
        def _():
            dbias_ref[...] = jnp.zeros_like(dbias_ref)

        lane = lax.broadcasted_iota(jnp.int32, (1, 128), 1)

        def by_head(blk):
            out = jnp.zeros((cb, 128), f32)
            for p in range(HEADS // 2):
                for e in range(2):
                    out = jnp.where(lane == 2 * p + e, _pick_lane(blk[:, p * 128:(p + 1) * 128], e), out)
            return out

        carry = jnp.zeros((1, 128), f32)
        tot = jnp.zeros((1, 128), f32)
        for i in reversed(range(seq // cb)):
            sl = slice(i * cb, (i + 1) * cb)
            dc = by_head(da_ref[sl, :] + db_ref[sl, :])
            dls = _dg(triu, dc, (((1,), (0,)), ((), ())), True) + carry
            carry = dls[0:1, :]
            df = dls * _sigmoid(-(f_ref[sl, :] + b_ref[...]))
            df_ref[sl, :] = df.astype(df_ref.dtype)
            tot = tot + jnp.sum(df, axis=0, keepdims=True)
        dbias_ref[...] += tot

    return pl.pallas_call(
        body, name="fox_gate_bwd", grid=(batch,),
        in_specs=[pl.BlockSpec((seq, 128), lambda b: (b, 0)), pl.BlockSpec((1, 128), lambda b: (0, 0)),
                  pl.BlockSpec((seq, HW), lambda b: (b, 0)), pl.BlockSpec((seq, HW), lambda b: (b, 0))],
        out_specs=[pl.BlockSpec((seq, 128), lambda b: (b, 0)), pl.BlockSpec((1, 128), lambda b: (0, 0))],
        out_shape=[jax.ShapeDtypeStruct(f.shape, bf16), jax.ShapeDtypeStruct((1, 128), f32)],
        compiler_params=_cp(("arbitrary",)),
    )(f, bias, dc_a, dc_b)


_HBM_SPEC = pl.BlockSpec(memory_space=pltpu.HBM)


def _side_out_shapes(srcs, per_peer):
    return [jax.ShapeDtypeStruct(((N_DEV,) + tuple(s.shape[1:] if per_peer else s.shape)), s.dtype) for s in srcs]


def _side_sems(n):
    if n == 0:
        return []
    return [pltpu.SemaphoreType.DMA((n, N_DEV - 1)), pltpu.SemaphoreType.DMA((n, N_DEV - 1)), pltpu.SemaphoreType.DMA((n,))]


def _peer_copies(src_refs, dst_refs, per_peer, sems):
    send_sems, recv_sems, local_sems = sems
    x, y, c = lax.axis_index("x"), lax.axis_index("y"), lax.axis_index("c")
    me = 4 * x + 2 * y + c

    def remote(src, dst, t, k, to):
        return pltpu.make_async_remote_copy(src_ref=src, dst_ref=dst, send_sem=send_sems.at[t, k - 1],
                                            recv_sem=recv_sems.at[t, k - 1], device_id=to,
                                            device_id_type=pl.DeviceIdType.MESH)

    direct, relays = [], []
    for t, (s, d) in enumerate(zip(src_refs, dst_refs)):
        direct.append((t, 0, pltpu.make_async_copy(s.at[me] if per_peer else s, d.at[me], local_sems.at[t])))
        for k in range(1, N_DEV):
            px = 1 - x if k & 4 else x
            py = 1 - y if k & 2 else y
            pc = 1 - c if k & 1 else c
            if per_peer:
                direct.append((t, k, remote(s.at[4 * px + 2 * py + pc], d.at[me], t, k, (px, py, pc))))
            elif k == 1 or not k & 1:
                direct.append((t, k, remote(s, d.at[me], t, k, (px, py, pc))))
            else:
                origin = d.at[4 * px + 2 * py + c]
                relays.append((t, k - 1, remote(origin, origin, t, k, (x, y, 1 - c))))
    return direct, relays


def _exchange_start(direct):
    for _, _, cp in direct:
        cp.start()


def _exchange_relay(direct, relays):
    landed = {(t, k): cp for t, k, cp in direct}
    for t, j, cp in relays:
        landed[(t, j)].wait_recv()
        cp.start()


def _exchange_finish(direct, relays):
    relayed = {(t, j) for t, j, _ in relays}
    for t, k, cp in direct:
        if k == 0:
            cp.wait()
        else:
            cp.wait_send()
            if (t, k) not in relayed:
                cp.wait_recv()
    for _, _, cp in relays:
        cp.wait()


def _side_exchange(src_refs, dst_refs, per_peer, sems, *grid):
    if not src_refs:
        return
    step, total = 0, 1
    for a, n in enumerate(grid):
        step, total = step * n + pl.program_id(a), total * n

    @pl.when(step == 0)
    def _():
        _exchange_start(_peer_copies(src_refs, dst_refs, per_peer, sems)[0])

    @pl.when(step == (3 * total) // 4)
    def _():
        _exchange_relay(*_peer_copies(src_refs, dst_refs, per_peer, sems))

    @pl.when(step == total - 1)
    def _():
        _exchange_finish(*_peer_copies(src_refs, dst_refs, per_peer, sems))


def _exchange(name, srcs, per_peer):
    n = len(srcs)

    def body(*refs):
        direct, relays = _peer_copies(refs[:n], refs[n:2 * n], per_peer, refs[2 * n:])
        _exchange_start(direct)
        _exchange_relay(direct, relays)
        _exchange_finish(direct, relays)

    return pl.pallas_call(
        body, name=name, in_specs=[_HBM_SPEC] * n, out_specs=[_HBM_SPEC] * n,
        out_shape=_side_out_shapes(srcs, per_peer), scratch_shapes=_side_sems(n),
    )(*srcs)


FOX_T = 512
_NEG = -1e30
_D2 = (((1,), (1,)), ((), ()))
_D1 = (((1,), (0,)), ((), ()))
_D0 = (((0,), (0,)), ((), ()))


def _bdot(a, b, dims):
    return lax.dot_general(a.astype(bf16), b.astype(bf16), dims, preferred_element_type=f32)


def _pick_lane(x, lane):
    idx = lax.broadcasted_iota(jnp.int32, x.shape, 1)
    return jnp.sum(jnp.where(idx == lane, x, 0.0), axis=1, keepdims=True)


def _pick_row(x, row):
    idx = lax.broadcasted_iota(jnp.int32, x.shape, 0)
    return jnp.sum(jnp.where(idx == row, x, 0.0), axis=0, keepdims=True)


def _fox_fwd(qkv, c, c_rows, batch, seq, side=None):
    t = min(FOX_T, seq)
    nq = seq // t
    scale = HD ** -0.5
    srcs, per_peer = side if side is not None else ([], False)
    n_s = len(srcs)

    def body(*refs):
        q_ref, k_ref, v_ref, cq_ref, ck_ref = refs[:5]
        o_ref, lse_ref = refs[5 + n_s:7 + n_s]
        _side_exchange(refs[5:5 + n_s], refs[7 + n_s:7 + 2 * n_s], per_peer, refs[7 + 2 * n_s:], batch, PAIRS, nq)
        pair, i = pl.program_id(1), pl.program_id(2)
        lane = lax.broadcasted_iota(jnp.int32, (1, PAIR_W), 1)
        first = (lane // HD) == 0
        mine = [first, jnp.logical_not(first)]
        q = q_ref[...] * scale
        qs = [jnp.where(mine[e], q, 0.0) for e in range(2)]
        cqs = [_pick_lane(cq_ref[...], 2 * pair + e) for e in range(2)]
        causal = lax.broadcasted_iota(jnp.int32, (t, t), 1) <= lax.broadcasted_iota(jnp.int32, (t, t), 0)

        def block(j, carry, diagonal):
            rows = pl.ds(pl.multiple_of(j * t, t), t)
            kj, vj = k_ref[rows, :], v_ref[rows, :]
            ck_blk = ck_ref[0, :, rows]
            out = []
            for e in range(2):
                m, acc = carry[2 * e:2 * e + 2]
                s = _bdot(qs[e], kj, _D2) + cqs[e] - _pick_row(ck_blk, 2 * pair + e)
                if diagonal:
                    s = jnp.where(causal, s, _NEG)
                m_new = jnp.maximum(m, jnp.max(s, axis=1, keepdims=True))
                p = jnp.exp(s - m_new)
                out += [m_new, jnp.exp(m - m_new) * acc + _bdot(p, jnp.where(mine[e], vj, 1.0), _D1)]
            return tuple(out)

        init = (jnp.full((t, 1), _NEG, f32), jnp.zeros((t, PAIR_W), f32)) * 2
        carry = lax.fori_loop(0, i, lambda j, cr: block(j, cr, False), init)
        m0, a0, m1, a1 = block(i, carry, True)
        l0, l1 = _pick_lane(a0, HD), _pick_lane(a1, 0)
        o_ref[...] = jnp.where(first, a0 / l0, a1 / l1)
        lse_ref[...] = jnp.where(lane == 0, m0 + jnp.log(l0), jnp.where(lane == 1, m1 + jnp.log(l1), 0.0))

    q_spec = pl.BlockSpec((t, PAIR_W), lambda b, p, i: (b * nq + i, p))
    res = pl.pallas_call(
        body, name="fox_attn_fwd", grid=(batch, PAIRS, nq),
        in_specs=[q_spec,
                  pl.BlockSpec((seq, PAIR_W), lambda b, p, i: (b, PAIRS + p)),
                  pl.BlockSpec((seq, PAIR_W), lambda b, p, i: (b, 2 * PAIRS + p)),
                  pl.BlockSpec((t, 128), lambda b, p, i: (b * nq + i, 0)),
                  pl.BlockSpec((1, 8, seq), lambda b, p, i: (b, 0, 0))] + [_HBM_SPEC] * n_s,
        out_specs=[q_spec, q_spec] + [_HBM_SPEC] * n_s,
        out_shape=[jax.ShapeDtypeStruct((batch * seq, HW), f32)] * 2 + _side_out_shapes(srcs, per_peer),
        scratch_shapes=_side_sems(n_s),
        compiler_params=_cp(("arbitrary", "arbitrary", "arbitrary")),
    )(qkv, qkv, qkv, c, c_rows, *srcs)
    return res[0], res[1], list(res[2:])


def _fox_bwd(qkv, c, c_rows, o, lse, do, batch, seq):
    t = min(FOX_T, seq)
    nq = seq // t
    scale = HD ** -0.5

    def body(q_ref, k_ref, v_ref, cq_ref, ck_ref, o_ref, lse_ref, do_ref,
             dq_ref, dk_ref, dv_ref, dcq_ref, dck_ref, acc0, acc1):
        pair, i = pl.program_id(1), pl.program_id(2)
        accs = [acc0, acc1]

        @pl.when(i == 0)
        def _():
            dv_ref[...] = jnp.zeros_like(dv_ref)
            acc0[...] = jnp.zeros_like(acc0)
            acc1[...] = jnp.zeros_like(acc1)

        lane = lax.broadcasted_iota(jnp.int32, (1, PAIR_W), 1)
        first = (lane // HD) == 0
        mine = [first, jnp.logical_not(first)]
        q, d_o, o_i = q_ref[...] * scale, do_ref[...], o_ref[...]
        q0s = [jnp.where(mine[e], q, 0.0) for e in range(2)]
        q1s = [jnp.where(mine[e], q, 1.0) for e in range(2)]
        dos = [jnp.where(mine[e], d_o, 0.0) for e in range(2)]
        deltas = [jnp.sum(dos[e] * o_i, axis=1, keepdims=True) for e in range(2)]
        lses = [_pick_lane(lse_ref[...], e) for e in range(2)]
        cqs = [_pick_lane(cq_ref[...], 2 * pair + e) for e in range(2)]
        causal = lax.broadcasted_iota(jnp.int32, (t, t), 1) <= lax.broadcasted_iota(jnp.int32, (t, t), 0)

        def block(j, dqs, diagonal):
            rows = pl.ds(pl.multiple_of(j * t, t), t)
            kj, vj = k_ref[rows, :], v_ref[rows, :]
            ck_blk = ck_ref[0, :, rows]
            out = []
            for e in range(2):
                s = _bdot(q0s[e], kj, _D2) + cqs[e] - _pick_row(ck_blk, 2 * pair + e)
                if diagonal:
                    s = jnp.where(causal, s, _NEG)
                p = jnp.exp(s - lses[e])
                ds = p * (_bdot(dos[e], vj, _D2) - deltas[e])
                dv_ref[rows, :] += _bdot(p, dos[e], _D0)
                accs[e][rows, :] += _bdot(ds, q1s[e], _D0)
                out.append(dqs[e] + _bdot(ds, jnp.where(mine[e], kj, 1.0), _D1))
            return tuple(out)

        zero = jnp.zeros((t, PAIR_W), f32)
        dqs = lax.fori_loop(0, i, lambda j, cr: block(j, cr, False), (zero, zero))
        dq0, dq1 = block(i, dqs, True)
        dq_ref[...] = jnp.where(first, dq0, dq1) * scale
        dcq_ref[...] = jnp.where(lane == 0, _pick_lane(dq0, HD), jnp.where(lane == 1, _pick_lane(dq1, 0), 0.0))

        @pl.when(i == nq - 1)
        def _():
            a0, a1 = acc0[...], acc1[...]
            dk_ref[...] = jnp.where(first, a0, a1)
            dck_ref[...] = jnp.where(lane == 0, -_pick_lane(a0, HD), jnp.where(lane == 1, -_pick_lane(a1, 0), 0.0))

    blk = lambda col: pl.BlockSpec((t, PAIR_W), lambda b, p, i: (b * nq + i, col * PAIRS + p))
    whole = lambda col: pl.BlockSpec((seq, PAIR_W), lambda b, p, i: (b, col * PAIRS + p))
    t_all = batch * seq
    return pl.pallas_call(
        body, name="fox_attn_bwd", grid=(batch, PAIRS, nq),
        in_specs=[blk(0), whole(1), whole(2),
                  pl.BlockSpec((t, 128), lambda b, p, i: (b * nq + i, 0)),
                  pl.BlockSpec((1, 8, seq), lambda b, p, i: (b, 0, 0)),
                  blk(0), blk(0), blk(0)],
        out_specs=[blk(0), whole(0), whole(0), blk(0), whole(0)],
        out_shape=[jax.ShapeDtypeStruct((t_all, HW), f32)] * 5,
        scratch_shapes=[pltpu.VMEM((seq, PAIR_W), f32), pltpu.VMEM((seq, PAIR_W), f32)],
        compiler_params=_cp(("parallel", "parallel", "arbitrary")),
    )(qkv, qkv, qkv, c, c_rows, o, lse, do)


MEM_TQ = 1024


def _mem_block(q, km, vm):
    nn, nt, _ = _make_mm(False, False)
    logits = nt(q, km) * (MEM_HD ** -0.5)
    m = lax.stop_gradient(jnp.max(logits, axis=-1, keepdims=True))
    e = jnp.exp(logits - m)
    return nn(e / jnp.sum(e, axis=-1, keepdims=True), vm)


def _mem_specs(seq, tq):
    nq = seq // tq
    qs = pl.BlockSpec((tq, MEM_HD), lambda b, h, i: (b * nq + i, h))
    ks = pl.BlockSpec((MEM_LEN, MEM_HD), lambda b, h, i: (b, h))
    vs = pl.BlockSpec((MEM_LEN, MEM_HD), lambda b, h, i: (b, MEM_HEADS + h))
    return nq, qs, ks, vs


def _mem_fwd(q, mem_kv, batch, seq):
    tq = min(MEM_TQ, seq)
    nq, qs, ks, vs = _mem_specs(seq, tq)

    def body(q_ref, k_ref, v_ref, o_ref):
        o_ref[...] = _mem_block(q_ref[...].astype(f32), k_ref[...], v_ref[...]).astype(o_ref.dtype)

    return pl.pallas_call(
        body, name="mem_attn_fwd", grid=(batch, MEM_HEADS, nq),
        in_specs=[qs, ks, vs], out_specs=qs, out_shape=jax.ShapeDtypeStruct(q.shape, bf16),
        compiler_params=_cp(("parallel", "parallel", "arbitrary")),
    )(q, mem_kv, mem_kv)


def _mem_bwd(q, mem_kv, do, batch, seq):
    tq = min(MEM_TQ, seq)
    nq, qs, ks, vs = _mem_specs(seq, tq)

    def body(q_ref, k_ref, v_ref, do_ref, dq_ref, dk_ref, dv_ref):
        _, vjp = jax.vjp(_mem_block, q_ref[...].astype(f32), k_ref[...], v_ref[...])
        dq, dk, dv = vjp(do_ref[...])
        dq_ref[...] = dq.astype(dq_ref.dtype)

        @pl.when(pl.program_id(2) == 0)
        def _():
            dk_ref[...] = jnp.zeros_like(dk_ref)
            dv_ref[...] = jnp.zeros_like(dv_ref)

        dk_ref[...] += dk
        dv_ref[...] += dv

    return pl.pallas_call(
        body, name="mem_attn_bwd", grid=(batch, MEM_HEADS, nq),
        in_specs=[qs, ks, vs, qs], out_specs=[qs, ks, ks],
        out_shape=[jax.ShapeDtypeStruct(q.shape, bf16), jax.ShapeDtypeStruct((batch * MEM_LEN, MEM_W), f32),
                   jax.ShapeDtypeStruct((batch * MEM_LEN, MEM_W), f32)],
        compiler_params=_cp(("parallel", "parallel", "arbitrary")),
    )(q, mem_kv, mem_kv, do)


@jax.custom_vjp
def _halves(x):
    c = x.shape[1] // 2
    return x[:, :c], x[:, c:]


_halves.defvjp(lambda x: ((x[:, :x.shape[1] // 2], x[:, x.shape[1] // 2:]), None),
               lambda _, g: (jnp.concatenate(g, axis=1),))


@jax.custom_vjp
def _lead_halves(x):
    n = x.shape[0] // 2
    return x[:n], x[n:]


_lead_halves.defvjp(lambda x: ((x[:x.shape[0] // 2], x[x.shape[0] // 2:]), None),
                    lambda _, g: (jnp.concatenate(g, axis=0),))


def _scan_chunk(s0, r, wl, k, v, a, b):
    nn, nt, tn = _make_mm(True, False)
    nn_exact, _, _ = _make_mm(True, True)
    _, nt_exact, _ = _make_mm(True, "split")
    hp, c, lanes = r.shape
    row = lax.broadcasted_iota(jnp.int32, (c, c), 0)
    col = lax.broadcasted_iota(jnp.int32, (c, c), 1)
    first = (lax.broadcasted_iota(jnp.int32, (1, 1, lanes), 2) // HD) == 0
    tri = jnp.broadcast_to((col <= row).astype(f32)[None], (hp, c, c))
    lg = nn_exact(tri, wl)
    lg_end = lg[:, c - 1:c, :]
    grow, shrink, to_end = jnp.exp(lg), jnp.exp(-lg), jnp.exp(lg_end - lg)
    rt, kt, bt, at = r * grow, k * shrink, b * shrink, a * jnp.exp(lg - wl)
    strict, incl = (col < row)[None], (col <= row)[None]
    twice = lambda t: jnp.concatenate([t, t], axis=0)
    queries = jnp.concatenate([at, rt], axis=1)
    per_head = jnp.concatenate([jnp.where(first, queries, 0.0), jnp.where(first, 0.0, queries)], axis=0)
    (ab, rb), (ak, rk) = _halves(nt_exact(per_head, twice(bt))), _halves(nt_exact(per_head, twice(kt)))
    l_ab = jnp.where(strict, ab, 0.0)
    a_ak = jnp.where(strict, ak, 0.0)
    a_rb = jnp.where(incl, rb, 0.0)
    a_rk = jnp.where(incl, rk, 0.0)
    inv = (col == row).astype(f32)[None] + l_ab
    power, n = l_ab, 1
    while 2 * n < c:
        power = nn(power, power)
        inv = inv + nn(inv, power)
        n *= 2

    def apply(m, t):
        lo, hi = _lead_halves(nn(m, twice(t)))
        return jnp.where(first, lo, hi)

    sa = apply(inv, nt(at, s0) + apply(a_ak, v))
    y = nt(rt, s0) + apply(a_rk, v) + apply(a_rb, sa)
    same_head = ((lax.broadcasted_iota(jnp.int32, (lanes, lanes), 0) // HD)
                 == (lax.broadcasted_iota(jnp.int32, (lanes, lanes), 1) // HD))[None]
    s1 = s0 * jnp.exp(lg_end) + jnp.where(same_head, tn(v, k * to_end) + tn(sa, b * to_end), 0.0)
    return y, s1


PAIRS = HEADS // 2
PAIR_W = 2 * HD


def _pair_stack(ref, off):
    return jnp.stack([ref[b, :, off + p * PAIR_W:off + (p + 1) * PAIR_W]
                      for b in range(ref.shape[0]) for p in range(PAIRS)])


def _pair_store(ref, off, val, add_ref=None):
    for b in range(ref.shape[0]):
        for p in range(PAIRS):
            sl = slice(off + p * PAIR_W, off + (p + 1) * PAIR_W)
            v = val[b * PAIRS + p]
            ref[b, :, sl] = v if add_ref is None else v + add_ref[b, :, sl]


def _scan_fwd(main6, batch, seq, side=None):
    c = min(SCAN_CHUNK, seq)
    nc = seq // c
    hp = batch * PAIRS
    srcs, per_peer = side if side is not None else ([], False)
    n_s = len(srcs)

    def body(*refs):
        z_ref, y_ref, s_ref, st = refs[0], refs[1 + n_s], refs[2 + n_s], refs[3 + 2 * n_s]
        _side_exchange(refs[1:1 + n_s], refs[3 + n_s:3 + 2 * n_s], per_peer, refs[4 + 2 * n_s:], nc)

        @pl.when(pl.program_id(0) == 0)
        def _():
            st[...] = jnp.zeros_like(st)

        s0 = st[...]
        s_ref[0] = s0
        y, s1 = _scan_chunk(s0, *[_pair_stack(z_ref, comp * HW) for comp in range(6)])
        _pair_store(y_ref, 0, y)
        st[...] = s1

    res = pl.pallas_call(
        body, name="rwkv_scan_fwd", grid=(nc,),
        in_specs=[pl.BlockSpec((batch, c, 6 * HW), lambda i: (0, i, 0))] + [_HBM_SPEC] * n_s,
        out_specs=[pl.BlockSpec((batch, c, HW), lambda i: (0, i, 0)),
                   pl.BlockSpec((1, hp, PAIR_W, PAIR_W), lambda i: (i, 0, 0, 0))] + [_HBM_SPEC] * n_s,
        out_shape=[jax.ShapeDtypeStruct((batch, seq, HW), f32), jax.ShapeDtypeStruct((nc, hp, PAIR_W, PAIR_W), f32)]
        + _side_out_shapes(srcs, per_peer),
        scratch_shapes=[pltpu.VMEM((hp, PAIR_W, PAIR_W), f32)] + _side_sems(n_s),
        compiler_params=_cp(("arbitrary",)),
    )(main6.reshape(batch, seq, 6 * HW), *srcs)
    return res[0].reshape(batch * seq, HW), res[1], list(res[2:])


def _scan_bwd(main6, states, dy, extra, batch, seq, side=None):
    c = min(SCAN_CHUNK, seq)
    nc = seq // c
    hp = batch * PAIRS
    srcs, per_peer = side if side is not None else ([], False)
    n_s = len(srcs)

    def body(*refs):
        z_ref, s_ref, dy_ref, ex_ref = refs[:4]
        dz_ref, dst = refs[4 + n_s], refs[5 + 2 * n_s]
        _side_exchange(refs[4:4 + n_s], refs[5 + n_s:5 + 2 * n_s], per_peer, refs[6 + 2 * n_s:], nc)

        @pl.when(pl.program_id(0) == 0)
        def _():
            dst[...] = jnp.zeros_like(dst)

        _, vjp = jax.vjp(_scan_chunk, s_ref[0], *[_pair_stack(z_ref, comp * HW) for comp in range(6)])
        g = vjp((_pair_stack(dy_ref, 0), dst[...]))
        dst[...] = g[0]
        for comp in range(6):
            _pair_store(dz_ref, comp * HW, g[1 + comp], ex_ref)

    back = lambda i: (0, nc - 1 - i, 0)
    wide = pl.BlockSpec((batch, c, 6 * HW), back)
    res = pl.pallas_call(
        body, name="rwkv_scan_bwd", grid=(nc,),
        in_specs=[wide, pl.BlockSpec((1, hp, PAIR_W, PAIR_W), lambda i: (nc - 1 - i, 0, 0, 0)),
                  pl.BlockSpec((batch, c, HW), back), wide] + [_HBM_SPEC] * n_s,
        out_specs=[wide] + [_HBM_SPEC] * n_s,
        out_shape=[jax.ShapeDtypeStruct((batch, seq, 6 * HW), f32)] + _side_out_shapes(srcs, per_peer),
        scratch_shapes=[pltpu.VMEM((hp, PAIR_W, PAIR_W), f32)] + _side_sems(n_s),
        compiler_params=_cp(("arbitrary",)),
    )(main6.reshape(batch, seq, 6 * HW), states, dy.reshape(batch, seq, HW), extra.reshape(batch, seq, 6 * HW), *srcs)
    return res[0].reshape(batch * seq, 6 * HW), list(res[1:])


def _pad_cols(x, width):
    return jnp.pad(x, ((0, 0), (0, width - x.shape[1])))


def _split_w_in(wt):
    z = lambda rows: jnp.zeros((rows, wt.shape[1]), wt.dtype)
    w_r = jnp.concatenate([wt[1544:3080], wt[3080:3144], z(64), wt[3144:3208], z(64), wt[3208:3336]], axis=0)
    return wt[:1536], jnp.concatenate([wt[1536:1544], z(120)], axis=0), w_r, wt[3336:3848], wt[3848:]


def _merge_w_in(g_qkv, g_f, g_r, g_mq, g_g):
    return jnp.concatenate([g_qkv, g_f[:8], g_r[:1536], g_r[1536:1600], g_r[1664:1728], g_r[1792:], g_mq, g_g], axis=0)


def _pad_lora(v):
    z64 = jnp.zeros((1, 64), v.dtype)
    return jnp.concatenate([v[:, :1536], v[:, 1536:1600], z64, v[:, 1600:1664], z64, v[:, 1664:]], axis=1)


def _unpad_lora(v):
    return jnp.concatenate([v[:, :1536], v[:, 1536:1600], v[:, 1664:1728], v[:, 1792:]], axis=1)


def _local_step(x, mem, target, w, p, late=None, early=None, last=None):
    batch, seq, _ = x.shape
    t = batch * seq
    x2, tg2, mem2 = x.reshape(t, D), target.reshape(t, D), mem.reshape(batch * MEM_LEN, D)
    w_qkv, w_f, w_r, w_mq, w_g3 = _split_w_in(w["w_in"])
    mu = _pad_lora(p["rwkv_mu"])
    bias = _pad_cols(p["fox_f_bias"], 128)
    r_k = p["rwkv_r_k"].reshape(1, HW)
    post_params = [p["rwkv_gn_g"], p["rwkv_gn_b"], r_k]
    rw_widths = [HW, HW, HW, LORA_PAD, LORA_PAD, LORA_PAD]
    six = [HW] * 6

    p_g, u = _matmul("proj_gate", _lazy(_fn_rms, [(x2, [D])], D, params=[p["pre1_g"]]), w_g3, "nt", out_dtype=bf16)
    p_qkv = _matmul("proj_qkv", u, w_qkv, "nt", out_dtype=bf16)
    p_f = _matmul("proj_f", u, w_f, "nt")
    p_r = _matmul("proj_rwkv", u, w_r, "nt")
    p_mq = _matmul("proj_memq", u, w_mq, "nt", out_dtype=bf16)

    c = _fox_gate_fwd(p_f, bias, batch, seq)
    c_rows = c[:, :HEADS].reshape(batch, seq, HEADS).transpose(0, 2, 1)
    fox_o, lse, gathered = _fox_fwd(p_qkv, c, c_rows, batch, seq, side=(late[0], False) if late else None)
    if late:
        w = {**w, **late[2](gathered, 0)}
    fox_out = fox_o.astype(bf16)

    w_up = jnp.pad(w["rwkv_w_up"].astype(f32), ((0, LORA_PAD - 64), (0, 0)))
    a_up = jnp.pad(w["rwkv_a_up"].astype(f32), ((0, LORA_PAD - 64), (0, 0)))
    pre_params = [p["rwkv_w0"], w_up, p["rwkv_a0"], a_up, w["rwkv_g_up"].astype(f32), p["rwkv_k_k"], p["rwkv_k_a"]]
    ps = _tokshift_fwd(p_r, mu, batch, seq)
    main6, g_rw = _rows_fwd("rwkv_pre", _fn_rwkv_pre, [], [(ps, rw_widths)], pre_params, [six, [HW]], tm=256)
    y_rw, states, gathered = _scan_fwd(main6, batch, seq, side=(late[1], False) if late else None)
    if late:
        w = {**w, **late[2](gathered, 1)}
    post_consts = []
    post_rows = [(y_rw, [HW]), (main6, six), (g_rw, [HW])]

    def fn_post(y, r, _wl, k2, v, _a, _b, g, gn_g, gn_b, rk):
        return _fn_rwkv_post(y, r, k2, v, g, gn_g, gn_b, rk)

    (rwkv_out,) = _rows_fwd("rwkv_post", fn_post, post_consts, post_rows, post_params, [[HW]], dtypes=[bf16], tm=256)

    mem_kv, memn = _matmul("proj_memkv", _lazy(_fn_rms, [(mem2, [D])], D, params=[p["mem_norm_g"]]), w["w_mem_kv"], "nn")
    mem_out = _mem_fwd(p_mq, mem_kv, batch, seq)

    a_fox = _matmul("out_fox", fox_out, w["w_fox_out"], "nn", out_dtype=bf16)
    a_rwkv = _matmul("out_rwkv", rwkv_out, w["w_rwkv_out"], "nn", out_dtype=bf16)
    a_mem = _matmul("out_mem", mem_out, w["w_mem_out"], "nn", out_dtype=bf16)
    merge_rows = [(a_fox, [D]), (a_rwkv, [D]), (a_mem, [D]), (p_g, [D, D, D])]
    yy, merged = _matmul("out_o", _lazy(_fn_merge, merge_rows, D), w["w_o"], "nn")
    post1_rows = [(yy, [D]), (x2, [D])]
    post1_params = [p["post1_g"], p["pre2_g"]]
    h1, u2 = _rows_fwd("post1", _fn_post1, [], post1_rows, post1_params, [[D], [D]], dtypes=[f32, bf16])
    gp = _matmul("ffn_gate", u2, w["w_ffn_gate"], "nt", out_dtype=bf16)
    up = _matmul("ffn_up", u2, w["w_ffn_up"], "nt", out_dtype=bf16)
    ffn, hmid = _matmul("ffn_down", _lazy(_fn_swiglu, [(gp, [D_FF]), (up, [D_FF])], D_FF), w["w_ffn_down"], "nn")
    final_rows = [(ffn, [D]), (h1, [D])]

    gw, gp_ = {}, {}
    (d_ffn, d_h1), (gp_["post2_g"], loss) = _rows_bwd("final", _fn_final, [(tg2, [D])], final_rows, [p["post2_g"]], [], [],
                                                      n_sums=1, dtypes=[bf16, f32])
    gw["w_ffn_down"] = _matmul("ffn_down_dw", hmid, d_ffn, "tn", out_dtype=bf16)
    d_gp, d_up = _matmul_then_vjp("ffn_down_dx", d_ffn, w["w_ffn_down"], "nt", _fn_swiglu,
                                  [(gp, [D_FF]), (up, [D_FF])], [bf16, bf16])
    d_u2 = _matmul("ffn_gate_dx", d_gp, w["w_ffn_gate"], "nn")
    d_u2 = _matmul("ffn_up_dx", d_up, w["w_ffn_up"], "nn", add=d_u2)
    gw["w_ffn_gate"] = _matmul("ffn_gate_dw", d_gp, u2, "tn", out_dtype=bf16)
    gw["w_ffn_up"] = _matmul("ffn_up_dw", d_up, u2, "tn", out_dtype=bf16)
    (d_yy, d_x_res), (gp_["post1_g"], gp_["pre2_g"]) = _rows_bwd(
        "post1_bwd", _fn_post1, [], post1_rows, post1_params, [[D], [D]], [d_h1, d_u2], dtypes=[bf16, f32])
    gw["w_o"] = _matmul("out_o_dw", merged, d_yy, "tn", out_dtype=bf16)
    d_a_fox, d_a_rwkv, d_a_mem, d_p_g = _matmul_then_vjp("out_o_dx", d_yy, w["w_o"], "nt", _fn_merge, merge_rows, [bf16] * 4)
    d_fox_out = _matmul("out_fox_dx", d_a_fox, w["w_fox_out"], "nt")
    gw["w_fox_out"] = _matmul("out_fox_dw", fox_out, d_a_fox, "tn", out_dtype=bf16)
    d_rwkv_out = _matmul("out_rwkv_dx", d_a_rwkv, w["w_rwkv_out"], "nt")
    gw["w_rwkv_out"] = _matmul("out_rwkv_dw", rwkv_out, d_a_rwkv, "tn", out_dtype=bf16)
    d_mem_out = _matmul("out_mem_dx", d_a_mem, w["w_mem_out"], "nt")
    gw["w_mem_out"] = _matmul("out_mem_dw", mem_out, d_a_mem, "tn", out_dtype=bf16)

    d_p_mq, d_km, d_vm = _mem_bwd(p_mq, mem_kv, d_mem_out, batch, seq)
    d_mem_kv = jnp.concatenate([d_km, d_vm], axis=1).astype(bf16)
    gw["w_mem_kv"] = _matmul("proj_memkv_dw", memn, d_mem_kv, "tn", out_dtype=bf16)
    d_memn = _matmul("proj_memkv_dx", d_mem_kv, w["w_mem_kv"], "nt")
    _, (gp_["mem_norm_g"],) = _rows_bwd("rms_mem_bwd", _fn_rms, [], [(mem2, [D])], [p["mem_norm_g"]], [[D]], [d_memn])

    d_q, d_k, d_v, d_cq, d_ck = _fox_bwd(p_qkv, c, c_rows, fox_o, lse, d_fox_out, batch, seq)
    d_p_qkv = jnp.concatenate([d_q, d_k, d_v], axis=1).astype(bf16)
    d_p_f, d_bias = _fox_gate_bwd(p_f, bias, d_cq, d_ck, batch, seq)
    gp_["fox_f_bias"] = d_bias[:, :HEADS]

    (d_y_rw, d_main6_post, d_g_rw), (gp_["rwkv_gn_g"], gp_["rwkv_gn_b"], d_rk) = _rows_bwd(
        "rwkv_post_bwd", fn_post, post_consts, post_rows, post_params, [[HW]], [d_rwkv_out], tm=256)
    gp_["rwkv_r_k"] = d_rk.reshape(1, HEADS, HD)
    d_main6, early_got = _scan_bwd(main6, states, d_y_rw, d_main6_post, batch, seq,
                                   side=(early(gw), True) if early else None)

    def fn_pre_sum(*args):
        return _fn_rwkv_pre(*args)

    (d_ps,), d_pre = _rows_bwd("rwkv_pre_bwd", fn_pre_sum, [], [(ps, rw_widths)], pre_params, [six, [HW]],
                               [d_main6, d_g_rw], tm=256)
    gp_["rwkv_w0"], d_w_up, gp_["rwkv_a0"], d_a_up, gw["rwkv_g_up"], gp_["rwkv_k_k"], gp_["rwkv_k_a"] = d_pre
    gw["rwkv_w_up"], gw["rwkv_a_up"] = d_w_up[:64], d_a_up[:64]
    d_p_r, d_mu = _tokshift_bwd(p_r, mu, d_ps, batch, seq)
    gp_["rwkv_mu"] = _unpad_lora(d_mu)

    gw["w_in"] = _merge_w_in(_matmul("proj_qkv_dw", d_p_qkv, u, "tn", out_dtype=bf16), _matmul("proj_f_dw", d_p_f, u, "tn", out_dtype=bf16),
                             _matmul("proj_rwkv_dw", d_p_r, u, "tn", out_dtype=bf16), _matmul("proj_memq_dw", d_p_mq, u, "tn", out_dtype=bf16),
                             _matmul("proj_gate_dw", d_p_g, u, "tn", out_dtype=bf16))
    d_x, gp_["pre1_g"], last_got = _input_cotangent(
        "proj_dx", [d_p_qkv, d_p_f, d_p_r, d_p_mq, d_p_g], [w_qkv, w_f, w_r, w_mq, w_g3], x2, p["pre1_g"], d_x_res,
        side=(last(gw), True) if last else None)
    return loss, d_x.reshape(x.shape), gw, gp_, early_got, last_got


def _adamw(name, recv, row_off, w, m, v):
    _, rows, cols = w.shape
    row_tiles = [t for t in range(16, min(rows, 128) + 1, 16) if rows % t == 0 and row_off % t == 0]
    if row_tiles:
        tr, tc = max(row_tiles), cols
        first, grid = row_off // tr, (rows // tr,)
        at = lambda i: (0, first + i, 0)
        mine = lambda i: (0, i, 0)
    else:
        assert row_off == 0 and recv.shape[1] == rows
        tr, tc = rows, 128
        grid = (cols // tc,)
        at = mine = lambda i: (0, 0, i)

    def body(g_ref, w_ref, m_ref, v_ref, go_ref, d_ref, mo_ref, vo_ref):
        g = g_ref[0].astype(f32)
        for s in range(1, N_DEV):
            g = g + g_ref[s].astype(f32)
        m_new = ADAM_B1 * m_ref[0] + (1.0 - ADAM_B1) * g
        v_new = ADAM_B2 * v_ref[0] + (1.0 - ADAM_B2) * (g * g)
        m_hat = m_new / (1.0 - ADAM_B1 ** ADAM_STEP)
        v_hat = v_new / (1.0 - ADAM_B2 ** ADAM_STEP)
        go_ref[0] = g
        d_ref[0] = -ADAM_LR * (m_hat / (jnp.sqrt(v_hat) + ADAM_EPS) + ADAM_WD * w_ref[0])
        mo_ref[0] = m_new
        vo_ref[0] = v_new

    spec = pl.BlockSpec((1, tr, tc), mine)
    return pl.pallas_call(
        body, name=name, grid=grid,
        in_specs=[pl.BlockSpec((N_DEV, tr, tc), at), spec, spec, spec],
        out_specs=[spec] * 4, out_shape=[jax.ShapeDtypeStruct(w.shape, f32)] * 4,
        compiler_params=_cp(("parallel",)),
    )(recv, w, m, v)


GROUPS = (
    ("in", ("w_in",), 0),
    ("memkv", ("w_mem_kv",), 0),
    ("ffn_gu", ("w_ffn_gate", "w_ffn_up"), 0),
    ("down_o", ("w_ffn_down", "w_o"), 0),
    ("outs", ("w_fox_out", "w_rwkv_out", "w_mem_out"), 0),
    ("lora", ("rwkv_w_up", "rwkv_a_up", "rwkv_g_up"), 0),
)
FIRST_GROUPS = ("in", "memkv")
LATE_GROUPS = (("down_o", "outs", "lora"), ("ffn_gu",))
EARLY_GRAD_GROUPS = ("memkv", "ffn_gu", "down_o", "outs")
LAST_GRAD_GROUPS = ("in", "lora")
SHARD_AXIS = {n: a for n, _, a in SHARDED}
SMALL_ROWS = 16


def _group_local(shards, members, join):
    parts = [shards[n].reshape(shards[n].shape[-2:]) for n in members]
    return parts[0] if len(parts) == 1 else jnp.concatenate(parts, axis=join)


def _group_split(arr, members, join, lead=False):
    out, off = {}, 0
    for n in members:
        shape = dict((k, s) for k, s, _ in SHARDED)[n]
        size = _block_shape(shape, SHARD_AXIS[n])[join]
        idx = [slice(None)] * arr.ndim
        idx[arr.ndim - 2 + join] = slice(off, off + size)
        out[n] = arr[tuple(idx)]
        off += size
    return out


def _full_from_blocks(blocks, axis):
    if axis == 0:
        return blocks.reshape(-1, blocks.shape[2])
    return blocks.transpose(1, 0, 2).reshape(blocks.shape[1], -1)


def _blocks_from_full(full, axis):
    if axis == 0:
        return full.reshape(N_DEV, -1, full.shape[1])
    return full.reshape(full.shape[0], N_DEV, -1).transpose(1, 0, 2)


def _assemble(gathered, names):
    out = {}
    for arr, g in zip(gathered, names):
        _, members, join = [grp for grp in GROUPS if grp[0] == g][0]
        for n, blk in _group_split(arr, members, join, lead=True).items():
            out[n] = _full_from_blocks(blk, SHARD_AXIS[n])
    return out


def _grad_blocks(gw, names):
    out = []
    for g in names:
        _, members, join = [grp for grp in GROUPS if grp[0] == g][0]
        parts = [_blocks_from_full(gw[n].astype(bf16), SHARD_AXIS[n]) for n in members]
        out.append(parts[0] if len(parts) == 1 else jnp.concatenate(parts, axis=1 + join))
    return out


def _small_pack(d):
    flat = jnp.concatenate([d[n].reshape(-1) for n, _ in REPLICATED])
    return jnp.pad(flat, (0, SMALL_ROWS * LANES - REPL_ELEMS)).reshape(SMALL_ROWS, LANES)


def _small_unpack(packed):
    out, flat, off = {}, packed.reshape(-1), 0
    for n, shape in REPLICATED:
        k = _rows_of((LANES,) + shape)
        out[n] = flat[off:off + k].reshape(shape)
        off += k
    return out


def kernel(x, mem, pre1_g, post1_g, pre2_g, post2_g, mem_norm_g, w_in, fox_f_bias, rwkv_mu, rwkv_w0, rwkv_w_up, rwkv_a0, rwkv_a_up, rwkv_g_up, rwkv_k_k, rwkv_k_a, rwkv_r_k, rwkv_gn_g, rwkv_gn_b, w_mem_kv, w_fox_out, w_rwkv_out, w_mem_out, w_o, w_ffn_gate, w_ffn_up, w_ffn_down, loss_target, m_pre1_g, m_post1_g, m_pre2_g, m_post2_g, m_mem_norm_g, m_w_in, m_fox_f_bias, m_rwkv_mu, m_rwkv_w0, m_rwkv_w_up, m_rwkv_a0, m_rwkv_a_up, m_rwkv_g_up, m_rwkv_k_k, m_rwkv_k_a, m_rwkv_r_k, m_rwkv_gn_g, m_rwkv_gn_b, m_w_mem_kv, m_w_fox_out, m_w_rwkv_out, m_w_mem_out, m_w_o, m_w_ffn_gate, m_w_ffn_up, m_w_ffn_down, v_pre1_g, v_post1_g, v_pre2_g, v_post2_g, v_mem_norm_g, v_w_in, v_fox_f_bias, v_rwkv_mu, v_rwkv_w0, v_rwkv_w_up, v_rwkv_a0, v_rwkv_a_up, v_rwkv_g_up, v_rwkv_k_k, v_rwkv_k_a, v_rwkv_r_k, v_rwkv_gn_g, v_rwkv_gn_b, v_w_mem_kv, v_w_fox_out, v_w_rwkv_out, v_w_mem_out, v_w_o, v_w_ffn_gate, v_w_ffn_up, v_w_ffn_down):
    args = dict(locals())
    turn = lambda n, a: jnp.swapaxes(a, 1, 2) if n in TRANSPOSED else a
    wts = {n: turn(n, args[n]) for n in WEIGHT_ORDER}
    ms = {n: turn(n, args["m_" + n]) for n in WEIGHT_ORDER}
    vs = {n: turn(n, args["v_" + n]) for n in WEIGHT_ORDER}

    groups = {g: (members, join) for g, members, join in GROUPS}
    w_bf16 = {n: wts[n].astype(bf16) for n, _, _ in SHARDED}

    def send(g):
        return _group_local(w_bf16, *groups[g])

    first = _exchange("gather_first", [send(g) for g in FIRST_GROUPS], per_peer=False)
    full = _assemble(first, FIRST_GROUPS)
    small_in = {n: (wts[n] if n == "rwkv_r_k" else wts[n].reshape(wts[n].shape[-2:])) for n, _ in REPLICATED}
    late = ([send(g) for g in LATE_GROUPS[0]], [send(g) for g in LATE_GROUPS[1]],
            lambda got, which: _assemble(got, LATE_GROUPS[which]))
    loss_part, grad_x, gw, gp, early_got, last_got = _local_step(
        x, mem, loss_target, full, small_in, late=late, early=lambda g: _grad_blocks(g, EARLY_GRAD_GROUPS),
        last=lambda g: _grad_blocks(g, LAST_GRAD_GROUPS))
    (small_got,) = _exchange("exchange_small", [_small_pack(gp).astype(bf16)], per_peer=False)
    received = dict(zip(EARLY_GRAD_GROUPS + LAST_GRAD_GROUPS, list(early_got) + list(last_got)))

    outs = [{}, {}, {}, {}]
    for g, members, _ in GROUPS:
        off = 0
        for n in members:
            for o, arr in zip(outs, _adamw("adamw_" + n, received[g], off, wts[n], ms[n], vs[n])):
                o[n] = arr
            off += wts[n].shape[1]
    res = _adamw("adamw_small", small_got, 0, *[_small_pack(d)[None] for d in (wts, ms, vs)])
    for o, arr in zip(outs, res):
        o.update(_small_unpack(arr))
    loss = lax.psum(loss_part[0, 0], ("x", "y", "c"))
    return (loss, grad_x, *[turn(n, o[n].reshape(wts[n].shape)) for o in outs for n in WEIGHT_ORDER])
```

```python
import functools

import jax
import jax.numpy as jnp
from jax import lax
from jax.experimental import pallas as pl
from jax.experimental.pallas import tpu as pltpu

f32 = jnp.float32
bf16 = jnp.bfloat16
_HI = lax.Precision.HIGHEST

D = 1024
HEADS = 8
HD = 64
HW = HEADS * HD
MEM_HEADS = 4
MEM_HD = 128
MEM_W = 512
MEM_LEN = 256
D_FF = 2816
LORA_PAD = 128
NORM_EPS = 1e-6
GN_EPS = 64e-5
SCAN_CHUNK = 64
N_DEV = 8
LANES = 1024
VMEM_LIMIT = 56 * 1024 * 1024

ADAM_LR = 0.001
ADAM_B1 = 0.9
ADAM_B2 = 0.999
ADAM_EPS = 1e-08
ADAM_WD = 0.01
ADAM_STEP = 10

TRANSPOSED = ("w_in", "w_ffn_gate", "w_ffn_up")
SHARDED = (
    ("w_in", (6920, 1024), 0),
    ("w_ffn_gate", (2816, 1024), 0),
    ("w_ffn_up", (2816, 1024), 0),
    ("w_ffn_down", (2816, 1024), 0),
    ("w_mem_kv", (1024, 1024), 0),
    ("w_o", (1024, 1024), 0),
    ("w_fox_out", (512, 1024), 1),
    ("w_rwkv_out", (512, 1024), 1),
    ("w_mem_out", (512, 1024), 1),
    ("rwkv_w_up", (64, 512), 1),
    ("rwkv_a_up", (64, 512), 1),
    ("rwkv_g_up", (128, 512), 1),
)
REPLICATED = (
    ("pre1_g", (1, 1024)), ("post1_g", (1, 1024)), ("pre2_g", (1, 1024)), ("post2_g", (1, 1024)),
    ("mem_norm_g", (1, 1024)), ("fox_f_bias", (1, 8)), ("rwkv_mu", (1, 1792)), ("rwkv_w0", (1, 512)),
    ("rwkv_a0", (1, 512)), ("rwkv_k_k", (1, 512)), ("rwkv_k_a", (1, 512)), ("rwkv_r_k", (1, 8, 64)),
    ("rwkv_gn_g", (1, 512)), ("rwkv_gn_b", (1, 512)),
)
WEIGHT_ORDER = ('pre1_g', 'post1_g', 'pre2_g', 'post2_g', 'mem_norm_g', 'w_in', 'fox_f_bias', 'rwkv_mu',
                'rwkv_w0', 'rwkv_w_up', 'rwkv_a0', 'rwkv_a_up', 'rwkv_g_up', 'rwkv_k_k', 'rwkv_k_a',
                'rwkv_r_k', 'rwkv_gn_g', 'rwkv_gn_b', 'w_mem_kv', 'w_fox_out', 'w_rwkv_out', 'w_mem_out',
                'w_o', 'w_ffn_gate', 'w_ffn_up', 'w_ffn_down')


def _block_shape(shape, axis):
    return tuple(s // N_DEV if i == axis else s for i, s in enumerate(shape))


def _rows_of(shape):
    n = 1
    for s in shape:
        n *= s
    return n // LANES


REPL_ELEMS = sum(_rows_of((LANES,) + s) for _, s in REPLICATED)


def _cp(sem=None):
    return pltpu.CompilerParams(dimension_semantics=sem, vmem_limit_bytes=VMEM_LIMIT)


def _tile(dim, cap):
    best = None
    for t in range(128, min(dim, cap) + 1, 128):
        if dim % t == 0:
            best = t
    return best if best is not None else dim


def _two_terms(x):
    hi = x.astype(bf16)
    return hi, (x - hi.astype(f32)).astype(bf16)


def _dg(a, b, dims, exact):
    if exact == "split":
        (a_hi, a_lo), (b_hi, b_lo) = _two_terms(a), _two_terms(b)
        dot = functools.partial(lax.dot_general, dimension_numbers=dims, preferred_element_type=f32)
        return dot(a_hi, b_hi) + (dot(a_hi, b_lo) + dot(a_lo, b_hi))
    if exact:
        return lax.dot_general(a, b, dims, precision=_HI, preferred_element_type=f32)
    return lax.dot_general(a.astype(bf16), b.astype(bf16), dims, preferred_element_type=f32)


def _make_mm(batched, exact):
    o = 1 if batched else 0
    bd = ((0,), (0,)) if batched else ((), ())
    d_nn = (((1 + o,), (o,)), bd)
    d_nt = (((1 + o,), (1 + o,)), bd)
    d_tn = (((o,), (o,)), bd)

    @jax.custom_vjp
    def nn(a, b):
        return _dg(a, b, d_nn, exact)

    @jax.custom_vjp
    def nt(a, b):
        return _dg(a, b, d_nt, exact)

    @jax.custom_vjp
    def tn(a, b):
        return _dg(a, b, d_tn, exact)

    nn.defvjp(lambda a, b: (_dg(a, b, d_nn, exact), (a, b)),
              lambda res, g: (_dg(g, res[1], d_nt, exact), _dg(res[0], g, d_tn, exact)))
    nt.defvjp(lambda a, b: (_dg(a, b, d_nt, exact), (a, b)),
              lambda res, g: (_dg(g, res[1], d_nn, exact), _dg(g, res[0], d_tn, exact)))
    tn.defvjp(lambda a, b: (_dg(a, b, d_tn, exact), (a, b)),
              lambda res, g: (_dg(res[1], g, d_nt, exact), _dg(res[0], g, d_nn, exact)))
    return nn, nt, tn


def _sigmoid(x):
    return 1.0 / (1.0 + jnp.exp(-x))


def _head_sum_raw(x):
    width = 2 * HD
    i = lax.broadcasted_iota(jnp.int32, (width, width), 0) // HD
    j = lax.broadcasted_iota(jnp.int32, (width, width), 1) // HD
    m = (i == j).astype(bf16)
    dims = (((1,), (0,)), ((), ()))
    out = []
    for p in range(x.shape[1] // width):
        xp = x[:, p * width:(p + 1) * width]
        hi = xp.astype(bf16)
        lo = (xp - hi.astype(f32)).astype(bf16)
        out.append(lax.dot_general(hi, m, dims, preferred_element_type=f32)
                   + lax.dot_general(lo, m, dims, preferred_element_type=f32))
    return jnp.concatenate(out, axis=1)


@jax.custom_vjp
def _head_sum(x):
    return _head_sum_raw(x)


_head_sum.defvjp(lambda x: (_head_sum_raw(x), None), lambda _, g: (_head_sum_raw(g),))


WEIGHT_TILE_BYTES = 13 * 512 * 1024
ACC_TILE_BYTES = 8 * 1024 * 1024


def _lazy(fn, rows, width, params=()):
    return (fn, rows, width, list(params))


def _matmul(name, a, b, mode, add=None, out_dtype=f32):
    has_add = add is not None
    if isinstance(a, tuple):
        a_fn, a_rows, a_width, a_params = a
        a_arrays = [r for r, _ in a_rows]
        a_shape = (a_arrays[0].shape[0], a_width)
    else:
        a_fn, a_rows, a_params, a_arrays, a_shape = None, None, [], [a], a.shape
    n_r = len(a_arrays)
    n_a = n_r + len(a_params)

    def load_a(refs):
        if a_fn is None:
            return refs[0][...].astype(bf16)
        pieces = []
        for r, (_, widths) in zip(refs[:n_r], a_rows):
            pieces += _pieces(r, widths)
        return a_fn(*pieces, *[p[...] for p in refs[n_r:]])[0].astype(bf16)

    if mode == "tn":
        assert a_fn is None
        (k, m), (_, n) = a_shape, b.shape
        tn = _tile(n, max(128, ACC_TILE_BYTES // (4 * m)))
        tk = _tile(k, 1024)
        nk = k // tk

        def body(*refs):
            b_ref, o_ref, acc = refs[n_a:]

            @pl.when(pl.program_id(1) == 0)
            def _():
                acc[...] = jnp.zeros_like(acc)

            acc[...] += lax.dot_general(load_a(refs[:n_a]), b_ref[...].astype(bf16),
                                        (((0,), (0,)), ((), ())), preferred_element_type=f32)

            @pl.when(pl.program_id(1) == nk - 1)
            def _():
                o_ref[...] = acc[...].astype(o_ref.dtype)

        return pl.pallas_call(
            body, name=name, grid=(n // tn, nk),
            in_specs=[pl.BlockSpec((tk, r.shape[1]), lambda j, kk: (kk, 0)) for r in a_arrays]
            + [pl.BlockSpec((tk, tn), lambda j, kk: (kk, j))],
            out_specs=pl.BlockSpec((m, tn), lambda j, kk: (0, j)), out_shape=jax.ShapeDtypeStruct((m, n), out_dtype),
            scratch_shapes=[pltpu.VMEM((m, tn), f32)],
            compiler_params=_cp(("parallel", "arbitrary")),
        )(*a_arrays, b)

    (m, k) = a_shape
    n = b.shape[1] if mode == "nn" else b.shape[0]
    tm = _tile(m, 1024 if a_fn is None else 512)
    tn = _tile(n, max(128, WEIGHT_TILE_BYTES // (2 * k)))
    dims = (((1,), (0,)), ((), ())) if mode == "nn" else (((1,), (1,)), ((), ()))
    b_spec = pl.BlockSpec((k, tn), lambda j, i: (0, j)) if mode == "nn" else pl.BlockSpec((tn, k), lambda j, i: (j, 0))
    o_spec = pl.BlockSpec((tm, tn), lambda j, i: (i, j))

    keep = a_fn is not None
    assert not keep or tn == n

    def body(*refs):
        b_ref = refs[n_a]
        a_val = load_a(refs[:n_a])
        r = lax.dot_general(a_val, b_ref[...].astype(bf16), dims, preferred_element_type=f32)
        if has_add:
            r = r + refs[n_a + 1][...]
        if keep:
            refs[-2][...] = r.astype(refs[-2].dtype)
            refs[-1][...] = a_val
        else:
            refs[-1][...] = r.astype(refs[-1].dtype)

    res = pl.pallas_call(
        body, name=name, grid=(n // tn, m // tm),
        in_specs=[pl.BlockSpec((tm, r.shape[1]), lambda j, i: (i, 0)) for r in a_arrays]
        + [pl.BlockSpec(p.shape, lambda j, i: (0, 0)) for p in a_params] + [b_spec] + ([o_spec] if has_add else []),
        out_specs=[o_spec] + ([pl.BlockSpec((tm, k), lambda j, i: (i, 0))] if keep else []),
        out_shape=[jax.ShapeDtypeStruct((m, n), out_dtype)] + ([jax.ShapeDtypeStruct((m, k), bf16)] if keep else []),
        compiler_params=_cp(("parallel", "arbitrary")),
    )(*a_arrays, *a_params, b, *([add] if has_add else []))
    return tuple(res) if keep else res[0]


def _input_cotangent(name, a_list, b_list, x, gain, add, side=None):
    m = a_list[0].shape[0]
    tm = _tile(m, 256)
    n_g = len(a_list)
    srcs, per_peer = side if side is not None else ([], False)
    n_s = len(srcs)

    def body(*refs):
        x_ref, g_ref, add_ref = refs[2 * n_g:2 * n_g + 3]
        src_refs = refs[2 * n_g + 3:2 * n_g + 3 + n_s]
        dx_ref, dg_ref = refs[2 * n_g + 3 + n_s:2 * n_g + 5 + n_s]
        _side_exchange(src_refs, refs[2 * n_g + 5 + n_s:2 * n_g + 5 + 2 * n_s], per_peer, refs[2 * n_g + 5 + 2 * n_s:], m // tm)
        d_u = None
        for g in range(n_g):
            r = lax.dot_general(refs[g][...].astype(bf16), refs[n_g + g][...].astype(bf16), (((1,), (0,)), ((), ())),
                                preferred_element_type=f32)
            d_u = r if d_u is None else d_u + r
        _, vjp = jax.vjp(_rms, x_ref[...], g_ref[...])
        d_x, d_gain = vjp(d_u)
        dx_ref[...] = d_x + add_ref[...]

        @pl.when(pl.program_id(0) == 0)
        def _():
            dg_ref[...] = jnp.zeros_like(dg_ref)

        dg_ref[...] += d_gain

    rows = pl.BlockSpec((tm, x.shape[1]), lambda i: (i, 0))
    whole = lambda b: pl.BlockSpec(b.shape, lambda i: (0, 0))
    res = pl.pallas_call(
        body, name=name, grid=(m // tm,),
        in_specs=[pl.BlockSpec((tm, a.shape[1]), lambda i: (i, 0)) for a in a_list] + [whole(b) for b in b_list]
        + [rows, whole(gain), rows] + [_HBM_SPEC] * n_s,
        out_specs=[rows, whole(gain)] + [_HBM_SPEC] * n_s,
        out_shape=[jax.ShapeDtypeStruct(x.shape, f32), jax.ShapeDtypeStruct(gain.shape, f32)] + _side_out_shapes(srcs, per_peer),
        scratch_shapes=_side_sems(n_s),
        compiler_params=_cp(("arbitrary",)),
    )(*a_list, *b_list, x, gain, add, *srcs)
    return res[0], res[1], list(res[2:])


def _pieces(ref, widths):
    out, off = [], 0
    for w in widths:
        out.append(ref[:, off:off + w].astype(f32))
        off += w
    return out


def _store_pieces(ref, widths, vals, add_ref=None):
    off = 0
    for w, v in zip(widths, vals):
        ref[:, off:off + w] = (v if add_ref is None else v + add_ref[:, off:off + w]).astype(ref.dtype)
        off += w


def _rows_fwd(name, fn, consts, rows, params, outs, n_sums=0, tm=512, dtypes=None):
    t = (consts + rows)[0][0].shape[0]
    tm = min(tm, t)
    ins = consts + rows
    n_in, n_p, n_o = len(ins), len(params), len(outs)
    dtypes = dtypes or [f32] * n_o

    def body(*refs):
        in_refs, p_refs = refs[:n_in], refs[n_in:n_in + n_p]
        o_refs, s_refs = refs[n_in + n_p:n_in + n_p + n_o], refs[n_in + n_p + n_o:]
        vals = []
        for r, (_, widths) in zip(in_refs, ins):
            vals += _pieces(r, widths)
        res = fn(*vals, *[p[...] for p in p_refs])
        pos = 0
        for r, widths in zip(o_refs, outs):
            _store_pieces(r, widths, res[pos:pos + len(widths)])
            pos += len(widths)

        @pl.when(pl.program_id(0) == 0)
        def _():
            for s in s_refs:
                s[...] = jnp.zeros_like(s)

        for s, v in zip(s_refs, res[pos:]):
            s[...] += v

    row_spec = lambda w: pl.BlockSpec((tm, w), lambda i: (i, 0))
    full = lambda p: pl.BlockSpec(p.shape, lambda i: (0,) * p.ndim)
    return pl.pallas_call(
        body, name=name, grid=(t // tm,),
        in_specs=[row_spec(sum(w)) for _, w in ins] + [full(p) for p in params],
        out_specs=[row_spec(sum(w)) for w in outs] + [pl.BlockSpec((1, 1), lambda i: (0, 0))] * n_sums,
        out_shape=[jax.ShapeDtypeStruct((t, sum(w)), dt) for w, dt in zip(outs, dtypes)] + [jax.ShapeDtypeStruct((1, 1), f32)] * n_sums,
        compiler_params=_cp(("arbitrary",)),
    )(*[a for a, _ in ins], *params)


def _rows_bwd(name, fn, consts, rows, params, outs, cts, n_sums=0, add=None, tm=512, dtypes=None):
    t = (consts + rows)[0][0].shape[0]
    tm = min(tm, t)
    n_c, n_r, n_p, n_o = len(consts), len(rows), len(params), len(outs)
    has_add = add is not None
    dtypes = dtypes or [f32] * n_r

    def body(*refs):
        pos = 0
        c_refs = refs[pos:pos + n_c]; pos += n_c
        r_refs = refs[pos:pos + n_r]; pos += n_r
        p_refs = refs[pos:pos + n_p]; pos += n_p
        ct_refs = refs[pos:pos + n_o]; pos += n_o
        add_ref = refs[pos] if has_add else None
        pos += 1 if has_add else 0
        dr_refs = refs[pos:pos + n_r]; pos += n_r
        dp_refs = refs[pos:pos + n_p]; pos += n_p
        s_refs = refs[pos:pos + n_sums]
        cvals, rvals = [], []
        for r, (_, widths) in zip(c_refs, consts):
            cvals += _pieces(r, widths)
        for r, (_, widths) in zip(r_refs, rows):
            rvals += _pieces(r, widths)
        pvals = [p[...] for p in p_refs]
        ctv = []
        for r, widths in zip(ct_refs, outs):
            ctv += _pieces(r, widths)
        ctv += [jnp.ones((1, 1), f32)] * n_sums
        primal, vjp = jax.vjp(lambda *rp: tuple(fn(*cvals, *rp)), *rvals, *pvals)
        g = vjp(tuple(ctv))
        pos = 0
        for idx, (r, (_, widths)) in enumerate(zip(dr_refs, rows)):
            _store_pieces(r, widths, g[pos:pos + len(widths)], add_ref if idx == 0 else None)
            pos += len(widths)

        @pl.when(pl.program_id(0) == 0)
        def _():
            for acc in list(dp_refs) + list(s_refs):
                acc[...] = jnp.zeros_like(acc)

        for dp, v in zip(dp_refs, g[pos:]):
            dp[...] += v
        for s, v in zip(s_refs, primal[len(primal) - n_sums:]):
            s[...] += v

    row_spec = lambda w: pl.BlockSpec((tm, w), lambda i: (i, 0))
    full = lambda p: pl.BlockSpec(p.shape, lambda i: (0,) * p.ndim)
    args = [a for a, _ in consts + rows] + list(params) + list(cts) + ([add] if has_add else [])
    res = pl.pallas_call(
        body, name=name, grid=(t // tm,),
        in_specs=[row_spec(sum(w)) for _, w in consts + rows] + [full(p) for p in params]
        + [row_spec(sum(w)) for w in outs] + ([row_spec(add.shape[1])] if has_add else []),
        out_specs=[row_spec(sum(w)) for _, w in rows] + [full(p) for p in params]
        + [pl.BlockSpec((1, 1), lambda i: (0, 0))] * n_sums,
        out_shape=[jax.ShapeDtypeStruct((t, sum(w)), dt) for (_, w), dt in zip(rows, dtypes)]
        + [jax.ShapeDtypeStruct(p.shape, f32) for p in params] + [jax.ShapeDtypeStruct((1, 1), f32)] * n_sums,
        compiler_params=_cp(("arbitrary",)),
    )(*args)
    return res[:n_r], res[n_r:n_r + n_p] + res[n_r + n_p:]


def _matmul_then_vjp(name, a, b, mode, fn, rows, dtypes, tm=256):
    m, k = a.shape
    tm = min(tm, m)
    dims = (((1,), (0,)), ((), ())) if mode == "nn" else (((1,), (1,)), ((), ()))
    n_r = len(rows)

    def body(*refs):
        a_ref, b_ref = refs[:2]
        ct = lax.dot_general(a_ref[...].astype(bf16), b_ref[...].astype(bf16), dims, preferred_element_type=f32)
        rvals = []
        for r, (_, widths) in zip(refs[2:2 + n_r], rows):
            rvals += _pieces(r, widths)
        _, vjp = jax.vjp(lambda *rp: fn(*rp)[0], *rvals)
        g = vjp(ct)
        pos = 0
        for r, (_, widths) in zip(refs[2 + n_r:], rows):
            _store_pieces(r, widths, g[pos:pos + len(widths)])
            pos += len(widths)

    row_spec = lambda w: pl.BlockSpec((tm, w), lambda i: (i, 0))
    return pl.pallas_call(
        body, name=name, grid=(m // tm,),
        in_specs=[row_spec(k), pl.BlockSpec(b.shape, lambda i: (0, 0))] + [row_spec(r.shape[1]) for r, _ in rows],
        out_specs=[row_spec(r.shape[1]) for r, _ in rows],
        out_shape=[jax.ShapeDtypeStruct(r.shape, dt) for (r, _), dt in zip(rows, dtypes)],
        compiler_params=_cp(("parallel",)),
    )(a, b, *[r for r, _ in rows])


def _rms(x, g):
    return x * lax.rsqrt(jnp.mean(x * x, axis=-1, keepdims=True) + NORM_EPS) * g


def _fn_rms(x, g):
    return (_rms(x, g),)


def _fn_rwkv_pre(r, k, v, wd, ad, gd, w0, w_up, a0, a_up, g_up, k_k, k_a):
    nn, _, _ = _make_mm(False, False)
    w_log = -_sigmoid(w0 + nn(jnp.tanh(wd), w_up)) * 0.6065306597126334
    a = _sigmoid(a0 + nn(ad, a_up))
    g = nn(_sigmoid(gd), g_up)
    kk = k * k_k
    kk = kk * lax.rsqrt(jnp.maximum(_head_sum(kk * kk), 1e-24))
    k2 = k * (1.0 + (a - 1.0) * k_a)
    return r, k2, v, w_log, -kk, kk * a, g


def _fn_rwkv_post(y, r, k2, v, g, gn_g, gn_b, r_k):
    mean = _head_sum(y) * (1.0 / HD)
    yc = y - mean
    var = _head_sum(yc * yc) * (1.0 / HD)
    yn = yc * lax.rsqrt(var + GN_EPS) * gn_g + gn_b
    bonus = _head_sum(r * k2 * r_k) * v
    return ((yn + bonus) * g,)


def _fn_merge(a_fox, a_rwkv, a_mem, g_fox, g_rwkv, g_mem):
    return (_sigmoid(g_fox) * a_fox + _sigmoid(g_rwkv) * a_rwkv + _sigmoid(g_mem) * a_mem,)


def _fn_post1(y, x, post1_g, pre2_g):
    h1 = x + _rms(y, post1_g)
    return h1, _rms(h1, pre2_g)


def _fn_swiglu(gp, up):
    return (gp * _sigmoid(gp) * up,)


def _fn_final(target, ffn, h1, post2_g):
    err = h1 + _rms(ffn, post2_g) - target
    per_row = jnp.mean(err * err, axis=-1, keepdims=True)
    return (0.5 * jnp.sum(per_row, axis=0, keepdims=True),)


def _shift_down(x):
    row = lax.broadcasted_iota(jnp.int32, x.shape, 0)
    return jnp.where(row == 0, 0.0, pltpu.roll(x, 1, 0))


def _shift_up(x):
    s = x.shape[0]
    row = lax.broadcasted_iota(jnp.int32, x.shape, 0)
    return jnp.where(row == s - 1, 0.0, pltpu.roll(x, s - 1, 0))


def _tokshift_fwd(p, mu, batch, seq):
    w = p.shape[1]
    tc = _tile(w, 384)

    def body(p_ref, mu_ref, o_ref):
        x = p_ref[...]
        o_ref[...] = x + (_shift_down(x) - x) * mu_ref[...]

    return pl.pallas_call(
        body, name="tokshift_fwd", grid=(w // tc, batch),
        in_specs=[pl.BlockSpec((seq, tc), lambda j, b: (b, j)), pl.BlockSpec((1, tc), lambda j, b: (0, j))],
        out_specs=pl.BlockSpec((seq, tc), lambda j, b: (b, j)),
        out_shape=jax.ShapeDtypeStruct(p.shape, f32),
        compiler_params=_cp(("parallel", "arbitrary")),
    )(p, mu)


def _tokshift_bwd(p, mu, dps, batch, seq):
    w = p.shape[1]
    tc = _tile(w, 384)

    def body(p_ref, mu_ref, d_ref, dp_ref, dmu_ref):
        x, mu_v, d = p_ref[...], mu_ref[...], d_ref[...]
        dp_ref[...] = (d * (1.0 - mu_v) + _shift_up(d * mu_v)).astype(dp_ref.dtype)

        @pl.when(pl.program_id(1) == 0)
        def _():
            dmu_ref[...] = jnp.zeros_like(dmu_ref)

        dmu_ref[...] += jnp.sum(d * (_shift_down(x) - x), axis=0, keepdims=True)

    return pl.pallas_call(
        body, name="tokshift_bwd", grid=(w // tc, batch),
        in_specs=[pl.BlockSpec((seq, tc), lambda j, b: (b, j)), pl.BlockSpec((1, tc), lambda j, b: (0, j)),
                  pl.BlockSpec((seq, tc), lambda j, b: (b, j))],
        out_specs=[pl.BlockSpec((seq, tc), lambda j, b: (b, j)), pl.BlockSpec((1, tc), lambda j, b: (0, j))],
        out_shape=[jax.ShapeDtypeStruct(p.shape, bf16), jax.ShapeDtypeStruct(mu.shape, f32)],
        compiler_params=_cp(("parallel", "arbitrary")),
    )(p, mu, dps)


def _cum_block(seq):
    return _tile(seq, 256)


def _fox_gate_fwd(f, bias, batch, seq):
    cb = _cum_block(seq)

    def body(f_ref, b_ref, c_ref):
        row = lax.broadcasted_iota(jnp.int32, (cb, cb), 0)
        col = lax.broadcasted_iota(jnp.int32, (cb, cb), 1)
        tri = (col <= row).astype(f32)
        carry = jnp.zeros((1, 128), f32)
        for i in range(seq // cb):
            z = f_ref[i * cb:(i + 1) * cb, :] + b_ref[...]
            ls = jnp.minimum(z, 0.0) - jnp.log(1.0 + jnp.exp(-jnp.abs(z)))
            c = _dg(tri, ls, (((1,), (0,)), ((), ())), True) + carry
            c_ref[i * cb:(i + 1) * cb, :] = c
            carry = c[cb - 1:cb, :]

    return pl.pallas_call(
        body, name="fox_gate_fwd", grid=(batch,),
        in_specs=[pl.BlockSpec((seq, 128), lambda b: (b, 0)), pl.BlockSpec((1, 128), lambda b: (0, 0))],
        out_specs=pl.BlockSpec((seq, 128), lambda b: (b, 0)),
        out_shape=jax.ShapeDtypeStruct(f.shape, f32),
        compiler_params=_cp(("arbitrary",)),
    )(f, bias)


def _fox_gate_bwd(f, bias, dc_a, dc_b, batch, seq):
    cb = _cum_block(seq)

    def body(f_ref, b_ref, da_ref, db_ref, df_ref, dbias_ref):
        row = lax.broadcasted_iota(jnp.int32, (cb, cb), 0)
        col = lax.broadcasted_iota(jnp.int32, (cb, cb), 1)
        triu = (col >= row).astype(f32)

        @pl.when(pl.program_id(0) == 0)
        def _():
            dbias_ref[...] = jnp.zeros_like(dbias_ref)

        lane = lax.broadcasted_iota(jnp.int32, (1, 128), 1)

        def by_head(blk):
            out = jnp.zeros((cb, 128), f32)
            for p in range(HEADS // 2):
                for e in range(2):
                    out = jnp.where(lane == 2 * p + e, _pick_lane(blk[:, p * 128:(p + 1) * 128], e), out)
            return out

        carry = jnp.zeros((1, 128), f32)
        tot = jnp.zeros((1, 128), f32)
        for i in reversed(range(seq // cb)):
            sl = slice(i * cb, (i + 1) * cb)
            dc = by_head(da_ref[sl, :] + db_ref[sl, :])
            dls = _dg(triu, dc, (((1,), (0,)), ((), ())), True) + carry
            carry = dls[0:1, :]
            df = dls * _sigmoid(-(f_ref[sl, :] + b_ref[...]))
            df_ref[sl, :] = df.astype(df_ref.dtype)
            tot = tot + jnp.sum(df, axis=0, keepdims=True)
        dbias_ref[...] += tot

    return pl.pallas_call(
        body, name="fox_gate_bwd", grid=(batch,),
        in_specs=[pl.BlockSpec((seq, 128), lambda b: (b, 0)), pl.BlockSpec((1, 128), lambda b: (0, 0)),
                  pl.BlockSpec((seq, HW), lambda b: (b, 0)), pl.BlockSpec((seq, HW), lambda b: (b, 0))],
        out_specs=[pl.BlockSpec((seq, 128), lambda b: (b, 0)), pl.BlockSpec((1, 128), lambda b: (0, 0))],
        out_shape=[jax.ShapeDtypeStruct(f.shape, bf16), jax.ShapeDtypeStruct((1, 128), f32)],
        compiler_params=_cp(("arbitrary",)),
    )(f, bias, dc_a, dc_b)


_HBM_SPEC = pl.BlockSpec(memory_space=pltpu.HBM)


def _side_out_shapes(srcs, per_peer):
    return [jax.ShapeDtypeStruct(((N_DEV,) + tuple(s.shape[1:] if per_peer else s.shape)), s.dtype) for s in srcs]


def _side_sems(n):
    if n == 0:
        return []
    return [pltpu.SemaphoreType.DMA((n, N_DEV - 1)), pltpu.SemaphoreType.DMA((n, N_DEV - 1)), pltpu.SemaphoreType.DMA((n,))]


def _peer_copies(src_refs, dst_refs, per_peer, sems):
    send_sems, recv_sems, local_sems = sems
    x, y, c = lax.axis_index("x"), lax.axis_index("y"), lax.axis_index("c")
    me = 4 * x + 2 * y + c

    def remote(src, dst, t, k, to):
        return pltpu.make_async_remote_copy(src_ref=src, dst_ref=dst, send_sem=send_sems.at[t, k - 1],
                                            recv_sem=recv_sems.at[t, k - 1], device_id=to,
                                            device_id_type=pl.DeviceIdType.MESH)

    direct, relays = [], []
    for t, (s, d) in enumerate(zip(src_refs, dst_refs)):
        direct.append((t, 0, pltpu.make_async_copy(s.at[me] if per_peer else s, d.at[me], local_sems.at[t])))
        for k in range(1, N_DEV):
            px = 1 - x if k & 4 else x
            py = 1 - y if k & 2 else y
            pc = 1 - c if k & 1 else c
            if per_peer:
                direct.append((t, k, remote(s.at[4 * px + 2 * py + pc], d.at[me], t, k, (px, py, pc))))
            elif k == 1 or not k & 1:
                direct.append((t, k, remote(s, d.at[me], t, k, (px, py, pc))))
            else:
                origin = d.at[4 * px + 2 * py + c]
                relays.append((t, k - 1, remote(origin, origin, t, k, (x, y, 1 - c))))
    return direct, relays


def _exchange_start(direct):
    for _, _, cp in direct:
        cp.start()


def _exchange_relay(direct, relays):
    landed = {(t, k): cp for t, k, cp in direct}
    for t, j, cp in relays:
        landed[(t, j)].wait_recv()
        cp.start()


def _exchange_finish(direct, relays):
    relayed = {(t, j) for t, j, _ in relays}
    for t, k, cp in direct:
        if k == 0:
            cp.wait()
        else:
            cp.wait_send()
            if (t, k) not in relayed:
                cp.wait_recv()
    for _, _, cp in relays:
        cp.wait()


def _side_exchange(src_refs, dst_refs, per_peer, sems, *grid):
    if not src_refs:
        return
    step, total = 0, 1
    for a, n in enumerate(grid):
        step, total = step * n + pl.program_id(a), total * n

    @pl.when(step == 0)
    def _():
        _exchange_start(_peer_copies(src_refs, dst_refs, per_peer, sems)[0])

    @pl.when(step == (3 * total) // 4)
    def _():
        _exchange_relay(*_peer_copies(src_refs, dst_refs, per_peer, sems))

    @pl.when(step == total - 1)
    def _():
        _exchange_finish(*_peer_copies(src_refs, dst_refs, per_peer, sems))


def _exchange(name, srcs, per_peer):
    n = len(srcs)

    def body(*refs):
        direct, relays = _peer_copies(refs[:n], refs[n:2 * n], per_peer, refs[2 * n:])
        _exchange_start(direct)
        _exchange_relay(direct, relays)
        _exchange_finish(direct, relays)

    return pl.pallas_call(
        body, name=name, in_specs=[_HBM_SPEC] * n, out_specs=[_HBM_SPEC] * n,
        out_shape=_side_out_shapes(srcs, per_peer), scratch_shapes=_side_sems(n),
    )(*srcs)


FOX_T = 512
_NEG = -1e30
_D2 = (((1,), (1,)), ((), ()))
_D1 = (((1,), (0,)), ((), ()))
_D0 = (((0,), (0,)), ((), ()))


def _bdot(a, b, dims):
    return lax.dot_general(a.astype(bf16), b.astype(bf16), dims, preferred_element_type=f32)


def _pick_lane(x, lane):
    idx = lax.broadcasted_iota(jnp.int32, x.shape, 1)
    return jnp.sum(jnp.where(idx == lane, x, 0.0), axis=1, keepdims=True)


def _pick_row(x, row):
    idx = lax.broadcasted_iota(jnp.int32, x.shape, 0)
    return jnp.sum(jnp.where(idx == row, x, 0.0), axis=0, keepdims=True)


def _fox_fwd(qkv, c, c_rows, batch, seq, side=None):
    t = min(FOX_T, seq)
    nq = seq // t
    scale = HD ** -0.5
    srcs, per_peer = side if side is not None else ([], False)
    n_s = len(srcs)

    def body(*refs):
        q_ref, k_ref, v_ref, cq_ref, ck_ref = refs[:5]
        o_ref, lse_ref = refs[5 + n_s:7 + n_s]
        _side_exchange(refs[5:5 + n_s], refs[7 + n_s:7 + 2 * n_s], per_peer, refs[7 + 2 * n_s:], batch, PAIRS, nq)
        pair, i = pl.program_id(1), pl.program_id(2)
        lane = lax.broadcasted_iota(jnp.int32, (1, PAIR_W), 1)
        first = (lane // HD) == 0
        mine = [first, jnp.logical_not(first)]
        q = q_ref[...] * scale
        qs = [jnp.where(mine[e], q, 0.0) for e in range(2)]
        cqs = [_pick_lane(cq_ref[...], 2 * pair + e) for e in range(2)]
        causal = lax.broadcasted_iota(jnp.int32, (t, t), 1) <= lax.broadcasted_iota(jnp.int32, (t, t), 0)

        def block(j, carry, diagonal):
            rows = pl.ds(pl.multiple_of(j * t, t), t)
            kj, vj = k_ref[rows, :], v_ref[rows, :]
            ck_blk = ck_ref[0, :, rows]
            out = []
            for e in range(2):
                m, acc = carry[2 * e:2 * e + 2]
                s = _bdot(qs[e], kj, _D2) + cqs[e] - _pick_row(ck_blk, 2 * pair + e)
                if diagonal:
                    s = jnp.where(causal, s, _NEG)
                m_new = jnp.maximum(m, jnp.max(s, axis=1, keepdims=True))
                p = jnp.exp(s - m_new)
                out += [m_new, jnp.exp(m - m_new) * acc + _bdot(p, jnp.where(mine[e], vj, 1.0), _D1)]
            return tuple(out)

        init = (jnp.full((t, 1), _NEG, f32), jnp.zeros((t, PAIR_W), f32)) * 2
        carry = lax.fori_loop(0, i, lambda j, cr: block(j, cr, False), init)
        m0, a0, m1, a1 = block(i, carry, True)
        l0, l1 = _pick_lane(a0, HD), _pick_lane(a1, 0)
        o_ref[...] = jnp.where(first, a0 / l0, a1 / l1)
        lse_ref[...] = jnp.where(lane == 0, m0 + jnp.log(l0), jnp.where(lane == 1, m1 + jnp.log(l1), 0.0))

    q_spec = pl.BlockSpec((t, PAIR_W), lambda b, p, i: (b * nq + i, p))
    res = pl.pallas_call(
        body, name="fox_attn_fwd", grid=(batch, PAIRS, nq),
        in_specs=[q_spec,
                  pl.BlockSpec((seq, PAIR_W), lambda b, p, i: (b, PAIRS + p)),
                  pl.BlockSpec((seq, PAIR_W), lambda b, p, i: (b, 2 * PAIRS + p)),
                  pl.BlockSpec((t, 128), lambda b, p, i: (b * nq + i, 0)),
                  pl.BlockSpec((1, 8, seq), lambda b, p, i: (b, 0, 0))] + [_HBM_SPEC] * n_s,
        out_specs=[q_spec, q_spec] + [_HBM_SPEC] * n_s,
        out_shape=[jax.ShapeDtypeStruct((batch * seq, HW), f32)] * 2 + _side_out_shapes(srcs, per_peer),
        scratch_shapes=_side_sems(n_s),
        compiler_params=_cp(("arbitrary", "arbitrary", "arbitrary")),
    )(qkv, qkv, qkv, c, c_rows, *srcs)
    return res[0], res[1], list(res[2:])


def _fox_bwd(qkv, c, c_rows, o, lse, do, batch, seq):
    t = min(FOX_T, seq)
    nq = seq // t
    scale = HD ** -0.5

    def body(q_ref, k_ref, v_ref, cq_ref, ck_ref, o_ref, lse_ref, do_ref,
             dq_ref, dk_ref, dv_ref, dcq_ref, dck_ref, acc0, acc1):
        pair, i = pl.program_id(1), pl.program_id(2)
        accs = [acc0, acc1]

        @pl.when(i == 0)
        def _():
            dv_ref[...] = jnp.zeros_like(dv_ref)
            acc0[...] = jnp.zeros_like(acc0)
            acc1[...] = jnp.zeros_like(acc1)

        lane = lax.broadcasted_iota(jnp.int32, (1, PAIR_W), 1)
        first = (lane // HD) == 0
        mine = [first, jnp.logical_not(first)]
        q, d_o, o_i = q_ref[...] * scale, do_ref[...], o_ref[...]
        q0s = [jnp.where(mine[e], q, 0.0) for e in range(2)]
        q1s = [jnp.where(mine[e], q, 1.0) for e in range(2)]
        dos = [jnp.where(mine[e], d_o, 0.0) for e in range(2)]
        deltas = [jnp.sum(dos[e] * o_i, axis=1, keepdims=True) for e in range(2)]
        lses = [_pick_lane(lse_ref[...], e) for e in range(2)]
        cqs = [_pick_lane(cq_ref[...], 2 * pair + e) for e in range(2)]
        causal = lax.broadcasted_iota(jnp.int32, (t, t), 1) <= lax.broadcasted_iota(jnp.int32, (t, t), 0)

        def block(j, dqs, diagonal):
            rows = pl.ds(pl.multiple_of(j * t, t), t)
            kj, vj = k_ref[rows, :], v_ref[rows, :]
            ck_blk = ck_ref[0, :, rows]
            out = []
            for e in range(2):
                s = _bdot(q0s[e], kj, _D2) + cqs[e] - _pick_row(ck_blk, 2 * pair + e)
                if diagonal:
                    s = jnp.where(causal, s, _NEG)
                p = jnp.exp(s - lses[e])
                ds = p * (_bdot(dos[e], vj, _D2) - deltas[e])
                dv_ref[rows, :] += _bdot(p, dos[e], _D0)
                accs[e][rows, :] += _bdot(ds, q1s[e], _D0)
                out.append(dqs[e] + _bdot(ds, jnp.where(mine[e], kj, 1.0), _D1))
            return tuple(out)

        zero = jnp.zeros((t, PAIR_W), f32)
        dqs = lax.fori_loop(0, i, lambda j, cr: block(j, cr, False), (zero, zero))
        dq0, dq1 = block(i, dqs, True)
        dq_ref[...] = jnp.where(first, dq0, dq1) * scale
        dcq_ref[...] = jnp.where(lane == 0, _pick_lane(dq0, HD), jnp.where(lane == 1, _pick_lane(dq1, 0), 0.0))

        @pl.when(i == nq - 1)
        def _():
            a0, a1 = acc0[...], acc1[...]
            dk_ref[...] = jnp.where(first, a0, a1)
            dck_ref[...] = jnp.where(lane == 0, -_pick_lane(a0, HD), jnp.where(lane == 1, -_pick_lane(a1, 0), 0.0))

    blk = lambda col: pl.BlockSpec((t, PAIR_W), lambda b, p, i: (b * nq + i, col * PAIRS + p))
    whole = lambda col: pl.BlockSpec((seq, PAIR_W), lambda b, p, i: (b, col * PAIRS + p))
    t_all = batch * seq
    return pl.pallas_call(
        body, name="fox_attn_bwd", grid=(batch, PAIRS, nq),
        in_specs=[blk(0), whole(1), whole(2),
                  pl.BlockSpec((t, 128), lambda b, p, i: (b * nq + i, 0)),
                  pl.BlockSpec((1, 8, seq), lambda b, p, i: (b, 0, 0)),
                  blk(0), blk(0), blk(0)],
        out_specs=[blk(0), whole(0), whole(0), blk(0), whole(0)],
        out_shape=[jax.ShapeDtypeStruct((t_all, HW), f32)] * 5,
        scratch_shapes=[pltpu.VMEM((seq, PAIR_W), f32), pltpu.VMEM((seq, PAIR_W), f32)],
        compiler_params=_cp(("parallel", "parallel", "arbitrary")),
    )(qkv, qkv, qkv, c, c_rows, o, lse, do)


MEM_TQ = 1024


def _mem_block(q, km, vm):
    nn, nt, _ = _make_mm(False, False)
    logits = nt(q, km) * (MEM_HD ** -0.5)
    m = lax.stop_gradient(jnp.max(logits, axis=-1, keepdims=True))
    e = jnp.exp(logits - m)
    return nn(e / jnp.sum(e, axis=-1, keepdims=True), vm)


def _mem_specs(seq, tq):
    nq = seq // tq
    qs = pl.BlockSpec((tq, MEM_HD), lambda b, h, i: (b * nq + i, h))
    ks = pl.BlockSpec((MEM_LEN, MEM_HD), lambda b, h, i: (b, h))
    vs = pl.BlockSpec((MEM_LEN, MEM_HD), lambda b, h, i: (b, MEM_HEADS + h))
    return nq, qs, ks, vs


def _mem_fwd(q, mem_kv, batch, seq):
    tq = min(MEM_TQ, seq)
    nq, qs, ks, vs = _mem_specs(seq, tq)

    def body(q_ref, k_ref, v_ref, o_ref):
        o_ref[...] = _mem_block(q_ref[...].astype(f32), k_ref[...], v_ref[...]).astype(o_ref.dtype)

    return pl.pallas_call(
        body, name="mem_attn_fwd", grid=(batch, MEM_HEADS, nq),
        in_specs=[qs, ks, vs], out_specs=qs, out_shape=jax.ShapeDtypeStruct(q.shape, bf16),
        compiler_params=_cp(("parallel", "parallel", "arbitrary")),
    )(q, mem_kv, mem_kv)


def _mem_bwd(q, mem_kv, do, batch, seq):
    tq = min(MEM_TQ, seq)
    nq, qs, ks, vs = _mem_specs(seq, tq)

    def body(q_ref, k_ref, v_ref, do_ref, dq_ref, dk_ref, dv_ref):
        _, vjp = jax.vjp(_mem_block, q_ref[...].astype(f32), k_ref[...], v_ref[...])
        dq, dk, dv = vjp(do_ref[...])
        dq_ref[...] = dq.astype(dq_ref.dtype)

        @pl.when(pl.program_id(2) == 0)
        def _():
            dk_ref[...] = jnp.zeros_like(dk_ref)
            dv_ref[...] = jnp.zeros_like(dv_ref)

        dk_ref[...] += dk
        dv_ref[...] += dv

    return pl.pallas_call(
        body, name="mem_attn_bwd", grid=(batch, MEM_HEADS, nq),
        in_specs=[qs, ks, vs, qs], out_specs=[qs, ks, ks],
        out_shape=[jax.ShapeDtypeStruct(q.shape, bf16), jax.ShapeDtypeStruct((batch * MEM_LEN, MEM_W), f32),
                   jax.ShapeDtypeStruct((batch * MEM_LEN, MEM_W), f32)],
        compiler_params=_cp(("parallel", "parallel", "arbitrary")),
    )(q, mem_kv, mem_kv, do)


@jax.custom_vjp
def _halves(x):
    c = x.shape[1] // 2
    return x[:, :c], x[:, c:]


_halves.defvjp(lambda x: ((x[:, :x.shape[1] // 2], x[:, x.shape[1] // 2:]), None),
               lambda _, g: (jnp.concatenate(g, axis=1),))


@jax.custom_vjp
def _lead_halves(x):
    n = x.shape[0] // 2
    return x[:n], x[n:]


_lead_halves.defvjp(lambda x: ((x[:x.shape[0] // 2], x[x.shape[0] // 2:]), None),
                    lambda _, g: (jnp.concatenate(g, axis=0),))


def _scan_chunk(s0, r, wl, k, v, a, b):
    nn, nt, tn = _make_mm(True, False)
    nn_exact, _, _ = _make_mm(True, True)
    _, nt_exact, _ = _make_mm(True, "split")
    hp, c, lanes = r.shape
    row = lax.broadcasted_iota(jnp.int32, (c, c), 0)
    col = lax.broadcasted_iota(jnp.int32, (c, c), 1)
    first = (lax.broadcasted_iota(jnp.int32, (1, 1, lanes), 2) // HD) == 0
    tri = jnp.broadcast_to((col <= row).astype(f32)[None], (hp, c, c))
    lg = nn_exact(tri, wl)
    lg_end = lg[:, c - 1:c, :]
    grow, shrink, to_end = jnp.exp(lg), jnp.exp(-lg), jnp.exp(lg_end - lg)
    rt, kt, bt, at = r * grow, k * shrink, b * shrink, a * jnp.exp(lg - wl)
    strict, incl = (col < row)[None], (col <= row)[None]
    twice = lambda t: jnp.concatenate([t, t], axis=0)
    queries = jnp.concatenate([at, rt], axis=1)
    per_head = jnp.concatenate([jnp.where(first, queries, 0.0), jnp.where(first, 0.0, queries)], axis=0)
    (ab, rb), (ak, rk) = _halves(nt_exact(per_head, twice(bt))), _halves(nt_exact(per_head, twice(kt)))
    l_ab = jnp.where(strict, ab, 0.0)
    a_ak = jnp.where(strict, ak, 0.0)
    a_rb = jnp.where(incl, rb, 0.0)
    a_rk = jnp.where(incl, rk, 0.0)
    inv = (col == row).astype(f32)[None] + l_ab
    power, n = l_ab, 1
    while 2 * n < c:
        power = nn(power, power)
        inv = inv + nn(inv, power)
        n *= 2

    def apply(m, t):
        lo, hi = _lead_halves(nn(m, twice(t)))
        return jnp.where(first, lo, hi)

    sa = apply(inv, nt(at, s0) + apply(a_ak, v))
    y = nt(rt, s0) + apply(a_rk, v) + apply(a_rb, sa)
    same_head = ((lax.broadcasted_iota(jnp.int32, (lanes, lanes), 0) // HD)
                 == (lax.broadcasted_iota(jnp.int32, (lanes, lanes), 1) // HD))[None]
    s1 = s0 * jnp.exp(lg_end) + jnp.where(same_head, tn(v, k * to_end) + tn(sa, b * to_end), 0.0)
    return y, s1


PAIRS = HEADS // 2
PAIR_W = 2 * HD
SCAN_ARGS = (0, 3, 1, 2, 4, 5)


def _pair_stack(ref, off):
    return jnp.stack([ref[b, :, off + p * PAIR_W:off + (p + 1) * PAIR_W]
                      for b in range(ref.shape[0]) for p in range(PAIRS)])


def _pair_store(ref, off, val, add_ref=None):
    for b in range(ref.shape[0]):
        for p in range(PAIRS):
            sl = slice(off + p * PAIR_W, off + (p + 1) * PAIR_W)
            v = val[b * PAIRS + p]
            ref[b, :, sl] = v if add_ref is None else v + add_ref[b, :, sl]


def _scan_fwd(main6, batch, seq, side=None):
    c = min(SCAN_CHUNK, seq)
    nc = seq // c
    hp = batch * PAIRS
    srcs, per_peer = side if side is not None else ([], False)
    n_s = len(srcs)

    def body(*refs):
        z_ref, y_ref, s_ref, st = refs[0], refs[1 + n_s], refs[2 + n_s], refs[3 + 2 * n_s]
        _side_exchange(refs[1:1 + n_s], refs[3 + n_s:3 + 2 * n_s], per_peer, refs[4 + 2 * n_s:], nc)

        @pl.when(pl.program_id(0) == 0)
        def _():
            st[...] = jnp.zeros_like(st)

        s0 = st[...]
        s_ref[0] = s0
        y, s1 = _scan_chunk(s0, *[_pair_stack(z_ref, comp * HW) for comp in SCAN_ARGS])
        _pair_store(y_ref, 0, y)
        st[...] = s1

    res = pl.pallas_call(
        body, name="rwkv_scan_fwd", grid=(nc,),
        in_specs=[pl.BlockSpec((batch, c, 6 * HW), lambda i: (0, i, 0))] + [_HBM_SPEC] * n_s,
        out_specs=[pl.BlockSpec((batch, c, HW), lambda i: (0, i, 0)),
                   pl.BlockSpec((1, hp, PAIR_W, PAIR_W), lambda i: (i, 0, 0, 0))] + [_HBM_SPEC] * n_s,
        out_shape=[jax.ShapeDtypeStruct((batch, seq, HW), f32), jax.ShapeDtypeStruct((nc, hp, PAIR_W, PAIR_W), f32)]
        + _side_out_shapes(srcs, per_peer),
        scratch_shapes=[pltpu.VMEM((hp, PAIR_W, PAIR_W), f32)] + _side_sems(n_s),
        compiler_params=_cp(("arbitrary",)),
    )(main6.reshape(batch, seq, 6 * HW), *srcs)
    return res[0].reshape(batch * seq, HW), res[1], list(res[2:])


def _scan_bwd(main6, states, dy, extra, batch, seq, side=None):
    c = min(SCAN_CHUNK, seq)
    nc = seq // c
    hp = batch * PAIRS
    srcs, per_peer = side if side is not None else ([], False)
    n_s = len(srcs)

    def body(*refs):
        z_ref, s_ref, dy_ref, ex_ref = refs[:4]
        dz_ref, dst = refs[4 + n_s], refs[5 + 2 * n_s]
        _side_exchange(refs[4:4 + n_s], refs[5 + n_s:5 + 2 * n_s], per_peer, refs[6 + 2 * n_s:], nc)

        @pl.when(pl.program_id(0) == 0)
        def _():
            dst[...] = jnp.zeros_like(dst)

        _, vjp = jax.vjp(_scan_chunk, s_ref[0], *[_pair_stack(z_ref, comp * HW) for comp in SCAN_ARGS])
        g = vjp((_pair_stack(dy_ref, 0), dst[...]))
        dst[...] = g[0]
        for arg, comp in enumerate(SCAN_ARGS):
            _pair_store(dz_ref, comp * HW, g[1 + arg], ex_ref if comp < 3 else None)

    back = lambda i: (0, nc - 1 - i, 0)
    wide = pl.BlockSpec((batch, c, 6 * HW), back)
    res = pl.pallas_call(
        body, name="rwkv_scan_bwd", grid=(nc,),
        in_specs=[wide, pl.BlockSpec((1, hp, PAIR_W, PAIR_W), lambda i: (nc - 1 - i, 0, 0, 0)),
                  pl.BlockSpec((batch, c, HW), back), pl.BlockSpec((batch, c, 3 * HW), back)] + [_HBM_SPEC] * n_s,
        out_specs=[wide] + [_HBM_SPEC] * n_s,
        out_shape=[jax.ShapeDtypeStruct((batch, seq, 6 * HW), f32)] + _side_out_shapes(srcs, per_peer),
        scratch_shapes=[pltpu.VMEM((hp, PAIR_W, PAIR_W), f32)] + _side_sems(n_s),
        compiler_params=_cp(("arbitrary",)),
    )(main6.reshape(batch, seq, 6 * HW), states, dy.reshape(batch, seq, HW), extra.reshape(batch, seq, 3 * HW), *srcs)
    return res[0].reshape(batch * seq, 6 * HW), list(res[1:])


def _pad_cols(x, width):
    return jnp.pad(x, ((0, 0), (0, width - x.shape[1])))


def _split_w_in(wt):
    z = lambda rows: jnp.zeros((rows, wt.shape[1]), wt.dtype)
    w_r = jnp.concatenate([wt[1544:3080], wt[3080:3144], z(64), wt[3144:3208], z(64), wt[3208:3336]], axis=0)
    return wt[:1536], jnp.concatenate([wt[1536:1544], z(120)], axis=0), w_r, wt[3336:3848], wt[3848:]


def _merge_w_in(g_qkv, g_f, g_r, g_mq, g_g):
    return jnp.concatenate([g_qkv, g_f[:8], g_r[:1536], g_r[1536:1600], g_r[1664:1728], g_r[1792:], g_mq, g_g], axis=0)


def _pad_lora(v):
    z64 = jnp.zeros((1, 64), v.dtype)
    return jnp.concatenate([v[:, :1536], v[:, 1536:1600], z64, v[:, 1600:1664], z64, v[:, 1664:]], axis=1)


def _unpad_lora(v):
    return jnp.concatenate([v[:, :1536], v[:, 1536:1600], v[:, 1664:1728], v[:, 1792:]], axis=1)


def _local_step(x, mem, target, w, p, late=None, early=None, last=None):
    batch, seq, _ = x.shape
    t = batch * seq
    x2, tg2, mem2 = x.reshape(t, D), target.reshape(t, D), mem.reshape(batch * MEM_LEN, D)
    w_qkv, w_f, w_r, w_mq, w_g3 = _split_w_in(w["w_in"])
    mu = _pad_lora(p["rwkv_mu"])
    bias = _pad_cols(p["fox_f_bias"], 128)
    r_k = p["rwkv_r_k"].reshape(1, HW)
    post_params = [p["rwkv_gn_g"], p["rwkv_gn_b"], r_k]
    rw_widths = [HW, HW, HW, LORA_PAD, LORA_PAD, LORA_PAD]
    six = [HW] * 6

    p_g, u = _matmul("proj_gate", _lazy(_fn_rms, [(x2, [D])], D, params=[p["pre1_g"]]), w_g3, "nt", out_dtype=bf16)
    p_qkv = _matmul("proj_qkv", u, w_qkv, "nt", out_dtype=bf16)
    p_f = _matmul("proj_f", u, w_f, "nt")
    p_r = _matmul("proj_rwkv", u, w_r, "nt")
    p_mq = _matmul("proj_memq", u, w_mq, "nt", out_dtype=bf16)

    c = _fox_gate_fwd(p_f, bias, batch, seq)
    c_rows = c[:, :HEADS].reshape(batch, seq, HEADS).transpose(0, 2, 1)
    fox_o, lse, gathered = _fox_fwd(p_qkv, c, c_rows, batch, seq, side=(late[0], False) if late else None)
    if late:
        w = {**w, **late[2](gathered, 0)}
    fox_out = fox_o.astype(bf16)

    w_up = jnp.pad(w["rwkv_w_up"].astype(f32), ((0, LORA_PAD - 64), (0, 0)))
    a_up = jnp.pad(w["rwkv_a_up"].astype(f32), ((0, LORA_PAD - 64), (0, 0)))
    pre_params = [p["rwkv_w0"], w_up, p["rwkv_a0"], a_up, w["rwkv_g_up"].astype(f32), p["rwkv_k_k"], p["rwkv_k_a"]]
    ps = _tokshift_fwd(p_r, mu, batch, seq)
    main6, g_rw = _rows_fwd("rwkv_pre", _fn_rwkv_pre, [], [(ps, rw_widths)], pre_params, [six, [HW]], tm=256)
    y_rw, states, gathered = _scan_fwd(main6, batch, seq, side=(late[1], False) if late else None)
    if late:
        w = {**w, **late[2](gathered, 1)}
    post_consts = []
    post_rows = [(y_rw, [HW]), (main6, [HW, HW, HW]), (g_rw, [HW])]
    fn_post = _fn_rwkv_post

    (rwkv_out,) = _rows_fwd("rwkv_post", fn_post, post_consts, post_rows, post_params, [[HW]], dtypes=[bf16], tm=256)

    mem_kv, memn = _matmul("proj_memkv", _lazy(_fn_rms, [(mem2, [D])], D, params=[p["mem_norm_g"]]), w["w_mem_kv"], "nn")
    mem_out = _mem_fwd(p_mq, mem_kv, batch, seq)

    a_fox = _matmul("out_fox", fox_out, w["w_fox_out"], "nn", out_dtype=bf16)
    a_rwkv = _matmul("out_rwkv", rwkv_out, w["w_rwkv_out"], "nn", out_dtype=bf16)
    a_mem = _matmul("out_mem", mem_out, w["w_mem_out"], "nn", out_dtype=bf16)
    merge_rows = [(a_fox, [D]), (a_rwkv, [D]), (a_mem, [D]), (p_g, [D, D, D])]
    yy, merged = _matmul("out_o", _lazy(_fn_merge, merge_rows, D), w["w_o"], "nn")
    post1_rows = [(yy, [D]), (x2, [D])]
    post1_params = [p["post1_g"], p["pre2_g"]]
    h1, u2 = _rows_fwd("post1", _fn_post1, [], post1_rows, post1_params, [[D], [D]], dtypes=[f32, bf16])
    gp = _matmul("ffn_gate", u2, w["w_ffn_gate"], "nt", out_dtype=bf16)
    up = _matmul("ffn_up", u2, w["w_ffn_up"], "nt", out_dtype=bf16)
    ffn, hmid = _matmul("ffn_down", _lazy(_fn_swiglu, [(gp, [D_FF]), (up, [D_FF])], D_FF), w["w_ffn_down"], "nn")
    final_rows = [(ffn, [D]), (h1, [D])]

    gw, gp_ = {}, {}
    (d_ffn, d_h1), (gp_["post2_g"], loss) = _rows_bwd("final", _fn_final, [(tg2, [D])], final_rows, [p["post2_g"]], [], [],
                                                      n_sums=1, dtypes=[bf16, f32])
    gw["w_ffn_down"] = _matmul("ffn_down_dw", hmid, d_ffn, "tn", out_dtype=bf16)
    d_gp, d_up = _matmul_then_vjp("ffn_down_dx", d_ffn, w["w_ffn_down"], "nt", _fn_swiglu,
                                  [(gp, [D_FF]), (up, [D_FF])], [bf16, bf16])
    d_u2 = _matmul("ffn_gate_dx", d_gp, w["w_ffn_gate"], "nn")
    d_u2 = _matmul("ffn_up_dx", d_up, w["w_ffn_up"], "nn", add=d_u2)
    gw["w_ffn_gate"] = _matmul("ffn_gate_dw", d_gp, u2, "tn", out_dtype=bf16)
    gw["w_ffn_up"] = _matmul("ffn_up_dw", d_up, u2, "tn", out_dtype=bf16)
    (d_yy, d_x_res), (gp_["post1_g"], gp_["pre2_g"]) = _rows_bwd(
        "post1_bwd", _fn_post1, [], post1_rows, post1_params, [[D], [D]], [d_h1, d_u2], dtypes=[bf16, f32])
    gw["w_o"] = _matmul("out_o_dw", merged, d_yy, "tn", out_dtype=bf16)
    d_a_fox, d_a_rwkv, d_a_mem, d_p_g = _matmul_then_vjp("out_o_dx", d_yy, w["w_o"], "nt", _fn_merge, merge_rows, [bf16] * 4)
    d_fox_out = _matmul("out_fox_dx", d_a_fox, w["w_fox_out"], "nt")
    gw["w_fox_out"] = _matmul("out_fox_dw", fox_out, d_a_fox, "tn", out_dtype=bf16)
    d_rwkv_out = _matmul("out_rwkv_dx", d_a_rwkv, w["w_rwkv_out"], "nt")
    gw["w_rwkv_out"] = _matmul("out_rwkv_dw", rwkv_out, d_a_rwkv, "tn", out_dtype=bf16)
    d_mem_out = _matmul("out_mem_dx", d_a_mem, w["w_mem_out"], "nt")
    gw["w_mem_out"] = _matmul("out_mem_dw", mem_out, d_a_mem, "tn", out_dtype=bf16)

    d_p_mq, d_km, d_vm = _mem_bwd(p_mq, mem_kv, d_mem_out, batch, seq)
    d_mem_kv = jnp.concatenate([d_km, d_vm], axis=1).astype(bf16)
    gw["w_mem_kv"] = _matmul("proj_memkv_dw", memn, d_mem_kv, "tn", out_dtype=bf16)
    d_memn = _matmul("proj_memkv_dx", d_mem_kv, w["w_mem_kv"], "nt")
    _, (gp_["mem_norm_g"],) = _rows_bwd("rms_mem_bwd", _fn_rms, [], [(mem2, [D])], [p["mem_norm_g"]], [[D]], [d_memn])

    d_q, d_k, d_v, d_cq, d_ck = _fox_bwd(p_qkv, c, c_rows, fox_o, lse, d_fox_out, batch, seq)
    d_p_qkv = jnp.concatenate([d_q, d_k, d_v], axis=1).astype(bf16)
    d_p_f, d_bias = _fox_gate_bwd(p_f, bias, d_cq, d_ck, batch, seq)
    gp_["fox_f_bias"] = d_bias[:, :HEADS]

    (d_y_rw, d_main6_post, d_g_rw), (gp_["rwkv_gn_g"], gp_["rwkv_gn_b"], d_rk) = _rows_bwd(
        "rwkv_post_bwd", fn_post, post_consts, post_rows, post_params, [[HW]], [d_rwkv_out], tm=256)
    gp_["rwkv_r_k"] = d_rk.reshape(1, HEADS, HD)
    d_main6, early_got = _scan_bwd(main6, states, d_y_rw, d_main6_post, batch, seq,
                                   side=(early(gw), True) if early else None)

    def fn_pre_sum(*args):
        return _fn_rwkv_pre(*args)

    (d_ps,), d_pre = _rows_bwd("rwkv_pre_bwd", fn_pre_sum, [], [(ps, rw_widths)], pre_params, [six, [HW]],
                               [d_main6, d_g_rw], tm=256)
    gp_["rwkv_w0"], d_w_up, gp_["rwkv_a0"], d_a_up, gw["rwkv_g_up"], gp_["rwkv_k_k"], gp_["rwkv_k_a"] = d_pre
    gw["rwkv_w_up"], gw["rwkv_a_up"] = d_w_up[:64], d_a_up[:64]
    d_p_r, d_mu = _tokshift_bwd(p_r, mu, d_ps, batch, seq)
    gp_["rwkv_mu"] = _unpad_lora(d_mu)

    gw["w_in"] = _merge_w_in(_matmul("proj_qkv_dw", d_p_qkv, u, "tn", out_dtype=bf16), _matmul("proj_f_dw", d_p_f, u, "tn", out_dtype=bf16),
                             _matmul("proj_rwkv_dw", d_p_r, u, "tn", out_dtype=bf16), _matmul("proj_memq_dw", d_p_mq, u, "tn", out_dtype=bf16),
                             _matmul("proj_gate_dw", d_p_g, u, "tn", out_dtype=bf16))
    d_x, gp_["pre1_g"], last_got = _input_cotangent(
        "proj_dx", [d_p_qkv, d_p_f, d_p_r, d_p_mq, d_p_g], [w_qkv, w_f, w_r, w_mq, w_g3], x2, p["pre1_g"], d_x_res,
        side=(last(gw), True) if last else None)
    return loss, d_x.reshape(x.shape), gw, gp_, early_got, last_got


def _adamw(name, recv, row_off, w, m, v):
    _, rows, cols = w.shape
    row_tiles = [t for t in range(16, min(rows, 128) + 1, 16) if rows % t == 0 and row_off % t == 0]
    if row_tiles:
        tr, tc = max(row_tiles), cols
        first, grid = row_off // tr, (rows // tr,)
        at = lambda i: (0, first + i, 0)
        mine = lambda i: (0, i, 0)
    else:
        assert row_off == 0 and recv.shape[1] == rows
        tr, tc = rows, 128
        grid = (cols // tc,)
        at = mine = lambda i: (0, 0, i)

    def body(g_ref, w_ref, m_ref, v_ref, go_ref, d_ref, mo_ref, vo_ref):
        g = g_ref[0].astype(f32)
        for s in range(1, N_DEV):
            g = g + g_ref[s].astype(f32)
        m_new = ADAM_B1 * m_ref[0] + (1.0 - ADAM_B1) * g
        v_new = ADAM_B2 * v_ref[0] + (1.0 - ADAM_B2) * (g * g)
        m_hat = m_new / (1.0 - ADAM_B1 ** ADAM_STEP)
        v_hat = v_new / (1.0 - ADAM_B2 ** ADAM_STEP)
        go_ref[0] = g
        d_ref[0] = -ADAM_LR * (m_hat / (jnp.sqrt(v_hat) + ADAM_EPS) + ADAM_WD * w_ref[0])
        mo_ref[0] = m_new
        vo_ref[0] = v_new

    spec = pl.BlockSpec((1, tr, tc), mine)
    return pl.pallas_call(
        body, name=name, grid=grid,
        in_specs=[pl.BlockSpec((N_DEV, tr, tc), at), spec, spec, spec],
        out_specs=[spec] * 4, out_shape=[jax.ShapeDtypeStruct(w.shape, f32)] * 4,
        compiler_params=_cp(("parallel",)),
    )(recv, w, m, v)


GROUPS = (
    ("in", ("w_in",), 0),
    ("memkv", ("w_mem_kv",), 0),
    ("ffn_gu", ("w_ffn_gate", "w_ffn_up"), 0),
    ("down_o", ("w_ffn_down", "w_o"), 0),
    ("outs", ("w_fox_out", "w_rwkv_out", "w_mem_out"), 0),
    ("lora", ("rwkv_w_up", "rwkv_a_up", "rwkv_g_up"), 0),
)
FIRST_GROUPS = ("in", "memkv")
LATE_GROUPS = (("down_o", "outs", "lora"), ("ffn_gu",))
EARLY_GRAD_GROUPS = ("memkv", "ffn_gu", "down_o", "outs")
LAST_GRAD_GROUPS = ("in", "lora")
SHARD_AXIS = {n: a for n, _, a in SHARDED}
SMALL_ROWS = 16


def _group_local(shards, members, join):
    parts = [shards[n].reshape(shards[n].shape[-2:]) for n in members]
    return parts[0] if len(parts) == 1 else jnp.concatenate(parts, axis=join)


def _group_split(arr, members, join, lead=False):
    out, off = {}, 0
    for n in members:
        shape = dict((k, s) for k, s, _ in SHARDED)[n]
        size = _block_shape(shape, SHARD_AXIS[n])[join]
        idx = [slice(None)] * arr.ndim
        idx[arr.ndim - 2 + join] = slice(off, off + size)
        out[n] = arr[tuple(idx)]
        off += size
    return out


def _full_from_blocks(blocks, axis):
    if axis == 0:
        return blocks.reshape(-1, blocks.shape[2])
    return blocks.transpose(1, 0, 2).reshape(blocks.shape[1], -1)


def _blocks_from_full(full, axis):
    if axis == 0:
        return full.reshape(N_DEV, -1, full.shape[1])
    return full.reshape(full.shape[0], N_DEV, -1).transpose(1, 0, 2)


def _assemble(gathered, names):
    out = {}
    for arr, g in zip(gathered, names):
        _, members, join = [grp for grp in GROUPS if grp[0] == g][0]
        for n, blk in _group_split(arr, members, join, lead=True).items():
            out[n] = _full_from_blocks(blk, SHARD_AXIS[n])
    return out


def _grad_blocks(gw, names):
    out = []
    for g in names:
        _, members, join = [grp for grp in GROUPS if grp[0] == g][0]
        parts = [_blocks_from_full(gw[n].astype(bf16), SHARD_AXIS[n]) for n in members]
        out.append(parts[0] if len(parts) == 1 else jnp.concatenate(parts, axis=1 + join))
    return out


def _small_pack(d):
    flat = jnp.concatenate([d[n].reshape(-1) for n, _ in REPLICATED])
    return jnp.pad(flat, (0, SMALL_ROWS * LANES - REPL_ELEMS)).reshape(SMALL_ROWS, LANES)


def _small_unpack(packed):
    out, flat, off = {}, packed.reshape(-1), 0
    for n, shape in REPLICATED:
        k = _rows_of((LANES,) + shape)
        out[n] = flat[off:off + k].reshape(shape)
        off += k
    return out


def kernel(x, mem, pre1_g, post1_g, pre2_g, post2_g, mem_norm_g, w_in, fox_f_bias, rwkv_mu, rwkv_w0, rwkv_w_up, rwkv_a0, rwkv_a_up, rwkv_g_up, rwkv_k_k, rwkv_k_a, rwkv_r_k, rwkv_gn_g, rwkv_gn_b, w_mem_kv, w_fox_out, w_rwkv_out, w_mem_out, w_o, w_ffn_gate, w_ffn_up, w_ffn_down, loss_target, m_pre1_g, m_post1_g, m_pre2_g, m_post2_g, m_mem_norm_g, m_w_in, m_fox_f_bias, m_rwkv_mu, m_rwkv_w0, m_rwkv_w_up, m_rwkv_a0, m_rwkv_a_up, m_rwkv_g_up, m_rwkv_k_k, m_rwkv_k_a, m_rwkv_r_k, m_rwkv_gn_g, m_rwkv_gn_b, m_w_mem_kv, m_w_fox_out, m_w_rwkv_out, m_w_mem_out, m_w_o, m_w_ffn_gate, m_w_ffn_up, m_w_ffn_down, v_pre1_g, v_post1_g, v_pre2_g, v_post2_g, v_mem_norm_g, v_w_in, v_fox_f_bias, v_rwkv_mu, v_rwkv_w0, v_rwkv_w_up, v_rwkv_a0, v_rwkv_a_up, v_rwkv_g_up, v_rwkv_k_k, v_rwkv_k_a, v_rwkv_r_k, v_rwkv_gn_g, v_rwkv_gn_b, v_w_mem_kv, v_w_fox_out, v_w_rwkv_out, v_w_mem_out, v_w_o, v_w_ffn_gate, v_w_ffn_up, v_w_ffn_down):
    args = dict(locals())
    turn = lambda n, a: jnp.swapaxes(a, 1, 2) if n in TRANSPOSED else a
    wts = {n: turn(n, args[n]) for n in WEIGHT_ORDER}
    ms = {n: turn(n, args["m_" + n]) for n in WEIGHT_ORDER}
    vs = {n: turn(n, args["v_" + n]) for n in WEIGHT_ORDER}

    groups = {g: (members, join) for g, members, join in GROUPS}
    w_bf16 = {n: wts[n].astype(bf16) for n, _, _ in SHARDED}

    def send(g):
        return _group_local(w_bf16, *groups[g])

    first = _exchange("gather_first", [send(g) for g in FIRST_GROUPS], per_peer=False)
    full = _assemble(first, FIRST_GROUPS)
    small_in = {n: (wts[n] if n == "rwkv_r_k" else wts[n].reshape(wts[n].shape[-2:])) for n, _ in REPLICATED}
    late = ([send(g) for g in LATE_GROUPS[0]], [send(g) for g in LATE_GROUPS[1]],
            lambda got, which: _assemble(got, LATE_GROUPS[which]))
    loss_part, grad_x, gw, gp, early_got, last_got = _local_step(
        x, mem, loss_target, full, small_in, late=late, early=lambda g: _grad_blocks(g, EARLY_GRAD_GROUPS),
        last=lambda g: _grad_blocks(g, LAST_GRAD_GROUPS))
    (small_got,) = _exchange("exchange_small", [_small_pack(gp).astype(bf16)], per_peer=False)
    received = dict(zip(EARLY_GRAD_GROUPS + LAST_GRAD_GROUPS, list(early_got) + list(last_got)))

    outs = [{}, {}, {}, {}]
    for g, members, _ in GROUPS:
        off = 0
        for n in members:
            for o, arr in zip(outs, _adamw("adamw_" + n, received[g], off, wts[n], ms[n], vs[n])):
                o[n] = arr
            off += wts[n].shape[1]
    res = _adamw("adamw_small", small_got, 0, *[_small_pack(d)[None] for d in (wts, ms, vs)])
    for o, arr in zip(outs, res):
        o.update(_small_unpack(arr))
    loss = lax.psum(loss_part[0, 0], ("x", "y", "c"))
    return (loss, grad_x, *[turn(n, o[n].reshape(wts[n].shape)) for o in outs for n in WEIGHT_ORDER])
```

```python
import functools

import jax
import jax.numpy as jnp
from jax import lax
from jax.experimental import pallas as pl
from jax.experimental.pallas import tpu as pltpu

f32 = jnp.float32
bf16 = jnp.bfloat16
_HI = lax.Precision.HIGHEST

D = 1024
HEADS = 8
HD = 64
HW = HEADS * HD
MEM_HEADS = 4
MEM_HD = 128
MEM_W = 512
MEM_LEN = 256
D_FF = 2816
LORA_PAD = 128
NORM_EPS = 1e-6
GN_EPS = 64e-5
SCAN_CHUNK = 64
N_DEV = 8
LANES = 1024
VMEM_LIMIT = 56 * 1024 * 1024

ADAM_LR = 0.001
ADAM_B1 = 0.9
ADAM_B2 = 0.999
ADAM_EPS = 1e-08
ADAM_WD = 0.01
ADAM_STEP = 10

TRANSPOSED = ("w_in", "w_ffn_gate", "w_ffn_up")
SHARDED = (
    ("w_in", (6920, 1024), 0),
    ("w_ffn_gate", (2816, 1024), 0),
    ("w_ffn_up", (2816, 1024), 0),
    ("w_ffn_down", (2816, 1024), 0),
    ("w_mem_kv", (1024, 1024), 0),
    ("w_o", (1024, 1024), 0),
    ("w_fox_out", (512, 1024), 1),
    ("w_rwkv_out", (512, 1024), 1),
    ("w_mem_out", (512, 1024), 1),
    ("rwkv_w_up", (64, 512), 1),
    ("rwkv_a_up", (64, 512), 1),
    ("rwkv_g_up", (128, 512), 1),
)
REPLICATED = (
    ("pre1_g", (1, 1024)), ("post1_g", (1, 1024)), ("pre2_g", (1, 1024)), ("post2_g", (1, 1024)),
    ("mem_norm_g", (1, 1024)), ("fox_f_bias", (1, 8)), ("rwkv_mu", (1, 1792)), ("rwkv_w0", (1, 512)),
    ("rwkv_a0", (1, 512)), ("rwkv_k_k", (1, 512)), ("rwkv_k_a", (1, 512)), ("rwkv_r_k", (1, 8, 64)),
    ("rwkv_gn_g", (1, 512)), ("rwkv_gn_b", (1, 512)),
)
WEIGHT_ORDER = ('pre1_g', 'post1_g', 'pre2_g', 'post2_g', 'mem_norm_g', 'w_in', 'fox_f_bias', 'rwkv_mu',
                'rwkv_w0', 'rwkv_w_up', 'rwkv_a0', 'rwkv_a_up', 'rwkv_g_up', 'rwkv_k_k', 'rwkv_k_a',
                'rwkv_r_k', 'rwkv_gn_g', 'rwkv_gn_b', 'w_mem_kv', 'w_fox_out', 'w_rwkv_out', 'w_mem_out',
                'w_o', 'w_ffn_gate', 'w_ffn_up', 'w_ffn_down')


def _block_shape(shape, axis):
    return tuple(s // N_DEV if i == axis else s for i, s in enumerate(shape))


def _rows_of(shape):
    n = 1
    for s in shape:
        n *= s
    return n // LANES


REPL_ELEMS = sum(_rows_of((LANES,) + s) for _, s in REPLICATED)


def _cp(sem=None):
    return pltpu.CompilerParams(dimension_semantics=sem, vmem_limit_bytes=VMEM_LIMIT)


def _tile(dim, cap):
    best = None
    for t in range(128, min(dim, cap) + 1, 128):
        if dim % t == 0:
            best = t
    return best if best is not None else dim


def _two_terms(x):
    hi = x.astype(bf16)
    return hi, (x - hi.astype(f32)).astype(bf16)


def _dg(a, b, dims, exact):
    if exact == "split":
        (a_hi, a_lo), (b_hi, b_lo) = _two_terms(a), _two_terms(b)
        dot = functools.partial(lax.dot_general, dimension_numbers=dims, preferred_element_type=f32)
        return dot(a_hi, b_hi) + (dot(a_hi, b_lo) + dot(a_lo, b_hi))
    if exact:
        return lax.dot_general(a, b, dims, precision=_HI, preferred_element_type=f32)
    return lax.dot_general(a.astype(bf16), b.astype(bf16), dims, preferred_element_type=f32)


def _make_mm(batched, exact):
    o = 1 if batched else 0
    bd = ((0,), (0,)) if batched else ((), ())
    d_nn = (((1 + o,), (o,)), bd)
    d_nt = (((1 + o,), (1 + o,)), bd)
    d_tn = (((o,), (o,)), bd)

    @jax.custom_vjp
    def nn(a, b):
        return _dg(a, b, d_nn, exact)

    @jax.custom_vjp
    def nt(a, b):
        return _dg(a, b, d_nt, exact)

    @jax.custom_vjp
    def tn(a, b):
        return _dg(a, b, d_tn, exact)

    nn.defvjp(lambda a, b: (_dg(a, b, d_nn, exact), (a, b)),
              lambda res, g: (_dg(g, res[1], d_nt, exact), _dg(res[0], g, d_tn, exact)))
    nt.defvjp(lambda a, b: (_dg(a, b, d_nt, exact), (a, b)),
              lambda res, g: (_dg(g, res[1], d_nn, exact), _dg(g, res[0], d_tn, exact)))
    tn.defvjp(lambda a, b: (_dg(a, b, d_tn, exact), (a, b)),
              lambda res, g: (_dg(res[1], g, d_nt, exact), _dg(res[0], g, d_nn, exact)))
    return nn, nt, tn


def _sigmoid(x):
    return 1.0 / (1.0 + jnp.exp(-x))


def _head_sum_raw(x):
    width = 2 * HD
    i = lax.broadcasted_iota(jnp.int32, (width, width), 0) // HD
    j = lax.broadcasted_iota(jnp.int32, (width, width), 1) // HD
    m = (i == j).astype(bf16)
    dims = (((1,), (0,)), ((), ()))
    out = []
    for p in range(x.shape[1] // width):
        xp = x[:, p * width:(p + 1) * width]
        hi = xp.astype(bf16)
        lo = (xp - hi.astype(f32)).astype(bf16)
        out.append(lax.dot_general(hi, m, dims, preferred_element_type=f32)
                   + lax.dot_general(lo, m, dims, preferred_element_type=f32))
    return jnp.concatenate(out, axis=1)


@jax.custom_vjp
def _head_sum(x):
    return _head_sum_raw(x)


_head_sum.defvjp(lambda x: (_head_sum_raw(x), None), lambda _, g: (_head_sum_raw(g),))


WEIGHT_TILE_BYTES = 13 * 512 * 1024
ACC_TILE_BYTES = 8 * 1024 * 1024


def _lazy(fn, rows, width, params=()):
    return (fn, rows, width, list(params))


def _matmul(name, a, b, mode, add=None, out_dtype=f32):
    has_add = add is not None
    if isinstance(a, tuple):
        a_fn, a_rows, a_width, a_params = a
        a_arrays = [r for r, _ in a_rows]
        a_shape = (a_arrays[0].shape[0], a_width)
    else:
        a_fn, a_rows, a_params, a_arrays, a_shape = None, None, [], [a], a.shape
    n_r = len(a_arrays)
    n_a = n_r + len(a_params)

    def load_a(refs):
        if a_fn is None:
            return refs[0][...].astype(bf16)
        pieces = []
        for r, (_, widths) in zip(refs[:n_r], a_rows):
            pieces += _pieces(r, widths)
        return a_fn(*pieces, *[p[...] for p in refs[n_r:]])[0].astype(bf16)

    if mode == "tn":
        assert a_fn is None
        (k, m), (_, n) = a_shape, b.shape
        tn = _tile(n, max(128, ACC_TILE_BYTES // (4 * m)))
        tk = _tile(k, 1024)
        nk = k // tk

        def body(*refs):
            b_ref, o_ref, acc = refs[n_a:]

            @pl.when(pl.program_id(1) == 0)
            def _():
                acc[...] = jnp.zeros_like(acc)

            acc[...] += lax.dot_general(load_a(refs[:n_a]), b_ref[...].astype(bf16),
                                        (((0,), (0,)), ((), ())), preferred_element_type=f32)

            @pl.when(pl.program_id(1) == nk - 1)
            def _():
                o_ref[...] = acc[...].astype(o_ref.dtype)

        return pl.pallas_call(
            body, name=name, grid=(n // tn, nk),
            in_specs=[pl.BlockSpec((tk, r.shape[1]), lambda j, kk: (kk, 0)) for r in a_arrays]
            + [pl.BlockSpec((tk, tn), lambda j, kk: (kk, j))],
            out_specs=pl.BlockSpec((m, tn), lambda j, kk: (0, j)), out_shape=jax.ShapeDtypeStruct((m, n), out_dtype),
            scratch_shapes=[pltpu.VMEM((m, tn), f32)],
            compiler_params=_cp(("parallel", "arbitrary")),
        )(*a_arrays, b)

    (m, k) = a_shape
    n = b.shape[1] if mode == "nn" else b.shape[0]
    tm = _tile(m, 1024 if a_fn is None else 512)
    tn = _tile(n, max(128, WEIGHT_TILE_BYTES // (2 * k)))
    dims = (((1,), (0,)), ((), ())) if mode == "nn" else (((1,), (1,)), ((), ()))
    b_spec = pl.BlockSpec((k, tn), lambda j, i: (0, j)) if mode == "nn" else pl.BlockSpec((tn, k), lambda j, i: (j, 0))
    o_spec = pl.BlockSpec((tm, tn), lambda j, i: (i, j))

    keep = a_fn is not None
    assert not keep or tn == n

    def body(*refs):
        b_ref = refs[n_a]
        a_val = load_a(refs[:n_a])
        r = lax.dot_general(a_val, b_ref[...].astype(bf16), dims, preferred_element_type=f32)
        if has_add:
            r = r + refs[n_a + 1][...]
        if keep:
            refs[-2][...] = r.astype(refs[-2].dtype)
            refs[-1][...] = a_val
        else:
            refs[-1][...] = r.astype(refs[-1].dtype)

    res = pl.pallas_call(
        body, name=name, grid=(n // tn, m // tm),
        in_specs=[pl.BlockSpec((tm, r.shape[1]), lambda j, i: (i, 0)) for r in a_arrays]
        + [pl.BlockSpec(p.shape, lambda j, i: (0, 0)) for p in a_params] + [b_spec] + ([o_spec] if has_add else []),
        out_specs=[o_spec] + ([pl.BlockSpec((tm, k), lambda j, i: (i, 0))] if keep else []),
        out_shape=[jax.ShapeDtypeStruct((m, n), out_dtype)] + ([jax.ShapeDtypeStruct((m, k), bf16)] if keep else []),
        compiler_params=_cp(("parallel", "arbitrary")),
    )(*a_arrays, *a_params, b, *([add] if has_add else []))
    return tuple(res) if keep else res[0]


def _input_cotangent(name, a_list, b_list, x, gain, add, side=None):
    m = a_list[0].shape[0]
    tm = _tile(m, 256)
    n_g = len(a_list)
    srcs, per_peer = side if side is not None else ([], False)
    n_s = len(srcs)

    def body(*refs):
        x_ref, g_ref, add_ref = refs[2 * n_g:2 * n_g + 3]
        src_refs = refs[2 * n_g + 3:2 * n_g + 3 + n_s]
        dx_ref, dg_ref = refs[2 * n_g + 3 + n_s:2 * n_g + 5 + n_s]
        _side_exchange(src_refs, refs[2 * n_g + 5 + n_s:2 * n_g + 5 + 2 * n_s], per_peer, refs[2 * n_g + 5 + 2 * n_s:], m // tm)
        d_u = None
        for g in range(n_g):
            r = lax.dot_general(refs[g][...].astype(bf16), refs[n_g + g][...].astype(bf16), (((1,), (0,)), ((), ())),
                                preferred_element_type=f32)
            d_u = r if d_u is None else d_u + r
        _, vjp = jax.vjp(_rms, x_ref[...], g_ref[...])
        d_x, d_gain = vjp(d_u)
        dx_ref[...] = d_x + add_ref[...]

        @pl.when(pl.program_id(0) == 0)
        def _():
            dg_ref[...] = jnp.zeros_like(dg_ref)

        dg_ref[...] += d_gain

    rows = pl.BlockSpec((tm, x.shape[1]), lambda i: (i, 0))
    whole = lambda b: pl.BlockSpec(b.shape, lambda i: (0, 0))
    res = pl.pallas_call(
        body, name=name, grid=(m // tm,),
        in_specs=[pl.BlockSpec((tm, a.shape[1]), lambda i: (i, 0)) for a in a_list] + [whole(b) for b in b_list]
        + [rows, whole(gain), rows] + [_HBM_SPEC] * n_s,
        out_specs=[rows, whole(gain)] + [_HBM_SPEC] * n_s,
        out_shape=[jax.ShapeDtypeStruct(x.shape, f32), jax.ShapeDtypeStruct(gain.shape, f32)] + _side_out_shapes(srcs, per_peer),
        scratch_shapes=_side_sems(n_s),
        compiler_params=_cp(("arbitrary",)),
    )(*a_list, *b_list, x, gain, add, *srcs)
    return res[0], res[1], list(res[2:])


def _pieces(ref, widths):
    out, off = [], 0
    for w in widths:
        out.append(ref[:, off:off + w].astype(f32))
        off += w
    return out


def _store_pieces(ref, widths, vals, add_ref=None):
    off = 0
    for w, v in zip(widths, vals):
        ref[:, off:off + w] = (v if add_ref is None else v + add_ref[:, off:off + w]).astype(ref.dtype)
        off += w


def _rows_fwd(name, fn, consts, rows, params, outs, n_sums=0, tm=512, dtypes=None):
    t = (consts + rows)[0][0].shape[0]
    tm = min(tm, t)
    ins = consts + rows
    n_in, n_p, n_o = len(ins), len(params), len(outs)
    dtypes = dtypes or [f32] * n_o

    def body(*refs):
        in_refs, p_refs = refs[:n_in], refs[n_in:n_in + n_p]
        o_refs, s_refs = refs[n_in + n_p:n_in + n_p + n_o], refs[n_in + n_p + n_o:]
        vals = []
        for r, (_, widths) in zip(in_refs, ins):
            vals += _pieces(r, widths)
        res = fn(*vals, *[p[...] for p in p_refs])
        pos = 0
        for r, widths in zip(o_refs, outs):
            _store_pieces(r, widths, res[pos:pos + len(widths)])
            pos += len(widths)

        @pl.when(pl.program_id(0) == 0)
        def _():
            for s in s_refs:
                s[...] = jnp.zeros_like(s)

        for s, v in zip(s_refs, res[pos:]):
            s[...] += v

    row_spec = lambda w: pl.BlockSpec((tm, w), lambda i: (i, 0))
    full = lambda p: pl.BlockSpec(p.shape, lambda i: (0,) * p.ndim)
    return pl.pallas_call(
        body, name=name, grid=(t // tm,),
        in_specs=[row_spec(sum(w)) for _, w in ins] + [full(p) for p in params],
        out_specs=[row_spec(sum(w)) for w in outs] + [pl.BlockSpec((1, 1), lambda i: (0, 0))] * n_sums,
        out_shape=[jax.ShapeDtypeStruct((t, sum(w)), dt) for w, dt in zip(outs, dtypes)] + [jax.ShapeDtypeStruct((1, 1), f32)] * n_sums,
        compiler_params=_cp(("arbitrary",)),
    )(*[a for a, _ in ins], *params)


def _rows_bwd(name, fn, consts, rows, params, outs, cts, n_sums=0, add=None, tm=512, dtypes=None):
    t = (consts + rows)[0][0].shape[0]
    tm = min(tm, t)
    n_c, n_r, n_p, n_o = len(consts), len(rows), len(params), len(outs)
    has_add = add is not None
    dtypes = dtypes or [f32] * n_r

    def body(*refs):
        pos = 0
        c_refs = refs[pos:pos + n_c]; pos += n_c
        r_refs = refs[pos:pos + n_r]; pos += n_r
        p_refs = refs[pos:pos + n_p]; pos += n_p
        ct_refs = refs[pos:pos + n_o]; pos += n_o
        add_ref = refs[pos] if has_add else None
        pos += 1 if has_add else 0
        dr_refs = refs[pos:pos + n_r]; pos += n_r
        dp_refs = refs[pos:pos + n_p]; pos += n_p
        s_refs = refs[pos:pos + n_sums]
        cvals, rvals = [], []
        for r, (_, widths) in zip(c_refs, consts):
            cvals += _pieces(r, widths)
        for r, (_, widths) in zip(r_refs, rows):
            rvals += _pieces(r, widths)
        pvals = [p[...] for p in p_refs]
        ctv = []
        for r, widths in zip(ct_refs, outs):
            ctv += _pieces(r, widths)
        ctv += [jnp.ones((1, 1), f32)] * n_sums
        primal, vjp = jax.vjp(lambda *rp: tuple(fn(*cvals, *rp)), *rvals, *pvals)
        g = vjp(tuple(ctv))
        pos = 0
        for idx, (r, (_, widths)) in enumerate(zip(dr_refs, rows)):
            _store_pieces(r, widths, g[pos:pos + len(widths)], add_ref if idx == 0 else None)
            pos += len(widths)

        @pl.when(pl.program_id(0) == 0)
        def _():
            for acc in list(dp_refs) + list(s_refs):
                acc[...] = jnp.zeros_like(acc)

        for dp, v in zip(dp_refs, g[pos:]):
            dp[...] += v
        for s, v in zip(s_refs, primal[len(primal) - n_sums:]):
            s[...] += v

    row_spec = lambda w: pl.BlockSpec((tm, w), lambda i: (i, 0))
    full = lambda p: pl.BlockSpec(p.shape, lambda i: (0,) * p.ndim)
    args = [a for a, _ in consts + rows] + list(params) + list(cts) + ([add] if has_add else [])
    res = pl.pallas_call(
        body, name=name, grid=(t // tm,),
        in_specs=[row_spec(sum(w)) for _, w in consts + rows] + [full(p) for p in params]
        + [row_spec(sum(w)) for w in outs] + ([row_spec(add.shape[1])] if has_add else []),
        out_specs=[row_spec(sum(w)) for _, w in rows] + [full(p) for p in params]
        + [pl.BlockSpec((1, 1), lambda i: (0, 0))] * n_sums,
        out_shape=[jax.ShapeDtypeStruct((t, sum(w)), dt) for (_, w), dt in zip(rows, dtypes)]
        + [jax.ShapeDtypeStruct(p.shape, f32) for p in params] + [jax.ShapeDtypeStruct((1, 1), f32)] * n_sums,
        compiler_params=_cp(("arbitrary",)),
    )(*args)
    return res[:n_r], res[n_r:n_r + n_p] + res[n_r + n_p:]


def _matmul_then_vjp(name, a, b, mode, fn, rows, dtypes, tm=256):
    m, k = a.shape
    tm = min(tm, m)
    dims = (((1,), (0,)), ((), ())) if mode == "nn" else (((1,), (1,)), ((), ()))
    n_r = len(rows)

    def body(*refs):
        a_ref, b_ref = refs[:2]
        ct = lax.dot_general(a_ref[...].astype(bf16), b_ref[...].astype(bf16), dims, preferred_element_type=f32)
        rvals = []
        for r, (_, widths) in zip(refs[2:2 + n_r], rows):
            rvals += _pieces(r, widths)
        _, vjp = jax.vjp(lambda *rp: fn(*rp)[0], *rvals)
        g = vjp(ct)
        pos = 0
        for r, (_, widths) in zip(refs[2 + n_r:], rows):
            _store_pieces(r, widths, g[pos:pos + len(widths)])
            pos += len(widths)

    row_spec = lambda w: pl.BlockSpec((tm, w), lambda i: (i, 0))
    return pl.pallas_call(
        body, name=name, grid=(m // tm,),
        in_specs=[row_spec(k), pl.BlockSpec(b.shape, lambda i: (0, 0))] + [row_spec(r.shape[1]) for r, _ in rows],
        out_specs=[row_spec(r.shape[1]) for r, _ in rows],
        out_shape=[jax.ShapeDtypeStruct(r.shape, dt) for (r, _), dt in zip(rows, dtypes)],
        compiler_params=_cp(("parallel",)),
    )(a, b, *[r for r, _ in rows])


def _rms(x, g):
    return x * lax.rsqrt(jnp.mean(x * x, axis=-1, keepdims=True) + NORM_EPS) * g


def _fn_rms(x, g):
    return (_rms(x, g),)


def _fn_rwkv_pre(r, k, v, wd, ad, gd, w0, w_up, a0, a_up, g_up, k_k, k_a):
    nn, _, _ = _make_mm(False, False)
    w_log = -_sigmoid(w0 + nn(jnp.tanh(wd), w_up)) * 0.6065306597126334
    a = _sigmoid(a0 + nn(ad, a_up))
    g = nn(_sigmoid(gd), g_up)
    kk = k * k_k
    kk = kk * lax.rsqrt(jnp.maximum(_head_sum(kk * kk), 1e-24))
    k2 = k * (1.0 + (a - 1.0) * k_a)
    return r, k2, v, w_log, -kk, kk * a, g


def _fn_rwkv_post(y, r, k2, v, g, gn_g, gn_b, r_k):
    mean = _head_sum(y) * (1.0 / HD)
    yc = y - mean
    var = _head_sum(yc * yc) * (1.0 / HD)
    yn = yc * lax.rsqrt(var + GN_EPS) * gn_g + gn_b
    bonus = _head_sum(r * k2 * r_k) * v
    return ((yn + bonus) * g,)


def _fn_merge(a_fox, a_rwkv, a_mem, g_fox, g_rwkv, g_mem):
    return (_sigmoid(g_fox) * a_fox + _sigmoid(g_rwkv) * a_rwkv + _sigmoid(g_mem) * a_mem,)


def _fn_post1(y, x, post1_g, pre2_g):
    h1 = x + _rms(y, post1_g)
    return h1, _rms(h1, pre2_g)


def _fn_swiglu(gp, up):
    return (gp * _sigmoid(gp) * up,)


def _fn_final(target, ffn, h1, post2_g):
    err = h1 + _rms(ffn, post2_g) - target
    per_row = jnp.mean(err * err, axis=-1, keepdims=True)
    return (0.5 * jnp.sum(per_row, axis=0, keepdims=True),)


def _shift_down(x):
    row = lax.broadcasted_iota(jnp.int32, x.shape, 0)
    return jnp.where(row == 0, 0.0, pltpu.roll(x, 1, 0))


def _shift_up(x):
    s = x.shape[0]
    row = lax.broadcasted_iota(jnp.int32, x.shape, 0)
    return jnp.where(row == s - 1, 0.0, pltpu.roll(x, s - 1, 0))


def _tokshift_fwd(p, mu, batch, seq):
    w = p.shape[1]
    tc = _tile(w, 384)

    def body(p_ref, mu_ref, o_ref):
        x = p_ref[...]
        o_ref[...] = x + (_shift_down(x) - x) * mu_ref[...]

    return pl.pallas_call(
        body, name="tokshift_fwd", grid=(w // tc, batch),
        in_specs=[pl.BlockSpec((seq, tc), lambda j, b: (b, j)), pl.BlockSpec((1, tc), lambda j, b: (0, j))],
        out_specs=pl.BlockSpec((seq, tc), lambda j, b: (b, j)),
        out_shape=jax.ShapeDtypeStruct(p.shape, f32),
        compiler_params=_cp(("parallel", "arbitrary")),
    )(p, mu)


def _tokshift_bwd(p, mu, dps, batch, seq):
    w = p.shape[1]
    tc = _tile(w, 384)

    def body(p_ref, mu_ref, d_ref, dp_ref, dmu_ref):
        x, mu_v, d = p_ref[...], mu_ref[...], d_ref[...]
        dp_ref[...] = (d * (1.0 - mu_v) + _shift_up(d * mu_v)).astype(dp_ref.dtype)

        @pl.when(pl.program_id(1) == 0)
        def _():
            dmu_ref[...] = jnp.zeros_like(dmu_ref)

        dmu_ref[...] += jnp.sum(d * (_shift_down(x) - x), axis=0, keepdims=True)

    return pl.pallas_call(
        body, name="tokshift_bwd", grid=(w // tc, batch),
        in_specs=[pl.BlockSpec((seq, tc), lambda j, b: (b, j)), pl.BlockSpec((1, tc), lambda j, b: (0, j)),
                  pl.BlockSpec((seq, tc), lambda j, b: (b, j))],
        out_specs=[pl.BlockSpec((seq, tc), lambda j, b: (b, j)), pl.BlockSpec((1, tc), lambda j, b: (0, j))],
        out_shape=[jax.ShapeDtypeStruct(p.shape, bf16), jax.ShapeDtypeStruct(mu.shape, f32)],
        compiler_params=_cp(("parallel", "arbitrary")),
    )(p, mu, dps)


def _cum_block(seq):
    return _tile(seq, 256)


def _fox_gate_fwd(f, bias, batch, seq):
    cb = _cum_block(seq)

    def body(f_ref, b_ref, c_ref):
        row = lax.broadcasted_iota(jnp.int32, (cb, cb), 0)
        col = lax.broadcasted_iota(jnp.int32, (cb, cb), 1)
        tri = (col <= row).astype(f32)
        carry = jnp.zeros((1, 128), f32)
        for i in range(seq // cb):
            z = f_ref[i * cb:(i + 1) * cb, :] + b_ref[...]
            ls = jnp.minimum(z, 0.0) - jnp.log(1.0 + jnp.exp(-jnp.abs(z)))
            c = _dg(tri, ls, (((1,), (0,)), ((), ())), True) + carry
            c_ref[i * cb:(i + 1) * cb, :] = c
            carry = c[cb - 1:cb, :]

    return pl.pallas_call(
        body, name="fox_gate_fwd", grid=(batch,),
        in_specs=[pl.BlockSpec((seq, 128), lambda b: (b, 0)), pl.BlockSpec((1, 128), lambda b: (0, 0))],
        out_specs=pl.BlockSpec((seq, 128), lambda b: (b, 0)),
        out_shape=jax.ShapeDtypeStruct(f.shape, f32),
        compiler_params=_cp(("arbitrary",)),
    )(f, bias)


def _fox_gate_bwd(f, bias, dc_a, dc_b, batch, seq):
    cb = _cum_block(seq)

    def body(f_ref, b_ref, da_ref, db_ref, df_ref, dbias_ref):
        row = lax.broadcasted_iota(jnp.int32, (cb, cb), 0)
        col = lax.broadcasted_iota(jnp.int32, (cb, cb), 1)
        triu = (col >= row).astype(f32)

        @pl.when(pl.program_id(0) == 0)
        def _():
            dbias_ref[...] = jnp.zeros_like(dbias_ref)

        lane = lax.broadcasted_iota(jnp.int32, (1, 128), 1)

        def by_head(blk):
            out = jnp.zeros((cb, 128), f32)
            for p in range(HEADS // 2):
                for e in range(2):
                    out = jnp.where(lane == 2 * p + e, _pick_lane(blk[:, p * 128:(p + 1) * 128], e), out)
            return out

        carry = jnp.zeros((1, 128), f32)
        tot = jnp.zeros((1, 128), f32)
        for i in reversed(range(seq // cb)):
            sl = slice(i * cb, (i + 1) * cb)
            dc = by_head(da_ref[sl, :] + db_ref[sl, :])
            dls = _dg(triu, dc, (((1,), (0,)), ((), ())), True) + carry
            carry = dls[0:1, :]
            df = dls * _sigmoid(-(f_ref[sl, :] + b_ref[...]))
            df_ref[sl, :] = df.astype(df_ref.dtype)
            tot = tot + jnp.sum(df, axis=0, keepdims=True)
        dbias_ref[...] += tot

    return pl.pallas_call(
        body, name="fox_gate_bwd", grid=(batch,),
        in_specs=[pl.BlockSpec((seq, 128), lambda b: (b, 0)), pl.BlockSpec((1, 128), lambda b: (0, 0)),
                  pl.BlockSpec((seq, HW), lambda b: (b, 0)), pl.BlockSpec((seq, HW), lambda b: (b, 0))],
        out_specs=[pl.BlockSpec((seq, 128), lambda b: (b, 0)), pl.BlockSpec((1, 128), lambda b: (0, 0))],
        out_shape=[jax.ShapeDtypeStruct(f.shape, bf16), jax.ShapeDtypeStruct((1, 128), f32)],
        compiler_params=_cp(("arbitrary",)),
    )(f, bias, dc_a, dc_b)


_HBM_SPEC = pl.BlockSpec(memory_space=pltpu.HBM)


def _side_out_shapes(srcs, per_peer):
    return [jax.ShapeDtypeStruct(((N_DEV,) + tuple(s.shape[1:] if per_peer else s.shape)), s.dtype) for s in srcs]


def _side_sems(n):
    if n == 0:
        return []
    return [pltpu.SemaphoreType.DMA((n, N_DEV - 1)), pltpu.SemaphoreType.DMA((n, N_DEV - 1)), pltpu.SemaphoreType.DMA((n,))]


def _peer_copies(src_refs, dst_refs, per_peer, sems):
    send_sems, recv_sems, local_sems = sems
    x, y, c = lax.axis_index("x"), lax.axis_index("y"), lax.axis_index("c")
    me = 4 * x + 2 * y + c

    def remote(src, dst, t, k, to):
        return pltpu.make_async_remote_copy(src_ref=src, dst_ref=dst, send_sem=send_sems.at[t, k - 1],
                                            recv_sem=recv_sems.at[t, k - 1], device_id=to,
                                            device_id_type=pl.DeviceIdType.MESH)

    direct, relays = [], []
    for t, (s, d) in enumerate(zip(src_refs, dst_refs)):
        direct.append((t, 0, pltpu.make_async_copy(s.at[me] if per_peer else s, d.at[me], local_sems.at[t])))
        for k in range(1, N_DEV):
            px = 1 - x if k & 4 else x
            py = 1 - y if k & 2 else y
            pc = 1 - c if k & 1 else c
            if per_peer:
                direct.append((t, k, remote(s.at[4 * px + 2 * py + pc], d.at[me], t, k, (px, py, pc))))
            elif k == 1 or not k & 1:
                direct.append((t, k, remote(s, d.at[me], t, k, (px, py, pc))))
            else:
                origin = d.at[4 * px + 2 * py + c]
                relays.append((t, k - 1, remote(origin, origin, t, k, (x, y, 1 - c))))
    return direct, relays


def _exchange_start(direct):
    for _, _, cp in direct:
        cp.start()


def _exchange_relay(direct, relays):
    landed = {(t, k): cp for t, k, cp in direct}
    for t, j, cp in relays:
        landed[(t, j)].wait_recv()
        cp.start()


def _exchange_finish(direct, relays):
    relayed = {(t, j) for t, j, _ in relays}
    for t, k, cp in direct:
        if k == 0:
            cp.wait()
        else:
            cp.wait_send()
            if (t, k) not in relayed:
                cp.wait_recv()
    for _, _, cp in relays:
        cp.wait()


def _side_exchange(src_refs, dst_refs, per_peer, sems, *grid):
    if not src_refs:
        return
    step, total = 0, 1
    for a, n in enumerate(grid):
        step, total = step * n + pl.program_id(a), total * n

    @pl.when(step == 0)
    def _():
        _exchange_start(_peer_copies(src_refs, dst_refs, per_peer, sems)[0])

    @pl.when(step == (3 * total) // 4)
    def _():
        _exchange_relay(*_peer_copies(src_refs, dst_refs, per_peer, sems))

    @pl.when(step == total - 1)
    def _():
        _exchange_finish(*_peer_copies(src_refs, dst_refs, per_peer, sems))


def _exchange(name, srcs, per_peer):
    n = len(srcs)

    def body(*refs):
        direct, relays = _peer_copies(refs[:n], refs[n:2 * n], per_peer, refs[2 * n:])
        _exchange_start(direct)
        _exchange_relay(direct, relays)
        _exchange_finish(direct, relays)

    return pl.pallas_call(
        body, name=name, in_specs=[_HBM_SPEC] * n, out_specs=[_HBM_SPEC] * n,
        out_shape=_side_out_shapes(srcs, per_peer), scratch_shapes=_side_sems(n),
    )(*srcs)


FOX_T = 512
_NEG = -1e30
_D2 = (((1,), (1,)), ((), ()))
_D1 = (((1,), (0,)), ((), ()))
_D0 = (((0,), (0,)), ((), ()))


def _bdot(a, b, dims):
    return lax.dot_general(a.astype(bf16), b.astype(bf16), dims, preferred_element_type=f32)


def _pick_lane(x, lane):
    idx = lax.broadcasted_iota(jnp.int32, x.shape, 1)
    return jnp.sum(jnp.where(idx == lane, x, 0.0), axis=1, keepdims=True)


def _pick_row(x, row):
    idx = lax.broadcasted_iota(jnp.int32, x.shape, 0)
    return jnp.sum(jnp.where(idx == row, x, 0.0), axis=0, keepdims=True)


def _fox_fwd(qkv, c, c_rows, batch, seq, side=None):
    t = min(FOX_T, seq)
    nq = seq // t
    scale = HD ** -0.5
    srcs, per_peer = side if side is not None else ([], False)
    n_s = len(srcs)

    def body(*refs):
        q_ref, k_ref, v_ref, cq_ref, ck_ref = refs[:5]
        o_ref, lse_ref = refs[5 + n_s:7 + n_s]
        _side_exchange(refs[5:5 + n_s], refs[7 + n_s:7 + 2 * n_s], per_peer, refs[7 + 2 * n_s:], batch, PAIRS, nq)
        pair, i = pl.program_id(1), pl.program_id(2)
        lane = lax.broadcasted_iota(jnp.int32, (1, PAIR_W), 1)
        first = (lane // HD) == 0
        mine = [first, jnp.logical_not(first)]
        q = q_ref[...] * scale
        qs = [jnp.where(mine[e], q, 0.0) for e in range(2)]
        cqs = [_pick_lane(cq_ref[...], 2 * pair + e) for e in range(2)]
        causal = lax.broadcasted_iota(jnp.int32, (t, t), 1) <= lax.broadcasted_iota(jnp.int32, (t, t), 0)

        def block(j, carry, diagonal):
            rows = pl.ds(pl.multiple_of(j * t, t), t)
            kj, vj = k_ref[rows, :], v_ref[rows, :]
            ck_blk = ck_ref[0, :, rows]
            out = []
            for e in range(2):
                m, acc = carry[2 * e:2 * e + 2]
                s = _bdot(qs[e], kj, _D2) + cqs[e] - _pick_row(ck_blk, 2 * pair + e)
                if diagonal:
                    s = jnp.where(causal, s, _NEG)
                m_new = jnp.maximum(m, jnp.max(s, axis=1, keepdims=True))
                p = jnp.exp(s - m_new)
                out += [m_new, jnp.exp(m - m_new) * acc + _bdot(p, jnp.where(mine[e], vj, 1.0), _D1)]
            return tuple(out)

        init = (jnp.full((t, 1), _NEG, f32), jnp.zeros((t, PAIR_W), f32)) * 2
        carry = lax.fori_loop(0, i, lambda j, cr: block(j, cr, False), init)
        m0, a0, m1, a1 = block(i, carry, True)
        l0, l1 = _pick_lane(a0, HD), _pick_lane(a1, 0)
        o_ref[...] = jnp.where(first, a0 / l0, a1 / l1)
        lse_ref[...] = jnp.where(lane == 0, m0 + jnp.log(l0), jnp.where(lane == 1, m1 + jnp.log(l1), 0.0))

    q_spec = pl.BlockSpec((t, PAIR_W), lambda b, p, i: (b * nq + i, p))
    res = pl.pallas_call(
        body, name="fox_attn_fwd", grid=(batch, PAIRS, nq),
        in_specs=[q_spec,
                  pl.BlockSpec((seq, PAIR_W), lambda b, p, i: (b, PAIRS + p)),
                  pl.BlockSpec((seq, PAIR_W), lambda b, p, i: (b, 2 * PAIRS + p)),
                  pl.BlockSpec((t, 128), lambda b, p, i: (b * nq + i, 0)),
                  pl.BlockSpec((1, 8, seq), lambda b, p, i: (b, 0, 0))] + [_HBM_SPEC] * n_s,
        out_specs=[q_spec, q_spec] + [_HBM_SPEC] * n_s,
        out_shape=[jax.ShapeDtypeStruct((batch * seq, HW), f32)] * 2 + _side_out_shapes(srcs, per_peer),
        scratch_shapes=_side_sems(n_s),
        compiler_params=_cp(("arbitrary", "arbitrary", "arbitrary")),
    )(qkv, qkv, qkv, c, c_rows, *srcs)
    return res[0], res[1], list(res[2:])


def _fox_bwd(qkv, c, c_rows, o, lse, do, batch, seq):
    t = min(FOX_T, seq)
    nq = seq // t
    scale = HD ** -0.5

    def body(q_ref, k_ref, v_ref, cq_ref, ck_ref, o_ref, lse_ref, do_ref,
             dq_ref, dk_ref, dv_ref, dcq_ref, dck_ref, acc0, acc1):
        pair, i = pl.program_id(1), pl.program_id(2)
        accs = [acc0, acc1]

        @pl.when(i == 0)
        def _():
            dv_ref[...] = jnp.zeros_like(dv_ref)
            acc0[...] = jnp.zeros_like(acc0)
            acc1[...] = jnp.zeros_like(acc1)

        lane = lax.broadcasted_iota(jnp.int32, (1, PAIR_W), 1)
        first = (lane // HD) == 0
        mine = [first, jnp.logical_not(first)]
        q, d_o, o_i = q_ref[...] * scale, do_ref[...], o_ref[...]
        q0s = [jnp.where(mine[e], q, 0.0) for e in range(2)]
        q1s = [jnp.where(mine[e], q, 1.0) for e in range(2)]
        dos = [jnp.where(mine[e], d_o, 0.0) for e in range(2)]
        deltas = [jnp.sum(dos[e] * o_i, axis=1, keepdims=True) for e in range(2)]
        lses = [_pick_lane(lse_ref[...], e) for e in range(2)]
        cqs = [_pick_lane(cq_ref[...], 2 * pair + e) for e in range(2)]
        causal = lax.broadcasted_iota(jnp.int32, (t, t), 1) <= lax.broadcasted_iota(jnp.int32, (t, t), 0)

        def block(j, dqs, diagonal):
            rows = pl.ds(pl.multiple_of(j * t, t), t)
            kj, vj = k_ref[rows, :], v_ref[rows, :]
            ck_blk = ck_ref[0, :, rows]
            out = []
            for e in range(2):
                s = _bdot(q0s[e], kj, _D2) + cqs[e] - _pick_row(ck_blk, 2 * pair + e)
                if diagonal:
                    s = jnp.where(causal, s, _NEG)
                p = jnp.exp(s - lses[e])
                ds = p * (_bdot(dos[e], vj, _D2) - deltas[e])
                dv_ref[rows, :] += _bdot(p, dos[e], _D0)
                accs[e][rows, :] += _bdot(ds, q1s[e], _D0)
                out.append(dqs[e] + _bdot(ds, jnp.where(mine[e], kj, 1.0), _D1))
            return tuple(out)

        zero = jnp.zeros((t, PAIR_W), f32)
        dqs = lax.fori_loop(0, i, lambda j, cr: block(j, cr, False), (zero, zero))
        dq0, dq1 = block(i, dqs, True)
        dq_ref[...] = jnp.where(first, dq0, dq1) * scale
        dcq_ref[...] = jnp.where(lane == 0, _pick_lane(dq0, HD), jnp.where(lane == 1, _pick_lane(dq1, 0), 0.0))

        @pl.when(i == nq - 1)
        def _():
            a0, a1 = acc0[...], acc1[...]
            dk_ref[...] = jnp.where(first, a0, a1)
            dck_ref[...] = jnp.where(lane == 0, -_pick_lane(a0, HD), jnp.where(lane == 1, -_pick_lane(a1, 0), 0.0))

    blk = lambda col: pl.BlockSpec((t, PAIR_W), lambda b, p, i: (b * nq + i, col * PAIRS + p))
    whole = lambda col: pl.BlockSpec((seq, PAIR_W), lambda b, p, i: (b, col * PAIRS + p))
    t_all = batch * seq
    return pl.pallas_call(
        body, name="fox_attn_bwd", grid=(batch, PAIRS, nq),
        in_specs=[blk(0), whole(1), whole(2),
                  pl.BlockSpec((t, 128), lambda b, p, i: (b * nq + i, 0)),
                  pl.BlockSpec((1, 8, seq), lambda b, p, i: (b, 0, 0)),
                  blk(0), blk(0), blk(0)],
        out_specs=[blk(0), whole(0), whole(0), blk(0), whole(0)],
        out_shape=[jax.ShapeDtypeStruct((t_all, HW), f32)] * 5,
        scratch_shapes=[pltpu.VMEM((seq, PAIR_W), f32), pltpu.VMEM((seq, PAIR_W), f32)],
        compiler_params=_cp(("parallel", "parallel", "arbitrary")),
    )(qkv, qkv, qkv, c, c_rows, o, lse, do)


MEM_TQ = 1024


def _mem_block(q, km, vm):
    nn, nt, _ = _make_mm(False, False)
    logits = nt(q, km) * (MEM_HD ** -0.5)
    m = lax.stop_gradient(jnp.max(logits, axis=-1, keepdims=True))
    e = jnp.exp(logits - m)
    return nn(e / jnp.sum(e, axis=-1, keepdims=True), vm)


def _mem_specs(seq, tq):
    nq = seq // tq
    qs = pl.BlockSpec((tq, MEM_HD), lambda b, h, i: (b * nq + i, h))
    ks = pl.BlockSpec((MEM_LEN, MEM_HD), lambda b, h, i: (b, h))
    vs = pl.BlockSpec((MEM_LEN, MEM_HD), lambda b, h, i: (b, MEM_HEADS + h))
    return nq, qs, ks, vs


def _mem_fwd(q, mem_kv, batch, seq):
    tq = min(MEM_TQ, seq)
    nq, qs, ks, vs = _mem_specs(seq, tq)

    def body(q_ref, k_ref, v_ref, o_ref):
        o_ref[...] = _mem_block(q_ref[...].astype(f32), k_ref[...], v_ref[...]).astype(o_ref.dtype)

    return pl.pallas_call(
        body, name="mem_attn_fwd", grid=(batch, MEM_HEADS, nq),
        in_specs=[qs, ks, vs], out_specs=qs, out_shape=jax.ShapeDtypeStruct(q.shape, bf16),
        compiler_params=_cp(("parallel", "parallel", "arbitrary")),
    )(q, mem_kv, mem_kv)


def _mem_bwd(q, mem_kv, do, batch, seq):
    tq = min(MEM_TQ, seq)
    nq, qs, ks, vs = _mem_specs(seq, tq)

    def body(q_ref, k_ref, v_ref, do_ref, dq_ref, dk_ref, dv_ref):
        _, vjp = jax.vjp(_mem_block, q_ref[...].astype(f32), k_ref[...], v_ref[...])
        dq, dk, dv = vjp(do_ref[...])
        dq_ref[...] = dq.astype(dq_ref.dtype)

        @pl.when(pl.program_id(2) == 0)
        def _():
            dk_ref[...] = jnp.zeros_like(dk_ref)
            dv_ref[...] = jnp.zeros_like(dv_ref)

        dk_ref[...] += dk
        dv_ref[...] += dv

    return pl.pallas_call(
        body, name="mem_attn_bwd", grid=(batch, MEM_HEADS, nq),
        in_specs=[qs, ks, vs, qs], out_specs=[qs, ks, ks],
        out_shape=[jax.ShapeDtypeStruct(q.shape, bf16), jax.ShapeDtypeStruct((batch * MEM_LEN, MEM_W), f32),
                   jax.ShapeDtypeStruct((batch * MEM_LEN, MEM_W), f32)],
        compiler_params=_cp(("parallel", "parallel", "arbitrary")),
    )(q, mem_kv, mem_kv, do)


@jax.custom_vjp
def _halves(x):
    c = x.shape[1] // 2
    return x[:, :c], x[:, c:]


_halves.defvjp(lambda x: ((x[:, :x.shape[1] // 2], x[:, x.shape[1] // 2:]), None),
               lambda _, g: (jnp.concatenate(g, axis=1),))


@jax.custom_vjp
def _lead_halves(x):
    n = x.shape[0] // 2
    return x[:n], x[n:]


_lead_halves.defvjp(lambda x: ((x[:x.shape[0] // 2], x[x.shape[0] // 2:]), None),
                    lambda _, g: (jnp.concatenate(g, axis=0),))


def _scan_chunk(s0, r, wl, k, v, a, b):
    nn, nt, tn = _make_mm(True, False)
    nn_exact, _, _ = _make_mm(True, True)
    _, nt_exact, _ = _make_mm(True, "split")
    hp, c, lanes = r.shape
    row = lax.broadcasted_iota(jnp.int32, (c, c), 0)
    col = lax.broadcasted_iota(jnp.int32, (c, c), 1)
    first = (lax.broadcasted_iota(jnp.int32, (1, 1, lanes), 2) // HD) == 0
    tri = jnp.broadcast_to((col <= row).astype(f32)[None], (hp, c, c))
    lg = nn_exact(tri, wl)
    lg_end = lg[:, c - 1:c, :]
    grow, shrink, to_end = jnp.exp(lg), jnp.exp(-lg), jnp.exp(lg_end - lg)
    rt, kt, bt, at = r * grow, k * shrink, b * shrink, a * jnp.exp(lg - wl)
    strict, incl = (col < row)[None], (col <= row)[None]
    twice = lambda t: jnp.concatenate([t, t], axis=0)
    queries = jnp.concatenate([at, rt], axis=1)
    per_head = jnp.concatenate([jnp.where(first, queries, 0.0), jnp.where(first, 0.0, queries)], axis=0)
    (ab, rb), (ak, rk) = _halves(nt_exact(per_head, twice(bt))), _halves(nt_exact(per_head, twice(kt)))
    l_ab = jnp.where(strict, ab, 0.0)
    a_ak = jnp.where(strict, ak, 0.0)
    a_rb = jnp.where(incl, rb, 0.0)
    a_rk = jnp.where(incl, rk, 0.0)
    inv = (col == row).astype(f32)[None] + l_ab
    power, n = l_ab, 1
    while 2 * n < c:
        power = nn(power, power)
        inv = inv + nn(inv, power)
        n *= 2

    def apply(m, t):
        lo, hi = _lead_halves(nn(m, twice(t)))
        return jnp.where(first, lo, hi)

    sa = apply(inv, nt(at, s0) + apply(a_ak, v))
    y = nt(rt, s0) + apply(a_rk, v) + apply(a_rb, sa)
    same_head = ((lax.broadcasted_iota(jnp.int32, (lanes, lanes), 0) // HD)
                 == (lax.broadcasted_iota(jnp.int32, (lanes, lanes), 1) // HD))[None]
    s1 = s0 * jnp.exp(lg_end) + jnp.where(same_head, tn(v, k * to_end) + tn(sa, b * to_end), 0.0)
    return y, s1


PAIRS = HEADS // 2
PAIR_W = 2 * HD
SCAN_ARGS = (0, 3, 1, 2, 4, 5)


def _pair_stack(ref, off):
    return jnp.stack([ref[b, :, off + p * PAIR_W:off + (p + 1) * PAIR_W]
                      for b in range(ref.shape[0]) for p in range(PAIRS)])


def _pair_store(ref, off, val, add_ref=None):
    for b in range(ref.shape[0]):
        for p in range(PAIRS):
            sl = slice(off + p * PAIR_W, off + (p + 1) * PAIR_W)
            v = val[b * PAIRS + p]
            ref[b, :, sl] = v if add_ref is None else v + add_ref[b, :, sl]


def _scan_fwd(main6, batch, seq, side=None):
    c = min(SCAN_CHUNK, seq)
    nc = seq // c
    hp = batch * PAIRS
    srcs, per_peer = side if side is not None else ([], False)
    n_s = len(srcs)

    def body(*refs):
        z_ref, y_ref, s_ref, st = refs[0], refs[1 + n_s], refs[2 + n_s], refs[3 + 2 * n_s]
        _side_exchange(refs[1:1 + n_s], refs[3 + n_s:3 + 2 * n_s], per_peer, refs[4 + 2 * n_s:], nc)

        @pl.when(pl.program_id(0) == 0)
        def _():
            st[...] = jnp.zeros_like(st)

        s0 = st[...]
        s_ref[0] = s0
        y, s1 = _scan_chunk(s0, *[_pair_stack(z_ref, comp * HW) for comp in SCAN_ARGS])
        _pair_store(y_ref, 0, y)
        st[...] = s1

    res = pl.pallas_call(
        body, name="rwkv_scan_fwd", grid=(nc,),
        in_specs=[pl.BlockSpec((batch, c, 6 * HW), lambda i: (0, i, 0))] + [_HBM_SPEC] * n_s,
        out_specs=[pl.BlockSpec((batch, c, HW), lambda i: (0, i, 0)),
                   pl.BlockSpec((1, hp, PAIR_W, PAIR_W), lambda i: (i, 0, 0, 0))] + [_HBM_SPEC] * n_s,
        out_shape=[jax.ShapeDtypeStruct((batch, seq, HW), f32), jax.ShapeDtypeStruct((nc, hp, PAIR_W, PAIR_W), f32)]
        + _side_out_shapes(srcs, per_peer),
        scratch_shapes=[pltpu.VMEM((hp, PAIR_W, PAIR_W), f32)] + _side_sems(n_s),
        compiler_params=_cp(("arbitrary",)),
    )(main6.reshape(batch, seq, 6 * HW), *srcs)
    return res[0].reshape(batch * seq, HW), res[1], list(res[2:])


def _scan_bwd(main6, states, dy, extra, batch, seq, side=None):
    c = min(SCAN_CHUNK, seq)
    nc = seq // c
    hp = batch * PAIRS
    srcs, per_peer = side if side is not None else ([], False)
    n_s = len(srcs)

    def body(*refs):
        z_ref, s_ref, dy_ref, ex_ref = refs[:4]
        dz_ref, dst = refs[4 + n_s], refs[5 + 2 * n_s]
        _side_exchange(refs[4:4 + n_s], refs[5 + n_s:5 + 2 * n_s], per_peer, refs[6 + 2 * n_s:], nc)

        @pl.when(pl.program_id(0) == 0)
        def _():
            dst[...] = jnp.zeros_like(dst)

        _, vjp = jax.vjp(_scan_chunk, s_ref[0], *[_pair_stack(z_ref, comp * HW) for comp in SCAN_ARGS])
        g = vjp((_pair_stack(dy_ref, 0), dst[...]))
        dst[...] = g[0]
        for arg, comp in enumerate(SCAN_ARGS):
            _pair_store(dz_ref, comp * HW, g[1 + arg], ex_ref if comp < 3 else None)

    back = lambda i: (0, nc - 1 - i, 0)
    wide = pl.BlockSpec((batch, c, 6 * HW), back)
    res = pl.pallas_call(
        body, name="rwkv_scan_bwd", grid=(nc,),
        in_specs=[wide, pl.BlockSpec((1, hp, PAIR_W, PAIR_W), lambda i: (nc - 1 - i, 0, 0, 0)),
                  pl.BlockSpec((batch, c, HW), back), pl.BlockSpec((batch, c, 3 * HW), back)] + [_HBM_SPEC] * n_s,
        out_specs=[wide] + [_HBM_SPEC] * n_s,
        out_shape=[jax.ShapeDtypeStruct((batch, seq, 6 * HW), f32)] + _side_out_shapes(srcs, per_peer),
        scratch_shapes=[pltpu.VMEM((hp, PAIR_W, PAIR_W), f32)] + _side_sems(n_s),
        compiler_params=_cp(("arbitrary",)),
    )(main6.reshape(batch, seq, 6 * HW), states, dy.reshape(batch, seq, HW), extra.reshape(batch, seq, 3 * HW), *srcs)
    return res[0].reshape(batch * seq, 6 * HW), list(res[1:])


def _pad_cols(x, width):
    return jnp.pad(x, ((0, 0), (0, width - x.shape[1])))


def _split_w_in(wt):
    z = lambda rows: jnp.zeros((rows, wt.shape[1]), wt.dtype)
    w_r = jnp.concatenate([wt[1544:3080], wt[3080:3144], z(64), wt[3144:3208], z(64), wt[3208:3336]], axis=0)
    return wt[:1536], jnp.concatenate([wt[1536:1544], z(120)], axis=0), w_r, wt[3336:3848], wt[3848:]


def _merge_w_in(g_qkv, g_f, g_r, g_mq, g_g):
    return jnp.concatenate([g_qkv, g_f[:8], g_r[:1536], g_r[1536:1600], g_r[1664:1728], g_r[1792:], g_mq, g_g], axis=0)


def _pad_lora(v):
    z64 = jnp.zeros((1, 64), v.dtype)
    return jnp.concatenate([v[:, :1536], v[:, 1536:1600], z64, v[:, 1600:1664], z64, v[:, 1664:]], axis=1)


def _unpad_lora(v):
    return jnp.concatenate([v[:, :1536], v[:, 1536:1600], v[:, 1664:1728], v[:, 1792:]], axis=1)


def _local_step(x, mem, target, w, p, late=None, early=None, last=None):
    batch, seq, _ = x.shape
    t = batch * seq
    x2, tg2, mem2 = x.reshape(t, D), target.reshape(t, D), mem.reshape(batch * MEM_LEN, D)
    w_qkv, w_f, w_r, w_mq, w_g3 = _split_w_in(w["w_in"])
    mu = _pad_lora(p["rwkv_mu"])
    bias = _pad_cols(p["fox_f_bias"], 128)
    r_k = p["rwkv_r_k"].reshape(1, HW)
    post_params = [p["rwkv_gn_g"], p["rwkv_gn_b"], r_k]
    rw_widths = [HW, HW, HW, LORA_PAD, LORA_PAD, LORA_PAD]
    six = [HW] * 6

    p_g, u = _matmul("proj_gate", _lazy(_fn_rms, [(x2, [D])], D, params=[p["pre1_g"]]), w_g3, "nt", out_dtype=bf16)
    p_qkv = _matmul("proj_qkv", u, w_qkv, "nt", out_dtype=bf16)
    p_f = _matmul("proj_f", u, w_f, "nt")
    p_r = _matmul("proj_rwkv", u, w_r, "nt")
    p_mq = _matmul("proj_memq", u, w_mq, "nt", out_dtype=bf16)

    c = _fox_gate_fwd(p_f, bias, batch, seq)
    c_rows = c[:, :HEADS].reshape(batch, seq, HEADS).transpose(0, 2, 1)
    fox_o, lse, gathered = _fox_fwd(p_qkv, c, c_rows, batch, seq, side=(late[0], False) if late else None)
    if late:
        w = {**w, **late[2](gathered, 0)}
    fox_out = fox_o.astype(bf16)

    w_up = jnp.pad(w["rwkv_w_up"].astype(f32), ((0, LORA_PAD - 64), (0, 0)))
    a_up = jnp.pad(w["rwkv_a_up"].astype(f32), ((0, LORA_PAD - 64), (0, 0)))
    pre_params = [p["rwkv_w0"], w_up, p["rwkv_a0"], a_up, w["rwkv_g_up"].astype(f32), p["rwkv_k_k"], p["rwkv_k_a"]]
    ps = _tokshift_fwd(p_r, mu, batch, seq)
    main6, g_rw = _rows_fwd("rwkv_pre", _fn_rwkv_pre, [], [(ps, rw_widths)], pre_params, [six, [HW]], tm=256)
    y_rw, states, gathered = _scan_fwd(main6, batch, seq, side=(late[1], False) if late else None)
    if late:
        w = {**w, **late[2](gathered, 1)}
    post_consts = []
    post_rows = [(y_rw, [HW]), (main6, [HW, HW, HW]), (g_rw, [HW])]
    fn_post = _fn_rwkv_post

    (rwkv_out,) = _rows_fwd("rwkv_post", fn_post, post_consts, post_rows, post_params, [[HW]], dtypes=[bf16], tm=256)

    mem_kv, memn = _matmul("proj_memkv", _lazy(_fn_rms, [(mem2, [D])], D, params=[p["mem_norm_g"]]), w["w_mem_kv"], "nn")
    mem_out = _mem_fwd(p_mq, mem_kv, batch, seq)

    a_fox = _matmul("out_fox", fox_out, w["w_fox_out"], "nn", out_dtype=bf16)
    a_rwkv = _matmul("out_rwkv", rwkv_out, w["w_rwkv_out"], "nn", out_dtype=bf16)
    a_mem = _matmul("out_mem", mem_out, w["w_mem_out"], "nn", out_dtype=bf16)
    merge_rows = [(a_fox, [D]), (a_rwkv, [D]), (a_mem, [D]), (p_g, [D, D, D])]
    yy, merged = _matmul("out_o", _lazy(_fn_merge, merge_rows, D), w["w_o"], "nn")
    post1_rows = [(yy, [D]), (x2, [D])]
    post1_params = [p["post1_g"], p["pre2_g"]]
    h1, u2 = _rows_fwd("post1", _fn_post1, [], post1_rows, post1_params, [[D], [D]], dtypes=[f32, bf16])
    gp = _matmul("ffn_gate", u2, w["w_ffn_gate"], "nt", out_dtype=bf16)
    up = _matmul("ffn_up", u2, w["w_ffn_up"], "nt", out_dtype=bf16)
    ffn, hmid = _matmul("ffn_down", _lazy(_fn_swiglu, [(gp, [D_FF]), (up, [D_FF])], D_FF), w["w_ffn_down"], "nn")
    final_rows = [(ffn, [D]), (h1, [D])]

    gw, gp_ = {}, {}
    (d_ffn, d_h1), (gp_["post2_g"], loss) = _rows_bwd("final", _fn_final, [(tg2, [D])], final_rows, [p["post2_g"]], [], [],
                                                      n_sums=1, dtypes=[bf16, f32])
    gw["w_ffn_down"] = _matmul("ffn_down_dw", hmid, d_ffn, "tn", out_dtype=bf16)
    d_gp, d_up = _matmul_then_vjp("ffn_down_dx", d_ffn, w["w_ffn_down"], "nt", _fn_swiglu,
                                  [(gp, [D_FF]), (up, [D_FF])], [bf16, bf16])
    d_u2 = _matmul("ffn_gate_dx", d_gp, w["w_ffn_gate"], "nn")
    d_u2 = _matmul("ffn_up_dx", d_up, w["w_ffn_up"], "nn", add=d_u2)
    gw["w_ffn_gate"] = _matmul("ffn_gate_dw", d_gp, u2, "tn", out_dtype=bf16)
    gw["w_ffn_up"] = _matmul("ffn_up_dw", d_up, u2, "tn", out_dtype=bf16)
    (d_yy, d_x_res), (gp_["post1_g"], gp_["pre2_g"]) = _rows_bwd(
        "post1_bwd", _fn_post1, [], post1_rows, post1_params, [[D], [D]], [d_h1, d_u2], dtypes=[bf16, f32])
    gw["w_o"] = _matmul("out_o_dw", merged, d_yy, "tn", out_dtype=bf16)
    d_a_fox, d_a_rwkv, d_a_mem, d_p_g = _matmul_then_vjp("out_o_dx", d_yy, w["w_o"], "nt", _fn_merge, merge_rows, [bf16] * 4)
    d_fox_out = _matmul("out_fox_dx", d_a_fox, w["w_fox_out"], "nt")
    gw["w_fox_out"] = _matmul("out_fox_dw", fox_out, d_a_fox, "tn", out_dtype=bf16)
    d_rwkv_out = _matmul("out_rwkv_dx", d_a_rwkv, w["w_rwkv_out"], "nt")
    gw["w_rwkv_out"] = _matmul("out_rwkv_dw", rwkv_out, d_a_rwkv, "tn", out_dtype=bf16)
    d_mem_out = _matmul("out_mem_dx", d_a_mem, w["w_mem_out"], "nt")
    gw["w_mem_out"] = _matmul("out_mem_dw", mem_out, d_a_mem, "tn", out_dtype=bf16)

    d_p_mq, d_km, d_vm = _mem_bwd(p_mq, mem_kv, d_mem_out, batch, seq)
    d_mem_kv = jnp.concatenate([d_km, d_vm], axis=1).astype(bf16)
    gw["w_mem_kv"] = _matmul("proj_memkv_dw", memn, d_mem_kv, "tn", out_dtype=bf16)
    d_memn = _matmul("proj_memkv_dx", d_mem_kv, w["w_mem_kv"], "nt")
    _, (gp_["mem_norm_g"],) = _rows_bwd("rms_mem_bwd", _fn_rms, [], [(mem2, [D])], [p["mem_norm_g"]], [[D]], [d_memn])

    d_q, d_k, d_v, d_cq, d_ck = _fox_bwd(p_qkv, c, c_rows, fox_o, lse, d_fox_out, batch, seq)
    d_p_qkv = jnp.concatenate([d_q, d_k, d_v], axis=1).astype(bf16)
    d_p_f, d_bias = _fox_gate_bwd(p_f, bias, d_cq, d_ck, batch, seq)
    gp_["fox_f_bias"] = d_bias[:, :HEADS]

    (d_y_rw, d_main6_post, d_g_rw), (gp_["rwkv_gn_g"], gp_["rwkv_gn_b"], d_rk) = _rows_bwd(
        "rwkv_post_bwd", fn_post, post_consts, post_rows, post_params, [[HW]], [d_rwkv_out], tm=256)
    gp_["rwkv_r_k"] = d_rk.reshape(1, HEADS, HD)
    d_main6, early_got = _scan_bwd(main6, states, d_y_rw, d_main6_post, batch, seq,
                                   side=(early(gw), True) if early else None)

    def fn_pre_sum(*args):
        return _fn_rwkv_pre(*args)

    (d_ps,), d_pre = _rows_bwd("rwkv_pre_bwd", fn_pre_sum, [], [(ps, rw_widths)], pre_params, [six, [HW]],
                               [d_main6, d_g_rw], tm=256)
    gp_["rwkv_w0"], d_w_up, gp_["rwkv_a0"], d_a_up, gw["rwkv_g_up"], gp_["rwkv_k_k"], gp_["rwkv_k_a"] = d_pre
    gw["rwkv_w_up"], gw["rwkv_a_up"] = d_w_up[:64], d_a_up[:64]
    d_p_r, d_mu = _tokshift_bwd(p_r, mu, d_ps, batch, seq)
    gp_["rwkv_mu"] = _unpad_lora(d_mu)

    gw["w_in"] = _merge_w_in(_matmul("proj_qkv_dw", d_p_qkv, u, "tn", out_dtype=bf16), _matmul("proj_f_dw", d_p_f, u, "tn", out_dtype=bf16),
                             _matmul("proj_rwkv_dw", d_p_r, u, "tn", out_dtype=bf16), _matmul("proj_memq_dw", d_p_mq, u, "tn", out_dtype=bf16),
                             _matmul("proj_gate_dw", d_p_g, u, "tn", out_dtype=bf16))
    d_x, gp_["pre1_g"], last_got = _input_cotangent(
        "proj_dx", [d_p_qkv, d_p_f, d_p_r, d_p_mq, d_p_g], [w_qkv, w_f, w_r, w_mq, w_g3], x2, p["pre1_g"], d_x_res,
        side=(last(gw), True) if last else None)
    return loss, d_x.reshape(x.shape), gw, gp_, early_got, last_got


def _adamw(name, recv, row_off, w, m, v):
    _, rows, cols = w.shape
    row_tiles = [t for t in range(16, min(rows, 128) + 1, 16) if rows % t == 0 and row_off % t == 0]
    if row_tiles:
        tr, tc = max(row_tiles), cols
        first, grid = row_off // tr, (rows // tr,)
        at = lambda i: (0, first + i, 0)
        mine = lambda i: (0, i, 0)
    else:
        assert row_off == 0 and recv.shape[1] == rows
        tr, tc = rows, 128
        grid = (cols // tc,)
        at = mine = lambda i: (0, 0, i)

    def body(g_ref, w_ref, m_ref, v_ref, go_ref, d_ref, mo_ref, vo_ref):
        g = g_ref[0].astype(f32)
        for s in range(1, N_DEV):
            g = g + g_ref[s].astype(f32)
        m_new = ADAM_B1 * m_ref[0] + (1.0 - ADAM_B1) * g
        v_new = ADAM_B2 * v_ref[0] + (1.0 - ADAM_B2) * (g * g)
        m_hat = m_new / (1.0 - ADAM_B1 ** ADAM_STEP)
        v_hat = v_new / (1.0 - ADAM_B2 ** ADAM_STEP)
        go_ref[0] = g
        d_ref[0] = -ADAM_LR * (m_hat / (jnp.sqrt(v_hat) + ADAM_EPS) + ADAM_WD * w_ref[0])
        mo_ref[0] = m_new
        vo_ref[0] = v_new

    spec = pl.BlockSpec((1, tr, tc), mine)
    return pl.pallas_call(
        body, name=name, grid=grid,
        in_specs=[pl.BlockSpec((N_DEV, tr, tc), at), spec, spec, spec],
        out_specs=[spec] * 4, out_shape=[jax.ShapeDtypeStruct(w.shape, f32)] * 4,
        compiler_params=_cp(("parallel",)),
    )(recv, w, m, v)


GROUPS = (
    ("in", ("w_in",), 0),
    ("memkv", ("w_mem_kv",), 0),
    ("ffn_gu", ("w_ffn_gate", "w_ffn_up"), 0),
    ("down_o", ("w_ffn_down", "w_o"), 0),
    ("outs", ("w_fox_out", "w_rwkv_out", "w_mem_out"), 0),
    ("lora", ("rwkv_w_up", "rwkv_a_up", "rwkv_g_up"), 0),
)
FIRST_GROUPS = ("in", "memkv")
LATE_GROUPS = (("down_o", "outs", "lora"), ("ffn_gu",))
EARLY_GRAD_GROUPS = ("memkv", "ffn_gu", "down_o", "outs")
LAST_GRAD_GROUPS = ("in", "lora")
SHARD_AXIS = {n: a for n, _, a in SHARDED}
SMALL_ROWS = 16
LOSS_LANES = 128


def _group_local(shards, members, join):
    parts = [shards[n].reshape(shards[n].shape[-2:]) for n in members]
    return parts[0] if len(parts) == 1 else jnp.concatenate(parts, axis=join)


def _group_split(arr, members, join, lead=False):
    out, off = {}, 0
    for n in members:
        shape = dict((k, s) for k, s, _ in SHARDED)[n]
        size = _block_shape(shape, SHARD_AXIS[n])[join]
        idx = [slice(None)] * arr.ndim
        idx[arr.ndim - 2 + join] = slice(off, off + size)
        out[n] = arr[tuple(idx)]
        off += size
    return out


def _full_from_blocks(blocks, axis):
    if axis == 0:
        return blocks.reshape(-1, blocks.shape[2])
    return blocks.transpose(1, 0, 2).reshape(blocks.shape[1], -1)


def _blocks_from_full(full, axis):
    if axis == 0:
        return full.reshape(N_DEV, -1, full.shape[1])
    return full.reshape(full.shape[0], N_DEV, -1).transpose(1, 0, 2)


def _assemble(gathered, names):
    out = {}
    for arr, g in zip(gathered, names):
        _, members, join = [grp for grp in GROUPS if grp[0] == g][0]
        for n, blk in _group_split(arr, members, join, lead=True).items():
            out[n] = _full_from_blocks(blk, SHARD_AXIS[n])
    return out


def _grad_blocks(gw, names):
    out = []
    for g in names:
        _, members, join = [grp for grp in GROUPS if grp[0] == g][0]
        parts = [_blocks_from_full(gw[n].astype(bf16), SHARD_AXIS[n]) for n in members]
        out.append(parts[0] if len(parts) == 1 else jnp.concatenate(parts, axis=1 + join))
    return out


def _small_pack(d):
    flat = jnp.concatenate([d[n].reshape(-1) for n, _ in REPLICATED])
    return jnp.pad(flat, (0, SMALL_ROWS * LANES - REPL_ELEMS)).reshape(SMALL_ROWS, LANES)


def _small_unpack(packed):
    out, flat, off = {}, packed.reshape(-1), 0
    for n, shape in REPLICATED:
        k = _rows_of((LANES,) + shape)
        out[n] = flat[off:off + k].reshape(shape)
        off += k
    return out


def kernel(x, mem, pre1_g, post1_g, pre2_g, post2_g, mem_norm_g, w_in, fox_f_bias, rwkv_mu, rwkv_w0, rwkv_w_up, rwkv_a0, rwkv_a_up, rwkv_g_up, rwkv_k_k, rwkv_k_a, rwkv_r_k, rwkv_gn_g, rwkv_gn_b, w_mem_kv, w_fox_out, w_rwkv_out, w_mem_out, w_o, w_ffn_gate, w_ffn_up, w_ffn_down, loss_target, m_pre1_g, m_post1_g, m_pre2_g, m_post2_g, m_mem_norm_g, m_w_in, m_fox_f_bias, m_rwkv_mu, m_rwkv_w0, m_rwkv_w_up, m_rwkv_a0, m_rwkv_a_up, m_rwkv_g_up, m_rwkv_k_k, m_rwkv_k_a, m_rwkv_r_k, m_rwkv_gn_g, m_rwkv_gn_b, m_w_mem_kv, m_w_fox_out, m_w_rwkv_out, m_w_mem_out, m_w_o, m_w_ffn_gate, m_w_ffn_up, m_w_ffn_down, v_pre1_g, v_post1_g, v_pre2_g, v_post2_g, v_mem_norm_g, v_w_in, v_fox_f_bias, v_rwkv_mu, v_rwkv_w0, v_rwkv_w_up, v_rwkv_a0, v_rwkv_a_up, v_rwkv_g_up, v_rwkv_k_k, v_rwkv_k_a, v_rwkv_r_k, v_rwkv_gn_g, v_rwkv_gn_b, v_w_mem_kv, v_w_fox_out, v_w_rwkv_out, v_w_mem_out, v_w_o, v_w_ffn_gate, v_w_ffn_up, v_w_ffn_down):
    args = dict(locals())
    turn = lambda n, a: jnp.swapaxes(a, 1, 2) if n in TRANSPOSED else a
    wts = {n: turn(n, args[n]) for n in WEIGHT_ORDER}
    ms = {n: turn(n, args["m_" + n]) for n in WEIGHT_ORDER}
    vs = {n: turn(n, args["v_" + n]) for n in WEIGHT_ORDER}

    groups = {g: (members, join) for g, members, join in GROUPS}
    w_bf16 = {n: wts[n].astype(bf16) for n, _, _ in SHARDED}

    def send(g):
        return _group_local(w_bf16, *groups[g])

    first = _exchange("gather_first", [send(g) for g in FIRST_GROUPS], per_peer=False)
    full = _assemble(first, FIRST_GROUPS)
    small_in = {n: (wts[n] if n == "rwkv_r_k" else wts[n].reshape(wts[n].shape[-2:])) for n, _ in REPLICATED}
    late = ([send(g) for g in LATE_GROUPS[0]], [send(g) for g in LATE_GROUPS[1]],
            lambda got, which: _assemble(got, LATE_GROUPS[which]))
    loss_part, grad_x, gw, gp, early_got, last_got = _local_step(
        x, mem, loss_target, full, small_in, late=late, early=lambda g: _grad_blocks(g, EARLY_GRAD_GROUPS),
        last=lambda g: _grad_blocks(g, LAST_GRAD_GROUPS))
    small_got, loss_got = _exchange("exchange_small", [_small_pack(gp).astype(bf16), jnp.broadcast_to(loss_part, (8, LOSS_LANES))],
                                    per_peer=False)
    received = dict(zip(EARLY_GRAD_GROUPS + LAST_GRAD_GROUPS, list(early_got) + list(last_got)))

    outs = [{}, {}, {}, {}]
    for g, members, _ in GROUPS:
        off = 0
        for n in members:
            for o, arr in zip(outs, _adamw("adamw_" + n, received[g], off, wts[n], ms[n], vs[n])):
                o[n] = arr
            off += wts[n].shape[1]
    res = _adamw("adamw_small", small_got, 0, *[_small_pack(d)[None] for d in (wts, ms, vs)])
    for o, arr in zip(outs, res):
        o.update(_small_unpack(arr))
    loss = jnp.sum(loss_got[:, 0, 0])
    return (loss, grad_x, *[turn(n, o[n].reshape(wts[n].shape)) for o in outs for n in WEIGHT_ORDER])
```

```python
import functools

import jax
import jax.numpy as jnp
from jax import lax
from jax.experimental import pallas as pl
from jax.experimental.pallas import tpu as pltpu

f32 = jnp.float32
bf16 = jnp.bfloat16
_HI = lax.Precision.HIGHEST

D = 1024
HEADS = 8
HD = 64
HW = HEADS * HD
MEM_HEADS = 4
MEM_HD = 128
MEM_W = 512
MEM_LEN = 256
D_FF = 2816
LORA_PAD = 128
NORM_EPS = 1e-6
GN_EPS = 64e-5
SCAN_CHUNK = 64
N_DEV = 8
LANES = 1024
VMEM_LIMIT = 56 * 1024 * 1024

ADAM_LR = 0.001
ADAM_B1 = 0.9
ADAM_B2 = 0.999
ADAM_EPS = 1e-08
ADAM_WD = 0.01
ADAM_STEP = 10

TRANSPOSED = ("w_in", "w_ffn_gate", "w_ffn_up")
SHARDED = (
    ("w_in", (6920, 1024), 0),
    ("w_ffn_gate", (2816, 1024), 0),
    ("w_ffn_up", (2816, 1024), 0),
    ("w_ffn_down", (2816, 1024), 0),
    ("w_mem_kv", (1024, 1024), 0),
    ("w_o", (1024, 1024), 0),
    ("w_fox_out", (512, 1024), 1),
    ("w_rwkv_out", (512, 1024), 1),
    ("w_mem_out", (512, 1024), 1),
    ("rwkv_w_up", (64, 512), 1),
    ("rwkv_a_up", (64, 512), 1),
    ("rwkv_g_up", (128, 512), 1),
)
REPLICATED = (
    ("pre1_g", (1, 1024)), ("post1_g", (1, 1024)), ("pre2_g", (1, 1024)), ("post2_g", (1, 1024)),
    ("mem_norm_g", (1, 1024)), ("fox_f_bias", (1, 8)), ("rwkv_mu", (1, 1792)), ("rwkv_w0", (1, 512)),
    ("rwkv_a0", (1, 512)), ("rwkv_k_k", (1, 512)), ("rwkv_k_a", (1, 512)), ("rwkv_r_k", (1, 8, 64)),
    ("rwkv_gn_g", (1, 512)), ("rwkv_gn_b", (1, 512)),
)
WEIGHT_ORDER = ('pre1_g', 'post1_g', 'pre2_g', 'post2_g', 'mem_norm_g', 'w_in', 'fox_f_bias', 'rwkv_mu',
                'rwkv_w0', 'rwkv_w_up', 'rwkv_a0', 'rwkv_a_up', 'rwkv_g_up', 'rwkv_k_k', 'rwkv_k_a',
                'rwkv_r_k', 'rwkv_gn_g', 'rwkv_gn_b', 'w_mem_kv', 'w_fox_out', 'w_rwkv_out', 'w_mem_out',
                'w_o', 'w_ffn_gate', 'w_ffn_up', 'w_ffn_down')


def _block_shape(shape, axis):
    return tuple(s // N_DEV if i == axis else s for i, s in enumerate(shape))


def _rows_of(shape):
    n = 1
    for s in shape:
        n *= s
    return n // LANES


REPL_ELEMS = sum(_rows_of((LANES,) + s) for _, s in REPLICATED)


def _cp(sem=None):
    return pltpu.CompilerParams(dimension_semantics=sem, vmem_limit_bytes=VMEM_LIMIT)


def _tile(dim, cap):
    best = None
    for t in range(128, min(dim, cap) + 1, 128):
        if dim % t == 0:
            best = t
    return best if best is not None else dim


def _two_terms(x):
    hi = x.astype(bf16)
    return hi, (x - hi.astype(f32)).astype(bf16)


def _dg(a, b, dims, exact):
    if exact == "split":
        (a_hi, a_lo), (b_hi, b_lo) = _two_terms(a), _two_terms(b)
        dot = functools.partial(lax.dot_general, dimension_numbers=dims, preferred_element_type=f32)
        return dot(a_hi, b_hi) + (dot(a_hi, b_lo) + dot(a_lo, b_hi))
    if exact:
        return lax.dot_general(a, b, dims, precision=_HI, preferred_element_type=f32)
    return lax.dot_general(a.astype(bf16), b.astype(bf16), dims, preferred_element_type=f32)


def _make_mm(batched, exact):
    o = 1 if batched else 0
    bd = ((0,), (0,)) if batched else ((), ())
    d_nn = (((1 + o,), (o,)), bd)
    d_nt = (((1 + o,), (1 + o,)), bd)
    d_tn = (((o,), (o,)), bd)

    @jax.custom_vjp
    def nn(a, b):
        return _dg(a, b, d_nn, exact)

    @jax.custom_vjp
    def nt(a, b):
        return _dg(a, b, d_nt, exact)

    @jax.custom_vjp
    def tn(a, b):
        return _dg(a, b, d_tn, exact)

    nn.defvjp(lambda a, b: (_dg(a, b, d_nn, exact), (a, b)),
              lambda res, g: (_dg(g, res[1], d_nt, exact), _dg(res[0], g, d_tn, exact)))
    nt.defvjp(lambda a, b: (_dg(a, b, d_nt, exact), (a, b)),
              lambda res, g: (_dg(g, res[1], d_nn, exact), _dg(g, res[0], d_tn, exact)))
    tn.defvjp(lambda a, b: (_dg(a, b, d_tn, exact), (a, b)),
              lambda res, g: (_dg(res[1], g, d_nt, exact), _dg(res[0], g, d_nn, exact)))
    return nn, nt, tn


def _sigmoid(x):
    return 1.0 / (1.0 + jnp.exp(-x))


def _head_sum_raw(x):
    width = 2 * HD
    i = lax.broadcasted_iota(jnp.int32, (width, width), 0) // HD
    j = lax.broadcasted_iota(jnp.int32, (width, width), 1) // HD
    m = (i == j).astype(bf16)
    dims = (((1,), (0,)), ((), ()))
    out = []
    for p in range(x.shape[1] // width):
        xp = x[:, p * width:(p + 1) * width]
        hi = xp.astype(bf16)
        lo = (xp - hi.astype(f32)).astype(bf16)
        out.append(lax.dot_general(hi, m, dims, preferred_element_type=f32)
                   + lax.dot_general(lo, m, dims, preferred_element_type=f32))
    return jnp.concatenate(out, axis=1)


@jax.custom_vjp
def _head_sum(x):
    return _head_sum_raw(x)


_head_sum.defvjp(lambda x: (_head_sum_raw(x), None), lambda _, g: (_head_sum_raw(g),))


WEIGHT_TILE_BYTES = 13 * 512 * 1024
ACC_TILE_BYTES = 8 * 1024 * 1024


def _lazy(fn, rows, width, params=()):
    return (fn, rows, width, list(params))


def _matmul(name, a, b, mode, add=None, out_dtype=f32):
    has_add = add is not None
    if isinstance(a, tuple):
        a_fn, a_rows, a_width, a_params = a
        a_arrays = [r for r, _ in a_rows]
        a_shape = (a_arrays[0].shape[0], a_width)
    else:
        a_fn, a_rows, a_params, a_arrays, a_shape = None, None, [], [a], a.shape
    n_r = len(a_arrays)
    n_a = n_r + len(a_params)

    def load_a(refs):
        if a_fn is None:
            return refs[0][...].astype(bf16)
        pieces = []
        for r, (_, widths) in zip(refs[:n_r], a_rows):
            pieces += _pieces(r, widths)
        return a_fn(*pieces, *[p[...] for p in refs[n_r:]])[0].astype(bf16)

    if mode == "tn":
        assert a_fn is None
        (k, m), (_, n) = a_shape, b.shape
        tn = _tile(n, max(128, ACC_TILE_BYTES // (4 * m)))
        tk = _tile(k, 1024)
        nk = k // tk

        def body(*refs):
            b_ref, o_ref, acc = refs[n_a:]

            @pl.when(pl.program_id(1) == 0)
            def _():
                acc[...] = jnp.zeros_like(acc)

            acc[...] += lax.dot_general(load_a(refs[:n_a]), b_ref[...].astype(bf16),
                                        (((0,), (0,)), ((), ())), preferred_element_type=f32)

            @pl.when(pl.program_id(1) == nk - 1)
            def _():
                o_ref[...] = acc[...].astype(o_ref.dtype)

        return pl.pallas_call(
            body, name=name, grid=(n // tn, nk),
            in_specs=[pl.BlockSpec((tk, r.shape[1]), lambda j, kk: (kk, 0)) for r in a_arrays]
            + [pl.BlockSpec((tk, tn), lambda j, kk: (kk, j))],
            out_specs=pl.BlockSpec((m, tn), lambda j, kk: (0, j)), out_shape=jax.ShapeDtypeStruct((m, n), out_dtype),
            scratch_shapes=[pltpu.VMEM((m, tn), f32)],
            compiler_params=_cp(("parallel", "arbitrary")),
        )(*a_arrays, b)

    (m, k) = a_shape
    n = b.shape[1] if mode == "nn" else b.shape[0]
    tm = _tile(m, 1024 if a_fn is None else 512)
    tn = _tile(n, max(128, WEIGHT_TILE_BYTES // (2 * k)))
    dims = (((1,), (0,)), ((), ())) if mode == "nn" else (((1,), (1,)), ((), ()))
    b_spec = pl.BlockSpec((k, tn), lambda j, i: (0, j)) if mode == "nn" else pl.BlockSpec((tn, k), lambda j, i: (j, 0))
    o_spec = pl.BlockSpec((tm, tn), lambda j, i: (i, j))

    keep = a_fn is not None
    assert not keep or tn == n

    def body(*refs):
        b_ref = refs[n_a]
        a_val = load_a(refs[:n_a])
        r = lax.dot_general(a_val, b_ref[...].astype(bf16), dims, preferred_element_type=f32)
        if has_add:
            r = r + refs[n_a + 1][...]
        if keep:
            refs[-2][...] = r.astype(refs[-2].dtype)
            refs[-1][...] = a_val
        else:
            refs[-1][...] = r.astype(refs[-1].dtype)

    res = pl.pallas_call(
        body, name=name, grid=(n // tn, m // tm),
        in_specs=[pl.BlockSpec((tm, r.shape[1]), lambda j, i: (i, 0)) for r in a_arrays]
        + [pl.BlockSpec(p.shape, lambda j, i: (0, 0)) for p in a_params] + [b_spec] + ([o_spec] if has_add else []),
        out_specs=[o_spec] + ([pl.BlockSpec((tm, k), lambda j, i: (i, 0))] if keep else []),
        out_shape=[jax.ShapeDtypeStruct((m, n), out_dtype)] + ([jax.ShapeDtypeStruct((m, k), bf16)] if keep else []),
        compiler_params=_cp(("parallel", "arbitrary")),
    )(*a_arrays, *a_params, b, *([add] if has_add else []))
    return tuple(res) if keep else res[0]


def _input_cotangent(name, a_list, b_list, x, gain, add, side=None):
    m = a_list[0].shape[0]
    tm = _tile(m, 256)
    n_g = len(a_list)
    srcs, per_peer = side if side is not None else ([], False)
    n_s = len(srcs)

    def body(*refs):
        x_ref, g_ref, add_ref = refs[2 * n_g:2 * n_g + 3]
        src_refs = refs[2 * n_g + 3:2 * n_g + 3 + n_s]
        dx_ref, dg_ref = refs[2 * n_g + 3 + n_s:2 * n_g + 5 + n_s]
        _side_exchange(src_refs, refs[2 * n_g + 5 + n_s:2 * n_g + 5 + 2 * n_s], per_peer, refs[2 * n_g + 5 + 2 * n_s:], m // tm)
        d_u = None
        for g in range(n_g):
            r = lax.dot_general(refs[g][...].astype(bf16), refs[n_g + g][...].astype(bf16), (((1,), (0,)), ((), ())),
                                preferred_element_type=f32)
            d_u = r if d_u is None else d_u + r
        _, vjp = jax.vjp(_rms, x_ref[...], g_ref[...])
        d_x, d_gain = vjp(d_u)
        dx_ref[...] = d_x + add_ref[...]

        @pl.when(pl.program_id(0) == 0)
        def _():
            dg_ref[...] = jnp.zeros_like(dg_ref)

        dg_ref[...] += d_gain

    rows = pl.BlockSpec((tm, x.shape[1]), lambda i: (i, 0))
    whole = lambda b: pl.BlockSpec(b.shape, lambda i: (0, 0))
    res = pl.pallas_call(
        body, name=name, grid=(m // tm,),
        in_specs=[pl.BlockSpec((tm, a.shape[1]), lambda i: (i, 0)) for a in a_list] + [whole(b) for b in b_list]
        + [rows, whole(gain), rows] + [_HBM_SPEC] * n_s,
        out_specs=[rows, whole(gain)] + [_HBM_SPEC] * n_s,
        out_shape=[jax.ShapeDtypeStruct(x.shape, f32), jax.ShapeDtypeStruct(gain.shape, f32)] + _side_out_shapes(srcs, per_peer),
        scratch_shapes=_side_sems(n_s),
        compiler_params=_cp(("arbitrary",)),
    )(*a_list, *b_list, x, gain, add, *srcs)
    return res[0], res[1], list(res[2:])


def _pieces(ref, widths):
    out, off = [], 0
    for w in widths:
        out.append(ref[:, off:off + w].astype(f32))
        off += w
    return out


def _store_pieces(ref, widths, vals, add_ref=None):
    off = 0
    for w, v in zip(widths, vals):
        ref[:, off:off + w] = (v if add_ref is None else v + add_ref[:, off:off + w]).astype(ref.dtype)
        off += w


def _rows_fwd(name, fn, consts, rows, params, outs, n_sums=0, tm=512, dtypes=None):
    t = (consts + rows)[0][0].shape[0]
    tm = min(tm, t)
    ins = consts + rows
    n_in, n_p, n_o = len(ins), len(params), len(outs)
    dtypes = dtypes or [f32] * n_o

    def body(*refs):
        in_refs, p_refs = refs[:n_in], refs[n_in:n_in + n_p]
        o_refs, s_refs = refs[n_in + n_p:n_in + n_p + n_o], refs[n_in + n_p + n_o:]
        vals = []
        for r, (_, widths) in zip(in_refs, ins):
            vals += _pieces(r, widths)
        res = fn(*vals, *[p[...] for p in p_refs])
        pos = 0
        for r, widths in zip(o_refs, outs):
            _store_pieces(r, widths, res[pos:pos + len(widths)])
            pos += len(widths)

        @pl.when(pl.program_id(0) == 0)
        def _():
            for s in s_refs:
                s[...] = jnp.zeros_like(s)

        for s, v in zip(s_refs, res[pos:]):
            s[...] += v

    row_spec = lambda w: pl.BlockSpec((tm, w), lambda i: (i, 0))
    full = lambda p: pl.BlockSpec(p.shape, lambda i: (0,) * p.ndim)
    return pl.pallas_call(
        body, name=name, grid=(t // tm,),
        in_specs=[row_spec(sum(w)) for _, w in ins] + [full(p) for p in params],
        out_specs=[row_spec(sum(w)) for w in outs] + [pl.BlockSpec((1, 1), lambda i: (0, 0))] * n_sums,
        out_shape=[jax.ShapeDtypeStruct((t, sum(w)), dt) for w, dt in zip(outs, dtypes)] + [jax.ShapeDtypeStruct((1, 1), f32)] * n_sums,
        compiler_params=_cp(("arbitrary",)),
    )(*[a for a, _ in ins], *params)


def _rows_bwd(name, fn, consts, rows, params, outs, cts, n_sums=0, add=None, tm=512, dtypes=None):
    t = (consts + rows)[0][0].shape[0]
    tm = min(tm, t)
    n_c, n_r, n_p, n_o = len(consts), len(rows), len(params), len(outs)
    has_add = add is not None
    dtypes = dtypes or [f32] * n_r

    def body(*refs):
        pos = 0
        c_refs = refs[pos:pos + n_c]; pos += n_c
        r_refs = refs[pos:pos + n_r]; pos += n_r
        p_refs = refs[pos:pos + n_p]; pos += n_p
        ct_refs = refs[pos:pos + n_o]; pos += n_o
        add_ref = refs[pos] if has_add else None
        pos += 1 if has_add else 0
        dr_refs = refs[pos:pos + n_r]; pos += n_r
        dp_refs = refs[pos:pos + n_p]; pos += n_p
        s_refs = refs[pos:pos + n_sums]
        cvals, rvals = [], []
        for r, (_, widths) in zip(c_refs, consts):
            cvals += _pieces(r, widths)
        for r, (_, widths) in zip(r_refs, rows):
            rvals += _pieces(r, widths)
        pvals = [p[...] for p in p_refs]
        ctv = []
        for r, widths in zip(ct_refs, outs):
            ctv += _pieces(r, widths)
        ctv += [jnp.ones((1, 1), f32)] * n_sums
        primal, vjp = jax.vjp(lambda *rp: tuple(fn(*cvals, *rp)), *rvals, *pvals)
        g = vjp(tuple(ctv))
        pos = 0
        for idx, (r, (_, widths)) in enumerate(zip(dr_refs, rows)):
            _store_pieces(r, widths, g[pos:pos + len(widths)], add_ref if idx == 0 else None)
            pos += len(widths)

        @pl.when(pl.program_id(0) == 0)
        def _():
            for acc in list(dp_refs) + list(s_refs):
                acc[...] = jnp.zeros_like(acc)

        for dp, v in zip(dp_refs, g[pos:]):
            dp[...] += v
        for s, v in zip(s_refs, primal[len(primal) - n_sums:]):
            s[...] += v

    row_spec = lambda w: pl.BlockSpec((tm, w), lambda i: (i, 0))
    full = lambda p: pl.BlockSpec(p.shape, lambda i: (0,) * p.ndim)
    args = [a for a, _ in consts + rows] + list(params) + list(cts) + ([add] if has_add else [])
    res = pl.pallas_call(
        body, name=name, grid=(t // tm,),
        in_specs=[row_spec(sum(w)) for _, w in consts + rows] + [full(p) for p in params]
        + [row_spec(sum(w)) for w in outs] + ([row_spec(add.shape[1])] if has_add else []),
        out_specs=[row_spec(sum(w)) for _, w in rows] + [full(p) for p in params]
        + [pl.BlockSpec((1, 1), lambda i: (0, 0))] * n_sums,
        out_shape=[jax.ShapeDtypeStruct((t, sum(w)), dt) for (_, w), dt in zip(rows, dtypes)]
        + [jax.ShapeDtypeStruct(p.shape, f32) for p in params] + [jax.ShapeDtypeStruct((1, 1), f32)] * n_sums,
        compiler_params=_cp(("arbitrary",)),
    )(*args)
    return res[:n_r], res[n_r:n_r + n_p] + res[n_r + n_p:]


def _matmul_then_vjp(name, a, b, mode, fn, rows, dtypes, params=(), first_cts=(), add=None, tm=256):
    m, k = a.shape
    tm = min(tm, m)
    dims = (((1,), (0,)), ((), ())) if mode == "nn" else (((1,), (1,)), ((), ()))
    n_r, n_p, n_c = len(rows), len(params), len(first_cts)
    has_add = add is not None

    def body(*refs):
        a_ref, b_ref = refs[:2]
        pos = 2
        r_refs = refs[pos:pos + n_r]; pos += n_r
        p_refs = refs[pos:pos + n_p]; pos += n_p
        c_refs = refs[pos:pos + n_c]; pos += n_c
        add_ref = refs[pos] if has_add else None
        pos += 1 if has_add else 0
        dr_refs = refs[pos:pos + n_r]; pos += n_r
        dp_refs = refs[pos:pos + n_p]
        ct = lax.dot_general(a_ref[...].astype(bf16), b_ref[...].astype(bf16), dims, preferred_element_type=f32)
        if has_add:
            ct = ct + add_ref[...]
        rvals = []
        for r, (_, widths) in zip(r_refs, rows):
            rvals += _pieces(r, widths)
        _, vjp = jax.vjp(lambda *rp: tuple(fn(*rp)), *rvals, *[p[...] for p in p_refs])
        g = vjp(tuple(c[...].astype(f32) for c in c_refs) + (ct,))
        pos = 0
        for r, (_, widths) in zip(dr_refs, rows):
            _store_pieces(r, widths, g[pos:pos + len(widths)])
            pos += len(widths)

        @pl.when(pl.program_id(0) == 0)
        def _():
            for dp in dp_refs:
                dp[...] = jnp.zeros_like(dp)

        for dp, v in zip(dp_refs, g[pos:]):
            dp[...] += v

    row_spec = lambda w: pl.BlockSpec((tm, w), lambda i: (i, 0))
    whole = lambda p: pl.BlockSpec(p.shape, lambda i: (0, 0))
    res = pl.pallas_call(
        body, name=name, grid=(m // tm,),
        in_specs=[row_spec(k), whole(b)] + [row_spec(r.shape[1]) for r, _ in rows] + [whole(p) for p in params]
        + [row_spec(c.shape[1]) for c in first_cts] + ([row_spec(add.shape[1])] if has_add else []),
        out_specs=[row_spec(r.shape[1]) for r, _ in rows] + [whole(p) for p in params],
        out_shape=[jax.ShapeDtypeStruct(r.shape, dt) for (r, _), dt in zip(rows, dtypes)]
        + [jax.ShapeDtypeStruct(p.shape, f32) for p in params],
        compiler_params=_cp(("arbitrary",)),
    )(a, b, *[r for r, _ in rows], *params, *first_cts, *([add] if has_add else []))
    return res[:n_r], res[n_r:]


def _rms(x, g):
    return x * lax.rsqrt(jnp.mean(x * x, axis=-1, keepdims=True) + NORM_EPS) * g


def _fn_rms(x, g):
    return (_rms(x, g),)


def _fn_rwkv_pre(r, k, v, wd, ad, gd, w0, w_up, a0, a_up, g_up, k_k, k_a):
    nn, _, _ = _make_mm(False, False)
    w_log = -_sigmoid(w0 + nn(jnp.tanh(wd), w_up)) * 0.6065306597126334
    a = _sigmoid(a0 + nn(ad, a_up))
    g = nn(_sigmoid(gd), g_up)
    kk = k * k_k
    kk = kk * lax.rsqrt(jnp.maximum(_head_sum(kk * kk), 1e-24))
    k2 = k * (1.0 + (a - 1.0) * k_a)
    return r, k2, v, w_log, -kk, kk * a, g


def _fn_rwkv_post(y, r, k2, v, g, gn_g, gn_b, r_k):
    mean = _head_sum(y) * (1.0 / HD)
    yc = y - mean
    var = _head_sum(yc * yc) * (1.0 / HD)
    yn = yc * lax.rsqrt(var + GN_EPS) * gn_g + gn_b
    bonus = _head_sum(r * k2 * r_k) * v
    return ((yn + bonus) * g,)


def _fn_merge(a_fox, a_rwkv, a_mem, g_fox, g_rwkv, g_mem):
    return (_sigmoid(g_fox) * a_fox + _sigmoid(g_rwkv) * a_rwkv + _sigmoid(g_mem) * a_mem,)


def _fn_post1(y, x, post1_g, pre2_g):
    h1 = x + _rms(y, post1_g)
    return h1, _rms(h1, pre2_g)


def _fn_swiglu(gp, up):
    return (gp * _sigmoid(gp) * up,)


def _fn_final(target, ffn, h1, post2_g):
    err = h1 + _rms(ffn, post2_g) - target
    per_row = jnp.mean(err * err, axis=-1, keepdims=True)
    return (0.5 * jnp.sum(per_row, axis=0, keepdims=True),)


def _shift_down(x):
    row = lax.broadcasted_iota(jnp.int32, x.shape, 0)
    return jnp.where(row == 0, 0.0, pltpu.roll(x, 1, 0))


def _shift_up(x):
    s = x.shape[0]
    row = lax.broadcasted_iota(jnp.int32, x.shape, 0)
    return jnp.where(row == s - 1, 0.0, pltpu.roll(x, s - 1, 0))


def _tokshift_fwd(p, mu, batch, seq):
    w = p.shape[1]
    tc = _tile(w, 384)

    def body(p_ref, mu_ref, o_ref):
        x = p_ref[...]
        o_ref[...] = x + (_shift_down(x) - x) * mu_ref[...]

    return pl.pallas_call(
        body, name="tokshift_fwd", grid=(w // tc, batch),
        in_specs=[pl.BlockSpec((seq, tc), lambda j, b: (b, j)), pl.BlockSpec((1, tc), lambda j, b: (0, j))],
        out_specs=pl.BlockSpec((seq, tc), lambda j, b: (b, j)),
        out_shape=jax.ShapeDtypeStruct(p.shape, f32),
        compiler_params=_cp(("parallel", "arbitrary")),
    )(p, mu)


def _tokshift_bwd(p, mu, dps, batch, seq):
    w = p.shape[1]
    tc = _tile(w, 384)

    def body(p_ref, mu_ref, d_ref, dp_ref, dmu_ref):
        x, mu_v, d = p_ref[...], mu_ref[...], d_ref[...]
        dp_ref[...] = (d * (1.0 - mu_v) + _shift_up(d * mu_v)).astype(dp_ref.dtype)

        @pl.when(pl.program_id(1) == 0)
        def _():
            dmu_ref[...] = jnp.zeros_like(dmu_ref)

        dmu_ref[...] += jnp.sum(d * (_shift_down(x) - x), axis=0, keepdims=True)

    return pl.pallas_call(
        body, name="tokshift_bwd", grid=(w // tc, batch),
        in_specs=[pl.BlockSpec((seq, tc), lambda j, b: (b, j)), pl.BlockSpec((1, tc), lambda j, b: (0, j)),
                  pl.BlockSpec((seq, tc), lambda j, b: (b, j))],
        out_specs=[pl.BlockSpec((seq, tc), lambda j, b: (b, j)), pl.BlockSpec((1, tc), lambda j, b: (0, j))],
        out_shape=[jax.ShapeDtypeStruct(p.shape, bf16), jax.ShapeDtypeStruct(mu.shape, f32)],
        compiler_params=_cp(("parallel", "arbitrary")),
    )(p, mu, dps)


def _cum_block(seq):
    return _tile(seq, 256)


def _fox_gate_fwd(f, bias, batch, seq):
    cb = _cum_block(seq)

    def body(f_ref, b_ref, c_ref):
        row = lax.broadcasted_iota(jnp.int32, (cb, cb), 0)
        col = lax.broadcasted_iota(jnp.int32, (cb, cb), 1)
        tri = (col <= row).astype(f32)
        carry = jnp.zeros((1, 128), f32)
        for i in range(seq // cb):
            z = f_ref[i * cb:(i + 1) * cb, :] + b_ref[...]
            ls = jnp.minimum(z, 0.0) - jnp.log(1.0 + jnp.exp(-jnp.abs(z)))
            c = _dg(tri, ls, (((1,), (0,)), ((), ())), True) + carry
            c_ref[i * cb:(i + 1) * cb, :] = c
            carry = c[cb - 1:cb, :]

    return pl.pallas_call(
        body, name="fox_gate_fwd", grid=(batch,),
        in_specs=[pl.BlockSpec((seq, 128), lambda b: (b, 0)), pl.BlockSpec((1, 128), lambda b: (0, 0))],
        out_specs=pl.BlockSpec((seq, 128), lambda b: (b, 0)),
        out_shape=jax.ShapeDtypeStruct(f.shape, f32),
        compiler_params=_cp(("arbitrary",)),
    )(f, bias)


def _fox_gate_bwd(f, bias, dc_a, dc_b, batch, seq):
    cb = _cum_block(seq)

    def body(f_ref, b_ref, da_ref, db_ref, df_ref, dbias_ref):
        row = lax.broadcasted_iota(jnp.int32, (cb, cb), 0)
        col = lax.broadcasted_iota(jnp.int32, (cb, cb), 1)
        triu = (col >= row).astype(f32)

        @pl.when(pl.program_id(0) == 0)
        def _():
            dbias_ref[...] = jnp.zeros_like(dbias_ref)

        lane = lax.broadcasted_iota(jnp.int32, (1, 128), 1)

        def by_head(blk):
            out = jnp.zeros((cb, 128), f32)
            for p in range(HEADS // 2):
                for e in range(2):
                    out = jnp.where(lane == 2 * p + e, _pick_lane(blk[:, p * 128:(p + 1) * 128], e), out)
            return out

        carry = jnp.zeros((1, 128), f32)
        tot = jnp.zeros((1, 128), f32)
        for i in reversed(range(seq // cb)):
            sl = slice(i * cb, (i + 1) * cb)
            dc = by_head(da_ref[sl, :] + db_ref[sl, :])
            dls = _dg(triu, dc, (((1,), (0,)), ((), ())), True) + carry
            carry = dls[0:1, :]
            df = dls * _sigmoid(-(f_ref[sl, :] + b_ref[...]))
            df_ref[sl, :] = df.astype(df_ref.dtype)
            tot = tot + jnp.sum(df, axis=0, keepdims=True)
        dbias_ref[...] += tot

    return pl.pallas_call(
        body, name="fox_gate_bwd", grid=(batch,),
        in_specs=[pl.BlockSpec((seq, 128), lambda b: (b, 0)), pl.BlockSpec((1, 128), lambda b: (0, 0)),
                  pl.BlockSpec((seq, HW), lambda b: (b, 0)), pl.BlockSpec((seq, HW), lambda b: (b, 0))],
        out_specs=[pl.BlockSpec((seq, 128), lambda b: (b, 0)), pl.BlockSpec((1, 128), lambda b: (0, 0))],
        out_shape=[jax.ShapeDtypeStruct(f.shape, bf16), jax.ShapeDtypeStruct((1, 128), f32)],
        compiler_params=_cp(("arbitrary",)),
    )(f, bias, dc_a, dc_b)


_HBM_SPEC = pl.BlockSpec(memory_space=pltpu.HBM)


def _side_out_shapes(srcs, per_peer):
    return [jax.ShapeDtypeStruct(((N_DEV,) + tuple(s.shape[1:] if per_peer else s.shape)), s.dtype) for s in srcs]


def _side_sems(n):
    if n == 0:
        return []
    return [pltpu.SemaphoreType.DMA((n, N_DEV - 1)), pltpu.SemaphoreType.DMA((n, N_DEV - 1)), pltpu.SemaphoreType.DMA((n,))]


def _peer_copies(src_refs, dst_refs, per_peer, sems):
    send_sems, recv_sems, local_sems = sems
    x, y, c = lax.axis_index("x"), lax.axis_index("y"), lax.axis_index("c")
    me = 4 * x + 2 * y + c

    def remote(src, dst, t, k, to):
        return pltpu.make_async_remote_copy(src_ref=src, dst_ref=dst, send_sem=send_sems.at[t, k - 1],
                                            recv_sem=recv_sems.at[t, k - 1], device_id=to,
                                            device_id_type=pl.DeviceIdType.MESH)

    direct, relays = [], []
    for t, (s, d) in enumerate(zip(src_refs, dst_refs)):
        direct.append((t, 0, pltpu.make_async_copy(s.at[me] if per_peer else s, d.at[me], local_sems.at[t])))
        for k in range(1, N_DEV):
            px = 1 - x if k & 4 else x
            py = 1 - y if k & 2 else y
            pc = 1 - c if k & 1 else c
            if per_peer:
                direct.append((t, k, remote(s.at[4 * px + 2 * py + pc], d.at[me], t, k, (px, py, pc))))
            elif k == 1 or not k & 1:
                direct.append((t, k, remote(s, d.at[me], t, k, (px, py, pc))))
            else:
                origin = d.at[4 * px + 2 * py + c]
                relays.append((t, k - 1, remote(origin, origin, t, k, (x, y, 1 - c))))
    return direct, relays


def _exchange_start(direct):
    for _, _, cp in direct:
        cp.start()


def _exchange_relay(direct, relays):
    landed = {(t, k): cp for t, k, cp in direct}
    for t, j, cp in relays:
        landed[(t, j)].wait_recv()
        cp.start()


def _exchange_finish(direct, relays):
    relayed = {(t, j) for t, j, _ in relays}
    for t, k, cp in direct:
        if k == 0:
            cp.wait()
        else:
            cp.wait_send()
            if (t, k) not in relayed:
                cp.wait_recv()
    for _, _, cp in relays:
        cp.wait()


def _side_exchange(src_refs, dst_refs, per_peer, sems, *grid):
    if not src_refs:
        return
    step, total = 0, 1
    for a, n in enumerate(grid):
        step, total = step * n + pl.program_id(a), total * n

    @pl.when(step == 0)
    def _():
        _exchange_start(_peer_copies(src_refs, dst_refs, per_peer, sems)[0])

    @pl.when(step == (3 * total) // 4)
    def _():
        _exchange_relay(*_peer_copies(src_refs, dst_refs, per_peer, sems))

    @pl.when(step == total - 1)
    def _():
        _exchange_finish(*_peer_copies(src_refs, dst_refs, per_peer, sems))


def _exchange(name, srcs, per_peer):
    n = len(srcs)

    def body(*refs):
        direct, relays = _peer_copies(refs[:n], refs[n:2 * n], per_peer, refs[2 * n:])
        _exchange_start(direct)
        _exchange_relay(direct, relays)
        _exchange_finish(direct, relays)

    return pl.pallas_call(
        body, name=name, in_specs=[_HBM_SPEC] * n, out_specs=[_HBM_SPEC] * n,
        out_shape=_side_out_shapes(srcs, per_peer), scratch_shapes=_side_sems(n),
    )(*srcs)


FOX_T = 512
_NEG = -1e30
_D2 = (((1,), (1,)), ((), ()))
_D1 = (((1,), (0,)), ((), ()))
_D0 = (((0,), (0,)), ((), ()))


def _bdot(a, b, dims):
    return lax.dot_general(a.astype(bf16), b.astype(bf16), dims, preferred_element_type=f32)


def _pick_lane(x, lane):
    idx = lax.broadcasted_iota(jnp.int32, x.shape, 1)
    return jnp.sum(jnp.where(idx == lane, x, 0.0), axis=1, keepdims=True)


def _pick_row(x, row):
    idx = lax.broadcasted_iota(jnp.int32, x.shape, 0)
    return jnp.sum(jnp.where(idx == row, x, 0.0), axis=0, keepdims=True)


def _fox_fwd(qkv, c, c_rows, batch, seq, side=None):
    t = min(FOX_T, seq)
    nq = seq // t
    scale = HD ** -0.5
    srcs, per_peer = side if side is not None else ([], False)
    n_s = len(srcs)

    def body(*refs):
        q_ref, k_ref, v_ref, cq_ref, ck_ref = refs[:5]
        o_ref, lse_ref = refs[5 + n_s:7 + n_s]
        _side_exchange(refs[5:5 + n_s], refs[7 + n_s:7 + 2 * n_s], per_peer, refs[7 + 2 * n_s:], batch, PAIRS, nq)
        pair, i = pl.program_id(1), pl.program_id(2)
        lane = lax.broadcasted_iota(jnp.int32, (1, PAIR_W), 1)
        first = (lane // HD) == 0
        mine = [first, jnp.logical_not(first)]
        q = q_ref[...] * scale
        qs = [jnp.where(mine[e], q, 0.0) for e in range(2)]
        cqs = [_pick_lane(cq_ref[...], 2 * pair + e) for e in range(2)]
        causal = lax.broadcasted_iota(jnp.int32, (t, t), 1) <= lax.broadcasted_iota(jnp.int32, (t, t), 0)

        def block(j, carry, diagonal):
            rows = pl.ds(pl.multiple_of(j * t, t), t)
            kj, vj = k_ref[rows, :], v_ref[rows, :]
            ck_blk = ck_ref[0, :, rows]
            out = []
            for e in range(2):
                m, acc = carry[2 * e:2 * e + 2]
                s = _bdot(qs[e], kj, _D2) + cqs[e] - _pick_row(ck_blk, 2 * pair + e)
                if diagonal:
                    s = jnp.where(causal, s, _NEG)
                m_new = jnp.maximum(m, jnp.max(s, axis=1, keepdims=True))
                p = jnp.exp(s - m_new)
                out += [m_new, jnp.exp(m - m_new) * acc + _bdot(p, jnp.where(mine[e], vj, 1.0), _D1)]
            return tuple(out)

        init = (jnp.full((t, 1), _NEG, f32), jnp.zeros((t, PAIR_W), f32)) * 2
        carry = lax.fori_loop(0, i, lambda j, cr: block(j, cr, False), init)
        m0, a0, m1, a1 = block(i, carry, True)
        l0, l1 = _pick_lane(a0, HD), _pick_lane(a1, 0)
        o_ref[...] = jnp.where(first, a0 / l0, a1 / l1)
        lse_ref[...] = jnp.where(lane == 0, m0 + jnp.log(l0), jnp.where(lane == 1, m1 + jnp.log(l1), 0.0))

    q_spec = pl.BlockSpec((t, PAIR_W), lambda b, p, i: (b * nq + i, p))
    res = pl.pallas_call(
        body, name="fox_attn_fwd", grid=(batch, PAIRS, nq),
        in_specs=[q_spec,
                  pl.BlockSpec((seq, PAIR_W), lambda b, p, i: (b, PAIRS + p)),
                  pl.BlockSpec((seq, PAIR_W), lambda b, p, i: (b, 2 * PAIRS + p)),
                  pl.BlockSpec((t, 128), lambda b, p, i: (b * nq + i, 0)),
                  pl.BlockSpec((1, 8, seq), lambda b, p, i: (b, 0, 0))] + [_HBM_SPEC] * n_s,
        out_specs=[q_spec, q_spec] + [_HBM_SPEC] * n_s,
        out_shape=[jax.ShapeDtypeStruct((batch * seq, HW), f32)] * 2 + _side_out_shapes(srcs, per_peer),
        scratch_shapes=_side_sems(n_s),
        compiler_params=_cp(("arbitrary", "arbitrary", "arbitrary")),
    )(qkv, qkv, qkv, c, c_rows, *srcs)
    return res[0], res[1], list(res[2:])


def _fox_bwd(qkv, c, c_rows, o, lse, do, batch, seq):
    t = min(FOX_T, seq)
    nq = seq // t
    scale = HD ** -0.5

    def body(q_ref, k_ref, v_ref, cq_ref, ck_ref, o_ref, lse_ref, do_ref,
             dq_ref, dk_ref, dv_ref, dcq_ref, dck_ref, acc0, acc1):
        pair, i = pl.program_id(1), pl.program_id(2)
        accs = [acc0, acc1]

        @pl.when(i == 0)
        def _():
            dv_ref[...] = jnp.zeros_like(dv_ref)
            acc0[...] = jnp.zeros_like(acc0)
            acc1[...] = jnp.zeros_like(acc1)

        lane = lax.broadcasted_iota(jnp.int32, (1, PAIR_W), 1)
        first = (lane // HD) == 0
        mine = [first, jnp.logical_not(first)]
        q, d_o, o_i = q_ref[...] * scale, do_ref[...], o_ref[...]
        q0s = [jnp.where(mine[e], q, 0.0) for e in range(2)]
        q1s = [jnp.where(mine[e], q, 1.0) for e in range(2)]
        dos = [jnp.where(mine[e], d_o, 0.0) for e in range(2)]
        deltas = [jnp.sum(dos[e] * o_i, axis=1, keepdims=True) for e in range(2)]
        lses = [_pick_lane(lse_ref[...], e) for e in range(2)]
        cqs = [_pick_lane(cq_ref[...], 2 * pair + e) for e in range(2)]
        causal = lax.broadcasted_iota(jnp.int32, (t, t), 1) <= lax.broadcasted_iota(jnp.int32, (t, t), 0)

        def block(j, dqs, diagonal):
            rows = pl.ds(pl.multiple_of(j * t, t), t)
            kj, vj = k_ref[rows, :], v_ref[rows, :]
            ck_blk = ck_ref[0, :, rows]
            out = []
            for e in range(2):
                s = _bdot(q0s[e], kj, _D2) + cqs[e] - _pick_row(ck_blk, 2 * pair + e)
                if diagonal:
                    s = jnp.where(causal, s, _NEG)
                p = jnp.exp(s - lses[e])
                ds = p * (_bdot(dos[e], vj, _D2) - deltas[e])
                dv_ref[rows, :] += _bdot(p, dos[e], _D0)
                accs[e][rows, :] += _bdot(ds, q1s[e], _D0)
                out.append(dqs[e] + _bdot(ds, jnp.where(mine[e], kj, 1.0), _D1))
            return tuple(out)

        zero = jnp.zeros((t, PAIR_W), f32)
        dqs = lax.fori_loop(0, i, lambda j, cr: block(j, cr, False), (zero, zero))
        dq0, dq1 = block(i, dqs, True)
        dq_ref[...] = jnp.where(first, dq0, dq1) * scale
        dcq_ref[...] = jnp.where(lane == 0, _pick_lane(dq0, HD), jnp.where(lane == 1, _pick_lane(dq1, 0), 0.0))

        @pl.when(i == nq - 1)
        def _():
            a0, a1 = acc0[...], acc1[...]
            dk_ref[...] = jnp.where(first, a0, a1)
            dck_ref[...] = jnp.where(lane == 0, -_pick_lane(a0, HD), jnp.where(lane == 1, -_pick_lane(a1, 0), 0.0))

    blk = lambda col: pl.BlockSpec((t, PAIR_W), lambda b, p, i: (b * nq + i, col * PAIRS + p))
    whole = lambda col: pl.BlockSpec((seq, PAIR_W), lambda b, p, i: (b, col * PAIRS + p))
    t_all = batch * seq
    return pl.pallas_call(
        body, name="fox_attn_bwd", grid=(batch, PAIRS, nq),
        in_specs=[blk(0), whole(1), whole(2),
                  pl.BlockSpec((t, 128), lambda b, p, i: (b * nq + i, 0)),
                  pl.BlockSpec((1, 8, seq), lambda b, p, i: (b, 0, 0)),
                  blk(0), blk(0), blk(0)],
        out_specs=[blk(0), whole(0), whole(0), blk(0), whole(0)],
        out_shape=[jax.ShapeDtypeStruct((t_all, HW), f32)] * 5,
        scratch_shapes=[pltpu.VMEM((seq, PAIR_W), f32), pltpu.VMEM((seq, PAIR_W), f32)],
        compiler_params=_cp(("parallel", "parallel", "arbitrary")),
    )(qkv, qkv, qkv, c, c_rows, o, lse, do)


MEM_TQ = 1024


def _mem_block(q, km, vm):
    nn, nt, _ = _make_mm(False, False)
    logits = nt(q, km) * (MEM_HD ** -0.5)
    m = lax.stop_gradient(jnp.max(logits, axis=-1, keepdims=True))
    e = jnp.exp(logits - m)
    return nn(e / jnp.sum(e, axis=-1, keepdims=True), vm)


def _mem_specs(seq, tq):
    nq = seq // tq
    qs = pl.BlockSpec((tq, MEM_HD), lambda b, h, i: (b * nq + i, h))
    ks = pl.BlockSpec((MEM_LEN, MEM_HD), lambda b, h, i: (b, h))
    vs = pl.BlockSpec((MEM_LEN, MEM_HD), lambda b, h, i: (b, MEM_HEADS + h))
    return nq, qs, ks, vs


def _mem_fwd(q, mem_kv, batch, seq):
    tq = min(MEM_TQ, seq)
    nq, qs, ks, vs = _mem_specs(seq, tq)

    def body(q_ref, k_ref, v_ref, o_ref):
        o_ref[...] = _mem_block(q_ref[...].astype(f32), k_ref[...], v_ref[...]).astype(o_ref.dtype)

    return pl.pallas_call(
        body, name="mem_attn_fwd", grid=(batch, MEM_HEADS, nq),
        in_specs=[qs, ks, vs], out_specs=qs, out_shape=jax.ShapeDtypeStruct(q.shape, bf16),
        compiler_params=_cp(("parallel", "parallel", "arbitrary")),
    )(q, mem_kv, mem_kv)


def _mem_bwd(q, mem_kv, do, batch, seq):
    tq = min(MEM_TQ, seq)
    nq, qs, ks, vs = _mem_specs(seq, tq)

    def body(q_ref, k_ref, v_ref, do_ref, dq_ref, dk_ref, dv_ref):
        _, vjp = jax.vjp(_mem_block, q_ref[...].astype(f32), k_ref[...], v_ref[...])
        dq, dk, dv = vjp(do_ref[...])
        dq_ref[...] = dq.astype(dq_ref.dtype)

        @pl.when(pl.program_id(2) == 0)
        def _():
            dk_ref[...] = jnp.zeros_like(dk_ref)
            dv_ref[...] = jnp.zeros_like(dv_ref)

        dk_ref[...] += dk
        dv_ref[...] += dv

    return pl.pallas_call(
        body, name="mem_attn_bwd", grid=(batch, MEM_HEADS, nq),
        in_specs=[qs, ks, vs, qs], out_specs=[qs, ks, ks],
        out_shape=[jax.ShapeDtypeStruct(q.shape, bf16), jax.ShapeDtypeStruct((batch * MEM_LEN, MEM_W), f32),
                   jax.ShapeDtypeStruct((batch * MEM_LEN, MEM_W), f32)],
        compiler_params=_cp(("parallel", "parallel", "arbitrary")),
    )(q, mem_kv, mem_kv, do)


@jax.custom_vjp
def _halves(x):
    c = x.shape[1] // 2
    return x[:, :c], x[:, c:]


_halves.defvjp(lambda x: ((x[:, :x.shape[1] // 2], x[:, x.shape[1] // 2:]), None),
               lambda _, g: (jnp.concatenate(g, axis=1),))


@jax.custom_vjp
def _lead_halves(x):
    n = x.shape[0] // 2
    return x[:n], x[n:]


_lead_halves.defvjp(lambda x: ((x[:x.shape[0] // 2], x[x.shape[0] // 2:]), None),
                    lambda _, g: (jnp.concatenate(g, axis=0),))


def _scan_chunk(s0, r, wl, k, v, a, b):
    nn, nt, tn = _make_mm(True, False)
    nn_exact, _, _ = _make_mm(True, True)
    _, nt_exact, _ = _make_mm(True, "split")
    hp, c, lanes = r.shape
    row = lax.broadcasted_iota(jnp.int32, (c, c), 0)
    col = lax.broadcasted_iota(jnp.int32, (c, c), 1)
    first = (lax.broadcasted_iota(jnp.int32, (1, 1, lanes), 2) // HD) == 0
    tri = jnp.broadcast_to((col <= row).astype(f32)[None], (hp, c, c))
    lg = nn_exact(tri, wl)
    lg_end = lg[:, c - 1:c, :]
    grow, shrink, to_end = jnp.exp(lg), jnp.exp(-lg), jnp.exp(lg_end - lg)
    rt, kt, bt, at = r * grow, k * shrink, b * shrink, a * jnp.exp(lg - wl)
    strict, incl = (col < row)[None], (col <= row)[None]
    twice = lambda t: jnp.concatenate([t, t], axis=0)
    queries = jnp.concatenate([at, rt], axis=1)
    per_head = jnp.concatenate([jnp.where(first, queries, 0.0), jnp.where(first, 0.0, queries)], axis=0)
    (ab, rb), (ak, rk) = _halves(nt_exact(per_head, twice(bt))), _halves(nt_exact(per_head, twice(kt)))
    l_ab = jnp.where(strict, ab, 0.0)
    a_ak = jnp.where(strict, ak, 0.0)
    a_rb = jnp.where(incl, rb, 0.0)
    a_rk = jnp.where(incl, rk, 0.0)
    inv = (col == row).astype(f32)[None] + l_ab
    power, n = l_ab, 1
    while 2 * n < c:
        power = nn(power, power)
        inv = inv + nn(inv, power)
        n *= 2

    def apply(m, t):
        lo, hi = _lead_halves(nn(m, twice(t)))
        return jnp.where(first, lo, hi)

    sa = apply(inv, nt(at, s0) + apply(a_ak, v))
    y = nt(rt, s0) + apply(a_rk, v) + apply(a_rb, sa)
    same_head = ((lax.broadcasted_iota(jnp.int32, (lanes, lanes), 0) // HD)
                 == (lax.broadcasted_iota(jnp.int32, (lanes, lanes), 1) // HD))[None]
    s1 = s0 * jnp.exp(lg_end) + jnp.where(same_head, tn(v, k * to_end) + tn(sa, b * to_end), 0.0)
    return y, s1


PAIRS = HEADS // 2
PAIR_W = 2 * HD
SCAN_ARGS = (0, 3, 1, 2, 4, 5)


def _pair_stack(ref, off):
    return jnp.stack([ref[b, :, off + p * PAIR_W:off + (p + 1) * PAIR_W]
                      for b in range(ref.shape[0]) for p in range(PAIRS)])


def _pair_store(ref, off, val, add_ref=None):
    for b in range(ref.shape[0]):
        for p in range(PAIRS):
            sl = slice(off + p * PAIR_W, off + (p + 1) * PAIR_W)
            v = val[b * PAIRS + p]
            ref[b, :, sl] = v if add_ref is None else v + add_ref[b, :, sl]


def _scan_fwd(main6, batch, seq, side=None):
    c = min(SCAN_CHUNK, seq)
    nc = seq // c
    hp = batch * PAIRS
    srcs, per_peer = side if side is not None else ([], False)
    n_s = len(srcs)

    def body(*refs):
        z_ref, y_ref, s_ref, st = refs[0], refs[1 + n_s], refs[2 + n_s], refs[3 + 2 * n_s]
        _side_exchange(refs[1:1 + n_s], refs[3 + n_s:3 + 2 * n_s], per_peer, refs[4 + 2 * n_s:], nc)

        @pl.when(pl.program_id(0) == 0)
        def _():
            st[...] = jnp.zeros_like(st)

        s0 = st[...]
        s_ref[0] = s0
        y, s1 = _scan_chunk(s0, *[_pair_stack(z_ref, comp * HW) for comp in SCAN_ARGS])
        _pair_store(y_ref, 0, y)
        st[...] = s1

    res = pl.pallas_call(
        body, name="rwkv_scan_fwd", grid=(nc,),
        in_specs=[pl.BlockSpec((batch, c, 6 * HW), lambda i: (0, i, 0))] + [_HBM_SPEC] * n_s,
        out_specs=[pl.BlockSpec((batch, c, HW), lambda i: (0, i, 0)),
                   pl.BlockSpec((1, hp, PAIR_W, PAIR_W), lambda i: (i, 0, 0, 0))] + [_HBM_SPEC] * n_s,
        out_shape=[jax.ShapeDtypeStruct((batch, seq, HW), f32), jax.ShapeDtypeStruct((nc, hp, PAIR_W, PAIR_W), f32)]
        + _side_out_shapes(srcs, per_peer),
        scratch_shapes=[pltpu.VMEM((hp, PAIR_W, PAIR_W), f32)] + _side_sems(n_s),
        compiler_params=_cp(("arbitrary",)),
    )(main6.reshape(batch, seq, 6 * HW), *srcs)
    return res[0].reshape(batch * seq, HW), res[1], list(res[2:])


def _scan_bwd(main6, states, dy, extra, batch, seq, side=None):
    c = min(SCAN_CHUNK, seq)
    nc = seq // c
    hp = batch * PAIRS
    srcs, per_peer = side if side is not None else ([], False)
    n_s = len(srcs)

    def body(*refs):
        z_ref, s_ref, dy_ref, ex_ref = refs[:4]
        dz_ref, dst = refs[4 + n_s], refs[5 + 2 * n_s]
        _side_exchange(refs[4:4 + n_s], refs[5 + n_s:5 + 2 * n_s], per_peer, refs[6 + 2 * n_s:], nc)

        @pl.when(pl.program_id(0) == 0)
        def _():
            dst[...] = jnp.zeros_like(dst)

        _, vjp = jax.vjp(_scan_chunk, s_ref[0], *[_pair_stack(z_ref, comp * HW) for comp in SCAN_ARGS])
        g = vjp((_pair_stack(dy_ref, 0), dst[...]))
        dst[...] = g[0]
        for arg, comp in enumerate(SCAN_ARGS):
            _pair_store(dz_ref, comp * HW, g[1 + arg], ex_ref if comp < 3 else None)

    back = lambda i: (0, nc - 1 - i, 0)
    wide = pl.BlockSpec((batch, c, 6 * HW), back)
    res = pl.pallas_call(
        body, name="rwkv_scan_bwd", grid=(nc,),
        in_specs=[wide, pl.BlockSpec((1, hp, PAIR_W, PAIR_W), lambda i: (nc - 1 - i, 0, 0, 0)),
                  pl.BlockSpec((batch, c, HW), back), pl.BlockSpec((batch, c, 3 * HW), back)] + [_HBM_SPEC] * n_s,
        out_specs=[wide] + [_HBM_SPEC] * n_s,
        out_shape=[jax.ShapeDtypeStruct((batch, seq, 6 * HW), f32)] + _side_out_shapes(srcs, per_peer),
        scratch_shapes=[pltpu.VMEM((hp, PAIR_W, PAIR_W), f32)] + _side_sems(n_s),
        compiler_params=_cp(("arbitrary",)),
    )(main6.reshape(batch, seq, 6 * HW), states, dy.reshape(batch, seq, HW), extra.reshape(batch, seq, 3 * HW), *srcs)
    return res[0].reshape(batch * seq, 6 * HW), list(res[1:])


def _pad_cols(x, width):
    return jnp.pad(x, ((0, 0), (0, width - x.shape[1])))


def _split_w_in(wt):
    z = lambda rows: jnp.zeros((rows, wt.shape[1]), wt.dtype)
    w_r = jnp.concatenate([wt[1544:3080], wt[3080:3144], z(64), wt[3144:3208], z(64), wt[3208:3336]], axis=0)
    return wt[:1536], jnp.concatenate([wt[1536:1544], z(120)], axis=0), w_r, wt[3336:3848], wt[3848:]


def _merge_w_in(g_qkv, g_f, g_r, g_mq, g_g):
    return jnp.concatenate([g_qkv, g_f[:8], g_r[:1536], g_r[1536:1600], g_r[1664:1728], g_r[1792:], g_mq, g_g], axis=0)


def _pad_lora(v):
    z64 = jnp.zeros((1, 64), v.dtype)
    return jnp.concatenate([v[:, :1536], v[:, 1536:1600], z64, v[:, 1600:1664], z64, v[:, 1664:]], axis=1)


def _unpad_lora(v):
    return jnp.concatenate([v[:, :1536], v[:, 1536:1600], v[:, 1664:1728], v[:, 1792:]], axis=1)


def _local_step(x, mem, target, w, p, late=None, early=None, last=None):
    batch, seq, _ = x.shape
    t = batch * seq
    x2, tg2, mem2 = x.reshape(t, D), target.reshape(t, D), mem.reshape(batch * MEM_LEN, D)
    w_qkv, w_f, w_r, w_mq, w_g3 = _split_w_in(w["w_in"])
    mu = _pad_lora(p["rwkv_mu"])
    bias = _pad_cols(p["fox_f_bias"], 128)
    r_k = p["rwkv_r_k"].reshape(1, HW)
    post_params = [p["rwkv_gn_g"], p["rwkv_gn_b"], r_k]
    rw_widths = [HW, HW, HW, LORA_PAD, LORA_PAD, LORA_PAD]
    six = [HW] * 6

    p_g, u = _matmul("proj_gate", _lazy(_fn_rms, [(x2, [D])], D, params=[p["pre1_g"]]), w_g3, "nt", out_dtype=bf16)
    p_qkv = _matmul("proj_qkv", u, w_qkv, "nt", out_dtype=bf16)
    p_f = _matmul("proj_f", u, w_f, "nt")
    p_r = _matmul("proj_rwkv", u, w_r, "nt")
    p_mq = _matmul("proj_memq", u, w_mq, "nt", out_dtype=bf16)

    c = _fox_gate_fwd(p_f, bias, batch, seq)
    c_rows = c[:, :HEADS].reshape(batch, seq, HEADS).transpose(0, 2, 1)
    fox_o, lse, gathered = _fox_fwd(p_qkv, c, c_rows, batch, seq, side=(late[0], False) if late else None)
    if late:
        w = {**w, **late[2](gathered, 0)}
    fox_out = fox_o.astype(bf16)

    w_up = jnp.pad(w["rwkv_w_up"].astype(f32), ((0, LORA_PAD - 64), (0, 0)))
    a_up = jnp.pad(w["rwkv_a_up"].astype(f32), ((0, LORA_PAD - 64), (0, 0)))
    pre_params = [p["rwkv_w0"], w_up, p["rwkv_a0"], a_up, w["rwkv_g_up"].astype(f32), p["rwkv_k_k"], p["rwkv_k_a"]]
    ps = _tokshift_fwd(p_r, mu, batch, seq)
    main6, g_rw = _rows_fwd("rwkv_pre", _fn_rwkv_pre, [], [(ps, rw_widths)], pre_params, [six, [HW]], tm=256)
    y_rw, states, gathered = _scan_fwd(main6, batch, seq, side=(late[1], False) if late else None)
    if late:
        w = {**w, **late[2](gathered, 1)}
    post_consts = []
    post_rows = [(y_rw, [HW]), (main6, [HW, HW, HW]), (g_rw, [HW])]
    fn_post = _fn_rwkv_post

    (rwkv_out,) = _rows_fwd("rwkv_post", fn_post, post_consts, post_rows, post_params, [[HW]], dtypes=[bf16], tm=256)

    mem_kv, memn = _matmul("proj_memkv", _lazy(_fn_rms, [(mem2, [D])], D, params=[p["mem_norm_g"]]), w["w_mem_kv"], "nn")
    mem_out = _mem_fwd(p_mq, mem_kv, batch, seq)

    a_fox = _matmul("out_fox", fox_out, w["w_fox_out"], "nn", out_dtype=bf16)
    a_rwkv = _matmul("out_rwkv", rwkv_out, w["w_rwkv_out"], "nn", out_dtype=bf16)
    a_mem = _matmul("out_mem", mem_out, w["w_mem_out"], "nn", out_dtype=bf16)
    merge_rows = [(a_fox, [D]), (a_rwkv, [D]), (a_mem, [D]), (p_g, [D, D, D])]
    yy, merged = _matmul("out_o", _lazy(_fn_merge, merge_rows, D), w["w_o"], "nn")
    post1_rows = [(yy, [D]), (x2, [D])]
    post1_params = [p["post1_g"], p["pre2_g"]]
    h1, u2 = _rows_fwd("post1", _fn_post1, [], post1_rows, post1_params, [[D], [D]], dtypes=[f32, bf16])
    gp = _matmul("ffn_gate", u2, w["w_ffn_gate"], "nt", out_dtype=bf16)
    up = _matmul("ffn_up", u2, w["w_ffn_up"], "nt", out_dtype=bf16)
    ffn, hmid = _matmul("ffn_down", _lazy(_fn_swiglu, [(gp, [D_FF]), (up, [D_FF])], D_FF), w["w_ffn_down"], "nn")
    final_rows = [(ffn, [D]), (h1, [D])]

    gw, gp_ = {}, {}
    (d_ffn, d_h1), (gp_["post2_g"], loss) = _rows_bwd("final", _fn_final, [(tg2, [D])], final_rows, [p["post2_g"]], [], [],
                                                      n_sums=1, dtypes=[bf16, f32])
    gw["w_ffn_down"] = _matmul("ffn_down_dw", hmid, d_ffn, "tn", out_dtype=bf16)
    (d_gp, d_up), _ = _matmul_then_vjp("ffn_down_dx", d_ffn, w["w_ffn_down"], "nt", _fn_swiglu,
                                       [(gp, [D_FF]), (up, [D_FF])], [bf16, bf16])
    gw["w_ffn_gate"] = _matmul("ffn_gate_dw", d_gp, u2, "tn", out_dtype=bf16)
    gw["w_ffn_up"] = _matmul("ffn_up_dw", d_up, u2, "tn", out_dtype=bf16)
    d_u2_gate = _matmul("ffn_gate_dx", d_gp, w["w_ffn_gate"], "nn")
    (d_yy, d_x_res), (gp_["post1_g"], gp_["pre2_g"]) = _matmul_then_vjp(
        "ffn_up_dx", d_up, w["w_ffn_up"], "nn", _fn_post1, post1_rows, [bf16, f32], params=post1_params,
        first_cts=[d_h1], add=d_u2_gate)
    gw["w_o"] = _matmul("out_o_dw", merged, d_yy, "tn", out_dtype=bf16)
    (d_a_fox, d_a_rwkv, d_a_mem, d_p_g), _ = _matmul_then_vjp("out_o_dx", d_yy, w["w_o"], "nt", _fn_merge, merge_rows,
                                                             [bf16] * 4)
    d_fox_out = _matmul("out_fox_dx", d_a_fox, w["w_fox_out"], "nt")
    gw["w_fox_out"] = _matmul("out_fox_dw", fox_out, d_a_fox, "tn", out_dtype=bf16)
    d_rwkv_out = _matmul("out_rwkv_dx", d_a_rwkv, w["w_rwkv_out"], "nt")
    gw["w_rwkv_out"] = _matmul("out_rwkv_dw", rwkv_out, d_a_rwkv, "tn", out_dtype=bf16)
    d_mem_out = _matmul("out_mem_dx", d_a_mem, w["w_mem_out"], "nt")
    gw["w_mem_out"] = _matmul("out_mem_dw", mem_out, d_a_mem, "tn", out_dtype=bf16)

    d_p_mq, d_km, d_vm = _mem_bwd(p_mq, mem_kv, d_mem_out, batch, seq)
    d_mem_kv = jnp.concatenate([d_km, d_vm], axis=1).astype(bf16)
    gw["w_mem_kv"] = _matmul("proj_memkv_dw", memn, d_mem_kv, "tn", out_dtype=bf16)
    d_memn = _matmul("proj_memkv_dx", d_mem_kv, w["w_mem_kv"], "nt")
    _, (gp_["mem_norm_g"],) = _rows_bwd("rms_mem_bwd", _fn_rms, [], [(mem2, [D])], [p["mem_norm_g"]], [[D]], [d_memn])

    d_q, d_k, d_v, d_cq, d_ck = _fox_bwd(p_qkv, c, c_rows, fox_o, lse, d_fox_out, batch, seq)
    d_p_qkv = jnp.concatenate([d_q, d_k, d_v], axis=1).astype(bf16)
    d_p_f, d_bias = _fox_gate_bwd(p_f, bias, d_cq, d_ck, batch, seq)
    gp_["fox_f_bias"] = d_bias[:, :HEADS]

    (d_y_rw, d_main6_post, d_g_rw), (gp_["rwkv_gn_g"], gp_["rwkv_gn_b"], d_rk) = _rows_bwd(
        "rwkv_post_bwd", fn_post, post_consts, post_rows, post_params, [[HW]], [d_rwkv_out], tm=256)
    gp_["rwkv_r_k"] = d_rk.reshape(1, HEADS, HD)
    d_main6, early_got = _scan_bwd(main6, states, d_y_rw, d_main6_post, batch, seq,
                                   side=(early(gw), True) if early else None)

    def fn_pre_sum(*args):
        return _fn_rwkv_pre(*args)

    (d_ps,), d_pre = _rows_bwd("rwkv_pre_bwd", fn_pre_sum, [], [(ps, rw_widths)], pre_params, [six, [HW]],
                               [d_main6, d_g_rw], tm=256)
    gp_["rwkv_w0"], d_w_up, gp_["rwkv_a0"], d_a_up, gw["rwkv_g_up"], gp_["rwkv_k_k"], gp_["rwkv_k_a"] = d_pre
    gw["rwkv_w_up"], gw["rwkv_a_up"] = d_w_up[:64], d_a_up[:64]
    d_p_r, d_mu = _tokshift_bwd(p_r, mu, d_ps, batch, seq)
    gp_["rwkv_mu"] = _unpad_lora(d_mu)

    gw["w_in"] = _merge_w_in(_matmul("proj_qkv_dw", d_p_qkv, u, "tn", out_dtype=bf16), _matmul("proj_f_dw", d_p_f, u, "tn", out_dtype=bf16),
                             _matmul("proj_rwkv_dw", d_p_r, u, "tn", out_dtype=bf16), _matmul("proj_memq_dw", d_p_mq, u, "tn", out_dtype=bf16),
                             _matmul("proj_gate_dw", d_p_g, u, "tn", out_dtype=bf16))
    d_x, gp_["pre1_g"], last_got = _input_cotangent(
        "proj_dx", [d_p_qkv, d_p_f, d_p_r, d_p_mq, d_p_g], [w_qkv, w_f, w_r, w_mq, w_g3], x2, p["pre1_g"], d_x_res,
        side=(last(gw), True) if last else None)
    return loss, d_x.reshape(x.shape), gw, gp_, early_got, last_got


def _adamw(name, recv, row_off, w, m, v):
    _, rows, cols = w.shape
    row_tiles = [t for t in range(16, min(rows, 128) + 1, 16) if rows % t == 0 and row_off % t == 0]
    if row_tiles:
        tr, tc = max(row_tiles), cols
        first, grid = row_off // tr, (rows // tr,)
        at = lambda i: (0, first + i, 0)
        mine = lambda i: (0, i, 0)
    else:
        assert row_off == 0 and recv.shape[1] == rows
        tr, tc = rows, 128
        grid = (cols // tc,)
        at = mine = lambda i: (0, 0, i)

    def body(g_ref, w_ref, m_ref, v_ref, go_ref, d_ref, mo_ref, vo_ref):
        g = g_ref[0].astype(f32)
        for s in range(1, N_DEV):
            g = g + g_ref[s].astype(f32)
        m_new = ADAM_B1 * m_ref[0] + (1.0 - ADAM_B1) * g
        v_new = ADAM_B2 * v_ref[0] + (1.0 - ADAM_B2) * (g * g)
        m_hat = m_new / (1.0 - ADAM_B1 ** ADAM_STEP)
        v_hat = v_new / (1.0 - ADAM_B2 ** ADAM_STEP)
        go_ref[0] = g
        d_ref[0] = -ADAM_LR * (m_hat / (jnp.sqrt(v_hat) + ADAM_EPS) + ADAM_WD * w_ref[0])
        mo_ref[0] = m_new
        vo_ref[0] = v_new

    spec = pl.BlockSpec((1, tr, tc), mine)
    return pl.pallas_call(
        body, name=name, grid=grid,
        in_specs=[pl.BlockSpec((N_DEV, tr, tc), at), spec, spec, spec],
        out_specs=[spec] * 4, out_shape=[jax.ShapeDtypeStruct(w.shape, f32)] * 4,
        compiler_params=_cp(("parallel",)),
    )(recv, w, m, v)


GROUPS = (
    ("in", ("w_in",), 0),
    ("memkv", ("w_mem_kv",), 0),
    ("ffn_gu", ("w_ffn_gate", "w_ffn_up"), 0),
    ("down_o", ("w_ffn_down", "w_o"), 0),
    ("outs", ("w_fox_out", "w_rwkv_out", "w_mem_out"), 0),
    ("lora", ("rwkv_w_up", "rwkv_a_up", "rwkv_g_up"), 0),
)
FIRST_GROUPS = ("in", "memkv")
LATE_GROUPS = (("down_o", "outs", "lora"), ("ffn_gu",))
EARLY_GRAD_GROUPS = ("memkv", "ffn_gu", "down_o", "outs")
LAST_GRAD_GROUPS = ("in", "lora")
SHARD_AXIS = {n: a for n, _, a in SHARDED}
SMALL_ROWS = 16
LOSS_LANES = 128


def _group_local(shards, members, join):
    parts = [shards[n].reshape(shards[n].shape[-2:]) for n in members]
    return parts[0] if len(parts) == 1 else jnp.concatenate(parts, axis=join)


def _group_split(arr, members, join, lead=False):
    out, off = {}, 0
    for n in members:
        shape = dict((k, s) for k, s, _ in SHARDED)[n]
        size = _block_shape(shape, SHARD_AXIS[n])[join]
        idx = [slice(None)] * arr.ndim
        idx[arr.ndim - 2 + join] = slice(off, off + size)
        out[n] = arr[tuple(idx)]
        off += size
    return out


def _full_from_blocks(blocks, axis):
    if axis == 0:
        return blocks.reshape(-1, blocks.shape[2])
    return blocks.transpose(1, 0, 2).reshape(blocks.shape[1], -1)


def _blocks_from_full(full, axis):
    if axis == 0:
        return full.reshape(N_DEV, -1, full.shape[1])
    return full.reshape(full.shape[0], N_DEV, -1).transpose(1, 0, 2)


def _assemble(gathered, names):
    out = {}
    for arr, g in zip(gathered, names):
        _, members, join = [grp for grp in GROUPS if grp[0] == g][0]
        for n, blk in _group_split(arr, members, join, lead=True).items():
            out[n] = _full_from_blocks(blk, SHARD_AXIS[n])
    return out


def _grad_blocks(gw, names):
    out = []
    for g in names:
        _, members, join = [grp for grp in GROUPS if grp[0] == g][0]
        parts = [_blocks_from_full(gw[n].astype(bf16), SHARD_AXIS[n]) for n in members]
        out.append(parts[0] if len(parts) == 1 else jnp.concatenate(parts, axis=1 + join))
    return out


def _small_pack(d):
    flat = jnp.concatenate([d[n].reshape(-1) for n, _ in REPLICATED])
    return jnp.pad(flat, (0, SMALL_ROWS * LANES - REPL_ELEMS)).reshape(SMALL_ROWS, LANES)


def _small_unpack(packed):
    out, flat, off = {}, packed.reshape(-1), 0
    for n, shape in REPLICATED:
        k = _rows_of((LANES,) + shape)
        out[n] = flat[off:off + k].reshape(shape)
        off += k
    return out


def kernel(x, mem, pre1_g, post1_g, pre2_g, post2_g, mem_norm_g, w_in, fox_f_bias, rwkv_mu, rwkv_w0, rwkv_w_up, rwkv_a0, rwkv_a_up, rwkv_g_up, rwkv_k_k, rwkv_k_a, rwkv_r_k, rwkv_gn_g, rwkv_gn_b, w_mem_kv, w_fox_out, w_rwkv_out, w_mem_out, w_o, w_ffn_gate, w_ffn_up, w_ffn_down, loss_target, m_pre1_g, m_post1_g, m_pre2_g, m_post2_g, m_mem_norm_g, m_w_in, m_fox_f_bias, m_rwkv_mu, m_rwkv_w0, m_rwkv_w_up, m_rwkv_a0, m_rwkv_a_up, m_rwkv_g_up, m_rwkv_k_k, m_rwkv_k_a, m_rwkv_r_k, m_rwkv_gn_g, m_rwkv_gn_b, m_w_mem_kv, m_w_fox_out, m_w_rwkv_out, m_w_mem_out, m_w_o, m_w_ffn_gate, m_w_ffn_up, m_w_ffn_down, v_pre1_g, v_post1_g, v_pre2_g, v_post2_g, v_mem_norm_g, v_w_in, v_fox_f_bias, v_rwkv_mu, v_rwkv_w0, v_rwkv_w_up, v_rwkv_a0, v_rwkv_a_up, v_rwkv_g_up, v_rwkv_k_k, v_rwkv_k_a, v_rwkv_r_k, v_rwkv_gn_g, v_rwkv_gn_b, v_w_mem_kv, v_w_fox_out, v_w_rwkv_out, v_w_mem_out, v_w_o, v_w_ffn_gate, v_w_ffn_up, v_w_ffn_down):
    args = dict(locals())
    turn = lambda n, a: jnp.swapaxes(a, 1, 2) if n in TRANSPOSED else a
    wts = {n: turn(n, args[n]) for n in WEIGHT_ORDER}
    ms = {n: turn(n, args["m_" + n]) for n in WEIGHT_ORDER}
    vs = {n: turn(n, args["v_" + n]) for n in WEIGHT_ORDER}

    groups = {g: (members, join) for g, members, join in GROUPS}
    w_bf16 = {n: wts[n].astype(bf16) for n, _, _ in SHARDED}

    def send(g):
        return _group_local(w_bf16, *groups[g])

    first = _exchange("gather_first", [send(g) for g in FIRST_GROUPS], per_peer=False)
    full = _assemble(first, FIRST_GROUPS)
    small_in = {n: (wts[n] if n == "rwkv_r_k" else wts[n].reshape(wts[n].shape[-2:])) for n, _ in REPLICATED}
    late = ([send(g) for g in LATE_GROUPS[0]], [send(g) for g in LATE_GROUPS[1]],
            lambda got, which: _assemble(got, LATE_GROUPS[which]))
    loss_part, grad_x, gw, gp, early_got, last_got = _local_step(
        x, mem, loss_target, full, small_in, late=late, early=lambda g: _grad_blocks(g, EARLY_GRAD_GROUPS),
        last=lambda g: _grad_blocks(g, LAST_GRAD_GROUPS))
    small_got, loss_got = _exchange("exchange_small", [_small_pack(gp).astype(bf16), jnp.broadcast_to(loss_part, (8, LOSS_LANES))],
                                    per_peer=False)
    received = dict(zip(EARLY_GRAD_GROUPS + LAST_GRAD_GROUPS, list(early_got) + list(last_got)))

    outs = [{}, {}, {}, {}]
    for g, members, _ in GROUPS:
        off = 0
        for n in members:
            for o, arr in zip(outs, _adamw("adamw_" + n, received[g], off, wts[n], ms[n], vs[n])):
                o[n] = arr
            off += wts[n].shape[1]
    res = _adamw("adamw_small", small_got, 0, *[_small_pack(d)[None] for d in (wts, ms, vs)])
    for o, arr in zip(outs, res):
        o.update(_small_unpack(arr))
    loss = jnp.sum(loss_got[:, 0, 0])
    return (loss, grad_x, *[turn(n, o[n].reshape(wts[n].shape)) for o in outs for n in WEIGHT_ORDER])
```

```python
import functools

import jax
import jax.numpy as jnp
from jax import lax
from jax.experimental import pallas as pl
from jax.experimental.pallas import tpu as pltpu

f32 = jnp.float32
bf16 = jnp.bfloat16
_HI = lax.Precision.HIGHEST

D = 1024
HEADS = 8
HD = 64
HW = HEADS * HD
MEM_HEADS = 4
MEM_HD = 128
MEM_W = 512
MEM_LEN = 256
D_FF = 2816
LORA_PAD = 128
NORM_EPS = 1e-6
GN_EPS = 64e-5
SCAN_CHUNK = 64
N_DEV = 8
LANES = 1024
VMEM_LIMIT = 56 * 1024 * 1024

ADAM_LR = 0.001
ADAM_B1 = 0.9
ADAM_B2 = 0.999
ADAM_EPS = 1e-08
ADAM_WD = 0.01
ADAM_STEP = 10

TRANSPOSED = ("w_in", "w_ffn_gate", "w_ffn_up")
SHARDED = (
    ("w_in", (6920, 1024), 0),
    ("w_ffn_gate", (2816, 1024), 0),
    ("w_ffn_up", (2816, 1024), 0),
    ("w_ffn_down", (2816, 1024), 0),
    ("w_mem_kv", (1024, 1024), 0),
    ("w_o", (1024, 1024), 0),
    ("w_fox_out", (512, 1024), 1),
    ("w_rwkv_out", (512, 1024), 1),
    ("w_mem_out", (512, 1024), 1),
    ("rwkv_w_up", (64, 512), 1),
    ("rwkv_a_up", (64, 512), 1),
    ("rwkv_g_up", (128, 512), 1),
)
REPLICATED = (
    ("pre1_g", (1, 1024)), ("post1_g", (1, 1024)), ("pre2_g", (1, 1024)), ("post2_g", (1, 1024)),
    ("mem_norm_g", (1, 1024)), ("fox_f_bias", (1, 8)), ("rwkv_mu", (1, 1792)), ("rwkv_w0", (1, 512)),
    ("rwkv_a0", (1, 512)), ("rwkv_k_k", (1, 512)), ("rwkv_k_a", (1, 512)), ("rwkv_r_k", (1, 8, 64)),
    ("rwkv_gn_g", (1, 512)), ("rwkv_gn_b", (1, 512)),
)
WEIGHT_ORDER = ('pre1_g', 'post1_g', 'pre2_g', 'post2_g', 'mem_norm_g', 'w_in', 'fox_f_bias', 'rwkv_mu',
                'rwkv_w0', 'rwkv_w_up', 'rwkv_a0', 'rwkv_a_up', 'rwkv_g_up', 'rwkv_k_k', 'rwkv_k_a',
                'rwkv_r_k', 'rwkv_gn_g', 'rwkv_gn_b', 'w_mem_kv', 'w_fox_out', 'w_rwkv_out', 'w_mem_out',
                'w_o', 'w_ffn_gate', 'w_ffn_up', 'w_ffn_down')


def _block_shape(shape, axis):
    return tuple(s // N_DEV if i == axis else s for i, s in enumerate(shape))


def _rows_of(shape):
    n = 1
    for s in shape:
        n *= s
    return n // LANES


REPL_ELEMS = sum(_rows_of((LANES,) + s) for _, s in REPLICATED)


def _cp(sem=None):
    return pltpu.CompilerParams(dimension_semantics=sem, vmem_limit_bytes=VMEM_LIMIT)


def _tile(dim, cap):
    best = None
    for t in range(128, min(dim, cap) + 1, 128):
        if dim % t == 0:
            best = t
    return best if best is not None else dim


def _two_terms(x):
    hi = x.astype(bf16)
    return hi, (x - hi.astype(f32)).astype(bf16)


def _dg(a, b, dims, exact):
    if exact == "split":
        (a_hi, a_lo), (b_hi, b_lo) = _two_terms(a), _two_terms(b)
        dot = functools.partial(lax.dot_general, dimension_numbers=dims, preferred_element_type=f32)
        return dot(a_hi, b_hi) + (dot(a_hi, b_lo) + dot(a_lo, b_hi))
    if exact:
        return lax.dot_general(a, b, dims, precision=_HI, preferred_element_type=f32)
    return lax.dot_general(a.astype(bf16), b.astype(bf16), dims, preferred_element_type=f32)


def _make_mm(batched, exact):
    o = 1 if batched else 0
    bd = ((0,), (0,)) if batched else ((), ())
    d_nn = (((1 + o,), (o,)), bd)
    d_nt = (((1 + o,), (1 + o,)), bd)
    d_tn = (((o,), (o,)), bd)

    @jax.custom_vjp
    def nn(a, b):
        return _dg(a, b, d_nn, exact)

    @jax.custom_vjp
    def nt(a, b):
        return _dg(a, b, d_nt, exact)

    @jax.custom_vjp
    def tn(a, b):
        return _dg(a, b, d_tn, exact)

    nn.defvjp(lambda a, b: (_dg(a, b, d_nn, exact), (a, b)),
              lambda res, g: (_dg(g, res[1], d_nt, exact), _dg(res[0], g, d_tn, exact)))
    nt.defvjp(lambda a, b: (_dg(a, b, d_nt, exact), (a, b)),
              lambda res, g: (_dg(g, res[1], d_nn, exact), _dg(g, res[0], d_tn, exact)))
    tn.defvjp(lambda a, b: (_dg(a, b, d_tn, exact), (a, b)),
              lambda res, g: (_dg(res[1], g, d_nt, exact), _dg(res[0], g, d_nn, exact)))
    return nn, nt, tn


def _sigmoid(x):
    return 1.0 / (1.0 + jnp.exp(-x))


def _head_sum_raw(x):
    width = 2 * HD
    i = lax.broadcasted_iota(jnp.int32, (width, width), 0) // HD
    j = lax.broadcasted_iota(jnp.int32, (width, width), 1) // HD
    m = (i == j).astype(bf16)
    dims = (((1,), (0,)), ((), ()))
    out = []
    for p in range(x.shape[1] // width):
        xp = x[:, p * width:(p + 1) * width]
        hi = xp.astype(bf16)
        lo = (xp - hi.astype(f32)).astype(bf16)
        out.append(lax.dot_general(hi, m, dims, preferred_element_type=f32)
                   + lax.dot_general(lo, m, dims, preferred_element_type=f32))
    return jnp.concatenate(out, axis=1)


@jax.custom_vjp
def _head_sum(x):
    return _head_sum_raw(x)


_head_sum.defvjp(lambda x: (_head_sum_raw(x), None), lambda _, g: (_head_sum_raw(g),))


WEIGHT_TILE_BYTES = 13 * 512 * 1024
ACC_TILE_BYTES = 8 * 1024 * 1024


def _lazy(fn, rows, width, params=()):
    return (fn, rows, width, list(params))


def _matmul(name, a, b, mode, add=None, out_dtype=f32):
    has_add = add is not None
    if isinstance(a, tuple):
        a_fn, a_rows, a_width, a_params = a
        a_arrays = [r for r, _ in a_rows]
        a_shape = (a_arrays[0].shape[0], a_width)
    else:
        a_fn, a_rows, a_params, a_arrays, a_shape = None, None, [], [a], a.shape
    n_r = len(a_arrays)
    n_a = n_r + len(a_params)

    def load_a(refs):
        if a_fn is None:
            return refs[0][...].astype(bf16)
        pieces = []
        for r, (_, widths) in zip(refs[:n_r], a_rows):
            pieces += _pieces(r, widths)
        return a_fn(*pieces, *[p[...] for p in refs[n_r:]])[0].astype(bf16)

    if mode == "tn":
        assert a_fn is None
        (k, m), (_, n) = a_shape, b.shape
        tn = _tile(n, max(128, ACC_TILE_BYTES // (4 * m)))
        tk = _tile(k, 2048)
        nk = k // tk

        def body(*refs):
            b_ref, o_ref, acc = refs[n_a:]

            @pl.when(pl.program_id(1) == 0)
            def _():
                acc[...] = jnp.zeros_like(acc)

            acc[...] += lax.dot_general(load_a(refs[:n_a]), b_ref[...].astype(bf16),
                                        (((0,), (0,)), ((), ())), preferred_element_type=f32)

            @pl.when(pl.program_id(1) == nk - 1)
            def _():
                o_ref[...] = acc[...].astype(o_ref.dtype)

        return pl.pallas_call(
            body, name=name, grid=(n // tn, nk),
            in_specs=[pl.BlockSpec((tk, r.shape[1]), lambda j, kk: (kk, 0)) for r in a_arrays]
            + [pl.BlockSpec((tk, tn), lambda j, kk: (kk, j))],
            out_specs=pl.BlockSpec((m, tn), lambda j, kk: (0, j)), out_shape=jax.ShapeDtypeStruct((m, n), out_dtype),
            scratch_shapes=[pltpu.VMEM((m, tn), f32)],
            compiler_params=_cp(("parallel", "arbitrary")),
        )(*a_arrays, b)

    (m, k) = a_shape
    n = b.shape[1] if mode == "nn" else b.shape[0]
    tm = _tile(m, 1024 if a_fn is None else 512)
    tn = _tile(n, max(128, WEIGHT_TILE_BYTES // (2 * k)))
    dims = (((1,), (0,)), ((), ())) if mode == "nn" else (((1,), (1,)), ((), ()))
    b_spec = pl.BlockSpec((k, tn), lambda j, i: (0, j)) if mode == "nn" else pl.BlockSpec((tn, k), lambda j, i: (j, 0))
    o_spec = pl.BlockSpec((tm, tn), lambda j, i: (i, j))

    keep = a_fn is not None
    assert not keep or tn == n

    def body(*refs):
        b_ref = refs[n_a]
        a_val = load_a(refs[:n_a])
        r = lax.dot_general(a_val, b_ref[...].astype(bf16), dims, preferred_element_type=f32)
        if has_add:
            r = r + refs[n_a + 1][...]
        if keep:
            refs[-2][...] = r.astype(refs[-2].dtype)
            refs[-1][...] = a_val
        else:
            refs[-1][...] = r.astype(refs[-1].dtype)

    res = pl.pallas_call(
        body, name=name, grid=(n // tn, m // tm),
        in_specs=[pl.BlockSpec((tm, r.shape[1]), lambda j, i: (i, 0)) for r in a_arrays]
        + [pl.BlockSpec(p.shape, lambda j, i: (0, 0)) for p in a_params] + [b_spec] + ([o_spec] if has_add else []),
        out_specs=[o_spec] + ([pl.BlockSpec((tm, k), lambda j, i: (i, 0))] if keep else []),
        out_shape=[jax.ShapeDtypeStruct((m, n), out_dtype)] + ([jax.ShapeDtypeStruct((m, k), bf16)] if keep else []),
        compiler_params=_cp(("parallel", "arbitrary")),
    )(*a_arrays, *a_params, b, *([add] if has_add else []))
    return tuple(res) if keep else res[0]


def _input_cotangent(name, a_list, b_list, x, gain, add, side=None):
    m = a_list[0].shape[0]
    tm = _tile(m, 256)
    n_g = len(a_list)
    srcs, per_peer = side if side is not None else ([], False)
    n_s = len(srcs)

    def body(*refs):
        x_ref, g_ref, add_ref = refs[2 * n_g:2 * n_g + 3]
        src_refs = refs[2 * n_g + 3:2 * n_g + 3 + n_s]
        dx_ref, dg_ref = refs[2 * n_g + 3 + n_s:2 * n_g + 5 + n_s]
        _side_exchange(src_refs, refs[2 * n_g + 5 + n_s:2 * n_g + 5 + 2 * n_s], per_peer, refs[2 * n_g + 5 + 2 * n_s:], m // tm)
        d_u = None
        for g in range(n_g):
            r = lax.dot_general(refs[g][...].astype(bf16), refs[n_g + g][...].astype(bf16), (((1,), (0,)), ((), ())),
                                preferred_element_type=f32)
            d_u = r if d_u is None else d_u + r
        _, vjp = jax.vjp(_rms, x_ref[...], g_ref[...])
        d_x, d_gain = vjp(d_u)
        dx_ref[...] = d_x + add_ref[...]

        @pl.when(pl.program_id(0) == 0)
        def _():
            dg_ref[...] = jnp.zeros_like(dg_ref)

        dg_ref[...] += d_gain

    rows = pl.BlockSpec((tm, x.shape[1]), lambda i: (i, 0))
    whole = lambda b: pl.BlockSpec(b.shape, lambda i: (0, 0))
    res = pl.pallas_call(
        body, name=name, grid=(m // tm,),
        in_specs=[pl.BlockSpec((tm, a.shape[1]), lambda i: (i, 0)) for a in a_list] + [whole(b) for b in b_list]
        + [rows, whole(gain), rows] + [_HBM_SPEC] * n_s,
        out_specs=[rows, whole(gain)] + [_HBM_SPEC] * n_s,
        out_shape=[jax.ShapeDtypeStruct(x.shape, f32), jax.ShapeDtypeStruct(gain.shape, f32)] + _side_out_shapes(srcs, per_peer),
        scratch_shapes=_side_sems(n_s),
        compiler_params=_cp(("arbitrary",)),
    )(*a_list, *b_list, x, gain, add, *srcs)
    return res[0], res[1], list(res[2:])


def _pieces(ref, widths):
    out, off = [], 0
    for w in widths:
        out.append(ref[:, off:off + w].astype(f32))
        off += w
    return out


def _store_pieces(ref, widths, vals, add_ref=None):
    off = 0
    for w, v in zip(widths, vals):
        ref[:, off:off + w] = (v if add_ref is None else v + add_ref[:, off:off + w]).astype(ref.dtype)
        off += w


def _rows_fwd(name, fn, consts, rows, params, outs, n_sums=0, tm=512, dtypes=None):
    t = (consts + rows)[0][0].shape[0]
    tm = min(tm, t)
    ins = consts + rows
    n_in, n_p, n_o = len(ins), len(params), len(outs)
    dtypes = dtypes or [f32] * n_o

    def body(*refs):
        in_refs, p_refs = refs[:n_in], refs[n_in:n_in + n_p]
        o_refs, s_refs = refs[n_in + n_p:n_in + n_p + n_o], refs[n_in + n_p + n_o:]
        vals = []
        for r, (_, widths) in zip(in_refs, ins):
            vals += _pieces(r, widths)
        res = fn(*vals, *[p[...] for p in p_refs])
        pos = 0
        for r, widths in zip(o_refs, outs):
            _store_pieces(r, widths, res[pos:pos + len(widths)])
            pos += len(widths)

        @pl.when(pl.program_id(0) == 0)
        def _():
            for s in s_refs:
                s[...] = jnp.zeros_like(s)

        for s, v in zip(s_refs, res[pos:]):
            s[...] += v

    row_spec = lambda w: pl.BlockSpec((tm, w), lambda i: (i, 0))
    full = lambda p: pl.BlockSpec(p.shape, lambda i: (0,) * p.ndim)
    return pl.pallas_call(
        body, name=name, grid=(t // tm,),
        in_specs=[row_spec(sum(w)) for _, w in ins] + [full(p) for p in params],
        out_specs=[row_spec(sum(w)) for w in outs] + [pl.BlockSpec((1, 1), lambda i: (0, 0))] * n_sums,
        out_shape=[jax.ShapeDtypeStruct((t, sum(w)), dt) for w, dt in zip(outs, dtypes)] + [jax.ShapeDtypeStruct((1, 1), f32)] * n_sums,
        compiler_params=_cp(("arbitrary",)),
    )(*[a for a, _ in ins], *params)


def _rows_bwd(name, fn, consts, rows, params, outs, cts, n_sums=0, add=None, tm=512, dtypes=None):
    t = (consts + rows)[0][0].shape[0]
    tm = min(tm, t)
    n_c, n_r, n_p, n_o = len(consts), len(rows), len(params), len(outs)
    has_add = add is not None
    dtypes = dtypes or [f32] * n_r

    def body(*refs):
        pos = 0
        c_refs = refs[pos:pos + n_c]; pos += n_c
        r_refs = refs[pos:pos + n_r]; pos += n_r
        p_refs = refs[pos:pos + n_p]; pos += n_p
        ct_refs = refs[pos:pos + n_o]; pos += n_o
        add_ref = refs[pos] if has_add else None
        pos += 1 if has_add else 0
        dr_refs = refs[pos:pos + n_r]; pos += n_r
        dp_refs = refs[pos:pos + n_p]; pos += n_p
        s_refs = refs[pos:pos + n_sums]
        cvals, rvals = [], []
        for r, (_, widths) in zip(c_refs, consts):
            cvals += _pieces(r, widths)
        for r, (_, widths) in zip(r_refs, rows):
            rvals += _pieces(r, widths)
        pvals = [p[...] for p in p_refs]
        ctv = []
        for r, widths in zip(ct_refs, outs):
            ctv += _pieces(r, widths)
        ctv += [jnp.ones((1, 1), f32)] * n_sums
        primal, vjp = jax.vjp(lambda *rp: tuple(fn(*cvals, *rp)), *rvals, *pvals)
        g = vjp(tuple(ctv))
        pos = 0
        for idx, (r, (_, widths)) in enumerate(zip(dr_refs, rows)):
            _store_pieces(r, widths, g[pos:pos + len(widths)], add_ref if idx == 0 else None)
            pos += len(widths)

        @pl.when(pl.program_id(0) == 0)
        def _():
            for acc in list(dp_refs) + list(s_refs):
                acc[...] = jnp.zeros_like(acc)

        for dp, v in zip(dp_refs, g[pos:]):
            dp[...] += v
        for s, v in zip(s_refs, primal[len(primal) - n_sums:]):
            s[...] += v

    row_spec = lambda w: pl.BlockSpec((tm, w), lambda i: (i, 0))
    full = lambda p: pl.BlockSpec(p.shape, lambda i: (0,) * p.ndim)
    args = [a for a, _ in consts + rows] + list(params) + list(cts) + ([add] if has_add else [])
    res = pl.pallas_call(
        body, name=name, grid=(t // tm,),
        in_specs=[row_spec(sum(w)) for _, w in consts + rows] + [full(p) for p in params]
        + [row_spec(sum(w)) for w in outs] + ([row_spec(add.shape[1])] if has_add else []),
        out_specs=[row_spec(sum(w)) for _, w in rows] + [full(p) for p in params]
        + [pl.BlockSpec((1, 1), lambda i: (0, 0))] * n_sums,
        out_shape=[jax.ShapeDtypeStruct((t, sum(w)), dt) for (_, w), dt in zip(rows, dtypes)]
        + [jax.ShapeDtypeStruct(p.shape, f32) for p in params] + [jax.ShapeDtypeStruct((1, 1), f32)] * n_sums,
        compiler_params=_cp(("arbitrary",)),
    )(*args)
    return res[:n_r], res[n_r:n_r + n_p] + res[n_r + n_p:]


def _matmul_then_vjp(name, a, b, mode, fn, rows, dtypes, params=(), first_cts=(), add=None, tm=256):
    m, k = a.shape
    tm = min(tm, m)
    dims = (((1,), (0,)), ((), ())) if mode == "nn" else (((1,), (1,)), ((), ()))
    n_r, n_p, n_c = len(rows), len(params), len(first_cts)
    has_add = add is not None

    def body(*refs):
        a_ref, b_ref = refs[:2]
        pos = 2
        r_refs = refs[pos:pos + n_r]; pos += n_r
        p_refs = refs[pos:pos + n_p]; pos += n_p
        c_refs = refs[pos:pos + n_c]; pos += n_c
        add_ref = refs[pos] if has_add else None
        pos += 1 if has_add else 0
        dr_refs = refs[pos:pos + n_r]; pos += n_r
        dp_refs = refs[pos:pos + n_p]
        ct = lax.dot_general(a_ref[...].astype(bf16), b_ref[...].astype(bf16), dims, preferred_element_type=f32)
        if has_add:
            ct = ct + add_ref[...]
        rvals = []
        for r, (_, widths) in zip(r_refs, rows):
            rvals += _pieces(r, widths)
        _, vjp = jax.vjp(lambda *rp: tuple(fn(*rp)), *rvals, *[p[...] for p in p_refs])
        g = vjp(tuple(c[...].astype(f32) for c in c_refs) + (ct,))
        pos = 0
        for r, (_, widths) in zip(dr_refs, rows):
            _store_pieces(r, widths, g[pos:pos + len(widths)])
            pos += len(widths)

        @pl.when(pl.program_id(0) == 0)
        def _():
            for dp in dp_refs:
                dp[...] = jnp.zeros_like(dp)

        for dp, v in zip(dp_refs, g[pos:]):
            dp[...] += v

    row_spec = lambda w: pl.BlockSpec((tm, w), lambda i: (i, 0))
    whole = lambda p: pl.BlockSpec(p.shape, lambda i: (0, 0))
    res = pl.pallas_call(
        body, name=name, grid=(m // tm,),
        in_specs=[row_spec(k), whole(b)] + [row_spec(r.shape[1]) for r, _ in rows] + [whole(p) for p in params]
        + [row_spec(c.shape[1]) for c in first_cts] + ([row_spec(add.shape[1])] if has_add else []),
        out_specs=[row_spec(r.shape[1]) for r, _ in rows] + [whole(p) for p in params],
        out_shape=[jax.ShapeDtypeStruct(r.shape, dt) for (r, _), dt in zip(rows, dtypes)]
        + [jax.ShapeDtypeStruct(p.shape, f32) for p in params],
        compiler_params=_cp(("arbitrary",)),
    )(a, b, *[r for r, _ in rows], *params, *first_cts, *([add] if has_add else []))
    return res[:n_r], res[n_r:]


def _rms(x, g):
    return x * lax.rsqrt(jnp.mean(x * x, axis=-1, keepdims=True) + NORM_EPS) * g


def _fn_rms(x, g):
    return (_rms(x, g),)


def _fn_rwkv_pre(r, k, v, wd, ad, gd, w0, w_up, a0, a_up, g_up, k_k, k_a):
    nn, _, _ = _make_mm(False, False)
    w_log = -_sigmoid(w0 + nn(jnp.tanh(wd), w_up)) * 0.6065306597126334
    a = _sigmoid(a0 + nn(ad, a_up))
    g = nn(_sigmoid(gd), g_up)
    kk = k * k_k
    kk = kk * lax.rsqrt(jnp.maximum(_head_sum(kk * kk), 1e-24))
    k2 = k * (1.0 + (a - 1.0) * k_a)
    return r, k2, v, w_log, -kk, kk * a, g


def _fn_rwkv_post(y, r, k2, v, g, gn_g, gn_b, r_k):
    mean = _head_sum(y) * (1.0 / HD)
    yc = y - mean
    var = _head_sum(yc * yc) * (1.0 / HD)
    yn = yc * lax.rsqrt(var + GN_EPS) * gn_g + gn_b
    bonus = _head_sum(r * k2 * r_k) * v
    return ((yn + bonus) * g,)


def _fn_merge(a_fox, a_rwkv, a_mem, g_fox, g_rwkv, g_mem):
    return (_sigmoid(g_fox) * a_fox + _sigmoid(g_rwkv) * a_rwkv + _sigmoid(g_mem) * a_mem,)


def _fn_post1(y, x, post1_g, pre2_g):
    h1 = x + _rms(y, post1_g)
    return h1, _rms(h1, pre2_g)


def _fn_swiglu(gp, up):
    return (gp * _sigmoid(gp) * up,)


def _fn_final(target, ffn, h1, post2_g):
    err = h1 + _rms(ffn, post2_g) - target
    per_row = jnp.mean(err * err, axis=-1, keepdims=True)
    return (0.5 * jnp.sum(per_row, axis=0, keepdims=True),)


def _shift_down(x):
    row = lax.broadcasted_iota(jnp.int32, x.shape, 0)
    return jnp.where(row == 0, 0.0, pltpu.roll(x, 1, 0))


def _shift_up(x):
    s = x.shape[0]
    row = lax.broadcasted_iota(jnp.int32, x.shape, 0)
    return jnp.where(row == s - 1, 0.0, pltpu.roll(x, s - 1, 0))


def _tokshift_fwd(p, mu, batch, seq):
    w = p.shape[1]
    tc = _tile(w, 384)

    def body(p_ref, mu_ref, o_ref):
        x = p_ref[...]
        o_ref[...] = x + (_shift_down(x) - x) * mu_ref[...]

    return pl.pallas_call(
        body, name="tokshift_fwd", grid=(w // tc, batch),
        in_specs=[pl.BlockSpec((seq, tc), lambda j, b: (b, j)), pl.BlockSpec((1, tc), lambda j, b: (0, j))],
        out_specs=pl.BlockSpec((seq, tc), lambda j, b: (b, j)),
        out_shape=jax.ShapeDtypeStruct(p.shape, f32),
        compiler_params=_cp(("parallel", "arbitrary")),
    )(p, mu)


def _tokshift_bwd(p, mu, dps, batch, seq):
    w = p.shape[1]
    tc = _tile(w, 384)

    def body(p_ref, mu_ref, d_ref, dp_ref, dmu_ref):
        x, mu_v, d = p_ref[...], mu_ref[...], d_ref[...]
        dp_ref[...] = (d * (1.0 - mu_v) + _shift_up(d * mu_v)).astype(dp_ref.dtype)

        @pl.when(pl.program_id(1) == 0)
        def _():
            dmu_ref[...] = jnp.zeros_like(dmu_ref)

        dmu_ref[...] += jnp.sum(d * (_shift_down(x) - x), axis=0, keepdims=True)

    return pl.pallas_call(
        body, name="tokshift_bwd", grid=(w // tc, batch),
        in_specs=[pl.BlockSpec((seq, tc), lambda j, b: (b, j)), pl.BlockSpec((1, tc), lambda j, b: (0, j)),
                  pl.BlockSpec((seq, tc), lambda j, b: (b, j))],
        out_specs=[pl.BlockSpec((seq, tc), lambda j, b: (b, j)), pl.BlockSpec((1, tc), lambda j, b: (0, j))],
        out_shape=[jax.ShapeDtypeStruct(p.shape, bf16), jax.ShapeDtypeStruct(mu.shape, f32)],
        compiler_params=_cp(("parallel", "arbitrary")),
    )(p, mu, dps)


def _cum_block(seq):
    return _tile(seq, 256)


def _fox_gate_fwd(f, bias, batch, seq):
    cb = _cum_block(seq)

    def body(f_ref, b_ref, c_ref):
        row = lax.broadcasted_iota(jnp.int32, (cb, cb), 0)
        col = lax.broadcasted_iota(jnp.int32, (cb, cb), 1)
        tri = (col <= row).astype(f32)
        carry = jnp.zeros((1, 128), f32)
        for i in range(seq // cb):
            z = f_ref[i * cb:(i + 1) * cb, :] + b_ref[...]
            ls = jnp.minimum(z, 0.0) - jnp.log(1.0 + jnp.exp(-jnp.abs(z)))
            c = _dg(tri, ls, (((1,), (0,)), ((), ())), True) + carry
            c_ref[i * cb:(i + 1) * cb, :] = c
            carry = c[cb - 1:cb, :]

    return pl.pallas_call(
        body, name="fox_gate_fwd", grid=(batch,),
        in_specs=[pl.BlockSpec((seq, 128), lambda b: (b, 0)), pl.BlockSpec((1, 128), lambda b: (0, 0))],
        out_specs=pl.BlockSpec((seq, 128), lambda b: (b, 0)),
        out_shape=jax.ShapeDtypeStruct(f.shape, f32),
        compiler_params=_cp(("arbitrary",)),
    )(f, bias)


def _fox_gate_bwd(f, bias, dc_a, dc_b, batch, seq):
    cb = _cum_block(seq)

    def body(f_ref, b_ref, da_ref, db_ref, df_ref, dbias_ref):
        row = lax.broadcasted_iota(jnp.int32, (cb, cb), 0)
        col = lax.broadcasted_iota(jnp.int32, (cb, cb), 1)
        triu = (col >= row).astype(f32)

        @pl.when(pl.program_id(0) == 0)
        def _():
            dbias_ref[...] = jnp.zeros_like(dbias_ref)

        lane = lax.broadcasted_iota(jnp.int32, (1, 128), 1)

        def by_head(blk):
            out = jnp.zeros((cb, 128), f32)
            for p in range(HEADS // 2):
                for e in range(2):
                    out = jnp.where(lane == 2 * p + e, _pick_lane(blk[:, p * 128:(p + 1) * 128], e), out)
            return out

        carry = jnp.zeros((1, 128), f32)
        tot = jnp.zeros((1, 128), f32)
        for i in reversed(range(seq // cb)):
            sl = slice(i * cb, (i + 1) * cb)
            dc = by_head(da_ref[sl, :] + db_ref[sl, :])
            dls = _dg(triu, dc, (((1,), (0,)), ((), ())), True) + carry
            carry = dls[0:1, :]
            df = dls * _sigmoid(-(f_ref[sl, :] + b_ref[...]))
            df_ref[sl, :] = df.astype(df_ref.dtype)
            tot = tot + jnp.sum(df, axis=0, keepdims=True)
        dbias_ref[...] += tot

    return pl.pallas_call(
        body, name="fox_gate_bwd", grid=(batch,),
        in_specs=[pl.BlockSpec((seq, 128), lambda b: (b, 0)), pl.BlockSpec((1, 128), lambda b: (0, 0)),
                  pl.BlockSpec((seq, HW), lambda b: (b, 0)), pl.BlockSpec((seq, HW), lambda b: (b, 0))],
        out_specs=[pl.BlockSpec((seq, 128), lambda b: (b, 0)), pl.BlockSpec((1, 128), lambda b: (0, 0))],
        out_shape=[jax.ShapeDtypeStruct(f.shape, bf16), jax.ShapeDtypeStruct((1, 128), f32)],
        compiler_params=_cp(("arbitrary",)),
    )(f, bias, dc_a, dc_b)


_HBM_SPEC = pl.BlockSpec(memory_space=pltpu.HBM)


def _side_out_shapes(srcs, per_peer):
    return [jax.ShapeDtypeStruct(((N_DEV,) + tuple(s.shape[1:] if per_peer else s.shape)), s.dtype) for s in srcs]


def _side_sems(n):
    if n == 0:
        return []
    return [pltpu.SemaphoreType.DMA((n, N_DEV - 1)), pltpu.SemaphoreType.DMA((n, N_DEV - 1)), pltpu.SemaphoreType.DMA((n,))]


def _peer_copies(src_refs, dst_refs, per_peer, sems):
    send_sems, recv_sems, local_sems = sems
    x, y, c = lax.axis_index("x"), lax.axis_index("y"), lax.axis_index("c")
    me = 4 * x + 2 * y + c

    def remote(src, dst, t, k, to):
        return pltpu.make_async_remote_copy(src_ref=src, dst_ref=dst, send_sem=send_sems.at[t, k - 1],
                                            recv_sem=recv_sems.at[t, k - 1], device_id=to,
                                            device_id_type=pl.DeviceIdType.MESH)

    direct, relays = [], []
    for t, (s, d) in enumerate(zip(src_refs, dst_refs)):
        direct.append((t, 0, pltpu.make_async_copy(s.at[me] if per_peer else s, d.at[me], local_sems.at[t])))
        for k in range(1, N_DEV):
            px = 1 - x if k & 4 else x
            py = 1 - y if k & 2 else y
            pc = 1 - c if k & 1 else c
            if per_peer:
                direct.append((t, k, remote(s.at[4 * px + 2 * py + pc], d.at[me], t, k, (px, py, pc))))
            elif k == 1 or not k & 1:
                direct.append((t, k, remote(s, d.at[me], t, k, (px, py, pc))))
            else:
                origin = d.at[4 * px + 2 * py + c]
                relays.append((t, k - 1, remote(origin, origin, t, k, (x, y, 1 - c))))
    return direct, relays


def _exchange_start(direct):
    for _, _, cp in direct:
        cp.start()


def _exchange_relay(direct, relays):
    landed = {(t, k): cp for t, k, cp in direct}
    for t, j, cp in relays:
        landed[(t, j)].wait_recv()
        cp.start()


def _exchange_finish(direct, relays):
    relayed = {(t, j) for t, j, _ in relays}
    for t, k, cp in direct:
        if k == 0:
            cp.wait()
        else:
            cp.wait_send()
            if (t, k) not in relayed:
                cp.wait_recv()
    for _, _, cp in relays:
        cp.wait()


def _side_exchange(src_refs, dst_refs, per_peer, sems, *grid):
    if not src_refs:
        return
    step, total = 0, 1
    for a, n in enumerate(grid):
        step, total = step * n + pl.program_id(a), total * n

    @pl.when(step == 0)
    def _():
        _exchange_start(_peer_copies(src_refs, dst_refs, per_peer, sems)[0])

    @pl.when(step == (3 * total) // 4)
    def _():
        _exchange_relay(*_peer_copies(src_refs, dst_refs, per_peer, sems))

    @pl.when(step == total - 1)
    def _():
        _exchange_finish(*_peer_copies(src_refs, dst_refs, per_peer, sems))


def _exchange(name, srcs, per_peer):
    n = len(srcs)

    def body(*refs):
        direct, relays = _peer_copies(refs[:n], refs[n:2 * n], per_peer, refs[2 * n:])
        _exchange_start(direct)
        _exchange_relay(direct, relays)
        _exchange_finish(direct, relays)

    return pl.pallas_call(
        body, name=name, in_specs=[_HBM_SPEC] * n, out_specs=[_HBM_SPEC] * n,
        out_shape=_side_out_shapes(srcs, per_peer), scratch_shapes=_side_sems(n),
    )(*srcs)


FOX_T = 512
_NEG = -1e30
_D2 = (((1,), (1,)), ((), ()))
_D1 = (((1,), (0,)), ((), ()))
_D0 = (((0,), (0,)), ((), ()))


def _bdot(a, b, dims):
    return lax.dot_general(a.astype(bf16), b.astype(bf16), dims, preferred_element_type=f32)


def _pick_lane(x, lane):
    idx = lax.broadcasted_iota(jnp.int32, x.shape, 1)
    return jnp.sum(jnp.where(idx == lane, x, 0.0), axis=1, keepdims=True)


def _pick_row(x, row):
    idx = lax.broadcasted_iota(jnp.int32, x.shape, 0)
    return jnp.sum(jnp.where(idx == row, x, 0.0), axis=0, keepdims=True)


def _fox_fwd(qkv, c, c_rows, batch, seq, side=None):
    t = min(FOX_T, seq)
    nq = seq // t
    scale = HD ** -0.5
    srcs, per_peer = side if side is not None else ([], False)
    n_s = len(srcs)

    def body(*refs):
        q_ref, k_ref, v_ref, cq_ref, ck_ref = refs[:5]
        o_ref, lse_ref = refs[5 + n_s:7 + n_s]
        _side_exchange(refs[5:5 + n_s], refs[7 + n_s:7 + 2 * n_s], per_peer, refs[7 + 2 * n_s:], batch, PAIRS, nq)
        pair, i = pl.program_id(1), pl.program_id(2)
        lane = lax.broadcasted_iota(jnp.int32, (1, PAIR_W), 1)
        first = (lane // HD) == 0
        mine = [first, jnp.logical_not(first)]
        q = q_ref[...] * scale
        qs = [jnp.where(mine[e], q, 0.0) for e in range(2)]
        cqs = [_pick_lane(cq_ref[...], 2 * pair + e) for e in range(2)]
        causal = lax.broadcasted_iota(jnp.int32, (t, t), 1) <= lax.broadcasted_iota(jnp.int32, (t, t), 0)

        def block(j, carry, diagonal):
            rows = pl.ds(pl.multiple_of(j * t, t), t)
            kj, vj = k_ref[rows, :], v_ref[rows, :]
            ck_blk = ck_ref[0, :, rows]
            out = []
            for e in range(2):
                m, acc = carry[2 * e:2 * e + 2]
                s = _bdot(qs[e], kj, _D2) + cqs[e] - _pick_row(ck_blk, 2 * pair + e)
                if diagonal:
                    s = jnp.where(causal, s, _NEG)
                m_new = jnp.maximum(m, jnp.max(s, axis=1, keepdims=True))
                p = jnp.exp(s - m_new)
                out += [m_new, jnp.exp(m - m_new) * acc + _bdot(p, jnp.where(mine[e], vj, 1.0), _D1)]
            return tuple(out)

        init = (jnp.full((t, 1), _NEG, f32), jnp.zeros((t, PAIR_W), f32)) * 2
        carry = lax.fori_loop(0, i, lambda j, cr: block(j, cr, False), init)
        m0, a0, m1, a1 = block(i, carry, True)
        l0, l1 = _pick_lane(a0, HD), _pick_lane(a1, 0)
        o_ref[...] = jnp.where(first, a0 / l0, a1 / l1)
        lse_ref[...] = jnp.where(lane == 0, m0 + jnp.log(l0), jnp.where(lane == 1, m1 + jnp.log(l1), 0.0))

    q_spec = pl.BlockSpec((t, PAIR_W), lambda b, p, i: (b * nq + i, p))
    res = pl.pallas_call(
        body, name="fox_attn_fwd", grid=(batch, PAIRS, nq),
        in_specs=[q_spec,
                  pl.BlockSpec((seq, PAIR_W), lambda b, p, i: (b, PAIRS + p)),
                  pl.BlockSpec((seq, PAIR_W), lambda b, p, i: (b, 2 * PAIRS + p)),
                  pl.BlockSpec((t, 128), lambda b, p, i: (b * nq + i, 0)),
                  pl.BlockSpec((1, 8, seq), lambda b, p, i: (b, 0, 0))] + [_HBM_SPEC] * n_s,
        out_specs=[q_spec, q_spec] + [_HBM_SPEC] * n_s,
        out_shape=[jax.ShapeDtypeStruct((batch * seq, HW), f32)] * 2 + _side_out_shapes(srcs, per_peer),
        scratch_shapes=_side_sems(n_s),
        compiler_params=_cp(("arbitrary", "arbitrary", "arbitrary")),
    )(qkv, qkv, qkv, c, c_rows, *srcs)
    return res[0], res[1], list(res[2:])


def _fox_bwd(qkv, c, c_rows, o, lse, do, batch, seq):
    t = min(FOX_T, seq)
    nq = seq // t
    scale = HD ** -0.5

    def body(q_ref, k_ref, v_ref, cq_ref, ck_ref, o_ref, lse_ref, do_ref,
             dq_ref, dk_ref, dv_ref, dcq_ref, dck_ref, acc0, acc1):
        pair, i = pl.program_id(1), pl.program_id(2)
        accs = [acc0, acc1]

        @pl.when(i == 0)
        def _():
            dv_ref[...] = jnp.zeros_like(dv_ref)
            acc0[...] = jnp.zeros_like(acc0)
            acc1[...] = jnp.zeros_like(acc1)

        lane = lax.broadcasted_iota(jnp.int32, (1, PAIR_W), 1)
        first = (lane // HD) == 0
        mine = [first, jnp.logical_not(first)]
        q, d_o, o_i = q_ref[...] * scale, do_ref[...], o_ref[...]
        q0s = [jnp.where(mine[e], q, 0.0) for e in range(2)]
        q1s = [jnp.where(mine[e], q, 1.0) for e in range(2)]
        dos = [jnp.where(mine[e], d_o, 0.0) for e in range(2)]
        deltas = [jnp.sum(dos[e] * o_i, axis=1, keepdims=True) for e in range(2)]
        lses = [_pick_lane(lse_ref[...], e) for e in range(2)]
        cqs = [_pick_lane(cq_ref[...], 2 * pair + e) for e in range(2)]
        causal = lax.broadcasted_iota(jnp.int32, (t, t), 1) <= lax.broadcasted_iota(jnp.int32, (t, t), 0)

        def block(j, dqs, diagonal):
            rows = pl.ds(pl.multiple_of(j * t, t), t)
            kj, vj = k_ref[rows, :], v_ref[rows, :]
            ck_blk = ck_ref[0, :, rows]
            out = []
            for e in range(2):
                s = _bdot(q0s[e], kj, _D2) + cqs[e] - _pick_row(ck_blk, 2 * pair + e)
                if diagonal:
                    s = jnp.where(causal, s, _NEG)
                p = jnp.exp(s - lses[e])
                ds = p * (_bdot(dos[e], vj, _D2) - deltas[e])
                dv_ref[rows, :] += _bdot(p, dos[e], _D0)
                accs[e][rows, :] += _bdot(ds, q1s[e], _D0)
                out.append(dqs[e] + _bdot(ds, jnp.where(mine[e], kj, 1.0), _D1))
            return tuple(out)

        zero = jnp.zeros((t, PAIR_W), f32)
        dqs = lax.fori_loop(0, i, lambda j, cr: block(j, cr, False), (zero, zero))
        dq0, dq1 = block(i, dqs, True)
        dq_ref[...] = jnp.where(first, dq0, dq1) * scale
        dcq_ref[...] = jnp.where(lane == 0, _pick_lane(dq0, HD), jnp.where(lane == 1, _pick_lane(dq1, 0), 0.0))

        @pl.when(i == nq - 1)
        def _():
            a0, a1 = acc0[...], acc1[...]
            dk_ref[...] = jnp.where(first, a0, a1)
            dck_ref[...] = jnp.where(lane == 0, -_pick_lane(a0, HD), jnp.where(lane == 1, -_pick_lane(a1, 0), 0.0))

    blk = lambda col: pl.BlockSpec((t, PAIR_W), lambda b, p, i: (b * nq + i, col * PAIRS + p))
    whole = lambda col: pl.BlockSpec((seq, PAIR_W), lambda b, p, i: (b, col * PAIRS + p))
    t_all = batch * seq
    return pl.pallas_call(
        body, name="fox_attn_bwd", grid=(batch, PAIRS, nq),
        in_specs=[blk(0), whole(1), whole(2),
                  pl.BlockSpec((t, 128), lambda b, p, i: (b * nq + i, 0)),
                  pl.BlockSpec((1, 8, seq), lambda b, p, i: (b, 0, 0)),
                  blk(0), blk(0), blk(0)],
        out_specs=[blk(0), whole(0), whole(0), blk(0), whole(0)],
        out_shape=[jax.ShapeDtypeStruct((t_all, HW), f32)] * 5,
        scratch_shapes=[pltpu.VMEM((seq, PAIR_W), f32), pltpu.VMEM((seq, PAIR_W), f32)],
        compiler_params=_cp(("parallel", "parallel", "arbitrary")),
    )(qkv, qkv, qkv, c, c_rows, o, lse, do)


MEM_TQ = 1024


def _mem_block(q, km, vm):
    nn, nt, _ = _make_mm(False, False)
    logits = nt(q, km) * (MEM_HD ** -0.5)
    m = lax.stop_gradient(jnp.max(logits, axis=-1, keepdims=True))
    e = jnp.exp(logits - m)
    return nn(e / jnp.sum(e, axis=-1, keepdims=True), vm)


def _mem_specs(seq, tq):
    nq = seq // tq
    qs = pl.BlockSpec((tq, MEM_HD), lambda b, h, i: (b * nq + i, h))
    ks = pl.BlockSpec((MEM_LEN, MEM_HD), lambda b, h, i: (b, h))
    vs = pl.BlockSpec((MEM_LEN, MEM_HD), lambda b, h, i: (b, MEM_HEADS + h))
    return nq, qs, ks, vs


def _mem_fwd(q, mem_kv, batch, seq):
    tq = min(MEM_TQ, seq)
    nq, qs, ks, vs = _mem_specs(seq, tq)

    def body(q_ref, k_ref, v_ref, o_ref):
        o_ref[...] = _mem_block(q_ref[...].astype(f32), k_ref[...], v_ref[...]).astype(o_ref.dtype)

    return pl.pallas_call(
        body, name="mem_attn_fwd", grid=(batch, MEM_HEADS, nq),
        in_specs=[qs, ks, vs], out_specs=qs, out_shape=jax.ShapeDtypeStruct(q.shape, bf16),
        compiler_params=_cp(("parallel", "parallel", "arbitrary")),
    )(q, mem_kv, mem_kv)


def _mem_bwd(q, mem_kv, do, batch, seq):
    tq = min(MEM_TQ, seq)
    nq, qs, ks, vs = _mem_specs(seq, tq)

    def body(q_ref, k_ref, v_ref, do_ref, dq_ref, dk_ref, dv_ref):
        _, vjp = jax.vjp(_mem_block, q_ref[...].astype(f32), k_ref[...], v_ref[...])
        dq, dk, dv = vjp(do_ref[...])
        dq_ref[...] = dq.astype(dq_ref.dtype)

        @pl.when(pl.program_id(2) == 0)
        def _():
            dk_ref[...] = jnp.zeros_like(dk_ref)
            dv_ref[...] = jnp.zeros_like(dv_ref)

        dk_ref[...] += dk
        dv_ref[...] += dv

    return pl.pallas_call(
        body, name="mem_attn_bwd", grid=(batch, MEM_HEADS, nq),
        in_specs=[qs, ks, vs, qs], out_specs=[qs, ks, ks],
        out_shape=[jax.ShapeDtypeStruct(q.shape, bf16), jax.ShapeDtypeStruct((batch * MEM_LEN, MEM_W), f32),
                   jax.ShapeDtypeStruct((batch * MEM_LEN, MEM_W), f32)],
        compiler_params=_cp(("parallel", "parallel", "arbitrary")),
    )(q, mem_kv, mem_kv, do)


@jax.custom_vjp
def _halves(x):
    c = x.shape[1] // 2
    return x[:, :c], x[:, c:]


_halves.defvjp(lambda x: ((x[:, :x.shape[1] // 2], x[:, x.shape[1] // 2:]), None),
               lambda _, g: (jnp.concatenate(g, axis=1),))


@jax.custom_vjp
def _lead_halves(x):
    n = x.shape[0] // 2
    return x[:n], x[n:]


_lead_halves.defvjp(lambda x: ((x[:x.shape[0] // 2], x[x.shape[0] // 2:]), None),
                    lambda _, g: (jnp.concatenate(g, axis=0),))


def _scan_chunk(s0, r, wl, k, v, a, b):
    nn, nt, tn = _make_mm(True, False)
    nn_exact, _, _ = _make_mm(True, True)
    _, nt_exact, _ = _make_mm(True, "split")
    hp, c, lanes = r.shape
    row = lax.broadcasted_iota(jnp.int32, (c, c), 0)
    col = lax.broadcasted_iota(jnp.int32, (c, c), 1)
    first = (lax.broadcasted_iota(jnp.int32, (1, 1, lanes), 2) // HD) == 0
    tri = jnp.broadcast_to((col <= row).astype(f32)[None], (hp, c, c))
    lg = nn_exact(tri, wl)
    lg_end = lg[:, c - 1:c, :]
    grow, shrink, to_end = jnp.exp(lg), jnp.exp(-lg), jnp.exp(lg_end - lg)
    rt, kt, bt, at = r * grow, k * shrink, b * shrink, a * jnp.exp(lg - wl)
    strict, incl = (col < row)[None], (col <= row)[None]
    twice = lambda t: jnp.concatenate([t, t], axis=0)
    queries = jnp.concatenate([at, rt], axis=1)
    per_head = jnp.concatenate([jnp.where(first, queries, 0.0), jnp.where(first, 0.0, queries)], axis=0)
    (ab, rb), (ak, rk) = _halves(nt_exact(per_head, twice(bt))), _halves(nt_exact(per_head, twice(kt)))
    l_ab = jnp.where(strict, ab, 0.0)
    a_ak = jnp.where(strict, ak, 0.0)
    a_rb = jnp.where(incl, rb, 0.0)
    a_rk = jnp.where(incl, rk, 0.0)
    inv = (col == row).astype(f32)[None] + l_ab
    power, n = l_ab, 1
    while 2 * n < c:
        power = nn(power, power)
        inv = inv + nn(inv, power)
        n *= 2

    def apply(m, t):
        lo, hi = _lead_halves(nn(m, twice(t)))
        return jnp.where(first, lo, hi)

    sa = apply(inv, nt(at, s0) + apply(a_ak, v))
    y = nt(rt, s0) + apply(a_rk, v) + apply(a_rb, sa)
    same_head = ((lax.broadcasted_iota(jnp.int32, (lanes, lanes), 0) // HD)
                 == (lax.broadcasted_iota(jnp.int32, (lanes, lanes), 1) // HD))[None]
    s1 = s0 * jnp.exp(lg_end) + jnp.where(same_head, tn(v, k * to_end) + tn(sa, b * to_end), 0.0)
    return y, s1


PAIRS = HEADS // 2
PAIR_W = 2 * HD
SCAN_ARGS = (0, 3, 1, 2, 4, 5)


def _pair_stack(ref, off):
    return jnp.stack([ref[b, :, off + p * PAIR_W:off + (p + 1) * PAIR_W]
                      for b in range(ref.shape[0]) for p in range(PAIRS)])


def _pair_store(ref, off, val, add_ref=None):
    for b in range(ref.shape[0]):
        for p in range(PAIRS):
            sl = slice(off + p * PAIR_W, off + (p + 1) * PAIR_W)
            v = val[b * PAIRS + p]
            ref[b, :, sl] = v if add_ref is None else v + add_ref[b, :, sl]


def _scan_fwd(main6, batch, seq, side=None):
    c = min(SCAN_CHUNK, seq)
    nc = seq // c
    hp = batch * PAIRS
    srcs, per_peer = side if side is not None else ([], False)
    n_s = len(srcs)

    def body(*refs):
        z_ref, y_ref, s_ref, st = refs[0], refs[1 + n_s], refs[2 + n_s], refs[3 + 2 * n_s]
        _side_exchange(refs[1:1 + n_s], refs[3 + n_s:3 + 2 * n_s], per_peer, refs[4 + 2 * n_s:], nc)

        @pl.when(pl.program_id(0) == 0)
        def _():
            st[...] = jnp.zeros_like(st)

        s0 = st[...]
        s_ref[0] = s0
        y, s1 = _scan_chunk(s0, *[_pair_stack(z_ref, comp * HW) for comp in SCAN_ARGS])
        _pair_store(y_ref, 0, y)
        st[...] = s1

    res = pl.pallas_call(
        body, name="rwkv_scan_fwd", grid=(nc,),
        in_specs=[pl.BlockSpec((batch, c, 6 * HW), lambda i: (0, i, 0))] + [_HBM_SPEC] * n_s,
        out_specs=[pl.BlockSpec((batch, c, HW), lambda i: (0, i, 0)),
                   pl.BlockSpec((1, hp, PAIR_W, PAIR_W), lambda i: (i, 0, 0, 0))] + [_HBM_SPEC] * n_s,
        out_shape=[jax.ShapeDtypeStruct((batch, seq, HW), f32), jax.ShapeDtypeStruct((nc, hp, PAIR_W, PAIR_W), f32)]
        + _side_out_shapes(srcs, per_peer),
        scratch_shapes=[pltpu.VMEM((hp, PAIR_W, PAIR_W), f32)] + _side_sems(n_s),
        compiler_params=_cp(("arbitrary",)),
    )(main6.reshape(batch, seq, 6 * HW), *srcs)
    return res[0].reshape(batch * seq, HW), res[1], list(res[2:])


def _scan_bwd(main6, states, dy, extra, batch, seq, side=None):
    c = min(SCAN_CHUNK, seq)
    nc = seq // c
    hp = batch * PAIRS
    srcs, per_peer = side if side is not None else ([], False)
    n_s = len(srcs)

    def body(*refs):
        z_ref, s_ref, dy_ref, ex_ref = refs[:4]
        dz_ref, dst = refs[4 + n_s], refs[5 + 2 * n_s]
        _side_exchange(refs[4:4 + n_s], refs[5 + n_s:5 + 2 * n_s], per_peer, refs[6 + 2 * n_s:], nc)

        @pl.when(pl.program_id(0) == 0)
        def _():
            dst[...] = jnp.zeros_like(dst)

        _, vjp = jax.vjp(_scan_chunk, s_ref[0], *[_pair_stack(z_ref, comp * HW) for comp in SCAN_ARGS])
        g = vjp((_pair_stack(dy_ref, 0), dst[...]))
        dst[...] = g[0]
        for arg, comp in enumerate(SCAN_ARGS):
            _pair_store(dz_ref, comp * HW, g[1 + arg], ex_ref if comp < 3 else None)

    back = lambda i: (0, nc - 1 - i, 0)
    wide = pl.BlockSpec((batch, c, 6 * HW), back)
    res = pl.pallas_call(
        body, name="rwkv_scan_bwd", grid=(nc,),
        in_specs=[wide, pl.BlockSpec((1, hp, PAIR_W, PAIR_W), lambda i: (nc - 1 - i, 0, 0, 0)),
                  pl.BlockSpec((batch, c, HW), back), pl.BlockSpec((batch, c, 3 * HW), back)] + [_HBM_SPEC] * n_s,
        out_specs=[wide] + [_HBM_SPEC] * n_s,
        out_shape=[jax.ShapeDtypeStruct((batch, seq, 6 * HW), f32)] + _side_out_shapes(srcs, per_peer),
        scratch_shapes=[pltpu.VMEM((hp, PAIR_W, PAIR_W), f32)] + _side_sems(n_s),
        compiler_params=_cp(("arbitrary",)),
    )(main6.reshape(batch, seq, 6 * HW), states, dy.reshape(batch, seq, HW), extra.reshape(batch, seq, 3 * HW), *srcs)
    return res[0].reshape(batch * seq, 6 * HW), list(res[1:])


def _pad_cols(x, width):
    return jnp.pad(x, ((0, 0), (0, width - x.shape[1])))


def _split_w_in(wt):
    z = lambda rows: jnp.zeros((rows, wt.shape[1]), wt.dtype)
    w_r = jnp.concatenate([wt[1544:3080], wt[3080:3144], z(64), wt[3144:3208], z(64), wt[3208:3336]], axis=0)
    return wt[:1536], jnp.concatenate([wt[1536:1544], z(120)], axis=0), w_r, wt[3336:3848], wt[3848:]


def _merge_w_in(g_qkv, g_f, g_r, g_mq, g_g):
    return jnp.concatenate([g_qkv, g_f[:8], g_r[:1536], g_r[1536:1600], g_r[1664:1728], g_r[1792:], g_mq, g_g], axis=0)


def _pad_lora(v):
    z64 = jnp.zeros((1, 64), v.dtype)
    return jnp.concatenate([v[:, :1536], v[:, 1536:1600], z64, v[:, 1600:1664], z64, v[:, 1664:]], axis=1)


def _unpad_lora(v):
    return jnp.concatenate([v[:, :1536], v[:, 1536:1600], v[:, 1664:1728], v[:, 1792:]], axis=1)


def _local_step(x, mem, target, w, p, late=None, early=None, last=None):
    batch, seq, _ = x.shape
    t = batch * seq
    x2, tg2, mem2 = x.reshape(t, D), target.reshape(t, D), mem.reshape(batch * MEM_LEN, D)
    w_qkv, w_f, w_r, w_mq, w_g3 = _split_w_in(w["w_in"])
    mu = _pad_lora(p["rwkv_mu"])
    bias = _pad_cols(p["fox_f_bias"], 128)
    r_k = p["rwkv_r_k"].reshape(1, HW)
    post_params = [p["rwkv_gn_g"], p["rwkv_gn_b"], r_k]
    rw_widths = [HW, HW, HW, LORA_PAD, LORA_PAD, LORA_PAD]
    six = [HW] * 6

    p_g, u = _matmul("proj_gate", _lazy(_fn_rms, [(x2, [D])], D, params=[p["pre1_g"]]), w_g3, "nt", out_dtype=bf16)
    p_qkv = _matmul("proj_qkv", u, w_qkv, "nt", out_dtype=bf16)
    p_f = _matmul("proj_f", u, w_f, "nt")
    p_r = _matmul("proj_rwkv", u, w_r, "nt")
    p_mq = _matmul("proj_memq", u, w_mq, "nt", out_dtype=bf16)

    c = _fox_gate_fwd(p_f, bias, batch, seq)
    c_rows = c[:, :HEADS].reshape(batch, seq, HEADS).transpose(0, 2, 1)
    fox_o, lse, gathered = _fox_fwd(p_qkv, c, c_rows, batch, seq, side=(late[0], False) if late else None)
    if late:
        w = {**w, **late[2](gathered, 0)}
    fox_out = fox_o.astype(bf16)

    w_up = jnp.pad(w["rwkv_w_up"].astype(f32), ((0, LORA_PAD - 64), (0, 0)))
    a_up = jnp.pad(w["rwkv_a_up"].astype(f32), ((0, LORA_PAD - 64), (0, 0)))
    pre_params = [p["rwkv_w0"], w_up, p["rwkv_a0"], a_up, w["rwkv_g_up"].astype(f32), p["rwkv_k_k"], p["rwkv_k_a"]]
    ps = _tokshift_fwd(p_r, mu, batch, seq)
    main6, g_rw = _rows_fwd("rwkv_pre", _fn_rwkv_pre, [], [(ps, rw_widths)], pre_params, [six, [HW]], tm=256)
    y_rw, states, gathered = _scan_fwd(main6, batch, seq, side=(late[1], False) if late else None)
    if late:
        w = {**w, **late[2](gathered, 1)}
    post_consts = []
    post_rows = [(y_rw, [HW]), (main6, [HW, HW, HW]), (g_rw, [HW])]
    fn_post = _fn_rwkv_post

    (rwkv_out,) = _rows_fwd("rwkv_post", fn_post, post_consts, post_rows, post_params, [[HW]], dtypes=[bf16], tm=256)

    mem_kv, memn = _matmul("proj_memkv", _lazy(_fn_rms, [(mem2, [D])], D, params=[p["mem_norm_g"]]), w["w_mem_kv"], "nn")
    mem_out = _mem_fwd(p_mq, mem_kv, batch, seq)

    a_fox = _matmul("out_fox", fox_out, w["w_fox_out"], "nn", out_dtype=bf16)
    a_rwkv = _matmul("out_rwkv", rwkv_out, w["w_rwkv_out"], "nn", out_dtype=bf16)
    a_mem = _matmul("out_mem", mem_out, w["w_mem_out"], "nn", out_dtype=bf16)
    merge_rows = [(a_fox, [D]), (a_rwkv, [D]), (a_mem, [D]), (p_g, [D, D, D])]
    yy, merged = _matmul("out_o", _lazy(_fn_merge, merge_rows, D), w["w_o"], "nn")
    post1_rows = [(yy, [D]), (x2, [D])]
    post1_params = [p["post1_g"], p["pre2_g"]]
    h1, u2 = _rows_fwd("post1", _fn_post1, [], post1_rows, post1_params, [[D], [D]], dtypes=[f32, bf16])
    gp = _matmul("ffn_gate", u2, w["w_ffn_gate"], "nt", out_dtype=bf16)
    up = _matmul("ffn_up", u2, w["w_ffn_up"], "nt", out_dtype=bf16)
    ffn, hmid = _matmul("ffn_down", _lazy(_fn_swiglu, [(gp, [D_FF]), (up, [D_FF])], D_FF), w["w_ffn_down"], "nn")
    final_rows = [(ffn, [D]), (h1, [D])]

    gw, gp_ = {}, {}
    (d_ffn, d_h1), (gp_["post2_g"], loss) = _rows_bwd("final", _fn_final, [(tg2, [D])], final_rows, [p["post2_g"]], [], [],
                                                      n_sums=1, dtypes=[bf16, f32])
    gw["w_ffn_down"] = _matmul("ffn_down_dw", hmid, d_ffn, "tn", out_dtype=bf16)
    (d_gp, d_up), _ = _matmul_then_vjp("ffn_down_dx", d_ffn, w["w_ffn_down"], "nt", _fn_swiglu,
                                       [(gp, [D_FF]), (up, [D_FF])], [bf16, bf16])
    gw["w_ffn_gate"] = _matmul("ffn_gate_dw", d_gp, u2, "tn", out_dtype=bf16)
    gw["w_ffn_up"] = _matmul("ffn_up_dw", d_up, u2, "tn", out_dtype=bf16)
    d_u2_gate = _matmul("ffn_gate_dx", d_gp, w["w_ffn_gate"], "nn")
    (d_yy, d_x_res), (gp_["post1_g"], gp_["pre2_g"]) = _matmul_then_vjp(
        "ffn_up_dx", d_up, w["w_ffn_up"], "nn", _fn_post1, post1_rows, [bf16, f32], params=post1_params,
        first_cts=[d_h1], add=d_u2_gate)
    gw["w_o"] = _matmul("out_o_dw", merged, d_yy, "tn", out_dtype=bf16)
    (d_a_fox, d_a_rwkv, d_a_mem, d_p_g), _ = _matmul_then_vjp("out_o_dx", d_yy, w["w_o"], "nt", _fn_merge, merge_rows,
                                                             [bf16] * 4)
    d_fox_out = _matmul("out_fox_dx", d_a_fox, w["w_fox_out"], "nt")
    gw["w_fox_out"] = _matmul("out_fox_dw", fox_out, d_a_fox, "tn", out_dtype=bf16)
    d_rwkv_out = _matmul("out_rwkv_dx", d_a_rwkv, w["w_rwkv_out"], "nt")
    gw["w_rwkv_out"] = _matmul("out_rwkv_dw", rwkv_out, d_a_rwkv, "tn", out_dtype=bf16)
    d_mem_out = _matmul("out_mem_dx", d_a_mem, w["w_mem_out"], "nt")
    gw["w_mem_out"] = _matmul("out_mem_dw", mem_out, d_a_mem, "tn", out_dtype=bf16)

    d_p_mq, d_km, d_vm = _mem_bwd(p_mq, mem_kv, d_mem_out, batch, seq)
    d_mem_kv = jnp.concatenate([d_km, d_vm], axis=1).astype(bf16)
    gw["w_mem_kv"] = _matmul("proj_memkv_dw", memn, d_mem_kv, "tn", out_dtype=bf16)
    d_memn = _matmul("proj_memkv_dx", d_mem_kv, w["w_mem_kv"], "nt")
    _, (gp_["mem_norm_g"],) = _rows_bwd("rms_mem_bwd", _fn_rms, [], [(mem2, [D])], [p["mem_norm_g"]], [[D]], [d_memn])

    d_q, d_k, d_v, d_cq, d_ck = _fox_bwd(p_qkv, c, c_rows, fox_o, lse, d_fox_out, batch, seq)
    d_p_qkv = jnp.concatenate([d_q, d_k, d_v], axis=1).astype(bf16)
    d_p_f, d_bias = _fox_gate_bwd(p_f, bias, d_cq, d_ck, batch, seq)
    gp_["fox_f_bias"] = d_bias[:, :HEADS]

    (d_y_rw, d_main6_post, d_g_rw), (gp_["rwkv_gn_g"], gp_["rwkv_gn_b"], d_rk) = _rows_bwd(
        "rwkv_post_bwd", fn_post, post_consts, post_rows, post_params, [[HW]], [d_rwkv_out], tm=256)
    gp_["rwkv_r_k"] = d_rk.reshape(1, HEADS, HD)
    d_main6, early_got = _scan_bwd(main6, states, d_y_rw, d_main6_post, batch, seq,
                                   side=(early(gw), True) if early else None)

    def fn_pre_sum(*args):
        return _fn_rwkv_pre(*args)

    (d_ps,), d_pre = _rows_bwd("rwkv_pre_bwd", fn_pre_sum, [], [(ps, rw_widths)], pre_params, [six, [HW]],
                               [d_main6, d_g_rw], tm=256)
    gp_["rwkv_w0"], d_w_up, gp_["rwkv_a0"], d_a_up, gw["rwkv_g_up"], gp_["rwkv_k_k"], gp_["rwkv_k_a"] = d_pre
    gw["rwkv_w_up"], gw["rwkv_a_up"] = d_w_up[:64], d_a_up[:64]
    d_p_r, d_mu = _tokshift_bwd(p_r, mu, d_ps, batch, seq)
    gp_["rwkv_mu"] = _unpad_lora(d_mu)

    gw["w_in"] = _merge_w_in(_matmul("proj_qkv_dw", d_p_qkv, u, "tn", out_dtype=bf16), _matmul("proj_f_dw", d_p_f, u, "tn", out_dtype=bf16),
                             _matmul("proj_rwkv_dw", d_p_r, u, "tn", out_dtype=bf16), _matmul("proj_memq_dw", d_p_mq, u, "tn", out_dtype=bf16),
                             _matmul("proj_gate_dw", d_p_g, u, "tn", out_dtype=bf16))
    d_x, gp_["pre1_g"], last_got = _input_cotangent(
        "proj_dx", [d_p_qkv, d_p_f, d_p_r, d_p_mq, d_p_g], [w_qkv, w_f, w_r, w_mq, w_g3], x2, p["pre1_g"], d_x_res,
        side=(last(gw), True) if last else None)
    return loss, d_x.reshape(x.shape), gw, gp_, early_got, last_got


def _adamw(name, recv, row_off, w, m, v):
    _, rows, cols = w.shape
    row_tiles = [t for t in range(16, min(rows, 128) + 1, 16) if rows % t == 0 and row_off % t == 0]
    if row_tiles:
        tr, tc = max(row_tiles), cols
        first, grid = row_off // tr, (rows // tr,)
        at = lambda i: (0, first + i, 0)
        mine = lambda i: (0, i, 0)
    else:
        assert row_off == 0 and recv.shape[1] == rows
        tr, tc = rows, 128
        grid = (cols // tc,)
        at = mine = lambda i: (0, 0, i)

    def body(g_ref, w_ref, m_ref, v_ref, go_ref, d_ref, mo_ref, vo_ref):
        g = g_ref[0].astype(f32)
        for s in range(1, N_DEV):
            g = g + g_ref[s].astype(f32)
        m_new = ADAM_B1 * m_ref[0] + (1.0 - ADAM_B1) * g
        v_new = ADAM_B2 * v_ref[0] + (1.0 - ADAM_B2) * (g * g)
        m_hat = m_new / (1.0 - ADAM_B1 ** ADAM_STEP)
        v_hat = v_new / (1.0 - ADAM_B2 ** ADAM_STEP)
        go_ref[0] = g
        d_ref[0] = -ADAM_LR * (m_hat / (jnp.sqrt(v_hat) + ADAM_EPS) + ADAM_WD * w_ref[0])
        mo_ref[0] = m_new
        vo_ref[0] = v_new

    spec = pl.BlockSpec((1, tr, tc), mine)
    return pl.pallas_call(
        body, name=name, grid=grid,
        in_specs=[pl.BlockSpec((N_DEV, tr, tc), at), spec, spec, spec],
        out_specs=[spec] * 4, out_shape=[jax.ShapeDtypeStruct(w.shape, f32)] * 4,
        compiler_params=_cp(("parallel",)),
    )(recv, w, m, v)


GROUPS = (
    ("in", ("w_in",), 0),
    ("memkv", ("w_mem_kv",), 0),
    ("ffn_gu", ("w_ffn_gate", "w_ffn_up"), 0),
    ("down_o", ("w_ffn_down", "w_o"), 0),
    ("outs", ("w_fox_out", "w_rwkv_out", "w_mem_out"), 0),
    ("lora", ("rwkv_w_up", "rwkv_a_up", "rwkv_g_up"), 0),
)
FIRST_GROUPS = ("in", "memkv")
LATE_GROUPS = (("down_o", "outs", "lora"), ("ffn_gu",))
EARLY_GRAD_GROUPS = ("memkv", "ffn_gu", "down_o", "outs")
LAST_GRAD_GROUPS = ("in", "lora")
SHARD_AXIS = {n: a for n, _, a in SHARDED}
SMALL_ROWS = 16
LOSS_LANES = 128


def _group_local(shards, members, join):
    parts = [shards[n].reshape(shards[n].shape[-2:]) for n in members]
    return parts[0] if len(parts) == 1 else jnp.concatenate(parts, axis=join)


def _group_split(arr, members, join, lead=False):
    out, off = {}, 0
    for n in members:
        shape = dict((k, s) for k, s, _ in SHARDED)[n]
        size = _block_shape(shape, SHARD_AXIS[n])[join]
        idx = [slice(None)] * arr.ndim
        idx[arr.ndim - 2 + join] = slice(off, off + size)
        out[n] = arr[tuple(idx)]
        off += size
    return out


def _full_from_blocks(blocks, axis):
    if axis == 0:
        return blocks.reshape(-1, blocks.shape[2])
    return blocks.transpose(1, 0, 2).reshape(blocks.shape[1], -1)


def _blocks_from_full(full, axis):
    if axis == 0:
        return full.reshape(N_DEV, -1, full.shape[1])
    return full.reshape(full.shape[0], N_DEV, -1).transpose(1, 0, 2)


def _assemble(gathered, names):
    out = {}
    for arr, g in zip(gathered, names):
        _, members, join = [grp for grp in GROUPS if grp[0] == g][0]
        for n, blk in _group_split(arr, members, join, lead=True).items():
            out[n] = _full_from_blocks(blk, SHARD_AXIS[n])
    return out


def _grad_blocks(gw, names):
    out = []
    for g in names:
        _, members, join = [grp for grp in GROUPS if grp[0] == g][0]
        parts = [_blocks_from_full(gw[n].astype(bf16), SHARD_AXIS[n]) for n in members]
        out.append(parts[0] if len(parts) == 1 else jnp.concatenate(parts, axis=1 + join))
    return out


def _small_pack(d):
    flat = jnp.concatenate([d[n].reshape(-1) for n, _ in REPLICATED])
    return jnp.pad(flat, (0, SMALL_ROWS * LANES - REPL_ELEMS)).reshape(SMALL_ROWS, LANES)


def _small_unpack(packed):
    out, flat, off = {}, packed.reshape(-1), 0
    for n, shape in REPLICATED:
        k = _rows_of((LANES,) + shape)
        out[n] = flat[off:off + k].reshape(shape)
        off += k
    return out


def kernel(x, mem, pre1_g, post1_g, pre2_g, post2_g, mem_norm_g, w_in, fox_f_bias, rwkv_mu, rwkv_w0, rwkv_w_up, rwkv_a0, rwkv_a_up, rwkv_g_up, rwkv_k_k, rwkv_k_a, rwkv_r_k, rwkv_gn_g, rwkv_gn_b, w_mem_kv, w_fox_out, w_rwkv_out, w_mem_out, w_o, w_ffn_gate, w_ffn_up, w_ffn_down, loss_target, m_pre1_g, m_post1_g, m_pre2_g, m_post2_g, m_mem_norm_g, m_w_in, m_fox_f_bias, m_rwkv_mu, m_rwkv_w0, m_rwkv_w_up, m_rwkv_a0, m_rwkv_a_up, m_rwkv_g_up, m_rwkv_k_k, m_rwkv_k_a, m_rwkv_r_k, m_rwkv_gn_g, m_rwkv_gn_b, m_w_mem_kv, m_w_fox_out, m_w_rwkv_out, m_w_mem_out, m_w_o, m_w_ffn_gate, m_w_ffn_up, m_w_ffn_down, v_pre1_g, v_post1_g, v_pre2_g, v_post2_g, v_mem_norm_g, v_w_in, v_fox_f_bias, v_rwkv_mu, v_rwkv_w0, v_rwkv_w_up, v_rwkv_a0, v_rwkv_a_up, v_rwkv_g_up, v_rwkv_k_k, v_rwkv_k_a, v_rwkv_r_k, v_rwkv_gn_g, v_rwkv_gn_b, v_w_mem_kv, v_w_fox_out, v_w_rwkv_out, v_w_mem_out, v_w_o, v_w_ffn_gate, v_w_ffn_up, v_w_ffn_down):
    args = dict(locals())
    turn = lambda n, a: jnp.swapaxes(a, 1, 2) if n in TRANSPOSED else a
    wts = {n: turn(n, args[n]) for n in WEIGHT_ORDER}
    ms = {n: turn(n, args["m_" + n]) for n in WEIGHT_ORDER}
    vs = {n: turn(n, args["v_" + n]) for n in WEIGHT_ORDER}

    groups = {g: (members, join) for g, members, join in GROUPS}
    w_bf16 = {n: wts[n].astype(bf16) for n, _, _ in SHARDED}

    def send(g):
        return _group_local(w_bf16, *groups[g])

    first = _exchange("gather_first", [send(g) for g in FIRST_GROUPS], per_peer=False)
    full = _assemble(first, FIRST_GROUPS)
    small_in = {n: (wts[n] if n == "rwkv_r_k" else wts[n].reshape(wts[n].shape[-2:])) for n, _ in REPLICATED}
    late = ([send(g) for g in LATE_GROUPS[0]], [send(g) for g in LATE_GROUPS[1]],
            lambda got, which: _assemble(got, LATE_GROUPS[which]))
    loss_part, grad_x, gw, gp, early_got, last_got = _local_step(
        x, mem, loss_target, full, small_in, late=late, early=lambda g: _grad_blocks(g, EARLY_GRAD_GROUPS),
        last=lambda g: _grad_blocks(g, LAST_GRAD_GROUPS))
    small_got, loss_got = _exchange("exchange_small", [_small_pack(gp).astype(bf16), jnp.broadcast_to(loss_part, (8, LOSS_LANES))],
                                    per_peer=False)
    received = dict(zip(EARLY_GRAD_GROUPS + LAST_GRAD_GROUPS, list(early_got) + list(last_got)))

    outs = [{}, {}, {}, {}]
    for g, members, _ in GROUPS:
        off = 0
        for n in members:
            for o, arr in zip(outs, _adamw("adamw_" + n, received[g], off, wts[n], ms[n], vs[n])):
                o[n] = arr
            off += wts[n].shape[1]
    res = _adamw("adamw_small", small_got, 0, *[_small_pack(d)[None] for d in (wts, ms, vs)])
    for o, arr in zip(outs, res):
        o.update(_small_unpack(arr))
    loss = jnp.sum(loss_got[:, 0, 0])
    return (loss, grad_x, *[turn(n, o[n].reshape(wts[n].shape)) for o in outs for n in WEIGHT_ORDER])
```

```python
import functools

import jax
import jax.numpy as jnp
from jax import lax
from jax.experimental import pallas as pl
from jax.experimental.pallas import tpu as pltpu

f32 = jnp.float32
bf16 = jnp.bfloat16
_HI = lax.Precision.HIGHEST

D = 1024
HEADS = 8
HD = 64
HW = HEADS * HD
MEM_HEADS = 4
MEM_HD = 128
MEM_W = 512
MEM_LEN = 256
D_FF = 2816
LORA_PAD = 128
NORM_EPS = 1e-6
GN_EPS = 64e-5
SCAN_CHUNK = 64
N_DEV = 8
LANES = 1024
VMEM_LIMIT = 56 * 1024 * 1024

ADAM_LR = 0.001
ADAM_B1 = 0.9
ADAM_B2 = 0.999
ADAM_EPS = 1e-08
ADAM_WD = 0.01
ADAM_STEP = 10

TRANSPOSED = ("w_in", "w_ffn_gate", "w_ffn_up")
SHARDED = (
    ("w_in", (6920, 1024), 0),
    ("w_ffn_gate", (2816, 1024), 0),
    ("w_ffn_up", (2816, 1024), 0),
    ("w_ffn_down", (2816, 1024), 0),
    ("w_mem_kv", (1024, 1024), 0),
    ("w_o", (1024, 1024), 0),
    ("w_fox_out", (512, 1024), 1),
    ("w_rwkv_out", (512, 1024), 1),
    ("w_mem_out", (512, 1024), 1),
    ("rwkv_w_up", (64, 512), 1),
    ("rwkv_a_up", (64, 512), 1),
    ("rwkv_g_up", (128, 512), 1),
)
REPLICATED = (
    ("pre1_g", (1, 1024)), ("post1_g", (1, 1024)), ("pre2_g", (1, 1024)), ("post2_g", (1, 1024)),
    ("mem_norm_g", (1, 1024)), ("fox_f_bias", (1, 8)), ("rwkv_mu", (1, 1792)), ("rwkv_w0", (1, 512)),
    ("rwkv_a0", (1, 512)), ("rwkv_k_k", (1, 512)), ("rwkv_k_a", (1, 512)), ("rwkv_r_k", (1, 8, 64)),
    ("rwkv_gn_g", (1, 512)), ("rwkv_gn_b", (1, 512)),
)
WEIGHT_ORDER = ('pre1_g', 'post1_g', 'pre2_g', 'post2_g', 'mem_norm_g', 'w_in', 'fox_f_bias', 'rwkv_mu',
                'rwkv_w0', 'rwkv_w_up', 'rwkv_a0', 'rwkv_a_up', 'rwkv_g_up', 'rwkv_k_k', 'rwkv_k_a',
                'rwkv_r_k', 'rwkv_gn_g', 'rwkv_gn_b', 'w_mem_kv', 'w_fox_out', 'w_rwkv_out', 'w_mem_out',
                'w_o', 'w_ffn_gate', 'w_ffn_up', 'w_ffn_down')


def _block_shape(shape, axis):
    return tuple(s // N_DEV if i == axis else s for i, s in enumerate(shape))


def _rows_of(shape):
    n = 1
    for s in shape:
        n *= s
    return n // LANES


REPL_ELEMS = sum(_rows_of((LANES,) + s) for _, s in REPLICATED)


def _cp(sem=None):
    return pltpu.CompilerParams(dimension_semantics=sem, vmem_limit_bytes=VMEM_LIMIT)


def _tile(dim, cap):
    best = None
    for t in range(128, min(dim, cap) + 1, 128):
        if dim % t == 0:
            best = t
    return best if best is not None else dim


def _two_terms(x):
    hi = x.astype(bf16)
    return hi, (x - hi.astype(f32)).astype(bf16)


def _dg(a, b, dims, exact):
    if exact == "split":
        (a_hi, a_lo), (b_hi, b_lo) = _two_terms(a), _two_terms(b)
        dot = functools.partial(lax.dot_general, dimension_numbers=dims, preferred_element_type=f32)
        return dot(a_hi, b_hi) + (dot(a_hi, b_lo) + dot(a_lo, b_hi))
    if exact:
        return lax.dot_general(a, b, dims, precision=_HI, preferred_element_type=f32)
    return lax.dot_general(a.astype(bf16), b.astype(bf16), dims, preferred_element_type=f32)


def _make_mm(batched, exact):
    o = 1 if batched else 0
    bd = ((0,), (0,)) if batched else ((), ())
    d_nn = (((1 + o,), (o,)), bd)
    d_nt = (((1 + o,), (1 + o,)), bd)
    d_tn = (((o,), (o,)), bd)

    @jax.custom_vjp
    def nn(a, b):
        return _dg(a, b, d_nn, exact)

    @jax.custom_vjp
    def nt(a, b):
        return _dg(a, b, d_nt, exact)

    @jax.custom_vjp
    def tn(a, b):
        return _dg(a, b, d_tn, exact)

    nn.defvjp(lambda a, b: (_dg(a, b, d_nn, exact), (a, b)),
              lambda res, g: (_dg(g, res[1], d_nt, exact), _dg(res[0], g, d_tn, exact)))
    nt.defvjp(lambda a, b: (_dg(a, b, d_nt, exact), (a, b)),
              lambda res, g: (_dg(g, res[1], d_nn, exact), _dg(g, res[0], d_tn, exact)))
    tn.defvjp(lambda a, b: (_dg(a, b, d_tn, exact), (a, b)),
              lambda res, g: (_dg(res[1], g, d_nt, exact), _dg(res[0], g, d_nn, exact)))
    return nn, nt, tn


def _sigmoid(x):
    return 1.0 / (1.0 + jnp.exp(-x))


def _head_sum_raw(x):
    width = 2 * HD
    i = lax.broadcasted_iota(jnp.int32, (width, width), 0) // HD
    j = lax.broadcasted_iota(jnp.int32, (width, width), 1) // HD
    m = (i == j).astype(bf16)
    dims = (((1,), (0,)), ((), ()))
    out = []
    for p in range(x.shape[1] // width):
        xp = x[:, p * width:(p + 1) * width]
        hi = xp.astype(bf16)
        lo = (xp - hi.astype(f32)).astype(bf16)
        out.append(lax.dot_general(hi, m, dims, preferred_element_type=f32)
                   + lax.dot_general(lo, m, dims, preferred_element_type=f32))
    return jnp.concatenate(out, axis=1)


@jax.custom_vjp
def _head_sum(x):
    return _head_sum_raw(x)


_head_sum.defvjp(lambda x: (_head_sum_raw(x), None), lambda _, g: (_head_sum_raw(g),))


WEIGHT_TILE_BYTES = 13 * 512 * 1024
ACC_TILE_BYTES = 8 * 1024 * 1024


def _lazy(fn, rows, width, params=()):
    return (fn, rows, width, list(params))


def _matmul(name, a, b, mode, add=None, out_dtype=f32):
    has_add = add is not None
    if isinstance(a, tuple):
        a_fn, a_rows, a_width, a_params = a
        a_arrays = [r for r, _ in a_rows]
        a_shape = (a_arrays[0].shape[0], a_width)
    else:
        a_fn, a_rows, a_params, a_arrays, a_shape = None, None, [], [a], a.shape
    n_r = len(a_arrays)
    n_a = n_r + len(a_params)

    def load_a(refs):
        if a_fn is None:
            return refs[0][...].astype(bf16)
        pieces = []
        for r, (_, widths) in zip(refs[:n_r], a_rows):
            pieces += _pieces(r, widths)
        return a_fn(*pieces, *[p[...] for p in refs[n_r:]])[0].astype(bf16)

    if mode == "tn":
        assert a_fn is None
        (k, m), (_, n) = a_shape, b.shape
        tn = _tile(n, max(128, ACC_TILE_BYTES // (4 * m)))
        tk = _tile(k, 2048)
        nk = k // tk

        def body(*refs):
            b_ref, o_ref, acc = refs[n_a:]

            @pl.when(pl.program_id(1) == 0)
            def _():
                acc[...] = jnp.zeros_like(acc)

            acc[...] += lax.dot_general(load_a(refs[:n_a]), b_ref[...].astype(bf16),
                                        (((0,), (0,)), ((), ())), preferred_element_type=f32)

            @pl.when(pl.program_id(1) == nk - 1)
            def _():
                o_ref[...] = acc[...].astype(o_ref.dtype)

        return pl.pallas_call(
            body, name=name, grid=(n // tn, nk),
            in_specs=[pl.BlockSpec((tk, r.shape[1]), lambda j, kk: (kk, 0)) for r in a_arrays]
            + [pl.BlockSpec((tk, tn), lambda j, kk: (kk, j))],
            out_specs=pl.BlockSpec((m, tn), lambda j, kk: (0, j)), out_shape=jax.ShapeDtypeStruct((m, n), out_dtype),
            scratch_shapes=[pltpu.VMEM((m, tn), f32)],
            compiler_params=_cp(("parallel", "arbitrary")),
        )(*a_arrays, b)

    (m, k) = a_shape
    n = b.shape[1] if mode == "nn" else b.shape[0]
    tm = _tile(m, 1024 if a_fn is None else 512)
    tn = _tile(n, max(128, WEIGHT_TILE_BYTES // (2 * k)))
    dims = (((1,), (0,)), ((), ())) if mode == "nn" else (((1,), (1,)), ((), ()))
    b_spec = pl.BlockSpec((k, tn), lambda j, i: (0, j)) if mode == "nn" else pl.BlockSpec((tn, k), lambda j, i: (j, 0))
    o_spec = pl.BlockSpec((tm, tn), lambda j, i: (i, j))

    keep = a_fn is not None
    assert not keep or tn == n

    def body(*refs):
        b_ref = refs[n_a]
        a_val = load_a(refs[:n_a])
        r = lax.dot_general(a_val, b_ref[...].astype(bf16), dims, preferred_element_type=f32)
        if has_add:
            r = r + refs[n_a + 1][...]
        if keep:
            refs[-2][...] = r.astype(refs[-2].dtype)
            refs[-1][...] = a_val
        else:
            refs[-1][...] = r.astype(refs[-1].dtype)

    res = pl.pallas_call(
        body, name=name, grid=(n // tn, m // tm),
        in_specs=[pl.BlockSpec((tm, r.shape[1]), lambda j, i: (i, 0)) for r in a_arrays]
        + [pl.BlockSpec(p.shape, lambda j, i: (0, 0)) for p in a_params] + [b_spec] + ([o_spec] if has_add else []),
        out_specs=[o_spec] + ([pl.BlockSpec((tm, k), lambda j, i: (i, 0))] if keep else []),
        out_shape=[jax.ShapeDtypeStruct((m, n), out_dtype)] + ([jax.ShapeDtypeStruct((m, k), bf16)] if keep else []),
        compiler_params=_cp(("parallel", "arbitrary")),
    )(*a_arrays, *a_params, b, *([add] if has_add else []))
    return tuple(res) if keep else res[0]


def _input_cotangent(name, a_list, b_list, x, gain, add, side=None):
    m = a_list[0].shape[0]
    tm = _tile(m, 256)
    n_g = len(a_list)
    srcs, per_peer = side if side is not None else ([], False)
    n_s = len(srcs)

    def body(*refs):
        x_ref, g_ref, add_ref = refs[2 * n_g:2 * n_g + 3]
        src_refs = refs[2 * n_g + 3:2 * n_g + 3 + n_s]
        dx_ref, dg_ref = refs[2 * n_g + 3 + n_s:2 * n_g + 5 + n_s]
        _side_exchange(src_refs, refs[2 * n_g + 5 + n_s:2 * n_g + 5 + 2 * n_s], per_peer, refs[2 * n_g + 5 + 2 * n_s:], m // tm)
        d_u = None
        for g in range(n_g):
            r = lax.dot_general(refs[g][...].astype(bf16), refs[n_g + g][...].astype(bf16), (((1,), (0,)), ((), ())),
                                preferred_element_type=f32)
            d_u = r if d_u is None else d_u + r
        _, vjp = jax.vjp(_rms, x_ref[...], g_ref[...])
        d_x, d_gain = vjp(d_u)
        dx_ref[...] = d_x + add_ref[...]

        @pl.when(pl.program_id(0) == 0)
        def _():
            dg_ref[...] = jnp.zeros_like(dg_ref)

        dg_ref[...] += d_gain

    rows = pl.BlockSpec((tm, x.shape[1]), lambda i: (i, 0))
    whole = lambda b: pl.BlockSpec(b.shape, lambda i: (0, 0))
    res = pl.pallas_call(
        body, name=name, grid=(m // tm,),
        in_specs=[pl.BlockSpec((tm, a.shape[1]), lambda i: (i, 0)) for a in a_list] + [whole(b) for b in b_list]
        + [rows, whole(gain), rows] + [_HBM_SPEC] * n_s,
        out_specs=[rows, whole(gain)] + [_HBM_SPEC] * n_s,
        out_shape=[jax.ShapeDtypeStruct(x.shape, f32), jax.ShapeDtypeStruct(gain.shape, f32)] + _side_out_shapes(srcs, per_peer),
        scratch_shapes=_side_sems(n_s),
        compiler_params=_cp(("arbitrary",)),
    )(*a_list, *b_list, x, gain, add, *srcs)
    return res[0], res[1], list(res[2:])


def _pieces(ref, widths):
    out, off = [], 0
    for w in widths:
        out.append(ref[:, off:off + w].astype(f32))
        off += w
    return out


def _store_pieces(ref, widths, vals, add_ref=None):
    off = 0
    for w, v in zip(widths, vals):
        ref[:, off:off + w] = (v if add_ref is None else v + add_ref[:, off:off + w]).astype(ref.dtype)
        off += w


def _rows_fwd(name, fn, consts, rows, params, outs, n_sums=0, tm=512, dtypes=None):
    t = (consts + rows)[0][0].shape[0]
    tm = min(tm, t)
    ins = consts + rows
    n_in, n_p, n_o = len(ins), len(params), len(outs)
    dtypes = dtypes or [f32] * n_o

    def body(*refs):
        in_refs, p_refs = refs[:n_in], refs[n_in:n_in + n_p]
        o_refs, s_refs = refs[n_in + n_p:n_in + n_p + n_o], refs[n_in + n_p + n_o:]
        vals = []
        for r, (_, widths) in zip(in_refs, ins):
            vals += _pieces(r, widths)
        res = fn(*vals, *[p[...] for p in p_refs])
        pos = 0
        for r, widths in zip(o_refs, outs):
            _store_pieces(r, widths, res[pos:pos + len(widths)])
            pos += len(widths)

        @pl.when(pl.program_id(0) == 0)
        def _():
            for s in s_refs:
                s[...] = jnp.zeros_like(s)

        for s, v in zip(s_refs, res[pos:]):
            s[...] += v

    row_spec = lambda w: pl.BlockSpec((tm, w), lambda i: (i, 0))
    full = lambda p: pl.BlockSpec(p.shape, lambda i: (0,) * p.ndim)
    return pl.pallas_call(
        body, name=name, grid=(t // tm,),
        in_specs=[row_spec(sum(w)) for _, w in ins] + [full(p) for p in params],
        out_specs=[row_spec(sum(w)) for w in outs] + [pl.BlockSpec((1, 1), lambda i: (0, 0))] * n_sums,
        out_shape=[jax.ShapeDtypeStruct((t, sum(w)), dt) for w, dt in zip(outs, dtypes)] + [jax.ShapeDtypeStruct((1, 1), f32)] * n_sums,
        compiler_params=_cp(("arbitrary",)),
    )(*[a for a, _ in ins], *params)


def _rows_bwd(name, fn, consts, rows, params, outs, cts, n_sums=0, add=None, tm=512, dtypes=None):
    t = (consts + rows)[0][0].shape[0]
    tm = min(tm, t)
    n_c, n_r, n_p, n_o = len(consts), len(rows), len(params), len(outs)
    has_add = add is not None
    dtypes = dtypes or [f32] * n_r

    def body(*refs):
        pos = 0
        c_refs = refs[pos:pos + n_c]; pos += n_c
        r_refs = refs[pos:pos + n_r]; pos += n_r
        p_refs = refs[pos:pos + n_p]; pos += n_p
        ct_refs = refs[pos:pos + n_o]; pos += n_o
        add_ref = refs[pos] if has_add else None
        pos += 1 if has_add else 0
        dr_refs = refs[pos:pos + n_r]; pos += n_r
        dp_refs = refs[pos:pos + n_p]; pos += n_p
        s_refs = refs[pos:pos + n_sums]
        cvals, rvals = [], []
        for r, (_, widths) in zip(c_refs, consts):
            cvals += _pieces(r, widths)
        for r, (_, widths) in zip(r_refs, rows):
            rvals += _pieces(r, widths)
        pvals = [p[...] for p in p_refs]
        ctv = []
        for r, widths in zip(ct_refs, outs):
            ctv += _pieces(r, widths)
        ctv += [jnp.ones((1, 1), f32)] * n_sums
        primal, vjp = jax.vjp(lambda *rp: tuple(fn(*cvals, *rp)), *rvals, *pvals)
        g = vjp(tuple(ctv))
        pos = 0
        for idx, (r, (_, widths)) in enumerate(zip(dr_refs, rows)):
            _store_pieces(r, widths, g[pos:pos + len(widths)], add_ref if idx == 0 else None)
            pos += len(widths)

        @pl.when(pl.program_id(0) == 0)
        def _():
            for acc in list(dp_refs) + list(s_refs):
                acc[...] = jnp.zeros_like(acc)

        for dp, v in zip(dp_refs, g[pos:]):
            dp[...] += v
        for s, v in zip(s_refs, primal[len(primal) - n_sums:]):
            s[...] += v

    row_spec = lambda w: pl.BlockSpec((tm, w), lambda i: (i, 0))
    full = lambda p: pl.BlockSpec(p.shape, lambda i: (0,) * p.ndim)
    args = [a for a, _ in consts + rows] + list(params) + list(cts) + ([add] if has_add else [])
    res = pl.pallas_call(
        body, name=name, grid=(t // tm,),
        in_specs=[row_spec(sum(w)) for _, w in consts + rows] + [full(p) for p in params]
        + [row_spec(sum(w)) for w in outs] + ([row_spec(add.shape[1])] if has_add else []),
        out_specs=[row_spec(sum(w)) for _, w in rows] + [full(p) for p in params]
        + [pl.BlockSpec((1, 1), lambda i: (0, 0))] * n_sums,
        out_shape=[jax.ShapeDtypeStruct((t, sum(w)), dt) for (_, w), dt in zip(rows, dtypes)]
        + [jax.ShapeDtypeStruct(p.shape, f32) for p in params] + [jax.ShapeDtypeStruct((1, 1), f32)] * n_sums,
        compiler_params=_cp(("arbitrary",)),
    )(*args)
    return res[:n_r], res[n_r:n_r + n_p] + res[n_r + n_p:]


def _matmul_then_vjp(name, a, b, mode, fn, rows, dtypes, params=(), first_cts=(), add=None, tm=256):
    m, k = a.shape
    tm = min(tm, m)
    dims = (((1,), (0,)), ((), ())) if mode == "nn" else (((1,), (1,)), ((), ()))
    n_r, n_p, n_c = len(rows), len(params), len(first_cts)
    has_add = add is not None

    def body(*refs):
        a_ref, b_ref = refs[:2]
        pos = 2
        r_refs = refs[pos:pos + n_r]; pos += n_r
        p_refs = refs[pos:pos + n_p]; pos += n_p
        c_refs = refs[pos:pos + n_c]; pos += n_c
        add_ref = refs[pos] if has_add else None
        pos += 1 if has_add else 0
        dr_refs = refs[pos:pos + n_r]; pos += n_r
        dp_refs = refs[pos:pos + n_p]
        ct = lax.dot_general(a_ref[...].astype(bf16), b_ref[...].astype(bf16), dims, preferred_element_type=f32)
        if has_add:
            ct = ct + add_ref[...]
        rvals = []
        for r, (_, widths) in zip(r_refs, rows):
            rvals += _pieces(r, widths)
        _, vjp = jax.vjp(lambda *rp: tuple(fn(*rp)), *rvals, *[p[...] for p in p_refs])
        g = vjp(tuple(c[...].astype(f32) for c in c_refs) + (ct,))
        pos = 0
        for r, (_, widths) in zip(dr_refs, rows):
            _store_pieces(r, widths, g[pos:pos + len(widths)])
            pos += len(widths)

        @pl.when(pl.program_id(0) == 0)
        def _():
            for dp in dp_refs:
                dp[...] = jnp.zeros_like(dp)

        for dp, v in zip(dp_refs, g[pos:]):
            dp[...] += v

    row_spec = lambda w: pl.BlockSpec((tm, w), lambda i: (i, 0))
    whole = lambda p: pl.BlockSpec(p.shape, lambda i: (0, 0))
    res = pl.pallas_call(
        body, name=name, grid=(m // tm,),
        in_specs=[row_spec(k), whole(b)] + [row_spec(sum(w)) for _, w in rows] + [whole(p) for p in params]
        + [row_spec(c.shape[1]) for c in first_cts] + ([row_spec(add.shape[1])] if has_add else []),
        out_specs=[row_spec(sum(w)) for _, w in rows] + [whole(p) for p in params],
        out_shape=[jax.ShapeDtypeStruct((m, sum(w)), dt) for (_, w), dt in zip(rows, dtypes)]
        + [jax.ShapeDtypeStruct(p.shape, f32) for p in params],
        compiler_params=_cp(("arbitrary",)),
    )(a, b, *[r for r, _ in rows], *params, *first_cts, *([add] if has_add else []))
    return res[:n_r], res[n_r:]


def _rms(x, g):
    return x * lax.rsqrt(jnp.mean(x * x, axis=-1, keepdims=True) + NORM_EPS) * g


def _fn_rms(x, g):
    return (_rms(x, g),)


def _fn_rwkv_pre(r, k, v, wd, ad, gd, w0, w_up, a0, a_up, g_up, k_k, k_a):
    nn, _, _ = _make_mm(False, False)
    w_log = -_sigmoid(w0 + nn(jnp.tanh(wd), w_up)) * 0.6065306597126334
    a = _sigmoid(a0 + nn(ad, a_up))
    g = nn(_sigmoid(gd), g_up)
    kk = k * k_k
    kk = kk * lax.rsqrt(jnp.maximum(_head_sum(kk * kk), 1e-24))
    k2 = k * (1.0 + (a - 1.0) * k_a)
    return r, k2, v, w_log, -kk, kk * a, g


def _fn_rwkv_post(y, r, k2, v, g, gn_g, gn_b, r_k):
    mean = _head_sum(y) * (1.0 / HD)
    yc = y - mean
    var = _head_sum(yc * yc) * (1.0 / HD)
    yn = yc * lax.rsqrt(var + GN_EPS) * gn_g + gn_b
    bonus = _head_sum(r * k2 * r_k) * v
    return ((yn + bonus) * g,)


def _fn_merge(a_fox, a_rwkv, a_mem, g_fox, g_rwkv, g_mem):
    return (_sigmoid(g_fox) * a_fox + _sigmoid(g_rwkv) * a_rwkv + _sigmoid(g_mem) * a_mem,)


def _fn_post1(y, x, post1_g, pre2_g):
    h1 = x + _rms(y, post1_g)
    return h1, _rms(h1, pre2_g)


def _fn_swiglu(gp, up):
    return (gp * _sigmoid(gp) * up,)


def _fn_final(target, ffn, h1, post2_g):
    err = h1 + _rms(ffn, post2_g) - target
    per_row = jnp.mean(err * err, axis=-1, keepdims=True)
    return (0.5 * jnp.sum(per_row, axis=0, keepdims=True),)


def _shift_down(x):
    row = lax.broadcasted_iota(jnp.int32, x.shape, 0)
    return jnp.where(row == 0, 0.0, pltpu.roll(x, 1, 0))


def _shift_up(x):
    s = x.shape[0]
    row = lax.broadcasted_iota(jnp.int32, x.shape, 0)
    return jnp.where(row == s - 1, 0.0, pltpu.roll(x, s - 1, 0))


def _tokshift_fwd(p, mu, batch, seq):
    w = p.shape[1]
    tc = _tile(w, 384)

    def body(p_ref, mu_ref, o_ref):
        x = p_ref[...]
        o_ref[...] = x + (_shift_down(x) - x) * mu_ref[...]

    return pl.pallas_call(
        body, name="tokshift_fwd", grid=(w // tc, batch),
        in_specs=[pl.BlockSpec((seq, tc), lambda j, b: (b, j)), pl.BlockSpec((1, tc), lambda j, b: (0, j))],
        out_specs=pl.BlockSpec((seq, tc), lambda j, b: (b, j)),
        out_shape=jax.ShapeDtypeStruct(p.shape, f32),
        compiler_params=_cp(("parallel", "arbitrary")),
    )(p, mu)


def _tokshift_bwd(p, mu, dps, batch, seq):
    w = p.shape[1]
    tc = _tile(w, 384)

    def body(p_ref, mu_ref, d_ref, dp_ref, dmu_ref):
        x, mu_v, d = p_ref[...], mu_ref[...], d_ref[...]
        dp_ref[...] = (d * (1.0 - mu_v) + _shift_up(d * mu_v)).astype(dp_ref.dtype)

        @pl.when(pl.program_id(1) == 0)
        def _():
            dmu_ref[...] = jnp.zeros_like(dmu_ref)

        dmu_ref[...] += jnp.sum(d * (_shift_down(x) - x), axis=0, keepdims=True)

    return pl.pallas_call(
        body, name="tokshift_bwd", grid=(w // tc, batch),
        in_specs=[pl.BlockSpec((seq, tc), lambda j, b: (b, j)), pl.BlockSpec((1, tc), lambda j, b: (0, j)),
                  pl.BlockSpec((seq, tc), lambda j, b: (b, j))],
        out_specs=[pl.BlockSpec((seq, tc), lambda j, b: (b, j)), pl.BlockSpec((1, tc), lambda j, b: (0, j))],
        out_shape=[jax.ShapeDtypeStruct(p.shape, bf16), jax.ShapeDtypeStruct(mu.shape, f32)],
        compiler_params=_cp(("parallel", "arbitrary")),
    )(p, mu, dps)


def _cum_block(seq):
    return _tile(seq, 256)


def _fox_gate_fwd(f, bias, batch, seq):
    cb = _cum_block(seq)

    def body(f_ref, b_ref, c_ref):
        row = lax.broadcasted_iota(jnp.int32, (cb, cb), 0)
        col = lax.broadcasted_iota(jnp.int32, (cb, cb), 1)
        tri = (col <= row).astype(f32)
        carry = jnp.zeros((1, 128), f32)
        for i in range(seq // cb):
            z = f_ref[i * cb:(i + 1) * cb, :] + b_ref[...]
            ls = jnp.minimum(z, 0.0) - jnp.log(1.0 + jnp.exp(-jnp.abs(z)))
            c = _dg(tri, ls, (((1,), (0,)), ((), ())), True) + carry
            c_ref[i * cb:(i + 1) * cb, :] = c
            carry = c[cb - 1:cb, :]

    return pl.pallas_call(
        body, name="fox_gate_fwd", grid=(batch,),
        in_specs=[pl.BlockSpec((seq, 128), lambda b: (b, 0)), pl.BlockSpec((1, 128), lambda b: (0, 0))],
        out_specs=pl.BlockSpec((seq, 128), lambda b: (b, 0)),
        out_shape=jax.ShapeDtypeStruct(f.shape, f32),
        compiler_params=_cp(("arbitrary",)),
    )(f, bias)


def _fox_gate_bwd(f, bias, dc_a, dc_b, batch, seq):
    cb = _cum_block(seq)

    def body(f_ref, b_ref, da_ref, db_ref, df_ref, dbias_ref):
        row = lax.broadcasted_iota(jnp.int32, (cb, cb), 0)
        col = lax.broadcasted_iota(jnp.int32, (cb, cb), 1)
        triu = (col >= row).astype(f32)

        @pl.when(pl.program_id(0) == 0)
        def _():
            dbias_ref[...] = jnp.zeros_like(dbias_ref)

        lane = lax.broadcasted_iota(jnp.int32, (1, 128), 1)

        def by_head(blk):
            out = jnp.zeros((cb, 128), f32)
            for p in range(HEADS // 2):
                for e in range(2):
                    out = jnp.where(lane == 2 * p + e, _pick_lane(blk[:, p * 128:(p + 1) * 128], e), out)
            return out

        carry = jnp.zeros((1, 128), f32)
        tot = jnp.zeros((1, 128), f32)
        for i in reversed(range(seq // cb)):
            sl = slice(i * cb, (i + 1) * cb)
            dc = by_head(da_ref[sl, :] + db_ref[sl, :])
            dls = _dg(triu, dc, (((1,), (0,)), ((), ())), True) + carry
            carry = dls[0:1, :]
            df = dls * _sigmoid(-(f_ref[sl, :] + b_ref[...]))
            df_ref[sl, :] = df.astype(df_ref.dtype)
            tot = tot + jnp.sum(df, axis=0, keepdims=True)
        dbias_ref[...] += tot

    return pl.pallas_call(
        body, name="fox_gate_bwd", grid=(batch,),
        in_specs=[pl.BlockSpec((seq, 128), lambda b: (b, 0)), pl.BlockSpec((1, 128), lambda b: (0, 0)),
                  pl.BlockSpec((seq, HW), lambda b: (b, 0)), pl.BlockSpec((seq, HW), lambda b: (b, 0))],
        out_specs=[pl.BlockSpec((seq, 128), lambda b: (b, 0)), pl.BlockSpec((1, 128), lambda b: (0, 0))],
        out_shape=[jax.ShapeDtypeStruct(f.shape, bf16), jax.ShapeDtypeStruct((1, 128), f32)],
        compiler_params=_cp(("arbitrary",)),
    )(f, bias, dc_a, dc_b)


_HBM_SPEC = pl.BlockSpec(memory_space=pltpu.HBM)


def _side_out_shapes(srcs, per_peer):
    return [jax.ShapeDtypeStruct(((N_DEV,) + tuple(s.shape[1:] if per_peer else s.shape)), s.dtype) for s in srcs]


def _side_sems(n):
    if n == 0:
        return []
    return [pltpu.SemaphoreType.DMA((n, N_DEV - 1)), pltpu.SemaphoreType.DMA((n, N_DEV - 1)), pltpu.SemaphoreType.DMA((n,))]


def _peer_copies(src_refs, dst_refs, per_peer, sems):
    send_sems, recv_sems, local_sems = sems
    x, y, c = lax.axis_index("x"), lax.axis_index("y"), lax.axis_index("c")
    me = 4 * x + 2 * y + c

    def remote(src, dst, t, k, to):
        return pltpu.make_async_remote_copy(src_ref=src, dst_ref=dst, send_sem=send_sems.at[t, k - 1],
                                            recv_sem=recv_sems.at[t, k - 1], device_id=to,
                                            device_id_type=pl.DeviceIdType.MESH)

    direct, relays = [], []
    for t, (s, d) in enumerate(zip(src_refs, dst_refs)):
        direct.append((t, 0, pltpu.make_async_copy(s.at[me] if per_peer else s, d.at[me], local_sems.at[t])))
        for k in range(1, N_DEV):
            px = 1 - x if k & 4 else x
            py = 1 - y if k & 2 else y
            pc = 1 - c if k & 1 else c
            if per_peer:
                direct.append((t, k, remote(s.at[4 * px + 2 * py + pc], d.at[me], t, k, (px, py, pc))))
            elif k == 1 or not k & 1:
                direct.append((t, k, remote(s, d.at[me], t, k, (px, py, pc))))
            else:
                origin = d.at[4 * px + 2 * py + c]
                relays.append((t, k - 1, remote(origin, origin, t, k, (x, y, 1 - c))))
    return direct, relays


def _exchange_start(direct):
    for _, _, cp in direct:
        cp.start()


def _exchange_relay(direct, relays):
    landed = {(t, k): cp for t, k, cp in direct}
    for t, j, cp in relays:
        landed[(t, j)].wait_recv()
        cp.start()


def _exchange_finish(direct, relays):
    relayed = {(t, j) for t, j, _ in relays}
    for t, k, cp in direct:
        if k == 0:
            cp.wait()
        else:
            cp.wait_send()
            if (t, k) not in relayed:
                cp.wait_recv()
    for _, _, cp in relays:
        cp.wait()


def _side_exchange(src_refs, dst_refs, per_peer, sems, *grid):
    if not src_refs:
        return
    step, total = 0, 1
    for a, n in enumerate(grid):
        step, total = step * n + pl.program_id(a), total * n

    @pl.when(step == 0)
    def _():
        _exchange_start(_peer_copies(src_refs, dst_refs, per_peer, sems)[0])

    @pl.when(step == (3 * total) // 4)
    def _():
        _exchange_relay(*_peer_copies(src_refs, dst_refs, per_peer, sems))

    @pl.when(step == total - 1)
    def _():
        _exchange_finish(*_peer_copies(src_refs, dst_refs, per_peer, sems))


def _exchange(name, srcs, per_peer):
    n = len(srcs)

    def body(*refs):
        direct, relays = _peer_copies(refs[:n], refs[n:2 * n], per_peer, refs[2 * n:])
        _exchange_start(direct)
        _exchange_relay(direct, relays)
        _exchange_finish(direct, relays)

    return pl.pallas_call(
        body, name=name, in_specs=[_HBM_SPEC] * n, out_specs=[_HBM_SPEC] * n,
        out_shape=_side_out_shapes(srcs, per_peer), scratch_shapes=_side_sems(n),
    )(*srcs)


FOX_T = 512
_NEG = -1e30
_D2 = (((1,), (1,)), ((), ()))
_D1 = (((1,), (0,)), ((), ()))
_D0 = (((0,), (0,)), ((), ()))


def _bdot(a, b, dims):
    return lax.dot_general(a.astype(bf16), b.astype(bf16), dims, preferred_element_type=f32)


def _pick_lane(x, lane):
    idx = lax.broadcasted_iota(jnp.int32, x.shape, 1)
    return jnp.sum(jnp.where(idx == lane, x, 0.0), axis=1, keepdims=True)


def _pick_row(x, row):
    idx = lax.broadcasted_iota(jnp.int32, x.shape, 0)
    return jnp.sum(jnp.where(idx == row, x, 0.0), axis=0, keepdims=True)


def _fox_fwd(qkv, c, c_rows, batch, seq, side=None):
    t = min(FOX_T, seq)
    nq = seq // t
    scale = HD ** -0.5
    srcs, per_peer = side if side is not None else ([], False)
    n_s = len(srcs)

    def body(*refs):
        q_ref, k_ref, v_ref, cq_ref, ck_ref = refs[:5]
        o_ref, lse_ref = refs[5 + n_s:7 + n_s]
        _side_exchange(refs[5:5 + n_s], refs[7 + n_s:7 + 2 * n_s], per_peer, refs[7 + 2 * n_s:], batch, PAIRS, nq)
        pair, i = pl.program_id(1), pl.program_id(2)
        lane = lax.broadcasted_iota(jnp.int32, (1, PAIR_W), 1)
        first = (lane // HD) == 0
        mine = [first, jnp.logical_not(first)]
        q = q_ref[...] * scale
        qs = [jnp.where(mine[e], q, 0.0) for e in range(2)]
        cqs = [_pick_lane(cq_ref[...], 2 * pair + e) for e in range(2)]
        causal = lax.broadcasted_iota(jnp.int32, (t, t), 1) <= lax.broadcasted_iota(jnp.int32, (t, t), 0)

        def block(j, carry, diagonal):
            rows = pl.ds(pl.multiple_of(j * t, t), t)
            kj, vj = k_ref[rows, :], v_ref[rows, :]
            ck_blk = ck_ref[0, :, rows]
            out = []
            for e in range(2):
                m, acc = carry[2 * e:2 * e + 2]
                s = _bdot(qs[e], kj, _D2) + cqs[e] - _pick_row(ck_blk, 2 * pair + e)
                if diagonal:
                    s = jnp.where(causal, s, _NEG)
                m_new = jnp.maximum(m, jnp.max(s, axis=1, keepdims=True))
                p = jnp.exp(s - m_new)
                out += [m_new, jnp.exp(m - m_new) * acc + _bdot(p, jnp.where(mine[e], vj, 1.0), _D1)]
            return tuple(out)

        init = (jnp.full((t, 1), _NEG, f32), jnp.zeros((t, PAIR_W), f32)) * 2
        carry = lax.fori_loop(0, i, lambda j, cr: block(j, cr, False), init)
        m0, a0, m1, a1 = block(i, carry, True)
        l0, l1 = _pick_lane(a0, HD), _pick_lane(a1, 0)
        o_ref[...] = jnp.where(first, a0 / l0, a1 / l1)
        lse_ref[...] = jnp.where(lane == 0, m0 + jnp.log(l0), jnp.where(lane == 1, m1 + jnp.log(l1), 0.0))

    q_spec = pl.BlockSpec((t, PAIR_W), lambda b, p, i: (b * nq + i, p))
    res = pl.pallas_call(
        body, name="fox_attn_fwd", grid=(batch, PAIRS, nq),
        in_specs=[q_spec,
                  pl.BlockSpec((seq, PAIR_W), lambda b, p, i: (b, PAIRS + p)),
                  pl.BlockSpec((seq, PAIR_W), lambda b, p, i: (b, 2 * PAIRS + p)),
                  pl.BlockSpec((t, 128), lambda b, p, i: (b * nq + i, 0)),
                  pl.BlockSpec((1, 8, seq), lambda b, p, i: (b, 0, 0))] + [_HBM_SPEC] * n_s,
        out_specs=[q_spec, q_spec] + [_HBM_SPEC] * n_s,
        out_shape=[jax.ShapeDtypeStruct((batch * seq, HW), f32)] * 2 + _side_out_shapes(srcs, per_peer),
        scratch_shapes=_side_sems(n_s),
        compiler_params=_cp(("arbitrary", "arbitrary", "arbitrary")),
    )(qkv, qkv, qkv, c, c_rows, *srcs)
    return res[0], res[1], list(res[2:])


def _fox_bwd(qkv, c, c_rows, o, lse, do, batch, seq):
    t = min(FOX_T, seq)
    nq = seq // t
    scale = HD ** -0.5

    def body(q_ref, k_ref, v_ref, cq_ref, ck_ref, o_ref, lse_ref, do_ref,
             dq_ref, dk_ref, dv_ref, dcq_ref, dck_ref, acc0, acc1):
        pair, i = pl.program_id(1), pl.program_id(2)
        accs = [acc0, acc1]

        @pl.when(i == 0)
        def _():
            dv_ref[...] = jnp.zeros_like(dv_ref)
            acc0[...] = jnp.zeros_like(acc0)
            acc1[...] = jnp.zeros_like(acc1)

        lane = lax.broadcasted_iota(jnp.int32, (1, PAIR_W), 1)
        first = (lane // HD) == 0
        mine = [first, jnp.logical_not(first)]
        q, d_o, o_i = q_ref[...] * scale, do_ref[...], o_ref[...]
        q0s = [jnp.where(mine[e], q, 0.0) for e in range(2)]
        q1s = [jnp.where(mine[e], q, 1.0) for e in range(2)]
        dos = [jnp.where(mine[e], d_o, 0.0) for e in range(2)]
        deltas = [jnp.sum(dos[e] * o_i, axis=1, keepdims=True) for e in range(2)]
        lses = [_pick_lane(lse_ref[...], e) for e in range(2)]
        cqs = [_pick_lane(cq_ref[...], 2 * pair + e) for e in range(2)]
        causal = lax.broadcasted_iota(jnp.int32, (t, t), 1) <= lax.broadcasted_iota(jnp.int32, (t, t), 0)

        def block(j, dqs, diagonal):
            rows = pl.ds(pl.multiple_of(j * t, t), t)
            kj, vj = k_ref[rows, :], v_ref[rows, :]
            ck_blk = ck_ref[0, :, rows]
            out = []
            for e in range(2):
                s = _bdot(q0s[e], kj, _D2) + cqs[e] - _pick_row(ck_blk, 2 * pair + e)
                if diagonal:
                    s = jnp.where(causal, s, _NEG)
                p = jnp.exp(s - lses[e])
                ds = p * (_bdot(dos[e], vj, _D2) - deltas[e])
                dv_ref[rows, :] += _bdot(p, dos[e], _D0)
                accs[e][rows, :] += _bdot(ds, q1s[e], _D0)
                out.append(dqs[e] + _bdot(ds, jnp.where(mine[e], kj, 1.0), _D1))
            return tuple(out)

        zero = jnp.zeros((t, PAIR_W), f32)
        dqs = lax.fori_loop(0, i, lambda j, cr: block(j, cr, False), (zero, zero))
        dq0, dq1 = block(i, dqs, True)
        dq_ref[...] = jnp.where(first, dq0, dq1) * scale
        dcq_ref[...] = jnp.where(lane == 0, _pick_lane(dq0, HD), jnp.where(lane == 1, _pick_lane(dq1, 0), 0.0))

        @pl.when(i == nq - 1)
        def _():
            a0, a1 = acc0[...], acc1[...]
            dk_ref[...] = jnp.where(first, a0, a1)
            dck_ref[...] = jnp.where(lane == 0, -_pick_lane(a0, HD), jnp.where(lane == 1, -_pick_lane(a1, 0), 0.0))

    blk = lambda col: pl.BlockSpec((t, PAIR_W), lambda b, p, i: (b * nq + i, col * PAIRS + p))
    whole = lambda col: pl.BlockSpec((seq, PAIR_W), lambda b, p, i: (b, col * PAIRS + p))
    t_all = batch * seq
    return pl.pallas_call(
        body, name="fox_attn_bwd", grid=(batch, PAIRS, nq),
        in_specs=[blk(0), whole(1), whole(2),
                  pl.BlockSpec((t, 128), lambda b, p, i: (b * nq + i, 0)),
                  pl.BlockSpec((1, 8, seq), lambda b, p, i: (b, 0, 0)),
                  blk(0), blk(0), blk(0)],
        out_specs=[blk(0), whole(0), whole(0), blk(0), whole(0)],
        out_shape=[jax.ShapeDtypeStruct((t_all, HW), f32)] * 5,
        scratch_shapes=[pltpu.VMEM((seq, PAIR_W), f32), pltpu.VMEM((seq, PAIR_W), f32)],
        compiler_params=_cp(("parallel", "parallel", "arbitrary")),
    )(qkv, qkv, qkv, c, c_rows, o, lse, do)


MEM_TQ = 1024


def _mem_block(q, km, vm):
    nn, nt, _ = _make_mm(False, False)
    logits = nt(q, km) * (MEM_HD ** -0.5)
    m = lax.stop_gradient(jnp.max(logits, axis=-1, keepdims=True))
    e = jnp.exp(logits - m)
    return nn(e / jnp.sum(e, axis=-1, keepdims=True), vm)


def _mem_specs(seq, tq):
    nq = seq // tq
    qs = pl.BlockSpec((tq, MEM_HD), lambda b, h, i: (b * nq + i, h))
    ks = pl.BlockSpec((MEM_LEN, MEM_HD), lambda b, h, i: (b, h))
    vs = pl.BlockSpec((MEM_LEN, MEM_HD), lambda b, h, i: (b, MEM_HEADS + h))
    return nq, qs, ks, vs


def _mem_fwd(q, mem_kv, batch, seq):
    tq = min(MEM_TQ, seq)
    nq, qs, ks, vs = _mem_specs(seq, tq)

    def body(q_ref, k_ref, v_ref, o_ref):
        o_ref[...] = _mem_block(q_ref[...].astype(f32), k_ref[...], v_ref[...]).astype(o_ref.dtype)

    return pl.pallas_call(
        body, name="mem_attn_fwd", grid=(batch, MEM_HEADS, nq),
        in_specs=[qs, ks, vs], out_specs=qs, out_shape=jax.ShapeDtypeStruct(q.shape, bf16),
        compiler_params=_cp(("parallel", "parallel", "arbitrary")),
    )(q, mem_kv, mem_kv)


def _mem_bwd(q, mem_kv, do, batch, seq):
    tq = min(MEM_TQ, seq)
    nq, qs, ks, vs = _mem_specs(seq, tq)

    def body(q_ref, k_ref, v_ref, do_ref, dq_ref, dk_ref, dv_ref):
        _, vjp = jax.vjp(_mem_block, q_ref[...].astype(f32), k_ref[...], v_ref[...])
        dq, dk, dv = vjp(do_ref[...])
        dq_ref[...] = dq.astype(dq_ref.dtype)

        @pl.when(pl.program_id(2) == 0)
        def _():
            dk_ref[...] = jnp.zeros_like(dk_ref)
            dv_ref[...] = jnp.zeros_like(dv_ref)

        dk_ref[...] += dk
        dv_ref[...] += dv

    return pl.pallas_call(
        body, name="mem_attn_bwd", grid=(batch, MEM_HEADS, nq),
        in_specs=[qs, ks, vs, qs], out_specs=[qs, ks, ks],
        out_shape=[jax.ShapeDtypeStruct(q.shape, bf16), jax.ShapeDtypeStruct((batch * MEM_LEN, MEM_W), f32),
                   jax.ShapeDtypeStruct((batch * MEM_LEN, MEM_W), f32)],
        compiler_params=_cp(("parallel", "parallel", "arbitrary")),
    )(q, mem_kv, mem_kv, do)


@jax.custom_vjp
def _halves(x):
    c = x.shape[1] // 2
    return x[:, :c], x[:, c:]


_halves.defvjp(lambda x: ((x[:, :x.shape[1] // 2], x[:, x.shape[1] // 2:]), None),
               lambda _, g: (jnp.concatenate(g, axis=1),))


@jax.custom_vjp
def _lead_halves(x):
    n = x.shape[0] // 2
    return x[:n], x[n:]


_lead_halves.defvjp(lambda x: ((x[:x.shape[0] // 2], x[x.shape[0] // 2:]), None),
                    lambda _, g: (jnp.concatenate(g, axis=0),))


def _scan_chunk(s0, r, wl, k, v, a, b):
    nn, nt, tn = _make_mm(True, False)
    nn_exact, _, _ = _make_mm(True, True)
    _, nt_exact, _ = _make_mm(True, "split")
    hp, c, lanes = r.shape
    row = lax.broadcasted_iota(jnp.int32, (c, c), 0)
    col = lax.broadcasted_iota(jnp.int32, (c, c), 1)
    first = (lax.broadcasted_iota(jnp.int32, (1, 1, lanes), 2) // HD) == 0
    tri = jnp.broadcast_to((col <= row).astype(f32)[None], (hp, c, c))
    lg = nn_exact(tri, wl)
    lg_end = lg[:, c - 1:c, :]
    grow, shrink, to_end = jnp.exp(lg), jnp.exp(-lg), jnp.exp(lg_end - lg)
    rt, kt, bt, at = r * grow, k * shrink, b * shrink, a * jnp.exp(lg - wl)
    strict, incl = (col < row)[None], (col <= row)[None]
    twice = lambda t: jnp.concatenate([t, t], axis=0)
    queries = jnp.concatenate([at, rt], axis=1)
    per_head = jnp.concatenate([jnp.where(first, queries, 0.0), jnp.where(first, 0.0, queries)], axis=0)
    (ab, rb), (ak, rk) = _halves(nt_exact(per_head, twice(bt))), _halves(nt_exact(per_head, twice(kt)))
    l_ab = jnp.where(strict, ab, 0.0)
    a_ak = jnp.where(strict, ak, 0.0)
    a_rb = jnp.where(incl, rb, 0.0)
    a_rk = jnp.where(incl, rk, 0.0)
    inv = (col == row).astype(f32)[None] + l_ab
    power, n = l_ab, 1
    while 2 * n < c:
        power = nn(power, power)
        inv = inv + nn(inv, power)
        n *= 2

    def apply(m, t):
        lo, hi = _lead_halves(nn(m, twice(t)))
        return jnp.where(first, lo, hi)

    sa = apply(inv, nt(at, s0) + apply(a_ak, v))
    y = nt(rt, s0) + apply(a_rk, v) + apply(a_rb, sa)
    same_head = ((lax.broadcasted_iota(jnp.int32, (lanes, lanes), 0) // HD)
                 == (lax.broadcasted_iota(jnp.int32, (lanes, lanes), 1) // HD))[None]
    s1 = s0 * jnp.exp(lg_end) + jnp.where(same_head, tn(v, k * to_end) + tn(sa, b * to_end), 0.0)
    return y, s1


PAIRS = HEADS // 2
PAIR_W = 2 * HD
SCAN_ARGS = (0, 3, 1, 2, 4, 5)


def _pair_stack(ref, off):
    return jnp.stack([ref[b, :, off + p * PAIR_W:off + (p + 1) * PAIR_W]
                      for b in range(ref.shape[0]) for p in range(PAIRS)])


def _pair_store(ref, off, val, add_ref=None):
    for b in range(ref.shape[0]):
        for p in range(PAIRS):
            sl = slice(off + p * PAIR_W, off + (p + 1) * PAIR_W)
            v = val[b * PAIRS + p]
            ref[b, :, sl] = v if add_ref is None else v + add_ref[b, :, sl]


def _scan_fwd(main6, batch, seq, side=None):
    c = min(SCAN_CHUNK, seq)
    nc = seq // c
    hp = batch * PAIRS
    srcs, per_peer = side if side is not None else ([], False)
    n_s = len(srcs)

    def body(*refs):
        z_ref, y_ref, s_ref, st = refs[0], refs[1 + n_s], refs[2 + n_s], refs[3 + 2 * n_s]
        _side_exchange(refs[1:1 + n_s], refs[3 + n_s:3 + 2 * n_s], per_peer, refs[4 + 2 * n_s:], nc)

        @pl.when(pl.program_id(0) == 0)
        def _():
            st[...] = jnp.zeros_like(st)

        s0 = st[...]
        s_ref[0] = s0
        y, s1 = _scan_chunk(s0, *[_pair_stack(z_ref, comp * HW) for comp in SCAN_ARGS])
        _pair_store(y_ref, 0, y)
        st[...] = s1

    res = pl.pallas_call(
        body, name="rwkv_scan_fwd", grid=(nc,),
        in_specs=[pl.BlockSpec((batch, c, 6 * HW), lambda i: (0, i, 0))] + [_HBM_SPEC] * n_s,
        out_specs=[pl.BlockSpec((batch, c, HW), lambda i: (0, i, 0)),
                   pl.BlockSpec((1, hp, PAIR_W, PAIR_W), lambda i: (i, 0, 0, 0))] + [_HBM_SPEC] * n_s,
        out_shape=[jax.ShapeDtypeStruct((batch, seq, HW), f32), jax.ShapeDtypeStruct((nc, hp, PAIR_W, PAIR_W), f32)]
        + _side_out_shapes(srcs, per_peer),
        scratch_shapes=[pltpu.VMEM((hp, PAIR_W, PAIR_W), f32)] + _side_sems(n_s),
        compiler_params=_cp(("arbitrary",)),
    )(main6.reshape(batch, seq, 6 * HW), *srcs)
    return res[0].reshape(batch * seq, HW), res[1], list(res[2:])


def _scan_bwd(main6, states, dy, extra, batch, seq, side=None):
    c = min(SCAN_CHUNK, seq)
    nc = seq // c
    hp = batch * PAIRS
    srcs, per_peer = side if side is not None else ([], False)
    n_s = len(srcs)

    def body(*refs):
        z_ref, s_ref, dy_ref, ex_ref = refs[:4]
        dz_ref, dst = refs[4 + n_s], refs[5 + 2 * n_s]
        _side_exchange(refs[4:4 + n_s], refs[5 + n_s:5 + 2 * n_s], per_peer, refs[6 + 2 * n_s:], nc)

        @pl.when(pl.program_id(0) == 0)
        def _():
            dst[...] = jnp.zeros_like(dst)

        _, vjp = jax.vjp(_scan_chunk, s_ref[0], *[_pair_stack(z_ref, comp * HW) for comp in SCAN_ARGS])
        g = vjp((_pair_stack(dy_ref, 0), dst[...]))
        dst[...] = g[0]
        for arg, comp in enumerate(SCAN_ARGS):
            _pair_store(dz_ref, comp * HW, g[1 + arg], ex_ref if comp < 3 else None)

    back = lambda i: (0, nc - 1 - i, 0)
    wide = pl.BlockSpec((batch, c, 6 * HW), back)
    res = pl.pallas_call(
        body, name="rwkv_scan_bwd", grid=(nc,),
        in_specs=[wide, pl.BlockSpec((1, hp, PAIR_W, PAIR_W), lambda i: (nc - 1 - i, 0, 0, 0)),
                  pl.BlockSpec((batch, c, HW), back), pl.BlockSpec((batch, c, 3 * HW), back)] + [_HBM_SPEC] * n_s,
        out_specs=[wide] + [_HBM_SPEC] * n_s,
        out_shape=[jax.ShapeDtypeStruct((batch, seq, 6 * HW), f32)] + _side_out_shapes(srcs, per_peer),
        scratch_shapes=[pltpu.VMEM((hp, PAIR_W, PAIR_W), f32)] + _side_sems(n_s),
        compiler_params=_cp(("arbitrary",)),
    )(main6.reshape(batch, seq, 6 * HW), states, dy.reshape(batch, seq, HW), extra.reshape(batch, seq, 3 * HW), *srcs)
    return res[0].reshape(batch * seq, 6 * HW), list(res[1:])


def _pad_cols(x, width):
    return jnp.pad(x, ((0, 0), (0, width - x.shape[1])))


def _split_w_in(wt):
    z = lambda rows: jnp.zeros((rows, wt.shape[1]), wt.dtype)
    w_r = jnp.concatenate([wt[1544:3080], wt[3080:3144], z(64), wt[3144:3208], z(64), wt[3208:3336]], axis=0)
    return wt[:1536], jnp.concatenate([wt[1536:1544], z(120)], axis=0), w_r, wt[3336:3848], wt[3848:]


def _merge_w_in(g_qkv, g_f, g_r, g_mq, g_g):
    return jnp.concatenate([g_qkv, g_f[:8], g_r[:1536], g_r[1536:1600], g_r[1664:1728], g_r[1792:], g_mq, g_g], axis=0)


def _pad_lora(v):
    z64 = jnp.zeros((1, 64), v.dtype)
    return jnp.concatenate([v[:, :1536], v[:, 1536:1600], z64, v[:, 1600:1664], z64, v[:, 1664:]], axis=1)


def _unpad_lora(v):
    return jnp.concatenate([v[:, :1536], v[:, 1536:1600], v[:, 1664:1728], v[:, 1792:]], axis=1)


def _local_step(x, mem, target, w, p, late=None, early=None, last=None):
    batch, seq, _ = x.shape
    t = batch * seq
    x2, tg2, mem2 = x.reshape(t, D), target.reshape(t, D), mem.reshape(batch * MEM_LEN, D)
    w_qkv, w_f, w_r, w_mq, w_g3 = _split_w_in(w["w_in"])
    mu = _pad_lora(p["rwkv_mu"])
    bias = _pad_cols(p["fox_f_bias"], 128)
    r_k = p["rwkv_r_k"].reshape(1, HW)
    post_params = [p["rwkv_gn_g"], p["rwkv_gn_b"], r_k]
    rw_widths = [HW, HW, HW, LORA_PAD, LORA_PAD, LORA_PAD]
    six = [HW] * 6

    p_g, u = _matmul("proj_gate", _lazy(_fn_rms, [(x2, [D])], D, params=[p["pre1_g"]]), w_g3, "nt", out_dtype=bf16)
    p_qkv = _matmul("proj_qkv", u, w_qkv, "nt", out_dtype=bf16)
    p_f = _matmul("proj_f", u, w_f, "nt")
    p_r = _matmul("proj_rwkv", u, w_r, "nt")
    p_mq = _matmul("proj_memq", u, w_mq, "nt", out_dtype=bf16)

    c = _fox_gate_fwd(p_f, bias, batch, seq)
    c_rows = c[:, :HEADS].reshape(batch, seq, HEADS).transpose(0, 2, 1)
    fox_o, lse, gathered = _fox_fwd(p_qkv, c, c_rows, batch, seq, side=(late[0], False) if late else None)
    if late:
        w = {**w, **late[2](gathered, 0)}
    fox_out = fox_o.astype(bf16)

    w_up = jnp.pad(w["rwkv_w_up"].astype(f32), ((0, LORA_PAD - 64), (0, 0)))
    a_up = jnp.pad(w["rwkv_a_up"].astype(f32), ((0, LORA_PAD - 64), (0, 0)))
    pre_params = [p["rwkv_w0"], w_up, p["rwkv_a0"], a_up, w["rwkv_g_up"].astype(f32), p["rwkv_k_k"], p["rwkv_k_a"]]
    ps = _tokshift_fwd(p_r, mu, batch, seq)
    main6, g_rw = _rows_fwd("rwkv_pre", _fn_rwkv_pre, [], [(ps, rw_widths)], pre_params, [six, [HW]], tm=256)
    y_rw, states, gathered = _scan_fwd(main6, batch, seq, side=(late[1], False) if late else None)
    if late:
        w = {**w, **late[2](gathered, 1)}
    post_consts = []
    post_rows = [(y_rw, [HW]), (main6, [HW, HW, HW]), (g_rw, [HW])]
    fn_post = _fn_rwkv_post

    (rwkv_out,) = _rows_fwd("rwkv_post", fn_post, post_consts, post_rows, post_params, [[HW]], dtypes=[bf16], tm=256)

    mem_kv, memn = _matmul("proj_memkv", _lazy(_fn_rms, [(mem2, [D])], D, params=[p["mem_norm_g"]]), w["w_mem_kv"], "nn")
    mem_out = _mem_fwd(p_mq, mem_kv, batch, seq)

    a_fox = _matmul("out_fox", fox_out, w["w_fox_out"], "nn", out_dtype=bf16)
    a_rwkv = _matmul("out_rwkv", rwkv_out, w["w_rwkv_out"], "nn", out_dtype=bf16)
    a_mem = _matmul("out_mem", mem_out, w["w_mem_out"], "nn", out_dtype=bf16)
    merge_rows = [(a_fox, [D]), (a_rwkv, [D]), (a_mem, [D]), (p_g, [D, D, D])]
    yy, merged = _matmul("out_o", _lazy(_fn_merge, merge_rows, D), w["w_o"], "nn")
    post1_rows = [(yy, [D]), (x2, [D])]
    post1_params = [p["post1_g"], p["pre2_g"]]
    h1, u2 = _rows_fwd("post1", _fn_post1, [], post1_rows, post1_params, [[D], [D]], dtypes=[f32, bf16])
    gp = _matmul("ffn_gate", u2, w["w_ffn_gate"], "nt", out_dtype=bf16)
    up = _matmul("ffn_up", u2, w["w_ffn_up"], "nt", out_dtype=bf16)
    ffn, hmid = _matmul("ffn_down", _lazy(_fn_swiglu, [(gp, [D_FF]), (up, [D_FF])], D_FF), w["w_ffn_down"], "nn")
    final_rows = [(ffn, [D]), (h1, [D])]

    gw, gp_ = {}, {}
    (d_ffn, d_h1), (gp_["post2_g"], loss) = _rows_bwd("final", _fn_final, [(tg2, [D])], final_rows, [p["post2_g"]], [], [],
                                                      n_sums=1, dtypes=[bf16, f32])
    gw["w_ffn_down"] = _matmul("ffn_down_dw", hmid, d_ffn, "tn", out_dtype=bf16)
    (d_gp, d_up), _ = _matmul_then_vjp("ffn_down_dx", d_ffn, w["w_ffn_down"], "nt", _fn_swiglu,
                                       [(gp, [D_FF]), (up, [D_FF])], [bf16, bf16])
    gw["w_ffn_gate"] = _matmul("ffn_gate_dw", d_gp, u2, "tn", out_dtype=bf16)
    gw["w_ffn_up"] = _matmul("ffn_up_dw", d_up, u2, "tn", out_dtype=bf16)
    d_u2_gate = _matmul("ffn_gate_dx", d_gp, w["w_ffn_gate"], "nn")
    (d_yy, d_x_res), (gp_["post1_g"], gp_["pre2_g"]) = _matmul_then_vjp(
        "ffn_up_dx", d_up, w["w_ffn_up"], "nn", _fn_post1, post1_rows, [bf16, f32], params=post1_params,
        first_cts=[d_h1], add=d_u2_gate)
    gw["w_o"] = _matmul("out_o_dw", merged, d_yy, "tn", out_dtype=bf16)
    (d_a_fox, d_a_rwkv, d_a_mem, d_p_g), _ = _matmul_then_vjp("out_o_dx", d_yy, w["w_o"], "nt", _fn_merge, merge_rows,
                                                             [bf16] * 4)
    d_fox_out = _matmul("out_fox_dx", d_a_fox, w["w_fox_out"], "nt")
    gw["w_fox_out"] = _matmul("out_fox_dw", fox_out, d_a_fox, "tn", out_dtype=bf16)
    gw["w_rwkv_out"] = _matmul("out_rwkv_dw", rwkv_out, d_a_rwkv, "tn", out_dtype=bf16)
    d_mem_out = _matmul("out_mem_dx", d_a_mem, w["w_mem_out"], "nt")
    gw["w_mem_out"] = _matmul("out_mem_dw", mem_out, d_a_mem, "tn", out_dtype=bf16)

    d_p_mq, d_km, d_vm = _mem_bwd(p_mq, mem_kv, d_mem_out, batch, seq)
    d_mem_kv = jnp.concatenate([d_km, d_vm], axis=1).astype(bf16)
    gw["w_mem_kv"] = _matmul("proj_memkv_dw", memn, d_mem_kv, "tn", out_dtype=bf16)
    d_memn = _matmul("proj_memkv_dx", d_mem_kv, w["w_mem_kv"], "nt")
    _, (gp_["mem_norm_g"],) = _rows_bwd("rms_mem_bwd", _fn_rms, [], [(mem2, [D])], [p["mem_norm_g"]], [[D]], [d_memn])

    d_q, d_k, d_v, d_cq, d_ck = _fox_bwd(p_qkv, c, c_rows, fox_o, lse, d_fox_out, batch, seq)
    d_p_qkv = jnp.concatenate([d_q, d_k, d_v], axis=1).astype(bf16)
    d_p_f, d_bias = _fox_gate_bwd(p_f, bias, d_cq, d_ck, batch, seq)
    gp_["fox_f_bias"] = d_bias[:, :HEADS]

    (d_y_rw, d_main6_post, d_g_rw), (gp_["rwkv_gn_g"], gp_["rwkv_gn_b"], d_rk) = _matmul_then_vjp(
        "out_rwkv_dx", d_a_rwkv, w["w_rwkv_out"], "nt", fn_post, post_rows, [f32] * 3, params=post_params)
    gp_["rwkv_r_k"] = d_rk.reshape(1, HEADS, HD)
    d_main6, early_got = _scan_bwd(main6, states, d_y_rw, d_main6_post, batch, seq,
                                   side=(early(gw), True) if early else None)

    def fn_pre_sum(*args):
        return _fn_rwkv_pre(*args)

    (d_ps,), d_pre = _rows_bwd("rwkv_pre_bwd", fn_pre_sum, [], [(ps, rw_widths)], pre_params, [six, [HW]],
                               [d_main6, d_g_rw], tm=256)
    gp_["rwkv_w0"], d_w_up, gp_["rwkv_a0"], d_a_up, gw["rwkv_g_up"], gp_["rwkv_k_k"], gp_["rwkv_k_a"] = d_pre
    gw["rwkv_w_up"], gw["rwkv_a_up"] = d_w_up[:64], d_a_up[:64]
    d_p_r, d_mu = _tokshift_bwd(p_r, mu, d_ps, batch, seq)
    gp_["rwkv_mu"] = _unpad_lora(d_mu)

    gw["w_in"] = _merge_w_in(_matmul("proj_qkv_dw", d_p_qkv, u, "tn", out_dtype=bf16), _matmul("proj_f_dw", d_p_f, u, "tn", out_dtype=bf16),
                             _matmul("proj_rwkv_dw", d_p_r, u, "tn", out_dtype=bf16), _matmul("proj_memq_dw", d_p_mq, u, "tn", out_dtype=bf16),
                             _matmul("proj_gate_dw", d_p_g, u, "tn", out_dtype=bf16))
    d_x, gp_["pre1_g"], last_got = _input_cotangent(
        "proj_dx", [d_p_qkv, d_p_f, d_p_r, d_p_mq, d_p_g], [w_qkv, w_f, w_r, w_mq, w_g3], x2, p["pre1_g"], d_x_res,
        side=(last(gw), True) if last else None)
    return loss, d_x.reshape(x.shape), gw, gp_, early_got, last_got


def _adamw(name, recv, row_off, w, m, v):
    _, rows, cols = w.shape
    row_tiles = [t for t in range(16, min(rows, 128) + 1, 16) if rows % t == 0 and row_off % t == 0]
    if row_tiles:
        tr, tc = max(row_tiles), cols
        first, grid = row_off // tr, (rows // tr,)
        at = lambda i: (0, first + i, 0)
        mine = lambda i: (0, i, 0)
    else:
        assert row_off == 0 and recv.shape[1] == rows
        tr, tc = rows, 128
        grid = (cols // tc,)
        at = mine = lambda i: (0, 0, i)

    def body(g_ref, w_ref, m_ref, v_ref, go_ref, d_ref, mo_ref, vo_ref):
        g = g_ref[0].astype(f32)
        for s in range(1, N_DEV):
            g = g + g_ref[s].astype(f32)
        m_new = ADAM_B1 * m_ref[0] + (1.0 - ADAM_B1) * g
        v_new = ADAM_B2 * v_ref[0] + (1.0 - ADAM_B2) * (g * g)
        m_hat = m_new / (1.0 - ADAM_B1 ** ADAM_STEP)
        v_hat = v_new / (1.0 - ADAM_B2 ** ADAM_STEP)
        go_ref[0] = g
        d_ref[0] = -ADAM_LR * (m_hat / (jnp.sqrt(v_hat) + ADAM_EPS) + ADAM_WD * w_ref[0])
        mo_ref[0] = m_new
        vo_ref[0] = v_new

    spec = pl.BlockSpec((1, tr, tc), mine)
    return pl.pallas_call(
        body, name=name, grid=grid,
        in_specs=[pl.BlockSpec((N_DEV, tr, tc), at), spec, spec, spec],
        out_specs=[spec] * 4, out_shape=[jax.ShapeDtypeStruct(w.shape, f32)] * 4,
        compiler_params=_cp(("parallel",)),
    )(recv, w, m, v)


GROUPS = (
    ("in", ("w_in",), 0),
    ("memkv", ("w_mem_kv",), 0),
    ("ffn_gu", ("w_ffn_gate", "w_ffn_up"), 0),
    ("down_o", ("w_ffn_down", "w_o"), 0),
    ("outs", ("w_fox_out", "w_rwkv_out", "w_mem_out"), 0),
    ("lora", ("rwkv_w_up", "rwkv_a_up", "rwkv_g_up"), 0),
)
FIRST_GROUPS = ("in", "memkv")
LATE_GROUPS = (("down_o", "outs", "lora"), ("ffn_gu",))
EARLY_GRAD_GROUPS = ("memkv", "ffn_gu", "down_o", "outs")
LAST_GRAD_GROUPS = ("in", "lora")
SHARD_AXIS = {n: a for n, _, a in SHARDED}
SMALL_ROWS = 16
LOSS_LANES = 128


def _group_local(shards, members, join):
    parts = [shards[n].reshape(shards[n].shape[-2:]) for n in members]
    return parts[0] if len(parts) == 1 else jnp.concatenate(parts, axis=join)


def _group_split(arr, members, join, lead=False):
    out, off = {}, 0
    for n in members:
        shape = dict((k, s) for k, s, _ in SHARDED)[n]
        size = _block_shape(shape, SHARD_AXIS[n])[join]
        idx = [slice(None)] * arr.ndim
        idx[arr.ndim - 2 + join] = slice(off, off + size)
        out[n] = arr[tuple(idx)]
        off += size
    return out


def _full_from_blocks(blocks, axis):
    if axis == 0:
        return blocks.reshape(-1, blocks.shape[2])
    return blocks.transpose(1, 0, 2).reshape(blocks.shape[1], -1)


def _blocks_from_full(full, axis):
    if axis == 0:
        return full.reshape(N_DEV, -1, full.shape[1])
    return full.reshape(full.shape[0], N_DEV, -1).transpose(1, 0, 2)


def _assemble(gathered, names):
    out = {}
    for arr, g in zip(gathered, names):
        _, members, join = [grp for grp in GROUPS if grp[0] == g][0]
        for n, blk in _group_split(arr, members, join, lead=True).items():
            out[n] = _full_from_blocks(blk, SHARD_AXIS[n])
    return out


def _grad_blocks(gw, names):
    out = []
    for g in names:
        _, members, join = [grp for grp in GROUPS if grp[0] == g][0]
        parts = [_blocks_from_full(gw[n].astype(bf16), SHARD_AXIS[n]) for n in members]
        out.append(parts[0] if len(parts) == 1 else jnp.concatenate(parts, axis=1 + join))
    return out


def _small_pack(d):
    flat = jnp.concatenate([d[n].reshape(-1) for n, _ in REPLICATED])
    return jnp.pad(flat, (0, SMALL_ROWS * LANES - REPL_ELEMS)).reshape(SMALL_ROWS, LANES)


def _small_unpack(packed):
    out, flat, off = {}, packed.reshape(-1), 0
    for n, shape in REPLICATED:
        k = _rows_of((LANES,) + shape)
        out[n] = flat[off:off + k].reshape(shape)
        off += k
    return out


def kernel(x, mem, pre1_g, post1_g, pre2_g, post2_g, mem_norm_g, w_in, fox_f_bias, rwkv_mu, rwkv_w0, rwkv_w_up, rwkv_a0, rwkv_a_up, rwkv_g_up, rwkv_k_k, rwkv_k_a, rwkv_r_k, rwkv_gn_g, rwkv_gn_b, w_mem_kv, w_fox_out, w_rwkv_out, w_mem_out, w_o, w_ffn_gate, w_ffn_up, w_ffn_down, loss_target, m_pre1_g, m_post1_g, m_pre2_g, m_post2_g, m_mem_norm_g, m_w_in, m_fox_f_bias, m_rwkv_mu, m_rwkv_w0, m_rwkv_w_up, m_rwkv_a0, m_rwkv_a_up, m_rwkv_g_up, m_rwkv_k_k, m_rwkv_k_a, m_rwkv_r_k, m_rwkv_gn_g, m_rwkv_gn_b, m_w_mem_kv, m_w_fox_out, m_w_rwkv_out, m_w_mem_out, m_w_o, m_w_ffn_gate, m_w_ffn_up, m_w_ffn_down, v_pre1_g, v_post1_g, v_pre2_g, v_post2_g, v_mem_norm_g, v_w_in, v_fox_f_bias, v_rwkv_mu, v_rwkv_w0, v_rwkv_w_up, v_rwkv_a0, v_rwkv_a_up, v_rwkv_g_up, v_rwkv_k_k, v_rwkv_k_a, v_rwkv_r_k, v_rwkv_gn_g, v_rwkv_gn_b, v_w_mem_kv, v_w_fox_out, v_w_rwkv_out, v_w_mem_out, v_w_o, v_w_ffn_gate, v_w_ffn_up, v_w_ffn_down):
    args = dict(locals())
    turn = lambda n, a: jnp.swapaxes(a, 1, 2) if n in TRANSPOSED else a
    wts = {n: turn(n, args[n]) for n in WEIGHT_ORDER}
    ms = {n: turn(n, args["m_" + n]) for n in WEIGHT_ORDER}
    vs = {n: turn(n, args["v_" + n]) for n in WEIGHT_ORDER}

    groups = {g: (members, join) for g, members, join in GROUPS}
    w_bf16 = {n: wts[n].astype(bf16) for n, _, _ in SHARDED}

    def send(g):
        return _group_local(w_bf16, *groups[g])

    first = _exchange("gather_first", [send(g) for g in FIRST_GROUPS], per_peer=False)
    full = _assemble(first, FIRST_GROUPS)
    small_in = {n: (wts[n] if n == "rwkv_r_k" else wts[n].reshape(wts[n].shape[-2:])) for n, _ in REPLICATED}
    late = ([send(g) for g in LATE_GROUPS[0]], [send(g) for g in LATE_GROUPS[1]],
            lambda got, which: _assemble(got, LATE_GROUPS[which]))
    loss_part, grad_x, gw, gp, early_got, last_got = _local_step(
        x, mem, loss_target, full, small_in, late=late, early=lambda g: _grad_blocks(g, EARLY_GRAD_GROUPS),
        last=lambda g: _grad_blocks(g, LAST_GRAD_GROUPS))
    small_got, loss_got = _exchange("exchange_small", [_small_pack(gp).astype(bf16), jnp.broadcast_to(loss_part, (8, LOSS_LANES))],
                                    per_peer=False)
    received = dict(zip(EARLY_GRAD_GROUPS + LAST_GRAD_GROUPS, list(early_got) + list(last_got)))

    outs = [{}, {}, {}, {}]
    for g, members, _ in GROUPS:
        off = 0
        for n in members:
            for o, arr in zip(outs, _adamw("adamw_" + n, received[g], off, wts[n], ms[n], vs[n])):
                o[n] = arr
            off += wts[n].shape[1]
    res = _adamw("adamw_small", small_got, 0, *[_small_pack(d)[None] for d in (wts, ms, vs)])
    for o, arr in zip(outs, res):
        o.update(_small_unpack(arr))
    loss = jnp.sum(loss_got[:, 0, 0])
    return (loss, grad_x, *[turn(n, o[n].reshape(wts[n].shape)) for o in outs for n in WEIGHT_ORDER])
```

```python
import functools

import jax
import jax.numpy as jnp
from jax import lax
from jax.experimental import pallas as pl
from jax.experimental.pallas import tpu as pltpu

f32 = jnp.float32
bf16 = jnp.bfloat16
_HI = lax.Precision.HIGHEST

D = 1024
HEADS = 8
HD = 64
HW = HEADS * HD
MEM_HEADS = 4
MEM_HD = 128
MEM_W = 512
MEM_LEN = 256
D_FF = 2816
LORA_PAD = 128
NORM_EPS = 1e-6
GN_EPS = 64e-5
SCAN_CHUNK = 64
N_DEV = 8
LANES = 1024
VMEM_LIMIT = 56 * 1024 * 1024

ADAM_LR = 0.001
ADAM_B1 = 0.9
ADAM_B2 = 0.999
ADAM_EPS = 1e-08
ADAM_WD = 0.01
ADAM_STEP = 10

TRANSPOSED = ("w_in", "w_ffn_gate", "w_ffn_up")
SHARDED = (
    ("w_in", (6920, 1024), 0),
    ("w_ffn_gate", (2816, 1024), 0),
    ("w_ffn_up", (2816, 1024), 0),
    ("w_ffn_down", (2816, 1024), 0),
    ("w_mem_kv", (1024, 1024), 0),
    ("w_o", (1024, 1024), 0),
    ("w_fox_out", (512, 1024), 1),
    ("w_rwkv_out", (512, 1024), 1),
    ("w_mem_out", (512, 1024), 1),
    ("rwkv_w_up", (64, 512), 1),
    ("rwkv_a_up", (64, 512), 1),
    ("rwkv_g_up", (128, 512), 1),
)
REPLICATED = (
    ("pre1_g", (1, 1024)), ("post1_g", (1, 1024)), ("pre2_g", (1, 1024)), ("post2_g", (1, 1024)),
    ("mem_norm_g", (1, 1024)), ("fox_f_bias", (1, 8)), ("rwkv_mu", (1, 1792)), ("rwkv_w0", (1, 512)),
    ("rwkv_a0", (1, 512)), ("rwkv_k_k", (1, 512)), ("rwkv_k_a", (1, 512)), ("rwkv_r_k", (1, 8, 64)),
    ("rwkv_gn_g", (1, 512)), ("rwkv_gn_b", (1, 512)),
)
WEIGHT_ORDER = ('pre1_g', 'post1_g', 'pre2_g', 'post2_g', 'mem_norm_g', 'w_in', 'fox_f_bias', 'rwkv_mu',
                'rwkv_w0', 'rwkv_w_up', 'rwkv_a0', 'rwkv_a_up', 'rwkv_g_up', 'rwkv_k_k', 'rwkv_k_a',
                'rwkv_r_k', 'rwkv_gn_g', 'rwkv_gn_b', 'w_mem_kv', 'w_fox_out', 'w_rwkv_out', 'w_mem_out',
                'w_o', 'w_ffn_gate', 'w_ffn_up', 'w_ffn_down')


def _block_shape(shape, axis):
    return tuple(s // N_DEV if i == axis else s for i, s in enumerate(shape))


def _rows_of(shape):
    n = 1
    for s in shape:
        n *= s
    return n // LANES


REPL_ELEMS = sum(_rows_of((LANES,) + s) for _, s in REPLICATED)


def _cp(sem=None):
    return pltpu.CompilerParams(dimension_semantics=sem, vmem_limit_bytes=VMEM_LIMIT)


def _tile(dim, cap):
    best = None
    for t in range(128, min(dim, cap) + 1, 128):
        if dim % t == 0:
            best = t
    return best if best is not None else dim


def _two_terms(x):
    hi = x.astype(bf16)
    return hi, (x - hi.astype(f32)).astype(bf16)


def _dg(a, b, dims, exact):
    if exact == "split":
        (a_hi, a_lo), (b_hi, b_lo) = _two_terms(a), _two_terms(b)
        dot = functools.partial(lax.dot_general, dimension_numbers=dims, preferred_element_type=f32)
        return dot(a_hi, b_hi) + (dot(a_hi, b_lo) + dot(a_lo, b_hi))
    if exact:
        return lax.dot_general(a, b, dims, precision=_HI, preferred_element_type=f32)
    return lax.dot_general(a.astype(bf16), b.astype(bf16), dims, preferred_element_type=f32)


def _make_mm(batched, exact):
    o = 1 if batched else 0
    bd = ((0,), (0,)) if batched else ((), ())
    d_nn = (((1 + o,), (o,)), bd)
    d_nt = (((1 + o,), (1 + o,)), bd)
    d_tn = (((o,), (o,)), bd)

    @jax.custom_vjp
    def nn(a, b):
        return _dg(a, b, d_nn, exact)

    @jax.custom_vjp
    def nt(a, b):
        return _dg(a, b, d_nt, exact)

    @jax.custom_vjp
    def tn(a, b):
        return _dg(a, b, d_tn, exact)

    nn.defvjp(lambda a, b: (_dg(a, b, d_nn, exact), (a, b)),
              lambda res, g: (_dg(g, res[1], d_nt, exact), _dg(res[0], g, d_tn, exact)))
    nt.defvjp(lambda a, b: (_dg(a, b, d_nt, exact), (a, b)),
              lambda res, g: (_dg(g, res[1], d_nn, exact), _dg(g, res[0], d_tn, exact)))
    tn.defvjp(lambda a, b: (_dg(a, b, d_tn, exact), (a, b)),
              lambda res, g: (_dg(res[1], g, d_nt, exact), _dg(res[0], g, d_nn, exact)))
    return nn, nt, tn


def _sigmoid(x):
    return 1.0 / (1.0 + jnp.exp(-x))


def _head_sum_raw(x):
    width = 2 * HD
    i = lax.broadcasted_iota(jnp.int32, (width, width), 0) // HD
    j = lax.broadcasted_iota(jnp.int32, (width, width), 1) // HD
    m = (i == j).astype(bf16)
    dims = (((1,), (0,)), ((), ()))
    out = []
    for p in range(x.shape[1] // width):
        xp = x[:, p * width:(p + 1) * width]
        hi = xp.astype(bf16)
        lo = (xp - hi.astype(f32)).astype(bf16)
        out.append(lax.dot_general(hi, m, dims, preferred_element_type=f32)
                   + lax.dot_general(lo, m, dims, preferred_element_type=f32))
    return jnp.concatenate(out, axis=1)


@jax.custom_vjp
def _head_sum(x):
    return _head_sum_raw(x)


_head_sum.defvjp(lambda x: (_head_sum_raw(x), None), lambda _, g: (_head_sum_raw(g),))


WEIGHT_TILE_BYTES = 13 * 512 * 1024
ACC_TILE_BYTES = 8 * 1024 * 1024


def _lazy(fn, rows, width, params=()):
    return (fn, rows, width, list(params))


def _matmul(name, a, b, mode, add=None, out_dtype=f32):
    has_add = add is not None
    if isinstance(a, tuple):
        a_fn, a_rows, a_width, a_params = a
        a_arrays = [r for r, _ in a_rows]
        a_shape = (a_arrays[0].shape[0], a_width)
    else:
        a_fn, a_rows, a_params, a_arrays, a_shape = None, None, [], [a], a.shape
    n_r = len(a_arrays)
    n_a = n_r + len(a_params)

    def load_a(refs):
        if a_fn is None:
            return refs[0][...].astype(bf16)
        pieces = []
        for r, (_, widths) in zip(refs[:n_r], a_rows):
            pieces += _pieces(r, widths)
        return a_fn(*pieces, *[p[...] for p in refs[n_r:]])[0].astype(bf16)

    if mode == "tn":
        assert a_fn is None
        (k, m), (_, n) = a_shape, b.shape
        tn = _tile(n, max(128, ACC_TILE_BYTES // (4 * m)))
        tk = _tile(k, 2048)
        nk = k // tk

        def body(*refs):
            b_ref, o_ref, acc = refs[n_a:]

            @pl.when(pl.program_id(1) == 0)
            def _():
                acc[...] = jnp.zeros_like(acc)

            acc[...] += lax.dot_general(load_a(refs[:n_a]), b_ref[...].astype(bf16),
                                        (((0,), (0,)), ((), ())), preferred_element_type=f32)

            @pl.when(pl.program_id(1) == nk - 1)
            def _():
                o_ref[...] = acc[...].astype(o_ref.dtype)

        return pl.pallas_call(
            body, name=name, grid=(n // tn, nk),
            in_specs=[pl.BlockSpec((tk, r.shape[1]), lambda j, kk: (kk, 0)) for r in a_arrays]
            + [pl.BlockSpec((tk, tn), lambda j, kk: (kk, j))],
            out_specs=pl.BlockSpec((m, tn), lambda j, kk: (0, j)), out_shape=jax.ShapeDtypeStruct((m, n), out_dtype),
            scratch_shapes=[pltpu.VMEM((m, tn), f32)],
            compiler_params=_cp(("parallel", "arbitrary")),
        )(*a_arrays, b)

    (m, k) = a_shape
    n = b.shape[1] if mode == "nn" else b.shape[0]
    tm = _tile(m, 1024 if a_fn is None else 512)
    tn = _tile(n, max(128, WEIGHT_TILE_BYTES // (2 * k)))
    dims = (((1,), (0,)), ((), ())) if mode == "nn" else (((1,), (1,)), ((), ()))
    b_spec = pl.BlockSpec((k, tn), lambda j, i: (0, j)) if mode == "nn" else pl.BlockSpec((tn, k), lambda j, i: (j, 0))
    o_spec = pl.BlockSpec((tm, tn), lambda j, i: (i, j))

    keep = a_fn is not None
    assert not keep or tn == n

    def body(*refs):
        b_ref = refs[n_a]
        a_val = load_a(refs[:n_a])
        r = lax.dot_general(a_val, b_ref[...].astype(bf16), dims, preferred_element_type=f32)
        if has_add:
            r = r + refs[n_a + 1][...]
        if keep:
            refs[-2][...] = r.astype(refs[-2].dtype)
            refs[-1][...] = a_val
        else:
            refs[-1][...] = r.astype(refs[-1].dtype)

    res = pl.pallas_call(
        body, name=name, grid=(n // tn, m // tm),
        in_specs=[pl.BlockSpec((tm, r.shape[1]), lambda j, i: (i, 0)) for r in a_arrays]
        + [pl.BlockSpec(p.shape, lambda j, i: (0, 0)) for p in a_params] + [b_spec] + ([o_spec] if has_add else []),
        out_specs=[o_spec] + ([pl.BlockSpec((tm, k), lambda j, i: (i, 0))] if keep else []),
        out_shape=[jax.ShapeDtypeStruct((m, n), out_dtype)] + ([jax.ShapeDtypeStruct((m, k), bf16)] if keep else []),
        compiler_params=_cp(("parallel", "arbitrary")),
    )(*a_arrays, *a_params, b, *([add] if has_add else []))
    return tuple(res) if keep else res[0]


def _input_cotangent(name, a_list, b_list, x, gain, add, side=None):
    m = a_list[0].shape[0]
    tm = _tile(m, 256)
    n_g = len(a_list)
    srcs, per_peer = side if side is not None else ([], False)
    n_s = len(srcs)

    def body(*refs):
        x_ref, g_ref, add_ref = refs[2 * n_g:2 * n_g + 3]
        src_refs = refs[2 * n_g + 3:2 * n_g + 3 + n_s]
        dx_ref, dg_ref = refs[2 * n_g + 3 + n_s:2 * n_g + 5 + n_s]
        _side_exchange(src_refs, refs[2 * n_g + 5 + n_s:2 * n_g + 5 + 2 * n_s], per_peer, refs[2 * n_g + 5 + 2 * n_s:], m // tm)
        d_u = None
        for g in range(n_g):
            r = lax.dot_general(refs[g][...].astype(bf16), refs[n_g + g][...].astype(bf16), (((1,), (0,)), ((), ())),
                                preferred_element_type=f32)
            d_u = r if d_u is None else d_u + r
        _, vjp = jax.vjp(_rms, x_ref[...], g_ref[...])
        d_x, d_gain = vjp(d_u)
        dx_ref[...] = d_x + add_ref[...]

        @pl.when(pl.program_id(0) == 0)
        def _():
            dg_ref[...] = jnp.zeros_like(dg_ref)

        dg_ref[...] += d_gain

    rows = pl.BlockSpec((tm, x.shape[1]), lambda i: (i, 0))
    whole = lambda b: pl.BlockSpec(b.shape, lambda i: (0, 0))
    res = pl.pallas_call(
        body, name=name, grid=(m // tm,),
        in_specs=[pl.BlockSpec((tm, a.shape[1]), lambda i: (i, 0)) for a in a_list] + [whole(b) for b in b_list]
        + [rows, whole(gain), rows] + [_HBM_SPEC] * n_s,
        out_specs=[rows, whole(gain)] + [_HBM_SPEC] * n_s,
        out_shape=[jax.ShapeDtypeStruct(x.shape, f32), jax.ShapeDtypeStruct(gain.shape, f32)] + _side_out_shapes(srcs, per_peer),
        scratch_shapes=_side_sems(n_s),
        compiler_params=_cp(("arbitrary",)),
    )(*a_list, *b_list, x, gain, add, *srcs)
    return res[0], res[1], list(res[2:])


def _pieces(ref, widths):
    out, off = [], 0
    for w in widths:
        out.append(ref[:, off:off + w].astype(f32))
        off += w
    return out


def _store_pieces(ref, widths, vals, add_ref=None):
    off = 0
    for w, v in zip(widths, vals):
        ref[:, off:off + w] = (v if add_ref is None else v + add_ref[:, off:off + w]).astype(ref.dtype)
        off += w


def _rows_fwd(name, fn, consts, rows, params, outs, n_sums=0, tm=512, dtypes=None):
    t = (consts + rows)[0][0].shape[0]
    tm = min(tm, t)
    ins = consts + rows
    n_in, n_p, n_o = len(ins), len(params), len(outs)
    dtypes = dtypes or [f32] * n_o

    def body(*refs):
        in_refs, p_refs = refs[:n_in], refs[n_in:n_in + n_p]
        o_refs, s_refs = refs[n_in + n_p:n_in + n_p + n_o], refs[n_in + n_p + n_o:]
        vals = []
        for r, (_, widths) in zip(in_refs, ins):
            vals += _pieces(r, widths)
        res = fn(*vals, *[p[...] for p in p_refs])
        pos = 0
        for r, widths in zip(o_refs, outs):
            _store_pieces(r, widths, res[pos:pos + len(widths)])
            pos += len(widths)

        @pl.when(pl.program_id(0) == 0)
        def _():
            for s in s_refs:
                s[...] = jnp.zeros_like(s)

        for s, v in zip(s_refs, res[pos:]):
            s[...] += v

    row_spec = lambda w: pl.BlockSpec((tm, w), lambda i: (i, 0))
    full = lambda p: pl.BlockSpec(p.shape, lambda i: (0,) * p.ndim)
    return pl.pallas_call(
        body, name=name, grid=(t // tm,),
        in_specs=[row_spec(sum(w)) for _, w in ins] + [full(p) for p in params],
        out_specs=[row_spec(sum(w)) for w in outs] + [pl.BlockSpec((1, 1), lambda i: (0, 0))] * n_sums,
        out_shape=[jax.ShapeDtypeStruct((t, sum(w)), dt) for w, dt in zip(outs, dtypes)] + [jax.ShapeDtypeStruct((1, 1), f32)] * n_sums,
        compiler_params=_cp(("arbitrary",)),
    )(*[a for a, _ in ins], *params)


def _rows_bwd(name, fn, consts, rows, params, outs, cts, n_sums=0, add=None, tm=512, dtypes=None):
    t = (consts + rows)[0][0].shape[0]
    tm = min(tm, t)
    n_c, n_r, n_p, n_o = len(consts), len(rows), len(params), len(outs)
    has_add = add is not None
    dtypes = dtypes or [f32] * n_r

    def body(*refs):
        pos = 0
        c_refs = refs[pos:pos + n_c]; pos += n_c
        r_refs = refs[pos:pos + n_r]; pos += n_r
        p_refs = refs[pos:pos + n_p]; pos += n_p
        ct_refs = refs[pos:pos + n_o]; pos += n_o
        add_ref = refs[pos] if has_add else None
        pos += 1 if has_add else 0
        dr_refs = refs[pos:pos + n_r]; pos += n_r
        dp_refs = refs[pos:pos + n_p]; pos += n_p
        s_refs = refs[pos:pos + n_sums]
        cvals, rvals = [], []
        for r, (_, widths) in zip(c_refs, consts):
            cvals += _pieces(r, widths)
        for r, (_, widths) in zip(r_refs, rows):
            rvals += _pieces(r, widths)
        pvals = [p[...] for p in p_refs]
        ctv = []
        for r, widths in zip(ct_refs, outs):
            ctv += _pieces(r, widths)
        ctv += [jnp.ones((1, 1), f32)] * n_sums
        primal, vjp = jax.vjp(lambda *rp: tuple(fn(*cvals, *rp)), *rvals, *pvals)
        g = vjp(tuple(ctv))
        pos = 0
        for idx, (r, (_, widths)) in enumerate(zip(dr_refs, rows)):
            _store_pieces(r, widths, g[pos:pos + len(widths)], add_ref if idx == 0 else None)
            pos += len(widths)

        @pl.when(pl.program_id(0) == 0)
        def _():
            for acc in list(dp_refs) + list(s_refs):
                acc[...] = jnp.zeros_like(acc)

        for dp, v in zip(dp_refs, g[pos:]):
            dp[...] += v
        for s, v in zip(s_refs, primal[len(primal) - n_sums:]):
            s[...] += v

    row_spec = lambda w: pl.BlockSpec((tm, w), lambda i: (i, 0))
    full = lambda p: pl.BlockSpec(p.shape, lambda i: (0,) * p.ndim)
    args = [a for a, _ in consts + rows] + list(params) + list(cts) + ([add] if has_add else [])
    res = pl.pallas_call(
        body, name=name, grid=(t // tm,),
        in_specs=[row_spec(sum(w)) for _, w in consts + rows] + [full(p) for p in params]
        + [row_spec(sum(w)) for w in outs] + ([row_spec(add.shape[1])] if has_add else []),
        out_specs=[row_spec(sum(w)) for _, w in rows] + [full(p) for p in params]
        + [pl.BlockSpec((1, 1), lambda i: (0, 0))] * n_sums,
        out_shape=[jax.ShapeDtypeStruct((t, sum(w)), dt) for (_, w), dt in zip(rows, dtypes)]
        + [jax.ShapeDtypeStruct(p.shape, f32) for p in params] + [jax.ShapeDtypeStruct((1, 1), f32)] * n_sums,
        compiler_params=_cp(("arbitrary",)),
    )(*args)
    return res[:n_r], res[n_r:n_r + n_p] + res[n_r + n_p:]


def _matmul_then_vjp(name, a, b, mode, fn, rows, dtypes, params=(), first_cts=(), add=None, tm=256):
    m, k = a.shape
    tm = min(tm, m)
    dims = (((1,), (0,)), ((), ())) if mode == "nn" else (((1,), (1,)), ((), ()))
    n_r, n_p, n_c = len(rows), len(params), len(first_cts)
    has_add = add is not None

    def body(*refs):
        a_ref, b_ref = refs[:2]
        pos = 2
        r_refs = refs[pos:pos + n_r]; pos += n_r
        p_refs = refs[pos:pos + n_p]; pos += n_p
        c_refs = refs[pos:pos + n_c]; pos += n_c
        add_ref = refs[pos] if has_add else None
        pos += 1 if has_add else 0
        dr_refs = refs[pos:pos + n_r]; pos += n_r
        dp_refs = refs[pos:pos + n_p]
        ct = lax.dot_general(a_ref[...].astype(bf16), b_ref[...].astype(bf16), dims, preferred_element_type=f32)
        if has_add:
            ct = ct + add_ref[...]
        rvals = []
        for r, (_, widths) in zip(r_refs, rows):
            rvals += _pieces(r, widths)
        _, vjp = jax.vjp(lambda *rp: tuple(fn(*rp)), *rvals, *[p[...] for p in p_refs])
        g = vjp(tuple(c[...].astype(f32) for c in c_refs) + (ct,))
        pos = 0
        for r, (_, widths) in zip(dr_refs, rows):
            _store_pieces(r, widths, g[pos:pos + len(widths)])
            pos += len(widths)

        @pl.when(pl.program_id(0) == 0)
        def _():
            for dp in dp_refs:
                dp[...] = jnp.zeros_like(dp)

        for dp, v in zip(dp_refs, g[pos:]):
            dp[...] += v

    row_spec = lambda w: pl.BlockSpec((tm, w), lambda i: (i, 0))
    whole = lambda p: pl.BlockSpec(p.shape, lambda i: (0, 0))
    res = pl.pallas_call(
        body, name=name, grid=(m // tm,),
        in_specs=[row_spec(k), whole(b)] + [row_spec(sum(w)) for _, w in rows] + [whole(p) for p in params]
        + [row_spec(c.shape[1]) for c in first_cts] + ([row_spec(add.shape[1])] if has_add else []),
        out_specs=[row_spec(sum(w)) for _, w in rows] + [whole(p) for p in params],
        out_shape=[jax.ShapeDtypeStruct((m, sum(w)), dt) for (_, w), dt in zip(rows, dtypes)]
        + [jax.ShapeDtypeStruct(p.shape, f32) for p in params],
        compiler_params=_cp(("arbitrary",)),
    )(a, b, *[r for r, _ in rows], *params, *first_cts, *([add] if has_add else []))
    return res[:n_r], res[n_r:]


def _rms(x, g):
    return x * lax.rsqrt(jnp.mean(x * x, axis=-1, keepdims=True) + NORM_EPS) * g


def _fn_rms(x, g):
    return (_rms(x, g),)


def _fn_rwkv_pre(r, k, v, wd, ad, gd, w0, w_up, a0, a_up, g_up, k_k, k_a):
    nn, _, _ = _make_mm(False, False)
    w_log = -_sigmoid(w0 + nn(jnp.tanh(wd), w_up)) * 0.6065306597126334
    a = _sigmoid(a0 + nn(ad, a_up))
    g = nn(_sigmoid(gd), g_up)
    kk = k * k_k
    kk = kk * lax.rsqrt(jnp.maximum(_head_sum(kk * kk), 1e-24))
    k2 = k * (1.0 + (a - 1.0) * k_a)
    return r, k2, v, w_log, -kk, kk * a, g


def _fn_rwkv_post(y, r, k2, v, g, gn_g, gn_b, r_k):
    mean = _head_sum(y) * (1.0 / HD)
    yc = y - mean
    var = _head_sum(yc * yc) * (1.0 / HD)
    yn = yc * lax.rsqrt(var + GN_EPS) * gn_g + gn_b
    bonus = _head_sum(r * k2 * r_k) * v
    return ((yn + bonus) * g,)


def _fn_merge(a_fox, a_rwkv, a_mem, g_fox, g_rwkv, g_mem):
    return (_sigmoid(g_fox) * a_fox + _sigmoid(g_rwkv) * a_rwkv + _sigmoid(g_mem) * a_mem,)


def _fn_post1(y, x, post1_g, pre2_g):
    h1 = x + _rms(y, post1_g)
    return h1, _rms(h1, pre2_g)


def _fn_swiglu(gp, up):
    return (gp * _sigmoid(gp) * up,)


def _fn_final(target, ffn, h1, post2_g):
    err = h1 + _rms(ffn, post2_g) - target
    per_row = jnp.mean(err * err, axis=-1, keepdims=True)
    return (0.5 * jnp.sum(per_row, axis=0, keepdims=True),)


def _shift_down(x):
    row = lax.broadcasted_iota(jnp.int32, x.shape, 0)
    return jnp.where(row == 0, 0.0, pltpu.roll(x, 1, 0))


def _shift_up(x):
    s = x.shape[0]
    row = lax.broadcasted_iota(jnp.int32, x.shape, 0)
    return jnp.where(row == s - 1, 0.0, pltpu.roll(x, s - 1, 0))


def _tokshift_fwd(p, mu, batch, seq):
    w = p.shape[1]
    tc = _tile(w, 384)

    def body(p_ref, mu_ref, o_ref):
        x = p_ref[...].astype(f32)
        o_ref[...] = x + (_shift_down(x) - x) * mu_ref[...]

    return pl.pallas_call(
        body, name="tokshift_fwd", grid=(w // tc, batch),
        in_specs=[pl.BlockSpec((seq, tc), lambda j, b: (b, j)), pl.BlockSpec((1, tc), lambda j, b: (0, j))],
        out_specs=pl.BlockSpec((seq, tc), lambda j, b: (b, j)),
        out_shape=jax.ShapeDtypeStruct(p.shape, f32),
        compiler_params=_cp(("parallel", "arbitrary")),
    )(p, mu)


def _tokshift_bwd(p, mu, dps, batch, seq):
    w = p.shape[1]
    tc = _tile(w, 384)

    def body(p_ref, mu_ref, d_ref, dp_ref, dmu_ref):
        x, mu_v, d = p_ref[...].astype(f32), mu_ref[...], d_ref[...]
        dp_ref[...] = (d * (1.0 - mu_v) + _shift_up(d * mu_v)).astype(dp_ref.dtype)

        @pl.when(pl.program_id(1) == 0)
        def _():
            dmu_ref[...] = jnp.zeros_like(dmu_ref)

        dmu_ref[...] += jnp.sum(d * (_shift_down(x) - x), axis=0, keepdims=True)

    return pl.pallas_call(
        body, name="tokshift_bwd", grid=(w // tc, batch),
        in_specs=[pl.BlockSpec((seq, tc), lambda j, b: (b, j)), pl.BlockSpec((1, tc), lambda j, b: (0, j)),
                  pl.BlockSpec((seq, tc), lambda j, b: (b, j))],
        out_specs=[pl.BlockSpec((seq, tc), lambda j, b: (b, j)), pl.BlockSpec((1, tc), lambda j, b: (0, j))],
        out_shape=[jax.ShapeDtypeStruct(p.shape, bf16), jax.ShapeDtypeStruct(mu.shape, f32)],
        compiler_params=_cp(("parallel", "arbitrary")),
    )(p, mu, dps)


def _cum_block(seq):
    return _tile(seq, 256)


def _fox_gate_fwd(f, bias, batch, seq):
    cb = _cum_block(seq)

    def body(f_ref, b_ref, c_ref):
        row = lax.broadcasted_iota(jnp.int32, (cb, cb), 0)
        col = lax.broadcasted_iota(jnp.int32, (cb, cb), 1)
        tri = (col <= row).astype(f32)
        carry = jnp.zeros((1, 128), f32)
        for i in range(seq // cb):
            z = f_ref[i * cb:(i + 1) * cb, :] + b_ref[...]
            ls = jnp.minimum(z, 0.0) - jnp.log(1.0 + jnp.exp(-jnp.abs(z)))
            c = _dg(tri, ls, (((1,), (0,)), ((), ())), True) + carry
            c_ref[i * cb:(i + 1) * cb, :] = c
            carry = c[cb - 1:cb, :]

    return pl.pallas_call(
        body, name="fox_gate_fwd", grid=(batch,),
        in_specs=[pl.BlockSpec((seq, 128), lambda b: (b, 0)), pl.BlockSpec((1, 128), lambda b: (0, 0))],
        out_specs=pl.BlockSpec((seq, 128), lambda b: (b, 0)),
        out_shape=jax.ShapeDtypeStruct(f.shape, f32),
        compiler_params=_cp(("arbitrary",)),
    )(f, bias)


def _fox_gate_bwd(f, bias, dc_a, dc_b, batch, seq):
    cb = _cum_block(seq)

    def body(f_ref, b_ref, da_ref, db_ref, df_ref, dbias_ref):
        row = lax.broadcasted_iota(jnp.int32, (cb, cb), 0)
        col = lax.broadcasted_iota(jnp.int32, (cb, cb), 1)
        triu = (col >= row).astype(f32)

        @pl.when(pl.program_id(0) == 0)
        def _():
            dbias_ref[...] = jnp.zeros_like(dbias_ref)

        lane = lax.broadcasted_iota(jnp.int32, (1, 128), 1)

        def by_head(blk):
            out = jnp.zeros((cb, 128), f32)
            for p in range(HEADS // 2):
                for e in range(2):
                    out = jnp.where(lane == 2 * p + e, _pick_lane(blk[:, p * 128:(p + 1) * 128], e), out)
            return out

        carry = jnp.zeros((1, 128), f32)
        tot = jnp.zeros((1, 128), f32)
        for i in reversed(range(seq // cb)):
            sl = slice(i * cb, (i + 1) * cb)
            dc = by_head(da_ref[sl, :] + db_ref[sl, :])
            dls = _dg(triu, dc, (((1,), (0,)), ((), ())), True) + carry
            carry = dls[0:1, :]
            df = dls * _sigmoid(-(f_ref[sl, :] + b_ref[...]))
            df_ref[sl, :] = df.astype(df_ref.dtype)
            tot = tot + jnp.sum(df, axis=0, keepdims=True)
        dbias_ref[...] += tot

    return pl.pallas_call(
        body, name="fox_gate_bwd", grid=(batch,),
        in_specs=[pl.BlockSpec((seq, 128), lambda b: (b, 0)), pl.BlockSpec((1, 128), lambda b: (0, 0)),
                  pl.BlockSpec((seq, HW), lambda b: (b, 0)), pl.BlockSpec((seq, HW), lambda b: (b, 0))],
        out_specs=[pl.BlockSpec((seq, 128), lambda b: (b, 0)), pl.BlockSpec((1, 128), lambda b: (0, 0))],
        out_shape=[jax.ShapeDtypeStruct(f.shape, bf16), jax.ShapeDtypeStruct((1, 128), f32)],
        compiler_params=_cp(("arbitrary",)),
    )(f, bias, dc_a, dc_b)


_HBM_SPEC = pl.BlockSpec(memory_space=pltpu.HBM)


def _side_out_shapes(srcs, per_peer):
    return [jax.ShapeDtypeStruct(((N_DEV,) + tuple(s.shape[1:] if per_peer else s.shape)), s.dtype) for s in srcs]


def _side_sems(n):
    if n == 0:
        return []
    return [pltpu.SemaphoreType.DMA((n, N_DEV - 1)), pltpu.SemaphoreType.DMA((n, N_DEV - 1)), pltpu.SemaphoreType.DMA((n,))]


def _peer_copies(src_refs, dst_refs, per_peer, sems):
    send_sems, recv_sems, local_sems = sems
    x, y, c = lax.axis_index("x"), lax.axis_index("y"), lax.axis_index("c")
    me = 4 * x + 2 * y + c

    def remote(src, dst, t, k, to):
        return pltpu.make_async_remote_copy(src_ref=src, dst_ref=dst, send_sem=send_sems.at[t, k - 1],
                                            recv_sem=recv_sems.at[t, k - 1], device_id=to,
                                            device_id_type=pl.DeviceIdType.MESH)

    direct, relays = [], []
    for t, (s, d) in enumerate(zip(src_refs, dst_refs)):
        direct.append((t, 0, pltpu.make_async_copy(s.at[me] if per_peer else s, d.at[me], local_sems.at[t])))
        for k in range(1, N_DEV):
            px = 1 - x if k & 4 else x
            py = 1 - y if k & 2 else y
            pc = 1 - c if k & 1 else c
            if per_peer:
                direct.append((t, k, remote(s.at[4 * px + 2 * py + pc], d.at[me], t, k, (px, py, pc))))
            elif k == 1 or not k & 1:
                direct.append((t, k, remote(s, d.at[me], t, k, (px, py, pc))))
            else:
                origin = d.at[4 * px + 2 * py + c]
                relays.append((t, k - 1, remote(origin, origin, t, k, (x, y, 1 - c))))
    return direct, relays


def _exchange_start(direct):
    for _, _, cp in direct:
        cp.start()


def _exchange_relay(direct, relays):
    landed = {(t, k): cp for t, k, cp in direct}
    for t, j, cp in relays:
        landed[(t, j)].wait_recv()
        cp.start()


def _exchange_finish(direct, relays):
    relayed = {(t, j) for t, j, _ in relays}
    for t, k, cp in direct:
        if k == 0:
            cp.wait()
        else:
            cp.wait_send()
            if (t, k) not in relayed:
                cp.wait_recv()
    for _, _, cp in relays:
        cp.wait()


def _side_exchange(src_refs, dst_refs, per_peer, sems, *grid):
    if not src_refs:
        return
    step, total = 0, 1
    for a, n in enumerate(grid):
        step, total = step * n + pl.program_id(a), total * n

    @pl.when(step == 0)
    def _():
        _exchange_start(_peer_copies(src_refs, dst_refs, per_peer, sems)[0])

    @pl.when(step == (3 * total) // 4)
    def _():
        _exchange_relay(*_peer_copies(src_refs, dst_refs, per_peer, sems))

    @pl.when(step == total - 1)
    def _():
        _exchange_finish(*_peer_copies(src_refs, dst_refs, per_peer, sems))


def _exchange(name, srcs, per_peer):
    n = len(srcs)

    def body(*refs):
        direct, relays = _peer_copies(refs[:n], refs[n:2 * n], per_peer, refs[2 * n:])
        _exchange_start(direct)
        _exchange_relay(direct, relays)
        _exchange_finish(direct, relays)

    return pl.pallas_call(
        body, name=name, in_specs=[_HBM_SPEC] * n, out_specs=[_HBM_SPEC] * n,
        out_shape=_side_out_shapes(srcs, per_peer), scratch_shapes=_side_sems(n),
    )(*srcs)


FOX_T = 512
_NEG = -1e30
_D2 = (((1,), (1,)), ((), ()))
_D1 = (((1,), (0,)), ((), ()))
_D0 = (((0,), (0,)), ((), ()))


def _bdot(a, b, dims):
    return lax.dot_general(a.astype(bf16), b.astype(bf16), dims, preferred_element_type=f32)


def _pick_lane(x, lane):
    idx = lax.broadcasted_iota(jnp.int32, x.shape, 1)
    return jnp.sum(jnp.where(idx == lane, x, 0.0), axis=1, keepdims=True)


def _pick_row(x, row):
    idx = lax.broadcasted_iota(jnp.int32, x.shape, 0)
    return jnp.sum(jnp.where(idx == row, x, 0.0), axis=0, keepdims=True)


def _fox_fwd(qkv, c, c_rows, batch, seq, side=None):
    t = min(FOX_T, seq)
    nq = seq // t
    scale = HD ** -0.5
    srcs, per_peer = side if side is not None else ([], False)
    n_s = len(srcs)

    def body(*refs):
        q_ref, k_ref, v_ref, cq_ref, ck_ref = refs[:5]
        o_ref, lse_ref = refs[5 + n_s:7 + n_s]
        _side_exchange(refs[5:5 + n_s], refs[7 + n_s:7 + 2 * n_s], per_peer, refs[7 + 2 * n_s:], batch, PAIRS, nq)
        pair, i = pl.program_id(1), pl.program_id(2)
        lane = lax.broadcasted_iota(jnp.int32, (1, PAIR_W), 1)
        first = (lane // HD) == 0
        mine = [first, jnp.logical_not(first)]
        q = q_ref[...] * scale
        qs = [jnp.where(mine[e], q, 0.0) for e in range(2)]
        cqs = [_pick_lane(cq_ref[...], 2 * pair + e) for e in range(2)]
        causal = lax.broadcasted_iota(jnp.int32, (t, t), 1) <= lax.broadcasted_iota(jnp.int32, (t, t), 0)

        def block(j, carry, diagonal):
            rows = pl.ds(pl.multiple_of(j * t, t), t)
            kj, vj = k_ref[rows, :], v_ref[rows, :]
            ck_blk = ck_ref[0, :, rows]
            out = []
            for e in range(2):
                m, acc = carry[2 * e:2 * e + 2]
                s = _bdot(qs[e], kj, _D2) + cqs[e] - _pick_row(ck_blk, 2 * pair + e)
                if diagonal:
                    s = jnp.where(causal, s, _NEG)
                m_new = jnp.maximum(m, jnp.max(s, axis=1, keepdims=True))
                p = jnp.exp(s - m_new)
                out += [m_new, jnp.exp(m - m_new) * acc + _bdot(p, jnp.where(mine[e], vj, 1.0), _D1)]
            return tuple(out)

        init = (jnp.full((t, 1), _NEG, f32), jnp.zeros((t, PAIR_W), f32)) * 2
        carry = lax.fori_loop(0, i, lambda j, cr: block(j, cr, False), init)
        m0, a0, m1, a1 = block(i, carry, True)
        l0, l1 = _pick_lane(a0, HD), _pick_lane(a1, 0)
        o_ref[...] = jnp.where(first, a0 / l0, a1 / l1)
        lse_ref[...] = jnp.where(lane == 0, m0 + jnp.log(l0), jnp.where(lane == 1, m1 + jnp.log(l1), 0.0))

    q_spec = pl.BlockSpec((t, PAIR_W), lambda b, p, i: (b * nq + i, p))
    res = pl.pallas_call(
        body, name="fox_attn_fwd", grid=(batch, PAIRS, nq),
        in_specs=[q_spec,
                  pl.BlockSpec((seq, PAIR_W), lambda b, p, i: (b, PAIRS + p)),
                  pl.BlockSpec((seq, PAIR_W), lambda b, p, i: (b, 2 * PAIRS + p)),
                  pl.BlockSpec((t, 128), lambda b, p, i: (b * nq + i, 0)),
                  pl.BlockSpec((1, 8, seq), lambda b, p, i: (b, 0, 0))] + [_HBM_SPEC] * n_s,
        out_specs=[q_spec, q_spec] + [_HBM_SPEC] * n_s,
        out_shape=[jax.ShapeDtypeStruct((batch * seq, HW), f32)] * 2 + _side_out_shapes(srcs, per_peer),
        scratch_shapes=_side_sems(n_s),
        compiler_params=_cp(("arbitrary", "arbitrary", "arbitrary")),
    )(qkv, qkv, qkv, c, c_rows, *srcs)
    return res[0], res[1], list(res[2:])


def _fox_bwd(qkv, c, c_rows, o, lse, do, batch, seq):
    t = min(FOX_T, seq)
    nq = seq // t
    scale = HD ** -0.5

    def body(q_ref, k_ref, v_ref, cq_ref, ck_ref, o_ref, lse_ref, do_ref,
             dq_ref, dk_ref, dv_ref, dcq_ref, dck_ref, acc0, acc1):
        pair, i = pl.program_id(1), pl.program_id(2)
        accs = [acc0, acc1]

        @pl.when(i == 0)
        def _():
            dv_ref[...] = jnp.zeros_like(dv_ref)
            acc0[...] = jnp.zeros_like(acc0)
            acc1[...] = jnp.zeros_like(acc1)

        lane = lax.broadcasted_iota(jnp.int32, (1, PAIR_W), 1)
        first = (lane // HD) == 0
        mine = [first, jnp.logical_not(first)]
        q, d_o, o_i = q_ref[...] * scale, do_ref[...], o_ref[...]
        q0s = [jnp.where(mine[e], q, 0.0) for e in range(2)]
        q1s = [jnp.where(mine[e], q, 1.0) for e in range(2)]
        dos = [jnp.where(mine[e], d_o, 0.0) for e in range(2)]
        deltas = [jnp.sum(dos[e] * o_i, axis=1, keepdims=True) for e in range(2)]
        lses = [_pick_lane(lse_ref[...], e) for e in range(2)]
        cqs = [_pick_lane(cq_ref[...], 2 * pair + e) for e in range(2)]
        causal = lax.broadcasted_iota(jnp.int32, (t, t), 1) <= lax.broadcasted_iota(jnp.int32, (t, t), 0)

        def block(j, dqs, diagonal):
            rows = pl.ds(pl.multiple_of(j * t, t), t)
            kj, vj = k_ref[rows, :], v_ref[rows, :]
            ck_blk = ck_ref[0, :, rows]
            out = []
            for e in range(2):
                s = _bdot(q0s[e], kj, _D2) + cqs[e] - _pick_row(ck_blk, 2 * pair + e)
                if diagonal:
                    s = jnp.where(causal, s, _NEG)
                p = jnp.exp(s - lses[e])
                ds = p * (_bdot(dos[e], vj, _D2) - deltas[e])
                dv_ref[rows, :] += _bdot(p, dos[e], _D0)
                accs[e][rows, :] += _bdot(ds, q1s[e], _D0)
                out.append(dqs[e] + _bdot(ds, jnp.where(mine[e], kj, 1.0), _D1))
            return tuple(out)

        zero = jnp.zeros((t, PAIR_W), f32)
        dqs = lax.fori_loop(0, i, lambda j, cr: block(j, cr, False), (zero, zero))
        dq0, dq1 = block(i, dqs, True)
        dq_ref[...] = jnp.where(first, dq0, dq1) * scale
        dcq_ref[...] = jnp.where(lane == 0, _pick_lane(dq0, HD), jnp.where(lane == 1, _pick_lane(dq1, 0), 0.0))

        @pl.when(i == nq - 1)
        def _():
            a0, a1 = acc0[...], acc1[...]
            dk_ref[...] = jnp.where(first, a0, a1)
            dck_ref[...] = jnp.where(lane == 0, -_pick_lane(a0, HD), jnp.where(lane == 1, -_pick_lane(a1, 0), 0.0))

    blk = lambda col: pl.BlockSpec((t, PAIR_W), lambda b, p, i: (b * nq + i, col * PAIRS + p))
    whole = lambda col: pl.BlockSpec((seq, PAIR_W), lambda b, p, i: (b, col * PAIRS + p))
    t_all = batch * seq
    return pl.pallas_call(
        body, name="fox_attn_bwd", grid=(batch, PAIRS, nq),
        in_specs=[blk(0), whole(1), whole(2),
                  pl.BlockSpec((t, 128), lambda b, p, i: (b * nq + i, 0)),
                  pl.BlockSpec((1, 8, seq), lambda b, p, i: (b, 0, 0)),
                  blk(0), blk(0), blk(0)],
        out_specs=[blk(0), whole(0), whole(0), blk(0), whole(0)],
        out_shape=[jax.ShapeDtypeStruct((t_all, HW), f32)] * 5,
        scratch_shapes=[pltpu.VMEM((seq, PAIR_W), f32), pltpu.VMEM((seq, PAIR_W), f32)],
        compiler_params=_cp(("parallel", "parallel", "arbitrary")),
    )(qkv, qkv, qkv, c, c_rows, o, lse, do)


MEM_TQ = 1024


def _mem_block(q, km, vm):
    nn, nt, _ = _make_mm(False, False)
    logits = nt(q, km) * (MEM_HD ** -0.5)
    m = lax.stop_gradient(jnp.max(logits, axis=-1, keepdims=True))
    e = jnp.exp(logits - m)
    return nn(e / jnp.sum(e, axis=-1, keepdims=True), vm)


def _mem_specs(seq, tq):
    nq = seq // tq
    qs = pl.BlockSpec((tq, MEM_HD), lambda b, h, i: (b * nq + i, h))
    ks = pl.BlockSpec((MEM_LEN, MEM_HD), lambda b, h, i: (b, h))
    vs = pl.BlockSpec((MEM_LEN, MEM_HD), lambda b, h, i: (b, MEM_HEADS + h))
    return nq, qs, ks, vs


def _mem_fwd(q, mem_kv, batch, seq):
    tq = min(MEM_TQ, seq)
    nq, qs, ks, vs = _mem_specs(seq, tq)

    def body(q_ref, k_ref, v_ref, o_ref):
        o_ref[...] = _mem_block(q_ref[...].astype(f32), k_ref[...], v_ref[...]).astype(o_ref.dtype)

    return pl.pallas_call(
        body, name="mem_attn_fwd", grid=(batch, MEM_HEADS, nq),
        in_specs=[qs, ks, vs], out_specs=qs, out_shape=jax.ShapeDtypeStruct(q.shape, bf16),
        compiler_params=_cp(("parallel", "parallel", "arbitrary")),
    )(q, mem_kv, mem_kv)


def _mem_bwd(q, mem_kv, do, batch, seq):
    tq = min(MEM_TQ, seq)
    nq, qs, ks, vs = _mem_specs(seq, tq)

    def body(q_ref, k_ref, v_ref, do_ref, dq_ref, dk_ref, dv_ref):
        _, vjp = jax.vjp(_mem_block, q_ref[...].astype(f32), k_ref[...], v_ref[...])
        dq, dk, dv = vjp(do_ref[...])
        dq_ref[...] = dq.astype(dq_ref.dtype)

        @pl.when(pl.program_id(2) == 0)
        def _():
            dk_ref[...] = jnp.zeros_like(dk_ref)
            dv_ref[...] = jnp.zeros_like(dv_ref)

        dk_ref[...] += dk
        dv_ref[...] += dv

    return pl.pallas_call(
        body, name="mem_attn_bwd", grid=(batch, MEM_HEADS, nq),
        in_specs=[qs, ks, vs, qs], out_specs=[qs, ks, ks],
        out_shape=[jax.ShapeDtypeStruct(q.shape, bf16), jax.ShapeDtypeStruct((batch * MEM_LEN, MEM_W), f32),
                   jax.ShapeDtypeStruct((batch * MEM_LEN, MEM_W), f32)],
        compiler_params=_cp(("parallel", "parallel", "arbitrary")),
    )(q, mem_kv, mem_kv, do)


@jax.custom_vjp
def _halves(x):
    c = x.shape[1] // 2
    return x[:, :c], x[:, c:]


_halves.defvjp(lambda x: ((x[:, :x.shape[1] // 2], x[:, x.shape[1] // 2:]), None),
               lambda _, g: (jnp.concatenate(g, axis=1),))


@jax.custom_vjp
def _lead_halves(x):
    n = x.shape[0] // 2
    return x[:n], x[n:]


_lead_halves.defvjp(lambda x: ((x[:x.shape[0] // 2], x[x.shape[0] // 2:]), None),
                    lambda _, g: (jnp.concatenate(g, axis=0),))


def _scan_chunk(s0, r, wl, k, v, a, b):
    nn, nt, tn = _make_mm(True, False)
    nn_exact, _, _ = _make_mm(True, True)
    _, nt_exact, _ = _make_mm(True, "split")
    hp, c, lanes = r.shape
    row = lax.broadcasted_iota(jnp.int32, (c, c), 0)
    col = lax.broadcasted_iota(jnp.int32, (c, c), 1)
    first = (lax.broadcasted_iota(jnp.int32, (1, 1, lanes), 2) // HD) == 0
    tri = jnp.broadcast_to((col <= row).astype(f32)[None], (hp, c, c))
    lg = nn_exact(tri, wl)
    lg_end = lg[:, c - 1:c, :]
    grow, shrink, to_end = jnp.exp(lg), jnp.exp(-lg), jnp.exp(lg_end - lg)
    rt, kt, bt, at = r * grow, k * shrink, b * shrink, a * jnp.exp(lg - wl)
    strict, incl = (col < row)[None], (col <= row)[None]
    twice = lambda t: jnp.concatenate([t, t], axis=0)
    queries = jnp.concatenate([at, rt], axis=1)
    per_head = jnp.concatenate([jnp.where(first, queries, 0.0), jnp.where(first, 0.0, queries)], axis=0)
    (ab, rb), (ak, rk) = _halves(nt_exact(per_head, twice(bt))), _halves(nt_exact(per_head, twice(kt)))
    l_ab = jnp.where(strict, ab, 0.0)
    a_ak = jnp.where(strict, ak, 0.0)
    a_rb = jnp.where(incl, rb, 0.0)
    a_rk = jnp.where(incl, rk, 0.0)
    inv = (col == row).astype(f32)[None] + l_ab
    power, n = l_ab, 1
    while 2 * n < c:
        power = nn(power, power)
        inv = inv + nn(inv, power)
        n *= 2

    def apply(m, t):
        lo, hi = _lead_halves(nn(m, twice(t)))
        return jnp.where(first, lo, hi)

    sa = apply(inv, nt(at, s0) + apply(a_ak, v))
    y = nt(rt, s0) + apply(a_rk, v) + apply(a_rb, sa)
    same_head = ((lax.broadcasted_iota(jnp.int32, (lanes, lanes), 0) // HD)
                 == (lax.broadcasted_iota(jnp.int32, (lanes, lanes), 1) // HD))[None]
    s1 = s0 * jnp.exp(lg_end) + jnp.where(same_head, tn(v, k * to_end) + tn(sa, b * to_end), 0.0)
    return y, s1


PAIRS = HEADS // 2
PAIR_W = 2 * HD
SCAN_ARGS = (0, 3, 1, 2, 4, 5)


def _pair_stack(ref, off):
    return jnp.stack([ref[b, :, off + p * PAIR_W:off + (p + 1) * PAIR_W]
                      for b in range(ref.shape[0]) for p in range(PAIRS)])


def _pair_store(ref, off, val, add_ref=None):
    for b in range(ref.shape[0]):
        for p in range(PAIRS):
            sl = slice(off + p * PAIR_W, off + (p + 1) * PAIR_W)
            v = val[b * PAIRS + p]
            ref[b, :, sl] = v if add_ref is None else v + add_ref[b, :, sl]


def _scan_fwd(main6, batch, seq, side=None):
    c = min(SCAN_CHUNK, seq)
    nc = seq // c
    hp = batch * PAIRS
    srcs, per_peer = side if side is not None else ([], False)
    n_s = len(srcs)

    def body(*refs):
        z_ref, y_ref, s_ref, st = refs[0], refs[1 + n_s], refs[2 + n_s], refs[3 + 2 * n_s]
        _side_exchange(refs[1:1 + n_s], refs[3 + n_s:3 + 2 * n_s], per_peer, refs[4 + 2 * n_s:], nc)

        @pl.when(pl.program_id(0) == 0)
        def _():
            st[...] = jnp.zeros_like(st)

        s0 = st[...]
        s_ref[0] = s0
        y, s1 = _scan_chunk(s0, *[_pair_stack(z_ref, comp * HW) for comp in SCAN_ARGS])
        _pair_store(y_ref, 0, y)
        st[...] = s1

    res = pl.pallas_call(
        body, name="rwkv_scan_fwd", grid=(nc,),
        in_specs=[pl.BlockSpec((batch, c, 6 * HW), lambda i: (0, i, 0))] + [_HBM_SPEC] * n_s,
        out_specs=[pl.BlockSpec((batch, c, HW), lambda i: (0, i, 0)),
                   pl.BlockSpec((1, hp, PAIR_W, PAIR_W), lambda i: (i, 0, 0, 0))] + [_HBM_SPEC] * n_s,
        out_shape=[jax.ShapeDtypeStruct((batch, seq, HW), f32), jax.ShapeDtypeStruct((nc, hp, PAIR_W, PAIR_W), f32)]
        + _side_out_shapes(srcs, per_peer),
        scratch_shapes=[pltpu.VMEM((hp, PAIR_W, PAIR_W), f32)] + _side_sems(n_s),
        compiler_params=_cp(("arbitrary",)),
    )(main6.reshape(batch, seq, 6 * HW), *srcs)
    return res[0].reshape(batch * seq, HW), res[1], list(res[2:])


def _scan_bwd(main6, states, dy, extra, batch, seq, side=None):
    c = min(SCAN_CHUNK, seq)
    nc = seq // c
    hp = batch * PAIRS
    srcs, per_peer = side if side is not None else ([], False)
    n_s = len(srcs)

    def body(*refs):
        z_ref, s_ref, dy_ref, ex_ref = refs[:4]
        dz_ref, dst = refs[4 + n_s], refs[5 + 2 * n_s]
        _side_exchange(refs[4:4 + n_s], refs[5 + n_s:5 + 2 * n_s], per_peer, refs[6 + 2 * n_s:], nc)

        @pl.when(pl.program_id(0) == 0)
        def _():
            dst[...] = jnp.zeros_like(dst)

        _, vjp = jax.vjp(_scan_chunk, s_ref[0], *[_pair_stack(z_ref, comp * HW) for comp in SCAN_ARGS])
        g = vjp((_pair_stack(dy_ref, 0), dst[...]))
        dst[...] = g[0]
        for arg, comp in enumerate(SCAN_ARGS):
            _pair_store(dz_ref, comp * HW, g[1 + arg], ex_ref if comp < 3 else None)

    back = lambda i: (0, nc - 1 - i, 0)
    wide = pl.BlockSpec((batch, c, 6 * HW), back)
    res = pl.pallas_call(
        body, name="rwkv_scan_bwd", grid=(nc,),
        in_specs=[wide, pl.BlockSpec((1, hp, PAIR_W, PAIR_W), lambda i: (nc - 1 - i, 0, 0, 0)),
                  pl.BlockSpec((batch, c, HW), back), pl.BlockSpec((batch, c, 3 * HW), back)] + [_HBM_SPEC] * n_s,
        out_specs=[wide] + [_HBM_SPEC] * n_s,
        out_shape=[jax.ShapeDtypeStruct((batch, seq, 6 * HW), f32)] + _side_out_shapes(srcs, per_peer),
        scratch_shapes=[pltpu.VMEM((hp, PAIR_W, PAIR_W), f32)] + _side_sems(n_s),
        compiler_params=_cp(("arbitrary",)),
    )(main6.reshape(batch, seq, 6 * HW), states, dy.reshape(batch, seq, HW), extra.reshape(batch, seq, 3 * HW), *srcs)
    return res[0].reshape(batch * seq, 6 * HW), list(res[1:])


def _pad_cols(x, width):
    return jnp.pad(x, ((0, 0), (0, width - x.shape[1])))


def _split_w_in(wt):
    z = lambda rows: jnp.zeros((rows, wt.shape[1]), wt.dtype)
    w_r = jnp.concatenate([wt[1544:3080], wt[3080:3144], z(64), wt[3144:3208], z(64), wt[3208:3336]], axis=0)
    return wt[:1536], jnp.concatenate([wt[1536:1544], z(120)], axis=0), w_r, wt[3336:3848], wt[3848:]


def _merge_w_in(g_qkv, g_f, g_r, g_mq, g_g):
    return jnp.concatenate([g_qkv, g_f[:8], g_r[:1536], g_r[1536:1600], g_r[1664:1728], g_r[1792:], g_mq, g_g], axis=0)


def _pad_lora(v):
    z64 = jnp.zeros((1, 64), v.dtype)
    return jnp.concatenate([v[:, :1536], v[:, 1536:1600], z64, v[:, 1600:1664], z64, v[:, 1664:]], axis=1)


def _unpad_lora(v):
    return jnp.concatenate([v[:, :1536], v[:, 1536:1600], v[:, 1664:1728], v[:, 1792:]], axis=1)


def _local_step(x, mem, target, w, p, late=None, early=None, last=None):
    batch, seq, _ = x.shape
    t = batch * seq
    x2, tg2, mem2 = x.reshape(t, D), target.reshape(t, D), mem.reshape(batch * MEM_LEN, D)
    w_qkv, w_f, w_r, w_mq, w_g3 = _split_w_in(w["w_in"])
    mu = _pad_lora(p["rwkv_mu"])
    bias = _pad_cols(p["fox_f_bias"], 128)
    r_k = p["rwkv_r_k"].reshape(1, HW)
    post_params = [p["rwkv_gn_g"], p["rwkv_gn_b"], r_k]
    rw_widths = [HW, HW, HW, LORA_PAD, LORA_PAD, LORA_PAD]
    six = [HW] * 6

    p_g, u = _matmul("proj_gate", _lazy(_fn_rms, [(x2, [D])], D, params=[p["pre1_g"]]), w_g3, "nt", out_dtype=bf16)
    p_qkv = _matmul("proj_qkv", u, w_qkv, "nt", out_dtype=bf16)
    p_f = _matmul("proj_f", u, w_f, "nt")
    p_r = _matmul("proj_rwkv", u, w_r, "nt", out_dtype=bf16)
    p_mq = _matmul("proj_memq", u, w_mq, "nt", out_dtype=bf16)

    c = _fox_gate_fwd(p_f, bias, batch, seq)
    c_rows = c[:, :HEADS].reshape(batch, seq, HEADS).transpose(0, 2, 1)
    fox_o, lse, gathered = _fox_fwd(p_qkv, c, c_rows, batch, seq, side=(late[0], False) if late else None)
    if late:
        w = {**w, **late[2](gathered, 0)}
    fox_out = fox_o.astype(bf16)

    w_up = jnp.pad(w["rwkv_w_up"].astype(f32), ((0, LORA_PAD - 64), (0, 0)))
    a_up = jnp.pad(w["rwkv_a_up"].astype(f32), ((0, LORA_PAD - 64), (0, 0)))
    pre_params = [p["rwkv_w0"], w_up, p["rwkv_a0"], a_up, w["rwkv_g_up"].astype(f32), p["rwkv_k_k"], p["rwkv_k_a"]]
    ps = _tokshift_fwd(p_r, mu, batch, seq)
    main6, g_rw = _rows_fwd("rwkv_pre", _fn_rwkv_pre, [], [(ps, rw_widths)], pre_params, [six, [HW]], tm=256)
    y_rw, states, gathered = _scan_fwd(main6, batch, seq, side=(late[1], False) if late else None)
    if late:
        w = {**w, **late[2](gathered, 1)}
    post_consts = []
    post_rows = [(y_rw, [HW]), (main6, [HW, HW, HW]), (g_rw, [HW])]
    fn_post = _fn_rwkv_post

    (rwkv_out,) = _rows_fwd("rwkv_post", fn_post, post_consts, post_rows, post_params, [[HW]], dtypes=[bf16], tm=256)

    mem_kv, memn = _matmul("proj_memkv", _lazy(_fn_rms, [(mem2, [D])], D, params=[p["mem_norm_g"]]), w["w_mem_kv"], "nn")
    mem_out = _mem_fwd(p_mq, mem_kv, batch, seq)

    a_fox = _matmul("out_fox", fox_out, w["w_fox_out"], "nn", out_dtype=bf16)
    a_rwkv = _matmul("out_rwkv", rwkv_out, w["w_rwkv_out"], "nn", out_dtype=bf16)
    a_mem = _matmul("out_mem", mem_out, w["w_mem_out"], "nn", out_dtype=bf16)
    merge_rows = [(a_fox, [D]), (a_rwkv, [D]), (a_mem, [D]), (p_g, [D, D, D])]
    yy, merged = _matmul("out_o", _lazy(_fn_merge, merge_rows, D), w["w_o"], "nn")
    post1_rows = [(yy, [D]), (x2, [D])]
    post1_params = [p["post1_g"], p["pre2_g"]]
    h1, u2 = _rows_fwd("post1", _fn_post1, [], post1_rows, post1_params, [[D], [D]], dtypes=[f32, bf16])
    gp = _matmul("ffn_gate", u2, w["w_ffn_gate"], "nt", out_dtype=bf16)
    up = _matmul("ffn_up", u2, w["w_ffn_up"], "nt", out_dtype=bf16)
    ffn, hmid = _matmul("ffn_down", _lazy(_fn_swiglu, [(gp, [D_FF]), (up, [D_FF])], D_FF), w["w_ffn_down"], "nn")
    final_rows = [(ffn, [D]), (h1, [D])]

    gw, gp_ = {}, {}
    (d_ffn, d_h1), (gp_["post2_g"], loss) = _rows_bwd("final", _fn_final, [(tg2, [D])], final_rows, [p["post2_g"]], [], [],
                                                      n_sums=1, dtypes=[bf16, f32])
    gw["w_ffn_down"] = _matmul("ffn_down_dw", hmid, d_ffn, "tn", out_dtype=bf16)
    (d_gp, d_up), _ = _matmul_then_vjp("ffn_down_dx", d_ffn, w["w_ffn_down"], "nt", _fn_swiglu,
                                       [(gp, [D_FF]), (up, [D_FF])], [bf16, bf16])
    gw["w_ffn_gate"] = _matmul("ffn_gate_dw", d_gp, u2, "tn", out_dtype=bf16)
    gw["w_ffn_up"] = _matmul("ffn_up_dw", d_up, u2, "tn", out_dtype=bf16)
    d_u2_gate = _matmul("ffn_gate_dx", d_gp, w["w_ffn_gate"], "nn")
    (d_yy, d_x_res), (gp_["post1_g"], gp_["pre2_g"]) = _matmul_then_vjp(
        "ffn_up_dx", d_up, w["w_ffn_up"], "nn", _fn_post1, post1_rows, [bf16, f32], params=post1_params,
        first_cts=[d_h1], add=d_u2_gate)
    gw["w_o"] = _matmul("out_o_dw", merged, d_yy, "tn", out_dtype=bf16)
    (d_a_fox, d_a_rwkv, d_a_mem, d_p_g), _ = _matmul_then_vjp("out_o_dx", d_yy, w["w_o"], "nt", _fn_merge, merge_rows,
                                                             [bf16] * 4)
    d_fox_out = _matmul("out_fox_dx", d_a_fox, w["w_fox_out"], "nt")
    gw["w_fox_out"] = _matmul("out_fox_dw", fox_out, d_a_fox, "tn", out_dtype=bf16)
    gw["w_rwkv_out"] = _matmul("out_rwkv_dw", rwkv_out, d_a_rwkv, "tn", out_dtype=bf16)
    d_mem_out = _matmul("out_mem_dx", d_a_mem, w["w_mem_out"], "nt")
    gw["w_mem_out"] = _matmul("out_mem_dw", mem_out, d_a_mem, "tn", out_dtype=bf16)

    d_p_mq, d_km, d_vm = _mem_bwd(p_mq, mem_kv, d_mem_out, batch, seq)
    d_mem_kv = jnp.concatenate([d_km, d_vm], axis=1).astype(bf16)
    gw["w_mem_kv"] = _matmul("proj_memkv_dw", memn, d_mem_kv, "tn", out_dtype=bf16)
    d_memn = _matmul("proj_memkv_dx", d_mem_kv, w["w_mem_kv"], "nt")
    _, (gp_["mem_norm_g"],) = _rows_bwd("rms_mem_bwd", _fn_rms, [], [(mem2, [D])], [p["mem_norm_g"]], [[D]], [d_memn])

    d_q, d_k, d_v, d_cq, d_ck = _fox_bwd(p_qkv, c, c_rows, fox_o, lse, d_fox_out, batch, seq)
    d_p_qkv = jnp.concatenate([d_q, d_k, d_v], axis=1).astype(bf16)
    d_p_f, d_bias = _fox_gate_bwd(p_f, bias, d_cq, d_ck, batch, seq)
    gp_["fox_f_bias"] = d_bias[:, :HEADS]

    (d_y_rw, d_main6_post, d_g_rw), (gp_["rwkv_gn_g"], gp_["rwkv_gn_b"], d_rk) = _matmul_then_vjp(
        "out_rwkv_dx", d_a_rwkv, w["w_rwkv_out"], "nt", fn_post, post_rows, [f32] * 3, params=post_params)
    gp_["rwkv_r_k"] = d_rk.reshape(1, HEADS, HD)
    d_main6, early_got = _scan_bwd(main6, states, d_y_rw, d_main6_post, batch, seq,
                                   side=(early(gw), True) if early else None)

    def fn_pre_sum(*args):
        return _fn_rwkv_pre(*args)

    (d_ps,), d_pre = _rows_bwd("rwkv_pre_bwd", fn_pre_sum, [], [(ps, rw_widths)], pre_params, [six, [HW]],
                               [d_main6, d_g_rw], tm=256)
    gp_["rwkv_w0"], d_w_up, gp_["rwkv_a0"], d_a_up, gw["rwkv_g_up"], gp_["rwkv_k_k"], gp_["rwkv_k_a"] = d_pre
    gw["rwkv_w_up"], gw["rwkv_a_up"] = d_w_up[:64], d_a_up[:64]
    d_p_r, d_mu = _tokshift_bwd(p_r, mu, d_ps, batch, seq)
    gp_["rwkv_mu"] = _unpad_lora(d_mu)

    gw["w_in"] = _merge_w_in(_matmul("proj_qkv_dw", d_p_qkv, u, "tn", out_dtype=bf16), _matmul("proj_f_dw", d_p_f, u, "tn", out_dtype=bf16),
                             _matmul("proj_rwkv_dw", d_p_r, u, "tn", out_dtype=bf16), _matmul("proj_memq_dw", d_p_mq, u, "tn", out_dtype=bf16),
                             _matmul("proj_gate_dw", d_p_g, u, "tn", out_dtype=bf16))
    d_x, gp_["pre1_g"], last_got = _input_cotangent(
        "proj_dx", [d_p_qkv, d_p_f, d_p_r, d_p_mq, d_p_g], [w_qkv, w_f, w_r, w_mq, w_g3], x2, p["pre1_g"], d_x_res,
        side=(last(gw), True) if last else None)
    return loss, d_x.reshape(x.shape), gw, gp_, early_got, last_got


def _adamw(name, recv, row_off, w, m, v):
    _, rows, cols = w.shape
    row_tiles = [t for t in range(16, min(rows, 128) + 1, 16) if rows % t == 0 and row_off % t == 0]
    if row_tiles:
        tr, tc = max(row_tiles), cols
        first, grid = row_off // tr, (rows // tr,)
        at = lambda i: (0, first + i, 0)
        mine = lambda i: (0, i, 0)
    else:
        assert row_off == 0 and recv.shape[1] == rows
        tr, tc = rows, 128
        grid = (cols // tc,)
        at = mine = lambda i: (0, 0, i)

    def body(g_ref, w_ref, m_ref, v_ref, go_ref, d_ref, mo_ref, vo_ref):
        g = g_ref[0].astype(f32)
        for s in range(1, N_DEV):
            g = g + g_ref[s].astype(f32)
        m_new = ADAM_B1 * m_ref[0] + (1.0 - ADAM_B1) * g
        v_new = ADAM_B2 * v_ref[0] + (1.0 - ADAM_B2) * (g * g)
        m_hat = m_new / (1.0 - ADAM_B1 ** ADAM_STEP)
        v_hat = v_new / (1.0 - ADAM_B2 ** ADAM_STEP)
        go_ref[0] = g
        d_ref[0] = -ADAM_LR * (m_hat / (jnp.sqrt(v_hat) + ADAM_EPS) + ADAM_WD * w_ref[0])
        mo_ref[0] = m_new
        vo_ref[0] = v_new

    spec = pl.BlockSpec((1, tr, tc), mine)
    return pl.pallas_call(
        body, name=name, grid=grid,
        in_specs=[pl.BlockSpec((N_DEV, tr, tc), at), spec, spec, spec],
        out_specs=[spec] * 4, out_shape=[jax.ShapeDtypeStruct(w.shape, f32)] * 4,
        compiler_params=_cp(("parallel",)),
    )(recv, w, m, v)


GROUPS = (
    ("in", ("w_in",), 0),
    ("memkv", ("w_mem_kv",), 0),
    ("ffn_gu", ("w_ffn_gate", "w_ffn_up"), 0),
    ("down_o", ("w_ffn_down", "w_o"), 0),
    ("outs", ("w_fox_out", "w_rwkv_out", "w_mem_out"), 0),
    ("lora", ("rwkv_w_up", "rwkv_a_up", "rwkv_g_up"), 0),
)
FIRST_GROUPS = ("in", "memkv")
LATE_GROUPS = (("down_o", "outs", "lora"), ("ffn_gu",))
EARLY_GRAD_GROUPS = ("memkv", "ffn_gu", "down_o", "outs")
LAST_GRAD_GROUPS = ("in", "lora")
SHARD_AXIS = {n: a for n, _, a in SHARDED}
SMALL_ROWS = 16
LOSS_LANES = 128


def _group_local(shards, members, join):
    parts = [shards[n].reshape(shards[n].shape[-2:]) for n in members]
    return parts[0] if len(parts) == 1 else jnp.concatenate(parts, axis=join)


def _group_split(arr, members, join, lead=False):
    out, off = {}, 0
    for n in members:
        shape = dict((k, s) for k, s, _ in SHARDED)[n]
        size = _block_shape(shape, SHARD_AXIS[n])[join]
        idx = [slice(None)] * arr.ndim
        idx[arr.ndim - 2 + join] = slice(off, off + size)
        out[n] = arr[tuple(idx)]
        off += size
    return out


def _full_from_blocks(blocks, axis):
    if axis == 0:
        return blocks.reshape(-1, blocks.shape[2])
    return blocks.transpose(1, 0, 2).reshape(blocks.shape[1], -1)


def _blocks_from_full(full, axis):
    if axis == 0:
        return full.reshape(N_DEV, -1, full.shape[1])
    return full.reshape(full.shape[0], N_DEV, -1).transpose(1, 0, 2)


def _assemble(gathered, names):
    out = {}
    for arr, g in zip(gathered, names):
        _, members, join = [grp for grp in GROUPS if grp[0] == g][0]
        for n, blk in _group_split(arr, members, join, lead=True).items():
            out[n] = _full_from_blocks(blk, SHARD_AXIS[n])
    return out


def _grad_blocks(gw, names):
    out = []
    for g in names:
        _, members, join = [grp for grp in GROUPS if grp[0] == g][0]
        parts = [_blocks_from_full(gw[n].astype(bf16), SHARD_AXIS[n]) for n in members]
        out.append(parts[0] if len(parts) == 1 else jnp.concatenate(parts, axis=1 + join))
    return out


def _small_pack(d):
    flat = jnp.concatenate([d[n].reshape(-1) for n, _ in REPLICATED])
    return jnp.pad(flat, (0, SMALL_ROWS * LANES - REPL_ELEMS)).reshape(SMALL_ROWS, LANES)


def _small_unpack(packed):
    out, flat, off = {}, packed.reshape(-1), 0
    for n, shape in REPLICATED:
        k = _rows_of((LANES,) + shape)
        out[n] = flat[off:off + k].reshape(shape)
        off += k
    return out


def kernel(x, mem, pre1_g, post1_g, pre2_g, post2_g, mem_norm_g, w_in, fox_f_bias, rwkv_mu, rwkv_w0, rwkv_w_up, rwkv_a0, rwkv_a_up, rwkv_g_up, rwkv_k_k, rwkv_k_a, rwkv_r_k, rwkv_gn_g, rwkv_gn_b, w_mem_kv, w_fox_out, w_rwkv_out, w_mem_out, w_o, w_ffn_gate, w_ffn_up, w_ffn_down, loss_target, m_pre1_g, m_post1_g, m_pre2_g, m_post2_g, m_mem_norm_g, m_w_in, m_fox_f_bias, m_rwkv_mu, m_rwkv_w0, m_rwkv_w_up, m_rwkv_a0, m_rwkv_a_up, m_rwkv_g_up, m_rwkv_k_k, m_rwkv_k_a, m_rwkv_r_k, m_rwkv_gn_g, m_rwkv_gn_b, m_w_mem_kv, m_w_fox_out, m_w_rwkv_out, m_w_mem_out, m_w_o, m_w_ffn_gate, m_w_ffn_up, m_w_ffn_down, v_pre1_g, v_post1_g, v_pre2_g, v_post2_g, v_mem_norm_g, v_w_in, v_fox_f_bias, v_rwkv_mu, v_rwkv_w0, v_rwkv_w_up, v_rwkv_a0, v_rwkv_a_up, v_rwkv_g_up, v_rwkv_k_k, v_rwkv_k_a, v_rwkv_r_k, v_rwkv_gn_g, v_rwkv_gn_b, v_w_mem_kv, v_w_fox_out, v_w_rwkv_out, v_w_mem_out, v_w_o, v_w_ffn_gate, v_w_ffn_up, v_w_ffn_down):
    args = dict(locals())
    turn = lambda n, a: jnp.swapaxes(a, 1, 2) if n in TRANSPOSED else a
    wts = {n: turn(n, args[n]) for n in WEIGHT_ORDER}
    ms = {n: turn(n, args["m_" + n]) for n in WEIGHT_ORDER}
    vs = {n: turn(n, args["v_" + n]) for n in WEIGHT_ORDER}

    groups = {g: (members, join) for g, members, join in GROUPS}
    w_bf16 = {n: wts[n].astype(bf16) for n, _, _ in SHARDED}

    def send(g):
        return _group_local(w_bf16, *groups[g])

    first = _exchange("gather_first", [send(g) for g in FIRST_GROUPS], per_peer=False)
    full = _assemble(first, FIRST_GROUPS)
    small_in = {n: (wts[n] if n == "rwkv_r_k" else wts[n].reshape(wts[n].shape[-2:])) for n, _ in REPLICATED}
    late = ([send(g) for g in LATE_GROUPS[0]], [send(g) for g in LATE_GROUPS[1]],
            lambda got, which: _assemble(got, LATE_GROUPS[which]))
    loss_part, grad_x, gw, gp, early_got, last_got = _local_step(
        x, mem, loss_target, full, small_in, late=late, early=lambda g: _grad_blocks(g, EARLY_GRAD_GROUPS),
        last=lambda g: _grad_blocks(g, LAST_GRAD_GROUPS))
    small_got, loss_got = _exchange("exchange_small", [_small_pack(gp).astype(bf16), jnp.broadcast_to(loss_part, (8, LOSS_LANES))],
                                    per_peer=False)
    received = dict(zip(EARLY_GRAD_GROUPS + LAST_GRAD_GROUPS, list(early_got) + list(last_got)))

    outs = [{}, {}, {}, {}]
    for g, members, _ in GROUPS:
        off = 0
        for n in members:
            for o, arr in zip(outs, _adamw("adamw_" + n, received[g], off, wts[n], ms[n], vs[n])):
                o[n] = arr
            off += wts[n].shape[1]
    res = _adamw("adamw_small", small_got, 0, *[_small_pack(d)[None] for d in (wts, ms, vs)])
    for o, arr in zip(outs, res):
        o.update(_small_unpack(arr))
    loss = jnp.sum(loss_got[:, 0, 0])
    return (loss, grad_x, *[turn(n, o[n].reshape(wts[n].shape)) for o in outs for n in WEIGHT_ORDER])
```

```python
import functools

import jax
import jax.numpy as jnp
from jax import lax
from jax.experimental import pallas as pl
from jax.experimental.pallas import tpu as pltpu

f32 = jnp.float32
bf16 = jnp.bfloat16
_HI = lax.Precision.HIGHEST

D = 1024
HEADS = 8
HD = 64
HW = HEADS * HD
MEM_HEADS = 4
MEM_HD = 128
MEM_W = 512
MEM_LEN = 256
D_FF = 2816
LORA_PAD = 128
NORM_EPS = 1e-6
GN_EPS = 64e-5
SCAN_CHUNK = 64
N_DEV = 8
LANES = 1024
VMEM_LIMIT = 56 * 1024 * 1024

ADAM_LR = 0.001
ADAM_B1 = 0.9
ADAM_B2 = 0.999
ADAM_EPS = 1e-08
ADAM_WD = 0.01
ADAM_STEP = 10

TRANSPOSED = ("w_in", "w_ffn_gate", "w_ffn_up")
SHARDED = (
    ("w_in", (6920, 1024), 0),
    ("w_ffn_gate", (2816, 1024), 0),
    ("w_ffn_up", (2816, 1024), 0),
    ("w_ffn_down", (2816, 1024), 0),
    ("w_mem_kv", (1024, 1024), 0),
    ("w_o", (1024, 1024), 0),
    ("w_fox_out", (512, 1024), 1),
    ("w_rwkv_out", (512, 1024), 1),
    ("w_mem_out", (512, 1024), 1),
    ("rwkv_w_up", (64, 512), 1),
    ("rwkv_a_up", (64, 512), 1),
    ("rwkv_g_up", (128, 512), 1),
)
REPLICATED = (
    ("pre1_g", (1, 1024)), ("post1_g", (1, 1024)), ("pre2_g", (1, 1024)), ("post2_g", (1, 1024)),
    ("mem_norm_g", (1, 1024)), ("fox_f_bias", (1, 8)), ("rwkv_mu", (1, 1792)), ("rwkv_w0", (1, 512)),
    ("rwkv_a0", (1, 512)), ("rwkv_k_k", (1, 512)), ("rwkv_k_a", (1, 512)), ("rwkv_r_k", (1, 8, 64)),
    ("rwkv_gn_g", (1, 512)), ("rwkv_gn_b", (1, 512)),
)
WEIGHT_ORDER = ('pre1_g', 'post1_g', 'pre2_g', 'post2_g', 'mem_norm_g', 'w_in', 'fox_f_bias', 'rwkv_mu',
                'rwkv_w0', 'rwkv_w_up', 'rwkv_a0', 'rwkv_a_up', 'rwkv_g_up', 'rwkv_k_k', 'rwkv_k_a',
                'rwkv_r_k', 'rwkv_gn_g', 'rwkv_gn_b', 'w_mem_kv', 'w_fox_out', 'w_rwkv_out', 'w_mem_out',
                'w_o', 'w_ffn_gate', 'w_ffn_up', 'w_ffn_down')


def _block_shape(shape, axis):
    return tuple(s // N_DEV if i == axis else s for i, s in enumerate(shape))


def _rows_of(shape):
    n = 1
    for s in shape:
        n *= s
    return n // LANES


REPL_ELEMS = sum(_rows_of((LANES,) + s) for _, s in REPLICATED)


def _cp(sem=None):
    return pltpu.CompilerParams(dimension_semantics=sem, vmem_limit_bytes=VMEM_LIMIT)


def _tile(dim, cap):
    best = None
    for t in range(128, min(dim, cap) + 1, 128):
        if dim % t == 0:
            best = t
    return best if best is not None else dim


def _two_terms(x):
    hi = x.astype(bf16)
    return hi, (x - hi.astype(f32)).astype(bf16)


def _dg(a, b, dims, exact):
    if exact == "split":
        (a_hi, a_lo), (b_hi, b_lo) = _two_terms(a), _two_terms(b)
        dot = functools.partial(lax.dot_general, dimension_numbers=dims, preferred_element_type=f32)
        return dot(a_hi, b_hi) + (dot(a_hi, b_lo) + dot(a_lo, b_hi))
    if exact:
        return lax.dot_general(a, b, dims, precision=_HI, preferred_element_type=f32)
    return lax.dot_general(a.astype(bf16), b.astype(bf16), dims, preferred_element_type=f32)


def _make_mm(batched, exact):
    o = 1 if batched else 0
    bd = ((0,), (0,)) if batched else ((), ())
    d_nn = (((1 + o,), (o,)), bd)
    d_nt = (((1 + o,), (1 + o,)), bd)
    d_tn = (((o,), (o,)), bd)

    @jax.custom_vjp
    def nn(a, b):
        return _dg(a, b, d_nn, exact)

    @jax.custom_vjp
    def nt(a, b):
        return _dg(a, b, d_nt, exact)

    @jax.custom_vjp
    def tn(a, b):
        return _dg(a, b, d_tn, exact)

    nn.defvjp(lambda a, b: (_dg(a, b, d_nn, exact), (a, b)),
              lambda res, g: (_dg(g, res[1], d_nt, exact), _dg(res[0], g, d_tn, exact)))
    nt.defvjp(lambda a, b: (_dg(a, b, d_nt, exact), (a, b)),
              lambda res, g: (_dg(g, res[1], d_nn, exact), _dg(g, res[0], d_tn, exact)))
    tn.defvjp(lambda a, b: (_dg(a, b, d_tn, exact), (a, b)),
              lambda res, g: (_dg(res[1], g, d_nt, exact), _dg(res[0], g, d_nn, exact)))
    return nn, nt, tn


def _sigmoid(x):
    return 1.0 / (1.0 + jnp.exp(-x))


def _head_sum_raw(x):
    width = 2 * HD
    i = lax.broadcasted_iota(jnp.int32, (width, width), 0) // HD
    j = lax.broadcasted_iota(jnp.int32, (width, width), 1) // HD
    m = (i == j).astype(bf16)
    dims = (((1,), (0,)), ((), ()))
    out = []
    for p in range(x.shape[1] // width):
        xp = x[:, p * width:(p + 1) * width]
        hi = xp.astype(bf16)
        lo = (xp - hi.astype(f32)).astype(bf16)
        out.append(lax.dot_general(hi, m, dims, preferred_element_type=f32)
                   + lax.dot_general(lo, m, dims, preferred_element_type=f32))
    return jnp.concatenate(out, axis=1)


@jax.custom_vjp
def _head_sum(x):
    return _head_sum_raw(x)


_head_sum.defvjp(lambda x: (_head_sum_raw(x), None), lambda _, g: (_head_sum_raw(g),))


WEIGHT_TILE_BYTES = 13 * 512 * 1024
ACC_TILE_BYTES = 8 * 1024 * 1024


def _lazy(fn, rows, width, params=()):
    return (fn, rows, width, list(params))


def _matmul(name, a, b, mode, add=None, out_dtype=f32):
    has_add = add is not None
    if isinstance(a, tuple):
        a_fn, a_rows, a_width, a_params = a
        a_arrays = [r for r, _ in a_rows]
        a_shape = (a_arrays[0].shape[0], a_width)
    else:
        a_fn, a_rows, a_params, a_arrays, a_shape = None, None, [], [a], a.shape
    n_r = len(a_arrays)
    n_a = n_r + len(a_params)

    def load_a(refs):
        if a_fn is None:
            return refs[0][...].astype(bf16)
        pieces = []
        for r, (_, widths) in zip(refs[:n_r], a_rows):
            pieces += _pieces(r, widths)
        return a_fn(*pieces, *[p[...] for p in refs[n_r:]])[0].astype(bf16)

    if mode == "tn":
        assert a_fn is None
        (k, m), (_, n) = a_shape, b.shape
        tn = _tile(n, max(128, ACC_TILE_BYTES // (4 * m)))
        tk = _tile(k, 2048)
        nk = k // tk

        def body(*refs):
            b_ref, o_ref, acc = refs[n_a:]

            @pl.when(pl.program_id(1) == 0)
            def _():
                acc[...] = jnp.zeros_like(acc)

            acc[...] += lax.dot_general(load_a(refs[:n_a]), b_ref[...].astype(bf16),
                                        (((0,), (0,)), ((), ())), preferred_element_type=f32)

            @pl.when(pl.program_id(1) == nk - 1)
            def _():
                o_ref[...] = acc[...].astype(o_ref.dtype)

        return pl.pallas_call(
            body, name=name, grid=(n // tn, nk),
            in_specs=[pl.BlockSpec((tk, r.shape[1]), lambda j, kk: (kk, 0)) for r in a_arrays]
            + [pl.BlockSpec((tk, tn), lambda j, kk: (kk, j))],
            out_specs=pl.BlockSpec((m, tn), lambda j, kk: (0, j)), out_shape=jax.ShapeDtypeStruct((m, n), out_dtype),
            scratch_shapes=[pltpu.VMEM((m, tn), f32)],
            compiler_params=_cp(("parallel", "arbitrary")),
        )(*a_arrays, b)

    (m, k) = a_shape
    n = b.shape[1] if mode == "nn" else b.shape[0]
    tm = _tile(m, 1024 if a_fn is None else 512)
    tn = _tile(n, max(128, WEIGHT_TILE_BYTES // (2 * k)))
    dims = (((1,), (0,)), ((), ())) if mode == "nn" else (((1,), (1,)), ((), ()))
    b_spec = pl.BlockSpec((k, tn), lambda j, i: (0, j)) if mode == "nn" else pl.BlockSpec((tn, k), lambda j, i: (j, 0))
    o_spec = pl.BlockSpec((tm, tn), lambda j, i: (i, j))

    keep = a_fn is not None
    assert not keep or tn == n

    def body(*refs):
        b_ref = refs[n_a]
        a_val = load_a(refs[:n_a])
        r = lax.dot_general(a_val, b_ref[...].astype(bf16), dims, preferred_element_type=f32)
        if has_add:
            r = r + refs[n_a + 1][...]
        if keep:
            refs[-2][...] = r.astype(refs[-2].dtype)
            refs[-1][...] = a_val
        else:
            refs[-1][...] = r.astype(refs[-1].dtype)

    res = pl.pallas_call(
        body, name=name, grid=(n // tn, m // tm),
        in_specs=[pl.BlockSpec((tm, r.shape[1]), lambda j, i: (i, 0)) for r in a_arrays]
        + [pl.BlockSpec(p.shape, lambda j, i: (0, 0)) for p in a_params] + [b_spec] + ([o_spec] if has_add else []),
        out_specs=[o_spec] + ([pl.BlockSpec((tm, k), lambda j, i: (i, 0))] if keep else []),
        out_shape=[jax.ShapeDtypeStruct((m, n), out_dtype)] + ([jax.ShapeDtypeStruct((m, k), bf16)] if keep else []),
        compiler_params=_cp(("parallel", "arbitrary")),
    )(*a_arrays, *a_params, b, *([add] if has_add else []))
    return tuple(res) if keep else res[0]


def _input_cotangent(name, a_list, b_list, x, gain, add, side=None):
    m = a_list[0].shape[0]
    tm = _tile(m, 256)
    n_g = len(a_list)
    srcs, per_peer = side if side is not None else ([], False)
    n_s = len(srcs)

    def body(*refs):
        x_ref, g_ref, add_ref = refs[2 * n_g:2 * n_g + 3]
        src_refs = refs[2 * n_g + 3:2 * n_g + 3 + n_s]
        dx_ref, dg_ref = refs[2 * n_g + 3 + n_s:2 * n_g + 5 + n_s]
        _side_exchange(src_refs, refs[2 * n_g + 5 + n_s:2 * n_g + 5 + 2 * n_s], per_peer, refs[2 * n_g + 5 + 2 * n_s:], m // tm)
        d_u = None
        for g in range(n_g):
            r = lax.dot_general(refs[g][...].astype(bf16), refs[n_g + g][...].astype(bf16), (((1,), (0,)), ((), ())),
                                preferred_element_type=f32)
            d_u = r if d_u is None else d_u + r
        _, vjp = jax.vjp(_rms, x_ref[...], g_ref[...])
        d_x, d_gain = vjp(d_u)
        dx_ref[...] = d_x + add_ref[...]

        @pl.when(pl.program_id(0) == 0)
        def _():
            dg_ref[...] = jnp.zeros_like(dg_ref)

        dg_ref[...] += d_gain

    rows = pl.BlockSpec((tm, x.shape[1]), lambda i: (i, 0))
    whole = lambda b: pl.BlockSpec(b.shape, lambda i: (0, 0))
    res = pl.pallas_call(
        body, name=name, grid=(m // tm,),
        in_specs=[pl.BlockSpec((tm, a.shape[1]), lambda i: (i, 0)) for a in a_list] + [whole(b) for b in b_list]
        + [rows, whole(gain), rows] + [_HBM_SPEC] * n_s,
        out_specs=[rows, whole(gain)] + [_HBM_SPEC] * n_s,
        out_shape=[jax.ShapeDtypeStruct(x.shape, f32), jax.ShapeDtypeStruct(gain.shape, f32)] + _side_out_shapes(srcs, per_peer),
        scratch_shapes=_side_sems(n_s),
        compiler_params=_cp(("arbitrary",)),
    )(*a_list, *b_list, x, gain, add, *srcs)
    return res[0], res[1], list(res[2:])


def _pieces(ref, widths):
    out, off = [], 0
    for w in widths:
        out.append(ref[:, off:off + w].astype(f32))
        off += w
    return out


def _store_pieces(ref, widths, vals, add_ref=None):
    off = 0
    for w, v in zip(widths, vals):
        ref[:, off:off + w] = (v if add_ref is None else v + add_ref[:, off:off + w]).astype(ref.dtype)
        off += w


def _rows_fwd(name, fn, consts, rows, params, outs, n_sums=0, tm=512, dtypes=None):
    t = (consts + rows)[0][0].shape[0]
    tm = min(tm, t)
    ins = consts + rows
    n_in, n_p, n_o = len(ins), len(params), len(outs)
    dtypes = dtypes or [f32] * n_o

    def body(*refs):
        in_refs, p_refs = refs[:n_in], refs[n_in:n_in + n_p]
        o_refs, s_refs = refs[n_in + n_p:n_in + n_p + n_o], refs[n_in + n_p + n_o:]
        vals = []
        for r, (_, widths) in zip(in_refs, ins):
            vals += _pieces(r, widths)
        res = fn(*vals, *[p[...] for p in p_refs])
        pos = 0
        for r, widths in zip(o_refs, outs):
            _store_pieces(r, widths, res[pos:pos + len(widths)])
            pos += len(widths)

        @pl.when(pl.program_id(0) == 0)
        def _():
            for s in s_refs:
                s[...] = jnp.zeros_like(s)

        for s, v in zip(s_refs, res[pos:]):
            s[...] += v

    row_spec = lambda w: pl.BlockSpec((tm, w), lambda i: (i, 0))
    full = lambda p: pl.BlockSpec(p.shape, lambda i: (0,) * p.ndim)
    return pl.pallas_call(
        body, name=name, grid=(t // tm,),
        in_specs=[row_spec(sum(w)) for _, w in ins] + [full(p) for p in params],
        out_specs=[row_spec(sum(w)) for w in outs] + [pl.BlockSpec((1, 1), lambda i: (0, 0))] * n_sums,
        out_shape=[jax.ShapeDtypeStruct((t, sum(w)), dt) for w, dt in zip(outs, dtypes)] + [jax.ShapeDtypeStruct((1, 1), f32)] * n_sums,
        compiler_params=_cp(("arbitrary",)),
    )(*[a for a, _ in ins], *params)


def _rows_bwd(name, fn, consts, rows, params, outs, cts, n_sums=0, add=None, tm=512, dtypes=None):
    t = (consts + rows)[0][0].shape[0]
    tm = min(tm, t)
    n_c, n_r, n_p, n_o = len(consts), len(rows), len(params), len(outs)
    has_add = add is not None
    dtypes = dtypes or [f32] * n_r

    def body(*refs):
        pos = 0
        c_refs = refs[pos:pos + n_c]; pos += n_c
        r_refs = refs[pos:pos + n_r]; pos += n_r
        p_refs = refs[pos:pos + n_p]; pos += n_p
        ct_refs = refs[pos:pos + n_o]; pos += n_o
        add_ref = refs[pos] if has_add else None
        pos += 1 if has_add else 0
        dr_refs = refs[pos:pos + n_r]; pos += n_r
        dp_refs = refs[pos:pos + n_p]; pos += n_p
        s_refs = refs[pos:pos + n_sums]
        cvals, rvals = [], []
        for r, (_, widths) in zip(c_refs, consts):
            cvals += _pieces(r, widths)
        for r, (_, widths) in zip(r_refs, rows):
            rvals += _pieces(r, widths)
        pvals = [p[...] for p in p_refs]
        ctv = []
        for r, widths in zip(ct_refs, outs):
            ctv += _pieces(r, widths)
        ctv += [jnp.ones((1, 1), f32)] * n_sums
        primal, vjp = jax.vjp(lambda *rp: tuple(fn(*cvals, *rp)), *rvals, *pvals)
        g = vjp(tuple(ctv))
        pos = 0
        for idx, (r, (_, widths)) in enumerate(zip(dr_refs, rows)):
            _store_pieces(r, widths, g[pos:pos + len(widths)], add_ref if idx == 0 else None)
            pos += len(widths)

        @pl.when(pl.program_id(0) == 0)
        def _():
            for acc in list(dp_refs) + list(s_refs):
                acc[...] = jnp.zeros_like(acc)

        for dp, v in zip(dp_refs, g[pos:]):
            dp[...] += v
        for s, v in zip(s_refs, primal[len(primal) - n_sums:]):
            s[...] += v

    row_spec = lambda w: pl.BlockSpec((tm, w), lambda i: (i, 0))
    full = lambda p: pl.BlockSpec(p.shape, lambda i: (0,) * p.ndim)
    args = [a for a, _ in consts + rows] + list(params) + list(cts) + ([add] if has_add else [])
    res = pl.pallas_call(
        body, name=name, grid=(t // tm,),
        in_specs=[row_spec(sum(w)) for _, w in consts + rows] + [full(p) for p in params]
        + [row_spec(sum(w)) for w in outs] + ([row_spec(add.shape[1])] if has_add else []),
        out_specs=[row_spec(sum(w)) for _, w in rows] + [full(p) for p in params]
        + [pl.BlockSpec((1, 1), lambda i: (0, 0))] * n_sums,
        out_shape=[jax.ShapeDtypeStruct((t, sum(w)), dt) for (_, w), dt in zip(rows, dtypes)]
        + [jax.ShapeDtypeStruct(p.shape, f32) for p in params] + [jax.ShapeDtypeStruct((1, 1), f32)] * n_sums,
        compiler_params=_cp(("arbitrary",)),
    )(*args)
    return res[:n_r], res[n_r:n_r + n_p] + res[n_r + n_p:]


def _matmul_then_vjp(name, a, b, mode, fn, rows, dtypes, params=(), first_cts=(), add=None, tm=256):
    m, k = a.shape
    tm = min(tm, m)
    dims = (((1,), (0,)), ((), ())) if mode == "nn" else (((1,), (1,)), ((), ()))
    n_r, n_p, n_c = len(rows), len(params), len(first_cts)
    has_add = add is not None

    def body(*refs):
        a_ref, b_ref = refs[:2]
        pos = 2
        r_refs = refs[pos:pos + n_r]; pos += n_r
        p_refs = refs[pos:pos + n_p]; pos += n_p
        c_refs = refs[pos:pos + n_c]; pos += n_c
        add_ref = refs[pos] if has_add else None
        pos += 1 if has_add else 0
        dr_refs = refs[pos:pos + n_r]; pos += n_r
        dp_refs = refs[pos:pos + n_p]
        ct = lax.dot_general(a_ref[...].astype(bf16), b_ref[...].astype(bf16), dims, preferred_element_type=f32)
        if has_add:
            ct = ct + add_ref[...]
        rvals = []
        for r, (_, widths) in zip(r_refs, rows):
            rvals += _pieces(r, widths)
        _, vjp = jax.vjp(lambda *rp: tuple(fn(*rp)), *rvals, *[p[...] for p in p_refs])
        g = vjp(tuple(c[...].astype(f32) for c in c_refs) + (ct,))
        pos = 0
        for r, (_, widths) in zip(dr_refs, rows):
            _store_pieces(r, widths, g[pos:pos + len(widths)])
            pos += len(widths)

        @pl.when(pl.program_id(0) == 0)
        def _():
            for dp in dp_refs:
                dp[...] = jnp.zeros_like(dp)

        for dp, v in zip(dp_refs, g[pos:]):
            dp[...] += v

    row_spec = lambda w: pl.BlockSpec((tm, w), lambda i: (i, 0))
    whole = lambda p: pl.BlockSpec(p.shape, lambda i: (0, 0))
    res = pl.pallas_call(
        body, name=name, grid=(m // tm,),
        in_specs=[row_spec(k), whole(b)] + [row_spec(sum(w)) for _, w in rows] + [whole(p) for p in params]
        + [row_spec(c.shape[1]) for c in first_cts] + ([row_spec(add.shape[1])] if has_add else []),
        out_specs=[row_spec(sum(w)) for _, w in rows] + [whole(p) for p in params],
        out_shape=[jax.ShapeDtypeStruct((m, sum(w)), dt) for (_, w), dt in zip(rows, dtypes)]
        + [jax.ShapeDtypeStruct(p.shape, f32) for p in params],
        compiler_params=_cp(("arbitrary",)),
    )(a, b, *[r for r, _ in rows], *params, *first_cts, *([add] if has_add else []))
    return res[:n_r], res[n_r:]


def _rms(x, g):
    return x * lax.rsqrt(jnp.mean(x * x, axis=-1, keepdims=True) + NORM_EPS) * g


def _fn_rms(x, g):
    return (_rms(x, g),)


def _fn_rwkv_pre(r, k, v, wd, ad, gd, w0, w_up, a0, a_up, g_up, k_k, k_a):
    nn, _, _ = _make_mm(False, False)
    w_log = -_sigmoid(w0 + nn(jnp.tanh(wd), w_up)) * 0.6065306597126334
    a = _sigmoid(a0 + nn(ad, a_up))
    g = nn(_sigmoid(gd), g_up)
    kk = k * k_k
    kk = kk * lax.rsqrt(jnp.maximum(_head_sum(kk * kk), 1e-24))
    k2 = k * (1.0 + (a - 1.0) * k_a)
    return r, k2, v, w_log, -kk, kk * a, g


def _fn_rwkv_post(y, r, k2, v, g, gn_g, gn_b, r_k):
    mean = _head_sum(y) * (1.0 / HD)
    yc = y - mean
    var = _head_sum(yc * yc) * (1.0 / HD)
    yn = yc * lax.rsqrt(var + GN_EPS) * gn_g + gn_b
    bonus = _head_sum(r * k2 * r_k) * v
    return ((yn + bonus) * g,)


def _fn_merge(a_fox, a_rwkv, a_mem, g_fox, g_rwkv, g_mem):
    return (_sigmoid(g_fox) * a_fox + _sigmoid(g_rwkv) * a_rwkv + _sigmoid(g_mem) * a_mem,)


def _fn_post1(y, x, post1_g, pre2_g):
    h1 = x + _rms(y, post1_g)
    return h1, _rms(h1, pre2_g)


def _fn_swiglu(gp, up):
    return (gp * _sigmoid(gp) * up,)


def _fn_final(target, ffn, h1, post2_g):
    err = h1 + _rms(ffn, post2_g) - target
    per_row = jnp.mean(err * err, axis=-1, keepdims=True)
    return (0.5 * jnp.sum(per_row, axis=0, keepdims=True),)


def _shift_down(x):
    row = lax.broadcasted_iota(jnp.int32, x.shape, 0)
    return jnp.where(row == 0, 0.0, pltpu.roll(x, 1, 0))


def _shift_up(x):
    s = x.shape[0]
    row = lax.broadcasted_iota(jnp.int32, x.shape, 0)
    return jnp.where(row == s - 1, 0.0, pltpu.roll(x, s - 1, 0))


def _tokshift_fwd(p, mu, batch, seq):
    w = p.shape[1]
    tc = _tile(w, 384)

    def body(p_ref, mu_ref, o_ref):
        x = p_ref[...].astype(f32)
        o_ref[...] = (x + (_shift_down(x) - x) * mu_ref[...]).astype(o_ref.dtype)

    return pl.pallas_call(
        body, name="tokshift_fwd", grid=(w // tc, batch),
        in_specs=[pl.BlockSpec((seq, tc), lambda j, b: (b, j)), pl.BlockSpec((1, tc), lambda j, b: (0, j))],
        out_specs=pl.BlockSpec((seq, tc), lambda j, b: (b, j)),
        out_shape=jax.ShapeDtypeStruct(p.shape, bf16),
        compiler_params=_cp(("parallel", "arbitrary")),
    )(p, mu)


def _tokshift_bwd(p, mu, dps, batch, seq):
    w = p.shape[1]
    tc = _tile(w, 384)

    def body(p_ref, mu_ref, d_ref, dp_ref, dmu_ref):
        x, mu_v, d = p_ref[...].astype(f32), mu_ref[...], d_ref[...].astype(f32)
        dp_ref[...] = (d * (1.0 - mu_v) + _shift_up(d * mu_v)).astype(dp_ref.dtype)

        @pl.when(pl.program_id(1) == 0)
        def _():
            dmu_ref[...] = jnp.zeros_like(dmu_ref)

        dmu_ref[...] += jnp.sum(d * (_shift_down(x) - x), axis=0, keepdims=True)

    return pl.pallas_call(
        body, name="tokshift_bwd", grid=(w // tc, batch),
        in_specs=[pl.BlockSpec((seq, tc), lambda j, b: (b, j)), pl.BlockSpec((1, tc), lambda j, b: (0, j)),
                  pl.BlockSpec((seq, tc), lambda j, b: (b, j))],
        out_specs=[pl.BlockSpec((seq, tc), lambda j, b: (b, j)), pl.BlockSpec((1, tc), lambda j, b: (0, j))],
        out_shape=[jax.ShapeDtypeStruct(p.shape, bf16), jax.ShapeDtypeStruct(mu.shape, f32)],
        compiler_params=_cp(("parallel", "arbitrary")),
    )(p, mu, dps)


def _cum_block(seq):
    return _tile(seq, 256)


def _fox_gate_fwd(f, bias, batch, seq):
    cb = _cum_block(seq)

    def body(f_ref, b_ref, c_ref):
        row = lax.broadcasted_iota(jnp.int32, (cb, cb), 0)
        col = lax.broadcasted_iota(jnp.int32, (cb, cb), 1)
        tri = (col <= row).astype(f32)
        carry = jnp.zeros((1, 128), f32)
        for i in range(seq // cb):
            z = f_ref[i * cb:(i + 1) * cb, :] + b_ref[...]
            ls = jnp.minimum(z, 0.0) - jnp.log(1.0 + jnp.exp(-jnp.abs(z)))
            c = _dg(tri, ls, (((1,), (0,)), ((), ())), True) + carry
            c_ref[i * cb:(i + 1) * cb, :] = c
            carry = c[cb - 1:cb, :]

    return pl.pallas_call(
        body, name="fox_gate_fwd", grid=(batch,),
        in_specs=[pl.BlockSpec((seq, 128), lambda b: (b, 0)), pl.BlockSpec((1, 128), lambda b: (0, 0))],
        out_specs=pl.BlockSpec((seq, 128), lambda b: (b, 0)),
        out_shape=jax.ShapeDtypeStruct(f.shape, f32),
        compiler_params=_cp(("arbitrary",)),
    )(f, bias)


def _fox_gate_bwd(f, bias, dc_a, dc_b, batch, seq):
    cb = _cum_block(seq)

    def body(f_ref, b_ref, da_ref, db_ref, df_ref, dbias_ref):
        row = lax.broadcasted_iota(jnp.int32, (cb, cb), 0)
        col = lax.broadcasted_iota(jnp.int32, (cb, cb), 1)
        triu = (col >= row).astype(f32)

        @pl.when(pl.program_id(0) == 0)
        def _():
            dbias_ref[...] = jnp.zeros_like(dbias_ref)

        lane = lax.broadcasted_iota(jnp.int32, (1, 128), 1)

        def by_head(blk):
            out = jnp.zeros((cb, 128), f32)
            for p in range(HEADS // 2):
                for e in range(2):
                    out = jnp.where(lane == 2 * p + e, _pick_lane(blk[:, p * 128:(p + 1) * 128], e), out)
            return out

        carry = jnp.zeros((1, 128), f32)
        tot = jnp.zeros((1, 128), f32)
        for i in reversed(range(seq // cb)):
            sl = slice(i * cb, (i + 1) * cb)
            dc = by_head(da_ref[sl, :] + db_ref[sl, :])
            dls = _dg(triu, dc, (((1,), (0,)), ((), ())), True) + carry
            carry = dls[0:1, :]
            df = dls * _sigmoid(-(f_ref[sl, :] + b_ref[...]))
            df_ref[sl, :] = df.astype(df_ref.dtype)
            tot = tot + jnp.sum(df, axis=0, keepdims=True)
        dbias_ref[...] += tot

    return pl.pallas_call(
        body, name="fox_gate_bwd", grid=(batch,),
        in_specs=[pl.BlockSpec((seq, 128), lambda b: (b, 0)), pl.BlockSpec((1, 128), lambda b: (0, 0)),
                  pl.BlockSpec((seq, HW), lambda b: (b, 0)), pl.BlockSpec((seq, HW), lambda b: (b, 0))],
        out_specs=[pl.BlockSpec((seq, 128), lambda b: (b, 0)), pl.BlockSpec((1, 128), lambda b: (0, 0))],
        out_shape=[jax.ShapeDtypeStruct(f.shape, bf16), jax.ShapeDtypeStruct((1, 128), f32)],
        compiler_params=_cp(("arbitrary",)),
    )(f, bias, dc_a, dc_b)


_HBM_SPEC = pl.BlockSpec(memory_space=pltpu.HBM)


def _side_out_shapes(srcs, per_peer):
    return [jax.ShapeDtypeStruct(((N_DEV,) + tuple(s.shape[1:] if per_peer else s.shape)), s.dtype) for s in srcs]


def _side_sems(n):
    if n == 0:
        return []
    return [pltpu.SemaphoreType.DMA((n, N_DEV - 1)), pltpu.SemaphoreType.DMA((n, N_DEV - 1)), pltpu.SemaphoreType.DMA((n,))]


def _peer_copies(src_refs, dst_refs, per_peer, sems):
    send_sems, recv_sems, local_sems = sems
    x, y, c = lax.axis_index("x"), lax.axis_index("y"), lax.axis_index("c")
    me = 4 * x + 2 * y + c

    def remote(src, dst, t, k, to):
        return pltpu.make_async_remote_copy(src_ref=src, dst_ref=dst, send_sem=send_sems.at[t, k - 1],
                                            recv_sem=recv_sems.at[t, k - 1], device_id=to,
                                            device_id_type=pl.DeviceIdType.MESH)

    direct, relays = [], []
    for t, (s, d) in enumerate(zip(src_refs, dst_refs)):
        direct.append((t, 0, pltpu.make_async_copy(s.at[me] if per_peer else s, d.at[me], local_sems.at[t])))
        for k in range(1, N_DEV):
            px = 1 - x if k & 4 else x
            py = 1 - y if k & 2 else y
            pc = 1 - c if k & 1 else c
            if per_peer:
                direct.append((t, k, remote(s.at[4 * px + 2 * py + pc], d.at[me], t, k, (px, py, pc))))
            elif k == 1 or not k & 1:
                direct.append((t, k, remote(s, d.at[me], t, k, (px, py, pc))))
            else:
                origin = d.at[4 * px + 2 * py + c]
                relays.append((t, k - 1, remote(origin, origin, t, k, (x, y, 1 - c))))
    return direct, relays


def _exchange_start(direct):
    for _, _, cp in direct:
        cp.start()


def _exchange_relay(direct, relays):
    landed = {(t, k): cp for t, k, cp in direct}
    for t, j, cp in relays:
        landed[(t, j)].wait_recv()
        cp.start()


def _exchange_finish(direct, relays):
    relayed = {(t, j) for t, j, _ in relays}
    for t, k, cp in direct:
        if k == 0:
            cp.wait()
        else:
            cp.wait_send()
            if (t, k) not in relayed:
                cp.wait_recv()
    for _, _, cp in relays:
        cp.wait()


def _side_exchange(src_refs, dst_refs, per_peer, sems, *grid):
    if not src_refs:
        return
    step, total = 0, 1
    for a, n in enumerate(grid):
        step, total = step * n + pl.program_id(a), total * n

    @pl.when(step == 0)
    def _():
        _exchange_start(_peer_copies(src_refs, dst_refs, per_peer, sems)[0])

    @pl.when(step == (3 * total) // 4)
    def _():
        _exchange_relay(*_peer_copies(src_refs, dst_refs, per_peer, sems))

    @pl.when(step == total - 1)
    def _():
        _exchange_finish(*_peer_copies(src_refs, dst_refs, per_peer, sems))


def _exchange(name, srcs, per_peer):
    n = len(srcs)

    def body(*refs):
        direct, relays = _peer_copies(refs[:n], refs[n:2 * n], per_peer, refs[2 * n:])
        _exchange_start(direct)
        _exchange_relay(direct, relays)
        _exchange_finish(direct, relays)

    return pl.pallas_call(
        body, name=name, in_specs=[_HBM_SPEC] * n, out_specs=[_HBM_SPEC] * n,
        out_shape=_side_out_shapes(srcs, per_peer), scratch_shapes=_side_sems(n),
    )(*srcs)


FOX_T = 512
_NEG = -1e30
_D2 = (((1,), (1,)), ((), ()))
_D1 = (((1,), (0,)), ((), ()))
_D0 = (((0,), (0,)), ((), ()))


def _bdot(a, b, dims):
    return lax.dot_general(a.astype(bf16), b.astype(bf16), dims, preferred_element_type=f32)


def _pick_lane(x, lane):
    idx = lax.broadcasted_iota(jnp.int32, x.shape, 1)
    return jnp.sum(jnp.where(idx == lane, x, 0.0), axis=1, keepdims=True)


def _pick_row(x, row):
    idx = lax.broadcasted_iota(jnp.int32, x.shape, 0)
    return jnp.sum(jnp.where(idx == row, x, 0.0), axis=0, keepdims=True)


def _fox_fwd(qkv, c, c_rows, batch, seq, side=None):
    t = min(FOX_T, seq)
    nq = seq // t
    scale = HD ** -0.5
    srcs, per_peer = side if side is not None else ([], False)
    n_s = len(srcs)

    def body(*refs):
        q_ref, k_ref, v_ref, cq_ref, ck_ref = refs[:5]
        o_ref, lse_ref = refs[5 + n_s:7 + n_s]
        _side_exchange(refs[5:5 + n_s], refs[7 + n_s:7 + 2 * n_s], per_peer, refs[7 + 2 * n_s:], batch, PAIRS, nq)
        pair, i = pl.program_id(1), pl.program_id(2)
        lane = lax.broadcasted_iota(jnp.int32, (1, PAIR_W), 1)
        first = (lane // HD) == 0
        mine = [first, jnp.logical_not(first)]
        q = q_ref[...] * scale
        qs = [jnp.where(mine[e], q, 0.0) for e in range(2)]
        cqs = [_pick_lane(cq_ref[...], 2 * pair + e) for e in range(2)]
        causal = lax.broadcasted_iota(jnp.int32, (t, t), 1) <= lax.broadcasted_iota(jnp.int32, (t, t), 0)

        def block(j, carry, diagonal):
            rows = pl.ds(pl.multiple_of(j * t, t), t)
            kj, vj = k_ref[rows, :], v_ref[rows, :]
            ck_blk = ck_ref[0, :, rows]
            out = []
            for e in range(2):
                m, acc = carry[2 * e:2 * e + 2]
                s = _bdot(qs[e], kj, _D2) + cqs[e] - _pick_row(ck_blk, 2 * pair + e)
                if diagonal:
                    s = jnp.where(causal, s, _NEG)
                m_new = jnp.maximum(m, jnp.max(s, axis=1, keepdims=True))
                p = jnp.exp(s - m_new)
                out += [m_new, jnp.exp(m - m_new) * acc + _bdot(p, jnp.where(mine[e], vj, 1.0), _D1)]
            return tuple(out)

        init = (jnp.full((t, 1), _NEG, f32), jnp.zeros((t, PAIR_W), f32)) * 2
        carry = lax.fori_loop(0, i, lambda j, cr: block(j, cr, False), init)
        m0, a0, m1, a1 = block(i, carry, True)
        l0, l1 = _pick_lane(a0, HD), _pick_lane(a1, 0)
        o_ref[...] = jnp.where(first, a0 / l0, a1 / l1)
        lse_ref[...] = jnp.where(lane == 0, m0 + jnp.log(l0), jnp.where(lane == 1, m1 + jnp.log(l1), 0.0))

    q_spec = pl.BlockSpec((t, PAIR_W), lambda b, p, i: (b * nq + i, p))
    res = pl.pallas_call(
        body, name="fox_attn_fwd", grid=(batch, PAIRS, nq),
        in_specs=[q_spec,
                  pl.BlockSpec((seq, PAIR_W), lambda b, p, i: (b, PAIRS + p)),
                  pl.BlockSpec((seq, PAIR_W), lambda b, p, i: (b, 2 * PAIRS + p)),
                  pl.BlockSpec((t, 128), lambda b, p, i: (b * nq + i, 0)),
                  pl.BlockSpec((1, 8, seq), lambda b, p, i: (b, 0, 0))] + [_HBM_SPEC] * n_s,
        out_specs=[q_spec, q_spec] + [_HBM_SPEC] * n_s,
        out_shape=[jax.ShapeDtypeStruct((batch * seq, HW), f32)] * 2 + _side_out_shapes(srcs, per_peer),
        scratch_shapes=_side_sems(n_s),
        compiler_params=_cp(("arbitrary", "arbitrary", "arbitrary")),
    )(qkv, qkv, qkv, c, c_rows, *srcs)
    return res[0], res[1], list(res[2:])


def _fox_bwd(qkv, c, c_rows, o, lse, do, batch, seq):
    t = min(FOX_T, seq)
    nq = seq // t
    scale = HD ** -0.5

    def body(q_ref, k_ref, v_ref, cq_ref, ck_ref, o_ref, lse_ref, do_ref,
             dq_ref, dk_ref, dv_ref, dcq_ref, dck_ref, acc0, acc1):
        pair, i = pl.program_id(1), pl.program_id(2)
        accs = [acc0, acc1]

        @pl.when(i == 0)
        def _():
            dv_ref[...] = jnp.zeros_like(dv_ref)
            acc0[...] = jnp.zeros_like(acc0)
            acc1[...] = jnp.zeros_like(acc1)

        lane = lax.broadcasted_iota(jnp.int32, (1, PAIR_W), 1)
        first = (lane // HD) == 0
        mine = [first, jnp.logical_not(first)]
        q, d_o, o_i = q_ref[...] * scale, do_ref[...], o_ref[...]
        q0s = [jnp.where(mine[e], q, 0.0) for e in range(2)]
        q1s = [jnp.where(mine[e], q, 1.0) for e in range(2)]
        dos = [jnp.where(mine[e], d_o, 0.0) for e in range(2)]
        deltas = [jnp.sum(dos[e] * o_i, axis=1, keepdims=True) for e in range(2)]
        lses = [_pick_lane(lse_ref[...], e) for e in range(2)]
        cqs = [_pick_lane(cq_ref[...], 2 * pair + e) for e in range(2)]
        causal = lax.broadcasted_iota(jnp.int32, (t, t), 1) <= lax.broadcasted_iota(jnp.int32, (t, t), 0)

        def block(j, dqs, diagonal):
            rows = pl.ds(pl.multiple_of(j * t, t), t)
            kj, vj = k_ref[rows, :], v_ref[rows, :]
            ck_blk = ck_ref[0, :, rows]
            out = []
            for e in range(2):
                s = _bdot(q0s[e], kj, _D2) + cqs[e] - _pick_row(ck_blk, 2 * pair + e)
                if diagonal:
                    s = jnp.where(causal, s, _NEG)
                p = jnp.exp(s - lses[e])
                ds = p * (_bdot(dos[e], vj, _D2) - deltas[e])
                dv_ref[rows, :] += _bdot(p, dos[e], _D0)
                accs[e][rows, :] += _bdot(ds, q1s[e], _D0)
                out.append(dqs[e] + _bdot(ds, jnp.where(mine[e], kj, 1.0), _D1))
            return tuple(out)

        zero = jnp.zeros((t, PAIR_W), f32)
        dqs = lax.fori_loop(0, i, lambda j, cr: block(j, cr, False), (zero, zero))
        dq0, dq1 = block(i, dqs, True)
        dq_ref[...] = jnp.where(first, dq0, dq1) * scale
        dcq_ref[...] = jnp.where(lane == 0, _pick_lane(dq0, HD), jnp.where(lane == 1, _pick_lane(dq1, 0), 0.0))

        @pl.when(i == nq - 1)
        def _():
            a0, a1 = acc0[...], acc1[...]
            dk_ref[...] = jnp.where(first, a0, a1)
            dck_ref[...] = jnp.where(lane == 0, -_pick_lane(a0, HD), jnp.where(lane == 1, -_pick_lane(a1, 0), 0.0))

    blk = lambda col: pl.BlockSpec((t, PAIR_W), lambda b, p, i: (b * nq + i, col * PAIRS + p))
    whole = lambda col: pl.BlockSpec((seq, PAIR_W), lambda b, p, i: (b, col * PAIRS + p))
    t_all = batch * seq
    return pl.pallas_call(
        body, name="fox_attn_bwd", grid=(batch, PAIRS, nq),
        in_specs=[blk(0), whole(1), whole(2),
                  pl.BlockSpec((t, 128), lambda b, p, i: (b * nq + i, 0)),
                  pl.BlockSpec((1, 8, seq), lambda b, p, i: (b, 0, 0)),
                  blk(0), blk(0), blk(0)],
        out_specs=[blk(0), whole(0), whole(0), blk(0), whole(0)],
        out_shape=[jax.ShapeDtypeStruct((t_all, HW), f32)] * 5,
        scratch_shapes=[pltpu.VMEM((seq, PAIR_W), f32), pltpu.VMEM((seq, PAIR_W), f32)],
        compiler_params=_cp(("parallel", "parallel", "arbitrary")),
    )(qkv, qkv, qkv, c, c_rows, o, lse, do)


MEM_TQ = 1024


def _mem_block(q, km, vm):
    nn, nt, _ = _make_mm(False, False)
    logits = nt(q, km) * (MEM_HD ** -0.5)
    m = lax.stop_gradient(jnp.max(logits, axis=-1, keepdims=True))
    e = jnp.exp(logits - m)
    return nn(e / jnp.sum(e, axis=-1, keepdims=True), vm)


def _mem_specs(seq, tq):
    nq = seq // tq
    qs = pl.BlockSpec((tq, MEM_HD), lambda b, h, i: (b * nq + i, h))
    ks = pl.BlockSpec((MEM_LEN, MEM_HD), lambda b, h, i: (b, h))
    vs = pl.BlockSpec((MEM_LEN, MEM_HD), lambda b, h, i: (b, MEM_HEADS + h))
    return nq, qs, ks, vs


def _mem_fwd(q, mem_kv, batch, seq):
    tq = min(MEM_TQ, seq)
    nq, qs, ks, vs = _mem_specs(seq, tq)

    def body(q_ref, k_ref, v_ref, o_ref):
        o_ref[...] = _mem_block(q_ref[...].astype(f32), k_ref[...], v_ref[...]).astype(o_ref.dtype)

    return pl.pallas_call(
        body, name="mem_attn_fwd", grid=(batch, MEM_HEADS, nq),
        in_specs=[qs, ks, vs], out_specs=qs, out_shape=jax.ShapeDtypeStruct(q.shape, bf16),
        compiler_params=_cp(("parallel", "parallel", "arbitrary")),
    )(q, mem_kv, mem_kv)


def _mem_bwd(q, mem_kv, do, batch, seq):
    tq = min(MEM_TQ, seq)
    nq, qs, ks, vs = _mem_specs(seq, tq)

    def body(q_ref, k_ref, v_ref, do_ref, dq_ref, dk_ref, dv_ref):
        _, vjp = jax.vjp(_mem_block, q_ref[...].astype(f32), k_ref[...], v_ref[...])
        dq, dk, dv = vjp(do_ref[...])
        dq_ref[...] = dq.astype(dq_ref.dtype)

        @pl.when(pl.program_id(2) == 0)
        def _():
            dk_ref[...] = jnp.zeros_like(dk_ref)
            dv_ref[...] = jnp.zeros_like(dv_ref)

        dk_ref[...] += dk
        dv_ref[...] += dv

    return pl.pallas_call(
        body, name="mem_attn_bwd", grid=(batch, MEM_HEADS, nq),
        in_specs=[qs, ks, vs, qs], out_specs=[qs, ks, ks],
        out_shape=[jax.ShapeDtypeStruct(q.shape, bf16), jax.ShapeDtypeStruct((batch * MEM_LEN, MEM_W), f32),
                   jax.ShapeDtypeStruct((batch * MEM_LEN, MEM_W), f32)],
        compiler_params=_cp(("parallel", "parallel", "arbitrary")),
    )(q, mem_kv, mem_kv, do)


@jax.custom_vjp
def _halves(x):
    c = x.shape[1] // 2
    return x[:, :c], x[:, c:]


_halves.defvjp(lambda x: ((x[:, :x.shape[1] // 2], x[:, x.shape[1] // 2:]), None),
               lambda _, g: (jnp.concatenate(g, axis=1),))


@jax.custom_vjp
def _lead_halves(x):
    n = x.shape[0] // 2
    return x[:n], x[n:]


_lead_halves.defvjp(lambda x: ((x[:x.shape[0] // 2], x[x.shape[0] // 2:]), None),
                    lambda _, g: (jnp.concatenate(g, axis=0),))


def _scan_chunk(s0, r, wl, k, v, a, b):
    nn, nt, tn = _make_mm(True, False)
    nn_exact, _, _ = _make_mm(True, True)
    _, nt_exact, _ = _make_mm(True, "split")
    hp, c, lanes = r.shape
    row = lax.broadcasted_iota(jnp.int32, (c, c), 0)
    col = lax.broadcasted_iota(jnp.int32, (c, c), 1)
    first = (lax.broadcasted_iota(jnp.int32, (1, 1, lanes), 2) // HD) == 0
    tri = jnp.broadcast_to((col <= row).astype(f32)[None], (hp, c, c))
    lg = nn_exact(tri, wl)
    lg_end = lg[:, c - 1:c, :]
    grow, shrink, to_end = jnp.exp(lg), jnp.exp(-lg), jnp.exp(lg_end - lg)
    rt, kt, bt, at = r * grow, k * shrink, b * shrink, a * jnp.exp(lg - wl)
    strict, incl = (col < row)[None], (col <= row)[None]
    twice = lambda t: jnp.concatenate([t, t], axis=0)
    queries = jnp.concatenate([at, rt], axis=1)
    per_head = jnp.concatenate([jnp.where(first, queries, 0.0), jnp.where(first, 0.0, queries)], axis=0)
    (ab, rb), (ak, rk) = _halves(nt_exact(per_head, twice(bt))), _halves(nt_exact(per_head, twice(kt)))
    l_ab = jnp.where(strict, ab, 0.0)
    a_ak = jnp.where(strict, ak, 0.0)
    a_rb = jnp.where(incl, rb, 0.0)
    a_rk = jnp.where(incl, rk, 0.0)
    inv = (col == row).astype(f32)[None] + l_ab
    power, n = l_ab, 1
    while 2 * n < c:
        power = nn(power, power)
        inv = inv + nn(inv, power)
        n *= 2

    def apply(m, t):
        lo, hi = _lead_halves(nn(m, twice(t)))
        return jnp.where(first, lo, hi)

    sa = apply(inv, nt(at, s0) + apply(a_ak, v))
    y = nt(rt, s0) + apply(a_rk, v) + apply(a_rb, sa)
    same_head = ((lax.broadcasted_iota(jnp.int32, (lanes, lanes), 0) // HD)
                 == (lax.broadcasted_iota(jnp.int32, (lanes, lanes), 1) // HD))[None]
    s1 = s0 * jnp.exp(lg_end) + jnp.where(same_head, tn(v, k * to_end) + tn(sa, b * to_end), 0.0)
    return y, s1


PAIRS = HEADS // 2
PAIR_W = 2 * HD
SCAN_ARGS = (0, 3, 1, 2, 4, 5)


def _pair_stack(ref, off):
    return jnp.stack([ref[b, :, off + p * PAIR_W:off + (p + 1) * PAIR_W]
                      for b in range(ref.shape[0]) for p in range(PAIRS)])


def _pair_store(ref, off, val, add_ref=None):
    for b in range(ref.shape[0]):
        for p in range(PAIRS):
            sl = slice(off + p * PAIR_W, off + (p + 1) * PAIR_W)
            v = val[b * PAIRS + p]
            ref[b, :, sl] = v if add_ref is None else v + add_ref[b, :, sl]


def _scan_fwd(main6, batch, seq, side=None):
    c = min(SCAN_CHUNK, seq)
    nc = seq // c
    hp = batch * PAIRS
    srcs, per_peer = side if side is not None else ([], False)
    n_s = len(srcs)

    def body(*refs):
        z_ref, y_ref, s_ref, st = refs[0], refs[1 + n_s], refs[2 + n_s], refs[3 + 2 * n_s]
        _side_exchange(refs[1:1 + n_s], refs[3 + n_s:3 + 2 * n_s], per_peer, refs[4 + 2 * n_s:], nc)

        @pl.when(pl.program_id(0) == 0)
        def _():
            st[...] = jnp.zeros_like(st)

        s0 = st[...]
        s_ref[0] = s0
        y, s1 = _scan_chunk(s0, *[_pair_stack(z_ref, comp * HW) for comp in SCAN_ARGS])
        _pair_store(y_ref, 0, y)
        st[...] = s1

    res = pl.pallas_call(
        body, name="rwkv_scan_fwd", grid=(nc,),
        in_specs=[pl.BlockSpec((batch, c, 6 * HW), lambda i: (0, i, 0))] + [_HBM_SPEC] * n_s,
        out_specs=[pl.BlockSpec((batch, c, HW), lambda i: (0, i, 0)),
                   pl.BlockSpec((1, hp, PAIR_W, PAIR_W), lambda i: (i, 0, 0, 0))] + [_HBM_SPEC] * n_s,
        out_shape=[jax.ShapeDtypeStruct((batch, seq, HW), f32), jax.ShapeDtypeStruct((nc, hp, PAIR_W, PAIR_W), f32)]
        + _side_out_shapes(srcs, per_peer),
        scratch_shapes=[pltpu.VMEM((hp, PAIR_W, PAIR_W), f32)] + _side_sems(n_s),
        compiler_params=_cp(("arbitrary",)),
    )(main6.reshape(batch, seq, 6 * HW), *srcs)
    return res[0].reshape(batch * seq, HW), res[1], list(res[2:])


def _scan_bwd(main6, states, dy, extra, batch, seq, side=None):
    c = min(SCAN_CHUNK, seq)
    nc = seq // c
    hp = batch * PAIRS
    srcs, per_peer = side if side is not None else ([], False)
    n_s = len(srcs)

    def body(*refs):
        z_ref, s_ref, dy_ref, ex_ref = refs[:4]
        dz_ref, dst = refs[4 + n_s], refs[5 + 2 * n_s]
        _side_exchange(refs[4:4 + n_s], refs[5 + n_s:5 + 2 * n_s], per_peer, refs[6 + 2 * n_s:], nc)

        @pl.when(pl.program_id(0) == 0)
        def _():
            dst[...] = jnp.zeros_like(dst)

        _, vjp = jax.vjp(_scan_chunk, s_ref[0], *[_pair_stack(z_ref, comp * HW) for comp in SCAN_ARGS])
        g = vjp((_pair_stack(dy_ref, 0), dst[...]))
        dst[...] = g[0]
        for arg, comp in enumerate(SCAN_ARGS):
            _pair_store(dz_ref, comp * HW, g[1 + arg], ex_ref if comp < 3 else None)

    back = lambda i: (0, nc - 1 - i, 0)
    wide = pl.BlockSpec((batch, c, 6 * HW), back)
    res = pl.pallas_call(
        body, name="rwkv_scan_bwd", grid=(nc,),
        in_specs=[wide, pl.BlockSpec((1, hp, PAIR_W, PAIR_W), lambda i: (nc - 1 - i, 0, 0, 0)),
                  pl.BlockSpec((batch, c, HW), back), pl.BlockSpec((batch, c, 3 * HW), back)] + [_HBM_SPEC] * n_s,
        out_specs=[wide] + [_HBM_SPEC] * n_s,
        out_shape=[jax.ShapeDtypeStruct((batch, seq, 6 * HW), f32)] + _side_out_shapes(srcs, per_peer),
        scratch_shapes=[pltpu.VMEM((hp, PAIR_W, PAIR_W), f32)] + _side_sems(n_s),
        compiler_params=_cp(("arbitrary",)),
    )(main6.reshape(batch, seq, 6 * HW), states, dy.reshape(batch, seq, HW), extra.reshape(batch, seq, 3 * HW), *srcs)
    return res[0].reshape(batch * seq, 6 * HW), list(res[1:])


def _pad_cols(x, width):
    return jnp.pad(x, ((0, 0), (0, width - x.shape[1])))


def _split_w_in(wt):
    z = lambda rows: jnp.zeros((rows, wt.shape[1]), wt.dtype)
    w_r = jnp.concatenate([wt[1544:3080], wt[3080:3144], z(64), wt[3144:3208], z(64), wt[3208:3336]], axis=0)
    return wt[:1536], jnp.concatenate([wt[1536:1544], z(120)], axis=0), w_r, wt[3336:3848], wt[3848:]


def _merge_w_in(g_qkv, g_f, g_r, g_mq, g_g):
    return jnp.concatenate([g_qkv, g_f[:8], g_r[:1536], g_r[1536:1600], g_r[1664:1728], g_r[1792:], g_mq, g_g], axis=0)


def _pad_lora(v):
    z64 = jnp.zeros((1, 64), v.dtype)
    return jnp.concatenate([v[:, :1536], v[:, 1536:1600], z64, v[:, 1600:1664], z64, v[:, 1664:]], axis=1)


def _unpad_lora(v):
    return jnp.concatenate([v[:, :1536], v[:, 1536:1600], v[:, 1664:1728], v[:, 1792:]], axis=1)


def _local_step(x, mem, target, w, p, late=None, early=None, last=None):
    batch, seq, _ = x.shape
    t = batch * seq
    x2, tg2, mem2 = x.reshape(t, D), target.reshape(t, D), mem.reshape(batch * MEM_LEN, D)
    w_qkv, w_f, w_r, w_mq, w_g3 = _split_w_in(w["w_in"])
    mu = _pad_lora(p["rwkv_mu"])
    bias = _pad_cols(p["fox_f_bias"], 128)
    r_k = p["rwkv_r_k"].reshape(1, HW)
    post_params = [p["rwkv_gn_g"], p["rwkv_gn_b"], r_k]
    rw_widths = [HW, HW, HW, LORA_PAD, LORA_PAD, LORA_PAD]
    six = [HW] * 6

    p_g, u = _matmul("proj_gate", _lazy(_fn_rms, [(x2, [D])], D, params=[p["pre1_g"]]), w_g3, "nt", out_dtype=bf16)
    p_qkv = _matmul("proj_qkv", u, w_qkv, "nt", out_dtype=bf16)
    p_f = _matmul("proj_f", u, w_f, "nt")
    p_r = _matmul("proj_rwkv", u, w_r, "nt", out_dtype=bf16)
    p_mq = _matmul("proj_memq", u, w_mq, "nt", out_dtype=bf16)

    c = _fox_gate_fwd(p_f, bias, batch, seq)
    c_rows = c[:, :HEADS].reshape(batch, seq, HEADS).transpose(0, 2, 1)
    fox_o, lse, gathered = _fox_fwd(p_qkv, c, c_rows, batch, seq, side=(late[0], False) if late else None)
    if late:
        w = {**w, **late[2](gathered, 0)}
    fox_out = fox_o.astype(bf16)

    w_up = jnp.pad(w["rwkv_w_up"].astype(f32), ((0, LORA_PAD - 64), (0, 0)))
    a_up = jnp.pad(w["rwkv_a_up"].astype(f32), ((0, LORA_PAD - 64), (0, 0)))
    pre_params = [p["rwkv_w0"], w_up, p["rwkv_a0"], a_up, w["rwkv_g_up"].astype(f32), p["rwkv_k_k"], p["rwkv_k_a"]]
    ps = _tokshift_fwd(p_r, mu, batch, seq)
    main6, g_rw = _rows_fwd("rwkv_pre", _fn_rwkv_pre, [], [(ps, rw_widths)], pre_params, [six, [HW]], tm=256)
    y_rw, states, gathered = _scan_fwd(main6, batch, seq, side=(late[1], False) if late else None)
    if late:
        w = {**w, **late[2](gathered, 1)}
    post_consts = []
    post_rows = [(y_rw, [HW]), (main6, [HW, HW, HW]), (g_rw, [HW])]
    fn_post = _fn_rwkv_post

    (rwkv_out,) = _rows_fwd("rwkv_post", fn_post, post_consts, post_rows, post_params, [[HW]], dtypes=[bf16], tm=256)

    mem_kv, memn = _matmul("proj_memkv", _lazy(_fn_rms, [(mem2, [D])], D, params=[p["mem_norm_g"]]), w["w_mem_kv"], "nn")
    mem_out = _mem_fwd(p_mq, mem_kv, batch, seq)

    a_fox = _matmul("out_fox", fox_out, w["w_fox_out"], "nn", out_dtype=bf16)
    a_rwkv = _matmul("out_rwkv", rwkv_out, w["w_rwkv_out"], "nn", out_dtype=bf16)
    a_mem = _matmul("out_mem", mem_out, w["w_mem_out"], "nn", out_dtype=bf16)
    merge_rows = [(a_fox, [D]), (a_rwkv, [D]), (a_mem, [D]), (p_g, [D, D, D])]
    yy, merged = _matmul("out_o", _lazy(_fn_merge, merge_rows, D), w["w_o"], "nn")
    post1_rows = [(yy, [D]), (x2, [D])]
    post1_params = [p["post1_g"], p["pre2_g"]]
    h1, u2 = _rows_fwd("post1", _fn_post1, [], post1_rows, post1_params, [[D], [D]], dtypes=[f32, bf16])
    gp = _matmul("ffn_gate", u2, w["w_ffn_gate"], "nt", out_dtype=bf16)
    up = _matmul("ffn_up", u2, w["w_ffn_up"], "nt", out_dtype=bf16)
    ffn, hmid = _matmul("ffn_down", _lazy(_fn_swiglu, [(gp, [D_FF]), (up, [D_FF])], D_FF), w["w_ffn_down"], "nn")
    final_rows = [(ffn, [D]), (h1, [D])]

    gw, gp_ = {}, {}
    (d_ffn, d_h1), (gp_["post2_g"], loss) = _rows_bwd("final", _fn_final, [(tg2, [D])], final_rows, [p["post2_g"]], [], [],
                                                      n_sums=1, dtypes=[bf16, f32])
    gw["w_ffn_down"] = _matmul("ffn_down_dw", hmid, d_ffn, "tn", out_dtype=bf16)
    (d_gp, d_up), _ = _matmul_then_vjp("ffn_down_dx", d_ffn, w["w_ffn_down"], "nt", _fn_swiglu,
                                       [(gp, [D_FF]), (up, [D_FF])], [bf16, bf16])
    gw["w_ffn_gate"] = _matmul("ffn_gate_dw", d_gp, u2, "tn", out_dtype=bf16)
    gw["w_ffn_up"] = _matmul("ffn_up_dw", d_up, u2, "tn", out_dtype=bf16)
    d_u2_gate = _matmul("ffn_gate_dx", d_gp, w["w_ffn_gate"], "nn")
    (d_yy, d_x_res), (gp_["post1_g"], gp_["pre2_g"]) = _matmul_then_vjp(
        "ffn_up_dx", d_up, w["w_ffn_up"], "nn", _fn_post1, post1_rows, [bf16, f32], params=post1_params,
        first_cts=[d_h1], add=d_u2_gate)
    gw["w_o"] = _matmul("out_o_dw", merged, d_yy, "tn", out_dtype=bf16)
    (d_a_fox, d_a_rwkv, d_a_mem, d_p_g), _ = _matmul_then_vjp("out_o_dx", d_yy, w["w_o"], "nt", _fn_merge, merge_rows,
                                                             [bf16] * 4)
    d_fox_out = _matmul("out_fox_dx", d_a_fox, w["w_fox_out"], "nt")
    gw["w_fox_out"] = _matmul("out_fox_dw", fox_out, d_a_fox, "tn", out_dtype=bf16)
    gw["w_rwkv_out"] = _matmul("out_rwkv_dw", rwkv_out, d_a_rwkv, "tn", out_dtype=bf16)
    d_mem_out = _matmul("out_mem_dx", d_a_mem, w["w_mem_out"], "nt")
    gw["w_mem_out"] = _matmul("out_mem_dw", mem_out, d_a_mem, "tn", out_dtype=bf16)

    d_p_mq, d_km, d_vm = _mem_bwd(p_mq, mem_kv, d_mem_out, batch, seq)
    d_mem_kv = jnp.concatenate([d_km, d_vm], axis=1).astype(bf16)
    gw["w_mem_kv"] = _matmul("proj_memkv_dw", memn, d_mem_kv, "tn", out_dtype=bf16)
    d_memn = _matmul("proj_memkv_dx", d_mem_kv, w["w_mem_kv"], "nt")
    _, (gp_["mem_norm_g"],) = _rows_bwd("rms_mem_bwd", _fn_rms, [], [(mem2, [D])], [p["mem_norm_g"]], [[D]], [d_memn])

    d_q, d_k, d_v, d_cq, d_ck = _fox_bwd(p_qkv, c, c_rows, fox_o, lse, d_fox_out, batch, seq)
    d_p_qkv = jnp.concatenate([d_q, d_k, d_v], axis=1).astype(bf16)
    d_p_f, d_bias = _fox_gate_bwd(p_f, bias, d_cq, d_ck, batch, seq)
    gp_["fox_f_bias"] = d_bias[:, :HEADS]

    (d_y_rw, d_main6_post, d_g_rw), (gp_["rwkv_gn_g"], gp_["rwkv_gn_b"], d_rk) = _matmul_then_vjp(
        "out_rwkv_dx", d_a_rwkv, w["w_rwkv_out"], "nt", fn_post, post_rows, [f32] * 3, params=post_params)
    gp_["rwkv_r_k"] = d_rk.reshape(1, HEADS, HD)
    d_main6, early_got = _scan_bwd(main6, states, d_y_rw, d_main6_post, batch, seq,
                                   side=(early(gw), True) if early else None)

    def fn_pre_sum(*args):
        return _fn_rwkv_pre(*args)

    (d_ps,), d_pre = _rows_bwd("rwkv_pre_bwd", fn_pre_sum, [], [(ps, rw_widths)], pre_params, [six, [HW]],
                               [d_main6, d_g_rw], tm=256, dtypes=[bf16])
    gp_["rwkv_w0"], d_w_up, gp_["rwkv_a0"], d_a_up, gw["rwkv_g_up"], gp_["rwkv_k_k"], gp_["rwkv_k_a"] = d_pre
    gw["rwkv_w_up"], gw["rwkv_a_up"] = d_w_up[:64], d_a_up[:64]
    d_p_r, d_mu = _tokshift_bwd(p_r, mu, d_ps, batch, seq)
    gp_["rwkv_mu"] = _unpad_lora(d_mu)

    gw["w_in"] = _merge_w_in(_matmul("proj_qkv_dw", d_p_qkv, u, "tn", out_dtype=bf16), _matmul("proj_f_dw", d_p_f, u, "tn", out_dtype=bf16),
                             _matmul("proj_rwkv_dw", d_p_r, u, "tn", out_dtype=bf16), _matmul("proj_memq_dw", d_p_mq, u, "tn", out_dtype=bf16),
                             _matmul("proj_gate_dw", d_p_g, u, "tn", out_dtype=bf16))
    d_x, gp_["pre1_g"], last_got = _input_cotangent(
        "proj_dx", [d_p_qkv, d_p_f, d_p_r, d_p_mq, d_p_g], [w_qkv, w_f, w_r, w_mq, w_g3], x2, p["pre1_g"], d_x_res,
        side=(last(gw), True) if last else None)
    return loss, d_x.reshape(x.shape), gw, gp_, early_got, last_got


def _adamw(name, recv, row_off, w, m, v):
    _, rows, cols = w.shape
    row_tiles = [t for t in range(16, min(rows, 128) + 1, 16) if rows % t == 0 and row_off % t == 0]
    if row_tiles:
        tr, tc = max(row_tiles), cols
        first, grid = row_off // tr, (rows // tr,)
        at = lambda i: (0, first + i, 0)
        mine = lambda i: (0, i, 0)
    else:
        assert row_off == 0 and recv.shape[1] == rows
        tr, tc = rows, 128
        grid = (cols // tc,)
        at = mine = lambda i: (0, 0, i)

    def body(g_ref, w_ref, m_ref, v_ref, go_ref, d_ref, mo_ref, vo_ref):
        g = g_ref[0].astype(f32)
        for s in range(1, N_DEV):
            g = g + g_ref[s].astype(f32)
        m_new = ADAM_B1 * m_ref[0] + (1.0 - ADAM_B1) * g
        v_new = ADAM_B2 * v_ref[0] + (1.0 - ADAM_B2) * (g * g)
        m_hat = m_new / (1.0 - ADAM_B1 ** ADAM_STEP)
        v_hat = v_new / (1.0 - ADAM_B2 ** ADAM_STEP)
        go_ref[0] = g
        d_ref[0] = -ADAM_LR * (m_hat / (jnp.sqrt(v_hat) + ADAM_EPS) + ADAM_WD * w_ref[0])
        mo_ref[0] = m_new
        vo_ref[0] = v_new

    spec = pl.BlockSpec((1, tr, tc), mine)
    return pl.pallas_call(
        body, name=name, grid=grid,
        in_specs=[pl.BlockSpec((N_DEV, tr, tc), at), spec, spec, spec],
        out_specs=[spec] * 4, out_shape=[jax.ShapeDtypeStruct(w.shape, f32)] * 4,
        compiler_params=_cp(("parallel",)),
    )(recv, w, m, v)


GROUPS = (
    ("in", ("w_in",), 0),
    ("memkv", ("w_mem_kv",), 0),
    ("ffn_gu", ("w_ffn_gate", "w_ffn_up"), 0),
    ("down_o", ("w_ffn_down", "w_o"), 0),
    ("outs", ("w_fox_out", "w_rwkv_out", "w_mem_out"), 0),
    ("lora", ("rwkv_w_up", "rwkv_a_up", "rwkv_g_up"), 0),
)
FIRST_GROUPS = ("in", "memkv")
LATE_GROUPS = (("down_o", "outs", "lora"), ("ffn_gu",))
EARLY_GRAD_GROUPS = ("memkv", "ffn_gu", "down_o", "outs")
LAST_GRAD_GROUPS = ("in", "lora")
SHARD_AXIS = {n: a for n, _, a in SHARDED}
SMALL_ROWS = 16
LOSS_LANES = 128


def _group_local(shards, members, join):
    parts = [shards[n].reshape(shards[n].shape[-2:]) for n in members]
    return parts[0] if len(parts) == 1 else jnp.concatenate(parts, axis=join)


def _group_split(arr, members, join, lead=False):
    out, off = {}, 0
    for n in members:
        shape = dict((k, s) for k, s, _ in SHARDED)[n]
        size = _block_shape(shape, SHARD_AXIS[n])[join]
        idx = [slice(None)] * arr.ndim
        idx[arr.ndim - 2 + join] = slice(off, off + size)
        out[n] = arr[tuple(idx)]
        off += size
    return out


def _full_from_blocks(blocks, axis):
    if axis == 0:
        return blocks.reshape(-1, blocks.shape[2])
    return blocks.transpose(1, 0, 2).reshape(blocks.shape[1], -1)


def _blocks_from_full(full, axis):
    if axis == 0:
        return full.reshape(N_DEV, -1, full.shape[1])
    return full.reshape(full.shape[0], N_DEV, -1).transpose(1, 0, 2)


def _assemble(gathered, names):
    out = {}
    for arr, g in zip(gathered, names):
        _, members, join = [grp for grp in GROUPS if grp[0] == g][0]
        for n, blk in _group_split(arr, members, join, lead=True).items():
            out[n] = _full_from_blocks(blk, SHARD_AXIS[n])
    return out


def _grad_blocks(gw, names):
    out = []
    for g in names:
        _, members, join = [grp for grp in GROUPS if grp[0] == g][0]
        parts = [_blocks_from_full(gw[n].astype(bf16), SHARD_AXIS[n]) for n in members]
        out.append(parts[0] if len(parts) == 1 else jnp.concatenate(parts, axis=1 + join))
    return out


def _small_pack(d):
    flat = jnp.concatenate([d[n].reshape(-1) for n, _ in REPLICATED])
    return jnp.pad(flat, (0, SMALL_ROWS * LANES - REPL_ELEMS)).reshape(SMALL_ROWS, LANES)


def _small_unpack(packed):
    out, flat, off = {}, packed.reshape(-1), 0
    for n, shape in REPLICATED:
        k = _rows_of((LANES,) + shape)
        out[n] = flat[off:off + k].reshape(shape)
        off += k
    return out


def kernel(x, mem, pre1_g, post1_g, pre2_g, post2_g, mem_norm_g, w_in, fox_f_bias, rwkv_mu, rwkv_w0, rwkv_w_up, rwkv_a0, rwkv_a_up, rwkv_g_up, rwkv_k_k, rwkv_k_a, rwkv_r_k, rwkv_gn_g, rwkv_gn_b, w_mem_kv, w_fox_out, w_rwkv_out, w_mem_out, w_o, w_ffn_gate, w_ffn_up, w_ffn_down, loss_target, m_pre1_g, m_post1_g, m_pre2_g, m_post2_g, m_mem_norm_g, m_w_in, m_fox_f_bias, m_rwkv_mu, m_rwkv_w0, m_rwkv_w_up, m_rwkv_a0, m_rwkv_a_up, m_rwkv_g_up, m_rwkv_k_k, m_rwkv_k_a, m_rwkv_r_k, m_rwkv_gn_g, m_rwkv_gn_b, m_w_mem_kv, m_w_fox_out, m_w_rwkv_out, m_w_mem_out, m_w_o, m_w_ffn_gate, m_w_ffn_up, m_w_ffn_down, v_pre1_g, v_post1_g, v_pre2_g, v_post2_g, v_mem_norm_g, v_w_in, v_fox_f_bias, v_rwkv_mu, v_rwkv_w0, v_rwkv_w_up, v_rwkv_a0, v_rwkv_a_up, v_rwkv_g_up, v_rwkv_k_k, v_rwkv_k_a, v_rwkv_r_k, v_rwkv_gn_g, v_rwkv_gn_b, v_w_mem_kv, v_w_fox_out, v_w_rwkv_out, v_w_mem_out, v_w_o, v_w_ffn_gate, v_w_ffn_up, v_w_ffn_down):
    args = dict(locals())
    turn = lambda n, a: jnp.swapaxes(a, 1, 2) if n in TRANSPOSED else a
    wts = {n: turn(n, args[n]) for n in WEIGHT_ORDER}
    ms = {n: turn(n, args["m_" + n]) for n in WEIGHT_ORDER}
    vs = {n: turn(n, args["v_" + n]) for n in WEIGHT_ORDER}

    groups = {g: (members, join) for g, members, join in GROUPS}
    w_bf16 = {n: wts[n].astype(bf16) for n, _, _ in SHARDED}

    def send(g):
        return _group_local(w_bf16, *groups[g])

    first = _exchange("gather_first", [send(g) for g in FIRST_GROUPS], per_peer=False)
    full = _assemble(first, FIRST_GROUPS)
    small_in = {n: (wts[n] if n == "rwkv_r_k" else wts[n].reshape(wts[n].shape[-2:])) for n, _ in REPLICATED}
    late = ([send(g) for g in LATE_GROUPS[0]], [send(g) for g in LATE_GROUPS[1]],
            lambda got, which: _assemble(got, LATE_GROUPS[which]))
    loss_part, grad_x, gw, gp, early_got, last_got = _local_step(
        x, mem, loss_target, full, small_in, late=late, early=lambda g: _grad_blocks(g, EARLY_GRAD_GROUPS),
        last=lambda g: _grad_blocks(g, LAST_GRAD_GROUPS))
    small_got, loss_got = _exchange("exchange_small", [_small_pack(gp).astype(bf16), jnp.broadcast_to(loss_part, (8, LOSS_LANES))],
                                    per_peer=False)
    received = dict(zip(EARLY_GRAD_GROUPS + LAST_GRAD_GROUPS, list(early_got) + list(last_got)))

    outs = [{}, {}, {}, {}]
    for g, members, _ in GROUPS:
        off = 0
        for n in members:
            for o, arr in zip(outs, _adamw("adamw_" + n, received[g], off, wts[n], ms[n], vs[n])):
                o[n] = arr
            off += wts[n].shape[1]
    res = _adamw("adamw_small", small_got, 0, *[_small_pack(d)[None] for d in (wts, ms, vs)])
    for o, arr in zip(outs, res):
        o.update(_small_unpack(arr))
    loss = jnp.sum(loss_got[:, 0, 0])
    return (loss, grad_x, *[turn(n, o[n].reshape(wts[n].shape)) for o in outs for n in WEIGHT_ORDER])
```

```python
import functools

import jax
import jax.numpy as jnp
from jax import lax
from jax.experimental import pallas as pl
from jax.experimental.pallas import tpu as pltpu

f32 = jnp.float32
bf16 = jnp.bfloat16
_HI = lax.Precision.HIGHEST

D = 1024
HEADS = 8
HD = 64
HW = HEADS * HD
MEM_HEADS = 4
MEM_HD = 128
MEM_W = 512
MEM_LEN = 256
D_FF = 2816
LORA_PAD = 128
NORM_EPS = 1e-6
GN_EPS = 64e-5
SCAN_CHUNK = 64
N_DEV = 8
LANES = 1024
VMEM_LIMIT = 56 * 1024 * 1024

ADAM_LR = 0.001
ADAM_B1 = 0.9
ADAM_B2 = 0.999
ADAM_EPS = 1e-08
ADAM_WD = 0.01
ADAM_STEP = 10

TRANSPOSED = ("w_in", "w_ffn_gate", "w_ffn_up")
SHARDED = (
    ("w_in", (6920, 1024), 0),
    ("w_ffn_gate", (2816, 1024), 0),
    ("w_ffn_up", (2816, 1024), 0),
    ("w_ffn_down", (2816, 1024), 0),
    ("w_mem_kv", (1024, 1024), 0),
    ("w_o", (1024, 1024), 0),
    ("w_fox_out", (512, 1024), 1),
    ("w_rwkv_out", (512, 1024), 1),
    ("w_mem_out", (512, 1024), 1),
    ("rwkv_w_up", (64, 512), 1),
    ("rwkv_a_up", (64, 512), 1),
    ("rwkv_g_up", (128, 512), 1),
)
REPLICATED = (
    ("pre1_g", (1, 1024)), ("post1_g", (1, 1024)), ("pre2_g", (1, 1024)), ("post2_g", (1, 1024)),
    ("mem_norm_g", (1, 1024)), ("fox_f_bias", (1, 8)), ("rwkv_mu", (1, 1792)), ("rwkv_w0", (1, 512)),
    ("rwkv_a0", (1, 512)), ("rwkv_k_k", (1, 512)), ("rwkv_k_a", (1, 512)), ("rwkv_r_k", (1, 8, 64)),
    ("rwkv_gn_g", (1, 512)), ("rwkv_gn_b", (1, 512)),
)
WEIGHT_ORDER = ('pre1_g', 'post1_g', 'pre2_g', 'post2_g', 'mem_norm_g', 'w_in', 'fox_f_bias', 'rwkv_mu',
                'rwkv_w0', 'rwkv_w_up', 'rwkv_a0', 'rwkv_a_up', 'rwkv_g_up', 'rwkv_k_k', 'rwkv_k_a',
                'rwkv_r_k', 'rwkv_gn_g', 'rwkv_gn_b', 'w_mem_kv', 'w_fox_out', 'w_rwkv_out', 'w_mem_out',
                'w_o', 'w_ffn_gate', 'w_ffn_up', 'w_ffn_down')


def _block_shape(shape, axis):
    return tuple(s // N_DEV if i == axis else s for i, s in enumerate(shape))


def _rows_of(shape):
    n = 1
    for s in shape:
        n *= s
    return n // LANES


REPL_ELEMS = sum(_rows_of((LANES,) + s) for _, s in REPLICATED)


def _cp(sem=None):
    return pltpu.CompilerParams(dimension_semantics=sem, vmem_limit_bytes=VMEM_LIMIT)


def _tile(dim, cap):
    best = None
    for t in range(128, min(dim, cap) + 1, 128):
        if dim % t == 0:
            best = t
    return best if best is not None else dim


def _two_terms(x):
    hi = x.astype(bf16)
    return hi, (x - hi.astype(f32)).astype(bf16)


def _dg(a, b, dims, exact):
    if exact == "split":
        (a_hi, a_lo), (b_hi, b_lo) = _two_terms(a), _two_terms(b)
        dot = functools.partial(lax.dot_general, dimension_numbers=dims, preferred_element_type=f32)
        return dot(a_hi, b_hi) + (dot(a_hi, b_lo) + dot(a_lo, b_hi))
    if exact:
        return lax.dot_general(a, b, dims, precision=_HI, preferred_element_type=f32)
    return lax.dot_general(a.astype(bf16), b.astype(bf16), dims, preferred_element_type=f32)


def _make_mm(batched, exact):
    o = 1 if batched else 0
    bd = ((0,), (0,)) if batched else ((), ())
    d_nn = (((1 + o,), (o,)), bd)
    d_nt = (((1 + o,), (1 + o,)), bd)
    d_tn = (((o,), (o,)), bd)

    @jax.custom_vjp
    def nn(a, b):
        return _dg(a, b, d_nn, exact)

    @jax.custom_vjp
    def nt(a, b):
        return _dg(a, b, d_nt, exact)

    @jax.custom_vjp
    def tn(a, b):
        return _dg(a, b, d_tn, exact)

    nn.defvjp(lambda a, b: (_dg(a, b, d_nn, exact), (a, b)),
              lambda res, g: (_dg(g, res[1], d_nt, exact), _dg(res[0], g, d_tn, exact)))
    nt.defvjp(lambda a, b: (_dg(a, b, d_nt, exact), (a, b)),
              lambda res, g: (_dg(g, res[1], d_nn, exact), _dg(g, res[0], d_tn, exact)))
    tn.defvjp(lambda a, b: (_dg(a, b, d_tn, exact), (a, b)),
              lambda res, g: (_dg(res[1], g, d_nt, exact), _dg(res[0], g, d_nn, exact)))
    return nn, nt, tn


def _sigmoid(x):
    return 1.0 / (1.0 + jnp.exp(-x))


def _head_sum_raw(x):
    width = 2 * HD
    i = lax.broadcasted_iota(jnp.int32, (width, width), 0) // HD
    j = lax.broadcasted_iota(jnp.int32, (width, width), 1) // HD
    m = (i == j).astype(bf16)
    dims = (((1,), (0,)), ((), ()))
    out = []
    for p in range(x.shape[1] // width):
        xp = x[:, p * width:(p + 1) * width]
        hi = xp.astype(bf16)
        lo = (xp - hi.astype(f32)).astype(bf16)
        out.append(lax.dot_general(hi, m, dims, preferred_element_type=f32)
                   + lax.dot_general(lo, m, dims, preferred_element_type=f32))
    return jnp.concatenate(out, axis=1)


@jax.custom_vjp
def _head_sum(x):
    return _head_sum_raw(x)


_head_sum.defvjp(lambda x: (_head_sum_raw(x), None), lambda _, g: (_head_sum_raw(g),))


WEIGHT_TILE_BYTES = 13 * 512 * 1024
ACC_TILE_BYTES = 8 * 1024 * 1024


def _lazy(fn, rows, width, params=()):
    return (fn, rows, width, list(params))


def _matmul(name, a, b, mode, add=None, out_dtype=f32):
    has_add = add is not None
    if isinstance(a, tuple):
        a_fn, a_rows, a_width, a_params = a
        a_arrays = [r for r, _ in a_rows]
        a_shape = (a_arrays[0].shape[0], a_width)
    else:
        a_fn, a_rows, a_params, a_arrays, a_shape = None, None, [], [a], a.shape
    n_r = len(a_arrays)
    n_a = n_r + len(a_params)

    def load_a(refs):
        if a_fn is None:
            return refs[0][...].astype(bf16)
        pieces = []
        for r, (_, widths) in zip(refs[:n_r], a_rows):
            pieces += _pieces(r, widths)
        return a_fn(*pieces, *[p[...] for p in refs[n_r:]])[0].astype(bf16)

    if mode == "tn":
        assert a_fn is None
        (k, m), (_, n) = a_shape, b.shape
        tn = _tile(n, max(128, ACC_TILE_BYTES // (4 * m)))
        tk = _tile(k, 2048)
        nk = k // tk

        def body(*refs):
            b_ref, o_ref, acc = refs[n_a:]

            @pl.when(pl.program_id(1) == 0)
            def _():
                acc[...] = jnp.zeros_like(acc)

            acc[...] += lax.dot_general(load_a(refs[:n_a]), b_ref[...].astype(bf16),
                                        (((0,), (0,)), ((), ())), preferred_element_type=f32)

            @pl.when(pl.program_id(1) == nk - 1)
            def _():
                o_ref[...] = acc[...].astype(o_ref.dtype)

        return pl.pallas_call(
            body, name=name, grid=(n // tn, nk),
            in_specs=[pl.BlockSpec((tk, r.shape[1]), lambda j, kk: (kk, 0)) for r in a_arrays]
            + [pl.BlockSpec((tk, tn), lambda j, kk: (kk, j))],
            out_specs=pl.BlockSpec((m, tn), lambda j, kk: (0, j)), out_shape=jax.ShapeDtypeStruct((m, n), out_dtype),
            scratch_shapes=[pltpu.VMEM((m, tn), f32)],
            compiler_params=_cp(("parallel", "arbitrary")),
        )(*a_arrays, b)

    (m, k) = a_shape
    n = b.shape[1] if mode == "nn" else b.shape[0]
    tm = _tile(m, 1024 if a_fn is None else 512)
    tn = _tile(n, max(128, WEIGHT_TILE_BYTES // (2 * k)))
    dims = (((1,), (0,)), ((), ())) if mode == "nn" else (((1,), (1,)), ((), ()))
    b_spec = pl.BlockSpec((k, tn), lambda j, i: (0, j)) if mode == "nn" else pl.BlockSpec((tn, k), lambda j, i: (j, 0))
    o_spec = pl.BlockSpec((tm, tn), lambda j, i: (i, j))

    keep = a_fn is not None
    assert not keep or tn == n

    def body(*refs):
        b_ref = refs[n_a]
        a_val = load_a(refs[:n_a])
        r = lax.dot_general(a_val, b_ref[...].astype(bf16), dims, preferred_element_type=f32)
        if has_add:
            r = r + refs[n_a + 1][...]
        if keep:
            refs[-2][...] = r.astype(refs[-2].dtype)
            refs[-1][...] = a_val
        else:
            refs[-1][...] = r.astype(refs[-1].dtype)

    res = pl.pallas_call(
        body, name=name, grid=(n // tn, m // tm),
        in_specs=[pl.BlockSpec((tm, r.shape[1]), lambda j, i: (i, 0)) for r in a_arrays]
        + [pl.BlockSpec(p.shape, lambda j, i: (0, 0)) for p in a_params] + [b_spec] + ([o_spec] if has_add else []),
        out_specs=[o_spec] + ([pl.BlockSpec((tm, k), lambda j, i: (i, 0))] if keep else []),
        out_shape=[jax.ShapeDtypeStruct((m, n), out_dtype)] + ([jax.ShapeDtypeStruct((m, k), bf16)] if keep else []),
        compiler_params=_cp(("parallel", "arbitrary")),
    )(*a_arrays, *a_params, b, *([add] if has_add else []))
    return tuple(res) if keep else res[0]


def _input_cotangent(name, a_list, b_list, x, gain, add, side=None):
    m = a_list[0].shape[0]
    tm = _tile(m, 256)
    n_g = len(a_list)
    srcs, per_peer = side if side is not None else ([], False)
    n_s = len(srcs)

    def body(*refs):
        x_ref, g_ref, add_ref = refs[2 * n_g:2 * n_g + 3]
        src_refs = refs[2 * n_g + 3:2 * n_g + 3 + n_s]
        dx_ref, dg_ref = refs[2 * n_g + 3 + n_s:2 * n_g + 5 + n_s]
        _side_exchange(src_refs, refs[2 * n_g + 5 + n_s:2 * n_g + 5 + 2 * n_s], per_peer, refs[2 * n_g + 5 + 2 * n_s:], m // tm)
        d_u = None
        for g in range(n_g):
            r = lax.dot_general(refs[g][...].astype(bf16), refs[n_g + g][...].astype(bf16), (((1,), (0,)), ((), ())),
                                preferred_element_type=f32)
            d_u = r if d_u is None else d_u + r
        _, vjp = jax.vjp(_rms, x_ref[...], g_ref[...])
        d_x, d_gain = vjp(d_u)
        dx_ref[...] = d_x + add_ref[...]

        @pl.when(pl.program_id(0) == 0)
        def _():
            dg_ref[...] = jnp.zeros_like(dg_ref)

        dg_ref[...] += d_gain

    rows = pl.BlockSpec((tm, x.shape[1]), lambda i: (i, 0))
    whole = lambda b: pl.BlockSpec(b.shape, lambda i: (0, 0))
    res = pl.pallas_call(
        body, name=name, grid=(m // tm,),
        in_specs=[pl.BlockSpec((tm, a.shape[1]), lambda i: (i, 0)) for a in a_list] + [whole(b) for b in b_list]
        + [rows, whole(gain), rows] + [_HBM_SPEC] * n_s,
        out_specs=[rows, whole(gain)] + [_HBM_SPEC] * n_s,
        out_shape=[jax.ShapeDtypeStruct(x.shape, f32), jax.ShapeDtypeStruct(gain.shape, f32)] + _side_out_shapes(srcs, per_peer),
        scratch_shapes=_side_sems(n_s),
        compiler_params=_cp(("arbitrary",)),
    )(*a_list, *b_list, x, gain, add, *srcs)
    return res[0], res[1], list(res[2:])


def _pieces(ref, widths):
    out, off = [], 0
    for w in widths:
        out.append(ref[:, off:off + w].astype(f32))
        off += w
    return out


def _store_pieces(ref, widths, vals, add_ref=None):
    off = 0
    for w, v in zip(widths, vals):
        ref[:, off:off + w] = (v if add_ref is None else v + add_ref[:, off:off + w]).astype(ref.dtype)
        off += w


def _rows_fwd(name, fn, consts, rows, params, outs, n_sums=0, tm=512, dtypes=None):
    t = (consts + rows)[0][0].shape[0]
    tm = min(tm, t)
    ins = consts + rows
    n_in, n_p, n_o = len(ins), len(params), len(outs)
    dtypes = dtypes or [f32] * n_o

    def body(*refs):
        in_refs, p_refs = refs[:n_in], refs[n_in:n_in + n_p]
        o_refs, s_refs = refs[n_in + n_p:n_in + n_p + n_o], refs[n_in + n_p + n_o:]
        vals = []
        for r, (_, widths) in zip(in_refs, ins):
            vals += _pieces(r, widths)
        res = fn(*vals, *[p[...] for p in p_refs])
        pos = 0
        for r, widths in zip(o_refs, outs):
            _store_pieces(r, widths, res[pos:pos + len(widths)])
            pos += len(widths)

        @pl.when(pl.program_id(0) == 0)
        def _():
            for s in s_refs:
                s[...] = jnp.zeros_like(s)

        for s, v in zip(s_refs, res[pos:]):
            s[...] += v

    row_spec = lambda w: pl.BlockSpec((tm, w), lambda i: (i, 0))
    full = lambda p: pl.BlockSpec(p.shape, lambda i: (0,) * p.ndim)
    return pl.pallas_call(
        body, name=name, grid=(t // tm,),
        in_specs=[row_spec(sum(w)) for _, w in ins] + [full(p) for p in params],
        out_specs=[row_spec(sum(w)) for w in outs] + [pl.BlockSpec((1, 1), lambda i: (0, 0))] * n_sums,
        out_shape=[jax.ShapeDtypeStruct((t, sum(w)), dt) for w, dt in zip(outs, dtypes)] + [jax.ShapeDtypeStruct((1, 1), f32)] * n_sums,
        compiler_params=_cp(("arbitrary",)),
    )(*[a for a, _ in ins], *params)


def _rows_bwd(name, fn, consts, rows, params, outs, cts, n_sums=0, add=None, tm=512, dtypes=None):
    t = (consts + rows)[0][0].shape[0]
    tm = min(tm, t)
    n_c, n_r, n_p, n_o = len(consts), len(rows), len(params), len(outs)
    has_add = add is not None
    dtypes = dtypes or [f32] * n_r

    def body(*refs):
        pos = 0
        c_refs = refs[pos:pos + n_c]; pos += n_c
        r_refs = refs[pos:pos + n_r]; pos += n_r
        p_refs = refs[pos:pos + n_p]; pos += n_p
        ct_refs = refs[pos:pos + n_o]; pos += n_o
        add_ref = refs[pos] if has_add else None
        pos += 1 if has_add else 0
        dr_refs = refs[pos:pos + n_r]; pos += n_r
        dp_refs = refs[pos:pos + n_p]; pos += n_p
        s_refs = refs[pos:pos + n_sums]
        cvals, rvals = [], []
        for r, (_, widths) in zip(c_refs, consts):
            cvals += _pieces(r, widths)
        for r, (_, widths) in zip(r_refs, rows):
            rvals += _pieces(r, widths)
        pvals = [p[...] for p in p_refs]
        ctv = []
        for r, widths in zip(ct_refs, outs):
            ctv += _pieces(r, widths)
        ctv += [jnp.ones((1, 1), f32)] * n_sums
        primal, vjp = jax.vjp(lambda *rp: tuple(fn(*cvals, *rp)), *rvals, *pvals)
        g = vjp(tuple(ctv))
        pos = 0
        for idx, (r, (_, widths)) in enumerate(zip(dr_refs, rows)):
            _store_pieces(r, widths, g[pos:pos + len(widths)], add_ref if idx == 0 else None)
            pos += len(widths)

        @pl.when(pl.program_id(0) == 0)
        def _():
            for acc in list(dp_refs) + list(s_refs):
                acc[...] = jnp.zeros_like(acc)

        for dp, v in zip(dp_refs, g[pos:]):
            dp[...] += v
        for s, v in zip(s_refs, primal[len(primal) - n_sums:]):
            s[...] += v

    row_spec = lambda w: pl.BlockSpec((tm, w), lambda i: (i, 0))
    full = lambda p: pl.BlockSpec(p.shape, lambda i: (0,) * p.ndim)
    args = [a for a, _ in consts + rows] + list(params) + list(cts) + ([add] if has_add else [])
    res = pl.pallas_call(
        body, name=name, grid=(t // tm,),
        in_specs=[row_spec(sum(w)) for _, w in consts + rows] + [full(p) for p in params]
        + [row_spec(sum(w)) for w in outs] + ([row_spec(add.shape[1])] if has_add else []),
        out_specs=[row_spec(sum(w)) for _, w in rows] + [full(p) for p in params]
        + [pl.BlockSpec((1, 1), lambda i: (0, 0))] * n_sums,
        out_shape=[jax.ShapeDtypeStruct((t, sum(w)), dt) for (_, w), dt in zip(rows, dtypes)]
        + [jax.ShapeDtypeStruct(p.shape, f32) for p in params] + [jax.ShapeDtypeStruct((1, 1), f32)] * n_sums,
        compiler_params=_cp(("arbitrary",)),
    )(*args)
    return res[:n_r], res[n_r:n_r + n_p] + res[n_r + n_p:]


def _matmul_then_vjp(name, a, b, mode, fn, rows, dtypes, params=(), first_cts=(), add=None, tm=256):
    m, k = a.shape
    tm = min(tm, m)
    dims = (((1,), (0,)), ((), ())) if mode == "nn" else (((1,), (1,)), ((), ()))
    n_r, n_p, n_c = len(rows), len(params), len(first_cts)
    has_add = add is not None

    def body(*refs):
        a_ref, b_ref = refs[:2]
        pos = 2
        r_refs = refs[pos:pos + n_r]; pos += n_r
        p_refs = refs[pos:pos + n_p]; pos += n_p
        c_refs = refs[pos:pos + n_c]; pos += n_c
        add_ref = refs[pos] if has_add else None
        pos += 1 if has_add else 0
        dr_refs = refs[pos:pos + n_r]; pos += n_r
        dp_refs = refs[pos:pos + n_p]
        ct = lax.dot_general(a_ref[...].astype(bf16), b_ref[...].astype(bf16), dims, preferred_element_type=f32)
        if has_add:
            ct = ct + add_ref[...]
        rvals = []
        for r, (_, widths) in zip(r_refs, rows):
            rvals += _pieces(r, widths)
        _, vjp = jax.vjp(lambda *rp: tuple(fn(*rp)), *rvals, *[p[...] for p in p_refs])
        g = vjp(tuple(c[...].astype(f32) for c in c_refs) + (ct,))
        pos = 0
        for r, (_, widths) in zip(dr_refs, rows):
            _store_pieces(r, widths, g[pos:pos + len(widths)])
            pos += len(widths)

        @pl.when(pl.program_id(0) == 0)
        def _():
            for dp in dp_refs:
                dp[...] = jnp.zeros_like(dp)

        for dp, v in zip(dp_refs, g[pos:]):
            dp[...] += v

    row_spec = lambda w: pl.BlockSpec((tm, w), lambda i: (i, 0))
    whole = lambda p: pl.BlockSpec(p.shape, lambda i: (0, 0))
    res = pl.pallas_call(
        body, name=name, grid=(m // tm,),
        in_specs=[row_spec(k), whole(b)] + [row_spec(sum(w)) for _, w in rows] + [whole(p) for p in params]
        + [row_spec(c.shape[1]) for c in first_cts] + ([row_spec(add.shape[1])] if has_add else []),
        out_specs=[row_spec(sum(w)) for _, w in rows] + [whole(p) for p in params],
        out_shape=[jax.ShapeDtypeStruct((m, sum(w)), dt) for (_, w), dt in zip(rows, dtypes)]
        + [jax.ShapeDtypeStruct(p.shape, f32) for p in params],
        compiler_params=_cp(("arbitrary",)),
    )(a, b, *[r for r, _ in rows], *params, *first_cts, *([add] if has_add else []))
    return res[:n_r], res[n_r:]


def _rms(x, g):
    return x * lax.rsqrt(jnp.mean(x * x, axis=-1, keepdims=True) + NORM_EPS) * g


def _fn_rms(x, g):
    return (_rms(x, g),)


def _fn_rwkv_pre(r, k, v, wd, ad, gd, w0, w_up, a0, a_up, g_up, k_k, k_a):
    nn, _, _ = _make_mm(False, False)
    w_log = -_sigmoid(w0 + nn(jnp.tanh(wd), w_up)) * 0.6065306597126334
    a = _sigmoid(a0 + nn(ad, a_up))
    g = nn(_sigmoid(gd), g_up)
    kk = k * k_k
    kk = kk * lax.rsqrt(jnp.maximum(_head_sum(kk * kk), 1e-24))
    k2 = k * (1.0 + (a - 1.0) * k_a)
    return r, k2, v, w_log, -kk, kk * a, g


def _fn_rwkv_post(y, r, k2, v, g, gn_g, gn_b, r_k):
    mean = _head_sum(y) * (1.0 / HD)
    yc = y - mean
    var = _head_sum(yc * yc) * (1.0 / HD)
    yn = yc * lax.rsqrt(var + GN_EPS) * gn_g + gn_b
    bonus = _head_sum(r * k2 * r_k) * v
    return ((yn + bonus) * g,)


def _fn_merge(a_fox, a_rwkv, a_mem, g_fox, g_rwkv, g_mem):
    return (_sigmoid(g_fox) * a_fox + _sigmoid(g_rwkv) * a_rwkv + _sigmoid(g_mem) * a_mem,)


def _fn_post1(y, x, post1_g, pre2_g):
    h1 = x + _rms(y, post1_g)
    return h1, _rms(h1, pre2_g)


def _fn_swiglu(gp, up):
    return (gp * _sigmoid(gp) * up,)


def _fn_final(target, ffn, h1, post2_g):
    err = h1 + _rms(ffn, post2_g) - target
    per_row = jnp.mean(err * err, axis=-1, keepdims=True)
    return (0.5 * jnp.sum(per_row, axis=0, keepdims=True),)


def _shift_down(x):
    row = lax.broadcasted_iota(jnp.int32, x.shape, 0)
    return jnp.where(row == 0, 0.0, pltpu.roll(x, 1, 0))


def _shift_up(x):
    s = x.shape[0]
    row = lax.broadcasted_iota(jnp.int32, x.shape, 0)
    return jnp.where(row == s - 1, 0.0, pltpu.roll(x, s - 1, 0))


def _tokshift_fwd(p, mu, batch, seq):
    w = p.shape[1]
    tc = _tile(w, 384)

    def body(p_ref, mu_ref, o_ref):
        x = p_ref[...].astype(f32)
        o_ref[...] = (x + (_shift_down(x) - x) * mu_ref[...]).astype(o_ref.dtype)

    return pl.pallas_call(
        body, name="tokshift_fwd", grid=(w // tc, batch),
        in_specs=[pl.BlockSpec((seq, tc), lambda j, b: (b, j)), pl.BlockSpec((1, tc), lambda j, b: (0, j))],
        out_specs=pl.BlockSpec((seq, tc), lambda j, b: (b, j)),
        out_shape=jax.ShapeDtypeStruct(p.shape, bf16),
        compiler_params=_cp(("parallel", "arbitrary")),
    )(p, mu)


def _tokshift_bwd(p, mu, dps, batch, seq):
    w = p.shape[1]
    tc = _tile(w, 384)

    def body(p_ref, mu_ref, d_ref, dp_ref, dmu_ref):
        x, mu_v, d = p_ref[...].astype(f32), mu_ref[...], d_ref[...].astype(f32)
        dp_ref[...] = (d * (1.0 - mu_v) + _shift_up(d * mu_v)).astype(dp_ref.dtype)

        @pl.when(pl.program_id(1) == 0)
        def _():
            dmu_ref[...] = jnp.zeros_like(dmu_ref)

        dmu_ref[...] += jnp.sum(d * (_shift_down(x) - x), axis=0, keepdims=True)

    return pl.pallas_call(
        body, name="tokshift_bwd", grid=(w // tc, batch),
        in_specs=[pl.BlockSpec((seq, tc), lambda j, b: (b, j)), pl.BlockSpec((1, tc), lambda j, b: (0, j)),
                  pl.BlockSpec((seq, tc), lambda j, b: (b, j))],
        out_specs=[pl.BlockSpec((seq, tc), lambda j, b: (b, j)), pl.BlockSpec((1, tc), lambda j, b: (0, j))],
        out_shape=[jax.ShapeDtypeStruct(p.shape, bf16), jax.ShapeDtypeStruct(mu.shape, f32)],
        compiler_params=_cp(("parallel", "arbitrary")),
    )(p, mu, dps)


def _cum_block(seq):
    return _tile(seq, 256)


def _fox_gate_fwd(f, bias, batch, seq):
    cb = _cum_block(seq)

    def body(f_ref, b_ref, c_ref):
        row = lax.broadcasted_iota(jnp.int32, (cb, cb), 0)
        col = lax.broadcasted_iota(jnp.int32, (cb, cb), 1)
        tri = (col <= row).astype(f32)
        carry = jnp.zeros((1, 128), f32)
        for i in range(seq // cb):
            z = f_ref[i * cb:(i + 1) * cb, :] + b_ref[...]
            ls = jnp.minimum(z, 0.0) - jnp.log(1.0 + jnp.exp(-jnp.abs(z)))
            c = _dg(tri, ls, (((1,), (0,)), ((), ())), True) + carry
            c_ref[i * cb:(i + 1) * cb, :] = c
            carry = c[cb - 1:cb, :]

    return pl.pallas_call(
        body, name="fox_gate_fwd", grid=(batch,),
        in_specs=[pl.BlockSpec((seq, 128), lambda b: (b, 0)), pl.BlockSpec((1, 128), lambda b: (0, 0))],
        out_specs=pl.BlockSpec((seq, 128), lambda b: (b, 0)),
        out_shape=jax.ShapeDtypeStruct(f.shape, f32),
        compiler_params=_cp(("arbitrary",)),
    )(f, bias)


def _fox_gate_bwd(f, bias, dc_a, dc_b, batch, seq):
    cb = _cum_block(seq)

    def body(f_ref, b_ref, da_ref, db_ref, df_ref, dbias_ref):
        row = lax.broadcasted_iota(jnp.int32, (cb, cb), 0)
        col = lax.broadcasted_iota(jnp.int32, (cb, cb), 1)
        triu = (col >= row).astype(f32)

        @pl.when(pl.program_id(0) == 0)
        def _():
            dbias_ref[...] = jnp.zeros_like(dbias_ref)

        lane = lax.broadcasted_iota(jnp.int32, (1, 128), 1)

        def by_head(blk):
            out = jnp.zeros((cb, 128), f32)
            for p in range(HEADS // 2):
                for e in range(2):
                    out = jnp.where(lane == 2 * p + e, _pick_lane(blk[:, p * 128:(p + 1) * 128], e), out)
            return out

        carry = jnp.zeros((1, 128), f32)
        tot = jnp.zeros((1, 128), f32)
        for i in reversed(range(seq // cb)):
            sl = slice(i * cb, (i + 1) * cb)
            dc = by_head(da_ref[sl, :] + db_ref[sl, :])
            dls = _dg(triu, dc, (((1,), (0,)), ((), ())), True) + carry
            carry = dls[0:1, :]
            df = dls * _sigmoid(-(f_ref[sl, :] + b_ref[...]))
            df_ref[sl, :] = df.astype(df_ref.dtype)
            tot = tot + jnp.sum(df, axis=0, keepdims=True)
        dbias_ref[...] += tot

    return pl.pallas_call(
        body, name="fox_gate_bwd", grid=(batch,),
        in_specs=[pl.BlockSpec((seq, 128), lambda b: (b, 0)), pl.BlockSpec((1, 128), lambda b: (0, 0)),
                  pl.BlockSpec((seq, HW), lambda b: (b, 0)), pl.BlockSpec((seq, HW), lambda b: (b, 0))],
        out_specs=[pl.BlockSpec((seq, 128), lambda b: (b, 0)), pl.BlockSpec((1, 128), lambda b: (0, 0))],
        out_shape=[jax.ShapeDtypeStruct(f.shape, bf16), jax.ShapeDtypeStruct((1, 128), f32)],
        compiler_params=_cp(("arbitrary",)),
    )(f, bias, dc_a, dc_b)


_HBM_SPEC = pl.BlockSpec(memory_space=pltpu.HBM)


def _side_out_shapes(srcs, per_peer):
    return [jax.ShapeDtypeStruct(((N_DEV,) + tuple(s.shape[1:] if per_peer else s.shape)), s.dtype) for s in srcs]


def _side_sems(n):
    if n == 0:
        return []
    return [pltpu.SemaphoreType.DMA((n, N_DEV - 1)), pltpu.SemaphoreType.DMA((n, N_DEV - 1)), pltpu.SemaphoreType.DMA((n,))]


def _peer_copies(src_refs, dst_refs, per_peer, sems):
    send_sems, recv_sems, local_sems = sems
    x, y, c = lax.axis_index("x"), lax.axis_index("y"), lax.axis_index("c")
    me = 4 * x + 2 * y + c

    def remote(src, dst, t, k, to):
        return pltpu.make_async_remote_copy(src_ref=src, dst_ref=dst, send_sem=send_sems.at[t, k - 1],
                                            recv_sem=recv_sems.at[t, k - 1], device_id=to,
                                            device_id_type=pl.DeviceIdType.MESH)

    direct, relays = [], []
    for t, (s, d) in enumerate(zip(src_refs, dst_refs)):
        direct.append((t, 0, pltpu.make_async_copy(s.at[me] if per_peer else s, d.at[me], local_sems.at[t])))
        for k in range(1, N_DEV):
            px = 1 - x if k & 4 else x
            py = 1 - y if k & 2 else y
            pc = 1 - c if k & 1 else c
            if per_peer:
                direct.append((t, k, remote(s.at[4 * px + 2 * py + pc], d.at[me], t, k, (px, py, pc))))
            elif k == 1 or not k & 1:
                direct.append((t, k, remote(s, d.at[me], t, k, (px, py, pc))))
            else:
                origin = d.at[4 * px + 2 * py + c]
                relays.append((t, k - 1, remote(origin, origin, t, k, (x, y, 1 - c))))
    return direct, relays


def _exchange_start(direct):
    for _, _, cp in direct:
        cp.start()


def _exchange_relay(direct, relays):
    landed = {(t, k): cp for t, k, cp in direct}
    for t, j, cp in relays:
        landed[(t, j)].wait_recv()
        cp.start()


def _exchange_finish(direct, relays):
    relayed = {(t, j) for t, j, _ in relays}
    for t, k, cp in direct:
        if k == 0:
            cp.wait()
        else:
            cp.wait_send()
            if (t, k) not in relayed:
                cp.wait_recv()
    for _, _, cp in relays:
        cp.wait()


def _side_exchange(src_refs, dst_refs, per_peer, sems, *grid):
    if not src_refs:
        return
    step, total = 0, 1
    for a, n in enumerate(grid):
        step, total = step * n + pl.program_id(a), total * n

    @pl.when(step == 0)
    def _():
        _exchange_start(_peer_copies(src_refs, dst_refs, per_peer, sems)[0])

    @pl.when(step == (3 * total) // 4)
    def _():
        _exchange_relay(*_peer_copies(src_refs, dst_refs, per_peer, sems))

    @pl.when(step == total - 1)
    def _():
        _exchange_finish(*_peer_copies(src_refs, dst_refs, per_peer, sems))


def _exchange(name, srcs, per_peer):
    n = len(srcs)

    def body(*refs):
        direct, relays = _peer_copies(refs[:n], refs[n:2 * n], per_peer, refs[2 * n:])
        _exchange_start(direct)
        _exchange_relay(direct, relays)
        _exchange_finish(direct, relays)

    return pl.pallas_call(
        body, name=name, in_specs=[_HBM_SPEC] * n, out_specs=[_HBM_SPEC] * n,
        out_shape=_side_out_shapes(srcs, per_peer), scratch_shapes=_side_sems(n),
    )(*srcs)


FOX_T = 512
_NEG = -1e30
_D2 = (((1,), (1,)), ((), ()))
_D1 = (((1,), (0,)), ((), ()))
_D0 = (((0,), (0,)), ((), ()))


def _bdot(a, b, dims):
    return lax.dot_general(a.astype(bf16), b.astype(bf16), dims, preferred_element_type=f32)


def _pick_lane(x, lane):
    idx = lax.broadcasted_iota(jnp.int32, x.shape, 1)
    return jnp.sum(jnp.where(idx == lane, x, 0.0), axis=1, keepdims=True)


def _pick_row(x, row):
    idx = lax.broadcasted_iota(jnp.int32, x.shape, 0)
    return jnp.sum(jnp.where(idx == row, x, 0.0), axis=0, keepdims=True)


def _fox_fwd(qkv, c, c_rows, batch, seq, side=None):
    t = min(FOX_T, seq)
    nq = seq // t
    scale = HD ** -0.5
    srcs, per_peer = side if side is not None else ([], False)
    n_s = len(srcs)

    def body(*refs):
        q_ref, k_ref, v_ref, cq_ref, ck_ref = refs[:5]
        o_ref, lse_ref = refs[5 + n_s:7 + n_s]
        _side_exchange(refs[5:5 + n_s], refs[7 + n_s:7 + 2 * n_s], per_peer, refs[7 + 2 * n_s:], batch, PAIRS, nq)
        pair, i = pl.program_id(1), pl.program_id(2)
        lane = lax.broadcasted_iota(jnp.int32, (1, PAIR_W), 1)
        first = (lane // HD) == 0
        mine = [first, jnp.logical_not(first)]
        q = q_ref[...] * scale
        qs = [jnp.where(mine[e], q, 0.0) for e in range(2)]
        cqs = [_pick_lane(cq_ref[...], 2 * pair + e) for e in range(2)]
        causal = lax.broadcasted_iota(jnp.int32, (t, t), 1) <= lax.broadcasted_iota(jnp.int32, (t, t), 0)

        def block(j, carry, diagonal):
            rows = pl.ds(pl.multiple_of(j * t, t), t)
            kj, vj = k_ref[rows, :], v_ref[rows, :]
            ck_blk = ck_ref[0, :, rows]
            out = []
            for e in range(2):
                m, acc = carry[2 * e:2 * e + 2]
                s = _bdot(qs[e], kj, _D2) + cqs[e] - _pick_row(ck_blk, 2 * pair + e)
                if diagonal:
                    s = jnp.where(causal, s, _NEG)
                m_new = jnp.maximum(m, jnp.max(s, axis=1, keepdims=True))
                p = jnp.exp(s - m_new)
                out += [m_new, jnp.exp(m - m_new) * acc + _bdot(p, jnp.where(mine[e], vj, 1.0), _D1)]
            return tuple(out)

        init = (jnp.full((t, 1), _NEG, f32), jnp.zeros((t, PAIR_W), f32)) * 2
        carry = lax.fori_loop(0, i, lambda j, cr: block(j, cr, False), init)
        m0, a0, m1, a1 = block(i, carry, True)
        l0, l1 = _pick_lane(a0, HD), _pick_lane(a1, 0)
        o_ref[...] = jnp.where(first, a0 / l0, a1 / l1)
        lse_ref[...] = jnp.where(lane == 0, m0 + jnp.log(l0), jnp.where(lane == 1, m1 + jnp.log(l1), 0.0))

    q_spec = pl.BlockSpec((t, PAIR_W), lambda b, p, i: (b * nq + i, p))
    res = pl.pallas_call(
        body, name="fox_attn_fwd", grid=(batch, PAIRS, nq),
        in_specs=[q_spec,
                  pl.BlockSpec((seq, PAIR_W), lambda b, p, i: (b, PAIRS + p)),
                  pl.BlockSpec((seq, PAIR_W), lambda b, p, i: (b, 2 * PAIRS + p)),
                  pl.BlockSpec((t, 128), lambda b, p, i: (b * nq + i, 0)),
                  pl.BlockSpec((1, 8, seq), lambda b, p, i: (b, 0, 0))] + [_HBM_SPEC] * n_s,
        out_specs=[q_spec, q_spec] + [_HBM_SPEC] * n_s,
        out_shape=[jax.ShapeDtypeStruct((batch * seq, HW), f32)] * 2 + _side_out_shapes(srcs, per_peer),
        scratch_shapes=_side_sems(n_s),
        compiler_params=_cp(("arbitrary", "arbitrary", "arbitrary")),
    )(qkv, qkv, qkv, c, c_rows, *srcs)
    return res[0], res[1], list(res[2:])


def _fox_bwd(qkv, c, c_rows, o, lse, do, batch, seq):
    t = min(FOX_T, seq)
    nq = seq // t
    scale = HD ** -0.5

    def body(q_ref, k_ref, v_ref, cq_ref, ck_ref, o_ref, lse_ref, do_ref,
             dq_ref, dk_ref, dv_ref, dcq_ref, dck_ref, acc0, acc1):
        pair, i = pl.program_id(1), pl.program_id(2)
        accs = [acc0, acc1]

        @pl.when(i == 0)
        def _():
            dv_ref[...] = jnp.zeros_like(dv_ref)
            acc0[...] = jnp.zeros_like(acc0)
            acc1[...] = jnp.zeros_like(acc1)

        lane = lax.broadcasted_iota(jnp.int32, (1, PAIR_W), 1)
        first = (lane // HD) == 0
        mine = [first, jnp.logical_not(first)]
        q, d_o, o_i = q_ref[...] * scale, do_ref[...], o_ref[...]
        q0s = [jnp.where(mine[e], q, 0.0) for e in range(2)]
        q1s = [jnp.where(mine[e], q, 1.0) for e in range(2)]
        dos = [jnp.where(mine[e], d_o, 0.0) for e in range(2)]
        deltas = [jnp.sum(dos[e] * o_i, axis=1, keepdims=True) for e in range(2)]
        lses = [_pick_lane(lse_ref[...], e) for e in range(2)]
        cqs = [_pick_lane(cq_ref[...], 2 * pair + e) for e in range(2)]
        causal = lax.broadcasted_iota(jnp.int32, (t, t), 1) <= lax.broadcasted_iota(jnp.int32, (t, t), 0)

        def block(j, dqs, diagonal):
            rows = pl.ds(pl.multiple_of(j * t, t), t)
            kj, vj = k_ref[rows, :], v_ref[rows, :]
            ck_blk = ck_ref[0, :, rows]
            out = []
            for e in range(2):
                s = _bdot(q0s[e], kj, _D2) + cqs[e] - _pick_row(ck_blk, 2 * pair + e)
                if diagonal:
                    s = jnp.where(causal, s, _NEG)
                p = jnp.exp(s - lses[e])
                ds = p * (_bdot(dos[e], vj, _D2) - deltas[e])
                dv_ref[rows, :] += _bdot(p, dos[e], _D0)
                accs[e][rows, :] += _bdot(ds, q1s[e], _D0)
                out.append(dqs[e] + _bdot(ds, jnp.where(mine[e], kj, 1.0), _D1))
            return tuple(out)

        zero = jnp.zeros((t, PAIR_W), f32)
        dqs = lax.fori_loop(0, i, lambda j, cr: block(j, cr, False), (zero, zero))
        dq0, dq1 = block(i, dqs, True)
        dq_ref[...] = jnp.where(first, dq0, dq1) * scale
        dcq_ref[...] = jnp.where(lane == 0, _pick_lane(dq0, HD), jnp.where(lane == 1, _pick_lane(dq1, 0), 0.0))

        @pl.when(i == nq - 1)
        def _():
            a0, a1 = acc0[...], acc1[...]
            dk_ref[...] = jnp.where(first, a0, a1)
            dck_ref[...] = jnp.where(lane == 0, -_pick_lane(a0, HD), jnp.where(lane == 1, -_pick_lane(a1, 0), 0.0))

    blk = lambda col: pl.BlockSpec((t, PAIR_W), lambda b, p, i: (b * nq + i, col * PAIRS + p))
    whole = lambda col: pl.BlockSpec((seq, PAIR_W), lambda b, p, i: (b, col * PAIRS + p))
    t_all = batch * seq
    return pl.pallas_call(
        body, name="fox_attn_bwd", grid=(batch, PAIRS, nq),
        in_specs=[blk(0), whole(1), whole(2),
                  pl.BlockSpec((t, 128), lambda b, p, i: (b * nq + i, 0)),
                  pl.BlockSpec((1, 8, seq), lambda b, p, i: (b, 0, 0)),
                  blk(0), blk(0), blk(0)],
        out_specs=[blk(0), whole(0), whole(0), blk(0), whole(0)],
        out_shape=[jax.ShapeDtypeStruct((t_all, HW), f32)] * 5,
        scratch_shapes=[pltpu.VMEM((seq, PAIR_W), f32), pltpu.VMEM((seq, PAIR_W), f32)],
        compiler_params=_cp(("parallel", "parallel", "arbitrary")),
    )(qkv, qkv, qkv, c, c_rows, o, lse, do)


MEM_TQ = 1024


def _mem_block(q, km, vm):
    nn, nt, _ = _make_mm(False, False)
    logits = nt(q, km) * (MEM_HD ** -0.5)
    m = lax.stop_gradient(jnp.max(logits, axis=-1, keepdims=True))
    e = jnp.exp(logits - m)
    return nn(e / jnp.sum(e, axis=-1, keepdims=True), vm)


def _mem_specs(seq, tq):
    nq = seq // tq
    qs = pl.BlockSpec((tq, MEM_HD), lambda b, h, i: (b * nq + i, h))
    ks = pl.BlockSpec((MEM_LEN, MEM_HD), lambda b, h, i: (b, h))
    vs = pl.BlockSpec((MEM_LEN, MEM_HD), lambda b, h, i: (b, MEM_HEADS + h))
    return nq, qs, ks, vs


def _mem_fwd(q, mem_kv, batch, seq):
    tq = min(MEM_TQ, seq)
    nq, qs, ks, vs = _mem_specs(seq, tq)

    def body(q_ref, k_ref, v_ref, o_ref):
        o_ref[...] = _mem_block(q_ref[...].astype(f32), k_ref[...], v_ref[...]).astype(o_ref.dtype)

    return pl.pallas_call(
        body, name="mem_attn_fwd", grid=(batch, MEM_HEADS, nq),
        in_specs=[qs, ks, vs], out_specs=qs, out_shape=jax.ShapeDtypeStruct(q.shape, bf16),
        compiler_params=_cp(("parallel", "parallel", "arbitrary")),
    )(q, mem_kv, mem_kv)


def _mem_bwd(q, mem_kv, do, batch, seq):
    tq = min(MEM_TQ, seq)
    nq, qs, ks, vs = _mem_specs(seq, tq)

    def body(q_ref, k_ref, v_ref, do_ref, dq_ref, dk_ref, dv_ref):
        _, vjp = jax.vjp(_mem_block, q_ref[...].astype(f32), k_ref[...], v_ref[...])
        dq, dk, dv = vjp(do_ref[...])
        dq_ref[...] = dq.astype(dq_ref.dtype)

        @pl.when(pl.program_id(2) == 0)
        def _():
            dk_ref[...] = jnp.zeros_like(dk_ref)
            dv_ref[...] = jnp.zeros_like(dv_ref)

        dk_ref[...] += dk
        dv_ref[...] += dv

    return pl.pallas_call(
        body, name="mem_attn_bwd", grid=(batch, MEM_HEADS, nq),
        in_specs=[qs, ks, vs, qs], out_specs=[qs, ks, ks],
        out_shape=[jax.ShapeDtypeStruct(q.shape, bf16), jax.ShapeDtypeStruct((batch * MEM_LEN, MEM_W), f32),
                   jax.ShapeDtypeStruct((batch * MEM_LEN, MEM_W), f32)],
        compiler_params=_cp(("parallel", "parallel", "arbitrary")),
    )(q, mem_kv, mem_kv, do)


@jax.custom_vjp
def _halves(x):
    c = x.shape[1] // 2
    return x[:, :c], x[:, c:]


_halves.defvjp(lambda x: ((x[:, :x.shape[1] // 2], x[:, x.shape[1] // 2:]), None),
               lambda _, g: (jnp.concatenate(g, axis=1),))


@jax.custom_vjp
def _lead_halves(x):
    n = x.shape[0] // 2
    return x[:n], x[n:]


_lead_halves.defvjp(lambda x: ((x[:x.shape[0] // 2], x[x.shape[0] // 2:]), None),
                    lambda _, g: (jnp.concatenate(g, axis=0),))


def _scan_chunk(s0, r, wl, k, v, a, b):
    nn, nt, tn = _make_mm(True, False)
    nn_exact, _, _ = _make_mm(True, True)
    _, nt_exact, _ = _make_mm(True, "split")
    hp, c, lanes = r.shape
    row = lax.broadcasted_iota(jnp.int32, (c, c), 0)
    col = lax.broadcasted_iota(jnp.int32, (c, c), 1)
    first = (lax.broadcasted_iota(jnp.int32, (1, 1, lanes), 2) // HD) == 0
    tri = jnp.broadcast_to((col <= row).astype(f32)[None], (hp, c, c))
    lg = nn_exact(tri, wl)
    lg_end = lg[:, c - 1:c, :]
    grow, shrink, to_end = jnp.exp(lg), jnp.exp(-lg), jnp.exp(lg_end - lg)
    rt, kt, bt, at = r * grow, k * shrink, b * shrink, a * jnp.exp(lg - wl)
    strict, incl = (col < row)[None], (col <= row)[None]
    twice = lambda t: jnp.concatenate([t, t], axis=0)
    queries = jnp.concatenate([at, rt], axis=1)
    per_head = jnp.concatenate([jnp.where(first, queries, 0.0), jnp.where(first, 0.0, queries)], axis=0)
    (ab, rb), (ak, rk) = _halves(nt_exact(per_head, twice(bt))), _halves(nt_exact(per_head, twice(kt)))
    l_ab = jnp.where(strict, ab, 0.0)
    a_ak = jnp.where(strict, ak, 0.0)
    a_rb = jnp.where(incl, rb, 0.0)
    a_rk = jnp.where(incl, rk, 0.0)
    inv = (col == row).astype(f32)[None] + l_ab
    power, n = l_ab, 1
    while 2 * n < c:
        power = nn(power, power)
        inv = inv + nn(inv, power)
        n *= 2

    def apply(m, t):
        lo, hi = _lead_halves(nn(m, twice(t)))
        return jnp.where(first, lo, hi)

    sa = apply(inv, nt(at, s0) + apply(a_ak, v))
    y = nt(rt, s0) + apply(a_rk, v) + apply(a_rb, sa)
    same_head = ((lax.broadcasted_iota(jnp.int32, (lanes, lanes), 0) // HD)
                 == (lax.broadcasted_iota(jnp.int32, (lanes, lanes), 1) // HD))[None]
    s1 = s0 * jnp.exp(lg_end) + jnp.where(same_head, tn(v, k * to_end) + tn(sa, b * to_end), 0.0)
    return y, s1


PAIRS = HEADS // 2
PAIR_W = 2 * HD
SCAN_ARGS = (0, 3, 1, 2, 4, 5)


def _pair_stack(ref, off):
    return jnp.stack([ref[b, :, off + p * PAIR_W:off + (p + 1) * PAIR_W]
                      for b in range(ref.shape[0]) for p in range(PAIRS)])


def _pair_store(ref, off, val, add_ref=None):
    for b in range(ref.shape[0]):
        for p in range(PAIRS):
            sl = slice(off + p * PAIR_W, off + (p + 1) * PAIR_W)
            v = val[b * PAIRS + p]
            ref[b, :, sl] = v if add_ref is None else v + add_ref[b, :, sl]


def _scan_fwd(main6, batch, seq, side=None):
    c = min(SCAN_CHUNK, seq)
    nc = seq // c
    hp = batch * PAIRS
    srcs, per_peer = side if side is not None else ([], False)
    n_s = len(srcs)

    def body(*refs):
        z_ref, y_ref, s_ref, st = refs[0], refs[1 + n_s], refs[2 + n_s], refs[3 + 2 * n_s]
        _side_exchange(refs[1:1 + n_s], refs[3 + n_s:3 + 2 * n_s], per_peer, refs[4 + 2 * n_s:], nc)

        @pl.when(pl.program_id(0) == 0)
        def _():
            st[...] = jnp.zeros_like(st)

        s0 = st[...]
        s_ref[0] = s0
        y, s1 = _scan_chunk(s0, *[_pair_stack(z_ref, comp * HW) for comp in SCAN_ARGS])
        _pair_store(y_ref, 0, y)
        st[...] = s1

    res = pl.pallas_call(
        body, name="rwkv_scan_fwd", grid=(nc,),
        in_specs=[pl.BlockSpec((batch, c, 6 * HW), lambda i: (0, i, 0))] + [_HBM_SPEC] * n_s,
        out_specs=[pl.BlockSpec((batch, c, HW), lambda i: (0, i, 0)),
                   pl.BlockSpec((1, hp, PAIR_W, PAIR_W), lambda i: (i, 0, 0, 0))] + [_HBM_SPEC] * n_s,
        out_shape=[jax.ShapeDtypeStruct((batch, seq, HW), f32), jax.ShapeDtypeStruct((nc, hp, PAIR_W, PAIR_W), f32)]
        + _side_out_shapes(srcs, per_peer),
        scratch_shapes=[pltpu.VMEM((hp, PAIR_W, PAIR_W), f32)] + _side_sems(n_s),
        compiler_params=_cp(("arbitrary",)),
    )(main6.reshape(batch, seq, 6 * HW), *srcs)
    return res[0].reshape(batch * seq, HW), res[1], list(res[2:])


def _scan_bwd(main6, states, dy, extra, batch, seq, side=None):
    c = min(SCAN_CHUNK, seq)
    nc = seq // c
    hp = batch * PAIRS
    srcs, per_peer = side if side is not None else ([], False)
    n_s = len(srcs)

    def body(*refs):
        z_ref, s_ref, dy_ref, ex_ref = refs[:4]
        dz_ref, dst = refs[4 + n_s], refs[5 + 2 * n_s]
        _side_exchange(refs[4:4 + n_s], refs[5 + n_s:5 + 2 * n_s], per_peer, refs[6 + 2 * n_s:], nc)

        @pl.when(pl.program_id(0) == 0)
        def _():
            dst[...] = jnp.zeros_like(dst)

        _, vjp = jax.vjp(_scan_chunk, s_ref[0], *[_pair_stack(z_ref, comp * HW) for comp in SCAN_ARGS])
        g = vjp((_pair_stack(dy_ref, 0), dst[...]))
        dst[...] = g[0]
        for arg, comp in enumerate(SCAN_ARGS):
            _pair_store(dz_ref, comp * HW, g[1 + arg], ex_ref if comp < 3 else None)

    back = lambda i: (0, nc - 1 - i, 0)
    wide = pl.BlockSpec((batch, c, 6 * HW), back)
    res = pl.pallas_call(
        body, name="rwkv_scan_bwd", grid=(nc,),
        in_specs=[wide, pl.BlockSpec((1, hp, PAIR_W, PAIR_W), lambda i: (nc - 1 - i, 0, 0, 0)),
                  pl.BlockSpec((batch, c, HW), back), pl.BlockSpec((batch, c, 3 * HW), back)] + [_HBM_SPEC] * n_s,
        out_specs=[wide] + [_HBM_SPEC] * n_s,
        out_shape=[jax.ShapeDtypeStruct((batch, seq, 6 * HW), f32)] + _side_out_shapes(srcs, per_peer),
        scratch_shapes=[pltpu.VMEM((hp, PAIR_W, PAIR_W), f32)] + _side_sems(n_s),
        compiler_params=_cp(("arbitrary",)),
    )(main6.reshape(batch, seq, 6 * HW), states, dy.reshape(batch, seq, HW), extra.reshape(batch, seq, 3 * HW), *srcs)
    return res[0].reshape(batch * seq, 6 * HW), list(res[1:])


def _pad_cols(x, width):
    return jnp.pad(x, ((0, 0), (0, width - x.shape[1])))


def _split_w_in(wt):
    z = lambda rows: jnp.zeros((rows, wt.shape[1]), wt.dtype)
    w_r = jnp.concatenate([wt[1544:3080], wt[3080:3144], z(64), wt[3144:3208], z(64), wt[3208:3336]], axis=0)
    return wt[:1536], jnp.concatenate([wt[1536:1544], z(120)], axis=0), w_r, wt[3336:3848], wt[3848:]


def _merge_w_in(g_qkv, g_f, g_r, g_mq, g_g):
    return jnp.concatenate([g_qkv, g_f[:8], g_r[:1536], g_r[1536:1600], g_r[1664:1728], g_r[1792:], g_mq, g_g], axis=0)


def _pad_lora(v):
    z64 = jnp.zeros((1, 64), v.dtype)
    return jnp.concatenate([v[:, :1536], v[:, 1536:1600], z64, v[:, 1600:1664], z64, v[:, 1664:]], axis=1)


def _unpad_lora(v):
    return jnp.concatenate([v[:, :1536], v[:, 1536:1600], v[:, 1664:1728], v[:, 1792:]], axis=1)


def _local_step(x, mem, target, w, p, late=None, early=None, last=None):
    batch, seq, _ = x.shape
    t = batch * seq
    x2, tg2, mem2 = x.reshape(t, D), target.reshape(t, D), mem.reshape(batch * MEM_LEN, D)
    w_qkv, w_f, w_r, w_mq, w_g3 = _split_w_in(w["w_in"])
    mu = _pad_lora(p["rwkv_mu"])
    bias = _pad_cols(p["fox_f_bias"], 128)
    r_k = p["rwkv_r_k"].reshape(1, HW)
    post_params = [p["rwkv_gn_g"], p["rwkv_gn_b"], r_k]
    rw_widths = [HW, HW, HW, LORA_PAD, LORA_PAD, LORA_PAD]
    six = [HW] * 6

    p_g, u = _matmul("proj_gate", _lazy(_fn_rms, [(x2, [D])], D, params=[p["pre1_g"]]), w_g3, "nt", out_dtype=bf16)
    p_qkv = _matmul("proj_qkv", u, w_qkv, "nt", out_dtype=bf16)
    p_f = _matmul("proj_f", u, w_f, "nt")
    p_r = _matmul("proj_rwkv", u, w_r, "nt", out_dtype=bf16)
    p_mq = _matmul("proj_memq", u, w_mq, "nt", out_dtype=bf16)

    c = _fox_gate_fwd(p_f, bias, batch, seq)
    c_rows = c[:, :HEADS].reshape(batch, seq, HEADS).transpose(0, 2, 1)
    fox_o, lse, gathered = _fox_fwd(p_qkv, c, c_rows, batch, seq, side=(late[0], False) if late else None)
    if late:
        w = {**w, **late[2](gathered, 0)}
    fox_out = fox_o.astype(bf16)

    w_up = jnp.pad(w["rwkv_w_up"].astype(f32), ((0, LORA_PAD - 64), (0, 0)))
    a_up = jnp.pad(w["rwkv_a_up"].astype(f32), ((0, LORA_PAD - 64), (0, 0)))
    pre_params = [p["rwkv_w0"], w_up, p["rwkv_a0"], a_up, w["rwkv_g_up"].astype(f32), p["rwkv_k_k"], p["rwkv_k_a"]]
    ps = _tokshift_fwd(p_r, mu, batch, seq)
    main6, g_rw = _rows_fwd("rwkv_pre", _fn_rwkv_pre, [], [(ps, rw_widths)], pre_params, [six, [HW]], tm=256,
                            dtypes=[f32, bf16])
    y_rw, states, gathered = _scan_fwd(main6, batch, seq, side=(late[1], False) if late else None)
    if late:
        w = {**w, **late[2](gathered, 1)}
    post_consts = []
    post_rows = [(y_rw, [HW]), (main6, [HW, HW, HW]), (g_rw, [HW])]
    fn_post = _fn_rwkv_post

    (rwkv_out,) = _rows_fwd("rwkv_post", fn_post, post_consts, post_rows, post_params, [[HW]], dtypes=[bf16], tm=256)

    mem_kv, memn = _matmul("proj_memkv", _lazy(_fn_rms, [(mem2, [D])], D, params=[p["mem_norm_g"]]), w["w_mem_kv"], "nn")
    mem_out = _mem_fwd(p_mq, mem_kv, batch, seq)

    a_fox = _matmul("out_fox", fox_out, w["w_fox_out"], "nn", out_dtype=bf16)
    a_rwkv = _matmul("out_rwkv", rwkv_out, w["w_rwkv_out"], "nn", out_dtype=bf16)
    a_mem = _matmul("out_mem", mem_out, w["w_mem_out"], "nn", out_dtype=bf16)
    merge_rows = [(a_fox, [D]), (a_rwkv, [D]), (a_mem, [D]), (p_g, [D, D, D])]
    yy, merged = _matmul("out_o", _lazy(_fn_merge, merge_rows, D), w["w_o"], "nn")
    post1_rows = [(yy, [D]), (x2, [D])]
    post1_params = [p["post1_g"], p["pre2_g"]]
    h1, u2 = _rows_fwd("post1", _fn_post1, [], post1_rows, post1_params, [[D], [D]], dtypes=[f32, bf16])
    gp = _matmul("ffn_gate", u2, w["w_ffn_gate"], "nt", out_dtype=bf16)
    up = _matmul("ffn_up", u2, w["w_ffn_up"], "nt", out_dtype=bf16)
    ffn, hmid = _matmul("ffn_down", _lazy(_fn_swiglu, [(gp, [D_FF]), (up, [D_FF])], D_FF), w["w_ffn_down"], "nn")
    final_rows = [(ffn, [D]), (h1, [D])]

    gw, gp_ = {}, {}
    (d_ffn, d_h1), (gp_["post2_g"], loss) = _rows_bwd("final", _fn_final, [(tg2, [D])], final_rows, [p["post2_g"]], [], [],
                                                      n_sums=1, dtypes=[bf16, f32])
    gw["w_ffn_down"] = _matmul("ffn_down_dw", hmid, d_ffn, "tn", out_dtype=bf16)
    (d_gp, d_up), _ = _matmul_then_vjp("ffn_down_dx", d_ffn, w["w_ffn_down"], "nt", _fn_swiglu,
                                       [(gp, [D_FF]), (up, [D_FF])], [bf16, bf16])
    gw["w_ffn_gate"] = _matmul("ffn_gate_dw", d_gp, u2, "tn", out_dtype=bf16)
    gw["w_ffn_up"] = _matmul("ffn_up_dw", d_up, u2, "tn", out_dtype=bf16)
    d_u2_gate = _matmul("ffn_gate_dx", d_gp, w["w_ffn_gate"], "nn")
    (d_yy, d_x_res), (gp_["post1_g"], gp_["pre2_g"]) = _matmul_then_vjp(
        "ffn_up_dx", d_up, w["w_ffn_up"], "nn", _fn_post1, post1_rows, [bf16, f32], params=post1_params,
        first_cts=[d_h1], add=d_u2_gate)
    gw["w_o"] = _matmul("out_o_dw", merged, d_yy, "tn", out_dtype=bf16)
    (d_a_fox, d_a_rwkv, d_a_mem, d_p_g), _ = _matmul_then_vjp("out_o_dx", d_yy, w["w_o"], "nt", _fn_merge, merge_rows,
                                                             [bf16] * 4)
    d_fox_out = _matmul("out_fox_dx", d_a_fox, w["w_fox_out"], "nt")
    gw["w_fox_out"] = _matmul("out_fox_dw", fox_out, d_a_fox, "tn", out_dtype=bf16)
    gw["w_rwkv_out"] = _matmul("out_rwkv_dw", rwkv_out, d_a_rwkv, "tn", out_dtype=bf16)
    d_mem_out = _matmul("out_mem_dx", d_a_mem, w["w_mem_out"], "nt")
    gw["w_mem_out"] = _matmul("out_mem_dw", mem_out, d_a_mem, "tn", out_dtype=bf16)

    d_p_mq, d_km, d_vm = _mem_bwd(p_mq, mem_kv, d_mem_out, batch, seq)
    d_mem_kv = jnp.concatenate([d_km, d_vm], axis=1).astype(bf16)
    gw["w_mem_kv"] = _matmul("proj_memkv_dw", memn, d_mem_kv, "tn", out_dtype=bf16)
    d_memn = _matmul("proj_memkv_dx", d_mem_kv, w["w_mem_kv"], "nt")
    _, (gp_["mem_norm_g"],) = _rows_bwd("rms_mem_bwd", _fn_rms, [], [(mem2, [D])], [p["mem_norm_g"]], [[D]], [d_memn])

    d_q, d_k, d_v, d_cq, d_ck = _fox_bwd(p_qkv, c, c_rows, fox_o, lse, d_fox_out, batch, seq)
    d_p_qkv = jnp.concatenate([d_q, d_k, d_v], axis=1).astype(bf16)
    d_p_f, d_bias = _fox_gate_bwd(p_f, bias, d_cq, d_ck, batch, seq)
    gp_["fox_f_bias"] = d_bias[:, :HEADS]

    (d_y_rw, d_main6_post, d_g_rw), (gp_["rwkv_gn_g"], gp_["rwkv_gn_b"], d_rk) = _matmul_then_vjp(
        "out_rwkv_dx", d_a_rwkv, w["w_rwkv_out"], "nt", fn_post, post_rows, [f32, f32, bf16], params=post_params)
    gp_["rwkv_r_k"] = d_rk.reshape(1, HEADS, HD)
    d_main6, early_got = _scan_bwd(main6, states, d_y_rw, d_main6_post, batch, seq,
                                   side=(early(gw), True) if early else None)

    def fn_pre_sum(*args):
        return _fn_rwkv_pre(*args)

    (d_ps,), d_pre = _rows_bwd("rwkv_pre_bwd", fn_pre_sum, [], [(ps, rw_widths)], pre_params, [six, [HW]],
                               [d_main6, d_g_rw], tm=256, dtypes=[bf16])
    gp_["rwkv_w0"], d_w_up, gp_["rwkv_a0"], d_a_up, gw["rwkv_g_up"], gp_["rwkv_k_k"], gp_["rwkv_k_a"] = d_pre
    gw["rwkv_w_up"], gw["rwkv_a_up"] = d_w_up[:64], d_a_up[:64]
    d_p_r, d_mu = _tokshift_bwd(p_r, mu, d_ps, batch, seq)
    gp_["rwkv_mu"] = _unpad_lora(d_mu)

    gw["w_in"] = _merge_w_in(_matmul("proj_qkv_dw", d_p_qkv, u, "tn", out_dtype=bf16), _matmul("proj_f_dw", d_p_f, u, "tn", out_dtype=bf16),
                             _matmul("proj_rwkv_dw", d_p_r, u, "tn", out_dtype=bf16), _matmul("proj_memq_dw", d_p_mq, u, "tn", out_dtype=bf16),
                             _matmul("proj_gate_dw", d_p_g, u, "tn", out_dtype=bf16))
    d_x, gp_["pre1_g"], last_got = _input_cotangent(
        "proj_dx", [d_p_qkv, d_p_f, d_p_r, d_p_mq, d_p_g], [w_qkv, w_f, w_r, w_mq, w_g3], x2, p["pre1_g"], d_x_res,
        side=(last(gw), True) if last else None)
    return loss, d_x.reshape(x.shape), gw, gp_, early_got, last_got


def _adamw(name, recv, row_off, w, m, v):
    _, rows, cols = w.shape
    row_tiles = [t for t in range(16, min(rows, 128) + 1, 16) if rows % t == 0 and row_off % t == 0]
    if row_tiles:
        tr, tc = max(row_tiles), cols
        first, grid = row_off // tr, (rows // tr,)
        at = lambda i: (0, first + i, 0)
        mine = lambda i: (0, i, 0)
    else:
        assert row_off == 0 and recv.shape[1] == rows
        tr, tc = rows, 128
        grid = (cols // tc,)
        at = mine = lambda i: (0, 0, i)

    def body(g_ref, w_ref, m_ref, v_ref, go_ref, d_ref, mo_ref, vo_ref):
        g = g_ref[0].astype(f32)
        for s in range(1, N_DEV):
            g = g + g_ref[s].astype(f32)
        m_new = ADAM_B1 * m_ref[0] + (1.0 - ADAM_B1) * g
        v_new = ADAM_B2 * v_ref[0] + (1.0 - ADAM_B2) * (g * g)
        m_hat = m_new / (1.0 - ADAM_B1 ** ADAM_STEP)
        v_hat = v_new / (1.0 - ADAM_B2 ** ADAM_STEP)
        go_ref[0] = g
        d_ref[0] = -ADAM_LR * (m_hat / (jnp.sqrt(v_hat) + ADAM_EPS) + ADAM_WD * w_ref[0])
        mo_ref[0] = m_new
        vo_ref[0] = v_new

    spec = pl.BlockSpec((1, tr, tc), mine)
    return pl.pallas_call(
        body, name=name, grid=grid,
        in_specs=[pl.BlockSpec((N_DEV, tr, tc), at), spec, spec, spec],
        out_specs=[spec] * 4, out_shape=[jax.ShapeDtypeStruct(w.shape, f32)] * 4,
        compiler_params=_cp(("parallel",)),
    )(recv, w, m, v)


GROUPS = (
    ("in", ("w_in",), 0),
    ("memkv", ("w_mem_kv",), 0),
    ("ffn_gu", ("w_ffn_gate", "w_ffn_up"), 0),
    ("down_o", ("w_ffn_down", "w_o"), 0),
    ("outs", ("w_fox_out", "w_rwkv_out", "w_mem_out"), 0),
    ("lora", ("rwkv_w_up", "rwkv_a_up", "rwkv_g_up"), 0),
)
FIRST_GROUPS = ("in", "memkv")
LATE_GROUPS = (("down_o", "outs", "lora"), ("ffn_gu",))
EARLY_GRAD_GROUPS = ("memkv", "ffn_gu", "down_o", "outs")
LAST_GRAD_GROUPS = ("in", "lora")
SHARD_AXIS = {n: a for n, _, a in SHARDED}
SMALL_ROWS = 16
LOSS_LANES = 128


def _group_local(shards, members, join):
    parts = [shards[n].reshape(shards[n].shape[-2:]) for n in members]
    return parts[0] if len(parts) == 1 else jnp.concatenate(parts, axis=join)


def _group_split(arr, members, join, lead=False):
    out, off = {}, 0
    for n in members:
        shape = dict((k, s) for k, s, _ in SHARDED)[n]
        size = _block_shape(shape, SHARD_AXIS[n])[join]
        idx = [slice(None)] * arr.ndim
        idx[arr.ndim - 2 + join] = slice(off, off + size)
        out[n] = arr[tuple(idx)]
        off += size
    return out


def _full_from_blocks(blocks, axis):
    if axis == 0:
        return blocks.reshape(-1, blocks.shape[2])
    return blocks.transpose(1, 0, 2).reshape(blocks.shape[1], -1)


def _blocks_from_full(full, axis):
    if axis == 0:
        return full.reshape(N_DEV, -1, full.shape[1])
    return full.reshape(full.shape[0], N_DEV, -1).transpose(1, 0, 2)


def _assemble(gathered, names):
    out = {}
    for arr, g in zip(gathered, names):
        _, members, join = [grp for grp in GROUPS if grp[0] == g][0]
        for n, blk in _group_split(arr, members, join, lead=True).items():
            out[n] = _full_from_blocks(blk, SHARD_AXIS[n])
    return out


def _grad_blocks(gw, names):
    out = []
    for g in names:
        _, members, join = [grp for grp in GROUPS if grp[0] == g][0]
        parts = [_blocks_from_full(gw[n].astype(bf16), SHARD_AXIS[n]) for n in members]
        out.append(parts[0] if len(parts) == 1 else jnp.concatenate(parts, axis=1 + join))
    return out


def _small_pack(d):
    flat = jnp.concatenate([d[n].reshape(-1) for n, _ in REPLICATED])
    return jnp.pad(flat, (0, SMALL_ROWS * LANES - REPL_ELEMS)).reshape(SMALL_ROWS, LANES)


def _small_unpack(packed):
    out, flat, off = {}, packed.reshape(-1), 0
    for n, shape in REPLICATED:
        k = _rows_of((LANES,) + shape)
        out[n] = flat[off:off + k].reshape(shape)
        off += k
    return out


def kernel(x, mem, pre1_g, post1_g, pre2_g, post2_g, mem_norm_g, w_in, fox_f_bias, rwkv_mu, rwkv_w0, rwkv_w_up, rwkv_a0, rwkv_a_up, rwkv_g_up, rwkv_k_k, rwkv_k_a, rwkv_r_k, rwkv_gn_g, rwkv_gn_b, w_mem_kv, w_fox_out, w_rwkv_out, w_mem_out, w_o, w_ffn_gate, w_ffn_up, w_ffn_down, loss_target, m_pre1_g, m_post1_g, m_pre2_g, m_post2_g, m_mem_norm_g, m_w_in, m_fox_f_bias, m_rwkv_mu, m_rwkv_w0, m_rwkv_w_up, m_rwkv_a0, m_rwkv_a_up, m_rwkv_g_up, m_rwkv_k_k, m_rwkv_k_a, m_rwkv_r_k, m_rwkv_gn_g, m_rwkv_gn_b, m_w_mem_kv, m_w_fox_out, m_w_rwkv_out, m_w_mem_out, m_w_o, m_w_ffn_gate, m_w_ffn_up, m_w_ffn_down, v_pre1_g, v_post1_g, v_pre2_g, v_post2_g, v_mem_norm_g, v_w_in, v_fox_f_bias, v_rwkv_mu, v_rwkv_w0, v_rwkv_w_up, v_rwkv_a0, v_rwkv_a_up, v_rwkv_g_up, v_rwkv_k_k, v_rwkv_k_a, v_rwkv_r_k, v_rwkv_gn_g, v_rwkv_gn_b, v_w_mem_kv, v_w_fox_out, v_w_rwkv_out, v_w_mem_out, v_w_o, v_w_ffn_gate, v_w_ffn_up, v_w_ffn_down):
    args = dict(locals())
    turn = lambda n, a: jnp.swapaxes(a, 1, 2) if n in TRANSPOSED else a
    wts = {n: turn(n, args[n]) for n in WEIGHT_ORDER}
    ms = {n: turn(n, args["m_" + n]) for n in WEIGHT_ORDER}
    vs = {n: turn(n, args["v_" + n]) for n in WEIGHT_ORDER}

    groups = {g: (members, join) for g, members, join in GROUPS}
    w_bf16 = {n: wts[n].astype(bf16) for n, _, _ in SHARDED}

    def send(g):
        return _group_local(w_bf16, *groups[g])

    first = _exchange("gather_first", [send(g) for g in FIRST_GROUPS], per_peer=False)
    full = _assemble(first, FIRST_GROUPS)
    small_in = {n: (wts[n] if n == "rwkv_r_k" else wts[n].reshape(wts[n].shape[-2:])) for n, _ in REPLICATED}
    late = ([send(g) for g in LATE_GROUPS[0]], [send(g) for g in LATE_GROUPS[1]],
            lambda got, which: _assemble(got, LATE_GROUPS[which]))
    loss_part, grad_x, gw, gp, early_got, last_got = _local_step(
        x, mem, loss_target, full, small_in, late=late, early=lambda g: _grad_blocks(g, EARLY_GRAD_GROUPS),
        last=lambda g: _grad_blocks(g, LAST_GRAD_GROUPS))
    small_got, loss_got = _exchange("exchange_small", [_small_pack(gp).astype(bf16), jnp.broadcast_to(loss_part, (8, LOSS_LANES))],
                                    per_peer=False)
    received = dict(zip(EARLY_GRAD_GROUPS + LAST_GRAD_GROUPS, list(early_got) + list(last_got)))

    outs = [{}, {}, {}, {}]
    for g, members, _ in GROUPS:
        off = 0
        for n in members:
            for o, arr in zip(outs, _adamw("adamw_" + n, received[g], off, wts[n], ms[n], vs[n])):
                o[n] = arr
            off += wts[n].shape[1]
    res = _adamw("adamw_small", small_got, 0, *[_small_pack(d)[None] for d in (wts, ms, vs)])
    for o, arr in zip(outs, res):
        o.update(_small_unpack(arr))
    loss = jnp.sum(loss_got[:, 0, 0])
    return (loss, grad_x, *[turn(n, o[n].reshape(wts[n].shape)) for o in outs for n in WEIGHT_ORDER])
```

```python
import functools

import jax
import jax.numpy as jnp
from jax import lax
from jax.experimental import pallas as pl
from jax.experimental.pallas import tpu as pltpu

f32 = jnp.float32
bf16 = jnp.bfloat16
_HI = lax.Precision.HIGHEST

D = 1024
HEADS = 8
HD = 64
HW = HEADS * HD
MEM_HEADS = 4
MEM_HD = 128
MEM_W = 512
MEM_LEN = 256
D_FF = 2816
LORA_PAD = 128
NORM_EPS = 1e-6
GN_EPS = 64e-5
SCAN_CHUNK = 64
N_DEV = 8
LANES = 1024
VMEM_LIMIT = 56 * 1024 * 1024

ADAM_LR = 0.001
ADAM_B1 = 0.9
ADAM_B2 = 0.999
ADAM_EPS = 1e-08
ADAM_WD = 0.01
ADAM_STEP = 10

TRANSPOSED = ("w_in", "w_ffn_gate", "w_ffn_up")
SHARDED = (
    ("w_in", (6920, 1024), 0),
    ("w_ffn_gate", (2816, 1024), 0),
    ("w_ffn_up", (2816, 1024), 0),
    ("w_ffn_down", (2816, 1024), 0),
    ("w_mem_kv", (1024, 1024), 0),
    ("w_o", (1024, 1024), 0),
    ("w_fox_out", (512, 1024), 1),
    ("w_rwkv_out", (512, 1024), 1),
    ("w_mem_out", (512, 1024), 1),
    ("rwkv_w_up", (64, 512), 1),
    ("rwkv_a_up", (64, 512), 1),
    ("rwkv_g_up", (128, 512), 1),
)
REPLICATED = (
    ("pre1_g", (1, 1024)), ("post1_g", (1, 1024)), ("pre2_g", (1, 1024)), ("post2_g", (1, 1024)),
    ("mem_norm_g", (1, 1024)), ("fox_f_bias", (1, 8)), ("rwkv_mu", (1, 1792)), ("rwkv_w0", (1, 512)),
    ("rwkv_a0", (1, 512)), ("rwkv_k_k", (1, 512)), ("rwkv_k_a", (1, 512)), ("rwkv_r_k", (1, 8, 64)),
    ("rwkv_gn_g", (1, 512)), ("rwkv_gn_b", (1, 512)),
)
WEIGHT_ORDER = ('pre1_g', 'post1_g', 'pre2_g', 'post2_g', 'mem_norm_g', 'w_in', 'fox_f_bias', 'rwkv_mu',
                'rwkv_w0', 'rwkv_w_up', 'rwkv_a0', 'rwkv_a_up', 'rwkv_g_up', 'rwkv_k_k', 'rwkv_k_a',
                'rwkv_r_k', 'rwkv_gn_g', 'rwkv_gn_b', 'w_mem_kv', 'w_fox_out', 'w_rwkv_out', 'w_mem_out',
                'w_o', 'w_ffn_gate', 'w_ffn_up', 'w_ffn_down')


def _block_shape(shape, axis):
    return tuple(s // N_DEV if i == axis else s for i, s in enumerate(shape))


def _rows_of(shape):
    n = 1
    for s in shape:
        n *= s
    return n // LANES


REPL_ELEMS = sum(_rows_of((LANES,) + s) for _, s in REPLICATED)


def _cp(sem=None):
    return pltpu.CompilerParams(dimension_semantics=sem, vmem_limit_bytes=VMEM_LIMIT)


def _tile(dim, cap):
    best = None
    for t in range(128, min(dim, cap) + 1, 128):
        if dim % t == 0:
            best = t
    return best if best is not None else dim


def _two_terms(x):
    hi = x.astype(bf16)
    return hi, (x - hi.astype(f32)).astype(bf16)


def _dg(a, b, dims, exact):
    if exact == "mask":
        hi, mid = _two_terms(b)
        lo = (b - hi.astype(f32) - mid.astype(f32)).astype(bf16)
        dot = functools.partial(lax.dot_general, a.astype(bf16), dimension_numbers=dims, preferred_element_type=f32)
        return dot(hi) + (dot(mid) + dot(lo))
    if exact == "split":
        (a_hi, a_lo), (b_hi, b_lo) = _two_terms(a), _two_terms(b)
        dot = functools.partial(lax.dot_general, dimension_numbers=dims, preferred_element_type=f32)
        return dot(a_hi, b_hi) + (dot(a_hi, b_lo) + dot(a_lo, b_hi))
    if exact:
        return lax.dot_general(a, b, dims, precision=_HI, preferred_element_type=f32)
    return lax.dot_general(a.astype(bf16), b.astype(bf16), dims, preferred_element_type=f32)


def _make_mm(batched, exact):
    o = 1 if batched else 0
    bd = ((0,), (0,)) if batched else ((), ())
    d_nn = (((1 + o,), (o,)), bd)
    d_nt = (((1 + o,), (1 + o,)), bd)
    d_tn = (((o,), (o,)), bd)

    @jax.custom_vjp
    def nn(a, b):
        return _dg(a, b, d_nn, exact)

    @jax.custom_vjp
    def nt(a, b):
        return _dg(a, b, d_nt, exact)

    @jax.custom_vjp
    def tn(a, b):
        return _dg(a, b, d_tn, exact)

    nn.defvjp(lambda a, b: (_dg(a, b, d_nn, exact), (a, b)),
              lambda res, g: (_dg(g, res[1], d_nt, exact), _dg(res[0], g, d_tn, exact)))
    nt.defvjp(lambda a, b: (_dg(a, b, d_nt, exact), (a, b)),
              lambda res, g: (_dg(g, res[1], d_nn, exact), _dg(g, res[0], d_tn, exact)))
    tn.defvjp(lambda a, b: (_dg(a, b, d_tn, exact), (a, b)),
              lambda res, g: (_dg(res[1], g, d_nt, exact), _dg(res[0], g, d_nn, exact)))
    return nn, nt, tn


def _sigmoid(x):
    return 1.0 / (1.0 + jnp.exp(-x))


def _head_sum_raw(x):
    width = 2 * HD
    i = lax.broadcasted_iota(jnp.int32, (width, width), 0) // HD
    j = lax.broadcasted_iota(jnp.int32, (width, width), 1) // HD
    m = (i == j).astype(bf16)
    dims = (((1,), (0,)), ((), ()))
    out = []
    for p in range(x.shape[1] // width):
        xp = x[:, p * width:(p + 1) * width]
        hi = xp.astype(bf16)
        lo = (xp - hi.astype(f32)).astype(bf16)
        out.append(lax.dot_general(hi, m, dims, preferred_element_type=f32)
                   + lax.dot_general(lo, m, dims, preferred_element_type=f32))
    return jnp.concatenate(out, axis=1)


@jax.custom_vjp
def _head_sum(x):
    return _head_sum_raw(x)


_head_sum.defvjp(lambda x: (_head_sum_raw(x), None), lambda _, g: (_head_sum_raw(g),))


WEIGHT_TILE_BYTES = 13 * 512 * 1024
ACC_TILE_BYTES = 8 * 1024 * 1024


def _lazy(fn, rows, width, params=()):
    return (fn, rows, width, list(params))


def _matmul(name, a, b, mode, add=None, out_dtype=f32):
    has_add = add is not None
    if isinstance(a, tuple):
        a_fn, a_rows, a_width, a_params = a
        a_arrays = [r for r, _ in a_rows]
        a_shape = (a_arrays[0].shape[0], a_width)
    else:
        a_fn, a_rows, a_params, a_arrays, a_shape = None, None, [], [a], a.shape
    n_r = len(a_arrays)
    n_a = n_r + len(a_params)

    def load_a(refs):
        if a_fn is None:
            return refs[0][...].astype(bf16)
        pieces = []
        for r, (_, widths) in zip(refs[:n_r], a_rows):
            pieces += _pieces(r, widths)
        return a_fn(*pieces, *[p[...] for p in refs[n_r:]])[0].astype(bf16)

    if mode == "tn":
        assert a_fn is None
        (k, m), (_, n) = a_shape, b.shape
        tn = _tile(n, max(128, ACC_TILE_BYTES // (4 * m)))
        tk = _tile(k, 2048)
        nk = k // tk

        def body(*refs):
            b_ref, o_ref, acc = refs[n_a:]

            @pl.when(pl.program_id(1) == 0)
            def _():
                acc[...] = jnp.zeros_like(acc)

            acc[...] += lax.dot_general(load_a(refs[:n_a]), b_ref[...].astype(bf16),
                                        (((0,), (0,)), ((), ())), preferred_element_type=f32)

            @pl.when(pl.program_id(1) == nk - 1)
            def _():
                o_ref[...] = acc[...].astype(o_ref.dtype)

        return pl.pallas_call(
            body, name=name, grid=(n // tn, nk),
            in_specs=[pl.BlockSpec((tk, r.shape[1]), lambda j, kk: (kk, 0)) for r in a_arrays]
            + [pl.BlockSpec((tk, tn), lambda j, kk: (kk, j))],
            out_specs=pl.BlockSpec((m, tn), lambda j, kk: (0, j)), out_shape=jax.ShapeDtypeStruct((m, n), out_dtype),
            scratch_shapes=[pltpu.VMEM((m, tn), f32)],
            compiler_params=_cp(("parallel", "arbitrary")),
        )(*a_arrays, b)

    (m, k) = a_shape
    n = b.shape[1] if mode == "nn" else b.shape[0]
    tm = _tile(m, 1024 if a_fn is None else 512)
    tn = _tile(n, max(128, WEIGHT_TILE_BYTES // (2 * k)))
    dims = (((1,), (0,)), ((), ())) if mode == "nn" else (((1,), (1,)), ((), ()))
    b_spec = pl.BlockSpec((k, tn), lambda j, i: (0, j)) if mode == "nn" else pl.BlockSpec((tn, k), lambda j, i: (j, 0))
    o_spec = pl.BlockSpec((tm, tn), lambda j, i: (i, j))

    keep = a_fn is not None
    assert not keep or tn == n

    def body(*refs):
        b_ref = refs[n_a]
        a_val = load_a(refs[:n_a])
        r = lax.dot_general(a_val, b_ref[...].astype(bf16), dims, preferred_element_type=f32)
        if has_add:
            r = r + refs[n_a + 1][...]
        if keep:
            refs[-2][...] = r.astype(refs[-2].dtype)
            refs[-1][...] = a_val
        else:
            refs[-1][...] = r.astype(refs[-1].dtype)

    res = pl.pallas_call(
        body, name=name, grid=(n // tn, m // tm),
        in_specs=[pl.BlockSpec((tm, r.shape[1]), lambda j, i: (i, 0)) for r in a_arrays]
        + [pl.BlockSpec(p.shape, lambda j, i: (0, 0)) for p in a_params] + [b_spec] + ([o_spec] if has_add else []),
        out_specs=[o_spec] + ([pl.BlockSpec((tm, k), lambda j, i: (i, 0))] if keep else []),
        out_shape=[jax.ShapeDtypeStruct((m, n), out_dtype)] + ([jax.ShapeDtypeStruct((m, k), bf16)] if keep else []),
        compiler_params=_cp(("parallel", "arbitrary")),
    )(*a_arrays, *a_params, b, *([add] if has_add else []))
    return tuple(res) if keep else res[0]


def _input_cotangent(name, a_list, b_list, x, gain, add, side=None):
    m = a_list[0].shape[0]
    tm = _tile(m, 256)
    n_g = len(a_list)
    srcs, per_peer = side if side is not None else ([], False)
    n_s = len(srcs)

    def body(*refs):
        x_ref, g_ref, add_ref = refs[2 * n_g:2 * n_g + 3]
        src_refs = refs[2 * n_g + 3:2 * n_g + 3 + n_s]
        dx_ref, dg_ref = refs[2 * n_g + 3 + n_s:2 * n_g + 5 + n_s]
        _side_exchange(src_refs, refs[2 * n_g + 5 + n_s:2 * n_g + 5 + 2 * n_s], per_peer, refs[2 * n_g + 5 + 2 * n_s:], m // tm)
        d_u = None
        for g in range(n_g):
            r = lax.dot_general(refs[g][...].astype(bf16), refs[n_g + g][...].astype(bf16), (((1,), (0,)), ((), ())),
                                preferred_element_type=f32)
            d_u = r if d_u is None else d_u + r
        _, vjp = jax.vjp(_rms, x_ref[...], g_ref[...])
        d_x, d_gain = vjp(d_u)
        dx_ref[...] = d_x + add_ref[...]

        @pl.when(pl.program_id(0) == 0)
        def _():
            dg_ref[...] = jnp.zeros_like(dg_ref)

        dg_ref[...] += d_gain

    rows = pl.BlockSpec((tm, x.shape[1]), lambda i: (i, 0))
    whole = lambda b: pl.BlockSpec(b.shape, lambda i: (0, 0))
    res = pl.pallas_call(
        body, name=name, grid=(m // tm,),
        in_specs=[pl.BlockSpec((tm, a.shape[1]), lambda i: (i, 0)) for a in a_list] + [whole(b) for b in b_list]
        + [rows, whole(gain), rows] + [_HBM_SPEC] * n_s,
        out_specs=[rows, whole(gain)] + [_HBM_SPEC] * n_s,
        out_shape=[jax.ShapeDtypeStruct(x.shape, f32), jax.ShapeDtypeStruct(gain.shape, f32)] + _side_out_shapes(srcs, per_peer),
        scratch_shapes=_side_sems(n_s),
        compiler_params=_cp(("arbitrary",)),
    )(*a_list, *b_list, x, gain, add, *srcs)
    return res[0], res[1], list(res[2:])


def _pieces(ref, widths):
    out, off = [], 0
    for w in widths:
        out.append(ref[:, off:off + w].astype(f32))
        off += w
    return out


def _store_pieces(ref, widths, vals, add_ref=None):
    off = 0
    for w, v in zip(widths, vals):
        ref[:, off:off + w] = (v if add_ref is None else v + add_ref[:, off:off + w]).astype(ref.dtype)
        off += w


def _rows_fwd(name, fn, consts, rows, params, outs, n_sums=0, tm=512, dtypes=None):
    t = (consts + rows)[0][0].shape[0]
    tm = min(tm, t)
    ins = consts + rows
    n_in, n_p, n_o = len(ins), len(params), len(outs)
    dtypes = dtypes or [f32] * n_o

    def body(*refs):
        in_refs, p_refs = refs[:n_in], refs[n_in:n_in + n_p]
        o_refs, s_refs = refs[n_in + n_p:n_in + n_p + n_o], refs[n_in + n_p + n_o:]
        vals = []
        for r, (_, widths) in zip(in_refs, ins):
            vals += _pieces(r, widths)
        res = fn(*vals, *[p[...] for p in p_refs])
        pos = 0
        for r, widths in zip(o_refs, outs):
            _store_pieces(r, widths, res[pos:pos + len(widths)])
            pos += len(widths)

        @pl.when(pl.program_id(0) == 0)
        def _():
            for s in s_refs:
                s[...] = jnp.zeros_like(s)

        for s, v in zip(s_refs, res[pos:]):
            s[...] += v

    row_spec = lambda w: pl.BlockSpec((tm, w), lambda i: (i, 0))
    full = lambda p: pl.BlockSpec(p.shape, lambda i: (0,) * p.ndim)
    return pl.pallas_call(
        body, name=name, grid=(t // tm,),
        in_specs=[row_spec(sum(w)) for _, w in ins] + [full(p) for p in params],
        out_specs=[row_spec(sum(w)) for w in outs] + [pl.BlockSpec((1, 1), lambda i: (0, 0))] * n_sums,
        out_shape=[jax.ShapeDtypeStruct((t, sum(w)), dt) for w, dt in zip(outs, dtypes)] + [jax.ShapeDtypeStruct((1, 1), f32)] * n_sums,
        compiler_params=_cp(("arbitrary",)),
    )(*[a for a, _ in ins], *params)


def _rows_bwd(name, fn, consts, rows, params, outs, cts, n_sums=0, add=None, tm=512, dtypes=None):
    t = (consts + rows)[0][0].shape[0]
    tm = min(tm, t)
    n_c, n_r, n_p, n_o = len(consts), len(rows), len(params), len(outs)
    has_add = add is not None
    dtypes = dtypes or [f32] * n_r

    def body(*refs):
        pos = 0
        c_refs = refs[pos:pos + n_c]; pos += n_c
        r_refs = refs[pos:pos + n_r]; pos += n_r
        p_refs = refs[pos:pos + n_p]; pos += n_p
        ct_refs = refs[pos:pos + n_o]; pos += n_o
        add_ref = refs[pos] if has_add else None
        pos += 1 if has_add else 0
        dr_refs = refs[pos:pos + n_r]; pos += n_r
        dp_refs = refs[pos:pos + n_p]; pos += n_p
        s_refs = refs[pos:pos + n_sums]
        cvals, rvals = [], []
        for r, (_, widths) in zip(c_refs, consts):
            cvals += _pieces(r, widths)
        for r, (_, widths) in zip(r_refs, rows):
            rvals += _pieces(r, widths)
        pvals = [p[...] for p in p_refs]
        ctv = []
        for r, widths in zip(ct_refs, outs):
            ctv += _pieces(r, widths)
        ctv += [jnp.ones((1, 1), f32)] * n_sums
        primal, vjp = jax.vjp(lambda *rp: tuple(fn(*cvals, *rp)), *rvals, *pvals)
        g = vjp(tuple(ctv))
        pos = 0
        for idx, (r, (_, widths)) in enumerate(zip(dr_refs, rows)):
            _store_pieces(r, widths, g[pos:pos + len(widths)], add_ref if idx == 0 else None)
            pos += len(widths)

        @pl.when(pl.program_id(0) == 0)
        def _():
            for acc in list(dp_refs) + list(s_refs):
                acc[...] = jnp.zeros_like(acc)

        for dp, v in zip(dp_refs, g[pos:]):
            dp[...] += v
        for s, v in zip(s_refs, primal[len(primal) - n_sums:]):
            s[...] += v

    row_spec = lambda w: pl.BlockSpec((tm, w), lambda i: (i, 0))
    full = lambda p: pl.BlockSpec(p.shape, lambda i: (0,) * p.ndim)
    args = [a for a, _ in consts + rows] + list(params) + list(cts) + ([add] if has_add else [])
    res = pl.pallas_call(
        body, name=name, grid=(t // tm,),
        in_specs=[row_spec(sum(w)) for _, w in consts + rows] + [full(p) for p in params]
        + [row_spec(sum(w)) for w in outs] + ([row_spec(add.shape[1])] if has_add else []),
        out_specs=[row_spec(sum(w)) for _, w in rows] + [full(p) for p in params]
        + [pl.BlockSpec((1, 1), lambda i: (0, 0))] * n_sums,
        out_shape=[jax.ShapeDtypeStruct((t, sum(w)), dt) for (_, w), dt in zip(rows, dtypes)]
        + [jax.ShapeDtypeStruct(p.shape, f32) for p in params] + [jax.ShapeDtypeStruct((1, 1), f32)] * n_sums,
        compiler_params=_cp(("arbitrary",)),
    )(*args)
    return res[:n_r], res[n_r:n_r + n_p] + res[n_r + n_p:]


def _matmul_then_vjp(name, a, b, mode, fn, rows, dtypes, params=(), first_cts=(), add=None, tm=256):
    m, k = a.shape
    tm = min(tm, m)
    dims = (((1,), (0,)), ((), ())) if mode == "nn" else (((1,), (1,)), ((), ()))
    n_r, n_p, n_c = len(rows), len(params), len(first_cts)
    has_add = add is not None

    def body(*refs):
        a_ref, b_ref = refs[:2]
        pos = 2
        r_refs = refs[pos:pos + n_r]; pos += n_r
        p_refs = refs[pos:pos + n_p]; pos += n_p
        c_refs = refs[pos:pos + n_c]; pos += n_c
        add_ref = refs[pos] if has_add else None
        pos += 1 if has_add else 0
        dr_refs = refs[pos:pos + n_r]; pos += n_r
        dp_refs = refs[pos:pos + n_p]
        ct = lax.dot_general(a_ref[...].astype(bf16), b_ref[...].astype(bf16), dims, preferred_element_type=f32)
        if has_add:
            ct = ct + add_ref[...]
        rvals = []
        for r, (_, widths) in zip(r_refs, rows):
            rvals += _pieces(r, widths)
        _, vjp = jax.vjp(lambda *rp: tuple(fn(*rp)), *rvals, *[p[...] for p in p_refs])
        g = vjp(tuple(c[...].astype(f32) for c in c_refs) + (ct,))
        pos = 0
        for r, (_, widths) in zip(dr_refs, rows):
            _store_pieces(r, widths, g[pos:pos + len(widths)])
            pos += len(widths)

        @pl.when(pl.program_id(0) == 0)
        def _():
            for dp in dp_refs:
                dp[...] = jnp.zeros_like(dp)

        for dp, v in zip(dp_refs, g[pos:]):
            dp[...] += v

    row_spec = lambda w: pl.BlockSpec((tm, w), lambda i: (i, 0))
    whole = lambda p: pl.BlockSpec(p.shape, lambda i: (0, 0))
    res = pl.pallas_call(
        body, name=name, grid=(m // tm,),
        in_specs=[row_spec(k), whole(b)] + [row_spec(sum(w)) for _, w in rows] + [whole(p) for p in params]
        + [row_spec(c.shape[1]) for c in first_cts] + ([row_spec(add.shape[1])] if has_add else []),
        out_specs=[row_spec(sum(w)) for _, w in rows] + [whole(p) for p in params],
        out_shape=[jax.ShapeDtypeStruct((m, sum(w)), dt) for (_, w), dt in zip(rows, dtypes)]
        + [jax.ShapeDtypeStruct(p.shape, f32) for p in params],
        compiler_params=_cp(("arbitrary",)),
    )(a, b, *[r for r, _ in rows], *params, *first_cts, *([add] if has_add else []))
    return res[:n_r], res[n_r:]


def _rms(x, g):
    return x * lax.rsqrt(jnp.mean(x * x, axis=-1, keepdims=True) + NORM_EPS) * g


def _fn_rms(x, g):
    return (_rms(x, g),)


def _fn_rwkv_pre(r, k, v, wd, ad, gd, w0, w_up, a0, a_up, g_up, k_k, k_a):
    nn, _, _ = _make_mm(False, False)
    w_log = -_sigmoid(w0 + nn(jnp.tanh(wd), w_up)) * 0.6065306597126334
    a = _sigmoid(a0 + nn(ad, a_up))
    g = nn(_sigmoid(gd), g_up)
    kk = k * k_k
    kk = kk * lax.rsqrt(jnp.maximum(_head_sum(kk * kk), 1e-24))
    k2 = k * (1.0 + (a - 1.0) * k_a)
    return r, k2, v, w_log, -kk, kk * a, g


def _fn_rwkv_post(y, r, k2, v, g, gn_g, gn_b, r_k):
    mean = _head_sum(y) * (1.0 / HD)
    yc = y - mean
    var = _head_sum(yc * yc) * (1.0 / HD)
    yn = yc * lax.rsqrt(var + GN_EPS) * gn_g + gn_b
    bonus = _head_sum(r * k2 * r_k) * v
    return ((yn + bonus) * g,)


def _fn_merge(a_fox, a_rwkv, a_mem, g_fox, g_rwkv, g_mem):
    return (_sigmoid(g_fox) * a_fox + _sigmoid(g_rwkv) * a_rwkv + _sigmoid(g_mem) * a_mem,)


def _fn_post1(y, x, post1_g, pre2_g):
    h1 = x + _rms(y, post1_g)
    return h1, _rms(h1, pre2_g)


def _fn_swiglu(gp, up):
    return (gp * _sigmoid(gp) * up,)


def _fn_final(target, ffn, h1, post2_g):
    err = h1 + _rms(ffn, post2_g) - target
    per_row = jnp.mean(err * err, axis=-1, keepdims=True)
    return (0.5 * jnp.sum(per_row, axis=0, keepdims=True),)


def _shift_down(x):
    row = lax.broadcasted_iota(jnp.int32, x.shape, 0)
    return jnp.where(row == 0, 0.0, pltpu.roll(x, 1, 0))


def _shift_up(x):
    s = x.shape[0]
    row = lax.broadcasted_iota(jnp.int32, x.shape, 0)
    return jnp.where(row == s - 1, 0.0, pltpu.roll(x, s - 1, 0))


def _tokshift_fwd(p, mu, batch, seq):
    w = p.shape[1]
    tc = _tile(w, 384)

    def body(p_ref, mu_ref, o_ref):
        x = p_ref[...].astype(f32)
        o_ref[...] = (x + (_shift_down(x) - x) * mu_ref[...]).astype(o_ref.dtype)

    return pl.pallas_call(
        body, name="tokshift_fwd", grid=(w // tc, batch),
        in_specs=[pl.BlockSpec((seq, tc), lambda j, b: (b, j)), pl.BlockSpec((1, tc), lambda j, b: (0, j))],
        out_specs=pl.BlockSpec((seq, tc), lambda j, b: (b, j)),
        out_shape=jax.ShapeDtypeStruct(p.shape, bf16),
        compiler_params=_cp(("parallel", "arbitrary")),
    )(p, mu)


def _tokshift_bwd(p, mu, dps, batch, seq):
    w = p.shape[1]
    tc = _tile(w, 384)

    def body(p_ref, mu_ref, d_ref, dp_ref, dmu_ref):
        x, mu_v, d = p_ref[...].astype(f32), mu_ref[...], d_ref[...].astype(f32)
        dp_ref[...] = (d * (1.0 - mu_v) + _shift_up(d * mu_v)).astype(dp_ref.dtype)

        @pl.when(pl.program_id(1) == 0)
        def _():
            dmu_ref[...] = jnp.zeros_like(dmu_ref)

        dmu_ref[...] += jnp.sum(d * (_shift_down(x) - x), axis=0, keepdims=True)

    return pl.pallas_call(
        body, name="tokshift_bwd", grid=(w // tc, batch),
        in_specs=[pl.BlockSpec((seq, tc), lambda j, b: (b, j)), pl.BlockSpec((1, tc), lambda j, b: (0, j)),
                  pl.BlockSpec((seq, tc), lambda j, b: (b, j))],
        out_specs=[pl.BlockSpec((seq, tc), lambda j, b: (b, j)), pl.BlockSpec((1, tc), lambda j, b: (0, j))],
        out_shape=[jax.ShapeDtypeStruct(p.shape, bf16), jax.ShapeDtypeStruct(mu.shape, f32)],
        compiler_params=_cp(("parallel", "arbitrary")),
    )(p, mu, dps)


def _cum_block(seq):
    return _tile(seq, 256)


def _fox_gate_fwd(f, bias, batch, seq):
    cb = _cum_block(seq)

    def body(f_ref, b_ref, c_ref):
        row = lax.broadcasted_iota(jnp.int32, (cb, cb), 0)
        col = lax.broadcasted_iota(jnp.int32, (cb, cb), 1)
        tri = (col <= row).astype(f32)
        carry = jnp.zeros((1, 128), f32)
        for i in range(seq // cb):
            z = f_ref[i * cb:(i + 1) * cb, :] + b_ref[...]
            ls = jnp.minimum(z, 0.0) - jnp.log(1.0 + jnp.exp(-jnp.abs(z)))
            c = _dg(tri, ls, (((1,), (0,)), ((), ())), "mask") + carry
            c_ref[i * cb:(i + 1) * cb, :] = c
            carry = c[cb - 1:cb, :]

    return pl.pallas_call(
        body, name="fox_gate_fwd", grid=(batch,),
        in_specs=[pl.BlockSpec((seq, 128), lambda b: (b, 0)), pl.BlockSpec((1, 128), lambda b: (0, 0))],
        out_specs=pl.BlockSpec((seq, 128), lambda b: (b, 0)),
        out_shape=jax.ShapeDtypeStruct(f.shape, f32),
        compiler_params=_cp(("arbitrary",)),
    )(f, bias)


def _fox_gate_bwd(f, bias, dc_a, dc_b, batch, seq):
    cb = _cum_block(seq)

    def body(f_ref, b_ref, da_ref, db_ref, df_ref, dbias_ref):
        row = lax.broadcasted_iota(jnp.int32, (cb, cb), 0)
        col = lax.broadcasted_iota(jnp.int32, (cb, cb), 1)
        triu = (col >= row).astype(f32)

        @pl.when(pl.program_id(0) == 0)
        def _():
            dbias_ref[...] = jnp.zeros_like(dbias_ref)

        lane = lax.broadcasted_iota(jnp.int32, (1, 128), 1)

        def by_head(blk):
            out = jnp.zeros((cb, 128), f32)
            for p in range(HEADS // 2):
                for e in range(2):
                    out = jnp.where(lane == 2 * p + e, _pick_lane(blk[:, p * 128:(p + 1) * 128], e), out)
            return out

        carry = jnp.zeros((1, 128), f32)
        tot = jnp.zeros((1, 128), f32)
        for i in reversed(range(seq // cb)):
            sl = slice(i * cb, (i + 1) * cb)
            dc = by_head(da_ref[sl, :] + db_ref[sl, :])
            dls = _dg(triu, dc, (((1,), (0,)), ((), ())), "mask") + carry
            carry = dls[0:1, :]
            df = dls * _sigmoid(-(f_ref[sl, :] + b_ref[...]))
            df_ref[sl, :] = df.astype(df_ref.dtype)
            tot = tot + jnp.sum(df, axis=0, keepdims=True)
        dbias_ref[...] += tot

    return pl.pallas_call(
        body, name="fox_gate_bwd", grid=(batch,),
        in_specs=[pl.BlockSpec((seq, 128), lambda b: (b, 0)), pl.BlockSpec((1, 128), lambda b: (0, 0)),
                  pl.BlockSpec((seq, HW), lambda b: (b, 0)), pl.BlockSpec((seq, HW), lambda b: (b, 0))],
        out_specs=[pl.BlockSpec((seq, 128), lambda b: (b, 0)), pl.BlockSpec((1, 128), lambda b: (0, 0))],
        out_shape=[jax.ShapeDtypeStruct(f.shape, bf16), jax.ShapeDtypeStruct((1, 128), f32)],
        compiler_params=_cp(("arbitrary",)),
    )(f, bias, dc_a, dc_b)


_HBM_SPEC = pl.BlockSpec(memory_space=pltpu.HBM)


def _side_out_shapes(srcs, per_peer):
    return [jax.ShapeDtypeStruct(((N_DEV,) + tuple(s.shape[1:] if per_peer else s.shape)), s.dtype) for s in srcs]


def _side_sems(n):
    if n == 0:
        return []
    return [pltpu.SemaphoreType.DMA((n, N_DEV - 1)), pltpu.SemaphoreType.DMA((n, N_DEV - 1)), pltpu.SemaphoreType.DMA((n,))]


def _peer_copies(src_refs, dst_refs, per_peer, sems):
    send_sems, recv_sems, local_sems = sems
    x, y, c = lax.axis_index("x"), lax.axis_index("y"), lax.axis_index("c")
    me = 4 * x + 2 * y + c

    def remote(src, dst, t, k, to):
        return pltpu.make_async_remote_copy(src_ref=src, dst_ref=dst, send_sem=send_sems.at[t, k - 1],
                                            recv_sem=recv_sems.at[t, k - 1], device_id=to,
                                            device_id_type=pl.DeviceIdType.MESH)

    direct, relays = [], []
    for t, (s, d) in enumerate(zip(src_refs, dst_refs)):
        direct.append((t, 0, pltpu.make_async_copy(s.at[me] if per_peer else s, d.at[me], local_sems.at[t])))
        for k in range(1, N_DEV):
            px = 1 - x if k & 4 else x
            py = 1 - y if k & 2 else y
            pc = 1 - c if k & 1 else c
            if per_peer:
                direct.append((t, k, remote(s.at[4 * px + 2 * py + pc], d.at[me], t, k, (px, py, pc))))
            elif k == 1 or not k & 1:
                direct.append((t, k, remote(s, d.at[me], t, k, (px, py, pc))))
            else:
                origin = d.at[4 * px + 2 * py + c]
                relays.append((t, k - 1, remote(origin, origin, t, k, (x, y, 1 - c))))
    return direct, relays


def _exchange_start(direct):
    for _, _, cp in direct:
        cp.start()


def _exchange_relay(direct, relays):
    landed = {(t, k): cp for t, k, cp in direct}
    for t, j, cp in relays:
        landed[(t, j)].wait_recv()
        cp.start()


def _exchange_finish(direct, relays):
    relayed = {(t, j) for t, j, _ in relays}
    for t, k, cp in direct:
        if k == 0:
            cp.wait()
        else:
            cp.wait_send()
            if (t, k) not in relayed:
                cp.wait_recv()
    for _, _, cp in relays:
        cp.wait()


def _side_exchange(src_refs, dst_refs, per_peer, sems, *grid):
    if not src_refs:
        return
    step, total = 0, 1
    for a, n in enumerate(grid):
        step, total = step * n + pl.program_id(a), total * n

    @pl.when(step == 0)
    def _():
        _exchange_start(_peer_copies(src_refs, dst_refs, per_peer, sems)[0])

    @pl.when(step == (3 * total) // 4)
    def _():
        _exchange_relay(*_peer_copies(src_refs, dst_refs, per_peer, sems))

    @pl.when(step == total - 1)
    def _():
        _exchange_finish(*_peer_copies(src_refs, dst_refs, per_peer, sems))


def _exchange(name, srcs, per_peer):
    n = len(srcs)

    def body(*refs):
        direct, relays = _peer_copies(refs[:n], refs[n:2 * n], per_peer, refs[2 * n:])
        _exchange_start(direct)
        _exchange_relay(direct, relays)
        _exchange_finish(direct, relays)

    return pl.pallas_call(
        body, name=name, in_specs=[_HBM_SPEC] * n, out_specs=[_HBM_SPEC] * n,
        out_shape=_side_out_shapes(srcs, per_peer), scratch_shapes=_side_sems(n),
    )(*srcs)


FOX_T = 512
_NEG = -1e30
_D2 = (((1,), (1,)), ((), ()))
_D1 = (((1,), (0,)), ((), ()))
_D0 = (((0,), (0,)), ((), ()))


def _bdot(a, b, dims):
    return lax.dot_general(a.astype(bf16), b.astype(bf16), dims, preferred_element_type=f32)


def _pick_lane(x, lane):
    idx = lax.broadcasted_iota(jnp.int32, x.shape, 1)
    return jnp.sum(jnp.where(idx == lane, x, 0.0), axis=1, keepdims=True)


def _pick_row(x, row):
    idx = lax.broadcasted_iota(jnp.int32, x.shape, 0)
    return jnp.sum(jnp.where(idx == row, x, 0.0), axis=0, keepdims=True)


def _fox_fwd(qkv, c, c_rows, batch, seq, side=None):
    t = min(FOX_T, seq)
    nq = seq // t
    scale = HD ** -0.5
    srcs, per_peer = side if side is not None else ([], False)
    n_s = len(srcs)

    def body(*refs):
        q_ref, k_ref, v_ref, cq_ref, ck_ref = refs[:5]
        o_ref, lse_ref = refs[5 + n_s:7 + n_s]
        _side_exchange(refs[5:5 + n_s], refs[7 + n_s:7 + 2 * n_s], per_peer, refs[7 + 2 * n_s:], batch, PAIRS, nq)
        pair, i = pl.program_id(1), pl.program_id(2)
        lane = lax.broadcasted_iota(jnp.int32, (1, PAIR_W), 1)
        first = (lane // HD) == 0
        mine = [first, jnp.logical_not(first)]
        q = q_ref[...] * scale
        qs = [jnp.where(mine[e], q, 0.0) for e in range(2)]
        cqs = [_pick_lane(cq_ref[...], 2 * pair + e) for e in range(2)]
        causal = lax.broadcasted_iota(jnp.int32, (t, t), 1) <= lax.broadcasted_iota(jnp.int32, (t, t), 0)

        def block(j, carry, diagonal):
            rows = pl.ds(pl.multiple_of(j * t, t), t)
            kj, vj = k_ref[rows, :], v_ref[rows, :]
            ck_blk = ck_ref[0, :, rows]
            out = []
            for e in range(2):
                m, acc = carry[2 * e:2 * e + 2]
                s = _bdot(qs[e], kj, _D2) + cqs[e] - _pick_row(ck_blk, 2 * pair + e)
                if diagonal:
                    s = jnp.where(causal, s, _NEG)
                m_new = jnp.maximum(m, jnp.max(s, axis=1, keepdims=True))
                p = jnp.exp(s - m_new)
                out += [m_new, jnp.exp(m - m_new) * acc + _bdot(p, jnp.where(mine[e], vj, 1.0), _D1)]
            return tuple(out)

        init = (jnp.full((t, 1), _NEG, f32), jnp.zeros((t, PAIR_W), f32)) * 2
        carry = lax.fori_loop(0, i, lambda j, cr: block(j, cr, False), init)
        m0, a0, m1, a1 = block(i, carry, True)
        l0, l1 = _pick_lane(a0, HD), _pick_lane(a1, 0)
        o_ref[...] = jnp.where(first, a0 / l0, a1 / l1)
        lse_ref[...] = jnp.where(lane == 0, m0 + jnp.log(l0), jnp.where(lane == 1, m1 + jnp.log(l1), 0.0))

    q_spec = pl.BlockSpec((t, PAIR_W), lambda b, p, i: (b * nq + i, p))
    res = pl.pallas_call(
        body, name="fox_attn_fwd", grid=(batch, PAIRS, nq),
        in_specs=[q_spec,
                  pl.BlockSpec((seq, PAIR_W), lambda b, p, i: (b, PAIRS + p)),
                  pl.BlockSpec((seq, PAIR_W), lambda b, p, i: (b, 2 * PAIRS + p)),
                  pl.BlockSpec((t, 128), lambda b, p, i: (b * nq + i, 0)),
                  pl.BlockSpec((1, 8, seq), lambda b, p, i: (b, 0, 0))] + [_HBM_SPEC] * n_s,
        out_specs=[q_spec, q_spec] + [_HBM_SPEC] * n_s,
        out_shape=[jax.ShapeDtypeStruct((batch * seq, HW), f32)] * 2 + _side_out_shapes(srcs, per_peer),
        scratch_shapes=_side_sems(n_s),
        compiler_params=_cp(("arbitrary", "arbitrary", "arbitrary")),
    )(qkv, qkv, qkv, c, c_rows, *srcs)
    return res[0], res[1], list(res[2:])


def _fox_bwd(qkv, c, c_rows, o, lse, do, batch, seq):
    t = min(FOX_T, seq)
    nq = seq // t
    scale = HD ** -0.5

    def body(q_ref, k_ref, v_ref, cq_ref, ck_ref, o_ref, lse_ref, do_ref,
             dq_ref, dk_ref, dv_ref, dcq_ref, dck_ref, acc0, acc1):
        pair, i = pl.program_id(1), pl.program_id(2)
        accs = [acc0, acc1]

        @pl.when(i == 0)
        def _():
            dv_ref[...] = jnp.zeros_like(dv_ref)
            acc0[...] = jnp.zeros_like(acc0)
            acc1[...] = jnp.zeros_like(acc1)

        lane = lax.broadcasted_iota(jnp.int32, (1, PAIR_W), 1)
        first = (lane // HD) == 0
        mine = [first, jnp.logical_not(first)]
        q, d_o, o_i = q_ref[...] * scale, do_ref[...], o_ref[...]
        q0s = [jnp.where(mine[e], q, 0.0) for e in range(2)]
        q1s = [jnp.where(mine[e], q, 1.0) for e in range(2)]
        dos = [jnp.where(mine[e], d_o, 0.0) for e in range(2)]
        deltas = [jnp.sum(dos[e] * o_i, axis=1, keepdims=True) for e in range(2)]
        lses = [_pick_lane(lse_ref[...], e) for e in range(2)]
        cqs = [_pick_lane(cq_ref[...], 2 * pair + e) for e in range(2)]
        causal = lax.broadcasted_iota(jnp.int32, (t, t), 1) <= lax.broadcasted_iota(jnp.int32, (t, t), 0)

        def block(j, dqs, diagonal):
            rows = pl.ds(pl.multiple_of(j * t, t), t)
            kj, vj = k_ref[rows, :], v_ref[rows, :]
            ck_blk = ck_ref[0, :, rows]
            out = []
            for e in range(2):
                s = _bdot(q0s[e], kj, _D2) + cqs[e] - _pick_row(ck_blk, 2 * pair + e)
                if diagonal:
                    s = jnp.where(causal, s, _NEG)
                p = jnp.exp(s - lses[e])
                ds = p * (_bdot(dos[e], vj, _D2) - deltas[e])
                dv_ref[rows, :] += _bdot(p, dos[e], _D0)
                accs[e][rows, :] += _bdot(ds, q1s[e], _D0)
                out.append(dqs[e] + _bdot(ds, jnp.where(mine[e], kj, 1.0), _D1))
            return tuple(out)

        zero = jnp.zeros((t, PAIR_W), f32)
        dqs = lax.fori_loop(0, i, lambda j, cr: block(j, cr, False), (zero, zero))
        dq0, dq1 = block(i, dqs, True)
        dq_ref[...] = jnp.where(first, dq0, dq1) * scale
        dcq_ref[...] = jnp.where(lane == 0, _pick_lane(dq0, HD), jnp.where(lane == 1, _pick_lane(dq1, 0), 0.0))

        @pl.when(i == nq - 1)
        def _():
            a0, a1 = acc0[...], acc1[...]
            dk_ref[...] = jnp.where(first, a0, a1)
            dck_ref[...] = jnp.where(lane == 0, -_pick_lane(a0, HD), jnp.where(lane == 1, -_pick_lane(a1, 0), 0.0))

    blk = lambda col: pl.BlockSpec((t, PAIR_W), lambda b, p, i: (b * nq + i, col * PAIRS + p))
    whole = lambda col: pl.BlockSpec((seq, PAIR_W), lambda b, p, i: (b, col * PAIRS + p))
    t_all = batch * seq
    return pl.pallas_call(
        body, name="fox_attn_bwd", grid=(batch, PAIRS, nq),
        in_specs=[blk(0), whole(1), whole(2),
                  pl.BlockSpec((t, 128), lambda b, p, i: (b * nq + i, 0)),
                  pl.BlockSpec((1, 8, seq), lambda b, p, i: (b, 0, 0)),
                  blk(0), blk(0), blk(0)],
        out_specs=[blk(0), whole(0), whole(0), blk(0), whole(0)],
        out_shape=[jax.ShapeDtypeStruct((t_all, HW), f32)] * 5,
        scratch_shapes=[pltpu.VMEM((seq, PAIR_W), f32), pltpu.VMEM((seq, PAIR_W), f32)],
        compiler_params=_cp(("parallel", "parallel", "arbitrary")),
    )(qkv, qkv, qkv, c, c_rows, o, lse, do)


MEM_TQ = 1024


def _mem_block(q, km, vm):
    nn, nt, _ = _make_mm(False, False)
    logits = nt(q, km) * (MEM_HD ** -0.5)
    m = lax.stop_gradient(jnp.max(logits, axis=-1, keepdims=True))
    e = jnp.exp(logits - m)
    return nn(e / jnp.sum(e, axis=-1, keepdims=True), vm)


def _mem_specs(seq, tq):
    nq = seq // tq
    qs = pl.BlockSpec((tq, MEM_HD), lambda b, h, i: (b * nq + i, h))
    ks = pl.BlockSpec((MEM_LEN, MEM_HD), lambda b, h, i: (b, h))
    vs = pl.BlockSpec((MEM_LEN, MEM_HD), lambda b, h, i: (b, MEM_HEADS + h))
    return nq, qs, ks, vs


def _mem_fwd(q, mem_kv, batch, seq):
    tq = min(MEM_TQ, seq)
    nq, qs, ks, vs = _mem_specs(seq, tq)

    def body(q_ref, k_ref, v_ref, o_ref):
        o_ref[...] = _mem_block(q_ref[...].astype(f32), k_ref[...], v_ref[...]).astype(o_ref.dtype)

    return pl.pallas_call(
        body, name="mem_attn_fwd", grid=(batch, MEM_HEADS, nq),
        in_specs=[qs, ks, vs], out_specs=qs, out_shape=jax.ShapeDtypeStruct(q.shape, bf16),
        compiler_params=_cp(("parallel", "parallel", "arbitrary")),
    )(q, mem_kv, mem_kv)


def _mem_bwd(q, mem_kv, do, batch, seq):
    tq = min(MEM_TQ, seq)
    nq, qs, ks, vs = _mem_specs(seq, tq)

    def body(q_ref, k_ref, v_ref, do_ref, dq_ref, dk_ref, dv_ref):
        _, vjp = jax.vjp(_mem_block, q_ref[...].astype(f32), k_ref[...], v_ref[...])
        dq, dk, dv = vjp(do_ref[...])
        dq_ref[...] = dq.astype(dq_ref.dtype)

        @pl.when(pl.program_id(2) == 0)
        def _():
            dk_ref[...] = jnp.zeros_like(dk_ref)
            dv_ref[...] = jnp.zeros_like(dv_ref)

        dk_ref[...] += dk
        dv_ref[...] += dv

    return pl.pallas_call(
        body, name="mem_attn_bwd", grid=(batch, MEM_HEADS, nq),
        in_specs=[qs, ks, vs, qs], out_specs=[qs, ks, ks],
        out_shape=[jax.ShapeDtypeStruct(q.shape, bf16), jax.ShapeDtypeStruct((batch * MEM_LEN, MEM_W), f32),
                   jax.ShapeDtypeStruct((batch * MEM_LEN, MEM_W), f32)],
        compiler_params=_cp(("parallel", "parallel", "arbitrary")),
    )(q, mem_kv, mem_kv, do)


@jax.custom_vjp
def _halves(x):
    c = x.shape[1] // 2
    return x[:, :c], x[:, c:]


_halves.defvjp(lambda x: ((x[:, :x.shape[1] // 2], x[:, x.shape[1] // 2:]), None),
               lambda _, g: (jnp.concatenate(g, axis=1),))


@jax.custom_vjp
def _lead_halves(x):
    n = x.shape[0] // 2
    return x[:n], x[n:]


_lead_halves.defvjp(lambda x: ((x[:x.shape[0] // 2], x[x.shape[0] // 2:]), None),
                    lambda _, g: (jnp.concatenate(g, axis=0),))


def _scan_chunk(s0, r, wl, k, v, a, b):
    nn, nt, tn = _make_mm(True, False)
    nn_exact, _, _ = _make_mm(True, "mask")
    _, nt_exact, _ = _make_mm(True, "split")
    hp, c, lanes = r.shape
    row = lax.broadcasted_iota(jnp.int32, (c, c), 0)
    col = lax.broadcasted_iota(jnp.int32, (c, c), 1)
    first = (lax.broadcasted_iota(jnp.int32, (1, 1, lanes), 2) // HD) == 0
    tri = jnp.broadcast_to((col <= row).astype(f32)[None], (hp, c, c))
    lg = nn_exact(tri, wl)
    lg_end = lg[:, c - 1:c, :]
    grow, shrink, to_end = jnp.exp(lg), jnp.exp(-lg), jnp.exp(lg_end - lg)
    rt, kt, bt, at = r * grow, k * shrink, b * shrink, a * jnp.exp(lg - wl)
    strict, incl = (col < row)[None], (col <= row)[None]
    twice = lambda t: jnp.concatenate([t, t], axis=0)
    queries = jnp.concatenate([at, rt], axis=1)
    per_head = jnp.concatenate([jnp.where(first, queries, 0.0), jnp.where(first, 0.0, queries)], axis=0)
    (ab, rb), (ak, rk) = _halves(nt_exact(per_head, twice(bt))), _halves(nt_exact(per_head, twice(kt)))
    l_ab = jnp.where(strict, ab, 0.0)
    a_ak = jnp.where(strict, ak, 0.0)
    a_rb = jnp.where(incl, rb, 0.0)
    a_rk = jnp.where(incl, rk, 0.0)
    inv = (col == row).astype(f32)[None] + l_ab
    power, n = l_ab, 1
    while 2 * n < c:
        power = nn(power, power)
        inv = inv + nn(inv, power)
        n *= 2

    def apply(m, t):
        lo, hi = _lead_halves(nn(m, twice(t)))
        return jnp.where(first, lo, hi)

    sa = apply(inv, nt(at, s0) + apply(a_ak, v))
    y = nt(rt, s0) + apply(a_rk, v) + apply(a_rb, sa)
    same_head = ((lax.broadcasted_iota(jnp.int32, (lanes, lanes), 0) // HD)
                 == (lax.broadcasted_iota(jnp.int32, (lanes, lanes), 1) // HD))[None]
    s1 = s0 * jnp.exp(lg_end) + jnp.where(same_head, tn(v, k * to_end) + tn(sa, b * to_end), 0.0)
    return y, s1


PAIRS = HEADS // 2
PAIR_W = 2 * HD
SCAN_ARGS = (0, 3, 1, 2, 4, 5)


def _pair_stack(ref, off):
    return jnp.stack([ref[b, :, off + p * PAIR_W:off + (p + 1) * PAIR_W]
                      for b in range(ref.shape[0]) for p in range(PAIRS)])


def _pair_store(ref, off, val, add_ref=None):
    for b in range(ref.shape[0]):
        for p in range(PAIRS):
            sl = slice(off + p * PAIR_W, off + (p + 1) * PAIR_W)
            v = val[b * PAIRS + p]
            ref[b, :, sl] = v if add_ref is None else v + add_ref[b, :, sl]


def _scan_fwd(main6, batch, seq, side=None):
    c = min(SCAN_CHUNK, seq)
    nc = seq // c
    hp = batch * PAIRS
    srcs, per_peer = side if side is not None else ([], False)
    n_s = len(srcs)

    def body(*refs):
        z_ref, y_ref, s_ref, st = refs[0], refs[1 + n_s], refs[2 + n_s], refs[3 + 2 * n_s]
        _side_exchange(refs[1:1 + n_s], refs[3 + n_s:3 + 2 * n_s], per_peer, refs[4 + 2 * n_s:], nc)

        @pl.when(pl.program_id(0) == 0)
        def _():
            st[...] = jnp.zeros_like(st)

        s0 = st[...]
        s_ref[0] = s0
        y, s1 = _scan_chunk(s0, *[_pair_stack(z_ref, comp * HW) for comp in SCAN_ARGS])
        _pair_store(y_ref, 0, y)
        st[...] = s1

    res = pl.pallas_call(
        body, name="rwkv_scan_fwd", grid=(nc,),
        in_specs=[pl.BlockSpec((batch, c, 6 * HW), lambda i: (0, i, 0))] + [_HBM_SPEC] * n_s,
        out_specs=[pl.BlockSpec((batch, c, HW), lambda i: (0, i, 0)),
                   pl.BlockSpec((1, hp, PAIR_W, PAIR_W), lambda i: (i, 0, 0, 0))] + [_HBM_SPEC] * n_s,
        out_shape=[jax.ShapeDtypeStruct((batch, seq, HW), f32), jax.ShapeDtypeStruct((nc, hp, PAIR_W, PAIR_W), f32)]
        + _side_out_shapes(srcs, per_peer),
        scratch_shapes=[pltpu.VMEM((hp, PAIR_W, PAIR_W), f32)] + _side_sems(n_s),
        compiler_params=_cp(("arbitrary",)),
    )(main6.reshape(batch, seq, 6 * HW), *srcs)
    return res[0].reshape(batch * seq, HW), res[1], list(res[2:])


def _scan_bwd(main6, states, dy, extra, batch, seq, side=None):
    c = min(SCAN_CHUNK, seq)
    nc = seq // c
    hp = batch * PAIRS
    srcs, per_peer = side if side is not None else ([], False)
    n_s = len(srcs)

    def body(*refs):
        z_ref, s_ref, dy_ref, ex_ref = refs[:4]
        dz_ref, dst = refs[4 + n_s], refs[5 + 2 * n_s]
        _side_exchange(refs[4:4 + n_s], refs[5 + n_s:5 + 2 * n_s], per_peer, refs[6 + 2 * n_s:], nc)

        @pl.when(pl.program_id(0) == 0)
        def _():
            dst[...] = jnp.zeros_like(dst)

        _, vjp = jax.vjp(_scan_chunk, s_ref[0], *[_pair_stack(z_ref, comp * HW) for comp in SCAN_ARGS])
        g = vjp((_pair_stack(dy_ref, 0), dst[...]))
        dst[...] = g[0]
        for arg, comp in enumerate(SCAN_ARGS):
            _pair_store(dz_ref, comp * HW, g[1 + arg], ex_ref if comp < 3 else None)

    back = lambda i: (0, nc - 1 - i, 0)
    wide = pl.BlockSpec((batch, c, 6 * HW), back)
    res = pl.pallas_call(
        body, name="rwkv_scan_bwd", grid=(nc,),
        in_specs=[wide, pl.BlockSpec((1, hp, PAIR_W, PAIR_W), lambda i: (nc - 1 - i, 0, 0, 0)),
                  pl.BlockSpec((batch, c, HW), back), pl.BlockSpec((batch, c, 3 * HW), back)] + [_HBM_SPEC] * n_s,
        out_specs=[wide] + [_HBM_SPEC] * n_s,
        out_shape=[jax.ShapeDtypeStruct((batch, seq, 6 * HW), f32)] + _side_out_shapes(srcs, per_peer),
        scratch_shapes=[pltpu.VMEM((hp, PAIR_W, PAIR_W), f32)] + _side_sems(n_s),
        compiler_params=_cp(("arbitrary",)),
    )(main6.reshape(batch, seq, 6 * HW), states, dy.reshape(batch, seq, HW), extra.reshape(batch, seq, 3 * HW), *srcs)
    return res[0].reshape(batch * seq, 6 * HW), list(res[1:])


def _pad_cols(x, width):
    return jnp.pad(x, ((0, 0), (0, width - x.shape[1])))


def _split_w_in(wt):
    z = lambda rows: jnp.zeros((rows, wt.shape[1]), wt.dtype)
    w_r = jnp.concatenate([wt[1544:3080], wt[3080:3144], z(64), wt[3144:3208], z(64), wt[3208:3336]], axis=0)
    return wt[:1536], jnp.concatenate([wt[1536:1544], z(120)], axis=0), w_r, wt[3336:3848], wt[3848:]


def _merge_w_in(g_qkv, g_f, g_r, g_mq, g_g):
    return jnp.concatenate([g_qkv, g_f[:8], g_r[:1536], g_r[1536:1600], g_r[1664:1728], g_r[1792:], g_mq, g_g], axis=0)


def _pad_lora(v):
    z64 = jnp.zeros((1, 64), v.dtype)
    return jnp.concatenate([v[:, :1536], v[:, 1536:1600], z64, v[:, 1600:1664], z64, v[:, 1664:]], axis=1)


def _unpad_lora(v):
    return jnp.concatenate([v[:, :1536], v[:, 1536:1600], v[:, 1664:1728], v[:, 1792:]], axis=1)


def _local_step(x, mem, target, w, p, late=None, early=None, last=None):
    batch, seq, _ = x.shape
    t = batch * seq
    x2, tg2, mem2 = x.reshape(t, D), target.reshape(t, D), mem.reshape(batch * MEM_LEN, D)
    w_qkv, w_f, w_r, w_mq, w_g3 = _split_w_in(w["w_in"])
    mu = _pad_lora(p["rwkv_mu"])
    bias = _pad_cols(p["fox_f_bias"], 128)
    r_k = p["rwkv_r_k"].reshape(1, HW)
    post_params = [p["rwkv_gn_g"], p["rwkv_gn_b"], r_k]
    rw_widths = [HW, HW, HW, LORA_PAD, LORA_PAD, LORA_PAD]
    six = [HW] * 6

    p_g, u = _matmul("proj_gate", _lazy(_fn_rms, [(x2, [D])], D, params=[p["pre1_g"]]), w_g3, "nt", out_dtype=bf16)
    p_qkv = _matmul("proj_qkv", u, w_qkv, "nt", out_dtype=bf16)
    p_f = _matmul("proj_f", u, w_f, "nt")
    p_r = _matmul("proj_rwkv", u, w_r, "nt", out_dtype=bf16)
    p_mq = _matmul("proj_memq", u, w_mq, "nt", out_dtype=bf16)

    c = _fox_gate_fwd(p_f, bias, batch, seq)
    c_rows = c[:, :HEADS].reshape(batch, seq, HEADS).transpose(0, 2, 1)
    fox_o, lse, gathered = _fox_fwd(p_qkv, c, c_rows, batch, seq, side=(late[0], False) if late else None)
    if late:
        w = {**w, **late[2](gathered, 0)}
    fox_out = fox_o.astype(bf16)

    w_up = jnp.pad(w["rwkv_w_up"].astype(f32), ((0, LORA_PAD - 64), (0, 0)))
    a_up = jnp.pad(w["rwkv_a_up"].astype(f32), ((0, LORA_PAD - 64), (0, 0)))
    pre_params = [p["rwkv_w0"], w_up, p["rwkv_a0"], a_up, w["rwkv_g_up"].astype(f32), p["rwkv_k_k"], p["rwkv_k_a"]]
    ps = _tokshift_fwd(p_r, mu, batch, seq)
    main6, g_rw = _rows_fwd("rwkv_pre", _fn_rwkv_pre, [], [(ps, rw_widths)], pre_params, [six, [HW]], tm=256)
    y_rw, states, gathered = _scan_fwd(main6, batch, seq, side=(late[1], False) if late else None)
    if late:
        w = {**w, **late[2](gathered, 1)}
    post_consts = []
    post_rows = [(y_rw, [HW]), (main6, [HW, HW, HW]), (g_rw, [HW])]
    fn_post = _fn_rwkv_post

    (rwkv_out,) = _rows_fwd("rwkv_post", fn_post, post_consts, post_rows, post_params, [[HW]], dtypes=[bf16], tm=256)

    mem_kv, memn = _matmul("proj_memkv", _lazy(_fn_rms, [(mem2, [D])], D, params=[p["mem_norm_g"]]), w["w_mem_kv"], "nn")
    mem_out = _mem_fwd(p_mq, mem_kv, batch, seq)

    a_fox = _matmul("out_fox", fox_out, w["w_fox_out"], "nn", out_dtype=bf16)
    a_rwkv = _matmul("out_rwkv", rwkv_out, w["w_rwkv_out"], "nn", out_dtype=bf16)
    a_mem = _matmul("out_mem", mem_out, w["w_mem_out"], "nn", out_dtype=bf16)
    merge_rows = [(a_fox, [D]), (a_rwkv, [D]), (a_mem, [D]), (p_g, [D, D, D])]
    yy, merged = _matmul("out_o", _lazy(_fn_merge, merge_rows, D), w["w_o"], "nn")
    post1_rows = [(yy, [D]), (x2, [D])]
    post1_params = [p["post1_g"], p["pre2_g"]]
    h1, u2 = _rows_fwd("post1", _fn_post1, [], post1_rows, post1_params, [[D], [D]], dtypes=[f32, bf16])
    gp = _matmul("ffn_gate", u2, w["w_ffn_gate"], "nt", out_dtype=bf16)
    up = _matmul("ffn_up", u2, w["w_ffn_up"], "nt", out_dtype=bf16)
    ffn, hmid = _matmul("ffn_down", _lazy(_fn_swiglu, [(gp, [D_FF]), (up, [D_FF])], D_FF), w["w_ffn_down"], "nn")
    final_rows = [(ffn, [D]), (h1, [D])]

    gw, gp_ = {}, {}
    (d_ffn, d_h1), (gp_["post2_g"], loss) = _rows_bwd("final", _fn_final, [(tg2, [D])], final_rows, [p["post2_g"]], [], [],
                                                      n_sums=1, dtypes=[bf16, f32])
    gw["w_ffn_down"] = _matmul("ffn_down_dw", hmid, d_ffn, "tn", out_dtype=bf16)
    (d_gp, d_up), _ = _matmul_then_vjp("ffn_down_dx", d_ffn, w["w_ffn_down"], "nt", _fn_swiglu,
                                       [(gp, [D_FF]), (up, [D_FF])], [bf16, bf16])
    gw["w_ffn_gate"] = _matmul("ffn_gate_dw", d_gp, u2, "tn", out_dtype=bf16)
    gw["w_ffn_up"] = _matmul("ffn_up_dw", d_up, u2, "tn", out_dtype=bf16)
    d_u2_gate = _matmul("ffn_gate_dx", d_gp, w["w_ffn_gate"], "nn")
    (d_yy, d_x_res), (gp_["post1_g"], gp_["pre2_g"]) = _matmul_then_vjp(
        "ffn_up_dx", d_up, w["w_ffn_up"], "nn", _fn_post1, post1_rows, [bf16, f32], params=post1_params,
        first_cts=[d_h1], add=d_u2_gate)
    gw["w_o"] = _matmul("out_o_dw", merged, d_yy, "tn", out_dtype=bf16)
    (d_a_fox, d_a_rwkv, d_a_mem, d_p_g), _ = _matmul_then_vjp("out_o_dx", d_yy, w["w_o"], "nt", _fn_merge, merge_rows,
                                                             [bf16] * 4)
    d_fox_out = _matmul("out_fox_dx", d_a_fox, w["w_fox_out"], "nt")
    gw["w_fox_out"] = _matmul("out_fox_dw", fox_out, d_a_fox, "tn", out_dtype=bf16)
    gw["w_rwkv_out"] = _matmul("out_rwkv_dw", rwkv_out, d_a_rwkv, "tn", out_dtype=bf16)
    d_mem_out = _matmul("out_mem_dx", d_a_mem, w["w_mem_out"], "nt")
    gw["w_mem_out"] = _matmul("out_mem_dw", mem_out, d_a_mem, "tn", out_dtype=bf16)

    d_p_mq, d_km, d_vm = _mem_bwd(p_mq, mem_kv, d_mem_out, batch, seq)
    d_mem_kv = jnp.concatenate([d_km, d_vm], axis=1).astype(bf16)
    gw["w_mem_kv"] = _matmul("proj_memkv_dw", memn, d_mem_kv, "tn", out_dtype=bf16)
    d_memn = _matmul("proj_memkv_dx", d_mem_kv, w["w_mem_kv"], "nt")
    _, (gp_["mem_norm_g"],) = _rows_bwd("rms_mem_bwd", _fn_rms, [], [(mem2, [D])], [p["mem_norm_g"]], [[D]], [d_memn])

    d_q, d_k, d_v, d_cq, d_ck = _fox_bwd(p_qkv, c, c_rows, fox_o, lse, d_fox_out, batch, seq)
    d_p_qkv = jnp.concatenate([d_q, d_k, d_v], axis=1).astype(bf16)
    d_p_f, d_bias = _fox_gate_bwd(p_f, bias, d_cq, d_ck, batch, seq)
    gp_["fox_f_bias"] = d_bias[:, :HEADS]

    (d_y_rw, d_main6_post, d_g_rw), (gp_["rwkv_gn_g"], gp_["rwkv_gn_b"], d_rk) = _matmul_then_vjp(
        "out_rwkv_dx", d_a_rwkv, w["w_rwkv_out"], "nt", fn_post, post_rows, [f32] * 3, params=post_params)
    gp_["rwkv_r_k"] = d_rk.reshape(1, HEADS, HD)
    d_main6, early_got = _scan_bwd(main6, states, d_y_rw, d_main6_post, batch, seq,
                                   side=(early(gw), True) if early else None)

    def fn_pre_sum(*args):
        return _fn_rwkv_pre(*args)

    (d_ps,), d_pre = _rows_bwd("rwkv_pre_bwd", fn_pre_sum, [], [(ps, rw_widths)], pre_params, [six, [HW]],
                               [d_main6, d_g_rw], tm=256, dtypes=[bf16])
    gp_["rwkv_w0"], d_w_up, gp_["rwkv_a0"], d_a_up, gw["rwkv_g_up"], gp_["rwkv_k_k"], gp_["rwkv_k_a"] = d_pre
    gw["rwkv_w_up"], gw["rwkv_a_up"] = d_w_up[:64], d_a_up[:64]
    d_p_r, d_mu = _tokshift_bwd(p_r, mu, d_ps, batch, seq)
    gp_["rwkv_mu"] = _unpad_lora(d_mu)

    gw["w_in"] = _merge_w_in(_matmul("proj_qkv_dw", d_p_qkv, u, "tn", out_dtype=bf16), _matmul("proj_f_dw", d_p_f, u, "tn", out_dtype=bf16),
                             _matmul("proj_rwkv_dw", d_p_r, u, "tn", out_dtype=bf16), _matmul("proj_memq_dw", d_p_mq, u, "tn", out_dtype=bf16),
                             _matmul("proj_gate_dw", d_p_g, u, "tn", out_dtype=bf16))
    d_x, gp_["pre1_g"], last_got = _input_cotangent(
        "proj_dx", [d_p_qkv, d_p_f, d_p_r, d_p_mq, d_p_g], [w_qkv, w_f, w_r, w_mq, w_g3], x2, p["pre1_g"], d_x_res,
        side=(last(gw), True) if last else None)
    return loss, d_x.reshape(x.shape), gw, gp_, early_got, last_got


def _adamw(name, recv, row_off, w, m, v):
    _, rows, cols = w.shape
    row_tiles = [t for t in range(16, min(rows, 128) + 1, 16) if rows % t == 0 and row_off % t == 0]
    if row_tiles:
        tr, tc = max(row_tiles), cols
        first, grid = row_off // tr, (rows // tr,)
        at = lambda i: (0, first + i, 0)
        mine = lambda i: (0, i, 0)
    else:
        assert row_off == 0 and recv.shape[1] == rows
        tr, tc = rows, 128
        grid = (cols // tc,)
        at = mine = lambda i: (0, 0, i)

    def body(g_ref, w_ref, m_ref, v_ref, go_ref, d_ref, mo_ref, vo_ref):
        g = g_ref[0].astype(f32)
        for s in range(1, N_DEV):
            g = g + g_ref[s].astype(f32)
        m_new = ADAM_B1 * m_ref[0] + (1.0 - ADAM_B1) * g
        v_new = ADAM_B2 * v_ref[0] + (1.0 - ADAM_B2) * (g * g)
        m_hat = m_new / (1.0 - ADAM_B1 ** ADAM_STEP)
        v_hat = v_new / (1.0 - ADAM_B2 ** ADAM_STEP)
        go_ref[0] = g
        d_ref[0] = -ADAM_LR * (m_hat / (jnp.sqrt(v_hat) + ADAM_EPS) + ADAM_WD * w_ref[0])
        mo_ref[0] = m_new
        vo_ref[0] = v_new

    spec = pl.BlockSpec((1, tr, tc), mine)
    return pl.pallas_call(
        body, name=name, grid=grid,
        in_specs=[pl.BlockSpec((N_DEV, tr, tc), at), spec, spec, spec],
        out_specs=[spec] * 4, out_shape=[jax.ShapeDtypeStruct(w.shape, f32)] * 4,
        compiler_params=_cp(("parallel",)),
    )(recv, w, m, v)


GROUPS = (
    ("in", ("w_in",), 0),
    ("memkv", ("w_mem_kv",), 0),
    ("ffn_gu", ("w_ffn_gate", "w_ffn_up"), 0),
    ("down_o", ("w_ffn_down", "w_o"), 0),
    ("outs", ("w_fox_out", "w_rwkv_out", "w_mem_out"), 0),
    ("lora", ("rwkv_w_up", "rwkv_a_up", "rwkv_g_up"), 0),
)
FIRST_GROUPS = ("in", "memkv")
LATE_GROUPS = (("down_o", "outs", "lora"), ("ffn_gu",))
EARLY_GRAD_GROUPS = ("memkv", "ffn_gu", "down_o", "outs")
LAST_GRAD_GROUPS = ("in", "lora")
SHARD_AXIS = {n: a for n, _, a in SHARDED}
SMALL_ROWS = 16
LOSS_LANES = 128


def _group_local(shards, members, join):
    parts = [shards[n].reshape(shards[n].shape[-2:]) for n in members]
    return parts[0] if len(parts) == 1 else jnp.concatenate(parts, axis=join)


def _group_split(arr, members, join, lead=False):
    out, off = {}, 0
    for n in members:
        shape = dict((k, s) for k, s, _ in SHARDED)[n]
        size = _block_shape(shape, SHARD_AXIS[n])[join]
        idx = [slice(None)] * arr.ndim
        idx[arr.ndim - 2 + join] = slice(off, off + size)
        out[n] = arr[tuple(idx)]
        off += size
    return out


def _full_from_blocks(blocks, axis):
    if axis == 0:
        return blocks.reshape(-1, blocks.shape[2])
    return blocks.transpose(1, 0, 2).reshape(blocks.shape[1], -1)


def _blocks_from_full(full, axis):
    if axis == 0:
        return full.reshape(N_DEV, -1, full.shape[1])
    return full.reshape(full.shape[0], N_DEV, -1).transpose(1, 0, 2)


def _assemble(gathered, names):
    out = {}
    for arr, g in zip(gathered, names):
        _, members, join = [grp for grp in GROUPS if grp[0] == g][0]
        for n, blk in _group_split(arr, members, join, lead=True).items():
            out[n] = _full_from_blocks(blk, SHARD_AXIS[n])
    return out


def _grad_blocks(gw, names):
    out = []
    for g in names:
        _, members, join = [grp for grp in GROUPS if grp[0] == g][0]
        parts = [_blocks_from_full(gw[n].astype(bf16), SHARD_AXIS[n]) for n in members]
        out.append(parts[0] if len(parts) == 1 else jnp.concatenate(parts, axis=1 + join))
    return out


def _small_pack(d):
    flat = jnp.concatenate([d[n].reshape(-1) for n, _ in REPLICATED])
    return jnp.pad(flat, (0, SMALL_ROWS * LANES - REPL_ELEMS)).reshape(SMALL_ROWS, LANES)


def _small_unpack(packed):
    out, flat, off = {}, packed.reshape(-1), 0
    for n, shape in REPLICATED:
        k = _rows_of((LANES,) + shape)
        out[n] = flat[off:off + k].reshape(shape)
        off += k
    return out


def kernel(x, mem, pre1_g, post1_g, pre2_g, post2_g, mem_norm_g, w_in, fox_f_bias, rwkv_mu, rwkv_w0, rwkv_w_up, rwkv_a0, rwkv_a_up, rwkv_g_up, rwkv_k_k, rwkv_k_a, rwkv_r_k, rwkv_gn_g, rwkv_gn_b, w_mem_kv, w_fox_out, w_rwkv_out, w_mem_out, w_o, w_ffn_gate, w_ffn_up, w_ffn_down, loss_target, m_pre1_g, m_post1_g, m_pre2_g, m_post2_g, m_mem_norm_g, m_w_in, m_fox_f_bias, m_rwkv_mu, m_rwkv_w0, m_rwkv_w_up, m_rwkv_a0, m_rwkv_a_up, m_rwkv_g_up, m_rwkv_k_k, m_rwkv_k_a, m_rwkv_r_k, m_rwkv_gn_g, m_rwkv_gn_b, m_w_mem_kv, m_w_fox_out, m_w_rwkv_out, m_w_mem_out, m_w_o, m_w_ffn_gate, m_w_ffn_up, m_w_ffn_down, v_pre1_g, v_post1_g, v_pre2_g, v_post2_g, v_mem_norm_g, v_w_in, v_fox_f_bias, v_rwkv_mu, v_rwkv_w0, v_rwkv_w_up, v_rwkv_a0, v_rwkv_a_up, v_rwkv_g_up, v_rwkv_k_k, v_rwkv_k_a, v_rwkv_r_k, v_rwkv_gn_g, v_rwkv_gn_b, v_w_mem_kv, v_w_fox_out, v_w_rwkv_out, v_w_mem_out, v_w_o, v_w_ffn_gate, v_w_ffn_up, v_w_ffn_down):
    args = dict(locals())
    turn = lambda n, a: jnp.swapaxes(a, 1, 2) if n in TRANSPOSED else a
    wts = {n: turn(n, args[n]) for n in WEIGHT_ORDER}
    ms = {n: turn(n, args["m_" + n]) for n in WEIGHT_ORDER}
    vs = {n: turn(n, args["v_" + n]) for n in WEIGHT_ORDER}

    groups = {g: (members, join) for g, members, join in GROUPS}
    w_bf16 = {n: wts[n].astype(bf16) for n, _, _ in SHARDED}

    def send(g):
        return _group_local(w_bf16, *groups[g])

    first = _exchange("gather_first", [send(g) for g in FIRST_GROUPS], per_peer=False)
    full = _assemble(first, FIRST_GROUPS)
    small_in = {n: (wts[n] if n == "rwkv_r_k" else wts[n].reshape(wts[n].shape[-2:])) for n, _ in REPLICATED}
    late = ([send(g) for g in LATE_GROUPS[0]], [send(g) for g in LATE_GROUPS[1]],
            lambda got, which: _assemble(got, LATE_GROUPS[which]))
    loss_part, grad_x, gw, gp, early_got, last_got = _local_step(
        x, mem, loss_target, full, small_in, late=late, early=lambda g: _grad_blocks(g, EARLY_GRAD_GROUPS),
        last=lambda g: _grad_blocks(g, LAST_GRAD_GROUPS))
    small_got, loss_got = _exchange("exchange_small", [_small_pack(gp).astype(bf16), jnp.broadcast_to(loss_part, (8, LOSS_LANES))],
                                    per_peer=False)
    received = dict(zip(EARLY_GRAD_GROUPS + LAST_GRAD_GROUPS, list(early_got) + list(last_got)))

    outs = [{}, {}, {}, {}]
    for g, members, _ in GROUPS:
        off = 0
        for n in members:
            for o, arr in zip(outs, _adamw("adamw_" + n, received[g], off, wts[n], ms[n], vs[n])):
                o[n] = arr
            off += wts[n].shape[1]
    res = _adamw("adamw_small", small_got, 0, *[_small_pack(d)[None] for d in (wts, ms, vs)])
    for o, arr in zip(outs, res):
        o.update(_small_unpack(arr))
    loss = jnp.sum(loss_got[:, 0, 0])
    return (loss, grad_x, *[turn(n, o[n].reshape(wts[n].shape)) for o in outs for n in WEIGHT_ORDER])
```
